```python
import jax, jax.numpy as jnp
from jax import lax
import numpy as np

D_MODEL = 1024
BATCH = 16
SEQ = 2048
DEPTH = 1

HEAD_DIM = 64
NA_HEADS = 8
NA_WIDTH = NA_HEADS * HEAD_DIM
GRID_W = 64
NA_ROWS_MAX = 8
NA_COLS = 16
SW_HEADS = 8
SW_KV_HEADS = 2
SW_GROUP = SW_HEADS // SW_KV_HEADS
SW_WIDTH = SW_HEADS * HEAD_DIM
SW_KV_WIDTH = SW_KV_HEADS * HEAD_DIM
SW_WINDOW = 128
SW_BLOCK = 128
MIX_WIDTH = NA_WIDTH + SW_WIDTH
IN_WIDTH = 3 * NA_WIDTH + SW_WIDTH + 2 * SW_KV_WIDTH
D_FF = 2816
CONV_W = 3
ROPE_THETA = 10000.0
EPS = 1e-6
NEG = -1e30

kernel_name = "hybrid_natten_swa_convffn_block"


def rmsnorm(x, g):
    xf = x.astype(jnp.float32)
    y = xf * lax.rsqrt(jnp.mean(xf * xf, axis=-1, keepdims=True) + EPS)
    return (y * g.astype(jnp.float32)).astype(x.dtype)


def rope(t, pos):
    half = HEAD_DIM // 2
    inv = ROPE_THETA ** (-jnp.arange(half, dtype=jnp.float32) / half)
    ang = pos.astype(jnp.float32)[:, None] * inv[None, :]
    cos = jnp.cos(ang)[None, :, None, :].astype(t.dtype)
    sin = jnp.sin(ang)[None, :, None, :].astype(t.dtype)
    t1, t2 = t[..., :half], t[..., half:]
    return jnp.concatenate([t1 * cos - t2 * sin, t2 * cos + t1 * sin], axis=-1)


def neighbourhood_attention(q, k, v, rpb):
    B, S = q.shape[0], q.shape[1]
    rows = S // GRID_W
    wr = min(NA_ROWS_MAX, rows)

    def grid(t):
        return t.reshape(B, rows, GRID_W, NA_HEADS, HEAD_DIM).transpose(0, 3, 1, 2, 4)

    r = jnp.arange(rows)
    rs = jnp.clip(r - wr // 2, 0, rows - wr)
    row_idx = rs[:, None] + jnp.arange(wr)[None, :]
    kb = jnp.take(grid(k), row_idx, axis=2).reshape(B, NA_HEADS, rows, wr * GRID_W, HEAD_DIM)
    vb = jnp.take(grid(v), row_idx, axis=2).reshape(B, NA_HEADS, rows, wr * GRID_W, HEAD_DIM)

    col = jnp.arange(GRID_W)
    cs = jnp.clip(col - NA_COLS // 2, 0, GRID_W - NA_COLS)
    col_ok = (col[None, :] >= cs[:, None]) & (col[None, :] < cs[:, None] + NA_COLS)
    dc = jnp.clip(col[None, :] - col[:, None] + NA_COLS - 1, 0, 2 * NA_COLS - 2)
    dr = row_idx - r[:, None] + NA_ROWS_MAX - 1
    bias = rpb[:, dr[:, None, :, None], dc[None, :, None, :]]
    bias = jnp.where(col_ok[None, None, :, None, :], bias.astype(jnp.float32), NEG)
    bias = bias.reshape(NA_HEADS, rows, GRID_W, wr * GRID_W)

    s = jnp.einsum('bhrqd,bhrkd->bhrqk', grid(q), kb,
                   preferred_element_type=jnp.float32) * (HEAD_DIM ** -0.5) + bias[None]
    p = jax.nn.softmax(s, axis=-1).astype(v.dtype)
    o = jnp.einsum('bhrqk,bhrkd->bhrqd', p, vb)
    return o.transpose(0, 2, 3, 1, 4).reshape(B, S, NA_WIDTH)


def window_sink_attention(q, k, v, sink):
    B, S = q.shape[0], q.shape[1]
    nb = S // SW_BLOCK
    qb = q.reshape(B, nb, SW_BLOCK, SW_KV_HEADS, SW_GROUP, HEAD_DIM)

    def band(t):
        tp = jnp.pad(t, ((0, 0), (SW_BLOCK, SW_BLOCK), (0, 0), (0, 0)))
        tp = tp.reshape(B, nb + 2, SW_BLOCK, SW_KV_HEADS, HEAD_DIM)
        return jnp.concatenate([tp[:, :-2], tp[:, 1:-1], tp[:, 2:]], axis=2)

    kw, vw = band(k), band(v)
    blk = jnp.arange(nb)[:, None] * SW_BLOCK
    qpos = blk + jnp.arange(SW_BLOCK)[None, :]
    kpos = blk - SW_BLOCK + jnp.arange(3 * SW_BLOCK)[None, :]
    ok = (jnp.abs(qpos[:, :, None] - kpos[:, None, :]) <= SW_WINDOW) \
        & ((kpos >= 0) & (kpos < S))[:, None, :]

    s = jnp.einsum('bnqhgd,bnkhd->bhgnqk', qb, kw,
                   preferred_element_type=jnp.float32) * (HEAD_DIM ** -0.5)
    s = jnp.where(ok[None, None, None], s, NEG)
    sk = sink.astype(jnp.float32).reshape(1, SW_KV_HEADS, SW_GROUP, 1, 1)
    m = jnp.maximum(jnp.max(s, axis=-1), sk)
    p = jnp.exp(s - m[..., None])
    den = jnp.sum(p, axis=-1) + jnp.exp(sk - m)
    p = (p / den[..., None]).astype(v.dtype)
    o = jnp.einsum('bhgnqk,bnkhd->bnqhgd', p, vw)
    return o.reshape(B, S, SW_WIDTH)


def _fwd_setup_inputs(seed: int = 0) -> dict:
    key = jax.random.key(seed)
    ks = jax.random.split(key, 20)
    L, D = DEPTH, D_MODEL

    def nrm(k, shape, scale):
        return jax.random.normal(k, shape, jnp.float32) * scale

    return {
        "x": nrm(ks[0], (BATCH, SEQ, D), 1.0),
        "c": nrm(ks[1], (BATCH, D), 1.0),
        "w_ada": nrm(ks[2], (L, D, 6 * D), 0.5 * D ** -0.5),
        "b_ada": nrm(ks[3], (L, 6 * D), 0.02),
        "g_attn": 1.0 + nrm(ks[4], (L, D), 0.02),
        "w_in": nrm(ks[5], (L, D, IN_WIDTH), D ** -0.5),
        "na_rpb": nrm(ks[6], (L, NA_HEADS, 2 * NA_ROWS_MAX - 1, 2 * NA_COLS - 1), 0.1),
        "sw_sink": nrm(ks[7], (L, SW_HEADS), 0.5),
        "g_na_out": 1.0 + nrm(ks[8], (L, NA_WIDTH), 0.02),
        "g_sw_out": 1.0 + nrm(ks[9], (L, SW_WIDTH), 0.02),
        "w_out": nrm(ks[10], (L, MIX_WIDTH, D), MIX_WIDTH ** -0.5),
        "g_ffn": 1.0 + nrm(ks[11], (L, D), 0.02),
        "w_up": nrm(ks[12], (L, D, 2 * D_FF), D ** -0.5),
        "conv_w": nrm(ks[13], (L, CONV_W, D_FF), CONV_W ** -0.5),
        "conv_b": nrm(ks[14], (L, D_FF), 0.02),
        "w_down": nrm(ks[15], (L, D_FF, D), D_FF ** -0.5),
        "g_final": 1.0 + nrm(ks[16], (D,), 0.02),
    }


def _fwd_reference(x, c, w_ada, b_ada, g_attn, w_in, na_rpb, sw_sink, g_na_out, g_sw_out,
              w_out, g_ffn, w_up, conv_w, conv_b, w_down, g_final):
    S = x.shape[1]
    pos = jnp.arange(S)
    splits = [NA_WIDTH, 2 * NA_WIDTH, 3 * NA_WIDTH, 3 * NA_WIDTH + SW_WIDTH,
              3 * NA_WIDTH + SW_WIDTH + SW_KV_WIDTH]
    for l in range(DEPTH):
        mod = jax.nn.silu(c) @ w_ada[l] + b_ada[l]
        shift_a, scale_a, gate_a, shift_f, scale_f, gate_f = [
            m[:, None, :] for m in jnp.split(mod, 6, axis=-1)]

        h = rmsnorm(x, g_attn[l]) * (1.0 + scale_a) + shift_a
        proj = h @ w_in[l]
        qa, ka, va, qb, kb, vb = jnp.split(proj, splits, axis=-1)
        Bn = x.shape[0]
        qa = qa.reshape(Bn, S, NA_HEADS, HEAD_DIM)
        ka = ka.reshape(Bn, S, NA_HEADS, HEAD_DIM)
        va = va.reshape(Bn, S, NA_HEADS, HEAD_DIM)
        qb = rope(qb.reshape(Bn, S, SW_HEADS, HEAD_DIM), pos)
        kb = rope(kb.reshape(Bn, S, SW_KV_HEADS, HEAD_DIM), pos)
        vb = vb.reshape(Bn, S, SW_KV_HEADS, HEAD_DIM)

        o_a = rmsnorm(neighbourhood_attention(qa, ka, va, na_rpb[l]), g_na_out[l])
        o_b = rmsnorm(window_sink_attention(qb, kb, vb, sw_sink[l]), g_sw_out[l])
        mix = jnp.concatenate([o_a, o_b], axis=-1) @ w_out[l]
        x = x + gate_a * mix

        h = rmsnorm(x, g_ffn[l]) * (1.0 + scale_f) + shift_f
        val, gt = jnp.split(h @ w_up[l], 2, axis=-1)
        gp = jnp.pad(gt, ((0, 0), (1, 1), (0, 0)))
        cw = conv_w[l]
        gc = gp[:, :-2] * cw[0] + gp[:, 1:-1] * cw[1] + gp[:, 2:] * cw[2] + conv_b[l]
        x = x + gate_f * ((jax.nn.silu(gc) * val) @ w_down[l])
    return rmsnorm(x, g_final)


import jax as _jax
import jax.numpy as _jnp

TWIN_FORMAT = 'train_step'
FWD_PARAMS = ['x', 'c', 'w_ada', 'b_ada', 'g_attn', 'w_in', 'na_rpb', 'sw_sink', 'g_na_out', 'g_sw_out', 'w_out', 'g_ffn', 'w_up', 'conv_w', 'conv_b', 'w_down', 'g_final']
TWIN_WEIGHTS = ['w_ada', 'b_ada', 'g_attn', 'w_in', 'na_rpb', 'sw_sink', 'g_na_out', 'g_sw_out', 'w_out', 'g_ffn', 'w_up', 'conv_w', 'conv_b', 'w_down', 'g_final']
TWIN_DIFF_INPUT = 'x'
TWIN_INPUTS = ['x', 'c', 'w_ada', 'b_ada', 'g_attn', 'w_in', 'na_rpb', 'sw_sink', 'g_na_out', 'g_sw_out', 'w_out', 'g_ffn', 'w_up', 'conv_w', 'conv_b', 'w_down', 'g_final', 'loss_target', 'm_w_ada', 'm_b_ada', 'm_g_attn', 'm_w_in', 'm_na_rpb', 'm_sw_sink', 'm_g_na_out', 'm_g_sw_out', 'm_w_out', 'm_g_ffn', 'm_w_up', 'm_conv_w', 'm_conv_b', 'm_w_down', 'm_g_final', 'v_w_ada', 'v_b_ada', 'v_g_attn', 'v_w_in', 'v_na_rpb', 'v_sw_sink', 'v_g_na_out', 'v_g_sw_out', 'v_w_out', 'v_g_ffn', 'v_w_up', 'v_conv_w', 'v_conv_b', 'v_w_down', 'v_g_final']
TWIN_OUTPUTS = ['loss', 'grad_x', 'grad_w_ada', 'grad_b_ada', 'grad_g_attn', 'grad_w_in', 'grad_na_rpb', 'grad_sw_sink', 'grad_g_na_out', 'grad_g_sw_out', 'grad_w_out', 'grad_g_ffn', 'grad_w_up', 'grad_conv_w', 'grad_conv_b', 'grad_w_down', 'grad_g_final', 'delta_w_ada', 'delta_b_ada', 'delta_g_attn', 'delta_w_in', 'delta_na_rpb', 'delta_sw_sink', 'delta_g_na_out', 'delta_g_sw_out', 'delta_w_out', 'delta_g_ffn', 'delta_w_up', 'delta_conv_w', 'delta_conv_b', 'delta_w_down', 'delta_g_final', 'new_m_w_ada', 'new_m_b_ada', 'new_m_g_attn', 'new_m_w_in', 'new_m_na_rpb', 'new_m_sw_sink', 'new_m_g_na_out', 'new_m_g_sw_out', 'new_m_w_out', 'new_m_g_ffn', 'new_m_w_up', 'new_m_conv_w', 'new_m_conv_b', 'new_m_w_down', 'new_m_g_final', 'new_v_w_ada', 'new_v_b_ada', 'new_v_g_attn', 'new_v_w_in', 'new_v_na_rpb', 'new_v_sw_sink', 'new_v_g_na_out', 'new_v_g_sw_out', 'new_v_w_out', 'new_v_g_ffn', 'new_v_w_up', 'new_v_conv_w', 'new_v_conv_b', 'new_v_w_down', 'new_v_g_final']
TWIN_LEAF_KINDS = {'loss': 'loss', 'grad_x': 'grad_x', 'grad_w_ada': 'grad_w', 'grad_b_ada': 'grad_w', 'grad_g_attn': 'grad_w', 'grad_w_in': 'grad_w', 'grad_na_rpb': 'grad_w', 'grad_sw_sink': 'grad_w', 'grad_g_na_out': 'grad_w', 'grad_g_sw_out': 'grad_w', 'grad_w_out': 'grad_w', 'grad_g_ffn': 'grad_w', 'grad_w_up': 'grad_w', 'grad_conv_w': 'grad_w', 'grad_conv_b': 'grad_w', 'grad_w_down': 'grad_w', 'grad_g_final': 'grad_w', 'delta_w_ada': 'delta_w', 'delta_b_ada': 'delta_w', 'delta_g_attn': 'delta_w', 'delta_w_in': 'delta_w', 'delta_na_rpb': 'delta_w', 'delta_sw_sink': 'delta_w', 'delta_g_na_out': 'delta_w', 'delta_g_sw_out': 'delta_w', 'delta_w_out': 'delta_w', 'delta_g_ffn': 'delta_w', 'delta_w_up': 'delta_w', 'delta_conv_w': 'delta_w', 'delta_conv_b': 'delta_w', 'delta_w_down': 'delta_w', 'delta_g_final': 'delta_w', 'new_m_w_ada': 'new_m', 'new_m_b_ada': 'new_m', 'new_m_g_attn': 'new_m', 'new_m_w_in': 'new_m', 'new_m_na_rpb': 'new_m', 'new_m_sw_sink': 'new_m', 'new_m_g_na_out': 'new_m', 'new_m_g_sw_out': 'new_m', 'new_m_w_out': 'new_m', 'new_m_g_ffn': 'new_m', 'new_m_w_up': 'new_m', 'new_m_conv_w': 'new_m', 'new_m_conv_b': 'new_m', 'new_m_w_down': 'new_m', 'new_m_g_final': 'new_m', 'new_v_w_ada': 'new_v', 'new_v_b_ada': 'new_v', 'new_v_g_attn': 'new_v', 'new_v_w_in': 'new_v', 'new_v_na_rpb': 'new_v', 'new_v_sw_sink': 'new_v', 'new_v_g_na_out': 'new_v', 'new_v_g_sw_out': 'new_v', 'new_v_w_out': 'new_v', 'new_v_g_ffn': 'new_v', 'new_v_w_up': 'new_v', 'new_v_conv_w': 'new_v', 'new_v_conv_b': 'new_v', 'new_v_w_down': 'new_v', 'new_v_g_final': 'new_v'}


def _forward(args):
    return _fwd_reference(*[args[k] for k in FWD_PARAMS])


def _output_shape():
    out = _jax.eval_shape(lambda: _forward(_fwd_setup_inputs(0)))
    return out.shape, out.dtype

N_MICROBATCH = 1
ADAM_LR = 0.001
ADAM_B1 = 0.9
ADAM_B2 = 0.999
ADAM_EPS = 1e-08
ADAM_WD = 0.01
ADAM_STEP = 10
PER_EXAMPLE_BATCH_AXIS = {'x': 0, 'c': 0, 'loss_target': 0}
SHARED_INPUTS = []
_WEIGHT_DTYPES = {'w_ada': _jnp.float32, 'b_ada': _jnp.float32, 'g_attn': _jnp.float32, 'w_in': _jnp.float32, 'na_rpb': _jnp.float32, 'sw_sink': _jnp.float32, 'g_na_out': _jnp.float32, 'g_sw_out': _jnp.float32, 'w_out': _jnp.float32, 'g_ffn': _jnp.float32, 'w_up': _jnp.float32, 'conv_w': _jnp.float32, 'conv_b': _jnp.float32, 'w_down': _jnp.float32, 'g_final': _jnp.float32}
MOMENT_SCALE = {'w_ada': 8.795312e-02, 'b_ada': 1.478513e-01, 'g_attn': 5.158392e-02, 'w_in': 4.677636e-02, 'na_rpb': 1.046542e-02, 'sw_sink': 1.432682e-03, 'g_na_out': 6.604983e-02, 'g_sw_out': 7.979177e-02, 'w_out': 6.694116e-02, 'g_ffn': 5.415396e-02, 'w_up': 2.343721e-02, 'conv_w': 2.359244e-02, 'conv_b': 2.048152e-02, 'w_down': 3.812768e-02, 'g_final': 3.229616e+01}


def _to_microbatches(a, axis):
    t = _jnp.moveaxis(a, axis, 0)
    t = t.reshape((N_MICROBATCH, t.shape[0] // N_MICROBATCH) + t.shape[1:])
    return _jnp.moveaxis(t, 1, axis + 1)


def setup_inputs(seed: int = 0) -> dict:
    inp = _fwd_setup_inputs(seed)
    key = _jax.random.fold_in(_jax.random.key(seed), 7919)
    shape, _ = _output_shape()
    out = dict(inp)
    out["loss_target"] = _jax.random.normal(_jax.random.fold_in(key, 0), shape, _jnp.float32)
    for i, name in enumerate(TWIN_WEIGHTS):
        w = inp[name].astype(_jnp.float32)
        if MOMENT_SCALE is None:
            s = _jnp.sqrt(_jnp.mean(_jnp.square(w)) + 1e-30)
        else:
            s = MOMENT_SCALE[name]
        km, kv = _jax.random.split(_jax.random.fold_in(key, i + 1))
        out[name] = w
        out["m_" + name] = s * _jax.random.normal(km, w.shape, _jnp.float32)
        out["v_" + name] = (s * s) * _jax.random.uniform(kv, w.shape, _jnp.float32, 0.5, 1.5)
    if N_MICROBATCH > 1:
        for name, axis in PER_EXAMPLE_BATCH_AXIS.items():
            out[name] = _to_microbatches(out[name], axis)
    return {'x': out['x'], 'c': out['c'], 'w_ada': out['w_ada'], 'b_ada': out['b_ada'], 'g_attn': out['g_attn'], 'w_in': out['w_in'], 'na_rpb': out['na_rpb'], 'sw_sink': out['sw_sink'], 'g_na_out': out['g_na_out'], 'g_sw_out': out['g_sw_out'], 'w_out': out['w_out'], 'g_ffn': out['g_ffn'], 'w_up': out['w_up'], 'conv_w': out['conv_w'], 'conv_b': out['conv_b'], 'w_down': out['w_down'], 'g_final': out['g_final'], 'loss_target': out['loss_target'], 'm_w_ada': out['m_w_ada'], 'm_b_ada': out['m_b_ada'], 'm_g_attn': out['m_g_attn'], 'm_w_in': out['m_w_in'], 'm_na_rpb': out['m_na_rpb'], 'm_sw_sink': out['m_sw_sink'], 'm_g_na_out': out['m_g_na_out'], 'm_g_sw_out': out['m_g_sw_out'], 'm_w_out': out['m_w_out'], 'm_g_ffn': out['m_g_ffn'], 'm_w_up': out['m_w_up'], 'm_conv_w': out['m_conv_w'], 'm_conv_b': out['m_conv_b'], 'm_w_down': out['m_w_down'], 'm_g_final': out['m_g_final'], 'v_w_ada': out['v_w_ada'], 'v_b_ada': out['v_b_ada'], 'v_g_attn': out['v_g_attn'], 'v_w_in': out['v_w_in'], 'v_na_rpb': out['v_na_rpb'], 'v_sw_sink': out['v_sw_sink'], 'v_g_na_out': out['v_g_na_out'], 'v_g_sw_out': out['v_g_sw_out'], 'v_w_out': out['v_w_out'], 'v_g_ffn': out['v_g_ffn'], 'v_w_up': out['v_w_up'], 'v_conv_w': out['v_conv_w'], 'v_conv_b': out['v_conv_b'], 'v_w_down': out['v_w_down'], 'v_g_final': out['v_g_final']}


def _loss(weights, diff, rest, loss_target):
    with _jax.named_scope("forward"):
        args = {**rest, TWIN_DIFF_INPUT: diff, **{k: w.astype(_WEIGHT_DTYPES[k]) for k, w in weights.items()}}
        y = _forward(args)
    with _jax.named_scope("loss_head"):
        err = _jnp.square(y.astype(_jnp.float32) - loss_target)
        return 0.5 * _jnp.sum(_jnp.mean(err, axis=-1)) if err.ndim else 0.5 * err


def _adamw(w, g, m, v):
    m = ADAM_B1 * m + (1.0 - ADAM_B1) * g
    v = ADAM_B2 * v + (1.0 - ADAM_B2) * _jnp.square(g)
    m_hat = m / (1.0 - ADAM_B1 ** ADAM_STEP)
    v_hat = v / (1.0 - ADAM_B2 ** ADAM_STEP)
    delta = -ADAM_LR * (m_hat / (_jnp.sqrt(v_hat) + ADAM_EPS) + ADAM_WD * w)
    return delta, m, v


def reference(x, c, w_ada, b_ada, g_attn, w_in, na_rpb, sw_sink, g_na_out, g_sw_out, w_out, g_ffn, w_up, conv_w, conv_b, w_down, g_final, loss_target, m_w_ada, m_b_ada, m_g_attn, m_w_in, m_na_rpb, m_sw_sink, m_g_na_out, m_g_sw_out, m_w_out, m_g_ffn, m_w_up, m_conv_w, m_conv_b, m_w_down, m_g_final, v_w_ada, v_b_ada, v_g_attn, v_w_in, v_na_rpb, v_sw_sink, v_g_na_out, v_g_sw_out, v_w_out, v_g_ffn, v_w_up, v_conv_w, v_conv_b, v_w_down, v_g_final):
    given = dict(x=x, c=c, w_ada=w_ada, b_ada=b_ada, g_attn=g_attn, w_in=w_in, na_rpb=na_rpb, sw_sink=sw_sink, g_na_out=g_na_out, g_sw_out=g_sw_out, w_out=w_out, g_ffn=g_ffn, w_up=w_up, conv_w=conv_w, conv_b=conv_b, w_down=w_down, g_final=g_final, loss_target=loss_target, m_w_ada=m_w_ada, m_b_ada=m_b_ada, m_g_attn=m_g_attn, m_w_in=m_w_in, m_na_rpb=m_na_rpb, m_sw_sink=m_sw_sink, m_g_na_out=m_g_na_out, m_g_sw_out=m_g_sw_out, m_w_out=m_w_out, m_g_ffn=m_g_ffn, m_w_up=m_w_up, m_conv_w=m_conv_w, m_conv_b=m_conv_b, m_w_down=m_w_down, m_g_final=m_g_final, v_w_ada=v_w_ada, v_b_ada=v_b_ada, v_g_attn=v_g_attn, v_w_in=v_w_in, v_na_rpb=v_na_rpb, v_sw_sink=v_sw_sink, v_g_na_out=v_g_na_out, v_g_sw_out=v_g_sw_out, v_w_out=v_w_out, v_g_ffn=v_g_ffn, v_w_up=v_w_up, v_conv_w=v_conv_w, v_conv_b=v_conv_b, v_w_down=v_w_down, v_g_final=v_g_final)
    weights = {n: given[n] for n in TWIN_WEIGHTS}
    shared = {n: given[n] for n in SHARED_INPUTS}
    per_example = {n: given[n] for n in ['x', 'c']}
    grad_fn = _jax.value_and_grad(_loss, argnums=(0, 1))

    def one_microbatch(ex, loss_target):
        ex = dict(ex)
        diff = ex.pop(TWIN_DIFF_INPUT)
        return grad_fn(weights, diff, {**shared, **ex}, loss_target)

    if N_MICROBATCH == 1:
        loss, (grad_w, grad_x) = one_microbatch(per_example, given["loss_target"])
    else:
        def body(carry, xs):
            loss_sum, grad_sum = carry
            l_k, (gw_k, gx_k) = one_microbatch(xs[0], xs[1])
            with _jax.named_scope("update"):
                return (loss_sum + l_k, _jax.tree.map(_jnp.add, grad_sum, gw_k)), gx_k

        init = (_jnp.zeros((), _jnp.float32), _jax.tree.map(_jnp.zeros_like, weights))
        (loss, grad_w), grad_x = _jax.lax.scan(body, init, (per_example, given["loss_target"]))
    with _jax.named_scope("update"):
        delta_w, new_m, new_v = {}, {}, {}
        for n in TWIN_WEIGHTS:
            delta_w[n], new_m[n], new_v[n] = _adamw(weights[n], grad_w[n], given["m_" + n], given["v_" + n])
    return (loss, grad_x, *[grad_w[n] for n in TWIN_WEIGHTS], *[delta_w[n] for n in TWIN_WEIGHTS],
            *[new_m[n] for n in TWIN_WEIGHTS], *[new_v[n] for n in TWIN_WEIGHTS])
```

```python
import functools

import numpy as np
import jax
import jax.numpy as jnp
from jax import lax
from jax.experimental import pallas as pl
from jax.experimental.pallas import tpu as pltpu

F32, BF16 = jnp.float32, jnp.bfloat16
MESH_ID = pl.DeviceIdType.MESH
N_DEV = 8

HEAD_DIM = 64
NA_HEADS = 8
SW_HEADS = 8
SW_KV_HEADS = 2
SW_GROUP = SW_HEADS // SW_KV_HEADS
NA_WIDTH = NA_HEADS * HEAD_DIM
SW_WIDTH = SW_HEADS * HEAD_DIM
SW_KV_WIDTH = SW_KV_HEADS * HEAD_DIM
ROPE_WIDTH = SW_WIDTH + SW_KV_WIDTH
IN_WIDTH = 3 * NA_WIDTH + SW_WIDTH + 2 * SW_KV_WIDTH
ROPE_LO = 3 * NA_WIDTH
GRID_W = 64
NA_ROWS_MAX = 8
NA_COLS = 16
N_DR = 2 * NA_ROWS_MAX - 1
N_DC = 2 * NA_COLS - 1
SW_WINDOW = 128
SW_BLOCK = 128
ROPE_THETA = 10000.0
EPS = 1e-6
NEG = -1e30
Q_SCALE = HEAD_DIM ** -0.5

ADAM_LR = 0.001
ADAM_B1 = 0.9
ADAM_B2 = 0.999
ADAM_EPS = 1e-08
ADAM_WD = 0.01
ADAM_STEP = 10

TOKEN_TILE = 256
VMEM_LIMIT = 56 * 1024 * 1024

PACK_W = 6144


def _nn(a, b):
    return jnp.dot(a, b, preferred_element_type=F32)


def _nt(a, b):
    return lax.dot_general(a, b, (((1,), (1,)), ((), ())), preferred_element_type=F32)


def _tn(a, b):
    return lax.dot_general(a, b, (((0,), (0,)), ((), ())), preferred_element_type=F32)


def _rms(x):
    r = lax.rsqrt(jnp.mean(x * x, axis=-1, keepdims=True) + EPS)
    return x * r, r


def _rms_bwd(xn, r, gy):
    return r * (gy - xn * jnp.mean(xn * gy, axis=-1, keepdims=True))


def _params(*sem):
    return pltpu.CompilerParams(dimension_semantics=sem, vmem_limit_bytes=VMEM_LIMIT)


def _full(shape):
    n = len(shape)
    return pl.BlockSpec(shape, lambda *_: (0,) * n)


def _mesh_pos():
    return lax.axis_index("x"), lax.axis_index("y"), lax.axis_index("c")


def _all_gather(x, name):
    def body(x_ref, out_ref, send_sems, recv_sems, local_sem):
        x_, y_, c_ = _mesh_pos()
        me, sibling = (x_, y_, c_), (x_, y_, 1 - c_)
        chips = [(1 - x_, y_), (x_, 1 - y_), (1 - x_, 1 - y_)]

        def rows(px, py, pc):
            return out_ref.at[4 * px + 2 * py + pc]

        def copy(k, block, to, src=None):
            return pltpu.make_async_remote_copy(
                src_ref=rows(*block) if src is None else src, dst_ref=rows(*block),
                send_sem=send_sems.at[k], recv_sem=recv_sems.at[k], device_id=to, device_id_type=MESH_ID)

        mine = pltpu.make_async_copy(x_ref, rows(*me), local_sem)
        mine.start()
        first = [copy(0, me, sibling, src=x_ref)]
        first += [copy(1 + j, me, (*chip, c_), src=x_ref) for j, chip in enumerate(chips)]
        for cp in first:
            cp.start()
        passed = [copy(4 + j, (*chip, c_), sibling) for j, chip in enumerate(chips)]
        for j, chip in enumerate(chips):
            copy(1 + j, (*chip, c_), me).wait_recv()
            passed[j].start()
        copy(0, sibling, me).wait_recv()
        for j, chip in enumerate(chips):
            copy(4 + j, (*chip, 1 - c_), me).wait_recv()
        for cp in first + passed:
            cp.wait_send()
        mine.wait()

    return pl.pallas_call(
        body, name=name,
        out_shape=jax.ShapeDtypeStruct((N_DEV,) + x.shape, x.dtype),
        in_specs=[pl.BlockSpec(memory_space=pl.ANY)],
        out_specs=pl.BlockSpec(memory_space=pl.ANY),
        scratch_shapes=[pltpu.SemaphoreType.DMA((7,)), pltpu.SemaphoreType.DMA((7,)), pltpu.SemaphoreType.DMA],
    )(x)


def _row_chunk(r):
    for rc in (128, 64, 32, 16):
        if r % rc == 0:
            return rc
    raise ValueError(f"rows {r} not a multiple of 16")


def _reduce_scatter(g8, name):
    _, R, C = g8.shape
    rc = _row_chunk(R)

    def body(g_ref, out_ref, recva, sendb, recvb, sa, ra, sb, rb):
        x_, y_, c_ = _mesh_pos()
        sibling = (x_, y_, 1 - c_)
        copies_a = []
        for k in range(4):
            cp = pltpu.make_async_remote_copy(
                src_ref=g_ref.at[2 * k + (1 - c_)], dst_ref=recva.at[k],
                send_sem=sa.at[k], recv_sem=ra.at[k], device_id=sibling, device_id_type=MESH_ID)
            cp.start()
            copies_a.append(cp)
        for cp in copies_a:
            cp.wait_recv()

        def chip_sum(k, rows):
            return g_ref[2 * k + c_, rows, :].astype(F32) + recva[k, rows, :].astype(F32)

        flips = [(1 - x_, y_), (x_, 1 - y_), (1 - x_, 1 - y_)]
        copies_b = []
        for j, (tx, ty) in enumerate(flips):
            kt = 2 * tx + ty

            def fill(i, carry, j=j, kt=kt):
                rows = pl.ds(pl.multiple_of(i * rc, rc), rc)
                sendb[j, rows, :] = chip_sum(kt, rows).astype(BF16)
                return carry

            lax.fori_loop(0, R // rc, fill, 0)
            cp = pltpu.make_async_remote_copy(
                src_ref=sendb.at[j], dst_ref=recvb.at[j],
                send_sem=sb.at[j], recv_sem=rb.at[j], device_id=(tx, ty, c_), device_id_type=MESH_ID)
            cp.start()
            copies_b.append(cp)
        for cp in copies_b:
            cp.wait_recv()
        kme = 2 * x_ + y_

        def total(i, carry):
            rows = pl.ds(pl.multiple_of(i * rc, rc), rc)
            acc = chip_sum(kme, rows)
            for j in range(3):
                acc = acc + recvb[j, rows, :].astype(F32)
            out_ref[rows, :] = acc
            return carry

        lax.fori_loop(0, R // rc, total, 0)
        for cp in copies_a + copies_b:
            cp.wait_send()

    vm = pl.BlockSpec(memory_space=pltpu.VMEM)
    return pl.pallas_call(
        body, name=name,
        out_shape=jax.ShapeDtypeStruct((R, C), F32),
        in_specs=[vm], out_specs=vm,
        scratch_shapes=[pltpu.VMEM((4, R, C), BF16), pltpu.VMEM((3, R, C), BF16), pltpu.VMEM((3, R, C), BF16),
                        pltpu.SemaphoreType.DMA((4,)), pltpu.SemaphoreType.DMA((4,)),
                        pltpu.SemaphoreType.DMA((3,)), pltpu.SemaphoreType.DMA((3,))],
        compiler_params=pltpu.CompilerParams(vmem_limit_bytes=VMEM_LIMIT),
    )(g8)


def _silu(v):
    return v * (1.0 / (1.0 + jnp.exp(-v)))


def _ada_fwd(c_all, w_ada_l, b_ada_l):
    def body(c_ref, w_ref, b_ref, o_ref):
        o_ref[...] = jnp.dot(_silu(c_ref[...]), w_ref[...], precision=lax.Precision.HIGHEST,
                             preferred_element_type=F32) + b_ref[...]
    return pl.pallas_call(body, name="ada_fwd", out_shape=jax.ShapeDtypeStruct((c_all.shape[0], w_ada_l.shape[1]), F32),
                          compiler_params=pltpu.CompilerParams(vmem_limit_bytes=VMEM_LIMIT))(c_all, w_ada_l, b_ada_l)


def _ada_bwd(c_all, dmod_cols):
    def body(c_ref, d_ref, o_ref):
        o_ref[...] = lax.dot_general(_silu(c_ref[...]), d_ref[...], (((0,), (0,)), ((), ())),
                                     precision=lax.Precision.HIGHEST, preferred_element_type=F32)
    return pl.pallas_call(body, name="ada_bwd", out_shape=jax.ShapeDtypeStruct((c_all.shape[1], dmod_cols.shape[1]), F32),
                          compiler_params=pltpu.CompilerParams(vmem_limit_bytes=VMEM_LIMIT))(c_all, dmod_cols)


def _toeplitz_onehot():
    q = np.arange(GRID_W)[:, None]
    k = np.arange(GRID_W)[None, :]
    dc = np.clip(k - q + NA_COLS - 1, 0, N_DC - 1).reshape(-1)
    e = np.zeros((128, GRID_W * GRID_W), np.float32)
    e[dc, np.arange(GRID_W * GRID_W)] = 1.0
    return jnp.asarray(e)


def _rpb_expand(rpb2d, onehot):
    def body(r_ref, e_ref, o_ref):
        o_ref[...] = jnp.dot(r_ref[...], e_ref[...], precision=lax.Precision.HIGHEST, preferred_element_type=F32)
    return pl.pallas_call(body, name="rpb_expand", out_shape=jax.ShapeDtypeStruct((128, GRID_W * GRID_W), F32),
                          compiler_params=pltpu.CompilerParams(vmem_limit_bytes=VMEM_LIMIT))(rpb2d, onehot)


def _rpb_reduce(dtz2d, onehot):
    def body(d_ref, e_ref, o_ref):
        o_ref[...] = lax.dot_general(d_ref[...], e_ref[...], (((1,), (1,)), ((), ())),
                                     precision=lax.Precision.HIGHEST, preferred_element_type=F32)
    return pl.pallas_call(body, name="rpb_reduce", out_shape=jax.ShapeDtypeStruct((128, 128), F32),
                          compiler_params=pltpu.CompilerParams(vmem_limit_bytes=VMEM_LIMIT))(dtz2d, onehot)


def _na_bias_table(na_rpb, onehot, wr):
    rpb2d = jnp.pad(na_rpb.reshape(NA_HEADS * N_DR, N_DC), ((0, 128 - NA_HEADS * N_DR), (0, 128 - N_DC)))
    tz = _rpb_expand(rpb2d, onehot)[:NA_HEADS * N_DR].reshape(NA_HEADS, N_DR, GRID_W, GRID_W)
    col = np.arange(GRID_W)
    cs = np.clip(col - NA_COLS // 2, 0, GRID_W - NA_COLS)
    col_ok = (col[None, :] >= cs[:, None]) & (col[None, :] < cs[:, None] + NA_COLS)
    shifts = []
    for s in range(NA_ROWS_MAX):
        dr = np.clip(np.arange(wr) + s, 0, N_DR - 1)
        t = jnp.stack([tz[:, int(d)] for d in dr], axis=2)
        t = jnp.where(jnp.asarray(col_ok)[None, :, None, :], t, NEG)
        shifts.append(t.reshape(NA_HEADS, GRID_W, wr * GRID_W))
    return jnp.stack(shifts, axis=1)


def _rope_tables(S):
    half = HEAD_DIM // 2
    inv = ROPE_THETA ** (-jnp.arange(half, dtype=F32) / half)
    ang = jnp.arange(S).astype(F32)[:, None] * inv[None, :]
    cos, sin = jnp.cos(ang), jnp.sin(ang)
    n = ROPE_WIDTH // HEAD_DIM
    return jnp.tile(jnp.concatenate([cos, cos], axis=1), (1, n)), jnp.tile(jnp.concatenate([-sin, sin], axis=1), (1, n))


def _rot_half(t):
    w = t.shape[1]
    lane = lax.broadcasted_iota(jnp.int32, t.shape, 1)
    return jnp.where((lane % HEAD_DIM) < HEAD_DIM // 2, pltpu.roll(t, w - HEAD_DIM // 2, axis=1),
                     pltpu.roll(t, HEAD_DIM // 2, axis=1))


def _heads_spec(nh, tps):
    return pl.BlockSpec((1, nh, TOKEN_TILE, HEAD_DIM), lambda i: (i // tps, 0, i % tps, 0))


def _tok_spec(w):
    return pl.BlockSpec((TOKEN_TILE, w), lambda i: (i, 0))


def _mod_spec(tps, d):
    return pl.BlockSpec((1, 6, d), lambda i: (i // tps, 0, 0))


def _bstat_spec(tps, w):
    return pl.BlockSpec((1, 8, w), lambda i: (i // tps, 0, 0))


def _attn_in(x2d, mod3, g_attn, w_in, cos_t, sin_t, B, S):
    T, D = x2d.shape
    tps = S // TOKEN_TILE

    def body(x_ref, mod_ref, g_ref, w_ref, cos_ref, sin_ref, h_ref, qa_ref, ka_ref, va_ref, qb_ref, kb_ref, vb_ref):
        xn, _ = _rms(x_ref[...])
        h = (xn * g_ref[...]) * (1.0 + mod_ref[0, 1:2, :]) + mod_ref[0, 0:1, :]
        hb = h.astype(BF16)
        h_ref[...] = hb
        proj = _nn(hb, w_ref[...])
        rb = proj[:, ROPE_LO:ROPE_LO + ROPE_WIDTH]
        rb = rb * cos_ref[...] + _rot_half(rb) * sin_ref[...]
        for hh in range(NA_HEADS):
            lo = hh * HEAD_DIM
            qa_ref[0, hh] = (proj[:, lo:lo + HEAD_DIM] * Q_SCALE).astype(BF16)
            ka_ref[0, hh] = proj[:, NA_WIDTH + lo:NA_WIDTH + lo + HEAD_DIM].astype(BF16)
            va_ref[0, hh] = proj[:, 2 * NA_WIDTH + lo:2 * NA_WIDTH + lo + HEAD_DIM].astype(BF16)
            qb_ref[0, hh] = (rb[:, lo:lo + HEAD_DIM] * Q_SCALE).astype(BF16)
        for hh in range(SW_KV_HEADS):
            lo = hh * HEAD_DIM
            kb_ref[0, hh] = rb[:, SW_WIDTH + lo:SW_WIDTH + lo + HEAD_DIM].astype(BF16)
            vo = ROPE_LO + ROPE_WIDTH + lo
            vb_ref[0, hh] = proj[:, vo:vo + HEAD_DIM].astype(BF16)

    hs = lambda nh: jax.ShapeDtypeStruct((B, nh, S, HEAD_DIM), BF16)
    return pl.pallas_call(
        body, name="attn_in", grid=(T // TOKEN_TILE,),
        in_specs=[_tok_spec(D), _mod_spec(tps, D), _full((1, D)), _full(w_in.shape),
                  pl.BlockSpec((TOKEN_TILE, ROPE_WIDTH), lambda i: (i % tps, 0)),
                  pl.BlockSpec((TOKEN_TILE, ROPE_WIDTH), lambda i: (i % tps, 0))],
        out_specs=[_tok_spec(D), _heads_spec(8, tps), _heads_spec(8, tps), _heads_spec(8, tps),
                   _heads_spec(8, tps), _heads_spec(2, tps), _heads_spec(2, tps)],
        out_shape=[jax.ShapeDtypeStruct((T, D), BF16), hs(8), hs(8), hs(8), hs(8), hs(2), hs(2)],
        compiler_params=_params("parallel"),
    )(x2d, mod3, g_attn, w_in, cos_t, sin_t)


def _cat_heads(ref, nh):
    return jnp.concatenate([ref[0, hh] for hh in range(nh)], axis=1)


def _attn_out(oa, ob, x2d, mod3, g_na, g_sw, w_out, S):
    T, D = x2d.shape
    tps = S // TOKEN_TILE

    def body(oa_ref, ob_ref, x_ref, mod_ref, gna_ref, gsw_ref, w_ref, mixin_ref, mix_ref, x1_ref):
        oan, _ = _rms(_cat_heads(oa_ref, NA_HEADS))
        obn, _ = _rms(_cat_heads(ob_ref, SW_HEADS))
        mixin = jnp.concatenate([oan * gna_ref[...], obn * gsw_ref[...]], axis=1).astype(BF16)
        mixin_ref[...] = mixin
        mix = _nn(mixin, w_ref[...])
        mix_ref[...] = mix
        x1_ref[...] = x_ref[...] + mod_ref[0, 2:3, :] * mix

    return pl.pallas_call(
        body, name="attn_out", grid=(T // TOKEN_TILE,),
        in_specs=[_heads_spec(8, tps), _heads_spec(8, tps), _tok_spec(D), _mod_spec(tps, D),
                  _full((1, NA_WIDTH)), _full((1, SW_WIDTH)), _full(w_out.shape)],
        out_specs=[_tok_spec(NA_WIDTH + SW_WIDTH), _tok_spec(D), _tok_spec(D)],
        out_shape=[jax.ShapeDtypeStruct((T, NA_WIDTH + SW_WIDTH), BF16), jax.ShapeDtypeStruct((T, D), F32),
                   jax.ShapeDtypeStruct((T, D), F32)],
        compiler_params=_params("parallel"),
    )(oa, ob, x2d, mod3, g_na, g_sw, w_out)


def _ffn_up(x1, mod3, g_ffn, w_up, S):
    T, D = x1.shape
    F = w_up.shape[1] // 2
    tps = S // TOKEN_TILE

    def body(x1_ref, mod_ref, g_ref, w_ref, h2_ref, val_ref, gt_ref):
        xn, _ = _rms(x1_ref[...])
        h2 = ((xn * g_ref[...]) * (1.0 + mod_ref[0, 4:5, :]) + mod_ref[0, 3:4, :]).astype(BF16)
        h2_ref[...] = h2
        u = _nn(h2, w_ref[...])
        val_ref[...] = u[:, :F].astype(BF16)
        gt_ref[...] = u[:, F:].astype(BF16)

    return pl.pallas_call(
        body, name="ffn_up", grid=(T // TOKEN_TILE,),
        in_specs=[_tok_spec(D), _mod_spec(tps, D), _full((1, D)), _full(w_up.shape)],
        out_specs=[_tok_spec(D), _tok_spec(F), _tok_spec(F)],
        out_shape=[jax.ShapeDtypeStruct((T, D), BF16), jax.ShapeDtypeStruct((T, F), BF16), jax.ShapeDtypeStruct((T, F), BF16)],
        compiler_params=_params("parallel"),
    )(x1, mod3, g_ffn, w_up)


def _halo_specs(T, tps, w):
    per = TOKEN_TILE // 8
    prev = pl.BlockSpec((8, w), lambda i: (jnp.maximum(i * per - 1, 0), 0))
    nxt = pl.BlockSpec((8, w), lambda i: (jnp.minimum((i + 1) * per, T // 8 - 1), 0))
    return prev, nxt


def _seq_shifts(cur, prev_ref, next_ref, tps):
    ti = pl.program_id(0) % tps
    row = lax.broadcasted_iota(jnp.int32, cur.shape, 0)
    before = jnp.where(ti > 0, prev_ref[7:8, :].astype(F32), 0.0)
    after = jnp.where(ti < tps - 1, next_ref[0:1, :].astype(F32), 0.0)
    down = jnp.where(row == 0, before, pltpu.roll(cur, 1, axis=0))
    up = jnp.where(row == cur.shape[0] - 1, after, pltpu.roll(cur, cur.shape[0] - 1, axis=0))
    return down, up


def _conv_gate(gt_ref, prev_ref, next_ref, cw_ref, cb_ref, tps):
    g = gt_ref[...].astype(F32)
    gprev, gnext = _seq_shifts(g, prev_ref, next_ref, tps)
    gc = gprev * cw_ref[0:1, :] + g * cw_ref[1:2, :] + gnext * cw_ref[2:3, :] + cb_ref[...]
    sig = 1.0 / (1.0 + jnp.exp(-gc))
    return g, gprev, gnext, gc, sig


def _ffn_down(gt, val, conv_w, conv_b, w_down, x1, mod3, g_final, target, B, S):
    T, D = x1.shape
    F = gt.shape[1]
    tps = S // TOKEN_TILE
    prev, nxt = _halo_specs(T, tps, F)

    def body(gt_ref, prev_ref, next_ref, val_ref, cw_ref, cb_ref, w_ref, x1_ref, mod_ref, gf_ref, tgt_ref,
             a_ref, dx2_ref, df_ref, gstat_ref, bstat_ref):
        i = pl.program_id(0)
        _, _, _, gc, sig = _conv_gate(gt_ref, prev_ref, next_ref, cw_ref, cb_ref, tps)
        a = (gc * sig * val_ref[...].astype(F32)).astype(BF16)
        a_ref[...] = a
        f = _nn(a, w_ref[...])
        gate = mod_ref[0, 5:6, :]
        x2 = x1_ref[...] + gate * f
        xn, r = _rms(x2)
        err = xn * gf_ref[...] - tgt_ref[...]
        dy = err * (1.0 / D)
        dx2 = _rms_bwd(xn, r, dy * gf_ref[...])
        dx2_ref[...] = dx2
        df_ref[...] = (gate * dx2).astype(BF16)

        @pl.when(i == 0)
        def _():
            gstat_ref[...] = jnp.zeros_like(gstat_ref)

        @pl.when(i % tps == 0)
        def _():
            bstat_ref[...] = jnp.zeros_like(bstat_ref)

        gstat_ref[0:1, :] += jnp.sum(dy * xn, axis=0, keepdims=True)
        tile_loss = jnp.sum(jnp.sum(err * err, axis=1, keepdims=True), axis=0, keepdims=True) * (0.5 / D)
        gstat_ref[1:2, :] += jnp.broadcast_to(tile_loss, (1, D))
        bstat_ref[0, 0:1, :] += jnp.sum(dx2 * f, axis=0, keepdims=True)

    return pl.pallas_call(
        body, name="ffn_down", grid=(T // TOKEN_TILE,),
        in_specs=[_tok_spec(F), prev, nxt, _tok_spec(F), _full(conv_w.shape), _full((1, F)), _full(w_down.shape),
                  _tok_spec(D), _mod_spec(tps, D), _full((1, D)), _tok_spec(D)],
        out_specs=[_tok_spec(F), _tok_spec(D), _tok_spec(D), _full((8, D)), _bstat_spec(tps, D)],
        out_shape=[jax.ShapeDtypeStruct((T, F), BF16), jax.ShapeDtypeStruct((T, D), F32), jax.ShapeDtypeStruct((T, D), BF16),
                   jax.ShapeDtypeStruct((8, D), F32), jax.ShapeDtypeStruct((B, 8, D), F32)],
        compiler_params=_params("arbitrary"),
    )(gt, gt, gt, val, conv_w, conv_b, w_down, x1, mod3, g_final, target)


def _ffn_down_bwd(df, w_down, gt, val, conv_w, conv_b, S):
    T, D = df.shape
    F = gt.shape[1]
    tps = S // TOKEN_TILE
    prev, nxt = _halo_specs(T, tps, F)

    def body(df_ref, w_ref, gt_ref, prev_ref, next_ref, val_ref, cw_ref, cb_ref, dval_ref, dgc_ref, cstat_ref):
        g, gprev, gnext, gc, sig = _conv_gate(gt_ref, prev_ref, next_ref, cw_ref, cb_ref, tps)
        da = _nt(df_ref[...], w_ref[...])
        dval_ref[...] = (da * (gc * sig)).astype(BF16)
        dgc = da * val_ref[...].astype(F32) * (sig * (1.0 + gc * (1.0 - sig)))
        dgc_ref[...] = dgc.astype(BF16)

        @pl.when(pl.program_id(0) == 0)
        def _():
            cstat_ref[...] = jnp.zeros_like(cstat_ref)

        cstat_ref[0:1, :] += jnp.sum(dgc, axis=0, keepdims=True)
        cstat_ref[1:2, :] += jnp.sum(dgc * gprev, axis=0, keepdims=True)
        cstat_ref[2:3, :] += jnp.sum(dgc * g, axis=0, keepdims=True)
        cstat_ref[3:4, :] += jnp.sum(dgc * gnext, axis=0, keepdims=True)

    return pl.pallas_call(
        body, name="ffn_down_bwd", grid=(T // TOKEN_TILE,),
        in_specs=[_tok_spec(D), _full(w_down.shape), _tok_spec(F), prev, nxt, _tok_spec(F), _full(conv_w.shape), _full((1, F))],
        out_specs=[_tok_spec(F), _tok_spec(F), _full((8, F))],
        out_shape=[jax.ShapeDtypeStruct((T, F), BF16), jax.ShapeDtypeStruct((T, F), BF16), jax.ShapeDtypeStruct((8, F), F32)],
        compiler_params=_params("arbitrary"),
    )(df, w_down, gt, gt, gt, val, conv_w, conv_b)


def _ffn_up_bwd(dgc, dval, conv_w, w_up, x1, mod3, g_ffn, dx2, mix, B, S):
    T, D = x1.shape
    F = dgc.shape[1]
    tps = S // TOKEN_TILE
    prev, nxt = _halo_specs(T, tps, F)

    def body(dgc_ref, prev_ref, next_ref, dval_ref, cw_ref, w_ref, x1_ref, mod_ref, g_ref, dx2_ref, mix_ref,
             du_ref, dx1_ref, dmix_ref, gstat_ref, bstat_ref):
        i = pl.program_id(0)
        d = dgc_ref[...].astype(F32)
        dprev, dnext = _seq_shifts(d, prev_ref, next_ref, tps)
        dgt = dnext * cw_ref[0:1, :] + d * cw_ref[1:2, :] + dprev * cw_ref[2:3, :]
        du = jnp.concatenate([dval_ref[...], dgt.astype(BF16)], axis=1)
        du_ref[...] = du
        dh2 = _nt(du, w_ref[...])
        xn, r = _rms(x1_ref[...])
        scale1 = 1.0 + mod_ref[0, 4:5, :]
        xg = xn * g_ref[...]
        dx1 = dx2_ref[...] + _rms_bwd(xn, r, dh2 * g_ref[...] * scale1)
        dx1_ref[...] = dx1
        dmix_ref[...] = (mod_ref[0, 2:3, :] * dx1).astype(BF16)

        @pl.when(i == 0)
        def _():
            gstat_ref[...] = jnp.zeros_like(gstat_ref)

        @pl.when(i % tps == 0)
        def _():
            bstat_ref[...] = jnp.zeros_like(bstat_ref)

        gstat_ref[0:1, :] += jnp.sum(dh2 * scale1 * xn, axis=0, keepdims=True)
        bstat_ref[0, 0:1, :] += jnp.sum(dh2, axis=0, keepdims=True)
        bstat_ref[0, 1:2, :] += jnp.sum(dh2 * xg, axis=0, keepdims=True)
        bstat_ref[0, 2:3, :] += jnp.sum(dx1 * mix_ref[...], axis=0, keepdims=True)

    return pl.pallas_call(
        body, name="ffn_up_bwd", grid=(T // TOKEN_TILE,),
        in_specs=[_tok_spec(F), prev, nxt, _tok_spec(F), _full(conv_w.shape), _full(w_up.shape), _tok_spec(D),
                  _mod_spec(tps, D), _full((1, D)), _tok_spec(D), _tok_spec(D)],
        out_specs=[_tok_spec(2 * F), _tok_spec(D), _tok_spec(D), _full((8, D)), _bstat_spec(tps, D)],
        out_shape=[jax.ShapeDtypeStruct((T, 2 * F), BF16), jax.ShapeDtypeStruct((T, D), F32), jax.ShapeDtypeStruct((T, D), BF16),
                   jax.ShapeDtypeStruct((8, D), F32), jax.ShapeDtypeStruct((B, 8, D), F32)],
        compiler_params=_params("arbitrary"),
    )(dgc, dgc, dgc, dval, conv_w, w_up, x1, mod3, g_ffn, dx2, mix)


def _attn_out_bwd(dmix, w_out, oa, ob, g_na, g_sw, B, S):
    T, D = dmix.shape
    tps = S // TOKEN_TILE

    def body(dmix_ref, w_ref, oa_ref, ob_ref, gna_ref, gsw_ref, doa_ref, dob_ref, gstat_ref):
        dmixin = _nt(dmix_ref[...], w_ref[...])

        @pl.when(pl.program_id(0) == 0)
        def _():
            gstat_ref[...] = jnp.zeros_like(gstat_ref)

        for k, (o_ref, g_ref, do_ref, nh) in enumerate(((oa_ref, gna_ref, doa_ref, NA_HEADS), (ob_ref, gsw_ref, dob_ref, SW_HEADS))):
            dn = dmixin[:, k * NA_WIDTH:(k + 1) * NA_WIDTH]
            on, r = _rms(_cat_heads(o_ref, nh))
            gstat_ref[k:k + 1, :] += jnp.sum(dn * on, axis=0, keepdims=True)
            do = _rms_bwd(on, r, dn * g_ref[...])
            for hh in range(nh):
                do_ref[0, hh] = do[:, hh * HEAD_DIM:(hh + 1) * HEAD_DIM].astype(BF16)

    hs = jax.ShapeDtypeStruct((B, 8, S, HEAD_DIM), BF16)
    return pl.pallas_call(
        body, name="attn_out_bwd", grid=(T // TOKEN_TILE,),
        in_specs=[_tok_spec(D), _full(w_out.shape), _heads_spec(8, tps), _heads_spec(8, tps), _full((1, NA_WIDTH)), _full((1, SW_WIDTH))],
        out_specs=[_heads_spec(8, tps), _heads_spec(8, tps), _full((8, NA_WIDTH))],
        out_shape=[hs, hs, jax.ShapeDtypeStruct((8, NA_WIDTH), F32)],
        compiler_params=_params("arbitrary"),
    )(dmix, w_out, oa, ob, g_na, g_sw)


def _attn_in_bwd(dqa, dka, dva, dqb, dkb, dvb, cos_t, sin_t, w_in, x2d, mod3, g_attn, dx1, B, S):
    T, D = x2d.shape
    tps = S // TOKEN_TILE

    def body(dqa_ref, dka_ref, dva_ref, dqb_ref, dkb_ref, dvb_ref, cos_ref, sin_ref, w_ref, x_ref, mod_ref, g_ref, dx1_ref,
             gx_ref, dproj_ref, gstat_ref, bstat_ref):
        i = pl.program_id(0)
        drb = jnp.concatenate([_cat_heads(dqb_ref, SW_HEADS) * Q_SCALE, _cat_heads(dkb_ref, SW_KV_HEADS)], axis=1)
        drb = drb * cos_ref[...] + _rot_half(drb * sin_ref[...])
        dproj = jnp.concatenate([_cat_heads(dqa_ref, NA_HEADS) * Q_SCALE, _cat_heads(dka_ref, NA_HEADS),
                                 _cat_heads(dva_ref, NA_HEADS), drb, _cat_heads(dvb_ref, SW_KV_HEADS)], axis=1).astype(BF16)
        dproj_ref[...] = dproj
        dh = _nt(dproj, w_ref[...])
        xn, r = _rms(x_ref[...])
        scale1 = 1.0 + mod_ref[0, 1:2, :]
        gx_ref[...] = dx1_ref[...] + _rms_bwd(xn, r, dh * g_ref[...] * scale1)

        @pl.when(i == 0)
        def _():
            gstat_ref[...] = jnp.zeros_like(gstat_ref)

        @pl.when(i % tps == 0)
        def _():
            bstat_ref[...] = jnp.zeros_like(bstat_ref)

        gstat_ref[0:1, :] += jnp.sum(dh * scale1 * xn, axis=0, keepdims=True)
        bstat_ref[0, 0:1, :] += jnp.sum(dh, axis=0, keepdims=True)
        bstat_ref[0, 1:2, :] += jnp.sum(dh * (xn * g_ref[...]), axis=0, keepdims=True)

    rope = pl.BlockSpec((TOKEN_TILE, ROPE_WIDTH), lambda i: (i % tps, 0))
    return pl.pallas_call(
        body, name="attn_in_bwd", grid=(T // TOKEN_TILE,),
        in_specs=[_heads_spec(8, tps), _heads_spec(8, tps), _heads_spec(8, tps), _heads_spec(8, tps), _heads_spec(2, tps),
                  _heads_spec(2, tps), rope, rope, _full(w_in.shape), _tok_spec(D), _mod_spec(tps, D), _full((1, D)), _tok_spec(D)],
        out_specs=[_tok_spec(D), _tok_spec(IN_WIDTH), _full((8, D)), _bstat_spec(tps, D)],
        out_shape=[jax.ShapeDtypeStruct((T, D), F32), jax.ShapeDtypeStruct((T, IN_WIDTH), BF16),
                   jax.ShapeDtypeStruct((8, D), F32), jax.ShapeDtypeStruct((B, 8, D), F32)],
        compiler_params=_params("arbitrary"),
    )(dqa, dka, dva, dqb, dkb, dvb, cos_t, sin_t, w_in, x2d, mod3, g_attn, dx1)


def _matmul_tn(a, b, name, tn=None, tk=512):
    T, M = a.shape
    N = b.shape[1]
    tn = N if tn is None else tn

    def body(a_ref, b_ref, o_ref):
        @pl.when(pl.program_id(1) == 0)
        def _():
            o_ref[...] = jnp.zeros_like(o_ref)

        o_ref[...] += _tn(a_ref[...], b_ref[...])

    return pl.pallas_call(
        body, name=name, grid=(N // tn, T // tk),
        in_specs=[pl.BlockSpec((tk, M), lambda j, k: (k, 0)), pl.BlockSpec((tk, tn), lambda j, k: (k, j))],
        out_specs=pl.BlockSpec((M, tn), lambda j, k: (0, j)),
        out_shape=jax.ShapeDtypeStruct((M, N), F32),
        compiler_params=_params("parallel", "arbitrary"),
    )(a, b)


def _na_geometry(S):
    rows = S // GRID_W
    wr = min(NA_ROWS_MAX, rows)
    return rows, wr


def _na_window(r, rows, wr):
    rs = jnp.clip(r - wr // 2, 0, rows - wr)
    return pl.multiple_of(rs * GRID_W, GRID_W), rs - r + NA_ROWS_MAX - 1


def _na_fwd(qa, ka, va, bias):
    B, H, S, _ = qa.shape
    rows, wr = _na_geometry(S)
    kw_n = wr * GRID_W

    def body(q_ref, k_ref, v_ref, b_ref, o_ref):
        def step(r, carry):
            start, sidx = _na_window(r, rows, wr)
            qrows = pl.ds(pl.multiple_of(r * GRID_W, GRID_W), GRID_W)
            s = _nt(q_ref[0, 0, qrows, :], k_ref[0, 0, pl.ds(start, kw_n), :]) + b_ref[0, sidx]
            p = jnp.exp(s - jnp.max(s, axis=1, keepdims=True))
            pn = (p / jnp.sum(p, axis=1, keepdims=True)).astype(BF16)
            o_ref[0, 0, qrows, :] = _nn(pn, v_ref[0, 0, pl.ds(start, kw_n), :])
            return carry

        lax.fori_loop(0, rows, step, 0)

    head = pl.BlockSpec((1, 1, S, HEAD_DIM), lambda b, h: (b, h, 0, 0))
    return pl.pallas_call(
        body, name="na_fwd", grid=(B, H),
        in_specs=[head, head, head, pl.BlockSpec((1, NA_ROWS_MAX, GRID_W, kw_n), lambda b, h: (h, 0, 0, 0))],
        out_specs=head, out_shape=jax.ShapeDtypeStruct((B, H, S, HEAD_DIM), F32),
        compiler_params=_params("parallel", "parallel"),
    )(qa, ka, va, bias)


def _na_bwd(qa, ka, va, bias, doa):
    B, H, S, _ = qa.shape
    rows, wr = _na_geometry(S)
    kw_n = wr * GRID_W

    def body(q_ref, k_ref, v_ref, b_ref, do_ref, dq_ref, dk_ref, dv_ref, dtz_ref):
        @pl.when(pl.program_id(1) == 0)
        def _():
            dtz_ref[...] = jnp.zeros_like(dtz_ref)

        dk_ref[...] = jnp.zeros_like(dk_ref)
        dv_ref[...] = jnp.zeros_like(dv_ref)

        def step(r, carry):
            start, sidx = _na_window(r, rows, wr)
            qrows = pl.ds(pl.multiple_of(r * GRID_W, GRID_W), GRID_W)
            krows = pl.ds(start, kw_n)
            q, kw, vw, do = q_ref[0, 0, qrows, :], k_ref[0, 0, krows, :], v_ref[0, 0, krows, :], do_ref[0, 0, qrows, :]
            s = _nt(q, kw) + b_ref[0, sidx]
            p = jnp.exp(s - jnp.max(s, axis=1, keepdims=True))
            pn = p / jnp.sum(p, axis=1, keepdims=True)
            dp = _nt(do, vw)
            ds = pn * (dp - jnp.sum(pn * dp, axis=1, keepdims=True))
            dsb = ds.astype(BF16)
            dq_ref[0, 0, qrows, :] = _nn(dsb, kw)
            dk_ref[0, 0, krows, :] += _tn(dsb, q)
            dv_ref[0, 0, krows, :] += _tn(pn.astype(BF16), do)
            for j in range(wr):
                dtz_ref[0, sidx + j] += ds[:, j * GRID_W:(j + 1) * GRID_W]
            return carry

        lax.fori_loop(0, rows, step, 0)

    head = pl.BlockSpec((1, 1, S, HEAD_DIM), lambda h, b: (b, h, 0, 0))
    hs = jax.ShapeDtypeStruct((B, H, S, HEAD_DIM), F32)
    return pl.pallas_call(
        body, name="na_bwd", grid=(H, B),
        in_specs=[head, head, head, pl.BlockSpec((1, NA_ROWS_MAX, GRID_W, kw_n), lambda h, b: (h, 0, 0, 0)), head],
        out_specs=[head, head, head, pl.BlockSpec((1, N_DR, GRID_W, GRID_W), lambda h, b: (h, 0, 0, 0))],
        out_shape=[hs, hs, hs, jax.ShapeDtypeStruct((H, N_DR, GRID_W, GRID_W), F32)],
        compiler_params=_params("parallel", "arbitrary"),
    )(qa, ka, va, bias, doa)


def _sw_band(n, S):
    kw_n = 3 * SW_BLOCK
    start = pl.multiple_of(jnp.clip(n * SW_BLOCK - SW_BLOCK, 0, S - kw_n), SW_BLOCK)
    qpos = n * SW_BLOCK + lax.broadcasted_iota(jnp.int32, (SW_BLOCK, kw_n), 0)
    kpos = start + lax.broadcasted_iota(jnp.int32, (SW_BLOCK, kw_n), 1)
    return start, jnp.abs(qpos - kpos) <= SW_WINDOW


def _sw_probs(q, kw, ok, sk):
    s = jnp.where(ok, _nt(q, kw), NEG)
    m = jnp.maximum(jnp.max(s, axis=1, keepdims=True), sk)
    p = jnp.exp(s - m)
    esk = jnp.exp(sk - m)
    inv = 1.0 / (jnp.sum(p, axis=1, keepdims=True) + esk)
    return p * inv, esk * inv


def _sw_fwd(sink, qb, kb, vb):
    B, _, S, _ = qb.shape
    kw_n = 3 * SW_BLOCK

    def body(sink_ref, q_ref, k_ref, v_ref, o_ref):
        kvh = pl.program_id(1)

        def step(n, carry):
            start, ok = _sw_band(n, S)
            qrows = pl.ds(pl.multiple_of(n * SW_BLOCK, SW_BLOCK), SW_BLOCK)
            kw, vw = k_ref[0, 0, pl.ds(start, kw_n), :], v_ref[0, 0, pl.ds(start, kw_n), :]
            for g in range(SW_GROUP):
                pn, _ = _sw_probs(q_ref[0, g, qrows, :], kw, ok, sink_ref[kvh * SW_GROUP + g])
                o_ref[0, g, qrows, :] = _nn(pn.astype(BF16), vw)
            return carry

        lax.fori_loop(0, S // SW_BLOCK, step, 0)

    qs = pl.BlockSpec((1, SW_GROUP, S, HEAD_DIM), lambda b, h: (b, h, 0, 0))
    ks = pl.BlockSpec((1, 1, S, HEAD_DIM), lambda b, h: (b, h, 0, 0))
    return pl.pallas_call(
        body, name="sw_fwd", grid=(B, SW_KV_HEADS),
        in_specs=[pl.BlockSpec(memory_space=pltpu.SMEM), qs, ks, ks],
        out_specs=qs, out_shape=jax.ShapeDtypeStruct((B, SW_HEADS, S, HEAD_DIM), F32),
        compiler_params=_params("parallel", "parallel"),
    )(sink, qb, kb, vb)


def _sw_bwd(sink, qb, kb, vb, dob):
    B, _, S, _ = qb.shape
    kw_n = 3 * SW_BLOCK

    def body(sink_ref, q_ref, k_ref, v_ref, do_ref, dq_ref, dk_ref, dv_ref, dsink_ref):
        kvh = pl.program_id(0)

        @pl.when(pl.program_id(1) == 0)
        def _():
            dsink_ref[...] = jnp.zeros_like(dsink_ref)

        dk_ref[...] = jnp.zeros_like(dk_ref)
        dv_ref[...] = jnp.zeros_like(dv_ref)

        def step(n, carry):
            start, ok = _sw_band(n, S)
            qrows = pl.ds(pl.multiple_of(n * SW_BLOCK, SW_BLOCK), SW_BLOCK)
            krows = pl.ds(start, kw_n)
            kw, vw = k_ref[0, 0, krows, :], v_ref[0, 0, krows, :]
            for g in range(SW_GROUP):
                q, do = q_ref[0, g, qrows, :], do_ref[0, g, qrows, :]
                pn, psink = _sw_probs(q, kw, ok, sink_ref[kvh * SW_GROUP + g])
                dp = _nt(do, vw)
                delta = jnp.sum(pn * dp, axis=1, keepdims=True)
                dsb = (pn * (dp - delta)).astype(BF16)
                dq_ref[0, g, qrows, :] = _nn(dsb, kw)
                dk_ref[0, 0, krows, :] += _tn(dsb, q)
                dv_ref[0, 0, krows, :] += _tn(pn.astype(BF16), do)
                dsink_ref[0, g:g + 1, :] += jnp.broadcast_to(-jnp.sum(psink * delta, axis=0, keepdims=True), (1, 128))
            return carry

        lax.fori_loop(0, S // SW_BLOCK, step, 0)

    qs = pl.BlockSpec((1, SW_GROUP, S, HEAD_DIM), lambda h, b: (b, h, 0, 0))
    ks = pl.BlockSpec((1, 1, S, HEAD_DIM), lambda h, b: (b, h, 0, 0))
    return pl.pallas_call(
        body, name="sw_bwd", grid=(SW_KV_HEADS, B),
        in_specs=[pl.BlockSpec(memory_space=pltpu.SMEM), qs, ks, ks, qs],
        out_specs=[qs, ks, ks, pl.BlockSpec((1, SW_GROUP, 128), lambda h, b: (h, 0, 0))],
        out_shape=[jax.ShapeDtypeStruct((B, SW_HEADS, S, HEAD_DIM), F32), jax.ShapeDtypeStruct((B, SW_KV_HEADS, S, HEAD_DIM), F32),
                   jax.ShapeDtypeStruct((B, SW_KV_HEADS, S, HEAD_DIM), F32), jax.ShapeDtypeStruct((SW_KV_HEADS, SW_GROUP, 128), F32)],
        compiler_params=_params("parallel", "arbitrary"),
    )(sink, qb, kb, vb, dob)


def _pack_sum(packs):
    W = packs.shape[1]

    def body(p_ref, o_ref):
        tot = p_ref[0:8, :]
        for d in range(1, N_DEV):
            tot = tot + p_ref[8 * d:8 * d + 8, :]
        o_ref[...] = tot
        o_ref[5:6, :] = tot[0:1, :] + tot[1:2, :]

    return pl.pallas_call(body, name="pack_sum", out_shape=jax.ShapeDtypeStruct((8, W), F32),
                          compiler_params=pltpu.CompilerParams(vmem_limit_bytes=VMEM_LIMIT))(packs)


def _adamw(w, g, m, v, name):
    c1 = 1.0 - ADAM_B1 ** ADAM_STEP
    c2 = 1.0 - ADAM_B2 ** ADAM_STEP

    def body(w_ref, g_ref, m_ref, v_ref, d_ref, nm_ref, nv_ref):
        g_ = g_ref[...]
        nm = ADAM_B1 * m_ref[...] + (1.0 - ADAM_B1) * g_
        nv = ADAM_B2 * v_ref[...] + (1.0 - ADAM_B2) * (g_ * g_)
        nm_ref[...] = nm
        nv_ref[...] = nv
        d_ref[...] = -ADAM_LR * ((nm / c1) / (jnp.sqrt(nv / c2) + ADAM_EPS) + ADAM_WD * w_ref[...])

    s = jax.ShapeDtypeStruct(w.shape, F32)
    return pl.pallas_call(body, name=name, out_shape=[s, s, s],
                          compiler_params=pltpu.CompilerParams(vmem_limit_bytes=VMEM_LIMIT))(w, g, m, v)


def _gather_cols(shard_bf, name):
    g = _all_gather(shard_bf, name)
    return jnp.transpose(g, (1, 0, 2)).reshape(shard_bf.shape[0], N_DEV * shard_bf.shape[1])


def _gather_rows(shard_bf, name):
    g = _all_gather(shard_bf, name)
    return g.reshape(N_DEV * shard_bf.shape[0], shard_bf.shape[1])


def _scatter_cols(dw, name):
    R, C8 = dw.shape
    g8 = jnp.transpose(dw.reshape(R, N_DEV, C8 // N_DEV), (1, 0, 2)).astype(BF16)
    return _reduce_scatter(g8, name)


def _scatter_rows(dw, name):
    R8, C = dw.shape
    return _reduce_scatter(dw.reshape(N_DEV, R8 // N_DEV, C).astype(BF16), name)


def _local_step(x, mod, g_attn, w_in, bias, sw_sink, g_na_out, g_sw_out, w_out, g_ffn, w_up, conv_w, conv_b, w_down,
                g_final, target):
    B, S, D = x.shape
    T = B * S
    x2d = x.reshape(T, D)
    mod3 = mod.reshape(B, 6, D)
    cos_t, sin_t = _rope_tables(S)
    sink = sw_sink.reshape(SW_HEADS)

    h, qa, ka, va, qb, kb, vb = _attn_in(x2d, mod3, g_attn, w_in, cos_t, sin_t, B, S)
    oa = _na_fwd(qa, ka, va, bias)
    ob = _sw_fwd(sink, qb, kb, vb)
    mixin, mix, x1 = _attn_out(oa, ob, x2d, mod3, g_na_out, g_sw_out, w_out, S)
    h2, val, gt = _ffn_up(x1, mod3, g_ffn, w_up, S)
    a, dx2, df, gstat_f, bstat_f = _ffn_down(gt, val, conv_w, conv_b, w_down, x1, mod3, g_final, target.reshape(T, D), B, S)

    dval, dgc, cstat = _ffn_down_bwd(df, w_down, gt, val, conv_w, conv_b, S)
    du, dx1, dmix, gstat_u, bstat_u = _ffn_up_bwd(dgc, dval, conv_w, w_up, x1, mod3, g_ffn, dx2, mix, B, S)
    doa, dob, gstat_o = _attn_out_bwd(dmix, w_out, oa, ob, g_na_out, g_sw_out, B, S)
    dqa, dka, dva, dtz = _na_bwd(qa, ka, va, bias, doa)
    dqb, dkb, dvb, dsink = _sw_bwd(sink, qb, kb, vb, dob)
    grad_x, dproj, gstat_i, bstat_i = _attn_in_bwd(dqa, dka, dva, dqb, dkb, dvb, cos_t, sin_t, w_in, x2d, mod3, g_attn, dx1, B, S)

    F = val.shape[1]
    dw_in = _matmul_tn(h, dproj, "dw_in")
    dw_out = _matmul_tn(mixin, dmix, "dw_out")
    dw_up = _matmul_tn(h2, du, "dw_up", tn=F)
    dw_down = _matmul_tn(a, df, "dw_down")

    dmod = jnp.stack([bstat_i[:, 0], bstat_i[:, 1], bstat_u[:, 2], bstat_u[:, 0], bstat_u[:, 1], bstat_f[:, 0]], axis=1)
    small = dict(g_attn=gstat_i[0], g_ffn=gstat_u[0], g_final=gstat_f[0], loss=gstat_f[1, 0], g_na_out=gstat_o[0], g_sw_out=gstat_o[1],
                 sw_sink=dsink[:, :, 0].reshape(SW_HEADS), conv_b=cstat[0], conv_w=cstat[1:4], dtz=dtz)
    return grad_x.reshape(B, S, D), dict(w_in=dw_in, w_out=dw_out, w_up=dw_up, w_down=dw_down), dmod, small


def _pad_lanes(v, w):
    return jnp.pad(v, (0, w - v.shape[0]))


def kernel(x, c, w_ada, b_ada, g_attn, w_in, na_rpb, sw_sink, g_na_out, g_sw_out, w_out, g_ffn, w_up, conv_w, conv_b, w_down, g_final, loss_target, m_w_ada, m_b_ada, m_g_attn, m_w_in, m_na_rpb, m_sw_sink, m_g_na_out, m_g_sw_out, m_w_out, m_g_ffn, m_w_up, m_conv_w, m_conv_b, m_w_down, m_g_final, v_w_ada, v_b_ada, v_g_attn, v_w_in, v_na_rpb, v_sw_sink, v_g_na_out, v_g_sw_out, v_w_out, v_g_ffn, v_w_up, v_conv_w, v_conv_b, v_w_down, v_g_final):
    B, S, D = x.shape
    me = 4 * lax.axis_index("x") + 2 * lax.axis_index("y") + lax.axis_index("c")
    ada_c = w_ada.shape[2]
    F_l = conv_w.shape[2]
    onehot = _toeplitz_onehot()

    cw_l = jnp.pad(conv_w[0], ((0, 8 - conv_w.shape[1]), (0, 0)))
    c_l = jnp.pad(c, ((0, 8 - B), (0, 0)))
    slabs = _all_gather(jnp.concatenate([c_l, cw_l], axis=1), "gather_c")
    c_all = slabs[:, :, :D].reshape(N_DEV * 8, D)
    conv_w_f = jnp.transpose(slabs[:, :3, D:], (1, 0, 2)).reshape(3, N_DEV * F_l)

    b_ada_l = lax.dynamic_slice(b_ada, (0, me * ada_c), (1, ada_c))
    mod_cols = _ada_fwd(c_all, w_ada[0], b_ada_l)
    mod_all = _all_gather(mod_cols, "gather_mod")
    mod_mine = lax.dynamic_slice(mod_all, (0, me * 8, 0), (N_DEV, B, ada_c))
    mod = jnp.transpose(mod_mine, (1, 0, 2)).reshape(B, N_DEV * ada_c)

    w_in_f = _gather_cols(w_in[0].astype(BF16), "gather_w_in")
    w_out_f = _gather_rows(w_out[0].astype(BF16), "gather_w_out")
    w_up_f = _gather_cols(w_up[0].astype(BF16), "gather_w_up")
    w_down_f = _gather_rows(w_down[0].astype(BF16), "gather_w_down")

    _, wr = _na_geometry(S)
    bias = _na_bias_table(na_rpb[0], onehot, wr)

    grad_x, dw, dmod, small = _local_step(x, mod, g_attn, w_in_f, bias, sw_sink, g_na_out, g_sw_out, w_out_f, g_ffn, w_up_f,
                                          conv_w_f, conv_b, w_down_f, g_final.reshape(1, D), loss_target)

    g_w_in = _scatter_cols(dw["w_in"], "scatter_w_in")
    g_w_out = _scatter_rows(dw["w_out"], "scatter_w_out")
    g_w_up = _scatter_cols(dw["w_up"], "scatter_w_up")
    g_w_down = _scatter_rows(dw["w_down"], "scatter_w_down")

    dtz2d = jnp.pad(small["dtz"].reshape(NA_HEADS * N_DR, GRID_W * GRID_W), ((0, 128 - NA_HEADS * N_DR), (0, 0)))
    drpb = _rpb_reduce(dtz2d, onehot)[:NA_HEADS * N_DR, :N_DC].reshape(-1)

    row2 = jnp.concatenate([small["g_attn"], small["g_ffn"], small["g_final"], small["g_na_out"], small["g_sw_out"],
                            _pad_lanes(small["sw_sink"], 128), _pad_lanes(small["loss"].reshape(1), 128)])
    rows = [dmod.reshape(B, 6 * D)[0], dmod.reshape(B, 6 * D)[1], _pad_lanes(row2, PACK_W), _pad_lanes(small["conv_b"], PACK_W),
            _pad_lanes(drpb, PACK_W)] + [jnp.zeros((PACK_W,), F32)] * 3
    packs = _all_gather(jnp.stack(rows), "gather_small")
    cw_rows = jnp.pad(small["conv_w"], ((0, 5), (0, 0)))
    cw_packs = _all_gather(cw_rows, "gather_dconv")
    tot = _pack_sum(packs.reshape(N_DEV * 8, PACK_W))
    cw_tot = _pack_sum(cw_packs.reshape(N_DEV * 8, cw_rows.shape[1]))

    o = 0
    rep = {}
    for nm, wd in (("g_attn", D), ("g_ffn", D), ("g_final", D), ("g_na_out", NA_WIDTH), ("g_sw_out", SW_WIDTH), ("sw_sink", 128), ("loss", 128)):
        rep[nm] = tot[2, o:o + wd]
        o += wd
    loss = rep["loss"][0]
    g_b_ada = tot[5:6, :]
    g_conv_b = tot[3:4, :conv_b.shape[1]]
    g_na_rpb = tot[4, :NA_HEADS * N_DR * N_DC].reshape(na_rpb.shape)
    g_conv_w = lax.dynamic_slice(cw_tot[0:3], (0, me * F_l), (3, F_l)).reshape(conv_w.shape)

    dmod_cols = lax.dynamic_slice(packs.reshape(N_DEV * 8, PACK_W), (0, me * ada_c), (N_DEV * 8, ada_c))
    g_w_ada = _ada_bwd(c_all, dmod_cols)[None]

    grads = dict(
        w_ada=g_w_ada, b_ada=g_b_ada, g_attn=rep["g_attn"][None], w_in=g_w_in[None], na_rpb=g_na_rpb,
        sw_sink=rep["sw_sink"][None, :SW_HEADS], g_na_out=rep["g_na_out"][None], g_sw_out=rep["g_sw_out"][None],
        w_out=g_w_out[None], g_ffn=rep["g_ffn"][None], w_up=g_w_up[None], conv_w=g_conv_w, conv_b=g_conv_b,
        w_down=g_w_down[None], g_final=rep["g_final"])
    weights = dict(w_ada=w_ada, b_ada=b_ada, g_attn=g_attn, w_in=w_in, na_rpb=na_rpb, sw_sink=sw_sink, g_na_out=g_na_out,
                   g_sw_out=g_sw_out, w_out=w_out, g_ffn=g_ffn, w_up=w_up, conv_w=conv_w, conv_b=conv_b, w_down=w_down, g_final=g_final)
    ms = dict(w_ada=m_w_ada, b_ada=m_b_ada, g_attn=m_g_attn, w_in=m_w_in, na_rpb=m_na_rpb, sw_sink=m_sw_sink, g_na_out=m_g_na_out,
              g_sw_out=m_g_sw_out, w_out=m_w_out, g_ffn=m_g_ffn, w_up=m_w_up, conv_w=m_conv_w, conv_b=m_conv_b, w_down=m_w_down, g_final=m_g_final)
    vs = dict(w_ada=v_w_ada, b_ada=v_b_ada, g_attn=v_g_attn, w_in=v_w_in, na_rpb=v_na_rpb, sw_sink=v_sw_sink, g_na_out=v_g_na_out,
              g_sw_out=v_g_sw_out, w_out=v_w_out, g_ffn=v_g_ffn, w_up=v_w_up, conv_w=v_conv_w, conv_b=v_conv_b, w_down=v_w_down, g_final=v_g_final)
    names = list(weights)
    deltas, new_m, new_v = {}, {}, {}
    for nm in names:
        shp = weights[nm].shape
        two_d = (shp[-2], shp[-1]) if len(shp) >= 3 and nm != "na_rpb" else (1, int(np.prod(shp)))
        r = lambda t: t.reshape(two_d)
        d_, m_, v_ = _adamw(r(weights[nm]), r(grads[nm]), r(ms[nm]), r(vs[nm]), "adamw_" + nm)
        deltas[nm], new_m[nm], new_v[nm] = d_.reshape(shp), m_.reshape(shp), v_.reshape(shp)
        grads[nm] = grads[nm].reshape(shp)
    return (loss, grad_x, *[grads[n] for n in names], *[deltas[n] for n in names], *[new_m[n] for n in names],
            *[new_v[n] for n in names])
```

```python
import functools

import numpy as np
import jax
import jax.numpy as jnp
from jax import lax
from jax.experimental import pallas as pl
from jax.experimental.pallas import tpu as pltpu

F32, BF16 = jnp.float32, jnp.bfloat16
MESH_ID = pl.DeviceIdType.MESH
N_DEV = 8

HEAD_DIM = 64
NA_HEADS = 8
SW_HEADS = 8
SW_KV_HEADS = 2
SW_GROUP = SW_HEADS // SW_KV_HEADS
NA_WIDTH = NA_HEADS * HEAD_DIM
SW_WIDTH = SW_HEADS * HEAD_DIM
SW_KV_WIDTH = SW_KV_HEADS * HEAD_DIM
ROPE_WIDTH = SW_WIDTH + SW_KV_WIDTH
IN_WIDTH = 3 * NA_WIDTH + SW_WIDTH + 2 * SW_KV_WIDTH
ROPE_LO = 3 * NA_WIDTH
GRID_W = 64
NA_ROWS_MAX = 8
NA_COLS = 16
N_DR = 2 * NA_ROWS_MAX - 1
N_DC = 2 * NA_COLS - 1
SW_WINDOW = 128
SW_BLOCK = 128
ROPE_THETA = 10000.0
EPS = 1e-6
NEG = -1e30
Q_SCALE = HEAD_DIM ** -0.5

ADAM_LR = 0.001
ADAM_B1 = 0.9
ADAM_B2 = 0.999
ADAM_EPS = 1e-08
ADAM_WD = 0.01
ADAM_STEP = 10

TOKEN_TILE = 256
VMEM_LIMIT = 56 * 1024 * 1024

PACK_W = 6144


def _nn(a, b):
    return jnp.dot(a, b, preferred_element_type=F32)


def _nt(a, b):
    return lax.dot_general(a, b, (((1,), (1,)), ((), ())), preferred_element_type=F32)


def _tn(a, b):
    return lax.dot_general(a, b, (((0,), (0,)), ((), ())), preferred_element_type=F32)


def _rms(x):
    r = lax.rsqrt(jnp.mean(x * x, axis=-1, keepdims=True) + EPS)
    return x * r, r


def _rms_bwd(xn, r, gy):
    return r * (gy - xn * jnp.mean(xn * gy, axis=-1, keepdims=True))


def _params(*sem):
    return pltpu.CompilerParams(dimension_semantics=sem, vmem_limit_bytes=VMEM_LIMIT)


def _full(shape):
    n = len(shape)
    return pl.BlockSpec(shape, lambda *_: (0,) * n)


def _mesh_pos():
    return lax.axis_index("x"), lax.axis_index("y"), lax.axis_index("c")


def _all_gather(x, name):
    def body(x_ref, out_ref, send_sems, recv_sems, local_sem):
        x_, y_, c_ = _mesh_pos()
        me, sibling = (x_, y_, c_), (x_, y_, 1 - c_)
        chips = [(1 - x_, y_), (x_, 1 - y_), (1 - x_, 1 - y_)]

        def rows(px, py, pc):
            return out_ref.at[4 * px + 2 * py + pc]

        def copy(k, block, to, src=None):
            return pltpu.make_async_remote_copy(
                src_ref=rows(*block) if src is None else src, dst_ref=rows(*block),
                send_sem=send_sems.at[k], recv_sem=recv_sems.at[k], device_id=to, device_id_type=MESH_ID)

        mine = pltpu.make_async_copy(x_ref, rows(*me), local_sem)
        mine.start()
        first = [copy(0, me, sibling, src=x_ref)]
        first += [copy(1 + j, me, (*chip, c_), src=x_ref) for j, chip in enumerate(chips)]
        for cp in first:
            cp.start()
        passed = [copy(4 + j, (*chip, c_), sibling) for j, chip in enumerate(chips)]
        for j, chip in enumerate(chips):
            copy(1 + j, (*chip, c_), me).wait_recv()
            passed[j].start()
        copy(0, sibling, me).wait_recv()
        for j, chip in enumerate(chips):
            copy(4 + j, (*chip, 1 - c_), me).wait_recv()
        for cp in first + passed:
            cp.wait_send()
        mine.wait()

    return pl.pallas_call(
        body, name=name,
        out_shape=jax.ShapeDtypeStruct((N_DEV,) + x.shape, x.dtype),
        in_specs=[pl.BlockSpec(memory_space=pl.ANY)],
        out_specs=pl.BlockSpec(memory_space=pl.ANY),
        scratch_shapes=[pltpu.SemaphoreType.DMA((7,)), pltpu.SemaphoreType.DMA((7,)), pltpu.SemaphoreType.DMA],
    )(x)


def _row_chunk(r):
    for rc in (128, 64, 32, 16):
        if r % rc == 0:
            return rc
    raise ValueError(f"rows {r} not a multiple of 16")


def _reduce_scatter(g8, name):
    _, R, C = g8.shape
    rc = _row_chunk(R)

    def body(g_ref, out_ref, recva, sendb, recvb, sa, ra, sb, rb):
        x_, y_, c_ = _mesh_pos()
        sibling = (x_, y_, 1 - c_)
        copies_a = []
        for k in range(4):
            cp = pltpu.make_async_remote_copy(
                src_ref=g_ref.at[2 * k + (1 - c_)], dst_ref=recva.at[k],
                send_sem=sa.at[k], recv_sem=ra.at[k], device_id=sibling, device_id_type=MESH_ID)
            cp.start()
            copies_a.append(cp)
        for cp in copies_a:
            cp.wait_recv()

        def chip_sum(k, rows):
            return g_ref[2 * k + c_, rows, :].astype(F32) + recva[k, rows, :].astype(F32)

        flips = [(1 - x_, y_), (x_, 1 - y_), (1 - x_, 1 - y_)]
        copies_b = []
        for j, (tx, ty) in enumerate(flips):
            kt = 2 * tx + ty

            def fill(i, carry, j=j, kt=kt):
                rows = pl.ds(pl.multiple_of(i * rc, rc), rc)
                sendb[j, rows, :] = chip_sum(kt, rows).astype(BF16)
                return carry

            lax.fori_loop(0, R // rc, fill, 0)
            cp = pltpu.make_async_remote_copy(
                src_ref=sendb.at[j], dst_ref=recvb.at[j],
                send_sem=sb.at[j], recv_sem=rb.at[j], device_id=(tx, ty, c_), device_id_type=MESH_ID)
            cp.start()
            copies_b.append(cp)
        for cp in copies_b:
            cp.wait_recv()
        kme = 2 * x_ + y_

        def total(i, carry):
            rows = pl.ds(pl.multiple_of(i * rc, rc), rc)
            acc = chip_sum(kme, rows)
            for j in range(3):
                acc = acc + recvb[j, rows, :].astype(F32)
            out_ref[rows, :] = acc
            return carry

        lax.fori_loop(0, R // rc, total, 0)
        for cp in copies_a + copies_b:
            cp.wait_send()

    vm = pl.BlockSpec(memory_space=pltpu.VMEM)
    return pl.pallas_call(
        body, name=name,
        out_shape=jax.ShapeDtypeStruct((R, C), F32),
        in_specs=[vm], out_specs=vm,
        scratch_shapes=[pltpu.VMEM((4, R, C), BF16), pltpu.VMEM((3, R, C), BF16), pltpu.VMEM((3, R, C), BF16),
                        pltpu.SemaphoreType.DMA((4,)), pltpu.SemaphoreType.DMA((4,)),
                        pltpu.SemaphoreType.DMA((3,)), pltpu.SemaphoreType.DMA((3,))],
        compiler_params=pltpu.CompilerParams(vmem_limit_bytes=VMEM_LIMIT),
    )(g8)


def _silu(v):
    return v * (1.0 / (1.0 + jnp.exp(-v)))


def _ada_fwd(c_all, w_ada_l, b_ada_l):
    def body(c_ref, w_ref, b_ref, o_ref):
        o_ref[...] = jnp.dot(_silu(c_ref[...]), w_ref[...], precision=lax.Precision.HIGHEST,
                             preferred_element_type=F32) + b_ref[...]
    return pl.pallas_call(body, name="ada_fwd", out_shape=jax.ShapeDtypeStruct((c_all.shape[0], w_ada_l.shape[1]), F32),
                          compiler_params=pltpu.CompilerParams(vmem_limit_bytes=VMEM_LIMIT))(c_all, w_ada_l, b_ada_l)


def _ada_bwd(c_all, dmod_cols):
    def body(c_ref, d_ref, o_ref):
        o_ref[...] = lax.dot_general(_silu(c_ref[...]), d_ref[...], (((0,), (0,)), ((), ())),
                                     precision=lax.Precision.HIGHEST, preferred_element_type=F32)
    return pl.pallas_call(body, name="ada_bwd", out_shape=jax.ShapeDtypeStruct((c_all.shape[1], dmod_cols.shape[1]), F32),
                          compiler_params=pltpu.CompilerParams(vmem_limit_bytes=VMEM_LIMIT))(c_all, dmod_cols)


def _toeplitz_onehot():
    k = np.arange(GRID_W)[:, None]
    q = np.arange(GRID_W)[None, :]
    dc = np.clip(k - q + NA_COLS - 1, 0, N_DC - 1).reshape(-1)
    e = np.zeros((128, GRID_W * GRID_W), np.float32)
    e[dc, np.arange(GRID_W * GRID_W)] = 1.0
    return jnp.asarray(e)


def _rpb_expand(rpb2d, onehot):
    def body(r_ref, e_ref, o_ref):
        o_ref[...] = jnp.dot(r_ref[...], e_ref[...], precision=lax.Precision.HIGHEST, preferred_element_type=F32)
    return pl.pallas_call(body, name="rpb_expand", out_shape=jax.ShapeDtypeStruct((128, GRID_W * GRID_W), F32),
                          compiler_params=pltpu.CompilerParams(vmem_limit_bytes=VMEM_LIMIT))(rpb2d, onehot)


def _rpb_reduce(dtz2d, onehot):
    def body(d_ref, e_ref, o_ref):
        o_ref[...] = lax.dot_general(d_ref[...], e_ref[...], (((1,), (1,)), ((), ())),
                                     precision=lax.Precision.HIGHEST, preferred_element_type=F32)
    return pl.pallas_call(body, name="rpb_reduce", out_shape=jax.ShapeDtypeStruct((128, 128), F32),
                          compiler_params=pltpu.CompilerParams(vmem_limit_bytes=VMEM_LIMIT))(dtz2d, onehot)


NA_PAIRS = NA_HEADS // 2


def _rpb_fold(dbt):
    _, _, kw_n, _ = dbt.shape
    wr = kw_n // GRID_W

    def body(d_ref, o_ref):
        for dr in range(N_DR):
            acc = jnp.zeros((GRID_W, 128), F32)
            for s in range(NA_ROWS_MAX):
                j = dr - s
                if 0 <= j < wr:
                    acc = acc + d_ref[0, s, j * GRID_W:(j + 1) * GRID_W, :]
            o_ref[0, dr] = acc

    return pl.pallas_call(
        body, name="rpb_fold", grid=(NA_PAIRS,),
        in_specs=[pl.BlockSpec((1, NA_ROWS_MAX, kw_n, 128), lambda p: (p, 0, 0, 0))],
        out_specs=pl.BlockSpec((1, N_DR, GRID_W, 128), lambda p: (p, 0, 0, 0)),
        out_shape=jax.ShapeDtypeStruct((NA_PAIRS, N_DR, GRID_W, 128), F32),
        compiler_params=_params("parallel"),
    )(dbt)


def _na_bias_table(na_rpb, onehot, wr):
    rpb2d = jnp.pad(na_rpb.reshape(NA_HEADS * N_DR, N_DC), ((0, 128 - NA_HEADS * N_DR), (0, 128 - N_DC)))
    tz = _rpb_expand(rpb2d, onehot)[:NA_HEADS * N_DR].reshape(NA_HEADS, N_DR, GRID_W, GRID_W)
    col = np.arange(GRID_W)
    cs = np.clip(col - NA_COLS // 2, 0, GRID_W - NA_COLS)
    ok_kq = ((col[None, :] >= cs[:, None]) & (col[None, :] < cs[:, None] + NA_COLS)).T
    shifts = []
    for s in range(NA_ROWS_MAX):
        dr = np.clip(np.arange(wr) + s, 0, N_DR - 1)
        t = jnp.stack([tz[:, int(d)] for d in dr], axis=1)
        t = jnp.where(jnp.asarray(ok_kq)[None, None], t, NEG)
        t = t.reshape(NA_PAIRS, 2, wr * GRID_W, GRID_W)
        shifts.append(jnp.transpose(t, (0, 2, 1, 3)).reshape(NA_PAIRS, wr * GRID_W, 128))
    return jnp.stack(shifts, axis=1)


def _na_bias_grad(dbt, onehot):
    fold = _rpb_fold(dbt).reshape(NA_PAIRS, N_DR, GRID_W, 2, GRID_W)
    dtz2d = jnp.transpose(fold, (0, 3, 1, 2, 4)).reshape(NA_HEADS * N_DR, GRID_W * GRID_W)
    dtz2d = jnp.pad(dtz2d, ((0, 128 - NA_HEADS * N_DR), (0, 0)))
    return _rpb_reduce(dtz2d, onehot)[:NA_HEADS * N_DR, :N_DC].reshape(-1)


def _rope_tables(S):
    half = HEAD_DIM // 2
    inv = ROPE_THETA ** (-jnp.arange(half, dtype=F32) / half)
    ang = jnp.arange(S).astype(F32)[:, None] * inv[None, :]
    cos, sin = jnp.cos(ang), jnp.sin(ang)
    n = ROPE_WIDTH // HEAD_DIM
    return jnp.tile(jnp.concatenate([cos, cos], axis=1), (1, n)), jnp.tile(jnp.concatenate([-sin, sin], axis=1), (1, n))


def _rot_half(t):
    w = t.shape[1]
    lane = lax.broadcasted_iota(jnp.int32, t.shape, 1)
    return jnp.where((lane % HEAD_DIM) < HEAD_DIM // 2, pltpu.roll(t, w - HEAD_DIM // 2, axis=1),
                     pltpu.roll(t, HEAD_DIM // 2, axis=1))


def _tok_spec(w):
    return pl.BlockSpec((TOKEN_TILE, w), lambda i: (i, 0))


def _mod_spec(tps, d):
    return pl.BlockSpec((1, 6, d), lambda i: (i // tps, 0, 0))


def _bstat_spec(tps, w):
    return pl.BlockSpec((1, 8, w), lambda i: (i // tps, 0, 0))


def _attn_in(x2d, mod3, g_attn, w_in, cos_t, sin_t, S):
    T, D = x2d.shape
    tps = S // TOKEN_TILE

    def body(x_ref, mod_ref, g_ref, w_ref, cos_ref, sin_ref, h_ref, qkv_ref):
        xn, _ = _rms(x_ref[...])
        h = (xn * g_ref[...]) * (1.0 + mod_ref[0, 1:2, :]) + mod_ref[0, 0:1, :]
        hb = h.astype(BF16)
        h_ref[...] = hb
        proj = _nn(hb, w_ref[...])
        rb = proj[:, ROPE_LO:ROPE_LO + ROPE_WIDTH]
        rb = rb * cos_ref[...] + _rot_half(rb) * sin_ref[...]
        qkv_ref[:, 0:NA_WIDTH] = (proj[:, 0:NA_WIDTH] * Q_SCALE).astype(BF16)
        qkv_ref[:, NA_WIDTH:ROPE_LO] = proj[:, NA_WIDTH:ROPE_LO].astype(BF16)
        qkv_ref[:, ROPE_LO:ROPE_LO + SW_WIDTH] = (rb[:, 0:SW_WIDTH] * Q_SCALE).astype(BF16)
        qkv_ref[:, ROPE_LO + SW_WIDTH:ROPE_LO + ROPE_WIDTH] = rb[:, SW_WIDTH:].astype(BF16)
        qkv_ref[:, ROPE_LO + ROPE_WIDTH:] = proj[:, ROPE_LO + ROPE_WIDTH:].astype(BF16)

    return pl.pallas_call(
        body, name="attn_in", grid=(T // TOKEN_TILE,),
        in_specs=[_tok_spec(D), _mod_spec(tps, D), _full((1, D)), _full(w_in.shape),
                  pl.BlockSpec((TOKEN_TILE, ROPE_WIDTH), lambda i: (i % tps, 0)),
                  pl.BlockSpec((TOKEN_TILE, ROPE_WIDTH), lambda i: (i % tps, 0))],
        out_specs=[_tok_spec(D), _tok_spec(IN_WIDTH)],
        out_shape=[jax.ShapeDtypeStruct((T, D), BF16), jax.ShapeDtypeStruct((T, IN_WIDTH), BF16)],
        compiler_params=_params("parallel"),
    )(x2d, mod3, g_attn, w_in, cos_t, sin_t)


def _attn_out(oa, ob, x2d, mod3, g_na, g_sw, w_out, S):
    T, D = x2d.shape
    tps = S // TOKEN_TILE

    def body(oa_ref, ob_ref, x_ref, mod_ref, gna_ref, gsw_ref, w_ref, mixin_ref, mix_ref, x1_ref):
        oan, _ = _rms(oa_ref[...])
        obn, _ = _rms(ob_ref[...])
        mixin = jnp.concatenate([oan * gna_ref[...], obn * gsw_ref[...]], axis=1).astype(BF16)
        mixin_ref[...] = mixin
        mix = _nn(mixin, w_ref[...])
        mix_ref[...] = mix
        x1_ref[...] = x_ref[...] + mod_ref[0, 2:3, :] * mix

    return pl.pallas_call(
        body, name="attn_out", grid=(T // TOKEN_TILE,),
        in_specs=[_tok_spec(NA_WIDTH), _tok_spec(SW_WIDTH), _tok_spec(D), _mod_spec(tps, D),
                  _full((1, NA_WIDTH)), _full((1, SW_WIDTH)), _full(w_out.shape)],
        out_specs=[_tok_spec(NA_WIDTH + SW_WIDTH), _tok_spec(D), _tok_spec(D)],
        out_shape=[jax.ShapeDtypeStruct((T, NA_WIDTH + SW_WIDTH), BF16), jax.ShapeDtypeStruct((T, D), F32),
                   jax.ShapeDtypeStruct((T, D), F32)],
        compiler_params=_params("parallel"),
    )(oa, ob, x2d, mod3, g_na, g_sw, w_out)


def _ffn_up(x1, mod3, g_ffn, w_up, S):
    T, D = x1.shape
    F = w_up.shape[1] // 2
    tps = S // TOKEN_TILE

    def body(x1_ref, mod_ref, g_ref, w_ref, h2_ref, val_ref, gt_ref):
        xn, _ = _rms(x1_ref[...])
        h2 = ((xn * g_ref[...]) * (1.0 + mod_ref[0, 4:5, :]) + mod_ref[0, 3:4, :]).astype(BF16)
        h2_ref[...] = h2
        u = _nn(h2, w_ref[...])
        val_ref[...] = u[:, :F].astype(BF16)
        gt_ref[...] = u[:, F:].astype(BF16)

    return pl.pallas_call(
        body, name="ffn_up", grid=(T // TOKEN_TILE,),
        in_specs=[_tok_spec(D), _mod_spec(tps, D), _full((1, D)), _full(w_up.shape)],
        out_specs=[_tok_spec(D), _tok_spec(F), _tok_spec(F)],
        out_shape=[jax.ShapeDtypeStruct((T, D), BF16), jax.ShapeDtypeStruct((T, F), BF16), jax.ShapeDtypeStruct((T, F), BF16)],
        compiler_params=_params("parallel"),
    )(x1, mod3, g_ffn, w_up)


def _halo_specs(T, tps, w):
    per = TOKEN_TILE // 8
    prev = pl.BlockSpec((8, w), lambda i: (jnp.maximum(i * per - 1, 0), 0))
    nxt = pl.BlockSpec((8, w), lambda i: (jnp.minimum((i + 1) * per, T // 8 - 1), 0))
    return prev, nxt


def _seq_shifts(cur, prev_ref, next_ref, tps):
    ti = pl.program_id(0) % tps
    row = lax.broadcasted_iota(jnp.int32, cur.shape, 0)
    before = jnp.where(ti > 0, prev_ref[7:8, :].astype(F32), 0.0)
    after = jnp.where(ti < tps - 1, next_ref[0:1, :].astype(F32), 0.0)
    down = jnp.where(row == 0, before, pltpu.roll(cur, 1, axis=0))
    up = jnp.where(row == cur.shape[0] - 1, after, pltpu.roll(cur, cur.shape[0] - 1, axis=0))
    return down, up


def _conv_gate(gt_ref, prev_ref, next_ref, cw_ref, cb_ref, tps):
    g = gt_ref[...].astype(F32)
    gprev, gnext = _seq_shifts(g, prev_ref, next_ref, tps)
    gc = gprev * cw_ref[0:1, :] + g * cw_ref[1:2, :] + gnext * cw_ref[2:3, :] + cb_ref[...]
    sig = 1.0 / (1.0 + jnp.exp(-gc))
    return g, gprev, gnext, gc, sig


def _ffn_down(gt, val, conv_w, conv_b, w_down, x1, mod3, g_final, target, B, S):
    T, D = x1.shape
    F = gt.shape[1]
    tps = S // TOKEN_TILE
    prev, nxt = _halo_specs(T, tps, F)

    def body(gt_ref, prev_ref, next_ref, val_ref, cw_ref, cb_ref, w_ref, x1_ref, mod_ref, gf_ref, tgt_ref,
             a_ref, dx2_ref, df_ref, gstat_ref, bstat_ref):
        i = pl.program_id(0)
        _, _, _, gc, sig = _conv_gate(gt_ref, prev_ref, next_ref, cw_ref, cb_ref, tps)
        a = (gc * sig * val_ref[...].astype(F32)).astype(BF16)
        a_ref[...] = a
        f = _nn(a, w_ref[...])
        gate = mod_ref[0, 5:6, :]
        x2 = x1_ref[...] + gate * f
        xn, r = _rms(x2)
        err = xn * gf_ref[...] - tgt_ref[...]
        dy = err * (1.0 / D)
        dx2 = _rms_bwd(xn, r, dy * gf_ref[...])
        dx2_ref[...] = dx2
        df_ref[...] = (gate * dx2).astype(BF16)

        @pl.when(i == 0)
        def _():
            gstat_ref[...] = jnp.zeros_like(gstat_ref)

        @pl.when(i % tps == 0)
        def _():
            bstat_ref[...] = jnp.zeros_like(bstat_ref)

        gstat_ref[0:1, :] += jnp.sum(dy * xn, axis=0, keepdims=True)
        tile_loss = jnp.sum(jnp.sum(err * err, axis=1, keepdims=True), axis=0, keepdims=True) * (0.5 / D)
        gstat_ref[1:2, :] += jnp.broadcast_to(tile_loss, (1, D))
        bstat_ref[0, 0:1, :] += jnp.sum(dx2 * f, axis=0, keepdims=True)

    return pl.pallas_call(
        body, name="ffn_down", grid=(T // TOKEN_TILE,),
        in_specs=[_tok_spec(F), prev, nxt, _tok_spec(F), _full(conv_w.shape), _full((1, F)), _full(w_down.shape),
                  _tok_spec(D), _mod_spec(tps, D), _full((1, D)), _tok_spec(D)],
        out_specs=[_tok_spec(F), _tok_spec(D), _tok_spec(D), _full((8, D)), _bstat_spec(tps, D)],
        out_shape=[jax.ShapeDtypeStruct((T, F), BF16), jax.ShapeDtypeStruct((T, D), F32), jax.ShapeDtypeStruct((T, D), BF16),
                   jax.ShapeDtypeStruct((8, D), F32), jax.ShapeDtypeStruct((B, 8, D), F32)],
        compiler_params=_params("arbitrary"),
    )(gt, gt, gt, val, conv_w, conv_b, w_down, x1, mod3, g_final, target)


def _ffn_down_bwd(df, w_down, gt, val, conv_w, conv_b, S):
    T, D = df.shape
    F = gt.shape[1]
    tps = S // TOKEN_TILE
    prev, nxt = _halo_specs(T, tps, F)

    def body(df_ref, w_ref, gt_ref, prev_ref, next_ref, val_ref, cw_ref, cb_ref, dval_ref, dgc_ref, cstat_ref):
        g, gprev, gnext, gc, sig = _conv_gate(gt_ref, prev_ref, next_ref, cw_ref, cb_ref, tps)
        da = _nt(df_ref[...], w_ref[...])
        dval_ref[...] = (da * (gc * sig)).astype(BF16)
        dgc = da * val_ref[...].astype(F32) * (sig * (1.0 + gc * (1.0 - sig)))
        dgc_ref[...] = dgc.astype(BF16)

        @pl.when(pl.program_id(0) == 0)
        def _():
            cstat_ref[...] = jnp.zeros_like(cstat_ref)

        cstat_ref[0:1, :] += jnp.sum(dgc, axis=0, keepdims=True)
        cstat_ref[1:2, :] += jnp.sum(dgc * gprev, axis=0, keepdims=True)
        cstat_ref[2:3, :] += jnp.sum(dgc * g, axis=0, keepdims=True)
        cstat_ref[3:4, :] += jnp.sum(dgc * gnext, axis=0, keepdims=True)

    return pl.pallas_call(
        body, name="ffn_down_bwd", grid=(T // TOKEN_TILE,),
        in_specs=[_tok_spec(D), _full(w_down.shape), _tok_spec(F), prev, nxt, _tok_spec(F), _full(conv_w.shape), _full((1, F))],
        out_specs=[_tok_spec(F), _tok_spec(F), _full((8, F))],
        out_shape=[jax.ShapeDtypeStruct((T, F), BF16), jax.ShapeDtypeStruct((T, F), BF16), jax.ShapeDtypeStruct((8, F), F32)],
        compiler_params=_params("arbitrary"),
    )(df, w_down, gt, gt, gt, val, conv_w, conv_b)


def _ffn_up_bwd(dgc, dval, conv_w, w_up, x1, mod3, g_ffn, dx2, mix, B, S):
    T, D = x1.shape
    F = dgc.shape[1]
    tps = S // TOKEN_TILE
    prev, nxt = _halo_specs(T, tps, F)

    def body(dgc_ref, prev_ref, next_ref, dval_ref, cw_ref, w_ref, x1_ref, mod_ref, g_ref, dx2_ref, mix_ref,
             du_ref, dx1_ref, dmix_ref, gstat_ref, bstat_ref):
        i = pl.program_id(0)
        d = dgc_ref[...].astype(F32)
        dprev, dnext = _seq_shifts(d, prev_ref, next_ref, tps)
        dgt = dnext * cw_ref[0:1, :] + d * cw_ref[1:2, :] + dprev * cw_ref[2:3, :]
        du = jnp.concatenate([dval_ref[...], dgt.astype(BF16)], axis=1)
        du_ref[...] = du
        dh2 = _nt(du, w_ref[...])
        xn, r = _rms(x1_ref[...])
        scale1 = 1.0 + mod_ref[0, 4:5, :]
        xg = xn * g_ref[...]
        dx1 = dx2_ref[...] + _rms_bwd(xn, r, dh2 * g_ref[...] * scale1)
        dx1_ref[...] = dx1
        dmix_ref[...] = (mod_ref[0, 2:3, :] * dx1).astype(BF16)

        @pl.when(i == 0)
        def _():
            gstat_ref[...] = jnp.zeros_like(gstat_ref)

        @pl.when(i % tps == 0)
        def _():
            bstat_ref[...] = jnp.zeros_like(bstat_ref)

        gstat_ref[0:1, :] += jnp.sum(dh2 * scale1 * xn, axis=0, keepdims=True)
        bstat_ref[0, 0:1, :] += jnp.sum(dh2, axis=0, keepdims=True)
        bstat_ref[0, 1:2, :] += jnp.sum(dh2 * xg, axis=0, keepdims=True)
        bstat_ref[0, 2:3, :] += jnp.sum(dx1 * mix_ref[...], axis=0, keepdims=True)

    return pl.pallas_call(
        body, name="ffn_up_bwd", grid=(T // TOKEN_TILE,),
        in_specs=[_tok_spec(F), prev, nxt, _tok_spec(F), _full(conv_w.shape), _full(w_up.shape), _tok_spec(D),
                  _mod_spec(tps, D), _full((1, D)), _tok_spec(D), _tok_spec(D)],
        out_specs=[_tok_spec(2 * F), _tok_spec(D), _tok_spec(D), _full((8, D)), _bstat_spec(tps, D)],
        out_shape=[jax.ShapeDtypeStruct((T, 2 * F), BF16), jax.ShapeDtypeStruct((T, D), F32), jax.ShapeDtypeStruct((T, D), BF16),
                   jax.ShapeDtypeStruct((8, D), F32), jax.ShapeDtypeStruct((B, 8, D), F32)],
        compiler_params=_params("arbitrary"),
    )(dgc, dgc, dgc, dval, conv_w, w_up, x1, mod3, g_ffn, dx2, mix)


def _attn_out_bwd(dmix, w_out, oa, ob, g_na, g_sw):
    T, D = dmix.shape

    def body(dmix_ref, w_ref, oa_ref, ob_ref, gna_ref, gsw_ref, doa_ref, dob_ref, gstat_ref):
        dmixin = _nt(dmix_ref[...], w_ref[...])

        @pl.when(pl.program_id(0) == 0)
        def _():
            gstat_ref[...] = jnp.zeros_like(gstat_ref)

        for k, (o_ref, g_ref, do_ref) in enumerate(((oa_ref, gna_ref, doa_ref), (ob_ref, gsw_ref, dob_ref))):
            dn = dmixin[:, k * NA_WIDTH:(k + 1) * NA_WIDTH]
            on, r = _rms(o_ref[...])
            gstat_ref[k:k + 1, :] += jnp.sum(dn * on, axis=0, keepdims=True)
            do_ref[...] = _rms_bwd(on, r, dn * g_ref[...]).astype(BF16)

    hs = jax.ShapeDtypeStruct((T, NA_WIDTH), BF16)
    return pl.pallas_call(
        body, name="attn_out_bwd", grid=(T // TOKEN_TILE,),
        in_specs=[_tok_spec(D), _full(w_out.shape), _tok_spec(NA_WIDTH), _tok_spec(SW_WIDTH), _full((1, NA_WIDTH)), _full((1, SW_WIDTH))],
        out_specs=[_tok_spec(NA_WIDTH), _tok_spec(SW_WIDTH), _full((8, NA_WIDTH))],
        out_shape=[hs, hs, jax.ShapeDtypeStruct((8, NA_WIDTH), F32)],
        compiler_params=_params("arbitrary"),
    )(dmix, w_out, oa, ob, g_na, g_sw)


def _attn_in_bwd(dqa, dka, dva, dqb, dkb, dvb, cos_t, sin_t, w_in, x2d, mod3, g_attn, dx1, B, S):
    T, D = x2d.shape
    tps = S // TOKEN_TILE

    def body(dqa_ref, dka_ref, dva_ref, dqb_ref, dkb_ref, dvb_ref, cos_ref, sin_ref, w_ref, x_ref, mod_ref, g_ref, dx1_ref,
             gx_ref, dproj_ref, gstat_ref, bstat_ref):
        i = pl.program_id(0)
        drb = jnp.concatenate([dqb_ref[...] * Q_SCALE, dkb_ref[...]], axis=1)
        drb = drb * cos_ref[...] + _rot_half(drb * sin_ref[...])
        dproj = jnp.concatenate([dqa_ref[...] * Q_SCALE, dka_ref[...], dva_ref[...], drb, dvb_ref[...]], axis=1).astype(BF16)
        dproj_ref[...] = dproj
        dh = _nt(dproj, w_ref[...])
        xn, r = _rms(x_ref[...])
        scale1 = 1.0 + mod_ref[0, 1:2, :]
        gx_ref[...] = dx1_ref[...] + _rms_bwd(xn, r, dh * g_ref[...] * scale1)

        @pl.when(i == 0)
        def _():
            gstat_ref[...] = jnp.zeros_like(gstat_ref)

        @pl.when(i % tps == 0)
        def _():
            bstat_ref[...] = jnp.zeros_like(bstat_ref)

        gstat_ref[0:1, :] += jnp.sum(dh * scale1 * xn, axis=0, keepdims=True)
        bstat_ref[0, 0:1, :] += jnp.sum(dh, axis=0, keepdims=True)
        bstat_ref[0, 1:2, :] += jnp.sum(dh * (xn * g_ref[...]), axis=0, keepdims=True)

    rope = pl.BlockSpec((TOKEN_TILE, ROPE_WIDTH), lambda i: (i % tps, 0))
    return pl.pallas_call(
        body, name="attn_in_bwd", grid=(T // TOKEN_TILE,),
        in_specs=[_tok_spec(NA_WIDTH), _tok_spec(NA_WIDTH), _tok_spec(NA_WIDTH), _tok_spec(SW_WIDTH), _tok_spec(SW_KV_WIDTH),
                  _tok_spec(SW_KV_WIDTH), rope, rope, _full(w_in.shape), _tok_spec(D), _mod_spec(tps, D), _full((1, D)), _tok_spec(D)],
        out_specs=[_tok_spec(D), _tok_spec(IN_WIDTH), _full((8, D)), _bstat_spec(tps, D)],
        out_shape=[jax.ShapeDtypeStruct((T, D), F32), jax.ShapeDtypeStruct((T, IN_WIDTH), BF16),
                   jax.ShapeDtypeStruct((8, D), F32), jax.ShapeDtypeStruct((B, 8, D), F32)],
        compiler_params=_params("arbitrary"),
    )(dqa, dka, dva, dqb, dkb, dvb, cos_t, sin_t, w_in, x2d, mod3, g_attn, dx1)


def _matmul_tn(a, b, name, tn=None, tk=512):
    T, M = a.shape
    N = b.shape[1]
    tn = N if tn is None else tn

    def body(a_ref, b_ref, o_ref):
        @pl.when(pl.program_id(1) == 0)
        def _():
            o_ref[...] = jnp.zeros_like(o_ref)

        o_ref[...] += _tn(a_ref[...], b_ref[...])

    return pl.pallas_call(
        body, name=name, grid=(N // tn, T // tk),
        in_specs=[pl.BlockSpec((tk, M), lambda j, k: (k, 0)), pl.BlockSpec((tk, tn), lambda j, k: (k, j))],
        out_specs=pl.BlockSpec((M, tn), lambda j, k: (0, j)),
        out_shape=jax.ShapeDtypeStruct((M, N), F32),
        compiler_params=_params("parallel", "arbitrary"),
    )(a, b)


def _na_geometry(S):
    rows = S // GRID_W
    wr = min(NA_ROWS_MAX, rows)
    return rows, wr


def _na_window(r, rows, wr):
    rs = jnp.clip(r - wr // 2, 0, rows - wr)
    return pl.multiple_of(rs * GRID_W, GRID_W), rs - r + NA_ROWS_MAX - 1


NA_STEP_PAIRS = 2
NA_GW = NA_STEP_PAIRS * 128


def _na_specs(S, kw_n, order):
    ng = NA_PAIRS // NA_STEP_PAIRS

    def col(k):
        return pl.BlockSpec((1, S, NA_GW), lambda *ids: (order(*ids)[0], 0, k * ng + order(*ids)[1]))
    bias = pl.BlockSpec((NA_STEP_PAIRS, NA_ROWS_MAX, kw_n, 128), lambda *ids: (order(*ids)[1], 0, 0, 0))
    out = pl.BlockSpec((1, S, NA_GW), lambda *ids: (order(*ids)[0], 0, order(*ids)[1]))
    return col(0), col(1), col(2), bias, out


def _block_diag(t):
    left = lax.broadcasted_iota(jnp.int32, t.shape, 1) < HEAD_DIM
    zero = jnp.zeros_like(t)
    return jnp.concatenate([jnp.where(left, t, zero), jnp.where(left, zero, t)], axis=0)


def _diag_blocks(res):
    left = lax.broadcasted_iota(jnp.int32, (HEAD_DIM, 128), 1) < HEAD_DIM
    return jnp.where(left, res[:HEAD_DIM], res[HEAD_DIM:])


def _col_softmax(st):
    e = jnp.exp(st - jnp.max(st, axis=0, keepdims=True))
    return e * (1.0 / jnp.sum(e, axis=0, keepdims=True))


def _na_fwd(qkv, bias):
    B, S, _ = qkv.shape
    rows, wr = _na_geometry(S)
    kw_n = wr * GRID_W

    def body(q_ref, k_ref, v_ref, b_ref, o_ref):
        def step(r, carry):
            start, sidx = _na_window(r, rows, wr)
            qrows = pl.ds(pl.multiple_of(r * GRID_W, GRID_W), GRID_W)
            krows = pl.ds(start, kw_n)
            lanes = [pl.ds(p * 128, 128) for p in range(NA_STEP_PAIRS)]
            st = [_nt(k_ref[0, krows, ln], _block_diag(q_ref[0, qrows, ln])) for ln in lanes]
            pn = [_col_softmax(st[p] + b_ref[p, sidx]).astype(BF16) for p in range(NA_STEP_PAIRS)]
            outs = [_diag_blocks(_tn(pn[p], v_ref[0, krows, lanes[p]])) for p in range(NA_STEP_PAIRS)]
            o_ref[0, qrows, :] = jnp.concatenate(outs, axis=1)
            return carry

        lax.fori_loop(0, rows, step, 0)

    q, k, v, bs, out = _na_specs(S, kw_n, lambda b, g: (b, g))
    return pl.pallas_call(
        body, name="na_fwd", grid=(B, NA_PAIRS // NA_STEP_PAIRS),
        in_specs=[q, k, v, bs], out_specs=out, out_shape=jax.ShapeDtypeStruct((B, S, NA_WIDTH), F32),
        compiler_params=_params("parallel", "parallel"),
    )(qkv, qkv, qkv, bias)


def _na_bwd(qkv, bias, doa):
    B, S, _ = qkv.shape
    rows, wr = _na_geometry(S)
    kw_n = wr * GRID_W

    def body(q_ref, k_ref, v_ref, b_ref, do_ref, dq_ref, dk_ref, dv_ref, db_ref):
        @pl.when(pl.program_id(1) == 0)
        def _():
            db_ref[...] = jnp.zeros_like(db_ref)

        dk_ref[...] = jnp.zeros_like(dk_ref)
        dv_ref[...] = jnp.zeros_like(dv_ref)

        def step(r, carry):
            start, sidx = _na_window(r, rows, wr)
            qrows = pl.ds(pl.multiple_of(r * GRID_W, GRID_W), GRID_W)
            krows = pl.ds(start, kw_n)
            pairs = range(NA_STEP_PAIRS)
            lanes = [pl.ds(p * 128, 128) for p in pairs]
            kp = [k_ref[0, krows, ln] for ln in lanes]
            qbd = [_block_diag(q_ref[0, qrows, ln]) for ln in lanes]
            dobd = [_block_diag(do_ref[0, qrows, ln]) for ln in lanes]
            st = [_nt(kp[p], qbd[p]) for p in pairs]
            dpt = [_nt(v_ref[0, krows, lanes[p]], dobd[p]) for p in pairs]
            pn = [_col_softmax(st[p] + b_ref[p, sidx]) for p in pairs]
            dst = [pn[p] * (dpt[p] - jnp.sum(pn[p] * dpt[p], axis=0, keepdims=True)) for p in pairs]
            dsb = [d.astype(BF16) for d in dst]
            dq_ref[0, qrows, :] = jnp.concatenate([_diag_blocks(_tn(dsb[p], kp[p])) for p in pairs], axis=1)
            dk_ref[0, krows, :] += jnp.concatenate([_nn(dsb[p], qbd[p]) for p in pairs], axis=1)
            dv_ref[0, krows, :] += jnp.concatenate([_nn(pn[p].astype(BF16), dobd[p]) for p in pairs], axis=1)
            for p in pairs:
                db_ref[p, sidx] += dst[p]
            return carry

        lax.fori_loop(0, rows, step, 0)

    q, k, v, bs, out = _na_specs(S, kw_n, lambda g, b: (b, g))
    hs = jax.ShapeDtypeStruct((B, S, NA_WIDTH), F32)
    return pl.pallas_call(
        body, name="na_bwd", grid=(NA_PAIRS // NA_STEP_PAIRS, B),
        in_specs=[q, k, v, bs, out],
        out_specs=[out, out, out, bs],
        out_shape=[hs, hs, hs, jax.ShapeDtypeStruct((NA_PAIRS, NA_ROWS_MAX, kw_n, 128), F32)],
        compiler_params=_params("parallel", "arbitrary"),
    )(qkv, qkv, qkv, bias, doa)


SW_PAIRS = SW_HEADS // 2


def _sw_band(n, S):
    kw_n = 3 * SW_BLOCK
    start = pl.multiple_of(jnp.clip(n * SW_BLOCK - SW_BLOCK, 0, S - kw_n), SW_BLOCK)
    kpos = start + lax.broadcasted_iota(jnp.int32, (kw_n, SW_BLOCK), 0)
    qpos = n * SW_BLOCK + lax.broadcasted_iota(jnp.int32, (kw_n, SW_BLOCK), 1)
    return start, jnp.abs(qpos - kpos) <= SW_WINDOW


def _kv_halves(t):
    left = lax.broadcasted_iota(jnp.int32, t.shape, 1) < HEAD_DIM
    swapped = pltpu.roll(t, HEAD_DIM, axis=1)
    zero = jnp.zeros_like(t)
    return {(0, 0): jnp.where(left, t, zero), (0, 1): jnp.where(left, zero, swapped),
            (1, 0): jnp.where(left, swapped, zero), (1, 1): jnp.where(left, zero, t)}


def _sw_probs(st, ok, sk):
    st = jnp.where(ok, st, NEG)
    m = jnp.maximum(jnp.max(st, axis=0, keepdims=True), sk)
    e = jnp.exp(st - m)
    esk = jnp.exp(sk - m)
    inv = 1.0 / (jnp.sum(e, axis=0, keepdims=True) + esk)
    return e * inv, esk * inv


def _sw_specs(S):
    q = pl.BlockSpec((1, S, SW_WIDTH), lambda b: (b, 0, ROPE_LO // SW_WIDTH))
    k = pl.BlockSpec((1, S, SW_KV_WIDTH), lambda b: (b, 0, (ROPE_LO + SW_WIDTH) // SW_KV_WIDTH))
    v = pl.BlockSpec((1, S, SW_KV_WIDTH), lambda b: (b, 0, (ROPE_LO + ROPE_WIDTH) // SW_KV_WIDTH))
    return q, k, v


def _sw_fwd(sink, qkv):
    B, S, _ = qkv.shape
    kw_n = 3 * SW_BLOCK

    def body(sink_ref, q_ref, k_ref, v_ref, o_ref):
        def step(n, carry):
            start, ok = _sw_band(n, S)
            qrows = pl.ds(pl.multiple_of(n * SW_BLOCK, SW_BLOCK), SW_BLOCK)
            krows = pl.ds(start, kw_n)
            kh, vh = _kv_halves(k_ref[0, krows, :]), _kv_halves(v_ref[0, krows, :])
            heads = [(p, e) for p in range(SW_PAIRS) for e in range(2)]
            qp = [q_ref[0, qrows, pl.ds(p * 128, 128)] for p in range(SW_PAIRS)]
            kv_of = lambda p: p // (SW_PAIRS // SW_KV_HEADS)
            st = {(p, e): _nt(kh[(kv_of(p), e)], qp[p]) for p, e in heads}
            pn = {(p, e): _sw_probs(st[(p, e)], ok, sink_ref[2 * p + e])[0].astype(BF16) for p, e in heads}
            outs = [_tn(pn[(p, 0)], vh[(kv_of(p), 0)]) + _tn(pn[(p, 1)], vh[(kv_of(p), 1)]) for p in range(SW_PAIRS)]
            o_ref[0, qrows, :] = jnp.concatenate(outs, axis=1)
            return carry

        lax.fori_loop(0, S // SW_BLOCK, step, 0)

    q, k, v = _sw_specs(S)
    return pl.pallas_call(
        body, name="sw_fwd", grid=(B,),
        in_specs=[pl.BlockSpec(memory_space=pltpu.SMEM), q, k, v],
        out_specs=pl.BlockSpec((1, S, SW_WIDTH), lambda b: (b, 0, 0)), out_shape=jax.ShapeDtypeStruct((B, S, SW_WIDTH), F32),
        compiler_params=_params("parallel"),
    )(sink, qkv, qkv, qkv)


def _sw_bwd(sink, qkv, dob):
    B, S, _ = qkv.shape
    kw_n = 3 * SW_BLOCK

    fold_rows = 256

    def body(sink_ref, q_ref, k_ref, v_ref, do_ref, dq_ref, dk_ref, dv_ref, dsink_ref, dk_acc, dv_acc):
        @pl.when(pl.program_id(0) == 0)
        def _():
            dsink_ref[...] = jnp.zeros_like(dsink_ref)

        dk_acc[...] = jnp.zeros_like(dk_acc)
        dv_acc[...] = jnp.zeros_like(dv_acc)
        ppk = SW_PAIRS // SW_KV_HEADS

        def step(n, carry):
            start, ok = _sw_band(n, S)
            qrows = pl.ds(pl.multiple_of(n * SW_BLOCK, SW_BLOCK), SW_BLOCK)
            krows = pl.ds(start, kw_n)
            kh, vh = _kv_halves(k_ref[0, krows, :]), _kv_halves(v_ref[0, krows, :])
            heads = [(p, e) for p in range(SW_PAIRS) for e in range(2)]
            qp = [q_ref[0, qrows, pl.ds(p * 128, 128)] for p in range(SW_PAIRS)]
            dop = [do_ref[0, qrows, pl.ds(p * 128, 128)] for p in range(SW_PAIRS)]
            st = {(p, e): _nt(kh[(p // ppk, e)], qp[p]) for p, e in heads}
            dpt = {(p, e): _nt(vh[(p // ppk, e)], dop[p]) for p, e in heads}
            pnb, dsb = {}, {}
            for p, e in heads:
                pn, psink = _sw_probs(st[(p, e)], ok, sink_ref[2 * p + e])
                delta = jnp.sum(pn * dpt[(p, e)], axis=0, keepdims=True)
                dsb[(p, e)] = (pn * (dpt[(p, e)] - delta)).astype(BF16)
                pnb[(p, e)] = pn.astype(BF16)
                dsink_ref[2 * p + e:2 * p + e + 1, :] += -(psink * delta)
            dq_ref[0, qrows, :] = jnp.concatenate(
                [_tn(dsb[(p, 0)], kh[(p // ppk, 0)]) + _tn(dsb[(p, 1)], kh[(p // ppk, 1)]) for p in range(SW_PAIRS)], axis=1)
            left = lax.broadcasted_iota(jnp.int32, (kw_n, 128), 1) < HEAD_DIM
            dks, dvs = [], []
            for kv in range(SW_KV_HEADS):
                dk = dv = None
                for p in range(kv * ppk, (kv + 1) * ppk):
                    dk_p = jnp.where(left, _nn(dsb[(p, 0)], qp[p]), _nn(dsb[(p, 1)], qp[p]))
                    dv_p = jnp.where(left, _nn(pnb[(p, 0)], dop[p]), _nn(pnb[(p, 1)], dop[p]))
                    dk = dk_p if dk is None else dk + dk_p
                    dv = dv_p if dv is None else dv + dv_p
                dks.append(dk)
                dvs.append(dv)
            dk_acc[krows, :] += jnp.concatenate(dks, axis=1)
            dv_acc[krows, :] += jnp.concatenate(dvs, axis=1)
            return carry

        lax.fori_loop(0, S // SW_BLOCK, step, 0)

        def fold(i, carry):
            rows = pl.ds(pl.multiple_of(i * fold_rows, fold_rows), fold_rows)
            left = lax.broadcasted_iota(jnp.int32, (fold_rows, 128), 1) < HEAD_DIM
            for acc, out_ref in ((dk_acc, dk_ref), (dv_acc, dv_ref)):
                a, b = acc[rows, 0:128], acc[rows, 128:256]
                out_ref[0, rows, :] = jnp.where(left, a + pltpu.roll(a, HEAD_DIM, axis=1), b + pltpu.roll(b, HEAD_DIM, axis=1))
            return carry

        lax.fori_loop(0, S // fold_rows, fold, 0)

        @pl.when(pl.program_id(0) == B - 1)
        def _():
            dsink_ref[...] = jnp.broadcast_to(jnp.sum(dsink_ref[...], axis=1, keepdims=True), dsink_ref.shape)

    q, k, v = _sw_specs(S)
    qo = pl.BlockSpec((1, S, SW_WIDTH), lambda b: (b, 0, 0))
    ko = pl.BlockSpec((1, S, SW_KV_WIDTH), lambda b: (b, 0, 0))
    return pl.pallas_call(
        body, name="sw_bwd", grid=(B,),
        in_specs=[pl.BlockSpec(memory_space=pltpu.SMEM), q, k, v, qo],
        out_specs=[qo, ko, ko, _full((SW_HEADS, 128))],
        out_shape=[jax.ShapeDtypeStruct((B, S, SW_WIDTH), F32), jax.ShapeDtypeStruct((B, S, SW_KV_WIDTH), F32),
                   jax.ShapeDtypeStruct((B, S, SW_KV_WIDTH), F32), jax.ShapeDtypeStruct((SW_HEADS, 128), F32)],
        scratch_shapes=[pltpu.VMEM((S, 2 * SW_KV_WIDTH), F32), pltpu.VMEM((S, 2 * SW_KV_WIDTH), F32)],
        compiler_params=_params("arbitrary"),
    )(sink, qkv, qkv, qkv, dob)


def _pack_sum(packs):
    W = packs.shape[1]

    def body(p_ref, o_ref):
        tot = p_ref[0:8, :]
        for d in range(1, N_DEV):
            tot = tot + p_ref[8 * d:8 * d + 8, :]
        o_ref[...] = tot
        o_ref[5:6, :] = tot[0:1, :] + tot[1:2, :]

    return pl.pallas_call(body, name="pack_sum", out_shape=jax.ShapeDtypeStruct((8, W), F32),
                          compiler_params=pltpu.CompilerParams(vmem_limit_bytes=VMEM_LIMIT))(packs)


def _adamw(w, g, m, v, name):
    c1 = 1.0 - ADAM_B1 ** ADAM_STEP
    c2 = 1.0 - ADAM_B2 ** ADAM_STEP

    def body(w_ref, g_ref, m_ref, v_ref, d_ref, nm_ref, nv_ref):
        g_ = g_ref[...]
        nm = ADAM_B1 * m_ref[...] + (1.0 - ADAM_B1) * g_
        nv = ADAM_B2 * v_ref[...] + (1.0 - ADAM_B2) * (g_ * g_)
        nm_ref[...] = nm
        nv_ref[...] = nv
        d_ref[...] = -ADAM_LR * ((nm / c1) / (jnp.sqrt(nv / c2) + ADAM_EPS) + ADAM_WD * w_ref[...])

    s = jax.ShapeDtypeStruct(w.shape, F32)
    return pl.pallas_call(body, name=name, out_shape=[s, s, s],
                          compiler_params=pltpu.CompilerParams(vmem_limit_bytes=VMEM_LIMIT))(w, g, m, v)


def _gather_cols(shard_bf, name):
    g = _all_gather(shard_bf, name)
    return jnp.transpose(g, (1, 0, 2)).reshape(shard_bf.shape[0], N_DEV * shard_bf.shape[1])


def _gather_rows(shard_bf, name):
    g = _all_gather(shard_bf, name)
    return g.reshape(N_DEV * shard_bf.shape[0], shard_bf.shape[1])


def _scatter_cols(dw, name):
    R, C8 = dw.shape
    g8 = jnp.transpose(dw.reshape(R, N_DEV, C8 // N_DEV), (1, 0, 2)).astype(BF16)
    return _reduce_scatter(g8, name)


def _scatter_rows(dw, name):
    R8, C = dw.shape
    return _reduce_scatter(dw.reshape(N_DEV, R8 // N_DEV, C).astype(BF16), name)


def _local_step(x, mod, g_attn, w_in, bias, sw_sink, g_na_out, g_sw_out, w_out, g_ffn, w_up, conv_w, conv_b, w_down,
                g_final, target):
    B, S, D = x.shape
    T = B * S
    x2d = x.reshape(T, D)
    mod3 = mod.reshape(B, 6, D)
    cos_t, sin_t = _rope_tables(S)
    sink = sw_sink.reshape(SW_HEADS)

    h, qkv = _attn_in(x2d, mod3, g_attn, w_in, cos_t, sin_t, S)
    qkv3 = qkv.reshape(B, S, IN_WIDTH)
    oa = _na_fwd(qkv3, bias).reshape(T, NA_WIDTH)
    ob = _sw_fwd(sink, qkv3).reshape(T, SW_WIDTH)
    mixin, mix, x1 = _attn_out(oa, ob, x2d, mod3, g_na_out, g_sw_out, w_out, S)
    h2, val, gt = _ffn_up(x1, mod3, g_ffn, w_up, S)
    a, dx2, df, gstat_f, bstat_f = _ffn_down(gt, val, conv_w, conv_b, w_down, x1, mod3, g_final, target.reshape(T, D), B, S)

    dval, dgc, cstat = _ffn_down_bwd(df, w_down, gt, val, conv_w, conv_b, S)
    du, dx1, dmix, gstat_u, bstat_u = _ffn_up_bwd(dgc, dval, conv_w, w_up, x1, mod3, g_ffn, dx2, mix, B, S)
    doa, dob, gstat_o = _attn_out_bwd(dmix, w_out, oa, ob, g_na_out, g_sw_out)
    dqa, dka, dva, dbt = _na_bwd(qkv3, bias, doa.reshape(B, S, NA_WIDTH))
    dqb, dkb, dvb, dsink = _sw_bwd(sink, qkv3, dob.reshape(B, S, SW_WIDTH))
    r2 = lambda t: t.reshape(T, t.shape[-1])
    grad_x, dproj, gstat_i, bstat_i = _attn_in_bwd(r2(dqa), r2(dka), r2(dva), r2(dqb), r2(dkb), r2(dvb), cos_t, sin_t, w_in, x2d, mod3,
                                                   g_attn, dx1, B, S)

    F = val.shape[1]
    dw_in = _matmul_tn(h, dproj, "dw_in")
    dw_out = _matmul_tn(mixin, dmix, "dw_out")
    dw_up = _matmul_tn(h2, du, "dw_up", tn=F)
    dw_down = _matmul_tn(a, df, "dw_down")

    dmod = jnp.stack([bstat_i[:, 0], bstat_i[:, 1], bstat_u[:, 2], bstat_u[:, 0], bstat_u[:, 1], bstat_f[:, 0]], axis=1)
    small = dict(g_attn=gstat_i[0], g_ffn=gstat_u[0], g_final=gstat_f[0], loss=gstat_f[1, 0], g_na_out=gstat_o[0], g_sw_out=gstat_o[1],
                 sw_sink=dsink[:, 0], conv_b=cstat[0], conv_w=cstat[1:4], dbt=dbt)
    return grad_x.reshape(B, S, D), dict(w_in=dw_in, w_out=dw_out, w_up=dw_up, w_down=dw_down), dmod, small


def _pad_lanes(v, w):
    return jnp.pad(v, (0, w - v.shape[0]))


def kernel(x, c, w_ada, b_ada, g_attn, w_in, na_rpb, sw_sink, g_na_out, g_sw_out, w_out, g_ffn, w_up, conv_w, conv_b, w_down, g_final, loss_target, m_w_ada, m_b_ada, m_g_attn, m_w_in, m_na_rpb, m_sw_sink, m_g_na_out, m_g_sw_out, m_w_out, m_g_ffn, m_w_up, m_conv_w, m_conv_b, m_w_down, m_g_final, v_w_ada, v_b_ada, v_g_attn, v_w_in, v_na_rpb, v_sw_sink, v_g_na_out, v_g_sw_out, v_w_out, v_g_ffn, v_w_up, v_conv_w, v_conv_b, v_w_down, v_g_final):
    B, S, D = x.shape
    me = 4 * lax.axis_index("x") + 2 * lax.axis_index("y") + lax.axis_index("c")
    ada_c = w_ada.shape[2]
    F_l = conv_w.shape[2]
    onehot = _toeplitz_onehot()

    cw_l = jnp.pad(conv_w[0], ((0, 8 - conv_w.shape[1]), (0, 0)))
    c_l = jnp.pad(c, ((0, 8 - B), (0, 0)))
    slabs = _all_gather(jnp.concatenate([c_l, cw_l], axis=1), "gather_c")
    c_all = slabs[:, :, :D].reshape(N_DEV * 8, D)
    conv_w_f = jnp.transpose(slabs[:, :3, D:], (1, 0, 2)).reshape(3, N_DEV * F_l)

    b_ada_l = lax.dynamic_slice(b_ada, (0, me * ada_c), (1, ada_c))
    mod_cols = _ada_fwd(c_all, w_ada[0], b_ada_l)
    mod_all = _all_gather(mod_cols, "gather_mod")
    mod_mine = lax.dynamic_slice(mod_all, (0, me * 8, 0), (N_DEV, B, ada_c))
    mod = jnp.transpose(mod_mine, (1, 0, 2)).reshape(B, N_DEV * ada_c)

    w_in_f = _gather_cols(w_in[0].astype(BF16), "gather_w_in")
    w_out_f = _gather_rows(w_out[0].astype(BF16), "gather_w_out")
    w_up_f = _gather_cols(w_up[0].astype(BF16), "gather_w_up")
    w_down_f = _gather_rows(w_down[0].astype(BF16), "gather_w_down")

    _, wr = _na_geometry(S)
    bias = _na_bias_table(na_rpb[0], onehot, wr)

    grad_x, dw, dmod, small = _local_step(x, mod, g_attn, w_in_f, bias, sw_sink, g_na_out, g_sw_out, w_out_f, g_ffn, w_up_f,
                                          conv_w_f, conv_b, w_down_f, g_final.reshape(1, D), loss_target)

    g_w_in = _scatter_cols(dw["w_in"], "scatter_w_in")
    g_w_out = _scatter_rows(dw["w_out"], "scatter_w_out")
    g_w_up = _scatter_cols(dw["w_up"], "scatter_w_up")
    g_w_down = _scatter_rows(dw["w_down"], "scatter_w_down")

    drpb = _na_bias_grad(small["dbt"], onehot)

    row2 = jnp.concatenate([small["g_attn"], small["g_ffn"], small["g_final"], small["g_na_out"], small["g_sw_out"],
                            _pad_lanes(small["sw_sink"], 128), _pad_lanes(small["loss"].reshape(1), 128)])
    rows = [dmod.reshape(B, 6 * D)[0], dmod.reshape(B, 6 * D)[1], _pad_lanes(row2, PACK_W), _pad_lanes(small["conv_b"], PACK_W),
            _pad_lanes(drpb, PACK_W)] + [jnp.zeros((PACK_W,), F32)] * 3
    packs = _all_gather(jnp.stack(rows), "gather_small")
    cw_rows = jnp.pad(small["conv_w"], ((0, 5), (0, 0)))
    cw_packs = _all_gather(cw_rows, "gather_dconv")
    tot = _pack_sum(packs.reshape(N_DEV * 8, PACK_W))
    cw_tot = _pack_sum(cw_packs.reshape(N_DEV * 8, cw_rows.shape[1]))

    o = 0
    rep = {}
    for nm, wd in (("g_attn", D), ("g_ffn", D), ("g_final", D), ("g_na_out", NA_WIDTH), ("g_sw_out", SW_WIDTH), ("sw_sink", 128), ("loss", 128)):
        rep[nm] = tot[2, o:o + wd]
        o += wd
    loss = rep["loss"][0]
    g_b_ada = tot[5:6, :]
    g_conv_b = tot[3:4, :conv_b.shape[1]]
    g_na_rpb = tot[4, :NA_HEADS * N_DR * N_DC].reshape(na_rpb.shape)
    g_conv_w = lax.dynamic_slice(cw_tot[0:3], (0, me * F_l), (3, F_l)).reshape(conv_w.shape)

    dmod_cols = lax.dynamic_slice(packs.reshape(N_DEV * 8, PACK_W), (0, me * ada_c), (N_DEV * 8, ada_c))
    g_w_ada = _ada_bwd(c_all, dmod_cols)[None]

    grads = dict(
        w_ada=g_w_ada, b_ada=g_b_ada, g_attn=rep["g_attn"][None], w_in=g_w_in[None], na_rpb=g_na_rpb,
        sw_sink=rep["sw_sink"][None, :SW_HEADS], g_na_out=rep["g_na_out"][None], g_sw_out=rep["g_sw_out"][None],
        w_out=g_w_out[None], g_ffn=rep["g_ffn"][None], w_up=g_w_up[None], conv_w=g_conv_w, conv_b=g_conv_b,
        w_down=g_w_down[None], g_final=rep["g_final"])
    weights = dict(w_ada=w_ada, b_ada=b_ada, g_attn=g_attn, w_in=w_in, na_rpb=na_rpb, sw_sink=sw_sink, g_na_out=g_na_out,
                   g_sw_out=g_sw_out, w_out=w_out, g_ffn=g_ffn, w_up=w_up, conv_w=conv_w, conv_b=conv_b, w_down=w_down, g_final=g_final)
    ms = dict(w_ada=m_w_ada, b_ada=m_b_ada, g_attn=m_g_attn, w_in=m_w_in, na_rpb=m_na_rpb, sw_sink=m_sw_sink, g_na_out=m_g_na_out,
              g_sw_out=m_g_sw_out, w_out=m_w_out, g_ffn=m_g_ffn, w_up=m_w_up, conv_w=m_conv_w, conv_b=m_conv_b, w_down=m_w_down, g_final=m_g_final)
    vs = dict(w_ada=v_w_ada, b_ada=v_b_ada, g_attn=v_g_attn, w_in=v_w_in, na_rpb=v_na_rpb, sw_sink=v_sw_sink, g_na_out=v_g_na_out,
              g_sw_out=v_g_sw_out, w_out=v_w_out, g_ffn=v_g_ffn, w_up=v_w_up, conv_w=v_conv_w, conv_b=v_conv_b, w_down=v_w_down, g_final=v_g_final)
    names = list(weights)
    deltas, new_m, new_v = {}, {}, {}
    for nm in names:
        shp = weights[nm].shape
        two_d = (shp[-2], shp[-1]) if len(shp) >= 3 and nm != "na_rpb" else (1, int(np.prod(shp)))
        r = lambda t: t.reshape(two_d)
        d_, m_, v_ = _adamw(r(weights[nm]), r(grads[nm]), r(ms[nm]), r(vs[nm]), "adamw_" + nm)
        deltas[nm], new_m[nm], new_v[nm] = d_.reshape(shp), m_.reshape(shp), v_.reshape(shp)
        grads[nm] = grads[nm].reshape(shp)
    return (loss, grad_x, *[grads[n] for n in names], *[deltas[n] for n in names], *[new_m[n] for n in names],
            *[new_v[n] for n in names])
```

```python
import functools

import numpy as np
import jax
import jax.numpy as jnp
from jax import lax
from jax.experimental import pallas as pl
from jax.experimental.pallas import tpu as pltpu

F32, BF16 = jnp.float32, jnp.bfloat16
MESH_ID = pl.DeviceIdType.MESH
N_DEV = 8

HEAD_DIM = 64
NA_HEADS = 8
SW_HEADS = 8
SW_KV_HEADS = 2
SW_GROUP = SW_HEADS // SW_KV_HEADS
NA_WIDTH = NA_HEADS * HEAD_DIM
SW_WIDTH = SW_HEADS * HEAD_DIM
SW_KV_WIDTH = SW_KV_HEADS * HEAD_DIM
ROPE_WIDTH = SW_WIDTH + SW_KV_WIDTH
IN_WIDTH = 3 * NA_WIDTH + SW_WIDTH + 2 * SW_KV_WIDTH
ROPE_LO = 3 * NA_WIDTH
GRID_W = 64
NA_ROWS_MAX = 8
NA_COLS = 16
N_DR = 2 * NA_ROWS_MAX - 1
N_DC = 2 * NA_COLS - 1
SW_WINDOW = 128
SW_BLOCK = 128
ROPE_THETA = 10000.0
EPS = 1e-6
NEG = -1e30
Q_SCALE = HEAD_DIM ** -0.5

ADAM_LR = 0.001
ADAM_B1 = 0.9
ADAM_B2 = 0.999
ADAM_EPS = 1e-08
ADAM_WD = 0.01
ADAM_STEP = 10

TOKEN_TILE = 256
VMEM_LIMIT = 56 * 1024 * 1024

PACK_W = 6144


def _nn(a, b):
    return jnp.dot(a, b, preferred_element_type=F32)


def _nt(a, b):
    return lax.dot_general(a, b, (((1,), (1,)), ((), ())), preferred_element_type=F32)


def _tn(a, b):
    return lax.dot_general(a, b, (((0,), (0,)), ((), ())), preferred_element_type=F32)


def _rms(x):
    r = lax.rsqrt(jnp.mean(x * x, axis=-1, keepdims=True) + EPS)
    return x * r, r


def _rms_bwd(xn, r, gy):
    return r * (gy - xn * jnp.mean(xn * gy, axis=-1, keepdims=True))


def _params(*sem):
    return pltpu.CompilerParams(dimension_semantics=sem, vmem_limit_bytes=VMEM_LIMIT)


def _full(shape):
    n = len(shape)
    return pl.BlockSpec(shape, lambda *_: (0,) * n)


def _mesh_pos():
    return lax.axis_index("x"), lax.axis_index("y"), lax.axis_index("c")


def _all_gather(x, name):
    def body(x_ref, out_ref, send_sems, recv_sems, local_sem):
        x_, y_, c_ = _mesh_pos()
        me, sibling = (x_, y_, c_), (x_, y_, 1 - c_)
        chips = [(1 - x_, y_), (x_, 1 - y_), (1 - x_, 1 - y_)]

        def rows(px, py, pc):
            return out_ref.at[4 * px + 2 * py + pc]

        def copy(k, block, to, src=None):
            return pltpu.make_async_remote_copy(
                src_ref=rows(*block) if src is None else src, dst_ref=rows(*block),
                send_sem=send_sems.at[k], recv_sem=recv_sems.at[k], device_id=to, device_id_type=MESH_ID)

        mine = pltpu.make_async_copy(x_ref, rows(*me), local_sem)
        mine.start()
        first = [copy(0, me, sibling, src=x_ref)]
        first += [copy(1 + j, me, (*chip, c_), src=x_ref) for j, chip in enumerate(chips)]
        for cp in first:
            cp.start()
        passed = [copy(4 + j, (*chip, c_), sibling) for j, chip in enumerate(chips)]
        for j, chip in enumerate(chips):
            copy(1 + j, (*chip, c_), me).wait_recv()
            passed[j].start()
        copy(0, sibling, me).wait_recv()
        for j, chip in enumerate(chips):
            copy(4 + j, (*chip, 1 - c_), me).wait_recv()
        for cp in first + passed:
            cp.wait_send()
        mine.wait()

    return pl.pallas_call(
        body, name=name,
        out_shape=jax.ShapeDtypeStruct((N_DEV,) + x.shape, x.dtype),
        in_specs=[pl.BlockSpec(memory_space=pl.ANY)],
        out_specs=pl.BlockSpec(memory_space=pl.ANY),
        scratch_shapes=[pltpu.SemaphoreType.DMA((7,)), pltpu.SemaphoreType.DMA((7,)), pltpu.SemaphoreType.DMA],
    )(x)


def _row_chunk(r):
    for rc in (128, 64, 32, 16):
        if r % rc == 0:
            return rc
    raise ValueError(f"rows {r} not a multiple of 16")


def _reduce_scatter(g8, name):
    _, R, C = g8.shape
    rc = _row_chunk(R)

    def body(g_ref, out_ref, recva, sendb, recvb, sa, ra, sb, rb):
        x_, y_, c_ = _mesh_pos()
        sibling = (x_, y_, 1 - c_)
        copies_a = []
        for k in range(4):
            cp = pltpu.make_async_remote_copy(
                src_ref=g_ref.at[2 * k + (1 - c_)], dst_ref=recva.at[k],
                send_sem=sa.at[k], recv_sem=ra.at[k], device_id=sibling, device_id_type=MESH_ID)
            cp.start()
            copies_a.append(cp)
        for cp in copies_a:
            cp.wait_recv()

        def chip_sum(k, rows):
            return g_ref[2 * k + c_, rows, :].astype(F32) + recva[k, rows, :].astype(F32)

        flips = [(1 - x_, y_), (x_, 1 - y_), (1 - x_, 1 - y_)]
        copies_b = []
        for j, (tx, ty) in enumerate(flips):
            kt = 2 * tx + ty

            def fill(i, carry, j=j, kt=kt):
                rows = pl.ds(pl.multiple_of(i * rc, rc), rc)
                sendb[j, rows, :] = chip_sum(kt, rows).astype(BF16)
                return carry

            lax.fori_loop(0, R // rc, fill, 0)
            cp = pltpu.make_async_remote_copy(
                src_ref=sendb.at[j], dst_ref=recvb.at[j],
                send_sem=sb.at[j], recv_sem=rb.at[j], device_id=(tx, ty, c_), device_id_type=MESH_ID)
            cp.start()
            copies_b.append(cp)
        for cp in copies_b:
            cp.wait_recv()
        kme = 2 * x_ + y_

        def total(i, carry):
            rows = pl.ds(pl.multiple_of(i * rc, rc), rc)
            acc = chip_sum(kme, rows)
            for j in range(3):
                acc = acc + recvb[j, rows, :].astype(F32)
            out_ref[rows, :] = acc
            return carry

        lax.fori_loop(0, R // rc, total, 0)
        for cp in copies_a + copies_b:
            cp.wait_send()

    vm = pl.BlockSpec(memory_space=pltpu.VMEM)
    return pl.pallas_call(
        body, name=name,
        out_shape=jax.ShapeDtypeStruct((R, C), F32),
        in_specs=[vm], out_specs=vm,
        scratch_shapes=[pltpu.VMEM((4, R, C), BF16), pltpu.VMEM((3, R, C), BF16), pltpu.VMEM((3, R, C), BF16),
                        pltpu.SemaphoreType.DMA((4,)), pltpu.SemaphoreType.DMA((4,)),
                        pltpu.SemaphoreType.DMA((3,)), pltpu.SemaphoreType.DMA((3,))],
        compiler_params=pltpu.CompilerParams(vmem_limit_bytes=VMEM_LIMIT),
    )(g8)


def _silu(v):
    return v * (1.0 / (1.0 + jnp.exp(-v)))


def _ada_fwd(c_all, w_ada_l, b_ada_l):
    def body(c_ref, w_ref, b_ref, o_ref):
        o_ref[...] = jnp.dot(_silu(c_ref[...]), w_ref[...], precision=lax.Precision.HIGHEST,
                             preferred_element_type=F32) + b_ref[...]
    return pl.pallas_call(body, name="ada_fwd", out_shape=jax.ShapeDtypeStruct((c_all.shape[0], w_ada_l.shape[1]), F32),
                          compiler_params=pltpu.CompilerParams(vmem_limit_bytes=VMEM_LIMIT))(c_all, w_ada_l, b_ada_l)


def _ada_bwd(c_all, dmod_cols):
    def body(c_ref, d_ref, o_ref):
        o_ref[...] = lax.dot_general(_silu(c_ref[...]), d_ref[...], (((0,), (0,)), ((), ())),
                                     precision=lax.Precision.HIGHEST, preferred_element_type=F32)
    return pl.pallas_call(body, name="ada_bwd", out_shape=jax.ShapeDtypeStruct((c_all.shape[1], dmod_cols.shape[1]), F32),
                          compiler_params=pltpu.CompilerParams(vmem_limit_bytes=VMEM_LIMIT))(c_all, dmod_cols)


def _toeplitz_onehot():
    k = np.arange(GRID_W)[:, None]
    q = np.arange(GRID_W)[None, :]
    dc = np.clip(k - q + NA_COLS - 1, 0, N_DC - 1).reshape(-1)
    e = np.zeros((128, GRID_W * GRID_W), np.float32)
    e[dc, np.arange(GRID_W * GRID_W)] = 1.0
    return jnp.asarray(e)


def _rpb_expand(rpb2d, onehot):
    def body(r_ref, e_ref, o_ref):
        o_ref[...] = jnp.dot(r_ref[...], e_ref[...], precision=lax.Precision.HIGHEST, preferred_element_type=F32)
    return pl.pallas_call(body, name="rpb_expand", out_shape=jax.ShapeDtypeStruct((128, GRID_W * GRID_W), F32),
                          compiler_params=pltpu.CompilerParams(vmem_limit_bytes=VMEM_LIMIT))(rpb2d, onehot)


def _rpb_reduce(dtz2d, onehot):
    def body(d_ref, e_ref, o_ref):
        o_ref[...] = lax.dot_general(d_ref[...], e_ref[...], (((1,), (1,)), ((), ())),
                                     precision=lax.Precision.HIGHEST, preferred_element_type=F32)
    return pl.pallas_call(body, name="rpb_reduce", out_shape=jax.ShapeDtypeStruct((128, 128), F32),
                          compiler_params=pltpu.CompilerParams(vmem_limit_bytes=VMEM_LIMIT))(dtz2d, onehot)


NA_PAIRS = NA_HEADS // 2


def _na_bias_table(na_rpb, onehot):
    rpb2d = jnp.pad(na_rpb.reshape(NA_HEADS * N_DR, N_DC), ((0, 128 - NA_HEADS * N_DR), (0, 128 - N_DC)))
    tz = _rpb_expand(rpb2d, onehot)[:NA_HEADS * N_DR].reshape(NA_PAIRS, 2, N_DR, GRID_W, GRID_W)
    col = np.arange(GRID_W)
    cs = np.clip(col - NA_COLS // 2, 0, GRID_W - NA_COLS)
    ok_kq = ((col[None, :] >= cs[:, None]) & (col[None, :] < cs[:, None] + NA_COLS)).T
    tz = jnp.where(jnp.asarray(ok_kq)[None, None, None], tz, NEG)
    return jnp.transpose(tz, (0, 2, 3, 1, 4)).reshape(NA_PAIRS, N_DR * GRID_W, 128)


def _na_bias_grad(db, onehot):
    fold = db.reshape(NA_PAIRS, N_DR, GRID_W, 2, GRID_W)
    dtz2d = jnp.transpose(fold, (0, 3, 1, 2, 4)).reshape(NA_HEADS * N_DR, GRID_W * GRID_W)
    dtz2d = jnp.pad(dtz2d, ((0, 128 - NA_HEADS * N_DR), (0, 0)))
    return _rpb_reduce(dtz2d, onehot)[:NA_HEADS * N_DR, :N_DC].reshape(-1)


def _rope_tables(S):
    half = HEAD_DIM // 2
    inv = ROPE_THETA ** (-jnp.arange(half, dtype=F32) / half)
    ang = jnp.arange(S).astype(F32)[:, None] * inv[None, :]
    cos, sin = jnp.cos(ang), jnp.sin(ang)
    n = ROPE_WIDTH // HEAD_DIM
    return jnp.tile(jnp.concatenate([cos, cos], axis=1), (1, n)), jnp.tile(jnp.concatenate([-sin, sin], axis=1), (1, n))


def _rot_half(t):
    w = t.shape[1]
    lane = lax.broadcasted_iota(jnp.int32, t.shape, 1)
    return jnp.where((lane % HEAD_DIM) < HEAD_DIM // 2, pltpu.roll(t, w - HEAD_DIM // 2, axis=1),
                     pltpu.roll(t, HEAD_DIM // 2, axis=1))


def _tok_spec(w):
    return pl.BlockSpec((TOKEN_TILE, w), lambda i: (i, 0))


def _mod_spec(tps, d):
    return pl.BlockSpec((1, 6, d), lambda i: (i // tps, 0, 0))


def _bstat_spec(tps, w):
    return pl.BlockSpec((1, 8, w), lambda i: (i // tps, 0, 0))


def _attn_in(x2d, mod3, g_attn, w_in, cos_t, sin_t, S):
    T, D = x2d.shape
    tps = S // TOKEN_TILE

    def body(x_ref, mod_ref, g_ref, w_ref, cos_ref, sin_ref, h_ref, qkv_ref):
        xn, _ = _rms(x_ref[...])
        h = (xn * g_ref[...]) * (1.0 + mod_ref[0, 1:2, :]) + mod_ref[0, 0:1, :]
        hb = h.astype(BF16)
        h_ref[...] = hb
        proj = _nt(hb, w_ref[...])
        rb = proj[:, ROPE_LO:ROPE_LO + ROPE_WIDTH]
        rb = rb * cos_ref[...] + _rot_half(rb) * sin_ref[...]
        qkv_ref[:, 0:NA_WIDTH] = (proj[:, 0:NA_WIDTH] * Q_SCALE).astype(BF16)
        qkv_ref[:, NA_WIDTH:ROPE_LO] = proj[:, NA_WIDTH:ROPE_LO].astype(BF16)
        qkv_ref[:, ROPE_LO:ROPE_LO + SW_WIDTH] = (rb[:, 0:SW_WIDTH] * Q_SCALE).astype(BF16)
        qkv_ref[:, ROPE_LO + SW_WIDTH:ROPE_LO + ROPE_WIDTH] = rb[:, SW_WIDTH:].astype(BF16)
        qkv_ref[:, ROPE_LO + ROPE_WIDTH:] = proj[:, ROPE_LO + ROPE_WIDTH:].astype(BF16)

    return pl.pallas_call(
        body, name="attn_in", grid=(T // TOKEN_TILE,),
        in_specs=[_tok_spec(D), _mod_spec(tps, D), _full((1, D)), _full(w_in.shape),
                  pl.BlockSpec((TOKEN_TILE, ROPE_WIDTH), lambda i: (i % tps, 0)),
                  pl.BlockSpec((TOKEN_TILE, ROPE_WIDTH), lambda i: (i % tps, 0))],
        out_specs=[_tok_spec(D), _tok_spec(IN_WIDTH)],
        out_shape=[jax.ShapeDtypeStruct((T, D), BF16), jax.ShapeDtypeStruct((T, IN_WIDTH), BF16)],
        compiler_params=_params("parallel"),
    )(x2d, mod3, g_attn, w_in, cos_t, sin_t)


def _attn_out(oa, ob, x2d, mod3, g_na, g_sw, w_out, S):
    T, D = x2d.shape
    tps = S // TOKEN_TILE

    def body(oa_ref, ob_ref, x_ref, mod_ref, gna_ref, gsw_ref, w_ref, mixin_ref, mix_ref, x1_ref):
        oan, _ = _rms(oa_ref[...])
        obn, _ = _rms(ob_ref[...])
        mixin = jnp.concatenate([oan * gna_ref[...], obn * gsw_ref[...]], axis=1).astype(BF16)
        mixin_ref[...] = mixin
        mix = _nn(mixin, w_ref[...])
        mix_ref[...] = mix
        x1_ref[...] = x_ref[...] + mod_ref[0, 2:3, :] * mix

    return pl.pallas_call(
        body, name="attn_out", grid=(T // TOKEN_TILE,),
        in_specs=[_tok_spec(NA_WIDTH), _tok_spec(SW_WIDTH), _tok_spec(D), _mod_spec(tps, D),
                  _full((1, NA_WIDTH)), _full((1, SW_WIDTH)), _full(w_out.shape)],
        out_specs=[_tok_spec(NA_WIDTH + SW_WIDTH), _tok_spec(D), _tok_spec(D)],
        out_shape=[jax.ShapeDtypeStruct((T, NA_WIDTH + SW_WIDTH), BF16), jax.ShapeDtypeStruct((T, D), F32),
                   jax.ShapeDtypeStruct((T, D), F32)],
        compiler_params=_params("parallel"),
    )(oa, ob, x2d, mod3, g_na, g_sw, w_out)


def _ffn_up(x1, mod3, g_ffn, w_up, S):
    T, D = x1.shape
    F = w_up.shape[0] // 2
    tps = S // TOKEN_TILE

    def body(x1_ref, mod_ref, g_ref, w_ref, h2_ref, val_ref, gt_ref):
        xn, _ = _rms(x1_ref[...])
        h2 = ((xn * g_ref[...]) * (1.0 + mod_ref[0, 4:5, :]) + mod_ref[0, 3:4, :]).astype(BF16)
        h2_ref[...] = h2
        u = _nt(h2, w_ref[...])
        val_ref[...] = u[:, :F].astype(BF16)
        gt_ref[...] = u[:, F:].astype(BF16)

    return pl.pallas_call(
        body, name="ffn_up", grid=(T // TOKEN_TILE,),
        in_specs=[_tok_spec(D), _mod_spec(tps, D), _full((1, D)), _full(w_up.shape)],
        out_specs=[_tok_spec(D), _tok_spec(F), _tok_spec(F)],
        out_shape=[jax.ShapeDtypeStruct((T, D), BF16), jax.ShapeDtypeStruct((T, F), BF16), jax.ShapeDtypeStruct((T, F), BF16)],
        compiler_params=_params("parallel"),
    )(x1, mod3, g_ffn, w_up)


def _halo_specs(T, tps, w):
    per = TOKEN_TILE // 8
    prev = pl.BlockSpec((8, w), lambda i: (jnp.maximum(i * per - 1, 0), 0))
    nxt = pl.BlockSpec((8, w), lambda i: (jnp.minimum((i + 1) * per, T // 8 - 1), 0))
    return prev, nxt


def _seq_shifts(cur, prev_ref, next_ref, tps):
    ti = pl.program_id(0) % tps
    row = lax.broadcasted_iota(jnp.int32, cur.shape, 0)
    before = jnp.where(ti > 0, prev_ref[7:8, :].astype(F32), 0.0)
    after = jnp.where(ti < tps - 1, next_ref[0:1, :].astype(F32), 0.0)
    down = jnp.where(row == 0, before, pltpu.roll(cur, 1, axis=0))
    up = jnp.where(row == cur.shape[0] - 1, after, pltpu.roll(cur, cur.shape[0] - 1, axis=0))
    return down, up


def _conv_gate(gt_ref, prev_ref, next_ref, cw_ref, cb_ref, tps):
    g = gt_ref[...].astype(F32)
    gprev, gnext = _seq_shifts(g, prev_ref, next_ref, tps)
    gc = gprev * cw_ref[0:1, :] + g * cw_ref[1:2, :] + gnext * cw_ref[2:3, :] + cb_ref[...]
    sig = 1.0 / (1.0 + jnp.exp(-gc))
    return g, gprev, gnext, gc, sig


def _ffn_down(gt, val, conv_w, conv_b, w_down, x1, mod3, g_final, target, B, S):
    T, D = x1.shape
    F = gt.shape[1]
    tps = S // TOKEN_TILE
    prev, nxt = _halo_specs(T, tps, F)

    def body(gt_ref, prev_ref, next_ref, val_ref, cw_ref, cb_ref, w_ref, x1_ref, mod_ref, gf_ref, tgt_ref,
             a_ref, dx2_ref, df_ref, gstat_ref, bstat_ref):
        i = pl.program_id(0)
        _, _, _, gc, sig = _conv_gate(gt_ref, prev_ref, next_ref, cw_ref, cb_ref, tps)
        a = (gc * sig * val_ref[...].astype(F32)).astype(BF16)
        a_ref[...] = a
        f = _nn(a, w_ref[...])
        gate = mod_ref[0, 5:6, :]
        x2 = x1_ref[...] + gate * f
        xn, r = _rms(x2)
        err = xn * gf_ref[...] - tgt_ref[...]
        dy = err * (1.0 / D)
        dx2 = _rms_bwd(xn, r, dy * gf_ref[...])
        dx2_ref[...] = dx2
        df_ref[...] = (gate * dx2).astype(BF16)

        @pl.when(i == 0)
        def _():
            gstat_ref[...] = jnp.zeros_like(gstat_ref)

        @pl.when(i % tps == 0)
        def _():
            bstat_ref[...] = jnp.zeros_like(bstat_ref)

        gstat_ref[0:1, :] += jnp.sum(dy * xn, axis=0, keepdims=True)
        tile_loss = jnp.sum(jnp.sum(err * err, axis=1, keepdims=True), axis=0, keepdims=True) * (0.5 / D)
        gstat_ref[1:2, :] += jnp.broadcast_to(tile_loss, (1, D))
        bstat_ref[0, 0:1, :] += jnp.sum(dx2 * f, axis=0, keepdims=True)

    return pl.pallas_call(
        body, name="ffn_down", grid=(T // TOKEN_TILE,),
        in_specs=[_tok_spec(F), prev, nxt, _tok_spec(F), _full(conv_w.shape), _full((1, F)), _full(w_down.shape),
                  _tok_spec(D), _mod_spec(tps, D), _full((1, D)), _tok_spec(D)],
        out_specs=[_tok_spec(F), _tok_spec(D), _tok_spec(D), _full((8, D)), _bstat_spec(tps, D)],
        out_shape=[jax.ShapeDtypeStruct((T, F), BF16), jax.ShapeDtypeStruct((T, D), F32), jax.ShapeDtypeStruct((T, D), BF16),
                   jax.ShapeDtypeStruct((8, D), F32), jax.ShapeDtypeStruct((B, 8, D), F32)],
        compiler_params=_params("arbitrary"),
    )(gt, gt, gt, val, conv_w, conv_b, w_down, x1, mod3, g_final, target)


def _ffn_down_bwd(df, w_down, gt, val, conv_w, conv_b, S):
    T, D = df.shape
    F = gt.shape[1]
    tps = S // TOKEN_TILE
    prev, nxt = _halo_specs(T, tps, F)

    def body(df_ref, w_ref, gt_ref, prev_ref, next_ref, val_ref, cw_ref, cb_ref, dval_ref, dgc_ref, cstat_ref):
        g, gprev, gnext, gc, sig = _conv_gate(gt_ref, prev_ref, next_ref, cw_ref, cb_ref, tps)
        da = _nt(df_ref[...], w_ref[...])
        dval_ref[...] = (da * (gc * sig)).astype(BF16)
        dgc = da * val_ref[...].astype(F32) * (sig * (1.0 + gc * (1.0 - sig)))
        dgc_ref[...] = dgc.astype(BF16)

        @pl.when(pl.program_id(0) == 0)
        def _():
            cstat_ref[...] = jnp.zeros_like(cstat_ref)

        cstat_ref[0:1, :] += jnp.sum(dgc, axis=0, keepdims=True)
        cstat_ref[1:2, :] += jnp.sum(dgc * gprev, axis=0, keepdims=True)
        cstat_ref[2:3, :] += jnp.sum(dgc * g, axis=0, keepdims=True)
        cstat_ref[3:4, :] += jnp.sum(dgc * gnext, axis=0, keepdims=True)

    return pl.pallas_call(
        body, name="ffn_down_bwd", grid=(T // TOKEN_TILE,),
        in_specs=[_tok_spec(D), _full(w_down.shape), _tok_spec(F), prev, nxt, _tok_spec(F), _full(conv_w.shape), _full((1, F))],
        out_specs=[_tok_spec(F), _tok_spec(F), _full((8, F))],
        out_shape=[jax.ShapeDtypeStruct((T, F), BF16), jax.ShapeDtypeStruct((T, F), BF16), jax.ShapeDtypeStruct((8, F), F32)],
        compiler_params=_params("arbitrary"),
    )(df, w_down, gt, gt, gt, val, conv_w, conv_b)


def _ffn_up_bwd(dgc, dval, conv_w, w_up, x1, mod3, g_ffn, dx2, mix, B, S):
    T, D = x1.shape
    F = dgc.shape[1]
    tps = S // TOKEN_TILE
    prev, nxt = _halo_specs(T, tps, F)

    def body(dgc_ref, prev_ref, next_ref, dval_ref, cw_ref, w_ref, x1_ref, mod_ref, g_ref, dx2_ref, mix_ref,
             du_ref, dx1_ref, dmix_ref, gstat_ref, bstat_ref):
        i = pl.program_id(0)
        d = dgc_ref[...].astype(F32)
        dprev, dnext = _seq_shifts(d, prev_ref, next_ref, tps)
        dgt = dnext * cw_ref[0:1, :] + d * cw_ref[1:2, :] + dprev * cw_ref[2:3, :]
        du = jnp.concatenate([dval_ref[...], dgt.astype(BF16)], axis=1)
        du_ref[...] = du
        dh2 = _nn(du, w_ref[...])
        xn, r = _rms(x1_ref[...])
        scale1 = 1.0 + mod_ref[0, 4:5, :]
        xg = xn * g_ref[...]
        dx1 = dx2_ref[...] + _rms_bwd(xn, r, dh2 * g_ref[...] * scale1)
        dx1_ref[...] = dx1
        dmix_ref[...] = (mod_ref[0, 2:3, :] * dx1).astype(BF16)

        @pl.when(i == 0)
        def _():
            gstat_ref[...] = jnp.zeros_like(gstat_ref)

        @pl.when(i % tps == 0)
        def _():
            bstat_ref[...] = jnp.zeros_like(bstat_ref)

        gstat_ref[0:1, :] += jnp.sum(dh2 * scale1 * xn, axis=0, keepdims=True)
        bstat_ref[0, 0:1, :] += jnp.sum(dh2, axis=0, keepdims=True)
        bstat_ref[0, 1:2, :] += jnp.sum(dh2 * xg, axis=0, keepdims=True)
        bstat_ref[0, 2:3, :] += jnp.sum(dx1 * mix_ref[...], axis=0, keepdims=True)

    return pl.pallas_call(
        body, name="ffn_up_bwd", grid=(T // TOKEN_TILE,),
        in_specs=[_tok_spec(F), prev, nxt, _tok_spec(F), _full(conv_w.shape), _full(w_up.shape), _tok_spec(D),
                  _mod_spec(tps, D), _full((1, D)), _tok_spec(D), _tok_spec(D)],
        out_specs=[_tok_spec(2 * F), _tok_spec(D), _tok_spec(D), _full((8, D)), _bstat_spec(tps, D)],
        out_shape=[jax.ShapeDtypeStruct((T, 2 * F), BF16), jax.ShapeDtypeStruct((T, D), F32), jax.ShapeDtypeStruct((T, D), BF16),
                   jax.ShapeDtypeStruct((8, D), F32), jax.ShapeDtypeStruct((B, 8, D), F32)],
        compiler_params=_params("arbitrary"),
    )(dgc, dgc, dgc, dval, conv_w, w_up, x1, mod3, g_ffn, dx2, mix)


def _attn_out_bwd(dmix, w_out, oa, ob, g_na, g_sw):
    T, D = dmix.shape

    def body(dmix_ref, w_ref, oa_ref, ob_ref, gna_ref, gsw_ref, doa_ref, dob_ref, gstat_ref):
        dmixin = _nt(dmix_ref[...], w_ref[...])

        @pl.when(pl.program_id(0) == 0)
        def _():
            gstat_ref[...] = jnp.zeros_like(gstat_ref)

        for k, (o_ref, g_ref, do_ref) in enumerate(((oa_ref, gna_ref, doa_ref), (ob_ref, gsw_ref, dob_ref))):
            dn = dmixin[:, k * NA_WIDTH:(k + 1) * NA_WIDTH]
            on, r = _rms(o_ref[...])
            gstat_ref[k:k + 1, :] += jnp.sum(dn * on, axis=0, keepdims=True)
            do_ref[...] = _rms_bwd(on, r, dn * g_ref[...]).astype(BF16)

    hs = jax.ShapeDtypeStruct((T, NA_WIDTH), BF16)
    return pl.pallas_call(
        body, name="attn_out_bwd", grid=(T // TOKEN_TILE,),
        in_specs=[_tok_spec(D), _full(w_out.shape), _tok_spec(NA_WIDTH), _tok_spec(SW_WIDTH), _full((1, NA_WIDTH)), _full((1, SW_WIDTH))],
        out_specs=[_tok_spec(NA_WIDTH), _tok_spec(SW_WIDTH), _full((8, NA_WIDTH))],
        out_shape=[hs, hs, jax.ShapeDtypeStruct((8, NA_WIDTH), F32)],
        compiler_params=_params("arbitrary"),
    )(dmix, w_out, oa, ob, g_na, g_sw)


def _attn_in_bwd(dqa, dka, dva, dqb, dkb, dvb, cos_t, sin_t, w_in, x2d, mod3, g_attn, dx1, B, S):
    T, D = x2d.shape
    tps = S // TOKEN_TILE

    def body(dqa_ref, dka_ref, dva_ref, dqb_ref, dkb_ref, dvb_ref, cos_ref, sin_ref, w_ref, x_ref, mod_ref, g_ref, dx1_ref,
             gx_ref, dproj_ref, gstat_ref, bstat_ref):
        i = pl.program_id(0)
        drb = jnp.concatenate([dqb_ref[...] * Q_SCALE, dkb_ref[...]], axis=1)
        drb = drb * cos_ref[...] + _rot_half(drb * sin_ref[...])
        dproj = jnp.concatenate([dqa_ref[...] * Q_SCALE, dka_ref[...], dva_ref[...], drb, dvb_ref[...]], axis=1).astype(BF16)
        dproj_ref[...] = dproj
        dh = _nn(dproj, w_ref[...])
        xn, r = _rms(x_ref[...])
        scale1 = 1.0 + mod_ref[0, 1:2, :]
        gx_ref[...] = dx1_ref[...] + _rms_bwd(xn, r, dh * g_ref[...] * scale1)

        @pl.when(i == 0)
        def _():
            gstat_ref[...] = jnp.zeros_like(gstat_ref)

        @pl.when(i % tps == 0)
        def _():
            bstat_ref[...] = jnp.zeros_like(bstat_ref)

        gstat_ref[0:1, :] += jnp.sum(dh * scale1 * xn, axis=0, keepdims=True)
        bstat_ref[0, 0:1, :] += jnp.sum(dh, axis=0, keepdims=True)
        bstat_ref[0, 1:2, :] += jnp.sum(dh * (xn * g_ref[...]), axis=0, keepdims=True)

    rope = pl.BlockSpec((TOKEN_TILE, ROPE_WIDTH), lambda i: (i % tps, 0))
    return pl.pallas_call(
        body, name="attn_in_bwd", grid=(T // TOKEN_TILE,),
        in_specs=[_tok_spec(NA_WIDTH), _tok_spec(NA_WIDTH), _tok_spec(NA_WIDTH), _tok_spec(SW_WIDTH), _tok_spec(SW_KV_WIDTH),
                  _tok_spec(SW_KV_WIDTH), rope, rope, _full(w_in.shape), _tok_spec(D), _mod_spec(tps, D), _full((1, D)), _tok_spec(D)],
        out_specs=[_tok_spec(D), _tok_spec(IN_WIDTH), _full((8, D)), _bstat_spec(tps, D)],
        out_shape=[jax.ShapeDtypeStruct((T, D), F32), jax.ShapeDtypeStruct((T, IN_WIDTH), BF16),
                   jax.ShapeDtypeStruct((8, D), F32), jax.ShapeDtypeStruct((B, 8, D), F32)],
        compiler_params=_params("arbitrary"),
    )(dqa, dka, dva, dqb, dkb, dvb, cos_t, sin_t, w_in, x2d, mod3, g_attn, dx1)


def _matmul_tn(a, b, name, tm=None, tk=512):
    T, M = a.shape
    N = b.shape[1]
    tm = M if tm is None else tm
    nk = T // tk

    def body(a_ref, b_ref, o_ref, acc):
        k = pl.program_id(1)

        @pl.when(k == 0)
        def _():
            acc[...] = jnp.zeros_like(acc)

        acc[...] += _tn(a_ref[...], b_ref[...])

        @pl.when(k == nk - 1)
        def _():
            o_ref[...] = acc[...].astype(BF16)

    return pl.pallas_call(
        body, name=name, grid=(M // tm, nk),
        in_specs=[pl.BlockSpec((tk, tm), lambda i, k: (k, i)), pl.BlockSpec((tk, N), lambda i, k: (k, 0))],
        out_specs=pl.BlockSpec((tm, N), lambda i, k: (i, 0)),
        out_shape=jax.ShapeDtypeStruct((M, N), BF16),
        scratch_shapes=[pltpu.VMEM((tm, N), F32)],
        compiler_params=_params("parallel", "arbitrary"),
    )(a, b)


def _na_geometry(S):
    rows = S // GRID_W
    wr = min(NA_ROWS_MAX, rows)
    return rows, wr


def _na_window(r, rows, wr):
    rs = jnp.clip(r - wr // 2, 0, rows - wr)
    return pl.multiple_of(rs * GRID_W, GRID_W), pl.multiple_of((rs - r + NA_ROWS_MAX - 1) * GRID_W, GRID_W)


NA_STEP_PAIRS = 2
NA_GW = NA_STEP_PAIRS * 128


def _na_specs(S, kw_n, order):
    ng = NA_PAIRS // NA_STEP_PAIRS

    def col(k):
        return pl.BlockSpec((1, S, NA_GW), lambda *ids: (order(*ids)[0], 0, k * ng + order(*ids)[1]))
    bias = pl.BlockSpec((NA_STEP_PAIRS, N_DR * GRID_W, 128), lambda *ids: (order(*ids)[1], 0, 0))
    out = pl.BlockSpec((1, S, NA_GW), lambda *ids: (order(*ids)[0], 0, order(*ids)[1]))
    return col(0), col(1), col(2), bias, out


def _block_diag(t):
    left = lax.broadcasted_iota(jnp.int32, t.shape, 1) < HEAD_DIM
    zero = jnp.zeros_like(t)
    return jnp.concatenate([jnp.where(left, t, zero), jnp.where(left, zero, t)], axis=0)


def _diag_blocks(res):
    left = lax.broadcasted_iota(jnp.int32, (HEAD_DIM, 128), 1) < HEAD_DIM
    return jnp.where(left, res[:HEAD_DIM], res[HEAD_DIM:])


def _col_softmax(st):
    e = jnp.exp(st - jnp.max(st, axis=0, keepdims=True))
    return e * (1.0 / jnp.sum(e, axis=0, keepdims=True))


def _na_fwd(qkv, bias):
    B, S, _ = qkv.shape
    rows, wr = _na_geometry(S)
    kw_n = wr * GRID_W

    def body(q_ref, k_ref, v_ref, b_ref, o_ref):
        def step(r, carry):
            start, boff = _na_window(r, rows, wr)
            qrows = pl.ds(pl.multiple_of(r * GRID_W, GRID_W), GRID_W)
            krows = pl.ds(start, kw_n)
            brows = pl.ds(boff, kw_n)
            lanes = [pl.ds(p * 128, 128) for p in range(NA_STEP_PAIRS)]
            st = [_nt(k_ref[0, krows, ln], _block_diag(q_ref[0, qrows, ln])) for ln in lanes]
            pn = [_col_softmax(st[p] + b_ref[p, brows, :]).astype(BF16) for p in range(NA_STEP_PAIRS)]
            outs = [_diag_blocks(_tn(pn[p], v_ref[0, krows, lanes[p]])) for p in range(NA_STEP_PAIRS)]
            o_ref[0, qrows, :] = jnp.concatenate(outs, axis=1)
            return carry

        lax.fori_loop(0, rows, step, 0)

    q, k, v, bs, out = _na_specs(S, kw_n, lambda b, g: (b, g))
    return pl.pallas_call(
        body, name="na_fwd", grid=(B, NA_PAIRS // NA_STEP_PAIRS),
        in_specs=[q, k, v, bs], out_specs=out, out_shape=jax.ShapeDtypeStruct((B, S, NA_WIDTH), F32),
        compiler_params=_params("parallel", "parallel"),
    )(qkv, qkv, qkv, bias)


def _na_bwd(qkv, bias, doa):
    B, S, _ = qkv.shape
    rows, wr = _na_geometry(S)
    kw_n = wr * GRID_W

    def body(q_ref, k_ref, v_ref, b_ref, do_ref, dq_ref, dk_ref, dv_ref, db_ref):
        @pl.when(pl.program_id(1) == 0)
        def _():
            db_ref[...] = jnp.zeros_like(db_ref)

        dk_ref[...] = jnp.zeros_like(dk_ref)
        dv_ref[...] = jnp.zeros_like(dv_ref)

        def step(r, carry):
            start, boff = _na_window(r, rows, wr)
            qrows = pl.ds(pl.multiple_of(r * GRID_W, GRID_W), GRID_W)
            krows = pl.ds(start, kw_n)
            brows = pl.ds(boff, kw_n)
            pairs = range(NA_STEP_PAIRS)
            lanes = [pl.ds(p * 128, 128) for p in pairs]
            kp = [k_ref[0, krows, ln] for ln in lanes]
            qbd = [_block_diag(q_ref[0, qrows, ln]) for ln in lanes]
            dobd = [_block_diag(do_ref[0, qrows, ln]) for ln in lanes]
            st = [_nt(kp[p], qbd[p]) for p in pairs]
            dpt = [_nt(v_ref[0, krows, lanes[p]], dobd[p]) for p in pairs]
            pn = [_col_softmax(st[p] + b_ref[p, brows, :]) for p in pairs]
            dst = [pn[p] * (dpt[p] - jnp.sum(pn[p] * dpt[p], axis=0, keepdims=True)) for p in pairs]
            dsb = [d.astype(BF16) for d in dst]
            dq_ref[0, qrows, :] = jnp.concatenate([_diag_blocks(_tn(dsb[p], kp[p])) for p in pairs], axis=1)
            dk_ref[0, krows, :] += jnp.concatenate([_nn(dsb[p], qbd[p]) for p in pairs], axis=1)
            dv_ref[0, krows, :] += jnp.concatenate([_nn(pn[p].astype(BF16), dobd[p]) for p in pairs], axis=1)
            for p in pairs:
                db_ref[p, brows, :] += dst[p]
            return carry

        lax.fori_loop(0, rows, step, 0)

    q, k, v, bs, out = _na_specs(S, kw_n, lambda g, b: (b, g))
    hs = jax.ShapeDtypeStruct((B, S, NA_WIDTH), F32)
    return pl.pallas_call(
        body, name="na_bwd", grid=(NA_PAIRS // NA_STEP_PAIRS, B),
        in_specs=[q, k, v, bs, out],
        out_specs=[out, out, out, bs],
        out_shape=[hs, hs, hs, jax.ShapeDtypeStruct((NA_PAIRS, N_DR * GRID_W, 128), F32)],
        compiler_params=_params("parallel", "arbitrary"),
    )(qkv, qkv, qkv, bias, doa)


SW_PAIRS = SW_HEADS // 2


def _sw_band(n, S):
    kw_n = 3 * SW_BLOCK
    start = pl.multiple_of(jnp.clip(n * SW_BLOCK - SW_BLOCK, 0, S - kw_n), SW_BLOCK)
    kpos = start + lax.broadcasted_iota(jnp.int32, (kw_n, SW_BLOCK), 0)
    qpos = n * SW_BLOCK + lax.broadcasted_iota(jnp.int32, (kw_n, SW_BLOCK), 1)
    return start, jnp.abs(qpos - kpos) <= SW_WINDOW


def _kv_halves(t):
    left = lax.broadcasted_iota(jnp.int32, t.shape, 1) < HEAD_DIM
    swapped = pltpu.roll(t, HEAD_DIM, axis=1)
    zero = jnp.zeros_like(t)
    return {(0, 0): jnp.where(left, t, zero), (0, 1): jnp.where(left, zero, swapped),
            (1, 0): jnp.where(left, swapped, zero), (1, 1): jnp.where(left, zero, t)}


def _sw_probs(st, ok, sk):
    st = jnp.where(ok, st, NEG)
    m = jnp.maximum(jnp.max(st, axis=0, keepdims=True), sk)
    e = jnp.exp(st - m)
    esk = jnp.exp(sk - m)
    inv = 1.0 / (jnp.sum(e, axis=0, keepdims=True) + esk)
    return e * inv, esk * inv


def _sw_specs(S):
    q = pl.BlockSpec((1, S, SW_WIDTH), lambda b: (b, 0, ROPE_LO // SW_WIDTH))
    k = pl.BlockSpec((1, S, SW_KV_WIDTH), lambda b: (b, 0, (ROPE_LO + SW_WIDTH) // SW_KV_WIDTH))
    v = pl.BlockSpec((1, S, SW_KV_WIDTH), lambda b: (b, 0, (ROPE_LO + ROPE_WIDTH) // SW_KV_WIDTH))
    return q, k, v


def _sw_fwd(sink, qkv):
    B, S, _ = qkv.shape
    kw_n = 3 * SW_BLOCK

    def body(sink_ref, q_ref, k_ref, v_ref, o_ref):
        def step(n, carry):
            start, ok = _sw_band(n, S)
            qrows = pl.ds(pl.multiple_of(n * SW_BLOCK, SW_BLOCK), SW_BLOCK)
            krows = pl.ds(start, kw_n)
            kh, vh = _kv_halves(k_ref[0, krows, :]), _kv_halves(v_ref[0, krows, :])
            heads = [(p, e) for p in range(SW_PAIRS) for e in range(2)]
            qp = [q_ref[0, qrows, pl.ds(p * 128, 128)] for p in range(SW_PAIRS)]
            kv_of = lambda p: p // (SW_PAIRS // SW_KV_HEADS)
            st = {(p, e): _nt(kh[(kv_of(p), e)], qp[p]) for p, e in heads}
            pn = {(p, e): _sw_probs(st[(p, e)], ok, sink_ref[2 * p + e])[0].astype(BF16) for p, e in heads}
            outs = [_tn(pn[(p, 0)], vh[(kv_of(p), 0)]) + _tn(pn[(p, 1)], vh[(kv_of(p), 1)]) for p in range(SW_PAIRS)]
            o_ref[0, qrows, :] = jnp.concatenate(outs, axis=1)
            return carry

        lax.fori_loop(0, S // SW_BLOCK, step, 0)

    q, k, v = _sw_specs(S)
    return pl.pallas_call(
        body, name="sw_fwd", grid=(B,),
        in_specs=[pl.BlockSpec(memory_space=pltpu.SMEM), q, k, v],
        out_specs=pl.BlockSpec((1, S, SW_WIDTH), lambda b: (b, 0, 0)), out_shape=jax.ShapeDtypeStruct((B, S, SW_WIDTH), F32),
        compiler_params=_params("parallel"),
    )(sink, qkv, qkv, qkv)


def _sw_bwd(sink, qkv, dob):
    B, S, _ = qkv.shape
    kw_n = 3 * SW_BLOCK

    fold_rows = 256

    def body(sink_ref, q_ref, k_ref, v_ref, do_ref, dq_ref, dk_ref, dv_ref, dsink_ref, dk_acc, dv_acc):
        @pl.when(pl.program_id(0) == 0)
        def _():
            dsink_ref[...] = jnp.zeros_like(dsink_ref)

        dk_acc[...] = jnp.zeros_like(dk_acc)
        dv_acc[...] = jnp.zeros_like(dv_acc)
        ppk = SW_PAIRS // SW_KV_HEADS

        def step(n, carry):
            start, ok = _sw_band(n, S)
            qrows = pl.ds(pl.multiple_of(n * SW_BLOCK, SW_BLOCK), SW_BLOCK)
            krows = pl.ds(start, kw_n)
            kh, vh = _kv_halves(k_ref[0, krows, :]), _kv_halves(v_ref[0, krows, :])
            heads = [(p, e) for p in range(SW_PAIRS) for e in range(2)]
            qp = [q_ref[0, qrows, pl.ds(p * 128, 128)] for p in range(SW_PAIRS)]
            dop = [do_ref[0, qrows, pl.ds(p * 128, 128)] for p in range(SW_PAIRS)]
            st = {(p, e): _nt(kh[(p // ppk, e)], qp[p]) for p, e in heads}
            dpt = {(p, e): _nt(vh[(p // ppk, e)], dop[p]) for p, e in heads}
            pnb, dsb = {}, {}
            for p, e in heads:
                pn, psink = _sw_probs(st[(p, e)], ok, sink_ref[2 * p + e])
                delta = jnp.sum(pn * dpt[(p, e)], axis=0, keepdims=True)
                dsb[(p, e)] = (pn * (dpt[(p, e)] - delta)).astype(BF16)
                pnb[(p, e)] = pn.astype(BF16)
                dsink_ref[2 * p + e:2 * p + e + 1, :] += -(psink * delta)
            dq_ref[0, qrows, :] = jnp.concatenate(
                [_tn(dsb[(p, 0)], kh[(p // ppk, 0)]) + _tn(dsb[(p, 1)], kh[(p // ppk, 1)]) for p in range(SW_PAIRS)], axis=1)
            left = lax.broadcasted_iota(jnp.int32, (kw_n, 128), 1) < HEAD_DIM
            dks, dvs = [], []
            for kv in range(SW_KV_HEADS):
                dk = dv = None
                for p in range(kv * ppk, (kv + 1) * ppk):
                    dk_p = jnp.where(left, _nn(dsb[(p, 0)], qp[p]), _nn(dsb[(p, 1)], qp[p]))
                    dv_p = jnp.where(left, _nn(pnb[(p, 0)], dop[p]), _nn(pnb[(p, 1)], dop[p]))
                    dk = dk_p if dk is None else dk + dk_p
                    dv = dv_p if dv is None else dv + dv_p
                dks.append(dk)
                dvs.append(dv)
            dk_acc[krows, :] += jnp.concatenate(dks, axis=1)
            dv_acc[krows, :] += jnp.concatenate(dvs, axis=1)
            return carry

        lax.fori_loop(0, S // SW_BLOCK, step, 0)

        def fold(i, carry):
            rows = pl.ds(pl.multiple_of(i * fold_rows, fold_rows), fold_rows)
            left = lax.broadcasted_iota(jnp.int32, (fold_rows, 128), 1) < HEAD_DIM
            for acc, out_ref in ((dk_acc, dk_ref), (dv_acc, dv_ref)):
                a, b = acc[rows, 0:128], acc[rows, 128:256]
                out_ref[0, rows, :] = jnp.where(left, a + pltpu.roll(a, HEAD_DIM, axis=1), b + pltpu.roll(b, HEAD_DIM, axis=1))
            return carry

        lax.fori_loop(0, S // fold_rows, fold, 0)

        @pl.when(pl.program_id(0) == B - 1)
        def _():
            dsink_ref[...] = jnp.broadcast_to(jnp.sum(dsink_ref[...], axis=1, keepdims=True), dsink_ref.shape)

    q, k, v = _sw_specs(S)
    qo = pl.BlockSpec((1, S, SW_WIDTH), lambda b: (b, 0, 0))
    ko = pl.BlockSpec((1, S, SW_KV_WIDTH), lambda b: (b, 0, 0))
    return pl.pallas_call(
        body, name="sw_bwd", grid=(B,),
        in_specs=[pl.BlockSpec(memory_space=pltpu.SMEM), q, k, v, qo],
        out_specs=[qo, ko, ko, _full((SW_HEADS, 128))],
        out_shape=[jax.ShapeDtypeStruct((B, S, SW_WIDTH), F32), jax.ShapeDtypeStruct((B, S, SW_KV_WIDTH), F32),
                   jax.ShapeDtypeStruct((B, S, SW_KV_WIDTH), F32), jax.ShapeDtypeStruct((SW_HEADS, 128), F32)],
        scratch_shapes=[pltpu.VMEM((S, 2 * SW_KV_WIDTH), F32), pltpu.VMEM((S, 2 * SW_KV_WIDTH), F32)],
        compiler_params=_params("arbitrary"),
    )(sink, qkv, qkv, qkv, dob)


def _pack_sum(packs):
    W = packs.shape[1]

    def body(p_ref, o_ref):
        tot = p_ref[0:8, :]
        for d in range(1, N_DEV):
            tot = tot + p_ref[8 * d:8 * d + 8, :]
        o_ref[...] = tot
        o_ref[5:6, :] = tot[0:1, :] + tot[1:2, :]

    return pl.pallas_call(body, name="pack_sum", out_shape=jax.ShapeDtypeStruct((8, W), F32),
                          compiler_params=pltpu.CompilerParams(vmem_limit_bytes=VMEM_LIMIT))(packs)


def _adamw(w, g, m, v, name):
    c1 = 1.0 - ADAM_B1 ** ADAM_STEP
    c2 = 1.0 - ADAM_B2 ** ADAM_STEP

    def body(w_ref, g_ref, m_ref, v_ref, d_ref, nm_ref, nv_ref):
        g_ = g_ref[...]
        nm = ADAM_B1 * m_ref[...] + (1.0 - ADAM_B1) * g_
        nv = ADAM_B2 * v_ref[...] + (1.0 - ADAM_B2) * (g_ * g_)
        nm_ref[...] = nm
        nv_ref[...] = nv
        d_ref[...] = -ADAM_LR * ((nm / c1) / (jnp.sqrt(nv / c2) + ADAM_EPS) + ADAM_WD * w_ref[...])

    s = jax.ShapeDtypeStruct(w.shape, F32)
    return pl.pallas_call(body, name=name, out_shape=[s, s, s],
                          compiler_params=pltpu.CompilerParams(vmem_limit_bytes=VMEM_LIMIT))(w, g, m, v)


def _gather_rows(shard, name):
    g = _all_gather(shard.astype(BF16), name)
    return g.reshape(N_DEV * shard.shape[0], shard.shape[1])


def _scatter_rows(dw, name):
    R8, C = dw.shape
    return _reduce_scatter(dw.reshape(N_DEV, R8 // N_DEV, C), name)


def _local_step(x, mod, g_attn, w_in, bias, sw_sink, g_na_out, g_sw_out, w_out, g_ffn, w_up, conv_w, conv_b, w_down,
                g_final, target):
    B, S, D = x.shape
    T = B * S
    x2d = x.reshape(T, D)
    mod3 = mod.reshape(B, 6, D)
    cos_t, sin_t = _rope_tables(S)
    sink = sw_sink.reshape(SW_HEADS)

    h, qkv = _attn_in(x2d, mod3, g_attn, w_in, cos_t, sin_t, S)
    qkv3 = qkv.reshape(B, S, IN_WIDTH)
    oa = _na_fwd(qkv3, bias).reshape(T, NA_WIDTH)
    ob = _sw_fwd(sink, qkv3).reshape(T, SW_WIDTH)
    mixin, mix, x1 = _attn_out(oa, ob, x2d, mod3, g_na_out, g_sw_out, w_out, S)
    h2, val, gt = _ffn_up(x1, mod3, g_ffn, w_up, S)
    a, dx2, df, gstat_f, bstat_f = _ffn_down(gt, val, conv_w, conv_b, w_down, x1, mod3, g_final, target.reshape(T, D), B, S)

    dval, dgc, cstat = _ffn_down_bwd(df, w_down, gt, val, conv_w, conv_b, S)
    du, dx1, dmix, gstat_u, bstat_u = _ffn_up_bwd(dgc, dval, conv_w, w_up, x1, mod3, g_ffn, dx2, mix, B, S)
    doa, dob, gstat_o = _attn_out_bwd(dmix, w_out, oa, ob, g_na_out, g_sw_out)
    dqa, dka, dva, dbt = _na_bwd(qkv3, bias, doa.reshape(B, S, NA_WIDTH))
    dqb, dkb, dvb, dsink = _sw_bwd(sink, qkv3, dob.reshape(B, S, SW_WIDTH))
    r2 = lambda t: t.reshape(T, t.shape[-1])
    grad_x, dproj, gstat_i, bstat_i = _attn_in_bwd(r2(dqa), r2(dka), r2(dva), r2(dqb), r2(dkb), r2(dvb), cos_t, sin_t, w_in, x2d, mod3,
                                                   g_attn, dx1, B, S)

    F = val.shape[1]
    dw_in = _matmul_tn(dproj, h, "dw_in")
    dw_out = _matmul_tn(mixin, dmix, "dw_out")
    dw_up = _matmul_tn(du, h2, "dw_up", tm=F)
    dw_down = _matmul_tn(a, df, "dw_down")

    dmod = jnp.stack([bstat_i[:, 0], bstat_i[:, 1], bstat_u[:, 2], bstat_u[:, 0], bstat_u[:, 1], bstat_f[:, 0]], axis=1)
    small = dict(g_attn=gstat_i[0], g_ffn=gstat_u[0], g_final=gstat_f[0], loss=gstat_f[1, 0], g_na_out=gstat_o[0], g_sw_out=gstat_o[1],
                 sw_sink=dsink[:, 0], conv_b=cstat[0], conv_w=cstat[1:4], dbt=dbt)
    return grad_x.reshape(B, S, D), dict(w_in=dw_in, w_out=dw_out, w_up=dw_up, w_down=dw_down), dmod, small


def _pad_lanes(v, w):
    return jnp.pad(v, (0, w - v.shape[0]))


def kernel(x, c, w_ada, b_ada, g_attn, w_in, na_rpb, sw_sink, g_na_out, g_sw_out, w_out, g_ffn, w_up, conv_w, conv_b, w_down, g_final, loss_target, m_w_ada, m_b_ada, m_g_attn, m_w_in, m_na_rpb, m_sw_sink, m_g_na_out, m_g_sw_out, m_w_out, m_g_ffn, m_w_up, m_conv_w, m_conv_b, m_w_down, m_g_final, v_w_ada, v_b_ada, v_g_attn, v_w_in, v_na_rpb, v_sw_sink, v_g_na_out, v_g_sw_out, v_w_out, v_g_ffn, v_w_up, v_conv_w, v_conv_b, v_w_down, v_g_final):
    B, S, D = x.shape
    me = 4 * lax.axis_index("x") + 2 * lax.axis_index("y") + lax.axis_index("c")
    ada_c = w_ada.shape[2]
    F_l = conv_w.shape[2]
    onehot = _toeplitz_onehot()

    cw_l = jnp.pad(conv_w[0], ((0, 8 - conv_w.shape[1]), (0, 0)))
    c_l = jnp.pad(c, ((0, 8 - B), (0, 0)))
    slabs = _all_gather(jnp.concatenate([c_l, cw_l], axis=1), "gather_c")
    c_all = slabs[:, :, :D].reshape(N_DEV * 8, D)
    conv_w_f = jnp.transpose(slabs[:, :3, D:], (1, 0, 2)).reshape(3, N_DEV * F_l)

    b_ada_l = lax.dynamic_slice(b_ada, (0, me * ada_c), (1, ada_c))
    mod_cols = _ada_fwd(c_all, w_ada[0], b_ada_l)
    mod_all = _all_gather(mod_cols, "gather_mod")
    mod_mine = lax.dynamic_slice(mod_all, (0, me * 8, 0), (N_DEV, B, ada_c))
    mod = jnp.transpose(mod_mine, (1, 0, 2)).reshape(B, N_DEV * ada_c)

    tr = {"w_in", "w_up"}
    w_in_f = _gather_rows(jnp.transpose(w_in[0]), "gather_w_in")
    w_out_f = _gather_rows(w_out[0], "gather_w_out")
    w_up_f = _gather_rows(jnp.transpose(w_up[0]), "gather_w_up")
    w_down_f = _gather_rows(w_down[0], "gather_w_down")

    bias = _na_bias_table(na_rpb[0], onehot)

    grad_x, dw, dmod, small = _local_step(x, mod, g_attn, w_in_f, bias, sw_sink, g_na_out, g_sw_out, w_out_f, g_ffn, w_up_f,
                                          conv_w_f, conv_b, w_down_f, g_final.reshape(1, D), loss_target)

    g_w_in = _scatter_rows(dw["w_in"], "scatter_w_in")
    g_w_out = _scatter_rows(dw["w_out"], "scatter_w_out")
    g_w_up = _scatter_rows(dw["w_up"], "scatter_w_up")
    g_w_down = _scatter_rows(dw["w_down"], "scatter_w_down")

    drpb = _na_bias_grad(small["dbt"], onehot)

    row2 = jnp.concatenate([small["g_attn"], small["g_ffn"], small["g_final"], small["g_na_out"], small["g_sw_out"],
                            _pad_lanes(small["sw_sink"], 128), _pad_lanes(small["loss"].reshape(1), 128)])
    rows = [dmod.reshape(B, 6 * D)[0], dmod.reshape(B, 6 * D)[1], _pad_lanes(row2, PACK_W), _pad_lanes(small["conv_b"], PACK_W),
            _pad_lanes(drpb, PACK_W)] + [jnp.zeros((PACK_W,), F32)] * 3
    packs = _all_gather(jnp.stack(rows), "gather_small")
    cw_rows = jnp.pad(small["conv_w"], ((0, 5), (0, 0)))
    cw_packs = _all_gather(cw_rows, "gather_dconv")
    tot = _pack_sum(packs.reshape(N_DEV * 8, PACK_W))
    cw_tot = _pack_sum(cw_packs.reshape(N_DEV * 8, cw_rows.shape[1]))

    o = 0
    rep = {}
    for nm, wd in (("g_attn", D), ("g_ffn", D), ("g_final", D), ("g_na_out", NA_WIDTH), ("g_sw_out", SW_WIDTH), ("sw_sink", 128), ("loss", 128)):
        rep[nm] = tot[2, o:o + wd]
        o += wd
    loss = rep["loss"][0]
    g_b_ada = tot[5:6, :]
    g_conv_b = tot[3:4, :conv_b.shape[1]]
    g_na_rpb = tot[4, :NA_HEADS * N_DR * N_DC].reshape(na_rpb.shape)
    g_conv_w = lax.dynamic_slice(cw_tot[0:3], (0, me * F_l), (3, F_l)).reshape(conv_w.shape)

    dmod_cols = lax.dynamic_slice(packs.reshape(N_DEV * 8, PACK_W), (0, me * ada_c), (N_DEV * 8, ada_c))
    g_w_ada = _ada_bwd(c_all, dmod_cols)[None]

    grads = dict(
        w_ada=g_w_ada, b_ada=g_b_ada, g_attn=rep["g_attn"][None], w_in=g_w_in, na_rpb=g_na_rpb,
        sw_sink=rep["sw_sink"][None, :SW_HEADS], g_na_out=rep["g_na_out"][None], g_sw_out=rep["g_sw_out"][None],
        w_out=g_w_out[None], g_ffn=rep["g_ffn"][None], w_up=g_w_up, conv_w=g_conv_w, conv_b=g_conv_b,
        w_down=g_w_down[None], g_final=rep["g_final"])
    weights = dict(w_ada=w_ada, b_ada=b_ada, g_attn=g_attn, w_in=w_in, na_rpb=na_rpb, sw_sink=sw_sink, g_na_out=g_na_out,
                   g_sw_out=g_sw_out, w_out=w_out, g_ffn=g_ffn, w_up=w_up, conv_w=conv_w, conv_b=conv_b, w_down=w_down, g_final=g_final)
    ms = dict(w_ada=m_w_ada, b_ada=m_b_ada, g_attn=m_g_attn, w_in=m_w_in, na_rpb=m_na_rpb, sw_sink=m_sw_sink, g_na_out=m_g_na_out,
              g_sw_out=m_g_sw_out, w_out=m_w_out, g_ffn=m_g_ffn, w_up=m_w_up, conv_w=m_conv_w, conv_b=m_conv_b, w_down=m_w_down, g_final=m_g_final)
    vs = dict(w_ada=v_w_ada, b_ada=v_b_ada, g_attn=v_g_attn, w_in=v_w_in, na_rpb=v_na_rpb, sw_sink=v_sw_sink, g_na_out=v_g_na_out,
              g_sw_out=v_g_sw_out, w_out=v_w_out, g_ffn=v_g_ffn, w_up=v_w_up, conv_w=v_conv_w, conv_b=v_conv_b, w_down=v_w_down, g_final=v_g_final)
    names = list(weights)
    deltas, new_m, new_v = {}, {}, {}
    for nm in names:
        shp = weights[nm].shape
        if nm in tr:
            r = lambda t: jnp.transpose(t[0])
            back = lambda t: jnp.transpose(t)[None]
            g2 = grads[nm]
        else:
            two_d = (shp[-2], shp[-1]) if len(shp) >= 3 and nm != "na_rpb" else (1, int(np.prod(shp)))
            r = lambda t: t.reshape(two_d)
            back = lambda t: t.reshape(shp)
            g2 = r(grads[nm])
        d_, m_, v_ = _adamw(r(weights[nm]), g2, r(ms[nm]), r(vs[nm]), "adamw_" + nm)
        deltas[nm], new_m[nm], new_v[nm], grads[nm] = back(d_), back(m_), back(v_), back(g2)
    return (loss, grad_x, *[grads[n] for n in names], *[deltas[n] for n in names], *[new_m[n] for n in names],
            *[new_v[n] for n in names])
```

```python
import functools

import numpy as np
import jax
import jax.numpy as jnp
from jax import lax
from jax.experimental import pallas as pl
from jax.experimental.pallas import tpu as pltpu

F32, BF16 = jnp.float32, jnp.bfloat16
MESH_ID = pl.DeviceIdType.MESH
N_DEV = 8

HEAD_DIM = 64
NA_HEADS = 8
SW_HEADS = 8
SW_KV_HEADS = 2
SW_GROUP = SW_HEADS // SW_KV_HEADS
NA_WIDTH = NA_HEADS * HEAD_DIM
SW_WIDTH = SW_HEADS * HEAD_DIM
SW_KV_WIDTH = SW_KV_HEADS * HEAD_DIM
ROPE_WIDTH = SW_WIDTH + SW_KV_WIDTH
IN_WIDTH = 3 * NA_WIDTH + SW_WIDTH + 2 * SW_KV_WIDTH
ROPE_LO = 3 * NA_WIDTH
GRID_W = 64
NA_ROWS_MAX = 8
NA_COLS = 16
N_DR = 2 * NA_ROWS_MAX - 1
N_DC = 2 * NA_COLS - 1
SW_WINDOW = 128
SW_BLOCK = 128
ROPE_THETA = 10000.0
EPS = 1e-6
NEG = -1e30
Q_SCALE = HEAD_DIM ** -0.5

ADAM_LR = 0.001
ADAM_B1 = 0.9
ADAM_B2 = 0.999
ADAM_EPS = 1e-08
ADAM_WD = 0.01
ADAM_STEP = 10

TOKEN_TILE = 256
VMEM_LIMIT = 56 * 1024 * 1024

PACK_W = 6144


def _nn(a, b):
    return jnp.dot(a, b, preferred_element_type=F32)


def _nt(a, b):
    return lax.dot_general(a, b, (((1,), (1,)), ((), ())), preferred_element_type=F32)


def _tn(a, b):
    return lax.dot_general(a, b, (((0,), (0,)), ((), ())), preferred_element_type=F32)


def _rms(x):
    r = lax.rsqrt(jnp.mean(x * x, axis=-1, keepdims=True) + EPS)
    return x * r, r


def _rms_bwd(xn, r, gy):
    return r * (gy - xn * jnp.mean(xn * gy, axis=-1, keepdims=True))


def _params(*sem):
    return pltpu.CompilerParams(dimension_semantics=sem, vmem_limit_bytes=VMEM_LIMIT)


def _full(shape):
    n = len(shape)
    return pl.BlockSpec(shape, lambda *_: (0,) * n)


def _mesh_pos():
    return lax.axis_index("x"), lax.axis_index("y"), lax.axis_index("c")


def _all_gather(x, name):
    def body(x_ref, out_ref, send_sems, recv_sems, local_sem):
        x_, y_, c_ = _mesh_pos()
        me, sibling = (x_, y_, c_), (x_, y_, 1 - c_)
        chips = [(1 - x_, y_), (x_, 1 - y_), (1 - x_, 1 - y_)]

        def rows(px, py, pc):
            return out_ref.at[4 * px + 2 * py + pc]

        def copy(k, block, to, src=None):
            return pltpu.make_async_remote_copy(
                src_ref=rows(*block) if src is None else src, dst_ref=rows(*block),
                send_sem=send_sems.at[k], recv_sem=recv_sems.at[k], device_id=to, device_id_type=MESH_ID)

        mine = pltpu.make_async_copy(x_ref, rows(*me), local_sem)
        mine.start()
        first = [copy(0, me, sibling, src=x_ref)]
        first += [copy(1 + j, me, (*chip, c_), src=x_ref) for j, chip in enumerate(chips)]
        for cp in first:
            cp.start()
        passed = [copy(4 + j, (*chip, c_), sibling) for j, chip in enumerate(chips)]
        for j, chip in enumerate(chips):
            copy(1 + j, (*chip, c_), me).wait_recv()
            passed[j].start()
        copy(0, sibling, me).wait_recv()
        for j, chip in enumerate(chips):
            copy(4 + j, (*chip, 1 - c_), me).wait_recv()
        for cp in first + passed:
            cp.wait_send()
        mine.wait()

    return pl.pallas_call(
        body, name=name,
        out_shape=jax.ShapeDtypeStruct((N_DEV,) + x.shape, x.dtype),
        in_specs=[pl.BlockSpec(memory_space=pl.ANY)],
        out_specs=pl.BlockSpec(memory_space=pl.ANY),
        scratch_shapes=[pltpu.SemaphoreType.DMA((7,)), pltpu.SemaphoreType.DMA((7,)), pltpu.SemaphoreType.DMA],
    )(x)


def _row_chunk(r):
    for rc in (128, 64, 32, 16):
        if r % rc == 0:
            return rc
    raise ValueError(f"rows {r} not a multiple of 16")


def _reduce_scatter(g8, name):
    _, R, C = g8.shape
    rc = _row_chunk(R)

    def body(g_ref, out_ref, recva, sendb, recvb, sa, ra, sb, rb):
        x_, y_, c_ = _mesh_pos()
        sibling = (x_, y_, 1 - c_)
        copies_a = []
        for k in range(4):
            cp = pltpu.make_async_remote_copy(
                src_ref=g_ref.at[2 * k + (1 - c_)], dst_ref=recva.at[k],
                send_sem=sa.at[k], recv_sem=ra.at[k], device_id=sibling, device_id_type=MESH_ID)
            cp.start()
            copies_a.append(cp)
        for cp in copies_a:
            cp.wait_recv()

        def chip_sum(k, rows):
            return g_ref[2 * k + c_, rows, :].astype(F32) + recva[k, rows, :].astype(F32)

        flips = [(1 - x_, y_), (x_, 1 - y_), (1 - x_, 1 - y_)]
        copies_b = []
        for j, (tx, ty) in enumerate(flips):
            kt = 2 * tx + ty

            def fill(i, carry, j=j, kt=kt):
                rows = pl.ds(pl.multiple_of(i * rc, rc), rc)
                sendb[j, rows, :] = chip_sum(kt, rows).astype(BF16)
                return carry

            lax.fori_loop(0, R // rc, fill, 0)
            cp = pltpu.make_async_remote_copy(
                src_ref=sendb.at[j], dst_ref=recvb.at[j],
                send_sem=sb.at[j], recv_sem=rb.at[j], device_id=(tx, ty, c_), device_id_type=MESH_ID)
            cp.start()
            copies_b.append(cp)
        for cp in copies_b:
            cp.wait_recv()
        kme = 2 * x_ + y_

        def total(i, carry):
            rows = pl.ds(pl.multiple_of(i * rc, rc), rc)
            acc = chip_sum(kme, rows)
            for j in range(3):
                acc = acc + recvb[j, rows, :].astype(F32)
            out_ref[rows, :] = acc
            return carry

        lax.fori_loop(0, R // rc, total, 0)
        for cp in copies_a + copies_b:
            cp.wait_send()

    vm = pl.BlockSpec(memory_space=pltpu.VMEM)
    return pl.pallas_call(
        body, name=name,
        out_shape=jax.ShapeDtypeStruct((R, C), F32),
        in_specs=[vm], out_specs=vm,
        scratch_shapes=[pltpu.VMEM((4, R, C), BF16), pltpu.VMEM((3, R, C), BF16), pltpu.VMEM((3, R, C), BF16),
                        pltpu.SemaphoreType.DMA((4,)), pltpu.SemaphoreType.DMA((4,)),
                        pltpu.SemaphoreType.DMA((3,)), pltpu.SemaphoreType.DMA((3,))],
        compiler_params=pltpu.CompilerParams(vmem_limit_bytes=VMEM_LIMIT),
    )(g8)


class _Task:
    def __init__(self, inputs, out_shapes, sems, start, finish, mid=None, mid_step=None):
        self.inputs, self.out_shapes, self.sems = list(inputs), list(out_shapes), list(sems)
        self.start, self.finish, self.mid, self.mid_step = start, finish, mid, mid_step


def _hosted_call(body, name, grid, in_specs, out_specs, out_shape, operands, tasks, scratch_shapes=()):
    n_in, n_out, n_scr = len(in_specs), len(out_specs), len(scratch_shapes)
    t_in = [len(t.inputs) for t in tasks]
    t_out = [len(t.out_shapes) for t in tasks]
    t_sem = [len(t.sems) for t in tasks]
    n_steps = int(np.prod(grid))

    def wrapped(*refs):
        ins, rest = refs[:n_in], refs[n_in:]
        task_ins, rest = rest[:sum(t_in)], rest[sum(t_in):]
        outs, rest = rest[:n_out], rest[n_out:]
        task_outs, rest = rest[:sum(t_out)], rest[sum(t_out):]
        scr, task_sems = rest[:n_scr], rest[n_scr:]
        step = pl.program_id(0)
        for ax in range(1, len(grid)):
            step = step * grid[ax] + pl.program_id(ax)
        parts = []
        oi = oo = os_ = 0
        for t, a, b, c in zip(tasks, t_in, t_out, t_sem):
            parts.append((t, task_ins[oi:oi + a], task_outs[oo:oo + b], task_sems[os_:os_ + c]))
            oi, oo, os_ = oi + a, oo + b, os_ + c
        for t, ti, to, ts in parts:
            pl.when(step == 0)(functools.partial(t.start, ti, to, ts))
            if t.mid is not None:
                pl.when(step == t.mid_step)(functools.partial(t.mid, ti, to, ts))
        body(*ins, *outs, *scr)
        for t, ti, to, ts in parts:
            pl.when(step == n_steps - 1)(functools.partial(t.finish, ti, to, ts))

    hbm = pl.BlockSpec(memory_space=pl.ANY)
    res = pl.pallas_call(
        wrapped, name=name, grid=grid,
        in_specs=list(in_specs) + [hbm] * sum(t_in),
        out_specs=list(out_specs) + [hbm] * sum(t_out),
        out_shape=list(out_shape) + [s for t in tasks for s in t.out_shapes],
        scratch_shapes=list(scratch_shapes) + [s for t in tasks for s in t.sems],
        compiler_params=_params(*(["arbitrary"] * len(grid))),
    )(*operands, *[a for t in tasks for a in t.inputs])
    own, extra = res[:n_out], res[n_out:]
    per_task, o = [], 0
    for b in t_out:
        per_task.append(extra[o:o + b])
        o += b
    return own, per_task


def _gather_task(shard, mid_step):
    def parts(ins, outs, sems):
        (x_ref,), (out_ref,), (send_sems, recv_sems, local_sem) = ins, outs, sems
        x_, y_, c_ = _mesh_pos()
        me, sibling = (x_, y_, c_), (x_, y_, 1 - c_)
        chips = [(1 - x_, y_), (x_, 1 - y_), (1 - x_, 1 - y_)]

        def rows(px, py, pc):
            return out_ref.at[4 * px + 2 * py + pc]

        def copy(k, block, to, src=None):
            return pltpu.make_async_remote_copy(
                src_ref=rows(*block) if src is None else src, dst_ref=rows(*block),
                send_sem=send_sems.at[k], recv_sem=recv_sems.at[k], device_id=to, device_id_type=MESH_ID)

        return dict(
            mine=lambda: pltpu.make_async_copy(x_ref, rows(*me), local_sem),
            first=lambda: [copy(0, me, sibling, src=x_ref)] + [copy(1 + j, me, (*chip, c_), src=x_ref) for j, chip in enumerate(chips)],
            passed=lambda: [copy(4 + j, (*chip, c_), sibling) for j, chip in enumerate(chips)],
            landed=lambda: [copy(1 + j, (*chip, c_), me) for j, chip in enumerate(chips)],
            last=lambda: [copy(0, sibling, me)] + [copy(4 + j, (*chip, 1 - c_), me) for j, chip in enumerate(chips)])

    def start(ins, outs, sems):
        p = parts(ins, outs, sems)
        p["mine"]().start()
        for cp in p["first"]():
            cp.start()

    def mid(ins, outs, sems):
        p = parts(ins, outs, sems)
        for cp, fw in zip(p["landed"](), p["passed"]()):
            cp.wait_recv()
            fw.start()

    def finish(ins, outs, sems):
        p = parts(ins, outs, sems)
        for cp in p["last"]():
            cp.wait_recv()
        for cp in p["first"]() + p["passed"]():
            cp.wait_send()
        p["mine"]().wait()

    return _Task([shard], [jax.ShapeDtypeStruct((N_DEV,) + shard.shape, shard.dtype)],
                 [pltpu.SemaphoreType.DMA((7,)), pltpu.SemaphoreType.DMA((7,)), pltpu.SemaphoreType.DMA],
                 start, finish, mid, mid_step)


def _swap_task(g8):
    _, R, C = g8.shape

    def copies(ins, outs, sems):
        (g_ref,), (recv_ref,), (ss, rs) = ins, outs, sems
        x_, y_, c_ = _mesh_pos()
        return [pltpu.make_async_remote_copy(src_ref=g_ref.at[2 * k + (1 - c_)], dst_ref=recv_ref.at[k], send_sem=ss.at[k],
                                             recv_sem=rs.at[k], device_id=(x_, y_, 1 - c_), device_id_type=MESH_ID)
                for k in range(4)]

    def start(ins, outs, sems):
        for cp in copies(ins, outs, sems):
            cp.start()

    def finish(ins, outs, sems):
        cps = copies(ins, outs, sems)
        for cp in cps:
            cp.wait_recv()
        for cp in cps:
            cp.wait_send()

    return _Task([g8], [jax.ShapeDtypeStruct((4, R, C), g8.dtype)],
                 [pltpu.SemaphoreType.DMA((4,)), pltpu.SemaphoreType.DMA((4,))], start, finish)


def _chip_sums(g8, recva):
    _, R, C = g8.shape
    rc = _row_chunk(R)

    def body(g_ref, a_ref, send_ref, own_ref):
        x_, y_, c_ = _mesh_pos()
        chips = [(1 - x_, y_), (x_, 1 - y_), (1 - x_, 1 - y_), (x_, y_)]

        def chunk(i, carry):
            rows = pl.ds(pl.multiple_of(i * rc, rc), rc)
            for j, (tx, ty) in enumerate(chips):
                k = 2 * tx + ty
                s = g_ref[2 * k + c_, rows, :].astype(F32) + a_ref[k, rows, :].astype(F32)
                if j < 3:
                    send_ref[j, rows, :] = s.astype(BF16)
                else:
                    own_ref[rows, :] = s
            return carry

        lax.fori_loop(0, R // rc, chunk, 0)

    return pl.pallas_call(body, name="chip_sums", out_shape=[jax.ShapeDtypeStruct((3, R, C), BF16), jax.ShapeDtypeStruct((R, C), F32)],
                          compiler_params=pltpu.CompilerParams(vmem_limit_bytes=VMEM_LIMIT))(g8, recva)


def _exchange_task(sendb):
    def copies(ins, outs, sems):
        (s_ref,), (recv_ref,), (ss, rs) = ins, outs, sems
        x_, y_, c_ = _mesh_pos()
        flips = [(1 - x_, y_), (x_, 1 - y_), (1 - x_, 1 - y_)]
        return [pltpu.make_async_remote_copy(src_ref=s_ref.at[j], dst_ref=recv_ref.at[j], send_sem=ss.at[j], recv_sem=rs.at[j],
                                             device_id=(tx, ty, c_), device_id_type=MESH_ID) for j, (tx, ty) in enumerate(flips)]

    def start(ins, outs, sems):
        for cp in copies(ins, outs, sems):
            cp.start()

    def finish(ins, outs, sems):
        cps = copies(ins, outs, sems)
        for cp in cps:
            cp.wait_recv()
        for cp in cps:
            cp.wait_send()

    return _Task([sendb], [jax.ShapeDtypeStruct(sendb.shape, sendb.dtype)],
                 [pltpu.SemaphoreType.DMA((3,)), pltpu.SemaphoreType.DMA((3,))], start, finish)


def _silu(v):
    return v * (1.0 / (1.0 + jnp.exp(-v)))


def _ada_fwd(c_all, w_ada_l, b_ada_l):
    def body(c_ref, w_ref, b_ref, o_ref):
        o_ref[...] = jnp.dot(_silu(c_ref[...]), w_ref[...], precision=lax.Precision.HIGHEST,
                             preferred_element_type=F32) + b_ref[...]
    return pl.pallas_call(body, name="ada_fwd", out_shape=jax.ShapeDtypeStruct((c_all.shape[0], w_ada_l.shape[1]), F32),
                          compiler_params=pltpu.CompilerParams(vmem_limit_bytes=VMEM_LIMIT))(c_all, w_ada_l, b_ada_l)


def _ada_bwd(c_all, dmod_cols):
    def body(c_ref, d_ref, o_ref):
        o_ref[...] = lax.dot_general(_silu(c_ref[...]), d_ref[...], (((0,), (0,)), ((), ())),
                                     precision=lax.Precision.HIGHEST, preferred_element_type=F32)
    return pl.pallas_call(body, name="ada_bwd", out_shape=jax.ShapeDtypeStruct((c_all.shape[1], dmod_cols.shape[1]), F32),
                          compiler_params=pltpu.CompilerParams(vmem_limit_bytes=VMEM_LIMIT))(c_all, dmod_cols)


def _toeplitz_onehot():
    k = np.arange(GRID_W)[:, None]
    q = np.arange(GRID_W)[None, :]
    dc = np.clip(k - q + NA_COLS - 1, 0, N_DC - 1).reshape(-1)
    e = np.zeros((128, GRID_W * GRID_W), np.float32)
    e[dc, np.arange(GRID_W * GRID_W)] = 1.0
    return jnp.asarray(e)


def _rpb_expand(rpb2d, onehot):
    def body(r_ref, e_ref, o_ref):
        o_ref[...] = jnp.dot(r_ref[...], e_ref[...], precision=lax.Precision.HIGHEST, preferred_element_type=F32)
    return pl.pallas_call(body, name="rpb_expand", out_shape=jax.ShapeDtypeStruct((128, GRID_W * GRID_W), F32),
                          compiler_params=pltpu.CompilerParams(vmem_limit_bytes=VMEM_LIMIT))(rpb2d, onehot)


def _rpb_reduce(dtz2d, onehot):
    def body(d_ref, e_ref, o_ref):
        o_ref[...] = lax.dot_general(d_ref[...], e_ref[...], (((1,), (1,)), ((), ())),
                                     precision=lax.Precision.HIGHEST, preferred_element_type=F32)
    return pl.pallas_call(body, name="rpb_reduce", out_shape=jax.ShapeDtypeStruct((128, 128), F32),
                          compiler_params=pltpu.CompilerParams(vmem_limit_bytes=VMEM_LIMIT))(dtz2d, onehot)


NA_PAIRS = NA_HEADS // 2


def _na_bias_table(na_rpb, onehot):
    rpb2d = jnp.pad(na_rpb.reshape(NA_HEADS * N_DR, N_DC), ((0, 128 - NA_HEADS * N_DR), (0, 128 - N_DC)))
    tz = _rpb_expand(rpb2d, onehot)[:NA_HEADS * N_DR].reshape(NA_PAIRS, 2, N_DR, GRID_W, GRID_W)
    col = np.arange(GRID_W)
    cs = np.clip(col - NA_COLS // 2, 0, GRID_W - NA_COLS)
    ok_kq = ((col[None, :] >= cs[:, None]) & (col[None, :] < cs[:, None] + NA_COLS)).T
    tz = jnp.where(jnp.asarray(ok_kq)[None, None, None], tz, NEG)
    return jnp.transpose(tz, (0, 2, 3, 1, 4)).reshape(NA_PAIRS, N_DR * GRID_W, 128)


def _na_bias_grad(db, onehot):
    fold = db.reshape(NA_PAIRS, N_DR, GRID_W, 2, GRID_W)
    dtz2d = jnp.transpose(fold, (0, 3, 1, 2, 4)).reshape(NA_HEADS * N_DR, GRID_W * GRID_W)
    dtz2d = jnp.pad(dtz2d, ((0, 128 - NA_HEADS * N_DR), (0, 0)))
    return _rpb_reduce(dtz2d, onehot)[:NA_HEADS * N_DR, :N_DC].reshape(-1)


def _rope_tables(S):
    half = HEAD_DIM // 2
    inv = ROPE_THETA ** (-jnp.arange(half, dtype=F32) / half)
    ang = jnp.arange(S).astype(F32)[:, None] * inv[None, :]
    cos, sin = jnp.cos(ang), jnp.sin(ang)
    n = ROPE_WIDTH // HEAD_DIM
    return jnp.tile(jnp.concatenate([cos, cos], axis=1), (1, n)), jnp.tile(jnp.concatenate([-sin, sin], axis=1), (1, n))


def _rot_half(t):
    w = t.shape[1]
    lane = lax.broadcasted_iota(jnp.int32, t.shape, 1)
    return jnp.where((lane % HEAD_DIM) < HEAD_DIM // 2, pltpu.roll(t, w - HEAD_DIM // 2, axis=1),
                     pltpu.roll(t, HEAD_DIM // 2, axis=1))


def _tok_spec(w):
    return pl.BlockSpec((TOKEN_TILE, w), lambda i: (i, 0))


def _mod_spec(tps, d):
    return pl.BlockSpec((1, 6, d), lambda i: (i // tps, 0, 0))


def _bstat_spec(tps, w):
    return pl.BlockSpec((1, 8, w), lambda i: (i // tps, 0, 0))


def _attn_in(x2d, mod3, g_attn, w_in, cos_t, sin_t, S, tasks=()):
    T, D = x2d.shape
    tps = S // TOKEN_TILE

    def body(x_ref, mod_ref, g_ref, w_ref, cos_ref, sin_ref, h_ref, qkv_ref):
        xn, _ = _rms(x_ref[...])
        h = (xn * g_ref[...]) * (1.0 + mod_ref[0, 1:2, :]) + mod_ref[0, 0:1, :]
        hb = h.astype(BF16)
        h_ref[...] = hb
        proj = _nt(hb, w_ref[...])
        rb = proj[:, ROPE_LO:ROPE_LO + ROPE_WIDTH]
        rb = rb * cos_ref[...] + _rot_half(rb) * sin_ref[...]
        qkv_ref[:, 0:NA_WIDTH] = (proj[:, 0:NA_WIDTH] * Q_SCALE).astype(BF16)
        qkv_ref[:, NA_WIDTH:ROPE_LO] = proj[:, NA_WIDTH:ROPE_LO].astype(BF16)
        qkv_ref[:, ROPE_LO:ROPE_LO + SW_WIDTH] = (rb[:, 0:SW_WIDTH] * Q_SCALE).astype(BF16)
        qkv_ref[:, ROPE_LO + SW_WIDTH:ROPE_LO + ROPE_WIDTH] = rb[:, SW_WIDTH:].astype(BF16)
        qkv_ref[:, ROPE_LO + ROPE_WIDTH:] = proj[:, ROPE_LO + ROPE_WIDTH:].astype(BF16)

    return _hosted_call(
        body, "attn_in", (T // TOKEN_TILE,),
        [_tok_spec(D), _mod_spec(tps, D), _full((1, D)), _full(w_in.shape),
         pl.BlockSpec((TOKEN_TILE, ROPE_WIDTH), lambda i: (i % tps, 0)),
         pl.BlockSpec((TOKEN_TILE, ROPE_WIDTH), lambda i: (i % tps, 0))],
        [_tok_spec(D), _tok_spec(IN_WIDTH)],
        [jax.ShapeDtypeStruct((T, D), BF16), jax.ShapeDtypeStruct((T, IN_WIDTH), BF16)],
        (x2d, mod3, g_attn, w_in, cos_t, sin_t), tasks)


def _attn_out(oa, ob, x2d, mod3, g_na, g_sw, w_out, S):
    T, D = x2d.shape
    tps = S // TOKEN_TILE

    def body(oa_ref, ob_ref, x_ref, mod_ref, gna_ref, gsw_ref, w_ref, mixin_ref, mix_ref, x1_ref):
        oan, _ = _rms(oa_ref[...])
        obn, _ = _rms(ob_ref[...])
        mixin = jnp.concatenate([oan * gna_ref[...], obn * gsw_ref[...]], axis=1).astype(BF16)
        mixin_ref[...] = mixin
        mix = _nn(mixin, w_ref[...])
        mix_ref[...] = mix
        x1_ref[...] = x_ref[...] + mod_ref[0, 2:3, :] * mix

    return pl.pallas_call(
        body, name="attn_out", grid=(T // TOKEN_TILE,),
        in_specs=[_tok_spec(NA_WIDTH), _tok_spec(SW_WIDTH), _tok_spec(D), _mod_spec(tps, D),
                  _full((1, NA_WIDTH)), _full((1, SW_WIDTH)), _full(w_out.shape)],
        out_specs=[_tok_spec(NA_WIDTH + SW_WIDTH), _tok_spec(D), _tok_spec(D)],
        out_shape=[jax.ShapeDtypeStruct((T, NA_WIDTH + SW_WIDTH), BF16), jax.ShapeDtypeStruct((T, D), F32),
                   jax.ShapeDtypeStruct((T, D), F32)],
        compiler_params=_params("parallel"),
    )(oa, ob, x2d, mod3, g_na, g_sw, w_out)


def _ffn_up(x1, mod3, g_ffn, w_up, S):
    T, D = x1.shape
    F = w_up.shape[0] // 2
    tps = S // TOKEN_TILE

    def body(x1_ref, mod_ref, g_ref, w_ref, h2_ref, val_ref, gt_ref):
        xn, _ = _rms(x1_ref[...])
        h2 = ((xn * g_ref[...]) * (1.0 + mod_ref[0, 4:5, :]) + mod_ref[0, 3:4, :]).astype(BF16)
        h2_ref[...] = h2
        u = _nt(h2, w_ref[...])
        val_ref[...] = u[:, :F].astype(BF16)
        gt_ref[...] = u[:, F:].astype(BF16)

    return pl.pallas_call(
        body, name="ffn_up", grid=(T // TOKEN_TILE,),
        in_specs=[_tok_spec(D), _mod_spec(tps, D), _full((1, D)), _full(w_up.shape)],
        out_specs=[_tok_spec(D), _tok_spec(F), _tok_spec(F)],
        out_shape=[jax.ShapeDtypeStruct((T, D), BF16), jax.ShapeDtypeStruct((T, F), BF16), jax.ShapeDtypeStruct((T, F), BF16)],
        compiler_params=_params("parallel"),
    )(x1, mod3, g_ffn, w_up)


def _halo_specs(T, tps, w):
    per = TOKEN_TILE // 8
    prev = pl.BlockSpec((8, w), lambda i: (jnp.maximum(i * per - 1, 0), 0))
    nxt = pl.BlockSpec((8, w), lambda i: (jnp.minimum((i + 1) * per, T // 8 - 1), 0))
    return prev, nxt


def _seq_shifts(cur, prev_ref, next_ref, tps):
    ti = pl.program_id(0) % tps
    row = lax.broadcasted_iota(jnp.int32, cur.shape, 0)
    before = jnp.where(ti > 0, prev_ref[7:8, :].astype(F32), 0.0)
    after = jnp.where(ti < tps - 1, next_ref[0:1, :].astype(F32), 0.0)
    down = jnp.where(row == 0, before, pltpu.roll(cur, 1, axis=0))
    up = jnp.where(row == cur.shape[0] - 1, after, pltpu.roll(cur, cur.shape[0] - 1, axis=0))
    return down, up


def _conv_gate(gt_ref, prev_ref, next_ref, cw_ref, cb_ref, tps):
    g = gt_ref[...].astype(F32)
    gprev, gnext = _seq_shifts(g, prev_ref, next_ref, tps)
    gc = gprev * cw_ref[0:1, :] + g * cw_ref[1:2, :] + gnext * cw_ref[2:3, :] + cb_ref[...]
    sig = 1.0 / (1.0 + jnp.exp(-gc))
    return g, gprev, gnext, gc, sig


def _ffn_down(gt, val, conv_w, conv_b, w_down, x1, mod3, g_final, target, B, S):
    T, D = x1.shape
    F = gt.shape[1]
    tps = S // TOKEN_TILE
    prev, nxt = _halo_specs(T, tps, F)

    def body(gt_ref, prev_ref, next_ref, val_ref, cw_ref, cb_ref, w_ref, x1_ref, mod_ref, gf_ref, tgt_ref,
             a_ref, dx2_ref, df_ref, gstat_ref, bstat_ref):
        i = pl.program_id(0)
        _, _, _, gc, sig = _conv_gate(gt_ref, prev_ref, next_ref, cw_ref, cb_ref, tps)
        a = (gc * sig * val_ref[...].astype(F32)).astype(BF16)
        a_ref[...] = a
        f = _nn(a, w_ref[...])
        gate = mod_ref[0, 5:6, :]
        x2 = x1_ref[...] + gate * f
        xn, r = _rms(x2)
        err = xn * gf_ref[...] - tgt_ref[...]
        dy = err * (1.0 / D)
        dx2 = _rms_bwd(xn, r, dy * gf_ref[...])
        dx2_ref[...] = dx2
        df_ref[...] = (gate * dx2).astype(BF16)

        @pl.when(i == 0)
        def _():
            gstat_ref[...] = jnp.zeros_like(gstat_ref)

        @pl.when(i % tps == 0)
        def _():
            bstat_ref[...] = jnp.zeros_like(bstat_ref)

        gstat_ref[0:1, :] += jnp.sum(dy * xn, axis=0, keepdims=True)
        tile_loss = jnp.sum(jnp.sum(err * err, axis=1, keepdims=True), axis=0, keepdims=True) * (0.5 / D)
        gstat_ref[1:2, :] += jnp.broadcast_to(tile_loss, (1, D))
        bstat_ref[0, 0:1, :] += jnp.sum(dx2 * f, axis=0, keepdims=True)

    return pl.pallas_call(
        body, name="ffn_down", grid=(T // TOKEN_TILE,),
        in_specs=[_tok_spec(F), prev, nxt, _tok_spec(F), _full(conv_w.shape), _full((1, F)), _full(w_down.shape),
                  _tok_spec(D), _mod_spec(tps, D), _full((1, D)), _tok_spec(D)],
        out_specs=[_tok_spec(F), _tok_spec(D), _tok_spec(D), _full((8, D)), _bstat_spec(tps, D)],
        out_shape=[jax.ShapeDtypeStruct((T, F), BF16), jax.ShapeDtypeStruct((T, D), F32), jax.ShapeDtypeStruct((T, D), BF16),
                   jax.ShapeDtypeStruct((8, D), F32), jax.ShapeDtypeStruct((B, 8, D), F32)],
        compiler_params=_params("arbitrary"),
    )(gt, gt, gt, val, conv_w, conv_b, w_down, x1, mod3, g_final, target)


def _ffn_down_bwd(df, w_down, gt, val, conv_w, conv_b, S, tasks=()):
    T, D = df.shape
    F = gt.shape[1]
    tps = S // TOKEN_TILE
    prev, nxt = _halo_specs(T, tps, F)

    def body(df_ref, w_ref, gt_ref, prev_ref, next_ref, val_ref, cw_ref, cb_ref, dval_ref, dgc_ref, cstat_ref):
        g, gprev, gnext, gc, sig = _conv_gate(gt_ref, prev_ref, next_ref, cw_ref, cb_ref, tps)
        da = _nt(df_ref[...], w_ref[...])
        dval_ref[...] = (da * (gc * sig)).astype(BF16)
        dgc = da * val_ref[...].astype(F32) * (sig * (1.0 + gc * (1.0 - sig)))
        dgc_ref[...] = dgc.astype(BF16)

        @pl.when(pl.program_id(0) == 0)
        def _():
            cstat_ref[...] = jnp.zeros_like(cstat_ref)

        cstat_ref[0:1, :] += jnp.sum(dgc, axis=0, keepdims=True)
        cstat_ref[1:2, :] += jnp.sum(dgc * gprev, axis=0, keepdims=True)
        cstat_ref[2:3, :] += jnp.sum(dgc * g, axis=0, keepdims=True)
        cstat_ref[3:4, :] += jnp.sum(dgc * gnext, axis=0, keepdims=True)

    return _hosted_call(
        body, "ffn_down_bwd", (T // TOKEN_TILE,),
        [_tok_spec(D), _full(w_down.shape), _tok_spec(F), prev, nxt, _tok_spec(F), _full(conv_w.shape), _full((1, F))],
        [_tok_spec(F), _tok_spec(F), _full((8, F))],
        [jax.ShapeDtypeStruct((T, F), BF16), jax.ShapeDtypeStruct((T, F), BF16), jax.ShapeDtypeStruct((8, F), F32)],
        (df, w_down, gt, gt, gt, val, conv_w, conv_b), tasks)


def _ffn_up_bwd(dgc, dval, conv_w, w_up, x1, mod3, g_ffn, dx2, mix, B, S, tasks=()):
    T, D = x1.shape
    F = dgc.shape[1]
    tps = S // TOKEN_TILE
    prev, nxt = _halo_specs(T, tps, F)

    def body(dgc_ref, prev_ref, next_ref, dval_ref, cw_ref, w_ref, x1_ref, mod_ref, g_ref, dx2_ref, mix_ref,
             du_ref, dx1_ref, dmix_ref, gstat_ref, bstat_ref):
        i = pl.program_id(0)
        d = dgc_ref[...].astype(F32)
        dprev, dnext = _seq_shifts(d, prev_ref, next_ref, tps)
        dgt = dnext * cw_ref[0:1, :] + d * cw_ref[1:2, :] + dprev * cw_ref[2:3, :]
        du = jnp.concatenate([dval_ref[...], dgt.astype(BF16)], axis=1)
        du_ref[...] = du
        dh2 = _nn(du, w_ref[...])
        xn, r = _rms(x1_ref[...])
        scale1 = 1.0 + mod_ref[0, 4:5, :]
        xg = xn * g_ref[...]
        dx1 = dx2_ref[...] + _rms_bwd(xn, r, dh2 * g_ref[...] * scale1)
        dx1_ref[...] = dx1
        dmix_ref[...] = (mod_ref[0, 2:3, :] * dx1).astype(BF16)

        @pl.when(i == 0)
        def _():
            gstat_ref[...] = jnp.zeros_like(gstat_ref)

        @pl.when(i % tps == 0)
        def _():
            bstat_ref[...] = jnp.zeros_like(bstat_ref)

        gstat_ref[0:1, :] += jnp.sum(dh2 * scale1 * xn, axis=0, keepdims=True)
        bstat_ref[0, 0:1, :] += jnp.sum(dh2, axis=0, keepdims=True)
        bstat_ref[0, 1:2, :] += jnp.sum(dh2 * xg, axis=0, keepdims=True)
        bstat_ref[0, 2:3, :] += jnp.sum(dx1 * mix_ref[...], axis=0, keepdims=True)

    return _hosted_call(
        body, "ffn_up_bwd", (T // TOKEN_TILE,),
        [_tok_spec(F), prev, nxt, _tok_spec(F), _full(conv_w.shape), _full(w_up.shape), _tok_spec(D),
         _mod_spec(tps, D), _full((1, D)), _tok_spec(D), _tok_spec(D)],
        [_tok_spec(2 * F), _tok_spec(D), _tok_spec(D), _full((8, D)), _bstat_spec(tps, D)],
        [jax.ShapeDtypeStruct((T, 2 * F), BF16), jax.ShapeDtypeStruct((T, D), F32), jax.ShapeDtypeStruct((T, D), BF16),
         jax.ShapeDtypeStruct((8, D), F32), jax.ShapeDtypeStruct((B, 8, D), F32)],
        (dgc, dgc, dgc, dval, conv_w, w_up, x1, mod3, g_ffn, dx2, mix), tasks)


def _attn_out_bwd(dmix, w_out, oa, ob, g_na, g_sw, tasks=()):
    T, D = dmix.shape

    def body(dmix_ref, w_ref, oa_ref, ob_ref, gna_ref, gsw_ref, doa_ref, dob_ref, gstat_ref):
        dmixin = _nt(dmix_ref[...], w_ref[...])

        @pl.when(pl.program_id(0) == 0)
        def _():
            gstat_ref[...] = jnp.zeros_like(gstat_ref)

        for k, (o_ref, g_ref, do_ref) in enumerate(((oa_ref, gna_ref, doa_ref), (ob_ref, gsw_ref, dob_ref))):
            dn = dmixin[:, k * NA_WIDTH:(k + 1) * NA_WIDTH]
            on, r = _rms(o_ref[...])
            gstat_ref[k:k + 1, :] += jnp.sum(dn * on, axis=0, keepdims=True)
            do_ref[...] = _rms_bwd(on, r, dn * g_ref[...]).astype(BF16)

    hs = jax.ShapeDtypeStruct((T, NA_WIDTH), BF16)
    return _hosted_call(
        body, "attn_out_bwd", (T // TOKEN_TILE,),
        [_tok_spec(D), _full(w_out.shape), _tok_spec(NA_WIDTH), _tok_spec(SW_WIDTH), _full((1, NA_WIDTH)), _full((1, SW_WIDTH))],
        [_tok_spec(NA_WIDTH), _tok_spec(SW_WIDTH), _full((8, NA_WIDTH))],
        [hs, hs, jax.ShapeDtypeStruct((8, NA_WIDTH), F32)],
        (dmix, w_out, oa, ob, g_na, g_sw), tasks)


def _attn_in_bwd(dqa, dka, dva, dqb, dkb, dvb, cos_t, sin_t, w_in, x2d, mod3, g_attn, dx1, B, S):
    T, D = x2d.shape
    tps = S // TOKEN_TILE

    def body(dqa_ref, dka_ref, dva_ref, dqb_ref, dkb_ref, dvb_ref, cos_ref, sin_ref, w_ref, x_ref, mod_ref, g_ref, dx1_ref,
             gx_ref, dproj_ref, gstat_ref, bstat_ref):
        i = pl.program_id(0)
        drb = jnp.concatenate([dqb_ref[...] * Q_SCALE, dkb_ref[...]], axis=1)
        drb = drb * cos_ref[...] + _rot_half(drb * sin_ref[...])
        dproj = jnp.concatenate([dqa_ref[...] * Q_SCALE, dka_ref[...], dva_ref[...], drb, dvb_ref[...]], axis=1).astype(BF16)
        dproj_ref[...] = dproj
        dh = _nn(dproj, w_ref[...])
        xn, r = _rms(x_ref[...])
        scale1 = 1.0 + mod_ref[0, 1:2, :]
        gx_ref[...] = dx1_ref[...] + _rms_bwd(xn, r, dh * g_ref[...] * scale1)

        @pl.when(i == 0)
        def _():
            gstat_ref[...] = jnp.zeros_like(gstat_ref)

        @pl.when(i % tps == 0)
        def _():
            bstat_ref[...] = jnp.zeros_like(bstat_ref)

        gstat_ref[0:1, :] += jnp.sum(dh * scale1 * xn, axis=0, keepdims=True)
        bstat_ref[0, 0:1, :] += jnp.sum(dh, axis=0, keepdims=True)
        bstat_ref[0, 1:2, :] += jnp.sum(dh * (xn * g_ref[...]), axis=0, keepdims=True)

    rope = pl.BlockSpec((TOKEN_TILE, ROPE_WIDTH), lambda i: (i % tps, 0))
    return pl.pallas_call(
        body, name="attn_in_bwd", grid=(T // TOKEN_TILE,),
        in_specs=[_tok_spec(NA_WIDTH), _tok_spec(NA_WIDTH), _tok_spec(NA_WIDTH), _tok_spec(SW_WIDTH), _tok_spec(SW_KV_WIDTH),
                  _tok_spec(SW_KV_WIDTH), rope, rope, _full(w_in.shape), _tok_spec(D), _mod_spec(tps, D), _full((1, D)), _tok_spec(D)],
        out_specs=[_tok_spec(D), _tok_spec(IN_WIDTH), _full((8, D)), _bstat_spec(tps, D)],
        out_shape=[jax.ShapeDtypeStruct((T, D), F32), jax.ShapeDtypeStruct((T, IN_WIDTH), BF16),
                   jax.ShapeDtypeStruct((8, D), F32), jax.ShapeDtypeStruct((B, 8, D), F32)],
        compiler_params=_params("arbitrary"),
    )(dqa, dka, dva, dqb, dkb, dvb, cos_t, sin_t, w_in, x2d, mod3, g_attn, dx1)


def _matmul_tn(a, b, name, tm=None, tk=512):
    T, M = a.shape
    N = b.shape[1]
    tm = M if tm is None else tm
    nk = T // tk

    def body(a_ref, b_ref, o_ref, acc):
        k = pl.program_id(1)

        @pl.when(k == 0)
        def _():
            acc[...] = jnp.zeros_like(acc)

        acc[...] += _tn(a_ref[...], b_ref[...])

        @pl.when(k == nk - 1)
        def _():
            o_ref[...] = acc[...].astype(BF16)

    return pl.pallas_call(
        body, name=name, grid=(M // tm, nk),
        in_specs=[pl.BlockSpec((tk, tm), lambda i, k: (k, i)), pl.BlockSpec((tk, N), lambda i, k: (k, 0))],
        out_specs=pl.BlockSpec((tm, N), lambda i, k: (i, 0)),
        out_shape=jax.ShapeDtypeStruct((M, N), BF16),
        scratch_shapes=[pltpu.VMEM((tm, N), F32)],
        compiler_params=_params("parallel", "arbitrary"),
    )(a, b)


def _na_geometry(S):
    rows = S // GRID_W
    wr = min(NA_ROWS_MAX, rows)
    return rows, wr


def _na_window(r, rows, wr):
    rs = jnp.clip(r - wr // 2, 0, rows - wr)
    return pl.multiple_of(rs * GRID_W, GRID_W), pl.multiple_of((rs - r + NA_ROWS_MAX - 1) * GRID_W, GRID_W)


NA_STEP_PAIRS = 2
NA_GW = NA_STEP_PAIRS * 128


def _na_specs(S, kw_n, order):
    ng = NA_PAIRS // NA_STEP_PAIRS

    def col(k):
        return pl.BlockSpec((1, S, NA_GW), lambda *ids: (order(*ids)[0], 0, k * ng + order(*ids)[1]))
    bias = pl.BlockSpec((NA_STEP_PAIRS, N_DR * GRID_W, 128), lambda *ids: (order(*ids)[1], 0, 0))
    out = pl.BlockSpec((1, S, NA_GW), lambda *ids: (order(*ids)[0], 0, order(*ids)[1]))
    return col(0), col(1), col(2), bias, out


def _block_diag(t):
    left = lax.broadcasted_iota(jnp.int32, t.shape, 1) < HEAD_DIM
    zero = jnp.zeros_like(t)
    return jnp.concatenate([jnp.where(left, t, zero), jnp.where(left, zero, t)], axis=0)


def _diag_blocks(res):
    left = lax.broadcasted_iota(jnp.int32, (HEAD_DIM, 128), 1) < HEAD_DIM
    return jnp.where(left, res[:HEAD_DIM], res[HEAD_DIM:])


def _col_softmax(st):
    e = jnp.exp(st - jnp.max(st, axis=0, keepdims=True))
    return e * (1.0 / jnp.sum(e, axis=0, keepdims=True))


def _na_fwd(qkv, bias, tasks=()):
    B, S, _ = qkv.shape
    rows, wr = _na_geometry(S)
    kw_n = wr * GRID_W

    def body(q_ref, k_ref, v_ref, b_ref, o_ref):
        def step(r, carry):
            start, boff = _na_window(r, rows, wr)
            qrows = pl.ds(pl.multiple_of(r * GRID_W, GRID_W), GRID_W)
            krows = pl.ds(start, kw_n)
            brows = pl.ds(boff, kw_n)
            lanes = [pl.ds(p * 128, 128) for p in range(NA_STEP_PAIRS)]
            st = [_nt(k_ref[0, krows, ln], _block_diag(q_ref[0, qrows, ln])) for ln in lanes]
            pn = [_col_softmax(st[p] + b_ref[p, brows, :]).astype(BF16) for p in range(NA_STEP_PAIRS)]
            outs = [_diag_blocks(_tn(pn[p], v_ref[0, krows, lanes[p]])) for p in range(NA_STEP_PAIRS)]
            o_ref[0, qrows, :] = jnp.concatenate(outs, axis=1)
            return carry

        lax.fori_loop(0, rows, step, 0)

    q, k, v, bs, out = _na_specs(S, kw_n, lambda b, g: (b, g))
    return _hosted_call(body, "na_fwd", (B, NA_PAIRS // NA_STEP_PAIRS), [q, k, v, bs], [out],
                        [jax.ShapeDtypeStruct((B, S, NA_WIDTH), F32)], (qkv, qkv, qkv, bias), tasks)


def _na_bwd(qkv, bias, doa, tasks=()):
    B, S, _ = qkv.shape
    rows, wr = _na_geometry(S)
    kw_n = wr * GRID_W

    def body(q_ref, k_ref, v_ref, b_ref, do_ref, dq_ref, dk_ref, dv_ref, db_ref):
        @pl.when(pl.program_id(1) == 0)
        def _():
            db_ref[...] = jnp.zeros_like(db_ref)

        dk_ref[...] = jnp.zeros_like(dk_ref)
        dv_ref[...] = jnp.zeros_like(dv_ref)

        def step(r, carry):
            start, boff = _na_window(r, rows, wr)
            qrows = pl.ds(pl.multiple_of(r * GRID_W, GRID_W), GRID_W)
            krows = pl.ds(start, kw_n)
            brows = pl.ds(boff, kw_n)
            pairs = range(NA_STEP_PAIRS)
            lanes = [pl.ds(p * 128, 128) for p in pairs]
            kp = [k_ref[0, krows, ln] for ln in lanes]
            qbd = [_block_diag(q_ref[0, qrows, ln]) for ln in lanes]
            dobd = [_block_diag(do_ref[0, qrows, ln]) for ln in lanes]
            st = [_nt(kp[p], qbd[p]) for p in pairs]
            dpt = [_nt(v_ref[0, krows, lanes[p]], dobd[p]) for p in pairs]
            pn = [_col_softmax(st[p] + b_ref[p, brows, :]) for p in pairs]
            dst = [pn[p] * (dpt[p] - jnp.sum(pn[p] * dpt[p], axis=0, keepdims=True)) for p in pairs]
            dsb = [d.astype(BF16) for d in dst]
            dq_ref[0, qrows, :] = jnp.concatenate([_diag_blocks(_tn(dsb[p], kp[p])) for p in pairs], axis=1)
            dk_ref[0, krows, :] += jnp.concatenate([_nn(dsb[p], qbd[p]) for p in pairs], axis=1)
            dv_ref[0, krows, :] += jnp.concatenate([_nn(pn[p].astype(BF16), dobd[p]) for p in pairs], axis=1)
            for p in pairs:
                db_ref[p, brows, :] += dst[p]
            return carry

        lax.fori_loop(0, rows, step, 0)

    q, k, v, bs, out = _na_specs(S, kw_n, lambda g, b: (b, g))
    hs = jax.ShapeDtypeStruct((B, S, NA_WIDTH), F32)
    return _hosted_call(body, "na_bwd", (NA_PAIRS // NA_STEP_PAIRS, B), [q, k, v, bs, out], [out, out, out, bs],
                        [hs, hs, hs, jax.ShapeDtypeStruct((NA_PAIRS, N_DR * GRID_W, 128), F32)], (qkv, qkv, qkv, bias, doa), tasks)


SW_PAIRS = SW_HEADS // 2


def _sw_band(n, S):
    kw_n = 3 * SW_BLOCK
    start = pl.multiple_of(jnp.clip(n * SW_BLOCK - SW_BLOCK, 0, S - kw_n), SW_BLOCK)
    kpos = start + lax.broadcasted_iota(jnp.int32, (kw_n, SW_BLOCK), 0)
    qpos = n * SW_BLOCK + lax.broadcasted_iota(jnp.int32, (kw_n, SW_BLOCK), 1)
    return start, jnp.abs(qpos - kpos) <= SW_WINDOW


def _kv_halves(t):
    left = lax.broadcasted_iota(jnp.int32, t.shape, 1) < HEAD_DIM
    swapped = pltpu.roll(t, HEAD_DIM, axis=1)
    zero = jnp.zeros_like(t)
    return {(0, 0): jnp.where(left, t, zero), (0, 1): jnp.where(left, zero, swapped),
            (1, 0): jnp.where(left, swapped, zero), (1, 1): jnp.where(left, zero, t)}


def _sw_probs(st, ok, sk):
    st = jnp.where(ok, st, NEG)
    m = jnp.maximum(jnp.max(st, axis=0, keepdims=True), sk)
    e = jnp.exp(st - m)
    esk = jnp.exp(sk - m)
    inv = 1.0 / (jnp.sum(e, axis=0, keepdims=True) + esk)
    return e * inv, esk * inv


def _sw_specs(S):
    q = pl.BlockSpec((1, S, SW_WIDTH), lambda b: (b, 0, ROPE_LO // SW_WIDTH))
    k = pl.BlockSpec((1, S, SW_KV_WIDTH), lambda b: (b, 0, (ROPE_LO + SW_WIDTH) // SW_KV_WIDTH))
    v = pl.BlockSpec((1, S, SW_KV_WIDTH), lambda b: (b, 0, (ROPE_LO + ROPE_WIDTH) // SW_KV_WIDTH))
    return q, k, v


def _sw_fwd(sink, qkv):
    B, S, _ = qkv.shape
    kw_n = 3 * SW_BLOCK

    def body(sink_ref, q_ref, k_ref, v_ref, o_ref):
        def step(n, carry):
            start, ok = _sw_band(n, S)
            qrows = pl.ds(pl.multiple_of(n * SW_BLOCK, SW_BLOCK), SW_BLOCK)
            krows = pl.ds(start, kw_n)
            kh, vh = _kv_halves(k_ref[0, krows, :]), _kv_halves(v_ref[0, krows, :])
            heads = [(p, e) for p in range(SW_PAIRS) for e in range(2)]
            qp = [q_ref[0, qrows, pl.ds(p * 128, 128)] for p in range(SW_PAIRS)]
            kv_of = lambda p: p // (SW_PAIRS // SW_KV_HEADS)
            st = {(p, e): _nt(kh[(kv_of(p), e)], qp[p]) for p, e in heads}
            pn = {(p, e): _sw_probs(st[(p, e)], ok, sink_ref[2 * p + e])[0].astype(BF16) for p, e in heads}
            outs = [_tn(pn[(p, 0)], vh[(kv_of(p), 0)]) + _tn(pn[(p, 1)], vh[(kv_of(p), 1)]) for p in range(SW_PAIRS)]
            o_ref[0, qrows, :] = jnp.concatenate(outs, axis=1)
            return carry

        lax.fori_loop(0, S // SW_BLOCK, step, 0)

    q, k, v = _sw_specs(S)
    return pl.pallas_call(
        body, name="sw_fwd", grid=(B,),
        in_specs=[pl.BlockSpec(memory_space=pltpu.SMEM), q, k, v],
        out_specs=pl.BlockSpec((1, S, SW_WIDTH), lambda b: (b, 0, 0)), out_shape=jax.ShapeDtypeStruct((B, S, SW_WIDTH), F32),
        compiler_params=_params("parallel"),
    )(sink, qkv, qkv, qkv)


def _sw_bwd(sink, qkv, dob):
    B, S, _ = qkv.shape
    kw_n = 3 * SW_BLOCK

    fold_rows = 256

    def body(sink_ref, q_ref, k_ref, v_ref, do_ref, dq_ref, dk_ref, dv_ref, dsink_ref, dk_acc, dv_acc):
        @pl.when(pl.program_id(0) == 0)
        def _():
            dsink_ref[...] = jnp.zeros_like(dsink_ref)

        dk_acc[...] = jnp.zeros_like(dk_acc)
        dv_acc[...] = jnp.zeros_like(dv_acc)
        ppk = SW_PAIRS // SW_KV_HEADS

        def step(n, carry):
            start, ok = _sw_band(n, S)
            qrows = pl.ds(pl.multiple_of(n * SW_BLOCK, SW_BLOCK), SW_BLOCK)
            krows = pl.ds(start, kw_n)
            kh, vh = _kv_halves(k_ref[0, krows, :]), _kv_halves(v_ref[0, krows, :])
            heads = [(p, e) for p in range(SW_PAIRS) for e in range(2)]
            qp = [q_ref[0, qrows, pl.ds(p * 128, 128)] for p in range(SW_PAIRS)]
            dop = [do_ref[0, qrows, pl.ds(p * 128, 128)] for p in range(SW_PAIRS)]
            st = {(p, e): _nt(kh[(p // ppk, e)], qp[p]) for p, e in heads}
            dpt = {(p, e): _nt(vh[(p // ppk, e)], dop[p]) for p, e in heads}
            pnb, dsb = {}, {}
            for p, e in heads:
                pn, psink = _sw_probs(st[(p, e)], ok, sink_ref[2 * p + e])
                delta = jnp.sum(pn * dpt[(p, e)], axis=0, keepdims=True)
                dsb[(p, e)] = (pn * (dpt[(p, e)] - delta)).astype(BF16)
                pnb[(p, e)] = pn.astype(BF16)
                dsink_ref[2 * p + e:2 * p + e + 1, :] += -(psink * delta)
            dq_ref[0, qrows, :] = jnp.concatenate(
                [_tn(dsb[(p, 0)], kh[(p // ppk, 0)]) + _tn(dsb[(p, 1)], kh[(p // ppk, 1)]) for p in range(SW_PAIRS)], axis=1)
            left = lax.broadcasted_iota(jnp.int32, (kw_n, 128), 1) < HEAD_DIM
            dks, dvs = [], []
            for kv in range(SW_KV_HEADS):
                dk = dv = None
                for p in range(kv * ppk, (kv + 1) * ppk):
                    dk_p = jnp.where(left, _nn(dsb[(p, 0)], qp[p]), _nn(dsb[(p, 1)], qp[p]))
                    dv_p = jnp.where(left, _nn(pnb[(p, 0)], dop[p]), _nn(pnb[(p, 1)], dop[p]))
                    dk = dk_p if dk is None else dk + dk_p
                    dv = dv_p if dv is None else dv + dv_p
                dks.append(dk)
                dvs.append(dv)
            dk_acc[krows, :] += jnp.concatenate(dks, axis=1)
            dv_acc[krows, :] += jnp.concatenate(dvs, axis=1)
            return carry

        lax.fori_loop(0, S // SW_BLOCK, step, 0)

        def fold(i, carry):
            rows = pl.ds(pl.multiple_of(i * fold_rows, fold_rows), fold_rows)
            left = lax.broadcasted_iota(jnp.int32, (fold_rows, 128), 1) < HEAD_DIM
            for acc, out_ref in ((dk_acc, dk_ref), (dv_acc, dv_ref)):
                a, b = acc[rows, 0:128], acc[rows, 128:256]
                out_ref[0, rows, :] = jnp.where(left, a + pltpu.roll(a, HEAD_DIM, axis=1), b + pltpu.roll(b, HEAD_DIM, axis=1))
            return carry

        lax.fori_loop(0, S // fold_rows, fold, 0)

        @pl.when(pl.program_id(0) == B - 1)
        def _():
            dsink_ref[...] = jnp.broadcast_to(jnp.sum(dsink_ref[...], axis=1, keepdims=True), dsink_ref.shape)

    q, k, v = _sw_specs(S)
    qo = pl.BlockSpec((1, S, SW_WIDTH), lambda b: (b, 0, 0))
    ko = pl.BlockSpec((1, S, SW_KV_WIDTH), lambda b: (b, 0, 0))
    return pl.pallas_call(
        body, name="sw_bwd", grid=(B,),
        in_specs=[pl.BlockSpec(memory_space=pltpu.SMEM), q, k, v, qo],
        out_specs=[qo, ko, ko, _full((SW_HEADS, 128))],
        out_shape=[jax.ShapeDtypeStruct((B, S, SW_WIDTH), F32), jax.ShapeDtypeStruct((B, S, SW_KV_WIDTH), F32),
                   jax.ShapeDtypeStruct((B, S, SW_KV_WIDTH), F32), jax.ShapeDtypeStruct((SW_HEADS, 128), F32)],
        scratch_shapes=[pltpu.VMEM((S, 2 * SW_KV_WIDTH), F32), pltpu.VMEM((S, 2 * SW_KV_WIDTH), F32)],
        compiler_params=_params("arbitrary"),
    )(sink, qkv, qkv, qkv, dob)


def _pack_sum(packs):
    W = packs.shape[1]

    def body(p_ref, o_ref, s_ref):
        tot = p_ref[0:8, :]
        for d in range(1, N_DEV):
            tot = tot + p_ref[8 * d:8 * d + 8, :]
        o_ref[...] = tot
        s_ref[...] = tot[0:1, :] + tot[1:2, :]

    return pl.pallas_call(body, name="pack_sum", out_shape=[jax.ShapeDtypeStruct((8, W), F32), jax.ShapeDtypeStruct((1, W), F32)],
                          compiler_params=pltpu.CompilerParams(vmem_limit_bytes=VMEM_LIMIT))(packs)


def _adam_update(w, g, m, v):
    c1 = 1.0 - ADAM_B1 ** ADAM_STEP
    c2 = 1.0 - ADAM_B2 ** ADAM_STEP
    nm = ADAM_B1 * m + (1.0 - ADAM_B1) * g
    nv = ADAM_B2 * v + (1.0 - ADAM_B2) * (g * g)
    return -ADAM_LR * ((nm / c1) / (jnp.sqrt(nv / c2) + ADAM_EPS) + ADAM_WD * w), nm, nv


def _adamw(w, g, m, v, name):
    def body(w_ref, g_ref, m_ref, v_ref, d_ref, nm_ref, nv_ref):
        d_ref[...], nm_ref[...], nv_ref[...] = _adam_update(w_ref[...], g_ref[...], m_ref[...], v_ref[...])

    s = jax.ShapeDtypeStruct(w.shape, F32)
    return pl.pallas_call(body, name=name, out_shape=[s, s, s],
                          compiler_params=pltpu.CompilerParams(vmem_limit_bytes=VMEM_LIMIT))(w, g, m, v)


def _sum_adamw(own, recvb, w, m, v, name):
    R, C = own.shape
    rc = _row_chunk(R)

    def body(own_ref, r_ref, w_ref, m_ref, v_ref, g_ref, d_ref, nm_ref, nv_ref):
        def chunk(i, carry):
            rows = pl.ds(pl.multiple_of(i * rc, rc), rc)
            g = own_ref[rows, :]
            for j in range(3):
                g = g + r_ref[j, rows, :].astype(F32)
            g_ref[rows, :] = g
            d_ref[rows, :], nm_ref[rows, :], nv_ref[rows, :] = _adam_update(w_ref[rows, :], g, m_ref[rows, :], v_ref[rows, :])
            return carry

        lax.fori_loop(0, R // rc, chunk, 0)

    s = jax.ShapeDtypeStruct((R, C), F32)
    return pl.pallas_call(body, name=name, out_shape=[s, s, s, s],
                          compiler_params=pltpu.CompilerParams(vmem_limit_bytes=VMEM_LIMIT))(own, recvb, w, m, v)


def _by_device(dw):
    return dw.reshape(N_DEV, dw.shape[0] // N_DEV, dw.shape[1])


def _local_step(x, mod, g_attn, w_in, bias, sw_sink, g_na_out, g_sw_out, w_out, g_ffn, w_up, conv_w, conv_b, w_down,
                g_final, target, sharded):
    B, S, D = x.shape
    T = B * S
    x2d = x.reshape(T, D)
    mod3 = mod.reshape(B, 6, D)
    cos_t, sin_t = _rope_tables(S)
    sink = sw_sink.reshape(SW_HEADS)
    n_tiles = T // TOKEN_TILE
    full = lambda g: g.reshape(N_DEV * g.shape[1], g.shape[2])

    tasks = [_gather_task(w_out, n_tiles // 2), _gather_task(w_down, 3 * n_tiles // 4)] if sharded else []
    (h, qkv), got = _attn_in(x2d, mod3, g_attn, w_in, cos_t, sin_t, S, tasks)
    if sharded:
        w_out, w_down = full(got[0][0]), full(got[1][0])
    qkv3 = qkv.reshape(B, S, IN_WIDTH)
    na_steps = B * (NA_PAIRS // NA_STEP_PAIRS)
    (oa,), got = _na_fwd(qkv3, bias, [_gather_task(w_up, na_steps - 1)] if sharded else [])
    if sharded:
        w_up = full(got[0][0])
    oa = oa.reshape(T, NA_WIDTH)
    ob = _sw_fwd(sink, qkv3).reshape(T, SW_WIDTH)
    mixin, mix, x1 = _attn_out(oa, ob, x2d, mod3, g_na_out, g_sw_out, w_out, S)
    h2, val, gt = _ffn_up(x1, mod3, g_ffn, w_up, S)
    a, dx2, df, gstat_f, bstat_f = _ffn_down(gt, val, conv_w, conv_b, w_down, x1, mod3, g_final, target.reshape(T, D), B, S)
    F = val.shape[1]

    dw_down = _matmul_tn(a, df, "dw_down")
    (dval, dgc, cstat), got = _ffn_down_bwd(df, w_down, gt, val, conv_w, conv_b, S, [_swap_task(_by_device(dw_down))] if sharded else [])
    if sharded:
        send_down, own_down = _chip_sums(_by_device(dw_down), got[0][0])
    (du, dx1, dmix, gstat_u, bstat_u), got = _ffn_up_bwd(dgc, dval, conv_w, w_up, x1, mod3, g_ffn, dx2, mix, B, S,
                                                         [_exchange_task(send_down)] if sharded else [])
    if sharded:
        dw_down = (own_down, got[0][0])
    dw_up = _matmul_tn(du, h2, "dw_up", tm=F)
    dw_out = _matmul_tn(mixin, dmix, "dw_out")
    (doa, dob, gstat_o), got = _attn_out_bwd(dmix, w_out, oa, ob, g_na_out, g_sw_out,
                                             [_swap_task(_by_device(dw_up)), _swap_task(_by_device(dw_out))] if sharded else [])
    if sharded:
        send_up, own_up = _chip_sums(_by_device(dw_up), got[0][0])
        send_out, own_out = _chip_sums(_by_device(dw_out), got[1][0])
    (dqa, dka, dva, dbt), got = _na_bwd(qkv3, bias, doa.reshape(B, S, NA_WIDTH),
                                        [_exchange_task(send_up), _exchange_task(send_out)] if sharded else [])
    if sharded:
        dw_up, dw_out = (own_up, got[0][0]), (own_out, got[1][0])
    dqb, dkb, dvb, dsink = _sw_bwd(sink, qkv3, dob.reshape(B, S, SW_WIDTH))
    r2 = lambda t: t.reshape(T, t.shape[-1])
    grad_x, dproj, gstat_i, bstat_i = _attn_in_bwd(r2(dqa), r2(dka), r2(dva), r2(dqb), r2(dkb), r2(dvb), cos_t, sin_t, w_in, x2d, mod3,
                                                   g_attn, dx1, B, S)
    dw_in = _matmul_tn(dproj, h, "dw_in")

    dmod = jnp.stack([bstat_i[:, 0], bstat_i[:, 1], bstat_u[:, 2], bstat_u[:, 0], bstat_u[:, 1], bstat_f[:, 0]], axis=1)
    small = dict(g_attn=gstat_i[0], g_ffn=gstat_u[0], g_final=gstat_f[0], loss=gstat_f[1, 0], g_na_out=gstat_o[0], g_sw_out=gstat_o[1],
                 sw_sink=dsink[:, 0], conv_b=cstat[0], conv_w=cstat[1:4], dbt=dbt)
    return grad_x.reshape(B, S, D), dict(w_in=dw_in, w_out=dw_out, w_up=dw_up, w_down=dw_down), dmod, small


def _pad_lanes(v, w):
    return jnp.pad(v, (0, w - v.shape[0]))


def kernel(x, c, w_ada, b_ada, g_attn, w_in, na_rpb, sw_sink, g_na_out, g_sw_out, w_out, g_ffn, w_up, conv_w, conv_b, w_down, g_final, loss_target, m_w_ada, m_b_ada, m_g_attn, m_w_in, m_na_rpb, m_sw_sink, m_g_na_out, m_g_sw_out, m_w_out, m_g_ffn, m_w_up, m_conv_w, m_conv_b, m_w_down, m_g_final, v_w_ada, v_b_ada, v_g_attn, v_w_in, v_na_rpb, v_sw_sink, v_g_na_out, v_g_sw_out, v_w_out, v_g_ffn, v_w_up, v_conv_w, v_conv_b, v_w_down, v_g_final):
    B, S, D = x.shape
    me = 4 * lax.axis_index("x") + 2 * lax.axis_index("y") + lax.axis_index("c")
    ada_c = w_ada.shape[2]
    F_l = conv_w.shape[2]
    onehot = _toeplitz_onehot()

    cw_l = jnp.pad(conv_w[0], ((0, 8 - conv_w.shape[1]), (0, 0)))
    c_l = jnp.pad(c, ((0, 8 - B), (0, 0)))
    slabs = _all_gather(jnp.concatenate([c_l, cw_l], axis=1), "gather_c")
    c_all = slabs[:, :, :D].reshape(N_DEV * 8, D)
    conv_w_f = jnp.transpose(slabs[:, :3, D:], (1, 0, 2)).reshape(3, N_DEV * F_l)

    b_ada_l = lax.dynamic_slice(b_ada, (0, me * ada_c), (1, ada_c))
    mod_cols = _ada_fwd(c_all, w_ada[0], b_ada_l)
    mod_all = _all_gather(mod_cols, "gather_mod")
    mod_mine = lax.dynamic_slice(mod_all, (0, me * 8, 0), (N_DEV, B, ada_c))
    mod = jnp.transpose(mod_mine, (1, 0, 2)).reshape(B, N_DEV * ada_c)

    tr = {"w_in", "w_up"}
    w_in_t = jnp.transpose(w_in[0])
    w_in_f = _all_gather(w_in_t.astype(BF16), "gather_w_in").reshape(N_DEV * w_in_t.shape[0], D)
    shards = dict(w_out=w_out[0].astype(BF16), w_up=jnp.transpose(w_up[0]).astype(BF16), w_down=w_down[0].astype(BF16))

    bias = _na_bias_table(na_rpb[0], onehot)

    grad_x, dw, dmod, small = _local_step(x, mod, g_attn, w_in_f, bias, sw_sink, g_na_out, g_sw_out, shards["w_out"], g_ffn,
                                          shards["w_up"], conv_w_f, conv_b, shards["w_down"], g_final.reshape(1, D), loss_target,
                                          sharded=True)
    g_w_in = _reduce_scatter(_by_device(dw["w_in"]), "scatter_w_in")

    drpb = _na_bias_grad(small["dbt"], onehot)

    row2 = jnp.concatenate([small["g_attn"], small["g_ffn"], small["g_final"], small["g_na_out"], small["g_sw_out"],
                            _pad_lanes(small["sw_sink"], 128), _pad_lanes(small["loss"].reshape(1), 128)])
    rows = [dmod.reshape(B, 6 * D)[0], dmod.reshape(B, 6 * D)[1], _pad_lanes(row2, PACK_W), _pad_lanes(small["conv_b"], PACK_W),
            _pad_lanes(drpb, PACK_W)] + [_pad_lanes(small["conv_w"][k], PACK_W) for k in range(3)]
    packs = _all_gather(jnp.stack(rows), "gather_small")
    tot, g_b_ada = _pack_sum(packs.reshape(N_DEV * 8, PACK_W))

    o = 0
    rep = {}
    for nm, wd in (("g_attn", D), ("g_ffn", D), ("g_final", D), ("g_na_out", NA_WIDTH), ("g_sw_out", SW_WIDTH), ("sw_sink", 128), ("loss", 128)):
        rep[nm] = tot[2, o:o + wd]
        o += wd
    loss = rep["loss"][0]
    g_conv_b = tot[3:4, :conv_b.shape[1]]
    g_na_rpb = tot[4, :NA_HEADS * N_DR * N_DC].reshape(na_rpb.shape)
    g_conv_w = lax.dynamic_slice(tot[5:8], (0, me * F_l), (3, F_l)).reshape(conv_w.shape)

    dmod_cols = lax.dynamic_slice(packs.reshape(N_DEV * 8, PACK_W), (0, me * ada_c), (N_DEV * 8, ada_c))
    g_w_ada = _ada_bwd(c_all, dmod_cols)[None]

    grads = dict(
        w_ada=g_w_ada, b_ada=g_b_ada, g_attn=rep["g_attn"][None], w_in=g_w_in, na_rpb=g_na_rpb,
        sw_sink=rep["sw_sink"][None, :SW_HEADS], g_na_out=rep["g_na_out"][None], g_sw_out=rep["g_sw_out"][None],
        w_out=None, g_ffn=rep["g_ffn"][None], w_up=None, conv_w=g_conv_w, conv_b=g_conv_b, w_down=None, g_final=rep["g_final"])
    weights = dict(w_ada=w_ada, b_ada=b_ada, g_attn=g_attn, w_in=w_in, na_rpb=na_rpb, sw_sink=sw_sink, g_na_out=g_na_out,
                   g_sw_out=g_sw_out, w_out=w_out, g_ffn=g_ffn, w_up=w_up, conv_w=conv_w, conv_b=conv_b, w_down=w_down, g_final=g_final)
    ms = dict(w_ada=m_w_ada, b_ada=m_b_ada, g_attn=m_g_attn, w_in=m_w_in, na_rpb=m_na_rpb, sw_sink=m_sw_sink, g_na_out=m_g_na_out,
              g_sw_out=m_g_sw_out, w_out=m_w_out, g_ffn=m_g_ffn, w_up=m_w_up, conv_w=m_conv_w, conv_b=m_conv_b, w_down=m_w_down, g_final=m_g_final)
    vs = dict(w_ada=v_w_ada, b_ada=v_b_ada, g_attn=v_g_attn, w_in=v_w_in, na_rpb=v_na_rpb, sw_sink=v_sw_sink, g_na_out=v_g_na_out,
              g_sw_out=v_g_sw_out, w_out=v_w_out, g_ffn=v_g_ffn, w_up=v_w_up, conv_w=v_conv_w, conv_b=v_conv_b, w_down=v_w_down, g_final=v_g_final)
    names = list(weights)
    deltas, new_m, new_v = {}, {}, {}
    for nm in names:
        shp = weights[nm].shape
        if nm in tr:
            r = lambda t: jnp.transpose(t[0])
            back = lambda t: jnp.transpose(t)[None]
        else:
            two_d = (shp[-2], shp[-1]) if len(shp) >= 3 and nm != "na_rpb" else (1, int(np.prod(shp)))
            r = lambda t: t.reshape(two_d)
            back = lambda t: t.reshape(shp)
        if nm in ("w_out", "w_up", "w_down"):
            g2, d_, m_, v_ = _sum_adamw(*dw[nm], r(weights[nm]), r(ms[nm]), r(vs[nm]), "adamw_" + nm)
        else:
            g2 = grads[nm] if nm in tr else r(grads[nm])
            d_, m_, v_ = _adamw(r(weights[nm]), g2, r(ms[nm]), r(vs[nm]), "adamw_" + nm)
        deltas[nm], new_m[nm], new_v[nm], grads[nm] = back(d_), back(m_), back(v_), back(g2)
    return (loss, grad_x, *[grads[n] for n in names], *[deltas[n] for n in names], *[new_m[n] for n in names],
            *[new_v[n] for n in names])
```

```python
import functools

import numpy as np
import jax
import jax.numpy as jnp
from jax import lax
from jax.experimental import pallas as pl
from jax.experimental.pallas import tpu as pltpu

F32, BF16 = jnp.float32, jnp.bfloat16
MESH_ID = pl.DeviceIdType.MESH
N_DEV = 8

HEAD_DIM = 64
NA_HEADS = 8
SW_HEADS = 8
SW_KV_HEADS = 2
SW_GROUP = SW_HEADS // SW_KV_HEADS
NA_WIDTH = NA_HEADS * HEAD_DIM
SW_WIDTH = SW_HEADS * HEAD_DIM
SW_KV_WIDTH = SW_KV_HEADS * HEAD_DIM
ROPE_WIDTH = SW_WIDTH + SW_KV_WIDTH
IN_WIDTH = 3 * NA_WIDTH + SW_WIDTH + 2 * SW_KV_WIDTH
ROPE_LO = 3 * NA_WIDTH
GRID_W = 64
NA_ROWS_MAX = 8
NA_COLS = 16
N_DR = 2 * NA_ROWS_MAX - 1
N_DC = 2 * NA_COLS - 1
SW_WINDOW = 128
SW_BLOCK = 128
ROPE_THETA = 10000.0
EPS = 1e-6
NEG = -1e30
Q_SCALE = HEAD_DIM ** -0.5

ADAM_LR = 0.001
ADAM_B1 = 0.9
ADAM_B2 = 0.999
ADAM_EPS = 1e-08
ADAM_WD = 0.01
ADAM_STEP = 10

TOKEN_TILE = 256
VMEM_LIMIT = 56 * 1024 * 1024

PACK_W = 6144


def _nn(a, b):
    return jnp.dot(a, b, preferred_element_type=F32)


def _nt(a, b):
    return lax.dot_general(a, b, (((1,), (1,)), ((), ())), preferred_element_type=F32)


def _tn(a, b):
    return lax.dot_general(a, b, (((0,), (0,)), ((), ())), preferred_element_type=F32)


def _rms(x):
    r = lax.rsqrt(jnp.mean(x * x, axis=-1, keepdims=True) + EPS)
    return x * r, r


def _rms_bwd(xn, r, gy):
    return r * (gy - xn * jnp.mean(xn * gy, axis=-1, keepdims=True))


def _params(*sem):
    return pltpu.CompilerParams(dimension_semantics=sem, vmem_limit_bytes=VMEM_LIMIT)


def _full(shape):
    n = len(shape)
    return pl.BlockSpec(shape, lambda *_: (0,) * n)


def _mesh_pos():
    return lax.axis_index("x"), lax.axis_index("y"), lax.axis_index("c")


def _all_gather(x, name):
    def body(x_ref, out_ref, send_sems, recv_sems, local_sem):
        x_, y_, c_ = _mesh_pos()
        me, sibling = (x_, y_, c_), (x_, y_, 1 - c_)
        chips = [(1 - x_, y_), (x_, 1 - y_), (1 - x_, 1 - y_)]

        def rows(px, py, pc):
            return out_ref.at[4 * px + 2 * py + pc]

        def copy(k, block, to, src=None):
            return pltpu.make_async_remote_copy(
                src_ref=rows(*block) if src is None else src, dst_ref=rows(*block),
                send_sem=send_sems.at[k], recv_sem=recv_sems.at[k], device_id=to, device_id_type=MESH_ID)

        mine = pltpu.make_async_copy(x_ref, rows(*me), local_sem)
        mine.start()
        first = [copy(0, me, sibling, src=x_ref)]
        first += [copy(1 + j, me, (*chip, c_), src=x_ref) for j, chip in enumerate(chips)]
        for cp in first:
            cp.start()
        passed = [copy(4 + j, (*chip, c_), sibling) for j, chip in enumerate(chips)]
        for j, chip in enumerate(chips):
            copy(1 + j, (*chip, c_), me).wait_recv()
            passed[j].start()
        copy(0, sibling, me).wait_recv()
        for j, chip in enumerate(chips):
            copy(4 + j, (*chip, 1 - c_), me).wait_recv()
        for cp in first + passed:
            cp.wait_send()
        mine.wait()

    return pl.pallas_call(
        body, name=name,
        out_shape=jax.ShapeDtypeStruct((N_DEV,) + x.shape, x.dtype),
        in_specs=[pl.BlockSpec(memory_space=pl.ANY)],
        out_specs=pl.BlockSpec(memory_space=pl.ANY),
        scratch_shapes=[pltpu.SemaphoreType.DMA((7,)), pltpu.SemaphoreType.DMA((7,)), pltpu.SemaphoreType.DMA],
    )(x)


def _row_chunk(r):
    for rc in (128, 64, 32, 16):
        if r % rc == 0:
            return rc
    raise ValueError(f"rows {r} not a multiple of 16")


def _reduce_scatter(g8, name):
    _, R, C = g8.shape
    rc = _row_chunk(R)

    def body(g_ref, out_ref, recva, sendb, recvb, sa, ra, sb, rb):
        x_, y_, c_ = _mesh_pos()
        sibling = (x_, y_, 1 - c_)
        copies_a = []
        for k in range(4):
            cp = pltpu.make_async_remote_copy(
                src_ref=g_ref.at[2 * k + (1 - c_)], dst_ref=recva.at[k],
                send_sem=sa.at[k], recv_sem=ra.at[k], device_id=sibling, device_id_type=MESH_ID)
            cp.start()
            copies_a.append(cp)
        for cp in copies_a:
            cp.wait_recv()

        def chip_sum(k, rows):
            return g_ref[2 * k + c_, rows, :].astype(F32) + recva[k, rows, :].astype(F32)

        flips = [(1 - x_, y_), (x_, 1 - y_), (1 - x_, 1 - y_)]
        copies_b = []
        for j, (tx, ty) in enumerate(flips):
            kt = 2 * tx + ty

            def fill(i, carry, j=j, kt=kt):
                rows = pl.ds(pl.multiple_of(i * rc, rc), rc)
                sendb[j, rows, :] = chip_sum(kt, rows).astype(BF16)
                return carry

            lax.fori_loop(0, R // rc, fill, 0)
            cp = pltpu.make_async_remote_copy(
                src_ref=sendb.at[j], dst_ref=recvb.at[j],
                send_sem=sb.at[j], recv_sem=rb.at[j], device_id=(tx, ty, c_), device_id_type=MESH_ID)
            cp.start()
            copies_b.append(cp)
        for cp in copies_b:
            cp.wait_recv()
        kme = 2 * x_ + y_

        def total(i, carry):
            rows = pl.ds(pl.multiple_of(i * rc, rc), rc)
            acc = chip_sum(kme, rows)
            for j in range(3):
                acc = acc + recvb[j, rows, :].astype(F32)
            out_ref[rows, :] = acc
            return carry

        lax.fori_loop(0, R // rc, total, 0)
        for cp in copies_a + copies_b:
            cp.wait_send()

    vm = pl.BlockSpec(memory_space=pltpu.VMEM)
    return pl.pallas_call(
        body, name=name,
        out_shape=jax.ShapeDtypeStruct((R, C), F32),
        in_specs=[vm], out_specs=vm,
        scratch_shapes=[pltpu.VMEM((4, R, C), BF16), pltpu.VMEM((3, R, C), BF16), pltpu.VMEM((3, R, C), BF16),
                        pltpu.SemaphoreType.DMA((4,)), pltpu.SemaphoreType.DMA((4,)),
                        pltpu.SemaphoreType.DMA((3,)), pltpu.SemaphoreType.DMA((3,))],
        compiler_params=pltpu.CompilerParams(vmem_limit_bytes=VMEM_LIMIT),
    )(g8)


class _Task:
    def __init__(self, inputs, out_shapes, sems, start, finish, mid=None, mid_step=None):
        self.inputs, self.out_shapes, self.sems = list(inputs), list(out_shapes), list(sems)
        self.start, self.finish, self.mid, self.mid_step = start, finish, mid, mid_step


def _hosted_call(body, name, grid, in_specs, out_specs, out_shape, operands, tasks, scratch_shapes=()):
    n_in, n_out, n_scr = len(in_specs), len(out_specs), len(scratch_shapes)
    t_in = [len(t.inputs) for t in tasks]
    t_out = [len(t.out_shapes) for t in tasks]
    t_sem = [len(t.sems) for t in tasks]
    n_steps = int(np.prod(grid))

    def wrapped(*refs):
        ins, rest = refs[:n_in], refs[n_in:]
        task_ins, rest = rest[:sum(t_in)], rest[sum(t_in):]
        outs, rest = rest[:n_out], rest[n_out:]
        task_outs, rest = rest[:sum(t_out)], rest[sum(t_out):]
        scr, task_sems = rest[:n_scr], rest[n_scr:]
        step = pl.program_id(0)
        for ax in range(1, len(grid)):
            step = step * grid[ax] + pl.program_id(ax)
        parts = []
        oi = oo = os_ = 0
        for t, a, b, c in zip(tasks, t_in, t_out, t_sem):
            parts.append((t, task_ins[oi:oi + a], task_outs[oo:oo + b], task_sems[os_:os_ + c]))
            oi, oo, os_ = oi + a, oo + b, os_ + c
        for t, ti, to, ts in parts:
            pl.when(step == 0)(functools.partial(t.start, ti, to, ts))
            if t.mid is not None:
                pl.when(step == t.mid_step)(functools.partial(t.mid, ti, to, ts))
        body(*ins, *outs, *scr)
        for t, ti, to, ts in parts:
            pl.when(step == n_steps - 1)(functools.partial(t.finish, ti, to, ts))

    hbm = pl.BlockSpec(memory_space=pl.ANY)
    res = pl.pallas_call(
        wrapped, name=name, grid=grid,
        in_specs=list(in_specs) + [hbm] * sum(t_in),
        out_specs=list(out_specs) + [hbm] * sum(t_out),
        out_shape=list(out_shape) + [s for t in tasks for s in t.out_shapes],
        scratch_shapes=list(scratch_shapes) + [s for t in tasks for s in t.sems],
        compiler_params=_params(*(["arbitrary"] * len(grid))),
    )(*operands, *[a for t in tasks for a in t.inputs])
    own, extra = res[:n_out], res[n_out:]
    per_task, o = [], 0
    for b in t_out:
        per_task.append(extra[o:o + b])
        o += b
    return own, per_task


def _gather_task(shard, mid_step):
    def parts(ins, outs, sems):
        (x_ref,), (out_ref,), (send_sems, recv_sems, local_sem) = ins, outs, sems
        x_, y_, c_ = _mesh_pos()
        me, sibling = (x_, y_, c_), (x_, y_, 1 - c_)
        chips = [(1 - x_, y_), (x_, 1 - y_), (1 - x_, 1 - y_)]

        def rows(px, py, pc):
            return out_ref.at[4 * px + 2 * py + pc]

        def copy(k, block, to, src=None):
            return pltpu.make_async_remote_copy(
                src_ref=rows(*block) if src is None else src, dst_ref=rows(*block),
                send_sem=send_sems.at[k], recv_sem=recv_sems.at[k], device_id=to, device_id_type=MESH_ID)

        return dict(
            mine=lambda: pltpu.make_async_copy(x_ref, rows(*me), local_sem),
            first=lambda: [copy(0, me, sibling, src=x_ref)] + [copy(1 + j, me, (*chip, c_), src=x_ref) for j, chip in enumerate(chips)],
            passed=lambda: [copy(4 + j, (*chip, c_), sibling) for j, chip in enumerate(chips)],
            landed=lambda: [copy(1 + j, (*chip, c_), me) for j, chip in enumerate(chips)],
            last=lambda: [copy(0, sibling, me)] + [copy(4 + j, (*chip, 1 - c_), me) for j, chip in enumerate(chips)])

    def start(ins, outs, sems):
        p = parts(ins, outs, sems)
        p["mine"]().start()
        for cp in p["first"]():
            cp.start()

    def mid(ins, outs, sems):
        p = parts(ins, outs, sems)
        for cp, fw in zip(p["landed"](), p["passed"]()):
            cp.wait_recv()
            fw.start()

    def finish(ins, outs, sems):
        p = parts(ins, outs, sems)
        for cp in p["last"]():
            cp.wait_recv()
        for cp in p["first"]() + p["passed"]():
            cp.wait_send()
        p["mine"]().wait()

    return _Task([shard], [jax.ShapeDtypeStruct((N_DEV,) + shard.shape, shard.dtype)],
                 [pltpu.SemaphoreType.DMA((7,)), pltpu.SemaphoreType.DMA((7,)), pltpu.SemaphoreType.DMA],
                 start, finish, mid, mid_step)


def _swap_task(g8):
    _, R, C = g8.shape

    def copies(ins, outs, sems):
        (g_ref,), (recv_ref,), (ss, rs) = ins, outs, sems
        x_, y_, c_ = _mesh_pos()
        return [pltpu.make_async_remote_copy(src_ref=g_ref.at[2 * k + (1 - c_)], dst_ref=recv_ref.at[k], send_sem=ss.at[k],
                                             recv_sem=rs.at[k], device_id=(x_, y_, 1 - c_), device_id_type=MESH_ID)
                for k in range(4)]

    def start(ins, outs, sems):
        for cp in copies(ins, outs, sems):
            cp.start()

    def finish(ins, outs, sems):
        cps = copies(ins, outs, sems)
        for cp in cps:
            cp.wait_recv()
        for cp in cps:
            cp.wait_send()

    return _Task([g8], [jax.ShapeDtypeStruct((4, R, C), g8.dtype)],
                 [pltpu.SemaphoreType.DMA((4,)), pltpu.SemaphoreType.DMA((4,))], start, finish)


def _chip_sums(g8, recva):
    _, R, C = g8.shape
    rc = _row_chunk(R)

    def body(g_ref, a_ref, send_ref, own_ref):
        x_, y_, c_ = _mesh_pos()
        chips = [(1 - x_, y_), (x_, 1 - y_), (1 - x_, 1 - y_), (x_, y_)]

        def chunk(i, carry):
            rows = pl.ds(pl.multiple_of(i * rc, rc), rc)
            for j, (tx, ty) in enumerate(chips):
                k = 2 * tx + ty
                s = g_ref[2 * k + c_, rows, :].astype(F32) + a_ref[k, rows, :].astype(F32)
                if j < 3:
                    send_ref[j, rows, :] = s.astype(BF16)
                else:
                    own_ref[rows, :] = s
            return carry

        lax.fori_loop(0, R // rc, chunk, 0)

    return pl.pallas_call(body, name="chip_sums", out_shape=[jax.ShapeDtypeStruct((3, R, C), BF16), jax.ShapeDtypeStruct((R, C), F32)],
                          compiler_params=pltpu.CompilerParams(vmem_limit_bytes=VMEM_LIMIT))(g8, recva)


def _exchange_task(sendb):
    def copies(ins, outs, sems):
        (s_ref,), (recv_ref,), (ss, rs) = ins, outs, sems
        x_, y_, c_ = _mesh_pos()
        flips = [(1 - x_, y_), (x_, 1 - y_), (1 - x_, 1 - y_)]
        return [pltpu.make_async_remote_copy(src_ref=s_ref.at[j], dst_ref=recv_ref.at[j], send_sem=ss.at[j], recv_sem=rs.at[j],
                                             device_id=(tx, ty, c_), device_id_type=MESH_ID) for j, (tx, ty) in enumerate(flips)]

    def start(ins, outs, sems):
        for cp in copies(ins, outs, sems):
            cp.start()

    def finish(ins, outs, sems):
        cps = copies(ins, outs, sems)
        for cp in cps:
            cp.wait_recv()
        for cp in cps:
            cp.wait_send()

    return _Task([sendb], [jax.ShapeDtypeStruct(sendb.shape, sendb.dtype)],
                 [pltpu.SemaphoreType.DMA((3,)), pltpu.SemaphoreType.DMA((3,))], start, finish)


def _silu(v):
    return v * (1.0 / (1.0 + jnp.exp(-v)))


def _ada_fwd(c_all, w_ada_l, b_ada_l):
    def body(c_ref, w_ref, b_ref, o_ref):
        o_ref[...] = jnp.dot(_silu(c_ref[...]), w_ref[...], precision=lax.Precision.HIGHEST,
                             preferred_element_type=F32) + b_ref[...]
    return pl.pallas_call(body, name="ada_fwd", out_shape=jax.ShapeDtypeStruct((c_all.shape[0], w_ada_l.shape[1]), F32),
                          compiler_params=pltpu.CompilerParams(vmem_limit_bytes=VMEM_LIMIT))(c_all, w_ada_l, b_ada_l)


def _ada_bwd(c_all, dmod_cols):
    def body(c_ref, d_ref, o_ref):
        o_ref[...] = lax.dot_general(_silu(c_ref[...]), d_ref[...], (((0,), (0,)), ((), ())),
                                     precision=lax.Precision.HIGHEST, preferred_element_type=F32)
    return pl.pallas_call(body, name="ada_bwd", out_shape=jax.ShapeDtypeStruct((c_all.shape[1], dmod_cols.shape[1]), F32),
                          compiler_params=pltpu.CompilerParams(vmem_limit_bytes=VMEM_LIMIT))(c_all, dmod_cols)


def _toeplitz_onehot():
    k = np.arange(GRID_W)[:, None]
    q = np.arange(GRID_W)[None, :]
    dc = np.clip(k - q + NA_COLS - 1, 0, N_DC - 1).reshape(-1)
    e = np.zeros((128, GRID_W * GRID_W), np.float32)
    e[dc, np.arange(GRID_W * GRID_W)] = 1.0
    return jnp.asarray(e)


def _rpb_expand(rpb2d, onehot):
    def body(r_ref, e_ref, o_ref):
        o_ref[...] = jnp.dot(r_ref[...], e_ref[...], precision=lax.Precision.HIGHEST, preferred_element_type=F32)
    return pl.pallas_call(body, name="rpb_expand", out_shape=jax.ShapeDtypeStruct((128, GRID_W * GRID_W), F32),
                          compiler_params=pltpu.CompilerParams(vmem_limit_bytes=VMEM_LIMIT))(rpb2d, onehot)


def _rpb_reduce(dtz2d, onehot):
    def body(d_ref, e_ref, o_ref):
        o_ref[...] = lax.dot_general(d_ref[...], e_ref[...], (((1,), (1,)), ((), ())),
                                     precision=lax.Precision.HIGHEST, preferred_element_type=F32)
    return pl.pallas_call(body, name="rpb_reduce", out_shape=jax.ShapeDtypeStruct((128, 128), F32),
                          compiler_params=pltpu.CompilerParams(vmem_limit_bytes=VMEM_LIMIT))(dtz2d, onehot)


NA_PAIRS = NA_HEADS // 2


def _na_bias_table(na_rpb, onehot):
    rpb2d = jnp.pad(na_rpb.reshape(NA_HEADS * N_DR, N_DC), ((0, 128 - NA_HEADS * N_DR), (0, 128 - N_DC)))
    tz = _rpb_expand(rpb2d, onehot)[:NA_HEADS * N_DR].reshape(NA_PAIRS, 2, N_DR, GRID_W, GRID_W)
    col = np.arange(GRID_W)
    cs = np.clip(col - NA_COLS // 2, 0, GRID_W - NA_COLS)
    ok_kq = ((col[None, :] >= cs[:, None]) & (col[None, :] < cs[:, None] + NA_COLS)).T
    tz = jnp.where(jnp.asarray(ok_kq)[None, None, None], tz, NEG)
    return jnp.transpose(tz, (0, 2, 3, 1, 4)).reshape(NA_PAIRS, N_DR * GRID_W, 128)


def _na_bias_grad(db, onehot):
    fold = db.reshape(NA_PAIRS, N_DR, GRID_W, 2, GRID_W)
    dtz2d = jnp.transpose(fold, (0, 3, 1, 2, 4)).reshape(NA_HEADS * N_DR, GRID_W * GRID_W)
    dtz2d = jnp.pad(dtz2d, ((0, 128 - NA_HEADS * N_DR), (0, 0)))
    return _rpb_reduce(dtz2d, onehot)[:NA_HEADS * N_DR, :N_DC].reshape(-1)


def _rope_tables(S):
    half = HEAD_DIM // 2
    inv = ROPE_THETA ** (-jnp.arange(half, dtype=F32) / half)
    ang = jnp.arange(S).astype(F32)[:, None] * inv[None, :]
    cos, sin = jnp.cos(ang), jnp.sin(ang)
    n = ROPE_WIDTH // HEAD_DIM
    return jnp.tile(jnp.concatenate([cos, cos], axis=1), (1, n)), jnp.tile(jnp.concatenate([-sin, sin], axis=1), (1, n))


def _rot_half(t):
    w = t.shape[1]
    lane = lax.broadcasted_iota(jnp.int32, t.shape, 1)
    return jnp.where((lane % HEAD_DIM) < HEAD_DIM // 2, pltpu.roll(t, w - HEAD_DIM // 2, axis=1),
                     pltpu.roll(t, HEAD_DIM // 2, axis=1))


def _tok_spec(w):
    return pl.BlockSpec((TOKEN_TILE, w), lambda i: (i, 0))


def _mod_spec(tps, d):
    return pl.BlockSpec((1, 6, d), lambda i: (i // tps, 0, 0))


def _bstat_spec(tps, w):
    return pl.BlockSpec((1, 8, w), lambda i: (i // tps, 0, 0))


def _attn_in(x2d, mod3, g_attn, w_in, cos_t, sin_t, S, tasks=()):
    T, D = x2d.shape
    tps = S // TOKEN_TILE

    def body(x_ref, mod_ref, g_ref, w_ref, cos_ref, sin_ref, h_ref, qkv_ref):
        xn, _ = _rms(x_ref[...])
        h = (xn * g_ref[...]) * (1.0 + mod_ref[0, 1:2, :]) + mod_ref[0, 0:1, :]
        hb = h.astype(BF16)
        h_ref[...] = hb
        proj = _nt(hb, w_ref[...])
        rb = proj[:, ROPE_LO:ROPE_LO + ROPE_WIDTH]
        rb = rb * cos_ref[...] + _rot_half(rb) * sin_ref[...]
        qkv_ref[:, 0:NA_WIDTH] = (proj[:, 0:NA_WIDTH] * Q_SCALE).astype(BF16)
        qkv_ref[:, NA_WIDTH:ROPE_LO] = proj[:, NA_WIDTH:ROPE_LO].astype(BF16)
        qkv_ref[:, ROPE_LO:ROPE_LO + SW_WIDTH] = (rb[:, 0:SW_WIDTH] * Q_SCALE).astype(BF16)
        qkv_ref[:, ROPE_LO + SW_WIDTH:ROPE_LO + ROPE_WIDTH] = rb[:, SW_WIDTH:].astype(BF16)
        qkv_ref[:, ROPE_LO + ROPE_WIDTH:] = proj[:, ROPE_LO + ROPE_WIDTH:].astype(BF16)

    return _hosted_call(
        body, "attn_in", (T // TOKEN_TILE,),
        [_tok_spec(D), _mod_spec(tps, D), _full((1, D)), _full(w_in.shape),
         pl.BlockSpec((TOKEN_TILE, ROPE_WIDTH), lambda i: (i % tps, 0)),
         pl.BlockSpec((TOKEN_TILE, ROPE_WIDTH), lambda i: (i % tps, 0))],
        [_tok_spec(D), _tok_spec(IN_WIDTH)],
        [jax.ShapeDtypeStruct((T, D), BF16), jax.ShapeDtypeStruct((T, IN_WIDTH), BF16)],
        (x2d, mod3, g_attn, w_in, cos_t, sin_t), tasks)


def _attn_out(oa, ob, x2d, mod3, g_na, g_sw, w_out, S):
    T, D = x2d.shape
    tps = S // TOKEN_TILE

    def body(oa_ref, ob_ref, x_ref, mod_ref, gna_ref, gsw_ref, w_ref, mixin_ref, mix_ref, x1_ref):
        oan, _ = _rms(oa_ref[...])
        obn, _ = _rms(ob_ref[...])
        mixin = jnp.concatenate([oan * gna_ref[...], obn * gsw_ref[...]], axis=1).astype(BF16)
        mixin_ref[...] = mixin
        mix = _nn(mixin, w_ref[...])
        mix_ref[...] = mix
        x1_ref[...] = x_ref[...] + mod_ref[0, 2:3, :] * mix

    return pl.pallas_call(
        body, name="attn_out", grid=(T // TOKEN_TILE,),
        in_specs=[_tok_spec(NA_WIDTH), _tok_spec(SW_WIDTH), _tok_spec(D), _mod_spec(tps, D),
                  _full((1, NA_WIDTH)), _full((1, SW_WIDTH)), _full(w_out.shape)],
        out_specs=[_tok_spec(NA_WIDTH + SW_WIDTH), _tok_spec(D), _tok_spec(D)],
        out_shape=[jax.ShapeDtypeStruct((T, NA_WIDTH + SW_WIDTH), BF16), jax.ShapeDtypeStruct((T, D), F32),
                   jax.ShapeDtypeStruct((T, D), F32)],
        compiler_params=_params("parallel"),
    )(oa, ob, x2d, mod3, g_na, g_sw, w_out)


def _ffn_up(x1, mod3, g_ffn, w_up, S):
    T, D = x1.shape
    F = w_up.shape[0] // 2
    tps = S // TOKEN_TILE

    def body(x1_ref, mod_ref, g_ref, w_ref, h2_ref, val_ref, gt_ref):
        xn, _ = _rms(x1_ref[...])
        h2 = ((xn * g_ref[...]) * (1.0 + mod_ref[0, 4:5, :]) + mod_ref[0, 3:4, :]).astype(BF16)
        h2_ref[...] = h2
        u = _nt(h2, w_ref[...])
        val_ref[...] = u[:, :F].astype(BF16)
        gt_ref[...] = u[:, F:].astype(BF16)

    return pl.pallas_call(
        body, name="ffn_up", grid=(T // TOKEN_TILE,),
        in_specs=[_tok_spec(D), _mod_spec(tps, D), _full((1, D)), _full(w_up.shape)],
        out_specs=[_tok_spec(D), _tok_spec(F), _tok_spec(F)],
        out_shape=[jax.ShapeDtypeStruct((T, D), BF16), jax.ShapeDtypeStruct((T, F), BF16), jax.ShapeDtypeStruct((T, F), BF16)],
        compiler_params=_params("parallel"),
    )(x1, mod3, g_ffn, w_up)


def _halo_specs(T, tps, w):
    per = TOKEN_TILE // 8
    prev = pl.BlockSpec((8, w), lambda i: (jnp.maximum(i * per - 1, 0), 0))
    nxt = pl.BlockSpec((8, w), lambda i: (jnp.minimum((i + 1) * per, T // 8 - 1), 0))
    return prev, nxt


def _seq_shifts(cur, prev_ref, next_ref, tps):
    tm = cur.shape[0]
    ti = pl.program_id(0) % tps
    c32 = cur.astype(F32)
    row = lax.broadcasted_iota(jnp.int32, cur.shape, 0)
    before = jnp.where(ti > 0, prev_ref[7:8, :].astype(F32), 0.0)
    after = jnp.where(ti < tps - 1, next_ref[0:1, :].astype(F32), 0.0)
    return jnp.where(row == 0, before, pltpu.roll(c32, 1, axis=0)), jnp.where(row == tm - 1, after, pltpu.roll(c32, tm - 1, axis=0))


def _ffn_down(gt, val, conv_w, conv_b, w_down, x1, mod3, g_final, target, B, S):
    T, D = x1.shape
    F = gt.shape[1]
    tps = S // TOKEN_TILE
    prev, nxt = _halo_specs(T, tps, F)

    def body(gt_ref, prev_ref, next_ref, val_ref, cw_ref, cb_ref, w_ref, x1_ref, mod_ref, gf_ref, tgt_ref,
             a_ref, act_ref, vd_ref, dx2_ref, df_ref, gstat_ref, bstat_ref):
        i = pl.program_id(0)
        g = gt_ref[...]
        gprev, gnext = _seq_shifts(g, prev_ref, next_ref, tps)
        gc = gprev * cw_ref[0:1, :] + g.astype(F32) * cw_ref[1:2, :] + gnext * cw_ref[2:3, :] + cb_ref[...]
        sig = 1.0 / (1.0 + jnp.exp(-gc))
        act = gc * sig
        val = val_ref[...].astype(F32)
        act_ref[...] = act.astype(BF16)
        vd_ref[...] = (val * (sig * (1.0 + gc * (1.0 - sig)))).astype(BF16)
        a = (act * val).astype(BF16)
        a_ref[...] = a
        f = _nn(a, w_ref[...])
        gate = mod_ref[0, 5:6, :]
        x2 = x1_ref[...] + gate * f
        xn, r = _rms(x2)
        err = xn * gf_ref[...] - tgt_ref[...]
        dy = err * (1.0 / D)
        dx2 = _rms_bwd(xn, r, dy * gf_ref[...])
        dx2_ref[...] = dx2
        df_ref[...] = (gate * dx2).astype(BF16)

        @pl.when(i == 0)
        def _():
            gstat_ref[...] = jnp.zeros_like(gstat_ref)

        @pl.when(i % tps == 0)
        def _():
            bstat_ref[...] = jnp.zeros_like(bstat_ref)

        gstat_ref[0:1, :] += jnp.sum(dy * xn, axis=0, keepdims=True)
        tile_loss = jnp.sum(jnp.sum(err * err, axis=1, keepdims=True), axis=0, keepdims=True) * (0.5 / D)
        gstat_ref[1:2, :] += jnp.broadcast_to(tile_loss, (1, D))
        bstat_ref[0, 0:1, :] += jnp.sum(dx2 * f, axis=0, keepdims=True)

    return pl.pallas_call(
        body, name="ffn_down", grid=(T // TOKEN_TILE,),
        in_specs=[_tok_spec(F), prev, nxt, _tok_spec(F), _full(conv_w.shape), _full((1, F)), _full(w_down.shape),
                  _tok_spec(D), _mod_spec(tps, D), _full((1, D)), _tok_spec(D)],
        out_specs=[_tok_spec(F), _tok_spec(F), _tok_spec(F), _tok_spec(D), _tok_spec(D), _full((8, D)), _bstat_spec(tps, D)],
        out_shape=[jax.ShapeDtypeStruct((T, F), BF16), jax.ShapeDtypeStruct((T, F), BF16), jax.ShapeDtypeStruct((T, F), BF16),
                   jax.ShapeDtypeStruct((T, D), F32), jax.ShapeDtypeStruct((T, D), BF16),
                   jax.ShapeDtypeStruct((8, D), F32), jax.ShapeDtypeStruct((B, 8, D), F32)],
        compiler_params=_params("arbitrary"),
    )(gt, gt, gt, val, conv_w, conv_b, w_down, x1, mod3, g_final, target)


def _ffn_down_bwd(df, w_down, act, vd, tasks=()):
    T, D = df.shape
    F = act.shape[1]

    def body(df_ref, w_ref, act_ref, vd_ref, dval_ref, dgc_ref, cstat_ref):
        da = _nt(df_ref[...], w_ref[...])
        dval_ref[...] = (da * act_ref[...].astype(F32)).astype(BF16)
        dgc = da * vd_ref[...].astype(F32)
        dgc_ref[...] = dgc.astype(BF16)

        @pl.when(pl.program_id(0) == 0)
        def _():
            cstat_ref[...] = jnp.zeros_like(cstat_ref)

        cstat_ref[0:1, :] += jnp.sum(dgc, axis=0, keepdims=True)

    return _hosted_call(
        body, "ffn_down_bwd", (T // TOKEN_TILE,),
        [_tok_spec(D), _full(w_down.shape), _tok_spec(F), _tok_spec(F)],
        [_tok_spec(F), _tok_spec(F), _full((8, F))],
        [jax.ShapeDtypeStruct((T, F), BF16), jax.ShapeDtypeStruct((T, F), BF16), jax.ShapeDtypeStruct((8, F), F32)],
        (df, w_down, act, vd), tasks)


def _ffn_up_bwd(dgc, dval, gt, conv_w, w_up, x1, mod3, g_ffn, dx2, mix, B, S, tasks=()):
    T, D = x1.shape
    F = dgc.shape[1]
    tps = S // TOKEN_TILE
    prev, nxt = _halo_specs(T, tps, F)

    def body(dgc_ref, prev_ref, next_ref, dval_ref, gt_ref, cw_ref, w_ref, x1_ref, mod_ref, g_ref, dx2_ref, mix_ref,
             du_ref, dx1_ref, dmix_ref, gstat_ref, bstat_ref, cstat_ref):
        i = pl.program_id(0)
        db = dgc_ref[...]
        d = db.astype(F32)
        dprev, dnext = _seq_shifts(db, prev_ref, next_ref, tps)
        g = gt_ref[...].astype(F32)

        @pl.when(i == 0)
        def _():
            cstat_ref[...] = jnp.zeros_like(cstat_ref)

        cstat_ref[1:2, :] += jnp.sum(dnext * g, axis=0, keepdims=True)
        cstat_ref[2:3, :] += jnp.sum(d * g, axis=0, keepdims=True)
        cstat_ref[3:4, :] += jnp.sum(dprev * g, axis=0, keepdims=True)
        dgt = dnext * cw_ref[0:1, :] + d * cw_ref[1:2, :] + dprev * cw_ref[2:3, :]
        du = jnp.concatenate([dval_ref[...], dgt.astype(BF16)], axis=1)
        du_ref[...] = du
        dh2 = _nn(du, w_ref[...])
        xn, r = _rms(x1_ref[...])
        scale1 = 1.0 + mod_ref[0, 4:5, :]
        xg = xn * g_ref[...]
        dx1 = dx2_ref[...] + _rms_bwd(xn, r, dh2 * g_ref[...] * scale1)
        dx1_ref[...] = dx1
        dmix_ref[...] = (mod_ref[0, 2:3, :] * dx1).astype(BF16)

        @pl.when(i == 0)
        def _():
            gstat_ref[...] = jnp.zeros_like(gstat_ref)

        @pl.when(i % tps == 0)
        def _():
            bstat_ref[...] = jnp.zeros_like(bstat_ref)

        gstat_ref[0:1, :] += jnp.sum(dh2 * scale1 * xn, axis=0, keepdims=True)
        bstat_ref[0, 0:1, :] += jnp.sum(dh2, axis=0, keepdims=True)
        bstat_ref[0, 1:2, :] += jnp.sum(dh2 * xg, axis=0, keepdims=True)
        bstat_ref[0, 2:3, :] += jnp.sum(dx1 * mix_ref[...], axis=0, keepdims=True)

    return _hosted_call(
        body, "ffn_up_bwd", (T // TOKEN_TILE,),
        [_tok_spec(F), prev, nxt, _tok_spec(F), _tok_spec(F), _full(conv_w.shape), _full(w_up.shape), _tok_spec(D),
         _mod_spec(tps, D), _full((1, D)), _tok_spec(D), _tok_spec(D)],
        [_tok_spec(2 * F), _tok_spec(D), _tok_spec(D), _full((8, D)), _bstat_spec(tps, D), _full((8, F))],
        [jax.ShapeDtypeStruct((T, 2 * F), BF16), jax.ShapeDtypeStruct((T, D), F32), jax.ShapeDtypeStruct((T, D), BF16),
         jax.ShapeDtypeStruct((8, D), F32), jax.ShapeDtypeStruct((B, 8, D), F32), jax.ShapeDtypeStruct((8, F), F32)],
        (dgc, dgc, dgc, dval, gt, conv_w, w_up, x1, mod3, g_ffn, dx2, mix), tasks)


def _attn_out_bwd(dmix, w_out, oa, ob, g_na, g_sw, tasks=()):
    T, D = dmix.shape

    def body(dmix_ref, w_ref, oa_ref, ob_ref, gna_ref, gsw_ref, doa_ref, dob_ref, gstat_ref):
        dmixin = _nt(dmix_ref[...], w_ref[...])

        @pl.when(pl.program_id(0) == 0)
        def _():
            gstat_ref[...] = jnp.zeros_like(gstat_ref)

        for k, (o_ref, g_ref, do_ref) in enumerate(((oa_ref, gna_ref, doa_ref), (ob_ref, gsw_ref, dob_ref))):
            dn = dmixin[:, k * NA_WIDTH:(k + 1) * NA_WIDTH]
            on, r = _rms(o_ref[...])
            gstat_ref[k:k + 1, :] += jnp.sum(dn * on, axis=0, keepdims=True)
            do_ref[...] = _rms_bwd(on, r, dn * g_ref[...]).astype(BF16)

    hs = jax.ShapeDtypeStruct((T, NA_WIDTH), BF16)
    return _hosted_call(
        body, "attn_out_bwd", (T // TOKEN_TILE,),
        [_tok_spec(D), _full(w_out.shape), _tok_spec(NA_WIDTH), _tok_spec(SW_WIDTH), _full((1, NA_WIDTH)), _full((1, SW_WIDTH))],
        [_tok_spec(NA_WIDTH), _tok_spec(SW_WIDTH), _full((8, NA_WIDTH))],
        [hs, hs, jax.ShapeDtypeStruct((8, NA_WIDTH), F32)],
        (dmix, w_out, oa, ob, g_na, g_sw), tasks)


def _attn_in_bwd(dqa, dka, dva, dqb, dkb, dvb, cos_t, sin_t, w_in, x2d, mod3, g_attn, dx1, B, S):
    T, D = x2d.shape
    tps = S // TOKEN_TILE

    def body(dqa_ref, dka_ref, dva_ref, dqb_ref, dkb_ref, dvb_ref, cos_ref, sin_ref, w_ref, x_ref, mod_ref, g_ref, dx1_ref,
             gx_ref, dproj_ref, gstat_ref, bstat_ref):
        i = pl.program_id(0)
        drb = jnp.concatenate([dqb_ref[...] * Q_SCALE, dkb_ref[...]], axis=1)
        drb = drb * cos_ref[...] + _rot_half(drb * sin_ref[...])
        dproj = jnp.concatenate([dqa_ref[...] * Q_SCALE, dka_ref[...], dva_ref[...], drb, dvb_ref[...]], axis=1).astype(BF16)
        dproj_ref[...] = dproj
        dh = _nn(dproj, w_ref[...])
        xn, r = _rms(x_ref[...])
        scale1 = 1.0 + mod_ref[0, 1:2, :]
        gx_ref[...] = dx1_ref[...] + _rms_bwd(xn, r, dh * g_ref[...] * scale1)

        @pl.when(i == 0)
        def _():
            gstat_ref[...] = jnp.zeros_like(gstat_ref)

        @pl.when(i % tps == 0)
        def _():
            bstat_ref[...] = jnp.zeros_like(bstat_ref)

        gstat_ref[0:1, :] += jnp.sum(dh * scale1 * xn, axis=0, keepdims=True)
        bstat_ref[0, 0:1, :] += jnp.sum(dh, axis=0, keepdims=True)
        bstat_ref[0, 1:2, :] += jnp.sum(dh * (xn * g_ref[...]), axis=0, keepdims=True)

    rope = pl.BlockSpec((TOKEN_TILE, ROPE_WIDTH), lambda i: (i % tps, 0))
    return pl.pallas_call(
        body, name="attn_in_bwd", grid=(T // TOKEN_TILE,),
        in_specs=[_tok_spec(NA_WIDTH), _tok_spec(NA_WIDTH), _tok_spec(NA_WIDTH), _tok_spec(SW_WIDTH), _tok_spec(SW_KV_WIDTH),
                  _tok_spec(SW_KV_WIDTH), rope, rope, _full(w_in.shape), _tok_spec(D), _mod_spec(tps, D), _full((1, D)), _tok_spec(D)],
        out_specs=[_tok_spec(D), _tok_spec(IN_WIDTH), _full((8, D)), _bstat_spec(tps, D)],
        out_shape=[jax.ShapeDtypeStruct((T, D), F32), jax.ShapeDtypeStruct((T, IN_WIDTH), BF16),
                   jax.ShapeDtypeStruct((8, D), F32), jax.ShapeDtypeStruct((B, 8, D), F32)],
        compiler_params=_params("arbitrary"),
    )(dqa, dka, dva, dqb, dkb, dvb, cos_t, sin_t, w_in, x2d, mod3, g_attn, dx1)


def _matmul_tn(a, b, name, tm=None, tk=512):
    T, M = a.shape
    N = b.shape[1]
    tm = M if tm is None else tm
    nk = T // tk

    def body(a_ref, b_ref, o_ref, acc):
        k = pl.program_id(1)

        @pl.when(k == 0)
        def _():
            acc[...] = jnp.zeros_like(acc)

        acc[...] += _tn(a_ref[...], b_ref[...])

        @pl.when(k == nk - 1)
        def _():
            o_ref[...] = acc[...].astype(BF16)

    return pl.pallas_call(
        body, name=name, grid=(M // tm, nk),
        in_specs=[pl.BlockSpec((tk, tm), lambda i, k: (k, i)), pl.BlockSpec((tk, N), lambda i, k: (k, 0))],
        out_specs=pl.BlockSpec((tm, N), lambda i, k: (i, 0)),
        out_shape=jax.ShapeDtypeStruct((M, N), BF16),
        scratch_shapes=[pltpu.VMEM((tm, N), F32)],
        compiler_params=_params("parallel", "arbitrary"),
    )(a, b)


def _na_geometry(S):
    rows = S // GRID_W
    wr = min(NA_ROWS_MAX, rows)
    return rows, wr


def _na_window(r, rows, wr):
    rs = jnp.clip(r - wr // 2, 0, rows - wr)
    return pl.multiple_of(rs * GRID_W, GRID_W), pl.multiple_of((rs - r + NA_ROWS_MAX - 1) * GRID_W, GRID_W)


NA_STEP_PAIRS = 2
NA_GW = NA_STEP_PAIRS * 128


def _na_specs(S, kw_n, order):
    ng = NA_PAIRS // NA_STEP_PAIRS

    def col(k):
        return pl.BlockSpec((1, S, NA_GW), lambda *ids: (order(*ids)[0], 0, k * ng + order(*ids)[1]))
    bias = pl.BlockSpec((NA_STEP_PAIRS, N_DR * GRID_W, 128), lambda *ids: (order(*ids)[1], 0, 0))
    out = pl.BlockSpec((1, S, NA_GW), lambda *ids: (order(*ids)[0], 0, order(*ids)[1]))
    return col(0), col(1), col(2), bias, out


def _block_diag(t):
    left = lax.broadcasted_iota(jnp.int32, t.shape, 1) < HEAD_DIM
    zero = jnp.zeros_like(t)
    return jnp.concatenate([jnp.where(left, t, zero), jnp.where(left, zero, t)], axis=0)


def _diag_blocks(res):
    left = lax.broadcasted_iota(jnp.int32, (HEAD_DIM, 128), 1) < HEAD_DIM
    return jnp.where(left, res[:HEAD_DIM], res[HEAD_DIM:])


def _col_softmax(st):
    e = jnp.exp(st - jnp.max(st, axis=0, keepdims=True))
    return e * (1.0 / jnp.sum(e, axis=0, keepdims=True))


def _na_fwd(qkv, bias, tasks=()):
    B, S, _ = qkv.shape
    rows, wr = _na_geometry(S)
    kw_n = wr * GRID_W

    def body(q_ref, k_ref, v_ref, b_ref, o_ref):
        def step(r, carry):
            start, boff = _na_window(r, rows, wr)
            qrows = pl.ds(pl.multiple_of(r * GRID_W, GRID_W), GRID_W)
            krows = pl.ds(start, kw_n)
            brows = pl.ds(boff, kw_n)
            lanes = [pl.ds(p * 128, 128) for p in range(NA_STEP_PAIRS)]
            st = [_nt(k_ref[0, krows, ln], _block_diag(q_ref[0, qrows, ln])) for ln in lanes]
            pn = [_col_softmax(st[p] + b_ref[p, brows, :]).astype(BF16) for p in range(NA_STEP_PAIRS)]
            outs = [_diag_blocks(_tn(pn[p], v_ref[0, krows, lanes[p]])) for p in range(NA_STEP_PAIRS)]
            o_ref[0, qrows, :] = jnp.concatenate(outs, axis=1)
            return carry

        lax.fori_loop(0, rows, step, 0)

    q, k, v, bs, out = _na_specs(S, kw_n, lambda b, g: (b, g))
    return _hosted_call(body, "na_fwd", (B, NA_PAIRS // NA_STEP_PAIRS), [q, k, v, bs], [out],
                        [jax.ShapeDtypeStruct((B, S, NA_WIDTH), F32)], (qkv, qkv, qkv, bias), tasks)


def _na_bwd(qkv, bias, doa, tasks=()):
    B, S, _ = qkv.shape
    rows, wr = _na_geometry(S)
    kw_n = wr * GRID_W

    def body(q_ref, k_ref, v_ref, b_ref, do_ref, dq_ref, dk_ref, dv_ref, db_ref):
        @pl.when(pl.program_id(1) == 0)
        def _():
            db_ref[...] = jnp.zeros_like(db_ref)

        dk_ref[...] = jnp.zeros_like(dk_ref)
        dv_ref[...] = jnp.zeros_like(dv_ref)

        def step(r, carry):
            start, boff = _na_window(r, rows, wr)
            qrows = pl.ds(pl.multiple_of(r * GRID_W, GRID_W), GRID_W)
            krows = pl.ds(start, kw_n)
            brows = pl.ds(boff, kw_n)
            pairs = range(NA_STEP_PAIRS)
            lanes = [pl.ds(p * 128, 128) for p in pairs]
            kp = [k_ref[0, krows, ln] for ln in lanes]
            qbd = [_block_diag(q_ref[0, qrows, ln]) for ln in lanes]
            dobd = [_block_diag(do_ref[0, qrows, ln]) for ln in lanes]
            st = [_nt(kp[p], qbd[p]) for p in pairs]
            dpt = [_nt(v_ref[0, krows, lanes[p]], dobd[p]) for p in pairs]
            pn = [_col_softmax(st[p] + b_ref[p, brows, :]) for p in pairs]
            dst = [pn[p] * (dpt[p] - jnp.sum(pn[p] * dpt[p], axis=0, keepdims=True)) for p in pairs]
            dsb = [d.astype(BF16) for d in dst]
            dq_ref[0, qrows, :] = jnp.concatenate([_diag_blocks(_tn(dsb[p], kp[p])) for p in pairs], axis=1)
            dk_ref[0, krows, :] += jnp.concatenate([_nn(dsb[p], qbd[p]) for p in pairs], axis=1)
            dv_ref[0, krows, :] += jnp.concatenate([_nn(pn[p].astype(BF16), dobd[p]) for p in pairs], axis=1)
            for p in pairs:
                db_ref[p, brows, :] += dst[p]
            return carry

        lax.fori_loop(0, rows, step, 0)

    q, k, v, bs, out = _na_specs(S, kw_n, lambda g, b: (b, g))
    hs = jax.ShapeDtypeStruct((B, S, NA_WIDTH), F32)
    return _hosted_call(body, "na_bwd", (NA_PAIRS // NA_STEP_PAIRS, B), [q, k, v, bs, out], [out, out, out, bs],
                        [hs, hs, hs, jax.ShapeDtypeStruct((NA_PAIRS, N_DR * GRID_W, 128), F32)], (qkv, qkv, qkv, bias, doa), tasks)


SW_PAIRS = SW_HEADS // 2


def _sw_band(n, S):
    kw_n = 3 * SW_BLOCK
    start = pl.multiple_of(jnp.clip(n * SW_BLOCK - SW_BLOCK, 0, S - kw_n), SW_BLOCK)
    kpos = start + lax.broadcasted_iota(jnp.int32, (kw_n, SW_BLOCK), 0)
    qpos = n * SW_BLOCK + lax.broadcasted_iota(jnp.int32, (kw_n, SW_BLOCK), 1)
    return start, jnp.abs(qpos - kpos) <= SW_WINDOW


def _kv_halves(t):
    left = lax.broadcasted_iota(jnp.int32, t.shape, 1) < HEAD_DIM
    swapped = pltpu.roll(t, HEAD_DIM, axis=1)
    zero = jnp.zeros_like(t)
    return {(0, 0): jnp.where(left, t, zero), (0, 1): jnp.where(left, zero, swapped),
            (1, 0): jnp.where(left, swapped, zero), (1, 1): jnp.where(left, zero, t)}


def _sw_probs(st, ok, sk):
    st = jnp.where(ok, st, NEG)
    m = jnp.maximum(jnp.max(st, axis=0, keepdims=True), sk)
    e = jnp.exp(st - m)
    esk = jnp.exp(sk - m)
    inv = 1.0 / (jnp.sum(e, axis=0, keepdims=True) + esk)
    return e * inv, esk * inv


def _sw_specs(S):
    q = pl.BlockSpec((1, S, SW_WIDTH), lambda b, *_: (b, 0, ROPE_LO // SW_WIDTH))
    k = pl.BlockSpec((1, S, SW_KV_WIDTH), lambda b, *_: (b, 0, (ROPE_LO + SW_WIDTH) // SW_KV_WIDTH))
    v = pl.BlockSpec((1, S, SW_KV_WIDTH), lambda b, *_: (b, 0, (ROPE_LO + ROPE_WIDTH) // SW_KV_WIDTH))
    return q, k, v


SW_FWD_SPLIT = 2


def _sw_fwd(sink, qkv, tasks=()):
    B, S, _ = qkv.shape
    kw_n = 3 * SW_BLOCK

    def body(sink_ref, q_ref, k_ref, v_ref, o_ref):
        def step(n, carry):
            start, ok = _sw_band(n, S)
            qrows = pl.ds(pl.multiple_of(n * SW_BLOCK, SW_BLOCK), SW_BLOCK)
            krows = pl.ds(start, kw_n)
            kh, vh = _kv_halves(k_ref[0, krows, :]), _kv_halves(v_ref[0, krows, :])
            heads = [(p, e) for p in range(SW_PAIRS) for e in range(2)]
            qp = [q_ref[0, qrows, pl.ds(p * 128, 128)] for p in range(SW_PAIRS)]
            kv_of = lambda p: p // (SW_PAIRS // SW_KV_HEADS)
            st = {(p, e): _nt(kh[(kv_of(p), e)], qp[p]) for p, e in heads}
            pn = {(p, e): _sw_probs(st[(p, e)], ok, sink_ref[2 * p + e])[0].astype(BF16) for p, e in heads}
            outs = [_tn(pn[(p, 0)], vh[(kv_of(p), 0)]) + _tn(pn[(p, 1)], vh[(kv_of(p), 1)]) for p in range(SW_PAIRS)]
            o_ref[0, qrows, :] = jnp.concatenate(outs, axis=1)
            return carry

        half = (S // SW_BLOCK) // SW_FWD_SPLIT
        lax.fori_loop(pl.program_id(1) * half, (pl.program_id(1) + 1) * half, step, 0)

    q, k, v = _sw_specs(S)
    return _hosted_call(
        body, "sw_fwd", (B, SW_FWD_SPLIT), [pl.BlockSpec(memory_space=pltpu.SMEM), q, k, v],
        [pl.BlockSpec((1, S, SW_WIDTH), lambda b, s: (b, 0, 0))], [jax.ShapeDtypeStruct((B, S, SW_WIDTH), F32)],
        (sink, qkv, qkv, qkv), tasks)


def _sw_bwd(sink, qkv, dob):
    B, S, _ = qkv.shape
    kw_n = 3 * SW_BLOCK

    fold_rows = 256

    def body(sink_ref, q_ref, k_ref, v_ref, do_ref, dq_ref, dk_ref, dv_ref, dsink_ref, dk_acc, dv_acc):
        @pl.when(pl.program_id(0) == 0)
        def _():
            dsink_ref[...] = jnp.zeros_like(dsink_ref)

        dk_acc[...] = jnp.zeros_like(dk_acc)
        dv_acc[...] = jnp.zeros_like(dv_acc)
        ppk = SW_PAIRS // SW_KV_HEADS

        def step(n, carry):
            start, ok = _sw_band(n, S)
            qrows = pl.ds(pl.multiple_of(n * SW_BLOCK, SW_BLOCK), SW_BLOCK)
            krows = pl.ds(start, kw_n)
            kh, vh = _kv_halves(k_ref[0, krows, :]), _kv_halves(v_ref[0, krows, :])
            heads = [(p, e) for p in range(SW_PAIRS) for e in range(2)]
            qp = [q_ref[0, qrows, pl.ds(p * 128, 128)] for p in range(SW_PAIRS)]
            dop = [do_ref[0, qrows, pl.ds(p * 128, 128)] for p in range(SW_PAIRS)]
            st = {(p, e): _nt(kh[(p // ppk, e)], qp[p]) for p, e in heads}
            dpt = {(p, e): _nt(vh[(p // ppk, e)], dop[p]) for p, e in heads}
            pnb, dsb = {}, {}
            for p, e in heads:
                pn, psink = _sw_probs(st[(p, e)], ok, sink_ref[2 * p + e])
                delta = jnp.sum(pn * dpt[(p, e)], axis=0, keepdims=True)
                dsb[(p, e)] = (pn * (dpt[(p, e)] - delta)).astype(BF16)
                pnb[(p, e)] = pn.astype(BF16)
                dsink_ref[2 * p + e:2 * p + e + 1, :] += -(psink * delta)
            dq_ref[0, qrows, :] = jnp.concatenate(
                [_tn(dsb[(p, 0)], kh[(p // ppk, 0)]) + _tn(dsb[(p, 1)], kh[(p // ppk, 1)]) for p in range(SW_PAIRS)], axis=1)
            left = lax.broadcasted_iota(jnp.int32, (kw_n, 128), 1) < HEAD_DIM
            dks, dvs = [], []
            for kv in range(SW_KV_HEADS):
                dk = dv = None
                for p in range(kv * ppk, (kv + 1) * ppk):
                    dk_p = jnp.where(left, _nn(dsb[(p, 0)], qp[p]), _nn(dsb[(p, 1)], qp[p]))
                    dv_p = jnp.where(left, _nn(pnb[(p, 0)], dop[p]), _nn(pnb[(p, 1)], dop[p]))
                    dk = dk_p if dk is None else dk + dk_p
                    dv = dv_p if dv is None else dv + dv_p
                dks.append(dk)
                dvs.append(dv)
            dk_acc[krows, :] += jnp.concatenate(dks, axis=1)
            dv_acc[krows, :] += jnp.concatenate(dvs, axis=1)
            return carry

        lax.fori_loop(0, S // SW_BLOCK, step, 0)

        def fold(i, carry):
            rows = pl.ds(pl.multiple_of(i * fold_rows, fold_rows), fold_rows)
            left = lax.broadcasted_iota(jnp.int32, (fold_rows, 128), 1) < HEAD_DIM
            for acc, out_ref in ((dk_acc, dk_ref), (dv_acc, dv_ref)):
                a, b = acc[rows, 0:128], acc[rows, 128:256]
                out_ref[0, rows, :] = jnp.where(left, a + pltpu.roll(a, HEAD_DIM, axis=1), b + pltpu.roll(b, HEAD_DIM, axis=1))
            return carry

        lax.fori_loop(0, S // fold_rows, fold, 0)

        @pl.when(pl.program_id(0) == B - 1)
        def _():
            dsink_ref[...] = jnp.broadcast_to(jnp.sum(dsink_ref[...], axis=1, keepdims=True), dsink_ref.shape)

    q, k, v = _sw_specs(S)
    qo = pl.BlockSpec((1, S, SW_WIDTH), lambda b: (b, 0, 0))
    ko = pl.BlockSpec((1, S, SW_KV_WIDTH), lambda b: (b, 0, 0))
    return pl.pallas_call(
        body, name="sw_bwd", grid=(B,),
        in_specs=[pl.BlockSpec(memory_space=pltpu.SMEM), q, k, v, qo],
        out_specs=[qo, ko, ko, _full((SW_HEADS, 128))],
        out_shape=[jax.ShapeDtypeStruct((B, S, SW_WIDTH), F32), jax.ShapeDtypeStruct((B, S, SW_KV_WIDTH), F32),
                   jax.ShapeDtypeStruct((B, S, SW_KV_WIDTH), F32), jax.ShapeDtypeStruct((SW_HEADS, 128), F32)],
        scratch_shapes=[pltpu.VMEM((S, 2 * SW_KV_WIDTH), F32), pltpu.VMEM((S, 2 * SW_KV_WIDTH), F32)],
        compiler_params=_params("arbitrary"),
    )(sink, qkv, qkv, qkv, dob)


def _pack_sum(packs):
    W = packs.shape[1]

    def body(p_ref, o_ref, s_ref):
        tot = p_ref[0:8, :]
        for d in range(1, N_DEV):
            tot = tot + p_ref[8 * d:8 * d + 8, :]
        o_ref[...] = tot
        s_ref[...] = tot[0:1, :] + tot[1:2, :]

    return pl.pallas_call(body, name="pack_sum", out_shape=[jax.ShapeDtypeStruct((8, W), F32), jax.ShapeDtypeStruct((1, W), F32)],
                          compiler_params=pltpu.CompilerParams(vmem_limit_bytes=VMEM_LIMIT))(packs)


def _adam_update(w, g, m, v):
    c1 = 1.0 - ADAM_B1 ** ADAM_STEP
    c2 = 1.0 - ADAM_B2 ** ADAM_STEP
    nm = ADAM_B1 * m + (1.0 - ADAM_B1) * g
    nv = ADAM_B2 * v + (1.0 - ADAM_B2) * (g * g)
    return -ADAM_LR * ((nm / c1) / (jnp.sqrt(nv / c2) + ADAM_EPS) + ADAM_WD * w), nm, nv


def _adamw(w, g, m, v, name):
    def body(w_ref, g_ref, m_ref, v_ref, d_ref, nm_ref, nv_ref):
        d_ref[...], nm_ref[...], nv_ref[...] = _adam_update(w_ref[...], g_ref[...], m_ref[...], v_ref[...])

    s = jax.ShapeDtypeStruct(w.shape, F32)
    return pl.pallas_call(body, name=name, out_shape=[s, s, s],
                          compiler_params=pltpu.CompilerParams(vmem_limit_bytes=VMEM_LIMIT))(w, g, m, v)


def _sum_adamw(own, recvb, w, m, v, name):
    R, C = own.shape
    rc = _row_chunk(R)

    def body(own_ref, r_ref, w_ref, m_ref, v_ref, g_ref, d_ref, nm_ref, nv_ref):
        def chunk(i, carry):
            rows = pl.ds(pl.multiple_of(i * rc, rc), rc)
            g = own_ref[rows, :]
            for j in range(3):
                g = g + r_ref[j, rows, :].astype(F32)
            g_ref[rows, :] = g
            d_ref[rows, :], nm_ref[rows, :], nv_ref[rows, :] = _adam_update(w_ref[rows, :], g, m_ref[rows, :], v_ref[rows, :])
            return carry

        lax.fori_loop(0, R // rc, chunk, 0)

    s = jax.ShapeDtypeStruct((R, C), F32)
    return pl.pallas_call(body, name=name, out_shape=[s, s, s, s],
                          compiler_params=pltpu.CompilerParams(vmem_limit_bytes=VMEM_LIMIT))(own, recvb, w, m, v)


def _by_device(dw):
    return dw.reshape(N_DEV, dw.shape[0] // N_DEV, dw.shape[1])


def _local_step(x, mod, g_attn, w_in, bias, sw_sink, g_na_out, g_sw_out, w_out, g_ffn, w_up, conv_w, conv_b, w_down,
                g_final, target, sharded):
    B, S, D = x.shape
    T = B * S
    x2d = x.reshape(T, D)
    mod3 = mod.reshape(B, 6, D)
    cos_t, sin_t = _rope_tables(S)
    sink = sw_sink.reshape(SW_HEADS)
    n_tiles = T // TOKEN_TILE
    full = lambda g: g.reshape(N_DEV * g.shape[1], g.shape[2])

    (h, qkv), got = _attn_in(x2d, mod3, g_attn, w_in, cos_t, sin_t, S, [_gather_task(w_out, n_tiles // 2)] if sharded else [])
    if sharded:
        w_out = full(got[0][0])
    qkv3 = qkv.reshape(B, S, IN_WIDTH)
    na_steps = B * (NA_PAIRS // NA_STEP_PAIRS)
    (oa,), got = _na_fwd(qkv3, bias, [_gather_task(w_up, na_steps - 1)] if sharded else [])
    if sharded:
        w_up = full(got[0][0])
    oa = oa.reshape(T, NA_WIDTH)
    (ob,), got = _sw_fwd(sink, qkv3, [_gather_task(w_down, B * SW_FWD_SPLIT - 1)] if sharded else [])
    if sharded:
        w_down = full(got[0][0])
    ob = ob.reshape(T, SW_WIDTH)
    mixin, mix, x1 = _attn_out(oa, ob, x2d, mod3, g_na_out, g_sw_out, w_out, S)
    h2, val, gt = _ffn_up(x1, mod3, g_ffn, w_up, S)
    a, act, vd, dx2, df, gstat_f, bstat_f = _ffn_down(gt, val, conv_w, conv_b, w_down, x1, mod3, g_final, target.reshape(T, D), B, S)
    F = val.shape[1]

    dw_down = _matmul_tn(a, df, "dw_down")
    (dval, dgc, cstat), got = _ffn_down_bwd(df, w_down, act, vd, [_swap_task(_by_device(dw_down))] if sharded else [])
    if sharded:
        send_down, own_down = _chip_sums(_by_device(dw_down), got[0][0])
    (du, dx1, dmix, gstat_u, bstat_u, cstat_w), got = _ffn_up_bwd(dgc, dval, gt, conv_w, w_up, x1, mod3, g_ffn, dx2, mix, B, S,
                                                                  [_exchange_task(send_down)] if sharded else [])
    if sharded:
        dw_down = (own_down, got[0][0])
    dw_up = _matmul_tn(du, h2, "dw_up", tm=F)
    dw_out = _matmul_tn(mixin, dmix, "dw_out")
    (doa, dob, gstat_o), got = _attn_out_bwd(dmix, w_out, oa, ob, g_na_out, g_sw_out,
                                             [_swap_task(_by_device(dw_up)), _swap_task(_by_device(dw_out))] if sharded else [])
    if sharded:
        send_up, own_up = _chip_sums(_by_device(dw_up), got[0][0])
        send_out, own_out = _chip_sums(_by_device(dw_out), got[1][0])
    (dqa, dka, dva, dbt), got = _na_bwd(qkv3, bias, doa.reshape(B, S, NA_WIDTH),
                                        [_exchange_task(send_up), _exchange_task(send_out)] if sharded else [])
    if sharded:
        dw_up, dw_out = (own_up, got[0][0]), (own_out, got[1][0])
    dqb, dkb, dvb, dsink = _sw_bwd(sink, qkv3, dob.reshape(B, S, SW_WIDTH))
    r2 = lambda t: t.reshape(T, t.shape[-1])
    grad_x, dproj, gstat_i, bstat_i = _attn_in_bwd(r2(dqa), r2(dka), r2(dva), r2(dqb), r2(dkb), r2(dvb), cos_t, sin_t, w_in, x2d, mod3,
                                                   g_attn, dx1, B, S)
    dw_in = _matmul_tn(dproj, h, "dw_in")

    dmod = jnp.stack([bstat_i[:, 0], bstat_i[:, 1], bstat_u[:, 2], bstat_u[:, 0], bstat_u[:, 1], bstat_f[:, 0]], axis=1)
    small = dict(g_attn=gstat_i[0], g_ffn=gstat_u[0], g_final=gstat_f[0], loss=gstat_f[1, 0], g_na_out=gstat_o[0], g_sw_out=gstat_o[1],
                 sw_sink=dsink[:, 0], conv_b=cstat[0], conv_w=cstat_w[1:4], dbt=dbt)
    return grad_x.reshape(B, S, D), dict(w_in=dw_in, w_out=dw_out, w_up=dw_up, w_down=dw_down), dmod, small


def _pad_lanes(v, w):
    return jnp.pad(v, (0, w - v.shape[0]))


def kernel(x, c, w_ada, b_ada, g_attn, w_in, na_rpb, sw_sink, g_na_out, g_sw_out, w_out, g_ffn, w_up, conv_w, conv_b, w_down, g_final, loss_target, m_w_ada, m_b_ada, m_g_attn, m_w_in, m_na_rpb, m_sw_sink, m_g_na_out, m_g_sw_out, m_w_out, m_g_ffn, m_w_up, m_conv_w, m_conv_b, m_w_down, m_g_final, v_w_ada, v_b_ada, v_g_attn, v_w_in, v_na_rpb, v_sw_sink, v_g_na_out, v_g_sw_out, v_w_out, v_g_ffn, v_w_up, v_conv_w, v_conv_b, v_w_down, v_g_final):
    B, S, D = x.shape
    me = 4 * lax.axis_index("x") + 2 * lax.axis_index("y") + lax.axis_index("c")
    ada_c = w_ada.shape[2]
    F_l = conv_w.shape[2]
    onehot = _toeplitz_onehot()

    cw_l = jnp.pad(conv_w[0], ((0, 8 - conv_w.shape[1]), (0, 0)))
    c_l = jnp.pad(c, ((0, 8 - B), (0, 0)))
    slabs = _all_gather(jnp.concatenate([c_l, cw_l], axis=1), "gather_c")
    c_all = slabs[:, :, :D].reshape(N_DEV * 8, D)
    conv_w_f = jnp.transpose(slabs[:, :3, D:], (1, 0, 2)).reshape(3, N_DEV * F_l)

    b_ada_l = lax.dynamic_slice(b_ada, (0, me * ada_c), (1, ada_c))
    mod_cols = _ada_fwd(c_all, w_ada[0], b_ada_l)
    mod_all = _all_gather(mod_cols, "gather_mod")
    mod_mine = lax.dynamic_slice(mod_all, (0, me * 8, 0), (N_DEV, B, ada_c))
    mod = jnp.transpose(mod_mine, (1, 0, 2)).reshape(B, N_DEV * ada_c)

    tr = {"w_in", "w_up"}
    w_in_t = jnp.transpose(w_in[0])
    w_in_f = _all_gather(w_in_t.astype(BF16), "gather_w_in").reshape(N_DEV * w_in_t.shape[0], D)
    shards = dict(w_out=w_out[0].astype(BF16), w_up=jnp.transpose(w_up[0]).astype(BF16), w_down=w_down[0].astype(BF16))

    bias = _na_bias_table(na_rpb[0], onehot)

    grad_x, dw, dmod, small = _local_step(x, mod, g_attn, w_in_f, bias, sw_sink, g_na_out, g_sw_out, shards["w_out"], g_ffn,
                                          shards["w_up"], conv_w_f, conv_b, shards["w_down"], g_final.reshape(1, D), loss_target,
                                          sharded=True)
    g_w_in = _reduce_scatter(_by_device(dw["w_in"]), "scatter_w_in")

    drpb = _na_bias_grad(small["dbt"], onehot)

    row2 = jnp.concatenate([small["g_attn"], small["g_ffn"], small["g_final"], small["g_na_out"], small["g_sw_out"],
                            _pad_lanes(small["sw_sink"], 128), _pad_lanes(small["loss"].reshape(1), 128)])
    rows = [dmod.reshape(B, 6 * D)[0], dmod.reshape(B, 6 * D)[1], _pad_lanes(row2, PACK_W), _pad_lanes(small["conv_b"], PACK_W),
            _pad_lanes(drpb, PACK_W)] + [_pad_lanes(small["conv_w"][k], PACK_W) for k in range(3)]
    packs = _all_gather(jnp.stack(rows), "gather_small")
    tot, g_b_ada = _pack_sum(packs.reshape(N_DEV * 8, PACK_W))

    o = 0
    rep = {}
    for nm, wd in (("g_attn", D), ("g_ffn", D), ("g_final", D), ("g_na_out", NA_WIDTH), ("g_sw_out", SW_WIDTH), ("sw_sink", 128), ("loss", 128)):
        rep[nm] = tot[2, o:o + wd]
        o += wd
    loss = rep["loss"][0]
    g_conv_b = tot[3:4, :conv_b.shape[1]]
    g_na_rpb = tot[4, :NA_HEADS * N_DR * N_DC].reshape(na_rpb.shape)
    g_conv_w = lax.dynamic_slice(tot[5:8], (0, me * F_l), (3, F_l)).reshape(conv_w.shape)

    dmod_cols = lax.dynamic_slice(packs.reshape(N_DEV * 8, PACK_W), (0, me * ada_c), (N_DEV * 8, ada_c))
    g_w_ada = _ada_bwd(c_all, dmod_cols)[None]

    grads = dict(
        w_ada=g_w_ada, b_ada=g_b_ada, g_attn=rep["g_attn"][None], w_in=g_w_in, na_rpb=g_na_rpb,
        sw_sink=rep["sw_sink"][None, :SW_HEADS], g_na_out=rep["g_na_out"][None], g_sw_out=rep["g_sw_out"][None],
        w_out=None, g_ffn=rep["g_ffn"][None], w_up=None, conv_w=g_conv_w, conv_b=g_conv_b, w_down=None, g_final=rep["g_final"])
    weights = dict(w_ada=w_ada, b_ada=b_ada, g_attn=g_attn, w_in=w_in, na_rpb=na_rpb, sw_sink=sw_sink, g_na_out=g_na_out,
                   g_sw_out=g_sw_out, w_out=w_out, g_ffn=g_ffn, w_up=w_up, conv_w=conv_w, conv_b=conv_b, w_down=w_down, g_final=g_final)
    ms = dict(w_ada=m_w_ada, b_ada=m_b_ada, g_attn=m_g_attn, w_in=m_w_in, na_rpb=m_na_rpb, sw_sink=m_sw_sink, g_na_out=m_g_na_out,
              g_sw_out=m_g_sw_out, w_out=m_w_out, g_ffn=m_g_ffn, w_up=m_w_up, conv_w=m_conv_w, conv_b=m_conv_b, w_down=m_w_down, g_final=m_g_final)
    vs = dict(w_ada=v_w_ada, b_ada=v_b_ada, g_attn=v_g_attn, w_in=v_w_in, na_rpb=v_na_rpb, sw_sink=v_sw_sink, g_na_out=v_g_na_out,
              g_sw_out=v_g_sw_out, w_out=v_w_out, g_ffn=v_g_ffn, w_up=v_w_up, conv_w=v_conv_w, conv_b=v_conv_b, w_down=v_w_down, g_final=v_g_final)
    names = list(weights)
    deltas, new_m, new_v = {}, {}, {}
    for nm in names:
        shp = weights[nm].shape
        if nm in tr:
            r = lambda t: jnp.transpose(t[0])
            back = lambda t: jnp.transpose(t)[None]
        else:
            two_d = (shp[-2], shp[-1]) if len(shp) >= 3 and nm != "na_rpb" else (1, int(np.prod(shp)))
            r = lambda t: t.reshape(two_d)
            back = lambda t: t.reshape(shp)
        if nm in ("w_out", "w_up", "w_down"):
            g2, d_, m_, v_ = _sum_adamw(*dw[nm], r(weights[nm]), r(ms[nm]), r(vs[nm]), "adamw_" + nm)
        else:
            g2 = grads[nm] if nm in tr else r(grads[nm])
            d_, m_, v_ = _adamw(r(weights[nm]), g2, r(ms[nm]), r(vs[nm]), "adamw_" + nm)
        deltas[nm], new_m[nm], new_v[nm], grads[nm] = back(d_), back(m_), back(v_), back(g2)
    return (loss, grad_x, *[grads[n] for n in names], *[deltas[n] for n in names], *[new_m[n] for n in names],
            *[new_v[n] for n in names])
```

```python
import functools

import numpy as np
import jax
import jax.numpy as jnp
from jax import lax
from jax.experimental import pallas as pl
from jax.experimental.pallas import tpu as pltpu

F32, BF16 = jnp.float32, jnp.bfloat16
MESH_ID = pl.DeviceIdType.MESH
N_DEV = 8

HEAD_DIM = 64
NA_HEADS = 8
SW_HEADS = 8
SW_KV_HEADS = 2
SW_GROUP = SW_HEADS // SW_KV_HEADS
NA_WIDTH = NA_HEADS * HEAD_DIM
SW_WIDTH = SW_HEADS * HEAD_DIM
SW_KV_WIDTH = SW_KV_HEADS * HEAD_DIM
ROPE_WIDTH = SW_WIDTH + SW_KV_WIDTH
IN_WIDTH = 3 * NA_WIDTH + SW_WIDTH + 2 * SW_KV_WIDTH
ROPE_LO = 3 * NA_WIDTH
GRID_W = 64
NA_ROWS_MAX = 8
NA_COLS = 16
N_DR = 2 * NA_ROWS_MAX - 1
N_DC = 2 * NA_COLS - 1
SW_WINDOW = 128
SW_BLOCK = 128
ROPE_THETA = 10000.0
EPS = 1e-6
NEG = -1e30
Q_SCALE = HEAD_DIM ** -0.5

ADAM_LR = 0.001
ADAM_B1 = 0.9
ADAM_B2 = 0.999
ADAM_EPS = 1e-08
ADAM_WD = 0.01
ADAM_STEP = 10

TOKEN_TILE = 256
VMEM_LIMIT = 56 * 1024 * 1024

PACK_W = 6144


def _nn(a, b):
    return jnp.dot(a, b, preferred_element_type=F32)


def _nt(a, b):
    return lax.dot_general(a, b, (((1,), (1,)), ((), ())), preferred_element_type=F32)


def _tn(a, b):
    return lax.dot_general(a, b, (((0,), (0,)), ((), ())), preferred_element_type=F32)


def _rms(x):
    r = lax.rsqrt(jnp.mean(x * x, axis=-1, keepdims=True) + EPS)
    return x * r, r


def _rms_bwd(xn, r, gy):
    return r * (gy - xn * jnp.mean(xn * gy, axis=-1, keepdims=True))


def _params(*sem):
    return pltpu.CompilerParams(dimension_semantics=sem, vmem_limit_bytes=VMEM_LIMIT)


def _full(shape):
    n = len(shape)
    return pl.BlockSpec(shape, lambda *_: (0,) * n)


def _mesh_pos():
    return lax.axis_index("x"), lax.axis_index("y"), lax.axis_index("c")


def _all_gather(x, name):
    def body(x_ref, out_ref, send_sems, recv_sems, local_sem):
        x_, y_, c_ = _mesh_pos()
        me, sibling = (x_, y_, c_), (x_, y_, 1 - c_)
        chips = [(1 - x_, y_), (x_, 1 - y_), (1 - x_, 1 - y_)]

        def rows(px, py, pc):
            return out_ref.at[4 * px + 2 * py + pc]

        def copy(k, block, to, src=None):
            return pltpu.make_async_remote_copy(
                src_ref=rows(*block) if src is None else src, dst_ref=rows(*block),
                send_sem=send_sems.at[k], recv_sem=recv_sems.at[k], device_id=to, device_id_type=MESH_ID)

        mine = pltpu.make_async_copy(x_ref, rows(*me), local_sem)
        mine.start()
        first = [copy(0, me, sibling, src=x_ref)]
        first += [copy(1 + j, me, (*chip, c_), src=x_ref) for j, chip in enumerate(chips)]
        for cp in first:
            cp.start()
        passed = [copy(4 + j, (*chip, c_), sibling) for j, chip in enumerate(chips)]
        for j, chip in enumerate(chips):
            copy(1 + j, (*chip, c_), me).wait_recv()
            passed[j].start()
        copy(0, sibling, me).wait_recv()
        for j, chip in enumerate(chips):
            copy(4 + j, (*chip, 1 - c_), me).wait_recv()
        for cp in first + passed:
            cp.wait_send()
        mine.wait()

    return pl.pallas_call(
        body, name=name,
        out_shape=jax.ShapeDtypeStruct((N_DEV,) + x.shape, x.dtype),
        in_specs=[pl.BlockSpec(memory_space=pl.ANY)],
        out_specs=pl.BlockSpec(memory_space=pl.ANY),
        scratch_shapes=[pltpu.SemaphoreType.DMA((7,)), pltpu.SemaphoreType.DMA((7,)), pltpu.SemaphoreType.DMA],
    )(x)


def _row_chunk(r):
    for rc in (128, 64, 32, 16):
        if r % rc == 0:
            return rc
    raise ValueError(f"rows {r} not a multiple of 16")


def _reduce_scatter(g8, name):
    _, R, C = g8.shape
    rc = _row_chunk(R)

    def body(g_ref, out_ref, recva, sendb, recvb, sa, ra, sb, rb):
        x_, y_, c_ = _mesh_pos()
        sibling = (x_, y_, 1 - c_)
        copies_a = []
        for k in range(4):
            cp = pltpu.make_async_remote_copy(
                src_ref=g_ref.at[2 * k + (1 - c_)], dst_ref=recva.at[k],
                send_sem=sa.at[k], recv_sem=ra.at[k], device_id=sibling, device_id_type=MESH_ID)
            cp.start()
            copies_a.append(cp)
        for cp in copies_a:
            cp.wait_recv()

        def chip_sum(k, rows):
            return g_ref[2 * k + c_, rows, :].astype(F32) + recva[k, rows, :].astype(F32)

        flips = [(1 - x_, y_), (x_, 1 - y_), (1 - x_, 1 - y_)]
        copies_b = []
        for j, (tx, ty) in enumerate(flips):
            kt = 2 * tx + ty

            def fill(i, carry, j=j, kt=kt):
                rows = pl.ds(pl.multiple_of(i * rc, rc), rc)
                sendb[j, rows, :] = chip_sum(kt, rows).astype(BF16)
                return carry

            lax.fori_loop(0, R // rc, fill, 0)
            cp = pltpu.make_async_remote_copy(
                src_ref=sendb.at[j], dst_ref=recvb.at[j],
                send_sem=sb.at[j], recv_sem=rb.at[j], device_id=(tx, ty, c_), device_id_type=MESH_ID)
            cp.start()
            copies_b.append(cp)
        for cp in copies_b:
            cp.wait_recv()
        kme = 2 * x_ + y_

        def total(i, carry):
            rows = pl.ds(pl.multiple_of(i * rc, rc), rc)
            acc = chip_sum(kme, rows)
            for j in range(3):
                acc = acc + recvb[j, rows, :].astype(F32)
            out_ref[rows, :] = acc
            return carry

        lax.fori_loop(0, R // rc, total, 0)
        for cp in copies_a + copies_b:
            cp.wait_send()

    vm = pl.BlockSpec(memory_space=pltpu.VMEM)
    return pl.pallas_call(
        body, name=name,
        out_shape=jax.ShapeDtypeStruct((R, C), F32),
        in_specs=[vm], out_specs=vm,
        scratch_shapes=[pltpu.VMEM((4, R, C), BF16), pltpu.VMEM((3, R, C), BF16), pltpu.VMEM((3, R, C), BF16),
                        pltpu.SemaphoreType.DMA((4,)), pltpu.SemaphoreType.DMA((4,)),
                        pltpu.SemaphoreType.DMA((3,)), pltpu.SemaphoreType.DMA((3,))],
        compiler_params=pltpu.CompilerParams(vmem_limit_bytes=VMEM_LIMIT),
    )(g8)


class _Task:
    def __init__(self, inputs, out_shapes, sems, start, finish, mid=None, mid_step=None):
        self.inputs, self.out_shapes, self.sems = list(inputs), list(out_shapes), list(sems)
        self.start, self.finish, self.mid, self.mid_step = start, finish, mid, mid_step


def _hosted_call(body, name, grid, in_specs, out_specs, out_shape, operands, tasks, scratch_shapes=()):
    n_in, n_out, n_scr = len(in_specs), len(out_specs), len(scratch_shapes)
    t_in = [len(t.inputs) for t in tasks]
    t_out = [len(t.out_shapes) for t in tasks]
    t_sem = [len(t.sems) for t in tasks]
    n_steps = int(np.prod(grid))

    def wrapped(*refs):
        ins, rest = refs[:n_in], refs[n_in:]
        task_ins, rest = rest[:sum(t_in)], rest[sum(t_in):]
        outs, rest = rest[:n_out], rest[n_out:]
        task_outs, rest = rest[:sum(t_out)], rest[sum(t_out):]
        scr, task_sems = rest[:n_scr], rest[n_scr:]
        step = pl.program_id(0)
        for ax in range(1, len(grid)):
            step = step * grid[ax] + pl.program_id(ax)
        parts = []
        oi = oo = os_ = 0
        for t, a, b, c in zip(tasks, t_in, t_out, t_sem):
            parts.append((t, task_ins[oi:oi + a], task_outs[oo:oo + b], task_sems[os_:os_ + c]))
            oi, oo, os_ = oi + a, oo + b, os_ + c
        for t, ti, to, ts in parts:
            pl.when(step == 0)(functools.partial(t.start, ti, to, ts))
            if t.mid is not None:
                pl.when(step == t.mid_step)(functools.partial(t.mid, ti, to, ts))
        body(*ins, *outs, *scr)
        for t, ti, to, ts in parts:
            pl.when(step == n_steps - 1)(functools.partial(t.finish, ti, to, ts))

    hbm = pl.BlockSpec(memory_space=pl.ANY)
    res = pl.pallas_call(
        wrapped, name=name, grid=grid,
        in_specs=list(in_specs) + [hbm] * sum(t_in),
        out_specs=list(out_specs) + [hbm] * sum(t_out),
        out_shape=list(out_shape) + [s for t in tasks for s in t.out_shapes],
        scratch_shapes=list(scratch_shapes) + [s for t in tasks for s in t.sems],
        compiler_params=_params(*(["arbitrary"] * len(grid))),
    )(*operands, *[a for t in tasks for a in t.inputs])
    own, extra = res[:n_out], res[n_out:]
    per_task, o = [], 0
    for b in t_out:
        per_task.append(extra[o:o + b])
        o += b
    return own, per_task


def _gather_task(shard, mid_step):
    def parts(ins, outs, sems):
        (x_ref,), (out_ref,), (send_sems, recv_sems, local_sem) = ins, outs, sems
        x_, y_, c_ = _mesh_pos()
        me, sibling = (x_, y_, c_), (x_, y_, 1 - c_)
        chips = [(1 - x_, y_), (x_, 1 - y_), (1 - x_, 1 - y_)]

        def rows(px, py, pc):
            return out_ref.at[4 * px + 2 * py + pc]

        def copy(k, block, to, src=None):
            return pltpu.make_async_remote_copy(
                src_ref=rows(*block) if src is None else src, dst_ref=rows(*block),
                send_sem=send_sems.at[k], recv_sem=recv_sems.at[k], device_id=to, device_id_type=MESH_ID)

        return dict(
            mine=lambda: pltpu.make_async_copy(x_ref, rows(*me), local_sem),
            first=lambda: [copy(0, me, sibling, src=x_ref)] + [copy(1 + j, me, (*chip, c_), src=x_ref) for j, chip in enumerate(chips)],
            passed=lambda: [copy(4 + j, (*chip, c_), sibling) for j, chip in enumerate(chips)],
            landed=lambda: [copy(1 + j, (*chip, c_), me) for j, chip in enumerate(chips)],
            last=lambda: [copy(0, sibling, me)] + [copy(4 + j, (*chip, 1 - c_), me) for j, chip in enumerate(chips)])

    def start(ins, outs, sems):
        p = parts(ins, outs, sems)
        p["mine"]().start()
        for cp in p["first"]():
            cp.start()

    def mid(ins, outs, sems):
        p = parts(ins, outs, sems)
        for cp, fw in zip(p["landed"](), p["passed"]()):
            cp.wait_recv()
            fw.start()

    def finish(ins, outs, sems):
        p = parts(ins, outs, sems)
        for cp in p["last"]():
            cp.wait_recv()
        for cp in p["first"]() + p["passed"]():
            cp.wait_send()
        p["mine"]().wait()

    return _Task([shard], [jax.ShapeDtypeStruct((N_DEV,) + shard.shape, shard.dtype)],
                 [pltpu.SemaphoreType.DMA((7,)), pltpu.SemaphoreType.DMA((7,)), pltpu.SemaphoreType.DMA],
                 start, finish, mid, mid_step)


def _swap_task(g8):
    _, R, C = g8.shape

    def copies(ins, outs, sems):
        (g_ref,), (recv_ref,), (ss, rs) = ins, outs, sems
        x_, y_, c_ = _mesh_pos()
        return [pltpu.make_async_remote_copy(src_ref=g_ref.at[2 * k + (1 - c_)], dst_ref=recv_ref.at[k], send_sem=ss.at[k],
                                             recv_sem=rs.at[k], device_id=(x_, y_, 1 - c_), device_id_type=MESH_ID)
                for k in range(4)]

    def start(ins, outs, sems):
        for cp in copies(ins, outs, sems):
            cp.start()

    def finish(ins, outs, sems):
        cps = copies(ins, outs, sems)
        for cp in cps:
            cp.wait_recv()
        for cp in cps:
            cp.wait_send()

    return _Task([g8], [jax.ShapeDtypeStruct((4, R, C), g8.dtype)],
                 [pltpu.SemaphoreType.DMA((4,)), pltpu.SemaphoreType.DMA((4,))], start, finish)


def _chip_sums(g8, recva):
    _, R, C = g8.shape
    rc = _row_chunk(R)

    def body(g_ref, a_ref, send_ref, own_ref):
        x_, y_, c_ = _mesh_pos()
        chips = [(1 - x_, y_), (x_, 1 - y_), (1 - x_, 1 - y_), (x_, y_)]

        def chunk(i, carry):
            rows = pl.ds(pl.multiple_of(i * rc, rc), rc)
            for j, (tx, ty) in enumerate(chips):
                k = 2 * tx + ty
                s = g_ref[2 * k + c_, rows, :].astype(F32) + a_ref[k, rows, :].astype(F32)
                if j < 3:
                    send_ref[j, rows, :] = s.astype(BF16)
                else:
                    own_ref[rows, :] = s
            return carry

        lax.fori_loop(0, R // rc, chunk, 0)

    return pl.pallas_call(body, name="chip_sums", out_shape=[jax.ShapeDtypeStruct((3, R, C), BF16), jax.ShapeDtypeStruct((R, C), F32)],
                          compiler_params=pltpu.CompilerParams(vmem_limit_bytes=VMEM_LIMIT))(g8, recva)


def _exchange_task(sendb):
    def copies(ins, outs, sems):
        (s_ref,), (recv_ref,), (ss, rs) = ins, outs, sems
        x_, y_, c_ = _mesh_pos()
        flips = [(1 - x_, y_), (x_, 1 - y_), (1 - x_, 1 - y_)]
        return [pltpu.make_async_remote_copy(src_ref=s_ref.at[j], dst_ref=recv_ref.at[j], send_sem=ss.at[j], recv_sem=rs.at[j],
                                             device_id=(tx, ty, c_), device_id_type=MESH_ID) for j, (tx, ty) in enumerate(flips)]

    def start(ins, outs, sems):
        for cp in copies(ins, outs, sems):
            cp.start()

    def finish(ins, outs, sems):
        cps = copies(ins, outs, sems)
        for cp in cps:
            cp.wait_recv()
        for cp in cps:
            cp.wait_send()

    return _Task([sendb], [jax.ShapeDtypeStruct(sendb.shape, sendb.dtype)],
                 [pltpu.SemaphoreType.DMA((3,)), pltpu.SemaphoreType.DMA((3,))], start, finish)


def _silu(v):
    return v * (1.0 / (1.0 + jnp.exp(-v)))


def _ada_fwd(c_all, w_ada_l, b_ada_l):
    def body(c_ref, w_ref, b_ref, o_ref):
        o_ref[...] = jnp.dot(_silu(c_ref[...]), w_ref[...], precision=lax.Precision.HIGHEST,
                             preferred_element_type=F32) + b_ref[...]
    return pl.pallas_call(body, name="ada_fwd", out_shape=jax.ShapeDtypeStruct((c_all.shape[0], w_ada_l.shape[1]), F32),
                          compiler_params=pltpu.CompilerParams(vmem_limit_bytes=VMEM_LIMIT))(c_all, w_ada_l, b_ada_l)


def _ada_bwd(c_all, dmod_cols):
    def body(c_ref, d_ref, o_ref):
        o_ref[...] = lax.dot_general(_silu(c_ref[...]), d_ref[...], (((0,), (0,)), ((), ())),
                                     precision=lax.Precision.HIGHEST, preferred_element_type=F32)
    return pl.pallas_call(body, name="ada_bwd", out_shape=jax.ShapeDtypeStruct((c_all.shape[1], dmod_cols.shape[1]), F32),
                          compiler_params=pltpu.CompilerParams(vmem_limit_bytes=VMEM_LIMIT))(c_all, dmod_cols)


def _toeplitz_onehot():
    k = np.arange(GRID_W)[:, None]
    q = np.arange(GRID_W)[None, :]
    dc = np.clip(k - q + NA_COLS - 1, 0, N_DC - 1).reshape(-1)
    e = np.zeros((128, GRID_W * GRID_W), np.float32)
    e[dc, np.arange(GRID_W * GRID_W)] = 1.0
    return jnp.asarray(e)


def _rpb_expand(rpb2d, onehot):
    def body(r_ref, e_ref, o_ref):
        o_ref[...] = jnp.dot(r_ref[...], e_ref[...], precision=lax.Precision.HIGHEST, preferred_element_type=F32)
    return pl.pallas_call(body, name="rpb_expand", out_shape=jax.ShapeDtypeStruct((128, GRID_W * GRID_W), F32),
                          compiler_params=pltpu.CompilerParams(vmem_limit_bytes=VMEM_LIMIT))(rpb2d, onehot)


def _rpb_reduce(dtz2d, onehot):
    def body(d_ref, e_ref, o_ref):
        o_ref[...] = lax.dot_general(d_ref[...], e_ref[...], (((1,), (1,)), ((), ())),
                                     precision=lax.Precision.HIGHEST, preferred_element_type=F32)
    return pl.pallas_call(body, name="rpb_reduce", out_shape=jax.ShapeDtypeStruct((128, 128), F32),
                          compiler_params=pltpu.CompilerParams(vmem_limit_bytes=VMEM_LIMIT))(dtz2d, onehot)


NA_PAIRS = NA_HEADS // 2


def _na_bias_table(na_rpb, onehot):
    rpb2d = jnp.pad(na_rpb.reshape(NA_HEADS * N_DR, N_DC), ((0, 128 - NA_HEADS * N_DR), (0, 128 - N_DC)))
    tz = _rpb_expand(rpb2d, onehot)[:NA_HEADS * N_DR].reshape(NA_PAIRS, 2, N_DR, GRID_W, GRID_W)
    col = np.arange(GRID_W)
    cs = np.clip(col - NA_COLS // 2, 0, GRID_W - NA_COLS)
    ok_kq = ((col[None, :] >= cs[:, None]) & (col[None, :] < cs[:, None] + NA_COLS)).T
    tz = jnp.where(jnp.asarray(ok_kq)[None, None, None], tz, NEG)
    return jnp.transpose(tz, (0, 2, 3, 1, 4)).reshape(NA_PAIRS, N_DR * GRID_W, 128)


def _na_bias_grad(db, onehot):
    fold = db.reshape(NA_PAIRS, N_DR, GRID_W, 2, GRID_W)
    dtz2d = jnp.transpose(fold, (0, 3, 1, 2, 4)).reshape(NA_HEADS * N_DR, GRID_W * GRID_W)
    dtz2d = jnp.pad(dtz2d, ((0, 128 - NA_HEADS * N_DR), (0, 0)))
    return _rpb_reduce(dtz2d, onehot)[:NA_HEADS * N_DR, :N_DC].reshape(-1)


def _rope_tables(S):
    half = HEAD_DIM // 2
    inv = ROPE_THETA ** (-jnp.arange(half, dtype=F32) / half)
    ang = jnp.arange(S).astype(F32)[:, None] * inv[None, :]
    cos, sin = jnp.cos(ang), jnp.sin(ang)
    n = ROPE_WIDTH // HEAD_DIM
    return jnp.tile(jnp.concatenate([cos, cos], axis=1), (1, n)), jnp.tile(jnp.concatenate([-sin, sin], axis=1), (1, n))


def _rot_half(t):
    w = t.shape[1]
    lane = lax.broadcasted_iota(jnp.int32, t.shape, 1)
    return jnp.where((lane % HEAD_DIM) < HEAD_DIM // 2, pltpu.roll(t, w - HEAD_DIM // 2, axis=1),
                     pltpu.roll(t, HEAD_DIM // 2, axis=1))


def _tok_spec(w):
    return pl.BlockSpec((TOKEN_TILE, w), lambda i: (i, 0))


def _mod_spec(tps, d):
    return pl.BlockSpec((1, 6, d), lambda i: (i // tps, 0, 0))


def _bstat_spec(tps, w):
    return pl.BlockSpec((1, 8, w), lambda i: (i // tps, 0, 0))


def _attn_in(x2d, mod3, g_attn, w_in, cos_t, sin_t, S, tasks=()):
    T, D = x2d.shape
    tps = S // TOKEN_TILE

    def body(x_ref, mod_ref, g_ref, w_ref, cos_ref, sin_ref, h_ref, qkv_ref):
        xn, _ = _rms(x_ref[...])
        h = (xn * g_ref[...]) * (1.0 + mod_ref[0, 1:2, :]) + mod_ref[0, 0:1, :]
        hb = h.astype(BF16)
        h_ref[...] = hb
        proj = _nt(hb, w_ref[...])
        rb = proj[:, ROPE_LO:ROPE_LO + ROPE_WIDTH]
        rb = rb * cos_ref[...] + _rot_half(rb) * sin_ref[...]
        qkv_ref[:, 0:NA_WIDTH] = (proj[:, 0:NA_WIDTH] * Q_SCALE).astype(BF16)
        qkv_ref[:, NA_WIDTH:ROPE_LO] = proj[:, NA_WIDTH:ROPE_LO].astype(BF16)
        qkv_ref[:, ROPE_LO:ROPE_LO + SW_WIDTH] = (rb[:, 0:SW_WIDTH] * Q_SCALE).astype(BF16)
        qkv_ref[:, ROPE_LO + SW_WIDTH:ROPE_LO + ROPE_WIDTH] = rb[:, SW_WIDTH:].astype(BF16)
        qkv_ref[:, ROPE_LO + ROPE_WIDTH:] = proj[:, ROPE_LO + ROPE_WIDTH:].astype(BF16)

    return _hosted_call(
        body, "attn_in", (T // TOKEN_TILE,),
        [_tok_spec(D), _mod_spec(tps, D), _full((1, D)), _full(w_in.shape),
         pl.BlockSpec((TOKEN_TILE, ROPE_WIDTH), lambda i: (i % tps, 0)),
         pl.BlockSpec((TOKEN_TILE, ROPE_WIDTH), lambda i: (i % tps, 0))],
        [_tok_spec(D), _tok_spec(IN_WIDTH)],
        [jax.ShapeDtypeStruct((T, D), BF16), jax.ShapeDtypeStruct((T, IN_WIDTH), BF16)],
        (x2d, mod3, g_attn, w_in, cos_t, sin_t), tasks)


def _attn_out(oa, ob, x2d, mod3, g_na, g_sw, w_out, S):
    T, D = x2d.shape
    tps = S // TOKEN_TILE

    def body(oa_ref, ob_ref, x_ref, mod_ref, gna_ref, gsw_ref, w_ref, mixin_ref, mix_ref, x1_ref):
        oan, _ = _rms(oa_ref[...])
        obn, _ = _rms(ob_ref[...])
        mixin = jnp.concatenate([oan * gna_ref[...], obn * gsw_ref[...]], axis=1).astype(BF16)
        mixin_ref[...] = mixin
        mix = _nn(mixin, w_ref[...])
        mix_ref[...] = mix
        x1_ref[...] = x_ref[...] + mod_ref[0, 2:3, :] * mix

    return pl.pallas_call(
        body, name="attn_out", grid=(T // TOKEN_TILE,),
        in_specs=[_tok_spec(NA_WIDTH), _tok_spec(SW_WIDTH), _tok_spec(D), _mod_spec(tps, D),
                  _full((1, NA_WIDTH)), _full((1, SW_WIDTH)), _full(w_out.shape)],
        out_specs=[_tok_spec(NA_WIDTH + SW_WIDTH), _tok_spec(D), _tok_spec(D)],
        out_shape=[jax.ShapeDtypeStruct((T, NA_WIDTH + SW_WIDTH), BF16), jax.ShapeDtypeStruct((T, D), F32),
                   jax.ShapeDtypeStruct((T, D), F32)],
        compiler_params=_params("parallel"),
    )(oa, ob, x2d, mod3, g_na, g_sw, w_out)


def _ffn_up(x1, mod3, g_ffn, w_up, S):
    T, D = x1.shape
    F = w_up.shape[0] // 2
    tps = S // TOKEN_TILE

    def body(x1_ref, mod_ref, g_ref, w_ref, h2_ref, val_ref, gt_ref):
        xn, _ = _rms(x1_ref[...])
        h2 = ((xn * g_ref[...]) * (1.0 + mod_ref[0, 4:5, :]) + mod_ref[0, 3:4, :]).astype(BF16)
        h2_ref[...] = h2
        u = _nt(h2, w_ref[...])
        val_ref[...] = u[:, :F].astype(BF16)
        gt_ref[...] = u[:, F:].astype(BF16)

    return pl.pallas_call(
        body, name="ffn_up", grid=(T // TOKEN_TILE,),
        in_specs=[_tok_spec(D), _mod_spec(tps, D), _full((1, D)), _full(w_up.shape)],
        out_specs=[_tok_spec(D), _tok_spec(F), _tok_spec(F)],
        out_shape=[jax.ShapeDtypeStruct((T, D), BF16), jax.ShapeDtypeStruct((T, F), BF16), jax.ShapeDtypeStruct((T, F), BF16)],
        compiler_params=_params("parallel"),
    )(x1, mod3, g_ffn, w_up)


def _halo_specs(T, tps, w):
    per = TOKEN_TILE // 8
    prev = pl.BlockSpec((8, w), lambda i: (jnp.maximum(i * per - 1, 0), 0))
    nxt = pl.BlockSpec((8, w), lambda i: (jnp.minimum((i + 1) * per, T // 8 - 1), 0))
    return prev, nxt


def _seq_shifts(cur, before, after, ti, tps):
    tm = cur.shape[0]
    row = lax.broadcasted_iota(jnp.int32, cur.shape, 0)
    before = jnp.where(ti > 0, before.astype(F32), 0.0)
    after = jnp.where(ti < tps - 1, after.astype(F32), 0.0)
    return jnp.where(row == 0, before, pltpu.roll(cur, 1, axis=0)), jnp.where(row == tm - 1, after, pltpu.roll(cur, tm - 1, axis=0))


def _ffn_down(gt, val, conv_w, conv_b, w_down, x1, mod3, g_final, target, B, S):
    T, D = x1.shape
    F = gt.shape[1]
    tps = S // TOKEN_TILE
    prev, nxt = _halo_specs(T, tps, F)

    def body(gt_ref, prev_ref, next_ref, val_ref, cw_ref, cb_ref, w_ref, x1_ref, mod_ref, gf_ref, tgt_ref,
             a_ref, act_ref, vd_ref, dx2_ref, df_ref, gstat_ref, bstat_ref):
        i = pl.program_id(0)
        g = gt_ref[...].astype(F32)
        gprev, gnext = _seq_shifts(g, prev_ref[7:8, :], next_ref[0:1, :], i % tps, tps)
        gc = gprev * cw_ref[0:1, :] + g * cw_ref[1:2, :] + gnext * cw_ref[2:3, :] + cb_ref[...]
        sig = 1.0 / (1.0 + jnp.exp(-gc))
        act = gc * sig
        val = val_ref[...].astype(F32)
        act_ref[...] = act.astype(BF16)
        vd_ref[...] = (val * (sig + act - act * sig)).astype(BF16)
        a = (act * val).astype(BF16)
        a_ref[...] = a
        f = _nn(a, w_ref[...])
        gate = mod_ref[0, 5:6, :]
        x2 = x1_ref[...] + gate * f
        xn, r = _rms(x2)
        err = xn * gf_ref[...] - tgt_ref[...]
        dy = err * (1.0 / D)
        dx2 = _rms_bwd(xn, r, dy * gf_ref[...])
        dx2_ref[...] = dx2
        df_ref[...] = (gate * dx2).astype(BF16)

        @pl.when(i == 0)
        def _():
            gstat_ref[...] = jnp.zeros_like(gstat_ref)

        @pl.when(i % tps == 0)
        def _():
            bstat_ref[...] = jnp.zeros_like(bstat_ref)

        gstat_ref[0:1, :] += jnp.sum(dy * xn, axis=0, keepdims=True)
        tile_loss = jnp.sum(jnp.sum(err * err, axis=1, keepdims=True), axis=0, keepdims=True) * (0.5 / D)
        gstat_ref[1:2, :] += jnp.broadcast_to(tile_loss, (1, D))
        bstat_ref[0, 0:1, :] += jnp.sum(dx2 * f, axis=0, keepdims=True)

    return pl.pallas_call(
        body, name="ffn_down", grid=(T // TOKEN_TILE,),
        in_specs=[_tok_spec(F), prev, nxt, _tok_spec(F), _full(conv_w.shape), _full((1, F)), _full(w_down.shape),
                  _tok_spec(D), _mod_spec(tps, D), _full((1, D)), _tok_spec(D)],
        out_specs=[_tok_spec(F), _tok_spec(F), _tok_spec(F), _tok_spec(D), _tok_spec(D), _full((8, D)), _bstat_spec(tps, D)],
        out_shape=[jax.ShapeDtypeStruct((T, F), BF16), jax.ShapeDtypeStruct((T, F), BF16), jax.ShapeDtypeStruct((T, F), BF16),
                   jax.ShapeDtypeStruct((T, D), F32), jax.ShapeDtypeStruct((T, D), BF16),
                   jax.ShapeDtypeStruct((8, D), F32), jax.ShapeDtypeStruct((B, 8, D), F32)],
        compiler_params=_params("arbitrary"),
    )(gt, gt, gt, val, conv_w, conv_b, w_down, x1, mod3, g_final, target)


def _ffn_down_bwd(df, w_down, act, vd, tasks=()):
    T, D = df.shape
    F = act.shape[1]

    def body(df_ref, w_ref, act_ref, vd_ref, dval_ref, dgc_ref, cstat_ref):
        da = _nt(df_ref[...], w_ref[...])
        dval_ref[...] = (da * act_ref[...].astype(F32)).astype(BF16)
        dgc = da * vd_ref[...].astype(F32)
        dgc_ref[...] = dgc.astype(BF16)

        @pl.when(pl.program_id(0) == 0)
        def _():
            cstat_ref[...] = jnp.zeros_like(cstat_ref)

        cstat_ref[0:1, :] += jnp.sum(dgc, axis=0, keepdims=True)

    return _hosted_call(
        body, "ffn_down_bwd", (T // TOKEN_TILE,),
        [_tok_spec(D), _full(w_down.shape), _tok_spec(F), _tok_spec(F)],
        [_tok_spec(F), _tok_spec(F), _full((8, F))],
        [jax.ShapeDtypeStruct((T, F), BF16), jax.ShapeDtypeStruct((T, F), BF16), jax.ShapeDtypeStruct((8, F), F32)],
        (df, w_down, act, vd), tasks)


def _dw_up_conv(dgc, dval, gt, conv_w, h2, S, tk=512):
    T, F = dgc.shape
    D = h2.shape[1]
    nk, tps, per = T // tk, S // tk, tk // 8
    fc = F // 2 if F % 256 == 0 else F
    nc = F // fc
    gated = lambda i: i // nc
    prev = pl.BlockSpec((8, fc), lambda i, k: (jnp.maximum(k * per - 1, 0) * gated(i), i % nc))
    nxt = pl.BlockSpec((8, fc), lambda i, k: (jnp.minimum((k + 1) * per, T // 8 - 1) * gated(i), i % nc))
    gate_half = pl.BlockSpec((tk, fc), lambda i, k: (k * gated(i), i % nc))
    val_half = pl.BlockSpec((tk, fc), lambda i, k: (k * (1 - gated(i)), i % nc))

    def body(dgc_ref, prev_ref, next_ref, dval_ref, gt_ref, cw_ref, h2_ref, du_ref, dw_ref, cstat_ref, acc):
        i, k = pl.program_id(0), pl.program_id(1)

        @pl.when(k == 0)
        def _():
            acc[...] = jnp.zeros_like(acc)
            cstat_ref[...] = jnp.zeros_like(cstat_ref)

        @pl.when(i < nc)
        def _():
            dv = dval_ref[...]
            du_ref[...] = dv
            acc[...] += _tn(dv, h2_ref[...])

        @pl.when(i >= nc)
        def _():
            d = dgc_ref[...].astype(F32)
            dprev, dnext = _seq_shifts(d, prev_ref[7:8, :], next_ref[0:1, :], k % tps, tps)
            g = gt_ref[...].astype(F32)
            cstat_ref[1:2, :] += jnp.sum(dnext * g, axis=0, keepdims=True)
            cstat_ref[2:3, :] += jnp.sum(d * g, axis=0, keepdims=True)
            cstat_ref[3:4, :] += jnp.sum(dprev * g, axis=0, keepdims=True)
            dgt = (dnext * cw_ref[0:1, :] + d * cw_ref[1:2, :] + dprev * cw_ref[2:3, :]).astype(BF16)
            du_ref[...] = dgt
            acc[...] += _tn(dgt, h2_ref[...])

        @pl.when(k == nk - 1)
        def _():
            dw_ref[...] = acc[...].astype(BF16)

    return pl.pallas_call(
        body, name="dw_up", grid=(2 * nc, nk),
        in_specs=[gate_half, prev, nxt, val_half, gate_half, pl.BlockSpec((conv_w.shape[0], fc), lambda i, k: (0, i % nc)),
                  pl.BlockSpec((tk, D), lambda i, k: (k, 0))],
        out_specs=[pl.BlockSpec((tk, fc), lambda i, k: (k, i)), pl.BlockSpec((fc, D), lambda i, k: (i, 0)),
                   pl.BlockSpec((8, fc), lambda i, k: (gated(i), i % nc))],
        out_shape=[jax.ShapeDtypeStruct((T, 2 * F), BF16), jax.ShapeDtypeStruct((2 * F, D), BF16), jax.ShapeDtypeStruct((16, F), F32)],
        scratch_shapes=[pltpu.VMEM((fc, D), F32)],
        compiler_params=_params("arbitrary", "arbitrary"),
    )(dgc, dgc, dgc, dval, gt, conv_w, h2)


def _ffn_up_bwd(du, w_up, x1, mod3, g_ffn, dx2, mix, B, S, tasks=()):
    T, D = x1.shape
    F = du.shape[1] // 2
    tps = S // TOKEN_TILE

    def body(du_ref, w_ref, x1_ref, mod_ref, g_ref, dx2_ref, mix_ref, dx1_ref, dmix_ref, gstat_ref, bstat_ref):
        i = pl.program_id(0)
        dh2 = _nn(du_ref[...], w_ref[...])
        xn, r = _rms(x1_ref[...])
        scale1 = 1.0 + mod_ref[0, 4:5, :]
        xg = xn * g_ref[...]
        dx1 = dx2_ref[...] + _rms_bwd(xn, r, dh2 * g_ref[...] * scale1)
        dx1_ref[...] = dx1
        dmix_ref[...] = (mod_ref[0, 2:3, :] * dx1).astype(BF16)

        @pl.when(i == 0)
        def _():
            gstat_ref[...] = jnp.zeros_like(gstat_ref)

        @pl.when(i % tps == 0)
        def _():
            bstat_ref[...] = jnp.zeros_like(bstat_ref)

        gstat_ref[0:1, :] += jnp.sum(dh2 * scale1 * xn, axis=0, keepdims=True)
        bstat_ref[0, 0:1, :] += jnp.sum(dh2, axis=0, keepdims=True)
        bstat_ref[0, 1:2, :] += jnp.sum(dh2 * xg, axis=0, keepdims=True)
        bstat_ref[0, 2:3, :] += jnp.sum(dx1 * mix_ref[...], axis=0, keepdims=True)

    return _hosted_call(
        body, "ffn_up_bwd", (T // TOKEN_TILE,),
        [_tok_spec(2 * F), _full(w_up.shape), _tok_spec(D), _mod_spec(tps, D), _full((1, D)), _tok_spec(D), _tok_spec(D)],
        [_tok_spec(D), _tok_spec(D), _full((8, D)), _bstat_spec(tps, D)],
        [jax.ShapeDtypeStruct((T, D), F32), jax.ShapeDtypeStruct((T, D), BF16),
         jax.ShapeDtypeStruct((8, D), F32), jax.ShapeDtypeStruct((B, 8, D), F32)],
        (du, w_up, x1, mod3, g_ffn, dx2, mix), tasks)


def _attn_out_bwd(dmix, w_out, oa, ob, g_na, g_sw, tasks=()):
    T, D = dmix.shape

    def body(dmix_ref, w_ref, oa_ref, ob_ref, gna_ref, gsw_ref, doa_ref, dob_ref, gstat_ref):
        dmixin = _nt(dmix_ref[...], w_ref[...])

        @pl.when(pl.program_id(0) == 0)
        def _():
            gstat_ref[...] = jnp.zeros_like(gstat_ref)

        for k, (o_ref, g_ref, do_ref) in enumerate(((oa_ref, gna_ref, doa_ref), (ob_ref, gsw_ref, dob_ref))):
            dn = dmixin[:, k * NA_WIDTH:(k + 1) * NA_WIDTH]
            on, r = _rms(o_ref[...])
            gstat_ref[k:k + 1, :] += jnp.sum(dn * on, axis=0, keepdims=True)
            do_ref[...] = _rms_bwd(on, r, dn * g_ref[...]).astype(BF16)

    hs = jax.ShapeDtypeStruct((T, NA_WIDTH), BF16)
    return _hosted_call(
        body, "attn_out_bwd", (T // TOKEN_TILE,),
        [_tok_spec(D), _full(w_out.shape), _tok_spec(NA_WIDTH), _tok_spec(SW_WIDTH), _full((1, NA_WIDTH)), _full((1, SW_WIDTH))],
        [_tok_spec(NA_WIDTH), _tok_spec(SW_WIDTH), _full((8, NA_WIDTH))],
        [hs, hs, jax.ShapeDtypeStruct((8, NA_WIDTH), F32)],
        (dmix, w_out, oa, ob, g_na, g_sw), tasks)


def _attn_in_bwd(dqa, dka, dva, dqb, dkb, dvb, cos_t, sin_t, w_in, x2d, mod3, g_attn, dx1, B, S):
    T, D = x2d.shape
    tps = S // TOKEN_TILE

    def body(dqa_ref, dka_ref, dva_ref, dqb_ref, dkb_ref, dvb_ref, cos_ref, sin_ref, w_ref, x_ref, mod_ref, g_ref, dx1_ref,
             gx_ref, dproj_ref, gstat_ref, bstat_ref):
        i = pl.program_id(0)
        drb = jnp.concatenate([dqb_ref[...] * Q_SCALE, dkb_ref[...]], axis=1)
        drb = drb * cos_ref[...] + _rot_half(drb * sin_ref[...])
        dproj = jnp.concatenate([dqa_ref[...] * Q_SCALE, dka_ref[...], dva_ref[...], drb, dvb_ref[...]], axis=1).astype(BF16)
        dproj_ref[...] = dproj
        dh = _nn(dproj, w_ref[...])
        xn, r = _rms(x_ref[...])
        scale1 = 1.0 + mod_ref[0, 1:2, :]
        gx_ref[...] = dx1_ref[...] + _rms_bwd(xn, r, dh * g_ref[...] * scale1)

        @pl.when(i == 0)
        def _():
            gstat_ref[...] = jnp.zeros_like(gstat_ref)

        @pl.when(i % tps == 0)
        def _():
            bstat_ref[...] = jnp.zeros_like(bstat_ref)

        gstat_ref[0:1, :] += jnp.sum(dh * scale1 * xn, axis=0, keepdims=True)
        bstat_ref[0, 0:1, :] += jnp.sum(dh, axis=0, keepdims=True)
        bstat_ref[0, 1:2, :] += jnp.sum(dh * (xn * g_ref[...]), axis=0, keepdims=True)

    rope = pl.BlockSpec((TOKEN_TILE, ROPE_WIDTH), lambda i: (i % tps, 0))
    return pl.pallas_call(
        body, name="attn_in_bwd", grid=(T // TOKEN_TILE,),
        in_specs=[_tok_spec(NA_WIDTH), _tok_spec(NA_WIDTH), _tok_spec(NA_WIDTH), _tok_spec(SW_WIDTH), _tok_spec(SW_KV_WIDTH),
                  _tok_spec(SW_KV_WIDTH), rope, rope, _full(w_in.shape), _tok_spec(D), _mod_spec(tps, D), _full((1, D)), _tok_spec(D)],
        out_specs=[_tok_spec(D), _tok_spec(IN_WIDTH), _full((8, D)), _bstat_spec(tps, D)],
        out_shape=[jax.ShapeDtypeStruct((T, D), F32), jax.ShapeDtypeStruct((T, IN_WIDTH), BF16),
                   jax.ShapeDtypeStruct((8, D), F32), jax.ShapeDtypeStruct((B, 8, D), F32)],
        compiler_params=_params("arbitrary"),
    )(dqa, dka, dva, dqb, dkb, dvb, cos_t, sin_t, w_in, x2d, mod3, g_attn, dx1)


def _matmul_tn(a, b, name, tm=None, tk=512):
    T, M = a.shape
    N = b.shape[1]
    tm = M if tm is None else tm
    nk = T // tk

    def body(a_ref, b_ref, o_ref, acc):
        k = pl.program_id(1)

        @pl.when(k == 0)
        def _():
            acc[...] = jnp.zeros_like(acc)

        acc[...] += _tn(a_ref[...], b_ref[...])

        @pl.when(k == nk - 1)
        def _():
            o_ref[...] = acc[...].astype(BF16)

    return pl.pallas_call(
        body, name=name, grid=(M // tm, nk),
        in_specs=[pl.BlockSpec((tk, tm), lambda i, k: (k, i)), pl.BlockSpec((tk, N), lambda i, k: (k, 0))],
        out_specs=pl.BlockSpec((tm, N), lambda i, k: (i, 0)),
        out_shape=jax.ShapeDtypeStruct((M, N), BF16),
        scratch_shapes=[pltpu.VMEM((tm, N), F32)],
        compiler_params=_params("parallel", "arbitrary"),
    )(a, b)


def _na_geometry(S):
    rows = S // GRID_W
    wr = min(NA_ROWS_MAX, rows)
    return rows, wr


def _na_window(r, rows, wr):
    rs = jnp.clip(r - wr // 2, 0, rows - wr)
    return pl.multiple_of(rs * GRID_W, GRID_W), pl.multiple_of((rs - r + NA_ROWS_MAX - 1) * GRID_W, GRID_W)


NA_STEP_PAIRS = 2
NA_GW = NA_STEP_PAIRS * 128


def _na_specs(S, kw_n, order):
    ng = NA_PAIRS // NA_STEP_PAIRS

    def col(k):
        return pl.BlockSpec((1, S, NA_GW), lambda *ids: (order(*ids)[0], 0, k * ng + order(*ids)[1]))
    bias = pl.BlockSpec((NA_STEP_PAIRS, N_DR * GRID_W, 128), lambda *ids: (order(*ids)[1], 0, 0))
    out = pl.BlockSpec((1, S, NA_GW), lambda *ids: (order(*ids)[0], 0, order(*ids)[1]))
    return col(0), col(1), col(2), bias, out


def _block_diag(t):
    left = lax.broadcasted_iota(jnp.int32, t.shape, 1) < HEAD_DIM
    zero = jnp.zeros_like(t)
    return jnp.concatenate([jnp.where(left, t, zero), jnp.where(left, zero, t)], axis=0)


def _diag_blocks(res):
    left = lax.broadcasted_iota(jnp.int32, (HEAD_DIM, 128), 1) < HEAD_DIM
    return jnp.where(left, res[:HEAD_DIM], res[HEAD_DIM:])


def _col_softmax(st):
    e = jnp.exp(st - jnp.max(st, axis=0, keepdims=True))
    return e * (1.0 / jnp.sum(e, axis=0, keepdims=True))


def _na_fwd(qkv, bias, tasks=()):
    B, S, _ = qkv.shape
    rows, wr = _na_geometry(S)
    kw_n = wr * GRID_W

    def body(q_ref, k_ref, v_ref, b_ref, o_ref):
        def step(r, carry):
            start, boff = _na_window(r, rows, wr)
            qrows = pl.ds(pl.multiple_of(r * GRID_W, GRID_W), GRID_W)
            krows = pl.ds(start, kw_n)
            brows = pl.ds(boff, kw_n)
            lanes = [pl.ds(p * 128, 128) for p in range(NA_STEP_PAIRS)]
            st = [_nt(k_ref[0, krows, ln], _block_diag(q_ref[0, qrows, ln])) for ln in lanes]
            pn = [_col_softmax(st[p] + b_ref[p, brows, :]).astype(BF16) for p in range(NA_STEP_PAIRS)]
            outs = [_diag_blocks(_tn(pn[p], v_ref[0, krows, lanes[p]])) for p in range(NA_STEP_PAIRS)]
            o_ref[0, qrows, :] = jnp.concatenate(outs, axis=1)
            return carry

        lax.fori_loop(0, rows, step, 0)

    q, k, v, bs, out = _na_specs(S, kw_n, lambda b, g: (b, g))
    return _hosted_call(body, "na_fwd", (B, NA_PAIRS // NA_STEP_PAIRS), [q, k, v, bs], [out],
                        [jax.ShapeDtypeStruct((B, S, NA_WIDTH), F32)], (qkv, qkv, qkv, bias), tasks)


def _na_bwd(qkv, bias, doa, tasks=()):
    B, S, _ = qkv.shape
    rows, wr = _na_geometry(S)
    kw_n = wr * GRID_W

    def body(q_ref, k_ref, v_ref, b_ref, do_ref, dq_ref, dk_ref, dv_ref, db_ref):
        @pl.when(pl.program_id(1) == 0)
        def _():
            db_ref[...] = jnp.zeros_like(db_ref)

        dk_ref[...] = jnp.zeros_like(dk_ref)
        dv_ref[...] = jnp.zeros_like(dv_ref)

        def step(r, carry):
            start, boff = _na_window(r, rows, wr)
            qrows = pl.ds(pl.multiple_of(r * GRID_W, GRID_W), GRID_W)
            krows = pl.ds(start, kw_n)
            brows = pl.ds(boff, kw_n)
            pairs = range(NA_STEP_PAIRS)
            lanes = [pl.ds(p * 128, 128) for p in pairs]
            kp = [k_ref[0, krows, ln] for ln in lanes]
            qbd = [_block_diag(q_ref[0, qrows, ln]) for ln in lanes]
            dobd = [_block_diag(do_ref[0, qrows, ln]) for ln in lanes]
            st = [_nt(kp[p], qbd[p]) for p in pairs]
            dpt = [_nt(v_ref[0, krows, lanes[p]], dobd[p]) for p in pairs]
            pn = [_col_softmax(st[p] + b_ref[p, brows, :]) for p in pairs]
            dst = [pn[p] * (dpt[p] - jnp.sum(pn[p] * dpt[p], axis=0, keepdims=True)) for p in pairs]
            dsb = [d.astype(BF16) for d in dst]
            dq_ref[0, qrows, :] = jnp.concatenate([_diag_blocks(_tn(dsb[p], kp[p])) for p in pairs], axis=1)
            dk_ref[0, krows, :] += jnp.concatenate([_nn(dsb[p], qbd[p]) for p in pairs], axis=1)
            dv_ref[0, krows, :] += jnp.concatenate([_nn(pn[p].astype(BF16), dobd[p]) for p in pairs], axis=1)
            for p in pairs:
                db_ref[p, brows, :] += dst[p]
            return carry

        lax.fori_loop(0, rows, step, 0)

    q, k, v, bs, out = _na_specs(S, kw_n, lambda g, b: (b, g))
    hs = jax.ShapeDtypeStruct((B, S, NA_WIDTH), F32)
    return _hosted_call(body, "na_bwd", (NA_PAIRS // NA_STEP_PAIRS, B), [q, k, v, bs, out], [out, out, out, bs],
                        [hs, hs, hs, jax.ShapeDtypeStruct((NA_PAIRS, N_DR * GRID_W, 128), F32)], (qkv, qkv, qkv, bias, doa), tasks)


SW_PAIRS = SW_HEADS // 2


def _sw_band(n, S):
    kw_n = 3 * SW_BLOCK
    start = pl.multiple_of(jnp.clip(n * SW_BLOCK - SW_BLOCK, 0, S - kw_n), SW_BLOCK)
    kpos = start + lax.broadcasted_iota(jnp.int32, (kw_n, SW_BLOCK), 0)
    qpos = n * SW_BLOCK + lax.broadcasted_iota(jnp.int32, (kw_n, SW_BLOCK), 1)
    return start, jnp.abs(qpos - kpos) <= SW_WINDOW


def _kv_halves(t):
    left = lax.broadcasted_iota(jnp.int32, t.shape, 1) < HEAD_DIM
    swapped = pltpu.roll(t, HEAD_DIM, axis=1)
    zero = jnp.zeros_like(t)
    return {(0, 0): jnp.where(left, t, zero), (0, 1): jnp.where(left, zero, swapped),
            (1, 0): jnp.where(left, swapped, zero), (1, 1): jnp.where(left, zero, t)}


def _sw_probs(st, ok, sk):
    st = jnp.where(ok, st, NEG)
    m = jnp.maximum(jnp.max(st, axis=0, keepdims=True), sk)
    e = jnp.exp(st - m)
    esk = jnp.exp(sk - m)
    inv = 1.0 / (jnp.sum(e, axis=0, keepdims=True) + esk)
    return e * inv, esk * inv


def _sw_specs(S):
    q = pl.BlockSpec((1, S, SW_WIDTH), lambda b, *_: (b, 0, ROPE_LO // SW_WIDTH))
    k = pl.BlockSpec((1, S, SW_KV_WIDTH), lambda b, *_: (b, 0, (ROPE_LO + SW_WIDTH) // SW_KV_WIDTH))
    v = pl.BlockSpec((1, S, SW_KV_WIDTH), lambda b, *_: (b, 0, (ROPE_LO + ROPE_WIDTH) // SW_KV_WIDTH))
    return q, k, v


SW_FWD_SPLIT = 2


def _sw_fwd(sink, qkv, tasks=()):
    B, S, _ = qkv.shape
    kw_n = 3 * SW_BLOCK

    def body(sink_ref, q_ref, k_ref, v_ref, o_ref):
        def step(n, carry):
            start, ok = _sw_band(n, S)
            qrows = pl.ds(pl.multiple_of(n * SW_BLOCK, SW_BLOCK), SW_BLOCK)
            krows = pl.ds(start, kw_n)
            kh, vh = _kv_halves(k_ref[0, krows, :]), _kv_halves(v_ref[0, krows, :])
            heads = [(p, e) for p in range(SW_PAIRS) for e in range(2)]
            qp = [q_ref[0, qrows, pl.ds(p * 128, 128)] for p in range(SW_PAIRS)]
            kv_of = lambda p: p // (SW_PAIRS // SW_KV_HEADS)
            st = {(p, e): _nt(kh[(kv_of(p), e)], qp[p]) for p, e in heads}
            pn = {(p, e): _sw_probs(st[(p, e)], ok, sink_ref[2 * p + e])[0].astype(BF16) for p, e in heads}
            outs = [_tn(pn[(p, 0)], vh[(kv_of(p), 0)]) + _tn(pn[(p, 1)], vh[(kv_of(p), 1)]) for p in range(SW_PAIRS)]
            o_ref[0, qrows, :] = jnp.concatenate(outs, axis=1)
            return carry

        half = (S // SW_BLOCK) // SW_FWD_SPLIT
        lax.fori_loop(pl.program_id(1) * half, (pl.program_id(1) + 1) * half, step, 0)

    q, k, v = _sw_specs(S)
    return _hosted_call(
        body, "sw_fwd", (B, SW_FWD_SPLIT), [pl.BlockSpec(memory_space=pltpu.SMEM), q, k, v],
        [pl.BlockSpec((1, S, SW_WIDTH), lambda b, s: (b, 0, 0))], [jax.ShapeDtypeStruct((B, S, SW_WIDTH), F32)],
        (sink, qkv, qkv, qkv), tasks)


def _sw_bwd(sink, qkv, dob):
    B, S, _ = qkv.shape
    kw_n = 3 * SW_BLOCK

    fold_rows = 256

    def body(sink_ref, q_ref, k_ref, v_ref, do_ref, dq_ref, dk_ref, dv_ref, dsink_ref, dk_acc, dv_acc):
        @pl.when(pl.program_id(0) == 0)
        def _():
            dsink_ref[...] = jnp.zeros_like(dsink_ref)

        dk_acc[...] = jnp.zeros_like(dk_acc)
        dv_acc[...] = jnp.zeros_like(dv_acc)
        ppk = SW_PAIRS // SW_KV_HEADS

        def step(n, carry):
            start, ok = _sw_band(n, S)
            qrows = pl.ds(pl.multiple_of(n * SW_BLOCK, SW_BLOCK), SW_BLOCK)
            krows = pl.ds(start, kw_n)
            kh, vh = _kv_halves(k_ref[0, krows, :]), _kv_halves(v_ref[0, krows, :])
            heads = [(p, e) for p in range(SW_PAIRS) for e in range(2)]
            qp = [q_ref[0, qrows, pl.ds(p * 128, 128)] for p in range(SW_PAIRS)]
            dop = [do_ref[0, qrows, pl.ds(p * 128, 128)] for p in range(SW_PAIRS)]
            st = {(p, e): _nt(kh[(p // ppk, e)], qp[p]) for p, e in heads}
            dpt = {(p, e): _nt(vh[(p // ppk, e)], dop[p]) for p, e in heads}
            pnb, dsb = {}, {}
            for p, e in heads:
                pn, psink = _sw_probs(st[(p, e)], ok, sink_ref[2 * p + e])
                delta = jnp.sum(pn * dpt[(p, e)], axis=0, keepdims=True)
                dsb[(p, e)] = (pn * (dpt[(p, e)] - delta)).astype(BF16)
                pnb[(p, e)] = pn.astype(BF16)
                dsink_ref[2 * p + e:2 * p + e + 1, :] += -(psink * delta)
            dq_ref[0, qrows, :] = jnp.concatenate(
                [_tn(dsb[(p, 0)], kh[(p // ppk, 0)]) + _tn(dsb[(p, 1)], kh[(p // ppk, 1)]) for p in range(SW_PAIRS)], axis=1)
            left = lax.broadcasted_iota(jnp.int32, (kw_n, 128), 1) < HEAD_DIM
            dks, dvs = [], []
            for kv in range(SW_KV_HEADS):
                dk = dv = None
                for p in range(kv * ppk, (kv + 1) * ppk):
                    dk_p = jnp.where(left, _nn(dsb[(p, 0)], qp[p]), _nn(dsb[(p, 1)], qp[p]))
                    dv_p = jnp.where(left, _nn(pnb[(p, 0)], dop[p]), _nn(pnb[(p, 1)], dop[p]))
                    dk = dk_p if dk is None else dk + dk_p
                    dv = dv_p if dv is None else dv + dv_p
                dks.append(dk)
                dvs.append(dv)
            dk_acc[krows, :] += jnp.concatenate(dks, axis=1)
            dv_acc[krows, :] += jnp.concatenate(dvs, axis=1)
            return carry

        lax.fori_loop(0, S // SW_BLOCK, step, 0)

        def fold(i, carry):
            rows = pl.ds(pl.multiple_of(i * fold_rows, fold_rows), fold_rows)
            left = lax.broadcasted_iota(jnp.int32, (fold_rows, 128), 1) < HEAD_DIM
            for acc, out_ref in ((dk_acc, dk_ref), (dv_acc, dv_ref)):
                a, b = acc[rows, 0:128], acc[rows, 128:256]
                out_ref[0, rows, :] = jnp.where(left, a + pltpu.roll(a, HEAD_DIM, axis=1), b + pltpu.roll(b, HEAD_DIM, axis=1))
            return carry

        lax.fori_loop(0, S // fold_rows, fold, 0)

        @pl.when(pl.program_id(0) == B - 1)
        def _():
            dsink_ref[...] = jnp.broadcast_to(jnp.sum(dsink_ref[...], axis=1, keepdims=True), dsink_ref.shape)

    q, k, v = _sw_specs(S)
    qo = pl.BlockSpec((1, S, SW_WIDTH), lambda b: (b, 0, 0))
    ko = pl.BlockSpec((1, S, SW_KV_WIDTH), lambda b: (b, 0, 0))
    return pl.pallas_call(
        body, name="sw_bwd", grid=(B,),
        in_specs=[pl.BlockSpec(memory_space=pltpu.SMEM), q, k, v, qo],
        out_specs=[qo, ko, ko, _full((SW_HEADS, 128))],
        out_shape=[jax.ShapeDtypeStruct((B, S, SW_WIDTH), F32), jax.ShapeDtypeStruct((B, S, SW_KV_WIDTH), F32),
                   jax.ShapeDtypeStruct((B, S, SW_KV_WIDTH), F32), jax.ShapeDtypeStruct((SW_HEADS, 128), F32)],
        scratch_shapes=[pltpu.VMEM((S, 2 * SW_KV_WIDTH), F32), pltpu.VMEM((S, 2 * SW_KV_WIDTH), F32)],
        compiler_params=_params("arbitrary"),
    )(sink, qkv, qkv, qkv, dob)


def _pack_sum(packs):
    W = packs.shape[1]

    def body(p_ref, o_ref, s_ref):
        tot = p_ref[0:8, :]
        for d in range(1, N_DEV):
            tot = tot + p_ref[8 * d:8 * d + 8, :]
        o_ref[...] = tot
        s_ref[...] = tot[0:1, :] + tot[1:2, :]

    return pl.pallas_call(body, name="pack_sum", out_shape=[jax.ShapeDtypeStruct((8, W), F32), jax.ShapeDtypeStruct((1, W), F32)],
                          compiler_params=pltpu.CompilerParams(vmem_limit_bytes=VMEM_LIMIT))(packs)


def _adam_update(w, g, m, v):
    c1 = 1.0 - ADAM_B1 ** ADAM_STEP
    c2 = 1.0 - ADAM_B2 ** ADAM_STEP
    nm = ADAM_B1 * m + (1.0 - ADAM_B1) * g
    nv = ADAM_B2 * v + (1.0 - ADAM_B2) * (g * g)
    return -ADAM_LR * ((nm / c1) / (jnp.sqrt(nv / c2) + ADAM_EPS) + ADAM_WD * w), nm, nv


def _adamw(w, g, m, v, name):
    def body(w_ref, g_ref, m_ref, v_ref, d_ref, nm_ref, nv_ref):
        d_ref[...], nm_ref[...], nv_ref[...] = _adam_update(w_ref[...], g_ref[...], m_ref[...], v_ref[...])

    s = jax.ShapeDtypeStruct(w.shape, F32)
    return pl.pallas_call(body, name=name, out_shape=[s, s, s],
                          compiler_params=pltpu.CompilerParams(vmem_limit_bytes=VMEM_LIMIT))(w, g, m, v)


def _sum_adamw(own, recvb, w, m, v, name):
    R, C = own.shape
    rc = _row_chunk(R)

    def body(own_ref, r_ref, w_ref, m_ref, v_ref, g_ref, d_ref, nm_ref, nv_ref):
        def chunk(i, carry):
            rows = pl.ds(pl.multiple_of(i * rc, rc), rc)
            g = own_ref[rows, :]
            for j in range(3):
                g = g + r_ref[j, rows, :].astype(F32)
            g_ref[rows, :] = g
            d_ref[rows, :], nm_ref[rows, :], nv_ref[rows, :] = _adam_update(w_ref[rows, :], g, m_ref[rows, :], v_ref[rows, :])
            return carry

        lax.fori_loop(0, R // rc, chunk, 0)

    s = jax.ShapeDtypeStruct((R, C), F32)
    return pl.pallas_call(body, name=name, out_shape=[s, s, s, s],
                          compiler_params=pltpu.CompilerParams(vmem_limit_bytes=VMEM_LIMIT))(own, recvb, w, m, v)


def _by_device(dw):
    return dw.reshape(N_DEV, dw.shape[0] // N_DEV, dw.shape[1])


def _local_step(x, mod, g_attn, w_in, bias, sw_sink, g_na_out, g_sw_out, w_out, g_ffn, w_up, conv_w, conv_b, w_down,
                g_final, target, sharded):
    B, S, D = x.shape
    T = B * S
    x2d = x.reshape(T, D)
    mod3 = mod.reshape(B, 6, D)
    cos_t, sin_t = _rope_tables(S)
    sink = sw_sink.reshape(SW_HEADS)
    n_tiles = T // TOKEN_TILE
    full = lambda g: g.reshape(N_DEV * g.shape[1], g.shape[2])

    (h, qkv), got = _attn_in(x2d, mod3, g_attn, w_in, cos_t, sin_t, S, [_gather_task(w_out, n_tiles // 2)] if sharded else [])
    if sharded:
        w_out = full(got[0][0])
    qkv3 = qkv.reshape(B, S, IN_WIDTH)
    na_steps = B * (NA_PAIRS // NA_STEP_PAIRS)
    (oa,), got = _na_fwd(qkv3, bias, [_gather_task(w_up, na_steps - 1)] if sharded else [])
    if sharded:
        w_up = full(got[0][0])
    oa = oa.reshape(T, NA_WIDTH)
    (ob,), got = _sw_fwd(sink, qkv3, [_gather_task(w_down, B * SW_FWD_SPLIT - 1)] if sharded else [])
    if sharded:
        w_down = full(got[0][0])
    ob = ob.reshape(T, SW_WIDTH)
    mixin, mix, x1 = _attn_out(oa, ob, x2d, mod3, g_na_out, g_sw_out, w_out, S)
    h2, val, gt = _ffn_up(x1, mod3, g_ffn, w_up, S)
    a, act, vd, dx2, df, gstat_f, bstat_f = _ffn_down(gt, val, conv_w, conv_b, w_down, x1, mod3, g_final, target.reshape(T, D), B, S)
    F = val.shape[1]

    dw_down = _matmul_tn(a, df, "dw_down")
    (dval, dgc, cstat), got = _ffn_down_bwd(df, w_down, act, vd, [_swap_task(_by_device(dw_down))] if sharded else [])
    if sharded:
        send_down, own_down = _chip_sums(_by_device(dw_down), got[0][0])
    du, dw_up, cstat_w = _dw_up_conv(dgc, dval, gt, conv_w, h2, S)
    (dx1, dmix, gstat_u, bstat_u), got = _ffn_up_bwd(du, w_up, x1, mod3, g_ffn, dx2, mix, B, S,
                                                     [_exchange_task(send_down), _swap_task(_by_device(dw_up))] if sharded else [])
    if sharded:
        dw_down = (own_down, got[0][0])
        send_up, own_up = _chip_sums(_by_device(dw_up), got[1][0])
    dw_out = _matmul_tn(mixin, dmix, "dw_out")
    (doa, dob, gstat_o), got = _attn_out_bwd(dmix, w_out, oa, ob, g_na_out, g_sw_out, [_swap_task(_by_device(dw_out))] if sharded else [])
    if sharded:
        send_out, own_out = _chip_sums(_by_device(dw_out), got[0][0])
    (dqa, dka, dva, dbt), got = _na_bwd(qkv3, bias, doa.reshape(B, S, NA_WIDTH),
                                        [_exchange_task(send_up), _exchange_task(send_out)] if sharded else [])
    if sharded:
        dw_up, dw_out = (own_up, got[0][0]), (own_out, got[1][0])
    dqb, dkb, dvb, dsink = _sw_bwd(sink, qkv3, dob.reshape(B, S, SW_WIDTH))
    r2 = lambda t: t.reshape(T, t.shape[-1])
    grad_x, dproj, gstat_i, bstat_i = _attn_in_bwd(r2(dqa), r2(dka), r2(dva), r2(dqb), r2(dkb), r2(dvb), cos_t, sin_t, w_in, x2d, mod3,
                                                   g_attn, dx1, B, S)
    dw_in = _matmul_tn(dproj, h, "dw_in")

    dmod = jnp.stack([bstat_i[:, 0], bstat_i[:, 1], bstat_u[:, 2], bstat_u[:, 0], bstat_u[:, 1], bstat_f[:, 0]], axis=1)
    small = dict(g_attn=gstat_i[0], g_ffn=gstat_u[0], g_final=gstat_f[0], loss=gstat_f[1, 0], g_na_out=gstat_o[0], g_sw_out=gstat_o[1],
                 sw_sink=dsink[:, 0], conv_b=cstat[0], conv_w=cstat_w[9:12], dbt=dbt)
    return grad_x.reshape(B, S, D), dict(w_in=dw_in, w_out=dw_out, w_up=dw_up, w_down=dw_down), dmod, small


def _pad_lanes(v, w):
    return jnp.pad(v, (0, w - v.shape[0]))


def kernel(x, c, w_ada, b_ada, g_attn, w_in, na_rpb, sw_sink, g_na_out, g_sw_out, w_out, g_ffn, w_up, conv_w, conv_b, w_down, g_final, loss_target, m_w_ada, m_b_ada, m_g_attn, m_w_in, m_na_rpb, m_sw_sink, m_g_na_out, m_g_sw_out, m_w_out, m_g_ffn, m_w_up, m_conv_w, m_conv_b, m_w_down, m_g_final, v_w_ada, v_b_ada, v_g_attn, v_w_in, v_na_rpb, v_sw_sink, v_g_na_out, v_g_sw_out, v_w_out, v_g_ffn, v_w_up, v_conv_w, v_conv_b, v_w_down, v_g_final):
    B, S, D = x.shape
    me = 4 * lax.axis_index("x") + 2 * lax.axis_index("y") + lax.axis_index("c")
    ada_c = w_ada.shape[2]
    F_l = conv_w.shape[2]
    onehot = _toeplitz_onehot()

    cw_l = jnp.pad(conv_w[0], ((0, 8 - conv_w.shape[1]), (0, 0)))
    c_l = jnp.pad(c, ((0, 8 - B), (0, 0)))
    slabs = _all_gather(jnp.concatenate([c_l, cw_l], axis=1), "gather_c")
    c_all = slabs[:, :, :D].reshape(N_DEV * 8, D)
    conv_w_f = jnp.transpose(slabs[:, :3, D:], (1, 0, 2)).reshape(3, N_DEV * F_l)

    b_ada_l = lax.dynamic_slice(b_ada, (0, me * ada_c), (1, ada_c))
    mod_cols = _ada_fwd(c_all, w_ada[0], b_ada_l)
    mod_all = _all_gather(mod_cols, "gather_mod")
    mod_mine = lax.dynamic_slice(mod_all, (0, me * 8, 0), (N_DEV, B, ada_c))
    mod = jnp.transpose(mod_mine, (1, 0, 2)).reshape(B, N_DEV * ada_c)

    tr = {"w_in", "w_up"}
    w_in_t = jnp.transpose(w_in[0])
    w_in_f = _all_gather(w_in_t.astype(BF16), "gather_w_in").reshape(N_DEV * w_in_t.shape[0], D)
    shards = dict(w_out=w_out[0].astype(BF16), w_up=jnp.transpose(w_up[0]).astype(BF16), w_down=w_down[0].astype(BF16))

    bias = _na_bias_table(na_rpb[0], onehot)

    grad_x, dw, dmod, small = _local_step(x, mod, g_attn, w_in_f, bias, sw_sink, g_na_out, g_sw_out, shards["w_out"], g_ffn,
                                          shards["w_up"], conv_w_f, conv_b, shards["w_down"], g_final.reshape(1, D), loss_target,
                                          sharded=True)
    g_w_in = _reduce_scatter(_by_device(dw["w_in"]), "scatter_w_in")

    drpb = _na_bias_grad(small["dbt"], onehot)

    row2 = jnp.concatenate([small["g_attn"], small["g_ffn"], small["g_final"], small["g_na_out"], small["g_sw_out"],
                            _pad_lanes(small["sw_sink"], 128), _pad_lanes(small["loss"].reshape(1), 128)])
    rows = [dmod.reshape(B, 6 * D)[0], dmod.reshape(B, 6 * D)[1], _pad_lanes(row2, PACK_W), _pad_lanes(small["conv_b"], PACK_W),
            _pad_lanes(drpb, PACK_W)] + [_pad_lanes(small["conv_w"][k], PACK_W) for k in range(3)]
    packs = _all_gather(jnp.stack(rows), "gather_small")
    tot, g_b_ada = _pack_sum(packs.reshape(N_DEV * 8, PACK_W))

    o = 0
    rep = {}
    for nm, wd in (("g_attn", D), ("g_ffn", D), ("g_final", D), ("g_na_out", NA_WIDTH), ("g_sw_out", SW_WIDTH), ("sw_sink", 128), ("loss", 128)):
        rep[nm] = tot[2, o:o + wd]
        o += wd
    loss = rep["loss"][0]
    g_conv_b = tot[3:4, :conv_b.shape[1]]
    g_na_rpb = tot[4, :NA_HEADS * N_DR * N_DC].reshape(na_rpb.shape)
    g_conv_w = lax.dynamic_slice(tot[5:8], (0, me * F_l), (3, F_l)).reshape(conv_w.shape)

    dmod_cols = lax.dynamic_slice(packs.reshape(N_DEV * 8, PACK_W), (0, me * ada_c), (N_DEV * 8, ada_c))
    g_w_ada = _ada_bwd(c_all, dmod_cols)[None]

    grads = dict(
        w_ada=g_w_ada, b_ada=g_b_ada, g_attn=rep["g_attn"][None], w_in=g_w_in, na_rpb=g_na_rpb,
        sw_sink=rep["sw_sink"][None, :SW_HEADS], g_na_out=rep["g_na_out"][None], g_sw_out=rep["g_sw_out"][None],
        w_out=None, g_ffn=rep["g_ffn"][None], w_up=None, conv_w=g_conv_w, conv_b=g_conv_b, w_down=None, g_final=rep["g_final"])
    weights = dict(w_ada=w_ada, b_ada=b_ada, g_attn=g_attn, w_in=w_in, na_rpb=na_rpb, sw_sink=sw_sink, g_na_out=g_na_out,
                   g_sw_out=g_sw_out, w_out=w_out, g_ffn=g_ffn, w_up=w_up, conv_w=conv_w, conv_b=conv_b, w_down=w_down, g_final=g_final)
    ms = dict(w_ada=m_w_ada, b_ada=m_b_ada, g_attn=m_g_attn, w_in=m_w_in, na_rpb=m_na_rpb, sw_sink=m_sw_sink, g_na_out=m_g_na_out,
              g_sw_out=m_g_sw_out, w_out=m_w_out, g_ffn=m_g_ffn, w_up=m_w_up, conv_w=m_conv_w, conv_b=m_conv_b, w_down=m_w_down, g_final=m_g_final)
    vs = dict(w_ada=v_w_ada, b_ada=v_b_ada, g_attn=v_g_attn, w_in=v_w_in, na_rpb=v_na_rpb, sw_sink=v_sw_sink, g_na_out=v_g_na_out,
              g_sw_out=v_g_sw_out, w_out=v_w_out, g_ffn=v_g_ffn, w_up=v_w_up, conv_w=v_conv_w, conv_b=v_conv_b, w_down=v_w_down, g_final=v_g_final)
    names = list(weights)
    deltas, new_m, new_v = {}, {}, {}
    for nm in names:
        shp = weights[nm].shape
        if nm in tr:
            r = lambda t: jnp.transpose(t[0])
            back = lambda t: jnp.transpose(t)[None]
        else:
            two_d = (shp[-2], shp[-1]) if len(shp) >= 3 and nm != "na_rpb" else (1, int(np.prod(shp)))
            r = lambda t: t.reshape(two_d)
            back = lambda t: t.reshape(shp)
        if nm in ("w_out", "w_up", "w_down"):
            g2, d_, m_, v_ = _sum_adamw(*dw[nm], r(weights[nm]), r(ms[nm]), r(vs[nm]), "adamw_" + nm)
        else:
            g2 = grads[nm] if nm in tr else r(grads[nm])
            d_, m_, v_ = _adamw(r(weights[nm]), g2, r(ms[nm]), r(vs[nm]), "adamw_" + nm)
        deltas[nm], new_m[nm], new_v[nm], grads[nm] = back(d_), back(m_), back(v_), back(g2)
    return (loss, grad_x, *[grads[n] for n in names], *[deltas[n] for n in names], *[new_m[n] for n in names],
            *[new_v[n] for n in names])
```

```python
import functools

import numpy as np
import jax
import jax.numpy as jnp
from jax import lax
from jax.experimental import pallas as pl
from jax.experimental.pallas import tpu as pltpu

F32, BF16 = jnp.float32, jnp.bfloat16
MESH_ID = pl.DeviceIdType.MESH
N_DEV = 8

HEAD_DIM = 64
NA_HEADS = 8
SW_HEADS = 8
SW_KV_HEADS = 2
SW_GROUP = SW_HEADS // SW_KV_HEADS
NA_WIDTH = NA_HEADS * HEAD_DIM
SW_WIDTH = SW_HEADS * HEAD_DIM
SW_KV_WIDTH = SW_KV_HEADS * HEAD_DIM
ROPE_WIDTH = SW_WIDTH + SW_KV_WIDTH
IN_WIDTH = 3 * NA_WIDTH + SW_WIDTH + 2 * SW_KV_WIDTH
ROPE_LO = 3 * NA_WIDTH
GRID_W = 64
NA_ROWS_MAX = 8
NA_COLS = 16
N_DR = 2 * NA_ROWS_MAX - 1
N_DC = 2 * NA_COLS - 1
SW_WINDOW = 128
SW_BLOCK = 128
ROPE_THETA = 10000.0
EPS = 1e-6
NEG = -1e30
Q_SCALE = HEAD_DIM ** -0.5

ADAM_LR = 0.001
ADAM_B1 = 0.9
ADAM_B2 = 0.999
ADAM_EPS = 1e-08
ADAM_WD = 0.01
ADAM_STEP = 10

TOKEN_TILE = 256
VMEM_LIMIT = 56 * 1024 * 1024

PACK_W = 6144


def _nn(a, b):
    return jnp.dot(a, b, preferred_element_type=F32)


def _nt(a, b):
    return lax.dot_general(a, b, (((1,), (1,)), ((), ())), preferred_element_type=F32)


def _tn(a, b):
    return lax.dot_general(a, b, (((0,), (0,)), ((), ())), preferred_element_type=F32)


def _rms(x):
    r = lax.rsqrt(jnp.mean(x * x, axis=-1, keepdims=True) + EPS)
    return x * r, r


def _rms_bwd(xn, r, gy):
    return r * (gy - xn * jnp.mean(xn * gy, axis=-1, keepdims=True))


def _params(*sem):
    return pltpu.CompilerParams(dimension_semantics=sem, vmem_limit_bytes=VMEM_LIMIT)


def _full(shape):
    n = len(shape)
    return pl.BlockSpec(shape, lambda *_: (0,) * n)


def _mesh_pos():
    return lax.axis_index("x"), lax.axis_index("y"), lax.axis_index("c")


def _all_gather(x, name):
    def body(x_ref, out_ref, send_sems, recv_sems, local_sem):
        x_, y_, c_ = _mesh_pos()
        me, sibling = (x_, y_, c_), (x_, y_, 1 - c_)
        chips = [(1 - x_, y_), (x_, 1 - y_), (1 - x_, 1 - y_)]

        def rows(px, py, pc):
            return out_ref.at[4 * px + 2 * py + pc]

        def copy(k, block, to, src=None):
            return pltpu.make_async_remote_copy(
                src_ref=rows(*block) if src is None else src, dst_ref=rows(*block),
                send_sem=send_sems.at[k], recv_sem=recv_sems.at[k], device_id=to, device_id_type=MESH_ID)

        mine = pltpu.make_async_copy(x_ref, rows(*me), local_sem)
        mine.start()
        first = [copy(0, me, sibling, src=x_ref)]
        first += [copy(1 + j, me, (*chip, c_), src=x_ref) for j, chip in enumerate(chips)]
        for cp in first:
            cp.start()
        passed = [copy(4 + j, (*chip, c_), sibling) for j, chip in enumerate(chips)]
        for j, chip in enumerate(chips):
            copy(1 + j, (*chip, c_), me).wait_recv()
            passed[j].start()
        copy(0, sibling, me).wait_recv()
        for j, chip in enumerate(chips):
            copy(4 + j, (*chip, 1 - c_), me).wait_recv()
        for cp in first + passed:
            cp.wait_send()
        mine.wait()

    return pl.pallas_call(
        body, name=name,
        out_shape=jax.ShapeDtypeStruct((N_DEV,) + x.shape, x.dtype),
        in_specs=[pl.BlockSpec(memory_space=pl.ANY)],
        out_specs=pl.BlockSpec(memory_space=pl.ANY),
        scratch_shapes=[pltpu.SemaphoreType.DMA((7,)), pltpu.SemaphoreType.DMA((7,)), pltpu.SemaphoreType.DMA],
    )(x)


def _row_chunk(r):
    for rc in (128, 64, 32, 16):
        if r % rc == 0:
            return rc
    raise ValueError(f"rows {r} not a multiple of 16")


def _reduce_scatter(g8, name):
    _, R, C = g8.shape
    rc = _row_chunk(R)

    def body(g_ref, out_ref, recva, sendb, recvb, sa, ra, sb, rb):
        x_, y_, c_ = _mesh_pos()
        sibling = (x_, y_, 1 - c_)
        copies_a = []
        for k in range(4):
            cp = pltpu.make_async_remote_copy(
                src_ref=g_ref.at[2 * k + (1 - c_)], dst_ref=recva.at[k],
                send_sem=sa.at[k], recv_sem=ra.at[k], device_id=sibling, device_id_type=MESH_ID)
            cp.start()
            copies_a.append(cp)
        for cp in copies_a:
            cp.wait_recv()

        def chip_sum(k, rows):
            return g_ref[2 * k + c_, rows, :].astype(F32) + recva[k, rows, :].astype(F32)

        flips = [(1 - x_, y_), (x_, 1 - y_), (1 - x_, 1 - y_)]
        copies_b = []
        for j, (tx, ty) in enumerate(flips):
            kt = 2 * tx + ty

            def fill(i, carry, j=j, kt=kt):
                rows = pl.ds(pl.multiple_of(i * rc, rc), rc)
                sendb[j, rows, :] = chip_sum(kt, rows).astype(BF16)
                return carry

            lax.fori_loop(0, R // rc, fill, 0)
            cp = pltpu.make_async_remote_copy(
                src_ref=sendb.at[j], dst_ref=recvb.at[j],
                send_sem=sb.at[j], recv_sem=rb.at[j], device_id=(tx, ty, c_), device_id_type=MESH_ID)
            cp.start()
            copies_b.append(cp)
        for cp in copies_b:
            cp.wait_recv()
        kme = 2 * x_ + y_

        def total(i, carry):
            rows = pl.ds(pl.multiple_of(i * rc, rc), rc)
            acc = chip_sum(kme, rows)
            for j in range(3):
                acc = acc + recvb[j, rows, :].astype(F32)
            out_ref[rows, :] = acc
            return carry

        lax.fori_loop(0, R // rc, total, 0)
        for cp in copies_a + copies_b:
            cp.wait_send()

    vm = pl.BlockSpec(memory_space=pltpu.VMEM)
    return pl.pallas_call(
        body, name=name,
        out_shape=jax.ShapeDtypeStruct((R, C), F32),
        in_specs=[vm], out_specs=vm,
        scratch_shapes=[pltpu.VMEM((4, R, C), BF16), pltpu.VMEM((3, R, C), BF16), pltpu.VMEM((3, R, C), BF16),
                        pltpu.SemaphoreType.DMA((4,)), pltpu.SemaphoreType.DMA((4,)),
                        pltpu.SemaphoreType.DMA((3,)), pltpu.SemaphoreType.DMA((3,))],
        compiler_params=pltpu.CompilerParams(vmem_limit_bytes=VMEM_LIMIT),
    )(g8)


class _Task:
    def __init__(self, inputs, out_shapes, sems, start, finish, mid=None, mid_step=None):
        self.inputs, self.out_shapes, self.sems = list(inputs), list(out_shapes), list(sems)
        self.start, self.finish, self.mid, self.mid_step = start, finish, mid, mid_step


def _hosted_call(body, name, grid, in_specs, out_specs, out_shape, operands, tasks, scratch_shapes=()):
    n_in, n_out, n_scr = len(in_specs), len(out_specs), len(scratch_shapes)
    t_in = [len(t.inputs) for t in tasks]
    t_out = [len(t.out_shapes) for t in tasks]
    t_sem = [len(t.sems) for t in tasks]
    n_steps = int(np.prod(grid))

    def wrapped(*refs):
        ins, rest = refs[:n_in], refs[n_in:]
        task_ins, rest = rest[:sum(t_in)], rest[sum(t_in):]
        outs, rest = rest[:n_out], rest[n_out:]
        task_outs, rest = rest[:sum(t_out)], rest[sum(t_out):]
        scr, task_sems = rest[:n_scr], rest[n_scr:]
        step = pl.program_id(0)
        for ax in range(1, len(grid)):
            step = step * grid[ax] + pl.program_id(ax)
        parts = []
        oi = oo = os_ = 0
        for t, a, b, c in zip(tasks, t_in, t_out, t_sem):
            parts.append((t, task_ins[oi:oi + a], task_outs[oo:oo + b], task_sems[os_:os_ + c]))
            oi, oo, os_ = oi + a, oo + b, os_ + c
        for t, ti, to, ts in parts:
            pl.when(step == 0)(functools.partial(t.start, ti, to, ts))
            if t.mid is not None:
                pl.when(step == t.mid_step)(functools.partial(t.mid, ti, to, ts))
        body(*ins, *outs, *scr)
        for t, ti, to, ts in parts:
            pl.when(step == n_steps - 1)(functools.partial(t.finish, ti, to, ts))

    hbm = pl.BlockSpec(memory_space=pl.ANY)
    res = pl.pallas_call(
        wrapped, name=name, grid=grid,
        in_specs=list(in_specs) + [hbm] * sum(t_in),
        out_specs=list(out_specs) + [hbm] * sum(t_out),
        out_shape=list(out_shape) + [s for t in tasks for s in t.out_shapes],
        scratch_shapes=list(scratch_shapes) + [s for t in tasks for s in t.sems],
        compiler_params=_params(*(["arbitrary"] * len(grid))),
    )(*operands, *[a for t in tasks for a in t.inputs])
    own, extra = res[:n_out], res[n_out:]
    per_task, o = [], 0
    for b in t_out:
        per_task.append(extra[o:o + b])
        o += b
    return own, per_task


def _gather_task(shard, mid_step):
    def parts(ins, outs, sems):
        (x_ref,), (out_ref,), (send_sems, recv_sems, local_sem) = ins, outs, sems
        x_, y_, c_ = _mesh_pos()
        me, sibling = (x_, y_, c_), (x_, y_, 1 - c_)
        chips = [(1 - x_, y_), (x_, 1 - y_), (1 - x_, 1 - y_)]

        def rows(px, py, pc):
            return out_ref.at[4 * px + 2 * py + pc]

        def copy(k, block, to, src=None):
            return pltpu.make_async_remote_copy(
                src_ref=rows(*block) if src is None else src, dst_ref=rows(*block),
                send_sem=send_sems.at[k], recv_sem=recv_sems.at[k], device_id=to, device_id_type=MESH_ID)

        return dict(
            mine=lambda: pltpu.make_async_copy(x_ref, rows(*me), local_sem),
            first=lambda: [copy(0, me, sibling, src=x_ref)] + [copy(1 + j, me, (*chip, c_), src=x_ref) for j, chip in enumerate(chips)],
            passed=lambda: [copy(4 + j, (*chip, c_), sibling) for j, chip in enumerate(chips)],
            landed=lambda: [copy(1 + j, (*chip, c_), me) for j, chip in enumerate(chips)],
            last=lambda: [copy(0, sibling, me)] + [copy(4 + j, (*chip, 1 - c_), me) for j, chip in enumerate(chips)])

    def start(ins, outs, sems):
        p = parts(ins, outs, sems)
        p["mine"]().start()
        for cp in p["first"]():
            cp.start()

    def mid(ins, outs, sems):
        p = parts(ins, outs, sems)
        for cp, fw in zip(p["landed"](), p["passed"]()):
            cp.wait_recv()
            fw.start()

    def finish(ins, outs, sems):
        p = parts(ins, outs, sems)
        for cp in p["last"]():
            cp.wait_recv()
        for cp in p["first"]() + p["passed"]():
            cp.wait_send()
        p["mine"]().wait()

    return _Task([shard], [jax.ShapeDtypeStruct((N_DEV,) + shard.shape, shard.dtype)],
                 [pltpu.SemaphoreType.DMA((7,)), pltpu.SemaphoreType.DMA((7,)), pltpu.SemaphoreType.DMA],
                 start, finish, mid, mid_step)


def _swap_task(g8):
    _, R, C = g8.shape

    def copies(ins, outs, sems):
        (g_ref,), (recv_ref,), (ss, rs) = ins, outs, sems
        x_, y_, c_ = _mesh_pos()
        return [pltpu.make_async_remote_copy(src_ref=g_ref.at[2 * k + (1 - c_)], dst_ref=recv_ref.at[k], send_sem=ss.at[k],
                                             recv_sem=rs.at[k], device_id=(x_, y_, 1 - c_), device_id_type=MESH_ID)
                for k in range(4)]

    def start(ins, outs, sems):
        for cp in copies(ins, outs, sems):
            cp.start()

    def finish(ins, outs, sems):
        cps = copies(ins, outs, sems)
        for cp in cps:
            cp.wait_recv()
        for cp in cps:
            cp.wait_send()

    return _Task([g8], [jax.ShapeDtypeStruct((4, R, C), g8.dtype)],
                 [pltpu.SemaphoreType.DMA((4,)), pltpu.SemaphoreType.DMA((4,))], start, finish)


def _chip_sums(g8, recva):
    _, R, C = g8.shape
    rc = _row_chunk(R)

    def body(g_ref, a_ref, send_ref, own_ref):
        x_, y_, c_ = _mesh_pos()
        chips = [(1 - x_, y_), (x_, 1 - y_), (1 - x_, 1 - y_), (x_, y_)]

        def chunk(i, carry):
            rows = pl.ds(pl.multiple_of(i * rc, rc), rc)
            for j, (tx, ty) in enumerate(chips):
                k = 2 * tx + ty
                s = g_ref[2 * k + c_, rows, :].astype(F32) + a_ref[k, rows, :].astype(F32)
                if j < 3:
                    send_ref[j, rows, :] = s.astype(BF16)
                else:
                    own_ref[rows, :] = s
            return carry

        lax.fori_loop(0, R // rc, chunk, 0)

    return pl.pallas_call(body, name="chip_sums", out_shape=[jax.ShapeDtypeStruct((3, R, C), BF16), jax.ShapeDtypeStruct((R, C), F32)],
                          compiler_params=pltpu.CompilerParams(vmem_limit_bytes=VMEM_LIMIT))(g8, recva)


def _exchange_task(sendb):
    def copies(ins, outs, sems):
        (s_ref,), (recv_ref,), (ss, rs) = ins, outs, sems
        x_, y_, c_ = _mesh_pos()
        flips = [(1 - x_, y_), (x_, 1 - y_), (1 - x_, 1 - y_)]
        return [pltpu.make_async_remote_copy(src_ref=s_ref.at[j], dst_ref=recv_ref.at[j], send_sem=ss.at[j], recv_sem=rs.at[j],
                                             device_id=(tx, ty, c_), device_id_type=MESH_ID) for j, (tx, ty) in enumerate(flips)]

    def start(ins, outs, sems):
        for cp in copies(ins, outs, sems):
            cp.start()

    def finish(ins, outs, sems):
        cps = copies(ins, outs, sems)
        for cp in cps:
            cp.wait_recv()
        for cp in cps:
            cp.wait_send()

    return _Task([sendb], [jax.ShapeDtypeStruct(sendb.shape, sendb.dtype)],
                 [pltpu.SemaphoreType.DMA((3,)), pltpu.SemaphoreType.DMA((3,))], start, finish)


def _silu(v):
    return v * (1.0 / (1.0 + jnp.exp(-v)))


def _ada_fwd(c_all, w_ada_l, b_ada_l):
    def body(c_ref, w_ref, b_ref, o_ref):
        o_ref[...] = jnp.dot(_silu(c_ref[...]), w_ref[...], precision=lax.Precision.HIGHEST,
                             preferred_element_type=F32) + b_ref[...]
    return pl.pallas_call(body, name="ada_fwd", out_shape=jax.ShapeDtypeStruct((c_all.shape[0], w_ada_l.shape[1]), F32),
                          compiler_params=pltpu.CompilerParams(vmem_limit_bytes=VMEM_LIMIT))(c_all, w_ada_l, b_ada_l)


def _ada_bwd(c_all, dmod_cols):
    def body(c_ref, d_ref, o_ref):
        o_ref[...] = lax.dot_general(_silu(c_ref[...]), d_ref[...], (((0,), (0,)), ((), ())),
                                     precision=lax.Precision.HIGHEST, preferred_element_type=F32)
    return pl.pallas_call(body, name="ada_bwd", out_shape=jax.ShapeDtypeStruct((c_all.shape[1], dmod_cols.shape[1]), F32),
                          compiler_params=pltpu.CompilerParams(vmem_limit_bytes=VMEM_LIMIT))(c_all, dmod_cols)


def _toeplitz_onehot():
    k = np.arange(GRID_W)[:, None]
    q = np.arange(GRID_W)[None, :]
    dc = np.clip(k - q + NA_COLS - 1, 0, N_DC - 1).reshape(-1)
    e = np.zeros((128, GRID_W * GRID_W), np.float32)
    e[dc, np.arange(GRID_W * GRID_W)] = 1.0
    return jnp.asarray(e)


def _rpb_expand(rpb2d, onehot):
    def body(r_ref, e_ref, o_ref):
        o_ref[...] = jnp.dot(r_ref[...], e_ref[...], precision=lax.Precision.HIGHEST, preferred_element_type=F32)
    return pl.pallas_call(body, name="rpb_expand", out_shape=jax.ShapeDtypeStruct((128, GRID_W * GRID_W), F32),
                          compiler_params=pltpu.CompilerParams(vmem_limit_bytes=VMEM_LIMIT))(rpb2d, onehot)


def _rpb_reduce(dtz2d, onehot):
    def body(d_ref, e_ref, o_ref):
        o_ref[...] = lax.dot_general(d_ref[...], e_ref[...], (((1,), (1,)), ((), ())),
                                     precision=lax.Precision.HIGHEST, preferred_element_type=F32)
    return pl.pallas_call(body, name="rpb_reduce", out_shape=jax.ShapeDtypeStruct((128, 128), F32),
                          compiler_params=pltpu.CompilerParams(vmem_limit_bytes=VMEM_LIMIT))(dtz2d, onehot)


NA_PAIRS = NA_HEADS // 2


def _na_bias_table(na_rpb, onehot):
    rpb2d = jnp.pad(na_rpb.reshape(NA_HEADS * N_DR, N_DC), ((0, 128 - NA_HEADS * N_DR), (0, 128 - N_DC)))
    tz = _rpb_expand(rpb2d, onehot)[:NA_HEADS * N_DR].reshape(NA_PAIRS, 2, N_DR, GRID_W, GRID_W)
    col = np.arange(GRID_W)
    cs = np.clip(col - NA_COLS // 2, 0, GRID_W - NA_COLS)
    ok_kq = ((col[None, :] >= cs[:, None]) & (col[None, :] < cs[:, None] + NA_COLS)).T
    tz = jnp.where(jnp.asarray(ok_kq)[None, None, None], tz, NEG)
    return jnp.transpose(tz, (0, 2, 3, 1, 4)).reshape(NA_PAIRS, N_DR * GRID_W, 128)


def _na_bias_grad(db, onehot):
    fold = db.reshape(NA_PAIRS, N_DR, GRID_W, 2, GRID_W)
    dtz2d = jnp.transpose(fold, (0, 3, 1, 2, 4)).reshape(NA_HEADS * N_DR, GRID_W * GRID_W)
    dtz2d = jnp.pad(dtz2d, ((0, 128 - NA_HEADS * N_DR), (0, 0)))
    return _rpb_reduce(dtz2d, onehot)[:NA_HEADS * N_DR, :N_DC].reshape(-1)


def _rope_tables(S):
    half = HEAD_DIM // 2
    inv = ROPE_THETA ** (-jnp.arange(half, dtype=F32) / half)
    ang = jnp.arange(S).astype(F32)[:, None] * inv[None, :]
    cos, sin = jnp.cos(ang), jnp.sin(ang)
    n = ROPE_WIDTH // HEAD_DIM
    return jnp.tile(jnp.concatenate([cos, cos], axis=1), (1, n)), jnp.tile(jnp.concatenate([-sin, sin], axis=1), (1, n))


def _rot_half(t):
    w = t.shape[1]
    lane = lax.broadcasted_iota(jnp.int32, t.shape, 1)
    return jnp.where((lane % HEAD_DIM) < HEAD_DIM // 2, pltpu.roll(t, w - HEAD_DIM // 2, axis=1),
                     pltpu.roll(t, HEAD_DIM // 2, axis=1))


def _tok_spec(w):
    return pl.BlockSpec((TOKEN_TILE, w), lambda i: (i, 0))


def _mod_spec(tps, d):
    return pl.BlockSpec((1, 6, d), lambda i: (i // tps, 0, 0))


def _bstat_spec(tps, w):
    return pl.BlockSpec((1, 8, w), lambda i: (i // tps, 0, 0))


def _attn_in(x2d, mod3, g_attn, w_in, cos_t, sin_t, S, tasks=()):
    T, D = x2d.shape
    tps = S // TOKEN_TILE

    def body(x_ref, mod_ref, g_ref, w_ref, cos_ref, sin_ref, h_ref, qkv_ref):
        xn, _ = _rms(x_ref[...])
        h = (xn * g_ref[...]) * (1.0 + mod_ref[0, 1:2, :]) + mod_ref[0, 0:1, :]
        hb = h.astype(BF16)
        h_ref[...] = hb
        proj = _nt(hb, w_ref[...])
        rb = proj[:, ROPE_LO:ROPE_LO + ROPE_WIDTH]
        rb = rb * cos_ref[...] + _rot_half(rb) * sin_ref[...]
        qkv_ref[:, 0:NA_WIDTH] = (proj[:, 0:NA_WIDTH] * Q_SCALE).astype(BF16)
        qkv_ref[:, NA_WIDTH:ROPE_LO] = proj[:, NA_WIDTH:ROPE_LO].astype(BF16)
        qkv_ref[:, ROPE_LO:ROPE_LO + SW_WIDTH] = (rb[:, 0:SW_WIDTH] * Q_SCALE).astype(BF16)
        qkv_ref[:, ROPE_LO + SW_WIDTH:ROPE_LO + ROPE_WIDTH] = rb[:, SW_WIDTH:].astype(BF16)
        qkv_ref[:, ROPE_LO + ROPE_WIDTH:] = proj[:, ROPE_LO + ROPE_WIDTH:].astype(BF16)

    return _hosted_call(
        body, "attn_in", (T // TOKEN_TILE,),
        [_tok_spec(D), _mod_spec(tps, D), _full((1, D)), _full(w_in.shape),
         pl.BlockSpec((TOKEN_TILE, ROPE_WIDTH), lambda i: (i % tps, 0)),
         pl.BlockSpec((TOKEN_TILE, ROPE_WIDTH), lambda i: (i % tps, 0))],
        [_tok_spec(D), _tok_spec(IN_WIDTH)],
        [jax.ShapeDtypeStruct((T, D), BF16), jax.ShapeDtypeStruct((T, IN_WIDTH), BF16)],
        (x2d, mod3, g_attn, w_in, cos_t, sin_t), tasks)


def _attn_out(oa, ob, x2d, mod3, g_na, g_sw, w_out, S):
    T, D = x2d.shape
    tps = S // TOKEN_TILE

    def body(oa_ref, ob_ref, x_ref, mod_ref, gna_ref, gsw_ref, w_ref, mixin_ref, mix_ref, x1_ref):
        oan, _ = _rms(oa_ref[...])
        obn, _ = _rms(ob_ref[...])
        mixin = jnp.concatenate([oan * gna_ref[...], obn * gsw_ref[...]], axis=1).astype(BF16)
        mixin_ref[...] = mixin
        mix = _nn(mixin, w_ref[...])
        mix_ref[...] = mix
        x1_ref[...] = x_ref[...] + mod_ref[0, 2:3, :] * mix

    return pl.pallas_call(
        body, name="attn_out", grid=(T // TOKEN_TILE,),
        in_specs=[_tok_spec(NA_WIDTH), _tok_spec(SW_WIDTH), _tok_spec(D), _mod_spec(tps, D),
                  _full((1, NA_WIDTH)), _full((1, SW_WIDTH)), _full(w_out.shape)],
        out_specs=[_tok_spec(NA_WIDTH + SW_WIDTH), _tok_spec(D), _tok_spec(D)],
        out_shape=[jax.ShapeDtypeStruct((T, NA_WIDTH + SW_WIDTH), BF16), jax.ShapeDtypeStruct((T, D), F32),
                   jax.ShapeDtypeStruct((T, D), F32)],
        compiler_params=_params("parallel"),
    )(oa, ob, x2d, mod3, g_na, g_sw, w_out)


def _ffn_up(x1, mod3, g_ffn, w_up, S):
    T, D = x1.shape
    F = w_up.shape[0] // 2
    tps = S // TOKEN_TILE

    def body(x1_ref, mod_ref, g_ref, w_ref, h2_ref, val_ref, gt_ref):
        xn, _ = _rms(x1_ref[...])
        h2 = ((xn * g_ref[...]) * (1.0 + mod_ref[0, 4:5, :]) + mod_ref[0, 3:4, :]).astype(BF16)
        h2_ref[...] = h2
        u = _nt(h2, w_ref[...])
        val_ref[...] = u[:, :F].astype(BF16)
        gt_ref[...] = u[:, F:].astype(BF16)

    return pl.pallas_call(
        body, name="ffn_up", grid=(T // TOKEN_TILE,),
        in_specs=[_tok_spec(D), _mod_spec(tps, D), _full((1, D)), _full(w_up.shape)],
        out_specs=[_tok_spec(D), _tok_spec(F), _tok_spec(F)],
        out_shape=[jax.ShapeDtypeStruct((T, D), BF16), jax.ShapeDtypeStruct((T, F), BF16), jax.ShapeDtypeStruct((T, F), BF16)],
        compiler_params=_params("parallel"),
    )(x1, mod3, g_ffn, w_up)


def _halo_specs(T, tps, w):
    per = TOKEN_TILE // 8
    prev = pl.BlockSpec((8, w), lambda i: (jnp.maximum(i * per - 1, 0), 0))
    nxt = pl.BlockSpec((8, w), lambda i: (jnp.minimum((i + 1) * per, T // 8 - 1), 0))
    return prev, nxt


def _seq_shifts(cur, before, after, ti, tps):
    tm = cur.shape[0]
    row = lax.broadcasted_iota(jnp.int32, cur.shape, 0)
    before = jnp.where(ti > 0, before.astype(F32), 0.0)
    after = jnp.where(ti < tps - 1, after.astype(F32), 0.0)
    return jnp.where(row == 0, before, pltpu.roll(cur, 1, axis=0)), jnp.where(row == tm - 1, after, pltpu.roll(cur, tm - 1, axis=0))


def _ffn_down(gt, val, conv_w, conv_b, w_down, x1, mod3, g_final, target, B, S):
    T, D = x1.shape
    F = gt.shape[1]
    tps = S // TOKEN_TILE
    prev, nxt = _halo_specs(T, tps, F)

    def body(gt_ref, prev_ref, next_ref, val_ref, cw_ref, cb_ref, w_ref, x1_ref, mod_ref, gf_ref, tgt_ref,
             a_ref, act_ref, vd_ref, dx2_ref, df_ref, gstat_ref, bstat_ref):
        i = pl.program_id(0)
        g = gt_ref[...].astype(F32)
        gprev, gnext = _seq_shifts(g, prev_ref[7:8, :], next_ref[0:1, :], i % tps, tps)
        gc = gprev * cw_ref[0:1, :] + g * cw_ref[1:2, :] + gnext * cw_ref[2:3, :] + cb_ref[...]
        sig = 1.0 / (1.0 + jnp.exp(-gc))
        act = gc * sig
        val = val_ref[...].astype(F32)
        act_ref[...] = act.astype(BF16)
        vd_ref[...] = (val * (sig + act - act * sig)).astype(BF16)
        a = (act * val).astype(BF16)
        a_ref[...] = a
        f = _nn(a, w_ref[...])
        gate = mod_ref[0, 5:6, :]
        x2 = x1_ref[...] + gate * f
        xn, r = _rms(x2)
        err = xn * gf_ref[...] - tgt_ref[...]
        dy = err * (1.0 / D)
        dx2 = _rms_bwd(xn, r, dy * gf_ref[...])
        dx2_ref[...] = dx2
        df_ref[...] = (gate * dx2).astype(BF16)

        @pl.when(i == 0)
        def _():
            gstat_ref[...] = jnp.zeros_like(gstat_ref)

        @pl.when(i % tps == 0)
        def _():
            bstat_ref[...] = jnp.zeros_like(bstat_ref)

        gstat_ref[0:1, :] += jnp.sum(dy * xn, axis=0, keepdims=True)
        tile_loss = jnp.sum(jnp.sum(err * err, axis=1, keepdims=True), axis=0, keepdims=True) * (0.5 / D)
        gstat_ref[1:2, :] += jnp.broadcast_to(tile_loss, (1, D))
        bstat_ref[0, 0:1, :] += jnp.sum(dx2 * f, axis=0, keepdims=True)

    return pl.pallas_call(
        body, name="ffn_down", grid=(T // TOKEN_TILE,),
        in_specs=[_tok_spec(F), prev, nxt, _tok_spec(F), _full(conv_w.shape), _full((1, F)), _full(w_down.shape),
                  _tok_spec(D), _mod_spec(tps, D), _full((1, D)), _tok_spec(D)],
        out_specs=[_tok_spec(F), _tok_spec(F), _tok_spec(F), _tok_spec(D), _tok_spec(D), _full((8, D)), _bstat_spec(tps, D)],
        out_shape=[jax.ShapeDtypeStruct((T, F), BF16), jax.ShapeDtypeStruct((T, F), BF16), jax.ShapeDtypeStruct((T, F), BF16),
                   jax.ShapeDtypeStruct((T, D), F32), jax.ShapeDtypeStruct((T, D), BF16),
                   jax.ShapeDtypeStruct((8, D), F32), jax.ShapeDtypeStruct((B, 8, D), F32)],
        compiler_params=_params("arbitrary"),
    )(gt, gt, gt, val, conv_w, conv_b, w_down, x1, mod3, g_final, target)


def _ffn_down_bwd(df, w_down, act, vd, tasks=()):
    T, D = df.shape
    F = act.shape[1]

    def body(df_ref, w_ref, act_ref, vd_ref, dval_ref, dgc_ref, cstat_ref):
        da = _nt(df_ref[...], w_ref[...])
        dval_ref[...] = (da * act_ref[...].astype(F32)).astype(BF16)
        dgc = da * vd_ref[...].astype(F32)
        dgc_ref[...] = dgc.astype(BF16)

        @pl.when(pl.program_id(0) == 0)
        def _():
            cstat_ref[...] = jnp.zeros_like(cstat_ref)

        cstat_ref[0:1, :] += jnp.sum(dgc, axis=0, keepdims=True)

    return _hosted_call(
        body, "ffn_down_bwd", (T // TOKEN_TILE,),
        [_tok_spec(D), _full(w_down.shape), _tok_spec(F), _tok_spec(F)],
        [_tok_spec(F), _tok_spec(F), _full((8, F))],
        [jax.ShapeDtypeStruct((T, F), BF16), jax.ShapeDtypeStruct((T, F), BF16), jax.ShapeDtypeStruct((8, F), F32)],
        (df, w_down, act, vd), tasks)


def _dw_up_conv(dgc, dval, gt, conv_w, h2, S, tk=512):
    T, F = dgc.shape
    D = h2.shape[1]
    nk, tps, per = T // tk, S // tk, tk // 8
    fc = F // 2 if F % 256 == 0 else F
    nc = F // fc
    gated = lambda i: i // nc
    prev = pl.BlockSpec((8, fc), lambda i, k: (jnp.maximum(k * per - 1, 0) * gated(i), i % nc))
    nxt = pl.BlockSpec((8, fc), lambda i, k: (jnp.minimum((k + 1) * per, T // 8 - 1) * gated(i), i % nc))
    gate_half = pl.BlockSpec((tk, fc), lambda i, k: (k * gated(i), i % nc))
    val_half = pl.BlockSpec((tk, fc), lambda i, k: (k * (1 - gated(i)), i % nc))

    def body(dgc_ref, prev_ref, next_ref, dval_ref, gt_ref, cw_ref, h2_ref, du_ref, dw_ref, cstat_ref, acc):
        i, k = pl.program_id(0), pl.program_id(1)

        @pl.when(k == 0)
        def _():
            acc[...] = jnp.zeros_like(acc)
            cstat_ref[...] = jnp.zeros_like(cstat_ref)

        @pl.when(i < nc)
        def _():
            dv = dval_ref[...]
            du_ref[...] = dv
            acc[...] += _tn(dv, h2_ref[...])

        @pl.when(i >= nc)
        def _():
            d = dgc_ref[...].astype(F32)
            dprev, dnext = _seq_shifts(d, prev_ref[7:8, :], next_ref[0:1, :], k % tps, tps)
            g = gt_ref[...].astype(F32)
            cstat_ref[1:2, :] += jnp.sum(dnext * g, axis=0, keepdims=True)
            cstat_ref[2:3, :] += jnp.sum(d * g, axis=0, keepdims=True)
            cstat_ref[3:4, :] += jnp.sum(dprev * g, axis=0, keepdims=True)
            dgt = (dnext * cw_ref[0:1, :] + d * cw_ref[1:2, :] + dprev * cw_ref[2:3, :]).astype(BF16)
            du_ref[...] = dgt
            acc[...] += _tn(dgt, h2_ref[...])

        @pl.when(k == nk - 1)
        def _():
            dw_ref[...] = acc[...].astype(BF16)

    return pl.pallas_call(
        body, name="dw_up", grid=(2 * nc, nk),
        in_specs=[gate_half, prev, nxt, val_half, gate_half, pl.BlockSpec((conv_w.shape[0], fc), lambda i, k: (0, i % nc)),
                  pl.BlockSpec((tk, D), lambda i, k: (k, 0))],
        out_specs=[pl.BlockSpec((tk, fc), lambda i, k: (k, i)), pl.BlockSpec((fc, D), lambda i, k: (i, 0)),
                   pl.BlockSpec((8, fc), lambda i, k: (gated(i), i % nc))],
        out_shape=[jax.ShapeDtypeStruct((T, 2 * F), BF16), jax.ShapeDtypeStruct((2 * F, D), BF16), jax.ShapeDtypeStruct((16, F), F32)],
        scratch_shapes=[pltpu.VMEM((fc, D), F32)],
        compiler_params=_params("arbitrary", "arbitrary"),
    )(dgc, dgc, dgc, dval, gt, conv_w, h2)


def _ffn_up_bwd(du, w_up, x1, mod3, g_ffn, dx2, mix, B, S, tasks=()):
    T, D = x1.shape
    F = du.shape[1] // 2
    tps = S // TOKEN_TILE

    def body(du_ref, w_ref, x1_ref, mod_ref, g_ref, dx2_ref, mix_ref, dx1_ref, dmix_ref, gstat_ref, bstat_ref):
        i = pl.program_id(0)
        dh2 = _nn(du_ref[...], w_ref[...])
        xn, r = _rms(x1_ref[...])
        scale1 = 1.0 + mod_ref[0, 4:5, :]
        xg = xn * g_ref[...]
        dx1 = dx2_ref[...] + _rms_bwd(xn, r, dh2 * g_ref[...] * scale1)
        dx1_ref[...] = dx1
        dmix_ref[...] = (mod_ref[0, 2:3, :] * dx1).astype(BF16)

        @pl.when(i == 0)
        def _():
            gstat_ref[...] = jnp.zeros_like(gstat_ref)

        @pl.when(i % tps == 0)
        def _():
            bstat_ref[...] = jnp.zeros_like(bstat_ref)

        gstat_ref[0:1, :] += jnp.sum(dh2 * scale1 * xn, axis=0, keepdims=True)
        bstat_ref[0, 0:1, :] += jnp.sum(dh2, axis=0, keepdims=True)
        bstat_ref[0, 1:2, :] += jnp.sum(dh2 * xg, axis=0, keepdims=True)
        bstat_ref[0, 2:3, :] += jnp.sum(dx1 * mix_ref[...], axis=0, keepdims=True)

    return _hosted_call(
        body, "ffn_up_bwd", (T // TOKEN_TILE,),
        [_tok_spec(2 * F), _full(w_up.shape), _tok_spec(D), _mod_spec(tps, D), _full((1, D)), _tok_spec(D), _tok_spec(D)],
        [_tok_spec(D), _tok_spec(D), _full((8, D)), _bstat_spec(tps, D)],
        [jax.ShapeDtypeStruct((T, D), F32), jax.ShapeDtypeStruct((T, D), BF16),
         jax.ShapeDtypeStruct((8, D), F32), jax.ShapeDtypeStruct((B, 8, D), F32)],
        (du, w_up, x1, mod3, g_ffn, dx2, mix), tasks)


def _attn_out_bwd(dmix, w_out, oa, ob, g_na, g_sw, tasks=()):
    T, D = dmix.shape

    def body(dmix_ref, w_ref, oa_ref, ob_ref, gna_ref, gsw_ref, doa_ref, dob_ref, gstat_ref):
        dmixin = _nt(dmix_ref[...], w_ref[...])

        @pl.when(pl.program_id(0) == 0)
        def _():
            gstat_ref[...] = jnp.zeros_like(gstat_ref)

        for k, (o_ref, g_ref, do_ref) in enumerate(((oa_ref, gna_ref, doa_ref), (ob_ref, gsw_ref, dob_ref))):
            dn = dmixin[:, k * NA_WIDTH:(k + 1) * NA_WIDTH]
            on, r = _rms(o_ref[...])
            gstat_ref[k:k + 1, :] += jnp.sum(dn * on, axis=0, keepdims=True)
            do_ref[...] = _rms_bwd(on, r, dn * g_ref[...]).astype(BF16)

    hs = jax.ShapeDtypeStruct((T, NA_WIDTH), BF16)
    return _hosted_call(
        body, "attn_out_bwd", (T // TOKEN_TILE,),
        [_tok_spec(D), _full(w_out.shape), _tok_spec(NA_WIDTH), _tok_spec(SW_WIDTH), _full((1, NA_WIDTH)), _full((1, SW_WIDTH))],
        [_tok_spec(NA_WIDTH), _tok_spec(SW_WIDTH), _full((8, NA_WIDTH))],
        [hs, hs, jax.ShapeDtypeStruct((8, NA_WIDTH), F32)],
        (dmix, w_out, oa, ob, g_na, g_sw), tasks)


def _attn_in_bwd(dqa, dka, dva, dqb, dkb, dvb, cos_t, sin_t, w_in, x2d, mod3, g_attn, dx1, B, S):
    T, D = x2d.shape
    tps = S // TOKEN_TILE

    def body(dqa_ref, dka_ref, dva_ref, dqb_ref, dkb_ref, dvb_ref, cos_ref, sin_ref, w_ref, x_ref, mod_ref, g_ref, dx1_ref,
             gx_ref, dproj_ref, gstat_ref, bstat_ref):
        i = pl.program_id(0)
        drb = jnp.concatenate([dqb_ref[...] * Q_SCALE, dkb_ref[...]], axis=1)
        drb = drb * cos_ref[...] + _rot_half(drb * sin_ref[...])
        dproj = jnp.concatenate([dqa_ref[...] * Q_SCALE, dka_ref[...], dva_ref[...], drb, dvb_ref[...]], axis=1).astype(BF16)
        dproj_ref[...] = dproj
        dh = _nn(dproj, w_ref[...])
        xn, r = _rms(x_ref[...])
        scale1 = 1.0 + mod_ref[0, 1:2, :]
        gx_ref[...] = dx1_ref[...] + _rms_bwd(xn, r, dh * g_ref[...] * scale1)

        @pl.when(i == 0)
        def _():
            gstat_ref[...] = jnp.zeros_like(gstat_ref)

        @pl.when(i % tps == 0)
        def _():
            bstat_ref[...] = jnp.zeros_like(bstat_ref)

        gstat_ref[0:1, :] += jnp.sum(dh * scale1 * xn, axis=0, keepdims=True)
        bstat_ref[0, 0:1, :] += jnp.sum(dh, axis=0, keepdims=True)
        bstat_ref[0, 1:2, :] += jnp.sum(dh * (xn * g_ref[...]), axis=0, keepdims=True)

    rope = pl.BlockSpec((TOKEN_TILE, ROPE_WIDTH), lambda i: (i % tps, 0))
    return pl.pallas_call(
        body, name="attn_in_bwd", grid=(T // TOKEN_TILE,),
        in_specs=[_tok_spec(NA_WIDTH), _tok_spec(NA_WIDTH), _tok_spec(NA_WIDTH), _tok_spec(SW_WIDTH), _tok_spec(SW_KV_WIDTH),
                  _tok_spec(SW_KV_WIDTH), rope, rope, _full(w_in.shape), _tok_spec(D), _mod_spec(tps, D), _full((1, D)), _tok_spec(D)],
        out_specs=[_tok_spec(D), _tok_spec(IN_WIDTH), _full((8, D)), _bstat_spec(tps, D)],
        out_shape=[jax.ShapeDtypeStruct((T, D), F32), jax.ShapeDtypeStruct((T, IN_WIDTH), BF16),
                   jax.ShapeDtypeStruct((8, D), F32), jax.ShapeDtypeStruct((B, 8, D), F32)],
        compiler_params=_params("arbitrary"),
    )(dqa, dka, dva, dqb, dkb, dvb, cos_t, sin_t, w_in, x2d, mod3, g_attn, dx1)


def _matmul_tn(a, b, name, tm=None, tk=512):
    T, M = a.shape
    N = b.shape[1]
    tm = M if tm is None else tm
    nk = T // tk

    def body(a_ref, b_ref, o_ref, acc):
        k = pl.program_id(1)

        @pl.when(k == 0)
        def _():
            acc[...] = jnp.zeros_like(acc)

        acc[...] += _tn(a_ref[...], b_ref[...])

        @pl.when(k == nk - 1)
        def _():
            o_ref[...] = acc[...].astype(BF16)

    return pl.pallas_call(
        body, name=name, grid=(M // tm, nk),
        in_specs=[pl.BlockSpec((tk, tm), lambda i, k: (k, i)), pl.BlockSpec((tk, N), lambda i, k: (k, 0))],
        out_specs=pl.BlockSpec((tm, N), lambda i, k: (i, 0)),
        out_shape=jax.ShapeDtypeStruct((M, N), BF16),
        scratch_shapes=[pltpu.VMEM((tm, N), F32)],
        compiler_params=_params("parallel", "arbitrary"),
    )(a, b)


def _na_geometry(S):
    rows = S // GRID_W
    wr = min(NA_ROWS_MAX, rows)
    return rows, wr


def _na_window(r, rows, wr):
    rs = jnp.clip(r - wr // 2, 0, rows - wr)
    return pl.multiple_of(rs * GRID_W, GRID_W), pl.multiple_of((rs - r + NA_ROWS_MAX - 1) * GRID_W, GRID_W)


NA_STEP_PAIRS = 2
NA_GW = NA_STEP_PAIRS * 128
NA_BWD_ROWS = 2
NA_ROWS_PER_STEP = 4


def _na_specs(S, kw_n, order):
    ng = NA_PAIRS // NA_STEP_PAIRS

    def col(k):
        return pl.BlockSpec((1, S, NA_GW), lambda *ids: (order(*ids)[0], 0, k * ng + order(*ids)[1]))
    bias = pl.BlockSpec((NA_STEP_PAIRS, N_DR * GRID_W, 128), lambda *ids: (order(*ids)[1], 0, 0))
    out = pl.BlockSpec((1, S, NA_GW), lambda *ids: (order(*ids)[0], 0, order(*ids)[1]))
    return col(0), col(1), col(2), bias, out


def _block_diag(t):
    left = lax.broadcasted_iota(jnp.int32, t.shape, 1) < HEAD_DIM
    zero = jnp.zeros_like(t)
    return jnp.concatenate([jnp.where(left, t, zero), jnp.where(left, zero, t)], axis=0)


def _diag_blocks(res):
    left = lax.broadcasted_iota(jnp.int32, (HEAD_DIM, 128), 1) < HEAD_DIM
    return jnp.where(left, res[:HEAD_DIM], res[HEAD_DIM:])


def _col_softmax(st):
    e = jnp.exp(st - jnp.max(st, axis=0, keepdims=True))
    return e * (1.0 / jnp.sum(e, axis=0, keepdims=True))


def _na_fwd(qkv, bias, tasks=()):
    B, S, _ = qkv.shape
    rows, wr = _na_geometry(S)
    kw_n = wr * GRID_W

    def body(q_ref, k_ref, v_ref, b_ref, o_ref):
        def step(it, carry):
            win = [_na_window(it * NA_ROWS_PER_STEP + u, rows, wr) for u in range(NA_ROWS_PER_STEP)]
            qrows = [pl.ds(pl.multiple_of((it * NA_ROWS_PER_STEP + u) * GRID_W, GRID_W), GRID_W) for u in range(NA_ROWS_PER_STEP)]
            krows = [pl.ds(w[0], kw_n) for w in win]
            brows = [pl.ds(w[1], kw_n) for w in win]
            lanes = [pl.ds(p * 128, 128) for p in range(NA_STEP_PAIRS)]
            chains = [(u, p) for u in range(NA_ROWS_PER_STEP) for p in range(NA_STEP_PAIRS)]
            st = {(u, p): _nt(k_ref[0, krows[u], lanes[p]], _block_diag(q_ref[0, qrows[u], lanes[p]])) for u, p in chains}
            pn = {(u, p): _col_softmax(st[(u, p)] + b_ref[p, brows[u], :]).astype(BF16) for u, p in chains}
            out = {(u, p): _diag_blocks(_tn(pn[(u, p)], v_ref[0, krows[u], lanes[p]])) for u, p in chains}
            for u in range(NA_ROWS_PER_STEP):
                o_ref[0, qrows[u], :] = jnp.concatenate([out[(u, p)] for p in range(NA_STEP_PAIRS)], axis=1)
            return carry

        lax.fori_loop(0, rows // NA_ROWS_PER_STEP, step, 0)

    q, k, v, bs, out = _na_specs(S, kw_n, lambda b, g: (b, g))
    return _hosted_call(body, "na_fwd", (B, NA_PAIRS // NA_STEP_PAIRS), [q, k, v, bs], [out],
                        [jax.ShapeDtypeStruct((B, S, NA_WIDTH), F32)], (qkv, qkv, qkv, bias), tasks)


def _na_bwd(qkv, bias, doa, tasks=()):
    B, S, _ = qkv.shape
    rows, wr = _na_geometry(S)
    kw_n = wr * GRID_W

    def body(q_ref, k_ref, v_ref, b_ref, do_ref, dq_ref, dk_ref, dv_ref, db_ref):
        @pl.when(pl.program_id(1) == 0)
        def _():
            db_ref[...] = jnp.zeros_like(db_ref)

        dk_ref[...] = jnp.zeros_like(dk_ref)
        dv_ref[...] = jnp.zeros_like(dv_ref)

        def step(it, carry):
            nu, pairs = range(NA_BWD_ROWS), range(NA_STEP_PAIRS)
            win = [_na_window(it * NA_BWD_ROWS + u, rows, wr) for u in nu]
            qrows = [pl.ds(pl.multiple_of((it * NA_BWD_ROWS + u) * GRID_W, GRID_W), GRID_W) for u in nu]
            krows = [pl.ds(w[0], kw_n) for w in win]
            brows = [pl.ds(w[1], kw_n) for w in win]
            lanes = [pl.ds(p * 128, 128) for p in pairs]
            chains = [(u, p) for u in nu for p in pairs]
            kp = {(u, p): k_ref[0, krows[u], lanes[p]] for u, p in chains}
            qbd = {(u, p): _block_diag(q_ref[0, qrows[u], lanes[p]]) for u, p in chains}
            dobd = {(u, p): _block_diag(do_ref[0, qrows[u], lanes[p]]) for u, p in chains}
            st = {c: _nt(kp[c], qbd[c]) for c in chains}
            dpt = {(u, p): _nt(v_ref[0, krows[u], lanes[p]], dobd[(u, p)]) for u, p in chains}
            pn = {(u, p): _col_softmax(st[(u, p)] + b_ref[p, brows[u], :]) for u, p in chains}
            dst = {c: pn[c] * (dpt[c] - jnp.sum(pn[c] * dpt[c], axis=0, keepdims=True)) for c in chains}
            dsb = {c: dst[c].astype(BF16) for c in chains}
            dq = {c: _diag_blocks(_tn(dsb[c], kp[c])) for c in chains}
            dk = {c: _nn(dsb[c], qbd[c]) for c in chains}
            dv = {c: _nn(pn[c].astype(BF16), dobd[c]) for c in chains}
            for u in nu:
                dq_ref[0, qrows[u], :] = jnp.concatenate([dq[(u, p)] for p in pairs], axis=1)
                dk_ref[0, krows[u], :] += jnp.concatenate([dk[(u, p)] for p in pairs], axis=1)
                dv_ref[0, krows[u], :] += jnp.concatenate([dv[(u, p)] for p in pairs], axis=1)
                for p in pairs:
                    db_ref[p, brows[u], :] += dst[(u, p)]
            return carry

        lax.fori_loop(0, rows // NA_BWD_ROWS, step, 0)

    q, k, v, bs, out = _na_specs(S, kw_n, lambda g, b: (b, g))
    hs = jax.ShapeDtypeStruct((B, S, NA_WIDTH), F32)
    return _hosted_call(body, "na_bwd", (NA_PAIRS // NA_STEP_PAIRS, B), [q, k, v, bs, out], [out, out, out, bs],
                        [hs, hs, hs, jax.ShapeDtypeStruct((NA_PAIRS, N_DR * GRID_W, 128), F32)], (qkv, qkv, qkv, bias, doa), tasks)


SW_PAIRS = SW_HEADS // 2


def _sw_band(n, S):
    kw_n = 3 * SW_BLOCK
    start = pl.multiple_of(jnp.clip(n * SW_BLOCK - SW_BLOCK, 0, S - kw_n), SW_BLOCK)
    kpos = start + lax.broadcasted_iota(jnp.int32, (kw_n, SW_BLOCK), 0)
    qpos = n * SW_BLOCK + lax.broadcasted_iota(jnp.int32, (kw_n, SW_BLOCK), 1)
    return start, jnp.abs(qpos - kpos) <= SW_WINDOW


def _kv_halves(t):
    left = lax.broadcasted_iota(jnp.int32, t.shape, 1) < HEAD_DIM
    swapped = pltpu.roll(t, HEAD_DIM, axis=1)
    zero = jnp.zeros_like(t)
    return {(0, 0): jnp.where(left, t, zero), (0, 1): jnp.where(left, zero, swapped),
            (1, 0): jnp.where(left, swapped, zero), (1, 1): jnp.where(left, zero, t)}


def _sw_probs(st, ok, sk):
    st = jnp.where(ok, st, NEG)
    m = jnp.maximum(jnp.max(st, axis=0, keepdims=True), sk)
    e = jnp.exp(st - m)
    esk = jnp.exp(sk - m)
    inv = 1.0 / (jnp.sum(e, axis=0, keepdims=True) + esk)
    return e * inv, esk * inv


def _sw_specs(S):
    q = pl.BlockSpec((1, S, SW_WIDTH), lambda b, *_: (b, 0, ROPE_LO // SW_WIDTH))
    k = pl.BlockSpec((1, S, SW_KV_WIDTH), lambda b, *_: (b, 0, (ROPE_LO + SW_WIDTH) // SW_KV_WIDTH))
    v = pl.BlockSpec((1, S, SW_KV_WIDTH), lambda b, *_: (b, 0, (ROPE_LO + ROPE_WIDTH) // SW_KV_WIDTH))
    return q, k, v


SW_FWD_SPLIT = 2


def _sw_fwd(sink, qkv, tasks=()):
    B, S, _ = qkv.shape
    kw_n = 3 * SW_BLOCK

    def body(sink_ref, q_ref, k_ref, v_ref, o_ref):
        def step(n, carry):
            start, ok = _sw_band(n, S)
            qrows = pl.ds(pl.multiple_of(n * SW_BLOCK, SW_BLOCK), SW_BLOCK)
            krows = pl.ds(start, kw_n)
            kh, vh = _kv_halves(k_ref[0, krows, :]), _kv_halves(v_ref[0, krows, :])
            heads = [(p, e) for p in range(SW_PAIRS) for e in range(2)]
            qp = [q_ref[0, qrows, pl.ds(p * 128, 128)] for p in range(SW_PAIRS)]
            kv_of = lambda p: p // (SW_PAIRS // SW_KV_HEADS)
            st = {(p, e): _nt(kh[(kv_of(p), e)], qp[p]) for p, e in heads}
            pn = {(p, e): _sw_probs(st[(p, e)], ok, sink_ref[2 * p + e])[0].astype(BF16) for p, e in heads}
            outs = [_tn(pn[(p, 0)], vh[(kv_of(p), 0)]) + _tn(pn[(p, 1)], vh[(kv_of(p), 1)]) for p in range(SW_PAIRS)]
            o_ref[0, qrows, :] = jnp.concatenate(outs, axis=1)
            return carry

        half = (S // SW_BLOCK) // SW_FWD_SPLIT
        lax.fori_loop(pl.program_id(1) * half, (pl.program_id(1) + 1) * half, step, 0)

    q, k, v = _sw_specs(S)
    return _hosted_call(
        body, "sw_fwd", (B, SW_FWD_SPLIT), [pl.BlockSpec(memory_space=pltpu.SMEM), q, k, v],
        [pl.BlockSpec((1, S, SW_WIDTH), lambda b, s: (b, 0, 0))], [jax.ShapeDtypeStruct((B, S, SW_WIDTH), F32)],
        (sink, qkv, qkv, qkv), tasks)


def _sw_bwd(sink, qkv, dob):
    B, S, _ = qkv.shape
    kw_n = 3 * SW_BLOCK

    fold_rows = 256

    def body(sink_ref, q_ref, k_ref, v_ref, do_ref, dq_ref, dk_ref, dv_ref, dsink_ref, dk_acc, dv_acc):
        @pl.when(pl.program_id(0) == 0)
        def _():
            dsink_ref[...] = jnp.zeros_like(dsink_ref)

        dk_acc[...] = jnp.zeros_like(dk_acc)
        dv_acc[...] = jnp.zeros_like(dv_acc)
        ppk = SW_PAIRS // SW_KV_HEADS

        def step(n, carry):
            start, ok = _sw_band(n, S)
            qrows = pl.ds(pl.multiple_of(n * SW_BLOCK, SW_BLOCK), SW_BLOCK)
            krows = pl.ds(start, kw_n)
            kh, vh = _kv_halves(k_ref[0, krows, :]), _kv_halves(v_ref[0, krows, :])
            heads = [(p, e) for p in range(SW_PAIRS) for e in range(2)]
            qp = [q_ref[0, qrows, pl.ds(p * 128, 128)] for p in range(SW_PAIRS)]
            dop = [do_ref[0, qrows, pl.ds(p * 128, 128)] for p in range(SW_PAIRS)]
            st = {(p, e): _nt(kh[(p // ppk, e)], qp[p]) for p, e in heads}
            dpt = {(p, e): _nt(vh[(p // ppk, e)], dop[p]) for p, e in heads}
            pnb, dsb = {}, {}
            for p, e in heads:
                pn, psink = _sw_probs(st[(p, e)], ok, sink_ref[2 * p + e])
                delta = jnp.sum(pn * dpt[(p, e)], axis=0, keepdims=True)
                dsb[(p, e)] = (pn * (dpt[(p, e)] - delta)).astype(BF16)
                pnb[(p, e)] = pn.astype(BF16)
                dsink_ref[2 * p + e:2 * p + e + 1, :] += -(psink * delta)
            dq_ref[0, qrows, :] = jnp.concatenate(
                [_tn(dsb[(p, 0)], kh[(p // ppk, 0)]) + _tn(dsb[(p, 1)], kh[(p // ppk, 1)]) for p in range(SW_PAIRS)], axis=1)
            left = lax.broadcasted_iota(jnp.int32, (kw_n, 128), 1) < HEAD_DIM
            dks, dvs = [], []
            for kv in range(SW_KV_HEADS):
                dk = dv = None
                for p in range(kv * ppk, (kv + 1) * ppk):
                    dk_p = jnp.where(left, _nn(dsb[(p, 0)], qp[p]), _nn(dsb[(p, 1)], qp[p]))
                    dv_p = jnp.where(left, _nn(pnb[(p, 0)], dop[p]), _nn(pnb[(p, 1)], dop[p]))
                    dk = dk_p if dk is None else dk + dk_p
                    dv = dv_p if dv is None else dv + dv_p
                dks.append(dk)
                dvs.append(dv)
            dk_acc[krows, :] += jnp.concatenate(dks, axis=1)
            dv_acc[krows, :] += jnp.concatenate(dvs, axis=1)
            return carry

        lax.fori_loop(0, S // SW_BLOCK, step, 0)

        def fold(i, carry):
            rows = pl.ds(pl.multiple_of(i * fold_rows, fold_rows), fold_rows)
            left = lax.broadcasted_iota(jnp.int32, (fold_rows, 128), 1) < HEAD_DIM
            for acc, out_ref in ((dk_acc, dk_ref), (dv_acc, dv_ref)):
                a, b = acc[rows, 0:128], acc[rows, 128:256]
                out_ref[0, rows, :] = jnp.where(left, a + pltpu.roll(a, HEAD_DIM, axis=1), b + pltpu.roll(b, HEAD_DIM, axis=1))
            return carry

        lax.fori_loop(0, S // fold_rows, fold, 0)

        @pl.when(pl.program_id(0) == B - 1)
        def _():
            dsink_ref[...] = jnp.broadcast_to(jnp.sum(dsink_ref[...], axis=1, keepdims=True), dsink_ref.shape)

    q, k, v = _sw_specs(S)
    qo = pl.BlockSpec((1, S, SW_WIDTH), lambda b: (b, 0, 0))
    ko = pl.BlockSpec((1, S, SW_KV_WIDTH), lambda b: (b, 0, 0))
    return pl.pallas_call(
        body, name="sw_bwd", grid=(B,),
        in_specs=[pl.BlockSpec(memory_space=pltpu.SMEM), q, k, v, qo],
        out_specs=[qo, ko, ko, _full((SW_HEADS, 128))],
        out_shape=[jax.ShapeDtypeStruct((B, S, SW_WIDTH), F32), jax.ShapeDtypeStruct((B, S, SW_KV_WIDTH), F32),
                   jax.ShapeDtypeStruct((B, S, SW_KV_WIDTH), F32), jax.ShapeDtypeStruct((SW_HEADS, 128), F32)],
        scratch_shapes=[pltpu.VMEM((S, 2 * SW_KV_WIDTH), F32), pltpu.VMEM((S, 2 * SW_KV_WIDTH), F32)],
        compiler_params=_params("arbitrary"),
    )(sink, qkv, qkv, qkv, dob)


def _pack_sum(packs):
    W = packs.shape[1]

    def body(p_ref, o_ref, s_ref):
        tot = p_ref[0:8, :]
        for d in range(1, N_DEV):
            tot = tot + p_ref[8 * d:8 * d + 8, :]
        o_ref[...] = tot
        s_ref[...] = tot[0:1, :] + tot[1:2, :]

    return pl.pallas_call(body, name="pack_sum", out_shape=[jax.ShapeDtypeStruct((8, W), F32), jax.ShapeDtypeStruct((1, W), F32)],
                          compiler_params=pltpu.CompilerParams(vmem_limit_bytes=VMEM_LIMIT))(packs)


def _adam_update(w, g, m, v):
    c1 = 1.0 - ADAM_B1 ** ADAM_STEP
    c2 = 1.0 - ADAM_B2 ** ADAM_STEP
    nm = ADAM_B1 * m + (1.0 - ADAM_B1) * g
    nv = ADAM_B2 * v + (1.0 - ADAM_B2) * (g * g)
    return -ADAM_LR * ((nm / c1) / (jnp.sqrt(nv / c2) + ADAM_EPS) + ADAM_WD * w), nm, nv


def _adamw(w, g, m, v, name):
    def body(w_ref, g_ref, m_ref, v_ref, d_ref, nm_ref, nv_ref):
        d_ref[...], nm_ref[...], nv_ref[...] = _adam_update(w_ref[...], g_ref[...], m_ref[...], v_ref[...])

    s = jax.ShapeDtypeStruct(w.shape, F32)
    return pl.pallas_call(body, name=name, out_shape=[s, s, s],
                          compiler_params=pltpu.CompilerParams(vmem_limit_bytes=VMEM_LIMIT))(w, g, m, v)


def _sum_adamw(own, recvb, w, m, v, name):
    R, C = own.shape
    rc = _row_chunk(R)

    def body(own_ref, r_ref, w_ref, m_ref, v_ref, g_ref, d_ref, nm_ref, nv_ref):
        def chunk(i, carry):
            rows = pl.ds(pl.multiple_of(i * rc, rc), rc)
            g = own_ref[rows, :]
            for j in range(3):
                g = g + r_ref[j, rows, :].astype(F32)
            g_ref[rows, :] = g
            d_ref[rows, :], nm_ref[rows, :], nv_ref[rows, :] = _adam_update(w_ref[rows, :], g, m_ref[rows, :], v_ref[rows, :])
            return carry

        lax.fori_loop(0, R // rc, chunk, 0)

    s = jax.ShapeDtypeStruct((R, C), F32)
    return pl.pallas_call(body, name=name, out_shape=[s, s, s, s],
                          compiler_params=pltpu.CompilerParams(vmem_limit_bytes=VMEM_LIMIT))(own, recvb, w, m, v)


def _by_device(dw):
    return dw.reshape(N_DEV, dw.shape[0] // N_DEV, dw.shape[1])


def _local_step(x, mod, g_attn, w_in, bias, sw_sink, g_na_out, g_sw_out, w_out, g_ffn, w_up, conv_w, conv_b, w_down,
                g_final, target, sharded):
    B, S, D = x.shape
    T = B * S
    x2d = x.reshape(T, D)
    mod3 = mod.reshape(B, 6, D)
    cos_t, sin_t = _rope_tables(S)
    sink = sw_sink.reshape(SW_HEADS)
    n_tiles = T // TOKEN_TILE
    full = lambda g: g.reshape(N_DEV * g.shape[1], g.shape[2])

    (h, qkv), got = _attn_in(x2d, mod3, g_attn, w_in, cos_t, sin_t, S, [_gather_task(w_out, n_tiles // 2)] if sharded else [])
    if sharded:
        w_out = full(got[0][0])
    qkv3 = qkv.reshape(B, S, IN_WIDTH)
    na_steps = B * (NA_PAIRS // NA_STEP_PAIRS)
    (oa,), got = _na_fwd(qkv3, bias, [_gather_task(w_up, na_steps - 1)] if sharded else [])
    if sharded:
        w_up = full(got[0][0])
    oa = oa.reshape(T, NA_WIDTH)
    (ob,), got = _sw_fwd(sink, qkv3, [_gather_task(w_down, B * SW_FWD_SPLIT - 1)] if sharded else [])
    if sharded:
        w_down = full(got[0][0])
    ob = ob.reshape(T, SW_WIDTH)
    mixin, mix, x1 = _attn_out(oa, ob, x2d, mod3, g_na_out, g_sw_out, w_out, S)
    h2, val, gt = _ffn_up(x1, mod3, g_ffn, w_up, S)
    a, act, vd, dx2, df, gstat_f, bstat_f = _ffn_down(gt, val, conv_w, conv_b, w_down, x1, mod3, g_final, target.reshape(T, D), B, S)
    F = val.shape[1]

    dw_down = _matmul_tn(a, df, "dw_down")
    (dval, dgc, cstat), got = _ffn_down_bwd(df, w_down, act, vd, [_swap_task(_by_device(dw_down))] if sharded else [])
    if sharded:
        send_down, own_down = _chip_sums(_by_device(dw_down), got[0][0])
    du, dw_up, cstat_w = _dw_up_conv(dgc, dval, gt, conv_w, h2, S)
    (dx1, dmix, gstat_u, bstat_u), got = _ffn_up_bwd(du, w_up, x1, mod3, g_ffn, dx2, mix, B, S,
                                                     [_exchange_task(send_down), _swap_task(_by_device(dw_up))] if sharded else [])
    if sharded:
        dw_down = (own_down, got[0][0])
        send_up, own_up = _chip_sums(_by_device(dw_up), got[1][0])
    dw_out = _matmul_tn(mixin, dmix, "dw_out")
    (doa, dob, gstat_o), got = _attn_out_bwd(dmix, w_out, oa, ob, g_na_out, g_sw_out, [_swap_task(_by_device(dw_out))] if sharded else [])
    if sharded:
        send_out, own_out = _chip_sums(_by_device(dw_out), got[0][0])
    (dqa, dka, dva, dbt), got = _na_bwd(qkv3, bias, doa.reshape(B, S, NA_WIDTH),
                                        [_exchange_task(send_up), _exchange_task(send_out)] if sharded else [])
    if sharded:
        dw_up, dw_out = (own_up, got[0][0]), (own_out, got[1][0])
    dqb, dkb, dvb, dsink = _sw_bwd(sink, qkv3, dob.reshape(B, S, SW_WIDTH))
    r2 = lambda t: t.reshape(T, t.shape[-1])
    grad_x, dproj, gstat_i, bstat_i = _attn_in_bwd(r2(dqa), r2(dka), r2(dva), r2(dqb), r2(dkb), r2(dvb), cos_t, sin_t, w_in, x2d, mod3,
                                                   g_attn, dx1, B, S)
    dw_in = _matmul_tn(dproj, h, "dw_in")

    dmod = jnp.stack([bstat_i[:, 0], bstat_i[:, 1], bstat_u[:, 2], bstat_u[:, 0], bstat_u[:, 1], bstat_f[:, 0]], axis=1)
    small = dict(g_attn=gstat_i[0], g_ffn=gstat_u[0], g_final=gstat_f[0], loss=gstat_f[1, 0], g_na_out=gstat_o[0], g_sw_out=gstat_o[1],
                 sw_sink=dsink[:, 0], conv_b=cstat[0], conv_w=cstat_w[9:12], dbt=dbt)
    return grad_x.reshape(B, S, D), dict(w_in=dw_in, w_out=dw_out, w_up=dw_up, w_down=dw_down), dmod, small


def _pad_lanes(v, w):
    return jnp.pad(v, (0, w - v.shape[0]))


def kernel(x, c, w_ada, b_ada, g_attn, w_in, na_rpb, sw_sink, g_na_out, g_sw_out, w_out, g_ffn, w_up, conv_w, conv_b, w_down, g_final, loss_target, m_w_ada, m_b_ada, m_g_attn, m_w_in, m_na_rpb, m_sw_sink, m_g_na_out, m_g_sw_out, m_w_out, m_g_ffn, m_w_up, m_conv_w, m_conv_b, m_w_down, m_g_final, v_w_ada, v_b_ada, v_g_attn, v_w_in, v_na_rpb, v_sw_sink, v_g_na_out, v_g_sw_out, v_w_out, v_g_ffn, v_w_up, v_conv_w, v_conv_b, v_w_down, v_g_final):
    B, S, D = x.shape
    me = 4 * lax.axis_index("x") + 2 * lax.axis_index("y") + lax.axis_index("c")
    ada_c = w_ada.shape[2]
    F_l = conv_w.shape[2]
    onehot = _toeplitz_onehot()

    cw_l = jnp.pad(conv_w[0], ((0, 8 - conv_w.shape[1]), (0, 0)))
    c_l = jnp.pad(c, ((0, 8 - B), (0, 0)))
    slabs = _all_gather(jnp.concatenate([c_l, cw_l], axis=1), "gather_c")
    c_all = slabs[:, :, :D].reshape(N_DEV * 8, D)
    conv_w_f = jnp.transpose(slabs[:, :3, D:], (1, 0, 2)).reshape(3, N_DEV * F_l)

    b_ada_l = lax.dynamic_slice(b_ada, (0, me * ada_c), (1, ada_c))
    mod_cols = _ada_fwd(c_all, w_ada[0], b_ada_l)
    mod_all = _all_gather(mod_cols, "gather_mod")
    mod_mine = lax.dynamic_slice(mod_all, (0, me * 8, 0), (N_DEV, B, ada_c))
    mod = jnp.transpose(mod_mine, (1, 0, 2)).reshape(B, N_DEV * ada_c)

    tr = {"w_in", "w_up"}
    w_in_t = jnp.transpose(w_in[0])
    w_in_f = _all_gather(w_in_t.astype(BF16), "gather_w_in").reshape(N_DEV * w_in_t.shape[0], D)
    shards = dict(w_out=w_out[0].astype(BF16), w_up=jnp.transpose(w_up[0]).astype(BF16), w_down=w_down[0].astype(BF16))

    bias = _na_bias_table(na_rpb[0], onehot)

    grad_x, dw, dmod, small = _local_step(x, mod, g_attn, w_in_f, bias, sw_sink, g_na_out, g_sw_out, shards["w_out"], g_ffn,
                                          shards["w_up"], conv_w_f, conv_b, shards["w_down"], g_final.reshape(1, D), loss_target,
                                          sharded=True)
    g_w_in = _reduce_scatter(_by_device(dw["w_in"]), "scatter_w_in")

    drpb = _na_bias_grad(small["dbt"], onehot)

    row2 = jnp.concatenate([small["g_attn"], small["g_ffn"], small["g_final"], small["g_na_out"], small["g_sw_out"],
                            _pad_lanes(small["sw_sink"], 128), _pad_lanes(small["loss"].reshape(1), 128)])
    rows = [dmod.reshape(B, 6 * D)[0], dmod.reshape(B, 6 * D)[1], _pad_lanes(row2, PACK_W), _pad_lanes(small["conv_b"], PACK_W),
            _pad_lanes(drpb, PACK_W)] + [_pad_lanes(small["conv_w"][k], PACK_W) for k in range(3)]
    packs = _all_gather(jnp.stack(rows), "gather_small")
    tot, g_b_ada = _pack_sum(packs.reshape(N_DEV * 8, PACK_W))

    o = 0
    rep = {}
    for nm, wd in (("g_attn", D), ("g_ffn", D), ("g_final", D), ("g_na_out", NA_WIDTH), ("g_sw_out", SW_WIDTH), ("sw_sink", 128), ("loss", 128)):
        rep[nm] = tot[2, o:o + wd]
        o += wd
    loss = rep["loss"][0]
    g_conv_b = tot[3:4, :conv_b.shape[1]]
    g_na_rpb = tot[4, :NA_HEADS * N_DR * N_DC].reshape(na_rpb.shape)
    g_conv_w = lax.dynamic_slice(tot[5:8], (0, me * F_l), (3, F_l)).reshape(conv_w.shape)

    dmod_cols = lax.dynamic_slice(packs.reshape(N_DEV * 8, PACK_W), (0, me * ada_c), (N_DEV * 8, ada_c))
    g_w_ada = _ada_bwd(c_all, dmod_cols)[None]

    grads = dict(
        w_ada=g_w_ada, b_ada=g_b_ada, g_attn=rep["g_attn"][None], w_in=g_w_in, na_rpb=g_na_rpb,
        sw_sink=rep["sw_sink"][None, :SW_HEADS], g_na_out=rep["g_na_out"][None], g_sw_out=rep["g_sw_out"][None],
        w_out=None, g_ffn=rep["g_ffn"][None], w_up=None, conv_w=g_conv_w, conv_b=g_conv_b, w_down=None, g_final=rep["g_final"])
    weights = dict(w_ada=w_ada, b_ada=b_ada, g_attn=g_attn, w_in=w_in, na_rpb=na_rpb, sw_sink=sw_sink, g_na_out=g_na_out,
                   g_sw_out=g_sw_out, w_out=w_out, g_ffn=g_ffn, w_up=w_up, conv_w=conv_w, conv_b=conv_b, w_down=w_down, g_final=g_final)
    ms = dict(w_ada=m_w_ada, b_ada=m_b_ada, g_attn=m_g_attn, w_in=m_w_in, na_rpb=m_na_rpb, sw_sink=m_sw_sink, g_na_out=m_g_na_out,
              g_sw_out=m_g_sw_out, w_out=m_w_out, g_ffn=m_g_ffn, w_up=m_w_up, conv_w=m_conv_w, conv_b=m_conv_b, w_down=m_w_down, g_final=m_g_final)
    vs = dict(w_ada=v_w_ada, b_ada=v_b_ada, g_attn=v_g_attn, w_in=v_w_in, na_rpb=v_na_rpb, sw_sink=v_sw_sink, g_na_out=v_g_na_out,
              g_sw_out=v_g_sw_out, w_out=v_w_out, g_ffn=v_g_ffn, w_up=v_w_up, conv_w=v_conv_w, conv_b=v_conv_b, w_down=v_w_down, g_final=v_g_final)
    names = list(weights)
    deltas, new_m, new_v = {}, {}, {}
    for nm in names:
        shp = weights[nm].shape
        if nm in tr:
            r = lambda t: jnp.transpose(t[0])
            back = lambda t: jnp.transpose(t)[None]
        else:
            two_d = (shp[-2], shp[-1]) if len(shp) >= 3 and nm != "na_rpb" else (1, int(np.prod(shp)))
            r = lambda t: t.reshape(two_d)
            back = lambda t: t.reshape(shp)
        if nm in ("w_out", "w_up", "w_down"):
            g2, d_, m_, v_ = _sum_adamw(*dw[nm], r(weights[nm]), r(ms[nm]), r(vs[nm]), "adamw_" + nm)
        else:
            g2 = grads[nm] if nm in tr else r(grads[nm])
            d_, m_, v_ = _adamw(r(weights[nm]), g2, r(ms[nm]), r(vs[nm]), "adamw_" + nm)
        deltas[nm], new_m[nm], new_v[nm], grads[nm] = back(d_), back(m_), back(v_), back(g2)
    return (loss, grad_x, *[grads[n] for n in names], *[deltas[n] for n in names], *[new_m[n] for n in names],
            *[new_v[n] for n in names])
```

```python
import functools

import numpy as np
import jax
import jax.numpy as jnp
from jax import lax
from jax.experimental import pallas as pl
from jax.experimental.pallas import tpu as pltpu

F32, BF16 = jnp.float32, jnp.bfloat16
MESH_ID = pl.DeviceIdType.MESH
N_DEV = 8

HEAD_DIM = 64
NA_HEADS = 8
SW_HEADS = 8
SW_KV_HEADS = 2
SW_GROUP = SW_HEADS // SW_KV_HEADS
NA_WIDTH = NA_HEADS * HEAD_DIM
SW_WIDTH = SW_HEADS * HEAD_DIM
SW_KV_WIDTH = SW_KV_HEADS * HEAD_DIM
ROPE_WIDTH = SW_WIDTH + SW_KV_WIDTH
IN_WIDTH = 3 * NA_WIDTH + SW_WIDTH + 2 * SW_KV_WIDTH
ROPE_LO = 3 * NA_WIDTH
GRID_W = 64
NA_ROWS_MAX = 8
NA_COLS = 16
N_DR = 2 * NA_ROWS_MAX - 1
N_DC = 2 * NA_COLS - 1
SW_WINDOW = 128
SW_BLOCK = 128
ROPE_THETA = 10000.0
EPS = 1e-6
NEG = -1e30
Q_SCALE = HEAD_DIM ** -0.5

ADAM_LR = 0.001
ADAM_B1 = 0.9
ADAM_B2 = 0.999
ADAM_EPS = 1e-08
ADAM_WD = 0.01
ADAM_STEP = 10

TOKEN_TILE = 256
VMEM_LIMIT = 56 * 1024 * 1024

PACK_W = 6144


def _nn(a, b):
    return jnp.dot(a, b, preferred_element_type=F32)


def _nt(a, b):
    return lax.dot_general(a, b, (((1,), (1,)), ((), ())), preferred_element_type=F32)


def _tn(a, b):
    return lax.dot_general(a, b, (((0,), (0,)), ((), ())), preferred_element_type=F32)


def _rms(x):
    r = lax.rsqrt(jnp.mean(x * x, axis=-1, keepdims=True) + EPS)
    return x * r, r


def _rms_bwd(xn, r, gy):
    return r * (gy - xn * jnp.mean(xn * gy, axis=-1, keepdims=True))


def _params(*sem):
    return pltpu.CompilerParams(dimension_semantics=sem, vmem_limit_bytes=VMEM_LIMIT)


def _full(shape):
    n = len(shape)
    return pl.BlockSpec(shape, lambda *_: (0,) * n)


def _mesh_pos():
    return lax.axis_index("x"), lax.axis_index("y"), lax.axis_index("c")


def _all_gather(arrays, name):
    tasks = [_gather_task(a, 0) for a in arrays]
    n = len(tasks)

    def body(*refs):
        ins, outs, sems = refs[:n], refs[n:2 * n], refs[2 * n:]
        parts = [((ins[i],), (outs[i],), sems[3 * i:3 * i + 3]) for i in range(n)]
        for phase in ("start", "mid", "finish"):
            for t, p in zip(tasks, parts):
                getattr(t, phase)(*p)

    hbm = pl.BlockSpec(memory_space=pl.ANY)
    return pl.pallas_call(
        body, name=name, out_shape=[s for t in tasks for s in t.out_shapes], in_specs=[hbm] * n, out_specs=[hbm] * n,
        scratch_shapes=[s for t in tasks for s in t.sems],
    )(*arrays)


def _row_chunk(r):
    for rc in (128, 64, 32, 16):
        if r % rc == 0:
            return rc
    raise ValueError(f"rows {r} not a multiple of 16")


def _reduce_scatter(g8, name, rider=None):
    _, R, C = g8.shape
    rc = _row_chunk(R)
    n_ri, n_ro = (len(rider.inputs), len(rider.out_shapes)) if rider is not None else (0, 0)

    def body(g_ref, *refs):
        r_ins, refs = refs[:n_ri], refs[n_ri:]
        out_ref, refs = refs[0], refs[1:]
        r_outs, refs = refs[:n_ro], refs[n_ro:]
        recva, sendb, recvb, sa, ra, sb, rb = refs[:7]
        r_sems = refs[7:]
        x_, y_, c_ = _mesh_pos()
        sibling = (x_, y_, 1 - c_)
        if rider is not None:
            rider.start(r_ins, r_outs, r_sems)
        copies_a = []
        for k in range(4):
            cp = pltpu.make_async_remote_copy(
                src_ref=g_ref.at[2 * k + (1 - c_)], dst_ref=recva.at[k],
                send_sem=sa.at[k], recv_sem=ra.at[k], device_id=sibling, device_id_type=MESH_ID)
            cp.start()
            copies_a.append(cp)
        for cp in copies_a:
            cp.wait_recv()
        if rider is not None and rider.mid is not None:
            rider.mid(r_ins, r_outs, r_sems)

        def chip_sum(k, rows):
            return g_ref[2 * k + c_, rows, :].astype(F32) + recva[k, rows, :].astype(F32)

        flips = [(1 - x_, y_), (x_, 1 - y_), (1 - x_, 1 - y_)]
        copies_b = []
        for j, (tx, ty) in enumerate(flips):
            kt = 2 * tx + ty

            def fill(i, carry, j=j, kt=kt):
                rows = pl.ds(pl.multiple_of(i * rc, rc), rc)
                sendb[j, rows, :] = chip_sum(kt, rows).astype(BF16)
                return carry

            lax.fori_loop(0, R // rc, fill, 0)
            cp = pltpu.make_async_remote_copy(
                src_ref=sendb.at[j], dst_ref=recvb.at[j],
                send_sem=sb.at[j], recv_sem=rb.at[j], device_id=(tx, ty, c_), device_id_type=MESH_ID)
            cp.start()
            copies_b.append(cp)
        for cp in copies_b:
            cp.wait_recv()
        kme = 2 * x_ + y_

        def total(i, carry):
            rows = pl.ds(pl.multiple_of(i * rc, rc), rc)
            acc = chip_sum(kme, rows)
            for j in range(3):
                acc = acc + recvb[j, rows, :].astype(F32)
            out_ref[rows, :] = acc
            return carry

        lax.fori_loop(0, R // rc, total, 0)
        for cp in copies_a + copies_b:
            cp.wait_send()
        if rider is not None:
            rider.finish(r_ins, r_outs, r_sems)

    vm = pl.BlockSpec(memory_space=pltpu.VMEM)
    hbm = pl.BlockSpec(memory_space=pl.ANY)
    res = pl.pallas_call(
        body, name=name,
        out_shape=[jax.ShapeDtypeStruct((R, C), F32)] + (rider.out_shapes if rider is not None else []),
        in_specs=[vm] + [hbm] * n_ri, out_specs=[vm] + [hbm] * n_ro,
        scratch_shapes=[pltpu.VMEM((4, R, C), BF16), pltpu.VMEM((3, R, C), BF16), pltpu.VMEM((3, R, C), BF16),
                        pltpu.SemaphoreType.DMA((4,)), pltpu.SemaphoreType.DMA((4,)),
                        pltpu.SemaphoreType.DMA((3,)), pltpu.SemaphoreType.DMA((3,))] + (rider.sems if rider is not None else []),
        compiler_params=pltpu.CompilerParams(vmem_limit_bytes=VMEM_LIMIT),
    )(g8, *(rider.inputs if rider is not None else []))
    return res[0], res[1:]


class _Task:
    def __init__(self, inputs, out_shapes, sems, start, finish, mid=None, mid_step=None, alias=None):
        self.inputs, self.out_shapes, self.sems = list(inputs), list(out_shapes), list(sems)
        self.start, self.finish, self.mid, self.mid_step = start, finish, mid, mid_step
        self.alias = alias


def _hosted_call(body, name, grid, in_specs, out_specs, out_shape, operands, tasks, scratch_shapes=()):
    n_in, n_out, n_scr = len(in_specs), len(out_specs), len(scratch_shapes)
    t_in = [len(t.inputs) for t in tasks]
    t_out = [len(t.out_shapes) for t in tasks]
    t_sem = [len(t.sems) for t in tasks]
    n_steps = int(np.prod(grid))

    def wrapped(*refs):
        ins, rest = refs[:n_in], refs[n_in:]
        task_ins, rest = rest[:sum(t_in)], rest[sum(t_in):]
        outs, rest = rest[:n_out], rest[n_out:]
        task_outs, rest = rest[:sum(t_out)], rest[sum(t_out):]
        scr, task_sems = rest[:n_scr], rest[n_scr:]
        step = pl.program_id(0)
        for ax in range(1, len(grid)):
            step = step * grid[ax] + pl.program_id(ax)
        parts = []
        oi = oo = os_ = 0
        for t, a, b, c in zip(tasks, t_in, t_out, t_sem):
            parts.append((t, task_ins[oi:oi + a], task_outs[oo:oo + b], task_sems[os_:os_ + c]))
            oi, oo, os_ = oi + a, oo + b, os_ + c
        for t, ti, to, ts in parts:
            pl.when(step == 0)(functools.partial(t.start, ti, to, ts))
            if t.mid is not None:
                pl.when(step == t.mid_step)(functools.partial(t.mid, ti, to, ts))
        body(*ins, *outs, *scr)
        for t, ti, to, ts in parts:
            pl.when(step == n_steps - 1)(functools.partial(t.finish, ti, to, ts))

    hbm = pl.BlockSpec(memory_space=pl.ANY)
    aliases, oi, oo = {}, n_in, n_out
    for t, a, b in zip(tasks, t_in, t_out):
        if t.alias is not None:
            aliases[oi + t.alias[0]] = oo + t.alias[1]
        oi, oo = oi + a, oo + b
    res = pl.pallas_call(
        wrapped, name=name, grid=grid,
        in_specs=list(in_specs) + [hbm] * sum(t_in),
        out_specs=list(out_specs) + [hbm] * sum(t_out),
        out_shape=list(out_shape) + [s for t in tasks for s in t.out_shapes],
        scratch_shapes=list(scratch_shapes) + [s for t in tasks for s in t.sems],
        input_output_aliases=aliases,
        compiler_params=_params(*(["arbitrary"] * len(grid))),
    )(*operands, *[a for t in tasks for a in t.inputs])
    own, extra = res[:n_out], res[n_out:]
    per_task, o = [], 0
    for b in t_out:
        per_task.append(extra[o:o + b])
        o += b
    return own, per_task


def _gather_task(shard, mid_step, rows=None, into=None):
    lo, n = (0, shard.shape[0]) if rows is None else rows

    def parts(ins, outs, sems):
        x_ref, out_ref, (send_sems, recv_sems, local_sem) = ins[0], outs[0], sems
        x_, y_, c_ = _mesh_pos()
        me, sibling = (x_, y_, c_), (x_, y_, 1 - c_)
        chips = [(1 - x_, y_), (x_, 1 - y_), (1 - x_, 1 - y_)]
        x_ref = x_ref.at[pl.ds(lo, n)]

        def rows(px, py, pc):
            return out_ref.at[4 * px + 2 * py + pc, pl.ds(lo, n)]

        def copy(k, block, to, src=None):
            return pltpu.make_async_remote_copy(
                src_ref=rows(*block) if src is None else src, dst_ref=rows(*block),
                send_sem=send_sems.at[k], recv_sem=recv_sems.at[k], device_id=to, device_id_type=MESH_ID)

        return dict(
            mine=lambda: pltpu.make_async_copy(x_ref, rows(*me), local_sem),
            first=lambda: [copy(0, me, sibling, src=x_ref)] + [copy(1 + j, me, (*chip, c_), src=x_ref) for j, chip in enumerate(chips)],
            passed=lambda: [copy(4 + j, (*chip, c_), sibling) for j, chip in enumerate(chips)],
            landed=lambda: [copy(1 + j, (*chip, c_), me) for j, chip in enumerate(chips)],
            last=lambda: [copy(0, sibling, me)] + [copy(4 + j, (*chip, 1 - c_), me) for j, chip in enumerate(chips)])

    def start(ins, outs, sems):
        p = parts(ins, outs, sems)
        p["mine"]().start()
        for cp in p["first"]():
            cp.start()

    def mid(ins, outs, sems):
        p = parts(ins, outs, sems)
        for cp, fw in zip(p["landed"](), p["passed"]()):
            cp.wait_recv()
            fw.start()

    def finish(ins, outs, sems):
        p = parts(ins, outs, sems)
        for cp in p["last"]():
            cp.wait_recv()
        for cp in p["first"]() + p["passed"]():
            cp.wait_send()
        p["mine"]().wait()

    return _Task([shard] if into is None else [shard, into], [jax.ShapeDtypeStruct((N_DEV,) + shard.shape, shard.dtype)],
                 [pltpu.SemaphoreType.DMA((7,)), pltpu.SemaphoreType.DMA((7,)), pltpu.SemaphoreType.DMA],
                 start, finish, mid, mid_step, alias=None if into is None else (1, 0))


def _swap_task(g8):
    _, R, C = g8.shape

    def copies(ins, outs, sems):
        (g_ref,), (recv_ref,), (ss, rs) = ins, outs, sems
        x_, y_, c_ = _mesh_pos()
        return [pltpu.make_async_remote_copy(src_ref=g_ref.at[2 * k + (1 - c_)], dst_ref=recv_ref.at[k], send_sem=ss.at[k],
                                             recv_sem=rs.at[k], device_id=(x_, y_, 1 - c_), device_id_type=MESH_ID)
                for k in range(4)]

    def start(ins, outs, sems):
        for cp in copies(ins, outs, sems):
            cp.start()

    def finish(ins, outs, sems):
        cps = copies(ins, outs, sems)
        for cp in cps:
            cp.wait_recv()
        for cp in cps:
            cp.wait_send()

    return _Task([g8], [jax.ShapeDtypeStruct((4, R, C), g8.dtype)],
                 [pltpu.SemaphoreType.DMA((4,)), pltpu.SemaphoreType.DMA((4,))], start, finish)


def _chip_sums(g8, recva):
    _, R, C = g8.shape
    rc = _row_chunk(R)

    def body(g_ref, a_ref, send_ref, own_ref):
        x_, y_, c_ = _mesh_pos()
        chips = [(1 - x_, y_), (x_, 1 - y_), (1 - x_, 1 - y_), (x_, y_)]

        def chunk(i, carry):
            rows = pl.ds(pl.multiple_of(i * rc, rc), rc)
            for j, (tx, ty) in enumerate(chips):
                k = 2 * tx + ty
                s = g_ref[2 * k + c_, rows, :].astype(F32) + a_ref[k, rows, :].astype(F32)
                if j < 3:
                    send_ref[j, rows, :] = s.astype(BF16)
                else:
                    own_ref[rows, :] = s
            return carry

        lax.fori_loop(0, R // rc, chunk, 0)

    return pl.pallas_call(body, name="chip_sums", out_shape=[jax.ShapeDtypeStruct((3, R, C), BF16), jax.ShapeDtypeStruct((R, C), F32)],
                          compiler_params=pltpu.CompilerParams(vmem_limit_bytes=VMEM_LIMIT))(g8, recva)


def _exchange_task(sendb):
    def copies(ins, outs, sems):
        (s_ref,), (recv_ref,), (ss, rs) = ins, outs, sems
        x_, y_, c_ = _mesh_pos()
        flips = [(1 - x_, y_), (x_, 1 - y_), (1 - x_, 1 - y_)]
        return [pltpu.make_async_remote_copy(src_ref=s_ref.at[j], dst_ref=recv_ref.at[j], send_sem=ss.at[j], recv_sem=rs.at[j],
                                             device_id=(tx, ty, c_), device_id_type=MESH_ID) for j, (tx, ty) in enumerate(flips)]

    def start(ins, outs, sems):
        for cp in copies(ins, outs, sems):
            cp.start()

    def finish(ins, outs, sems):
        cps = copies(ins, outs, sems)
        for cp in cps:
            cp.wait_recv()
        for cp in cps:
            cp.wait_send()

    return _Task([sendb], [jax.ShapeDtypeStruct(sendb.shape, sendb.dtype)],
                 [pltpu.SemaphoreType.DMA((3,)), pltpu.SemaphoreType.DMA((3,))], start, finish)


def _silu(v):
    return v * (1.0 / (1.0 + jnp.exp(-v)))


def _ada_fwd(c_all, w_ada_l, b_ada_l):
    def body(c_ref, w_ref, b_ref, o_ref):
        o_ref[...] = jnp.dot(_silu(c_ref[...]), w_ref[...], precision=lax.Precision.HIGHEST,
                             preferred_element_type=F32) + b_ref[...]
    return pl.pallas_call(body, name="ada_fwd", out_shape=jax.ShapeDtypeStruct((c_all.shape[0], w_ada_l.shape[1]), F32),
                          compiler_params=pltpu.CompilerParams(vmem_limit_bytes=VMEM_LIMIT))(c_all, w_ada_l, b_ada_l)


def _ada_bwd(c_all, dmod_cols):
    def body(c_ref, d_ref, o_ref):
        o_ref[...] = lax.dot_general(_silu(c_ref[...]), d_ref[...], (((0,), (0,)), ((), ())),
                                     precision=lax.Precision.HIGHEST, preferred_element_type=F32)
    return pl.pallas_call(body, name="ada_bwd", out_shape=jax.ShapeDtypeStruct((c_all.shape[1], dmod_cols.shape[1]), F32),
                          compiler_params=pltpu.CompilerParams(vmem_limit_bytes=VMEM_LIMIT))(c_all, dmod_cols)


def _toeplitz_onehot():
    k = np.arange(GRID_W)[:, None]
    q = np.arange(GRID_W)[None, :]
    dc = np.clip(k - q + NA_COLS - 1, 0, N_DC - 1).reshape(-1)
    e = np.zeros((128, GRID_W * GRID_W), np.float32)
    e[dc, np.arange(GRID_W * GRID_W)] = 1.0
    return jnp.asarray(e)


def _rpb_expand(rpb2d, onehot):
    def body(r_ref, e_ref, o_ref):
        o_ref[...] = jnp.dot(r_ref[...], e_ref[...], precision=lax.Precision.HIGHEST, preferred_element_type=F32)
    return pl.pallas_call(body, name="rpb_expand", out_shape=jax.ShapeDtypeStruct((128, GRID_W * GRID_W), F32),
                          compiler_params=pltpu.CompilerParams(vmem_limit_bytes=VMEM_LIMIT))(rpb2d, onehot)


def _rpb_reduce(dtz2d, onehot):
    def body(d_ref, e_ref, o_ref):
        o_ref[...] = lax.dot_general(d_ref[...], e_ref[...], (((1,), (1,)), ((), ())),
                                     precision=lax.Precision.HIGHEST, preferred_element_type=F32)
    return pl.pallas_call(body, name="rpb_reduce", out_shape=jax.ShapeDtypeStruct((128, 128), F32),
                          compiler_params=pltpu.CompilerParams(vmem_limit_bytes=VMEM_LIMIT))(dtz2d, onehot)


NA_PAIRS = NA_HEADS // 2


def _na_bias_table(na_rpb, onehot):
    rpb2d = jnp.pad(na_rpb.reshape(NA_HEADS * N_DR, N_DC), ((0, 128 - NA_HEADS * N_DR), (0, 128 - N_DC)))
    tz = _rpb_expand(rpb2d, onehot)[:NA_HEADS * N_DR].reshape(NA_PAIRS, 2, N_DR, GRID_W, GRID_W)
    col = np.arange(GRID_W)
    cs = np.clip(col - NA_COLS // 2, 0, GRID_W - NA_COLS)
    ok_kq = ((col[None, :] >= cs[:, None]) & (col[None, :] < cs[:, None] + NA_COLS)).T
    tz = jnp.where(jnp.asarray(ok_kq)[None, None, None], tz, NEG)
    return jnp.transpose(tz, (0, 2, 3, 1, 4)).reshape(NA_PAIRS, N_DR * GRID_W, 128)


def _na_bias_grad(db, onehot):
    fold = db.reshape(NA_PAIRS, N_DR, GRID_W, 2, GRID_W)
    dtz2d = jnp.transpose(fold, (0, 3, 1, 2, 4)).reshape(NA_HEADS * N_DR, GRID_W * GRID_W)
    dtz2d = jnp.pad(dtz2d, ((0, 128 - NA_HEADS * N_DR), (0, 0)))
    return _rpb_reduce(dtz2d, onehot)[:NA_HEADS * N_DR, :N_DC].reshape(-1)


def _rope_tables(S):
    half = HEAD_DIM // 2
    inv = ROPE_THETA ** (-jnp.arange(half, dtype=F32) / half)
    ang = jnp.arange(S).astype(F32)[:, None] * inv[None, :]
    cos, sin = jnp.cos(ang), jnp.sin(ang)
    n = ROPE_WIDTH // HEAD_DIM
    return jnp.tile(jnp.concatenate([cos, cos], axis=1), (1, n)), jnp.tile(jnp.concatenate([-sin, sin], axis=1), (1, n))


def _rot_half(t):
    w = t.shape[1]
    lane = lax.broadcasted_iota(jnp.int32, t.shape, 1)
    return jnp.where((lane % HEAD_DIM) < HEAD_DIM // 2, pltpu.roll(t, w - HEAD_DIM // 2, axis=1),
                     pltpu.roll(t, HEAD_DIM // 2, axis=1))


def _tok_spec(w):
    return pl.BlockSpec((TOKEN_TILE, w), lambda i: (i, 0))


def _mod_spec(tps, d):
    return pl.BlockSpec((1, 6, d), lambda i: (i // tps, 0, 0))


def _bstat_spec(tps, w):
    return pl.BlockSpec((1, 8, w), lambda i: (i // tps, 0, 0))


def _attn_in(x2d, mod3, g_attn, w_in, cos_t, sin_t, S, tasks=()):
    T, D = x2d.shape
    tps = S // TOKEN_TILE

    def body(x_ref, mod_ref, g_ref, w_ref, cos_ref, sin_ref, h_ref, qkv_ref):
        xn, _ = _rms(x_ref[...])
        h = (xn * g_ref[...]) * (1.0 + mod_ref[0, 1:2, :]) + mod_ref[0, 0:1, :]
        hb = h.astype(BF16)
        h_ref[...] = hb
        proj = _nt(hb, w_ref[...])
        rb = proj[:, ROPE_LO:ROPE_LO + ROPE_WIDTH]
        rb = rb * cos_ref[...] + _rot_half(rb) * sin_ref[...]
        qkv_ref[:, 0:NA_WIDTH] = (proj[:, 0:NA_WIDTH] * Q_SCALE).astype(BF16)
        qkv_ref[:, NA_WIDTH:ROPE_LO] = proj[:, NA_WIDTH:ROPE_LO].astype(BF16)
        qkv_ref[:, ROPE_LO:ROPE_LO + SW_WIDTH] = (rb[:, 0:SW_WIDTH] * Q_SCALE).astype(BF16)
        qkv_ref[:, ROPE_LO + SW_WIDTH:ROPE_LO + ROPE_WIDTH] = rb[:, SW_WIDTH:].astype(BF16)
        qkv_ref[:, ROPE_LO + ROPE_WIDTH:] = proj[:, ROPE_LO + ROPE_WIDTH:].astype(BF16)

    return _hosted_call(
        body, "attn_in", (T // TOKEN_TILE,),
        [_tok_spec(D), _mod_spec(tps, D), _full((1, D)), _full(w_in.shape),
         pl.BlockSpec((TOKEN_TILE, ROPE_WIDTH), lambda i: (i % tps, 0)),
         pl.BlockSpec((TOKEN_TILE, ROPE_WIDTH), lambda i: (i % tps, 0))],
        [_tok_spec(D), _tok_spec(IN_WIDTH)],
        [jax.ShapeDtypeStruct((T, D), BF16), jax.ShapeDtypeStruct((T, IN_WIDTH), BF16)],
        (x2d, mod3, g_attn, w_in, cos_t, sin_t), tasks)


def _attn_out(oa, ob, x2d, mod3, g_na, g_sw, w_out, S):
    T, D = x2d.shape
    tps = S // TOKEN_TILE

    def body(oa_ref, ob_ref, x_ref, mod_ref, gna_ref, gsw_ref, w_ref, mixin_ref, mix_ref, x1_ref):
        oan, _ = _rms(oa_ref[...])
        obn, _ = _rms(ob_ref[...])
        mixin = jnp.concatenate([oan * gna_ref[...], obn * gsw_ref[...]], axis=1).astype(BF16)
        mixin_ref[...] = mixin
        mix = _nn(mixin, w_ref[...])
        mix_ref[...] = mix
        x1_ref[...] = x_ref[...] + mod_ref[0, 2:3, :] * mix

    return pl.pallas_call(
        body, name="attn_out", grid=(T // TOKEN_TILE,),
        in_specs=[_tok_spec(NA_WIDTH), _tok_spec(SW_WIDTH), _tok_spec(D), _mod_spec(tps, D),
                  _full((1, NA_WIDTH)), _full((1, SW_WIDTH)), _full(w_out.shape)],
        out_specs=[_tok_spec(NA_WIDTH + SW_WIDTH), _tok_spec(D), _tok_spec(D)],
        out_shape=[jax.ShapeDtypeStruct((T, NA_WIDTH + SW_WIDTH), BF16), jax.ShapeDtypeStruct((T, D), F32),
                   jax.ShapeDtypeStruct((T, D), F32)],
        compiler_params=_params("parallel"),
    )(oa, ob, x2d, mod3, g_na, g_sw, w_out)


def _ffn_up(x1, mod3, g_ffn, w_up, S):
    T, D = x1.shape
    F = w_up.shape[0] // 2
    tps = S // TOKEN_TILE

    def body(x1_ref, mod_ref, g_ref, w_ref, h2_ref, val_ref, gt_ref):
        xn, _ = _rms(x1_ref[...])
        h2 = ((xn * g_ref[...]) * (1.0 + mod_ref[0, 4:5, :]) + mod_ref[0, 3:4, :]).astype(BF16)
        h2_ref[...] = h2
        u = _nt(h2, w_ref[...])
        val_ref[...] = u[:, :F].astype(BF16)
        gt_ref[...] = u[:, F:].astype(BF16)

    return pl.pallas_call(
        body, name="ffn_up", grid=(T // TOKEN_TILE,),
        in_specs=[_tok_spec(D), _mod_spec(tps, D), _full((1, D)), _full(w_up.shape)],
        out_specs=[_tok_spec(D), _tok_spec(F), _tok_spec(F)],
        out_shape=[jax.ShapeDtypeStruct((T, D), BF16), jax.ShapeDtypeStruct((T, F), BF16), jax.ShapeDtypeStruct((T, F), BF16)],
        compiler_params=_params("parallel"),
    )(x1, mod3, g_ffn, w_up)


def _halo_specs(T, tps, w):
    per = TOKEN_TILE // 8
    prev = pl.BlockSpec((8, w), lambda i: (jnp.maximum(i * per - 1, 0), 0))
    nxt = pl.BlockSpec((8, w), lambda i: (jnp.minimum((i + 1) * per, T // 8 - 1), 0))
    return prev, nxt


def _seq_shifts(cur, before, after, ti, tps):
    tm = cur.shape[0]
    row = lax.broadcasted_iota(jnp.int32, cur.shape, 0)
    before = jnp.where(ti > 0, before.astype(F32), 0.0)
    after = jnp.where(ti < tps - 1, after.astype(F32), 0.0)
    return jnp.where(row == 0, before, pltpu.roll(cur, 1, axis=0)), jnp.where(row == tm - 1, after, pltpu.roll(cur, tm - 1, axis=0))


def _ffn_down(gt, val, conv_w, conv_b, w_down, x1, mod3, g_final, target, B, S):
    T, D = x1.shape
    F = gt.shape[1]
    tps = S // TOKEN_TILE
    prev, nxt = _halo_specs(T, tps, F)

    def body(gt_ref, prev_ref, next_ref, val_ref, cw_ref, cb_ref, w_ref, x1_ref, mod_ref, gf_ref, tgt_ref,
             a_ref, act_ref, vd_ref, dx2_ref, df_ref, gstat_ref, bstat_ref):
        i = pl.program_id(0)
        g = gt_ref[...].astype(F32)
        gprev, gnext = _seq_shifts(g, prev_ref[7:8, :], next_ref[0:1, :], i % tps, tps)
        gc = gprev * cw_ref[0:1, :] + g * cw_ref[1:2, :] + gnext * cw_ref[2:3, :] + cb_ref[...]
        sig = 1.0 / (1.0 + jnp.exp(-gc))
        act = gc * sig
        val = val_ref[...].astype(F32)
        act_ref[...] = act.astype(BF16)
        vd_ref[...] = (val * (sig + act - act * sig)).astype(BF16)
        a = (act * val).astype(BF16)
        a_ref[...] = a
        f = _nn(a, w_ref[...])
        gate = mod_ref[0, 5:6, :]
        x2 = x1_ref[...] + gate * f
        xn, r = _rms(x2)
        err = xn * gf_ref[...] - tgt_ref[...]
        dy = err * (1.0 / D)
        dx2 = _rms_bwd(xn, r, dy * gf_ref[...])
        dx2_ref[...] = dx2
        df_ref[...] = (gate * dx2).astype(BF16)

        @pl.when(i == 0)
        def _():
            gstat_ref[...] = jnp.zeros_like(gstat_ref)

        @pl.when(i % tps == 0)
        def _():
            bstat_ref[...] = jnp.zeros_like(bstat_ref)

        gstat_ref[0:1, :] += jnp.sum(dy * xn, axis=0, keepdims=True)
        tile_loss = jnp.sum(jnp.sum(err * err, axis=1, keepdims=True), axis=0, keepdims=True) * (0.5 / D)
        gstat_ref[1:2, :] += jnp.broadcast_to(tile_loss, (1, D))
        bstat_ref[0, 0:1, :] += jnp.sum(dx2 * f, axis=0, keepdims=True)

    return pl.pallas_call(
        body, name="ffn_down", grid=(T // TOKEN_TILE,),
        in_specs=[_tok_spec(F), prev, nxt, _tok_spec(F), _full(conv_w.shape), _full((1, F)), _full(w_down.shape),
                  _tok_spec(D), _mod_spec(tps, D), _full((1, D)), _tok_spec(D)],
        out_specs=[_tok_spec(F), _tok_spec(F), _tok_spec(F), _tok_spec(D), _tok_spec(D), _full((8, D)), _bstat_spec(tps, D)],
        out_shape=[jax.ShapeDtypeStruct((T, F), BF16), jax.ShapeDtypeStruct((T, F), BF16), jax.ShapeDtypeStruct((T, F), BF16),
                   jax.ShapeDtypeStruct((T, D), F32), jax.ShapeDtypeStruct((T, D), BF16),
                   jax.ShapeDtypeStruct((8, D), F32), jax.ShapeDtypeStruct((B, 8, D), F32)],
        compiler_params=_params("arbitrary"),
    )(gt, gt, gt, val, conv_w, conv_b, w_down, x1, mod3, g_final, target)


def _ffn_down_bwd(df, w_down, act, vd, tasks=()):
    T, D = df.shape
    F = act.shape[1]

    def body(df_ref, w_ref, act_ref, vd_ref, dval_ref, dgc_ref, cstat_ref):
        da = _nt(df_ref[...], w_ref[...])
        dval_ref[...] = (da * act_ref[...].astype(F32)).astype(BF16)
        dgc = da * vd_ref[...].astype(F32)
        dgc_ref[...] = dgc.astype(BF16)

        @pl.when(pl.program_id(0) == 0)
        def _():
            cstat_ref[...] = jnp.zeros_like(cstat_ref)

        cstat_ref[0:1, :] += jnp.sum(dgc, axis=0, keepdims=True)

    return _hosted_call(
        body, "ffn_down_bwd", (T // TOKEN_TILE,),
        [_tok_spec(D), _full(w_down.shape), _tok_spec(F), _tok_spec(F)],
        [_tok_spec(F), _tok_spec(F), _full((8, F))],
        [jax.ShapeDtypeStruct((T, F), BF16), jax.ShapeDtypeStruct((T, F), BF16), jax.ShapeDtypeStruct((8, F), F32)],
        (df, w_down, act, vd), tasks)


def _dw_up_conv(dgc, dval, gt, conv_w, h2, S, tk=512):
    T, F = dgc.shape
    D = h2.shape[1]
    nk, tps, per = T // tk, S // tk, tk // 8
    fc = F // 2 if F % 256 == 0 else F
    nc = F // fc
    gated = lambda i: i // nc
    prev = pl.BlockSpec((8, fc), lambda i, k: (jnp.maximum(k * per - 1, 0) * gated(i), i % nc))
    nxt = pl.BlockSpec((8, fc), lambda i, k: (jnp.minimum((k + 1) * per, T // 8 - 1) * gated(i), i % nc))
    gate_half = pl.BlockSpec((tk, fc), lambda i, k: (k * gated(i), i % nc))
    val_half = pl.BlockSpec((tk, fc), lambda i, k: (k * (1 - gated(i)), i % nc))

    def body(dgc_ref, prev_ref, next_ref, dval_ref, gt_ref, cw_ref, h2_ref, du_ref, dw_ref, cstat_ref, acc):
        i, k = pl.program_id(0), pl.program_id(1)

        @pl.when(k == 0)
        def _():
            acc[...] = jnp.zeros_like(acc)
            cstat_ref[...] = jnp.zeros_like(cstat_ref)

        @pl.when(i < nc)
        def _():
            dv = dval_ref[...]
            du_ref[...] = dv
            acc[...] += _tn(dv, h2_ref[...])

        @pl.when(i >= nc)
        def _():
            d = dgc_ref[...].astype(F32)
            dprev, dnext = _seq_shifts(d, prev_ref[7:8, :], next_ref[0:1, :], k % tps, tps)
            g = gt_ref[...].astype(F32)
            cstat_ref[1:2, :] += jnp.sum(dnext * g, axis=0, keepdims=True)
            cstat_ref[2:3, :] += jnp.sum(d * g, axis=0, keepdims=True)
            cstat_ref[3:4, :] += jnp.sum(dprev * g, axis=0, keepdims=True)
            dgt = (dnext * cw_ref[0:1, :] + d * cw_ref[1:2, :] + dprev * cw_ref[2:3, :]).astype(BF16)
            du_ref[...] = dgt
            acc[...] += _tn(dgt, h2_ref[...])

        @pl.when(k == nk - 1)
        def _():
            dw_ref[...] = acc[...].astype(BF16)

    return pl.pallas_call(
        body, name="dw_up", grid=(2 * nc, nk),
        in_specs=[gate_half, prev, nxt, val_half, gate_half, pl.BlockSpec((conv_w.shape[0], fc), lambda i, k: (0, i % nc)),
                  pl.BlockSpec((tk, D), lambda i, k: (k, 0))],
        out_specs=[pl.BlockSpec((tk, fc), lambda i, k: (k, i)), pl.BlockSpec((fc, D), lambda i, k: (i, 0)),
                   pl.BlockSpec((8, fc), lambda i, k: (gated(i), i % nc))],
        out_shape=[jax.ShapeDtypeStruct((T, 2 * F), BF16), jax.ShapeDtypeStruct((2 * F, D), BF16), jax.ShapeDtypeStruct((16, F), F32)],
        scratch_shapes=[pltpu.VMEM((fc, D), F32)],
        compiler_params=_params("arbitrary", "arbitrary"),
    )(dgc, dgc, dgc, dval, gt, conv_w, h2)


def _ffn_up_bwd(du, w_up, x1, mod3, g_ffn, dx2, mix, B, S, tasks=()):
    T, D = x1.shape
    F = du.shape[1] // 2
    tps = S // TOKEN_TILE

    def body(du_ref, w_ref, x1_ref, mod_ref, g_ref, dx2_ref, mix_ref, dx1_ref, dmix_ref, gstat_ref, bstat_ref):
        i = pl.program_id(0)
        dh2 = _nn(du_ref[...], w_ref[...])
        xn, r = _rms(x1_ref[...])
        scale1 = 1.0 + mod_ref[0, 4:5, :]
        xg = xn * g_ref[...]
        dx1 = dx2_ref[...] + _rms_bwd(xn, r, dh2 * g_ref[...] * scale1)
        dx1_ref[...] = dx1
        dmix_ref[...] = (mod_ref[0, 2:3, :] * dx1).astype(BF16)

        @pl.when(i == 0)
        def _():
            gstat_ref[...] = jnp.zeros_like(gstat_ref)

        @pl.when(i % tps == 0)
        def _():
            bstat_ref[...] = jnp.zeros_like(bstat_ref)

        gstat_ref[0:1, :] += jnp.sum(dh2 * scale1 * xn, axis=0, keepdims=True)
        bstat_ref[0, 0:1, :] += jnp.sum(dh2, axis=0, keepdims=True)
        bstat_ref[0, 1:2, :] += jnp.sum(dh2 * xg, axis=0, keepdims=True)
        bstat_ref[0, 2:3, :] += jnp.sum(dx1 * mix_ref[...], axis=0, keepdims=True)

    return _hosted_call(
        body, "ffn_up_bwd", (T // TOKEN_TILE,),
        [_tok_spec(2 * F), _full(w_up.shape), _tok_spec(D), _mod_spec(tps, D), _full((1, D)), _tok_spec(D), _tok_spec(D)],
        [_tok_spec(D), _tok_spec(D), _full((8, D)), _bstat_spec(tps, D)],
        [jax.ShapeDtypeStruct((T, D), F32), jax.ShapeDtypeStruct((T, D), BF16),
         jax.ShapeDtypeStruct((8, D), F32), jax.ShapeDtypeStruct((B, 8, D), F32)],
        (du, w_up, x1, mod3, g_ffn, dx2, mix), tasks)


def _attn_out_bwd(dmix, w_out, oa, ob, g_na, g_sw, tasks=()):
    T, D = dmix.shape

    def body(dmix_ref, w_ref, oa_ref, ob_ref, gna_ref, gsw_ref, doa_ref, dob_ref, gstat_ref):
        dmixin = _nt(dmix_ref[...], w_ref[...])

        @pl.when(pl.program_id(0) == 0)
        def _():
            gstat_ref[...] = jnp.zeros_like(gstat_ref)

        for k, (o_ref, g_ref, do_ref) in enumerate(((oa_ref, gna_ref, doa_ref), (ob_ref, gsw_ref, dob_ref))):
            dn = dmixin[:, k * NA_WIDTH:(k + 1) * NA_WIDTH]
            on, r = _rms(o_ref[...])
            gstat_ref[k:k + 1, :] += jnp.sum(dn * on, axis=0, keepdims=True)
            do_ref[...] = _rms_bwd(on, r, dn * g_ref[...]).astype(BF16)

    hs = jax.ShapeDtypeStruct((T, NA_WIDTH), BF16)
    return _hosted_call(
        body, "attn_out_bwd", (T // TOKEN_TILE,),
        [_tok_spec(D), _full(w_out.shape), _tok_spec(NA_WIDTH), _tok_spec(SW_WIDTH), _full((1, NA_WIDTH)), _full((1, SW_WIDTH))],
        [_tok_spec(NA_WIDTH), _tok_spec(SW_WIDTH), _full((8, NA_WIDTH))],
        [hs, hs, jax.ShapeDtypeStruct((8, NA_WIDTH), F32)],
        (dmix, w_out, oa, ob, g_na, g_sw), tasks)


def _attn_in_bwd(dqa, dka, dva, dqb, dkb, dvb, cos_t, sin_t, w_in, x2d, mod3, g_attn, dx1, B, S):
    T, D = x2d.shape
    tps = S // TOKEN_TILE

    def body(dqa_ref, dka_ref, dva_ref, dqb_ref, dkb_ref, dvb_ref, cos_ref, sin_ref, w_ref, x_ref, mod_ref, g_ref, dx1_ref,
             gx_ref, dproj_ref, gstat_ref, bstat_ref):
        i = pl.program_id(0)
        drb = jnp.concatenate([dqb_ref[...] * Q_SCALE, dkb_ref[...]], axis=1)
        drb = drb * cos_ref[...] + _rot_half(drb * sin_ref[...])
        dproj = jnp.concatenate([dqa_ref[...] * Q_SCALE, dka_ref[...], dva_ref[...], drb, dvb_ref[...]], axis=1).astype(BF16)
        dproj_ref[...] = dproj
        dh = _nn(dproj, w_ref[...])
        xn, r = _rms(x_ref[...])
        scale1 = 1.0 + mod_ref[0, 1:2, :]
        gx_ref[...] = dx1_ref[...] + _rms_bwd(xn, r, dh * g_ref[...] * scale1)

        @pl.when(i == 0)
        def _():
            gstat_ref[...] = jnp.zeros_like(gstat_ref)

        @pl.when(i % tps == 0)
        def _():
            bstat_ref[...] = jnp.zeros_like(bstat_ref)

        gstat_ref[0:1, :] += jnp.sum(dh * scale1 * xn, axis=0, keepdims=True)
        bstat_ref[0, 0:1, :] += jnp.sum(dh, axis=0, keepdims=True)
        bstat_ref[0, 1:2, :] += jnp.sum(dh * (xn * g_ref[...]), axis=0, keepdims=True)

    rope = pl.BlockSpec((TOKEN_TILE, ROPE_WIDTH), lambda i: (i % tps, 0))
    return pl.pallas_call(
        body, name="attn_in_bwd", grid=(T // TOKEN_TILE,),
        in_specs=[_tok_spec(NA_WIDTH), _tok_spec(NA_WIDTH), _tok_spec(NA_WIDTH), _tok_spec(SW_WIDTH), _tok_spec(SW_KV_WIDTH),
                  _tok_spec(SW_KV_WIDTH), rope, rope, _full(w_in.shape), _tok_spec(D), _mod_spec(tps, D), _full((1, D)), _tok_spec(D)],
        out_specs=[_tok_spec(D), _tok_spec(IN_WIDTH), _full((8, D)), _bstat_spec(tps, D)],
        out_shape=[jax.ShapeDtypeStruct((T, D), F32), jax.ShapeDtypeStruct((T, IN_WIDTH), BF16),
                   jax.ShapeDtypeStruct((8, D), F32), jax.ShapeDtypeStruct((B, 8, D), F32)],
        compiler_params=_params("arbitrary"),
    )(dqa, dka, dva, dqb, dkb, dvb, cos_t, sin_t, w_in, x2d, mod3, g_attn, dx1)


def _matmul_tn(a, b, name, tm=None, tk=512):
    T, M = a.shape
    N = b.shape[1]
    tm = M if tm is None else tm
    nk = T // tk

    def body(a_ref, b_ref, o_ref, acc):
        k = pl.program_id(1)

        @pl.when(k == 0)
        def _():
            acc[...] = jnp.zeros_like(acc)

        acc[...] += _tn(a_ref[...], b_ref[...])

        @pl.when(k == nk - 1)
        def _():
            o_ref[...] = acc[...].astype(BF16)

    return pl.pallas_call(
        body, name=name, grid=(M // tm, nk),
        in_specs=[pl.BlockSpec((tk, tm), lambda i, k: (k, i)), pl.BlockSpec((tk, N), lambda i, k: (k, 0))],
        out_specs=pl.BlockSpec((tm, N), lambda i, k: (i, 0)),
        out_shape=jax.ShapeDtypeStruct((M, N), BF16),
        scratch_shapes=[pltpu.VMEM((tm, N), F32)],
        compiler_params=_params("parallel", "arbitrary"),
    )(a, b)


def _na_geometry(S):
    rows = S // GRID_W
    wr = min(NA_ROWS_MAX, rows)
    return rows, wr


def _na_window(r, rows, wr):
    rs = jnp.clip(r - wr // 2, 0, rows - wr)
    return pl.multiple_of(rs * GRID_W, GRID_W), pl.multiple_of((rs - r + NA_ROWS_MAX - 1) * GRID_W, GRID_W)


NA_STEP_PAIRS = 2
NA_GW = NA_STEP_PAIRS * 128
NA_BWD_ROWS = 2
NA_ROWS_PER_STEP = 4


def _na_specs(S, kw_n, order):
    ng = NA_PAIRS // NA_STEP_PAIRS

    def col(k):
        return pl.BlockSpec((1, S, NA_GW), lambda *ids: (order(*ids)[0], 0, k * ng + order(*ids)[1]))
    bias = pl.BlockSpec((NA_STEP_PAIRS, N_DR * GRID_W, 128), lambda *ids: (order(*ids)[1], 0, 0))
    out = pl.BlockSpec((1, S, NA_GW), lambda *ids: (order(*ids)[0], 0, order(*ids)[1]))
    return col(0), col(1), col(2), bias, out


def _block_diag(t):
    left = lax.broadcasted_iota(jnp.int32, t.shape, 1) < HEAD_DIM
    zero = jnp.zeros_like(t)
    return jnp.concatenate([jnp.where(left, t, zero), jnp.where(left, zero, t)], axis=0)


def _diag_blocks(res):
    left = lax.broadcasted_iota(jnp.int32, (HEAD_DIM, 128), 1) < HEAD_DIM
    return jnp.where(left, res[:HEAD_DIM], res[HEAD_DIM:])


def _col_softmax(st):
    e = jnp.exp(st - jnp.max(st, axis=0, keepdims=True))
    return e * (1.0 / jnp.sum(e, axis=0, keepdims=True))


def _na_fwd(qkv, bias, tasks=()):
    B, S, _ = qkv.shape
    rows, wr = _na_geometry(S)
    kw_n = wr * GRID_W

    def body(q_ref, k_ref, v_ref, b_ref, o_ref):
        def step(it, carry):
            win = [_na_window(it * NA_ROWS_PER_STEP + u, rows, wr) for u in range(NA_ROWS_PER_STEP)]
            qrows = [pl.ds(pl.multiple_of((it * NA_ROWS_PER_STEP + u) * GRID_W, GRID_W), GRID_W) for u in range(NA_ROWS_PER_STEP)]
            krows = [pl.ds(w[0], kw_n) for w in win]
            brows = [pl.ds(w[1], kw_n) for w in win]
            lanes = [pl.ds(p * 128, 128) for p in range(NA_STEP_PAIRS)]
            chains = [(u, p) for u in range(NA_ROWS_PER_STEP) for p in range(NA_STEP_PAIRS)]
            st = {(u, p): _nt(k_ref[0, krows[u], lanes[p]], _block_diag(q_ref[0, qrows[u], lanes[p]])) for u, p in chains}
            pn = {(u, p): _col_softmax(st[(u, p)] + b_ref[p, brows[u], :]).astype(BF16) for u, p in chains}
            out = {(u, p): _diag_blocks(_tn(pn[(u, p)], v_ref[0, krows[u], lanes[p]])) for u, p in chains}
            for u in range(NA_ROWS_PER_STEP):
                o_ref[0, qrows[u], :] = jnp.concatenate([out[(u, p)] for p in range(NA_STEP_PAIRS)], axis=1)
            return carry

        lax.fori_loop(0, rows // NA_ROWS_PER_STEP, step, 0)

    q, k, v, bs, out = _na_specs(S, kw_n, lambda b, g: (b, g))
    return _hosted_call(body, "na_fwd", (B, NA_PAIRS // NA_STEP_PAIRS), [q, k, v, bs], [out],
                        [jax.ShapeDtypeStruct((B, S, NA_WIDTH), F32)], (qkv, qkv, qkv, bias), tasks)


def _na_bwd(qkv, bias, doa, tasks=()):
    B, S, _ = qkv.shape
    rows, wr = _na_geometry(S)
    kw_n = wr * GRID_W

    def body(q_ref, k_ref, v_ref, b_ref, do_ref, dq_ref, dk_ref, dv_ref, db_ref):
        @pl.when(pl.program_id(1) == 0)
        def _():
            db_ref[...] = jnp.zeros_like(db_ref)

        dk_ref[...] = jnp.zeros_like(dk_ref)
        dv_ref[...] = jnp.zeros_like(dv_ref)

        def step(it, carry):
            nu, pairs = range(NA_BWD_ROWS), range(NA_STEP_PAIRS)
            win = [_na_window(it * NA_BWD_ROWS + u, rows, wr) for u in nu]
            qrows = [pl.ds(pl.multiple_of((it * NA_BWD_ROWS + u) * GRID_W, GRID_W), GRID_W) for u in nu]
            krows = [pl.ds(w[0], kw_n) for w in win]
            brows = [pl.ds(w[1], kw_n) for w in win]
            lanes = [pl.ds(p * 128, 128) for p in pairs]
            chains = [(u, p) for u in nu for p in pairs]
            kp = {(u, p): k_ref[0, krows[u], lanes[p]] for u, p in chains}
            qbd = {(u, p): _block_diag(q_ref[0, qrows[u], lanes[p]]) for u, p in chains}
            dobd = {(u, p): _block_diag(do_ref[0, qrows[u], lanes[p]]) for u, p in chains}
            st = {c: _nt(kp[c], qbd[c]) for c in chains}
            dpt = {(u, p): _nt(v_ref[0, krows[u], lanes[p]], dobd[(u, p)]) for u, p in chains}
            pn = {(u, p): _col_softmax(st[(u, p)] + b_ref[p, brows[u], :]) for u, p in chains}
            dst = {c: pn[c] * (dpt[c] - jnp.sum(pn[c] * dpt[c], axis=0, keepdims=True)) for c in chains}
            dsb = {c: dst[c].astype(BF16) for c in chains}
            dq = {c: _diag_blocks(_tn(dsb[c], kp[c])) for c in chains}
            dk = {c: _nn(dsb[c], qbd[c]) for c in chains}
            dv = {c: _nn(pn[c].astype(BF16), dobd[c]) for c in chains}
            for u in nu:
                dq_ref[0, qrows[u], :] = jnp.concatenate([dq[(u, p)] for p in pairs], axis=1)
                dk_ref[0, krows[u], :] += jnp.concatenate([dk[(u, p)] for p in pairs], axis=1)
                dv_ref[0, krows[u], :] += jnp.concatenate([dv[(u, p)] for p in pairs], axis=1)
                for p in pairs:
                    db_ref[p, brows[u], :] += dst[(u, p)]
            return carry

        lax.fori_loop(0, rows // NA_BWD_ROWS, step, 0)

    q, k, v, bs, out = _na_specs(S, kw_n, lambda g, b: (b, g))
    hs = jax.ShapeDtypeStruct((B, S, NA_WIDTH), F32)
    return _hosted_call(body, "na_bwd", (NA_PAIRS // NA_STEP_PAIRS, B), [q, k, v, bs, out], [out, out, out, bs],
                        [hs, hs, hs, jax.ShapeDtypeStruct((NA_PAIRS, N_DR * GRID_W, 128), F32)], (qkv, qkv, qkv, bias, doa), tasks)


SW_PAIRS = SW_HEADS // 2


def _sw_band(n, S):
    kw_n = 3 * SW_BLOCK
    start = pl.multiple_of(jnp.clip(n * SW_BLOCK - SW_BLOCK, 0, S - kw_n), SW_BLOCK)
    kpos = start + lax.broadcasted_iota(jnp.int32, (kw_n, SW_BLOCK), 0)
    qpos = n * SW_BLOCK + lax.broadcasted_iota(jnp.int32, (kw_n, SW_BLOCK), 1)
    return start, jnp.abs(qpos - kpos) <= SW_WINDOW


def _kv_halves(t):
    left = lax.broadcasted_iota(jnp.int32, t.shape, 1) < HEAD_DIM
    swapped = pltpu.roll(t, HEAD_DIM, axis=1)
    zero = jnp.zeros_like(t)
    return {(0, 0): jnp.where(left, t, zero), (0, 1): jnp.where(left, zero, swapped),
            (1, 0): jnp.where(left, swapped, zero), (1, 1): jnp.where(left, zero, t)}


def _sw_probs(st, ok, sk):
    st = jnp.where(ok, st, NEG)
    m = jnp.maximum(jnp.max(st, axis=0, keepdims=True), sk)
    e = jnp.exp(st - m)
    esk = jnp.exp(sk - m)
    inv = 1.0 / (jnp.sum(e, axis=0, keepdims=True) + esk)
    return e * inv, esk * inv


def _sw_specs(S):
    q = pl.BlockSpec((1, S, SW_WIDTH), lambda b, *_: (b, 0, ROPE_LO // SW_WIDTH))
    k = pl.BlockSpec((1, S, SW_KV_WIDTH), lambda b, *_: (b, 0, (ROPE_LO + SW_WIDTH) // SW_KV_WIDTH))
    v = pl.BlockSpec((1, S, SW_KV_WIDTH), lambda b, *_: (b, 0, (ROPE_LO + ROPE_WIDTH) // SW_KV_WIDTH))
    return q, k, v


SW_FWD_SPLIT = 2


def _sw_fwd(sink, qkv, tasks=()):
    B, S, _ = qkv.shape
    kw_n = 3 * SW_BLOCK

    def body(sink_ref, q_ref, k_ref, v_ref, o_ref):
        def step(n, carry):
            start, ok = _sw_band(n, S)
            qrows = pl.ds(pl.multiple_of(n * SW_BLOCK, SW_BLOCK), SW_BLOCK)
            krows = pl.ds(start, kw_n)
            kh, vh = _kv_halves(k_ref[0, krows, :]), _kv_halves(v_ref[0, krows, :])
            heads = [(p, e) for p in range(SW_PAIRS) for e in range(2)]
            qp = [q_ref[0, qrows, pl.ds(p * 128, 128)] for p in range(SW_PAIRS)]
            kv_of = lambda p: p // (SW_PAIRS // SW_KV_HEADS)
            st = {(p, e): _nt(kh[(kv_of(p), e)], qp[p]) for p, e in heads}
            pn = {(p, e): _sw_probs(st[(p, e)], ok, sink_ref[2 * p + e])[0].astype(BF16) for p, e in heads}
            outs = [_tn(pn[(p, 0)], vh[(kv_of(p), 0)]) + _tn(pn[(p, 1)], vh[(kv_of(p), 1)]) for p in range(SW_PAIRS)]
            o_ref[0, qrows, :] = jnp.concatenate(outs, axis=1)
            return carry

        half = (S // SW_BLOCK) // SW_FWD_SPLIT
        lax.fori_loop(pl.program_id(1) * half, (pl.program_id(1) + 1) * half, step, 0)

    q, k, v = _sw_specs(S)
    return _hosted_call(
        body, "sw_fwd", (B, SW_FWD_SPLIT), [pl.BlockSpec(memory_space=pltpu.SMEM), q, k, v],
        [pl.BlockSpec((1, S, SW_WIDTH), lambda b, s: (b, 0, 0))], [jax.ShapeDtypeStruct((B, S, SW_WIDTH), F32)],
        (sink, qkv, qkv, qkv), tasks)


def _sw_bwd(sink, qkv, dob):
    B, S, _ = qkv.shape
    kw_n = 3 * SW_BLOCK

    fold_rows = 256

    def body(sink_ref, q_ref, k_ref, v_ref, do_ref, dq_ref, dk_ref, dv_ref, dsink_ref, dk_acc, dv_acc):
        @pl.when(pl.program_id(0) == 0)
        def _():
            dsink_ref[...] = jnp.zeros_like(dsink_ref)

        dk_acc[...] = jnp.zeros_like(dk_acc)
        dv_acc[...] = jnp.zeros_like(dv_acc)
        ppk = SW_PAIRS // SW_KV_HEADS

        def step(n, carry):
            start, ok = _sw_band(n, S)
            qrows = pl.ds(pl.multiple_of(n * SW_BLOCK, SW_BLOCK), SW_BLOCK)
            krows = pl.ds(start, kw_n)
            kh, vh = _kv_halves(k_ref[0, krows, :]), _kv_halves(v_ref[0, krows, :])
            heads = [(p, e) for p in range(SW_PAIRS) for e in range(2)]
            qp = [q_ref[0, qrows, pl.ds(p * 128, 128)] for p in range(SW_PAIRS)]
            dop = [do_ref[0, qrows, pl.ds(p * 128, 128)] for p in range(SW_PAIRS)]
            st = {(p, e): _nt(kh[(p // ppk, e)], qp[p]) for p, e in heads}
            dpt = {(p, e): _nt(vh[(p // ppk, e)], dop[p]) for p, e in heads}
            pnb, dsb = {}, {}
            for p, e in heads:
                pn, psink = _sw_probs(st[(p, e)], ok, sink_ref[2 * p + e])
                delta = jnp.sum(pn * dpt[(p, e)], axis=0, keepdims=True)
                dsb[(p, e)] = (pn * (dpt[(p, e)] - delta)).astype(BF16)
                pnb[(p, e)] = pn.astype(BF16)
                dsink_ref[2 * p + e:2 * p + e + 1, :] += -(psink * delta)
            dq_ref[0, qrows, :] = jnp.concatenate(
                [_tn(dsb[(p, 0)], kh[(p // ppk, 0)]) + _tn(dsb[(p, 1)], kh[(p // ppk, 1)]) for p in range(SW_PAIRS)], axis=1)
            left = lax.broadcasted_iota(jnp.int32, (kw_n, 128), 1) < HEAD_DIM
            dks, dvs = [], []
            for kv in range(SW_KV_HEADS):
                dk = dv = None
                for p in range(kv * ppk, (kv + 1) * ppk):
                    dk_p = jnp.where(left, _nn(dsb[(p, 0)], qp[p]), _nn(dsb[(p, 1)], qp[p]))
                    dv_p = jnp.where(left, _nn(pnb[(p, 0)], dop[p]), _nn(pnb[(p, 1)], dop[p]))
                    dk = dk_p if dk is None else dk + dk_p
                    dv = dv_p if dv is None else dv + dv_p
                dks.append(dk)
                dvs.append(dv)
            dk_acc[krows, :] += jnp.concatenate(dks, axis=1)
            dv_acc[krows, :] += jnp.concatenate(dvs, axis=1)
            return carry

        lax.fori_loop(0, S // SW_BLOCK, step, 0)

        def fold(i, carry):
            rows = pl.ds(pl.multiple_of(i * fold_rows, fold_rows), fold_rows)
            left = lax.broadcasted_iota(jnp.int32, (fold_rows, 128), 1) < HEAD_DIM
            for acc, out_ref in ((dk_acc, dk_ref), (dv_acc, dv_ref)):
                a, b = acc[rows, 0:128], acc[rows, 128:256]
                out_ref[0, rows, :] = jnp.where(left, a + pltpu.roll(a, HEAD_DIM, axis=1), b + pltpu.roll(b, HEAD_DIM, axis=1))
            return carry

        lax.fori_loop(0, S // fold_rows, fold, 0)

        @pl.when(pl.program_id(0) == B - 1)
        def _():
            dsink_ref[...] = jnp.broadcast_to(jnp.sum(dsink_ref[...], axis=1, keepdims=True), dsink_ref.shape)

    q, k, v = _sw_specs(S)
    qo = pl.BlockSpec((1, S, SW_WIDTH), lambda b: (b, 0, 0))
    ko = pl.BlockSpec((1, S, SW_KV_WIDTH), lambda b: (b, 0, 0))
    return pl.pallas_call(
        body, name="sw_bwd", grid=(B,),
        in_specs=[pl.BlockSpec(memory_space=pltpu.SMEM), q, k, v, qo],
        out_specs=[qo, ko, ko, _full((SW_HEADS, 128))],
        out_shape=[jax.ShapeDtypeStruct((B, S, SW_WIDTH), F32), jax.ShapeDtypeStruct((B, S, SW_KV_WIDTH), F32),
                   jax.ShapeDtypeStruct((B, S, SW_KV_WIDTH), F32), jax.ShapeDtypeStruct((SW_HEADS, 128), F32)],
        scratch_shapes=[pltpu.VMEM((S, 2 * SW_KV_WIDTH), F32), pltpu.VMEM((S, 2 * SW_KV_WIDTH), F32)],
        compiler_params=_params("arbitrary"),
    )(sink, qkv, qkv, qkv, dob)


def _pack_sum(packs):
    W = packs.shape[1]

    def body(p_ref, o_ref, s_ref):
        tot = p_ref[0:8, :]
        for d in range(1, N_DEV):
            tot = tot + p_ref[8 * d:8 * d + 8, :]
        o_ref[...] = tot
        s_ref[...] = tot[0:1, :] + tot[1:2, :]

    return pl.pallas_call(body, name="pack_sum", out_shape=[jax.ShapeDtypeStruct((8, W), F32), jax.ShapeDtypeStruct((1, W), F32)],
                          compiler_params=pltpu.CompilerParams(vmem_limit_bytes=VMEM_LIMIT))(packs)


def _adam_update(w, g, m, v):
    c1 = 1.0 - ADAM_B1 ** ADAM_STEP
    c2 = 1.0 - ADAM_B2 ** ADAM_STEP
    nm = ADAM_B1 * m + (1.0 - ADAM_B1) * g
    nv = ADAM_B2 * v + (1.0 - ADAM_B2) * (g * g)
    return -ADAM_LR * ((nm / c1) / (jnp.sqrt(nv / c2) + ADAM_EPS) + ADAM_WD * w), nm, nv


def _adamw(w, g, m, v, name):
    def body(w_ref, g_ref, m_ref, v_ref, d_ref, nm_ref, nv_ref):
        d_ref[...], nm_ref[...], nv_ref[...] = _adam_update(w_ref[...], g_ref[...], m_ref[...], v_ref[...])

    s = jax.ShapeDtypeStruct(w.shape, F32)
    return pl.pallas_call(body, name=name, out_shape=[s, s, s],
                          compiler_params=pltpu.CompilerParams(vmem_limit_bytes=VMEM_LIMIT))(w, g, m, v)


def _sum_adamw(own, recvb, w, m, v, name):
    R, C = own.shape
    rc = _row_chunk(R)

    def body(own_ref, r_ref, w_ref, m_ref, v_ref, g_ref, d_ref, nm_ref, nv_ref):
        def chunk(i, carry):
            rows = pl.ds(pl.multiple_of(i * rc, rc), rc)
            g = own_ref[rows, :]
            for j in range(3):
                g = g + r_ref[j, rows, :].astype(F32)
            g_ref[rows, :] = g
            d_ref[rows, :], nm_ref[rows, :], nv_ref[rows, :] = _adam_update(w_ref[rows, :], g, m_ref[rows, :], v_ref[rows, :])
            return carry

        lax.fori_loop(0, R // rc, chunk, 0)

    s = jax.ShapeDtypeStruct((R, C), F32)
    return pl.pallas_call(body, name=name, out_shape=[s, s, s, s],
                          compiler_params=pltpu.CompilerParams(vmem_limit_bytes=VMEM_LIMIT))(own, recvb, w, m, v)


def _by_device(dw):
    return dw.reshape(N_DEV, dw.shape[0] // N_DEV, dw.shape[1])


def _local_step(x, mod, g_attn, w_in, bias, sw_sink, g_na_out, g_sw_out, w_out, g_ffn, w_up, conv_w, conv_b, w_down,
                g_final, target, sharded):
    B, S, D = x.shape
    T = B * S
    x2d = x.reshape(T, D)
    mod3 = mod.reshape(B, 6, D)
    cos_t, sin_t = _rope_tables(S)
    sink = sw_sink.reshape(SW_HEADS)
    n_tiles = T // TOKEN_TILE
    full = lambda g: g.reshape(N_DEV * g.shape[1], g.shape[2])

    if sharded:
        half = w_up.shape[0] // 2
        tasks = [_gather_task(w_out, n_tiles // 2), _gather_task(w_up, 3 * n_tiles // 4, rows=(0, half))]
    (h, qkv), got = _attn_in(x2d, mod3, g_attn, w_in, cos_t, sin_t, S, tasks if sharded else [])
    if sharded:
        w_out, w_up_part = full(got[0][0]), got[1][0]
    qkv3 = qkv.reshape(B, S, IN_WIDTH)
    na_steps = B * (NA_PAIRS // NA_STEP_PAIRS)
    (oa,), got = _na_fwd(qkv3, bias, [_gather_task(w_up, na_steps - 1, rows=(half, half), into=w_up_part)] if sharded else [])
    if sharded:
        w_up = full(got[0][0])
    oa = oa.reshape(T, NA_WIDTH)
    (ob,), got = _sw_fwd(sink, qkv3, [_gather_task(w_down, B * SW_FWD_SPLIT - 1)] if sharded else [])
    if sharded:
        w_down = full(got[0][0])
    ob = ob.reshape(T, SW_WIDTH)
    mixin, mix, x1 = _attn_out(oa, ob, x2d, mod3, g_na_out, g_sw_out, w_out, S)
    h2, val, gt = _ffn_up(x1, mod3, g_ffn, w_up, S)
    a, act, vd, dx2, df, gstat_f, bstat_f = _ffn_down(gt, val, conv_w, conv_b, w_down, x1, mod3, g_final, target.reshape(T, D), B, S)
    F = val.shape[1]

    dw_down = _matmul_tn(a, df, "dw_down")
    (dval, dgc, cstat), got = _ffn_down_bwd(df, w_down, act, vd, [_swap_task(_by_device(dw_down))] if sharded else [])
    if sharded:
        send_down, own_down = _chip_sums(_by_device(dw_down), got[0][0])
    du, dw_up, cstat_w = _dw_up_conv(dgc, dval, gt, conv_w, h2, S)
    (dx1, dmix, gstat_u, bstat_u), got = _ffn_up_bwd(du, w_up, x1, mod3, g_ffn, dx2, mix, B, S,
                                                     [_exchange_task(send_down), _swap_task(_by_device(dw_up))] if sharded else [])
    if sharded:
        dw_down = (own_down, got[0][0])
        send_up, own_up = _chip_sums(_by_device(dw_up), got[1][0])
    dw_out = _matmul_tn(mixin, dmix, "dw_out")
    (doa, dob, gstat_o), got = _attn_out_bwd(dmix, w_out, oa, ob, g_na_out, g_sw_out, [_swap_task(_by_device(dw_out))] if sharded else [])
    if sharded:
        send_out, own_out = _chip_sums(_by_device(dw_out), got[0][0])
    (dqa, dka, dva, dbt), got = _na_bwd(qkv3, bias, doa.reshape(B, S, NA_WIDTH),
                                        [_exchange_task(send_up), _exchange_task(send_out)] if sharded else [])
    if sharded:
        dw_up, dw_out = (own_up, got[0][0]), (own_out, got[1][0])
    dqb, dkb, dvb, dsink = _sw_bwd(sink, qkv3, dob.reshape(B, S, SW_WIDTH))
    r2 = lambda t: t.reshape(T, t.shape[-1])
    grad_x, dproj, gstat_i, bstat_i = _attn_in_bwd(r2(dqa), r2(dka), r2(dva), r2(dqb), r2(dkb), r2(dvb), cos_t, sin_t, w_in, x2d, mod3,
                                                   g_attn, dx1, B, S)
    dw_in = _matmul_tn(dproj, h, "dw_in")

    dmod = jnp.stack([bstat_i[:, 0], bstat_i[:, 1], bstat_u[:, 2], bstat_u[:, 0], bstat_u[:, 1], bstat_f[:, 0]], axis=1)
    small = dict(g_attn=gstat_i[0], g_ffn=gstat_u[0], g_final=gstat_f[0], loss=gstat_f[1, 0], g_na_out=gstat_o[0], g_sw_out=gstat_o[1],
                 sw_sink=dsink[:, 0], conv_b=cstat[0], conv_w=cstat_w[9:12], dbt=dbt)
    return grad_x.reshape(B, S, D), dict(w_in=dw_in, w_out=dw_out, w_up=dw_up, w_down=dw_down), dmod, small


def _pad_lanes(v, w):
    return jnp.pad(v, (0, w - v.shape[0]))


def kernel(x, c, w_ada, b_ada, g_attn, w_in, na_rpb, sw_sink, g_na_out, g_sw_out, w_out, g_ffn, w_up, conv_w, conv_b, w_down, g_final, loss_target, m_w_ada, m_b_ada, m_g_attn, m_w_in, m_na_rpb, m_sw_sink, m_g_na_out, m_g_sw_out, m_w_out, m_g_ffn, m_w_up, m_conv_w, m_conv_b, m_w_down, m_g_final, v_w_ada, v_b_ada, v_g_attn, v_w_in, v_na_rpb, v_sw_sink, v_g_na_out, v_g_sw_out, v_w_out, v_g_ffn, v_w_up, v_conv_w, v_conv_b, v_w_down, v_g_final):
    B, S, D = x.shape
    me = 4 * lax.axis_index("x") + 2 * lax.axis_index("y") + lax.axis_index("c")
    ada_c = w_ada.shape[2]
    F_l = conv_w.shape[2]
    onehot = _toeplitz_onehot()

    cw_l = jnp.pad(conv_w[0], ((0, 8 - conv_w.shape[1]), (0, 0)))
    c_l = jnp.pad(c, ((0, 8 - B), (0, 0)))
    (slabs,) = _all_gather([jnp.concatenate([c_l, cw_l], axis=1)], "gather_c")
    c_all = slabs[:, :, :D].reshape(N_DEV * 8, D)
    conv_w_f = jnp.transpose(slabs[:, :3, D:], (1, 0, 2)).reshape(3, N_DEV * F_l)

    tr = {"w_in", "w_up"}
    w_in_t = jnp.transpose(w_in[0])
    shards = dict(w_out=w_out[0].astype(BF16), w_up=jnp.transpose(w_up[0]).astype(BF16), w_down=w_down[0].astype(BF16))

    b_ada_l = lax.dynamic_slice(b_ada, (0, me * ada_c), (1, ada_c))
    mod_cols = _ada_fwd(c_all, w_ada[0], b_ada_l)
    mod_all, w_in_all = _all_gather([mod_cols, w_in_t.astype(BF16)], "gather_mod_w_in")
    mod_mine = lax.dynamic_slice(mod_all, (0, me * 8, 0), (N_DEV, B, ada_c))
    mod = jnp.transpose(mod_mine, (1, 0, 2)).reshape(B, N_DEV * ada_c)
    w_in_f = w_in_all.reshape(N_DEV * w_in_t.shape[0], D)

    bias = _na_bias_table(na_rpb[0], onehot)

    grad_x, dw, dmod, small = _local_step(x, mod, g_attn, w_in_f, bias, sw_sink, g_na_out, g_sw_out, shards["w_out"], g_ffn,
                                          shards["w_up"], conv_w_f, conv_b, shards["w_down"], g_final.reshape(1, D), loss_target,
                                          sharded=True)
    drpb = _na_bias_grad(small["dbt"], onehot)

    row2 = jnp.concatenate([small["g_attn"], small["g_ffn"], small["g_final"], small["g_na_out"], small["g_sw_out"],
                            _pad_lanes(small["sw_sink"], 128), _pad_lanes(small["loss"].reshape(1), 128)])
    rows = [dmod.reshape(B, 6 * D)[0], dmod.reshape(B, 6 * D)[1], _pad_lanes(row2, PACK_W), _pad_lanes(small["conv_b"], PACK_W),
            _pad_lanes(drpb, PACK_W)] + [_pad_lanes(small["conv_w"][k], PACK_W) for k in range(3)]
    g_w_in, (packs,) = _reduce_scatter(_by_device(dw["w_in"]), "scatter_w_in", rider=_gather_task(jnp.stack(rows), 0))
    tot, g_b_ada = _pack_sum(packs.reshape(N_DEV * 8, PACK_W))

    o = 0
    rep = {}
    for nm, wd in (("g_attn", D), ("g_ffn", D), ("g_final", D), ("g_na_out", NA_WIDTH), ("g_sw_out", SW_WIDTH), ("sw_sink", 128), ("loss", 128)):
        rep[nm] = tot[2, o:o + wd]
        o += wd
    loss = rep["loss"][0]
    g_conv_b = tot[3:4, :conv_b.shape[1]]
    g_na_rpb = tot[4, :NA_HEADS * N_DR * N_DC].reshape(na_rpb.shape)
    g_conv_w = lax.dynamic_slice(tot[5:8], (0, me * F_l), (3, F_l)).reshape(conv_w.shape)

    dmod_cols = lax.dynamic_slice(packs.reshape(N_DEV * 8, PACK_W), (0, me * ada_c), (N_DEV * 8, ada_c))
    g_w_ada = _ada_bwd(c_all, dmod_cols)[None]

    grads = dict(
        w_ada=g_w_ada, b_ada=g_b_ada, g_attn=rep["g_attn"][None], w_in=g_w_in, na_rpb=g_na_rpb,
        sw_sink=rep["sw_sink"][None, :SW_HEADS], g_na_out=rep["g_na_out"][None], g_sw_out=rep["g_sw_out"][None],
        w_out=None, g_ffn=rep["g_ffn"][None], w_up=None, conv_w=g_conv_w, conv_b=g_conv_b, w_down=None, g_final=rep["g_final"])
    weights = dict(w_ada=w_ada, b_ada=b_ada, g_attn=g_attn, w_in=w_in, na_rpb=na_rpb, sw_sink=sw_sink, g_na_out=g_na_out,
                   g_sw_out=g_sw_out, w_out=w_out, g_ffn=g_ffn, w_up=w_up, conv_w=conv_w, conv_b=conv_b, w_down=w_down, g_final=g_final)
    ms = dict(w_ada=m_w_ada, b_ada=m_b_ada, g_attn=m_g_attn, w_in=m_w_in, na_rpb=m_na_rpb, sw_sink=m_sw_sink, g_na_out=m_g_na_out,
              g_sw_out=m_g_sw_out, w_out=m_w_out, g_ffn=m_g_ffn, w_up=m_w_up, conv_w=m_conv_w, conv_b=m_conv_b, w_down=m_w_down, g_final=m_g_final)
    vs = dict(w_ada=v_w_ada, b_ada=v_b_ada, g_attn=v_g_attn, w_in=v_w_in, na_rpb=v_na_rpb, sw_sink=v_sw_sink, g_na_out=v_g_na_out,
              g_sw_out=v_g_sw_out, w_out=v_w_out, g_ffn=v_g_ffn, w_up=v_w_up, conv_w=v_conv_w, conv_b=v_conv_b, w_down=v_w_down, g_final=v_g_final)
    names = list(weights)
    deltas, new_m, new_v = {}, {}, {}
    for nm in names:
        shp = weights[nm].shape
        if nm in tr:
            r = lambda t: jnp.transpose(t[0])
            back = lambda t: jnp.transpose(t)[None]
        else:
            two_d = (shp[-2], shp[-1]) if len(shp) >= 3 and nm != "na_rpb" else (1, int(np.prod(shp)))
            r = lambda t: t.reshape(two_d)
            back = lambda t: t.reshape(shp)
        if nm in ("w_out", "w_up", "w_down"):
            g2, d_, m_, v_ = _sum_adamw(*dw[nm], r(weights[nm]), r(ms[nm]), r(vs[nm]), "adamw_" + nm)
        else:
            g2 = grads[nm] if nm in tr else r(grads[nm])
            d_, m_, v_ = _adamw(r(weights[nm]), g2, r(ms[nm]), r(vs[nm]), "adamw_" + nm)
        deltas[nm], new_m[nm], new_v[nm], grads[nm] = back(d_), back(m_), back(v_), back(g2)
    return (loss, grad_x, *[grads[n] for n in names], *[deltas[n] for n in names], *[new_m[n] for n in names],
            *[new_v[n] for n in names])
```

```python
import functools

import numpy as np
import jax
import jax.numpy as jnp
from jax import lax
from jax.experimental import pallas as pl
from jax.experimental.pallas import tpu as pltpu

F32, BF16 = jnp.float32, jnp.bfloat16
MESH_ID = pl.DeviceIdType.MESH
N_DEV = 8

HEAD_DIM = 64
NA_HEADS = 8
SW_HEADS = 8
SW_KV_HEADS = 2
SW_GROUP = SW_HEADS // SW_KV_HEADS
NA_WIDTH = NA_HEADS * HEAD_DIM
SW_WIDTH = SW_HEADS * HEAD_DIM
SW_KV_WIDTH = SW_KV_HEADS * HEAD_DIM
ROPE_WIDTH = SW_WIDTH + SW_KV_WIDTH
IN_WIDTH = 3 * NA_WIDTH + SW_WIDTH + 2 * SW_KV_WIDTH
ROPE_LO = 3 * NA_WIDTH
GRID_W = 64
NA_ROWS_MAX = 8
NA_COLS = 16
N_DR = 2 * NA_ROWS_MAX - 1
N_DC = 2 * NA_COLS - 1
SW_WINDOW = 128
SW_BLOCK = 128
ROPE_THETA = 10000.0
EPS = 1e-6
NEG = -1e30
Q_SCALE = HEAD_DIM ** -0.5

ADAM_LR = 0.001
ADAM_B1 = 0.9
ADAM_B2 = 0.999
ADAM_EPS = 1e-08
ADAM_WD = 0.01
ADAM_STEP = 10

TOKEN_TILE = 256
VMEM_LIMIT = 56 * 1024 * 1024

PACK_W = 6144


def _nn(a, b):
    return jnp.dot(a, b, preferred_element_type=F32)


def _nt(a, b):
    return lax.dot_general(a, b, (((1,), (1,)), ((), ())), preferred_element_type=F32)


def _tn(a, b):
    return lax.dot_general(a, b, (((0,), (0,)), ((), ())), preferred_element_type=F32)


def _rms(x):
    r = lax.rsqrt(jnp.mean(x * x, axis=-1, keepdims=True) + EPS)
    return x * r, r


def _rms_bwd(xn, r, gy):
    return r * (gy - xn * jnp.mean(xn * gy, axis=-1, keepdims=True))


def _params(*sem):
    return pltpu.CompilerParams(dimension_semantics=sem, vmem_limit_bytes=VMEM_LIMIT)


def _full(shape):
    n = len(shape)
    return pl.BlockSpec(shape, lambda *_: (0,) * n)


def _mesh_pos():
    return lax.axis_index("x"), lax.axis_index("y"), lax.axis_index("c")


def _all_gather(arrays, name):
    tasks = [_gather_task(a, 0) for a in arrays]
    n = len(tasks)

    def body(*refs):
        ins, outs, sems = refs[:n], refs[n:2 * n], refs[2 * n:]
        parts = [((ins[i],), (outs[i],), sems[3 * i:3 * i + 3]) for i in range(n)]
        for phase in ("start", "mid", "finish"):
            for t, p in zip(tasks, parts):
                getattr(t, phase)(*p)

    hbm = pl.BlockSpec(memory_space=pl.ANY)
    return pl.pallas_call(
        body, name=name, out_shape=[s for t in tasks for s in t.out_shapes], in_specs=[hbm] * n, out_specs=[hbm] * n,
        scratch_shapes=[s for t in tasks for s in t.sems],
    )(*arrays)


def _row_chunk(r):
    for rc in (128, 64, 32, 16):
        if r % rc == 0:
            return rc
    raise ValueError(f"rows {r} not a multiple of 16")


def _reduce_scatter(g8, name, rider=None):
    _, R, C = g8.shape
    rc = _row_chunk(R)
    n_ri, n_ro = (len(rider.inputs), len(rider.out_shapes)) if rider is not None else (0, 0)

    def body(g_ref, *refs):
        r_ins, refs = refs[:n_ri], refs[n_ri:]
        out_ref, refs = refs[0], refs[1:]
        r_outs, refs = refs[:n_ro], refs[n_ro:]
        recva, sendb, recvb, sa, ra, sb, rb = refs[:7]
        r_sems = refs[7:]
        x_, y_, c_ = _mesh_pos()
        sibling = (x_, y_, 1 - c_)
        if rider is not None:
            rider.start(r_ins, r_outs, r_sems)
        copies_a = []
        for k in range(4):
            cp = pltpu.make_async_remote_copy(
                src_ref=g_ref.at[2 * k + (1 - c_)], dst_ref=recva.at[k],
                send_sem=sa.at[k], recv_sem=ra.at[k], device_id=sibling, device_id_type=MESH_ID)
            cp.start()
            copies_a.append(cp)
        for cp in copies_a:
            cp.wait_recv()
        if rider is not None and rider.mid is not None:
            rider.mid(r_ins, r_outs, r_sems)

        def chip_sum(k, rows):
            return g_ref[2 * k + c_, rows, :].astype(F32) + recva[k, rows, :].astype(F32)

        flips = [(1 - x_, y_), (x_, 1 - y_), (1 - x_, 1 - y_)]
        copies_b = []
        for j, (tx, ty) in enumerate(flips):
            kt = 2 * tx + ty

            def fill(i, carry, j=j, kt=kt):
                rows = pl.ds(pl.multiple_of(i * rc, rc), rc)
                sendb[j, rows, :] = chip_sum(kt, rows).astype(BF16)
                return carry

            lax.fori_loop(0, R // rc, fill, 0)
            cp = pltpu.make_async_remote_copy(
                src_ref=sendb.at[j], dst_ref=recvb.at[j],
                send_sem=sb.at[j], recv_sem=rb.at[j], device_id=(tx, ty, c_), device_id_type=MESH_ID)
            cp.start()
            copies_b.append(cp)
        for cp in copies_b:
            cp.wait_recv()
        kme = 2 * x_ + y_

        def total(i, carry):
            rows = pl.ds(pl.multiple_of(i * rc, rc), rc)
            acc = chip_sum(kme, rows)
            for j in range(3):
                acc = acc + recvb[j, rows, :].astype(F32)
            out_ref[rows, :] = acc
            return carry

        lax.fori_loop(0, R // rc, total, 0)
        for cp in copies_a + copies_b:
            cp.wait_send()
        if rider is not None:
            rider.finish(r_ins, r_outs, r_sems)

    vm = pl.BlockSpec(memory_space=pltpu.VMEM)
    hbm = pl.BlockSpec(memory_space=pl.ANY)
    res = pl.pallas_call(
        body, name=name,
        out_shape=[jax.ShapeDtypeStruct((R, C), F32)] + (rider.out_shapes if rider is not None else []),
        in_specs=[vm] + [hbm] * n_ri, out_specs=[vm] + [hbm] * n_ro,
        scratch_shapes=[pltpu.VMEM((4, R, C), BF16), pltpu.VMEM((3, R, C), BF16), pltpu.VMEM((3, R, C), BF16),
                        pltpu.SemaphoreType.DMA((4,)), pltpu.SemaphoreType.DMA((4,)),
                        pltpu.SemaphoreType.DMA((3,)), pltpu.SemaphoreType.DMA((3,))] + (rider.sems if rider is not None else []),
        compiler_params=pltpu.CompilerParams(vmem_limit_bytes=VMEM_LIMIT),
    )(g8, *(rider.inputs if rider is not None else []))
    return res[0], res[1:]


class _Task:
    def __init__(self, inputs, out_shapes, sems, start, finish, mid=None, mid_step=None, alias=None):
        self.inputs, self.out_shapes, self.sems = list(inputs), list(out_shapes), list(sems)
        self.start, self.finish, self.mid, self.mid_step = start, finish, mid, mid_step
        self.alias = alias


def _hosted_call(body, name, grid, in_specs, out_specs, out_shape, operands, tasks, scratch_shapes=()):
    n_in, n_out, n_scr = len(in_specs), len(out_specs), len(scratch_shapes)
    t_in = [len(t.inputs) for t in tasks]
    t_out = [len(t.out_shapes) for t in tasks]
    t_sem = [len(t.sems) for t in tasks]
    n_steps = int(np.prod(grid))

    def wrapped(*refs):
        ins, rest = refs[:n_in], refs[n_in:]
        task_ins, rest = rest[:sum(t_in)], rest[sum(t_in):]
        outs, rest = rest[:n_out], rest[n_out:]
        task_outs, rest = rest[:sum(t_out)], rest[sum(t_out):]
        scr, task_sems = rest[:n_scr], rest[n_scr:]
        step = pl.program_id(0)
        for ax in range(1, len(grid)):
            step = step * grid[ax] + pl.program_id(ax)
        parts = []
        oi = oo = os_ = 0
        for t, a, b, c in zip(tasks, t_in, t_out, t_sem):
            parts.append((t, task_ins[oi:oi + a], task_outs[oo:oo + b], task_sems[os_:os_ + c]))
            oi, oo, os_ = oi + a, oo + b, os_ + c
        for t, ti, to, ts in parts:
            pl.when(step == 0)(functools.partial(t.start, ti, to, ts))
            if t.mid is not None:
                pl.when(step == t.mid_step)(functools.partial(t.mid, ti, to, ts))
        body(*ins, *outs, *scr)
        for t, ti, to, ts in parts:
            pl.when(step == n_steps - 1)(functools.partial(t.finish, ti, to, ts))

    hbm = pl.BlockSpec(memory_space=pl.ANY)
    aliases, oi, oo = {}, n_in, n_out
    for t, a, b in zip(tasks, t_in, t_out):
        if t.alias is not None:
            aliases[oi + t.alias[0]] = oo + t.alias[1]
        oi, oo = oi + a, oo + b
    res = pl.pallas_call(
        wrapped, name=name, grid=grid,
        in_specs=list(in_specs) + [hbm] * sum(t_in),
        out_specs=list(out_specs) + [hbm] * sum(t_out),
        out_shape=list(out_shape) + [s for t in tasks for s in t.out_shapes],
        scratch_shapes=list(scratch_shapes) + [s for t in tasks for s in t.sems],
        input_output_aliases=aliases,
        compiler_params=_params(*(["arbitrary"] * len(grid))),
    )(*operands, *[a for t in tasks for a in t.inputs])
    own, extra = res[:n_out], res[n_out:]
    per_task, o = [], 0
    for b in t_out:
        per_task.append(extra[o:o + b])
        o += b
    return own, per_task


def _gather_task(shard, mid_step, rows=None, into=None):
    lo, n = (0, shard.shape[0]) if rows is None else rows

    def parts(ins, outs, sems):
        x_ref, out_ref, (send_sems, recv_sems, local_sem) = ins[0], outs[0], sems
        x_, y_, c_ = _mesh_pos()
        me, sibling = (x_, y_, c_), (x_, y_, 1 - c_)
        chips = [(1 - x_, y_), (x_, 1 - y_), (1 - x_, 1 - y_)]
        x_ref = x_ref.at[pl.ds(lo, n)]

        def rows(px, py, pc):
            return out_ref.at[4 * px + 2 * py + pc, pl.ds(lo, n)]

        def copy(k, block, to, src=None):
            return pltpu.make_async_remote_copy(
                src_ref=rows(*block) if src is None else src, dst_ref=rows(*block),
                send_sem=send_sems.at[k], recv_sem=recv_sems.at[k], device_id=to, device_id_type=MESH_ID)

        return dict(
            mine=lambda: pltpu.make_async_copy(x_ref, rows(*me), local_sem),
            first=lambda: [copy(0, me, sibling, src=x_ref)] + [copy(1 + j, me, (*chip, c_), src=x_ref) for j, chip in enumerate(chips)],
            passed=lambda: [copy(4 + j, (*chip, c_), sibling) for j, chip in enumerate(chips)],
            landed=lambda: [copy(1 + j, (*chip, c_), me) for j, chip in enumerate(chips)],
            last=lambda: [copy(0, sibling, me)] + [copy(4 + j, (*chip, 1 - c_), me) for j, chip in enumerate(chips)])

    def start(ins, outs, sems):
        p = parts(ins, outs, sems)
        p["mine"]().start()
        for cp in p["first"]():
            cp.start()

    def mid(ins, outs, sems):
        p = parts(ins, outs, sems)
        for cp, fw in zip(p["landed"](), p["passed"]()):
            cp.wait_recv()
            fw.start()

    def finish(ins, outs, sems):
        p = parts(ins, outs, sems)
        for cp in p["last"]():
            cp.wait_recv()
        for cp in p["first"]() + p["passed"]():
            cp.wait_send()
        p["mine"]().wait()

    return _Task([shard] if into is None else [shard, into], [jax.ShapeDtypeStruct((N_DEV,) + shard.shape, shard.dtype)],
                 [pltpu.SemaphoreType.DMA((7,)), pltpu.SemaphoreType.DMA((7,)), pltpu.SemaphoreType.DMA],
                 start, finish, mid, mid_step, alias=None if into is None else (1, 0))


def _swap_task(g8):
    _, R, C = g8.shape

    def copies(ins, outs, sems):
        (g_ref,), (recv_ref,), (ss, rs) = ins, outs, sems
        x_, y_, c_ = _mesh_pos()
        return [pltpu.make_async_remote_copy(src_ref=g_ref.at[2 * k + (1 - c_)], dst_ref=recv_ref.at[k], send_sem=ss.at[k],
                                             recv_sem=rs.at[k], device_id=(x_, y_, 1 - c_), device_id_type=MESH_ID)
                for k in range(4)]

    def start(ins, outs, sems):
        for cp in copies(ins, outs, sems):
            cp.start()

    def finish(ins, outs, sems):
        cps = copies(ins, outs, sems)
        for cp in cps:
            cp.wait_recv()
        for cp in cps:
            cp.wait_send()

    return _Task([g8], [jax.ShapeDtypeStruct((4, R, C), g8.dtype)],
                 [pltpu.SemaphoreType.DMA((4,)), pltpu.SemaphoreType.DMA((4,))], start, finish)


def _chip_sums(g8, recva):
    _, R, C = g8.shape
    rc = _row_chunk(R)

    def body(g_ref, a_ref, send_ref, own_ref):
        x_, y_, c_ = _mesh_pos()
        chips = [(1 - x_, y_), (x_, 1 - y_), (1 - x_, 1 - y_), (x_, y_)]

        def chunk(i, carry):
            rows = pl.ds(pl.multiple_of(i * rc, rc), rc)
            for j, (tx, ty) in enumerate(chips):
                k = 2 * tx + ty
                s = g_ref[2 * k + c_, rows, :].astype(F32) + a_ref[k, rows, :].astype(F32)
                if j < 3:
                    send_ref[j, rows, :] = s.astype(BF16)
                else:
                    own_ref[rows, :] = s
            return carry

        lax.fori_loop(0, R // rc, chunk, 0)

    return pl.pallas_call(body, name="chip_sums", out_shape=[jax.ShapeDtypeStruct((3, R, C), BF16), jax.ShapeDtypeStruct((R, C), F32)],
                          compiler_params=pltpu.CompilerParams(vmem_limit_bytes=VMEM_LIMIT))(g8, recva)


def _exchange_task(sendb):
    def copies(ins, outs, sems):
        (s_ref,), (recv_ref,), (ss, rs) = ins, outs, sems
        x_, y_, c_ = _mesh_pos()
        flips = [(1 - x_, y_), (x_, 1 - y_), (1 - x_, 1 - y_)]
        return [pltpu.make_async_remote_copy(src_ref=s_ref.at[j], dst_ref=recv_ref.at[j], send_sem=ss.at[j], recv_sem=rs.at[j],
                                             device_id=(tx, ty, c_), device_id_type=MESH_ID) for j, (tx, ty) in enumerate(flips)]

    def start(ins, outs, sems):
        for cp in copies(ins, outs, sems):
            cp.start()

    def finish(ins, outs, sems):
        cps = copies(ins, outs, sems)
        for cp in cps:
            cp.wait_recv()
        for cp in cps:
            cp.wait_send()

    return _Task([sendb], [jax.ShapeDtypeStruct(sendb.shape, sendb.dtype)],
                 [pltpu.SemaphoreType.DMA((3,)), pltpu.SemaphoreType.DMA((3,))], start, finish)


def _silu(v):
    return v * (1.0 / (1.0 + jnp.exp(-v)))


def _ada_fwd(c_slab, w_ada_l, b_ada_l, w_in_shard):
    W = c_slab.shape[1]
    D, cols = w_ada_l.shape
    t_c, t_w = _gather_task(c_slab, 0), _gather_task(w_in_shard, 0)
    t_m = _gather_task(jax.ShapeDtypeStruct((N_DEV * 8, cols), F32), 0)

    def body(c_ref, w_ref, b_ref, ws_ref, slabs_ref, mod_ref, win_ref, c_vm, m_vm, copy_sem, *sems):
        sc, sw, sm = sems[0:3], sems[3:6], sems[6:9]
        t_w.start((ws_ref,), (win_ref,), sw)
        for phase in (t_c.start, t_c.mid, t_c.finish):
            phase((c_ref,), (slabs_ref,), sc)
        cp = pltpu.make_async_copy(slabs_ref, c_vm, copy_sem)
        cp.start()
        cp.wait()
        c_all = c_vm[:, :, 0:D].reshape(N_DEV * 8, D)
        m_vm[...] = jnp.dot(_silu(c_all), w_ref[...], precision=lax.Precision.HIGHEST, preferred_element_type=F32) + b_ref[...]
        t_m.start((m_vm,), (mod_ref,), sm)
        t_w.mid((ws_ref,), (win_ref,), sw)
        t_m.mid((m_vm,), (mod_ref,), sm)
        t_m.finish((m_vm,), (mod_ref,), sm)
        t_w.finish((ws_ref,), (win_ref,), sw)

    hbm, vm = pl.BlockSpec(memory_space=pl.ANY), pl.BlockSpec(memory_space=pltpu.VMEM)
    return pl.pallas_call(
        body, name="ada_fwd", in_specs=[hbm, vm, vm, hbm], out_specs=[hbm, hbm, hbm],
        out_shape=t_c.out_shapes + t_m.out_shapes + t_w.out_shapes,
        scratch_shapes=[pltpu.VMEM((N_DEV, 8, W), F32), pltpu.VMEM((N_DEV * 8, cols), F32), pltpu.SemaphoreType.DMA]
        + t_c.sems + t_w.sems + t_m.sems,
        compiler_params=pltpu.CompilerParams(vmem_limit_bytes=VMEM_LIMIT),
    )(c_slab, w_ada_l, b_ada_l, w_in_shard)


def _ada_bwd(c_all, dmod_cols):
    def body(c_ref, d_ref, o_ref):
        o_ref[...] = lax.dot_general(_silu(c_ref[...]), d_ref[...], (((0,), (0,)), ((), ())),
                                     precision=lax.Precision.HIGHEST, preferred_element_type=F32)
    return pl.pallas_call(body, name="ada_bwd", out_shape=jax.ShapeDtypeStruct((c_all.shape[1], dmod_cols.shape[1]), F32),
                          compiler_params=pltpu.CompilerParams(vmem_limit_bytes=VMEM_LIMIT))(c_all, dmod_cols)


NA_PAIRS = NA_HEADS // 2
N_DR_PAD = 16


def _na_bias_table(na_rpb):
    rev = jnp.pad(jnp.flip(na_rpb, axis=2), ((0, 0), (0, N_DR_PAD - N_DR), (0, GRID_W - N_DC)))
    rev = jnp.transpose(rev.reshape(NA_PAIRS, 2, N_DR_PAD, GRID_W), (0, 2, 1, 3)).reshape(NA_PAIRS, N_DR_PAD, 128)

    def body(r_ref, o_ref):
        k = lax.broadcasted_iota(jnp.int32, (GRID_W, 128), 0)
        lane = lax.broadcasted_iota(jnp.int32, (GRID_W, 128), 1)
        q = lane % GRID_W
        cs = jnp.clip(q - NA_COLS // 2, 0, GRID_W - NA_COLS)
        ok = (k >= cs) & (k < cs + NA_COLS)
        left = lane < GRID_W
        for dr in range(N_DR):
            row = jnp.broadcast_to(r_ref[0, dr:dr + 1, :], (GRID_W, 128))
            r0 = jnp.where(left, row, 0.0)
            r1 = jnp.where(left, pltpu.roll(row, GRID_W, axis=1), 0.0)
            y0 = pltpu.roll(r0, 128 - (NA_COLS - 1), axis=1, stride=1, stride_axis=0)
            y1 = pltpu.roll(r1, GRID_W - (NA_COLS - 1), axis=1, stride=1, stride_axis=0)
            o_ref[0, dr * GRID_W:(dr + 1) * GRID_W, :] = jnp.where(ok, jnp.where(left, y0, y1), NEG)

    return pl.pallas_call(
        body, name="rpb_expand", grid=(NA_PAIRS,),
        in_specs=[pl.BlockSpec((1, N_DR_PAD, 128), lambda p: (p, 0, 0))],
        out_specs=pl.BlockSpec((1, N_DR * GRID_W, 128), lambda p: (p, 0, 0)),
        out_shape=jax.ShapeDtypeStruct((NA_PAIRS, N_DR * GRID_W, 128), F32),
        compiler_params=_params("parallel"),
    )(rev)


def _na_bias_grad(db):
    a = np.arange(128)
    flip = jnp.asarray(((a[:, None] // GRID_W == a[None, :] // GRID_W)
                        & (a[:, None] % GRID_W + a[None, :] % GRID_W == GRID_W - 1)).astype(np.float32))

    def body(d_ref, j_ref, o_ref):
        o_ref[...] = jnp.zeros_like(o_ref)
        for dr in range(N_DR):
            t = jnp.dot(d_ref[0, dr * GRID_W:(dr + 1) * GRID_W, :], j_ref[...], precision=lax.Precision.HIGHEST, preferred_element_type=F32)
            t = pltpu.roll(t, GRID_W + NA_COLS, axis=1, stride=1, stride_axis=0)
            o_ref[0, dr:dr + 1, :] = jnp.sum(t, axis=0, keepdims=True)

    rows = pl.pallas_call(
        body, name="rpb_reduce", grid=(NA_PAIRS,),
        in_specs=[pl.BlockSpec((1, N_DR * GRID_W, 128), lambda p: (p, 0, 0)), _full((128, 128))],
        out_specs=pl.BlockSpec((1, N_DR_PAD, 128), lambda p: (p, 0, 0)),
        out_shape=jax.ShapeDtypeStruct((NA_PAIRS, N_DR_PAD, 128), F32),
        compiler_params=_params("parallel"),
    )(db, flip)
    g = rows.reshape(NA_PAIRS, N_DR_PAD, 2, GRID_W)[:, :N_DR, :, :N_DC]
    return jnp.transpose(g, (0, 2, 1, 3)).reshape(-1)


def _rope_tables(S):
    half = HEAD_DIM // 2
    inv = np.float32(ROPE_THETA) ** (-np.arange(half, dtype=np.float32) / np.float32(half))
    ang = np.arange(S).astype(np.float32)[:, None] * inv[None, :]
    cos, sin = np.cos(ang).astype(np.float32), np.sin(ang).astype(np.float32)
    return jnp.asarray(np.tile(np.concatenate([cos, cos], axis=1), (1, 2))), jnp.asarray(np.tile(np.concatenate([-sin, sin], axis=1), (1, 2)))


def _rope_spec(tps):
    return pl.BlockSpec((TOKEN_TILE, 2 * HEAD_DIM), lambda i: (i % tps, 0))


def _rot_half(t):
    w = t.shape[1]
    lane = lax.broadcasted_iota(jnp.int32, t.shape, 1)
    return jnp.where((lane % HEAD_DIM) < HEAD_DIM // 2, pltpu.roll(t, w - HEAD_DIM // 2, axis=1),
                     pltpu.roll(t, HEAD_DIM // 2, axis=1))


def _tok_spec(w):
    return pl.BlockSpec((TOKEN_TILE, w), lambda i: (i, 0))


def _mod_spec(tps, d):
    return pl.BlockSpec((1, 6, d), lambda i: (i // tps, 0, 0))


def _bstat_spec(tps, w):
    return pl.BlockSpec((1, 8, w), lambda i: (i // tps, 0, 0))


def _attn_in(x2d, mod3, g_attn, w_in, cos_t, sin_t, S, tasks=()):
    T, D = x2d.shape
    tps = S // TOKEN_TILE

    def body(x_ref, mod_ref, g_ref, w_ref, cos_ref, sin_ref, h_ref, qkv_ref):
        xn, _ = _rms(x_ref[...])
        h = (xn * g_ref[...]) * (1.0 + mod_ref[0, 1:2, :]) + mod_ref[0, 0:1, :]
        hb = h.astype(BF16)
        h_ref[...] = hb
        proj = _nt(hb, w_ref[...])
        rb = proj[:, ROPE_LO:ROPE_LO + ROPE_WIDTH]
        reps = (1, ROPE_WIDTH // (2 * HEAD_DIM))
        rb = rb * jnp.tile(cos_ref[...], reps) + _rot_half(rb) * jnp.tile(sin_ref[...], reps)
        qkv_ref[:, 0:NA_WIDTH] = (proj[:, 0:NA_WIDTH] * Q_SCALE).astype(BF16)
        qkv_ref[:, NA_WIDTH:ROPE_LO] = proj[:, NA_WIDTH:ROPE_LO].astype(BF16)
        qkv_ref[:, ROPE_LO:ROPE_LO + SW_WIDTH] = (rb[:, 0:SW_WIDTH] * Q_SCALE).astype(BF16)
        qkv_ref[:, ROPE_LO + SW_WIDTH:ROPE_LO + ROPE_WIDTH] = rb[:, SW_WIDTH:].astype(BF16)
        qkv_ref[:, ROPE_LO + ROPE_WIDTH:] = proj[:, ROPE_LO + ROPE_WIDTH:].astype(BF16)

    return _hosted_call(
        body, "attn_in", (T // TOKEN_TILE,),
        [_tok_spec(D), _mod_spec(tps, D), _full((1, D)), _full(w_in.shape), _rope_spec(tps), _rope_spec(tps)],
        [_tok_spec(D), _tok_spec(IN_WIDTH)],
        [jax.ShapeDtypeStruct((T, D), BF16), jax.ShapeDtypeStruct((T, IN_WIDTH), BF16)],
        (x2d, mod3, g_attn, w_in, cos_t, sin_t), tasks)


def _attn_out(oa, ob, x2d, mod3, g_na, g_sw, w_out, S):
    T, D = x2d.shape
    tps = S // TOKEN_TILE

    def body(oa_ref, ob_ref, x_ref, mod_ref, gna_ref, gsw_ref, w_ref, mixin_ref, mix_ref, x1_ref):
        oan, _ = _rms(oa_ref[...])
        obn, _ = _rms(ob_ref[...])
        mixin = jnp.concatenate([oan * gna_ref[...], obn * gsw_ref[...]], axis=1).astype(BF16)
        mixin_ref[...] = mixin
        mix = _nn(mixin, w_ref[...])
        mix_ref[...] = mix
        x1_ref[...] = x_ref[...] + mod_ref[0, 2:3, :] * mix

    return pl.pallas_call(
        body, name="attn_out", grid=(T // TOKEN_TILE,),
        in_specs=[_tok_spec(NA_WIDTH), _tok_spec(SW_WIDTH), _tok_spec(D), _mod_spec(tps, D),
                  _full((1, NA_WIDTH)), _full((1, SW_WIDTH)), _full(w_out.shape)],
        out_specs=[_tok_spec(NA_WIDTH + SW_WIDTH), _tok_spec(D), _tok_spec(D)],
        out_shape=[jax.ShapeDtypeStruct((T, NA_WIDTH + SW_WIDTH), BF16), jax.ShapeDtypeStruct((T, D), F32),
                   jax.ShapeDtypeStruct((T, D), F32)],
        compiler_params=_params("parallel"),
    )(oa, ob, x2d, mod3, g_na, g_sw, w_out)


def _ffn_up(x1, mod3, g_ffn, w_up, S):
    T, D = x1.shape
    F = w_up.shape[0] // 2
    tps = S // TOKEN_TILE

    def body(x1_ref, mod_ref, g_ref, w_ref, h2_ref, val_ref, gt_ref):
        xn, _ = _rms(x1_ref[...])
        h2 = ((xn * g_ref[...]) * (1.0 + mod_ref[0, 4:5, :]) + mod_ref[0, 3:4, :]).astype(BF16)
        h2_ref[...] = h2
        u = _nt(h2, w_ref[...])
        val_ref[...] = u[:, :F].astype(BF16)
        gt_ref[...] = u[:, F:].astype(BF16)

    return pl.pallas_call(
        body, name="ffn_up", grid=(T // TOKEN_TILE,),
        in_specs=[_tok_spec(D), _mod_spec(tps, D), _full((1, D)), _full(w_up.shape)],
        out_specs=[_tok_spec(D), _tok_spec(F), _tok_spec(F)],
        out_shape=[jax.ShapeDtypeStruct((T, D), BF16), jax.ShapeDtypeStruct((T, F), BF16), jax.ShapeDtypeStruct((T, F), BF16)],
        compiler_params=_params("parallel"),
    )(x1, mod3, g_ffn, w_up)


def _halo_specs(T, tps, w):
    per = TOKEN_TILE // 8
    prev = pl.BlockSpec((8, w), lambda i: (jnp.maximum(i * per - 1, 0), 0))
    nxt = pl.BlockSpec((8, w), lambda i: (jnp.minimum((i + 1) * per, T // 8 - 1), 0))
    return prev, nxt


def _seq_shifts(cur, before, after, ti, tps):
    tm = cur.shape[0]
    row = lax.broadcasted_iota(jnp.int32, cur.shape, 0)
    before = jnp.where(ti > 0, before.astype(F32), 0.0)
    after = jnp.where(ti < tps - 1, after.astype(F32), 0.0)
    return jnp.where(row == 0, before, pltpu.roll(cur, 1, axis=0)), jnp.where(row == tm - 1, after, pltpu.roll(cur, tm - 1, axis=0))


def _ffn_down(gt, val, conv_w, conv_b, w_down, x1, mod3, g_final, target, B, S):
    T, D = x1.shape
    F = gt.shape[1]
    tps = S // TOKEN_TILE
    prev, nxt = _halo_specs(T, tps, F)

    def body(gt_ref, prev_ref, next_ref, val_ref, cw_ref, cb_ref, w_ref, x1_ref, mod_ref, gf_ref, tgt_ref,
             a_ref, act_ref, vd_ref, dx2_ref, df_ref, gstat_ref, bstat_ref):
        i = pl.program_id(0)
        g = gt_ref[...].astype(F32)
        gprev, gnext = _seq_shifts(g, prev_ref[7:8, :], next_ref[0:1, :], i % tps, tps)
        gc = gprev * cw_ref[0:1, :] + g * cw_ref[1:2, :] + gnext * cw_ref[2:3, :] + cb_ref[...]
        sig = 1.0 / (1.0 + jnp.exp(-gc))
        act = gc * sig
        val = val_ref[...].astype(F32)
        act_ref[...] = act.astype(BF16)
        vd_ref[...] = (val * (sig + act - act * sig)).astype(BF16)
        a = (act * val).astype(BF16)
        a_ref[...] = a
        f = _nn(a, w_ref[...])
        gate = mod_ref[0, 5:6, :]
        x2 = x1_ref[...] + gate * f
        xn, r = _rms(x2)
        err = xn * gf_ref[...] - tgt_ref[...]
        dy = err * (1.0 / D)
        dx2 = _rms_bwd(xn, r, dy * gf_ref[...])
        dx2_ref[...] = dx2
        df_ref[...] = (gate * dx2).astype(BF16)

        @pl.when(i == 0)
        def _():
            gstat_ref[...] = jnp.zeros_like(gstat_ref)

        @pl.when(i % tps == 0)
        def _():
            bstat_ref[...] = jnp.zeros_like(bstat_ref)

        gstat_ref[0:1, :] += jnp.sum(dy * xn, axis=0, keepdims=True)
        tile_loss = jnp.sum(jnp.sum(err * err, axis=1, keepdims=True), axis=0, keepdims=True) * (0.5 / D)
        gstat_ref[1:2, :] += jnp.broadcast_to(tile_loss, (1, D))
        bstat_ref[0, 0:1, :] += jnp.sum(dx2 * f, axis=0, keepdims=True)

    return pl.pallas_call(
        body, name="ffn_down", grid=(T // TOKEN_TILE,),
        in_specs=[_tok_spec(F), prev, nxt, _tok_spec(F), _full(conv_w.shape), _full((1, F)), _full(w_down.shape),
                  _tok_spec(D), _mod_spec(tps, D), _full((1, D)), _tok_spec(D)],
        out_specs=[_tok_spec(F), _tok_spec(F), _tok_spec(F), _tok_spec(D), _tok_spec(D), _full((8, D)), _bstat_spec(tps, D)],
        out_shape=[jax.ShapeDtypeStruct((T, F), BF16), jax.ShapeDtypeStruct((T, F), BF16), jax.ShapeDtypeStruct((T, F), BF16),
                   jax.ShapeDtypeStruct((T, D), F32), jax.ShapeDtypeStruct((T, D), BF16),
                   jax.ShapeDtypeStruct((8, D), F32), jax.ShapeDtypeStruct((B, 8, D), F32)],
        compiler_params=_params("arbitrary"),
    )(gt, gt, gt, val, conv_w, conv_b, w_down, x1, mod3, g_final, target)


def _ffn_down_bwd(df, w_down, act, vd, tasks=()):
    T, D = df.shape
    F = act.shape[1]

    def body(df_ref, w_ref, act_ref, vd_ref, dval_ref, dgc_ref, cstat_ref):
        da = _nt(df_ref[...], w_ref[...])
        dval_ref[...] = (da * act_ref[...].astype(F32)).astype(BF16)
        dgc = da * vd_ref[...].astype(F32)
        dgc_ref[...] = dgc.astype(BF16)

        @pl.when(pl.program_id(0) == 0)
        def _():
            cstat_ref[...] = jnp.zeros_like(cstat_ref)

        cstat_ref[0:1, :] += jnp.sum(dgc, axis=0, keepdims=True)

    return _hosted_call(
        body, "ffn_down_bwd", (T // TOKEN_TILE,),
        [_tok_spec(D), _full(w_down.shape), _tok_spec(F), _tok_spec(F)],
        [_tok_spec(F), _tok_spec(F), _full((8, F))],
        [jax.ShapeDtypeStruct((T, F), BF16), jax.ShapeDtypeStruct((T, F), BF16), jax.ShapeDtypeStruct((8, F), F32)],
        (df, w_down, act, vd), tasks)


def _dw_up_conv(dgc, dval, gt, conv_w, h2, S, tk=512):
    T, F = dgc.shape
    D = h2.shape[1]
    nk, tps, per = T // tk, S // tk, tk // 8
    fc = F // 2 if F % 256 == 0 else F
    nc = F // fc
    gated = lambda i: i // nc
    prev = pl.BlockSpec((8, fc), lambda i, k: (jnp.maximum(k * per - 1, 0) * gated(i), i % nc))
    nxt = pl.BlockSpec((8, fc), lambda i, k: (jnp.minimum((k + 1) * per, T // 8 - 1) * gated(i), i % nc))
    gate_half = pl.BlockSpec((tk, fc), lambda i, k: (k * gated(i), i % nc))
    val_half = pl.BlockSpec((tk, fc), lambda i, k: (k * (1 - gated(i)), i % nc))

    def body(dgc_ref, prev_ref, next_ref, dval_ref, gt_ref, cw_ref, h2_ref, du_ref, dw_ref, cstat_ref, acc):
        i, k = pl.program_id(0), pl.program_id(1)

        @pl.when(k == 0)
        def _():
            acc[...] = jnp.zeros_like(acc)
            cstat_ref[...] = jnp.zeros_like(cstat_ref)

        @pl.when(i < nc)
        def _():
            dv = dval_ref[...]
            du_ref[...] = dv
            acc[...] += _tn(dv, h2_ref[...])

        @pl.when(i >= nc)
        def _():
            d = dgc_ref[...].astype(F32)
            dprev, dnext = _seq_shifts(d, prev_ref[7:8, :], next_ref[0:1, :], k % tps, tps)
            g = gt_ref[...].astype(F32)
            cstat_ref[1:2, :] += jnp.sum(dnext * g, axis=0, keepdims=True)
            cstat_ref[2:3, :] += jnp.sum(d * g, axis=0, keepdims=True)
            cstat_ref[3:4, :] += jnp.sum(dprev * g, axis=0, keepdims=True)
            dgt = (dnext * cw_ref[0:1, :] + d * cw_ref[1:2, :] + dprev * cw_ref[2:3, :]).astype(BF16)
            du_ref[...] = dgt
            acc[...] += _tn(dgt, h2_ref[...])

        @pl.when(k == nk - 1)
        def _():
            dw_ref[...] = acc[...].astype(BF16)

    return pl.pallas_call(
        body, name="dw_up", grid=(2 * nc, nk),
        in_specs=[gate_half, prev, nxt, val_half, gate_half, pl.BlockSpec((conv_w.shape[0], fc), lambda i, k: (0, i % nc)),
                  pl.BlockSpec((tk, D), lambda i, k: (k, 0))],
        out_specs=[pl.BlockSpec((tk, fc), lambda i, k: (k, i)), pl.BlockSpec((fc, D), lambda i, k: (i, 0)),
                   pl.BlockSpec((8, fc), lambda i, k: (gated(i), i % nc))],
        out_shape=[jax.ShapeDtypeStruct((T, 2 * F), BF16), jax.ShapeDtypeStruct((2 * F, D), BF16), jax.ShapeDtypeStruct((16, F), F32)],
        scratch_shapes=[pltpu.VMEM((fc, D), F32)],
        compiler_params=_params("arbitrary", "arbitrary"),
    )(dgc, dgc, dgc, dval, gt, conv_w, h2)


def _ffn_up_bwd(du, w_up, x1, mod3, g_ffn, dx2, mix, B, S, tasks=()):
    T, D = x1.shape
    F = du.shape[1] // 2
    tps = S // TOKEN_TILE

    def body(du_ref, w_ref, x1_ref, mod_ref, g_ref, dx2_ref, mix_ref, dx1_ref, dmix_ref, gstat_ref, bstat_ref):
        i = pl.program_id(0)
        dh2 = _nn(du_ref[...], w_ref[...])
        xn, r = _rms(x1_ref[...])
        scale1 = 1.0 + mod_ref[0, 4:5, :]
        xg = xn * g_ref[...]
        dx1 = dx2_ref[...] + _rms_bwd(xn, r, dh2 * g_ref[...] * scale1)
        dx1_ref[...] = dx1
        dmix_ref[...] = (mod_ref[0, 2:3, :] * dx1).astype(BF16)

        @pl.when(i == 0)
        def _():
            gstat_ref[...] = jnp.zeros_like(gstat_ref)

        @pl.when(i % tps == 0)
        def _():
            bstat_ref[...] = jnp.zeros_like(bstat_ref)

        gstat_ref[0:1, :] += jnp.sum(dh2 * scale1 * xn, axis=0, keepdims=True)
        bstat_ref[0, 0:1, :] += jnp.sum(dh2, axis=0, keepdims=True)
        bstat_ref[0, 1:2, :] += jnp.sum(dh2 * xg, axis=0, keepdims=True)
        bstat_ref[0, 2:3, :] += jnp.sum(dx1 * mix_ref[...], axis=0, keepdims=True)

    return _hosted_call(
        body, "ffn_up_bwd", (T // TOKEN_TILE,),
        [_tok_spec(2 * F), _full(w_up.shape), _tok_spec(D), _mod_spec(tps, D), _full((1, D)), _tok_spec(D), _tok_spec(D)],
        [_tok_spec(D), _tok_spec(D), _full((8, D)), _bstat_spec(tps, D)],
        [jax.ShapeDtypeStruct((T, D), F32), jax.ShapeDtypeStruct((T, D), BF16),
         jax.ShapeDtypeStruct((8, D), F32), jax.ShapeDtypeStruct((B, 8, D), F32)],
        (du, w_up, x1, mod3, g_ffn, dx2, mix), tasks)


def _attn_out_bwd(dmix, w_out, oa, ob, g_na, g_sw, tasks=()):
    T, D = dmix.shape

    def body(dmix_ref, w_ref, oa_ref, ob_ref, gna_ref, gsw_ref, doa_ref, dob_ref, gstat_ref):
        dmixin = _nt(dmix_ref[...], w_ref[...])

        @pl.when(pl.program_id(0) == 0)
        def _():
            gstat_ref[...] = jnp.zeros_like(gstat_ref)

        for k, (o_ref, g_ref, do_ref) in enumerate(((oa_ref, gna_ref, doa_ref), (ob_ref, gsw_ref, dob_ref))):
            dn = dmixin[:, k * NA_WIDTH:(k + 1) * NA_WIDTH]
            on, r = _rms(o_ref[...])
            gstat_ref[k:k + 1, :] += jnp.sum(dn * on, axis=0, keepdims=True)
            do_ref[...] = _rms_bwd(on, r, dn * g_ref[...]).astype(BF16)

    hs = jax.ShapeDtypeStruct((T, NA_WIDTH), BF16)
    return _hosted_call(
        body, "attn_out_bwd", (T // TOKEN_TILE,),
        [_tok_spec(D), _full(w_out.shape), _tok_spec(NA_WIDTH), _tok_spec(SW_WIDTH), _full((1, NA_WIDTH)), _full((1, SW_WIDTH))],
        [_tok_spec(NA_WIDTH), _tok_spec(SW_WIDTH), _full((8, NA_WIDTH))],
        [hs, hs, jax.ShapeDtypeStruct((8, NA_WIDTH), F32)],
        (dmix, w_out, oa, ob, g_na, g_sw), tasks)


def _attn_in_bwd(dqa, dka, dva, dqb, dkb, dvb, cos_t, sin_t, w_in, x2d, mod3, g_attn, dx1, B, S):
    T, D = x2d.shape
    tps = S // TOKEN_TILE

    def body(dqa_ref, dka_ref, dva_ref, dqb_ref, dkb_ref, dvb_ref, cos_ref, sin_ref, w_ref, x_ref, mod_ref, g_ref, dx1_ref,
             gx_ref, dproj_ref, gstat_ref, bstat_ref):
        i = pl.program_id(0)
        drb = jnp.concatenate([dqb_ref[...] * Q_SCALE, dkb_ref[...]], axis=1)
        reps = (1, ROPE_WIDTH // (2 * HEAD_DIM))
        drb = drb * jnp.tile(cos_ref[...], reps) + _rot_half(drb * jnp.tile(sin_ref[...], reps))
        dproj = jnp.concatenate([dqa_ref[...] * Q_SCALE, dka_ref[...], dva_ref[...], drb, dvb_ref[...]], axis=1).astype(BF16)
        dproj_ref[...] = dproj
        dh = _nn(dproj, w_ref[...])
        xn, r = _rms(x_ref[...])
        scale1 = 1.0 + mod_ref[0, 1:2, :]
        gx_ref[...] = dx1_ref[...] + _rms_bwd(xn, r, dh * g_ref[...] * scale1)

        @pl.when(i == 0)
        def _():
            gstat_ref[...] = jnp.zeros_like(gstat_ref)

        @pl.when(i % tps == 0)
        def _():
            bstat_ref[...] = jnp.zeros_like(bstat_ref)

        gstat_ref[0:1, :] += jnp.sum(dh * scale1 * xn, axis=0, keepdims=True)
        bstat_ref[0, 0:1, :] += jnp.sum(dh, axis=0, keepdims=True)
        bstat_ref[0, 1:2, :] += jnp.sum(dh * (xn * g_ref[...]), axis=0, keepdims=True)

    rope = _rope_spec(tps)
    return pl.pallas_call(
        body, name="attn_in_bwd", grid=(T // TOKEN_TILE,),
        in_specs=[_tok_spec(NA_WIDTH), _tok_spec(NA_WIDTH), _tok_spec(NA_WIDTH), _tok_spec(SW_WIDTH), _tok_spec(SW_KV_WIDTH),
                  _tok_spec(SW_KV_WIDTH), rope, rope, _full(w_in.shape), _tok_spec(D), _mod_spec(tps, D), _full((1, D)), _tok_spec(D)],
        out_specs=[_tok_spec(D), _tok_spec(IN_WIDTH), _full((8, D)), _bstat_spec(tps, D)],
        out_shape=[jax.ShapeDtypeStruct((T, D), F32), jax.ShapeDtypeStruct((T, IN_WIDTH), BF16),
                   jax.ShapeDtypeStruct((8, D), F32), jax.ShapeDtypeStruct((B, 8, D), F32)],
        compiler_params=_params("arbitrary"),
    )(dqa, dka, dva, dqb, dkb, dvb, cos_t, sin_t, w_in, x2d, mod3, g_attn, dx1)


def _matmul_tn(a, b, name, tm=None, tk=512):
    T, M = a.shape
    N = b.shape[1]
    tm = M if tm is None else tm
    nk = T // tk

    def body(a_ref, b_ref, o_ref, acc):
        k = pl.program_id(1)

        @pl.when(k == 0)
        def _():
            acc[...] = jnp.zeros_like(acc)

        acc[...] += _tn(a_ref[...], b_ref[...])

        @pl.when(k == nk - 1)
        def _():
            o_ref[...] = acc[...].astype(BF16)

    return pl.pallas_call(
        body, name=name, grid=(M // tm, nk),
        in_specs=[pl.BlockSpec((tk, tm), lambda i, k: (k, i)), pl.BlockSpec((tk, N), lambda i, k: (k, 0))],
        out_specs=pl.BlockSpec((tm, N), lambda i, k: (i, 0)),
        out_shape=jax.ShapeDtypeStruct((M, N), BF16),
        scratch_shapes=[pltpu.VMEM((tm, N), F32)],
        compiler_params=_params("parallel", "arbitrary"),
    )(a, b)


def _na_geometry(S):
    rows = S // GRID_W
    wr = min(NA_ROWS_MAX, rows)
    return rows, wr


def _na_window(r, rows, wr):
    rs = jnp.clip(r - wr // 2, 0, rows - wr)
    return pl.multiple_of(rs * GRID_W, GRID_W), pl.multiple_of((rs - r + NA_ROWS_MAX - 1) * GRID_W, GRID_W)


NA_STEP_PAIRS = 2
NA_GW = NA_STEP_PAIRS * 128
NA_BWD_ROWS = 2
NA_ROWS_PER_STEP = 4


def _na_specs(S, kw_n, order):
    ng = NA_PAIRS // NA_STEP_PAIRS

    def col(k):
        return pl.BlockSpec((1, S, NA_GW), lambda *ids: (order(*ids)[0], 0, k * ng + order(*ids)[1]))
    bias = pl.BlockSpec((NA_STEP_PAIRS, N_DR * GRID_W, 128), lambda *ids: (order(*ids)[1], 0, 0))
    out = pl.BlockSpec((1, S, NA_GW), lambda *ids: (order(*ids)[0], 0, order(*ids)[1]))
    return col(0), col(1), col(2), bias, out


def _block_diag(t):
    left = lax.broadcasted_iota(jnp.int32, t.shape, 1) < HEAD_DIM
    zero = jnp.zeros_like(t)
    return jnp.concatenate([jnp.where(left, t, zero), jnp.where(left, zero, t)], axis=0)


def _diag_blocks(res):
    left = lax.broadcasted_iota(jnp.int32, (HEAD_DIM, 128), 1) < HEAD_DIM
    return jnp.where(left, res[:HEAD_DIM], res[HEAD_DIM:])


def _col_softmax(st):
    e = jnp.exp(st - jnp.max(st, axis=0, keepdims=True))
    return e * (1.0 / jnp.sum(e, axis=0, keepdims=True))


def _na_fwd(qkv, bias, tasks=()):
    B, S, _ = qkv.shape
    rows, wr = _na_geometry(S)
    kw_n = wr * GRID_W

    def body(q_ref, k_ref, v_ref, b_ref, o_ref):
        def step(it, carry):
            win = [_na_window(it * NA_ROWS_PER_STEP + u, rows, wr) for u in range(NA_ROWS_PER_STEP)]
            qrows = [pl.ds(pl.multiple_of((it * NA_ROWS_PER_STEP + u) * GRID_W, GRID_W), GRID_W) for u in range(NA_ROWS_PER_STEP)]
            krows = [pl.ds(w[0], kw_n) for w in win]
            brows = [pl.ds(w[1], kw_n) for w in win]
            lanes = [pl.ds(p * 128, 128) for p in range(NA_STEP_PAIRS)]
            chains = [(u, p) for u in range(NA_ROWS_PER_STEP) for p in range(NA_STEP_PAIRS)]
            st = {(u, p): _nt(k_ref[0, krows[u], lanes[p]], _block_diag(q_ref[0, qrows[u], lanes[p]])) for u, p in chains}
            pn = {(u, p): _col_softmax(st[(u, p)] + b_ref[p, brows[u], :]).astype(BF16) for u, p in chains}
            out = {(u, p): _diag_blocks(_tn(pn[(u, p)], v_ref[0, krows[u], lanes[p]])) for u, p in chains}
            for u in range(NA_ROWS_PER_STEP):
                o_ref[0, qrows[u], :] = jnp.concatenate([out[(u, p)] for p in range(NA_STEP_PAIRS)], axis=1)
            return carry

        lax.fori_loop(0, rows // NA_ROWS_PER_STEP, step, 0)

    q, k, v, bs, out = _na_specs(S, kw_n, lambda b, g: (b, g))
    return _hosted_call(body, "na_fwd", (B, NA_PAIRS // NA_STEP_PAIRS), [q, k, v, bs], [out],
                        [jax.ShapeDtypeStruct((B, S, NA_WIDTH), F32)], (qkv, qkv, qkv, bias), tasks)


def _na_bwd(qkv, bias, doa, tasks=()):
    B, S, _ = qkv.shape
    rows, wr = _na_geometry(S)
    kw_n = wr * GRID_W

    def body(q_ref, k_ref, v_ref, b_ref, do_ref, dq_ref, dk_ref, dv_ref, db_ref):
        @pl.when(pl.program_id(1) == 0)
        def _():
            db_ref[...] = jnp.zeros_like(db_ref)

        dk_ref[...] = jnp.zeros_like(dk_ref)
        dv_ref[...] = jnp.zeros_like(dv_ref)

        def step(it, carry):
            nu, pairs = range(NA_BWD_ROWS), range(NA_STEP_PAIRS)
            win = [_na_window(it * NA_BWD_ROWS + u, rows, wr) for u in nu]
            qrows = [pl.ds(pl.multiple_of((it * NA_BWD_ROWS + u) * GRID_W, GRID_W), GRID_W) for u in nu]
            krows = [pl.ds(w[0], kw_n) for w in win]
            brows = [pl.ds(w[1], kw_n) for w in win]
            lanes = [pl.ds(p * 128, 128) for p in pairs]
            chains = [(u, p) for u in nu for p in pairs]
            kp = {(u, p): k_ref[0, krows[u], lanes[p]] for u, p in chains}
            qbd = {(u, p): _block_diag(q_ref[0, qrows[u], lanes[p]]) for u, p in chains}
            dobd = {(u, p): _block_diag(do_ref[0, qrows[u], lanes[p]]) for u, p in chains}
            st = {c: _nt(kp[c], qbd[c]) for c in chains}
            dpt = {(u, p): _nt(v_ref[0, krows[u], lanes[p]], dobd[(u, p)]) for u, p in chains}
            pn = {(u, p): _col_softmax(st[(u, p)] + b_ref[p, brows[u], :]) for u, p in chains}
            dst = {c: pn[c] * (dpt[c] - jnp.sum(pn[c] * dpt[c], axis=0, keepdims=True)) for c in chains}
            dsb = {c: dst[c].astype(BF16) for c in chains}
            dq = {c: _diag_blocks(_tn(dsb[c], kp[c])) for c in chains}
            dk = {c: _nn(dsb[c], qbd[c]) for c in chains}
            dv = {c: _nn(pn[c].astype(BF16), dobd[c]) for c in chains}
            for u in nu:
                dq_ref[0, qrows[u], :] = jnp.concatenate([dq[(u, p)] for p in pairs], axis=1)
                dk_ref[0, krows[u], :] += jnp.concatenate([dk[(u, p)] for p in pairs], axis=1)
                dv_ref[0, krows[u], :] += jnp.concatenate([dv[(u, p)] for p in pairs], axis=1)
                for p in pairs:
                    db_ref[p, brows[u], :] += dst[(u, p)]
            return carry

        lax.fori_loop(0, rows // NA_BWD_ROWS, step, 0)

    q, k, v, bs, out = _na_specs(S, kw_n, lambda g, b: (b, g))
    hs = jax.ShapeDtypeStruct((B, S, NA_WIDTH), F32)
    return _hosted_call(body, "na_bwd", (NA_PAIRS // NA_STEP_PAIRS, B), [q, k, v, bs, out], [out, out, out, bs],
                        [hs, hs, hs, jax.ShapeDtypeStruct((NA_PAIRS, N_DR * GRID_W, 128), F32)], (qkv, qkv, qkv, bias, doa), tasks)


SW_PAIRS = SW_HEADS // 2


def _sw_band(n, S):
    kw_n = 3 * SW_BLOCK
    start = pl.multiple_of(jnp.clip(n * SW_BLOCK - SW_BLOCK, 0, S - kw_n), SW_BLOCK)
    kpos = start + lax.broadcasted_iota(jnp.int32, (kw_n, SW_BLOCK), 0)
    qpos = n * SW_BLOCK + lax.broadcasted_iota(jnp.int32, (kw_n, SW_BLOCK), 1)
    return start, jnp.abs(qpos - kpos) <= SW_WINDOW


def _kv_halves(t):
    left = lax.broadcasted_iota(jnp.int32, t.shape, 1) < HEAD_DIM
    swapped = pltpu.roll(t, HEAD_DIM, axis=1)
    zero = jnp.zeros_like(t)
    return {(0, 0): jnp.where(left, t, zero), (0, 1): jnp.where(left, zero, swapped),
            (1, 0): jnp.where(left, swapped, zero), (1, 1): jnp.where(left, zero, t)}


def _sw_probs(st, ok, sk):
    st = jnp.where(ok, st, NEG)
    m = jnp.maximum(jnp.max(st, axis=0, keepdims=True), sk)
    e = jnp.exp(st - m)
    esk = jnp.exp(sk - m)
    inv = 1.0 / (jnp.sum(e, axis=0, keepdims=True) + esk)
    return e * inv, esk * inv


def _sw_specs(S):
    q = pl.BlockSpec((1, S, SW_WIDTH), lambda b, *_: (b, 0, ROPE_LO // SW_WIDTH))
    k = pl.BlockSpec((1, S, SW_KV_WIDTH), lambda b, *_: (b, 0, (ROPE_LO + SW_WIDTH) // SW_KV_WIDTH))
    v = pl.BlockSpec((1, S, SW_KV_WIDTH), lambda b, *_: (b, 0, (ROPE_LO + ROPE_WIDTH) // SW_KV_WIDTH))
    return q, k, v


SW_FWD_SPLIT = 2


def _sw_fwd(sink, qkv, tasks=()):
    B, S, _ = qkv.shape
    kw_n = 3 * SW_BLOCK

    def body(sink_ref, q_ref, k_ref, v_ref, o_ref):
        def step(n, carry):
            start, ok = _sw_band(n, S)
            qrows = pl.ds(pl.multiple_of(n * SW_BLOCK, SW_BLOCK), SW_BLOCK)
            krows = pl.ds(start, kw_n)
            kh, vh = _kv_halves(k_ref[0, krows, :]), _kv_halves(v_ref[0, krows, :])
            heads = [(p, e) for p in range(SW_PAIRS) for e in range(2)]
            qp = [q_ref[0, qrows, pl.ds(p * 128, 128)] for p in range(SW_PAIRS)]
            kv_of = lambda p: p // (SW_PAIRS // SW_KV_HEADS)
            st = {(p, e): _nt(kh[(kv_of(p), e)], qp[p]) for p, e in heads}
            pn = {(p, e): _sw_probs(st[(p, e)], ok, sink_ref[2 * p + e])[0].astype(BF16) for p, e in heads}
            outs = [_tn(pn[(p, 0)], vh[(kv_of(p), 0)]) + _tn(pn[(p, 1)], vh[(kv_of(p), 1)]) for p in range(SW_PAIRS)]
            o_ref[0, qrows, :] = jnp.concatenate(outs, axis=1)
            return carry

        half = (S // SW_BLOCK) // SW_FWD_SPLIT
        lax.fori_loop(pl.program_id(1) * half, (pl.program_id(1) + 1) * half, step, 0)

    q, k, v = _sw_specs(S)
    return _hosted_call(
        body, "sw_fwd", (B, SW_FWD_SPLIT), [pl.BlockSpec(memory_space=pltpu.SMEM), q, k, v],
        [pl.BlockSpec((1, S, SW_WIDTH), lambda b, s: (b, 0, 0))], [jax.ShapeDtypeStruct((B, S, SW_WIDTH), F32)],
        (sink, qkv, qkv, qkv), tasks)


def _sw_bwd(sink, qkv, dob):
    B, S, _ = qkv.shape
    kw_n = 3 * SW_BLOCK

    fold_rows = 256

    def body(sink_ref, q_ref, k_ref, v_ref, do_ref, dq_ref, dk_ref, dv_ref, dsink_ref, dk_acc, dv_acc):
        @pl.when(pl.program_id(0) == 0)
        def _():
            dsink_ref[...] = jnp.zeros_like(dsink_ref)

        dk_acc[...] = jnp.zeros_like(dk_acc)
        dv_acc[...] = jnp.zeros_like(dv_acc)
        ppk = SW_PAIRS // SW_KV_HEADS

        def step(n, carry):
            start, ok = _sw_band(n, S)
            qrows = pl.ds(pl.multiple_of(n * SW_BLOCK, SW_BLOCK), SW_BLOCK)
            krows = pl.ds(start, kw_n)
            kh, vh = _kv_halves(k_ref[0, krows, :]), _kv_halves(v_ref[0, krows, :])
            heads = [(p, e) for p in range(SW_PAIRS) for e in range(2)]
            qp = [q_ref[0, qrows, pl.ds(p * 128, 128)] for p in range(SW_PAIRS)]
            dop = [do_ref[0, qrows, pl.ds(p * 128, 128)] for p in range(SW_PAIRS)]
            st = {(p, e): _nt(kh[(p // ppk, e)], qp[p]) for p, e in heads}
            dpt = {(p, e): _nt(vh[(p // ppk, e)], dop[p]) for p, e in heads}
            pnb, dsb = {}, {}
            for p, e in heads:
                pn, psink = _sw_probs(st[(p, e)], ok, sink_ref[2 * p + e])
                delta = jnp.sum(pn * dpt[(p, e)], axis=0, keepdims=True)
                dsb[(p, e)] = (pn * (dpt[(p, e)] - delta)).astype(BF16)
                pnb[(p, e)] = pn.astype(BF16)
                dsink_ref[2 * p + e:2 * p + e + 1, :] += -(psink * delta)
            dq_ref[0, qrows, :] = jnp.concatenate(
                [_tn(dsb[(p, 0)], kh[(p // ppk, 0)]) + _tn(dsb[(p, 1)], kh[(p // ppk, 1)]) for p in range(SW_PAIRS)], axis=1)
            left = lax.broadcasted_iota(jnp.int32, (kw_n, 128), 1) < HEAD_DIM
            dks, dvs = [], []
            for kv in range(SW_KV_HEADS):
                dk = dv = None
                for p in range(kv * ppk, (kv + 1) * ppk):
                    dk_p = jnp.where(left, _nn(dsb[(p, 0)], qp[p]), _nn(dsb[(p, 1)], qp[p]))
                    dv_p = jnp.where(left, _nn(pnb[(p, 0)], dop[p]), _nn(pnb[(p, 1)], dop[p]))
                    dk = dk_p if dk is None else dk + dk_p
                    dv = dv_p if dv is None else dv + dv_p
                dks.append(dk)
                dvs.append(dv)
            dk_acc[krows, :] += jnp.concatenate(dks, axis=1)
            dv_acc[krows, :] += jnp.concatenate(dvs, axis=1)
            return carry

        lax.fori_loop(0, S // SW_BLOCK, step, 0)

        def fold(i, carry):
            rows = pl.ds(pl.multiple_of(i * fold_rows, fold_rows), fold_rows)
            left = lax.broadcasted_iota(jnp.int32, (fold_rows, 128), 1) < HEAD_DIM
            for acc, out_ref in ((dk_acc, dk_ref), (dv_acc, dv_ref)):
                a, b = acc[rows, 0:128], acc[rows, 128:256]
                out_ref[0, rows, :] = jnp.where(left, a + pltpu.roll(a, HEAD_DIM, axis=1), b + pltpu.roll(b, HEAD_DIM, axis=1))
            return carry

        lax.fori_loop(0, S // fold_rows, fold, 0)

        @pl.when(pl.program_id(0) == B - 1)
        def _():
            dsink_ref[...] = jnp.broadcast_to(jnp.sum(dsink_ref[...], axis=1, keepdims=True), dsink_ref.shape)

    q, k, v = _sw_specs(S)
    qo = pl.BlockSpec((1, S, SW_WIDTH), lambda b: (b, 0, 0))
    ko = pl.BlockSpec((1, S, SW_KV_WIDTH), lambda b: (b, 0, 0))
    return pl.pallas_call(
        body, name="sw_bwd", grid=(B,),
        in_specs=[pl.BlockSpec(memory_space=pltpu.SMEM), q, k, v, qo],
        out_specs=[qo, ko, ko, _full((SW_HEADS, 128))],
        out_shape=[jax.ShapeDtypeStruct((B, S, SW_WIDTH), F32), jax.ShapeDtypeStruct((B, S, SW_KV_WIDTH), F32),
                   jax.ShapeDtypeStruct((B, S, SW_KV_WIDTH), F32), jax.ShapeDtypeStruct((SW_HEADS, 128), F32)],
        scratch_shapes=[pltpu.VMEM((S, 2 * SW_KV_WIDTH), F32), pltpu.VMEM((S, 2 * SW_KV_WIDTH), F32)],
        compiler_params=_params("arbitrary"),
    )(sink, qkv, qkv, qkv, dob)


def _pack_sum(packs):
    W = packs.shape[1]

    def body(p_ref, o_ref, s_ref):
        tot = p_ref[0:8, :]
        for d in range(1, N_DEV):
            tot = tot + p_ref[8 * d:8 * d + 8, :]
        o_ref[...] = tot
        s_ref[...] = tot[0:1, :] + tot[1:2, :]

    return pl.pallas_call(body, name="pack_sum", out_shape=[jax.ShapeDtypeStruct((8, W), F32), jax.ShapeDtypeStruct((1, W), F32)],
                          compiler_params=pltpu.CompilerParams(vmem_limit_bytes=VMEM_LIMIT))(packs)


def _adam_update(w, g, m, v):
    c1 = 1.0 - ADAM_B1 ** ADAM_STEP
    c2 = 1.0 - ADAM_B2 ** ADAM_STEP
    nm = ADAM_B1 * m + (1.0 - ADAM_B1) * g
    nv = ADAM_B2 * v + (1.0 - ADAM_B2) * (g * g)
    return -ADAM_LR * ((nm / c1) / (jnp.sqrt(nv / c2) + ADAM_EPS) + ADAM_WD * w), nm, nv


def _adamw(w, g, m, v, name):
    def body(w_ref, g_ref, m_ref, v_ref, d_ref, nm_ref, nv_ref):
        d_ref[...], nm_ref[...], nv_ref[...] = _adam_update(w_ref[...], g_ref[...], m_ref[...], v_ref[...])

    s = jax.ShapeDtypeStruct(w.shape, F32)
    return pl.pallas_call(body, name=name, out_shape=[s, s, s],
                          compiler_params=pltpu.CompilerParams(vmem_limit_bytes=VMEM_LIMIT))(w, g, m, v)


def _sum_adamw(own, recvb, w, m, v, name):
    R, C = own.shape
    rc = _row_chunk(R)

    def body(own_ref, r_ref, w_ref, m_ref, v_ref, g_ref, d_ref, nm_ref, nv_ref):
        def chunk(i, carry):
            rows = pl.ds(pl.multiple_of(i * rc, rc), rc)
            g = own_ref[rows, :]
            for j in range(3):
                g = g + r_ref[j, rows, :].astype(F32)
            g_ref[rows, :] = g
            d_ref[rows, :], nm_ref[rows, :], nv_ref[rows, :] = _adam_update(w_ref[rows, :], g, m_ref[rows, :], v_ref[rows, :])
            return carry

        lax.fori_loop(0, R // rc, chunk, 0)

    s = jax.ShapeDtypeStruct((R, C), F32)
    return pl.pallas_call(body, name=name, out_shape=[s, s, s, s],
                          compiler_params=pltpu.CompilerParams(vmem_limit_bytes=VMEM_LIMIT))(own, recvb, w, m, v)


def _by_device(dw):
    return dw.reshape(N_DEV, dw.shape[0] // N_DEV, dw.shape[1])


def _local_step(x, mod, g_attn, w_in, bias, sw_sink, g_na_out, g_sw_out, w_out, g_ffn, w_up, conv_w, conv_b, w_down,
                g_final, target, sharded):
    B, S, D = x.shape
    T = B * S
    x2d = x.reshape(T, D)
    mod3 = mod.reshape(B, 6, D)
    cos_t, sin_t = _rope_tables(S)
    sink = sw_sink.reshape(SW_HEADS)
    n_tiles = T // TOKEN_TILE
    full = lambda g: g.reshape(N_DEV * g.shape[1], g.shape[2])

    if sharded:
        half = w_up.shape[0] // 2
        tasks = [_gather_task(w_out, n_tiles // 2), _gather_task(w_up, n_tiles - 1, rows=(0, half))]
    (h, qkv), got = _attn_in(x2d, mod3, g_attn, w_in, cos_t, sin_t, S, tasks if sharded else [])
    if sharded:
        w_out, w_up_part = full(got[0][0]), got[1][0]
    qkv3 = qkv.reshape(B, S, IN_WIDTH)
    na_steps = B * (NA_PAIRS // NA_STEP_PAIRS)
    (oa,), got = _na_fwd(qkv3, bias, [_gather_task(w_up, na_steps - 1, rows=(half, half), into=w_up_part)] if sharded else [])
    if sharded:
        w_up = full(got[0][0])
    oa = oa.reshape(T, NA_WIDTH)
    (ob,), got = _sw_fwd(sink, qkv3, [_gather_task(w_down, B * SW_FWD_SPLIT - 1)] if sharded else [])
    if sharded:
        w_down = full(got[0][0])
    ob = ob.reshape(T, SW_WIDTH)
    mixin, mix, x1 = _attn_out(oa, ob, x2d, mod3, g_na_out, g_sw_out, w_out, S)
    h2, val, gt = _ffn_up(x1, mod3, g_ffn, w_up, S)
    a, act, vd, dx2, df, gstat_f, bstat_f = _ffn_down(gt, val, conv_w, conv_b, w_down, x1, mod3, g_final, target.reshape(T, D), B, S)
    F = val.shape[1]

    dw_down = _matmul_tn(a, df, "dw_down")
    (dval, dgc, cstat), got = _ffn_down_bwd(df, w_down, act, vd, [_swap_task(_by_device(dw_down))] if sharded else [])
    if sharded:
        send_down, own_down = _chip_sums(_by_device(dw_down), got[0][0])
    du, dw_up, cstat_w = _dw_up_conv(dgc, dval, gt, conv_w, h2, S)
    (dx1, dmix, gstat_u, bstat_u), got = _ffn_up_bwd(du, w_up, x1, mod3, g_ffn, dx2, mix, B, S,
                                                     [_exchange_task(send_down), _swap_task(_by_device(dw_up))] if sharded else [])
    if sharded:
        dw_down = (own_down, got[0][0])
        send_up, own_up = _chip_sums(_by_device(dw_up), got[1][0])
    dw_out = _matmul_tn(mixin, dmix, "dw_out")
    (doa, dob, gstat_o), got = _attn_out_bwd(dmix, w_out, oa, ob, g_na_out, g_sw_out, [_swap_task(_by_device(dw_out))] if sharded else [])
    if sharded:
        send_out, own_out = _chip_sums(_by_device(dw_out), got[0][0])
    (dqa, dka, dva, dbt), got = _na_bwd(qkv3, bias, doa.reshape(B, S, NA_WIDTH),
                                        [_exchange_task(send_up), _exchange_task(send_out)] if sharded else [])
    if sharded:
        dw_up, dw_out = (own_up, got[0][0]), (own_out, got[1][0])
    dqb, dkb, dvb, dsink = _sw_bwd(sink, qkv3, dob.reshape(B, S, SW_WIDTH))
    r2 = lambda t: t.reshape(T, t.shape[-1])
    grad_x, dproj, gstat_i, bstat_i = _attn_in_bwd(r2(dqa), r2(dka), r2(dva), r2(dqb), r2(dkb), r2(dvb), cos_t, sin_t, w_in, x2d, mod3,
                                                   g_attn, dx1, B, S)
    dw_in = _matmul_tn(dproj, h, "dw_in")

    dmod = jnp.stack([bstat_i[:, 0], bstat_i[:, 1], bstat_u[:, 2], bstat_u[:, 0], bstat_u[:, 1], bstat_f[:, 0]], axis=1)
    small = dict(g_attn=gstat_i[0], g_ffn=gstat_u[0], g_final=gstat_f[0], loss=gstat_f[1, 0], g_na_out=gstat_o[0], g_sw_out=gstat_o[1],
                 sw_sink=dsink[:, 0], conv_b=cstat[0], conv_w=cstat_w[9:12], dbt=dbt)
    return grad_x.reshape(B, S, D), dict(w_in=dw_in, w_out=dw_out, w_up=dw_up, w_down=dw_down), dmod, small


def _pad_lanes(v, w):
    return jnp.pad(v, (0, w - v.shape[0]))


def kernel(x, c, w_ada, b_ada, g_attn, w_in, na_rpb, sw_sink, g_na_out, g_sw_out, w_out, g_ffn, w_up, conv_w, conv_b, w_down, g_final, loss_target, m_w_ada, m_b_ada, m_g_attn, m_w_in, m_na_rpb, m_sw_sink, m_g_na_out, m_g_sw_out, m_w_out, m_g_ffn, m_w_up, m_conv_w, m_conv_b, m_w_down, m_g_final, v_w_ada, v_b_ada, v_g_attn, v_w_in, v_na_rpb, v_sw_sink, v_g_na_out, v_g_sw_out, v_w_out, v_g_ffn, v_w_up, v_conv_w, v_conv_b, v_w_down, v_g_final):
    B, S, D = x.shape
    me = 4 * lax.axis_index("x") + 2 * lax.axis_index("y") + lax.axis_index("c")
    ada_c = w_ada.shape[2]
    F_l = conv_w.shape[2]

    cw_l = jnp.pad(conv_w[0], ((0, 8 - conv_w.shape[1]), (0, 0)))
    c_l = jnp.pad(c, ((0, 8 - B), (0, 0)))
    tr = {"w_in", "w_up"}
    w_in_t = jnp.transpose(w_in[0])
    shards = dict(w_out=w_out[0].astype(BF16), w_up=jnp.transpose(w_up[0]).astype(BF16), w_down=w_down[0].astype(BF16))

    b_ada_l = lax.dynamic_slice(b_ada, (0, me * ada_c), (1, ada_c))
    slabs, mod_all, w_in_all = _ada_fwd(jnp.concatenate([c_l, cw_l], axis=1), w_ada[0], b_ada_l, w_in_t.astype(BF16))
    c_all = slabs[:, :, :D].reshape(N_DEV * 8, D)
    conv_w_f = jnp.transpose(slabs[:, :3, D:], (1, 0, 2)).reshape(3, N_DEV * F_l)
    mod_mine = lax.dynamic_slice(mod_all, (0, me * 8, 0), (N_DEV, B, ada_c))
    mod = jnp.transpose(mod_mine, (1, 0, 2)).reshape(B, N_DEV * ada_c)
    w_in_f = w_in_all.reshape(N_DEV * w_in_t.shape[0], D)

    bias = _na_bias_table(na_rpb[0])

    grad_x, dw, dmod, small = _local_step(x, mod, g_attn, w_in_f, bias, sw_sink, g_na_out, g_sw_out, shards["w_out"], g_ffn,
                                          shards["w_up"], conv_w_f, conv_b, shards["w_down"], g_final.reshape(1, D), loss_target,
                                          sharded=True)
    drpb = _na_bias_grad(small["dbt"])

    row2 = jnp.concatenate([small["g_attn"], small["g_ffn"], small["g_final"], small["g_na_out"], small["g_sw_out"],
                            _pad_lanes(small["sw_sink"], 128), _pad_lanes(small["loss"].reshape(1), 128)])
    rows = [dmod.reshape(B, 6 * D)[0], dmod.reshape(B, 6 * D)[1], _pad_lanes(row2, PACK_W), _pad_lanes(small["conv_b"], PACK_W),
            _pad_lanes(drpb, PACK_W)] + [_pad_lanes(small["conv_w"][k], PACK_W) for k in range(3)]
    g_w_in, (packs,) = _reduce_scatter(_by_device(dw["w_in"]), "scatter_w_in", rider=_gather_task(jnp.stack(rows), 0))
    tot, g_b_ada = _pack_sum(packs.reshape(N_DEV * 8, PACK_W))

    o = 0
    rep = {}
    for nm, wd in (("g_attn", D), ("g_ffn", D), ("g_final", D), ("g_na_out", NA_WIDTH), ("g_sw_out", SW_WIDTH), ("sw_sink", 128), ("loss", 128)):
        rep[nm] = tot[2, o:o + wd]
        o += wd
    loss = rep["loss"][0]
    g_conv_b = tot[3:4, :conv_b.shape[1]]
    g_na_rpb = tot[4, :NA_HEADS * N_DR * N_DC].reshape(na_rpb.shape)
    g_conv_w = lax.dynamic_slice(tot[5:8], (0, me * F_l), (3, F_l)).reshape(conv_w.shape)

    dmod_cols = lax.dynamic_slice(packs.reshape(N_DEV * 8, PACK_W), (0, me * ada_c), (N_DEV * 8, ada_c))
    g_w_ada = _ada_bwd(c_all, dmod_cols)[None]

    grads = dict(
        w_ada=g_w_ada, b_ada=g_b_ada, g_attn=rep["g_attn"][None], w_in=g_w_in, na_rpb=g_na_rpb,
        sw_sink=rep["sw_sink"][None, :SW_HEADS], g_na_out=rep["g_na_out"][None], g_sw_out=rep["g_sw_out"][None],
        w_out=None, g_ffn=rep["g_ffn"][None], w_up=None, conv_w=g_conv_w, conv_b=g_conv_b, w_down=None, g_final=rep["g_final"])
    weights = dict(w_ada=w_ada, b_ada=b_ada, g_attn=g_attn, w_in=w_in, na_rpb=na_rpb, sw_sink=sw_sink, g_na_out=g_na_out,
                   g_sw_out=g_sw_out, w_out=w_out, g_ffn=g_ffn, w_up=w_up, conv_w=conv_w, conv_b=conv_b, w_down=w_down, g_final=g_final)
    ms = dict(w_ada=m_w_ada, b_ada=m_b_ada, g_attn=m_g_attn, w_in=m_w_in, na_rpb=m_na_rpb, sw_sink=m_sw_sink, g_na_out=m_g_na_out,
              g_sw_out=m_g_sw_out, w_out=m_w_out, g_ffn=m_g_ffn, w_up=m_w_up, conv_w=m_conv_w, conv_b=m_conv_b, w_down=m_w_down, g_final=m_g_final)
    vs = dict(w_ada=v_w_ada, b_ada=v_b_ada, g_attn=v_g_attn, w_in=v_w_in, na_rpb=v_na_rpb, sw_sink=v_sw_sink, g_na_out=v_g_na_out,
              g_sw_out=v_g_sw_out, w_out=v_w_out, g_ffn=v_g_ffn, w_up=v_w_up, conv_w=v_conv_w, conv_b=v_conv_b, w_down=v_w_down, g_final=v_g_final)
    names = list(weights)
    deltas, new_m, new_v = {}, {}, {}
    for nm in names:
        shp = weights[nm].shape
        if nm in tr:
            r = lambda t: jnp.transpose(t[0])
            back = lambda t: jnp.transpose(t)[None]
        else:
            two_d = (shp[-2], shp[-1]) if len(shp) >= 3 and nm != "na_rpb" else (1, int(np.prod(shp)))
            r = lambda t: t.reshape(two_d)
            back = lambda t: t.reshape(shp)
        if nm in ("w_out", "w_up", "w_down"):
            g2, d_, m_, v_ = _sum_adamw(*dw[nm], r(weights[nm]), r(ms[nm]), r(vs[nm]), "adamw_" + nm)
        else:
            g2 = grads[nm] if nm in tr else r(grads[nm])
            d_, m_, v_ = _adamw(r(weights[nm]), g2, r(ms[nm]), r(vs[nm]), "adamw_" + nm)
        deltas[nm], new_m[nm], new_v[nm], grads[nm] = back(d_), back(m_), back(v_), back(g2)
    return (loss, grad_x, *[grads[n] for n in names], *[deltas[n] for n in names], *[new_m[n] for n in names],
            *[new_v[n] for n in names])
```

```python
import functools

import numpy as np
import jax
import jax.numpy as jnp
from jax import lax
from jax.experimental import pallas as pl
from jax.experimental.pallas import tpu as pltpu

F32, BF16 = jnp.float32, jnp.bfloat16
MESH_ID = pl.DeviceIdType.MESH
N_DEV = 8

HEAD_DIM = 64
NA_HEADS = 8
SW_HEADS = 8
SW_KV_HEADS = 2
SW_GROUP = SW_HEADS // SW_KV_HEADS
NA_WIDTH = NA_HEADS * HEAD_DIM
SW_WIDTH = SW_HEADS * HEAD_DIM
SW_KV_WIDTH = SW_KV_HEADS * HEAD_DIM
ROPE_WIDTH = SW_WIDTH + SW_KV_WIDTH
IN_WIDTH = 3 * NA_WIDTH + SW_WIDTH + 2 * SW_KV_WIDTH
ROPE_LO = 3 * NA_WIDTH
GRID_W = 64
NA_ROWS_MAX = 8
NA_COLS = 16
N_DR = 2 * NA_ROWS_MAX - 1
N_DC = 2 * NA_COLS - 1
SW_WINDOW = 128
SW_BLOCK = 128
ROPE_THETA = 10000.0
EPS = 1e-6
NEG = -1e30
Q_SCALE = HEAD_DIM ** -0.5

ADAM_LR = 0.001
ADAM_B1 = 0.9
ADAM_B2 = 0.999
ADAM_EPS = 1e-08
ADAM_WD = 0.01
ADAM_STEP = 10

TOKEN_TILE = 256
VMEM_LIMIT = 56 * 1024 * 1024

PACK_W = 6144


def _nn(a, b):
    return jnp.dot(a, b, preferred_element_type=F32)


def _nt(a, b):
    return lax.dot_general(a, b, (((1,), (1,)), ((), ())), preferred_element_type=F32)


def _tn(a, b):
    return lax.dot_general(a, b, (((0,), (0,)), ((), ())), preferred_element_type=F32)


def _rms(x):
    r = lax.rsqrt(jnp.mean(x * x, axis=-1, keepdims=True) + EPS)
    return x * r, r


def _rms_bwd(xn, r, gy):
    return r * (gy - xn * jnp.mean(xn * gy, axis=-1, keepdims=True))


def _params(*sem):
    return pltpu.CompilerParams(dimension_semantics=sem, vmem_limit_bytes=VMEM_LIMIT)


def _full(shape):
    n = len(shape)
    return pl.BlockSpec(shape, lambda *_: (0,) * n)


def _mesh_pos():
    return lax.axis_index("x"), lax.axis_index("y"), lax.axis_index("c")


def _all_gather(arrays, name):
    tasks = [_gather_task(a, 0) for a in arrays]
    n = len(tasks)

    def body(*refs):
        ins, outs, sems = refs[:n], refs[n:2 * n], refs[2 * n:]
        parts = [((ins[i],), (outs[i],), sems[3 * i:3 * i + 3]) for i in range(n)]
        for phase in ("start", "mid", "finish"):
            for t, p in zip(tasks, parts):
                getattr(t, phase)(*p)

    hbm = pl.BlockSpec(memory_space=pl.ANY)
    return pl.pallas_call(
        body, name=name, out_shape=[s for t in tasks for s in t.out_shapes], in_specs=[hbm] * n, out_specs=[hbm] * n,
        scratch_shapes=[s for t in tasks for s in t.sems],
    )(*arrays)


def _row_chunk(r):
    for rc in (128, 64, 32, 16):
        if r % rc == 0:
            return rc
    raise ValueError(f"rows {r} not a multiple of 16")


def _reduce_scatter(g8, name, rider=None):
    _, R, C = g8.shape
    rc = _row_chunk(R)
    n_ri, n_ro = (len(rider.inputs), len(rider.out_shapes)) if rider is not None else (0, 0)

    def body(g_ref, *refs):
        r_ins, refs = refs[:n_ri], refs[n_ri:]
        out_ref, refs = refs[0], refs[1:]
        r_outs, refs = refs[:n_ro], refs[n_ro:]
        recva, sendb, recvb, sa, ra, sb, rb = refs[:7]
        r_sems = refs[7:]
        x_, y_, c_ = _mesh_pos()
        sibling = (x_, y_, 1 - c_)
        if rider is not None:
            rider.start(r_ins, r_outs, r_sems)
        copies_a = []
        for k in range(4):
            cp = pltpu.make_async_remote_copy(
                src_ref=g_ref.at[2 * k + (1 - c_)], dst_ref=recva.at[k],
                send_sem=sa.at[k], recv_sem=ra.at[k], device_id=sibling, device_id_type=MESH_ID)
            cp.start()
            copies_a.append(cp)
        for cp in copies_a:
            cp.wait_recv()
        if rider is not None and rider.mid is not None:
            rider.mid(r_ins, r_outs, r_sems)

        def chip_sum(k, rows):
            return g_ref[2 * k + c_, rows, :].astype(F32) + recva[k, rows, :].astype(F32)

        flips = [(1 - x_, y_), (x_, 1 - y_), (1 - x_, 1 - y_)]
        copies_b = []
        for j, (tx, ty) in enumerate(flips):
            kt = 2 * tx + ty

            def fill(i, carry, j=j, kt=kt):
                rows = pl.ds(pl.multiple_of(i * rc, rc), rc)
                sendb[j, rows, :] = chip_sum(kt, rows).astype(BF16)
                return carry

            lax.fori_loop(0, R // rc, fill, 0)
            cp = pltpu.make_async_remote_copy(
                src_ref=sendb.at[j], dst_ref=recvb.at[j],
                send_sem=sb.at[j], recv_sem=rb.at[j], device_id=(tx, ty, c_), device_id_type=MESH_ID)
            cp.start()
            copies_b.append(cp)
        for cp in copies_b:
            cp.wait_recv()
        kme = 2 * x_ + y_

        def total(i, carry):
            rows = pl.ds(pl.multiple_of(i * rc, rc), rc)
            acc = chip_sum(kme, rows)
            for j in range(3):
                acc = acc + recvb[j, rows, :].astype(F32)
            out_ref[rows, :] = acc
            return carry

        lax.fori_loop(0, R // rc, total, 0)
        for cp in copies_a + copies_b:
            cp.wait_send()
        if rider is not None:
            rider.finish(r_ins, r_outs, r_sems)

    vm = pl.BlockSpec(memory_space=pltpu.VMEM)
    hbm = pl.BlockSpec(memory_space=pl.ANY)
    res = pl.pallas_call(
        body, name=name,
        out_shape=[jax.ShapeDtypeStruct((R, C), F32)] + (rider.out_shapes if rider is not None else []),
        in_specs=[vm] + [hbm] * n_ri, out_specs=[vm] + [hbm] * n_ro,
        scratch_shapes=[pltpu.VMEM((4, R, C), BF16), pltpu.VMEM((3, R, C), BF16), pltpu.VMEM((3, R, C), BF16),
                        pltpu.SemaphoreType.DMA((4,)), pltpu.SemaphoreType.DMA((4,)),
                        pltpu.SemaphoreType.DMA((3,)), pltpu.SemaphoreType.DMA((3,))] + (rider.sems if rider is not None else []),
        compiler_params=pltpu.CompilerParams(vmem_limit_bytes=VMEM_LIMIT),
    )(g8, *(rider.inputs if rider is not None else []))
    return res[0], res[1:]


class _Task:
    def __init__(self, inputs, out_shapes, sems, start, finish, mid=None, mid_step=None, alias=None):
        self.inputs, self.out_shapes, self.sems = list(inputs), list(out_shapes), list(sems)
        self.start, self.finish, self.mid, self.mid_step = start, finish, mid, mid_step
        self.alias = alias


def _hosted_call(body, name, grid, in_specs, out_specs, out_shape, operands, tasks, scratch_shapes=()):
    n_in, n_out, n_scr = len(in_specs), len(out_specs), len(scratch_shapes)
    t_in = [len(t.inputs) for t in tasks]
    t_out = [len(t.out_shapes) for t in tasks]
    t_sem = [len(t.sems) for t in tasks]
    n_steps = int(np.prod(grid))

    def wrapped(*refs):
        ins, rest = refs[:n_in], refs[n_in:]
        task_ins, rest = rest[:sum(t_in)], rest[sum(t_in):]
        outs, rest = rest[:n_out], rest[n_out:]
        task_outs, rest = rest[:sum(t_out)], rest[sum(t_out):]
        scr, task_sems = rest[:n_scr], rest[n_scr:]
        step = pl.program_id(0)
        for ax in range(1, len(grid)):
            step = step * grid[ax] + pl.program_id(ax)
        parts = []
        oi = oo = os_ = 0
        for t, a, b, c in zip(tasks, t_in, t_out, t_sem):
            parts.append((t, task_ins[oi:oi + a], task_outs[oo:oo + b], task_sems[os_:os_ + c]))
            oi, oo, os_ = oi + a, oo + b, os_ + c
        for t, ti, to, ts in parts:
            pl.when(step == 0)(functools.partial(t.start, ti, to, ts))
            if t.mid is not None:
                pl.when(step == t.mid_step)(functools.partial(t.mid, ti, to, ts))
        body(*ins, *outs, *scr)
        for t, ti, to, ts in parts:
            pl.when(step == n_steps - 1)(functools.partial(t.finish, ti, to, ts))

    hbm = pl.BlockSpec(memory_space=pl.ANY)
    aliases, oi, oo = {}, n_in, n_out
    for t, a, b in zip(tasks, t_in, t_out):
        if t.alias is not None:
            aliases[oi + t.alias[0]] = oo + t.alias[1]
        oi, oo = oi + a, oo + b
    res = pl.pallas_call(
        wrapped, name=name, grid=grid,
        in_specs=list(in_specs) + [hbm] * sum(t_in),
        out_specs=list(out_specs) + [hbm] * sum(t_out),
        out_shape=list(out_shape) + [s for t in tasks for s in t.out_shapes],
        scratch_shapes=list(scratch_shapes) + [s for t in tasks for s in t.sems],
        input_output_aliases=aliases,
        compiler_params=_params(*(["arbitrary"] * len(grid))),
    )(*operands, *[a for t in tasks for a in t.inputs])
    own, extra = res[:n_out], res[n_out:]
    per_task, o = [], 0
    for b in t_out:
        per_task.append(extra[o:o + b])
        o += b
    return own, per_task


def _gather_task(shard, mid_step, rows=None, into=None):
    lo, n = (0, shard.shape[0]) if rows is None else rows

    def parts(ins, outs, sems):
        x_ref, out_ref, (send_sems, recv_sems, local_sem) = ins[0], outs[0], sems
        x_, y_, c_ = _mesh_pos()
        me, sibling = (x_, y_, c_), (x_, y_, 1 - c_)
        chips = [(1 - x_, y_), (x_, 1 - y_), (1 - x_, 1 - y_)]
        x_ref = x_ref.at[pl.ds(lo, n)]

        def rows(px, py, pc):
            return out_ref.at[4 * px + 2 * py + pc, pl.ds(lo, n)]

        def copy(k, block, to, src=None):
            return pltpu.make_async_remote_copy(
                src_ref=rows(*block) if src is None else src, dst_ref=rows(*block),
                send_sem=send_sems.at[k], recv_sem=recv_sems.at[k], device_id=to, device_id_type=MESH_ID)

        return dict(
            mine=lambda: pltpu.make_async_copy(x_ref, rows(*me), local_sem),
            first=lambda: [copy(0, me, sibling, src=x_ref)] + [copy(1 + j, me, (*chip, c_), src=x_ref) for j, chip in enumerate(chips)],
            passed=lambda: [copy(4 + j, (*chip, c_), sibling) for j, chip in enumerate(chips)],
            landed=lambda: [copy(1 + j, (*chip, c_), me) for j, chip in enumerate(chips)],
            last=lambda: [copy(0, sibling, me)] + [copy(4 + j, (*chip, 1 - c_), me) for j, chip in enumerate(chips)])

    def start(ins, outs, sems):
        p = parts(ins, outs, sems)
        p["mine"]().start()
        for cp in p["first"]():
            cp.start()

    def mid(ins, outs, sems):
        p = parts(ins, outs, sems)
        for cp, fw in zip(p["landed"](), p["passed"]()):
            cp.wait_recv()
            fw.start()

    def finish(ins, outs, sems):
        p = parts(ins, outs, sems)
        for cp in p["last"]():
            cp.wait_recv()
        for cp in p["first"]() + p["passed"]():
            cp.wait_send()
        p["mine"]().wait()

    return _Task([shard] if into is None else [shard, into], [jax.ShapeDtypeStruct((N_DEV,) + shard.shape, shard.dtype)],
                 [pltpu.SemaphoreType.DMA((7,)), pltpu.SemaphoreType.DMA((7,)), pltpu.SemaphoreType.DMA],
                 start, finish, mid, mid_step, alias=None if into is None else (1, 0))


def _swap_task(g8):
    _, R, C = g8.shape

    def copies(ins, outs, sems):
        (g_ref,), (recv_ref,), (ss, rs) = ins, outs, sems
        x_, y_, c_ = _mesh_pos()
        return [pltpu.make_async_remote_copy(src_ref=g_ref.at[2 * k + (1 - c_)], dst_ref=recv_ref.at[k], send_sem=ss.at[k],
                                             recv_sem=rs.at[k], device_id=(x_, y_, 1 - c_), device_id_type=MESH_ID)
                for k in range(4)]

    def start(ins, outs, sems):
        for cp in copies(ins, outs, sems):
            cp.start()

    def finish(ins, outs, sems):
        cps = copies(ins, outs, sems)
        for cp in cps:
            cp.wait_recv()
        for cp in cps:
            cp.wait_send()

    return _Task([g8], [jax.ShapeDtypeStruct((4, R, C), g8.dtype)],
                 [pltpu.SemaphoreType.DMA((4,)), pltpu.SemaphoreType.DMA((4,))], start, finish)


def _chip_sums(g8, recva):
    _, R, C = g8.shape
    rc = _row_chunk(R)

    def body(g_ref, a_ref, send_ref, own_ref):
        x_, y_, c_ = _mesh_pos()
        chips = [(1 - x_, y_), (x_, 1 - y_), (1 - x_, 1 - y_), (x_, y_)]

        def chunk(i, carry):
            rows = pl.ds(pl.multiple_of(i * rc, rc), rc)
            for j, (tx, ty) in enumerate(chips):
                k = 2 * tx + ty
                s = g_ref[2 * k + c_, rows, :].astype(F32) + a_ref[k, rows, :].astype(F32)
                if j < 3:
                    send_ref[j, rows, :] = s.astype(BF16)
                else:
                    own_ref[rows, :] = s
            return carry

        lax.fori_loop(0, R // rc, chunk, 0)

    return pl.pallas_call(body, name="chip_sums", out_shape=[jax.ShapeDtypeStruct((3, R, C), BF16), jax.ShapeDtypeStruct((R, C), F32)],
                          compiler_params=pltpu.CompilerParams(vmem_limit_bytes=VMEM_LIMIT))(g8, recva)


def _exchange_task(sendb):
    def copies(ins, outs, sems):
        (s_ref,), (recv_ref,), (ss, rs) = ins, outs, sems
        x_, y_, c_ = _mesh_pos()
        flips = [(1 - x_, y_), (x_, 1 - y_), (1 - x_, 1 - y_)]
        return [pltpu.make_async_remote_copy(src_ref=s_ref.at[j], dst_ref=recv_ref.at[j], send_sem=ss.at[j], recv_sem=rs.at[j],
                                             device_id=(tx, ty, c_), device_id_type=MESH_ID) for j, (tx, ty) in enumerate(flips)]

    def start(ins, outs, sems):
        for cp in copies(ins, outs, sems):
            cp.start()

    def finish(ins, outs, sems):
        cps = copies(ins, outs, sems)
        for cp in cps:
            cp.wait_recv()
        for cp in cps:
            cp.wait_send()

    return _Task([sendb], [jax.ShapeDtypeStruct(sendb.shape, sendb.dtype)],
                 [pltpu.SemaphoreType.DMA((3,)), pltpu.SemaphoreType.DMA((3,))], start, finish)


def _silu(v):
    return v * (1.0 / (1.0 + jnp.exp(-v)))


def _ada_fwd(c_slab, w_ada_l, b_ada_l, w_in_shard, n_seq):
    W = c_slab.shape[1]
    D, cols = w_ada_l.shape
    n_rows = N_DEV * n_seq
    t_c, t_w = _gather_task(c_slab, 0), _gather_task(w_in_shard, 0)
    t_m = _gather_task(jax.ShapeDtypeStruct((n_rows, cols), F32), 0)

    def body(c_ref, w_ref, b_ref, ws_ref, slabs_ref, mod_ref, win_ref, c_vm, m_vm, copy_sem, *sems):
        sc, sw, sm = sems[0:3], sems[3:6], sems[6:9]
        t_c.start((c_ref,), (slabs_ref,), sc)
        t_w.start((ws_ref,), (win_ref,), sw)
        t_c.mid((c_ref,), (slabs_ref,), sc)
        t_c.finish((c_ref,), (slabs_ref,), sc)
        cp = pltpu.make_async_copy(slabs_ref, c_vm, copy_sem)
        cp.start()
        cp.wait()
        c_all = c_vm[:, :, 0:D].reshape(N_DEV * 8, D)
        m64 = jnp.dot(_silu(c_all), w_ref[...], precision=lax.Precision.HIGHEST, preferred_element_type=F32) + b_ref[...]
        r = lax.broadcasted_iota(jnp.int32, (n_rows, N_DEV * 8), 0)
        c = lax.broadcasted_iota(jnp.int32, (n_rows, N_DEV * 8), 1)
        pick = jnp.where(c == 8 * (r // n_seq) + r % n_seq, 1.0, 0.0)
        m_vm[...] = jnp.dot(pick, m64, precision=lax.Precision.HIGHEST, preferred_element_type=F32)
        t_m.start((m_vm,), (mod_ref,), sm)
        t_w.mid((ws_ref,), (win_ref,), sw)
        t_m.mid((m_vm,), (mod_ref,), sm)
        t_m.finish((m_vm,), (mod_ref,), sm)
        t_w.finish((ws_ref,), (win_ref,), sw)

    hbm, vm = pl.BlockSpec(memory_space=pl.ANY), pl.BlockSpec(memory_space=pltpu.VMEM)
    return pl.pallas_call(
        body, name="ada_fwd", in_specs=[hbm, vm, vm, hbm], out_specs=[hbm, hbm, hbm],
        out_shape=t_c.out_shapes + t_m.out_shapes + t_w.out_shapes,
        scratch_shapes=[pltpu.VMEM((N_DEV, 8, W), F32), pltpu.VMEM((n_rows, cols), F32), pltpu.SemaphoreType.DMA]
        + t_c.sems + t_w.sems + t_m.sems,
        compiler_params=pltpu.CompilerParams(vmem_limit_bytes=VMEM_LIMIT),
    )(c_slab, w_ada_l, b_ada_l, w_in_shard)


def _ada_bwd(c_all, dmod_cols):
    def body(c_ref, d_ref, o_ref):
        o_ref[...] = lax.dot_general(_silu(c_ref[...]), d_ref[...], (((0,), (0,)), ((), ())),
                                     precision=lax.Precision.HIGHEST, preferred_element_type=F32)
    return pl.pallas_call(body, name="ada_bwd", out_shape=jax.ShapeDtypeStruct((c_all.shape[1], dmod_cols.shape[1]), F32),
                          compiler_params=pltpu.CompilerParams(vmem_limit_bytes=VMEM_LIMIT))(c_all, dmod_cols)


NA_PAIRS = NA_HEADS // 2
N_DR_PAD = 16


def _na_bias_table(na_rpb):
    rev = jnp.pad(jnp.flip(na_rpb, axis=2), ((0, 0), (0, N_DR_PAD - N_DR), (0, GRID_W - N_DC)))
    rev = jnp.transpose(rev.reshape(NA_PAIRS, 2, N_DR_PAD, GRID_W), (0, 2, 1, 3)).reshape(NA_PAIRS, N_DR_PAD, 128)

    def body(r_ref, o_ref):
        k = lax.broadcasted_iota(jnp.int32, (GRID_W, 128), 0)
        lane = lax.broadcasted_iota(jnp.int32, (GRID_W, 128), 1)
        q = lane % GRID_W
        cs = jnp.clip(q - NA_COLS // 2, 0, GRID_W - NA_COLS)
        ok = (k >= cs) & (k < cs + NA_COLS)
        left = lane < GRID_W
        for dr in range(N_DR):
            row = jnp.broadcast_to(r_ref[0, dr:dr + 1, :], (GRID_W, 128))
            r0 = jnp.where(left, row, 0.0)
            r1 = jnp.where(left, pltpu.roll(row, GRID_W, axis=1), 0.0)
            y0 = pltpu.roll(r0, 128 - (NA_COLS - 1), axis=1, stride=1, stride_axis=0)
            y1 = pltpu.roll(r1, GRID_W - (NA_COLS - 1), axis=1, stride=1, stride_axis=0)
            o_ref[0, dr * GRID_W:(dr + 1) * GRID_W, :] = jnp.where(ok, jnp.where(left, y0, y1), NEG)

    return pl.pallas_call(
        body, name="rpb_expand", grid=(NA_PAIRS,),
        in_specs=[pl.BlockSpec((1, N_DR_PAD, 128), lambda p: (p, 0, 0))],
        out_specs=pl.BlockSpec((1, N_DR * GRID_W, 128), lambda p: (p, 0, 0)),
        out_shape=jax.ShapeDtypeStruct((NA_PAIRS, N_DR * GRID_W, 128), F32),
        compiler_params=_params("parallel"),
    )(rev)


def _na_bias_grad(db):
    a = np.arange(128)
    flip = jnp.asarray(((a[:, None] // GRID_W == a[None, :] // GRID_W)
                        & (a[:, None] % GRID_W + a[None, :] % GRID_W == GRID_W - 1)).astype(np.float32))

    def body(d_ref, j_ref, o_ref):
        o_ref[...] = jnp.zeros_like(o_ref)
        for dr in range(N_DR):
            t = jnp.dot(d_ref[0, dr * GRID_W:(dr + 1) * GRID_W, :], j_ref[...], precision=lax.Precision.HIGHEST, preferred_element_type=F32)
            t = pltpu.roll(t, GRID_W + NA_COLS, axis=1, stride=1, stride_axis=0)
            o_ref[0, dr:dr + 1, :] = jnp.sum(t, axis=0, keepdims=True)

    rows = pl.pallas_call(
        body, name="rpb_reduce", grid=(NA_PAIRS,),
        in_specs=[pl.BlockSpec((1, N_DR * GRID_W, 128), lambda p: (p, 0, 0)), _full((128, 128))],
        out_specs=pl.BlockSpec((1, N_DR_PAD, 128), lambda p: (p, 0, 0)),
        out_shape=jax.ShapeDtypeStruct((NA_PAIRS, N_DR_PAD, 128), F32),
        compiler_params=_params("parallel"),
    )(db, flip)
    g = rows.reshape(NA_PAIRS, N_DR_PAD, 2, GRID_W)[:, :N_DR, :, :N_DC]
    return jnp.transpose(g, (0, 2, 1, 3)).reshape(-1)


def _rope_tables(S):
    half = HEAD_DIM // 2
    inv = np.float32(ROPE_THETA) ** (-np.arange(half, dtype=np.float32) / np.float32(half))
    ang = np.arange(S).astype(np.float32)[:, None] * inv[None, :]
    cos, sin = np.cos(ang).astype(np.float32), np.sin(ang).astype(np.float32)
    return jnp.asarray(np.tile(np.concatenate([cos, cos], axis=1), (1, 2))), jnp.asarray(np.tile(np.concatenate([-sin, sin], axis=1), (1, 2)))


def _rope_spec(tps):
    return pl.BlockSpec((TOKEN_TILE, 2 * HEAD_DIM), lambda i: (i % tps, 0))


def _rot_half(t):
    w = t.shape[1]
    lane = lax.broadcasted_iota(jnp.int32, t.shape, 1)
    return jnp.where((lane % HEAD_DIM) < HEAD_DIM // 2, pltpu.roll(t, w - HEAD_DIM // 2, axis=1),
                     pltpu.roll(t, HEAD_DIM // 2, axis=1))


def _tok_spec(w):
    return pl.BlockSpec((TOKEN_TILE, w), lambda i: (i, 0))


def _mod_spec(tps, d):
    return pl.BlockSpec((1, 6, d), lambda i: (i // tps, 0, 0))


def _bstat_spec(tps, w):
    return pl.BlockSpec((1, 8, w), lambda i: (i // tps, 0, 0))


def _attn_in(x2d, mod3, g_attn, w_in, cos_t, sin_t, S, tasks=()):
    T, D = x2d.shape
    tps = S // TOKEN_TILE

    def body(x_ref, mod_ref, g_ref, w_ref, cos_ref, sin_ref, h_ref, qkv_ref):
        xn, _ = _rms(x_ref[...])
        h = (xn * g_ref[...]) * (1.0 + mod_ref[0, 1:2, :]) + mod_ref[0, 0:1, :]
        hb = h.astype(BF16)
        h_ref[...] = hb
        proj = _nt(hb, w_ref[...])
        rb = proj[:, ROPE_LO:ROPE_LO + ROPE_WIDTH]
        reps = (1, ROPE_WIDTH // (2 * HEAD_DIM))
        rb = rb * jnp.tile(cos_ref[...], reps) + _rot_half(rb) * jnp.tile(sin_ref[...], reps)
        qkv_ref[:, 0:NA_WIDTH] = (proj[:, 0:NA_WIDTH] * Q_SCALE).astype(BF16)
        qkv_ref[:, NA_WIDTH:ROPE_LO] = proj[:, NA_WIDTH:ROPE_LO].astype(BF16)
        qkv_ref[:, ROPE_LO:ROPE_LO + SW_WIDTH] = (rb[:, 0:SW_WIDTH] * Q_SCALE).astype(BF16)
        qkv_ref[:, ROPE_LO + SW_WIDTH:ROPE_LO + ROPE_WIDTH] = rb[:, SW_WIDTH:].astype(BF16)
        qkv_ref[:, ROPE_LO + ROPE_WIDTH:] = proj[:, ROPE_LO + ROPE_WIDTH:].astype(BF16)

    return _hosted_call(
        body, "attn_in", (T // TOKEN_TILE,),
        [_tok_spec(D), _mod_spec(tps, D), _full((1, D)), _full(w_in.shape), _rope_spec(tps), _rope_spec(tps)],
        [_tok_spec(D), _tok_spec(IN_WIDTH)],
        [jax.ShapeDtypeStruct((T, D), BF16), jax.ShapeDtypeStruct((T, IN_WIDTH), BF16)],
        (x2d, mod3, g_attn, w_in, cos_t, sin_t), tasks)


def _attn_out(oa, ob, x2d, mod3, g_na, g_sw, w_out, S):
    T, D = x2d.shape
    tps = S // TOKEN_TILE

    def body(oa_ref, ob_ref, x_ref, mod_ref, gna_ref, gsw_ref, w_ref, mixin_ref, mix_ref, x1_ref):
        oan, _ = _rms(oa_ref[...])
        obn, _ = _rms(ob_ref[...])
        mixin = jnp.concatenate([oan * gna_ref[...], obn * gsw_ref[...]], axis=1).astype(BF16)
        mixin_ref[...] = mixin
        mix = _nn(mixin, w_ref[...])
        mix_ref[...] = mix
        x1_ref[...] = x_ref[...] + mod_ref[0, 2:3, :] * mix

    return pl.pallas_call(
        body, name="attn_out", grid=(T // TOKEN_TILE,),
        in_specs=[_tok_spec(NA_WIDTH), _tok_spec(SW_WIDTH), _tok_spec(D), _mod_spec(tps, D),
                  _full((1, NA_WIDTH)), _full((1, SW_WIDTH)), _full(w_out.shape)],
        out_specs=[_tok_spec(NA_WIDTH + SW_WIDTH), _tok_spec(D), _tok_spec(D)],
        out_shape=[jax.ShapeDtypeStruct((T, NA_WIDTH + SW_WIDTH), BF16), jax.ShapeDtypeStruct((T, D), F32),
                   jax.ShapeDtypeStruct((T, D), F32)],
        compiler_params=_params("parallel"),
    )(oa, ob, x2d, mod3, g_na, g_sw, w_out)


def _ffn_up(x1, mod3, g_ffn, w_up, S):
    T, D = x1.shape
    F = w_up.shape[0] // 2
    tps = S // TOKEN_TILE

    def body(x1_ref, mod_ref, g_ref, w_ref, h2_ref, val_ref, gt_ref):
        xn, _ = _rms(x1_ref[...])
        h2 = ((xn * g_ref[...]) * (1.0 + mod_ref[0, 4:5, :]) + mod_ref[0, 3:4, :]).astype(BF16)
        h2_ref[...] = h2
        u = _nt(h2, w_ref[...])
        val_ref[...] = u[:, :F].astype(BF16)
        gt_ref[...] = u[:, F:].astype(BF16)

    return pl.pallas_call(
        body, name="ffn_up", grid=(T // TOKEN_TILE,),
        in_specs=[_tok_spec(D), _mod_spec(tps, D), _full((1, D)), _full(w_up.shape)],
        out_specs=[_tok_spec(D), _tok_spec(F), _tok_spec(F)],
        out_shape=[jax.ShapeDtypeStruct((T, D), BF16), jax.ShapeDtypeStruct((T, F), BF16), jax.ShapeDtypeStruct((T, F), BF16)],
        compiler_params=_params("parallel"),
    )(x1, mod3, g_ffn, w_up)


def _halo_specs(T, tps, w):
    per = TOKEN_TILE // 8
    prev = pl.BlockSpec((8, w), lambda i: (jnp.maximum(i * per - 1, 0), 0))
    nxt = pl.BlockSpec((8, w), lambda i: (jnp.minimum((i + 1) * per, T // 8 - 1), 0))
    return prev, nxt


def _seq_shifts(cur, before, after, ti, tps):
    tm = cur.shape[0]
    row = lax.broadcasted_iota(jnp.int32, cur.shape, 0)
    before = jnp.where(ti > 0, before.astype(F32), 0.0)
    after = jnp.where(ti < tps - 1, after.astype(F32), 0.0)
    return jnp.where(row == 0, before, pltpu.roll(cur, 1, axis=0)), jnp.where(row == tm - 1, after, pltpu.roll(cur, tm - 1, axis=0))


def _ffn_down(gt, val, conv_w, conv_b, w_down, x1, mod3, g_final, target, B, S):
    T, D = x1.shape
    F = gt.shape[1]
    tps = S // TOKEN_TILE
    prev, nxt = _halo_specs(T, tps, F)

    def body(gt_ref, prev_ref, next_ref, val_ref, cw_ref, cb_ref, w_ref, x1_ref, mod_ref, gf_ref, tgt_ref,
             a_ref, act_ref, vd_ref, dx2_ref, df_ref, gstat_ref, bstat_ref):
        i = pl.program_id(0)
        g = gt_ref[...].astype(F32)
        gprev, gnext = _seq_shifts(g, prev_ref[7:8, :], next_ref[0:1, :], i % tps, tps)
        gc = gprev * cw_ref[0:1, :] + g * cw_ref[1:2, :] + gnext * cw_ref[2:3, :] + cb_ref[...]
        sig = 1.0 / (1.0 + jnp.exp(-gc))
        act = gc * sig
        val = val_ref[...].astype(F32)
        act_ref[...] = act.astype(BF16)
        vd_ref[...] = (val * (sig + act - act * sig)).astype(BF16)
        a = (act * val).astype(BF16)
        a_ref[...] = a
        f = _nn(a, w_ref[...])
        gate = mod_ref[0, 5:6, :]
        x2 = x1_ref[...] + gate * f
        xn, r = _rms(x2)
        err = xn * gf_ref[...] - tgt_ref[...]
        dy = err * (1.0 / D)
        dx2 = _rms_bwd(xn, r, dy * gf_ref[...])
        dx2_ref[...] = dx2
        df_ref[...] = (gate * dx2).astype(BF16)

        @pl.when(i == 0)
        def _():
            gstat_ref[...] = jnp.zeros_like(gstat_ref)

        @pl.when(i % tps == 0)
        def _():
            bstat_ref[...] = jnp.zeros_like(bstat_ref)

        gstat_ref[0:1, :] += jnp.sum(dy * xn, axis=0, keepdims=True)
        tile_loss = jnp.sum(jnp.sum(err * err, axis=1, keepdims=True), axis=0, keepdims=True) * (0.5 / D)
        gstat_ref[1:2, :] += jnp.broadcast_to(tile_loss, (1, D))
        bstat_ref[0, 0:1, :] += jnp.sum(dx2 * f, axis=0, keepdims=True)

    return pl.pallas_call(
        body, name="ffn_down", grid=(T // TOKEN_TILE,),
        in_specs=[_tok_spec(F), prev, nxt, _tok_spec(F), _full(conv_w.shape), _full((1, F)), _full(w_down.shape),
                  _tok_spec(D), _mod_spec(tps, D), _full((1, D)), _tok_spec(D)],
        out_specs=[_tok_spec(F), _tok_spec(F), _tok_spec(F), _tok_spec(D), _tok_spec(D), _full((8, D)), _bstat_spec(tps, D)],
        out_shape=[jax.ShapeDtypeStruct((T, F), BF16), jax.ShapeDtypeStruct((T, F), BF16), jax.ShapeDtypeStruct((T, F), BF16),
                   jax.ShapeDtypeStruct((T, D), F32), jax.ShapeDtypeStruct((T, D), BF16),
                   jax.ShapeDtypeStruct((8, D), F32), jax.ShapeDtypeStruct((B, 8, D), F32)],
        compiler_params=_params("arbitrary"),
    )(gt, gt, gt, val, conv_w, conv_b, w_down, x1, mod3, g_final, target)


def _ffn_down_bwd(df, w_down, act, vd, tasks=()):
    T, D = df.shape
    F = act.shape[1]

    def body(df_ref, w_ref, act_ref, vd_ref, dval_ref, dgc_ref, cstat_ref):
        da = _nt(df_ref[...], w_ref[...])
        dval_ref[...] = (da * act_ref[...].astype(F32)).astype(BF16)
        dgc = da * vd_ref[...].astype(F32)
        dgc_ref[...] = dgc.astype(BF16)

        @pl.when(pl.program_id(0) == 0)
        def _():
            cstat_ref[...] = jnp.zeros_like(cstat_ref)

        cstat_ref[0:1, :] += jnp.sum(dgc, axis=0, keepdims=True)

    return _hosted_call(
        body, "ffn_down_bwd", (T // TOKEN_TILE,),
        [_tok_spec(D), _full(w_down.shape), _tok_spec(F), _tok_spec(F)],
        [_tok_spec(F), _tok_spec(F), _full((8, F))],
        [jax.ShapeDtypeStruct((T, F), BF16), jax.ShapeDtypeStruct((T, F), BF16), jax.ShapeDtypeStruct((8, F), F32)],
        (df, w_down, act, vd), tasks)


def _dw_up_conv(dgc, dval, gt, conv_w, h2, S, tk=512):
    T, F = dgc.shape
    D = h2.shape[1]
    nk, tps, per = T // tk, S // tk, tk // 8
    fc = F // 2 if F % 256 == 0 else F
    nc = F // fc
    gated = lambda i: i // nc
    prev = pl.BlockSpec((8, fc), lambda i, k: (jnp.maximum(k * per - 1, 0) * gated(i), i % nc))
    nxt = pl.BlockSpec((8, fc), lambda i, k: (jnp.minimum((k + 1) * per, T // 8 - 1) * gated(i), i % nc))
    gate_half = pl.BlockSpec((tk, fc), lambda i, k: (k * gated(i), i % nc))
    val_half = pl.BlockSpec((tk, fc), lambda i, k: (k * (1 - gated(i)), i % nc))

    def body(dgc_ref, prev_ref, next_ref, dval_ref, gt_ref, cw_ref, h2_ref, du_ref, dw_ref, cstat_ref, acc):
        i, k = pl.program_id(0), pl.program_id(1)

        @pl.when(k == 0)
        def _():
            acc[...] = jnp.zeros_like(acc)
            cstat_ref[...] = jnp.zeros_like(cstat_ref)

        @pl.when(i < nc)
        def _():
            dv = dval_ref[...]
            du_ref[...] = dv
            acc[...] += _tn(dv, h2_ref[...])

        @pl.when(i >= nc)
        def _():
            d = dgc_ref[...].astype(F32)
            dprev, dnext = _seq_shifts(d, prev_ref[7:8, :], next_ref[0:1, :], k % tps, tps)
            g = gt_ref[...].astype(F32)
            cstat_ref[1:2, :] += jnp.sum(dnext * g, axis=0, keepdims=True)
            cstat_ref[2:3, :] += jnp.sum(d * g, axis=0, keepdims=True)
            cstat_ref[3:4, :] += jnp.sum(dprev * g, axis=0, keepdims=True)
            dgt = (dnext * cw_ref[0:1, :] + d * cw_ref[1:2, :] + dprev * cw_ref[2:3, :]).astype(BF16)
            du_ref[...] = dgt
            acc[...] += _tn(dgt, h2_ref[...])

        @pl.when(k == nk - 1)
        def _():
            dw_ref[...] = acc[...].astype(BF16)

    return pl.pallas_call(
        body, name="dw_up", grid=(2 * nc, nk),
        in_specs=[gate_half, prev, nxt, val_half, gate_half, pl.BlockSpec((conv_w.shape[0], fc), lambda i, k: (0, i % nc)),
                  pl.BlockSpec((tk, D), lambda i, k: (k, 0))],
        out_specs=[pl.BlockSpec((tk, fc), lambda i, k: (k, i)), pl.BlockSpec((fc, D), lambda i, k: (i, 0)),
                   pl.BlockSpec((8, fc), lambda i, k: (gated(i), i % nc))],
        out_shape=[jax.ShapeDtypeStruct((T, 2 * F), BF16), jax.ShapeDtypeStruct((2 * F, D), BF16), jax.ShapeDtypeStruct((16, F), F32)],
        scratch_shapes=[pltpu.VMEM((fc, D), F32)],
        compiler_params=_params("arbitrary", "arbitrary"),
    )(dgc, dgc, dgc, dval, gt, conv_w, h2)


def _ffn_up_bwd(du, w_up, x1, mod3, g_ffn, dx2, mix, B, S, tasks=()):
    T, D = x1.shape
    F = du.shape[1] // 2
    tps = S // TOKEN_TILE

    def body(du_ref, w_ref, x1_ref, mod_ref, g_ref, dx2_ref, mix_ref, dx1_ref, dmix_ref, gstat_ref, bstat_ref):
        i = pl.program_id(0)
        dh2 = _nn(du_ref[...], w_ref[...])
        xn, r = _rms(x1_ref[...])
        scale1 = 1.0 + mod_ref[0, 4:5, :]
        xg = xn * g_ref[...]
        dx1 = dx2_ref[...] + _rms_bwd(xn, r, dh2 * g_ref[...] * scale1)
        dx1_ref[...] = dx1
        dmix_ref[...] = (mod_ref[0, 2:3, :] * dx1).astype(BF16)

        @pl.when(i == 0)
        def _():
            gstat_ref[...] = jnp.zeros_like(gstat_ref)

        @pl.when(i % tps == 0)
        def _():
            bstat_ref[...] = jnp.zeros_like(bstat_ref)

        gstat_ref[0:1, :] += jnp.sum(dh2 * scale1 * xn, axis=0, keepdims=True)
        bstat_ref[0, 0:1, :] += jnp.sum(dh2, axis=0, keepdims=True)
        bstat_ref[0, 1:2, :] += jnp.sum(dh2 * xg, axis=0, keepdims=True)
        bstat_ref[0, 2:3, :] += jnp.sum(dx1 * mix_ref[...], axis=0, keepdims=True)

    return _hosted_call(
        body, "ffn_up_bwd", (T // TOKEN_TILE,),
        [_tok_spec(2 * F), _full(w_up.shape), _tok_spec(D), _mod_spec(tps, D), _full((1, D)), _tok_spec(D), _tok_spec(D)],
        [_tok_spec(D), _tok_spec(D), _full((8, D)), _bstat_spec(tps, D)],
        [jax.ShapeDtypeStruct((T, D), F32), jax.ShapeDtypeStruct((T, D), BF16),
         jax.ShapeDtypeStruct((8, D), F32), jax.ShapeDtypeStruct((B, 8, D), F32)],
        (du, w_up, x1, mod3, g_ffn, dx2, mix), tasks)


def _attn_out_bwd(dmix, w_out, oa, ob, g_na, g_sw, tasks=()):
    T, D = dmix.shape

    def body(dmix_ref, w_ref, oa_ref, ob_ref, gna_ref, gsw_ref, doa_ref, dob_ref, gstat_ref):
        dmixin = _nt(dmix_ref[...], w_ref[...])

        @pl.when(pl.program_id(0) == 0)
        def _():
            gstat_ref[...] = jnp.zeros_like(gstat_ref)

        for k, (o_ref, g_ref, do_ref) in enumerate(((oa_ref, gna_ref, doa_ref), (ob_ref, gsw_ref, dob_ref))):
            dn = dmixin[:, k * NA_WIDTH:(k + 1) * NA_WIDTH]
            on, r = _rms(o_ref[...])
            gstat_ref[k:k + 1, :] += jnp.sum(dn * on, axis=0, keepdims=True)
            do_ref[...] = _rms_bwd(on, r, dn * g_ref[...]).astype(BF16)

    hs = jax.ShapeDtypeStruct((T, NA_WIDTH), BF16)
    return _hosted_call(
        body, "attn_out_bwd", (T // TOKEN_TILE,),
        [_tok_spec(D), _full(w_out.shape), _tok_spec(NA_WIDTH), _tok_spec(SW_WIDTH), _full((1, NA_WIDTH)), _full((1, SW_WIDTH))],
        [_tok_spec(NA_WIDTH), _tok_spec(SW_WIDTH), _full((8, NA_WIDTH))],
        [hs, hs, jax.ShapeDtypeStruct((8, NA_WIDTH), F32)],
        (dmix, w_out, oa, ob, g_na, g_sw), tasks)


def _attn_in_bwd(dqa, dka, dva, dqb, dkb, dvb, cos_t, sin_t, w_in, x2d, mod3, g_attn, dx1, B, S):
    T, D = x2d.shape
    tps = S // TOKEN_TILE

    def body(dqa_ref, dka_ref, dva_ref, dqb_ref, dkb_ref, dvb_ref, cos_ref, sin_ref, w_ref, x_ref, mod_ref, g_ref, dx1_ref,
             gx_ref, dproj_ref, gstat_ref, bstat_ref):
        i = pl.program_id(0)
        drb = jnp.concatenate([dqb_ref[...] * Q_SCALE, dkb_ref[...]], axis=1)
        reps = (1, ROPE_WIDTH // (2 * HEAD_DIM))
        drb = drb * jnp.tile(cos_ref[...], reps) + _rot_half(drb * jnp.tile(sin_ref[...], reps))
        dproj = jnp.concatenate([dqa_ref[...] * Q_SCALE, dka_ref[...], dva_ref[...], drb, dvb_ref[...]], axis=1).astype(BF16)
        dproj_ref[...] = dproj
        dh = _nn(dproj, w_ref[...])
        xn, r = _rms(x_ref[...])
        scale1 = 1.0 + mod_ref[0, 1:2, :]
        gx_ref[...] = dx1_ref[...] + _rms_bwd(xn, r, dh * g_ref[...] * scale1)

        @pl.when(i == 0)
        def _():
            gstat_ref[...] = jnp.zeros_like(gstat_ref)

        @pl.when(i % tps == 0)
        def _():
            bstat_ref[...] = jnp.zeros_like(bstat_ref)

        gstat_ref[0:1, :] += jnp.sum(dh * scale1 * xn, axis=0, keepdims=True)
        bstat_ref[0, 0:1, :] += jnp.sum(dh, axis=0, keepdims=True)
        bstat_ref[0, 1:2, :] += jnp.sum(dh * (xn * g_ref[...]), axis=0, keepdims=True)

    rope = _rope_spec(tps)
    return pl.pallas_call(
        body, name="attn_in_bwd", grid=(T // TOKEN_TILE,),
        in_specs=[_tok_spec(NA_WIDTH), _tok_spec(NA_WIDTH), _tok_spec(NA_WIDTH), _tok_spec(SW_WIDTH), _tok_spec(SW_KV_WIDTH),
                  _tok_spec(SW_KV_WIDTH), rope, rope, _full(w_in.shape), _tok_spec(D), _mod_spec(tps, D), _full((1, D)), _tok_spec(D)],
        out_specs=[_tok_spec(D), _tok_spec(IN_WIDTH), _full((8, D)), _bstat_spec(tps, D)],
        out_shape=[jax.ShapeDtypeStruct((T, D), F32), jax.ShapeDtypeStruct((T, IN_WIDTH), BF16),
                   jax.ShapeDtypeStruct((8, D), F32), jax.ShapeDtypeStruct((B, 8, D), F32)],
        compiler_params=_params("arbitrary"),
    )(dqa, dka, dva, dqb, dkb, dvb, cos_t, sin_t, w_in, x2d, mod3, g_attn, dx1)


def _matmul_tn(a, b, name, tm=None, tk=512):
    T, M = a.shape
    N = b.shape[1]
    tm = M if tm is None else tm
    nk = T // tk

    def body(a_ref, b_ref, o_ref, acc):
        k = pl.program_id(1)

        @pl.when(k == 0)
        def _():
            acc[...] = jnp.zeros_like(acc)

        acc[...] += _tn(a_ref[...], b_ref[...])

        @pl.when(k == nk - 1)
        def _():
            o_ref[...] = acc[...].astype(BF16)

    return pl.pallas_call(
        body, name=name, grid=(M // tm, nk),
        in_specs=[pl.BlockSpec((tk, tm), lambda i, k: (k, i)), pl.BlockSpec((tk, N), lambda i, k: (k, 0))],
        out_specs=pl.BlockSpec((tm, N), lambda i, k: (i, 0)),
        out_shape=jax.ShapeDtypeStruct((M, N), BF16),
        scratch_shapes=[pltpu.VMEM((tm, N), F32)],
        compiler_params=_params("parallel", "arbitrary"),
    )(a, b)


def _na_geometry(S):
    rows = S // GRID_W
    wr = min(NA_ROWS_MAX, rows)
    return rows, wr


def _na_window(r, rows, wr):
    rs = jnp.clip(r - wr // 2, 0, rows - wr)
    return pl.multiple_of(rs * GRID_W, GRID_W), pl.multiple_of((rs - r + NA_ROWS_MAX - 1) * GRID_W, GRID_W)


NA_STEP_PAIRS = 2
NA_GW = NA_STEP_PAIRS * 128
NA_BWD_ROWS = 2
NA_ROWS_PER_STEP = 4


def _na_specs(S, kw_n, order):
    ng = NA_PAIRS // NA_STEP_PAIRS

    def col(k):
        return pl.BlockSpec((1, S, NA_GW), lambda *ids: (order(*ids)[0], 0, k * ng + order(*ids)[1]))
    bias = pl.BlockSpec((NA_STEP_PAIRS, N_DR * GRID_W, 128), lambda *ids: (order(*ids)[1], 0, 0))
    out = pl.BlockSpec((1, S, NA_GW), lambda *ids: (order(*ids)[0], 0, order(*ids)[1]))
    return col(0), col(1), col(2), bias, out


def _block_diag(t):
    left = lax.broadcasted_iota(jnp.int32, t.shape, 1) < HEAD_DIM
    zero = jnp.zeros_like(t)
    return jnp.concatenate([jnp.where(left, t, zero), jnp.where(left, zero, t)], axis=0)


def _diag_blocks(res):
    left = lax.broadcasted_iota(jnp.int32, (HEAD_DIM, 128), 1) < HEAD_DIM
    return jnp.where(left, res[:HEAD_DIM], res[HEAD_DIM:])


def _col_softmax(st):
    e = jnp.exp(st - jnp.max(st, axis=0, keepdims=True))
    return e * (1.0 / jnp.sum(e, axis=0, keepdims=True))


def _na_fwd(qkv, bias, tasks=()):
    B, S, _ = qkv.shape
    rows, wr = _na_geometry(S)
    kw_n = wr * GRID_W

    def body(q_ref, k_ref, v_ref, b_ref, o_ref):
        def step(it, carry):
            win = [_na_window(it * NA_ROWS_PER_STEP + u, rows, wr) for u in range(NA_ROWS_PER_STEP)]
            qrows = [pl.ds(pl.multiple_of((it * NA_ROWS_PER_STEP + u) * GRID_W, GRID_W), GRID_W) for u in range(NA_ROWS_PER_STEP)]
            krows = [pl.ds(w[0], kw_n) for w in win]
            brows = [pl.ds(w[1], kw_n) for w in win]
            lanes = [pl.ds(p * 128, 128) for p in range(NA_STEP_PAIRS)]
            chains = [(u, p) for u in range(NA_ROWS_PER_STEP) for p in range(NA_STEP_PAIRS)]
            st = {(u, p): _nt(k_ref[0, krows[u], lanes[p]], _block_diag(q_ref[0, qrows[u], lanes[p]])) for u, p in chains}
            pn = {(u, p): _col_softmax(st[(u, p)] + b_ref[p, brows[u], :]).astype(BF16) for u, p in chains}
            out = {(u, p): _diag_blocks(_tn(pn[(u, p)], v_ref[0, krows[u], lanes[p]])) for u, p in chains}
            for u in range(NA_ROWS_PER_STEP):
                o_ref[0, qrows[u], :] = jnp.concatenate([out[(u, p)] for p in range(NA_STEP_PAIRS)], axis=1)
            return carry

        lax.fori_loop(0, rows // NA_ROWS_PER_STEP, step, 0)

    q, k, v, bs, out = _na_specs(S, kw_n, lambda b, g: (b, g))
    return _hosted_call(body, "na_fwd", (B, NA_PAIRS // NA_STEP_PAIRS), [q, k, v, bs], [out],
                        [jax.ShapeDtypeStruct((B, S, NA_WIDTH), F32)], (qkv, qkv, qkv, bias), tasks)


def _na_bwd(qkv, bias, doa, tasks=()):
    B, S, _ = qkv.shape
    rows, wr = _na_geometry(S)
    kw_n = wr * GRID_W

    def body(q_ref, k_ref, v_ref, b_ref, do_ref, dq_ref, dk_ref, dv_ref, db_ref):
        @pl.when(pl.program_id(1) == 0)
        def _():
            db_ref[...] = jnp.zeros_like(db_ref)

        dk_ref[...] = jnp.zeros_like(dk_ref)
        dv_ref[...] = jnp.zeros_like(dv_ref)

        def step(it, carry):
            nu, pairs = range(NA_BWD_ROWS), range(NA_STEP_PAIRS)
            win = [_na_window(it * NA_BWD_ROWS + u, rows, wr) for u in nu]
            qrows = [pl.ds(pl.multiple_of((it * NA_BWD_ROWS + u) * GRID_W, GRID_W), GRID_W) for u in nu]
            krows = [pl.ds(w[0], kw_n) for w in win]
            brows = [pl.ds(w[1], kw_n) for w in win]
            lanes = [pl.ds(p * 128, 128) for p in pairs]
            chains = [(u, p) for u in nu for p in pairs]
            kp = {(u, p): k_ref[0, krows[u], lanes[p]] for u, p in chains}
            qbd = {(u, p): _block_diag(q_ref[0, qrows[u], lanes[p]]) for u, p in chains}
            dobd = {(u, p): _block_diag(do_ref[0, qrows[u], lanes[p]]) for u, p in chains}
            st = {c: _nt(kp[c], qbd[c]) for c in chains}
            dpt = {(u, p): _nt(v_ref[0, krows[u], lanes[p]], dobd[(u, p)]) for u, p in chains}
            pn = {(u, p): _col_softmax(st[(u, p)] + b_ref[p, brows[u], :]) for u, p in chains}
            dst = {c: pn[c] * (dpt[c] - jnp.sum(pn[c] * dpt[c], axis=0, keepdims=True)) for c in chains}
            dsb = {c: dst[c].astype(BF16) for c in chains}
            dq = {c: _diag_blocks(_tn(dsb[c], kp[c])) for c in chains}
            dk = {c: _nn(dsb[c], qbd[c]) for c in chains}
            dv = {c: _nn(pn[c].astype(BF16), dobd[c]) for c in chains}
            for u in nu:
                dq_ref[0, qrows[u], :] = jnp.concatenate([dq[(u, p)] for p in pairs], axis=1)
                dk_ref[0, krows[u], :] += jnp.concatenate([dk[(u, p)] for p in pairs], axis=1)
                dv_ref[0, krows[u], :] += jnp.concatenate([dv[(u, p)] for p in pairs], axis=1)
                for p in pairs:
                    db_ref[p, brows[u], :] += dst[(u, p)]
            return carry

        lax.fori_loop(0, rows // NA_BWD_ROWS, step, 0)

    q, k, v, bs, out = _na_specs(S, kw_n, lambda g, b: (b, g))
    hs = jax.ShapeDtypeStruct((B, S, NA_WIDTH), F32)
    return _hosted_call(body, "na_bwd", (NA_PAIRS // NA_STEP_PAIRS, B), [q, k, v, bs, out], [out, out, out, bs],
                        [hs, hs, hs, jax.ShapeDtypeStruct((NA_PAIRS, N_DR * GRID_W, 128), F32)], (qkv, qkv, qkv, bias, doa), tasks)


SW_PAIRS = SW_HEADS // 2


def _sw_band(n, S):
    kw_n = 3 * SW_BLOCK
    start = pl.multiple_of(jnp.clip(n * SW_BLOCK - SW_BLOCK, 0, S - kw_n), SW_BLOCK)
    kpos = start + lax.broadcasted_iota(jnp.int32, (kw_n, SW_BLOCK), 0)
    qpos = n * SW_BLOCK + lax.broadcasted_iota(jnp.int32, (kw_n, SW_BLOCK), 1)
    return start, jnp.abs(qpos - kpos) <= SW_WINDOW


def _kv_halves(t):
    left = lax.broadcasted_iota(jnp.int32, t.shape, 1) < HEAD_DIM
    swapped = pltpu.roll(t, HEAD_DIM, axis=1)
    zero = jnp.zeros_like(t)
    return {(0, 0): jnp.where(left, t, zero), (0, 1): jnp.where(left, zero, swapped),
            (1, 0): jnp.where(left, swapped, zero), (1, 1): jnp.where(left, zero, t)}


def _sw_probs(st, ok, sk):
    st = jnp.where(ok, st, NEG)
    m = jnp.maximum(jnp.max(st, axis=0, keepdims=True), sk)
    e = jnp.exp(st - m)
    esk = jnp.exp(sk - m)
    inv = 1.0 / (jnp.sum(e, axis=0, keepdims=True) + esk)
    return e * inv, esk * inv


def _sw_specs(S):
    q = pl.BlockSpec((1, S, SW_WIDTH), lambda b, *_: (b, 0, ROPE_LO // SW_WIDTH))
    k = pl.BlockSpec((1, S, SW_KV_WIDTH), lambda b, *_: (b, 0, (ROPE_LO + SW_WIDTH) // SW_KV_WIDTH))
    v = pl.BlockSpec((1, S, SW_KV_WIDTH), lambda b, *_: (b, 0, (ROPE_LO + ROPE_WIDTH) // SW_KV_WIDTH))
    return q, k, v


SW_FWD_SPLIT = 2


def _sw_fwd(sink, qkv, tasks=()):
    B, S, _ = qkv.shape
    kw_n = 3 * SW_BLOCK

    def body(sink_ref, q_ref, k_ref, v_ref, o_ref):
        def step(n, carry):
            start, ok = _sw_band(n, S)
            qrows = pl.ds(pl.multiple_of(n * SW_BLOCK, SW_BLOCK), SW_BLOCK)
            krows = pl.ds(start, kw_n)
            kh, vh = _kv_halves(k_ref[0, krows, :]), _kv_halves(v_ref[0, krows, :])
            heads = [(p, e) for p in range(SW_PAIRS) for e in range(2)]
            qp = [q_ref[0, qrows, pl.ds(p * 128, 128)] for p in range(SW_PAIRS)]
            kv_of = lambda p: p // (SW_PAIRS // SW_KV_HEADS)
            st = {(p, e): _nt(kh[(kv_of(p), e)], qp[p]) for p, e in heads}
            pn = {(p, e): _sw_probs(st[(p, e)], ok, sink_ref[2 * p + e])[0].astype(BF16) for p, e in heads}
            outs = [_tn(pn[(p, 0)], vh[(kv_of(p), 0)]) + _tn(pn[(p, 1)], vh[(kv_of(p), 1)]) for p in range(SW_PAIRS)]
            o_ref[0, qrows, :] = jnp.concatenate(outs, axis=1)
            return carry

        half = (S // SW_BLOCK) // SW_FWD_SPLIT
        lax.fori_loop(pl.program_id(1) * half, (pl.program_id(1) + 1) * half, step, 0)

    q, k, v = _sw_specs(S)
    return _hosted_call(
        body, "sw_fwd", (B, SW_FWD_SPLIT), [pl.BlockSpec(memory_space=pltpu.SMEM), q, k, v],
        [pl.BlockSpec((1, S, SW_WIDTH), lambda b, s: (b, 0, 0))], [jax.ShapeDtypeStruct((B, S, SW_WIDTH), F32)],
        (sink, qkv, qkv, qkv), tasks)


def _sw_bwd(sink, qkv, dob):
    B, S, _ = qkv.shape
    kw_n = 3 * SW_BLOCK

    fold_rows = 256

    def body(sink_ref, q_ref, k_ref, v_ref, do_ref, dq_ref, dk_ref, dv_ref, dsink_ref, dk_acc, dv_acc):
        @pl.when(pl.program_id(0) == 0)
        def _():
            dsink_ref[...] = jnp.zeros_like(dsink_ref)

        dk_acc[...] = jnp.zeros_like(dk_acc)
        dv_acc[...] = jnp.zeros_like(dv_acc)
        ppk = SW_PAIRS // SW_KV_HEADS

        def step(n, carry):
            start, ok = _sw_band(n, S)
            qrows = pl.ds(pl.multiple_of(n * SW_BLOCK, SW_BLOCK), SW_BLOCK)
            krows = pl.ds(start, kw_n)
            kh, vh = _kv_halves(k_ref[0, krows, :]), _kv_halves(v_ref[0, krows, :])
            heads = [(p, e) for p in range(SW_PAIRS) for e in range(2)]
            qp = [q_ref[0, qrows, pl.ds(p * 128, 128)] for p in range(SW_PAIRS)]
            dop = [do_ref[0, qrows, pl.ds(p * 128, 128)] for p in range(SW_PAIRS)]
            st = {(p, e): _nt(kh[(p // ppk, e)], qp[p]) for p, e in heads}
            dpt = {(p, e): _nt(vh[(p // ppk, e)], dop[p]) for p, e in heads}
            pnb, dsb = {}, {}
            for p, e in heads:
                pn, psink = _sw_probs(st[(p, e)], ok, sink_ref[2 * p + e])
                delta = jnp.sum(pn * dpt[(p, e)], axis=0, keepdims=True)
                dsb[(p, e)] = (pn * (dpt[(p, e)] - delta)).astype(BF16)
                pnb[(p, e)] = pn.astype(BF16)
                dsink_ref[2 * p + e:2 * p + e + 1, :] += -(psink * delta)
            dq_ref[0, qrows, :] = jnp.concatenate(
                [_tn(dsb[(p, 0)], kh[(p // ppk, 0)]) + _tn(dsb[(p, 1)], kh[(p // ppk, 1)]) for p in range(SW_PAIRS)], axis=1)
            left = lax.broadcasted_iota(jnp.int32, (kw_n, 128), 1) < HEAD_DIM
            dks, dvs = [], []
            for kv in range(SW_KV_HEADS):
                dk = dv = None
                for p in range(kv * ppk, (kv + 1) * ppk):
                    dk_p = jnp.where(left, _nn(dsb[(p, 0)], qp[p]), _nn(dsb[(p, 1)], qp[p]))
                    dv_p = jnp.where(left, _nn(pnb[(p, 0)], dop[p]), _nn(pnb[(p, 1)], dop[p]))
                    dk = dk_p if dk is None else dk + dk_p
                    dv = dv_p if dv is None else dv + dv_p
                dks.append(dk)
                dvs.append(dv)
            dk_acc[krows, :] += jnp.concatenate(dks, axis=1)
            dv_acc[krows, :] += jnp.concatenate(dvs, axis=1)
            return carry

        lax.fori_loop(0, S // SW_BLOCK, step, 0)

        def fold(i, carry):
            rows = pl.ds(pl.multiple_of(i * fold_rows, fold_rows), fold_rows)
            left = lax.broadcasted_iota(jnp.int32, (fold_rows, 128), 1) < HEAD_DIM
            for acc, out_ref in ((dk_acc, dk_ref), (dv_acc, dv_ref)):
                a, b = acc[rows, 0:128], acc[rows, 128:256]
                out_ref[0, rows, :] = jnp.where(left, a + pltpu.roll(a, HEAD_DIM, axis=1), b + pltpu.roll(b, HEAD_DIM, axis=1))
            return carry

        lax.fori_loop(0, S // fold_rows, fold, 0)

        @pl.when(pl.program_id(0) == B - 1)
        def _():
            dsink_ref[...] = jnp.broadcast_to(jnp.sum(dsink_ref[...], axis=1, keepdims=True), dsink_ref.shape)

    q, k, v = _sw_specs(S)
    qo = pl.BlockSpec((1, S, SW_WIDTH), lambda b: (b, 0, 0))
    ko = pl.BlockSpec((1, S, SW_KV_WIDTH), lambda b: (b, 0, 0))
    return pl.pallas_call(
        body, name="sw_bwd", grid=(B,),
        in_specs=[pl.BlockSpec(memory_space=pltpu.SMEM), q, k, v, qo],
        out_specs=[qo, ko, ko, _full((SW_HEADS, 128))],
        out_shape=[jax.ShapeDtypeStruct((B, S, SW_WIDTH), F32), jax.ShapeDtypeStruct((B, S, SW_KV_WIDTH), F32),
                   jax.ShapeDtypeStruct((B, S, SW_KV_WIDTH), F32), jax.ShapeDtypeStruct((SW_HEADS, 128), F32)],
        scratch_shapes=[pltpu.VMEM((S, 2 * SW_KV_WIDTH), F32), pltpu.VMEM((S, 2 * SW_KV_WIDTH), F32)],
        compiler_params=_params("arbitrary"),
    )(sink, qkv, qkv, qkv, dob)


def _pack_sum_adamw(packs, params):
    W = packs.shape[1]
    n_p = len(params)

    def body(p_ref, *refs):
        ins, tot_ref, outs = refs[:3 * n_p], refs[3 * n_p], refs[3 * n_p + 1:]
        tot = p_ref[0:8, :]
        for d in range(1, N_DEV):
            tot = tot + p_ref[8 * d:8 * d + 8, :]
        tot_ref[...] = tot
        for i, (w, _, _, rows, off) in enumerate(params):
            n = w.shape[1]
            g = tot[rows[0]:rows[0] + 1, off:off + n]
            for r in rows[1:]:
                g = g + tot[r:r + 1, off:off + n]
            w_ref, m_ref, v_ref = ins[3 * i:3 * i + 3]
            g_ref, d_ref, nm_ref, nv_ref = outs[4 * i:4 * i + 4]
            g_ref[...] = g
            d_ref[...], nm_ref[...], nv_ref[...] = _adam_update(w_ref[...], g, m_ref[...], v_ref[...])

    res = pl.pallas_call(
        body, name="small_adamw",
        out_shape=[jax.ShapeDtypeStruct((8, W), F32)] + [jax.ShapeDtypeStruct(p[0].shape, F32) for p in params for _ in range(4)],
        compiler_params=pltpu.CompilerParams(vmem_limit_bytes=VMEM_LIMIT),
    )(packs, *[a for p in params for a in p[:3]])
    return res[0], [res[1 + 4 * i:5 + 4 * i] for i in range(n_p)]


def _adam_update(w, g, m, v):
    c1 = 1.0 - ADAM_B1 ** ADAM_STEP
    c2 = 1.0 - ADAM_B2 ** ADAM_STEP
    nm = ADAM_B1 * m + (1.0 - ADAM_B1) * g
    nv = ADAM_B2 * v + (1.0 - ADAM_B2) * (g * g)
    return -ADAM_LR * ((nm / c1) / (jnp.sqrt(nv / c2) + ADAM_EPS) + ADAM_WD * w), nm, nv


def _adamw(w, g, m, v, name):
    def body(w_ref, g_ref, m_ref, v_ref, d_ref, nm_ref, nv_ref):
        d_ref[...], nm_ref[...], nv_ref[...] = _adam_update(w_ref[...], g_ref[...], m_ref[...], v_ref[...])

    s = jax.ShapeDtypeStruct(w.shape, F32)
    return pl.pallas_call(body, name=name, out_shape=[s, s, s],
                          compiler_params=pltpu.CompilerParams(vmem_limit_bytes=VMEM_LIMIT))(w, g, m, v)


def _sum_adamw(own, recvb, w, m, v, name):
    R, C = own.shape
    rc = _row_chunk(R)

    def body(own_ref, r_ref, w_ref, m_ref, v_ref, g_ref, d_ref, nm_ref, nv_ref):
        def chunk(i, carry):
            rows = pl.ds(pl.multiple_of(i * rc, rc), rc)
            g = own_ref[rows, :]
            for j in range(3):
                g = g + r_ref[j, rows, :].astype(F32)
            g_ref[rows, :] = g
            d_ref[rows, :], nm_ref[rows, :], nv_ref[rows, :] = _adam_update(w_ref[rows, :], g, m_ref[rows, :], v_ref[rows, :])
            return carry

        lax.fori_loop(0, R // rc, chunk, 0)

    s = jax.ShapeDtypeStruct((R, C), F32)
    return pl.pallas_call(body, name=name, out_shape=[s, s, s, s],
                          compiler_params=pltpu.CompilerParams(vmem_limit_bytes=VMEM_LIMIT))(own, recvb, w, m, v)


def _by_device(dw):
    return dw.reshape(N_DEV, dw.shape[0] // N_DEV, dw.shape[1])


def _local_step(x, mod, g_attn, w_in, bias, sw_sink, g_na_out, g_sw_out, w_out, g_ffn, w_up, conv_w, conv_b, w_down,
                g_final, target, sharded):
    B, S, D = x.shape
    T = B * S
    x2d = x.reshape(T, D)
    mod3 = mod.reshape(B, 6, D)
    cos_t, sin_t = _rope_tables(S)
    sink = sw_sink.reshape(SW_HEADS)
    n_tiles = T // TOKEN_TILE
    full = lambda g: g.reshape(N_DEV * g.shape[1], g.shape[2])

    if sharded:
        half = w_up.shape[0] // 2
        tasks = [_gather_task(w_out, n_tiles // 2), _gather_task(w_up, n_tiles - 1, rows=(0, half))]
    (h, qkv), got = _attn_in(x2d, mod3, g_attn, w_in, cos_t, sin_t, S, tasks if sharded else [])
    if sharded:
        w_out, w_up_part = full(got[0][0]), got[1][0]
    qkv3 = qkv.reshape(B, S, IN_WIDTH)
    na_steps = B * (NA_PAIRS // NA_STEP_PAIRS)
    (oa,), got = _na_fwd(qkv3, bias, [_gather_task(w_up, na_steps - 1, rows=(half, half), into=w_up_part)] if sharded else [])
    if sharded:
        w_up = full(got[0][0])
    oa = oa.reshape(T, NA_WIDTH)
    (ob,), got = _sw_fwd(sink, qkv3, [_gather_task(w_down, B * SW_FWD_SPLIT - 1)] if sharded else [])
    if sharded:
        w_down = full(got[0][0])
    ob = ob.reshape(T, SW_WIDTH)
    mixin, mix, x1 = _attn_out(oa, ob, x2d, mod3, g_na_out, g_sw_out, w_out, S)
    h2, val, gt = _ffn_up(x1, mod3, g_ffn, w_up, S)
    a, act, vd, dx2, df, gstat_f, bstat_f = _ffn_down(gt, val, conv_w, conv_b, w_down, x1, mod3, g_final, target.reshape(T, D), B, S)
    F = val.shape[1]

    dw_down = _matmul_tn(a, df, "dw_down")
    (dval, dgc, cstat), got = _ffn_down_bwd(df, w_down, act, vd, [_swap_task(_by_device(dw_down))] if sharded else [])
    if sharded:
        send_down, own_down = _chip_sums(_by_device(dw_down), got[0][0])
    du, dw_up, cstat_w = _dw_up_conv(dgc, dval, gt, conv_w, h2, S)
    (dx1, dmix, gstat_u, bstat_u), got = _ffn_up_bwd(du, w_up, x1, mod3, g_ffn, dx2, mix, B, S,
                                                     [_exchange_task(send_down), _swap_task(_by_device(dw_up))] if sharded else [])
    if sharded:
        dw_down = (own_down, got[0][0])
        send_up, own_up = _chip_sums(_by_device(dw_up), got[1][0])
    dw_out = _matmul_tn(mixin, dmix, "dw_out")
    (doa, dob, gstat_o), got = _attn_out_bwd(dmix, w_out, oa, ob, g_na_out, g_sw_out, [_swap_task(_by_device(dw_out))] if sharded else [])
    if sharded:
        send_out, own_out = _chip_sums(_by_device(dw_out), got[0][0])
    (dqa, dka, dva, dbt), got = _na_bwd(qkv3, bias, doa.reshape(B, S, NA_WIDTH),
                                        [_exchange_task(send_up), _exchange_task(send_out)] if sharded else [])
    if sharded:
        dw_up, dw_out = (own_up, got[0][0]), (own_out, got[1][0])
    dqb, dkb, dvb, dsink = _sw_bwd(sink, qkv3, dob.reshape(B, S, SW_WIDTH))
    r2 = lambda t: t.reshape(T, t.shape[-1])
    grad_x, dproj, gstat_i, bstat_i = _attn_in_bwd(r2(dqa), r2(dka), r2(dva), r2(dqb), r2(dkb), r2(dvb), cos_t, sin_t, w_in, x2d, mod3,
                                                   g_attn, dx1, B, S)
    dw_in = _matmul_tn(dproj, h, "dw_in")

    dmod = jnp.stack([bstat_i[:, 0], bstat_i[:, 1], bstat_u[:, 2], bstat_u[:, 0], bstat_u[:, 1], bstat_f[:, 0]], axis=1)
    small = dict(g_attn=gstat_i[0], g_ffn=gstat_u[0], g_final=gstat_f[0], loss=gstat_f[1, 0], g_na_out=gstat_o[0], g_sw_out=gstat_o[1],
                 sw_sink=dsink[:, 0], conv_b=cstat[0], conv_w=cstat_w[9:12], dbt=dbt)
    return grad_x.reshape(B, S, D), dict(w_in=dw_in, w_out=dw_out, w_up=dw_up, w_down=dw_down), dmod, small


def _pad_lanes(v, w):
    return jnp.pad(v, (0, w - v.shape[0]))


def kernel(x, c, w_ada, b_ada, g_attn, w_in, na_rpb, sw_sink, g_na_out, g_sw_out, w_out, g_ffn, w_up, conv_w, conv_b, w_down, g_final, loss_target, m_w_ada, m_b_ada, m_g_attn, m_w_in, m_na_rpb, m_sw_sink, m_g_na_out, m_g_sw_out, m_w_out, m_g_ffn, m_w_up, m_conv_w, m_conv_b, m_w_down, m_g_final, v_w_ada, v_b_ada, v_g_attn, v_w_in, v_na_rpb, v_sw_sink, v_g_na_out, v_g_sw_out, v_w_out, v_g_ffn, v_w_up, v_conv_w, v_conv_b, v_w_down, v_g_final):
    B, S, D = x.shape
    me = 4 * lax.axis_index("x") + 2 * lax.axis_index("y") + lax.axis_index("c")
    ada_c = w_ada.shape[2]
    F_l = conv_w.shape[2]

    cw_l = jnp.pad(conv_w[0], ((0, 8 - conv_w.shape[1]), (0, 0)))
    c_l = jnp.pad(c, ((0, 8 - B), (0, 0)))
    tr = {"w_in", "w_up"}
    w_in_t = jnp.transpose(w_in[0])
    shards = dict(w_out=w_out[0].astype(BF16), w_up=jnp.transpose(w_up[0]).astype(BF16), w_down=w_down[0].astype(BF16))

    b_ada_l = lax.dynamic_slice(b_ada, (0, me * ada_c), (1, ada_c))
    slabs, mod_all, w_in_all = _ada_fwd(jnp.concatenate([c_l, cw_l], axis=1), w_ada[0], b_ada_l, w_in_t.astype(BF16), B)
    c_all = slabs[:, :, :D].reshape(N_DEV * 8, D)
    conv_w_f = jnp.transpose(slabs[:, :3, D:], (1, 0, 2)).reshape(3, N_DEV * F_l)
    mod_mine = lax.dynamic_slice(mod_all, (0, me * B, 0), (N_DEV, B, ada_c))
    mod = jnp.transpose(mod_mine, (1, 0, 2)).reshape(B, N_DEV * ada_c)
    w_in_f = w_in_all.reshape(N_DEV * w_in_t.shape[0], D)

    bias = _na_bias_table(na_rpb[0])

    grad_x, dw, dmod, small = _local_step(x, mod, g_attn, w_in_f, bias, sw_sink, g_na_out, g_sw_out, shards["w_out"], g_ffn,
                                          shards["w_up"], conv_w_f, conv_b, shards["w_down"], g_final.reshape(1, D), loss_target,
                                          sharded=True)
    drpb = _na_bias_grad(small["dbt"])

    row2 = jnp.concatenate([small["g_attn"], small["g_ffn"], small["g_final"], small["g_na_out"], small["g_sw_out"],
                            _pad_lanes(small["sw_sink"], 128), _pad_lanes(small["loss"].reshape(1), 128)])
    rows = [dmod.reshape(B, 6 * D)[0], dmod.reshape(B, 6 * D)[1], _pad_lanes(row2, PACK_W), _pad_lanes(small["conv_b"], PACK_W),
            _pad_lanes(drpb, PACK_W)] + [_pad_lanes(small["conv_w"][k], PACK_W) for k in range(3)]
    g_w_in, (packs,) = _reduce_scatter(_by_device(dw["w_in"]), "scatter_w_in", rider=_gather_task(jnp.stack(rows), 0))
    weights = dict(w_ada=w_ada, b_ada=b_ada, g_attn=g_attn, w_in=w_in, na_rpb=na_rpb, sw_sink=sw_sink, g_na_out=g_na_out,
                   g_sw_out=g_sw_out, w_out=w_out, g_ffn=g_ffn, w_up=w_up, conv_w=conv_w, conv_b=conv_b, w_down=w_down, g_final=g_final)
    ms = dict(w_ada=m_w_ada, b_ada=m_b_ada, g_attn=m_g_attn, w_in=m_w_in, na_rpb=m_na_rpb, sw_sink=m_sw_sink, g_na_out=m_g_na_out,
              g_sw_out=m_g_sw_out, w_out=m_w_out, g_ffn=m_g_ffn, w_up=m_w_up, conv_w=m_conv_w, conv_b=m_conv_b, w_down=m_w_down, g_final=m_g_final)
    vs = dict(w_ada=v_w_ada, b_ada=v_b_ada, g_attn=v_g_attn, w_in=v_w_in, na_rpb=v_na_rpb, sw_sink=v_sw_sink, g_na_out=v_g_na_out,
              g_sw_out=v_g_sw_out, w_out=v_w_out, g_ffn=v_g_ffn, w_up=v_w_up, conv_w=v_conv_w, conv_b=v_conv_b, w_down=v_w_down, g_final=v_g_final)
    names = list(weights)
    grads, deltas, new_m, new_v = {}, {}, {}, {}
    flat = lambda t: t.reshape(1, -1)

    where = dict(b_ada=((0, 1), 0), g_attn=((2,), 0), g_ffn=((2,), D), g_final=((2,), 2 * D), g_na_out=((2,), 3 * D),
                 g_sw_out=((2,), 3 * D + NA_WIDTH), sw_sink=((2,), 3 * D + NA_WIDTH + SW_WIDTH), conv_b=((3,), 0), na_rpb=((4,), 0))
    tot, small_out = _pack_sum_adamw(packs.reshape(N_DEV * 8, PACK_W),
                                     [(flat(weights[n]), flat(ms[n]), flat(vs[n])) + where[n] for n in where])
    for n, (g_, d_, m_, v_) in zip(where, small_out):
        shp = weights[n].shape
        grads[n], deltas[n], new_m[n], new_v[n] = g_.reshape(shp), d_.reshape(shp), m_.reshape(shp), v_.reshape(shp)
    loss = tot[2, 3 * D + NA_WIDTH + SW_WIDTH + 128]

    dmod_cols = lax.dynamic_slice(packs.reshape(N_DEV * 8, PACK_W), (0, me * ada_c), (N_DEV * 8, ada_c))
    big = dict(w_ada=_ada_bwd(c_all, dmod_cols), w_in=g_w_in, conv_w=lax.dynamic_slice(tot[5:8], (0, me * F_l), (3, F_l)))
    for nm in ("w_ada", "w_in", "w_out", "w_up", "conv_w", "w_down"):
        if nm in tr:
            r, back = (lambda t: jnp.transpose(t[0])), (lambda t: jnp.transpose(t)[None])
        else:
            r, back = (lambda t: t[0]), (lambda t: t[None])
        if nm in big:
            g2 = big[nm]
            d_, m_, v_ = _adamw(r(weights[nm]), g2, r(ms[nm]), r(vs[nm]), "adamw_" + nm)
        else:
            g2, d_, m_, v_ = _sum_adamw(*dw[nm], r(weights[nm]), r(ms[nm]), r(vs[nm]), "adamw_" + nm)
        grads[nm], deltas[nm], new_m[nm], new_v[nm] = back(g2), back(d_), back(m_), back(v_)
    return (loss, grad_x, *[grads[n] for n in names], *[deltas[n] for n in names], *[new_m[n] for n in names],
            *[new_v[n] for n in names])
```

```python
import functools

import numpy as np
import jax
import jax.numpy as jnp
from jax import lax
from jax.experimental import pallas as pl
from jax.experimental.pallas import tpu as pltpu

F32, BF16 = jnp.float32, jnp.bfloat16
MESH_ID = pl.DeviceIdType.MESH
N_DEV = 8

HEAD_DIM = 64
NA_HEADS = 8
SW_HEADS = 8
SW_KV_HEADS = 2
SW_GROUP = SW_HEADS // SW_KV_HEADS
NA_WIDTH = NA_HEADS * HEAD_DIM
SW_WIDTH = SW_HEADS * HEAD_DIM
SW_KV_WIDTH = SW_KV_HEADS * HEAD_DIM
ROPE_WIDTH = SW_WIDTH + SW_KV_WIDTH
IN_WIDTH = 3 * NA_WIDTH + SW_WIDTH + 2 * SW_KV_WIDTH
ROPE_LO = 3 * NA_WIDTH
GRID_W = 64
NA_ROWS_MAX = 8
NA_COLS = 16
N_DR = 2 * NA_ROWS_MAX - 1
N_DC = 2 * NA_COLS - 1
SW_WINDOW = 128
SW_BLOCK = 128
ROPE_THETA = 10000.0
EPS = 1e-6
NEG = -1e30
Q_SCALE = HEAD_DIM ** -0.5

ADAM_LR = 0.001
ADAM_B1 = 0.9
ADAM_B2 = 0.999
ADAM_EPS = 1e-08
ADAM_WD = 0.01
ADAM_STEP = 10

TOKEN_TILE = 256
VMEM_LIMIT = 56 * 1024 * 1024

PACK_W = 6144


def _nn(a, b):
    return jnp.dot(a, b, preferred_element_type=F32)


def _nt(a, b):
    return lax.dot_general(a, b, (((1,), (1,)), ((), ())), preferred_element_type=F32)


def _tn(a, b):
    return lax.dot_general(a, b, (((0,), (0,)), ((), ())), preferred_element_type=F32)


def _rms(x):
    r = lax.rsqrt(jnp.mean(x * x, axis=-1, keepdims=True) + EPS)
    return x * r, r


def _rms_bwd(xn, r, gy):
    return r * (gy - xn * jnp.mean(xn * gy, axis=-1, keepdims=True))


def _params(*sem):
    return pltpu.CompilerParams(dimension_semantics=sem, vmem_limit_bytes=VMEM_LIMIT)


def _full(shape):
    n = len(shape)
    return pl.BlockSpec(shape, lambda *_: (0,) * n)


def _mesh_pos():
    return lax.axis_index("x"), lax.axis_index("y"), lax.axis_index("c")


def _row_chunk(r):
    for rc in (128, 64, 32, 16):
        if r % rc == 0:
            return rc
    raise ValueError(f"rows {r} not a multiple of 16")


class _Task:
    def __init__(self, inputs, out_shapes, sems, start, finish, mid=None, mid_step=None, alias=None):
        self.inputs, self.out_shapes, self.sems = list(inputs), list(out_shapes), list(sems)
        self.start, self.finish, self.mid, self.mid_step = start, finish, mid, mid_step
        self.alias = alias


def _hosted_call(body, name, grid, in_specs, out_specs, out_shape, operands, tasks, scratch_shapes=()):
    n_in, n_out, n_scr = len(in_specs), len(out_specs), len(scratch_shapes)
    t_in = [len(t.inputs) for t in tasks]
    t_out = [len(t.out_shapes) for t in tasks]
    t_sem = [len(t.sems) for t in tasks]
    n_steps = int(np.prod(grid))

    def wrapped(*refs):
        ins, rest = refs[:n_in], refs[n_in:]
        task_ins, rest = rest[:sum(t_in)], rest[sum(t_in):]
        outs, rest = rest[:n_out], rest[n_out:]
        task_outs, rest = rest[:sum(t_out)], rest[sum(t_out):]
        scr, task_sems = rest[:n_scr], rest[n_scr:]
        step = pl.program_id(0)
        for ax in range(1, len(grid)):
            step = step * grid[ax] + pl.program_id(ax)
        parts = []
        oi = oo = os_ = 0
        for t, a, b, c in zip(tasks, t_in, t_out, t_sem):
            parts.append((t, task_ins[oi:oi + a], task_outs[oo:oo + b], task_sems[os_:os_ + c]))
            oi, oo, os_ = oi + a, oo + b, os_ + c
        for t, ti, to, ts in parts:
            pl.when(step == 0)(functools.partial(t.start, ti, to, ts))
            if t.mid is not None:
                pl.when(step == t.mid_step)(functools.partial(t.mid, ti, to, ts))
        body(*ins, *outs, *scr)
        for t, ti, to, ts in parts:
            pl.when(step == n_steps - 1)(functools.partial(t.finish, ti, to, ts))

    hbm = pl.BlockSpec(memory_space=pl.ANY)
    aliases, oi, oo = {}, n_in, n_out
    for t, a, b in zip(tasks, t_in, t_out):
        if t.alias is not None:
            aliases[oi + t.alias[0]] = oo + t.alias[1]
        oi, oo = oi + a, oo + b
    res = pl.pallas_call(
        wrapped, name=name, grid=grid,
        in_specs=list(in_specs) + [hbm] * sum(t_in),
        out_specs=list(out_specs) + [hbm] * sum(t_out),
        out_shape=list(out_shape) + [s for t in tasks for s in t.out_shapes],
        scratch_shapes=list(scratch_shapes) + [s for t in tasks for s in t.sems],
        input_output_aliases=aliases,
        compiler_params=_params(*(["arbitrary"] * len(grid))),
    )(*operands, *[a for t in tasks for a in t.inputs])
    own, extra = res[:n_out], res[n_out:]
    per_task, o = [], 0
    for b in t_out:
        per_task.append(extra[o:o + b])
        o += b
    return own, per_task


def _gather_task(shard, mid_step, rows=None, into=None):
    lo, n = (0, shard.shape[0]) if rows is None else rows

    def parts(ins, outs, sems):
        x_ref, out_ref, (send_sems, recv_sems, local_sem) = ins[0], outs[0], sems
        x_, y_, c_ = _mesh_pos()
        me, sibling = (x_, y_, c_), (x_, y_, 1 - c_)
        chips = [(1 - x_, y_), (x_, 1 - y_), (1 - x_, 1 - y_)]
        x_ref = x_ref.at[pl.ds(lo, n)]

        def rows(px, py, pc):
            return out_ref.at[4 * px + 2 * py + pc, pl.ds(lo, n)]

        def copy(k, block, to, src=None):
            return pltpu.make_async_remote_copy(
                src_ref=rows(*block) if src is None else src, dst_ref=rows(*block),
                send_sem=send_sems.at[k], recv_sem=recv_sems.at[k], device_id=to, device_id_type=MESH_ID)

        return dict(
            mine=lambda: pltpu.make_async_copy(x_ref, rows(*me), local_sem),
            first=lambda: [copy(0, me, sibling, src=x_ref)] + [copy(1 + j, me, (*chip, c_), src=x_ref) for j, chip in enumerate(chips)],
            passed=lambda: [copy(4 + j, (*chip, c_), sibling) for j, chip in enumerate(chips)],
            landed=lambda: [copy(1 + j, (*chip, c_), me) for j, chip in enumerate(chips)],
            last=lambda: [copy(0, sibling, me)] + [copy(4 + j, (*chip, 1 - c_), me) for j, chip in enumerate(chips)])

    def start(ins, outs, sems):
        p = parts(ins, outs, sems)
        p["mine"]().start()
        for cp in p["first"]():
            cp.start()

    def mid(ins, outs, sems):
        p = parts(ins, outs, sems)
        for cp, fw in zip(p["landed"](), p["passed"]()):
            cp.wait_recv()
            fw.start()

    def finish(ins, outs, sems):
        p = parts(ins, outs, sems)
        for cp in p["last"]():
            cp.wait_recv()
        for cp in p["first"]() + p["passed"]():
            cp.wait_send()
        p["mine"]().wait()

    return _Task([shard] if into is None else [shard, into], [jax.ShapeDtypeStruct((N_DEV,) + shard.shape, shard.dtype)],
                 [pltpu.SemaphoreType.DMA((7,)), pltpu.SemaphoreType.DMA((7,)), pltpu.SemaphoreType.DMA],
                 start, finish, mid, mid_step, alias=None if into is None else (1, 0))


def _swap_task(g8):
    _, R, C = g8.shape

    def copies(ins, outs, sems):
        (g_ref,), (recv_ref,), (ss, rs) = ins, outs, sems
        x_, y_, c_ = _mesh_pos()
        return [pltpu.make_async_remote_copy(src_ref=g_ref.at[2 * k + (1 - c_)], dst_ref=recv_ref.at[k], send_sem=ss.at[k],
                                             recv_sem=rs.at[k], device_id=(x_, y_, 1 - c_), device_id_type=MESH_ID)
                for k in range(4)]

    def start(ins, outs, sems):
        for cp in copies(ins, outs, sems):
            cp.start()

    def finish(ins, outs, sems):
        cps = copies(ins, outs, sems)
        for cp in cps:
            cp.wait_recv()
        for cp in cps:
            cp.wait_send()

    return _Task([g8], [jax.ShapeDtypeStruct((4, R, C), g8.dtype)],
                 [pltpu.SemaphoreType.DMA((4,)), pltpu.SemaphoreType.DMA((4,))], start, finish)


def _chip_sums(g8, recva):
    _, R, C = g8.shape
    rc = _row_chunk(R)

    def body(g_ref, a_ref, send_ref, own_ref):
        x_, y_, c_ = _mesh_pos()
        chips = [(1 - x_, y_), (x_, 1 - y_), (1 - x_, 1 - y_), (x_, y_)]

        def chunk(i, carry):
            rows = pl.ds(pl.multiple_of(i * rc, rc), rc)
            for j, (tx, ty) in enumerate(chips):
                k = 2 * tx + ty
                s = g_ref[2 * k + c_, rows, :].astype(F32) + a_ref[k, rows, :].astype(F32)
                if j < 3:
                    send_ref[j, rows, :] = s.astype(BF16)
                else:
                    own_ref[rows, :] = s
            return carry

        lax.fori_loop(0, R // rc, chunk, 0)

    return pl.pallas_call(body, name="chip_sums", out_shape=[jax.ShapeDtypeStruct((3, R, C), BF16), jax.ShapeDtypeStruct((R, C), F32)],
                          compiler_params=pltpu.CompilerParams(vmem_limit_bytes=VMEM_LIMIT))(g8, recva)


def _exchange_task(sendb):
    def copies(ins, outs, sems):
        (s_ref,), (recv_ref,), (ss, rs) = ins, outs, sems
        x_, y_, c_ = _mesh_pos()
        flips = [(1 - x_, y_), (x_, 1 - y_), (1 - x_, 1 - y_)]
        return [pltpu.make_async_remote_copy(src_ref=s_ref.at[j], dst_ref=recv_ref.at[j], send_sem=ss.at[j], recv_sem=rs.at[j],
                                             device_id=(tx, ty, c_), device_id_type=MESH_ID) for j, (tx, ty) in enumerate(flips)]

    def start(ins, outs, sems):
        for cp in copies(ins, outs, sems):
            cp.start()

    def finish(ins, outs, sems):
        cps = copies(ins, outs, sems)
        for cp in cps:
            cp.wait_recv()
        for cp in cps:
            cp.wait_send()

    return _Task([sendb], [jax.ShapeDtypeStruct(sendb.shape, sendb.dtype)],
                 [pltpu.SemaphoreType.DMA((3,)), pltpu.SemaphoreType.DMA((3,))], start, finish)


def _silu(v):
    return v * (1.0 / (1.0 + jnp.exp(-v)))


def _ada_fwd(c_slab, w_ada_l, b_ada_l, w_in_shard, n_seq):
    W = c_slab.shape[1]
    D, cols = w_ada_l.shape
    n_rows = N_DEV * n_seq
    t_c, t_w = _gather_task(c_slab, 0), _gather_task(w_in_shard, 0)
    t_m = _gather_task(jax.ShapeDtypeStruct((n_rows, cols), F32), 0)

    def body(c_ref, w_ref, b_ref, ws_ref, slabs_ref, mod_ref, win_ref, c_vm, m_vm, copy_sem, *sems):
        sc, sw, sm = sems[0:3], sems[3:6], sems[6:9]
        t_c.start((c_ref,), (slabs_ref,), sc)
        t_w.start((ws_ref,), (win_ref,), sw)
        t_c.mid((c_ref,), (slabs_ref,), sc)
        t_c.finish((c_ref,), (slabs_ref,), sc)
        cp = pltpu.make_async_copy(slabs_ref, c_vm, copy_sem)
        cp.start()
        cp.wait()
        c_all = c_vm[:, :, 0:D].reshape(N_DEV * 8, D)
        m64 = jnp.dot(_silu(c_all), w_ref[...], precision=lax.Precision.HIGHEST, preferred_element_type=F32) + b_ref[...]
        r = lax.broadcasted_iota(jnp.int32, (n_rows, N_DEV * 8), 0)
        c = lax.broadcasted_iota(jnp.int32, (n_rows, N_DEV * 8), 1)
        pick = jnp.where(c == 8 * (r // n_seq) + r % n_seq, 1.0, 0.0)
        m_vm[...] = jnp.dot(pick, m64, precision=lax.Precision.HIGHEST, preferred_element_type=F32)
        t_m.start((m_vm,), (mod_ref,), sm)
        t_w.mid((ws_ref,), (win_ref,), sw)
        t_m.mid((m_vm,), (mod_ref,), sm)
        t_m.finish((m_vm,), (mod_ref,), sm)
        t_w.finish((ws_ref,), (win_ref,), sw)

    hbm, vm = pl.BlockSpec(memory_space=pl.ANY), pl.BlockSpec(memory_space=pltpu.VMEM)
    return pl.pallas_call(
        body, name="ada_fwd", in_specs=[hbm, vm, vm, hbm], out_specs=[hbm, hbm, hbm],
        out_shape=t_c.out_shapes + t_m.out_shapes + t_w.out_shapes,
        scratch_shapes=[pltpu.VMEM((N_DEV, 8, W), F32), pltpu.VMEM((n_rows, cols), F32), pltpu.SemaphoreType.DMA]
        + t_c.sems + t_w.sems + t_m.sems,
        compiler_params=pltpu.CompilerParams(vmem_limit_bytes=VMEM_LIMIT),
    )(c_slab, w_ada_l, b_ada_l, w_in_shard)


def _ada_bwd(c_all, dmod_cols):
    def body(c_ref, d_ref, o_ref):
        o_ref[...] = lax.dot_general(_silu(c_ref[...]), d_ref[...], (((0,), (0,)), ((), ())),
                                     precision=lax.Precision.HIGHEST, preferred_element_type=F32)
    return pl.pallas_call(body, name="ada_bwd", out_shape=jax.ShapeDtypeStruct((c_all.shape[1], dmod_cols.shape[1]), F32),
                          compiler_params=pltpu.CompilerParams(vmem_limit_bytes=VMEM_LIMIT))(c_all, dmod_cols)


NA_PAIRS = NA_HEADS // 2
N_DR_PAD = 16


def _na_bias_table(na_rpb):
    rev = jnp.pad(jnp.flip(na_rpb, axis=2), ((0, 0), (0, N_DR_PAD - N_DR), (0, GRID_W - N_DC)))
    rev = jnp.transpose(rev.reshape(NA_PAIRS, 2, N_DR_PAD, GRID_W), (0, 2, 1, 3)).reshape(NA_PAIRS, N_DR_PAD, 128)

    def body(r_ref, o_ref):
        k = lax.broadcasted_iota(jnp.int32, (GRID_W, 128), 0)
        lane = lax.broadcasted_iota(jnp.int32, (GRID_W, 128), 1)
        q = lane % GRID_W
        cs = jnp.clip(q - NA_COLS // 2, 0, GRID_W - NA_COLS)
        ok = (k >= cs) & (k < cs + NA_COLS)
        left = lane < GRID_W
        for dr in range(N_DR):
            row = jnp.broadcast_to(r_ref[0, dr:dr + 1, :], (GRID_W, 128))
            r0 = jnp.where(left, row, 0.0)
            r1 = jnp.where(left, pltpu.roll(row, GRID_W, axis=1), 0.0)
            y0 = pltpu.roll(r0, 128 - (NA_COLS - 1), axis=1, stride=1, stride_axis=0)
            y1 = pltpu.roll(r1, GRID_W - (NA_COLS - 1), axis=1, stride=1, stride_axis=0)
            o_ref[0, dr * GRID_W:(dr + 1) * GRID_W, :] = jnp.where(ok, jnp.where(left, y0, y1), NEG)

    return pl.pallas_call(
        body, name="rpb_expand", grid=(NA_PAIRS,),
        in_specs=[pl.BlockSpec((1, N_DR_PAD, 128), lambda p: (p, 0, 0))],
        out_specs=pl.BlockSpec((1, N_DR * GRID_W, 128), lambda p: (p, 0, 0)),
        out_shape=jax.ShapeDtypeStruct((NA_PAIRS, N_DR * GRID_W, 128), F32),
        compiler_params=_params("parallel"),
    )(rev)


def _na_bias_grad(db):
    a = np.arange(128)
    flip = jnp.asarray(((a[:, None] // GRID_W == a[None, :] // GRID_W)
                        & (a[:, None] % GRID_W + a[None, :] % GRID_W == GRID_W - 1)).astype(np.float32))

    def body(d_ref, j_ref, o_ref):
        o_ref[...] = jnp.zeros_like(o_ref)
        for dr in range(N_DR):
            t = jnp.dot(d_ref[0, dr * GRID_W:(dr + 1) * GRID_W, :], j_ref[...], precision=lax.Precision.HIGHEST, preferred_element_type=F32)
            t = pltpu.roll(t, GRID_W + NA_COLS, axis=1, stride=1, stride_axis=0)
            o_ref[0, dr:dr + 1, :] = jnp.sum(t, axis=0, keepdims=True)

    rows = pl.pallas_call(
        body, name="rpb_reduce", grid=(NA_PAIRS,),
        in_specs=[pl.BlockSpec((1, N_DR * GRID_W, 128), lambda p: (p, 0, 0)), _full((128, 128))],
        out_specs=pl.BlockSpec((1, N_DR_PAD, 128), lambda p: (p, 0, 0)),
        out_shape=jax.ShapeDtypeStruct((NA_PAIRS, N_DR_PAD, 128), F32),
        compiler_params=_params("parallel"),
    )(db, flip)
    g = rows.reshape(NA_PAIRS, N_DR_PAD, 2, GRID_W)[:, :N_DR, :, :N_DC]
    return jnp.transpose(g, (0, 2, 1, 3)).reshape(-1)


def _rope_tables(S):
    half = HEAD_DIM // 2
    inv = np.float32(ROPE_THETA) ** (-np.arange(half, dtype=np.float32) / np.float32(half))
    ang = np.arange(S).astype(np.float32)[:, None] * inv[None, :]
    cos, sin = np.cos(ang).astype(np.float32), np.sin(ang).astype(np.float32)
    return jnp.asarray(np.tile(np.concatenate([cos, cos], axis=1), (1, 2))), jnp.asarray(np.tile(np.concatenate([-sin, sin], axis=1), (1, 2)))


def _rope_spec(tps):
    return pl.BlockSpec((TOKEN_TILE, 2 * HEAD_DIM), lambda i: (i % tps, 0))


def _rot_half(t):
    w = t.shape[1]
    lane = lax.broadcasted_iota(jnp.int32, t.shape, 1)
    return jnp.where((lane % HEAD_DIM) < HEAD_DIM // 2, pltpu.roll(t, w - HEAD_DIM // 2, axis=1),
                     pltpu.roll(t, HEAD_DIM // 2, axis=1))


def _tok_spec(w):
    return pl.BlockSpec((TOKEN_TILE, w), lambda i: (i, 0))


def _mod_spec(tps, d):
    return pl.BlockSpec((1, 6, d), lambda i: (i // tps, 0, 0))


def _bstat_spec(tps, w):
    return pl.BlockSpec((1, 8, w), lambda i: (i // tps, 0, 0))


def _attn_in(x2d, mod3, g_attn, w_in, cos_t, sin_t, S, tasks=()):
    T, D = x2d.shape
    tps = S // TOKEN_TILE

    def body(x_ref, mod_ref, g_ref, w_ref, cos_ref, sin_ref, h_ref, qkv_ref):
        xn, _ = _rms(x_ref[...])
        h = (xn * g_ref[...]) * (1.0 + mod_ref[0, 1:2, :]) + mod_ref[0, 0:1, :]
        hb = h.astype(BF16)
        h_ref[...] = hb
        proj = _nt(hb, w_ref[...])
        rb = proj[:, ROPE_LO:ROPE_LO + ROPE_WIDTH]
        reps = (1, ROPE_WIDTH // (2 * HEAD_DIM))
        rb = rb * jnp.tile(cos_ref[...], reps) + _rot_half(rb) * jnp.tile(sin_ref[...], reps)
        qkv_ref[:, 0:NA_WIDTH] = (proj[:, 0:NA_WIDTH] * Q_SCALE).astype(BF16)
        qkv_ref[:, NA_WIDTH:ROPE_LO] = proj[:, NA_WIDTH:ROPE_LO].astype(BF16)
        qkv_ref[:, ROPE_LO:ROPE_LO + SW_WIDTH] = (rb[:, 0:SW_WIDTH] * Q_SCALE).astype(BF16)
        qkv_ref[:, ROPE_LO + SW_WIDTH:ROPE_LO + ROPE_WIDTH] = rb[:, SW_WIDTH:].astype(BF16)
        qkv_ref[:, ROPE_LO + ROPE_WIDTH:] = proj[:, ROPE_LO + ROPE_WIDTH:].astype(BF16)

    return _hosted_call(
        body, "attn_in", (T // TOKEN_TILE,),
        [_tok_spec(D), _mod_spec(tps, D), _full((1, D)), _full(w_in.shape), _rope_spec(tps), _rope_spec(tps)],
        [_tok_spec(D), _tok_spec(IN_WIDTH)],
        [jax.ShapeDtypeStruct((T, D), BF16), jax.ShapeDtypeStruct((T, IN_WIDTH), BF16)],
        (x2d, mod3, g_attn, w_in, cos_t, sin_t), tasks)


def _attn_out(oa, ob, x2d, mod3, g_na, g_sw, w_out, S):
    T, D = x2d.shape
    tps = S // TOKEN_TILE

    def body(oa_ref, ob_ref, x_ref, mod_ref, gna_ref, gsw_ref, w_ref, mixin_ref, mix_ref, x1_ref):
        oan, _ = _rms(oa_ref[...])
        obn, _ = _rms(ob_ref[...])
        mixin = jnp.concatenate([oan * gna_ref[...], obn * gsw_ref[...]], axis=1).astype(BF16)
        mixin_ref[...] = mixin
        mix = _nn(mixin, w_ref[...])
        mix_ref[...] = mix
        x1_ref[...] = x_ref[...] + mod_ref[0, 2:3, :] * mix

    return pl.pallas_call(
        body, name="attn_out", grid=(T // TOKEN_TILE,),
        in_specs=[_tok_spec(NA_WIDTH), _tok_spec(SW_WIDTH), _tok_spec(D), _mod_spec(tps, D),
                  _full((1, NA_WIDTH)), _full((1, SW_WIDTH)), _full(w_out.shape)],
        out_specs=[_tok_spec(NA_WIDTH + SW_WIDTH), _tok_spec(D), _tok_spec(D)],
        out_shape=[jax.ShapeDtypeStruct((T, NA_WIDTH + SW_WIDTH), BF16), jax.ShapeDtypeStruct((T, D), F32),
                   jax.ShapeDtypeStruct((T, D), F32)],
        compiler_params=_params("parallel"),
    )(oa, ob, x2d, mod3, g_na, g_sw, w_out)


def _ffn_up(x1, mod3, g_ffn, w_up, S):
    T, D = x1.shape
    F = w_up.shape[0] // 2
    tps = S // TOKEN_TILE

    def body(x1_ref, mod_ref, g_ref, w_ref, h2_ref, val_ref, gt_ref):
        xn, _ = _rms(x1_ref[...])
        h2 = ((xn * g_ref[...]) * (1.0 + mod_ref[0, 4:5, :]) + mod_ref[0, 3:4, :]).astype(BF16)
        h2_ref[...] = h2
        u = _nt(h2, w_ref[...])
        val_ref[...] = u[:, :F].astype(BF16)
        gt_ref[...] = u[:, F:].astype(BF16)

    return pl.pallas_call(
        body, name="ffn_up", grid=(T // TOKEN_TILE,),
        in_specs=[_tok_spec(D), _mod_spec(tps, D), _full((1, D)), _full(w_up.shape)],
        out_specs=[_tok_spec(D), _tok_spec(F), _tok_spec(F)],
        out_shape=[jax.ShapeDtypeStruct((T, D), BF16), jax.ShapeDtypeStruct((T, F), BF16), jax.ShapeDtypeStruct((T, F), BF16)],
        compiler_params=_params("parallel"),
    )(x1, mod3, g_ffn, w_up)


def _halo_specs(T, tps, w):
    per = TOKEN_TILE // 8
    prev = pl.BlockSpec((8, w), lambda i: (jnp.maximum(i * per - 1, 0), 0))
    nxt = pl.BlockSpec((8, w), lambda i: (jnp.minimum((i + 1) * per, T // 8 - 1), 0))
    return prev, nxt


def _seq_shifts(cur, before, after, ti, tps):
    tm = cur.shape[0]
    row = lax.broadcasted_iota(jnp.int32, cur.shape, 0)
    before = jnp.where(ti > 0, before.astype(F32), 0.0)
    after = jnp.where(ti < tps - 1, after.astype(F32), 0.0)
    return jnp.where(row == 0, before, pltpu.roll(cur, 1, axis=0)), jnp.where(row == tm - 1, after, pltpu.roll(cur, tm - 1, axis=0))


def _ffn_down(gt, val, conv_w, conv_b, w_down, x1, mod3, g_final, target, B, S):
    T, D = x1.shape
    F = gt.shape[1]
    tps = S // TOKEN_TILE
    prev, nxt = _halo_specs(T, tps, F)

    def body(gt_ref, prev_ref, next_ref, val_ref, cw_ref, cb_ref, w_ref, x1_ref, mod_ref, gf_ref, tgt_ref,
             a_ref, act_ref, vd_ref, dx2_ref, df_ref, gstat_ref, bstat_ref):
        i = pl.program_id(0)
        g = gt_ref[...].astype(F32)
        gprev, gnext = _seq_shifts(g, prev_ref[7:8, :], next_ref[0:1, :], i % tps, tps)
        gc = gprev * cw_ref[0:1, :] + g * cw_ref[1:2, :] + gnext * cw_ref[2:3, :] + cb_ref[...]
        sig = 1.0 / (1.0 + jnp.exp(-gc))
        act = gc * sig
        val = val_ref[...].astype(F32)
        act_ref[...] = act.astype(BF16)
        vd_ref[...] = (val * (sig + act - act * sig)).astype(BF16)
        a = (act * val).astype(BF16)
        a_ref[...] = a
        f = _nn(a, w_ref[...])
        gate = mod_ref[0, 5:6, :]
        x2 = x1_ref[...] + gate * f
        xn, r = _rms(x2)
        err = xn * gf_ref[...] - tgt_ref[...]
        dy = err * (1.0 / D)
        dx2 = _rms_bwd(xn, r, dy * gf_ref[...])
        dx2_ref[...] = dx2
        df_ref[...] = (gate * dx2).astype(BF16)

        @pl.when(i == 0)
        def _():
            gstat_ref[...] = jnp.zeros_like(gstat_ref)

        @pl.when(i % tps == 0)
        def _():
            bstat_ref[...] = jnp.zeros_like(bstat_ref)

        gstat_ref[0:1, :] += jnp.sum(dy * xn, axis=0, keepdims=True)
        tile_loss = jnp.sum(jnp.sum(err * err, axis=1, keepdims=True), axis=0, keepdims=True) * (0.5 / D)
        gstat_ref[1:2, :] += jnp.broadcast_to(tile_loss, (1, D))
        bstat_ref[0, 0:1, :] += jnp.sum(dx2 * f, axis=0, keepdims=True)

    return pl.pallas_call(
        body, name="ffn_down", grid=(T // TOKEN_TILE,),
        in_specs=[_tok_spec(F), prev, nxt, _tok_spec(F), _full(conv_w.shape), _full((1, F)), _full(w_down.shape),
                  _tok_spec(D), _mod_spec(tps, D), _full((1, D)), _tok_spec(D)],
        out_specs=[_tok_spec(F), _tok_spec(F), _tok_spec(F), _tok_spec(D), _tok_spec(D), _full((8, D)), _bstat_spec(tps, D)],
        out_shape=[jax.ShapeDtypeStruct((T, F), BF16), jax.ShapeDtypeStruct((T, F), BF16), jax.ShapeDtypeStruct((T, F), BF16),
                   jax.ShapeDtypeStruct((T, D), F32), jax.ShapeDtypeStruct((T, D), BF16),
                   jax.ShapeDtypeStruct((8, D), F32), jax.ShapeDtypeStruct((B, 8, D), F32)],
        compiler_params=_params("arbitrary"),
    )(gt, gt, gt, val, conv_w, conv_b, w_down, x1, mod3, g_final, target)


def _ffn_down_bwd(df, w_down, act, vd, tasks=()):
    T, D = df.shape
    F = act.shape[1]

    def body(df_ref, w_ref, act_ref, vd_ref, dval_ref, dgc_ref, cstat_ref):
        da = _nt(df_ref[...], w_ref[...])
        dval_ref[...] = (da * act_ref[...].astype(F32)).astype(BF16)
        dgc = da * vd_ref[...].astype(F32)
        dgc_ref[...] = dgc.astype(BF16)

        @pl.when(pl.program_id(0) == 0)
        def _():
            cstat_ref[...] = jnp.zeros_like(cstat_ref)

        cstat_ref[0:1, :] += jnp.sum(dgc, axis=0, keepdims=True)

    return _hosted_call(
        body, "ffn_down_bwd", (T // TOKEN_TILE,),
        [_tok_spec(D), _full(w_down.shape), _tok_spec(F), _tok_spec(F)],
        [_tok_spec(F), _tok_spec(F), _full((8, F))],
        [jax.ShapeDtypeStruct((T, F), BF16), jax.ShapeDtypeStruct((T, F), BF16), jax.ShapeDtypeStruct((8, F), F32)],
        (df, w_down, act, vd), tasks)


def _ffn_up_bwd(dgc, dval, gt, conv_w, w_up, x1, mod3, g_ffn, dx2, mix, B, S, tasks=()):
    T, D = x1.shape
    F = dgc.shape[1]
    tps = S // TOKEN_TILE
    prev, nxt = _halo_specs(T, tps, F)

    def body(dgc_ref, prev_ref, next_ref, dval_ref, gt_ref, cw_ref, w_ref, x1_ref, mod_ref, g_ref, dx2_ref, mix_ref,
             du_ref, dx1_ref, dmix_ref, gstat_ref, bstat_ref, cstat_ref):
        i = pl.program_id(0)
        d = dgc_ref[...].astype(F32)
        dprev, dnext = _seq_shifts(d, prev_ref[7:8, :], next_ref[0:1, :], i % tps, tps)
        g = gt_ref[...].astype(F32)

        @pl.when(i == 0)
        def _():
            cstat_ref[...] = jnp.zeros_like(cstat_ref)

        cstat_ref[1:2, :] += jnp.sum(dnext * g, axis=0, keepdims=True)
        cstat_ref[2:3, :] += jnp.sum(d * g, axis=0, keepdims=True)
        cstat_ref[3:4, :] += jnp.sum(dprev * g, axis=0, keepdims=True)
        dgt = dnext * cw_ref[0:1, :] + d * cw_ref[1:2, :] + dprev * cw_ref[2:3, :]
        du = jnp.concatenate([dval_ref[...], dgt.astype(BF16)], axis=1)
        du_ref[...] = du
        dh2 = _nn(du, w_ref[...])
        xn, r = _rms(x1_ref[...])
        scale1 = 1.0 + mod_ref[0, 4:5, :]
        xg = xn * g_ref[...]
        dx1 = dx2_ref[...] + _rms_bwd(xn, r, dh2 * g_ref[...] * scale1)
        dx1_ref[...] = dx1
        dmix_ref[...] = (mod_ref[0, 2:3, :] * dx1).astype(BF16)

        @pl.when(i == 0)
        def _():
            gstat_ref[...] = jnp.zeros_like(gstat_ref)

        @pl.when(i % tps == 0)
        def _():
            bstat_ref[...] = jnp.zeros_like(bstat_ref)

        gstat_ref[0:1, :] += jnp.sum(dh2 * scale1 * xn, axis=0, keepdims=True)
        bstat_ref[0, 0:1, :] += jnp.sum(dh2, axis=0, keepdims=True)
        bstat_ref[0, 1:2, :] += jnp.sum(dh2 * xg, axis=0, keepdims=True)
        bstat_ref[0, 2:3, :] += jnp.sum(dx1 * mix_ref[...], axis=0, keepdims=True)

    return _hosted_call(
        body, "ffn_up_bwd", (T // TOKEN_TILE,),
        [_tok_spec(F), prev, nxt, _tok_spec(F), _tok_spec(F), _full(conv_w.shape), _full(w_up.shape), _tok_spec(D),
         _mod_spec(tps, D), _full((1, D)), _tok_spec(D), _tok_spec(D)],
        [_tok_spec(2 * F), _tok_spec(D), _tok_spec(D), _full((8, D)), _bstat_spec(tps, D), _full((8, F))],
        [jax.ShapeDtypeStruct((T, 2 * F), BF16), jax.ShapeDtypeStruct((T, D), F32), jax.ShapeDtypeStruct((T, D), BF16),
         jax.ShapeDtypeStruct((8, D), F32), jax.ShapeDtypeStruct((B, 8, D), F32), jax.ShapeDtypeStruct((8, F), F32)],
        (dgc, dgc, dgc, dval, gt, conv_w, w_up, x1, mod3, g_ffn, dx2, mix), tasks)


def _attn_out_bwd(dmix, w_out, oa, ob, g_na, g_sw, tasks=()):
    T, D = dmix.shape

    def body(dmix_ref, w_ref, oa_ref, ob_ref, gna_ref, gsw_ref, doa_ref, dob_ref, gstat_ref):
        dmixin = _nt(dmix_ref[...], w_ref[...])

        @pl.when(pl.program_id(0) == 0)
        def _():
            gstat_ref[...] = jnp.zeros_like(gstat_ref)

        for k, (o_ref, g_ref, do_ref) in enumerate(((oa_ref, gna_ref, doa_ref), (ob_ref, gsw_ref, dob_ref))):
            dn = dmixin[:, k * NA_WIDTH:(k + 1) * NA_WIDTH]
            on, r = _rms(o_ref[...])
            gstat_ref[k:k + 1, :] += jnp.sum(dn * on, axis=0, keepdims=True)
            do_ref[...] = _rms_bwd(on, r, dn * g_ref[...]).astype(BF16)

    hs = jax.ShapeDtypeStruct((T, NA_WIDTH), BF16)
    return _hosted_call(
        body, "attn_out_bwd", (T // TOKEN_TILE,),
        [_tok_spec(D), _full(w_out.shape), _tok_spec(NA_WIDTH), _tok_spec(SW_WIDTH), _full((1, NA_WIDTH)), _full((1, SW_WIDTH))],
        [_tok_spec(NA_WIDTH), _tok_spec(SW_WIDTH), _full((8, NA_WIDTH))],
        [hs, hs, jax.ShapeDtypeStruct((8, NA_WIDTH), F32)],
        (dmix, w_out, oa, ob, g_na, g_sw), tasks)


def _attn_in_bwd(dqa, dka, dva, dqb, dkb, dvb, cos_t, sin_t, w_in, x2d, mod3, g_attn, dx1, B, S):
    T, D = x2d.shape
    tps = S // TOKEN_TILE

    def body(dqa_ref, dka_ref, dva_ref, dqb_ref, dkb_ref, dvb_ref, cos_ref, sin_ref, w_ref, x_ref, mod_ref, g_ref, dx1_ref,
             gx_ref, dproj_ref, gstat_ref, bstat_ref):
        i = pl.program_id(0)
        drb = jnp.concatenate([dqb_ref[...] * Q_SCALE, dkb_ref[...]], axis=1)
        reps = (1, ROPE_WIDTH // (2 * HEAD_DIM))
        drb = drb * jnp.tile(cos_ref[...], reps) + _rot_half(drb * jnp.tile(sin_ref[...], reps))
        dproj = jnp.concatenate([dqa_ref[...] * Q_SCALE, dka_ref[...], dva_ref[...], drb, dvb_ref[...]], axis=1).astype(BF16)
        dproj_ref[...] = dproj
        dh = _nn(dproj, w_ref[...])
        xn, r = _rms(x_ref[...])
        scale1 = 1.0 + mod_ref[0, 1:2, :]
        gx_ref[...] = dx1_ref[...] + _rms_bwd(xn, r, dh * g_ref[...] * scale1)

        @pl.when(i == 0)
        def _():
            gstat_ref[...] = jnp.zeros_like(gstat_ref)

        @pl.when(i % tps == 0)
        def _():
            bstat_ref[...] = jnp.zeros_like(bstat_ref)

        gstat_ref[0:1, :] += jnp.sum(dh * scale1 * xn, axis=0, keepdims=True)
        bstat_ref[0, 0:1, :] += jnp.sum(dh, axis=0, keepdims=True)
        bstat_ref[0, 1:2, :] += jnp.sum(dh * (xn * g_ref[...]), axis=0, keepdims=True)

    rope = _rope_spec(tps)
    return pl.pallas_call(
        body, name="attn_in_bwd", grid=(T // TOKEN_TILE,),
        in_specs=[_tok_spec(NA_WIDTH), _tok_spec(NA_WIDTH), _tok_spec(NA_WIDTH), _tok_spec(SW_WIDTH), _tok_spec(SW_KV_WIDTH),
                  _tok_spec(SW_KV_WIDTH), rope, rope, _full(w_in.shape), _tok_spec(D), _mod_spec(tps, D), _full((1, D)), _tok_spec(D)],
        out_specs=[_tok_spec(D), _tok_spec(IN_WIDTH), _full((8, D)), _bstat_spec(tps, D)],
        out_shape=[jax.ShapeDtypeStruct((T, D), F32), jax.ShapeDtypeStruct((T, IN_WIDTH), BF16),
                   jax.ShapeDtypeStruct((8, D), F32), jax.ShapeDtypeStruct((B, 8, D), F32)],
        compiler_params=_params("arbitrary"),
    )(dqa, dka, dva, dqb, dkb, dvb, cos_t, sin_t, w_in, x2d, mod3, g_attn, dx1)


def _matmul_tn(a, b, name, tm=None, tk=512):
    T, M = a.shape
    N = b.shape[1]
    tm = M if tm is None else tm
    nk = T // tk

    def body(a_ref, b_ref, o_ref, acc):
        k = pl.program_id(1)

        @pl.when(k == 0)
        def _():
            acc[...] = jnp.zeros_like(acc)

        acc[...] += _tn(a_ref[...], b_ref[...])

        @pl.when(k == nk - 1)
        def _():
            o_ref[...] = acc[...].astype(BF16)

    return pl.pallas_call(
        body, name=name, grid=(M // tm, nk),
        in_specs=[pl.BlockSpec((tk, tm), lambda i, k: (k, i)), pl.BlockSpec((tk, N), lambda i, k: (k, 0))],
        out_specs=pl.BlockSpec((tm, N), lambda i, k: (i, 0)),
        out_shape=jax.ShapeDtypeStruct((M, N), BF16),
        scratch_shapes=[pltpu.VMEM((tm, N), F32)],
        compiler_params=_params("parallel", "arbitrary"),
    )(a, b)


def _na_geometry(S):
    rows = S // GRID_W
    wr = min(NA_ROWS_MAX, rows)
    return rows, wr


def _na_window(r, rows, wr):
    rs = jnp.clip(r - wr // 2, 0, rows - wr)
    return pl.multiple_of(rs * GRID_W, GRID_W), pl.multiple_of((rs - r + NA_ROWS_MAX - 1) * GRID_W, GRID_W)


NA_STEP_PAIRS = 2
NA_GW = NA_STEP_PAIRS * 128
NA_BWD_ROWS = 2
NA_ROWS_PER_STEP = 4


def _na_specs(S, kw_n, order):
    ng = NA_PAIRS // NA_STEP_PAIRS

    def col(k):
        return pl.BlockSpec((1, S, NA_GW), lambda *ids: (order(*ids)[0], 0, k * ng + order(*ids)[1]))
    bias = pl.BlockSpec((NA_STEP_PAIRS, N_DR * GRID_W, 128), lambda *ids: (order(*ids)[1], 0, 0))
    out = pl.BlockSpec((1, S, NA_GW), lambda *ids: (order(*ids)[0], 0, order(*ids)[1]))
    return col(0), col(1), col(2), bias, out


def _block_diag(t):
    left = lax.broadcasted_iota(jnp.int32, t.shape, 1) < HEAD_DIM
    zero = jnp.zeros_like(t)
    return jnp.concatenate([jnp.where(left, t, zero), jnp.where(left, zero, t)], axis=0)


def _diag_blocks(res):
    left = lax.broadcasted_iota(jnp.int32, (HEAD_DIM, 128), 1) < HEAD_DIM
    return jnp.where(left, res[:HEAD_DIM], res[HEAD_DIM:])


def _col_softmax(st):
    e = jnp.exp(st - jnp.max(st, axis=0, keepdims=True))
    return e * (1.0 / jnp.sum(e, axis=0, keepdims=True))


def _na_fwd(qkv, bias, tasks=()):
    B, S, _ = qkv.shape
    rows, wr = _na_geometry(S)
    kw_n = wr * GRID_W

    def body(q_ref, k_ref, v_ref, b_ref, o_ref):
        def step(it, carry):
            win = [_na_window(it * NA_ROWS_PER_STEP + u, rows, wr) for u in range(NA_ROWS_PER_STEP)]
            qrows = [pl.ds(pl.multiple_of((it * NA_ROWS_PER_STEP + u) * GRID_W, GRID_W), GRID_W) for u in range(NA_ROWS_PER_STEP)]
            krows = [pl.ds(w[0], kw_n) for w in win]
            brows = [pl.ds(w[1], kw_n) for w in win]
            lanes = [pl.ds(p * 128, 128) for p in range(NA_STEP_PAIRS)]
            chains = [(u, p) for u in range(NA_ROWS_PER_STEP) for p in range(NA_STEP_PAIRS)]
            st = {(u, p): _nt(k_ref[0, krows[u], lanes[p]], _block_diag(q_ref[0, qrows[u], lanes[p]])) for u, p in chains}
            pn = {(u, p): _col_softmax(st[(u, p)] + b_ref[p, brows[u], :]).astype(BF16) for u, p in chains}
            out = {(u, p): _diag_blocks(_tn(pn[(u, p)], v_ref[0, krows[u], lanes[p]])) for u, p in chains}
            for u in range(NA_ROWS_PER_STEP):
                o_ref[0, qrows[u], :] = jnp.concatenate([out[(u, p)] for p in range(NA_STEP_PAIRS)], axis=1)
            return carry

        lax.fori_loop(0, rows // NA_ROWS_PER_STEP, step, 0)

    q, k, v, bs, out = _na_specs(S, kw_n, lambda b, g: (b, g))
    return _hosted_call(body, "na_fwd", (B, NA_PAIRS // NA_STEP_PAIRS), [q, k, v, bs], [out],
                        [jax.ShapeDtypeStruct((B, S, NA_WIDTH), F32)], (qkv, qkv, qkv, bias), tasks)


def _na_bwd(qkv, bias, doa, tasks=()):
    B, S, _ = qkv.shape
    rows, wr = _na_geometry(S)
    kw_n = wr * GRID_W

    def body(q_ref, k_ref, v_ref, b_ref, do_ref, dq_ref, dk_ref, dv_ref, db_ref):
        @pl.when(pl.program_id(1) == 0)
        def _():
            db_ref[...] = jnp.zeros_like(db_ref)

        dk_ref[...] = jnp.zeros_like(dk_ref)
        dv_ref[...] = jnp.zeros_like(dv_ref)

        def step(it, carry):
            nu, pairs = range(NA_BWD_ROWS), range(NA_STEP_PAIRS)
            win = [_na_window(it * NA_BWD_ROWS + u, rows, wr) for u in nu]
            qrows = [pl.ds(pl.multiple_of((it * NA_BWD_ROWS + u) * GRID_W, GRID_W), GRID_W) for u in nu]
            krows = [pl.ds(w[0], kw_n) for w in win]
            brows = [pl.ds(w[1], kw_n) for w in win]
            lanes = [pl.ds(p * 128, 128) for p in pairs]
            chains = [(u, p) for u in nu for p in pairs]
            kp = {(u, p): k_ref[0, krows[u], lanes[p]] for u, p in chains}
            qbd = {(u, p): _block_diag(q_ref[0, qrows[u], lanes[p]]) for u, p in chains}
            dobd = {(u, p): _block_diag(do_ref[0, qrows[u], lanes[p]]) for u, p in chains}
            st = {c: _nt(kp[c], qbd[c]) for c in chains}
            dpt = {(u, p): _nt(v_ref[0, krows[u], lanes[p]], dobd[(u, p)]) for u, p in chains}
            pn = {(u, p): _col_softmax(st[(u, p)] + b_ref[p, brows[u], :]) for u, p in chains}
            dst = {c: pn[c] * (dpt[c] - jnp.sum(pn[c] * dpt[c], axis=0, keepdims=True)) for c in chains}
            dsb = {c: dst[c].astype(BF16) for c in chains}
            dq = {c: _diag_blocks(_tn(dsb[c], kp[c])) for c in chains}
            dk = {c: _nn(dsb[c], qbd[c]) for c in chains}
            dv = {c: _nn(pn[c].astype(BF16), dobd[c]) for c in chains}
            for u in nu:
                dq_ref[0, qrows[u], :] = jnp.concatenate([dq[(u, p)] for p in pairs], axis=1)
                dk_ref[0, krows[u], :] += jnp.concatenate([dk[(u, p)] for p in pairs], axis=1)
                dv_ref[0, krows[u], :] += jnp.concatenate([dv[(u, p)] for p in pairs], axis=1)
                for p in pairs:
                    db_ref[p, brows[u], :] += dst[(u, p)]
            return carry

        lax.fori_loop(0, rows // NA_BWD_ROWS, step, 0)

    q, k, v, bs, out = _na_specs(S, kw_n, lambda g, b: (b, g))
    hs = jax.ShapeDtypeStruct((B, S, NA_WIDTH), F32)
    return _hosted_call(body, "na_bwd", (NA_PAIRS // NA_STEP_PAIRS, B), [q, k, v, bs, out], [out, out, out, bs],
                        [hs, hs, hs, jax.ShapeDtypeStruct((NA_PAIRS, N_DR * GRID_W, 128), F32)], (qkv, qkv, qkv, bias, doa), tasks)


SW_PAIRS = SW_HEADS // 2


def _sw_band(n, S):
    kw_n = 3 * SW_BLOCK
    start = pl.multiple_of(jnp.clip(n * SW_BLOCK - SW_BLOCK, 0, S - kw_n), SW_BLOCK)
    kpos = start + lax.broadcasted_iota(jnp.int32, (kw_n, SW_BLOCK), 0)
    qpos = n * SW_BLOCK + lax.broadcasted_iota(jnp.int32, (kw_n, SW_BLOCK), 1)
    return start, jnp.abs(qpos - kpos) <= SW_WINDOW


def _kv_halves(t):
    left = lax.broadcasted_iota(jnp.int32, t.shape, 1) < HEAD_DIM
    swapped = pltpu.roll(t, HEAD_DIM, axis=1)
    zero = jnp.zeros_like(t)
    return {(0, 0): jnp.where(left, t, zero), (0, 1): jnp.where(left, zero, swapped),
            (1, 0): jnp.where(left, swapped, zero), (1, 1): jnp.where(left, zero, t)}


def _sw_probs(st, ok, sk):
    st = jnp.where(ok, st, NEG)
    m = jnp.maximum(jnp.max(st, axis=0, keepdims=True), sk)
    e = jnp.exp(st - m)
    esk = jnp.exp(sk - m)
    inv = 1.0 / (jnp.sum(e, axis=0, keepdims=True) + esk)
    return e * inv, esk * inv


def _sw_specs(S):
    q = pl.BlockSpec((1, S, SW_WIDTH), lambda b, *_: (b, 0, ROPE_LO // SW_WIDTH))
    k = pl.BlockSpec((1, S, SW_KV_WIDTH), lambda b, *_: (b, 0, (ROPE_LO + SW_WIDTH) // SW_KV_WIDTH))
    v = pl.BlockSpec((1, S, SW_KV_WIDTH), lambda b, *_: (b, 0, (ROPE_LO + ROPE_WIDTH) // SW_KV_WIDTH))
    return q, k, v


SW_FWD_SPLIT = 2


def _sw_fwd(sink, qkv, tasks=()):
    B, S, _ = qkv.shape
    kw_n = 3 * SW_BLOCK

    def body(sink_ref, q_ref, k_ref, v_ref, o_ref):
        def step(n, carry):
            start, ok = _sw_band(n, S)
            qrows = pl.ds(pl.multiple_of(n * SW_BLOCK, SW_BLOCK), SW_BLOCK)
            krows = pl.ds(start, kw_n)
            kh, vh = _kv_halves(k_ref[0, krows, :]), _kv_halves(v_ref[0, krows, :])
            heads = [(p, e) for p in range(SW_PAIRS) for e in range(2)]
            qp = [q_ref[0, qrows, pl.ds(p * 128, 128)] for p in range(SW_PAIRS)]
            kv_of = lambda p: p // (SW_PAIRS // SW_KV_HEADS)
            st = {(p, e): _nt(kh[(kv_of(p), e)], qp[p]) for p, e in heads}
            pn = {(p, e): _sw_probs(st[(p, e)], ok, sink_ref[2 * p + e])[0].astype(BF16) for p, e in heads}
            outs = [_tn(pn[(p, 0)], vh[(kv_of(p), 0)]) + _tn(pn[(p, 1)], vh[(kv_of(p), 1)]) for p in range(SW_PAIRS)]
            o_ref[0, qrows, :] = jnp.concatenate(outs, axis=1)
            return carry

        half = (S // SW_BLOCK) // SW_FWD_SPLIT
        lax.fori_loop(pl.program_id(1) * half, (pl.program_id(1) + 1) * half, step, 0)

    q, k, v = _sw_specs(S)
    return _hosted_call(
        body, "sw_fwd", (B, SW_FWD_SPLIT), [pl.BlockSpec(memory_space=pltpu.SMEM), q, k, v],
        [pl.BlockSpec((1, S, SW_WIDTH), lambda b, s: (b, 0, 0))], [jax.ShapeDtypeStruct((B, S, SW_WIDTH), F32)],
        (sink, qkv, qkv, qkv), tasks)


def _sw_bwd(sink, qkv, dob):
    B, S, _ = qkv.shape
    kw_n = 3 * SW_BLOCK

    fold_rows = 256

    def body(sink_ref, q_ref, k_ref, v_ref, do_ref, dq_ref, dk_ref, dv_ref, dsink_ref, dk_acc, dv_acc):
        @pl.when(pl.program_id(0) == 0)
        def _():
            dsink_ref[...] = jnp.zeros_like(dsink_ref)

        dk_acc[...] = jnp.zeros_like(dk_acc)
        dv_acc[...] = jnp.zeros_like(dv_acc)
        ppk = SW_PAIRS // SW_KV_HEADS

        def step(n, carry):
            start, ok = _sw_band(n, S)
            qrows = pl.ds(pl.multiple_of(n * SW_BLOCK, SW_BLOCK), SW_BLOCK)
            krows = pl.ds(start, kw_n)
            kh, vh = _kv_halves(k_ref[0, krows, :]), _kv_halves(v_ref[0, krows, :])
            heads = [(p, e) for p in range(SW_PAIRS) for e in range(2)]
            qp = [q_ref[0, qrows, pl.ds(p * 128, 128)] for p in range(SW_PAIRS)]
            dop = [do_ref[0, qrows, pl.ds(p * 128, 128)] for p in range(SW_PAIRS)]
            st = {(p, e): _nt(kh[(p // ppk, e)], qp[p]) for p, e in heads}
            dpt = {(p, e): _nt(vh[(p // ppk, e)], dop[p]) for p, e in heads}
            pnb, dsb = {}, {}
            for p, e in heads:
                pn, psink = _sw_probs(st[(p, e)], ok, sink_ref[2 * p + e])
                delta = jnp.sum(pn * dpt[(p, e)], axis=0, keepdims=True)
                dsb[(p, e)] = (pn * (dpt[(p, e)] - delta)).astype(BF16)
                pnb[(p, e)] = pn.astype(BF16)
                dsink_ref[2 * p + e:2 * p + e + 1, :] += -(psink * delta)
            dq_ref[0, qrows, :] = jnp.concatenate(
                [_tn(dsb[(p, 0)], kh[(p // ppk, 0)]) + _tn(dsb[(p, 1)], kh[(p // ppk, 1)]) for p in range(SW_PAIRS)], axis=1)
            left = lax.broadcasted_iota(jnp.int32, (kw_n, 128), 1) < HEAD_DIM
            dks, dvs = [], []
            for kv in range(SW_KV_HEADS):
                dk = dv = None
                for p in range(kv * ppk, (kv + 1) * ppk):
                    dk_p = jnp.where(left, _nn(dsb[(p, 0)], qp[p]), _nn(dsb[(p, 1)], qp[p]))
                    dv_p = jnp.where(left, _nn(pnb[(p, 0)], dop[p]), _nn(pnb[(p, 1)], dop[p]))
                    dk = dk_p if dk is None else dk + dk_p
                    dv = dv_p if dv is None else dv + dv_p
                dks.append(dk)
                dvs.append(dv)
            dk_acc[krows, :] += jnp.concatenate(dks, axis=1)
            dv_acc[krows, :] += jnp.concatenate(dvs, axis=1)
            return carry

        lax.fori_loop(0, S // SW_BLOCK, step, 0)

        def fold(i, carry):
            rows = pl.ds(pl.multiple_of(i * fold_rows, fold_rows), fold_rows)
            left = lax.broadcasted_iota(jnp.int32, (fold_rows, 128), 1) < HEAD_DIM
            for acc, out_ref in ((dk_acc, dk_ref), (dv_acc, dv_ref)):
                a, b = acc[rows, 0:128], acc[rows, 128:256]
                out_ref[0, rows, :] = jnp.where(left, a + pltpu.roll(a, HEAD_DIM, axis=1), b + pltpu.roll(b, HEAD_DIM, axis=1))
            return carry

        lax.fori_loop(0, S // fold_rows, fold, 0)

        @pl.when(pl.program_id(0) == B - 1)
        def _():
            dsink_ref[...] = jnp.broadcast_to(jnp.sum(dsink_ref[...], axis=1, keepdims=True), dsink_ref.shape)

    q, k, v = _sw_specs(S)
    qo = pl.BlockSpec((1, S, SW_WIDTH), lambda b: (b, 0, 0))
    ko = pl.BlockSpec((1, S, SW_KV_WIDTH), lambda b: (b, 0, 0))
    return pl.pallas_call(
        body, name="sw_bwd", grid=(B,),
        in_specs=[pl.BlockSpec(memory_space=pltpu.SMEM), q, k, v, qo],
        out_specs=[qo, ko, ko, _full((SW_HEADS, 128))],
        out_shape=[jax.ShapeDtypeStruct((B, S, SW_WIDTH), F32), jax.ShapeDtypeStruct((B, S, SW_KV_WIDTH), F32),
                   jax.ShapeDtypeStruct((B, S, SW_KV_WIDTH), F32), jax.ShapeDtypeStruct((SW_HEADS, 128), F32)],
        scratch_shapes=[pltpu.VMEM((S, 2 * SW_KV_WIDTH), F32), pltpu.VMEM((S, 2 * SW_KV_WIDTH), F32)],
        compiler_params=_params("arbitrary"),
    )(sink, qkv, qkv, qkv, dob)


def _pack_sum_adamw(packs, params):
    W = packs.shape[1]
    n_p = len(params)

    def body(p_ref, *refs):
        ins, tot_ref, outs = refs[:3 * n_p], refs[3 * n_p], refs[3 * n_p + 1:]
        tot = p_ref[0:8, :]
        for d in range(1, N_DEV):
            tot = tot + p_ref[8 * d:8 * d + 8, :]
        tot_ref[...] = tot
        for i, (w, _, _, rows, off) in enumerate(params):
            n = w.shape[1]
            g = tot[rows[0]:rows[0] + 1, off:off + n]
            for r in rows[1:]:
                g = g + tot[r:r + 1, off:off + n]
            w_ref, m_ref, v_ref = ins[3 * i:3 * i + 3]
            g_ref, d_ref, nm_ref, nv_ref = outs[4 * i:4 * i + 4]
            g_ref[...] = g
            d_ref[...], nm_ref[...], nv_ref[...] = _adam_update(w_ref[...], g, m_ref[...], v_ref[...])

    res = pl.pallas_call(
        body, name="small_adamw",
        out_shape=[jax.ShapeDtypeStruct((8, W), F32)] + [jax.ShapeDtypeStruct(p[0].shape, F32) for p in params for _ in range(4)],
        compiler_params=pltpu.CompilerParams(vmem_limit_bytes=VMEM_LIMIT),
    )(packs, *[a for p in params for a in p[:3]])
    return res[0], [res[1 + 4 * i:5 + 4 * i] for i in range(n_p)]


def _adam_update(w, g, m, v):
    c1 = 1.0 - ADAM_B1 ** ADAM_STEP
    c2 = 1.0 - ADAM_B2 ** ADAM_STEP
    nm = ADAM_B1 * m + (1.0 - ADAM_B1) * g
    nv = ADAM_B2 * v + (1.0 - ADAM_B2) * (g * g)
    return -ADAM_LR * ((nm / c1) / (jnp.sqrt(nv / c2) + ADAM_EPS) + ADAM_WD * w), nm, nv


def _adamw(w, g, m, v, name):
    def body(w_ref, g_ref, m_ref, v_ref, d_ref, nm_ref, nv_ref):
        d_ref[...], nm_ref[...], nv_ref[...] = _adam_update(w_ref[...], g_ref[...], m_ref[...], v_ref[...])

    s = jax.ShapeDtypeStruct(w.shape, F32)
    return pl.pallas_call(body, name=name, out_shape=[s, s, s],
                          compiler_params=pltpu.CompilerParams(vmem_limit_bytes=VMEM_LIMIT))(w, g, m, v)


def _sum_adamw_rows(R):
    return max(r for r in range(16, min(R, 256) + 1, 16) if R % r == 0)


def _sum_adamw_steps(R):
    return R // _sum_adamw_rows(R)


def _sum_adamw(own, recvb, w, m, v, name, tasks=()):
    R, C = own.shape
    rc = _sum_adamw_rows(R)

    def body(own_ref, r_ref, w_ref, m_ref, v_ref, g_ref, d_ref, nm_ref, nv_ref):
        g = own_ref[...]
        for j in range(3):
            g = g + r_ref[j].astype(F32)
        g_ref[...] = g
        d_ref[...], nm_ref[...], nv_ref[...] = _adam_update(w_ref[...], g, m_ref[...], v_ref[...])

    blk = pl.BlockSpec((rc, C), lambda i: (i, 0))
    s = jax.ShapeDtypeStruct((R, C), F32)
    return _hosted_call(body, name, (R // rc,), [blk, pl.BlockSpec((3, rc, C), lambda i: (0, i, 0)), blk, blk, blk],
                        [blk, blk, blk, blk], [s, s, s, s], (own, recvb, w, m, v), tasks)


def _by_device(dw):
    return dw.reshape(N_DEV, dw.shape[0] // N_DEV, dw.shape[1])


def _local_step(x, mod, g_attn, w_in, bias, sw_sink, g_na_out, g_sw_out, w_out, g_ffn, w_up, conv_w, conv_b, w_down,
                g_final, target, sharded):
    B, S, D = x.shape
    T = B * S
    x2d = x.reshape(T, D)
    mod3 = mod.reshape(B, 6, D)
    cos_t, sin_t = _rope_tables(S)
    sink = sw_sink.reshape(SW_HEADS)
    n_tiles = T // TOKEN_TILE
    full = lambda g: g.reshape(N_DEV * g.shape[1], g.shape[2])

    if sharded:
        half = w_up.shape[0] // 2
        tasks = [_gather_task(w_out, n_tiles // 2), _gather_task(w_up, n_tiles - 1, rows=(0, half))]
    (h, qkv), got = _attn_in(x2d, mod3, g_attn, w_in, cos_t, sin_t, S, tasks if sharded else [])
    if sharded:
        w_out, w_up_part = full(got[0][0]), got[1][0]
    qkv3 = qkv.reshape(B, S, IN_WIDTH)
    na_steps = B * (NA_PAIRS // NA_STEP_PAIRS)
    (oa,), got = _na_fwd(qkv3, bias, [_gather_task(w_up, na_steps - 1, rows=(half, half), into=w_up_part)] if sharded else [])
    if sharded:
        w_up = full(got[0][0])
    oa = oa.reshape(T, NA_WIDTH)
    (ob,), got = _sw_fwd(sink, qkv3, [_gather_task(w_down, B * SW_FWD_SPLIT - 1)] if sharded else [])
    if sharded:
        w_down = full(got[0][0])
    ob = ob.reshape(T, SW_WIDTH)
    mixin, mix, x1 = _attn_out(oa, ob, x2d, mod3, g_na_out, g_sw_out, w_out, S)
    h2, val, gt = _ffn_up(x1, mod3, g_ffn, w_up, S)
    a, act, vd, dx2, df, gstat_f, bstat_f = _ffn_down(gt, val, conv_w, conv_b, w_down, x1, mod3, g_final, target.reshape(T, D), B, S)
    F = val.shape[1]

    dw_down = _matmul_tn(a, df, "dw_down")
    (dval, dgc, cstat), got = _ffn_down_bwd(df, w_down, act, vd, [_swap_task(_by_device(dw_down))] if sharded else [])
    if sharded:
        send_down, own_down = _chip_sums(_by_device(dw_down), got[0][0])
    (du, dx1, dmix, gstat_u, bstat_u, cstat_w), got = _ffn_up_bwd(dgc, dval, gt, conv_w, w_up, x1, mod3, g_ffn, dx2, mix, B, S,
                                                                  [_exchange_task(send_down)] if sharded else [])
    if sharded:
        dw_down = (own_down, got[0][0])
    dw_up = _matmul_tn(du, h2, "dw_up", tm=F)
    dw_out = _matmul_tn(mixin, dmix, "dw_out")
    (doa, dob, gstat_o), got = _attn_out_bwd(dmix, w_out, oa, ob, g_na_out, g_sw_out,
                                             [_swap_task(_by_device(dw_up)), _swap_task(_by_device(dw_out))] if sharded else [])
    if sharded:
        send_up, own_up = _chip_sums(_by_device(dw_up), got[0][0])
        send_out, own_out = _chip_sums(_by_device(dw_out), got[1][0])
    (dqa, dka, dva, dbt), got = _na_bwd(qkv3, bias, doa.reshape(B, S, NA_WIDTH),
                                        [_exchange_task(send_up), _exchange_task(send_out)] if sharded else [])
    if sharded:
        dw_up, dw_out = (own_up, got[0][0]), (own_out, got[1][0])
    dqb, dkb, dvb, dsink = _sw_bwd(sink, qkv3, dob.reshape(B, S, SW_WIDTH))
    r2 = lambda t: t.reshape(T, t.shape[-1])
    grad_x, dproj, gstat_i, bstat_i = _attn_in_bwd(r2(dqa), r2(dka), r2(dva), r2(dqb), r2(dkb), r2(dvb), cos_t, sin_t, w_in, x2d, mod3,
                                                   g_attn, dx1, B, S)
    dw_in = _matmul_tn(dproj, h, "dw_in")

    dmod = jnp.stack([bstat_i[:, 0], bstat_i[:, 1], bstat_u[:, 2], bstat_u[:, 0], bstat_u[:, 1], bstat_f[:, 0]], axis=1)
    small = dict(g_attn=gstat_i[0], g_ffn=gstat_u[0], g_final=gstat_f[0], loss=gstat_f[1, 0], g_na_out=gstat_o[0], g_sw_out=gstat_o[1],
                 sw_sink=dsink[:, 0], conv_b=cstat[0], conv_w=cstat_w[1:4], dbt=dbt)
    return grad_x.reshape(B, S, D), dict(w_in=dw_in, w_out=dw_out, w_up=dw_up, w_down=dw_down), dmod, small


def _pad_lanes(v, w):
    return jnp.pad(v, (0, w - v.shape[0]))


def kernel(x, c, w_ada, b_ada, g_attn, w_in, na_rpb, sw_sink, g_na_out, g_sw_out, w_out, g_ffn, w_up, conv_w, conv_b, w_down, g_final, loss_target, m_w_ada, m_b_ada, m_g_attn, m_w_in, m_na_rpb, m_sw_sink, m_g_na_out, m_g_sw_out, m_w_out, m_g_ffn, m_w_up, m_conv_w, m_conv_b, m_w_down, m_g_final, v_w_ada, v_b_ada, v_g_attn, v_w_in, v_na_rpb, v_sw_sink, v_g_na_out, v_g_sw_out, v_w_out, v_g_ffn, v_w_up, v_conv_w, v_conv_b, v_w_down, v_g_final):
    B, S, D = x.shape
    me = 4 * lax.axis_index("x") + 2 * lax.axis_index("y") + lax.axis_index("c")
    ada_c = w_ada.shape[2]
    F_l = conv_w.shape[2]

    cw_l = jnp.pad(conv_w[0], ((0, 8 - conv_w.shape[1]), (0, 0)))
    c_l = jnp.pad(c, ((0, 8 - B), (0, 0)))
    tr = {"w_in", "w_up"}
    w_in_t = jnp.transpose(w_in[0])
    shards = dict(w_out=w_out[0].astype(BF16), w_up=jnp.transpose(w_up[0]).astype(BF16), w_down=w_down[0].astype(BF16))

    b_ada_l = lax.dynamic_slice(b_ada, (0, me * ada_c), (1, ada_c))
    slabs, mod_all, w_in_all = _ada_fwd(jnp.concatenate([c_l, cw_l], axis=1), w_ada[0], b_ada_l, w_in_t.astype(BF16), B)
    c_all = slabs[:, :, :D].reshape(N_DEV * 8, D)
    conv_w_f = jnp.transpose(slabs[:, :3, D:], (1, 0, 2)).reshape(3, N_DEV * F_l)
    mod_mine = lax.dynamic_slice(mod_all, (0, me * B, 0), (N_DEV, B, ada_c))
    mod = jnp.transpose(mod_mine, (1, 0, 2)).reshape(B, N_DEV * ada_c)
    w_in_f = w_in_all.reshape(N_DEV * w_in_t.shape[0], D)

    bias = _na_bias_table(na_rpb[0])

    grad_x, dw, dmod, small = _local_step(x, mod, g_attn, w_in_f, bias, sw_sink, g_na_out, g_sw_out, shards["w_out"], g_ffn,
                                          shards["w_up"], conv_w_f, conv_b, shards["w_down"], g_final.reshape(1, D), loss_target,
                                          sharded=True)
    drpb = _na_bias_grad(small["dbt"])

    row2 = jnp.concatenate([small["g_attn"], small["g_ffn"], small["g_final"], small["g_na_out"], small["g_sw_out"],
                            _pad_lanes(small["sw_sink"], 128), _pad_lanes(small["loss"].reshape(1), 128)])
    rows = [dmod.reshape(B, 6 * D)[0], dmod.reshape(B, 6 * D)[1], _pad_lanes(row2, PACK_W), _pad_lanes(small["conv_b"], PACK_W),
            _pad_lanes(drpb, PACK_W)] + [_pad_lanes(small["conv_w"][k], PACK_W) for k in range(3)]
    weights = dict(w_ada=w_ada, b_ada=b_ada, g_attn=g_attn, w_in=w_in, na_rpb=na_rpb, sw_sink=sw_sink, g_na_out=g_na_out,
                   g_sw_out=g_sw_out, w_out=w_out, g_ffn=g_ffn, w_up=w_up, conv_w=conv_w, conv_b=conv_b, w_down=w_down, g_final=g_final)
    ms = dict(w_ada=m_w_ada, b_ada=m_b_ada, g_attn=m_g_attn, w_in=m_w_in, na_rpb=m_na_rpb, sw_sink=m_sw_sink, g_na_out=m_g_na_out,
              g_sw_out=m_g_sw_out, w_out=m_w_out, g_ffn=m_g_ffn, w_up=m_w_up, conv_w=m_conv_w, conv_b=m_conv_b, w_down=m_w_down, g_final=m_g_final)
    vs = dict(w_ada=v_w_ada, b_ada=v_b_ada, g_attn=v_g_attn, w_in=v_w_in, na_rpb=v_na_rpb, sw_sink=v_sw_sink, g_na_out=v_g_na_out,
              g_sw_out=v_g_sw_out, w_out=v_w_out, g_ffn=v_g_ffn, w_up=v_w_up, conv_w=v_conv_w, conv_b=v_conv_b, w_down=v_w_down, g_final=v_g_final)
    names = list(weights)
    grads, deltas, new_m, new_v = {}, {}, {}, {}
    flat = lambda t: t.reshape(1, -1)

    def shard2d(nm):
        if nm in tr:
            return (lambda t: jnp.transpose(t[0])), (lambda t: jnp.transpose(t)[None])
        return (lambda t: t[0]), (lambda t: t[None])

    def finish_sum(nm, own, recvb, tasks=()):
        r, back = shard2d(nm)
        (g2, d_, m_, v_), got = _sum_adamw(own, recvb, r(weights[nm]), r(ms[nm]), r(vs[nm]), "adamw_" + nm, tasks)
        grads[nm], deltas[nm], new_m[nm], new_v[nm] = back(g2), back(d_), back(m_), back(v_)
        return got

    g8_in = _by_device(dw["w_in"])
    got = finish_sum("w_down", *dw["w_down"], [_swap_task(g8_in)])
    send_in, own_in = _chip_sums(g8_in, got[0][0])
    n_up = _sum_adamw_steps(dw["w_up"][0].shape[0])
    got = finish_sum("w_up", *dw["w_up"], [_exchange_task(send_in), _gather_task(jnp.stack(rows), n_up - 1)])
    packs = got[1][0]
    finish_sum("w_in", own_in, got[0][0])
    finish_sum("w_out", *dw["w_out"])

    where = dict(b_ada=((0, 1), 0), g_attn=((2,), 0), g_ffn=((2,), D), g_final=((2,), 2 * D), g_na_out=((2,), 3 * D),
                 g_sw_out=((2,), 3 * D + NA_WIDTH), sw_sink=((2,), 3 * D + NA_WIDTH + SW_WIDTH), conv_b=((3,), 0), na_rpb=((4,), 0))
    tot, small_out = _pack_sum_adamw(packs.reshape(N_DEV * 8, PACK_W),
                                     [(flat(weights[n]), flat(ms[n]), flat(vs[n])) + where[n] for n in where])
    for n, (g_, d_, m_, v_) in zip(where, small_out):
        shp = weights[n].shape
        grads[n], deltas[n], new_m[n], new_v[n] = g_.reshape(shp), d_.reshape(shp), m_.reshape(shp), v_.reshape(shp)
    loss = tot[2, 3 * D + NA_WIDTH + SW_WIDTH + 128]

    dmod_cols = lax.dynamic_slice(packs.reshape(N_DEV * 8, PACK_W), (0, me * ada_c), (N_DEV * 8, ada_c))
    for nm, g2 in (("w_ada", _ada_bwd(c_all, dmod_cols)), ("conv_w", lax.dynamic_slice(tot[5:8], (0, me * F_l), (3, F_l)))):
        r, back = shard2d(nm)
        d_, m_, v_ = _adamw(r(weights[nm]), g2, r(ms[nm]), r(vs[nm]), "adamw_" + nm)
        grads[nm], deltas[nm], new_m[nm], new_v[nm] = back(g2), back(d_), back(m_), back(v_)
    return (loss, grad_x, *[grads[n] for n in names], *[deltas[n] for n in names], *[new_m[n] for n in names],
            *[new_v[n] for n in names])
```

```python
import functools

import numpy as np
import jax
import jax.numpy as jnp
from jax import lax
from jax.experimental import pallas as pl
from jax.experimental.pallas import tpu as pltpu

F32, BF16 = jnp.float32, jnp.bfloat16
MESH_ID = pl.DeviceIdType.MESH
N_DEV = 8

HEAD_DIM = 64
NA_HEADS = 8
SW_HEADS = 8
SW_KV_HEADS = 2
SW_GROUP = SW_HEADS // SW_KV_HEADS
NA_WIDTH = NA_HEADS * HEAD_DIM
SW_WIDTH = SW_HEADS * HEAD_DIM
SW_KV_WIDTH = SW_KV_HEADS * HEAD_DIM
ROPE_WIDTH = SW_WIDTH + SW_KV_WIDTH
IN_WIDTH = 3 * NA_WIDTH + SW_WIDTH + 2 * SW_KV_WIDTH
ROPE_LO = 3 * NA_WIDTH
GRID_W = 64
NA_ROWS_MAX = 8
NA_COLS = 16
N_DR = 2 * NA_ROWS_MAX - 1
N_DC = 2 * NA_COLS - 1
SW_WINDOW = 128
SW_BLOCK = 128
ROPE_THETA = 10000.0
EPS = 1e-6
NEG = -1e30
Q_SCALE = HEAD_DIM ** -0.5

ADAM_LR = 0.001
ADAM_B1 = 0.9
ADAM_B2 = 0.999
ADAM_EPS = 1e-08
ADAM_WD = 0.01
ADAM_STEP = 10

TOKEN_TILE = 256
WIDE_TILE = 512
VMEM_LIMIT = 56 * 1024 * 1024

PACK_W = 6144


def _nn(a, b):
    return jnp.dot(a, b, preferred_element_type=F32)


def _nt(a, b):
    return lax.dot_general(a, b, (((1,), (1,)), ((), ())), preferred_element_type=F32)


def _tn(a, b):
    return lax.dot_general(a, b, (((0,), (0,)), ((), ())), preferred_element_type=F32)


def _rms(x):
    r = lax.rsqrt(jnp.mean(x * x, axis=-1, keepdims=True) + EPS)
    return x * r, r


def _rms_bwd(xn, r, gy):
    return r * (gy - xn * jnp.mean(xn * gy, axis=-1, keepdims=True))


def _params(*sem):
    return pltpu.CompilerParams(dimension_semantics=sem, vmem_limit_bytes=VMEM_LIMIT)


def _full(shape):
    n = len(shape)
    return pl.BlockSpec(shape, lambda *_: (0,) * n)


def _mesh_pos():
    return lax.axis_index("x"), lax.axis_index("y"), lax.axis_index("c")


def _row_chunk(r):
    for rc in (128, 64, 32, 16):
        if r % rc == 0:
            return rc
    raise ValueError(f"rows {r} not a multiple of 16")


class _Task:
    def __init__(self, inputs, out_shapes, sems, start, finish, mid=None, mid_step=None, alias=None):
        self.inputs, self.out_shapes, self.sems = list(inputs), list(out_shapes), list(sems)
        self.start, self.finish, self.mid, self.mid_step = start, finish, mid, mid_step
        self.alias = alias


def _hosted_call(body, name, grid, in_specs, out_specs, out_shape, operands, tasks, scratch_shapes=()):
    n_in, n_out, n_scr = len(in_specs), len(out_specs), len(scratch_shapes)
    t_in = [len(t.inputs) for t in tasks]
    t_out = [len(t.out_shapes) for t in tasks]
    t_sem = [len(t.sems) for t in tasks]
    n_steps = int(np.prod(grid))

    def wrapped(*refs):
        ins, rest = refs[:n_in], refs[n_in:]
        task_ins, rest = rest[:sum(t_in)], rest[sum(t_in):]
        outs, rest = rest[:n_out], rest[n_out:]
        task_outs, rest = rest[:sum(t_out)], rest[sum(t_out):]
        scr, task_sems = rest[:n_scr], rest[n_scr:]
        step = pl.program_id(0)
        for ax in range(1, len(grid)):
            step = step * grid[ax] + pl.program_id(ax)
        parts = []
        oi = oo = os_ = 0
        for t, a, b, c in zip(tasks, t_in, t_out, t_sem):
            parts.append((t, task_ins[oi:oi + a], task_outs[oo:oo + b], task_sems[os_:os_ + c]))
            oi, oo, os_ = oi + a, oo + b, os_ + c
        for t, ti, to, ts in parts:
            pl.when(step == 0)(functools.partial(t.start, ti, to, ts))
            if t.mid is not None:
                pl.when(step == t.mid_step)(functools.partial(t.mid, ti, to, ts))
        body(*ins, *outs, *scr)
        for t, ti, to, ts in parts:
            pl.when(step == n_steps - 1)(functools.partial(t.finish, ti, to, ts))

    hbm = pl.BlockSpec(memory_space=pl.ANY)
    aliases, oi, oo = {}, n_in, n_out
    for t, a, b in zip(tasks, t_in, t_out):
        if t.alias is not None:
            aliases[oi + t.alias[0]] = oo + t.alias[1]
        oi, oo = oi + a, oo + b
    res = pl.pallas_call(
        wrapped, name=name, grid=grid,
        in_specs=list(in_specs) + [hbm] * sum(t_in),
        out_specs=list(out_specs) + [hbm] * sum(t_out),
        out_shape=list(out_shape) + [s for t in tasks for s in t.out_shapes],
        scratch_shapes=list(scratch_shapes) + [s for t in tasks for s in t.sems],
        input_output_aliases=aliases,
        compiler_params=_params(*(["arbitrary"] * len(grid))),
    )(*operands, *[a for t in tasks for a in t.inputs])
    own, extra = res[:n_out], res[n_out:]
    per_task, o = [], 0
    for b in t_out:
        per_task.append(extra[o:o + b])
        o += b
    return own, per_task


def _gather_task(shard, mid_step, rows=None, into=None):
    lo, n = (0, shard.shape[0]) if rows is None else rows

    def parts(ins, outs, sems):
        x_ref, out_ref, (send_sems, recv_sems, local_sem) = ins[0], outs[0], sems
        x_, y_, c_ = _mesh_pos()
        me, sibling = (x_, y_, c_), (x_, y_, 1 - c_)
        chips = [(1 - x_, y_), (x_, 1 - y_), (1 - x_, 1 - y_)]
        x_ref = x_ref.at[pl.ds(lo, n)]

        def rows(px, py, pc):
            return out_ref.at[4 * px + 2 * py + pc, pl.ds(lo, n)]

        def copy(k, block, to, src=None):
            return pltpu.make_async_remote_copy(
                src_ref=rows(*block) if src is None else src, dst_ref=rows(*block),
                send_sem=send_sems.at[k], recv_sem=recv_sems.at[k], device_id=to, device_id_type=MESH_ID)

        return dict(
            mine=lambda: pltpu.make_async_copy(x_ref, rows(*me), local_sem),
            first=lambda: [copy(0, me, sibling, src=x_ref)] + [copy(1 + j, me, (*chip, c_), src=x_ref) for j, chip in enumerate(chips)],
            passed=lambda: [copy(4 + j, (*chip, c_), sibling) for j, chip in enumerate(chips)],
            landed=lambda: [copy(1 + j, (*chip, c_), me) for j, chip in enumerate(chips)],
            last=lambda: [copy(0, sibling, me)] + [copy(4 + j, (*chip, 1 - c_), me) for j, chip in enumerate(chips)])

    def start(ins, outs, sems):
        p = parts(ins, outs, sems)
        p["mine"]().start()
        for cp in p["first"]():
            cp.start()

    def mid(ins, outs, sems):
        p = parts(ins, outs, sems)
        for cp, fw in zip(p["landed"](), p["passed"]()):
            cp.wait_recv()
            fw.start()

    def finish(ins, outs, sems):
        p = parts(ins, outs, sems)
        for cp in p["last"]():
            cp.wait_recv()
        for cp in p["first"]() + p["passed"]():
            cp.wait_send()
        p["mine"]().wait()

    return _Task([shard] if into is None else [shard, into], [jax.ShapeDtypeStruct((N_DEV,) + shard.shape, shard.dtype)],
                 [pltpu.SemaphoreType.DMA((7,)), pltpu.SemaphoreType.DMA((7,)), pltpu.SemaphoreType.DMA],
                 start, finish, mid, mid_step, alias=None if into is None else (1, 0))


def _swap_task(g8):
    _, R, C = g8.shape

    def copies(ins, outs, sems):
        (g_ref,), (recv_ref,), (ss, rs) = ins, outs, sems
        x_, y_, c_ = _mesh_pos()
        return [pltpu.make_async_remote_copy(src_ref=g_ref.at[2 * k + (1 - c_)], dst_ref=recv_ref.at[k], send_sem=ss.at[k],
                                             recv_sem=rs.at[k], device_id=(x_, y_, 1 - c_), device_id_type=MESH_ID)
                for k in range(4)]

    def start(ins, outs, sems):
        for cp in copies(ins, outs, sems):
            cp.start()

    def finish(ins, outs, sems):
        cps = copies(ins, outs, sems)
        for cp in cps:
            cp.wait_recv()
        for cp in cps:
            cp.wait_send()

    return _Task([g8], [jax.ShapeDtypeStruct((4, R, C), g8.dtype)],
                 [pltpu.SemaphoreType.DMA((4,)), pltpu.SemaphoreType.DMA((4,))], start, finish)


def _chip_sums(g8, recva):
    _, R, C = g8.shape
    rc = _row_chunk(R)

    def body(g_ref, a_ref, send_ref, own_ref):
        x_, y_, c_ = _mesh_pos()
        chips = [(1 - x_, y_), (x_, 1 - y_), (1 - x_, 1 - y_), (x_, y_)]

        def chunk(i, carry):
            rows = pl.ds(pl.multiple_of(i * rc, rc), rc)
            for j, (tx, ty) in enumerate(chips):
                k = 2 * tx + ty
                s = g_ref[2 * k + c_, rows, :].astype(F32) + a_ref[k, rows, :].astype(F32)
                if j < 3:
                    send_ref[j, rows, :] = s.astype(BF16)
                else:
                    own_ref[rows, :] = s
            return carry

        lax.fori_loop(0, R // rc, chunk, 0)

    return pl.pallas_call(body, name="chip_sums", out_shape=[jax.ShapeDtypeStruct((3, R, C), BF16), jax.ShapeDtypeStruct((R, C), F32)],
                          compiler_params=pltpu.CompilerParams(vmem_limit_bytes=VMEM_LIMIT))(g8, recva)


def _exchange_task(sendb):
    def copies(ins, outs, sems):
        (s_ref,), (recv_ref,), (ss, rs) = ins, outs, sems
        x_, y_, c_ = _mesh_pos()
        flips = [(1 - x_, y_), (x_, 1 - y_), (1 - x_, 1 - y_)]
        return [pltpu.make_async_remote_copy(src_ref=s_ref.at[j], dst_ref=recv_ref.at[j], send_sem=ss.at[j], recv_sem=rs.at[j],
                                             device_id=(tx, ty, c_), device_id_type=MESH_ID) for j, (tx, ty) in enumerate(flips)]

    def start(ins, outs, sems):
        for cp in copies(ins, outs, sems):
            cp.start()

    def finish(ins, outs, sems):
        cps = copies(ins, outs, sems)
        for cp in cps:
            cp.wait_recv()
        for cp in cps:
            cp.wait_send()

    return _Task([sendb], [jax.ShapeDtypeStruct(sendb.shape, sendb.dtype)],
                 [pltpu.SemaphoreType.DMA((3,)), pltpu.SemaphoreType.DMA((3,))], start, finish)


def _silu(v):
    return v * (1.0 / (1.0 + jnp.exp(-v)))


def _ada_fwd(c_slab, w_ada_l, b_ada_l, w_in_shard, n_seq):
    W = c_slab.shape[1]
    D, cols = w_ada_l.shape
    n_rows = N_DEV * n_seq
    t_c, t_w = _gather_task(c_slab, 0), _gather_task(w_in_shard, 0)
    t_m = _gather_task(jax.ShapeDtypeStruct((n_rows, cols), F32), 0)

    def body(c_ref, w_ref, b_ref, ws_ref, slabs_ref, mod_ref, win_ref, c_vm, m_vm, copy_sem, *sems):
        sc, sw, sm = sems[0:3], sems[3:6], sems[6:9]
        t_c.start((c_ref,), (slabs_ref,), sc)
        t_w.start((ws_ref,), (win_ref,), sw)
        t_c.mid((c_ref,), (slabs_ref,), sc)
        t_c.finish((c_ref,), (slabs_ref,), sc)
        cp = pltpu.make_async_copy(slabs_ref, c_vm, copy_sem)
        cp.start()
        cp.wait()
        c_all = c_vm[:, :, 0:D].reshape(N_DEV * 8, D)
        m64 = jnp.dot(_silu(c_all), w_ref[...], precision=lax.Precision.HIGHEST, preferred_element_type=F32) + b_ref[...]
        r = lax.broadcasted_iota(jnp.int32, (n_rows, N_DEV * 8), 0)
        c = lax.broadcasted_iota(jnp.int32, (n_rows, N_DEV * 8), 1)
        pick = jnp.where(c == 8 * (r // n_seq) + r % n_seq, 1.0, 0.0)
        m_vm[...] = jnp.dot(pick, m64, precision=lax.Precision.HIGHEST, preferred_element_type=F32)
        t_m.start((m_vm,), (mod_ref,), sm)
        t_w.mid((ws_ref,), (win_ref,), sw)
        t_m.mid((m_vm,), (mod_ref,), sm)
        t_m.finish((m_vm,), (mod_ref,), sm)
        t_w.finish((ws_ref,), (win_ref,), sw)

    hbm, vm = pl.BlockSpec(memory_space=pl.ANY), pl.BlockSpec(memory_space=pltpu.VMEM)
    return pl.pallas_call(
        body, name="ada_fwd", in_specs=[hbm, vm, vm, hbm], out_specs=[hbm, hbm, hbm],
        out_shape=t_c.out_shapes + t_m.out_shapes + t_w.out_shapes,
        scratch_shapes=[pltpu.VMEM((N_DEV, 8, W), F32), pltpu.VMEM((n_rows, cols), F32), pltpu.SemaphoreType.DMA]
        + t_c.sems + t_w.sems + t_m.sems,
        compiler_params=pltpu.CompilerParams(vmem_limit_bytes=VMEM_LIMIT),
    )(c_slab, w_ada_l, b_ada_l, w_in_shard)


def _ada_bwd(c_all, dmod_cols):
    def body(c_ref, d_ref, o_ref):
        o_ref[...] = lax.dot_general(_silu(c_ref[...]), d_ref[...], (((0,), (0,)), ((), ())),
                                     precision=lax.Precision.HIGHEST, preferred_element_type=F32)
    return pl.pallas_call(body, name="ada_bwd", out_shape=jax.ShapeDtypeStruct((c_all.shape[1], dmod_cols.shape[1]), F32),
                          compiler_params=pltpu.CompilerParams(vmem_limit_bytes=VMEM_LIMIT))(c_all, dmod_cols)


NA_PAIRS = NA_HEADS // 2
N_DR_PAD = 16


def _na_bias_table(na_rpb):
    rev = jnp.pad(jnp.flip(na_rpb, axis=2), ((0, 0), (0, N_DR_PAD - N_DR), (0, GRID_W - N_DC)))
    rev = jnp.transpose(rev.reshape(NA_PAIRS, 2, N_DR_PAD, GRID_W), (0, 2, 1, 3)).reshape(NA_PAIRS, N_DR_PAD, 128)

    def body(r_ref, o_ref):
        k = lax.broadcasted_iota(jnp.int32, (GRID_W, 128), 0)
        lane = lax.broadcasted_iota(jnp.int32, (GRID_W, 128), 1)
        q = lane % GRID_W
        cs = jnp.clip(q - NA_COLS // 2, 0, GRID_W - NA_COLS)
        ok = (k >= cs) & (k < cs + NA_COLS)
        left = lane < GRID_W
        for dr in range(N_DR):
            row = jnp.broadcast_to(r_ref[0, dr:dr + 1, :], (GRID_W, 128))
            r0 = jnp.where(left, row, 0.0)
            r1 = jnp.where(left, pltpu.roll(row, GRID_W, axis=1), 0.0)
            y0 = pltpu.roll(r0, 128 - (NA_COLS - 1), axis=1, stride=1, stride_axis=0)
            y1 = pltpu.roll(r1, GRID_W - (NA_COLS - 1), axis=1, stride=1, stride_axis=0)
            o_ref[0, dr * GRID_W:(dr + 1) * GRID_W, :] = jnp.where(ok, jnp.where(left, y0, y1), NEG)

    return pl.pallas_call(
        body, name="rpb_expand", grid=(NA_PAIRS,),
        in_specs=[pl.BlockSpec((1, N_DR_PAD, 128), lambda p: (p, 0, 0))],
        out_specs=pl.BlockSpec((1, N_DR * GRID_W, 128), lambda p: (p, 0, 0)),
        out_shape=jax.ShapeDtypeStruct((NA_PAIRS, N_DR * GRID_W, 128), F32),
        compiler_params=_params("parallel"),
    )(rev)


def _na_bias_grad(db):
    a = np.arange(128)
    flip = jnp.asarray(((a[:, None] // GRID_W == a[None, :] // GRID_W)
                        & (a[:, None] % GRID_W + a[None, :] % GRID_W == GRID_W - 1)).astype(np.float32))

    def body(d_ref, j_ref, o_ref):
        o_ref[...] = jnp.zeros_like(o_ref)
        for dr in range(N_DR):
            t = jnp.dot(d_ref[0, dr * GRID_W:(dr + 1) * GRID_W, :], j_ref[...], precision=lax.Precision.HIGHEST, preferred_element_type=F32)
            t = pltpu.roll(t, GRID_W + NA_COLS, axis=1, stride=1, stride_axis=0)
            o_ref[0, dr:dr + 1, :] = jnp.sum(t, axis=0, keepdims=True)

    rows = pl.pallas_call(
        body, name="rpb_reduce", grid=(NA_PAIRS,),
        in_specs=[pl.BlockSpec((1, N_DR * GRID_W, 128), lambda p: (p, 0, 0)), _full((128, 128))],
        out_specs=pl.BlockSpec((1, N_DR_PAD, 128), lambda p: (p, 0, 0)),
        out_shape=jax.ShapeDtypeStruct((NA_PAIRS, N_DR_PAD, 128), F32),
        compiler_params=_params("parallel"),
    )(db, flip)
    g = rows.reshape(NA_PAIRS, N_DR_PAD, 2, GRID_W)[:, :N_DR, :, :N_DC]
    return jnp.transpose(g, (0, 2, 1, 3)).reshape(-1)


def _rope_tables(S):
    half = HEAD_DIM // 2
    inv = np.float32(ROPE_THETA) ** (-np.arange(half, dtype=np.float32) / np.float32(half))
    ang = np.arange(S).astype(np.float32)[:, None] * inv[None, :]
    cos, sin = np.cos(ang).astype(np.float32), np.sin(ang).astype(np.float32)
    return jnp.asarray(np.tile(np.concatenate([cos, cos], axis=1), (1, 2))), jnp.asarray(np.tile(np.concatenate([-sin, sin], axis=1), (1, 2)))


def _rope_spec(tps, tm=TOKEN_TILE):
    return pl.BlockSpec((tm, 2 * HEAD_DIM), lambda i: (i % tps, 0))


def _rot_half(t):
    w = t.shape[1]
    lane = lax.broadcasted_iota(jnp.int32, t.shape, 1)
    return jnp.where((lane % HEAD_DIM) < HEAD_DIM // 2, pltpu.roll(t, w - HEAD_DIM // 2, axis=1),
                     pltpu.roll(t, HEAD_DIM // 2, axis=1))


def _tok_spec(w, tm=TOKEN_TILE):
    return pl.BlockSpec((tm, w), lambda i: (i, 0))


def _mod_spec(tps, d):
    return pl.BlockSpec((1, 6, d), lambda i: (i // tps, 0, 0))


def _bstat_spec(tps, w):
    return pl.BlockSpec((1, 8, w), lambda i: (i // tps, 0, 0))


def _attn_in(x2d, mod3, g_attn, w_in, cos_t, sin_t, S, tasks=(), tm=WIDE_TILE):
    T, D = x2d.shape
    tps = S // tm

    def body(x_ref, mod_ref, g_ref, w_ref, cos_ref, sin_ref, h_ref, qkv_ref):
        xn, _ = _rms(x_ref[...])
        h = (xn * g_ref[...]) * (1.0 + mod_ref[0, 1:2, :]) + mod_ref[0, 0:1, :]
        hb = h.astype(BF16)
        h_ref[...] = hb
        proj = _nt(hb, w_ref[...])
        rb = proj[:, ROPE_LO:ROPE_LO + ROPE_WIDTH]
        reps = (1, ROPE_WIDTH // (2 * HEAD_DIM))
        rb = rb * jnp.tile(cos_ref[...], reps) + _rot_half(rb) * jnp.tile(sin_ref[...], reps)
        qkv_ref[:, 0:NA_WIDTH] = (proj[:, 0:NA_WIDTH] * Q_SCALE).astype(BF16)
        qkv_ref[:, NA_WIDTH:ROPE_LO] = proj[:, NA_WIDTH:ROPE_LO].astype(BF16)
        qkv_ref[:, ROPE_LO:ROPE_LO + SW_WIDTH] = (rb[:, 0:SW_WIDTH] * Q_SCALE).astype(BF16)
        qkv_ref[:, ROPE_LO + SW_WIDTH:ROPE_LO + ROPE_WIDTH] = rb[:, SW_WIDTH:].astype(BF16)
        qkv_ref[:, ROPE_LO + ROPE_WIDTH:] = proj[:, ROPE_LO + ROPE_WIDTH:].astype(BF16)

    return _hosted_call(
        body, "attn_in", (T // tm,),
        [_tok_spec(D, tm), _mod_spec(tps, D), _full((1, D)), _full(w_in.shape), _rope_spec(tps, tm), _rope_spec(tps, tm)],
        [_tok_spec(D, tm), _tok_spec(IN_WIDTH, tm)],
        [jax.ShapeDtypeStruct((T, D), BF16), jax.ShapeDtypeStruct((T, IN_WIDTH), BF16)],
        (x2d, mod3, g_attn, w_in, cos_t, sin_t), tasks)


def _attn_out(oa, ob, x2d, mod3, g_na, g_sw, w_out, S, tm=WIDE_TILE):
    T, D = x2d.shape
    tps = S // tm

    def body(oa_ref, ob_ref, x_ref, mod_ref, gna_ref, gsw_ref, w_ref, mixin_ref, mix_ref, x1_ref):
        oan, _ = _rms(oa_ref[...])
        obn, _ = _rms(ob_ref[...])
        mixin = jnp.concatenate([oan * gna_ref[...], obn * gsw_ref[...]], axis=1).astype(BF16)
        mixin_ref[...] = mixin
        mix = _nn(mixin, w_ref[...])
        mix_ref[...] = mix
        x1_ref[...] = x_ref[...] + mod_ref[0, 2:3, :] * mix

    return pl.pallas_call(
        body, name="attn_out", grid=(T // tm,),
        in_specs=[_tok_spec(NA_WIDTH, tm), _tok_spec(SW_WIDTH, tm), _tok_spec(D, tm), _mod_spec(tps, D),
                  _full((1, NA_WIDTH)), _full((1, SW_WIDTH)), _full(w_out.shape)],
        out_specs=[_tok_spec(NA_WIDTH + SW_WIDTH, tm), _tok_spec(D, tm), _tok_spec(D, tm)],
        out_shape=[jax.ShapeDtypeStruct((T, NA_WIDTH + SW_WIDTH), BF16), jax.ShapeDtypeStruct((T, D), F32),
                   jax.ShapeDtypeStruct((T, D), F32)],
        compiler_params=_params("parallel"),
    )(oa, ob, x2d, mod3, g_na, g_sw, w_out)


def _ffn_up(x1, mod3, g_ffn, w_up, S, tm=WIDE_TILE):
    T, D = x1.shape
    F = w_up.shape[0] // 2
    tps = S // tm

    def body(x1_ref, mod_ref, g_ref, w_ref, h2_ref, val_ref, gt_ref):
        xn, _ = _rms(x1_ref[...])
        h2 = ((xn * g_ref[...]) * (1.0 + mod_ref[0, 4:5, :]) + mod_ref[0, 3:4, :]).astype(BF16)
        h2_ref[...] = h2
        u = _nt(h2, w_ref[...])
        val_ref[...] = u[:, :F].astype(BF16)
        gt_ref[...] = u[:, F:].astype(BF16)

    return pl.pallas_call(
        body, name="ffn_up", grid=(T // tm,),
        in_specs=[_tok_spec(D, tm), _mod_spec(tps, D), _full((1, D)), _full(w_up.shape)],
        out_specs=[_tok_spec(D, tm), _tok_spec(F, tm), _tok_spec(F, tm)],
        out_shape=[jax.ShapeDtypeStruct((T, D), BF16), jax.ShapeDtypeStruct((T, F), BF16), jax.ShapeDtypeStruct((T, F), BF16)],
        compiler_params=_params("parallel"),
    )(x1, mod3, g_ffn, w_up)


def _halo_specs(T, tps, w):
    per = TOKEN_TILE // 8
    prev = pl.BlockSpec((8, w), lambda i: (jnp.maximum(i * per - 1, 0), 0))
    nxt = pl.BlockSpec((8, w), lambda i: (jnp.minimum((i + 1) * per, T // 8 - 1), 0))
    return prev, nxt


def _seq_shifts(cur, before, after, ti, tps):
    tm = cur.shape[0]
    row = lax.broadcasted_iota(jnp.int32, cur.shape, 0)
    before = jnp.where(ti > 0, before.astype(F32), 0.0)
    after = jnp.where(ti < tps - 1, after.astype(F32), 0.0)
    return jnp.where(row == 0, before, pltpu.roll(cur, 1, axis=0)), jnp.where(row == tm - 1, after, pltpu.roll(cur, tm - 1, axis=0))


def _ffn_down(gt, val, conv_w, conv_b, w_down, x1, mod3, g_final, target, B, S):
    T, D = x1.shape
    F = gt.shape[1]
    tps = S // TOKEN_TILE
    prev, nxt = _halo_specs(T, tps, F)

    def body(gt_ref, prev_ref, next_ref, val_ref, cw_ref, cb_ref, w_ref, x1_ref, mod_ref, gf_ref, tgt_ref,
             a_ref, act_ref, vd_ref, dx2_ref, df_ref, gstat_ref, bstat_ref):
        i = pl.program_id(0)
        g = gt_ref[...].astype(F32)
        gprev, gnext = _seq_shifts(g, prev_ref[7:8, :], next_ref[0:1, :], i % tps, tps)
        gc = gprev * cw_ref[0:1, :] + g * cw_ref[1:2, :] + gnext * cw_ref[2:3, :] + cb_ref[...]
        sig = 1.0 / (1.0 + jnp.exp(-gc))
        act = gc * sig
        val = val_ref[...].astype(F32)
        act_ref[...] = act.astype(BF16)
        vd_ref[...] = (val * (sig + act - act * sig)).astype(BF16)
        a = (act * val).astype(BF16)
        a_ref[...] = a
        f = _nn(a, w_ref[...])
        gate = mod_ref[0, 5:6, :]
        x2 = x1_ref[...] + gate * f
        xn, r = _rms(x2)
        err = xn * gf_ref[...] - tgt_ref[...]
        dy = err * (1.0 / D)
        dx2 = _rms_bwd(xn, r, dy * gf_ref[...])
        dx2_ref[...] = dx2
        df_ref[...] = (gate * dx2).astype(BF16)

        @pl.when(i == 0)
        def _():
            gstat_ref[...] = jnp.zeros_like(gstat_ref)

        @pl.when(i % tps == 0)
        def _():
            bstat_ref[...] = jnp.zeros_like(bstat_ref)

        gstat_ref[0:1, :] += jnp.sum(dy * xn, axis=0, keepdims=True)
        tile_loss = jnp.sum(jnp.sum(err * err, axis=1, keepdims=True), axis=0, keepdims=True) * (0.5 / D)
        gstat_ref[1:2, :] += jnp.broadcast_to(tile_loss, (1, D))
        bstat_ref[0, 0:1, :] += jnp.sum(dx2 * f, axis=0, keepdims=True)

    return pl.pallas_call(
        body, name="ffn_down", grid=(T // TOKEN_TILE,),
        in_specs=[_tok_spec(F), prev, nxt, _tok_spec(F), _full(conv_w.shape), _full((1, F)), _full(w_down.shape),
                  _tok_spec(D), _mod_spec(tps, D), _full((1, D)), _tok_spec(D)],
        out_specs=[_tok_spec(F), _tok_spec(F), _tok_spec(F), _tok_spec(D), _tok_spec(D), _full((8, D)), _bstat_spec(tps, D)],
        out_shape=[jax.ShapeDtypeStruct((T, F), BF16), jax.ShapeDtypeStruct((T, F), BF16), jax.ShapeDtypeStruct((T, F), BF16),
                   jax.ShapeDtypeStruct((T, D), F32), jax.ShapeDtypeStruct((T, D), BF16),
                   jax.ShapeDtypeStruct((8, D), F32), jax.ShapeDtypeStruct((B, 8, D), F32)],
        compiler_params=_params("arbitrary"),
    )(gt, gt, gt, val, conv_w, conv_b, w_down, x1, mod3, g_final, target)


def _ffn_down_bwd(df, w_down, act, vd, tasks=()):
    T, D = df.shape
    F = act.shape[1]

    def body(df_ref, w_ref, act_ref, vd_ref, dval_ref, dgc_ref, cstat_ref):
        da = _nt(df_ref[...], w_ref[...])
        dval_ref[...] = (da * act_ref[...].astype(F32)).astype(BF16)
        dgc = da * vd_ref[...].astype(F32)
        dgc_ref[...] = dgc.astype(BF16)

        @pl.when(pl.program_id(0) == 0)
        def _():
            cstat_ref[...] = jnp.zeros_like(cstat_ref)

        cstat_ref[0:1, :] += jnp.sum(dgc, axis=0, keepdims=True)

    return _hosted_call(
        body, "ffn_down_bwd", (T // TOKEN_TILE,),
        [_tok_spec(D), _full(w_down.shape), _tok_spec(F), _tok_spec(F)],
        [_tok_spec(F), _tok_spec(F), _full((8, F))],
        [jax.ShapeDtypeStruct((T, F), BF16), jax.ShapeDtypeStruct((T, F), BF16), jax.ShapeDtypeStruct((8, F), F32)],
        (df, w_down, act, vd), tasks)


def _ffn_up_bwd(dgc, dval, gt, conv_w, w_up, x1, mod3, g_ffn, dx2, mix, B, S, tasks=()):
    T, D = x1.shape
    F = dgc.shape[1]
    tps = S // TOKEN_TILE
    prev, nxt = _halo_specs(T, tps, F)

    def body(dgc_ref, prev_ref, next_ref, dval_ref, gt_ref, cw_ref, w_ref, x1_ref, mod_ref, g_ref, dx2_ref, mix_ref,
             du_ref, dx1_ref, dmix_ref, gstat_ref, bstat_ref, cstat_ref):
        i = pl.program_id(0)
        d = dgc_ref[...].astype(F32)
        dprev, dnext = _seq_shifts(d, prev_ref[7:8, :], next_ref[0:1, :], i % tps, tps)
        g = gt_ref[...].astype(F32)

        @pl.when(i == 0)
        def _():
            cstat_ref[...] = jnp.zeros_like(cstat_ref)

        cstat_ref[1:2, :] += jnp.sum(dnext * g, axis=0, keepdims=True)
        cstat_ref[2:3, :] += jnp.sum(d * g, axis=0, keepdims=True)
        cstat_ref[3:4, :] += jnp.sum(dprev * g, axis=0, keepdims=True)
        dgt = dnext * cw_ref[0:1, :] + d * cw_ref[1:2, :] + dprev * cw_ref[2:3, :]
        du = jnp.concatenate([dval_ref[...], dgt.astype(BF16)], axis=1)
        du_ref[...] = du
        dh2 = _nn(du, w_ref[...])
        xn, r = _rms(x1_ref[...])
        scale1 = 1.0 + mod_ref[0, 4:5, :]
        xg = xn * g_ref[...]
        dx1 = dx2_ref[...] + _rms_bwd(xn, r, dh2 * g_ref[...] * scale1)
        dx1_ref[...] = dx1
        dmix_ref[...] = (mod_ref[0, 2:3, :] * dx1).astype(BF16)

        @pl.when(i == 0)
        def _():
            gstat_ref[...] = jnp.zeros_like(gstat_ref)

        @pl.when(i % tps == 0)
        def _():
            bstat_ref[...] = jnp.zeros_like(bstat_ref)

        gstat_ref[0:1, :] += jnp.sum(dh2 * scale1 * xn, axis=0, keepdims=True)
        bstat_ref[0, 0:1, :] += jnp.sum(dh2, axis=0, keepdims=True)
        bstat_ref[0, 1:2, :] += jnp.sum(dh2 * xg, axis=0, keepdims=True)
        bstat_ref[0, 2:3, :] += jnp.sum(dx1 * mix_ref[...], axis=0, keepdims=True)

    return _hosted_call(
        body, "ffn_up_bwd", (T // TOKEN_TILE,),
        [_tok_spec(F), prev, nxt, _tok_spec(F), _tok_spec(F), _full(conv_w.shape), _full(w_up.shape), _tok_spec(D),
         _mod_spec(tps, D), _full((1, D)), _tok_spec(D), _tok_spec(D)],
        [_tok_spec(2 * F), _tok_spec(D), _tok_spec(D), _full((8, D)), _bstat_spec(tps, D), _full((8, F))],
        [jax.ShapeDtypeStruct((T, 2 * F), BF16), jax.ShapeDtypeStruct((T, D), F32), jax.ShapeDtypeStruct((T, D), BF16),
         jax.ShapeDtypeStruct((8, D), F32), jax.ShapeDtypeStruct((B, 8, D), F32), jax.ShapeDtypeStruct((8, F), F32)],
        (dgc, dgc, dgc, dval, gt, conv_w, w_up, x1, mod3, g_ffn, dx2, mix), tasks)


def _attn_out_bwd(dmix, w_out, oa, ob, g_na, g_sw, tasks=(), tm=WIDE_TILE):
    T, D = dmix.shape

    def body(dmix_ref, w_ref, oa_ref, ob_ref, gna_ref, gsw_ref, doa_ref, dob_ref, gstat_ref):
        dmixin = _nt(dmix_ref[...], w_ref[...])

        @pl.when(pl.program_id(0) == 0)
        def _():
            gstat_ref[...] = jnp.zeros_like(gstat_ref)

        for k, (o_ref, g_ref, do_ref) in enumerate(((oa_ref, gna_ref, doa_ref), (ob_ref, gsw_ref, dob_ref))):
            dn = dmixin[:, k * NA_WIDTH:(k + 1) * NA_WIDTH]
            on, r = _rms(o_ref[...])
            gstat_ref[k:k + 1, :] += jnp.sum(dn * on, axis=0, keepdims=True)
            do_ref[...] = _rms_bwd(on, r, dn * g_ref[...]).astype(BF16)

    hs = jax.ShapeDtypeStruct((T, NA_WIDTH), BF16)
    return _hosted_call(
        body, "attn_out_bwd", (T // tm,),
        [_tok_spec(D, tm), _full(w_out.shape), _tok_spec(NA_WIDTH, tm), _tok_spec(SW_WIDTH, tm), _full((1, NA_WIDTH)), _full((1, SW_WIDTH))],
        [_tok_spec(NA_WIDTH, tm), _tok_spec(SW_WIDTH, tm), _full((8, NA_WIDTH))],
        [hs, hs, jax.ShapeDtypeStruct((8, NA_WIDTH), F32)],
        (dmix, w_out, oa, ob, g_na, g_sw), tasks)


def _attn_in_bwd(dqa, dka, dva, dqb, dkb, dvb, cos_t, sin_t, w_in, x2d, mod3, g_attn, dx1, B, S, tm=WIDE_TILE):
    T, D = x2d.shape
    tps = S // tm

    def body(dqa_ref, dka_ref, dva_ref, dqb_ref, dkb_ref, dvb_ref, cos_ref, sin_ref, w_ref, x_ref, mod_ref, g_ref, dx1_ref,
             gx_ref, dproj_ref, gstat_ref, bstat_ref):
        i = pl.program_id(0)
        drb = jnp.concatenate([dqb_ref[...] * Q_SCALE, dkb_ref[...]], axis=1)
        reps = (1, ROPE_WIDTH // (2 * HEAD_DIM))
        drb = drb * jnp.tile(cos_ref[...], reps) + _rot_half(drb * jnp.tile(sin_ref[...], reps))
        dproj = jnp.concatenate([dqa_ref[...] * Q_SCALE, dka_ref[...], dva_ref[...], drb, dvb_ref[...]], axis=1).astype(BF16)
        dproj_ref[...] = dproj
        dh = _nn(dproj, w_ref[...])
        xn, r = _rms(x_ref[...])
        scale1 = 1.0 + mod_ref[0, 1:2, :]
        gx_ref[...] = dx1_ref[...] + _rms_bwd(xn, r, dh * g_ref[...] * scale1)

        @pl.when(i == 0)
        def _():
            gstat_ref[...] = jnp.zeros_like(gstat_ref)

        @pl.when(i % tps == 0)
        def _():
            bstat_ref[...] = jnp.zeros_like(bstat_ref)

        gstat_ref[0:1, :] += jnp.sum(dh * scale1 * xn, axis=0, keepdims=True)
        bstat_ref[0, 0:1, :] += jnp.sum(dh, axis=0, keepdims=True)
        bstat_ref[0, 1:2, :] += jnp.sum(dh * (xn * g_ref[...]), axis=0, keepdims=True)

    rope = _rope_spec(tps, tm)
    return pl.pallas_call(
        body, name="attn_in_bwd", grid=(T // tm,),
        in_specs=[_tok_spec(NA_WIDTH, tm), _tok_spec(NA_WIDTH, tm), _tok_spec(NA_WIDTH, tm), _tok_spec(SW_WIDTH, tm),
                  _tok_spec(SW_KV_WIDTH, tm), _tok_spec(SW_KV_WIDTH, tm), rope, rope, _full(w_in.shape), _tok_spec(D, tm),
                  _mod_spec(tps, D), _full((1, D)), _tok_spec(D, tm)],
        out_specs=[_tok_spec(D, tm), _tok_spec(IN_WIDTH, tm), _full((8, D)), _bstat_spec(tps, D)],
        out_shape=[jax.ShapeDtypeStruct((T, D), F32), jax.ShapeDtypeStruct((T, IN_WIDTH), BF16),
                   jax.ShapeDtypeStruct((8, D), F32), jax.ShapeDtypeStruct((B, 8, D), F32)],
        compiler_params=_params("arbitrary"),
    )(dqa, dka, dva, dqb, dkb, dvb, cos_t, sin_t, w_in, x2d, mod3, g_attn, dx1)


def _matmul_tn(a, b, name, tm=None, tk=512):
    T, M = a.shape
    N = b.shape[1]
    tm = M if tm is None else tm
    nk = T // tk

    def body(a_ref, b_ref, o_ref, acc):
        k = pl.program_id(1)

        @pl.when(k == 0)
        def _():
            acc[...] = jnp.zeros_like(acc)

        acc[...] += _tn(a_ref[...], b_ref[...])

        @pl.when(k == nk - 1)
        def _():
            o_ref[...] = acc[...].astype(BF16)

    return pl.pallas_call(
        body, name=name, grid=(M // tm, nk),
        in_specs=[pl.BlockSpec((tk, tm), lambda i, k: (k, i)), pl.BlockSpec((tk, N), lambda i, k: (k, 0))],
        out_specs=pl.BlockSpec((tm, N), lambda i, k: (i, 0)),
        out_shape=jax.ShapeDtypeStruct((M, N), BF16),
        scratch_shapes=[pltpu.VMEM((tm, N), F32)],
        compiler_params=_params("parallel", "arbitrary"),
    )(a, b)


def _na_geometry(S):
    rows = S // GRID_W
    wr = min(NA_ROWS_MAX, rows)
    return rows, wr


def _na_window(r, rows, wr):
    rs = jnp.clip(r - wr // 2, 0, rows - wr)
    return pl.multiple_of(rs * GRID_W, GRID_W), pl.multiple_of((rs - r + NA_ROWS_MAX - 1) * GRID_W, GRID_W)


NA_STEP_PAIRS = 2
NA_GW = NA_STEP_PAIRS * 128
NA_BWD_ROWS = 2
NA_ROWS_PER_STEP = 4


def _na_specs(S, kw_n, order):
    ng = NA_PAIRS // NA_STEP_PAIRS

    def col(k):
        return pl.BlockSpec((1, S, NA_GW), lambda *ids: (order(*ids)[0], 0, k * ng + order(*ids)[1]))
    bias = pl.BlockSpec((NA_STEP_PAIRS, N_DR * GRID_W, 128), lambda *ids: (order(*ids)[1], 0, 0))
    out = pl.BlockSpec((1, S, NA_GW), lambda *ids: (order(*ids)[0], 0, order(*ids)[1]))
    return col(0), col(1), col(2), bias, out


def _block_diag(t):
    left = lax.broadcasted_iota(jnp.int32, t.shape, 1) < HEAD_DIM
    zero = jnp.zeros_like(t)
    return jnp.concatenate([jnp.where(left, t, zero), jnp.where(left, zero, t)], axis=0)


def _diag_blocks(res):
    left = lax.broadcasted_iota(jnp.int32, (HEAD_DIM, 128), 1) < HEAD_DIM
    return jnp.where(left, res[:HEAD_DIM], res[HEAD_DIM:])


def _col_softmax(st):
    e = jnp.exp(st - jnp.max(st, axis=0, keepdims=True))
    return e * (1.0 / jnp.sum(e, axis=0, keepdims=True))


def _na_fwd(qkv, bias, tasks=()):
    B, S, _ = qkv.shape
    rows, wr = _na_geometry(S)
    kw_n = wr * GRID_W

    def body(q_ref, k_ref, v_ref, b_ref, o_ref):
        def step(it, carry):
            win = [_na_window(it * NA_ROWS_PER_STEP + u, rows, wr) for u in range(NA_ROWS_PER_STEP)]
            qrows = [pl.ds(pl.multiple_of((it * NA_ROWS_PER_STEP + u) * GRID_W, GRID_W), GRID_W) for u in range(NA_ROWS_PER_STEP)]
            krows = [pl.ds(w[0], kw_n) for w in win]
            brows = [pl.ds(w[1], kw_n) for w in win]
            lanes = [pl.ds(p * 128, 128) for p in range(NA_STEP_PAIRS)]
            chains = [(u, p) for u in range(NA_ROWS_PER_STEP) for p in range(NA_STEP_PAIRS)]
            st = {(u, p): _nt(k_ref[0, krows[u], lanes[p]], _block_diag(q_ref[0, qrows[u], lanes[p]])) for u, p in chains}
            pn = {(u, p): _col_softmax(st[(u, p)] + b_ref[p, brows[u], :]).astype(BF16) for u, p in chains}
            out = {(u, p): _diag_blocks(_tn(pn[(u, p)], v_ref[0, krows[u], lanes[p]])) for u, p in chains}
            for u in range(NA_ROWS_PER_STEP):
                o_ref[0, qrows[u], :] = jnp.concatenate([out[(u, p)] for p in range(NA_STEP_PAIRS)], axis=1)
            return carry

        lax.fori_loop(0, rows // NA_ROWS_PER_STEP, step, 0)

    q, k, v, bs, out = _na_specs(S, kw_n, lambda b, g: (b, g))
    return _hosted_call(body, "na_fwd", (B, NA_PAIRS // NA_STEP_PAIRS), [q, k, v, bs], [out],
                        [jax.ShapeDtypeStruct((B, S, NA_WIDTH), F32)], (qkv, qkv, qkv, bias), tasks)


def _na_bwd(qkv, bias, doa, tasks=()):
    B, S, _ = qkv.shape
    rows, wr = _na_geometry(S)
    kw_n = wr * GRID_W

    def body(q_ref, k_ref, v_ref, b_ref, do_ref, dq_ref, dk_ref, dv_ref, db_ref):
        @pl.when(pl.program_id(1) == 0)
        def _():
            db_ref[...] = jnp.zeros_like(db_ref)

        dk_ref[...] = jnp.zeros_like(dk_ref)
        dv_ref[...] = jnp.zeros_like(dv_ref)

        def step(it, carry):
            nu, pairs = range(NA_BWD_ROWS), range(NA_STEP_PAIRS)
            win = [_na_window(it * NA_BWD_ROWS + u, rows, wr) for u in nu]
            qrows = [pl.ds(pl.multiple_of((it * NA_BWD_ROWS + u) * GRID_W, GRID_W), GRID_W) for u in nu]
            krows = [pl.ds(w[0], kw_n) for w in win]
            brows = [pl.ds(w[1], kw_n) for w in win]
            lanes = [pl.ds(p * 128, 128) for p in pairs]
            chains = [(u, p) for u in nu for p in pairs]
            kp = {(u, p): k_ref[0, krows[u], lanes[p]] for u, p in chains}
            qbd = {(u, p): _block_diag(q_ref[0, qrows[u], lanes[p]]) for u, p in chains}
            dobd = {(u, p): _block_diag(do_ref[0, qrows[u], lanes[p]]) for u, p in chains}
            st = {c: _nt(kp[c], qbd[c]) for c in chains}
            dpt = {(u, p): _nt(v_ref[0, krows[u], lanes[p]], dobd[(u, p)]) for u, p in chains}
            pn = {(u, p): _col_softmax(st[(u, p)] + b_ref[p, brows[u], :]) for u, p in chains}
            dst = {c: pn[c] * (dpt[c] - jnp.sum(pn[c] * dpt[c], axis=0, keepdims=True)) for c in chains}
            dsb = {c: dst[c].astype(BF16) for c in chains}
            dq = {c: _diag_blocks(_tn(dsb[c], kp[c])) for c in chains}
            dk = {c: _nn(dsb[c], qbd[c]) for c in chains}
            dv = {c: _nn(pn[c].astype(BF16), dobd[c]) for c in chains}
            for u in nu:
                dq_ref[0, qrows[u], :] = jnp.concatenate([dq[(u, p)] for p in pairs], axis=1)
                dk_ref[0, krows[u], :] += jnp.concatenate([dk[(u, p)] for p in pairs], axis=1)
                dv_ref[0, krows[u], :] += jnp.concatenate([dv[(u, p)] for p in pairs], axis=1)
                for p in pairs:
                    db_ref[p, brows[u], :] += dst[(u, p)]
            return carry

        lax.fori_loop(0, rows // NA_BWD_ROWS, step, 0)

    q, k, v, bs, out = _na_specs(S, kw_n, lambda g, b: (b, g))
    hs = jax.ShapeDtypeStruct((B, S, NA_WIDTH), F32)
    return _hosted_call(body, "na_bwd", (NA_PAIRS // NA_STEP_PAIRS, B), [q, k, v, bs, out], [out, out, out, bs],
                        [hs, hs, hs, jax.ShapeDtypeStruct((NA_PAIRS, N_DR * GRID_W, 128), F32)], (qkv, qkv, qkv, bias, doa), tasks)


SW_PAIRS = SW_HEADS // 2


def _sw_band(n, S):
    kw_n = 3 * SW_BLOCK
    start = pl.multiple_of(jnp.clip(n * SW_BLOCK - SW_BLOCK, 0, S - kw_n), SW_BLOCK)
    kpos = start + lax.broadcasted_iota(jnp.int32, (kw_n, SW_BLOCK), 0)
    qpos = n * SW_BLOCK + lax.broadcasted_iota(jnp.int32, (kw_n, SW_BLOCK), 1)
    return start, jnp.abs(qpos - kpos) <= SW_WINDOW


def _kv_halves(t):
    left = lax.broadcasted_iota(jnp.int32, t.shape, 1) < HEAD_DIM
    swapped = pltpu.roll(t, HEAD_DIM, axis=1)
    zero = jnp.zeros_like(t)
    return {(0, 0): jnp.where(left, t, zero), (0, 1): jnp.where(left, zero, swapped),
            (1, 0): jnp.where(left, swapped, zero), (1, 1): jnp.where(left, zero, t)}


def _sw_probs(st, ok, sk):
    st = jnp.where(ok, st, NEG)
    m = jnp.maximum(jnp.max(st, axis=0, keepdims=True), sk)
    e = jnp.exp(st - m)
    esk = jnp.exp(sk - m)
    inv = 1.0 / (jnp.sum(e, axis=0, keepdims=True) + esk)
    return e * inv, esk * inv


def _sw_specs(S):
    q = pl.BlockSpec((1, S, SW_WIDTH), lambda b, *_: (b, 0, ROPE_LO // SW_WIDTH))
    k = pl.BlockSpec((1, S, SW_KV_WIDTH), lambda b, *_: (b, 0, (ROPE_LO + SW_WIDTH) // SW_KV_WIDTH))
    v = pl.BlockSpec((1, S, SW_KV_WIDTH), lambda b, *_: (b, 0, (ROPE_LO + ROPE_WIDTH) // SW_KV_WIDTH))
    return q, k, v


SW_FWD_SPLIT = 2


def _sw_fwd(sink, qkv, tasks=()):
    B, S, _ = qkv.shape
    kw_n = 3 * SW_BLOCK

    def body(sink_ref, q_ref, k_ref, v_ref, o_ref):
        def step(n, carry):
            start, ok = _sw_band(n, S)
            qrows = pl.ds(pl.multiple_of(n * SW_BLOCK, SW_BLOCK), SW_BLOCK)
            krows = pl.ds(start, kw_n)
            kh, vh = _kv_halves(k_ref[0, krows, :]), _kv_halves(v_ref[0, krows, :])
            heads = [(p, e) for p in range(SW_PAIRS) for e in range(2)]
            qp = [q_ref[0, qrows, pl.ds(p * 128, 128)] for p in range(SW_PAIRS)]
            kv_of = lambda p: p // (SW_PAIRS // SW_KV_HEADS)
            st = {(p, e): _nt(kh[(kv_of(p), e)], qp[p]) for p, e in heads}
            pn = {(p, e): _sw_probs(st[(p, e)], ok, sink_ref[2 * p + e])[0].astype(BF16) for p, e in heads}
            outs = [_tn(pn[(p, 0)], vh[(kv_of(p), 0)]) + _tn(pn[(p, 1)], vh[(kv_of(p), 1)]) for p in range(SW_PAIRS)]
            o_ref[0, qrows, :] = jnp.concatenate(outs, axis=1)
            return carry

        half = (S // SW_BLOCK) // SW_FWD_SPLIT
        lax.fori_loop(pl.program_id(1) * half, (pl.program_id(1) + 1) * half, step, 0)

    q, k, v = _sw_specs(S)
    return _hosted_call(
        body, "sw_fwd", (B, SW_FWD_SPLIT), [pl.BlockSpec(memory_space=pltpu.SMEM), q, k, v],
        [pl.BlockSpec((1, S, SW_WIDTH), lambda b, s: (b, 0, 0))], [jax.ShapeDtypeStruct((B, S, SW_WIDTH), F32)],
        (sink, qkv, qkv, qkv), tasks)


def _sw_bwd(sink, qkv, dob):
    B, S, _ = qkv.shape
    kw_n = 3 * SW_BLOCK

    fold_rows = 256

    def body(sink_ref, q_ref, k_ref, v_ref, do_ref, dq_ref, dk_ref, dv_ref, dsink_ref, dk_acc, dv_acc):
        @pl.when(pl.program_id(0) == 0)
        def _():
            dsink_ref[...] = jnp.zeros_like(dsink_ref)

        dk_acc[...] = jnp.zeros_like(dk_acc)
        dv_acc[...] = jnp.zeros_like(dv_acc)
        ppk = SW_PAIRS // SW_KV_HEADS

        def step(n, carry):
            start, ok = _sw_band(n, S)
            qrows = pl.ds(pl.multiple_of(n * SW_BLOCK, SW_BLOCK), SW_BLOCK)
            krows = pl.ds(start, kw_n)
            kh, vh = _kv_halves(k_ref[0, krows, :]), _kv_halves(v_ref[0, krows, :])
            heads = [(p, e) for p in range(SW_PAIRS) for e in range(2)]
            qp = [q_ref[0, qrows, pl.ds(p * 128, 128)] for p in range(SW_PAIRS)]
            dop = [do_ref[0, qrows, pl.ds(p * 128, 128)] for p in range(SW_PAIRS)]
            st = {(p, e): _nt(kh[(p // ppk, e)], qp[p]) for p, e in heads}
            dpt = {(p, e): _nt(vh[(p // ppk, e)], dop[p]) for p, e in heads}
            pnb, dsb = {}, {}
            for p, e in heads:
                pn, psink = _sw_probs(st[(p, e)], ok, sink_ref[2 * p + e])
                delta = jnp.sum(pn * dpt[(p, e)], axis=0, keepdims=True)
                dsb[(p, e)] = (pn * (dpt[(p, e)] - delta)).astype(BF16)
                pnb[(p, e)] = pn.astype(BF16)
                dsink_ref[2 * p + e:2 * p + e + 1, :] += -(psink * delta)
            dq_ref[0, qrows, :] = jnp.concatenate(
                [_tn(dsb[(p, 0)], kh[(p // ppk, 0)]) + _tn(dsb[(p, 1)], kh[(p // ppk, 1)]) for p in range(SW_PAIRS)], axis=1)
            left = lax.broadcasted_iota(jnp.int32, (kw_n, 128), 1) < HEAD_DIM
            dks, dvs = [], []
            for kv in range(SW_KV_HEADS):
                dk = dv = None
                for p in range(kv * ppk, (kv + 1) * ppk):
                    dk_p = jnp.where(left, _nn(dsb[(p, 0)], qp[p]), _nn(dsb[(p, 1)], qp[p]))
                    dv_p = jnp.where(left, _nn(pnb[(p, 0)], dop[p]), _nn(pnb[(p, 1)], dop[p]))
                    dk = dk_p if dk is None else dk + dk_p
                    dv = dv_p if dv is None else dv + dv_p
                dks.append(dk)
                dvs.append(dv)
            dk_acc[krows, :] += jnp.concatenate(dks, axis=1)
            dv_acc[krows, :] += jnp.concatenate(dvs, axis=1)
            return carry

        lax.fori_loop(0, S // SW_BLOCK, step, 0)

        def fold(i, carry):
            rows = pl.ds(pl.multiple_of(i * fold_rows, fold_rows), fold_rows)
            left = lax.broadcasted_iota(jnp.int32, (fold_rows, 128), 1) < HEAD_DIM
            for acc, out_ref in ((dk_acc, dk_ref), (dv_acc, dv_ref)):
                a, b = acc[rows, 0:128], acc[rows, 128:256]
                out_ref[0, rows, :] = jnp.where(left, a + pltpu.roll(a, HEAD_DIM, axis=1), b + pltpu.roll(b, HEAD_DIM, axis=1))
            return carry

        lax.fori_loop(0, S // fold_rows, fold, 0)

        @pl.when(pl.program_id(0) == B - 1)
        def _():
            dsink_ref[...] = jnp.broadcast_to(jnp.sum(dsink_ref[...], axis=1, keepdims=True), dsink_ref.shape)

    q, k, v = _sw_specs(S)
    qo = pl.BlockSpec((1, S, SW_WIDTH), lambda b: (b, 0, 0))
    ko = pl.BlockSpec((1, S, SW_KV_WIDTH), lambda b: (b, 0, 0))
    return pl.pallas_call(
        body, name="sw_bwd", grid=(B,),
        in_specs=[pl.BlockSpec(memory_space=pltpu.SMEM), q, k, v, qo],
        out_specs=[qo, ko, ko, _full((SW_HEADS, 128))],
        out_shape=[jax.ShapeDtypeStruct((B, S, SW_WIDTH), F32), jax.ShapeDtypeStruct((B, S, SW_KV_WIDTH), F32),
                   jax.ShapeDtypeStruct((B, S, SW_KV_WIDTH), F32), jax.ShapeDtypeStruct((SW_HEADS, 128), F32)],
        scratch_shapes=[pltpu.VMEM((S, 2 * SW_KV_WIDTH), F32), pltpu.VMEM((S, 2 * SW_KV_WIDTH), F32)],
        compiler_params=_params("arbitrary"),
    )(sink, qkv, qkv, qkv, dob)


def _pack_sum_adamw(packs, params):
    W = packs.shape[1]
    n_p = len(params)

    def body(p_ref, *refs):
        ins, tot_ref, outs = refs[:3 * n_p], refs[3 * n_p], refs[3 * n_p + 1:]
        tot = p_ref[0:8, :]
        for d in range(1, N_DEV):
            tot = tot + p_ref[8 * d:8 * d + 8, :]
        tot_ref[...] = tot
        for i, (w, _, _, rows, off) in enumerate(params):
            n = w.shape[1]
            g = tot[rows[0]:rows[0] + 1, off:off + n]
            for r in rows[1:]:
                g = g + tot[r:r + 1, off:off + n]
            w_ref, m_ref, v_ref = ins[3 * i:3 * i + 3]
            g_ref, d_ref, nm_ref, nv_ref = outs[4 * i:4 * i + 4]
            g_ref[...] = g
            d_ref[...], nm_ref[...], nv_ref[...] = _adam_update(w_ref[...], g, m_ref[...], v_ref[...])

    res = pl.pallas_call(
        body, name="small_adamw",
        out_shape=[jax.ShapeDtypeStruct((8, W), F32)] + [jax.ShapeDtypeStruct(p[0].shape, F32) for p in params for _ in range(4)],
        compiler_params=pltpu.CompilerParams(vmem_limit_bytes=VMEM_LIMIT),
    )(packs, *[a for p in params for a in p[:3]])
    return res[0], [res[1 + 4 * i:5 + 4 * i] for i in range(n_p)]


def _adam_update(w, g, m, v):
    c1 = 1.0 - ADAM_B1 ** ADAM_STEP
    c2 = 1.0 - ADAM_B2 ** ADAM_STEP
    nm = ADAM_B1 * m + (1.0 - ADAM_B1) * g
    nv = ADAM_B2 * v + (1.0 - ADAM_B2) * (g * g)
    return -ADAM_LR * ((nm / c1) / (jnp.sqrt(nv / c2) + ADAM_EPS) + ADAM_WD * w), nm, nv


def _adamw(w, g, m, v, name):
    def body(w_ref, g_ref, m_ref, v_ref, d_ref, nm_ref, nv_ref):
        d_ref[...], nm_ref[...], nv_ref[...] = _adam_update(w_ref[...], g_ref[...], m_ref[...], v_ref[...])

    s = jax.ShapeDtypeStruct(w.shape, F32)
    return pl.pallas_call(body, name=name, out_shape=[s, s, s],
                          compiler_params=pltpu.CompilerParams(vmem_limit_bytes=VMEM_LIMIT))(w, g, m, v)


def _sum_adamw_rows(R):
    return max(r for r in range(16, min(R, 256) + 1, 16) if R % r == 0)


def _sum_adamw_steps(R):
    return R // _sum_adamw_rows(R)


def _sum_adamw(own, recvb, w, m, v, name, tasks=()):
    R, C = own.shape
    rc = _sum_adamw_rows(R)

    def body(own_ref, r_ref, w_ref, m_ref, v_ref, g_ref, d_ref, nm_ref, nv_ref):
        g = own_ref[...]
        for j in range(3):
            g = g + r_ref[j].astype(F32)
        g_ref[...] = g
        d_ref[...], nm_ref[...], nv_ref[...] = _adam_update(w_ref[...], g, m_ref[...], v_ref[...])

    blk = pl.BlockSpec((rc, C), lambda i: (i, 0))
    s = jax.ShapeDtypeStruct((R, C), F32)
    return _hosted_call(body, name, (R // rc,), [blk, pl.BlockSpec((3, rc, C), lambda i: (0, i, 0)), blk, blk, blk],
                        [blk, blk, blk, blk], [s, s, s, s], (own, recvb, w, m, v), tasks)


def _by_device(dw):
    return dw.reshape(N_DEV, dw.shape[0] // N_DEV, dw.shape[1])


def _local_step(x, mod, g_attn, w_in, bias, sw_sink, g_na_out, g_sw_out, w_out, g_ffn, w_up, conv_w, conv_b, w_down,
                g_final, target, sharded):
    B, S, D = x.shape
    T = B * S
    x2d = x.reshape(T, D)
    mod3 = mod.reshape(B, 6, D)
    cos_t, sin_t = _rope_tables(S)
    sink = sw_sink.reshape(SW_HEADS)
    n_tiles = T // WIDE_TILE
    full = lambda g: g.reshape(N_DEV * g.shape[1], g.shape[2])

    if sharded:
        half = w_up.shape[0] // 2
        tasks = [_gather_task(w_out, n_tiles // 2), _gather_task(w_up, n_tiles - 1, rows=(0, half))]
    (h, qkv), got = _attn_in(x2d, mod3, g_attn, w_in, cos_t, sin_t, S, tasks if sharded else [])
    if sharded:
        w_out, w_up_part = full(got[0][0]), got[1][0]
    qkv3 = qkv.reshape(B, S, IN_WIDTH)
    na_steps = B * (NA_PAIRS // NA_STEP_PAIRS)
    (oa,), got = _na_fwd(qkv3, bias, [_gather_task(w_up, na_steps - 1, rows=(half, half), into=w_up_part)] if sharded else [])
    if sharded:
        w_up = full(got[0][0])
    oa = oa.reshape(T, NA_WIDTH)
    (ob,), got = _sw_fwd(sink, qkv3, [_gather_task(w_down, B * SW_FWD_SPLIT - 1)] if sharded else [])
    if sharded:
        w_down = full(got[0][0])
    ob = ob.reshape(T, SW_WIDTH)
    mixin, mix, x1 = _attn_out(oa, ob, x2d, mod3, g_na_out, g_sw_out, w_out, S)
    h2, val, gt = _ffn_up(x1, mod3, g_ffn, w_up, S)
    a, act, vd, dx2, df, gstat_f, bstat_f = _ffn_down(gt, val, conv_w, conv_b, w_down, x1, mod3, g_final, target.reshape(T, D), B, S)
    F = val.shape[1]

    dw_down = _matmul_tn(a, df, "dw_down")
    (dval, dgc, cstat), got = _ffn_down_bwd(df, w_down, act, vd, [_swap_task(_by_device(dw_down))] if sharded else [])
    if sharded:
        send_down, own_down = _chip_sums(_by_device(dw_down), got[0][0])
    (du, dx1, dmix, gstat_u, bstat_u, cstat_w), got = _ffn_up_bwd(dgc, dval, gt, conv_w, w_up, x1, mod3, g_ffn, dx2, mix, B, S,
                                                                  [_exchange_task(send_down)] if sharded else [])
    if sharded:
        dw_down = (own_down, got[0][0])
    dw_up = _matmul_tn(du, h2, "dw_up", tm=F)
    dw_out = _matmul_tn(mixin, dmix, "dw_out")
    (doa, dob, gstat_o), got = _attn_out_bwd(dmix, w_out, oa, ob, g_na_out, g_sw_out,
                                             [_swap_task(_by_device(dw_up)), _swap_task(_by_device(dw_out))] if sharded else [])
    if sharded:
        send_up, own_up = _chip_sums(_by_device(dw_up), got[0][0])
        send_out, own_out = _chip_sums(_by_device(dw_out), got[1][0])
    (dqa, dka, dva, dbt), got = _na_bwd(qkv3, bias, doa.reshape(B, S, NA_WIDTH),
                                        [_exchange_task(send_up), _exchange_task(send_out)] if sharded else [])
    if sharded:
        dw_up, dw_out = (own_up, got[0][0]), (own_out, got[1][0])
    dqb, dkb, dvb, dsink = _sw_bwd(sink, qkv3, dob.reshape(B, S, SW_WIDTH))
    r2 = lambda t: t.reshape(T, t.shape[-1])
    grad_x, dproj, gstat_i, bstat_i = _attn_in_bwd(r2(dqa), r2(dka), r2(dva), r2(dqb), r2(dkb), r2(dvb), cos_t, sin_t, w_in, x2d, mod3,
                                                   g_attn, dx1, B, S)
    dw_in = _matmul_tn(dproj, h, "dw_in")

    dmod = jnp.stack([bstat_i[:, 0], bstat_i[:, 1], bstat_u[:, 2], bstat_u[:, 0], bstat_u[:, 1], bstat_f[:, 0]], axis=1)
    small = dict(g_attn=gstat_i[0], g_ffn=gstat_u[0], g_final=gstat_f[0], loss=gstat_f[1, 0], g_na_out=gstat_o[0], g_sw_out=gstat_o[1],
                 sw_sink=dsink[:, 0], conv_b=cstat[0], conv_w=cstat_w[1:4], dbt=dbt)
    return grad_x.reshape(B, S, D), dict(w_in=dw_in, w_out=dw_out, w_up=dw_up, w_down=dw_down), dmod, small


def _pad_lanes(v, w):
    return jnp.pad(v, (0, w - v.shape[0]))


def kernel(x, c, w_ada, b_ada, g_attn, w_in, na_rpb, sw_sink, g_na_out, g_sw_out, w_out, g_ffn, w_up, conv_w, conv_b, w_down, g_final, loss_target, m_w_ada, m_b_ada, m_g_attn, m_w_in, m_na_rpb, m_sw_sink, m_g_na_out, m_g_sw_out, m_w_out, m_g_ffn, m_w_up, m_conv_w, m_conv_b, m_w_down, m_g_final, v_w_ada, v_b_ada, v_g_attn, v_w_in, v_na_rpb, v_sw_sink, v_g_na_out, v_g_sw_out, v_w_out, v_g_ffn, v_w_up, v_conv_w, v_conv_b, v_w_down, v_g_final):
    B, S, D = x.shape
    me = 4 * lax.axis_index("x") + 2 * lax.axis_index("y") + lax.axis_index("c")
    ada_c = w_ada.shape[2]
    F_l = conv_w.shape[2]

    cw_l = jnp.pad(conv_w[0], ((0, 8 - conv_w.shape[1]), (0, 0)))
    c_l = jnp.pad(c, ((0, 8 - B), (0, 0)))
    tr = {"w_in", "w_up"}
    w_in_t = jnp.transpose(w_in[0])
    shards = dict(w_out=w_out[0].astype(BF16), w_up=jnp.transpose(w_up[0]).astype(BF16), w_down=w_down[0].astype(BF16))

    b_ada_l = lax.dynamic_slice(b_ada, (0, me * ada_c), (1, ada_c))
    slabs, mod_all, w_in_all = _ada_fwd(jnp.concatenate([c_l, cw_l], axis=1), w_ada[0], b_ada_l, w_in_t.astype(BF16), B)
    c_all = slabs[:, :, :D].reshape(N_DEV * 8, D)
    conv_w_f = jnp.transpose(slabs[:, :3, D:], (1, 0, 2)).reshape(3, N_DEV * F_l)
    mod_mine = lax.dynamic_slice(mod_all, (0, me * B, 0), (N_DEV, B, ada_c))
    mod = jnp.transpose(mod_mine, (1, 0, 2)).reshape(B, N_DEV * ada_c)
    w_in_f = w_in_all.reshape(N_DEV * w_in_t.shape[0], D)

    bias = _na_bias_table(na_rpb[0])

    grad_x, dw, dmod, small = _local_step(x, mod, g_attn, w_in_f, bias, sw_sink, g_na_out, g_sw_out, shards["w_out"], g_ffn,
                                          shards["w_up"], conv_w_f, conv_b, shards["w_down"], g_final.reshape(1, D), loss_target,
                                          sharded=True)
    drpb = _na_bias_grad(small["dbt"])

    row2 = jnp.concatenate([small["g_attn"], small["g_ffn"], small["g_final"], small["g_na_out"], small["g_sw_out"],
                            _pad_lanes(small["sw_sink"], 128), _pad_lanes(small["loss"].reshape(1), 128)])
    rows = [dmod.reshape(B, 6 * D)[0], dmod.reshape(B, 6 * D)[1], _pad_lanes(row2, PACK_W), _pad_lanes(small["conv_b"], PACK_W),
            _pad_lanes(drpb, PACK_W)] + [_pad_lanes(small["conv_w"][k], PACK_W) for k in range(3)]
    weights = dict(w_ada=w_ada, b_ada=b_ada, g_attn=g_attn, w_in=w_in, na_rpb=na_rpb, sw_sink=sw_sink, g_na_out=g_na_out,
                   g_sw_out=g_sw_out, w_out=w_out, g_ffn=g_ffn, w_up=w_up, conv_w=conv_w, conv_b=conv_b, w_down=w_down, g_final=g_final)
    ms = dict(w_ada=m_w_ada, b_ada=m_b_ada, g_attn=m_g_attn, w_in=m_w_in, na_rpb=m_na_rpb, sw_sink=m_sw_sink, g_na_out=m_g_na_out,
              g_sw_out=m_g_sw_out, w_out=m_w_out, g_ffn=m_g_ffn, w_up=m_w_up, conv_w=m_conv_w, conv_b=m_conv_b, w_down=m_w_down, g_final=m_g_final)
    vs = dict(w_ada=v_w_ada, b_ada=v_b_ada, g_attn=v_g_attn, w_in=v_w_in, na_rpb=v_na_rpb, sw_sink=v_sw_sink, g_na_out=v_g_na_out,
              g_sw_out=v_g_sw_out, w_out=v_w_out, g_ffn=v_g_ffn, w_up=v_w_up, conv_w=v_conv_w, conv_b=v_conv_b, w_down=v_w_down, g_final=v_g_final)
    names = list(weights)
    grads, deltas, new_m, new_v = {}, {}, {}, {}
    flat = lambda t: t.reshape(1, -1)

    def shard2d(nm):
        if nm in tr:
            return (lambda t: jnp.transpose(t[0])), (lambda t: jnp.transpose(t)[None])
        return (lambda t: t[0]), (lambda t: t[None])

    def finish_sum(nm, own, recvb, tasks=()):
        r, back = shard2d(nm)
        (g2, d_, m_, v_), got = _sum_adamw(own, recvb, r(weights[nm]), r(ms[nm]), r(vs[nm]), "adamw_" + nm, tasks)
        grads[nm], deltas[nm], new_m[nm], new_v[nm] = back(g2), back(d_), back(m_), back(v_)
        return got

    g8_in = _by_device(dw["w_in"])
    got = finish_sum("w_down", *dw["w_down"], [_swap_task(g8_in)])
    send_in, own_in = _chip_sums(g8_in, got[0][0])
    n_up = _sum_adamw_steps(dw["w_up"][0].shape[0])
    got = finish_sum("w_up", *dw["w_up"], [_exchange_task(send_in), _gather_task(jnp.stack(rows), n_up - 1)])
    packs = got[1][0]
    finish_sum("w_in", own_in, got[0][0])
    finish_sum("w_out", *dw["w_out"])

    where = dict(b_ada=((0, 1), 0), g_attn=((2,), 0), g_ffn=((2,), D), g_final=((2,), 2 * D), g_na_out=((2,), 3 * D),
                 g_sw_out=((2,), 3 * D + NA_WIDTH), sw_sink=((2,), 3 * D + NA_WIDTH + SW_WIDTH), conv_b=((3,), 0), na_rpb=((4,), 0))
    tot, small_out = _pack_sum_adamw(packs.reshape(N_DEV * 8, PACK_W),
                                     [(flat(weights[n]), flat(ms[n]), flat(vs[n])) + where[n] for n in where])
    for n, (g_, d_, m_, v_) in zip(where, small_out):
        shp = weights[n].shape
        grads[n], deltas[n], new_m[n], new_v[n] = g_.reshape(shp), d_.reshape(shp), m_.reshape(shp), v_.reshape(shp)
    loss = tot[2, 3 * D + NA_WIDTH + SW_WIDTH + 128]

    dmod_cols = lax.dynamic_slice(packs.reshape(N_DEV * 8, PACK_W), (0, me * ada_c), (N_DEV * 8, ada_c))
    for nm, g2 in (("w_ada", _ada_bwd(c_all, dmod_cols)), ("conv_w", lax.dynamic_slice(tot[5:8], (0, me * F_l), (3, F_l)))):
        r, back = shard2d(nm)
        d_, m_, v_ = _adamw(r(weights[nm]), g2, r(ms[nm]), r(vs[nm]), "adamw_" + nm)
        grads[nm], deltas[nm], new_m[nm], new_v[nm] = back(g2), back(d_), back(m_), back(v_)
    return (loss, grad_x, *[grads[n] for n in names], *[deltas[n] for n in names], *[new_m[n] for n in names],
            *[new_v[n] for n in names])
```

```python
import functools

import numpy as np
import jax
import jax.numpy as jnp
from jax import lax
from jax.experimental import pallas as pl
from jax.experimental.pallas import tpu as pltpu

F32, BF16 = jnp.float32, jnp.bfloat16
MESH_ID = pl.DeviceIdType.MESH
N_DEV = 8

HEAD_DIM = 64
NA_HEADS = 8
SW_HEADS = 8
SW_KV_HEADS = 2
SW_GROUP = SW_HEADS // SW_KV_HEADS
NA_WIDTH = NA_HEADS * HEAD_DIM
SW_WIDTH = SW_HEADS * HEAD_DIM
SW_KV_WIDTH = SW_KV_HEADS * HEAD_DIM
ROPE_WIDTH = SW_WIDTH + SW_KV_WIDTH
IN_WIDTH = 3 * NA_WIDTH + SW_WIDTH + 2 * SW_KV_WIDTH
ROPE_LO = 3 * NA_WIDTH
GRID_W = 64
NA_ROWS_MAX = 8
NA_COLS = 16
N_DR = 2 * NA_ROWS_MAX - 1
N_DC = 2 * NA_COLS - 1
SW_WINDOW = 128
SW_BLOCK = 128
ROPE_THETA = 10000.0
EPS = 1e-6
NEG = -1e30
Q_SCALE = HEAD_DIM ** -0.5

ADAM_LR = 0.001
ADAM_B1 = 0.9
ADAM_B2 = 0.999
ADAM_EPS = 1e-08
ADAM_WD = 0.01
ADAM_STEP = 10

TOKEN_TILE = 256
WIDE_TILE = 512
VMEM_LIMIT = 56 * 1024 * 1024

PACK_W = 6144


def _nn(a, b):
    return jnp.dot(a, b, preferred_element_type=F32)


def _nt(a, b):
    return lax.dot_general(a, b, (((1,), (1,)), ((), ())), preferred_element_type=F32)


def _tn(a, b):
    return lax.dot_general(a, b, (((0,), (0,)), ((), ())), preferred_element_type=F32)


def _rms(x):
    r = lax.rsqrt(jnp.mean(x * x, axis=-1, keepdims=True) + EPS)
    return x * r, r


def _rms_bwd(xn, r, gy):
    return r * (gy - xn * jnp.mean(xn * gy, axis=-1, keepdims=True))


def _params(*sem):
    return pltpu.CompilerParams(dimension_semantics=sem, vmem_limit_bytes=VMEM_LIMIT)


def _full(shape):
    n = len(shape)
    return pl.BlockSpec(shape, lambda *_: (0,) * n)


def _mesh_pos():
    return lax.axis_index("x"), lax.axis_index("y"), lax.axis_index("c")


def _row_chunk(r):
    for rc in (128, 64, 32, 16):
        if r % rc == 0:
            return rc
    raise ValueError(f"rows {r} not a multiple of 16")


class _Task:
    def __init__(self, inputs, out_shapes, sems, start, finish, mid=None, mid_step=None, alias=None):
        self.inputs, self.out_shapes, self.sems = list(inputs), list(out_shapes), list(sems)
        self.start, self.finish, self.mid, self.mid_step = start, finish, mid, mid_step
        self.alias = alias


def _hosted_call(body, name, grid, in_specs, out_specs, out_shape, operands, tasks, scratch_shapes=()):
    n_in, n_out, n_scr = len(in_specs), len(out_specs), len(scratch_shapes)
    t_in = [len(t.inputs) for t in tasks]
    t_out = [len(t.out_shapes) for t in tasks]
    t_sem = [len(t.sems) for t in tasks]
    n_steps = int(np.prod(grid))

    def wrapped(*refs):
        ins, rest = refs[:n_in], refs[n_in:]
        task_ins, rest = rest[:sum(t_in)], rest[sum(t_in):]
        outs, rest = rest[:n_out], rest[n_out:]
        task_outs, rest = rest[:sum(t_out)], rest[sum(t_out):]
        scr, task_sems = rest[:n_scr], rest[n_scr:]
        step = pl.program_id(0)
        for ax in range(1, len(grid)):
            step = step * grid[ax] + pl.program_id(ax)
        parts = []
        oi = oo = os_ = 0
        for t, a, b, c in zip(tasks, t_in, t_out, t_sem):
            parts.append((t, task_ins[oi:oi + a], task_outs[oo:oo + b], task_sems[os_:os_ + c]))
            oi, oo, os_ = oi + a, oo + b, os_ + c
        for t, ti, to, ts in parts:
            pl.when(step == 0)(functools.partial(t.start, ti, to, ts))
            if t.mid is not None:
                pl.when(step == t.mid_step)(functools.partial(t.mid, ti, to, ts))
        body(*ins, *outs, *scr)
        for t, ti, to, ts in parts:
            pl.when(step == n_steps - 1)(functools.partial(t.finish, ti, to, ts))

    hbm = pl.BlockSpec(memory_space=pl.ANY)
    aliases, oi, oo = {}, n_in, n_out
    for t, a, b in zip(tasks, t_in, t_out):
        if t.alias is not None:
            aliases[oi + t.alias[0]] = oo + t.alias[1]
        oi, oo = oi + a, oo + b
    res = pl.pallas_call(
        wrapped, name=name, grid=grid,
        in_specs=list(in_specs) + [hbm] * sum(t_in),
        out_specs=list(out_specs) + [hbm] * sum(t_out),
        out_shape=list(out_shape) + [s for t in tasks for s in t.out_shapes],
        scratch_shapes=list(scratch_shapes) + [s for t in tasks for s in t.sems],
        input_output_aliases=aliases,
        compiler_params=_params(*(["arbitrary"] * len(grid))),
    )(*operands, *[a for t in tasks for a in t.inputs])
    own, extra = res[:n_out], res[n_out:]
    per_task, o = [], 0
    for b in t_out:
        per_task.append(extra[o:o + b])
        o += b
    return own, per_task


def _gather_task(shard, mid_step, rows=None, into=None):
    lo, n = (0, shard.shape[0]) if rows is None else rows

    def parts(ins, outs, sems):
        x_ref, out_ref, (send_sems, recv_sems, local_sem) = ins[0], outs[0], sems
        x_, y_, c_ = _mesh_pos()
        me, sibling = (x_, y_, c_), (x_, y_, 1 - c_)
        chips = [(1 - x_, y_), (x_, 1 - y_), (1 - x_, 1 - y_)]
        x_ref = x_ref.at[pl.ds(lo, n)]

        def rows(px, py, pc):
            return out_ref.at[4 * px + 2 * py + pc, pl.ds(lo, n)]

        def copy(k, block, to, src=None):
            return pltpu.make_async_remote_copy(
                src_ref=rows(*block) if src is None else src, dst_ref=rows(*block),
                send_sem=send_sems.at[k], recv_sem=recv_sems.at[k], device_id=to, device_id_type=MESH_ID)

        return dict(
            mine=lambda: pltpu.make_async_copy(x_ref, rows(*me), local_sem),
            first=lambda: [copy(0, me, sibling, src=x_ref)] + [copy(1 + j, me, (*chip, c_), src=x_ref) for j, chip in enumerate(chips)],
            passed=lambda: [copy(4 + j, (*chip, c_), sibling) for j, chip in enumerate(chips)],
            landed=lambda: [copy(1 + j, (*chip, c_), me) for j, chip in enumerate(chips)],
            last=lambda: [copy(0, sibling, me)] + [copy(4 + j, (*chip, 1 - c_), me) for j, chip in enumerate(chips)])

    def start(ins, outs, sems):
        p = parts(ins, outs, sems)
        p["mine"]().start()
        for cp in p["first"]():
            cp.start()

    def mid(ins, outs, sems):
        p = parts(ins, outs, sems)
        for cp, fw in zip(p["landed"](), p["passed"]()):
            cp.wait_recv()
            fw.start()

    def finish(ins, outs, sems):
        p = parts(ins, outs, sems)
        for cp in p["last"]():
            cp.wait_recv()
        for cp in p["first"]() + p["passed"]():
            cp.wait_send()
        p["mine"]().wait()

    return _Task([shard] if into is None else [shard, into], [jax.ShapeDtypeStruct((N_DEV,) + shard.shape, shard.dtype)],
                 [pltpu.SemaphoreType.DMA((7,)), pltpu.SemaphoreType.DMA((7,)), pltpu.SemaphoreType.DMA],
                 start, finish, mid, mid_step, alias=None if into is None else (1, 0))


def _swap_task(g8):
    _, R, C = g8.shape

    def copies(ins, outs, sems):
        (g_ref,), (recv_ref,), (ss, rs) = ins, outs, sems
        x_, y_, c_ = _mesh_pos()
        return [pltpu.make_async_remote_copy(src_ref=g_ref.at[2 * k + (1 - c_)], dst_ref=recv_ref.at[k], send_sem=ss.at[k],
                                             recv_sem=rs.at[k], device_id=(x_, y_, 1 - c_), device_id_type=MESH_ID)
                for k in range(4)]

    def start(ins, outs, sems):
        for cp in copies(ins, outs, sems):
            cp.start()

    def finish(ins, outs, sems):
        cps = copies(ins, outs, sems)
        for cp in cps:
            cp.wait_recv()
        for cp in cps:
            cp.wait_send()

    return _Task([g8], [jax.ShapeDtypeStruct((4, R, C), g8.dtype)],
                 [pltpu.SemaphoreType.DMA((4,)), pltpu.SemaphoreType.DMA((4,))], start, finish)


def _chip_sums(g8, recva):
    _, R, C = g8.shape
    rc = _row_chunk(R)

    def body(g_ref, a_ref, send_ref, own_ref):
        x_, y_, c_ = _mesh_pos()
        chips = [(1 - x_, y_), (x_, 1 - y_), (1 - x_, 1 - y_), (x_, y_)]

        def chunk(i, carry):
            rows = pl.ds(pl.multiple_of(i * rc, rc), rc)
            for j, (tx, ty) in enumerate(chips):
                k = 2 * tx + ty
                s = g_ref[2 * k + c_, rows, :].astype(F32) + a_ref[k, rows, :].astype(F32)
                if j < 3:
                    send_ref[j, rows, :] = s.astype(BF16)
                else:
                    own_ref[rows, :] = s
            return carry

        lax.fori_loop(0, R // rc, chunk, 0)

    return pl.pallas_call(body, name="chip_sums", out_shape=[jax.ShapeDtypeStruct((3, R, C), BF16), jax.ShapeDtypeStruct((R, C), F32)],
                          compiler_params=pltpu.CompilerParams(vmem_limit_bytes=VMEM_LIMIT))(g8, recva)


def _exchange_task(sendb):
    def copies(ins, outs, sems):
        (s_ref,), (recv_ref,), (ss, rs) = ins, outs, sems
        x_, y_, c_ = _mesh_pos()
        flips = [(1 - x_, y_), (x_, 1 - y_), (1 - x_, 1 - y_)]
        return [pltpu.make_async_remote_copy(src_ref=s_ref.at[j], dst_ref=recv_ref.at[j], send_sem=ss.at[j], recv_sem=rs.at[j],
                                             device_id=(tx, ty, c_), device_id_type=MESH_ID) for j, (tx, ty) in enumerate(flips)]

    def start(ins, outs, sems):
        for cp in copies(ins, outs, sems):
            cp.start()

    def finish(ins, outs, sems):
        cps = copies(ins, outs, sems)
        for cp in cps:
            cp.wait_recv()
        for cp in cps:
            cp.wait_send()

    return _Task([sendb], [jax.ShapeDtypeStruct(sendb.shape, sendb.dtype)],
                 [pltpu.SemaphoreType.DMA((3,)), pltpu.SemaphoreType.DMA((3,))], start, finish)


def _silu(v):
    return v * (1.0 / (1.0 + jnp.exp(-v)))


def _ada_fwd(c_slab, w_ada_l, b_ada_l, w_in_shard, n_seq):
    W = c_slab.shape[1]
    D, cols = w_ada_l.shape
    n_rows = N_DEV * n_seq
    t_c, t_w = _gather_task(c_slab, 0), _gather_task(w_in_shard, 0)
    t_m = _gather_task(jax.ShapeDtypeStruct((n_rows, cols), F32), 0)

    def body(c_ref, w_ref, b_ref, ws_ref, slabs_ref, mod_ref, win_ref, c_vm, m_vm, copy_sem, *sems):
        sc, sw, sm = sems[0:3], sems[3:6], sems[6:9]
        t_c.start((c_ref,), (slabs_ref,), sc)
        t_w.start((ws_ref,), (win_ref,), sw)
        t_c.mid((c_ref,), (slabs_ref,), sc)
        t_c.finish((c_ref,), (slabs_ref,), sc)
        cp = pltpu.make_async_copy(slabs_ref, c_vm, copy_sem)
        cp.start()
        cp.wait()
        c_all = c_vm[:, :, 0:D].reshape(N_DEV * 8, D)
        m64 = jnp.dot(_silu(c_all), w_ref[...], precision=lax.Precision.HIGHEST, preferred_element_type=F32) + b_ref[...]
        r = lax.broadcasted_iota(jnp.int32, (n_rows, N_DEV * 8), 0)
        c = lax.broadcasted_iota(jnp.int32, (n_rows, N_DEV * 8), 1)
        pick = jnp.where(c == 8 * (r // n_seq) + r % n_seq, 1.0, 0.0)
        m_vm[...] = jnp.dot(pick, m64, precision=lax.Precision.HIGHEST, preferred_element_type=F32)
        t_m.start((m_vm,), (mod_ref,), sm)
        t_w.mid((ws_ref,), (win_ref,), sw)
        t_m.mid((m_vm,), (mod_ref,), sm)
        t_m.finish((m_vm,), (mod_ref,), sm)
        t_w.finish((ws_ref,), (win_ref,), sw)

    hbm, vm = pl.BlockSpec(memory_space=pl.ANY), pl.BlockSpec(memory_space=pltpu.VMEM)
    return pl.pallas_call(
        body, name="ada_fwd", in_specs=[hbm, vm, vm, hbm], out_specs=[hbm, hbm, hbm],
        out_shape=t_c.out_shapes + t_m.out_shapes + t_w.out_shapes,
        scratch_shapes=[pltpu.VMEM((N_DEV, 8, W), F32), pltpu.VMEM((n_rows, cols), F32), pltpu.SemaphoreType.DMA]
        + t_c.sems + t_w.sems + t_m.sems,
        compiler_params=pltpu.CompilerParams(vmem_limit_bytes=VMEM_LIMIT),
    )(c_slab, w_ada_l, b_ada_l, w_in_shard)


def _ada_bwd(c_all, dmod_cols):
    def body(c_ref, d_ref, o_ref):
        o_ref[...] = lax.dot_general(_silu(c_ref[...]), d_ref[...], (((0,), (0,)), ((), ())),
                                     precision=lax.Precision.HIGHEST, preferred_element_type=F32)
    return pl.pallas_call(body, name="ada_bwd", out_shape=jax.ShapeDtypeStruct((c_all.shape[1], dmod_cols.shape[1]), F32),
                          compiler_params=pltpu.CompilerParams(vmem_limit_bytes=VMEM_LIMIT))(c_all, dmod_cols)


NA_PAIRS = NA_HEADS // 2
N_DR_PAD = 16


def _na_bias_table(na_rpb):
    rev = jnp.pad(jnp.flip(na_rpb, axis=2), ((0, 0), (0, N_DR_PAD - N_DR), (0, GRID_W - N_DC)))
    rev = jnp.transpose(rev.reshape(NA_PAIRS, 2, N_DR_PAD, GRID_W), (0, 2, 1, 3)).reshape(NA_PAIRS, N_DR_PAD, 128)

    def body(r_ref, o_ref):
        k = lax.broadcasted_iota(jnp.int32, (GRID_W, 128), 0)
        lane = lax.broadcasted_iota(jnp.int32, (GRID_W, 128), 1)
        q = lane % GRID_W
        cs = jnp.clip(q - NA_COLS // 2, 0, GRID_W - NA_COLS)
        ok = (k >= cs) & (k < cs + NA_COLS)
        left = lane < GRID_W
        for dr in range(N_DR):
            row = jnp.broadcast_to(r_ref[0, dr:dr + 1, :], (GRID_W, 128))
            r0 = jnp.where(left, row, 0.0)
            r1 = jnp.where(left, pltpu.roll(row, GRID_W, axis=1), 0.0)
            y0 = pltpu.roll(r0, 128 - (NA_COLS - 1), axis=1, stride=1, stride_axis=0)
            y1 = pltpu.roll(r1, GRID_W - (NA_COLS - 1), axis=1, stride=1, stride_axis=0)
            o_ref[0, dr * GRID_W:(dr + 1) * GRID_W, :] = jnp.where(ok, jnp.where(left, y0, y1), NEG)

    return pl.pallas_call(
        body, name="rpb_expand", grid=(NA_PAIRS,),
        in_specs=[pl.BlockSpec((1, N_DR_PAD, 128), lambda p: (p, 0, 0))],
        out_specs=pl.BlockSpec((1, N_DR * GRID_W, 128), lambda p: (p, 0, 0)),
        out_shape=jax.ShapeDtypeStruct((NA_PAIRS, N_DR * GRID_W, 128), F32),
        compiler_params=_params("parallel"),
    )(rev)


def _na_bias_grad(db):
    a = np.arange(128)
    flip = jnp.asarray(((a[:, None] // GRID_W == a[None, :] // GRID_W)
                        & (a[:, None] % GRID_W + a[None, :] % GRID_W == GRID_W - 1)).astype(np.float32))

    def body(d_ref, j_ref, o_ref):
        o_ref[...] = jnp.zeros_like(o_ref)
        for dr in range(N_DR):
            t = jnp.dot(d_ref[0, dr * GRID_W:(dr + 1) * GRID_W, :], j_ref[...], precision=lax.Precision.HIGHEST, preferred_element_type=F32)
            t = pltpu.roll(t, GRID_W + NA_COLS, axis=1, stride=1, stride_axis=0)
            o_ref[0, dr:dr + 1, :] = jnp.sum(t, axis=0, keepdims=True)

    rows = pl.pallas_call(
        body, name="rpb_reduce", grid=(NA_PAIRS,),
        in_specs=[pl.BlockSpec((1, N_DR * GRID_W, 128), lambda p: (p, 0, 0)), _full((128, 128))],
        out_specs=pl.BlockSpec((1, N_DR_PAD, 128), lambda p: (p, 0, 0)),
        out_shape=jax.ShapeDtypeStruct((NA_PAIRS, N_DR_PAD, 128), F32),
        compiler_params=_params("parallel"),
    )(db, flip)
    g = rows.reshape(NA_PAIRS, N_DR_PAD, 2, GRID_W)[:, :N_DR, :, :N_DC]
    return jnp.transpose(g, (0, 2, 1, 3)).reshape(-1)


def _rope_tables(S):
    half = HEAD_DIM // 2
    inv = np.float32(ROPE_THETA) ** (-np.arange(half, dtype=np.float32) / np.float32(half))
    ang = np.arange(S).astype(np.float32)[:, None] * inv[None, :]
    cos, sin = np.cos(ang).astype(np.float32), np.sin(ang).astype(np.float32)
    return jnp.asarray(np.tile(np.concatenate([cos, cos], axis=1), (1, 2))), jnp.asarray(np.tile(np.concatenate([-sin, sin], axis=1), (1, 2)))


def _rope_spec(tps, tm=TOKEN_TILE):
    return pl.BlockSpec((tm, 2 * HEAD_DIM), lambda i: (i % tps, 0))


def _rot_half(t):
    w = t.shape[1]
    lane = lax.broadcasted_iota(jnp.int32, t.shape, 1)
    return jnp.where((lane % HEAD_DIM) < HEAD_DIM // 2, pltpu.roll(t, w - HEAD_DIM // 2, axis=1),
                     pltpu.roll(t, HEAD_DIM // 2, axis=1))


def _tok_spec(w, tm=TOKEN_TILE):
    return pl.BlockSpec((tm, w), lambda i: (i, 0))


def _mod_spec(tps, d):
    return pl.BlockSpec((1, 6, d), lambda i: (i // tps, 0, 0))


def _bstat_spec(tps, w):
    return pl.BlockSpec((1, 8, w), lambda i: (i // tps, 0, 0))


def _attn_in(x2d, mod3, g_attn, w_in, cos_t, sin_t, S, tasks=(), tm=WIDE_TILE):
    T, D = x2d.shape
    tps = S // tm

    def body(x_ref, mod_ref, g_ref, w_ref, cos_ref, sin_ref, h_ref, qkv_ref):
        xn, _ = _rms(x_ref[...])
        h = (xn * g_ref[...]) * (1.0 + mod_ref[0, 1:2, :]) + mod_ref[0, 0:1, :]
        hb = h.astype(BF16)
        h_ref[...] = hb
        proj = _nt(hb, w_ref[...])
        rb = proj[:, ROPE_LO:ROPE_LO + ROPE_WIDTH]
        reps = (1, ROPE_WIDTH // (2 * HEAD_DIM))
        rb = rb * jnp.tile(cos_ref[...], reps) + _rot_half(rb) * jnp.tile(sin_ref[...], reps)
        qkv_ref[:, 0:NA_WIDTH] = (proj[:, 0:NA_WIDTH] * Q_SCALE).astype(BF16)
        qkv_ref[:, NA_WIDTH:ROPE_LO] = proj[:, NA_WIDTH:ROPE_LO].astype(BF16)
        qkv_ref[:, ROPE_LO:ROPE_LO + SW_WIDTH] = (rb[:, 0:SW_WIDTH] * Q_SCALE).astype(BF16)
        qkv_ref[:, ROPE_LO + SW_WIDTH:ROPE_LO + ROPE_WIDTH] = rb[:, SW_WIDTH:].astype(BF16)
        qkv_ref[:, ROPE_LO + ROPE_WIDTH:] = proj[:, ROPE_LO + ROPE_WIDTH:].astype(BF16)

    return _hosted_call(
        body, "attn_in", (T // tm,),
        [_tok_spec(D, tm), _mod_spec(tps, D), _full((1, D)), _full(w_in.shape), _rope_spec(tps, tm), _rope_spec(tps, tm)],
        [_tok_spec(D, tm), _tok_spec(IN_WIDTH, tm)],
        [jax.ShapeDtypeStruct((T, D), BF16), jax.ShapeDtypeStruct((T, IN_WIDTH), BF16)],
        (x2d, mod3, g_attn, w_in, cos_t, sin_t), tasks)


def _attn_out(oa, ob, x2d, mod3, g_na, g_sw, w_out, S, tasks=(), tm=WIDE_TILE):
    T, D = x2d.shape
    tps = S // tm

    def body(oa_ref, ob_ref, x_ref, mod_ref, gna_ref, gsw_ref, w_ref, mixin_ref, mix_ref, x1_ref):
        oan, _ = _rms(oa_ref[...])
        obn, _ = _rms(ob_ref[...])
        mixin = jnp.concatenate([oan * gna_ref[...], obn * gsw_ref[...]], axis=1).astype(BF16)
        mixin_ref[...] = mixin
        mix = _nn(mixin, w_ref[...])
        mix_ref[...] = mix
        x1_ref[...] = x_ref[...] + mod_ref[0, 2:3, :] * mix

    return _hosted_call(
        body, "attn_out", (T // tm,),
        [_tok_spec(NA_WIDTH, tm), _tok_spec(SW_WIDTH, tm), _tok_spec(D, tm), _mod_spec(tps, D),
         _full((1, NA_WIDTH)), _full((1, SW_WIDTH)), _full(w_out.shape)],
        [_tok_spec(NA_WIDTH + SW_WIDTH, tm), _tok_spec(D, tm), _tok_spec(D, tm)],
        [jax.ShapeDtypeStruct((T, NA_WIDTH + SW_WIDTH), BF16), jax.ShapeDtypeStruct((T, D), F32), jax.ShapeDtypeStruct((T, D), F32)],
        (oa, ob, x2d, mod3, g_na, g_sw, w_out), tasks)


def _ffn_up(x1, mod3, g_ffn, w_up, S, tasks=(), tm=WIDE_TILE):
    T, D = x1.shape
    F = w_up.shape[0] // 2
    tps = S // tm

    def body(x1_ref, mod_ref, g_ref, w_ref, h2_ref, val_ref, gt_ref):
        xn, _ = _rms(x1_ref[...])
        h2 = ((xn * g_ref[...]) * (1.0 + mod_ref[0, 4:5, :]) + mod_ref[0, 3:4, :]).astype(BF16)
        h2_ref[...] = h2
        u = _nt(h2, w_ref[...])
        val_ref[...] = u[:, :F].astype(BF16)
        gt_ref[...] = u[:, F:].astype(BF16)

    return _hosted_call(
        body, "ffn_up", (T // tm,), [_tok_spec(D, tm), _mod_spec(tps, D), _full((1, D)), _full(w_up.shape)],
        [_tok_spec(D, tm), _tok_spec(F, tm), _tok_spec(F, tm)],
        [jax.ShapeDtypeStruct((T, D), BF16), jax.ShapeDtypeStruct((T, F), BF16), jax.ShapeDtypeStruct((T, F), BF16)],
        (x1, mod3, g_ffn, w_up), tasks)


def _halo_specs(T, tps, w):
    per = TOKEN_TILE // 8
    prev = pl.BlockSpec((8, w), lambda i: (jnp.maximum(i * per - 1, 0), 0))
    nxt = pl.BlockSpec((8, w), lambda i: (jnp.minimum((i + 1) * per, T // 8 - 1), 0))
    return prev, nxt


def _seq_shifts(cur, before, after, ti, tps):
    tm = cur.shape[0]
    row = lax.broadcasted_iota(jnp.int32, cur.shape, 0)
    before = jnp.where(ti > 0, before.astype(F32), 0.0)
    after = jnp.where(ti < tps - 1, after.astype(F32), 0.0)
    return jnp.where(row == 0, before, pltpu.roll(cur, 1, axis=0)), jnp.where(row == tm - 1, after, pltpu.roll(cur, tm - 1, axis=0))


def _ffn_down(gt, val, conv_w, conv_b, w_down, x1, mod3, g_final, target, B, S):
    T, D = x1.shape
    F = gt.shape[1]
    tps = S // TOKEN_TILE
    prev, nxt = _halo_specs(T, tps, F)

    def body(gt_ref, prev_ref, next_ref, val_ref, cw_ref, cb_ref, w_ref, x1_ref, mod_ref, gf_ref, tgt_ref,
             a_ref, act_ref, vd_ref, dx2_ref, df_ref, gstat_ref, bstat_ref):
        i = pl.program_id(0)
        g = gt_ref[...].astype(F32)
        gprev, gnext = _seq_shifts(g, prev_ref[7:8, :], next_ref[0:1, :], i % tps, tps)
        gc = gprev * cw_ref[0:1, :] + g * cw_ref[1:2, :] + gnext * cw_ref[2:3, :] + cb_ref[...]
        sig = 1.0 / (1.0 + jnp.exp(-gc))
        act = gc * sig
        val = val_ref[...].astype(F32)
        act_ref[...] = act.astype(BF16)
        vd_ref[...] = (val * (sig + act - act * sig)).astype(BF16)
        a = (act * val).astype(BF16)
        a_ref[...] = a
        f = _nn(a, w_ref[...])
        gate = mod_ref[0, 5:6, :]
        x2 = x1_ref[...] + gate * f
        xn, r = _rms(x2)
        err = xn * gf_ref[...] - tgt_ref[...]
        dy = err * (1.0 / D)
        dx2 = _rms_bwd(xn, r, dy * gf_ref[...])
        dx2_ref[...] = dx2
        df_ref[...] = (gate * dx2).astype(BF16)

        @pl.when(i == 0)
        def _():
            gstat_ref[...] = jnp.zeros_like(gstat_ref)

        @pl.when(i % tps == 0)
        def _():
            bstat_ref[...] = jnp.zeros_like(bstat_ref)

        gstat_ref[0:1, :] += jnp.sum(dy * xn, axis=0, keepdims=True)
        tile_loss = jnp.sum(jnp.sum(err * err, axis=1, keepdims=True), axis=0, keepdims=True) * (0.5 / D)
        gstat_ref[1:2, :] += jnp.broadcast_to(tile_loss, (1, D))
        bstat_ref[0, 0:1, :] += jnp.sum(dx2 * f, axis=0, keepdims=True)

    return pl.pallas_call(
        body, name="ffn_down", grid=(T // TOKEN_TILE,),
        in_specs=[_tok_spec(F), prev, nxt, _tok_spec(F), _full(conv_w.shape), _full((1, F)), _full(w_down.shape),
                  _tok_spec(D), _mod_spec(tps, D), _full((1, D)), _tok_spec(D)],
        out_specs=[_tok_spec(F), _tok_spec(F), _tok_spec(F), _tok_spec(D), _tok_spec(D), _full((8, D)), _bstat_spec(tps, D)],
        out_shape=[jax.ShapeDtypeStruct((T, F), BF16), jax.ShapeDtypeStruct((T, F), BF16), jax.ShapeDtypeStruct((T, F), BF16),
                   jax.ShapeDtypeStruct((T, D), F32), jax.ShapeDtypeStruct((T, D), BF16),
                   jax.ShapeDtypeStruct((8, D), F32), jax.ShapeDtypeStruct((B, 8, D), F32)],
        compiler_params=_params("arbitrary"),
    )(gt, gt, gt, val, conv_w, conv_b, w_down, x1, mod3, g_final, target)


def _ffn_down_bwd(df, w_down, act, vd, tasks=()):
    T, D = df.shape
    F = act.shape[1]

    def body(df_ref, w_ref, act_ref, vd_ref, dval_ref, dgc_ref, cstat_ref):
        da = _nt(df_ref[...], w_ref[...])
        dval_ref[...] = (da * act_ref[...].astype(F32)).astype(BF16)
        dgc = da * vd_ref[...].astype(F32)
        dgc_ref[...] = dgc.astype(BF16)

        @pl.when(pl.program_id(0) == 0)
        def _():
            cstat_ref[...] = jnp.zeros_like(cstat_ref)

        cstat_ref[0:1, :] += jnp.sum(dgc, axis=0, keepdims=True)

    return _hosted_call(
        body, "ffn_down_bwd", (T // TOKEN_TILE,),
        [_tok_spec(D), _full(w_down.shape), _tok_spec(F), _tok_spec(F)],
        [_tok_spec(F), _tok_spec(F), _full((8, F))],
        [jax.ShapeDtypeStruct((T, F), BF16), jax.ShapeDtypeStruct((T, F), BF16), jax.ShapeDtypeStruct((8, F), F32)],
        (df, w_down, act, vd), tasks)


def _ffn_up_bwd(dgc, dval, gt, conv_w, w_up, x1, mod3, g_ffn, dx2, mix, B, S, tasks=()):
    T, D = x1.shape
    F = dgc.shape[1]
    tps = S // TOKEN_TILE
    prev, nxt = _halo_specs(T, tps, F)

    def body(dgc_ref, prev_ref, next_ref, dval_ref, gt_ref, cw_ref, w_ref, x1_ref, mod_ref, g_ref, dx2_ref, mix_ref,
             du_ref, dx1_ref, dmix_ref, gstat_ref, bstat_ref, cstat_ref):
        i = pl.program_id(0)
        d = dgc_ref[...].astype(F32)
        dprev, dnext = _seq_shifts(d, prev_ref[7:8, :], next_ref[0:1, :], i % tps, tps)
        g = gt_ref[...].astype(F32)

        @pl.when(i == 0)
        def _():
            cstat_ref[...] = jnp.zeros_like(cstat_ref)

        cstat_ref[1:2, :] += jnp.sum(dnext * g, axis=0, keepdims=True)
        cstat_ref[2:3, :] += jnp.sum(d * g, axis=0, keepdims=True)
        cstat_ref[3:4, :] += jnp.sum(dprev * g, axis=0, keepdims=True)
        dgt = dnext * cw_ref[0:1, :] + d * cw_ref[1:2, :] + dprev * cw_ref[2:3, :]
        du = jnp.concatenate([dval_ref[...], dgt.astype(BF16)], axis=1)
        du_ref[...] = du
        dh2 = _nn(du, w_ref[...])
        xn, r = _rms(x1_ref[...])
        scale1 = 1.0 + mod_ref[0, 4:5, :]
        xg = xn * g_ref[...]
        dx1 = dx2_ref[...] + _rms_bwd(xn, r, dh2 * g_ref[...] * scale1)
        dx1_ref[...] = dx1
        dmix_ref[...] = (mod_ref[0, 2:3, :] * dx1).astype(BF16)

        @pl.when(i == 0)
        def _():
            gstat_ref[...] = jnp.zeros_like(gstat_ref)

        @pl.when(i % tps == 0)
        def _():
            bstat_ref[...] = jnp.zeros_like(bstat_ref)

        gstat_ref[0:1, :] += jnp.sum(dh2 * scale1 * xn, axis=0, keepdims=True)
        bstat_ref[0, 0:1, :] += jnp.sum(dh2, axis=0, keepdims=True)
        bstat_ref[0, 1:2, :] += jnp.sum(dh2 * xg, axis=0, keepdims=True)
        bstat_ref[0, 2:3, :] += jnp.sum(dx1 * mix_ref[...], axis=0, keepdims=True)

    return _hosted_call(
        body, "ffn_up_bwd", (T // TOKEN_TILE,),
        [_tok_spec(F), prev, nxt, _tok_spec(F), _tok_spec(F), _full(conv_w.shape), _full(w_up.shape), _tok_spec(D),
         _mod_spec(tps, D), _full((1, D)), _tok_spec(D), _tok_spec(D)],
        [_tok_spec(2 * F), _tok_spec(D), _tok_spec(D), _full((8, D)), _bstat_spec(tps, D), _full((8, F))],
        [jax.ShapeDtypeStruct((T, 2 * F), BF16), jax.ShapeDtypeStruct((T, D), F32), jax.ShapeDtypeStruct((T, D), BF16),
         jax.ShapeDtypeStruct((8, D), F32), jax.ShapeDtypeStruct((B, 8, D), F32), jax.ShapeDtypeStruct((8, F), F32)],
        (dgc, dgc, dgc, dval, gt, conv_w, w_up, x1, mod3, g_ffn, dx2, mix), tasks)


def _attn_out_bwd(dmix, w_out, oa, ob, g_na, g_sw, tasks=(), tm=WIDE_TILE):
    T, D = dmix.shape

    def body(dmix_ref, w_ref, oa_ref, ob_ref, gna_ref, gsw_ref, doa_ref, dob_ref, gstat_ref):
        dmixin = _nt(dmix_ref[...], w_ref[...])

        @pl.when(pl.program_id(0) == 0)
        def _():
            gstat_ref[...] = jnp.zeros_like(gstat_ref)

        for k, (o_ref, g_ref, do_ref) in enumerate(((oa_ref, gna_ref, doa_ref), (ob_ref, gsw_ref, dob_ref))):
            dn = dmixin[:, k * NA_WIDTH:(k + 1) * NA_WIDTH]
            on, r = _rms(o_ref[...])
            gstat_ref[k:k + 1, :] += jnp.sum(dn * on, axis=0, keepdims=True)
            do_ref[...] = _rms_bwd(on, r, dn * g_ref[...]).astype(BF16)

    hs = jax.ShapeDtypeStruct((T, NA_WIDTH), BF16)
    return _hosted_call(
        body, "attn_out_bwd", (T // tm,),
        [_tok_spec(D, tm), _full(w_out.shape), _tok_spec(NA_WIDTH, tm), _tok_spec(SW_WIDTH, tm), _full((1, NA_WIDTH)), _full((1, SW_WIDTH))],
        [_tok_spec(NA_WIDTH, tm), _tok_spec(SW_WIDTH, tm), _full((8, NA_WIDTH))],
        [hs, hs, jax.ShapeDtypeStruct((8, NA_WIDTH), F32)],
        (dmix, w_out, oa, ob, g_na, g_sw), tasks)


def _attn_in_bwd(dqa, dka, dva, dqb, dkb, dvb, cos_t, sin_t, w_in, x2d, mod3, g_attn, dx1, B, S, tm=WIDE_TILE):
    T, D = x2d.shape
    tps = S // tm

    def body(dqa_ref, dka_ref, dva_ref, dqb_ref, dkb_ref, dvb_ref, cos_ref, sin_ref, w_ref, x_ref, mod_ref, g_ref, dx1_ref,
             gx_ref, dproj_ref, gstat_ref, bstat_ref):
        i = pl.program_id(0)
        drb = jnp.concatenate([dqb_ref[...] * Q_SCALE, dkb_ref[...]], axis=1)
        reps = (1, ROPE_WIDTH // (2 * HEAD_DIM))
        drb = drb * jnp.tile(cos_ref[...], reps) + _rot_half(drb * jnp.tile(sin_ref[...], reps))
        dproj = jnp.concatenate([dqa_ref[...] * Q_SCALE, dka_ref[...], dva_ref[...], drb, dvb_ref[...]], axis=1).astype(BF16)
        dproj_ref[...] = dproj
        dh = _nn(dproj, w_ref[...])
        xn, r = _rms(x_ref[...])
        scale1 = 1.0 + mod_ref[0, 1:2, :]
        gx_ref[...] = dx1_ref[...] + _rms_bwd(xn, r, dh * g_ref[...] * scale1)

        @pl.when(i == 0)
        def _():
            gstat_ref[...] = jnp.zeros_like(gstat_ref)

        @pl.when(i % tps == 0)
        def _():
            bstat_ref[...] = jnp.zeros_like(bstat_ref)

        gstat_ref[0:1, :] += jnp.sum(dh * scale1 * xn, axis=0, keepdims=True)
        bstat_ref[0, 0:1, :] += jnp.sum(dh, axis=0, keepdims=True)
        bstat_ref[0, 1:2, :] += jnp.sum(dh * (xn * g_ref[...]), axis=0, keepdims=True)

    rope = _rope_spec(tps, tm)
    return pl.pallas_call(
        body, name="attn_in_bwd", grid=(T // tm,),
        in_specs=[_tok_spec(NA_WIDTH, tm), _tok_spec(NA_WIDTH, tm), _tok_spec(NA_WIDTH, tm), _tok_spec(SW_WIDTH, tm),
                  _tok_spec(SW_KV_WIDTH, tm), _tok_spec(SW_KV_WIDTH, tm), rope, rope, _full(w_in.shape), _tok_spec(D, tm),
                  _mod_spec(tps, D), _full((1, D)), _tok_spec(D, tm)],
        out_specs=[_tok_spec(D, tm), _tok_spec(IN_WIDTH, tm), _full((8, D)), _bstat_spec(tps, D)],
        out_shape=[jax.ShapeDtypeStruct((T, D), F32), jax.ShapeDtypeStruct((T, IN_WIDTH), BF16),
                   jax.ShapeDtypeStruct((8, D), F32), jax.ShapeDtypeStruct((B, 8, D), F32)],
        compiler_params=_params("arbitrary"),
    )(dqa, dka, dva, dqb, dkb, dvb, cos_t, sin_t, w_in, x2d, mod3, g_attn, dx1)


def _matmul_tn(a, b, name, tm=None, tk=512):
    T, M = a.shape
    N = b.shape[1]
    tm = M if tm is None else tm
    nk = T // tk

    def body(a_ref, b_ref, o_ref, acc):
        k = pl.program_id(1)

        @pl.when(k == 0)
        def _():
            acc[...] = jnp.zeros_like(acc)

        acc[...] += _tn(a_ref[...], b_ref[...])

        @pl.when(k == nk - 1)
        def _():
            o_ref[...] = acc[...].astype(BF16)

    return pl.pallas_call(
        body, name=name, grid=(M // tm, nk),
        in_specs=[pl.BlockSpec((tk, tm), lambda i, k: (k, i)), pl.BlockSpec((tk, N), lambda i, k: (k, 0))],
        out_specs=pl.BlockSpec((tm, N), lambda i, k: (i, 0)),
        out_shape=jax.ShapeDtypeStruct((M, N), BF16),
        scratch_shapes=[pltpu.VMEM((tm, N), F32)],
        compiler_params=_params("parallel", "arbitrary"),
    )(a, b)


def _na_geometry(S):
    rows = S // GRID_W
    wr = min(NA_ROWS_MAX, rows)
    return rows, wr


def _na_window(r, rows, wr):
    rs = jnp.clip(r - wr // 2, 0, rows - wr)
    return pl.multiple_of(rs * GRID_W, GRID_W), pl.multiple_of((rs - r + NA_ROWS_MAX - 1) * GRID_W, GRID_W)


NA_STEP_PAIRS = 2
NA_GW = NA_STEP_PAIRS * 128
NA_BWD_ROWS = 4
NA_ROWS_PER_STEP = 4


def _na_specs(S, kw_n, order):
    ng = NA_PAIRS // NA_STEP_PAIRS

    def col(k):
        return pl.BlockSpec((1, S, NA_GW), lambda *ids: (order(*ids)[0], 0, k * ng + order(*ids)[1]))
    bias = pl.BlockSpec((NA_STEP_PAIRS, N_DR * GRID_W, 128), lambda *ids: (order(*ids)[1], 0, 0))
    out = pl.BlockSpec((1, S, NA_GW), lambda *ids: (order(*ids)[0], 0, order(*ids)[1]))
    return col(0), col(1), col(2), bias, out


def _block_diag(t):
    left = lax.broadcasted_iota(jnp.int32, t.shape, 1) < HEAD_DIM
    zero = jnp.zeros_like(t)
    return jnp.concatenate([jnp.where(left, t, zero), jnp.where(left, zero, t)], axis=0)


def _diag_blocks(res):
    left = lax.broadcasted_iota(jnp.int32, (HEAD_DIM, 128), 1) < HEAD_DIM
    return jnp.where(left, res[:HEAD_DIM], res[HEAD_DIM:])


def _col_softmax(st):
    e = jnp.exp(st - jnp.max(st, axis=0, keepdims=True))
    return e * (1.0 / jnp.sum(e, axis=0, keepdims=True))


def _na_fwd(qkv, bias, tasks=()):
    B, S, _ = qkv.shape
    rows, wr = _na_geometry(S)
    kw_n = wr * GRID_W

    def body(q_ref, k_ref, v_ref, b_ref, o_ref):
        def step(it, carry):
            win = [_na_window(it * NA_ROWS_PER_STEP + u, rows, wr) for u in range(NA_ROWS_PER_STEP)]
            qrows = [pl.ds(pl.multiple_of((it * NA_ROWS_PER_STEP + u) * GRID_W, GRID_W), GRID_W) for u in range(NA_ROWS_PER_STEP)]
            krows = [pl.ds(w[0], kw_n) for w in win]
            brows = [pl.ds(w[1], kw_n) for w in win]
            lanes = [pl.ds(p * 128, 128) for p in range(NA_STEP_PAIRS)]
            chains = [(u, p) for u in range(NA_ROWS_PER_STEP) for p in range(NA_STEP_PAIRS)]
            st = {(u, p): _nt(k_ref[0, krows[u], lanes[p]], _block_diag(q_ref[0, qrows[u], lanes[p]])) for u, p in chains}
            pn = {(u, p): _col_softmax(st[(u, p)] + b_ref[p, brows[u], :]).astype(BF16) for u, p in chains}
            out = {(u, p): _diag_blocks(_tn(pn[(u, p)], v_ref[0, krows[u], lanes[p]])) for u, p in chains}
            for u in range(NA_ROWS_PER_STEP):
                o_ref[0, qrows[u], :] = jnp.concatenate([out[(u, p)] for p in range(NA_STEP_PAIRS)], axis=1)
            return carry

        lax.fori_loop(0, rows // NA_ROWS_PER_STEP, step, 0)

    q, k, v, bs, out = _na_specs(S, kw_n, lambda b, g: (b, g))
    return _hosted_call(body, "na_fwd", (B, NA_PAIRS // NA_STEP_PAIRS), [q, k, v, bs], [out],
                        [jax.ShapeDtypeStruct((B, S, NA_WIDTH), F32)], (qkv, qkv, qkv, bias), tasks)


def _na_bwd(qkv, bias, doa, tasks=()):
    B, S, _ = qkv.shape
    rows, wr = _na_geometry(S)
    kw_n = wr * GRID_W

    def body(q_ref, k_ref, v_ref, b_ref, do_ref, dq_ref, dk_ref, dv_ref, db_ref):
        @pl.when(pl.program_id(1) == 0)
        def _():
            db_ref[...] = jnp.zeros_like(db_ref)

        dk_ref[...] = jnp.zeros_like(dk_ref)
        dv_ref[...] = jnp.zeros_like(dv_ref)

        def step(it, carry):
            nu, pairs = range(NA_BWD_ROWS), range(NA_STEP_PAIRS)
            win = [_na_window(it * NA_BWD_ROWS + u, rows, wr) for u in nu]
            qrows = [pl.ds(pl.multiple_of((it * NA_BWD_ROWS + u) * GRID_W, GRID_W), GRID_W) for u in nu]
            krows = [pl.ds(w[0], kw_n) for w in win]
            brows = [pl.ds(w[1], kw_n) for w in win]
            lanes = [pl.ds(p * 128, 128) for p in pairs]
            chains = [(u, p) for u in nu for p in pairs]
            kp = {(u, p): k_ref[0, krows[u], lanes[p]] for u, p in chains}
            qbd = {(u, p): _block_diag(q_ref[0, qrows[u], lanes[p]]) for u, p in chains}
            dobd = {(u, p): _block_diag(do_ref[0, qrows[u], lanes[p]]) for u, p in chains}
            st = {c: _nt(kp[c], qbd[c]) for c in chains}
            dpt = {(u, p): _nt(v_ref[0, krows[u], lanes[p]], dobd[(u, p)]) for u, p in chains}
            pn = {(u, p): _col_softmax(st[(u, p)] + b_ref[p, brows[u], :]) for u, p in chains}
            dst = {c: pn[c] * (dpt[c] - jnp.sum(pn[c] * dpt[c], axis=0, keepdims=True)) for c in chains}
            dsb = {c: dst[c].astype(BF16) for c in chains}
            dq = {c: _diag_blocks(_tn(dsb[c], kp[c])) for c in chains}
            dk = {c: _nn(dsb[c], qbd[c]) for c in chains}
            dv = {c: _nn(pn[c].astype(BF16), dobd[c]) for c in chains}
            for u in nu:
                dq_ref[0, qrows[u], :] = jnp.concatenate([dq[(u, p)] for p in pairs], axis=1)
                dk_ref[0, krows[u], :] += jnp.concatenate([dk[(u, p)] for p in pairs], axis=1)
                dv_ref[0, krows[u], :] += jnp.concatenate([dv[(u, p)] for p in pairs], axis=1)
                for p in pairs:
                    db_ref[p, brows[u], :] += dst[(u, p)]
            return carry

        lax.fori_loop(0, rows // NA_BWD_ROWS, step, 0)

    q, k, v, bs, out = _na_specs(S, kw_n, lambda g, b: (b, g))
    hs = jax.ShapeDtypeStruct((B, S, NA_WIDTH), F32)
    return _hosted_call(body, "na_bwd", (NA_PAIRS // NA_STEP_PAIRS, B), [q, k, v, bs, out], [out, out, out, bs],
                        [hs, hs, hs, jax.ShapeDtypeStruct((NA_PAIRS, N_DR * GRID_W, 128), F32)], (qkv, qkv, qkv, bias, doa), tasks)


SW_PAIRS = SW_HEADS // 2


def _sw_band(n, S):
    kw_n = 3 * SW_BLOCK
    start = pl.multiple_of(jnp.clip(n * SW_BLOCK - SW_BLOCK, 0, S - kw_n), SW_BLOCK)
    kpos = start + lax.broadcasted_iota(jnp.int32, (kw_n, SW_BLOCK), 0)
    qpos = n * SW_BLOCK + lax.broadcasted_iota(jnp.int32, (kw_n, SW_BLOCK), 1)
    return start, jnp.abs(qpos - kpos) <= SW_WINDOW


def _kv_halves(t):
    left = lax.broadcasted_iota(jnp.int32, t.shape, 1) < HEAD_DIM
    swapped = pltpu.roll(t, HEAD_DIM, axis=1)
    zero = jnp.zeros_like(t)
    return {(0, 0): jnp.where(left, t, zero), (0, 1): jnp.where(left, zero, swapped),
            (1, 0): jnp.where(left, swapped, zero), (1, 1): jnp.where(left, zero, t)}


def _sw_probs(st, ok, sk):
    st = jnp.where(ok, st, NEG)
    m = jnp.maximum(jnp.max(st, axis=0, keepdims=True), sk)
    e = jnp.exp(st - m)
    esk = jnp.exp(sk - m)
    inv = 1.0 / (jnp.sum(e, axis=0, keepdims=True) + esk)
    return e * inv, esk * inv


def _sw_specs(S):
    q = pl.BlockSpec((1, S, SW_WIDTH), lambda b, *_: (b, 0, ROPE_LO // SW_WIDTH))
    k = pl.BlockSpec((1, S, SW_KV_WIDTH), lambda b, *_: (b, 0, (ROPE_LO + SW_WIDTH) // SW_KV_WIDTH))
    v = pl.BlockSpec((1, S, SW_KV_WIDTH), lambda b, *_: (b, 0, (ROPE_LO + ROPE_WIDTH) // SW_KV_WIDTH))
    return q, k, v


SW_FWD_SPLIT = 2


def _sw_fwd(sink, qkv, tasks=()):
    B, S, _ = qkv.shape
    kw_n = 3 * SW_BLOCK

    def body(sink_ref, q_ref, k_ref, v_ref, o_ref):
        def step(n, carry):
            start, ok = _sw_band(n, S)
            qrows = pl.ds(pl.multiple_of(n * SW_BLOCK, SW_BLOCK), SW_BLOCK)
            krows = pl.ds(start, kw_n)
            kh, vh = _kv_halves(k_ref[0, krows, :]), _kv_halves(v_ref[0, krows, :])
            heads = [(p, e) for p in range(SW_PAIRS) for e in range(2)]
            qp = [q_ref[0, qrows, pl.ds(p * 128, 128)] for p in range(SW_PAIRS)]
            kv_of = lambda p: p // (SW_PAIRS // SW_KV_HEADS)
            st = {(p, e): _nt(kh[(kv_of(p), e)], qp[p]) for p, e in heads}
            pn = {(p, e): _sw_probs(st[(p, e)], ok, sink_ref[2 * p + e])[0].astype(BF16) for p, e in heads}
            outs = [_tn(pn[(p, 0)], vh[(kv_of(p), 0)]) + _tn(pn[(p, 1)], vh[(kv_of(p), 1)]) for p in range(SW_PAIRS)]
            o_ref[0, qrows, :] = jnp.concatenate(outs, axis=1)
            return carry

        half = (S // SW_BLOCK) // SW_FWD_SPLIT
        lax.fori_loop(pl.program_id(1) * half, (pl.program_id(1) + 1) * half, step, 0)

    q, k, v = _sw_specs(S)
    return _hosted_call(
        body, "sw_fwd", (B, SW_FWD_SPLIT), [pl.BlockSpec(memory_space=pltpu.SMEM), q, k, v],
        [pl.BlockSpec((1, S, SW_WIDTH), lambda b, s: (b, 0, 0))], [jax.ShapeDtypeStruct((B, S, SW_WIDTH), F32)],
        (sink, qkv, qkv, qkv), tasks)


def _sw_bwd(sink, qkv, dob):
    B, S, _ = qkv.shape
    kw_n = 3 * SW_BLOCK

    fold_rows = 256

    def body(sink_ref, q_ref, k_ref, v_ref, do_ref, dq_ref, dk_ref, dv_ref, dsink_ref, dk_acc, dv_acc):
        @pl.when(pl.program_id(0) == 0)
        def _():
            dsink_ref[...] = jnp.zeros_like(dsink_ref)

        dk_acc[...] = jnp.zeros_like(dk_acc)
        dv_acc[...] = jnp.zeros_like(dv_acc)
        ppk = SW_PAIRS // SW_KV_HEADS

        def step(n, carry):
            start, ok = _sw_band(n, S)
            qrows = pl.ds(pl.multiple_of(n * SW_BLOCK, SW_BLOCK), SW_BLOCK)
            krows = pl.ds(start, kw_n)
            kh, vh = _kv_halves(k_ref[0, krows, :]), _kv_halves(v_ref[0, krows, :])
            heads = [(p, e) for p in range(SW_PAIRS) for e in range(2)]
            qp = [q_ref[0, qrows, pl.ds(p * 128, 128)] for p in range(SW_PAIRS)]
            dop = [do_ref[0, qrows, pl.ds(p * 128, 128)] for p in range(SW_PAIRS)]
            st = {(p, e): _nt(kh[(p // ppk, e)], qp[p]) for p, e in heads}
            dpt = {(p, e): _nt(vh[(p // ppk, e)], dop[p]) for p, e in heads}
            pnb, dsb = {}, {}
            for p, e in heads:
                pn, psink = _sw_probs(st[(p, e)], ok, sink_ref[2 * p + e])
                delta = jnp.sum(pn * dpt[(p, e)], axis=0, keepdims=True)
                dsb[(p, e)] = (pn * (dpt[(p, e)] - delta)).astype(BF16)
                pnb[(p, e)] = pn.astype(BF16)
                dsink_ref[2 * p + e:2 * p + e + 1, :] += -(psink * delta)
            dq_ref[0, qrows, :] = jnp.concatenate(
                [_tn(dsb[(p, 0)], kh[(p // ppk, 0)]) + _tn(dsb[(p, 1)], kh[(p // ppk, 1)]) for p in range(SW_PAIRS)], axis=1)
            left = lax.broadcasted_iota(jnp.int32, (kw_n, 128), 1) < HEAD_DIM
            dks, dvs = [], []
            for kv in range(SW_KV_HEADS):
                dk = dv = None
                for p in range(kv * ppk, (kv + 1) * ppk):
                    dk_p = jnp.where(left, _nn(dsb[(p, 0)], qp[p]), _nn(dsb[(p, 1)], qp[p]))
                    dv_p = jnp.where(left, _nn(pnb[(p, 0)], dop[p]), _nn(pnb[(p, 1)], dop[p]))
                    dk = dk_p if dk is None else dk + dk_p
                    dv = dv_p if dv is None else dv + dv_p
                dks.append(dk)
                dvs.append(dv)
            dk_acc[krows, :] += jnp.concatenate(dks, axis=1)
            dv_acc[krows, :] += jnp.concatenate(dvs, axis=1)
            return carry

        lax.fori_loop(0, S // SW_BLOCK, step, 0)

        def fold(i, carry):
            rows = pl.ds(pl.multiple_of(i * fold_rows, fold_rows), fold_rows)
            left = lax.broadcasted_iota(jnp.int32, (fold_rows, 128), 1) < HEAD_DIM
            for acc, out_ref in ((dk_acc, dk_ref), (dv_acc, dv_ref)):
                a, b = acc[rows, 0:128], acc[rows, 128:256]
                out_ref[0, rows, :] = jnp.where(left, a + pltpu.roll(a, HEAD_DIM, axis=1), b + pltpu.roll(b, HEAD_DIM, axis=1))
            return carry

        lax.fori_loop(0, S // fold_rows, fold, 0)

        @pl.when(pl.program_id(0) == B - 1)
        def _():
            dsink_ref[...] = jnp.broadcast_to(jnp.sum(dsink_ref[...], axis=1, keepdims=True), dsink_ref.shape)

    q, k, v = _sw_specs(S)
    qo = pl.BlockSpec((1, S, SW_WIDTH), lambda b: (b, 0, 0))
    ko = pl.BlockSpec((1, S, SW_KV_WIDTH), lambda b: (b, 0, 0))
    return pl.pallas_call(
        body, name="sw_bwd", grid=(B,),
        in_specs=[pl.BlockSpec(memory_space=pltpu.SMEM), q, k, v, qo],
        out_specs=[qo, ko, ko, _full((SW_HEADS, 128))],
        out_shape=[jax.ShapeDtypeStruct((B, S, SW_WIDTH), F32), jax.ShapeDtypeStruct((B, S, SW_KV_WIDTH), F32),
                   jax.ShapeDtypeStruct((B, S, SW_KV_WIDTH), F32), jax.ShapeDtypeStruct((SW_HEADS, 128), F32)],
        scratch_shapes=[pltpu.VMEM((S, 2 * SW_KV_WIDTH), F32), pltpu.VMEM((S, 2 * SW_KV_WIDTH), F32)],
        compiler_params=_params("arbitrary"),
    )(sink, qkv, qkv, qkv, dob)


def _pack_sum_adamw(packs, params):
    W = packs.shape[1]
    n_p = len(params)

    def body(p_ref, *refs):
        ins, tot_ref, outs = refs[:3 * n_p], refs[3 * n_p], refs[3 * n_p + 1:]
        tot = p_ref[0:8, :]
        for d in range(1, N_DEV):
            tot = tot + p_ref[8 * d:8 * d + 8, :]
        tot_ref[...] = tot
        for i, (w, _, _, rows, off) in enumerate(params):
            n = w.shape[1]
            g = tot[rows[0]:rows[0] + 1, off:off + n]
            for r in rows[1:]:
                g = g + tot[r:r + 1, off:off + n]
            w_ref, m_ref, v_ref = ins[3 * i:3 * i + 3]
            g_ref, d_ref, nm_ref, nv_ref = outs[4 * i:4 * i + 4]
            g_ref[...] = g
            d_ref[...], nm_ref[...], nv_ref[...] = _adam_update(w_ref[...], g, m_ref[...], v_ref[...])

    res = pl.pallas_call(
        body, name="small_adamw",
        out_shape=[jax.ShapeDtypeStruct((8, W), F32)] + [jax.ShapeDtypeStruct(p[0].shape, F32) for p in params for _ in range(4)],
        compiler_params=pltpu.CompilerParams(vmem_limit_bytes=VMEM_LIMIT),
    )(packs, *[a for p in params for a in p[:3]])
    return res[0], [res[1 + 4 * i:5 + 4 * i] for i in range(n_p)]


def _adam_update(w, g, m, v):
    c1 = 1.0 - ADAM_B1 ** ADAM_STEP
    c2 = 1.0 - ADAM_B2 ** ADAM_STEP
    nm = ADAM_B1 * m + (1.0 - ADAM_B1) * g
    nv = ADAM_B2 * v + (1.0 - ADAM_B2) * (g * g)
    return -ADAM_LR * ((nm / c1) / (jnp.sqrt(nv / c2) + ADAM_EPS) + ADAM_WD * w), nm, nv


def _adamw(w, g, m, v, name):
    def body(w_ref, g_ref, m_ref, v_ref, d_ref, nm_ref, nv_ref):
        d_ref[...], nm_ref[...], nv_ref[...] = _adam_update(w_ref[...], g_ref[...], m_ref[...], v_ref[...])

    s = jax.ShapeDtypeStruct(w.shape, F32)
    return pl.pallas_call(body, name=name, out_shape=[s, s, s],
                          compiler_params=pltpu.CompilerParams(vmem_limit_bytes=VMEM_LIMIT))(w, g, m, v)


def _sum_adamw_rows(R):
    return max(r for r in range(16, min(R, 256) + 1, 16) if R % r == 0)


def _sum_adamw_steps(R):
    return R // _sum_adamw_rows(R)


def _sum_adamw(own, recvb, w, m, v, name, tasks=()):
    R, C = own.shape
    rc = _sum_adamw_rows(R)

    def body(own_ref, r_ref, w_ref, m_ref, v_ref, g_ref, d_ref, nm_ref, nv_ref):
        g = own_ref[...]
        for j in range(3):
            g = g + r_ref[j].astype(F32)
        g_ref[...] = g
        d_ref[...], nm_ref[...], nv_ref[...] = _adam_update(w_ref[...], g, m_ref[...], v_ref[...])

    blk = pl.BlockSpec((rc, C), lambda i: (i, 0))
    s = jax.ShapeDtypeStruct((R, C), F32)
    return _hosted_call(body, name, (R // rc,), [blk, pl.BlockSpec((3, rc, C), lambda i: (0, i, 0)), blk, blk, blk],
                        [blk, blk, blk, blk], [s, s, s, s], (own, recvb, w, m, v), tasks)


def _by_device(dw):
    return dw.reshape(N_DEV, dw.shape[0] // N_DEV, dw.shape[1])


def _local_step(x, mod, g_attn, w_in, bias, sw_sink, g_na_out, g_sw_out, w_out, g_ffn, w_up, conv_w, conv_b, w_down,
                g_final, target, sharded):
    B, S, D = x.shape
    T = B * S
    x2d = x.reshape(T, D)
    mod3 = mod.reshape(B, 6, D)
    cos_t, sin_t = _rope_tables(S)
    sink = sw_sink.reshape(SW_HEADS)
    n_tiles = T // WIDE_TILE
    full = lambda g: g.reshape(N_DEV * g.shape[1], g.shape[2])

    rider = lambda w, steps, lo, n, into=None: [_gather_task(w, steps - 1, rows=(lo, n), into=into)] if sharded else []
    if sharded:
        qu, hd = w_up.shape[0] // 4, w_down.shape[0] // 2
    (h, qkv), got = _attn_in(x2d, mod3, g_attn, w_in, cos_t, sin_t, S,
                             [_gather_task(w_out, n_tiles // 2)] + rider(w_up, n_tiles, 0, qu) if sharded else [])
    if sharded:
        w_out, w_up_buf = full(got[0][0]), got[1][0]
    qkv3 = qkv.reshape(B, S, IN_WIDTH)
    (oa,), got = _na_fwd(qkv3, bias, rider(w_up, B * (NA_PAIRS // NA_STEP_PAIRS), qu, 2 * qu, w_up_buf) if sharded else [])
    if sharded:
        w_up_buf = got[0][0]
    oa = oa.reshape(T, NA_WIDTH)
    (ob,), got = _sw_fwd(sink, qkv3, rider(w_up, B * SW_FWD_SPLIT, 3 * qu, qu, w_up_buf) if sharded else [])
    if sharded:
        w_up = full(got[0][0])
    ob = ob.reshape(T, SW_WIDTH)
    (mixin, mix, x1), got = _attn_out(oa, ob, x2d, mod3, g_na_out, g_sw_out, w_out, S, rider(w_down, n_tiles, 0, hd) if sharded else [])
    (h2, val, gt), got = _ffn_up(x1, mod3, g_ffn, w_up, S, rider(w_down, n_tiles, hd, hd, got[0][0]) if sharded else [])
    if sharded:
        w_down = full(got[0][0])
    a, act, vd, dx2, df, gstat_f, bstat_f = _ffn_down(gt, val, conv_w, conv_b, w_down, x1, mod3, g_final, target.reshape(T, D), B, S)
    F = val.shape[1]

    dw_down = _matmul_tn(a, df, "dw_down")
    (dval, dgc, cstat), got = _ffn_down_bwd(df, w_down, act, vd, [_swap_task(_by_device(dw_down))] if sharded else [])
    if sharded:
        send_down, own_down = _chip_sums(_by_device(dw_down), got[0][0])
    (du, dx1, dmix, gstat_u, bstat_u, cstat_w), got = _ffn_up_bwd(dgc, dval, gt, conv_w, w_up, x1, mod3, g_ffn, dx2, mix, B, S,
                                                                  [_exchange_task(send_down)] if sharded else [])
    if sharded:
        dw_down = (own_down, got[0][0])
    dw_up = _matmul_tn(du, h2, "dw_up", tm=F)
    dw_out = _matmul_tn(mixin, dmix, "dw_out")
    (doa, dob, gstat_o), got = _attn_out_bwd(dmix, w_out, oa, ob, g_na_out, g_sw_out,
                                             [_swap_task(_by_device(dw_up)), _swap_task(_by_device(dw_out))] if sharded else [])
    if sharded:
        send_up, own_up = _chip_sums(_by_device(dw_up), got[0][0])
        send_out, own_out = _chip_sums(_by_device(dw_out), got[1][0])
    (dqa, dka, dva, dbt), got = _na_bwd(qkv3, bias, doa.reshape(B, S, NA_WIDTH),
                                        [_exchange_task(send_up), _exchange_task(send_out)] if sharded else [])
    if sharded:
        dw_up, dw_out = (own_up, got[0][0]), (own_out, got[1][0])
    dqb, dkb, dvb, dsink = _sw_bwd(sink, qkv3, dob.reshape(B, S, SW_WIDTH))
    r2 = lambda t: t.reshape(T, t.shape[-1])
    grad_x, dproj, gstat_i, bstat_i = _attn_in_bwd(r2(dqa), r2(dka), r2(dva), r2(dqb), r2(dkb), r2(dvb), cos_t, sin_t, w_in, x2d, mod3,
                                                   g_attn, dx1, B, S)
    dw_in = _matmul_tn(dproj, h, "dw_in")

    dmod = jnp.stack([bstat_i[:, 0], bstat_i[:, 1], bstat_u[:, 2], bstat_u[:, 0], bstat_u[:, 1], bstat_f[:, 0]], axis=1)
    small = dict(g_attn=gstat_i[0], g_ffn=gstat_u[0], g_final=gstat_f[0], loss=gstat_f[1, 0], g_na_out=gstat_o[0], g_sw_out=gstat_o[1],
                 sw_sink=dsink[:, 0], conv_b=cstat[0], conv_w=cstat_w[1:4], dbt=dbt)
    return grad_x.reshape(B, S, D), dict(w_in=dw_in, w_out=dw_out, w_up=dw_up, w_down=dw_down), dmod, small


def _pad_lanes(v, w):
    return jnp.pad(v, (0, w - v.shape[0]))


def kernel(x, c, w_ada, b_ada, g_attn, w_in, na_rpb, sw_sink, g_na_out, g_sw_out, w_out, g_ffn, w_up, conv_w, conv_b, w_down, g_final, loss_target, m_w_ada, m_b_ada, m_g_attn, m_w_in, m_na_rpb, m_sw_sink, m_g_na_out, m_g_sw_out, m_w_out, m_g_ffn, m_w_up, m_conv_w, m_conv_b, m_w_down, m_g_final, v_w_ada, v_b_ada, v_g_attn, v_w_in, v_na_rpb, v_sw_sink, v_g_na_out, v_g_sw_out, v_w_out, v_g_ffn, v_w_up, v_conv_w, v_conv_b, v_w_down, v_g_final):
    B, S, D = x.shape
    me = 4 * lax.axis_index("x") + 2 * lax.axis_index("y") + lax.axis_index("c")
    ada_c = w_ada.shape[2]
    F_l = conv_w.shape[2]

    cw_l = jnp.pad(conv_w[0], ((0, 8 - conv_w.shape[1]), (0, 0)))
    c_l = jnp.pad(c, ((0, 8 - B), (0, 0)))
    tr = {"w_in", "w_up"}
    w_in_t = jnp.transpose(w_in[0])
    shards = dict(w_out=w_out[0].astype(BF16), w_up=jnp.transpose(w_up[0]).astype(BF16), w_down=w_down[0].astype(BF16))

    b_ada_l = lax.dynamic_slice(b_ada, (0, me * ada_c), (1, ada_c))
    slabs, mod_all, w_in_all = _ada_fwd(jnp.concatenate([c_l, cw_l], axis=1), w_ada[0], b_ada_l, w_in_t.astype(BF16), B)
    c_all = slabs[:, :, :D].reshape(N_DEV * 8, D)
    conv_w_f = jnp.transpose(slabs[:, :3, D:], (1, 0, 2)).reshape(3, N_DEV * F_l)
    mod_mine = lax.dynamic_slice(mod_all, (0, me * B, 0), (N_DEV, B, ada_c))
    mod = jnp.transpose(mod_mine, (1, 0, 2)).reshape(B, N_DEV * ada_c)
    w_in_f = w_in_all.reshape(N_DEV * w_in_t.shape[0], D)

    bias = _na_bias_table(na_rpb[0])

    grad_x, dw, dmod, small = _local_step(x, mod, g_attn, w_in_f, bias, sw_sink, g_na_out, g_sw_out, shards["w_out"], g_ffn,
                                          shards["w_up"], conv_w_f, conv_b, shards["w_down"], g_final.reshape(1, D), loss_target,
                                          sharded=True)
    drpb = _na_bias_grad(small["dbt"])

    row2 = jnp.concatenate([small["g_attn"], small["g_ffn"], small["g_final"], small["g_na_out"], small["g_sw_out"],
                            _pad_lanes(small["sw_sink"], 128), _pad_lanes(small["loss"].reshape(1), 128)])
    rows = [dmod.reshape(B, 6 * D)[0], dmod.reshape(B, 6 * D)[1], _pad_lanes(row2, PACK_W), _pad_lanes(small["conv_b"], PACK_W),
            _pad_lanes(drpb, PACK_W)] + [_pad_lanes(small["conv_w"][k], PACK_W) for k in range(3)]
    weights = dict(w_ada=w_ada, b_ada=b_ada, g_attn=g_attn, w_in=w_in, na_rpb=na_rpb, sw_sink=sw_sink, g_na_out=g_na_out,
                   g_sw_out=g_sw_out, w_out=w_out, g_ffn=g_ffn, w_up=w_up, conv_w=conv_w, conv_b=conv_b, w_down=w_down, g_final=g_final)
    ms = dict(w_ada=m_w_ada, b_ada=m_b_ada, g_attn=m_g_attn, w_in=m_w_in, na_rpb=m_na_rpb, sw_sink=m_sw_sink, g_na_out=m_g_na_out,
              g_sw_out=m_g_sw_out, w_out=m_w_out, g_ffn=m_g_ffn, w_up=m_w_up, conv_w=m_conv_w, conv_b=m_conv_b, w_down=m_w_down, g_final=m_g_final)
    vs = dict(w_ada=v_w_ada, b_ada=v_b_ada, g_attn=v_g_attn, w_in=v_w_in, na_rpb=v_na_rpb, sw_sink=v_sw_sink, g_na_out=v_g_na_out,
              g_sw_out=v_g_sw_out, w_out=v_w_out, g_ffn=v_g_ffn, w_up=v_w_up, conv_w=v_conv_w, conv_b=v_conv_b, w_down=v_w_down, g_final=v_g_final)
    names = list(weights)
    grads, deltas, new_m, new_v = {}, {}, {}, {}
    flat = lambda t: t.reshape(1, -1)

    def shard2d(nm):
        if nm in tr:
            return (lambda t: jnp.transpose(t[0])), (lambda t: jnp.transpose(t)[None])
        return (lambda t: t[0]), (lambda t: t[None])

    def finish_sum(nm, own, recvb, tasks=()):
        r, back = shard2d(nm)
        (g2, d_, m_, v_), got = _sum_adamw(own, recvb, r(weights[nm]), r(ms[nm]), r(vs[nm]), "adamw_" + nm, tasks)
        grads[nm], deltas[nm], new_m[nm], new_v[nm] = back(g2), back(d_), back(m_), back(v_)
        return got

    g8_in = _by_device(dw["w_in"])
    got = finish_sum("w_down", *dw["w_down"], [_swap_task(g8_in)])
    send_in, own_in = _chip_sums(g8_in, got[0][0])
    n_up = _sum_adamw_steps(dw["w_up"][0].shape[0])
    got = finish_sum("w_up", *dw["w_up"], [_exchange_task(send_in), _gather_task(jnp.stack(rows), n_up - 1)])
    packs = got[1][0]
    finish_sum("w_in", own_in, got[0][0])
    finish_sum("w_out", *dw["w_out"])

    where = dict(b_ada=((0, 1), 0), g_attn=((2,), 0), g_ffn=((2,), D), g_final=((2,), 2 * D), g_na_out=((2,), 3 * D),
                 g_sw_out=((2,), 3 * D + NA_WIDTH), sw_sink=((2,), 3 * D + NA_WIDTH + SW_WIDTH), conv_b=((3,), 0), na_rpb=((4,), 0))
    tot, small_out = _pack_sum_adamw(packs.reshape(N_DEV * 8, PACK_W),
                                     [(flat(weights[n]), flat(ms[n]), flat(vs[n])) + where[n] for n in where])
    for n, (g_, d_, m_, v_) in zip(where, small_out):
        shp = weights[n].shape
        grads[n], deltas[n], new_m[n], new_v[n] = g_.reshape(shp), d_.reshape(shp), m_.reshape(shp), v_.reshape(shp)
    loss = tot[2, 3 * D + NA_WIDTH + SW_WIDTH + 128]

    dmod_cols = lax.dynamic_slice(packs.reshape(N_DEV * 8, PACK_W), (0, me * ada_c), (N_DEV * 8, ada_c))
    for nm, g2 in (("w_ada", _ada_bwd(c_all, dmod_cols)), ("conv_w", lax.dynamic_slice(tot[5:8], (0, me * F_l), (3, F_l)))):
        r, back = shard2d(nm)
        d_, m_, v_ = _adamw(r(weights[nm]), g2, r(ms[nm]), r(vs[nm]), "adamw_" + nm)
        grads[nm], deltas[nm], new_m[nm], new_v[nm] = back(g2), back(d_), back(m_), back(v_)
    return (loss, grad_x, *[grads[n] for n in names], *[deltas[n] for n in names], *[new_m[n] for n in names],
            *[new_v[n] for n in names])
```

```python
import functools

import numpy as np
import jax
import jax.numpy as jnp
from jax import lax
from jax.experimental import pallas as pl
from jax.experimental.pallas import tpu as pltpu

F32, BF16 = jnp.float32, jnp.bfloat16
MESH_ID = pl.DeviceIdType.MESH
N_DEV = 8

HEAD_DIM = 64
NA_HEADS = 8
SW_HEADS = 8
SW_KV_HEADS = 2
SW_GROUP = SW_HEADS // SW_KV_HEADS
NA_WIDTH = NA_HEADS * HEAD_DIM
SW_WIDTH = SW_HEADS * HEAD_DIM
SW_KV_WIDTH = SW_KV_HEADS * HEAD_DIM
ROPE_WIDTH = SW_WIDTH + SW_KV_WIDTH
IN_WIDTH = 3 * NA_WIDTH + SW_WIDTH + 2 * SW_KV_WIDTH
ROPE_LO = 3 * NA_WIDTH
GRID_W = 64
NA_ROWS_MAX = 8
NA_COLS = 16
N_DR = 2 * NA_ROWS_MAX - 1
N_DC = 2 * NA_COLS - 1
SW_WINDOW = 128
SW_BLOCK = 128
ROPE_THETA = 10000.0
EPS = 1e-6
NEG = -1e30
Q_SCALE = HEAD_DIM ** -0.5

ADAM_LR = 0.001
ADAM_B1 = 0.9
ADAM_B2 = 0.999
ADAM_EPS = 1e-08
ADAM_WD = 0.01
ADAM_STEP = 10

TOKEN_TILE = 256
WIDE_TILE = 512
VMEM_LIMIT = 56 * 1024 * 1024

PACK_W = 6144


def _nn(a, b):
    return jnp.dot(a, b, preferred_element_type=F32)


def _nt(a, b):
    return lax.dot_general(a, b, (((1,), (1,)), ((), ())), preferred_element_type=F32)


def _tn(a, b):
    return lax.dot_general(a, b, (((0,), (0,)), ((), ())), preferred_element_type=F32)


def _rms(x):
    r = lax.rsqrt(jnp.mean(x * x, axis=-1, keepdims=True) + EPS)
    return x * r, r


def _rms_bwd(xn, r, gy):
    return r * (gy - xn * jnp.mean(xn * gy, axis=-1, keepdims=True))


def _params(*sem):
    return pltpu.CompilerParams(dimension_semantics=sem, vmem_limit_bytes=VMEM_LIMIT)


def _full(shape):
    n = len(shape)
    return pl.BlockSpec(shape, lambda *_: (0,) * n)


def _mesh_pos():
    return lax.axis_index("x"), lax.axis_index("y"), lax.axis_index("c")


def _row_chunk(r):
    for rc in (128, 64, 32, 16):
        if r % rc == 0:
            return rc
    raise ValueError(f"rows {r} not a multiple of 16")


class _Task:
    def __init__(self, inputs, out_shapes, sems, start, finish, mid=None, mid_step=None, alias=None):
        self.inputs, self.out_shapes, self.sems = list(inputs), list(out_shapes), list(sems)
        self.start, self.finish, self.mid, self.mid_step = start, finish, mid, mid_step
        self.alias = alias


def _hosted_call(body, name, grid, in_specs, out_specs, out_shape, operands, tasks, scratch_shapes=()):
    n_in, n_out, n_scr = len(in_specs), len(out_specs), len(scratch_shapes)
    t_in = [len(t.inputs) for t in tasks]
    t_out = [len(t.out_shapes) for t in tasks]
    t_sem = [len(t.sems) for t in tasks]
    n_steps = int(np.prod(grid))

    def wrapped(*refs):
        ins, rest = refs[:n_in], refs[n_in:]
        task_ins, rest = rest[:sum(t_in)], rest[sum(t_in):]
        outs, rest = rest[:n_out], rest[n_out:]
        task_outs, rest = rest[:sum(t_out)], rest[sum(t_out):]
        scr, task_sems = rest[:n_scr], rest[n_scr:]
        step = pl.program_id(0)
        for ax in range(1, len(grid)):
            step = step * grid[ax] + pl.program_id(ax)
        parts = []
        oi = oo = os_ = 0
        for t, a, b, c in zip(tasks, t_in, t_out, t_sem):
            parts.append((t, task_ins[oi:oi + a], task_outs[oo:oo + b], task_sems[os_:os_ + c]))
            oi, oo, os_ = oi + a, oo + b, os_ + c
        for t, ti, to, ts in parts:
            pl.when(step == 0)(functools.partial(t.start, ti, to, ts))
            if t.mid is not None:
                pl.when(step == t.mid_step)(functools.partial(t.mid, ti, to, ts))
        body(*ins, *outs, *scr)
        for t, ti, to, ts in parts:
            pl.when(step == n_steps - 1)(functools.partial(t.finish, ti, to, ts))

    hbm = pl.BlockSpec(memory_space=pl.ANY)
    aliases, oi, oo = {}, n_in, n_out
    for t, a, b in zip(tasks, t_in, t_out):
        if t.alias is not None:
            aliases[oi + t.alias[0]] = oo + t.alias[1]
        oi, oo = oi + a, oo + b
    res = pl.pallas_call(
        wrapped, name=name, grid=grid,
        in_specs=list(in_specs) + [hbm] * sum(t_in),
        out_specs=list(out_specs) + [hbm] * sum(t_out),
        out_shape=list(out_shape) + [s for t in tasks for s in t.out_shapes],
        scratch_shapes=list(scratch_shapes) + [s for t in tasks for s in t.sems],
        input_output_aliases=aliases,
        compiler_params=_params(*(["arbitrary"] * len(grid))),
    )(*operands, *[a for t in tasks for a in t.inputs])
    own, extra = res[:n_out], res[n_out:]
    per_task, o = [], 0
    for b in t_out:
        per_task.append(extra[o:o + b])
        o += b
    return own, per_task


def _gather_task(shard, mid_step, rows=None, into=None):
    lo, n = (0, shard.shape[0]) if rows is None else rows

    def parts(ins, outs, sems):
        x_ref, out_ref, (send_sems, recv_sems, local_sem) = ins[0], outs[0], sems
        x_, y_, c_ = _mesh_pos()
        me, sibling = (x_, y_, c_), (x_, y_, 1 - c_)
        chips = [(1 - x_, y_), (x_, 1 - y_), (1 - x_, 1 - y_)]
        x_ref = x_ref.at[pl.ds(lo, n)]

        def rows(px, py, pc):
            return out_ref.at[4 * px + 2 * py + pc, pl.ds(lo, n)]

        def copy(k, block, to, src=None):
            return pltpu.make_async_remote_copy(
                src_ref=rows(*block) if src is None else src, dst_ref=rows(*block),
                send_sem=send_sems.at[k], recv_sem=recv_sems.at[k], device_id=to, device_id_type=MESH_ID)

        return dict(
            mine=lambda: pltpu.make_async_copy(x_ref, rows(*me), local_sem),
            first=lambda: [copy(0, me, sibling, src=x_ref)] + [copy(1 + j, me, (*chip, c_), src=x_ref) for j, chip in enumerate(chips)],
            passed=lambda: [copy(4 + j, (*chip, c_), sibling) for j, chip in enumerate(chips)],
            landed=lambda: [copy(1 + j, (*chip, c_), me) for j, chip in enumerate(chips)],
            last=lambda: [copy(0, sibling, me)] + [copy(4 + j, (*chip, 1 - c_), me) for j, chip in enumerate(chips)])

    def start(ins, outs, sems):
        p = parts(ins, outs, sems)
        p["mine"]().start()
        for cp in p["first"]():
            cp.start()

    def mid(ins, outs, sems):
        p = parts(ins, outs, sems)
        for cp, fw in zip(p["landed"](), p["passed"]()):
            cp.wait_recv()
            fw.start()

    def finish(ins, outs, sems):
        p = parts(ins, outs, sems)
        for cp in p["last"]():
            cp.wait_recv()
        for cp in p["first"]() + p["passed"]():
            cp.wait_send()
        p["mine"]().wait()

    return _Task([shard] if into is None else [shard, into], [jax.ShapeDtypeStruct((N_DEV,) + shard.shape, shard.dtype)],
                 [pltpu.SemaphoreType.DMA((7,)), pltpu.SemaphoreType.DMA((7,)), pltpu.SemaphoreType.DMA],
                 start, finish, mid, mid_step, alias=None if into is None else (1, 0))


def _swap_task(g8):
    _, R, C = g8.shape

    def copies(ins, outs, sems):
        (g_ref,), (recv_ref,), (ss, rs) = ins, outs, sems
        x_, y_, c_ = _mesh_pos()
        return [pltpu.make_async_remote_copy(src_ref=g_ref.at[2 * k + (1 - c_)], dst_ref=recv_ref.at[k], send_sem=ss.at[k],
                                             recv_sem=rs.at[k], device_id=(x_, y_, 1 - c_), device_id_type=MESH_ID)
                for k in range(4)]

    def start(ins, outs, sems):
        for cp in copies(ins, outs, sems):
            cp.start()

    def finish(ins, outs, sems):
        cps = copies(ins, outs, sems)
        for cp in cps:
            cp.wait_recv()
        for cp in cps:
            cp.wait_send()

    return _Task([g8], [jax.ShapeDtypeStruct((4, R, C), g8.dtype)],
                 [pltpu.SemaphoreType.DMA((4,)), pltpu.SemaphoreType.DMA((4,))], start, finish)


def _chip_sums(g8, recva):
    _, R, C = g8.shape
    rc = _row_chunk(R)

    def body(g_ref, a_ref, send_ref, own_ref):
        x_, y_, c_ = _mesh_pos()
        chips = [(1 - x_, y_), (x_, 1 - y_), (1 - x_, 1 - y_), (x_, y_)]

        def chunk(i, carry):
            rows = pl.ds(pl.multiple_of(i * rc, rc), rc)
            for j, (tx, ty) in enumerate(chips):
                k = 2 * tx + ty
                s = g_ref[2 * k + c_, rows, :].astype(F32) + a_ref[k, rows, :].astype(F32)
                if j < 3:
                    send_ref[j, rows, :] = s.astype(BF16)
                else:
                    own_ref[rows, :] = s
            return carry

        lax.fori_loop(0, R // rc, chunk, 0)

    return pl.pallas_call(body, name="chip_sums", out_shape=[jax.ShapeDtypeStruct((3, R, C), BF16), jax.ShapeDtypeStruct((R, C), F32)],
                          compiler_params=pltpu.CompilerParams(vmem_limit_bytes=VMEM_LIMIT))(g8, recva)


def _exchange_task(sendb):
    def copies(ins, outs, sems):
        (s_ref,), (recv_ref,), (ss, rs) = ins, outs, sems
        x_, y_, c_ = _mesh_pos()
        flips = [(1 - x_, y_), (x_, 1 - y_), (1 - x_, 1 - y_)]
        return [pltpu.make_async_remote_copy(src_ref=s_ref.at[j], dst_ref=recv_ref.at[j], send_sem=ss.at[j], recv_sem=rs.at[j],
                                             device_id=(tx, ty, c_), device_id_type=MESH_ID) for j, (tx, ty) in enumerate(flips)]

    def start(ins, outs, sems):
        for cp in copies(ins, outs, sems):
            cp.start()

    def finish(ins, outs, sems):
        cps = copies(ins, outs, sems)
        for cp in cps:
            cp.wait_recv()
        for cp in cps:
            cp.wait_send()

    return _Task([sendb], [jax.ShapeDtypeStruct(sendb.shape, sendb.dtype)],
                 [pltpu.SemaphoreType.DMA((3,)), pltpu.SemaphoreType.DMA((3,))], start, finish)


def _silu(v):
    return v * (1.0 / (1.0 + jnp.exp(-v)))


def _ada_fwd(c_slab, w_ada_l, b_ada_l, w_in_shard, n_seq):
    W = c_slab.shape[1]
    D, cols = w_ada_l.shape
    n_rows = N_DEV * n_seq
    t_c, t_w = _gather_task(c_slab, 0), _gather_task(w_in_shard, 0)
    t_m = _gather_task(jax.ShapeDtypeStruct((n_rows, cols), F32), 0)

    def body(c_ref, w_ref, b_ref, ws_ref, slabs_ref, mod_ref, win_ref, c_vm, m_vm, copy_sem, *sems):
        sc, sw, sm = sems[0:3], sems[3:6], sems[6:9]
        t_c.start((c_ref,), (slabs_ref,), sc)
        t_w.start((ws_ref,), (win_ref,), sw)
        t_c.mid((c_ref,), (slabs_ref,), sc)
        t_c.finish((c_ref,), (slabs_ref,), sc)
        cp = pltpu.make_async_copy(slabs_ref, c_vm, copy_sem)
        cp.start()
        cp.wait()
        c_all = c_vm[:, :, 0:D].reshape(N_DEV * 8, D)
        m64 = jnp.dot(_silu(c_all), w_ref[...], precision=lax.Precision.HIGHEST, preferred_element_type=F32) + b_ref[...]
        r = lax.broadcasted_iota(jnp.int32, (n_rows, N_DEV * 8), 0)
        c = lax.broadcasted_iota(jnp.int32, (n_rows, N_DEV * 8), 1)
        pick = jnp.where(c == 8 * (r // n_seq) + r % n_seq, 1.0, 0.0)
        m_vm[...] = jnp.dot(pick, m64, precision=lax.Precision.HIGHEST, preferred_element_type=F32)
        t_m.start((m_vm,), (mod_ref,), sm)
        t_w.mid((ws_ref,), (win_ref,), sw)
        t_m.mid((m_vm,), (mod_ref,), sm)
        t_m.finish((m_vm,), (mod_ref,), sm)
        t_w.finish((ws_ref,), (win_ref,), sw)

    hbm, vm = pl.BlockSpec(memory_space=pl.ANY), pl.BlockSpec(memory_space=pltpu.VMEM)
    return pl.pallas_call(
        body, name="ada_fwd", in_specs=[hbm, vm, vm, hbm], out_specs=[hbm, hbm, hbm],
        out_shape=t_c.out_shapes + t_m.out_shapes + t_w.out_shapes,
        scratch_shapes=[pltpu.VMEM((N_DEV, 8, W), F32), pltpu.VMEM((n_rows, cols), F32), pltpu.SemaphoreType.DMA]
        + t_c.sems + t_w.sems + t_m.sems,
        compiler_params=pltpu.CompilerParams(vmem_limit_bytes=VMEM_LIMIT),
    )(c_slab, w_ada_l, b_ada_l, w_in_shard)


def _ada_bwd(c_all, dmod_cols):
    def body(c_ref, d_ref, o_ref):
        o_ref[...] = lax.dot_general(_silu(c_ref[...]), d_ref[...], (((0,), (0,)), ((), ())),
                                     precision=lax.Precision.HIGHEST, preferred_element_type=F32)
    return pl.pallas_call(body, name="ada_bwd", out_shape=jax.ShapeDtypeStruct((c_all.shape[1], dmod_cols.shape[1]), F32),
                          compiler_params=pltpu.CompilerParams(vmem_limit_bytes=VMEM_LIMIT))(c_all, dmod_cols)


NA_PAIRS = NA_HEADS // 2
N_DR_PAD = 16


def _na_bias_table(na_rpb):
    rev = jnp.pad(jnp.flip(na_rpb, axis=2), ((0, 0), (0, N_DR_PAD - N_DR), (0, GRID_W - N_DC)))
    rev = jnp.transpose(rev.reshape(NA_PAIRS, 2, N_DR_PAD, GRID_W), (0, 2, 1, 3)).reshape(NA_PAIRS, N_DR_PAD, 128)

    def body(r_ref, o_ref):
        k = lax.broadcasted_iota(jnp.int32, (GRID_W, 128), 0)
        lane = lax.broadcasted_iota(jnp.int32, (GRID_W, 128), 1)
        q = lane % GRID_W
        cs = jnp.clip(q - NA_COLS // 2, 0, GRID_W - NA_COLS)
        ok = (k >= cs) & (k < cs + NA_COLS)
        left = lane < GRID_W
        for dr in range(N_DR):
            row = jnp.broadcast_to(r_ref[0, dr:dr + 1, :], (GRID_W, 128))
            r0 = jnp.where(left, row, 0.0)
            r1 = jnp.where(left, pltpu.roll(row, GRID_W, axis=1), 0.0)
            y0 = pltpu.roll(r0, 128 - (NA_COLS - 1), axis=1, stride=1, stride_axis=0)
            y1 = pltpu.roll(r1, GRID_W - (NA_COLS - 1), axis=1, stride=1, stride_axis=0)
            o_ref[0, dr * GRID_W:(dr + 1) * GRID_W, :] = jnp.where(ok, jnp.where(left, y0, y1), NEG)

    return pl.pallas_call(
        body, name="rpb_expand", grid=(NA_PAIRS,),
        in_specs=[pl.BlockSpec((1, N_DR_PAD, 128), lambda p: (p, 0, 0))],
        out_specs=pl.BlockSpec((1, N_DR * GRID_W, 128), lambda p: (p, 0, 0)),
        out_shape=jax.ShapeDtypeStruct((NA_PAIRS, N_DR * GRID_W, 128), F32),
        compiler_params=_params("parallel"),
    )(rev)


def _na_bias_grad(db):
    a = np.arange(128)
    flip = jnp.asarray(((a[:, None] // GRID_W == a[None, :] // GRID_W)
                        & (a[:, None] % GRID_W + a[None, :] % GRID_W == GRID_W - 1)).astype(np.float32))

    def body(d_ref, j_ref, o_ref):
        o_ref[...] = jnp.zeros_like(o_ref)
        for dr in range(N_DR):
            t = jnp.dot(d_ref[0, dr * GRID_W:(dr + 1) * GRID_W, :], j_ref[...], precision=lax.Precision.HIGHEST, preferred_element_type=F32)
            t = pltpu.roll(t, GRID_W + NA_COLS, axis=1, stride=1, stride_axis=0)
            o_ref[0, dr:dr + 1, :] = jnp.sum(t, axis=0, keepdims=True)

    rows = pl.pallas_call(
        body, name="rpb_reduce", grid=(NA_PAIRS,),
        in_specs=[pl.BlockSpec((1, N_DR * GRID_W, 128), lambda p: (p, 0, 0)), _full((128, 128))],
        out_specs=pl.BlockSpec((1, N_DR_PAD, 128), lambda p: (p, 0, 0)),
        out_shape=jax.ShapeDtypeStruct((NA_PAIRS, N_DR_PAD, 128), F32),
        compiler_params=_params("parallel"),
    )(db, flip)
    g = rows.reshape(NA_PAIRS, N_DR_PAD, 2, GRID_W)[:, :N_DR, :, :N_DC]
    return jnp.transpose(g, (0, 2, 1, 3)).reshape(-1)


def _rope_tables(S):
    half = HEAD_DIM // 2
    inv = np.float32(ROPE_THETA) ** (-np.arange(half, dtype=np.float32) / np.float32(half))
    ang = np.arange(S).astype(np.float32)[:, None] * inv[None, :]
    cos, sin = np.cos(ang).astype(np.float32), np.sin(ang).astype(np.float32)
    return jnp.asarray(np.tile(np.concatenate([cos, cos], axis=1), (1, 2))), jnp.asarray(np.tile(np.concatenate([-sin, sin], axis=1), (1, 2)))


def _rope_spec(tps, tm=TOKEN_TILE):
    return pl.BlockSpec((tm, 2 * HEAD_DIM), lambda i: (i % tps, 0))


def _rot_half(t):
    w = t.shape[1]
    lane = lax.broadcasted_iota(jnp.int32, t.shape, 1)
    return jnp.where((lane % HEAD_DIM) < HEAD_DIM // 2, pltpu.roll(t, w - HEAD_DIM // 2, axis=1),
                     pltpu.roll(t, HEAD_DIM // 2, axis=1))


def _tok_spec(w, tm=TOKEN_TILE):
    return pl.BlockSpec((tm, w), lambda i: (i, 0))


def _mod_spec(tps, d):
    return pl.BlockSpec((1, 6, d), lambda i: (i // tps, 0, 0))


def _bstat_spec(tps, w):
    return pl.BlockSpec((1, 8, w), lambda i: (i // tps, 0, 0))


def _attn_in(x2d, mod3, g_attn, w_in, cos_t, sin_t, S, tasks=(), tm=WIDE_TILE):
    T, D = x2d.shape
    tps = S // tm

    def body(x_ref, mod_ref, g_ref, w_ref, cos_ref, sin_ref, h_ref, qkv_ref):
        xn, _ = _rms(x_ref[...])
        h = (xn * g_ref[...]) * (1.0 + mod_ref[0, 1:2, :]) + mod_ref[0, 0:1, :]
        hb = h.astype(BF16)
        h_ref[...] = hb
        proj = _nt(hb, w_ref[...])
        rb = proj[:, ROPE_LO:ROPE_LO + ROPE_WIDTH]
        reps = (1, ROPE_WIDTH // (2 * HEAD_DIM))
        rb = rb * jnp.tile(cos_ref[...], reps) + _rot_half(rb) * jnp.tile(sin_ref[...], reps)
        qkv_ref[:, 0:NA_WIDTH] = (proj[:, 0:NA_WIDTH] * Q_SCALE).astype(BF16)
        qkv_ref[:, NA_WIDTH:ROPE_LO] = proj[:, NA_WIDTH:ROPE_LO].astype(BF16)
        qkv_ref[:, ROPE_LO:ROPE_LO + SW_WIDTH] = (rb[:, 0:SW_WIDTH] * Q_SCALE).astype(BF16)
        qkv_ref[:, ROPE_LO + SW_WIDTH:ROPE_LO + ROPE_WIDTH] = rb[:, SW_WIDTH:].astype(BF16)
        qkv_ref[:, ROPE_LO + ROPE_WIDTH:] = proj[:, ROPE_LO + ROPE_WIDTH:].astype(BF16)

    return _hosted_call(
        body, "attn_in", (T // tm,),
        [_tok_spec(D, tm), _mod_spec(tps, D), _full((1, D)), _full(w_in.shape), _rope_spec(tps, tm), _rope_spec(tps, tm)],
        [_tok_spec(D, tm), _tok_spec(IN_WIDTH, tm)],
        [jax.ShapeDtypeStruct((T, D), BF16), jax.ShapeDtypeStruct((T, IN_WIDTH), BF16)],
        (x2d, mod3, g_attn, w_in, cos_t, sin_t), tasks)


def _attn_out(oa, ob, x2d, mod3, g_na, g_sw, w_out, S, tasks=(), tm=WIDE_TILE):
    T, D = x2d.shape
    tps = S // tm

    def body(oa_ref, ob_ref, x_ref, mod_ref, gna_ref, gsw_ref, w_ref, mixin_ref, mix_ref, x1_ref):
        oan, _ = _rms(oa_ref[...])
        obn, _ = _rms(ob_ref[...])
        mixin = jnp.concatenate([oan * gna_ref[...], obn * gsw_ref[...]], axis=1).astype(BF16)
        mixin_ref[...] = mixin
        mix = _nn(mixin, w_ref[...])
        mix_ref[...] = mix
        x1_ref[...] = x_ref[...] + mod_ref[0, 2:3, :] * mix

    return _hosted_call(
        body, "attn_out", (T // tm,),
        [_tok_spec(NA_WIDTH, tm), _tok_spec(SW_WIDTH, tm), _tok_spec(D, tm), _mod_spec(tps, D),
         _full((1, NA_WIDTH)), _full((1, SW_WIDTH)), _full(w_out.shape)],
        [_tok_spec(NA_WIDTH + SW_WIDTH, tm), _tok_spec(D, tm), _tok_spec(D, tm)],
        [jax.ShapeDtypeStruct((T, NA_WIDTH + SW_WIDTH), BF16), jax.ShapeDtypeStruct((T, D), F32), jax.ShapeDtypeStruct((T, D), F32)],
        (oa, ob, x2d, mod3, g_na, g_sw, w_out), tasks)


def _ffn_up(x1, mod3, g_ffn, w_up, S, tasks=(), tm=WIDE_TILE):
    T, D = x1.shape
    F = w_up.shape[0] // 2
    tps = S // tm

    def body(x1_ref, mod_ref, g_ref, w_ref, h2_ref, val_ref, gt_ref):
        xn, _ = _rms(x1_ref[...])
        h2 = ((xn * g_ref[...]) * (1.0 + mod_ref[0, 4:5, :]) + mod_ref[0, 3:4, :]).astype(BF16)
        h2_ref[...] = h2
        u = _nt(h2, w_ref[...])
        val_ref[...] = u[:, :F].astype(BF16)
        gt_ref[...] = u[:, F:].astype(BF16)

    return _hosted_call(
        body, "ffn_up", (T // tm,), [_tok_spec(D, tm), _mod_spec(tps, D), _full((1, D)), _full(w_up.shape)],
        [_tok_spec(D, tm), _tok_spec(F, tm), _tok_spec(F, tm)],
        [jax.ShapeDtypeStruct((T, D), BF16), jax.ShapeDtypeStruct((T, F), BF16), jax.ShapeDtypeStruct((T, F), BF16)],
        (x1, mod3, g_ffn, w_up), tasks)


def _halo_specs(T, tps, w):
    per = TOKEN_TILE // 8
    prev = pl.BlockSpec((8, w), lambda i: (jnp.maximum(i * per - 1, 0), 0))
    nxt = pl.BlockSpec((8, w), lambda i: (jnp.minimum((i + 1) * per, T // 8 - 1), 0))
    return prev, nxt


def _seq_shifts(cur, before, after, ti, tps):
    tm = cur.shape[0]
    row = lax.broadcasted_iota(jnp.int32, cur.shape, 0)
    before = jnp.where(ti > 0, before.astype(F32), 0.0)
    after = jnp.where(ti < tps - 1, after.astype(F32), 0.0)
    return jnp.where(row == 0, before, pltpu.roll(cur, 1, axis=0)), jnp.where(row == tm - 1, after, pltpu.roll(cur, tm - 1, axis=0))


def _ffn_down(gt, val, conv_w, conv_b, w_down, x1, mod3, g_final, target, B, S):
    T, D = x1.shape
    F = gt.shape[1]
    tps = S // TOKEN_TILE
    prev, nxt = _halo_specs(T, tps, F)

    def body(gt_ref, prev_ref, next_ref, val_ref, cw_ref, cb_ref, w_ref, x1_ref, mod_ref, gf_ref, tgt_ref,
             a_ref, act_ref, vd_ref, dx2_ref, df_ref, gstat_ref, bstat_ref):
        i = pl.program_id(0)
        g = gt_ref[...].astype(F32)
        gprev, gnext = _seq_shifts(g, prev_ref[7:8, :], next_ref[0:1, :], i % tps, tps)
        gc = gprev * cw_ref[0:1, :] + g * cw_ref[1:2, :] + gnext * cw_ref[2:3, :] + cb_ref[...]
        sig = 1.0 / (1.0 + jnp.exp(-gc))
        act = gc * sig
        val = val_ref[...].astype(F32)
        act_ref[...] = act.astype(BF16)
        vd_ref[...] = (val * (sig + act - act * sig)).astype(BF16)
        a = (act * val).astype(BF16)
        a_ref[...] = a
        f = _nn(a, w_ref[...])
        gate = mod_ref[0, 5:6, :]
        x2 = x1_ref[...] + gate * f
        xn, r = _rms(x2)
        err = xn * gf_ref[...] - tgt_ref[...]
        dy = err * (1.0 / D)
        dx2 = _rms_bwd(xn, r, dy * gf_ref[...])
        dx2_ref[...] = dx2
        df_ref[...] = (gate * dx2).astype(BF16)

        @pl.when(i == 0)
        def _():
            gstat_ref[...] = jnp.zeros_like(gstat_ref)

        @pl.when(i % tps == 0)
        def _():
            bstat_ref[...] = jnp.zeros_like(bstat_ref)

        gstat_ref[0:1, :] += jnp.sum(dy * xn, axis=0, keepdims=True)
        tile_loss = jnp.sum(jnp.sum(err * err, axis=1, keepdims=True), axis=0, keepdims=True) * (0.5 / D)
        gstat_ref[1:2, :] += jnp.broadcast_to(tile_loss, (1, D))
        bstat_ref[0, 0:1, :] += jnp.sum(dx2 * f, axis=0, keepdims=True)

    return pl.pallas_call(
        body, name="ffn_down", grid=(T // TOKEN_TILE,),
        in_specs=[_tok_spec(F), prev, nxt, _tok_spec(F), _full(conv_w.shape), _full((1, F)), _full(w_down.shape),
                  _tok_spec(D), _mod_spec(tps, D), _full((1, D)), _tok_spec(D)],
        out_specs=[_tok_spec(F), _tok_spec(F), _tok_spec(F), _tok_spec(D), _tok_spec(D), _full((8, D)), _bstat_spec(tps, D)],
        out_shape=[jax.ShapeDtypeStruct((T, F), BF16), jax.ShapeDtypeStruct((T, F), BF16), jax.ShapeDtypeStruct((T, F), BF16),
                   jax.ShapeDtypeStruct((T, D), F32), jax.ShapeDtypeStruct((T, D), BF16),
                   jax.ShapeDtypeStruct((8, D), F32), jax.ShapeDtypeStruct((B, 8, D), F32)],
        compiler_params=_params("arbitrary"),
    )(gt, gt, gt, val, conv_w, conv_b, w_down, x1, mod3, g_final, target)


def _ffn_down_bwd(df, w_down, act, vd, tasks=()):
    T, D = df.shape
    F = act.shape[1]

    def body(df_ref, w_ref, act_ref, vd_ref, dval_ref, dgc_ref, cstat_ref):
        da = _nt(df_ref[...], w_ref[...])
        dval_ref[...] = (da * act_ref[...].astype(F32)).astype(BF16)
        dgc = da * vd_ref[...].astype(F32)
        dgc_ref[...] = dgc.astype(BF16)

        @pl.when(pl.program_id(0) == 0)
        def _():
            cstat_ref[...] = jnp.zeros_like(cstat_ref)

        cstat_ref[0:1, :] += jnp.sum(dgc, axis=0, keepdims=True)

    return _hosted_call(
        body, "ffn_down_bwd", (T // TOKEN_TILE,),
        [_tok_spec(D), _full(w_down.shape), _tok_spec(F), _tok_spec(F)],
        [_tok_spec(F), _tok_spec(F), _full((8, F))],
        [jax.ShapeDtypeStruct((T, F), BF16), jax.ShapeDtypeStruct((T, F), BF16), jax.ShapeDtypeStruct((8, F), F32)],
        (df, w_down, act, vd), tasks)


def _ffn_up_bwd(dgc, dval, gt, conv_w, w_up, x1, mod3, g_ffn, dx2, mix, B, S, tasks=()):
    T, D = x1.shape
    F = dgc.shape[1]
    tps = S // TOKEN_TILE
    prev, nxt = _halo_specs(T, tps, F)

    def body(dgc_ref, prev_ref, next_ref, dval_ref, gt_ref, cw_ref, w_ref, x1_ref, mod_ref, g_ref, dx2_ref, mix_ref,
             du_ref, dx1_ref, dmix_ref, gstat_ref, bstat_ref, cstat_ref):
        i = pl.program_id(0)
        d = dgc_ref[...].astype(F32)
        dprev, dnext = _seq_shifts(d, prev_ref[7:8, :], next_ref[0:1, :], i % tps, tps)
        g = gt_ref[...].astype(F32)

        @pl.when(i == 0)
        def _():
            cstat_ref[...] = jnp.zeros_like(cstat_ref)

        cstat_ref[1:2, :] += jnp.sum(dnext * g, axis=0, keepdims=True)
        cstat_ref[2:3, :] += jnp.sum(d * g, axis=0, keepdims=True)
        cstat_ref[3:4, :] += jnp.sum(dprev * g, axis=0, keepdims=True)
        dgt = dnext * cw_ref[0:1, :] + d * cw_ref[1:2, :] + dprev * cw_ref[2:3, :]
        du = jnp.concatenate([dval_ref[...], dgt.astype(BF16)], axis=1)
        du_ref[...] = du
        dh2 = _nn(du, w_ref[...])
        xn, r = _rms(x1_ref[...])
        scale1 = 1.0 + mod_ref[0, 4:5, :]
        xg = xn * g_ref[...]
        dx1 = dx2_ref[...] + _rms_bwd(xn, r, dh2 * g_ref[...] * scale1)
        dx1_ref[...] = dx1
        dmix_ref[...] = (mod_ref[0, 2:3, :] * dx1).astype(BF16)

        @pl.when(i == 0)
        def _():
            gstat_ref[...] = jnp.zeros_like(gstat_ref)

        @pl.when(i % tps == 0)
        def _():
            bstat_ref[...] = jnp.zeros_like(bstat_ref)

        gstat_ref[0:1, :] += jnp.sum(dh2 * scale1 * xn, axis=0, keepdims=True)
        bstat_ref[0, 0:1, :] += jnp.sum(dh2, axis=0, keepdims=True)
        bstat_ref[0, 1:2, :] += jnp.sum(dh2 * xg, axis=0, keepdims=True)
        bstat_ref[0, 2:3, :] += jnp.sum(dx1 * mix_ref[...], axis=0, keepdims=True)

    return _hosted_call(
        body, "ffn_up_bwd", (T // TOKEN_TILE,),
        [_tok_spec(F), prev, nxt, _tok_spec(F), _tok_spec(F), _full(conv_w.shape), _full(w_up.shape), _tok_spec(D),
         _mod_spec(tps, D), _full((1, D)), _tok_spec(D), _tok_spec(D)],
        [_tok_spec(2 * F), _tok_spec(D), _tok_spec(D), _full((8, D)), _bstat_spec(tps, D), _full((8, F))],
        [jax.ShapeDtypeStruct((T, 2 * F), BF16), jax.ShapeDtypeStruct((T, D), F32), jax.ShapeDtypeStruct((T, D), BF16),
         jax.ShapeDtypeStruct((8, D), F32), jax.ShapeDtypeStruct((B, 8, D), F32), jax.ShapeDtypeStruct((8, F), F32)],
        (dgc, dgc, dgc, dval, gt, conv_w, w_up, x1, mod3, g_ffn, dx2, mix), tasks)


def _attn_out_bwd(dmix, w_out, oa, ob, g_na, g_sw, tasks=(), tm=WIDE_TILE):
    T, D = dmix.shape

    def body(dmix_ref, w_ref, oa_ref, ob_ref, gna_ref, gsw_ref, doa_ref, dob_ref, gstat_ref):
        dmixin = _nt(dmix_ref[...], w_ref[...])

        @pl.when(pl.program_id(0) == 0)
        def _():
            gstat_ref[...] = jnp.zeros_like(gstat_ref)

        for k, (o_ref, g_ref, do_ref) in enumerate(((oa_ref, gna_ref, doa_ref), (ob_ref, gsw_ref, dob_ref))):
            dn = dmixin[:, k * NA_WIDTH:(k + 1) * NA_WIDTH]
            on, r = _rms(o_ref[...])
            gstat_ref[k:k + 1, :] += jnp.sum(dn * on, axis=0, keepdims=True)
            do_ref[...] = _rms_bwd(on, r, dn * g_ref[...]).astype(BF16)

    hs = jax.ShapeDtypeStruct((T, NA_WIDTH), BF16)
    return _hosted_call(
        body, "attn_out_bwd", (T // tm,),
        [_tok_spec(D, tm), _full(w_out.shape), _tok_spec(NA_WIDTH, tm), _tok_spec(SW_WIDTH, tm), _full((1, NA_WIDTH)), _full((1, SW_WIDTH))],
        [_tok_spec(NA_WIDTH, tm), _tok_spec(SW_WIDTH, tm), _full((8, NA_WIDTH))],
        [hs, hs, jax.ShapeDtypeStruct((8, NA_WIDTH), F32)],
        (dmix, w_out, oa, ob, g_na, g_sw), tasks)


def _attn_in_bwd(dqa, dka, dva, dqb, dkb, dvb, cos_t, sin_t, w_in, x2d, mod3, g_attn, dx1, B, S, tm=WIDE_TILE):
    T, D = x2d.shape
    tps = S // tm

    def body(dqa_ref, dka_ref, dva_ref, dqb_ref, dkb_ref, dvb_ref, cos_ref, sin_ref, w_ref, x_ref, mod_ref, g_ref, dx1_ref,
             gx_ref, dproj_ref, gstat_ref, bstat_ref):
        i = pl.program_id(0)
        drb = jnp.concatenate([dqb_ref[...] * Q_SCALE, dkb_ref[...]], axis=1)
        reps = (1, ROPE_WIDTH // (2 * HEAD_DIM))
        drb = drb * jnp.tile(cos_ref[...], reps) + _rot_half(drb * jnp.tile(sin_ref[...], reps))
        dproj = jnp.concatenate([dqa_ref[...] * Q_SCALE, dka_ref[...], dva_ref[...], drb, dvb_ref[...]], axis=1).astype(BF16)
        dproj_ref[...] = dproj
        dh = _nn(dproj, w_ref[...])
        xn, r = _rms(x_ref[...])
        scale1 = 1.0 + mod_ref[0, 1:2, :]
        gx_ref[...] = dx1_ref[...] + _rms_bwd(xn, r, dh * g_ref[...] * scale1)

        @pl.when(i == 0)
        def _():
            gstat_ref[...] = jnp.zeros_like(gstat_ref)

        @pl.when(i % tps == 0)
        def _():
            bstat_ref[...] = jnp.zeros_like(bstat_ref)

        gstat_ref[0:1, :] += jnp.sum(dh * scale1 * xn, axis=0, keepdims=True)
        bstat_ref[0, 0:1, :] += jnp.sum(dh, axis=0, keepdims=True)
        bstat_ref[0, 1:2, :] += jnp.sum(dh * (xn * g_ref[...]), axis=0, keepdims=True)

    rope = _rope_spec(tps, tm)
    return pl.pallas_call(
        body, name="attn_in_bwd", grid=(T // tm,),
        in_specs=[_tok_spec(NA_WIDTH, tm), _tok_spec(NA_WIDTH, tm), _tok_spec(NA_WIDTH, tm), _tok_spec(SW_WIDTH, tm),
                  _tok_spec(SW_KV_WIDTH, tm), _tok_spec(SW_KV_WIDTH, tm), rope, rope, _full(w_in.shape), _tok_spec(D, tm),
                  _mod_spec(tps, D), _full((1, D)), _tok_spec(D, tm)],
        out_specs=[_tok_spec(D, tm), _tok_spec(IN_WIDTH, tm), _full((8, D)), _bstat_spec(tps, D)],
        out_shape=[jax.ShapeDtypeStruct((T, D), F32), jax.ShapeDtypeStruct((T, IN_WIDTH), BF16),
                   jax.ShapeDtypeStruct((8, D), F32), jax.ShapeDtypeStruct((B, 8, D), F32)],
        compiler_params=_params("arbitrary"),
    )(dqa, dka, dva, dqb, dkb, dvb, cos_t, sin_t, w_in, x2d, mod3, g_attn, dx1)


def _matmul_tn(a, b, name, tm=None, tk=512):
    T, M = a.shape
    N = b.shape[1]
    tm = M if tm is None else tm
    nk = T // tk

    def body(a_ref, b_ref, o_ref, acc):
        k = pl.program_id(1)

        @pl.when(k == 0)
        def _():
            acc[...] = jnp.zeros_like(acc)

        acc[...] += _tn(a_ref[...], b_ref[...])

        @pl.when(k == nk - 1)
        def _():
            o_ref[...] = acc[...].astype(BF16)

    return pl.pallas_call(
        body, name=name, grid=(M // tm, nk),
        in_specs=[pl.BlockSpec((tk, tm), lambda i, k: (k, i)), pl.BlockSpec((tk, N), lambda i, k: (k, 0))],
        out_specs=pl.BlockSpec((tm, N), lambda i, k: (i, 0)),
        out_shape=jax.ShapeDtypeStruct((M, N), BF16),
        scratch_shapes=[pltpu.VMEM((tm, N), F32)],
        compiler_params=_params("parallel", "arbitrary"),
    )(a, b)


def _na_geometry(S):
    rows = S // GRID_W
    wr = min(NA_ROWS_MAX, rows)
    return rows, wr


def _na_window(r, rows, wr):
    rs = jnp.clip(r - wr // 2, 0, rows - wr)
    return pl.multiple_of(rs * GRID_W, GRID_W), pl.multiple_of((rs - r + NA_ROWS_MAX - 1) * GRID_W, GRID_W)


NA_STEP_PAIRS = 2
NA_GW = NA_STEP_PAIRS * 128
NA_BWD_ROWS = 4
NA_ROWS_PER_STEP = 4


def _na_specs(S, kw_n, order):
    ng = NA_PAIRS // NA_STEP_PAIRS

    def col(k):
        return pl.BlockSpec((1, S, NA_GW), lambda *ids: (order(*ids)[0], 0, k * ng + order(*ids)[1]))
    bias = pl.BlockSpec((NA_STEP_PAIRS, N_DR * GRID_W, 128), lambda *ids: (order(*ids)[1], 0, 0))
    out = pl.BlockSpec((1, S, NA_GW), lambda *ids: (order(*ids)[0], 0, order(*ids)[1]))
    return col(0), col(1), col(2), bias, out


def _block_diag(t):
    left = lax.broadcasted_iota(jnp.int32, t.shape, 1) < HEAD_DIM
    zero = jnp.zeros_like(t)
    return jnp.concatenate([jnp.where(left, t, zero), jnp.where(left, zero, t)], axis=0)


def _diag_blocks(res):
    left = lax.broadcasted_iota(jnp.int32, (HEAD_DIM, 128), 1) < HEAD_DIM
    return jnp.where(left, res[:HEAD_DIM], res[HEAD_DIM:])


def _col_softmax(st):
    e = jnp.exp(st - jnp.max(st, axis=0, keepdims=True))
    return e * (1.0 / jnp.sum(e, axis=0, keepdims=True))


def _na_fwd(qkv, bias, tasks=()):
    B, S, _ = qkv.shape
    rows, wr = _na_geometry(S)
    kw_n = wr * GRID_W

    def body(q_ref, k_ref, v_ref, b_ref, o_ref):
        def step(it, carry):
            win = [_na_window(it * NA_ROWS_PER_STEP + u, rows, wr) for u in range(NA_ROWS_PER_STEP)]
            qrows = [pl.ds(pl.multiple_of((it * NA_ROWS_PER_STEP + u) * GRID_W, GRID_W), GRID_W) for u in range(NA_ROWS_PER_STEP)]
            krows = [pl.ds(w[0], kw_n) for w in win]
            brows = [pl.ds(w[1], kw_n) for w in win]
            lanes = [pl.ds(p * 128, 128) for p in range(NA_STEP_PAIRS)]
            chains = [(u, p) for u in range(NA_ROWS_PER_STEP) for p in range(NA_STEP_PAIRS)]
            st = {(u, p): _nt(k_ref[0, krows[u], lanes[p]], _block_diag(q_ref[0, qrows[u], lanes[p]])) for u, p in chains}
            pn = {(u, p): _col_softmax(st[(u, p)] + b_ref[p, brows[u], :]).astype(BF16) for u, p in chains}
            out = {(u, p): _diag_blocks(_tn(pn[(u, p)], v_ref[0, krows[u], lanes[p]])) for u, p in chains}
            for u in range(NA_ROWS_PER_STEP):
                o_ref[0, qrows[u], :] = jnp.concatenate([out[(u, p)] for p in range(NA_STEP_PAIRS)], axis=1)
            return carry

        lax.fori_loop(0, rows // NA_ROWS_PER_STEP, step, 0)

    q, k, v, bs, out = _na_specs(S, kw_n, lambda b, g: (b, g))
    return _hosted_call(body, "na_fwd", (B, NA_PAIRS // NA_STEP_PAIRS), [q, k, v, bs], [out],
                        [jax.ShapeDtypeStruct((B, S, NA_WIDTH), F32)], (qkv, qkv, qkv, bias), tasks)


def _na_bwd(qkv, bias, doa, tasks=()):
    B, S, _ = qkv.shape
    rows, wr = _na_geometry(S)
    kw_n = wr * GRID_W

    def body(q_ref, k_ref, v_ref, b_ref, do_ref, dq_ref, dk_ref, dv_ref, db_ref):
        @pl.when(pl.program_id(1) == 0)
        def _():
            db_ref[...] = jnp.zeros_like(db_ref)

        dk_ref[...] = jnp.zeros_like(dk_ref)
        dv_ref[...] = jnp.zeros_like(dv_ref)

        def step(it, carry):
            nu, pairs = range(NA_BWD_ROWS), range(NA_STEP_PAIRS)
            win = [_na_window(it * NA_BWD_ROWS + u, rows, wr) for u in nu]
            qrows = [pl.ds(pl.multiple_of((it * NA_BWD_ROWS + u) * GRID_W, GRID_W), GRID_W) for u in nu]
            krows = [pl.ds(w[0], kw_n) for w in win]
            brows = [pl.ds(w[1], kw_n) for w in win]
            lanes = [pl.ds(p * 128, 128) for p in pairs]
            chains = [(u, p) for u in nu for p in pairs]
            kp = {(u, p): k_ref[0, krows[u], lanes[p]] for u, p in chains}
            qbd = {(u, p): _block_diag(q_ref[0, qrows[u], lanes[p]]) for u, p in chains}
            dobd = {(u, p): _block_diag(do_ref[0, qrows[u], lanes[p]]) for u, p in chains}
            st = {c: _nt(kp[c], qbd[c]) for c in chains}
            dpt = {(u, p): _nt(v_ref[0, krows[u], lanes[p]], dobd[(u, p)]) for u, p in chains}
            pn = {(u, p): _col_softmax(st[(u, p)] + b_ref[p, brows[u], :]) for u, p in chains}
            dst = {c: pn[c] * (dpt[c] - jnp.sum(pn[c] * dpt[c], axis=0, keepdims=True)) for c in chains}
            dsb = {c: dst[c].astype(BF16) for c in chains}
            dq = {c: _diag_blocks(_tn(dsb[c], kp[c])) for c in chains}
            dk = {c: _nn(dsb[c], qbd[c]) for c in chains}
            dv = {c: _nn(pn[c].astype(BF16), dobd[c]) for c in chains}
            for u in nu:
                dq_ref[0, qrows[u], :] = jnp.concatenate([dq[(u, p)] for p in pairs], axis=1)
                dk_ref[0, krows[u], :] += jnp.concatenate([dk[(u, p)] for p in pairs], axis=1)
                dv_ref[0, krows[u], :] += jnp.concatenate([dv[(u, p)] for p in pairs], axis=1)
                for p in pairs:
                    db_ref[p, brows[u], :] += dst[(u, p)]
            return carry

        lax.fori_loop(0, rows // NA_BWD_ROWS, step, 0)

    q, k, v, bs, out = _na_specs(S, kw_n, lambda g, b: (b, g))
    hs = jax.ShapeDtypeStruct((B, S, NA_WIDTH), F32)
    return _hosted_call(body, "na_bwd", (NA_PAIRS // NA_STEP_PAIRS, B), [q, k, v, bs, out], [out, out, out, bs],
                        [hs, hs, hs, jax.ShapeDtypeStruct((NA_PAIRS, N_DR * GRID_W, 128), F32)], (qkv, qkv, qkv, bias, doa), tasks)


SW_PAIRS = SW_HEADS // 2


def _sw_band(n, S):
    kw_n = 3 * SW_BLOCK
    start = pl.multiple_of(jnp.clip(n * SW_BLOCK - SW_BLOCK, 0, S - kw_n), SW_BLOCK)
    kpos = start + lax.broadcasted_iota(jnp.int32, (kw_n, SW_BLOCK), 0)
    qpos = n * SW_BLOCK + lax.broadcasted_iota(jnp.int32, (kw_n, SW_BLOCK), 1)
    return start, jnp.abs(qpos - kpos) <= SW_WINDOW


def _kv_halves(t):
    left = lax.broadcasted_iota(jnp.int32, t.shape, 1) < HEAD_DIM
    swapped = pltpu.roll(t, HEAD_DIM, axis=1)
    zero = jnp.zeros_like(t)
    return {(0, 0): jnp.where(left, t, zero), (0, 1): jnp.where(left, zero, swapped),
            (1, 0): jnp.where(left, swapped, zero), (1, 1): jnp.where(left, zero, t)}


def _sw_probs(st, ok, sk):
    st = jnp.where(ok, st, NEG)
    m = jnp.maximum(jnp.max(st, axis=0, keepdims=True), sk)
    e = jnp.exp(st - m)
    esk = jnp.exp(sk - m)
    inv = 1.0 / (jnp.sum(e, axis=0, keepdims=True) + esk)
    return e * inv, esk * inv


def _sw_specs(S):
    q = pl.BlockSpec((1, S, SW_WIDTH), lambda b, *_: (b, 0, ROPE_LO // SW_WIDTH))
    k = pl.BlockSpec((1, S, SW_KV_WIDTH), lambda b, *_: (b, 0, (ROPE_LO + SW_WIDTH) // SW_KV_WIDTH))
    v = pl.BlockSpec((1, S, SW_KV_WIDTH), lambda b, *_: (b, 0, (ROPE_LO + ROPE_WIDTH) // SW_KV_WIDTH))
    return q, k, v


SW_FWD_SPLIT = 2


def _sw_fwd(sink, qkv, tasks=()):
    B, S, _ = qkv.shape
    kw_n = 3 * SW_BLOCK

    def body(sink_ref, q_ref, k_ref, v_ref, o_ref):
        def step(n, carry):
            start, ok = _sw_band(n, S)
            qrows = pl.ds(pl.multiple_of(n * SW_BLOCK, SW_BLOCK), SW_BLOCK)
            krows = pl.ds(start, kw_n)
            kh, vh = _kv_halves(k_ref[0, krows, :]), _kv_halves(v_ref[0, krows, :])
            heads = [(p, e) for p in range(SW_PAIRS) for e in range(2)]
            qp = [q_ref[0, qrows, pl.ds(p * 128, 128)] for p in range(SW_PAIRS)]
            kv_of = lambda p: p // (SW_PAIRS // SW_KV_HEADS)
            st = {(p, e): _nt(kh[(kv_of(p), e)], qp[p]) for p, e in heads}
            pn = {(p, e): _sw_probs(st[(p, e)], ok, sink_ref[2 * p + e])[0].astype(BF16) for p, e in heads}
            outs = [_tn(pn[(p, 0)], vh[(kv_of(p), 0)]) + _tn(pn[(p, 1)], vh[(kv_of(p), 1)]) for p in range(SW_PAIRS)]
            o_ref[0, qrows, :] = jnp.concatenate(outs, axis=1)
            return carry

        half = (S // SW_BLOCK) // SW_FWD_SPLIT
        lax.fori_loop(pl.program_id(1) * half, (pl.program_id(1) + 1) * half, step, 0)

    q, k, v = _sw_specs(S)
    return _hosted_call(
        body, "sw_fwd", (B, SW_FWD_SPLIT), [pl.BlockSpec(memory_space=pltpu.SMEM), q, k, v],
        [pl.BlockSpec((1, S, SW_WIDTH), lambda b, s: (b, 0, 0))], [jax.ShapeDtypeStruct((B, S, SW_WIDTH), F32)],
        (sink, qkv, qkv, qkv), tasks)


def _sw_bwd(sink, qkv, dob):
    B, S, _ = qkv.shape
    kw_n = 3 * SW_BLOCK

    fold_rows = 256

    def body(sink_ref, q_ref, k_ref, v_ref, do_ref, dq_ref, dk_ref, dv_ref, dsink_ref, dk_acc, dv_acc):
        @pl.when(pl.program_id(0) == 0)
        def _():
            dsink_ref[...] = jnp.zeros_like(dsink_ref)

        dk_acc[...] = jnp.zeros_like(dk_acc)
        dv_acc[...] = jnp.zeros_like(dv_acc)
        ppk = SW_PAIRS // SW_KV_HEADS

        def step(n, carry):
            start, ok = _sw_band(n, S)
            qrows = pl.ds(pl.multiple_of(n * SW_BLOCK, SW_BLOCK), SW_BLOCK)
            krows = pl.ds(start, kw_n)
            kh, vh = _kv_halves(k_ref[0, krows, :]), _kv_halves(v_ref[0, krows, :])
            heads = [(p, e) for p in range(SW_PAIRS) for e in range(2)]
            qp = [q_ref[0, qrows, pl.ds(p * 128, 128)] for p in range(SW_PAIRS)]
            dop = [do_ref[0, qrows, pl.ds(p * 128, 128)] for p in range(SW_PAIRS)]
            st = {(p, e): _nt(kh[(p // ppk, e)], qp[p]) for p, e in heads}
            dpt = {(p, e): _nt(vh[(p // ppk, e)], dop[p]) for p, e in heads}
            pnb, dsb = {}, {}
            for p, e in heads:
                pn, psink = _sw_probs(st[(p, e)], ok, sink_ref[2 * p + e])
                delta = jnp.sum(pn * dpt[(p, e)], axis=0, keepdims=True)
                dsb[(p, e)] = (pn * (dpt[(p, e)] - delta)).astype(BF16)
                pnb[(p, e)] = pn.astype(BF16)
                dsink_ref[2 * p + e:2 * p + e + 1, :] += -(psink * delta)
            dq_ref[0, qrows, :] = jnp.concatenate(
                [_tn(dsb[(p, 0)], kh[(p // ppk, 0)]) + _tn(dsb[(p, 1)], kh[(p // ppk, 1)]) for p in range(SW_PAIRS)], axis=1)
            left = lax.broadcasted_iota(jnp.int32, (kw_n, 128), 1) < HEAD_DIM
            dks, dvs = [], []
            for kv in range(SW_KV_HEADS):
                dk = dv = None
                for p in range(kv * ppk, (kv + 1) * ppk):
                    dk_p = jnp.where(left, _nn(dsb[(p, 0)], qp[p]), _nn(dsb[(p, 1)], qp[p]))
                    dv_p = jnp.where(left, _nn(pnb[(p, 0)], dop[p]), _nn(pnb[(p, 1)], dop[p]))
                    dk = dk_p if dk is None else dk + dk_p
                    dv = dv_p if dv is None else dv + dv_p
                dks.append(dk)
                dvs.append(dv)
            dk_acc[krows, :] += jnp.concatenate(dks, axis=1)
            dv_acc[krows, :] += jnp.concatenate(dvs, axis=1)
            return carry

        lax.fori_loop(0, S // SW_BLOCK, step, 0)

        def fold(i, carry):
            rows = pl.ds(pl.multiple_of(i * fold_rows, fold_rows), fold_rows)
            left = lax.broadcasted_iota(jnp.int32, (fold_rows, 128), 1) < HEAD_DIM
            for acc, out_ref in ((dk_acc, dk_ref), (dv_acc, dv_ref)):
                a, b = acc[rows, 0:128], acc[rows, 128:256]
                out_ref[0, rows, :] = jnp.where(left, a + pltpu.roll(a, HEAD_DIM, axis=1), b + pltpu.roll(b, HEAD_DIM, axis=1))
            return carry

        lax.fori_loop(0, S // fold_rows, fold, 0)

        @pl.when(pl.program_id(0) == B - 1)
        def _():
            dsink_ref[...] = jnp.broadcast_to(jnp.sum(dsink_ref[...], axis=1, keepdims=True), dsink_ref.shape)

    q, k, v = _sw_specs(S)
    qo = pl.BlockSpec((1, S, SW_WIDTH), lambda b: (b, 0, 0))
    ko = pl.BlockSpec((1, S, SW_KV_WIDTH), lambda b: (b, 0, 0))
    return pl.pallas_call(
        body, name="sw_bwd", grid=(B,),
        in_specs=[pl.BlockSpec(memory_space=pltpu.SMEM), q, k, v, qo],
        out_specs=[qo, ko, ko, _full((SW_HEADS, 128))],
        out_shape=[jax.ShapeDtypeStruct((B, S, SW_WIDTH), F32), jax.ShapeDtypeStruct((B, S, SW_KV_WIDTH), F32),
                   jax.ShapeDtypeStruct((B, S, SW_KV_WIDTH), F32), jax.ShapeDtypeStruct((SW_HEADS, 128), F32)],
        scratch_shapes=[pltpu.VMEM((S, 2 * SW_KV_WIDTH), F32), pltpu.VMEM((S, 2 * SW_KV_WIDTH), F32)],
        compiler_params=_params("arbitrary"),
    )(sink, qkv, qkv, qkv, dob)


def _pack_sum_adamw(packs, params):
    W = packs.shape[1]
    n_p = len(params)

    def body(p_ref, *refs):
        ins, tot_ref, outs = refs[:3 * n_p], refs[3 * n_p], refs[3 * n_p + 1:]
        tot = p_ref[0:8, :]
        for d in range(1, N_DEV):
            tot = tot + p_ref[8 * d:8 * d + 8, :]
        tot_ref[...] = tot
        for i, (w, _, _, rows, off) in enumerate(params):
            n = w.shape[1]
            g = tot[rows[0]:rows[0] + 1, off:off + n]
            for r in rows[1:]:
                g = g + tot[r:r + 1, off:off + n]
            w_ref, m_ref, v_ref = ins[3 * i:3 * i + 3]
            g_ref, d_ref, nm_ref, nv_ref = outs[4 * i:4 * i + 4]
            g_ref[...] = g
            d_ref[...], nm_ref[...], nv_ref[...] = _adam_update(w_ref[...], g, m_ref[...], v_ref[...])

    res = pl.pallas_call(
        body, name="small_adamw",
        out_shape=[jax.ShapeDtypeStruct((8, W), F32)] + [jax.ShapeDtypeStruct(p[0].shape, F32) for p in params for _ in range(4)],
        compiler_params=pltpu.CompilerParams(vmem_limit_bytes=VMEM_LIMIT),
    )(packs, *[a for p in params for a in p[:3]])
    return res[0], [res[1 + 4 * i:5 + 4 * i] for i in range(n_p)]


def _adam_update(w, g, m, v):
    c1 = 1.0 - ADAM_B1 ** ADAM_STEP
    c2 = 1.0 - ADAM_B2 ** ADAM_STEP
    nm = ADAM_B1 * m + (1.0 - ADAM_B1) * g
    nv = ADAM_B2 * v + (1.0 - ADAM_B2) * (g * g)
    return -ADAM_LR * ((nm / c1) / (jnp.sqrt(nv / c2) + ADAM_EPS) + ADAM_WD * w), nm, nv


def _adamw(w, g, m, v, name):
    def body(w_ref, g_ref, m_ref, v_ref, d_ref, nm_ref, nv_ref):
        d_ref[...], nm_ref[...], nv_ref[...] = _adam_update(w_ref[...], g_ref[...], m_ref[...], v_ref[...])

    s = jax.ShapeDtypeStruct(w.shape, F32)
    return pl.pallas_call(body, name=name, out_shape=[s, s, s],
                          compiler_params=pltpu.CompilerParams(vmem_limit_bytes=VMEM_LIMIT))(w, g, m, v)


def _sum_adamw_rows(R):
    return max(r for r in range(16, min(R, 256) + 1, 16) if R % r == 0)


def _sum_adamw_steps(R):
    return R // _sum_adamw_rows(R)


def _sum_adamw(own, recvb, w, m, v, name, tasks=()):
    R, C = own.shape
    rc = _sum_adamw_rows(R)

    def body(own_ref, r_ref, w_ref, m_ref, v_ref, g_ref, d_ref, nm_ref, nv_ref):
        g = own_ref[...]
        for j in range(3):
            g = g + r_ref[j].astype(F32)
        g_ref[...] = g
        d_ref[...], nm_ref[...], nv_ref[...] = _adam_update(w_ref[...], g, m_ref[...], v_ref[...])

    blk = pl.BlockSpec((rc, C), lambda i: (i, 0))
    s = jax.ShapeDtypeStruct((R, C), F32)
    return _hosted_call(body, name, (R // rc,), [blk, pl.BlockSpec((3, rc, C), lambda i: (0, i, 0)), blk, blk, blk],
                        [blk, blk, blk, blk], [s, s, s, s], (own, recvb, w, m, v), tasks)


def _by_device(dw):
    return dw.reshape(N_DEV, dw.shape[0] // N_DEV, dw.shape[1])


def _local_step(x, mod, g_attn, w_in, bias, sw_sink, g_na_out, g_sw_out, w_out, g_ffn, w_up, conv_w, conv_b, w_down,
                g_final, target, sharded):
    B, S, D = x.shape
    T = B * S
    x2d = x.reshape(T, D)
    mod3 = mod.reshape(B, 6, D)
    cos_t, sin_t = _rope_tables(S)
    sink = sw_sink.reshape(SW_HEADS)
    n_tiles = T // WIDE_TILE
    full = lambda g: g.reshape(N_DEV * g.shape[1], g.shape[2])

    rider = lambda w, mid, lo, n, into=None: [_gather_task(w, mid, rows=(lo, n), into=into)] if sharded else []
    if sharded:
        qu, hd = w_up.shape[0] // 4, w_down.shape[0] // 2
    (h, qkv), got = _attn_in(x2d, mod3, g_attn, w_in, cos_t, sin_t, S,
                             [_gather_task(w_out, n_tiles // 2)] + rider(w_up, n_tiles - 1, 0, qu) if sharded else [])
    if sharded:
        w_out, w_up_buf = full(got[0][0]), got[1][0]
    qkv3 = qkv.reshape(B, S, IN_WIDTH)
    na_steps, sw_steps = B * (NA_PAIRS // NA_STEP_PAIRS), B * SW_FWD_SPLIT
    (oa,), got = _na_fwd(qkv3, bias, rider(w_up, na_steps - 1, qu, 2 * qu, w_up_buf) if sharded else [])
    if sharded:
        w_up_buf = got[0][0]
    oa = oa.reshape(T, NA_WIDTH)
    (ob,), got = _sw_fwd(sink, qkv3, rider(w_up, sw_steps // 2, 3 * qu, qu, w_up_buf) if sharded else [])
    if sharded:
        w_up = full(got[0][0])
    ob = ob.reshape(T, SW_WIDTH)
    (mixin, mix, x1), got = _attn_out(oa, ob, x2d, mod3, g_na_out, g_sw_out, w_out, S,
                                      rider(w_down, n_tiles // 2 + 1, 0, hd) if sharded else [])
    (h2, val, gt), got = _ffn_up(x1, mod3, g_ffn, w_up, S, rider(w_down, n_tiles // 2, hd, hd, got[0][0]) if sharded else [])
    if sharded:
        w_down = full(got[0][0])
    a, act, vd, dx2, df, gstat_f, bstat_f = _ffn_down(gt, val, conv_w, conv_b, w_down, x1, mod3, g_final, target.reshape(T, D), B, S)
    F = val.shape[1]

    dw_down = _matmul_tn(a, df, "dw_down")
    (dval, dgc, cstat), got = _ffn_down_bwd(df, w_down, act, vd, [_swap_task(_by_device(dw_down))] if sharded else [])
    if sharded:
        send_down, own_down = _chip_sums(_by_device(dw_down), got[0][0])
    (du, dx1, dmix, gstat_u, bstat_u, cstat_w), got = _ffn_up_bwd(dgc, dval, gt, conv_w, w_up, x1, mod3, g_ffn, dx2, mix, B, S,
                                                                  [_exchange_task(send_down)] if sharded else [])
    if sharded:
        dw_down = (own_down, got[0][0])
    dw_up = _matmul_tn(du, h2, "dw_up", tm=F)
    dw_out = _matmul_tn(mixin, dmix, "dw_out")
    (doa, dob, gstat_o), got = _attn_out_bwd(dmix, w_out, oa, ob, g_na_out, g_sw_out,
                                             [_swap_task(_by_device(dw_up)), _swap_task(_by_device(dw_out))] if sharded else [])
    if sharded:
        send_up, own_up = _chip_sums(_by_device(dw_up), got[0][0])
        send_out, own_out = _chip_sums(_by_device(dw_out), got[1][0])
    (dqa, dka, dva, dbt), got = _na_bwd(qkv3, bias, doa.reshape(B, S, NA_WIDTH),
                                        [_exchange_task(send_up), _exchange_task(send_out)] if sharded else [])
    if sharded:
        dw_up, dw_out = (own_up, got[0][0]), (own_out, got[1][0])
    dqb, dkb, dvb, dsink = _sw_bwd(sink, qkv3, dob.reshape(B, S, SW_WIDTH))
    r2 = lambda t: t.reshape(T, t.shape[-1])
    grad_x, dproj, gstat_i, bstat_i = _attn_in_bwd(r2(dqa), r2(dka), r2(dva), r2(dqb), r2(dkb), r2(dvb), cos_t, sin_t, w_in, x2d, mod3,
                                                   g_attn, dx1, B, S)
    dw_in = _matmul_tn(dproj, h, "dw_in")

    dmod = jnp.stack([bstat_i[:, 0], bstat_i[:, 1], bstat_u[:, 2], bstat_u[:, 0], bstat_u[:, 1], bstat_f[:, 0]], axis=1)
    small = dict(g_attn=gstat_i[0], g_ffn=gstat_u[0], g_final=gstat_f[0], loss=gstat_f[1, 0], g_na_out=gstat_o[0], g_sw_out=gstat_o[1],
                 sw_sink=dsink[:, 0], conv_b=cstat[0], conv_w=cstat_w[1:4], dbt=dbt)
    return grad_x.reshape(B, S, D), dict(w_in=dw_in, w_out=dw_out, w_up=dw_up, w_down=dw_down), dmod, small


def _pad_lanes(v, w):
    return jnp.pad(v, (0, w - v.shape[0]))


def kernel(x, c, w_ada, b_ada, g_attn, w_in, na_rpb, sw_sink, g_na_out, g_sw_out, w_out, g_ffn, w_up, conv_w, conv_b, w_down, g_final, loss_target, m_w_ada, m_b_ada, m_g_attn, m_w_in, m_na_rpb, m_sw_sink, m_g_na_out, m_g_sw_out, m_w_out, m_g_ffn, m_w_up, m_conv_w, m_conv_b, m_w_down, m_g_final, v_w_ada, v_b_ada, v_g_attn, v_w_in, v_na_rpb, v_sw_sink, v_g_na_out, v_g_sw_out, v_w_out, v_g_ffn, v_w_up, v_conv_w, v_conv_b, v_w_down, v_g_final):
    B, S, D = x.shape
    me = 4 * lax.axis_index("x") + 2 * lax.axis_index("y") + lax.axis_index("c")
    ada_c = w_ada.shape[2]
    F_l = conv_w.shape[2]

    cw_l = jnp.pad(conv_w[0], ((0, 8 - conv_w.shape[1]), (0, 0)))
    c_l = jnp.pad(c, ((0, 8 - B), (0, 0)))
    tr = {"w_in", "w_up"}
    w_in_t = jnp.transpose(w_in[0])
    shards = dict(w_out=w_out[0].astype(BF16), w_up=jnp.transpose(w_up[0]).astype(BF16), w_down=w_down[0].astype(BF16))

    b_ada_l = lax.dynamic_slice(b_ada, (0, me * ada_c), (1, ada_c))
    slabs, mod_all, w_in_all = _ada_fwd(jnp.concatenate([c_l, cw_l], axis=1), w_ada[0], b_ada_l, w_in_t.astype(BF16), B)
    c_all = slabs[:, :, :D].reshape(N_DEV * 8, D)
    conv_w_f = jnp.transpose(slabs[:, :3, D:], (1, 0, 2)).reshape(3, N_DEV * F_l)
    mod_mine = lax.dynamic_slice(mod_all, (0, me * B, 0), (N_DEV, B, ada_c))
    mod = jnp.transpose(mod_mine, (1, 0, 2)).reshape(B, N_DEV * ada_c)
    w_in_f = w_in_all.reshape(N_DEV * w_in_t.shape[0], D)

    bias = _na_bias_table(na_rpb[0])

    grad_x, dw, dmod, small = _local_step(x, mod, g_attn, w_in_f, bias, sw_sink, g_na_out, g_sw_out, shards["w_out"], g_ffn,
                                          shards["w_up"], conv_w_f, conv_b, shards["w_down"], g_final.reshape(1, D), loss_target,
                                          sharded=True)
    drpb = _na_bias_grad(small["dbt"])

    row2 = jnp.concatenate([small["g_attn"], small["g_ffn"], small["g_final"], small["g_na_out"], small["g_sw_out"],
                            _pad_lanes(small["sw_sink"], 128), _pad_lanes(small["loss"].reshape(1), 128)])
    rows = [dmod.reshape(B, 6 * D)[0], dmod.reshape(B, 6 * D)[1], _pad_lanes(row2, PACK_W), _pad_lanes(small["conv_b"], PACK_W),
            _pad_lanes(drpb, PACK_W)] + [_pad_lanes(small["conv_w"][k], PACK_W) for k in range(3)]
    weights = dict(w_ada=w_ada, b_ada=b_ada, g_attn=g_attn, w_in=w_in, na_rpb=na_rpb, sw_sink=sw_sink, g_na_out=g_na_out,
                   g_sw_out=g_sw_out, w_out=w_out, g_ffn=g_ffn, w_up=w_up, conv_w=conv_w, conv_b=conv_b, w_down=w_down, g_final=g_final)
    ms = dict(w_ada=m_w_ada, b_ada=m_b_ada, g_attn=m_g_attn, w_in=m_w_in, na_rpb=m_na_rpb, sw_sink=m_sw_sink, g_na_out=m_g_na_out,
              g_sw_out=m_g_sw_out, w_out=m_w_out, g_ffn=m_g_ffn, w_up=m_w_up, conv_w=m_conv_w, conv_b=m_conv_b, w_down=m_w_down, g_final=m_g_final)
    vs = dict(w_ada=v_w_ada, b_ada=v_b_ada, g_attn=v_g_attn, w_in=v_w_in, na_rpb=v_na_rpb, sw_sink=v_sw_sink, g_na_out=v_g_na_out,
              g_sw_out=v_g_sw_out, w_out=v_w_out, g_ffn=v_g_ffn, w_up=v_w_up, conv_w=v_conv_w, conv_b=v_conv_b, w_down=v_w_down, g_final=v_g_final)
    names = list(weights)
    grads, deltas, new_m, new_v = {}, {}, {}, {}
    flat = lambda t: t.reshape(1, -1)

    def shard2d(nm):
        if nm in tr:
            return (lambda t: jnp.transpose(t[0])), (lambda t: jnp.transpose(t)[None])
        return (lambda t: t[0]), (lambda t: t[None])

    def finish_sum(nm, own, recvb, tasks=()):
        r, back = shard2d(nm)
        (g2, d_, m_, v_), got = _sum_adamw(own, recvb, r(weights[nm]), r(ms[nm]), r(vs[nm]), "adamw_" + nm, tasks)
        grads[nm], deltas[nm], new_m[nm], new_v[nm] = back(g2), back(d_), back(m_), back(v_)
        return got

    g8_in = _by_device(dw["w_in"])
    got = finish_sum("w_down", *dw["w_down"], [_swap_task(g8_in)])
    send_in, own_in = _chip_sums(g8_in, got[0][0])
    n_up = _sum_adamw_steps(dw["w_up"][0].shape[0])
    got = finish_sum("w_up", *dw["w_up"], [_exchange_task(send_in), _gather_task(jnp.stack(rows), n_up - 1)])
    packs = got[1][0]
    finish_sum("w_in", own_in, got[0][0])
    finish_sum("w_out", *dw["w_out"])

    where = dict(b_ada=((0, 1), 0), g_attn=((2,), 0), g_ffn=((2,), D), g_final=((2,), 2 * D), g_na_out=((2,), 3 * D),
                 g_sw_out=((2,), 3 * D + NA_WIDTH), sw_sink=((2,), 3 * D + NA_WIDTH + SW_WIDTH), conv_b=((3,), 0), na_rpb=((4,), 0))
    tot, small_out = _pack_sum_adamw(packs.reshape(N_DEV * 8, PACK_W),
                                     [(flat(weights[n]), flat(ms[n]), flat(vs[n])) + where[n] for n in where])
    for n, (g_, d_, m_, v_) in zip(where, small_out):
        shp = weights[n].shape
        grads[n], deltas[n], new_m[n], new_v[n] = g_.reshape(shp), d_.reshape(shp), m_.reshape(shp), v_.reshape(shp)
    loss = tot[2, 3 * D + NA_WIDTH + SW_WIDTH + 128]

    dmod_cols = lax.dynamic_slice(packs.reshape(N_DEV * 8, PACK_W), (0, me * ada_c), (N_DEV * 8, ada_c))
    for nm, g2 in (("w_ada", _ada_bwd(c_all, dmod_cols)), ("conv_w", lax.dynamic_slice(tot[5:8], (0, me * F_l), (3, F_l)))):
        r, back = shard2d(nm)
        d_, m_, v_ = _adamw(r(weights[nm]), g2, r(ms[nm]), r(vs[nm]), "adamw_" + nm)
        grads[nm], deltas[nm], new_m[nm], new_v[nm] = back(g2), back(d_), back(m_), back(v_)
    return (loss, grad_x, *[grads[n] for n in names], *[deltas[n] for n in names], *[new_m[n] for n in names],
            *[new_v[n] for n in names])
```

```python
import functools

import numpy as np
import jax
import jax.numpy as jnp
from jax import lax
from jax.experimental import pallas as pl
from jax.experimental.pallas import tpu as pltpu

F32, BF16 = jnp.float32, jnp.bfloat16
MESH_ID = pl.DeviceIdType.MESH
N_DEV = 8

HEAD_DIM = 64
NA_HEADS = 8
SW_HEADS = 8
SW_KV_HEADS = 2
SW_GROUP = SW_HEADS // SW_KV_HEADS
NA_WIDTH = NA_HEADS * HEAD_DIM
SW_WIDTH = SW_HEADS * HEAD_DIM
SW_KV_WIDTH = SW_KV_HEADS * HEAD_DIM
ROPE_WIDTH = SW_WIDTH + SW_KV_WIDTH
IN_WIDTH = 3 * NA_WIDTH + SW_WIDTH + 2 * SW_KV_WIDTH
ROPE_LO = 3 * NA_WIDTH
GRID_W = 64
NA_ROWS_MAX = 8
NA_COLS = 16
N_DR = 2 * NA_ROWS_MAX - 1
N_DC = 2 * NA_COLS - 1
SW_WINDOW = 128
SW_BLOCK = 128
ROPE_THETA = 10000.0
EPS = 1e-6
NEG = -1e30
Q_SCALE = HEAD_DIM ** -0.5

ADAM_LR = 0.001
ADAM_B1 = 0.9
ADAM_B2 = 0.999
ADAM_EPS = 1e-08
ADAM_WD = 0.01
ADAM_STEP = 10

TOKEN_TILE = 256
WIDE_TILE = 512
VMEM_LIMIT = 56 * 1024 * 1024

PACK_W = 6144


def _nn(a, b):
    return jnp.dot(a, b, preferred_element_type=F32)


def _nt(a, b):
    return lax.dot_general(a, b, (((1,), (1,)), ((), ())), preferred_element_type=F32)


def _tn(a, b):
    return lax.dot_general(a, b, (((0,), (0,)), ((), ())), preferred_element_type=F32)


def _rms(x):
    r = lax.rsqrt(jnp.mean(x * x, axis=-1, keepdims=True) + EPS)
    return x * r, r


def _rms_bwd(xn, r, gy):
    return r * (gy - xn * jnp.mean(xn * gy, axis=-1, keepdims=True))


def _params(*sem):
    return pltpu.CompilerParams(dimension_semantics=sem, vmem_limit_bytes=VMEM_LIMIT)


def _full(shape):
    n = len(shape)
    return pl.BlockSpec(shape, lambda *_: (0,) * n)


def _mesh_pos():
    return lax.axis_index("x"), lax.axis_index("y"), lax.axis_index("c")


def _row_chunk(r):
    for rc in (128, 64, 32, 16):
        if r % rc == 0:
            return rc
    raise ValueError(f"rows {r} not a multiple of 16")


class _Task:
    def __init__(self, inputs, out_shapes, sems, start, finish, mid=None, mid_step=None, alias=None):
        self.inputs, self.out_shapes, self.sems = list(inputs), list(out_shapes), list(sems)
        self.start, self.finish, self.mid, self.mid_step = start, finish, mid, mid_step
        self.alias = alias


def _hosted_call(body, name, grid, in_specs, out_specs, out_shape, operands, tasks, scratch_shapes=()):
    n_in, n_out, n_scr = len(in_specs), len(out_specs), len(scratch_shapes)
    t_in = [len(t.inputs) for t in tasks]
    t_out = [len(t.out_shapes) for t in tasks]
    t_sem = [len(t.sems) for t in tasks]
    n_steps = int(np.prod(grid))

    def wrapped(*refs):
        ins, rest = refs[:n_in], refs[n_in:]
        task_ins, rest = rest[:sum(t_in)], rest[sum(t_in):]
        outs, rest = rest[:n_out], rest[n_out:]
        task_outs, rest = rest[:sum(t_out)], rest[sum(t_out):]
        scr, task_sems = rest[:n_scr], rest[n_scr:]
        step = pl.program_id(0)
        for ax in range(1, len(grid)):
            step = step * grid[ax] + pl.program_id(ax)
        parts = []
        oi = oo = os_ = 0
        for t, a, b, c in zip(tasks, t_in, t_out, t_sem):
            parts.append((t, task_ins[oi:oi + a], task_outs[oo:oo + b], task_sems[os_:os_ + c]))
            oi, oo, os_ = oi + a, oo + b, os_ + c
        for t, ti, to, ts in parts:
            pl.when(step == 0)(functools.partial(t.start, ti, to, ts))
            if t.mid is not None:
                pl.when(step == t.mid_step)(functools.partial(t.mid, ti, to, ts))
        body(*ins, *outs, *scr)
        for t, ti, to, ts in parts:
            pl.when(step == n_steps - 1)(functools.partial(t.finish, ti, to, ts))

    hbm = pl.BlockSpec(memory_space=pl.ANY)
    aliases, oi, oo = {}, n_in, n_out
    for t, a, b in zip(tasks, t_in, t_out):
        if t.alias is not None:
            aliases[oi + t.alias[0]] = oo + t.alias[1]
        oi, oo = oi + a, oo + b
    res = pl.pallas_call(
        wrapped, name=name, grid=grid,
        in_specs=list(in_specs) + [hbm] * sum(t_in),
        out_specs=list(out_specs) + [hbm] * sum(t_out),
        out_shape=list(out_shape) + [s for t in tasks for s in t.out_shapes],
        scratch_shapes=list(scratch_shapes) + [s for t in tasks for s in t.sems],
        input_output_aliases=aliases,
        compiler_params=_params(*(["arbitrary"] * len(grid))),
    )(*operands, *[a for t in tasks for a in t.inputs])
    own, extra = res[:n_out], res[n_out:]
    per_task, o = [], 0
    for b in t_out:
        per_task.append(extra[o:o + b])
        o += b
    return own, per_task


def _gather_task(shard, mid_step, rows=None, into=None):
    lo, n = (0, shard.shape[0]) if rows is None else rows

    def parts(ins, outs, sems):
        x_ref, out_ref, (send_sems, recv_sems, local_sem) = ins[0], outs[0], sems
        x_, y_, c_ = _mesh_pos()
        me, sibling = (x_, y_, c_), (x_, y_, 1 - c_)
        chips = [(1 - x_, y_), (x_, 1 - y_), (1 - x_, 1 - y_)]
        x_ref = x_ref.at[pl.ds(lo, n)]

        def rows(px, py, pc):
            return out_ref.at[4 * px + 2 * py + pc, pl.ds(lo, n)]

        def copy(k, block, to, src=None):
            return pltpu.make_async_remote_copy(
                src_ref=rows(*block) if src is None else src, dst_ref=rows(*block),
                send_sem=send_sems.at[k], recv_sem=recv_sems.at[k], device_id=to, device_id_type=MESH_ID)

        return dict(
            mine=lambda: pltpu.make_async_copy(x_ref, rows(*me), local_sem),
            first=lambda: [copy(0, me, sibling, src=x_ref)] + [copy(1 + j, me, (*chip, c_), src=x_ref) for j, chip in enumerate(chips)],
            passed=lambda: [copy(4 + j, (*chip, c_), sibling) for j, chip in enumerate(chips)],
            landed=lambda: [copy(1 + j, (*chip, c_), me) for j, chip in enumerate(chips)],
            last=lambda: [copy(0, sibling, me)] + [copy(4 + j, (*chip, 1 - c_), me) for j, chip in enumerate(chips)])

    def start(ins, outs, sems):
        p = parts(ins, outs, sems)
        p["mine"]().start()
        for cp in p["first"]():
            cp.start()

    def mid(ins, outs, sems):
        p = parts(ins, outs, sems)
        for cp, fw in zip(p["landed"](), p["passed"]()):
            cp.wait_recv()
            fw.start()

    def finish(ins, outs, sems):
        p = parts(ins, outs, sems)
        for cp in p["last"]():
            cp.wait_recv()
        for cp in p["first"]() + p["passed"]():
            cp.wait_send()
        p["mine"]().wait()

    return _Task([shard] if into is None else [shard, into], [jax.ShapeDtypeStruct((N_DEV,) + shard.shape, shard.dtype)],
                 [pltpu.SemaphoreType.DMA((7,)), pltpu.SemaphoreType.DMA((7,)), pltpu.SemaphoreType.DMA],
                 start, finish, mid, mid_step, alias=None if into is None else (1, 0))


def _swap_task(g8):
    _, R, C = g8.shape

    def copies(ins, outs, sems):
        (g_ref,), (recv_ref,), (ss, rs) = ins, outs, sems
        x_, y_, c_ = _mesh_pos()
        return [pltpu.make_async_remote_copy(src_ref=g_ref.at[2 * k + (1 - c_)], dst_ref=recv_ref.at[k], send_sem=ss.at[k],
                                             recv_sem=rs.at[k], device_id=(x_, y_, 1 - c_), device_id_type=MESH_ID)
                for k in range(4)]

    def start(ins, outs, sems):
        for cp in copies(ins, outs, sems):
            cp.start()

    def finish(ins, outs, sems):
        cps = copies(ins, outs, sems)
        for cp in cps:
            cp.wait_recv()
        for cp in cps:
            cp.wait_send()

    return _Task([g8], [jax.ShapeDtypeStruct((4, R, C), g8.dtype)],
                 [pltpu.SemaphoreType.DMA((4,)), pltpu.SemaphoreType.DMA((4,))], start, finish)


def _chip_sums(g8, recva):
    _, R, C = g8.shape
    rc = _row_chunk(R)

    def body(g_ref, a_ref, send_ref, own_ref):
        x_, y_, c_ = _mesh_pos()
        chips = [(1 - x_, y_), (x_, 1 - y_), (1 - x_, 1 - y_), (x_, y_)]

        def chunk(i, carry):
            rows = pl.ds(pl.multiple_of(i * rc, rc), rc)
            for j, (tx, ty) in enumerate(chips):
                k = 2 * tx + ty
                s = g_ref[2 * k + c_, rows, :].astype(F32) + a_ref[k, rows, :].astype(F32)
                if j < 3:
                    send_ref[j, rows, :] = s.astype(BF16)
                else:
                    own_ref[rows, :] = s
            return carry

        lax.fori_loop(0, R // rc, chunk, 0)

    return pl.pallas_call(body, name="chip_sums", out_shape=[jax.ShapeDtypeStruct((3, R, C), BF16), jax.ShapeDtypeStruct((R, C), F32)],
                          compiler_params=pltpu.CompilerParams(vmem_limit_bytes=VMEM_LIMIT))(g8, recva)


def _exchange_task(sendb):
    def copies(ins, outs, sems):
        (s_ref,), (recv_ref,), (ss, rs) = ins, outs, sems
        x_, y_, c_ = _mesh_pos()
        flips = [(1 - x_, y_), (x_, 1 - y_), (1 - x_, 1 - y_)]
        return [pltpu.make_async_remote_copy(src_ref=s_ref.at[j], dst_ref=recv_ref.at[j], send_sem=ss.at[j], recv_sem=rs.at[j],
                                             device_id=(tx, ty, c_), device_id_type=MESH_ID) for j, (tx, ty) in enumerate(flips)]

    def start(ins, outs, sems):
        for cp in copies(ins, outs, sems):
            cp.start()

    def finish(ins, outs, sems):
        cps = copies(ins, outs, sems)
        for cp in cps:
            cp.wait_recv()
        for cp in cps:
            cp.wait_send()

    return _Task([sendb], [jax.ShapeDtypeStruct(sendb.shape, sendb.dtype)],
                 [pltpu.SemaphoreType.DMA((3,)), pltpu.SemaphoreType.DMA((3,))], start, finish)


def _silu(v):
    return v * (1.0 / (1.0 + jnp.exp(-v)))


def _ada_fwd(c_slab, w_ada_l, b_ada_l, w_in_shard, n_seq):
    W = c_slab.shape[1]
    D, cols = w_ada_l.shape
    n_rows = N_DEV * n_seq
    t_c, t_w = _gather_task(c_slab, 0), _gather_task(w_in_shard, 0)
    t_m = _gather_task(jax.ShapeDtypeStruct((n_rows, cols), F32), 0)

    def body(c_ref, w_ref, b_ref, ws_ref, slabs_ref, mod_ref, win_ref, c_vm, m_vm, copy_sem, *sems):
        sc, sw, sm = sems[0:3], sems[3:6], sems[6:9]
        t_c.start((c_ref,), (slabs_ref,), sc)
        t_w.start((ws_ref,), (win_ref,), sw)
        t_c.mid((c_ref,), (slabs_ref,), sc)
        t_c.finish((c_ref,), (slabs_ref,), sc)
        cp = pltpu.make_async_copy(slabs_ref, c_vm, copy_sem)
        cp.start()
        cp.wait()
        c_all = c_vm[:, :, 0:D].reshape(N_DEV * 8, D)
        m64 = jnp.dot(_silu(c_all), w_ref[...], precision=lax.Precision.HIGHEST, preferred_element_type=F32) + b_ref[...]
        r = lax.broadcasted_iota(jnp.int32, (n_rows, N_DEV * 8), 0)
        c = lax.broadcasted_iota(jnp.int32, (n_rows, N_DEV * 8), 1)
        pick = jnp.where(c == 8 * (r // n_seq) + r % n_seq, 1.0, 0.0)
        m_vm[...] = jnp.dot(pick, m64, precision=lax.Precision.HIGHEST, preferred_element_type=F32)
        t_m.start((m_vm,), (mod_ref,), sm)
        t_w.mid((ws_ref,), (win_ref,), sw)
        t_m.mid((m_vm,), (mod_ref,), sm)
        t_m.finish((m_vm,), (mod_ref,), sm)
        t_w.finish((ws_ref,), (win_ref,), sw)

    hbm, vm = pl.BlockSpec(memory_space=pl.ANY), pl.BlockSpec(memory_space=pltpu.VMEM)
    return pl.pallas_call(
        body, name="ada_fwd", in_specs=[hbm, vm, vm, hbm], out_specs=[hbm, hbm, hbm],
        out_shape=t_c.out_shapes + t_m.out_shapes + t_w.out_shapes,
        scratch_shapes=[pltpu.VMEM((N_DEV, 8, W), F32), pltpu.VMEM((n_rows, cols), F32), pltpu.SemaphoreType.DMA]
        + t_c.sems + t_w.sems + t_m.sems,
        compiler_params=pltpu.CompilerParams(vmem_limit_bytes=VMEM_LIMIT),
    )(c_slab, w_ada_l, b_ada_l, w_in_shard)


def _ada_bwd(c_all, dmod_cols):
    def body(c_ref, d_ref, o_ref):
        o_ref[...] = lax.dot_general(_silu(c_ref[...]), d_ref[...], (((0,), (0,)), ((), ())),
                                     precision=lax.Precision.HIGHEST, preferred_element_type=F32)
    return pl.pallas_call(body, name="ada_bwd", out_shape=jax.ShapeDtypeStruct((c_all.shape[1], dmod_cols.shape[1]), F32),
                          compiler_params=pltpu.CompilerParams(vmem_limit_bytes=VMEM_LIMIT))(c_all, dmod_cols)


NA_PAIRS = NA_HEADS // 2
N_DR_PAD = 16


def _na_bias_table(na_rpb):
    rev = jnp.pad(jnp.flip(na_rpb, axis=2), ((0, 0), (0, N_DR_PAD - N_DR), (0, GRID_W - N_DC)))
    rev = jnp.transpose(rev.reshape(NA_PAIRS, 2, N_DR_PAD, GRID_W), (0, 2, 1, 3)).reshape(NA_PAIRS, N_DR_PAD, 128)

    def body(r_ref, o_ref):
        k = lax.broadcasted_iota(jnp.int32, (GRID_W, 128), 0)
        lane = lax.broadcasted_iota(jnp.int32, (GRID_W, 128), 1)
        q = lane % GRID_W
        cs = jnp.clip(q - NA_COLS // 2, 0, GRID_W - NA_COLS)
        ok = (k >= cs) & (k < cs + NA_COLS)
        left = lane < GRID_W
        for dr in range(N_DR):
            row = jnp.broadcast_to(r_ref[0, dr:dr + 1, :], (GRID_W, 128))
            r0 = jnp.where(left, row, 0.0)
            r1 = jnp.where(left, pltpu.roll(row, GRID_W, axis=1), 0.0)
            y0 = pltpu.roll(r0, 128 - (NA_COLS - 1), axis=1, stride=1, stride_axis=0)
            y1 = pltpu.roll(r1, GRID_W - (NA_COLS - 1), axis=1, stride=1, stride_axis=0)
            o_ref[0, dr * GRID_W:(dr + 1) * GRID_W, :] = jnp.where(ok, jnp.where(left, y0, y1), NEG)

    return pl.pallas_call(
        body, name="rpb_expand", grid=(NA_PAIRS,),
        in_specs=[pl.BlockSpec((1, N_DR_PAD, 128), lambda p: (p, 0, 0))],
        out_specs=pl.BlockSpec((1, N_DR * GRID_W, 128), lambda p: (p, 0, 0)),
        out_shape=jax.ShapeDtypeStruct((NA_PAIRS, N_DR * GRID_W, 128), F32),
        compiler_params=_params("parallel"),
    )(rev)


def _na_bias_grad(db):
    a = np.arange(128)
    flip = jnp.asarray(((a[:, None] // GRID_W == a[None, :] // GRID_W)
                        & (a[:, None] % GRID_W + a[None, :] % GRID_W == GRID_W - 1)).astype(np.float32))

    def body(d_ref, j_ref, o_ref):
        o_ref[...] = jnp.zeros_like(o_ref)
        for dr in range(N_DR):
            t = jnp.dot(d_ref[0, dr * GRID_W:(dr + 1) * GRID_W, :], j_ref[...], precision=lax.Precision.HIGHEST, preferred_element_type=F32)
            t = pltpu.roll(t, GRID_W + NA_COLS, axis=1, stride=1, stride_axis=0)
            o_ref[0, dr:dr + 1, :] = jnp.sum(t, axis=0, keepdims=True)

    rows = pl.pallas_call(
        body, name="rpb_reduce", grid=(NA_PAIRS,),
        in_specs=[pl.BlockSpec((1, N_DR * GRID_W, 128), lambda p: (p, 0, 0)), _full((128, 128))],
        out_specs=pl.BlockSpec((1, N_DR_PAD, 128), lambda p: (p, 0, 0)),
        out_shape=jax.ShapeDtypeStruct((NA_PAIRS, N_DR_PAD, 128), F32),
        compiler_params=_params("parallel"),
    )(db, flip)
    g = rows.reshape(NA_PAIRS, N_DR_PAD, 2, GRID_W)[:, :N_DR, :, :N_DC]
    return jnp.transpose(g, (0, 2, 1, 3)).reshape(-1)


def _rope_tables(S):
    half = HEAD_DIM // 2
    inv = np.float32(ROPE_THETA) ** (-np.arange(half, dtype=np.float32) / np.float32(half))
    ang = np.arange(S).astype(np.float32)[:, None] * inv[None, :]
    cos, sin = np.cos(ang).astype(np.float32), np.sin(ang).astype(np.float32)
    return jnp.asarray(np.tile(np.concatenate([cos, cos], axis=1), (1, 2))), jnp.asarray(np.tile(np.concatenate([-sin, sin], axis=1), (1, 2)))


def _rope_spec(tps, tm=TOKEN_TILE):
    return pl.BlockSpec((tm, 2 * HEAD_DIM), lambda i: (i % tps, 0))


def _rot_half(t):
    w = t.shape[1]
    lane = lax.broadcasted_iota(jnp.int32, t.shape, 1)
    return jnp.where((lane % HEAD_DIM) < HEAD_DIM // 2, pltpu.roll(t, w - HEAD_DIM // 2, axis=1),
                     pltpu.roll(t, HEAD_DIM // 2, axis=1))


def _tok_spec(w, tm=TOKEN_TILE):
    return pl.BlockSpec((tm, w), lambda i: (i, 0))


def _mod_spec(tps, d):
    return pl.BlockSpec((1, 6, d), lambda i: (i // tps, 0, 0))


def _bstat_spec(tps, w):
    return pl.BlockSpec((1, 8, w), lambda i: (i // tps, 0, 0))


def _attn_in(x2d, mod3, g_attn, w_in, cos_t, sin_t, S, tasks=(), tm=WIDE_TILE):
    T, D = x2d.shape
    tps = S // tm

    def body(x_ref, mod_ref, g_ref, w_ref, cos_ref, sin_ref, h_ref, qkv_ref):
        xn, _ = _rms(x_ref[...])
        h = (xn * g_ref[...]) * (1.0 + mod_ref[0, 1:2, :]) + mod_ref[0, 0:1, :]
        hb = h.astype(BF16)
        h_ref[...] = hb
        proj = _nt(hb, w_ref[...])
        rb = proj[:, ROPE_LO:ROPE_LO + ROPE_WIDTH]
        reps = (1, ROPE_WIDTH // (2 * HEAD_DIM))
        rb = rb * jnp.tile(cos_ref[...], reps) + _rot_half(rb) * jnp.tile(sin_ref[...], reps)
        qkv_ref[:, 0:NA_WIDTH] = (proj[:, 0:NA_WIDTH] * Q_SCALE).astype(BF16)
        qkv_ref[:, NA_WIDTH:ROPE_LO] = proj[:, NA_WIDTH:ROPE_LO].astype(BF16)
        qkv_ref[:, ROPE_LO:ROPE_LO + SW_WIDTH] = (rb[:, 0:SW_WIDTH] * Q_SCALE).astype(BF16)
        qkv_ref[:, ROPE_LO + SW_WIDTH:ROPE_LO + ROPE_WIDTH] = rb[:, SW_WIDTH:].astype(BF16)
        qkv_ref[:, ROPE_LO + ROPE_WIDTH:] = proj[:, ROPE_LO + ROPE_WIDTH:].astype(BF16)

    return _hosted_call(
        body, "attn_in", (T // tm,),
        [_tok_spec(D, tm), _mod_spec(tps, D), _full((1, D)), _full(w_in.shape), _rope_spec(tps, tm), _rope_spec(tps, tm)],
        [_tok_spec(D, tm), _tok_spec(IN_WIDTH, tm)],
        [jax.ShapeDtypeStruct((T, D), BF16), jax.ShapeDtypeStruct((T, IN_WIDTH), BF16)],
        (x2d, mod3, g_attn, w_in, cos_t, sin_t), tasks)


def _attn_out(oa, ob, x2d, mod3, g_na, g_sw, w_out, S, tasks=(), tm=WIDE_TILE):
    T, D = x2d.shape
    tps = S // tm

    def body(oa_ref, ob_ref, x_ref, mod_ref, gna_ref, gsw_ref, w_ref, mixin_ref, mix_ref, x1_ref):
        oan, _ = _rms(oa_ref[...])
        obn, _ = _rms(ob_ref[...])
        mixin = jnp.concatenate([oan * gna_ref[...], obn * gsw_ref[...]], axis=1).astype(BF16)
        mixin_ref[...] = mixin
        mix = _nn(mixin, w_ref[...])
        mix_ref[...] = mix
        x1_ref[...] = x_ref[...] + mod_ref[0, 2:3, :] * mix

    return _hosted_call(
        body, "attn_out", (T // tm,),
        [_tok_spec(NA_WIDTH, tm), _tok_spec(SW_WIDTH, tm), _tok_spec(D, tm), _mod_spec(tps, D),
         _full((1, NA_WIDTH)), _full((1, SW_WIDTH)), _full(w_out.shape)],
        [_tok_spec(NA_WIDTH + SW_WIDTH, tm), _tok_spec(D, tm), _tok_spec(D, tm)],
        [jax.ShapeDtypeStruct((T, NA_WIDTH + SW_WIDTH), BF16), jax.ShapeDtypeStruct((T, D), F32), jax.ShapeDtypeStruct((T, D), F32)],
        (oa, ob, x2d, mod3, g_na, g_sw, w_out), tasks)


def _ffn_up(x1, mod3, g_ffn, w_up, S, tasks=(), tm=WIDE_TILE):
    T, D = x1.shape
    F = w_up.shape[0] // 2
    tps = S // tm

    def body(x1_ref, mod_ref, g_ref, w_ref, h2_ref, val_ref, gt_ref):
        xn, _ = _rms(x1_ref[...])
        h2 = ((xn * g_ref[...]) * (1.0 + mod_ref[0, 4:5, :]) + mod_ref[0, 3:4, :]).astype(BF16)
        h2_ref[...] = h2
        u = _nt(h2, w_ref[...])
        val_ref[...] = u[:, :F].astype(BF16)
        gt_ref[...] = u[:, F:].astype(BF16)

    return _hosted_call(
        body, "ffn_up", (T // tm,), [_tok_spec(D, tm), _mod_spec(tps, D), _full((1, D)), _full(w_up.shape)],
        [_tok_spec(D, tm), _tok_spec(F, tm), _tok_spec(F, tm)],
        [jax.ShapeDtypeStruct((T, D), BF16), jax.ShapeDtypeStruct((T, F), BF16), jax.ShapeDtypeStruct((T, F), BF16)],
        (x1, mod3, g_ffn, w_up), tasks)


def _halo_specs(T, tps, w):
    per = TOKEN_TILE // 8
    prev = pl.BlockSpec((8, w), lambda i: (jnp.maximum(i * per - 1, 0), 0))
    nxt = pl.BlockSpec((8, w), lambda i: (jnp.minimum((i + 1) * per, T // 8 - 1), 0))
    return prev, nxt


def _seq_shifts(cur, before, after, ti, tps):
    tm = cur.shape[0]
    row = lax.broadcasted_iota(jnp.int32, cur.shape, 0)
    before = jnp.where(ti > 0, before.astype(F32), 0.0)
    after = jnp.where(ti < tps - 1, after.astype(F32), 0.0)
    return jnp.where(row == 0, before, pltpu.roll(cur, 1, axis=0)), jnp.where(row == tm - 1, after, pltpu.roll(cur, tm - 1, axis=0))


def _ffn_down(gt, val, conv_w, conv_b, w_down, x1, mod3, g_final, target, B, S):
    T, D = x1.shape
    F = gt.shape[1]
    tps = S // TOKEN_TILE
    prev, nxt = _halo_specs(T, tps, F)

    def body(gt_ref, prev_ref, next_ref, val_ref, cw_ref, cb_ref, w_ref, x1_ref, mod_ref, gf_ref, tgt_ref,
             a_ref, act_ref, vd_ref, dx2_ref, df_ref, gstat_ref, bstat_ref):
        i = pl.program_id(0)
        g = gt_ref[...].astype(F32)
        gprev, gnext = _seq_shifts(g, prev_ref[7:8, :], next_ref[0:1, :], i % tps, tps)
        gc = gprev * cw_ref[0:1, :] + g * cw_ref[1:2, :] + gnext * cw_ref[2:3, :] + cb_ref[...]
        sig = 1.0 / (1.0 + jnp.exp(-gc))
        act = gc * sig
        val = val_ref[...].astype(F32)
        act_ref[...] = act.astype(BF16)
        vd_ref[...] = (val * (sig + act - act * sig)).astype(BF16)
        a = (act * val).astype(BF16)
        a_ref[...] = a
        f = _nn(a, w_ref[...])
        gate = mod_ref[0, 5:6, :]
        x2 = x1_ref[...] + gate * f
        xn, r = _rms(x2)
        err = xn * gf_ref[...] - tgt_ref[...]
        dy = err * (1.0 / D)
        dx2 = _rms_bwd(xn, r, dy * gf_ref[...])
        dx2_ref[...] = dx2
        df_ref[...] = (gate * dx2).astype(BF16)

        @pl.when(i == 0)
        def _():
            gstat_ref[...] = jnp.zeros_like(gstat_ref)

        @pl.when(i % tps == 0)
        def _():
            bstat_ref[...] = jnp.zeros_like(bstat_ref)

        gstat_ref[0:1, :] += jnp.sum(dy * xn, axis=0, keepdims=True)
        tile_loss = jnp.sum(jnp.sum(err * err, axis=1, keepdims=True), axis=0, keepdims=True) * (0.5 / D)
        gstat_ref[1:2, :] += jnp.broadcast_to(tile_loss, (1, D))
        bstat_ref[0, 0:1, :] += jnp.sum(dx2 * f, axis=0, keepdims=True)

    return pl.pallas_call(
        body, name="ffn_down", grid=(T // TOKEN_TILE,),
        in_specs=[_tok_spec(F), prev, nxt, _tok_spec(F), _full(conv_w.shape), _full((1, F)), _full(w_down.shape),
                  _tok_spec(D), _mod_spec(tps, D), _full((1, D)), _tok_spec(D)],
        out_specs=[_tok_spec(F), _tok_spec(F), _tok_spec(F), _tok_spec(D), _tok_spec(D), _full((8, D)), _bstat_spec(tps, D)],
        out_shape=[jax.ShapeDtypeStruct((T, F), BF16), jax.ShapeDtypeStruct((T, F), BF16), jax.ShapeDtypeStruct((T, F), BF16),
                   jax.ShapeDtypeStruct((T, D), F32), jax.ShapeDtypeStruct((T, D), BF16),
                   jax.ShapeDtypeStruct((8, D), F32), jax.ShapeDtypeStruct((B, 8, D), F32)],
        compiler_params=_params("arbitrary"),
    )(gt, gt, gt, val, conv_w, conv_b, w_down, x1, mod3, g_final, target)


def _ffn_down_bwd(df, w_down, act, vd, tasks=()):
    T, D = df.shape
    F = act.shape[1]

    def body(df_ref, w_ref, act_ref, vd_ref, dval_ref, dgc_ref, cstat_ref):
        da = _nt(df_ref[...], w_ref[...])
        dval_ref[...] = (da * act_ref[...].astype(F32)).astype(BF16)
        dgc = da * vd_ref[...].astype(F32)
        dgc_ref[...] = dgc.astype(BF16)

        @pl.when(pl.program_id(0) == 0)
        def _():
            cstat_ref[...] = jnp.zeros_like(cstat_ref)

        cstat_ref[0:1, :] += jnp.sum(dgc, axis=0, keepdims=True)

    return _hosted_call(
        body, "ffn_down_bwd", (T // TOKEN_TILE,),
        [_tok_spec(D), _full(w_down.shape), _tok_spec(F), _tok_spec(F)],
        [_tok_spec(F), _tok_spec(F), _full((8, F))],
        [jax.ShapeDtypeStruct((T, F), BF16), jax.ShapeDtypeStruct((T, F), BF16), jax.ShapeDtypeStruct((8, F), F32)],
        (df, w_down, act, vd), tasks)


def _ffn_up_bwd(dgc, dval, gt, conv_w, w_up, x1, mod3, g_ffn, dx2, mix, B, S, tasks=()):
    T, D = x1.shape
    F = dgc.shape[1]
    tps = S // TOKEN_TILE
    prev, nxt = _halo_specs(T, tps, F)

    def body(dgc_ref, prev_ref, next_ref, dval_ref, gt_ref, cw_ref, w_ref, x1_ref, mod_ref, g_ref, dx2_ref, mix_ref,
             du_ref, dx1_ref, dmix_ref, gstat_ref, bstat_ref, cstat_ref):
        i = pl.program_id(0)
        d = dgc_ref[...].astype(F32)
        dprev, dnext = _seq_shifts(d, prev_ref[7:8, :], next_ref[0:1, :], i % tps, tps)
        g = gt_ref[...].astype(F32)

        @pl.when(i == 0)
        def _():
            cstat_ref[...] = jnp.zeros_like(cstat_ref)

        cstat_ref[1:2, :] += jnp.sum(dnext * g, axis=0, keepdims=True)
        cstat_ref[2:3, :] += jnp.sum(d * g, axis=0, keepdims=True)
        cstat_ref[3:4, :] += jnp.sum(dprev * g, axis=0, keepdims=True)
        dgt = dnext * cw_ref[0:1, :] + d * cw_ref[1:2, :] + dprev * cw_ref[2:3, :]
        du = jnp.concatenate([dval_ref[...], dgt.astype(BF16)], axis=1)
        du_ref[...] = du
        dh2 = _nn(du, w_ref[...])
        xn, r = _rms(x1_ref[...])
        scale1 = 1.0 + mod_ref[0, 4:5, :]
        xg = xn * g_ref[...]
        dx1 = dx2_ref[...] + _rms_bwd(xn, r, dh2 * g_ref[...] * scale1)
        dx1_ref[...] = dx1
        dmix_ref[...] = (mod_ref[0, 2:3, :] * dx1).astype(BF16)

        @pl.when(i == 0)
        def _():
            gstat_ref[...] = jnp.zeros_like(gstat_ref)

        @pl.when(i % tps == 0)
        def _():
            bstat_ref[...] = jnp.zeros_like(bstat_ref)

        gstat_ref[0:1, :] += jnp.sum(dh2 * scale1 * xn, axis=0, keepdims=True)
        bstat_ref[0, 0:1, :] += jnp.sum(dh2, axis=0, keepdims=True)
        bstat_ref[0, 1:2, :] += jnp.sum(dh2 * xg, axis=0, keepdims=True)
        bstat_ref[0, 2:3, :] += jnp.sum(dx1 * mix_ref[...], axis=0, keepdims=True)

    return _hosted_call(
        body, "ffn_up_bwd", (T // TOKEN_TILE,),
        [_tok_spec(F), prev, nxt, _tok_spec(F), _tok_spec(F), _full(conv_w.shape), _full(w_up.shape), _tok_spec(D),
         _mod_spec(tps, D), _full((1, D)), _tok_spec(D), _tok_spec(D)],
        [_tok_spec(2 * F), _tok_spec(D), _tok_spec(D), _full((8, D)), _bstat_spec(tps, D), _full((8, F))],
        [jax.ShapeDtypeStruct((T, 2 * F), BF16), jax.ShapeDtypeStruct((T, D), F32), jax.ShapeDtypeStruct((T, D), BF16),
         jax.ShapeDtypeStruct((8, D), F32), jax.ShapeDtypeStruct((B, 8, D), F32), jax.ShapeDtypeStruct((8, F), F32)],
        (dgc, dgc, dgc, dval, gt, conv_w, w_up, x1, mod3, g_ffn, dx2, mix), tasks)


def _attn_out_bwd(dmix, w_out, oa, ob, g_na, g_sw, tasks=(), tm=WIDE_TILE):
    T, D = dmix.shape

    def body(dmix_ref, w_ref, oa_ref, ob_ref, gna_ref, gsw_ref, doa_ref, dob_ref, gstat_ref):
        dmixin = _nt(dmix_ref[...], w_ref[...])

        @pl.when(pl.program_id(0) == 0)
        def _():
            gstat_ref[...] = jnp.zeros_like(gstat_ref)

        for k, (o_ref, g_ref, do_ref) in enumerate(((oa_ref, gna_ref, doa_ref), (ob_ref, gsw_ref, dob_ref))):
            dn = dmixin[:, k * NA_WIDTH:(k + 1) * NA_WIDTH]
            on, r = _rms(o_ref[...])
            gstat_ref[k:k + 1, :] += jnp.sum(dn * on, axis=0, keepdims=True)
            do_ref[...] = _rms_bwd(on, r, dn * g_ref[...]).astype(BF16)

    hs = jax.ShapeDtypeStruct((T, NA_WIDTH), BF16)
    return _hosted_call(
        body, "attn_out_bwd", (T // tm,),
        [_tok_spec(D, tm), _full(w_out.shape), _tok_spec(NA_WIDTH, tm), _tok_spec(SW_WIDTH, tm), _full((1, NA_WIDTH)), _full((1, SW_WIDTH))],
        [_tok_spec(NA_WIDTH, tm), _tok_spec(SW_WIDTH, tm), _full((8, NA_WIDTH))],
        [hs, hs, jax.ShapeDtypeStruct((8, NA_WIDTH), F32)],
        (dmix, w_out, oa, ob, g_na, g_sw), tasks)


def _attn_in_bwd(dqa, dka, dva, dqb, dkb, dvb, cos_t, sin_t, w_in, x2d, mod3, g_attn, dx1, B, S, tm=WIDE_TILE):
    T, D = x2d.shape
    tps = S // tm

    def body(dqa_ref, dka_ref, dva_ref, dqb_ref, dkb_ref, dvb_ref, cos_ref, sin_ref, w_ref, x_ref, mod_ref, g_ref, dx1_ref,
             gx_ref, dproj_ref, gstat_ref, bstat_ref):
        i = pl.program_id(0)
        drb = jnp.concatenate([dqb_ref[...] * Q_SCALE, dkb_ref[...]], axis=1)
        reps = (1, ROPE_WIDTH // (2 * HEAD_DIM))
        drb = drb * jnp.tile(cos_ref[...], reps) + _rot_half(drb * jnp.tile(sin_ref[...], reps))
        dproj = jnp.concatenate([dqa_ref[...] * Q_SCALE, dka_ref[...], dva_ref[...], drb, dvb_ref[...]], axis=1).astype(BF16)
        dproj_ref[...] = dproj
        dh = _nn(dproj, w_ref[...])
        xn, r = _rms(x_ref[...])
        scale1 = 1.0 + mod_ref[0, 1:2, :]
        gx_ref[...] = dx1_ref[...] + _rms_bwd(xn, r, dh * g_ref[...] * scale1)

        @pl.when(i == 0)
        def _():
            gstat_ref[...] = jnp.zeros_like(gstat_ref)

        @pl.when(i % tps == 0)
        def _():
            bstat_ref[...] = jnp.zeros_like(bstat_ref)

        gstat_ref[0:1, :] += jnp.sum(dh * scale1 * xn, axis=0, keepdims=True)
        bstat_ref[0, 0:1, :] += jnp.sum(dh, axis=0, keepdims=True)
        bstat_ref[0, 1:2, :] += jnp.sum(dh * (xn * g_ref[...]), axis=0, keepdims=True)

    rope = _rope_spec(tps, tm)
    return pl.pallas_call(
        body, name="attn_in_bwd", grid=(T // tm,),
        in_specs=[_tok_spec(NA_WIDTH, tm), _tok_spec(NA_WIDTH, tm), _tok_spec(NA_WIDTH, tm), _tok_spec(SW_WIDTH, tm),
                  _tok_spec(SW_KV_WIDTH, tm), _tok_spec(SW_KV_WIDTH, tm), rope, rope, _full(w_in.shape), _tok_spec(D, tm),
                  _mod_spec(tps, D), _full((1, D)), _tok_spec(D, tm)],
        out_specs=[_tok_spec(D, tm), _tok_spec(IN_WIDTH, tm), _full((8, D)), _bstat_spec(tps, D)],
        out_shape=[jax.ShapeDtypeStruct((T, D), F32), jax.ShapeDtypeStruct((T, IN_WIDTH), BF16),
                   jax.ShapeDtypeStruct((8, D), F32), jax.ShapeDtypeStruct((B, 8, D), F32)],
        compiler_params=_params("arbitrary"),
    )(dqa, dka, dva, dqb, dkb, dvb, cos_t, sin_t, w_in, x2d, mod3, g_attn, dx1)


def _matmul_tn(a, b, name, tm=None, tk=512):
    T, M = a.shape
    N = b.shape[1]
    tm = M if tm is None else tm
    nk = T // tk

    def body(a_ref, b_ref, o_ref, acc):
        k = pl.program_id(1)

        @pl.when(k == 0)
        def _():
            acc[...] = jnp.zeros_like(acc)

        acc[...] += _tn(a_ref[...], b_ref[...])

        @pl.when(k == nk - 1)
        def _():
            o_ref[...] = acc[...].astype(BF16)

    return pl.pallas_call(
        body, name=name, grid=(M // tm, nk),
        in_specs=[pl.BlockSpec((tk, tm), lambda i, k: (k, i)), pl.BlockSpec((tk, N), lambda i, k: (k, 0))],
        out_specs=pl.BlockSpec((tm, N), lambda i, k: (i, 0)),
        out_shape=jax.ShapeDtypeStruct((M, N), BF16),
        scratch_shapes=[pltpu.VMEM((tm, N), F32)],
        compiler_params=_params("parallel", "arbitrary"),
    )(a, b)


def _na_geometry(S):
    rows = S // GRID_W
    wr = min(NA_ROWS_MAX, rows)
    return rows, wr


def _na_window(r, rows, wr):
    rs = jnp.clip(r - wr // 2, 0, rows - wr)
    return pl.multiple_of(rs * GRID_W, GRID_W), pl.multiple_of((rs - r + NA_ROWS_MAX - 1) * GRID_W, GRID_W)


NA_STEP_PAIRS = 2
NA_GW = NA_STEP_PAIRS * 128
NA_BWD_ROWS = 4
NA_ROWS_PER_STEP = 4


def _na_specs(S, kw_n, order):
    ng = NA_PAIRS // NA_STEP_PAIRS

    def col(k):
        return pl.BlockSpec((1, S, NA_GW), lambda *ids: (order(*ids)[0], 0, k * ng + order(*ids)[1]))
    bias = pl.BlockSpec((NA_STEP_PAIRS, N_DR * GRID_W, 128), lambda *ids: (order(*ids)[1], 0, 0))
    out = pl.BlockSpec((1, S, NA_GW), lambda *ids: (order(*ids)[0], 0, order(*ids)[1]))
    return col(0), col(1), col(2), bias, out


def _block_diag(t):
    left = lax.broadcasted_iota(jnp.int32, t.shape, 1) < HEAD_DIM
    zero = jnp.zeros_like(t)
    return jnp.concatenate([jnp.where(left, t, zero), jnp.where(left, zero, t)], axis=0)


def _diag_blocks(res):
    left = lax.broadcasted_iota(jnp.int32, (HEAD_DIM, 128), 1) < HEAD_DIM
    return jnp.where(left, res[:HEAD_DIM], res[HEAD_DIM:])


def _col_softmax(st):
    e = jnp.exp(st - jnp.max(st, axis=0, keepdims=True))
    return e * (1.0 / jnp.sum(e, axis=0, keepdims=True))


def _na_fwd(qkv, bias, tasks=()):
    B, S, _ = qkv.shape
    rows, wr = _na_geometry(S)
    kw_n = wr * GRID_W

    def body(q_ref, k_ref, v_ref, b_ref, o_ref):
        def step(it, carry):
            win = [_na_window(it * NA_ROWS_PER_STEP + u, rows, wr) for u in range(NA_ROWS_PER_STEP)]
            qrows = [pl.ds(pl.multiple_of((it * NA_ROWS_PER_STEP + u) * GRID_W, GRID_W), GRID_W) for u in range(NA_ROWS_PER_STEP)]
            krows = [pl.ds(w[0], kw_n) for w in win]
            brows = [pl.ds(w[1], kw_n) for w in win]
            lanes = [pl.ds(p * 128, 128) for p in range(NA_STEP_PAIRS)]
            chains = [(u, p) for u in range(NA_ROWS_PER_STEP) for p in range(NA_STEP_PAIRS)]
            st = {(u, p): _nt(k_ref[0, krows[u], lanes[p]], _block_diag(q_ref[0, qrows[u], lanes[p]])) for u, p in chains}
            pn = {(u, p): _col_softmax(st[(u, p)] + b_ref[p, brows[u], :]).astype(BF16) for u, p in chains}
            out = {(u, p): _diag_blocks(_tn(pn[(u, p)], v_ref[0, krows[u], lanes[p]])) for u, p in chains}
            for u in range(NA_ROWS_PER_STEP):
                o_ref[0, qrows[u], :] = jnp.concatenate([out[(u, p)] for p in range(NA_STEP_PAIRS)], axis=1)
            return carry

        lax.fori_loop(0, rows // NA_ROWS_PER_STEP, step, 0)

    q, k, v, bs, out = _na_specs(S, kw_n, lambda b, g: (b, g))
    return _hosted_call(body, "na_fwd", (B, NA_PAIRS // NA_STEP_PAIRS), [q, k, v, bs], [out],
                        [jax.ShapeDtypeStruct((B, S, NA_WIDTH), F32)], (qkv, qkv, qkv, bias), tasks)


def _na_bwd(qkv, bias, doa, tasks=()):
    B, S, _ = qkv.shape
    rows, wr = _na_geometry(S)
    kw_n = wr * GRID_W

    def body(q_ref, k_ref, v_ref, b_ref, do_ref, dq_ref, dk_ref, dv_ref, db_ref):
        @pl.when(pl.program_id(1) == 0)
        def _():
            db_ref[...] = jnp.zeros_like(db_ref)

        dk_ref[...] = jnp.zeros_like(dk_ref)
        dv_ref[...] = jnp.zeros_like(dv_ref)

        def step(it, carry):
            nu, pairs = range(NA_BWD_ROWS), range(NA_STEP_PAIRS)
            win = [_na_window(it * NA_BWD_ROWS + u, rows, wr) for u in nu]
            qrows = [pl.ds(pl.multiple_of((it * NA_BWD_ROWS + u) * GRID_W, GRID_W), GRID_W) for u in nu]
            krows = [pl.ds(w[0], kw_n) for w in win]
            brows = [pl.ds(w[1], kw_n) for w in win]
            lanes = [pl.ds(p * 128, 128) for p in pairs]
            chains = [(u, p) for u in nu for p in pairs]
            kp = {(u, p): k_ref[0, krows[u], lanes[p]] for u, p in chains}
            qbd = {(u, p): _block_diag(q_ref[0, qrows[u], lanes[p]]) for u, p in chains}
            dobd = {(u, p): _block_diag(do_ref[0, qrows[u], lanes[p]]) for u, p in chains}
            st = {c: _nt(kp[c], qbd[c]) for c in chains}
            dpt = {(u, p): _nt(v_ref[0, krows[u], lanes[p]], dobd[(u, p)]) for u, p in chains}
            pn = {(u, p): _col_softmax(st[(u, p)] + b_ref[p, brows[u], :]) for u, p in chains}
            dst = {c: pn[c] * (dpt[c] - jnp.sum(pn[c] * dpt[c], axis=0, keepdims=True)) for c in chains}
            dsb = {c: dst[c].astype(BF16) for c in chains}
            dq = {c: _diag_blocks(_tn(dsb[c], kp[c])) for c in chains}
            dk = {c: _nn(dsb[c], qbd[c]) for c in chains}
            dv = {c: _nn(pn[c].astype(BF16), dobd[c]) for c in chains}
            for u in nu:
                dq_ref[0, qrows[u], :] = jnp.concatenate([dq[(u, p)] for p in pairs], axis=1)
                dk_ref[0, krows[u], :] += jnp.concatenate([dk[(u, p)] for p in pairs], axis=1)
                dv_ref[0, krows[u], :] += jnp.concatenate([dv[(u, p)] for p in pairs], axis=1)
                for p in pairs:
                    db_ref[p, brows[u], :] += dst[(u, p)]
            return carry

        lax.fori_loop(0, rows // NA_BWD_ROWS, step, 0)

    q, k, v, bs, out = _na_specs(S, kw_n, lambda g, b: (b, g))
    hs = jax.ShapeDtypeStruct((B, S, NA_WIDTH), F32)
    return _hosted_call(body, "na_bwd", (NA_PAIRS // NA_STEP_PAIRS, B), [q, k, v, bs, out], [out, out, out, bs],
                        [hs, hs, hs, jax.ShapeDtypeStruct((NA_PAIRS, N_DR * GRID_W, 128), F32)], (qkv, qkv, qkv, bias, doa), tasks)


SW_PAIRS = SW_HEADS // 2


def _sw_band(n, S):
    kw_n = 3 * SW_BLOCK
    start = pl.multiple_of(jnp.clip(n * SW_BLOCK - SW_BLOCK, 0, S - kw_n), SW_BLOCK)
    kpos = start + lax.broadcasted_iota(jnp.int32, (kw_n, SW_BLOCK), 0)
    qpos = n * SW_BLOCK + lax.broadcasted_iota(jnp.int32, (kw_n, SW_BLOCK), 1)
    return start, jnp.abs(qpos - kpos) <= SW_WINDOW


def _kv_halves(t):
    left = lax.broadcasted_iota(jnp.int32, t.shape, 1) < HEAD_DIM
    swapped = pltpu.roll(t, HEAD_DIM, axis=1)
    zero = jnp.zeros_like(t)
    return {(0, 0): jnp.where(left, t, zero), (0, 1): jnp.where(left, zero, swapped),
            (1, 0): jnp.where(left, swapped, zero), (1, 1): jnp.where(left, zero, t)}


def _sw_probs(st, ok, sk):
    st = jnp.where(ok, st, NEG)
    m = jnp.maximum(jnp.max(st, axis=0, keepdims=True), sk)
    e = jnp.exp(st - m)
    esk = jnp.exp(sk - m)
    inv = 1.0 / (jnp.sum(e, axis=0, keepdims=True) + esk)
    return e * inv, esk * inv


def _sw_specs(S):
    q = pl.BlockSpec((1, S, SW_WIDTH), lambda b, *_: (b, 0, ROPE_LO // SW_WIDTH))
    k = pl.BlockSpec((1, S, SW_KV_WIDTH), lambda b, *_: (b, 0, (ROPE_LO + SW_WIDTH) // SW_KV_WIDTH))
    v = pl.BlockSpec((1, S, SW_KV_WIDTH), lambda b, *_: (b, 0, (ROPE_LO + ROPE_WIDTH) // SW_KV_WIDTH))
    return q, k, v


SW_FWD_SPLIT = 2


def _sw_fwd(sink, qkv, tasks=()):
    B, S, _ = qkv.shape
    kw_n = 3 * SW_BLOCK

    def body(sink_ref, q_ref, k_ref, v_ref, o_ref):
        def step(n, carry):
            start, ok = _sw_band(n, S)
            qrows = pl.ds(pl.multiple_of(n * SW_BLOCK, SW_BLOCK), SW_BLOCK)
            krows = pl.ds(start, kw_n)
            kh, vh = _kv_halves(k_ref[0, krows, :]), _kv_halves(v_ref[0, krows, :])
            heads = [(p, e) for p in range(SW_PAIRS) for e in range(2)]
            qp = [q_ref[0, qrows, pl.ds(p * 128, 128)] for p in range(SW_PAIRS)]
            kv_of = lambda p: p // (SW_PAIRS // SW_KV_HEADS)
            st = {(p, e): _nt(kh[(kv_of(p), e)], qp[p]) for p, e in heads}
            pn = {(p, e): _sw_probs(st[(p, e)], ok, sink_ref[2 * p + e])[0].astype(BF16) for p, e in heads}
            outs = [_tn(pn[(p, 0)], vh[(kv_of(p), 0)]) + _tn(pn[(p, 1)], vh[(kv_of(p), 1)]) for p in range(SW_PAIRS)]
            o_ref[0, qrows, :] = jnp.concatenate(outs, axis=1)
            return carry

        half = (S // SW_BLOCK) // SW_FWD_SPLIT
        lax.fori_loop(pl.program_id(1) * half, (pl.program_id(1) + 1) * half, step, 0)

    q, k, v = _sw_specs(S)
    return _hosted_call(
        body, "sw_fwd", (B, SW_FWD_SPLIT), [pl.BlockSpec(memory_space=pltpu.SMEM), q, k, v],
        [pl.BlockSpec((1, S, SW_WIDTH), lambda b, s: (b, 0, 0))], [jax.ShapeDtypeStruct((B, S, SW_WIDTH), F32)],
        (sink, qkv, qkv, qkv), tasks)


def _sw_bwd(sink, qkv, dob):
    B, S, _ = qkv.shape
    kw_n = 3 * SW_BLOCK

    fold_rows = 256

    def body(sink_ref, q_ref, k_ref, v_ref, do_ref, dq_ref, dk_ref, dv_ref, dsink_ref, dk_acc, dv_acc):
        @pl.when(pl.program_id(0) == 0)
        def _():
            dsink_ref[...] = jnp.zeros_like(dsink_ref)

        dk_acc[...] = jnp.zeros_like(dk_acc)
        dv_acc[...] = jnp.zeros_like(dv_acc)
        ppk = SW_PAIRS // SW_KV_HEADS

        def step(n, carry):
            start, ok = _sw_band(n, S)
            qrows = pl.ds(pl.multiple_of(n * SW_BLOCK, SW_BLOCK), SW_BLOCK)
            krows = pl.ds(start, kw_n)
            kh, vh = _kv_halves(k_ref[0, krows, :]), _kv_halves(v_ref[0, krows, :])
            heads = [(p, e) for p in range(SW_PAIRS) for e in range(2)]
            qp = [q_ref[0, qrows, pl.ds(p * 128, 128)] for p in range(SW_PAIRS)]
            dop = [do_ref[0, qrows, pl.ds(p * 128, 128)] for p in range(SW_PAIRS)]
            st = {(p, e): _nt(kh[(p // ppk, e)], qp[p]) for p, e in heads}
            dpt = {(p, e): _nt(vh[(p // ppk, e)], dop[p]) for p, e in heads}
            pnb, dsb = {}, {}
            for p, e in heads:
                pn, psink = _sw_probs(st[(p, e)], ok, sink_ref[2 * p + e])
                delta = jnp.sum(pn * dpt[(p, e)], axis=0, keepdims=True)
                dsb[(p, e)] = (pn * (dpt[(p, e)] - delta)).astype(BF16)
                pnb[(p, e)] = pn.astype(BF16)
                dsink_ref[2 * p + e:2 * p + e + 1, :] += -(psink * delta)
            dq_ref[0, qrows, :] = jnp.concatenate(
                [_tn(dsb[(p, 0)], kh[(p // ppk, 0)]) + _tn(dsb[(p, 1)], kh[(p // ppk, 1)]) for p in range(SW_PAIRS)], axis=1)
            left = lax.broadcasted_iota(jnp.int32, (kw_n, 128), 1) < HEAD_DIM
            dks, dvs = [], []
            for kv in range(SW_KV_HEADS):
                dk = dv = None
                for p in range(kv * ppk, (kv + 1) * ppk):
                    dk_p = jnp.where(left, _nn(dsb[(p, 0)], qp[p]), _nn(dsb[(p, 1)], qp[p]))
                    dv_p = jnp.where(left, _nn(pnb[(p, 0)], dop[p]), _nn(pnb[(p, 1)], dop[p]))
                    dk = dk_p if dk is None else dk + dk_p
                    dv = dv_p if dv is None else dv + dv_p
                dks.append(dk)
                dvs.append(dv)
            dk_acc[krows, :] += jnp.concatenate(dks, axis=1)
            dv_acc[krows, :] += jnp.concatenate(dvs, axis=1)
            return carry

        lax.fori_loop(0, S // SW_BLOCK, step, 0)

        def fold(i, carry):
            rows = pl.ds(pl.multiple_of(i * fold_rows, fold_rows), fold_rows)
            left = lax.broadcasted_iota(jnp.int32, (fold_rows, 128), 1) < HEAD_DIM
            for acc, out_ref in ((dk_acc, dk_ref), (dv_acc, dv_ref)):
                a, b = acc[rows, 0:128], acc[rows, 128:256]
                out_ref[0, rows, :] = jnp.where(left, a + pltpu.roll(a, HEAD_DIM, axis=1), b + pltpu.roll(b, HEAD_DIM, axis=1))
            return carry

        lax.fori_loop(0, S // fold_rows, fold, 0)

        @pl.when(pl.program_id(0) == B - 1)
        def _():
            dsink_ref[...] = jnp.broadcast_to(jnp.sum(dsink_ref[...], axis=1, keepdims=True), dsink_ref.shape)

    q, k, v = _sw_specs(S)
    qo = pl.BlockSpec((1, S, SW_WIDTH), lambda b: (b, 0, 0))
    ko = pl.BlockSpec((1, S, SW_KV_WIDTH), lambda b: (b, 0, 0))
    return pl.pallas_call(
        body, name="sw_bwd", grid=(B,),
        in_specs=[pl.BlockSpec(memory_space=pltpu.SMEM), q, k, v, qo],
        out_specs=[qo, ko, ko, _full((SW_HEADS, 128))],
        out_shape=[jax.ShapeDtypeStruct((B, S, SW_WIDTH), F32), jax.ShapeDtypeStruct((B, S, SW_KV_WIDTH), F32),
                   jax.ShapeDtypeStruct((B, S, SW_KV_WIDTH), F32), jax.ShapeDtypeStruct((SW_HEADS, 128), F32)],
        scratch_shapes=[pltpu.VMEM((S, 2 * SW_KV_WIDTH), F32), pltpu.VMEM((S, 2 * SW_KV_WIDTH), F32)],
        compiler_params=_params("arbitrary"),
    )(sink, qkv, qkv, qkv, dob)


def _pack_sum_adamw(packs, params):
    W = packs.shape[1]
    n_p = len(params)

    def body(p_ref, *refs):
        ins, tot_ref, outs = refs[:3 * n_p], refs[3 * n_p], refs[3 * n_p + 1:]
        tot = p_ref[0:8, :]
        for d in range(1, N_DEV):
            tot = tot + p_ref[8 * d:8 * d + 8, :]
        tot_ref[...] = tot
        for i, (w, _, _, rows, off) in enumerate(params):
            n = w.shape[1]
            g = tot[rows[0]:rows[0] + 1, off:off + n]
            for r in rows[1:]:
                g = g + tot[r:r + 1, off:off + n]
            w_ref, m_ref, v_ref = ins[3 * i:3 * i + 3]
            g_ref, d_ref, nm_ref, nv_ref = outs[4 * i:4 * i + 4]
            g_ref[...] = g
            d_ref[...], nm_ref[...], nv_ref[...] = _adam_update(w_ref[...], g, m_ref[...], v_ref[...])

    res = pl.pallas_call(
        body, name="small_adamw",
        out_shape=[jax.ShapeDtypeStruct((8, W), F32)] + [jax.ShapeDtypeStruct(p[0].shape, F32) for p in params for _ in range(4)],
        compiler_params=pltpu.CompilerParams(vmem_limit_bytes=VMEM_LIMIT),
    )(packs, *[a for p in params for a in p[:3]])
    return res[0], [res[1 + 4 * i:5 + 4 * i] for i in range(n_p)]


def _adam_update(w, g, m, v):
    c1 = 1.0 - ADAM_B1 ** ADAM_STEP
    c2 = 1.0 - ADAM_B2 ** ADAM_STEP
    nm = ADAM_B1 * m + (1.0 - ADAM_B1) * g
    nv = ADAM_B2 * v + (1.0 - ADAM_B2) * (g * g)
    return -ADAM_LR * ((nm / c1) / (jnp.sqrt(nv / c2) + ADAM_EPS) + ADAM_WD * w), nm, nv


def _adamw(w, g, m, v, name):
    def body(w_ref, g_ref, m_ref, v_ref, d_ref, nm_ref, nv_ref):
        d_ref[...], nm_ref[...], nv_ref[...] = _adam_update(w_ref[...], g_ref[...], m_ref[...], v_ref[...])

    s = jax.ShapeDtypeStruct(w.shape, F32)
    return pl.pallas_call(body, name=name, out_shape=[s, s, s],
                          compiler_params=pltpu.CompilerParams(vmem_limit_bytes=VMEM_LIMIT))(w, g, m, v)


def _sum_adamw_rows(R):
    return max(r for r in range(16, min(R, 256) + 1, 16) if R % r == 0)


def _sum_adamw_steps(R):
    return R // _sum_adamw_rows(R)


def _sum_adamw(own, recvb, w, m, v, name, tasks=()):
    R, C = own.shape
    rc = _sum_adamw_rows(R)

    def body(own_ref, r_ref, w_ref, m_ref, v_ref, g_ref, d_ref, nm_ref, nv_ref):
        g = own_ref[...]
        for j in range(3):
            g = g + r_ref[j].astype(F32)
        g_ref[...] = g
        d_ref[...], nm_ref[...], nv_ref[...] = _adam_update(w_ref[...], g, m_ref[...], v_ref[...])

    blk = pl.BlockSpec((rc, C), lambda i: (i, 0))
    s = jax.ShapeDtypeStruct((R, C), F32)
    return _hosted_call(body, name, (R // rc,), [blk, pl.BlockSpec((3, rc, C), lambda i: (0, i, 0)), blk, blk, blk],
                        [blk, blk, blk, blk], [s, s, s, s], (own, recvb, w, m, v), tasks)


def _by_device(dw):
    return dw.reshape(N_DEV, dw.shape[0] // N_DEV, dw.shape[1])


def _local_step(x, mod, g_attn, w_in, bias, sw_sink, g_na_out, g_sw_out, w_out, g_ffn, w_up, conv_w, conv_b, w_down,
                g_final, target, sharded):
    B, S, D = x.shape
    T = B * S
    x2d = x.reshape(T, D)
    mod3 = mod.reshape(B, 6, D)
    cos_t, sin_t = _rope_tables(S)
    sink = sw_sink.reshape(SW_HEADS)
    n_tiles = T // WIDE_TILE
    full = lambda g: g.reshape(N_DEV * g.shape[1], g.shape[2])

    rider = lambda w, mid, lo, n, into=None: [_gather_task(w, mid, rows=(lo, n), into=into)] if sharded else []
    if sharded:
        qu, hd = w_up.shape[0] // 4, w_down.shape[0] // 2
    (h, qkv), got = _attn_in(x2d, mod3, g_attn, w_in, cos_t, sin_t, S,
                             [_gather_task(w_out, n_tiles // 2)] + rider(w_up, n_tiles - 1, 0, qu) if sharded else [])
    if sharded:
        w_out, w_up_buf = full(got[0][0]), got[1][0]
    qkv3 = qkv.reshape(B, S, IN_WIDTH)
    na_steps, sw_steps = B * (NA_PAIRS // NA_STEP_PAIRS), B * SW_FWD_SPLIT
    (oa,), got = _na_fwd(qkv3, bias, rider(w_up, na_steps - 1, qu, 2 * qu, w_up_buf) if sharded else [])
    if sharded:
        w_up_buf = got[0][0]
    oa = oa.reshape(T, NA_WIDTH)
    (ob,), got = _sw_fwd(sink, qkv3, rider(w_up, sw_steps // 2, 3 * qu, qu, w_up_buf) if sharded else [])
    if sharded:
        w_up = full(got[0][0])
    ob = ob.reshape(T, SW_WIDTH)
    (mixin, mix, x1), got = _attn_out(oa, ob, x2d, mod3, g_na_out, g_sw_out, w_out, S,
                                      rider(w_down, n_tiles // 2 + 1, 0, hd) if sharded else [])
    (h2, val, gt), got = _ffn_up(x1, mod3, g_ffn, w_up, S, rider(w_down, n_tiles // 2, hd, hd, got[0][0]) if sharded else [])
    if sharded:
        w_down = full(got[0][0])
    a, act, vd, dx2, df, gstat_f, bstat_f = _ffn_down(gt, val, conv_w, conv_b, w_down, x1, mod3, g_final, target.reshape(T, D), B, S)
    F = val.shape[1]

    dw_down = _matmul_tn(a, df, "dw_down")
    (dval, dgc, cstat), got = _ffn_down_bwd(df, w_down, act, vd, [_swap_task(_by_device(dw_down))] if sharded else [])
    if sharded:
        send_down, own_down = _chip_sums(_by_device(dw_down), got[0][0])
    (du, dx1, dmix, gstat_u, bstat_u, cstat_w), got = _ffn_up_bwd(dgc, dval, gt, conv_w, w_up, x1, mod3, g_ffn, dx2, mix, B, S,
                                                                  [_exchange_task(send_down)] if sharded else [])
    if sharded:
        dw_down = (own_down, got[0][0])
    dw_up = _matmul_tn(du, h2, "dw_up", tm=F)
    dw_out = _matmul_tn(mixin, dmix, "dw_out")
    (doa, dob, gstat_o), got = _attn_out_bwd(dmix, w_out, oa, ob, g_na_out, g_sw_out,
                                             [_swap_task(_by_device(dw_up)), _swap_task(_by_device(dw_out))] if sharded else [])
    if sharded:
        send_up, own_up = _chip_sums(_by_device(dw_up), got[0][0])
        send_out, own_out = _chip_sums(_by_device(dw_out), got[1][0])
    (dqa, dka, dva, dbt), got = _na_bwd(qkv3, bias, doa.reshape(B, S, NA_WIDTH),
                                        [_exchange_task(send_up), _exchange_task(send_out)] if sharded else [])
    if sharded:
        dw_up, dw_out = (own_up, got[0][0]), (own_out, got[1][0])
    dqb, dkb, dvb, dsink = _sw_bwd(sink, qkv3, dob.reshape(B, S, SW_WIDTH))
    r2 = lambda t: t.reshape(T, t.shape[-1])
    grad_x, dproj, gstat_i, bstat_i = _attn_in_bwd(r2(dqa), r2(dka), r2(dva), r2(dqb), r2(dkb), r2(dvb), cos_t, sin_t, w_in, x2d, mod3,
                                                   g_attn, dx1, B, S)
    dw_in = _matmul_tn(dproj, h, "dw_in")

    dmod = jnp.stack([bstat_i[:, 0], bstat_i[:, 1], bstat_u[:, 2], bstat_u[:, 0], bstat_u[:, 1], bstat_f[:, 0]], axis=1)
    small = dict(g_attn=gstat_i[0], g_ffn=gstat_u[0], g_final=gstat_f[0], loss=gstat_f[1, 0], g_na_out=gstat_o[0], g_sw_out=gstat_o[1],
                 sw_sink=dsink[:, 0], conv_b=cstat[0], conv_w=cstat_w[1:4], dbt=dbt,
                 raw=(bstat_i, bstat_u, bstat_f, gstat_i, gstat_u, gstat_f, gstat_o, dsink, cstat, cstat_w))
    return grad_x.reshape(B, S, D), dict(w_in=dw_in, w_out=dw_out, w_up=dw_up, w_down=dw_down), dmod, small


def _pack_slab(raw, drpb):
    D, F = raw[3].shape[1], raw[8].shape[1]
    n_seq = raw[0].shape[0]

    def body(bi_ref, bu_ref, bf_ref, gi_ref, gu_ref, gf_ref, go_ref, ds_ref, cs_ref, cw_ref, rp_ref, o_ref):
        o_ref[...] = jnp.zeros_like(o_ref)
        for b in range(n_seq):
            mods = (bi_ref[b, 0:1, :], bi_ref[b, 1:2, :], bu_ref[b, 2:3, :], bu_ref[b, 0:1, :], bu_ref[b, 1:2, :], bf_ref[b, 0:1, :])
            for k, row in enumerate(mods):
                o_ref[b:b + 1, k * D:(k + 1) * D] = row
        o = 0
        for row in (gi_ref[0:1, :], gu_ref[0:1, :], gf_ref[0:1, :], go_ref[0:1, :], go_ref[1:2, :]):
            o_ref[2:3, o:o + row.shape[1]] = row
            o += row.shape[1]
        ds = ds_ref[...]
        eye = lax.broadcasted_iota(jnp.int32, ds.shape, 0) == lax.broadcasted_iota(jnp.int32, ds.shape, 1)
        o_ref[2:3, o:o + 128] = jnp.sum(jnp.where(eye, ds, 0.0), axis=0, keepdims=True)
        o_ref[2:3, o + 128:o + 256] = gf_ref[1:2, 0:128]
        o_ref[3:4, 0:F] = cs_ref[0:1, :]
        o_ref[4:5, 0:rp_ref.shape[1]] = rp_ref[...]
        o_ref[5:8, 0:F] = cw_ref[1:4, :]

    return pl.pallas_call(body, name="pack_slab", out_shape=jax.ShapeDtypeStruct((8, PACK_W), F32),
                          compiler_params=pltpu.CompilerParams(vmem_limit_bytes=VMEM_LIMIT))(*raw, drpb)


def kernel(x, c, w_ada, b_ada, g_attn, w_in, na_rpb, sw_sink, g_na_out, g_sw_out, w_out, g_ffn, w_up, conv_w, conv_b, w_down, g_final, loss_target, m_w_ada, m_b_ada, m_g_attn, m_w_in, m_na_rpb, m_sw_sink, m_g_na_out, m_g_sw_out, m_w_out, m_g_ffn, m_w_up, m_conv_w, m_conv_b, m_w_down, m_g_final, v_w_ada, v_b_ada, v_g_attn, v_w_in, v_na_rpb, v_sw_sink, v_g_na_out, v_g_sw_out, v_w_out, v_g_ffn, v_w_up, v_conv_w, v_conv_b, v_w_down, v_g_final):
    B, S, D = x.shape
    me = 4 * lax.axis_index("x") + 2 * lax.axis_index("y") + lax.axis_index("c")
    ada_c = w_ada.shape[2]
    F_l = conv_w.shape[2]

    cw_l = jnp.pad(conv_w[0], ((0, 8 - conv_w.shape[1]), (0, 0)))
    c_l = jnp.pad(c, ((0, 8 - B), (0, 0)))
    tr = {"w_in", "w_up"}
    w_in_t = jnp.transpose(w_in[0])
    shards = dict(w_out=w_out[0].astype(BF16), w_up=jnp.transpose(w_up[0]).astype(BF16), w_down=w_down[0].astype(BF16))

    b_ada_l = lax.dynamic_slice(b_ada, (0, me * ada_c), (1, ada_c))
    slabs, mod_all, w_in_all = _ada_fwd(jnp.concatenate([c_l, cw_l], axis=1), w_ada[0], b_ada_l, w_in_t.astype(BF16), B)
    c_all = slabs[:, :, :D].reshape(N_DEV * 8, D)
    conv_w_f = jnp.transpose(slabs[:, :3, D:], (1, 0, 2)).reshape(3, N_DEV * F_l)
    mod_mine = lax.dynamic_slice(mod_all, (0, me * B, 0), (N_DEV, B, ada_c))
    mod = jnp.transpose(mod_mine, (1, 0, 2)).reshape(B, N_DEV * ada_c)
    w_in_f = w_in_all.reshape(N_DEV * w_in_t.shape[0], D)

    bias = _na_bias_table(na_rpb[0])

    grad_x, dw, dmod, small = _local_step(x, mod, g_attn, w_in_f, bias, sw_sink, g_na_out, g_sw_out, shards["w_out"], g_ffn,
                                          shards["w_up"], conv_w_f, conv_b, shards["w_down"], g_final.reshape(1, D), loss_target,
                                          sharded=True)
    drpb = _na_bias_grad(small["dbt"])

    slab = _pack_slab(small["raw"], drpb.reshape(1, -1))
    weights = dict(w_ada=w_ada, b_ada=b_ada, g_attn=g_attn, w_in=w_in, na_rpb=na_rpb, sw_sink=sw_sink, g_na_out=g_na_out,
                   g_sw_out=g_sw_out, w_out=w_out, g_ffn=g_ffn, w_up=w_up, conv_w=conv_w, conv_b=conv_b, w_down=w_down, g_final=g_final)
    ms = dict(w_ada=m_w_ada, b_ada=m_b_ada, g_attn=m_g_attn, w_in=m_w_in, na_rpb=m_na_rpb, sw_sink=m_sw_sink, g_na_out=m_g_na_out,
              g_sw_out=m_g_sw_out, w_out=m_w_out, g_ffn=m_g_ffn, w_up=m_w_up, conv_w=m_conv_w, conv_b=m_conv_b, w_down=m_w_down, g_final=m_g_final)
    vs = dict(w_ada=v_w_ada, b_ada=v_b_ada, g_attn=v_g_attn, w_in=v_w_in, na_rpb=v_na_rpb, sw_sink=v_sw_sink, g_na_out=v_g_na_out,
              g_sw_out=v_g_sw_out, w_out=v_w_out, g_ffn=v_g_ffn, w_up=v_w_up, conv_w=v_conv_w, conv_b=v_conv_b, w_down=v_w_down, g_final=v_g_final)
    names = list(weights)
    grads, deltas, new_m, new_v = {}, {}, {}, {}
    flat = lambda t: t.reshape(1, -1)

    def shard2d(nm):
        if nm in tr:
            return (lambda t: jnp.transpose(t[0])), (lambda t: jnp.transpose(t)[None])
        return (lambda t: t[0]), (lambda t: t[None])

    def finish_sum(nm, own, recvb, tasks=()):
        r, back = shard2d(nm)
        (g2, d_, m_, v_), got = _sum_adamw(own, recvb, r(weights[nm]), r(ms[nm]), r(vs[nm]), "adamw_" + nm, tasks)
        grads[nm], deltas[nm], new_m[nm], new_v[nm] = back(g2), back(d_), back(m_), back(v_)
        return got

    g8_in = _by_device(dw["w_in"])
    got = finish_sum("w_down", *dw["w_down"], [_swap_task(g8_in)])
    send_in, own_in = _chip_sums(g8_in, got[0][0])
    n_up = _sum_adamw_steps(dw["w_up"][0].shape[0])
    got = finish_sum("w_up", *dw["w_up"], [_exchange_task(send_in), _gather_task(slab, n_up - 1)])
    packs = got[1][0]
    finish_sum("w_in", own_in, got[0][0])
    finish_sum("w_out", *dw["w_out"])

    where = dict(b_ada=((0, 1), 0), g_attn=((2,), 0), g_ffn=((2,), D), g_final=((2,), 2 * D), g_na_out=((2,), 3 * D),
                 g_sw_out=((2,), 3 * D + NA_WIDTH), sw_sink=((2,), 3 * D + NA_WIDTH + SW_WIDTH), conv_b=((3,), 0), na_rpb=((4,), 0))
    tot, small_out = _pack_sum_adamw(packs.reshape(N_DEV * 8, PACK_W),
                                     [(flat(weights[n]), flat(ms[n]), flat(vs[n])) + where[n] for n in where])
    for n, (g_, d_, m_, v_) in zip(where, small_out):
        shp = weights[n].shape
        grads[n], deltas[n], new_m[n], new_v[n] = g_.reshape(shp), d_.reshape(shp), m_.reshape(shp), v_.reshape(shp)
    loss = tot[2, 3 * D + NA_WIDTH + SW_WIDTH + 128]

    dmod_cols = lax.dynamic_slice(packs.reshape(N_DEV * 8, PACK_W), (0, me * ada_c), (N_DEV * 8, ada_c))
    for nm, g2 in (("w_ada", _ada_bwd(c_all, dmod_cols)), ("conv_w", lax.dynamic_slice(tot[5:8], (0, me * F_l), (3, F_l)))):
        r, back = shard2d(nm)
        d_, m_, v_ = _adamw(r(weights[nm]), g2, r(ms[nm]), r(vs[nm]), "adamw_" + nm)
        grads[nm], deltas[nm], new_m[nm], new_v[nm] = back(g2), back(d_), back(m_), back(v_)
    return (loss, grad_x, *[grads[n] for n in names], *[deltas[n] for n in names], *[new_m[n] for n in names],
            *[new_v[n] for n in names])
```

```python
import functools

import numpy as np
import jax
import jax.numpy as jnp
from jax import lax
from jax.experimental import pallas as pl
from jax.experimental.pallas import tpu as pltpu

F32, BF16 = jnp.float32, jnp.bfloat16
MESH_ID = pl.DeviceIdType.MESH
N_DEV = 8

HEAD_DIM = 64
NA_HEADS = 8
SW_HEADS = 8
SW_KV_HEADS = 2
SW_GROUP = SW_HEADS // SW_KV_HEADS
NA_WIDTH = NA_HEADS * HEAD_DIM
SW_WIDTH = SW_HEADS * HEAD_DIM
SW_KV_WIDTH = SW_KV_HEADS * HEAD_DIM
ROPE_WIDTH = SW_WIDTH + SW_KV_WIDTH
IN_WIDTH = 3 * NA_WIDTH + SW_WIDTH + 2 * SW_KV_WIDTH
ROPE_LO = 3 * NA_WIDTH
GRID_W = 64
NA_ROWS_MAX = 8
NA_COLS = 16
N_DR = 2 * NA_ROWS_MAX - 1
N_DC = 2 * NA_COLS - 1
SW_WINDOW = 128
SW_BLOCK = 128
ROPE_THETA = 10000.0
EPS = 1e-6
NEG = -1e30
Q_SCALE = HEAD_DIM ** -0.5

ADAM_LR = 0.001
ADAM_B1 = 0.9
ADAM_B2 = 0.999
ADAM_EPS = 1e-08
ADAM_WD = 0.01
ADAM_STEP = 10

TOKEN_TILE = 256
WIDE_TILE = 512
VMEM_LIMIT = 56 * 1024 * 1024

PACK_W = 6144


def _nn(a, b):
    return jnp.dot(a, b, preferred_element_type=F32)


def _nt(a, b):
    return lax.dot_general(a, b, (((1,), (1,)), ((), ())), preferred_element_type=F32)


def _tn(a, b):
    return lax.dot_general(a, b, (((0,), (0,)), ((), ())), preferred_element_type=F32)


def _rms(x):
    r = lax.rsqrt(jnp.mean(x * x, axis=-1, keepdims=True) + EPS)
    return x * r, r


def _rms_bwd(xn, r, gy):
    return r * (gy - xn * jnp.mean(xn * gy, axis=-1, keepdims=True))


def _params(*sem):
    return pltpu.CompilerParams(dimension_semantics=sem, vmem_limit_bytes=VMEM_LIMIT)


def _full(shape):
    n = len(shape)
    return pl.BlockSpec(shape, lambda *_: (0,) * n)


def _mesh_pos():
    return lax.axis_index("x"), lax.axis_index("y"), lax.axis_index("c")


def _row_chunk(r):
    for rc in (128, 64, 32, 16):
        if r % rc == 0:
            return rc
    raise ValueError(f"rows {r} not a multiple of 16")


class _Task:
    def __init__(self, inputs, out_shapes, sems, start, finish, mid=None, mid_step=None, alias=None):
        self.inputs, self.out_shapes, self.sems = list(inputs), list(out_shapes), list(sems)
        self.start, self.finish, self.mid, self.mid_step = start, finish, mid, mid_step
        self.alias = alias


def _hosted_call(body, name, grid, in_specs, out_specs, out_shape, operands, tasks, scratch_shapes=()):
    n_in, n_out, n_scr = len(in_specs), len(out_specs), len(scratch_shapes)
    t_in = [len(t.inputs) for t in tasks]
    t_out = [len(t.out_shapes) for t in tasks]
    t_sem = [len(t.sems) for t in tasks]
    n_steps = int(np.prod(grid))

    def wrapped(*refs):
        ins, rest = refs[:n_in], refs[n_in:]
        task_ins, rest = rest[:sum(t_in)], rest[sum(t_in):]
        outs, rest = rest[:n_out], rest[n_out:]
        task_outs, rest = rest[:sum(t_out)], rest[sum(t_out):]
        scr, task_sems = rest[:n_scr], rest[n_scr:]
        step = pl.program_id(0)
        for ax in range(1, len(grid)):
            step = step * grid[ax] + pl.program_id(ax)
        parts = []
        oi = oo = os_ = 0
        for t, a, b, c in zip(tasks, t_in, t_out, t_sem):
            parts.append((t, task_ins[oi:oi + a], task_outs[oo:oo + b], task_sems[os_:os_ + c]))
            oi, oo, os_ = oi + a, oo + b, os_ + c
        for t, ti, to, ts in parts:
            pl.when(step == 0)(functools.partial(t.start, ti, to, ts))
            if t.mid is not None:
                pl.when(step == t.mid_step)(functools.partial(t.mid, ti, to, ts))
        body(*ins, *outs, *scr)
        for t, ti, to, ts in parts:
            pl.when(step == n_steps - 1)(functools.partial(t.finish, ti, to, ts))

    hbm = pl.BlockSpec(memory_space=pl.ANY)
    aliases, oi, oo = {}, n_in, n_out
    for t, a, b in zip(tasks, t_in, t_out):
        if t.alias is not None:
            aliases[oi + t.alias[0]] = oo + t.alias[1]
        oi, oo = oi + a, oo + b
    res = pl.pallas_call(
        wrapped, name=name, grid=grid,
        in_specs=list(in_specs) + [hbm] * sum(t_in),
        out_specs=list(out_specs) + [hbm] * sum(t_out),
        out_shape=list(out_shape) + [s for t in tasks for s in t.out_shapes],
        scratch_shapes=list(scratch_shapes) + [s for t in tasks for s in t.sems],
        input_output_aliases=aliases,
        compiler_params=_params(*(["arbitrary"] * len(grid))),
    )(*operands, *[a for t in tasks for a in t.inputs])
    own, extra = res[:n_out], res[n_out:]
    per_task, o = [], 0
    for b in t_out:
        per_task.append(extra[o:o + b])
        o += b
    return own, per_task


def _gather_task(shard, mid_step, rows=None, into=None):
    lo, n = (0, shard.shape[0]) if rows is None else rows

    def parts(ins, outs, sems):
        x_ref, out_ref, (send_sems, recv_sems, local_sem) = ins[0], outs[0], sems
        x_, y_, c_ = _mesh_pos()
        me, sibling = (x_, y_, c_), (x_, y_, 1 - c_)
        chips = [(1 - x_, y_), (x_, 1 - y_), (1 - x_, 1 - y_)]
        x_ref = x_ref.at[pl.ds(lo, n)]

        def rows(px, py, pc):
            return out_ref.at[4 * px + 2 * py + pc, pl.ds(lo, n)]

        def copy(k, block, to, src=None):
            return pltpu.make_async_remote_copy(
                src_ref=rows(*block) if src is None else src, dst_ref=rows(*block),
                send_sem=send_sems.at[k], recv_sem=recv_sems.at[k], device_id=to, device_id_type=MESH_ID)

        return dict(
            mine=lambda: pltpu.make_async_copy(x_ref, rows(*me), local_sem),
            first=lambda: [copy(0, me, sibling, src=x_ref)] + [copy(1 + j, me, (*chip, c_), src=x_ref) for j, chip in enumerate(chips)],
            passed=lambda: [copy(4 + j, (*chip, c_), sibling) for j, chip in enumerate(chips)],
            landed=lambda: [copy(1 + j, (*chip, c_), me) for j, chip in enumerate(chips)],
            last=lambda: [copy(0, sibling, me)] + [copy(4 + j, (*chip, 1 - c_), me) for j, chip in enumerate(chips)])

    def start(ins, outs, sems):
        p = parts(ins, outs, sems)
        p["mine"]().start()
        for cp in p["first"]():
            cp.start()

    def mid(ins, outs, sems):
        p = parts(ins, outs, sems)
        for cp, fw in zip(p["landed"](), p["passed"]()):
            cp.wait_recv()
            fw.start()

    def finish(ins, outs, sems):
        p = parts(ins, outs, sems)
        for cp in p["last"]():
            cp.wait_recv()
        for cp in p["first"]() + p["passed"]():
            cp.wait_send()
        p["mine"]().wait()

    return _Task([shard] if into is None else [shard, into], [jax.ShapeDtypeStruct((N_DEV,) + shard.shape, shard.dtype)],
                 [pltpu.SemaphoreType.DMA((7,)), pltpu.SemaphoreType.DMA((7,)), pltpu.SemaphoreType.DMA],
                 start, finish, mid, mid_step, alias=None if into is None else (1, 0))


def _swap_task(g8):
    _, R, C = g8.shape

    def copies(ins, outs, sems):
        (g_ref,), (recv_ref,), (ss, rs) = ins, outs, sems
        x_, y_, c_ = _mesh_pos()
        return [pltpu.make_async_remote_copy(src_ref=g_ref.at[2 * k + (1 - c_)], dst_ref=recv_ref.at[k], send_sem=ss.at[k],
                                             recv_sem=rs.at[k], device_id=(x_, y_, 1 - c_), device_id_type=MESH_ID)
                for k in range(4)]

    def start(ins, outs, sems):
        for cp in copies(ins, outs, sems):
            cp.start()

    def finish(ins, outs, sems):
        cps = copies(ins, outs, sems)
        for cp in cps:
            cp.wait_recv()
        for cp in cps:
            cp.wait_send()

    return _Task([g8], [jax.ShapeDtypeStruct((4, R, C), g8.dtype)],
                 [pltpu.SemaphoreType.DMA((4,)), pltpu.SemaphoreType.DMA((4,))], start, finish)


def _chip_sums(g8, recva):
    _, R, C = g8.shape
    rc = _row_chunk(R)

    def body(g_ref, a_ref, send_ref, own_ref):
        x_, y_, c_ = _mesh_pos()
        chips = [(1 - x_, y_), (x_, 1 - y_), (1 - x_, 1 - y_), (x_, y_)]

        def chunk(i, carry):
            rows = pl.ds(pl.multiple_of(i * rc, rc), rc)
            for j, (tx, ty) in enumerate(chips):
                k = 2 * tx + ty
                s = g_ref[2 * k + c_, rows, :].astype(F32) + a_ref[k, rows, :].astype(F32)
                if j < 3:
                    send_ref[j, rows, :] = s.astype(BF16)
                else:
                    own_ref[rows, :] = s
            return carry

        lax.fori_loop(0, R // rc, chunk, 0)

    return pl.pallas_call(body, name="chip_sums", out_shape=[jax.ShapeDtypeStruct((3, R, C), BF16), jax.ShapeDtypeStruct((R, C), F32)],
                          compiler_params=pltpu.CompilerParams(vmem_limit_bytes=VMEM_LIMIT))(g8, recva)


def _exchange_task(sendb):
    def copies(ins, outs, sems):
        (s_ref,), (recv_ref,), (ss, rs) = ins, outs, sems
        x_, y_, c_ = _mesh_pos()
        flips = [(1 - x_, y_), (x_, 1 - y_), (1 - x_, 1 - y_)]
        return [pltpu.make_async_remote_copy(src_ref=s_ref.at[j], dst_ref=recv_ref.at[j], send_sem=ss.at[j], recv_sem=rs.at[j],
                                             device_id=(tx, ty, c_), device_id_type=MESH_ID) for j, (tx, ty) in enumerate(flips)]

    def start(ins, outs, sems):
        for cp in copies(ins, outs, sems):
            cp.start()

    def finish(ins, outs, sems):
        cps = copies(ins, outs, sems)
        for cp in cps:
            cp.wait_recv()
        for cp in cps:
            cp.wait_send()

    return _Task([sendb], [jax.ShapeDtypeStruct(sendb.shape, sendb.dtype)],
                 [pltpu.SemaphoreType.DMA((3,)), pltpu.SemaphoreType.DMA((3,))], start, finish)


def _silu(v):
    return v * (1.0 / (1.0 + jnp.exp(-v)))


def _ada_fwd(c_slab, w_ada_l, b_ada_l, w_in_shard, n_seq):
    W = c_slab.shape[1]
    D, cols = w_ada_l.shape
    n_rows = N_DEV * n_seq
    t_c, t_w = _gather_task(c_slab, 0), _gather_task(w_in_shard, 0)
    t_m = _gather_task(jax.ShapeDtypeStruct((n_rows, cols), F32), 0)

    def body(c_ref, w_ref, b_ref, ws_ref, slabs_ref, mod_ref, win_ref, c_vm, m_vm, copy_sem, *sems):
        sc, sw, sm = sems[0:3], sems[3:6], sems[6:9]
        t_c.start((c_ref,), (slabs_ref,), sc)
        t_w.start((ws_ref,), (win_ref,), sw)
        t_c.mid((c_ref,), (slabs_ref,), sc)
        t_c.finish((c_ref,), (slabs_ref,), sc)
        cp = pltpu.make_async_copy(slabs_ref, c_vm, copy_sem)
        cp.start()
        cp.wait()
        c_all = c_vm[:, :, 0:D].reshape(N_DEV * 8, D)
        m64 = jnp.dot(_silu(c_all), w_ref[...], precision=lax.Precision.HIGHEST, preferred_element_type=F32) + b_ref[...]
        r = lax.broadcasted_iota(jnp.int32, (n_rows, N_DEV * 8), 0)
        c = lax.broadcasted_iota(jnp.int32, (n_rows, N_DEV * 8), 1)
        pick = jnp.where(c == 8 * (r // n_seq) + r % n_seq, 1.0, 0.0)
        m_vm[...] = jnp.dot(pick, m64, precision=lax.Precision.HIGHEST, preferred_element_type=F32)
        t_m.start((m_vm,), (mod_ref,), sm)
        t_w.mid((ws_ref,), (win_ref,), sw)
        t_m.mid((m_vm,), (mod_ref,), sm)
        t_m.finish((m_vm,), (mod_ref,), sm)
        t_w.finish((ws_ref,), (win_ref,), sw)

    hbm, vm = pl.BlockSpec(memory_space=pl.ANY), pl.BlockSpec(memory_space=pltpu.VMEM)
    return pl.pallas_call(
        body, name="ada_fwd", in_specs=[hbm, vm, vm, hbm], out_specs=[hbm, hbm, hbm],
        out_shape=t_c.out_shapes + t_m.out_shapes + t_w.out_shapes,
        scratch_shapes=[pltpu.VMEM((N_DEV, 8, W), F32), pltpu.VMEM((n_rows, cols), F32), pltpu.SemaphoreType.DMA]
        + t_c.sems + t_w.sems + t_m.sems,
        compiler_params=pltpu.CompilerParams(vmem_limit_bytes=VMEM_LIMIT),
    )(c_slab, w_ada_l, b_ada_l, w_in_shard)


def _ada_bwd(c_all, dmod_cols):
    def body(c_ref, d_ref, o_ref):
        o_ref[...] = lax.dot_general(_silu(c_ref[...]), d_ref[...], (((0,), (0,)), ((), ())),
                                     precision=lax.Precision.HIGHEST, preferred_element_type=F32)
    return pl.pallas_call(body, name="ada_bwd", out_shape=jax.ShapeDtypeStruct((c_all.shape[1], dmod_cols.shape[1]), F32),
                          compiler_params=pltpu.CompilerParams(vmem_limit_bytes=VMEM_LIMIT))(c_all, dmod_cols)


NA_PAIRS = NA_HEADS // 2
N_DR_PAD = 16


def _na_bias_table(na_rpb):
    rev = jnp.pad(jnp.flip(na_rpb, axis=2), ((0, 0), (0, N_DR_PAD - N_DR), (0, GRID_W - N_DC)))
    rev = jnp.transpose(rev.reshape(NA_PAIRS, 2, N_DR_PAD, GRID_W), (0, 2, 1, 3)).reshape(NA_PAIRS, N_DR_PAD, 128)

    def body(r_ref, o_ref):
        k = lax.broadcasted_iota(jnp.int32, (GRID_W, 128), 0)
        lane = lax.broadcasted_iota(jnp.int32, (GRID_W, 128), 1)
        q = lane % GRID_W
        cs = jnp.clip(q - NA_COLS // 2, 0, GRID_W - NA_COLS)
        ok = (k >= cs) & (k < cs + NA_COLS)
        left = lane < GRID_W
        for dr in range(N_DR):
            row = jnp.broadcast_to(r_ref[0, dr:dr + 1, :], (GRID_W, 128))
            r0 = jnp.where(left, row, 0.0)
            r1 = jnp.where(left, pltpu.roll(row, GRID_W, axis=1), 0.0)
            y0 = pltpu.roll(r0, 128 - (NA_COLS - 1), axis=1, stride=1, stride_axis=0)
            y1 = pltpu.roll(r1, GRID_W - (NA_COLS - 1), axis=1, stride=1, stride_axis=0)
            o_ref[0, dr * GRID_W:(dr + 1) * GRID_W, :] = jnp.where(ok, jnp.where(left, y0, y1), NEG)

    return pl.pallas_call(
        body, name="rpb_expand", grid=(NA_PAIRS,),
        in_specs=[pl.BlockSpec((1, N_DR_PAD, 128), lambda p: (p, 0, 0))],
        out_specs=pl.BlockSpec((1, N_DR * GRID_W, 128), lambda p: (p, 0, 0)),
        out_shape=jax.ShapeDtypeStruct((NA_PAIRS, N_DR * GRID_W, 128), F32),
        compiler_params=_params("parallel"),
    )(rev)


def _na_bias_grad(db):
    a = np.arange(128)
    flip = jnp.asarray(((a[:, None] // GRID_W == a[None, :] // GRID_W)
                        & (a[:, None] % GRID_W + a[None, :] % GRID_W == GRID_W - 1)).astype(np.float32))

    def body(d_ref, j_ref, o_ref):
        o_ref[...] = jnp.zeros_like(o_ref)
        for dr in range(N_DR):
            t = jnp.dot(d_ref[0, dr * GRID_W:(dr + 1) * GRID_W, :], j_ref[...], precision=lax.Precision.HIGHEST, preferred_element_type=F32)
            t = pltpu.roll(t, GRID_W + NA_COLS, axis=1, stride=1, stride_axis=0)
            o_ref[0, dr:dr + 1, :] = jnp.sum(t, axis=0, keepdims=True)

    rows = pl.pallas_call(
        body, name="rpb_reduce", grid=(NA_PAIRS,),
        in_specs=[pl.BlockSpec((1, N_DR * GRID_W, 128), lambda p: (p, 0, 0)), _full((128, 128))],
        out_specs=pl.BlockSpec((1, N_DR_PAD, 128), lambda p: (p, 0, 0)),
        out_shape=jax.ShapeDtypeStruct((NA_PAIRS, N_DR_PAD, 128), F32),
        compiler_params=_params("parallel"),
    )(db, flip)
    g = rows.reshape(NA_PAIRS, N_DR_PAD, 2, GRID_W)[:, :N_DR, :, :N_DC]
    return jnp.transpose(g, (0, 2, 1, 3)).reshape(-1)


def _rope_tables(S):
    half = HEAD_DIM // 2
    inv = np.float32(ROPE_THETA) ** (-np.arange(half, dtype=np.float32) / np.float32(half))
    ang = np.arange(S).astype(np.float32)[:, None] * inv[None, :]
    cos, sin = np.cos(ang).astype(np.float32), np.sin(ang).astype(np.float32)
    return jnp.asarray(np.tile(np.concatenate([cos, cos], axis=1), (1, 2))), jnp.asarray(np.tile(np.concatenate([-sin, sin], axis=1), (1, 2)))


def _rope_spec(tps, tm=TOKEN_TILE):
    return pl.BlockSpec((tm, 2 * HEAD_DIM), lambda i: (i % tps, 0))


def _rot_half(t):
    w = t.shape[1]
    lane = lax.broadcasted_iota(jnp.int32, t.shape, 1)
    return jnp.where((lane % HEAD_DIM) < HEAD_DIM // 2, pltpu.roll(t, w - HEAD_DIM // 2, axis=1),
                     pltpu.roll(t, HEAD_DIM // 2, axis=1))


def _tok_spec(w, tm=TOKEN_TILE):
    return pl.BlockSpec((tm, w), lambda i: (i, 0))


def _mod_spec(tps, d):
    return pl.BlockSpec((1, 6, d), lambda i: (i // tps, 0, 0))


def _bstat_spec(tps, w):
    return pl.BlockSpec((1, 8, w), lambda i: (i // tps, 0, 0))


def _attn_in(x2d, mod3, g_attn, w_in, cos_t, sin_t, S, tasks=(), tm=WIDE_TILE):
    T, D = x2d.shape
    tps = S // tm

    def body(x_ref, mod_ref, g_ref, w_ref, cos_ref, sin_ref, h_ref, qkv_ref):
        xn, _ = _rms(x_ref[...])
        h = (xn * g_ref[...]) * (1.0 + mod_ref[0, 1:2, :]) + mod_ref[0, 0:1, :]
        hb = h.astype(BF16)
        h_ref[...] = hb
        proj = _nt(hb, w_ref[...])
        rb = proj[:, ROPE_LO:ROPE_LO + ROPE_WIDTH]
        reps = (1, ROPE_WIDTH // (2 * HEAD_DIM))
        rb = rb * jnp.tile(cos_ref[...], reps) + _rot_half(rb) * jnp.tile(sin_ref[...], reps)
        qkv_ref[:, 0:NA_WIDTH] = (proj[:, 0:NA_WIDTH] * Q_SCALE).astype(BF16)
        qkv_ref[:, NA_WIDTH:ROPE_LO] = proj[:, NA_WIDTH:ROPE_LO].astype(BF16)
        qkv_ref[:, ROPE_LO:ROPE_LO + SW_WIDTH] = (rb[:, 0:SW_WIDTH] * Q_SCALE).astype(BF16)
        qkv_ref[:, ROPE_LO + SW_WIDTH:ROPE_LO + ROPE_WIDTH] = rb[:, SW_WIDTH:].astype(BF16)
        qkv_ref[:, ROPE_LO + ROPE_WIDTH:] = proj[:, ROPE_LO + ROPE_WIDTH:].astype(BF16)

    return _hosted_call(
        body, "attn_in", (T // tm,),
        [_tok_spec(D, tm), _mod_spec(tps, D), _full((1, D)), _full(w_in.shape), _rope_spec(tps, tm), _rope_spec(tps, tm)],
        [_tok_spec(D, tm), _tok_spec(IN_WIDTH, tm)],
        [jax.ShapeDtypeStruct((T, D), BF16), jax.ShapeDtypeStruct((T, IN_WIDTH), BF16)],
        (x2d, mod3, g_attn, w_in, cos_t, sin_t), tasks)


def _attn_out(oa, ob, x2d, mod3, g_na, g_sw, w_out, S, tasks=(), tm=WIDE_TILE):
    T, D = x2d.shape
    tps = S // tm

    def body(oa_ref, ob_ref, x_ref, mod_ref, gna_ref, gsw_ref, w_ref, mixin_ref, mix_ref, x1_ref):
        oan, _ = _rms(oa_ref[...])
        obn, _ = _rms(ob_ref[...])
        mixin = jnp.concatenate([oan * gna_ref[...], obn * gsw_ref[...]], axis=1).astype(BF16)
        mixin_ref[...] = mixin
        mix = _nn(mixin, w_ref[...])
        mix_ref[...] = mix
        x1_ref[...] = x_ref[...] + mod_ref[0, 2:3, :] * mix

    return _hosted_call(
        body, "attn_out", (T // tm,),
        [_tok_spec(NA_WIDTH, tm), _tok_spec(SW_WIDTH, tm), _tok_spec(D, tm), _mod_spec(tps, D),
         _full((1, NA_WIDTH)), _full((1, SW_WIDTH)), _full(w_out.shape)],
        [_tok_spec(NA_WIDTH + SW_WIDTH, tm), _tok_spec(D, tm), _tok_spec(D, tm)],
        [jax.ShapeDtypeStruct((T, NA_WIDTH + SW_WIDTH), BF16), jax.ShapeDtypeStruct((T, D), F32), jax.ShapeDtypeStruct((T, D), F32)],
        (oa, ob, x2d, mod3, g_na, g_sw, w_out), tasks)


def _ffn_up(x1, mod3, g_ffn, w_up, S, tasks=(), tm=WIDE_TILE):
    T, D = x1.shape
    F = w_up.shape[0] // 2
    tps = S // tm

    def body(x1_ref, mod_ref, g_ref, w_ref, h2_ref, val_ref, gt_ref):
        xn, _ = _rms(x1_ref[...])
        h2 = ((xn * g_ref[...]) * (1.0 + mod_ref[0, 4:5, :]) + mod_ref[0, 3:4, :]).astype(BF16)
        h2_ref[...] = h2
        u = _nt(h2, w_ref[...])
        val_ref[...] = u[:, :F].astype(BF16)
        gt_ref[...] = u[:, F:].astype(BF16)

    return _hosted_call(
        body, "ffn_up", (T // tm,), [_tok_spec(D, tm), _mod_spec(tps, D), _full((1, D)), _full(w_up.shape)],
        [_tok_spec(D, tm), _tok_spec(F, tm), _tok_spec(F, tm)],
        [jax.ShapeDtypeStruct((T, D), BF16), jax.ShapeDtypeStruct((T, F), BF16), jax.ShapeDtypeStruct((T, F), BF16)],
        (x1, mod3, g_ffn, w_up), tasks)


def _halo_specs(T, tps, w):
    per = TOKEN_TILE // 8
    prev = pl.BlockSpec((8, w), lambda i: (jnp.maximum(i * per - 1, 0), 0))
    nxt = pl.BlockSpec((8, w), lambda i: (jnp.minimum((i + 1) * per, T // 8 - 1), 0))
    return prev, nxt


def _seq_shifts(cur, before, after, ti, tps):
    tm = cur.shape[0]
    row = lax.broadcasted_iota(jnp.int32, cur.shape, 0)
    before = jnp.where(ti > 0, before.astype(F32), 0.0)
    after = jnp.where(ti < tps - 1, after.astype(F32), 0.0)
    return jnp.where(row == 0, before, pltpu.roll(cur, 1, axis=0)), jnp.where(row == tm - 1, after, pltpu.roll(cur, tm - 1, axis=0))


def _ffn_down(gt, val, conv_w, conv_b, w_down, x1, mod3, g_final, target, B, S):
    T, D = x1.shape
    F = gt.shape[1]
    tps = S // TOKEN_TILE
    prev, nxt = _halo_specs(T, tps, F)

    def body(gt_ref, prev_ref, next_ref, val_ref, cw_ref, cb_ref, w_ref, x1_ref, mod_ref, gf_ref, tgt_ref,
             a_ref, act_ref, vd_ref, dx2_ref, df_ref, gstat_ref, bstat_ref):
        i = pl.program_id(0)
        g = gt_ref[...].astype(F32)
        gprev, gnext = _seq_shifts(g, prev_ref[7:8, :], next_ref[0:1, :], i % tps, tps)
        gc = gprev * cw_ref[0:1, :] + g * cw_ref[1:2, :] + gnext * cw_ref[2:3, :] + cb_ref[...]
        sig = 1.0 / (1.0 + jnp.exp(-gc))
        act = gc * sig
        val = val_ref[...].astype(F32)
        act_ref[...] = act.astype(BF16)
        vd_ref[...] = (val * (sig + act - act * sig)).astype(BF16)
        a = (act * val).astype(BF16)
        a_ref[...] = a
        f = _nn(a, w_ref[...])
        gate = mod_ref[0, 5:6, :]
        x2 = x1_ref[...] + gate * f
        xn, r = _rms(x2)
        err = xn * gf_ref[...] - tgt_ref[...]
        dy = err * (1.0 / D)
        dx2 = _rms_bwd(xn, r, dy * gf_ref[...])
        dx2_ref[...] = dx2
        df_ref[...] = (gate * dx2).astype(BF16)

        @pl.when(i == 0)
        def _():
            gstat_ref[...] = jnp.zeros_like(gstat_ref)

        @pl.when(i % tps == 0)
        def _():
            bstat_ref[...] = jnp.zeros_like(bstat_ref)

        gstat_ref[0:1, :] += jnp.sum(dy * xn, axis=0, keepdims=True)
        tile_loss = jnp.sum(jnp.sum(err * err, axis=1, keepdims=True), axis=0, keepdims=True) * (0.5 / D)
        gstat_ref[1:2, :] += jnp.broadcast_to(tile_loss, (1, D))
        bstat_ref[0, 0:1, :] += jnp.sum(dx2 * f, axis=0, keepdims=True)

    return pl.pallas_call(
        body, name="ffn_down", grid=(T // TOKEN_TILE,),
        in_specs=[_tok_spec(F), prev, nxt, _tok_spec(F), _full(conv_w.shape), _full((1, F)), _full(w_down.shape),
                  _tok_spec(D), _mod_spec(tps, D), _full((1, D)), _tok_spec(D)],
        out_specs=[_tok_spec(F), _tok_spec(F), _tok_spec(F), _tok_spec(D), _tok_spec(D), _full((8, D)), _bstat_spec(tps, D)],
        out_shape=[jax.ShapeDtypeStruct((T, F), BF16), jax.ShapeDtypeStruct((T, F), BF16), jax.ShapeDtypeStruct((T, F), BF16),
                   jax.ShapeDtypeStruct((T, D), F32), jax.ShapeDtypeStruct((T, D), BF16),
                   jax.ShapeDtypeStruct((8, D), F32), jax.ShapeDtypeStruct((B, 8, D), F32)],
        compiler_params=_params("arbitrary"),
    )(gt, gt, gt, val, conv_w, conv_b, w_down, x1, mod3, g_final, target)


def _ffn_down_bwd(df, w_down, act, vd, tasks=(), tm=WIDE_TILE):
    T, D = df.shape
    F = act.shape[1]

    def body(df_ref, w_ref, act_ref, vd_ref, dval_ref, dgc_ref, cstat_ref):
        da = _nt(df_ref[...], w_ref[...])
        dval_ref[...] = (da * act_ref[...].astype(F32)).astype(BF16)
        dgc = da * vd_ref[...].astype(F32)
        dgc_ref[...] = dgc.astype(BF16)

        @pl.when(pl.program_id(0) == 0)
        def _():
            cstat_ref[...] = jnp.zeros_like(cstat_ref)

        cstat_ref[0:1, :] += jnp.sum(dgc, axis=0, keepdims=True)

    return _hosted_call(
        body, "ffn_down_bwd", (T // tm,),
        [_tok_spec(D, tm), _full(w_down.shape), _tok_spec(F, tm), _tok_spec(F, tm)],
        [_tok_spec(F, tm), _tok_spec(F, tm), _full((8, F))],
        [jax.ShapeDtypeStruct((T, F), BF16), jax.ShapeDtypeStruct((T, F), BF16), jax.ShapeDtypeStruct((8, F), F32)],
        (df, w_down, act, vd), tasks)


def _ffn_up_bwd(dgc, dval, gt, conv_w, w_up, x1, mod3, g_ffn, dx2, mix, B, S, tasks=()):
    T, D = x1.shape
    F = dgc.shape[1]
    tps = S // TOKEN_TILE
    prev, nxt = _halo_specs(T, tps, F)

    def body(dgc_ref, prev_ref, next_ref, dval_ref, gt_ref, cw_ref, w_ref, x1_ref, mod_ref, g_ref, dx2_ref, mix_ref,
             du_ref, dx1_ref, dmix_ref, gstat_ref, bstat_ref, cstat_ref):
        i = pl.program_id(0)
        d = dgc_ref[...].astype(F32)
        dprev, dnext = _seq_shifts(d, prev_ref[7:8, :], next_ref[0:1, :], i % tps, tps)
        g = gt_ref[...].astype(F32)

        @pl.when(i == 0)
        def _():
            cstat_ref[...] = jnp.zeros_like(cstat_ref)

        cstat_ref[1:2, :] += jnp.sum(dnext * g, axis=0, keepdims=True)
        cstat_ref[2:3, :] += jnp.sum(d * g, axis=0, keepdims=True)
        cstat_ref[3:4, :] += jnp.sum(dprev * g, axis=0, keepdims=True)
        dgt = dnext * cw_ref[0:1, :] + d * cw_ref[1:2, :] + dprev * cw_ref[2:3, :]
        du = jnp.concatenate([dval_ref[...], dgt.astype(BF16)], axis=1)
        du_ref[...] = du
        dh2 = _nn(du, w_ref[...])
        xn, r = _rms(x1_ref[...])
        scale1 = 1.0 + mod_ref[0, 4:5, :]
        xg = xn * g_ref[...]
        dx1 = dx2_ref[...] + _rms_bwd(xn, r, dh2 * g_ref[...] * scale1)
        dx1_ref[...] = dx1
        dmix_ref[...] = (mod_ref[0, 2:3, :] * dx1).astype(BF16)

        @pl.when(i == 0)
        def _():
            gstat_ref[...] = jnp.zeros_like(gstat_ref)

        @pl.when(i % tps == 0)
        def _():
            bstat_ref[...] = jnp.zeros_like(bstat_ref)

        gstat_ref[0:1, :] += jnp.sum(dh2 * scale1 * xn, axis=0, keepdims=True)
        bstat_ref[0, 0:1, :] += jnp.sum(dh2, axis=0, keepdims=True)
        bstat_ref[0, 1:2, :] += jnp.sum(dh2 * xg, axis=0, keepdims=True)
        bstat_ref[0, 2:3, :] += jnp.sum(dx1 * mix_ref[...], axis=0, keepdims=True)

    return _hosted_call(
        body, "ffn_up_bwd", (T // TOKEN_TILE,),
        [_tok_spec(F), prev, nxt, _tok_spec(F), _tok_spec(F), _full(conv_w.shape), _full(w_up.shape), _tok_spec(D),
         _mod_spec(tps, D), _full((1, D)), _tok_spec(D), _tok_spec(D)],
        [_tok_spec(2 * F), _tok_spec(D), _tok_spec(D), _full((8, D)), _bstat_spec(tps, D), _full((8, F))],
        [jax.ShapeDtypeStruct((T, 2 * F), BF16), jax.ShapeDtypeStruct((T, D), F32), jax.ShapeDtypeStruct((T, D), BF16),
         jax.ShapeDtypeStruct((8, D), F32), jax.ShapeDtypeStruct((B, 8, D), F32), jax.ShapeDtypeStruct((8, F), F32)],
        (dgc, dgc, dgc, dval, gt, conv_w, w_up, x1, mod3, g_ffn, dx2, mix), tasks)


def _attn_out_bwd(dmix, w_out, oa, ob, g_na, g_sw, tasks=(), tm=WIDE_TILE):
    T, D = dmix.shape

    def body(dmix_ref, w_ref, oa_ref, ob_ref, gna_ref, gsw_ref, doa_ref, dob_ref, gstat_ref):
        dmixin = _nt(dmix_ref[...], w_ref[...])

        @pl.when(pl.program_id(0) == 0)
        def _():
            gstat_ref[...] = jnp.zeros_like(gstat_ref)

        for k, (o_ref, g_ref, do_ref) in enumerate(((oa_ref, gna_ref, doa_ref), (ob_ref, gsw_ref, dob_ref))):
            dn = dmixin[:, k * NA_WIDTH:(k + 1) * NA_WIDTH]
            on, r = _rms(o_ref[...])
            gstat_ref[k:k + 1, :] += jnp.sum(dn * on, axis=0, keepdims=True)
            do_ref[...] = _rms_bwd(on, r, dn * g_ref[...]).astype(BF16)

    hs = jax.ShapeDtypeStruct((T, NA_WIDTH), BF16)
    return _hosted_call(
        body, "attn_out_bwd", (T // tm,),
        [_tok_spec(D, tm), _full(w_out.shape), _tok_spec(NA_WIDTH, tm), _tok_spec(SW_WIDTH, tm), _full((1, NA_WIDTH)), _full((1, SW_WIDTH))],
        [_tok_spec(NA_WIDTH, tm), _tok_spec(SW_WIDTH, tm), _full((8, NA_WIDTH))],
        [hs, hs, jax.ShapeDtypeStruct((8, NA_WIDTH), F32)],
        (dmix, w_out, oa, ob, g_na, g_sw), tasks)


def _attn_in_bwd(dqa, dka, dva, dqb, dkb, dvb, cos_t, sin_t, w_in, x2d, mod3, g_attn, dx1, B, S, tm=WIDE_TILE):
    T, D = x2d.shape
    tps = S // tm

    def body(dqa_ref, dka_ref, dva_ref, dqb_ref, dkb_ref, dvb_ref, cos_ref, sin_ref, w_ref, x_ref, mod_ref, g_ref, dx1_ref,
             gx_ref, dproj_ref, gstat_ref, bstat_ref):
        i = pl.program_id(0)
        drb = jnp.concatenate([dqb_ref[...] * Q_SCALE, dkb_ref[...]], axis=1)
        reps = (1, ROPE_WIDTH // (2 * HEAD_DIM))
        drb = drb * jnp.tile(cos_ref[...], reps) + _rot_half(drb * jnp.tile(sin_ref[...], reps))
        dproj = jnp.concatenate([dqa_ref[...] * Q_SCALE, dka_ref[...], dva_ref[...], drb, dvb_ref[...]], axis=1).astype(BF16)
        dproj_ref[...] = dproj
        dh = _nn(dproj, w_ref[...])
        xn, r = _rms(x_ref[...])
        scale1 = 1.0 + mod_ref[0, 1:2, :]
        gx_ref[...] = dx1_ref[...] + _rms_bwd(xn, r, dh * g_ref[...] * scale1)

        @pl.when(i == 0)
        def _():
            gstat_ref[...] = jnp.zeros_like(gstat_ref)

        @pl.when(i % tps == 0)
        def _():
            bstat_ref[...] = jnp.zeros_like(bstat_ref)

        gstat_ref[0:1, :] += jnp.sum(dh * scale1 * xn, axis=0, keepdims=True)
        bstat_ref[0, 0:1, :] += jnp.sum(dh, axis=0, keepdims=True)
        bstat_ref[0, 1:2, :] += jnp.sum(dh * (xn * g_ref[...]), axis=0, keepdims=True)

    rope = _rope_spec(tps, tm)
    return pl.pallas_call(
        body, name="attn_in_bwd", grid=(T // tm,),
        in_specs=[_tok_spec(NA_WIDTH, tm), _tok_spec(NA_WIDTH, tm), _tok_spec(NA_WIDTH, tm), _tok_spec(SW_WIDTH, tm),
                  _tok_spec(SW_KV_WIDTH, tm), _tok_spec(SW_KV_WIDTH, tm), rope, rope, _full(w_in.shape), _tok_spec(D, tm),
                  _mod_spec(tps, D), _full((1, D)), _tok_spec(D, tm)],
        out_specs=[_tok_spec(D, tm), _tok_spec(IN_WIDTH, tm), _full((8, D)), _bstat_spec(tps, D)],
        out_shape=[jax.ShapeDtypeStruct((T, D), F32), jax.ShapeDtypeStruct((T, IN_WIDTH), BF16),
                   jax.ShapeDtypeStruct((8, D), F32), jax.ShapeDtypeStruct((B, 8, D), F32)],
        compiler_params=_params("arbitrary"),
    )(dqa, dka, dva, dqb, dkb, dvb, cos_t, sin_t, w_in, x2d, mod3, g_attn, dx1)


def _matmul_tn(a, b, name, tm=None, tk=512):
    T, M = a.shape
    N = b.shape[1]
    tm = M if tm is None else tm
    nk = T // tk

    def body(a_ref, b_ref, o_ref, acc):
        k = pl.program_id(1)

        @pl.when(k == 0)
        def _():
            acc[...] = jnp.zeros_like(acc)

        acc[...] += _tn(a_ref[...], b_ref[...])

        @pl.when(k == nk - 1)
        def _():
            o_ref[...] = acc[...].astype(BF16)

    return pl.pallas_call(
        body, name=name, grid=(M // tm, nk),
        in_specs=[pl.BlockSpec((tk, tm), lambda i, k: (k, i)), pl.BlockSpec((tk, N), lambda i, k: (k, 0))],
        out_specs=pl.BlockSpec((tm, N), lambda i, k: (i, 0)),
        out_shape=jax.ShapeDtypeStruct((M, N), BF16),
        scratch_shapes=[pltpu.VMEM((tm, N), F32)],
        compiler_params=_params("parallel", "arbitrary"),
    )(a, b)


def _na_geometry(S):
    rows = S // GRID_W
    wr = min(NA_ROWS_MAX, rows)
    return rows, wr


def _na_window(r, rows, wr):
    rs = jnp.clip(r - wr // 2, 0, rows - wr)
    return pl.multiple_of(rs * GRID_W, GRID_W), pl.multiple_of((rs - r + NA_ROWS_MAX - 1) * GRID_W, GRID_W)


NA_STEP_PAIRS = 2
NA_GW = NA_STEP_PAIRS * 128
NA_BWD_ROWS = 4
NA_ROWS_PER_STEP = 4


def _na_specs(S, kw_n, order):
    ng = NA_PAIRS // NA_STEP_PAIRS

    def col(k):
        return pl.BlockSpec((1, S, NA_GW), lambda *ids: (order(*ids)[0], 0, k * ng + order(*ids)[1]))
    bias = pl.BlockSpec((NA_STEP_PAIRS, N_DR * GRID_W, 128), lambda *ids: (order(*ids)[1], 0, 0))
    out = pl.BlockSpec((1, S, NA_GW), lambda *ids: (order(*ids)[0], 0, order(*ids)[1]))
    return col(0), col(1), col(2), bias, out


def _block_diag(t):
    left = lax.broadcasted_iota(jnp.int32, t.shape, 1) < HEAD_DIM
    zero = jnp.zeros_like(t)
    return jnp.concatenate([jnp.where(left, t, zero), jnp.where(left, zero, t)], axis=0)


def _diag_blocks(res):
    left = lax.broadcasted_iota(jnp.int32, (HEAD_DIM, 128), 1) < HEAD_DIM
    return jnp.where(left, res[:HEAD_DIM], res[HEAD_DIM:])


def _col_softmax(st):
    e = jnp.exp(st - jnp.max(st, axis=0, keepdims=True))
    return e * (1.0 / jnp.sum(e, axis=0, keepdims=True))


def _na_fwd(qkv, bias, tasks=()):
    B, S, _ = qkv.shape
    rows, wr = _na_geometry(S)
    kw_n = wr * GRID_W

    def body(q_ref, k_ref, v_ref, b_ref, o_ref):
        def step(it, carry):
            win = [_na_window(it * NA_ROWS_PER_STEP + u, rows, wr) for u in range(NA_ROWS_PER_STEP)]
            qrows = [pl.ds(pl.multiple_of((it * NA_ROWS_PER_STEP + u) * GRID_W, GRID_W), GRID_W) for u in range(NA_ROWS_PER_STEP)]
            krows = [pl.ds(w[0], kw_n) for w in win]
            brows = [pl.ds(w[1], kw_n) for w in win]
            lanes = [pl.ds(p * 128, 128) for p in range(NA_STEP_PAIRS)]
            chains = [(u, p) for u in range(NA_ROWS_PER_STEP) for p in range(NA_STEP_PAIRS)]
            st = {(u, p): _nt(k_ref[0, krows[u], lanes[p]], _block_diag(q_ref[0, qrows[u], lanes[p]])) for u, p in chains}
            pn = {(u, p): _col_softmax(st[(u, p)] + b_ref[p, brows[u], :]).astype(BF16) for u, p in chains}
            out = {(u, p): _diag_blocks(_tn(pn[(u, p)], v_ref[0, krows[u], lanes[p]])) for u, p in chains}
            for u in range(NA_ROWS_PER_STEP):
                o_ref[0, qrows[u], :] = jnp.concatenate([out[(u, p)] for p in range(NA_STEP_PAIRS)], axis=1)
            return carry

        lax.fori_loop(0, rows // NA_ROWS_PER_STEP, step, 0)

    q, k, v, bs, out = _na_specs(S, kw_n, lambda b, g: (b, g))
    return _hosted_call(body, "na_fwd", (B, NA_PAIRS // NA_STEP_PAIRS), [q, k, v, bs], [out],
                        [jax.ShapeDtypeStruct((B, S, NA_WIDTH), F32)], (qkv, qkv, qkv, bias), tasks)


def _na_bwd(qkv, bias, doa, tasks=()):
    B, S, _ = qkv.shape
    rows, wr = _na_geometry(S)
    kw_n = wr * GRID_W

    def body(q_ref, k_ref, v_ref, b_ref, do_ref, dq_ref, dk_ref, dv_ref, db_ref):
        @pl.when(pl.program_id(1) == 0)
        def _():
            db_ref[...] = jnp.zeros_like(db_ref)

        dk_ref[...] = jnp.zeros_like(dk_ref)
        dv_ref[...] = jnp.zeros_like(dv_ref)

        def step(it, carry):
            nu, pairs = range(NA_BWD_ROWS), range(NA_STEP_PAIRS)
            win = [_na_window(it * NA_BWD_ROWS + u, rows, wr) for u in nu]
            qrows = [pl.ds(pl.multiple_of((it * NA_BWD_ROWS + u) * GRID_W, GRID_W), GRID_W) for u in nu]
            krows = [pl.ds(w[0], kw_n) for w in win]
            brows = [pl.ds(w[1], kw_n) for w in win]
            lanes = [pl.ds(p * 128, 128) for p in pairs]
            chains = [(u, p) for u in nu for p in pairs]
            kp = {(u, p): k_ref[0, krows[u], lanes[p]] for u, p in chains}
            qbd = {(u, p): _block_diag(q_ref[0, qrows[u], lanes[p]]) for u, p in chains}
            dobd = {(u, p): _block_diag(do_ref[0, qrows[u], lanes[p]]) for u, p in chains}
            st = {c: _nt(kp[c], qbd[c]) for c in chains}
            dpt = {(u, p): _nt(v_ref[0, krows[u], lanes[p]], dobd[(u, p)]) for u, p in chains}
            pn = {(u, p): _col_softmax(st[(u, p)] + b_ref[p, brows[u], :]) for u, p in chains}
            dst = {c: pn[c] * (dpt[c] - jnp.sum(pn[c] * dpt[c], axis=0, keepdims=True)) for c in chains}
            dsb = {c: dst[c].astype(BF16) for c in chains}
            dq = {c: _diag_blocks(_tn(dsb[c], kp[c])) for c in chains}
            dk = {c: _nn(dsb[c], qbd[c]) for c in chains}
            dv = {c: _nn(pn[c].astype(BF16), dobd[c]) for c in chains}
            for u in nu:
                dq_ref[0, qrows[u], :] = jnp.concatenate([dq[(u, p)] for p in pairs], axis=1)
                dk_ref[0, krows[u], :] += jnp.concatenate([dk[(u, p)] for p in pairs], axis=1)
                dv_ref[0, krows[u], :] += jnp.concatenate([dv[(u, p)] for p in pairs], axis=1)
                for p in pairs:
                    db_ref[p, brows[u], :] += dst[(u, p)]
            return carry

        lax.fori_loop(0, rows // NA_BWD_ROWS, step, 0)

    q, k, v, bs, out = _na_specs(S, kw_n, lambda g, b: (b, g))
    hs = jax.ShapeDtypeStruct((B, S, NA_WIDTH), F32)
    return _hosted_call(body, "na_bwd", (NA_PAIRS // NA_STEP_PAIRS, B), [q, k, v, bs, out], [out, out, out, bs],
                        [hs, hs, hs, jax.ShapeDtypeStruct((NA_PAIRS, N_DR * GRID_W, 128), F32)], (qkv, qkv, qkv, bias, doa), tasks)


SW_PAIRS = SW_HEADS // 2


def _sw_band(n, S):
    kw_n = 3 * SW_BLOCK
    start = pl.multiple_of(jnp.clip(n * SW_BLOCK - SW_BLOCK, 0, S - kw_n), SW_BLOCK)
    kpos = start + lax.broadcasted_iota(jnp.int32, (kw_n, SW_BLOCK), 0)
    qpos = n * SW_BLOCK + lax.broadcasted_iota(jnp.int32, (kw_n, SW_BLOCK), 1)
    return start, jnp.abs(qpos - kpos) <= SW_WINDOW


def _kv_halves(t):
    left = lax.broadcasted_iota(jnp.int32, t.shape, 1) < HEAD_DIM
    swapped = pltpu.roll(t, HEAD_DIM, axis=1)
    zero = jnp.zeros_like(t)
    return {(0, 0): jnp.where(left, t, zero), (0, 1): jnp.where(left, zero, swapped),
            (1, 0): jnp.where(left, swapped, zero), (1, 1): jnp.where(left, zero, t)}


def _sw_probs(st, ok, sk):
    st = jnp.where(ok, st, NEG)
    m = jnp.maximum(jnp.max(st, axis=0, keepdims=True), sk)
    e = jnp.exp(st - m)
    esk = jnp.exp(sk - m)
    inv = 1.0 / (jnp.sum(e, axis=0, keepdims=True) + esk)
    return e * inv, esk * inv


def _sw_specs(S):
    q = pl.BlockSpec((1, S, SW_WIDTH), lambda b, *_: (b, 0, ROPE_LO // SW_WIDTH))
    k = pl.BlockSpec((1, S, SW_KV_WIDTH), lambda b, *_: (b, 0, (ROPE_LO + SW_WIDTH) // SW_KV_WIDTH))
    v = pl.BlockSpec((1, S, SW_KV_WIDTH), lambda b, *_: (b, 0, (ROPE_LO + ROPE_WIDTH) // SW_KV_WIDTH))
    return q, k, v


SW_FWD_SPLIT = 2


def _sw_fwd(sink, qkv, tasks=()):
    B, S, _ = qkv.shape
    kw_n = 3 * SW_BLOCK

    def body(sink_ref, q_ref, k_ref, v_ref, o_ref):
        def step(n, carry):
            start, ok = _sw_band(n, S)
            qrows = pl.ds(pl.multiple_of(n * SW_BLOCK, SW_BLOCK), SW_BLOCK)
            krows = pl.ds(start, kw_n)
            kh, vh = _kv_halves(k_ref[0, krows, :]), _kv_halves(v_ref[0, krows, :])
            heads = [(p, e) for p in range(SW_PAIRS) for e in range(2)]
            qp = [q_ref[0, qrows, pl.ds(p * 128, 128)] for p in range(SW_PAIRS)]
            kv_of = lambda p: p // (SW_PAIRS // SW_KV_HEADS)
            st = {(p, e): _nt(kh[(kv_of(p), e)], qp[p]) for p, e in heads}
            pn = {(p, e): _sw_probs(st[(p, e)], ok, sink_ref[2 * p + e])[0].astype(BF16) for p, e in heads}
            outs = [_tn(pn[(p, 0)], vh[(kv_of(p), 0)]) + _tn(pn[(p, 1)], vh[(kv_of(p), 1)]) for p in range(SW_PAIRS)]
            o_ref[0, qrows, :] = jnp.concatenate(outs, axis=1)
            return carry

        half = (S // SW_BLOCK) // SW_FWD_SPLIT
        lax.fori_loop(pl.program_id(1) * half, (pl.program_id(1) + 1) * half, step, 0)

    q, k, v = _sw_specs(S)
    return _hosted_call(
        body, "sw_fwd", (B, SW_FWD_SPLIT), [pl.BlockSpec(memory_space=pltpu.SMEM), q, k, v],
        [pl.BlockSpec((1, S, SW_WIDTH), lambda b, s: (b, 0, 0))], [jax.ShapeDtypeStruct((B, S, SW_WIDTH), F32)],
        (sink, qkv, qkv, qkv), tasks)


def _sw_bwd(sink, qkv, dob):
    B, S, _ = qkv.shape
    kw_n = 3 * SW_BLOCK

    fold_rows = 256

    def body(sink_ref, q_ref, k_ref, v_ref, do_ref, dq_ref, dk_ref, dv_ref, dsink_ref, dk_acc, dv_acc):
        @pl.when(pl.program_id(0) == 0)
        def _():
            dsink_ref[...] = jnp.zeros_like(dsink_ref)

        dk_acc[...] = jnp.zeros_like(dk_acc)
        dv_acc[...] = jnp.zeros_like(dv_acc)
        ppk = SW_PAIRS // SW_KV_HEADS

        def step(n, carry):
            start, ok = _sw_band(n, S)
            qrows = pl.ds(pl.multiple_of(n * SW_BLOCK, SW_BLOCK), SW_BLOCK)
            krows = pl.ds(start, kw_n)
            kh, vh = _kv_halves(k_ref[0, krows, :]), _kv_halves(v_ref[0, krows, :])
            heads = [(p, e) for p in range(SW_PAIRS) for e in range(2)]
            qp = [q_ref[0, qrows, pl.ds(p * 128, 128)] for p in range(SW_PAIRS)]
            dop = [do_ref[0, qrows, pl.ds(p * 128, 128)] for p in range(SW_PAIRS)]
            st = {(p, e): _nt(kh[(p // ppk, e)], qp[p]) for p, e in heads}
            dpt = {(p, e): _nt(vh[(p // ppk, e)], dop[p]) for p, e in heads}
            pnb, dsb = {}, {}
            for p, e in heads:
                pn, psink = _sw_probs(st[(p, e)], ok, sink_ref[2 * p + e])
                delta = jnp.sum(pn * dpt[(p, e)], axis=0, keepdims=True)
                dsb[(p, e)] = (pn * (dpt[(p, e)] - delta)).astype(BF16)
                pnb[(p, e)] = pn.astype(BF16)
                dsink_ref[2 * p + e:2 * p + e + 1, :] += -(psink * delta)
            dq_ref[0, qrows, :] = jnp.concatenate(
                [_tn(dsb[(p, 0)], kh[(p // ppk, 0)]) + _tn(dsb[(p, 1)], kh[(p // ppk, 1)]) for p in range(SW_PAIRS)], axis=1)
            left = lax.broadcasted_iota(jnp.int32, (kw_n, 128), 1) < HEAD_DIM
            dks, dvs = [], []
            for kv in range(SW_KV_HEADS):
                dk = dv = None
                for p in range(kv * ppk, (kv + 1) * ppk):
                    dk_p = jnp.where(left, _nn(dsb[(p, 0)], qp[p]), _nn(dsb[(p, 1)], qp[p]))
                    dv_p = jnp.where(left, _nn(pnb[(p, 0)], dop[p]), _nn(pnb[(p, 1)], dop[p]))
                    dk = dk_p if dk is None else dk + dk_p
                    dv = dv_p if dv is None else dv + dv_p
                dks.append(dk)
                dvs.append(dv)
            dk_acc[krows, :] += jnp.concatenate(dks, axis=1)
            dv_acc[krows, :] += jnp.concatenate(dvs, axis=1)
            return carry

        lax.fori_loop(0, S // SW_BLOCK, step, 0)

        def fold(i, carry):
            rows = pl.ds(pl.multiple_of(i * fold_rows, fold_rows), fold_rows)
            left = lax.broadcasted_iota(jnp.int32, (fold_rows, 128), 1) < HEAD_DIM
            for acc, out_ref in ((dk_acc, dk_ref), (dv_acc, dv_ref)):
                a, b = acc[rows, 0:128], acc[rows, 128:256]
                out_ref[0, rows, :] = jnp.where(left, a + pltpu.roll(a, HEAD_DIM, axis=1), b + pltpu.roll(b, HEAD_DIM, axis=1))
            return carry

        lax.fori_loop(0, S // fold_rows, fold, 0)

        @pl.when(pl.program_id(0) == B - 1)
        def _():
            dsink_ref[...] = jnp.broadcast_to(jnp.sum(dsink_ref[...], axis=1, keepdims=True), dsink_ref.shape)

    q, k, v = _sw_specs(S)
    qo = pl.BlockSpec((1, S, SW_WIDTH), lambda b: (b, 0, 0))
    ko = pl.BlockSpec((1, S, SW_KV_WIDTH), lambda b: (b, 0, 0))
    return pl.pallas_call(
        body, name="sw_bwd", grid=(B,),
        in_specs=[pl.BlockSpec(memory_space=pltpu.SMEM), q, k, v, qo],
        out_specs=[qo, ko, ko, _full((SW_HEADS, 128))],
        out_shape=[jax.ShapeDtypeStruct((B, S, SW_WIDTH), F32), jax.ShapeDtypeStruct((B, S, SW_KV_WIDTH), F32),
                   jax.ShapeDtypeStruct((B, S, SW_KV_WIDTH), F32), jax.ShapeDtypeStruct((SW_HEADS, 128), F32)],
        scratch_shapes=[pltpu.VMEM((S, 2 * SW_KV_WIDTH), F32), pltpu.VMEM((S, 2 * SW_KV_WIDTH), F32)],
        compiler_params=_params("arbitrary"),
    )(sink, qkv, qkv, qkv, dob)


def _pack_sum_adamw(packs, params):
    W = packs.shape[1]
    n_p = len(params)

    def body(p_ref, *refs):
        ins, tot_ref, outs = refs[:3 * n_p], refs[3 * n_p], refs[3 * n_p + 1:]
        tot = p_ref[0:8, :]
        for d in range(1, N_DEV):
            tot = tot + p_ref[8 * d:8 * d + 8, :]
        tot_ref[...] = tot
        for i, (w, _, _, rows, off) in enumerate(params):
            n = w.shape[1]
            g = tot[rows[0]:rows[0] + 1, off:off + n]
            for r in rows[1:]:
                g = g + tot[r:r + 1, off:off + n]
            w_ref, m_ref, v_ref = ins[3 * i:3 * i + 3]
            g_ref, d_ref, nm_ref, nv_ref = outs[4 * i:4 * i + 4]
            g_ref[...] = g
            d_ref[...], nm_ref[...], nv_ref[...] = _adam_update(w_ref[...], g, m_ref[...], v_ref[...])

    res = pl.pallas_call(
        body, name="small_adamw",
        out_shape=[jax.ShapeDtypeStruct((8, W), F32)] + [jax.ShapeDtypeStruct(p[0].shape, F32) for p in params for _ in range(4)],
        compiler_params=pltpu.CompilerParams(vmem_limit_bytes=VMEM_LIMIT),
    )(packs, *[a for p in params for a in p[:3]])
    return res[0], [res[1 + 4 * i:5 + 4 * i] for i in range(n_p)]


def _adam_update(w, g, m, v):
    c1 = 1.0 - ADAM_B1 ** ADAM_STEP
    c2 = 1.0 - ADAM_B2 ** ADAM_STEP
    nm = ADAM_B1 * m + (1.0 - ADAM_B1) * g
    nv = ADAM_B2 * v + (1.0 - ADAM_B2) * (g * g)
    return -ADAM_LR * ((nm / c1) / (jnp.sqrt(nv / c2) + ADAM_EPS) + ADAM_WD * w), nm, nv


def _adamw(w, g, m, v, name):
    def body(w_ref, g_ref, m_ref, v_ref, d_ref, nm_ref, nv_ref):
        d_ref[...], nm_ref[...], nv_ref[...] = _adam_update(w_ref[...], g_ref[...], m_ref[...], v_ref[...])

    s = jax.ShapeDtypeStruct(w.shape, F32)
    return pl.pallas_call(body, name=name, out_shape=[s, s, s],
                          compiler_params=pltpu.CompilerParams(vmem_limit_bytes=VMEM_LIMIT))(w, g, m, v)


def _sum_adamw_rows(R):
    return max(r for r in range(16, min(R, 256) + 1, 16) if R % r == 0)


def _sum_adamw_steps(R):
    return R // _sum_adamw_rows(R)


def _sum_adamw(own, recvb, w, m, v, name, tasks=()):
    R, C = own.shape
    rc = _sum_adamw_rows(R)

    def body(own_ref, r_ref, w_ref, m_ref, v_ref, g_ref, d_ref, nm_ref, nv_ref):
        g = own_ref[...]
        for j in range(3):
            g = g + r_ref[j].astype(F32)
        g_ref[...] = g
        d_ref[...], nm_ref[...], nv_ref[...] = _adam_update(w_ref[...], g, m_ref[...], v_ref[...])

    blk = pl.BlockSpec((rc, C), lambda i: (i, 0))
    s = jax.ShapeDtypeStruct((R, C), F32)
    return _hosted_call(body, name, (R // rc,), [blk, pl.BlockSpec((3, rc, C), lambda i: (0, i, 0)), blk, blk, blk],
                        [blk, blk, blk, blk], [s, s, s, s], (own, recvb, w, m, v), tasks)


def _by_device(dw):
    return dw.reshape(N_DEV, dw.shape[0] // N_DEV, dw.shape[1])


def _local_step(x, mod, g_attn, w_in, bias, sw_sink, g_na_out, g_sw_out, w_out, g_ffn, w_up, conv_w, conv_b, w_down,
                g_final, target, sharded):
    B, S, D = x.shape
    T = B * S
    x2d = x.reshape(T, D)
    mod3 = mod.reshape(B, 6, D)
    cos_t, sin_t = _rope_tables(S)
    sink = sw_sink.reshape(SW_HEADS)
    n_tiles = T // WIDE_TILE
    full = lambda g: g.reshape(N_DEV * g.shape[1], g.shape[2])

    rider = lambda w, mid, lo, n, into=None: [_gather_task(w, mid, rows=(lo, n), into=into)] if sharded else []
    if sharded:
        qu, hd = w_up.shape[0] // 4, w_down.shape[0] // 2
    (h, qkv), got = _attn_in(x2d, mod3, g_attn, w_in, cos_t, sin_t, S,
                             [_gather_task(w_out, n_tiles // 2)] + rider(w_up, n_tiles - 1, 0, qu) if sharded else [])
    if sharded:
        w_out, w_up_buf = full(got[0][0]), got[1][0]
    qkv3 = qkv.reshape(B, S, IN_WIDTH)
    na_steps, sw_steps = B * (NA_PAIRS // NA_STEP_PAIRS), B * SW_FWD_SPLIT
    (oa,), got = _na_fwd(qkv3, bias, rider(w_up, na_steps - 1, qu, 2 * qu, w_up_buf) if sharded else [])
    if sharded:
        w_up_buf = got[0][0]
    oa = oa.reshape(T, NA_WIDTH)
    (ob,), got = _sw_fwd(sink, qkv3, rider(w_up, sw_steps // 2, 3 * qu, qu, w_up_buf) if sharded else [])
    if sharded:
        w_up = full(got[0][0])
    ob = ob.reshape(T, SW_WIDTH)
    (mixin, mix, x1), _ = _attn_out(oa, ob, x2d, mod3, g_na_out, g_sw_out, w_out, S)
    (h2, val, gt), got = _ffn_up(x1, mod3, g_ffn, w_up, S, rider(w_down, 3 * n_tiles // 4, 0, 2 * hd) if sharded else [])
    if sharded:
        w_down = full(got[0][0])
    a, act, vd, dx2, df, gstat_f, bstat_f = _ffn_down(gt, val, conv_w, conv_b, w_down, x1, mod3, g_final, target.reshape(T, D), B, S)
    F = val.shape[1]

    dw_down = _matmul_tn(a, df, "dw_down")
    (dval, dgc, cstat), got = _ffn_down_bwd(df, w_down, act, vd, [_swap_task(_by_device(dw_down))] if sharded else [])
    if sharded:
        send_down, own_down = _chip_sums(_by_device(dw_down), got[0][0])
    (du, dx1, dmix, gstat_u, bstat_u, cstat_w), got = _ffn_up_bwd(dgc, dval, gt, conv_w, w_up, x1, mod3, g_ffn, dx2, mix, B, S,
                                                                  [_exchange_task(send_down)] if sharded else [])
    if sharded:
        dw_down = (own_down, got[0][0])
    dw_up = _matmul_tn(du, h2, "dw_up", tm=F)
    dw_out = _matmul_tn(mixin, dmix, "dw_out")
    (doa, dob, gstat_o), got = _attn_out_bwd(dmix, w_out, oa, ob, g_na_out, g_sw_out,
                                             [_swap_task(_by_device(dw_up)), _swap_task(_by_device(dw_out))] if sharded else [])
    if sharded:
        send_up, own_up = _chip_sums(_by_device(dw_up), got[0][0])
        send_out, own_out = _chip_sums(_by_device(dw_out), got[1][0])
    (dqa, dka, dva, dbt), got = _na_bwd(qkv3, bias, doa.reshape(B, S, NA_WIDTH),
                                        [_exchange_task(send_up), _exchange_task(send_out)] if sharded else [])
    if sharded:
        dw_up, dw_out = (own_up, got[0][0]), (own_out, got[1][0])
    dqb, dkb, dvb, dsink = _sw_bwd(sink, qkv3, dob.reshape(B, S, SW_WIDTH))
    r2 = lambda t: t.reshape(T, t.shape[-1])
    grad_x, dproj, gstat_i, bstat_i = _attn_in_bwd(r2(dqa), r2(dka), r2(dva), r2(dqb), r2(dkb), r2(dvb), cos_t, sin_t, w_in, x2d, mod3,
                                                   g_attn, dx1, B, S)
    dw_in = _matmul_tn(dproj, h, "dw_in")

    dmod = jnp.stack([bstat_i[:, 0], bstat_i[:, 1], bstat_u[:, 2], bstat_u[:, 0], bstat_u[:, 1], bstat_f[:, 0]], axis=1)
    small = dict(g_attn=gstat_i[0], g_ffn=gstat_u[0], g_final=gstat_f[0], loss=gstat_f[1, 0], g_na_out=gstat_o[0], g_sw_out=gstat_o[1],
                 sw_sink=dsink[:, 0], conv_b=cstat[0], conv_w=cstat_w[1:4], dbt=dbt,
                 raw=(bstat_i, bstat_u, bstat_f, gstat_i, gstat_u, gstat_f, gstat_o, dsink, cstat, cstat_w))
    return grad_x.reshape(B, S, D), dict(w_in=dw_in, w_out=dw_out, w_up=dw_up, w_down=dw_down), dmod, small


def _pack_slab(raw, drpb):
    D, F = raw[3].shape[1], raw[8].shape[1]
    n_seq = raw[0].shape[0]

    def body(bi_ref, bu_ref, bf_ref, gi_ref, gu_ref, gf_ref, go_ref, ds_ref, cs_ref, cw_ref, rp_ref, o_ref):
        o_ref[...] = jnp.zeros_like(o_ref)
        for b in range(n_seq):
            mods = (bi_ref[b, 0:1, :], bi_ref[b, 1:2, :], bu_ref[b, 2:3, :], bu_ref[b, 0:1, :], bu_ref[b, 1:2, :], bf_ref[b, 0:1, :])
            for k, row in enumerate(mods):
                o_ref[b:b + 1, k * D:(k + 1) * D] = row
        o = 0
        for row in (gi_ref[0:1, :], gu_ref[0:1, :], gf_ref[0:1, :], go_ref[0:1, :], go_ref[1:2, :]):
            o_ref[2:3, o:o + row.shape[1]] = row
            o += row.shape[1]
        ds = ds_ref[...]
        eye = lax.broadcasted_iota(jnp.int32, ds.shape, 0) == lax.broadcasted_iota(jnp.int32, ds.shape, 1)
        o_ref[2:3, o:o + 128] = jnp.sum(jnp.where(eye, ds, 0.0), axis=0, keepdims=True)
        o_ref[2:3, o + 128:o + 256] = gf_ref[1:2, 0:128]
        o_ref[3:4, 0:F] = cs_ref[0:1, :]
        o_ref[4:5, 0:rp_ref.shape[1]] = rp_ref[...]
        o_ref[5:8, 0:F] = cw_ref[1:4, :]

    return pl.pallas_call(body, name="pack_slab", out_shape=jax.ShapeDtypeStruct((8, PACK_W), F32),
                          compiler_params=pltpu.CompilerParams(vmem_limit_bytes=VMEM_LIMIT))(*raw, drpb)


def kernel(x, c, w_ada, b_ada, g_attn, w_in, na_rpb, sw_sink, g_na_out, g_sw_out, w_out, g_ffn, w_up, conv_w, conv_b, w_down, g_final, loss_target, m_w_ada, m_b_ada, m_g_attn, m_w_in, m_na_rpb, m_sw_sink, m_g_na_out, m_g_sw_out, m_w_out, m_g_ffn, m_w_up, m_conv_w, m_conv_b, m_w_down, m_g_final, v_w_ada, v_b_ada, v_g_attn, v_w_in, v_na_rpb, v_sw_sink, v_g_na_out, v_g_sw_out, v_w_out, v_g_ffn, v_w_up, v_conv_w, v_conv_b, v_w_down, v_g_final):
    B, S, D = x.shape
    me = 4 * lax.axis_index("x") + 2 * lax.axis_index("y") + lax.axis_index("c")
    ada_c = w_ada.shape[2]
    F_l = conv_w.shape[2]

    cw_l = jnp.pad(conv_w[0], ((0, 8 - conv_w.shape[1]), (0, 0)))
    c_l = jnp.pad(c, ((0, 8 - B), (0, 0)))
    tr = {"w_in", "w_up"}
    w_in_t = jnp.transpose(w_in[0])
    shards = dict(w_out=w_out[0].astype(BF16), w_up=jnp.transpose(w_up[0]).astype(BF16), w_down=w_down[0].astype(BF16))

    b_ada_l = lax.dynamic_slice(b_ada, (0, me * ada_c), (1, ada_c))
    slabs, mod_all, w_in_all = _ada_fwd(jnp.concatenate([c_l, cw_l], axis=1), w_ada[0], b_ada_l, w_in_t.astype(BF16), B)
    c_all = slabs[:, :, :D].reshape(N_DEV * 8, D)
    conv_w_f = jnp.transpose(slabs[:, :3, D:], (1, 0, 2)).reshape(3, N_DEV * F_l)
    mod_mine = lax.dynamic_slice(mod_all, (0, me * B, 0), (N_DEV, B, ada_c))
    mod = jnp.transpose(mod_mine, (1, 0, 2)).reshape(B, N_DEV * ada_c)
    w_in_f = w_in_all.reshape(N_DEV * w_in_t.shape[0], D)

    bias = _na_bias_table(na_rpb[0])

    grad_x, dw, dmod, small = _local_step(x, mod, g_attn, w_in_f, bias, sw_sink, g_na_out, g_sw_out, shards["w_out"], g_ffn,
                                          shards["w_up"], conv_w_f, conv_b, shards["w_down"], g_final.reshape(1, D), loss_target,
                                          sharded=True)
    drpb = _na_bias_grad(small["dbt"])

    slab = _pack_slab(small["raw"], drpb.reshape(1, -1))
    weights = dict(w_ada=w_ada, b_ada=b_ada, g_attn=g_attn, w_in=w_in, na_rpb=na_rpb, sw_sink=sw_sink, g_na_out=g_na_out,
                   g_sw_out=g_sw_out, w_out=w_out, g_ffn=g_ffn, w_up=w_up, conv_w=conv_w, conv_b=conv_b, w_down=w_down, g_final=g_final)
    ms = dict(w_ada=m_w_ada, b_ada=m_b_ada, g_attn=m_g_attn, w_in=m_w_in, na_rpb=m_na_rpb, sw_sink=m_sw_sink, g_na_out=m_g_na_out,
              g_sw_out=m_g_sw_out, w_out=m_w_out, g_ffn=m_g_ffn, w_up=m_w_up, conv_w=m_conv_w, conv_b=m_conv_b, w_down=m_w_down, g_final=m_g_final)
    vs = dict(w_ada=v_w_ada, b_ada=v_b_ada, g_attn=v_g_attn, w_in=v_w_in, na_rpb=v_na_rpb, sw_sink=v_sw_sink, g_na_out=v_g_na_out,
              g_sw_out=v_g_sw_out, w_out=v_w_out, g_ffn=v_g_ffn, w_up=v_w_up, conv_w=v_conv_w, conv_b=v_conv_b, w_down=v_w_down, g_final=v_g_final)
    names = list(weights)
    grads, deltas, new_m, new_v = {}, {}, {}, {}
    flat = lambda t: t.reshape(1, -1)

    def shard2d(nm):
        if nm in tr:
            return (lambda t: jnp.transpose(t[0])), (lambda t: jnp.transpose(t)[None])
        return (lambda t: t[0]), (lambda t: t[None])

    def finish_sum(nm, own, recvb, tasks=()):
        r, back = shard2d(nm)
        (g2, d_, m_, v_), got = _sum_adamw(own, recvb, r(weights[nm]), r(ms[nm]), r(vs[nm]), "adamw_" + nm, tasks)
        grads[nm], deltas[nm], new_m[nm], new_v[nm] = back(g2), back(d_), back(m_), back(v_)
        return got

    g8_in = _by_device(dw["w_in"])
    got = finish_sum("w_down", *dw["w_down"], [_swap_task(g8_in)])
    send_in, own_in = _chip_sums(g8_in, got[0][0])
    n_up = _sum_adamw_steps(dw["w_up"][0].shape[0])
    got = finish_sum("w_up", *dw["w_up"], [_exchange_task(send_in), _gather_task(slab, n_up - 1)])
    packs = got[1][0]
    finish_sum("w_in", own_in, got[0][0])
    finish_sum("w_out", *dw["w_out"])

    where = dict(b_ada=((0, 1), 0), g_attn=((2,), 0), g_ffn=((2,), D), g_final=((2,), 2 * D), g_na_out=((2,), 3 * D),
                 g_sw_out=((2,), 3 * D + NA_WIDTH), sw_sink=((2,), 3 * D + NA_WIDTH + SW_WIDTH), conv_b=((3,), 0), na_rpb=((4,), 0))
    tot, small_out = _pack_sum_adamw(packs.reshape(N_DEV * 8, PACK_W),
                                     [(flat(weights[n]), flat(ms[n]), flat(vs[n])) + where[n] for n in where])
    for n, (g_, d_, m_, v_) in zip(where, small_out):
        shp = weights[n].shape
        grads[n], deltas[n], new_m[n], new_v[n] = g_.reshape(shp), d_.reshape(shp), m_.reshape(shp), v_.reshape(shp)
    loss = tot[2, 3 * D + NA_WIDTH + SW_WIDTH + 128]

    dmod_cols = lax.dynamic_slice(packs.reshape(N_DEV * 8, PACK_W), (0, me * ada_c), (N_DEV * 8, ada_c))
    for nm, g2 in (("w_ada", _ada_bwd(c_all, dmod_cols)), ("conv_w", lax.dynamic_slice(tot[5:8], (0, me * F_l), (3, F_l)))):
        r, back = shard2d(nm)
        d_, m_, v_ = _adamw(r(weights[nm]), g2, r(ms[nm]), r(vs[nm]), "adamw_" + nm)
        grads[nm], deltas[nm], new_m[nm], new_v[nm] = back(g2), back(d_), back(m_), back(v_)
    return (loss, grad_x, *[grads[n] for n in names], *[deltas[n] for n in names], *[new_m[n] for n in names],
            *[new_v[n] for n in names])
```

```python
import functools

import numpy as np
import jax
import jax.numpy as jnp
from jax import lax
from jax.experimental import pallas as pl
from jax.experimental.pallas import tpu as pltpu

F32, BF16 = jnp.float32, jnp.bfloat16
MESH_ID = pl.DeviceIdType.MESH
N_DEV = 8

HEAD_DIM = 64
NA_HEADS = 8
SW_HEADS = 8
SW_KV_HEADS = 2
SW_GROUP = SW_HEADS // SW_KV_HEADS
NA_WIDTH = NA_HEADS * HEAD_DIM
SW_WIDTH = SW_HEADS * HEAD_DIM
SW_KV_WIDTH = SW_KV_HEADS * HEAD_DIM
ROPE_WIDTH = SW_WIDTH + SW_KV_WIDTH
IN_WIDTH = 3 * NA_WIDTH + SW_WIDTH + 2 * SW_KV_WIDTH
ROPE_LO = 3 * NA_WIDTH
GRID_W = 64
NA_ROWS_MAX = 8
NA_COLS = 16
N_DR = 2 * NA_ROWS_MAX - 1
N_DC = 2 * NA_COLS - 1
SW_WINDOW = 128
SW_BLOCK = 128
ROPE_THETA = 10000.0
EPS = 1e-6
NEG = -1e30
Q_SCALE = HEAD_DIM ** -0.5

ADAM_LR = 0.001
ADAM_B1 = 0.9
ADAM_B2 = 0.999
ADAM_EPS = 1e-08
ADAM_WD = 0.01
ADAM_STEP = 10

TOKEN_TILE = 256
WIDE_TILE = 512
VMEM_LIMIT = 56 * 1024 * 1024

PACK_W = 6144


def _nn(a, b):
    return jnp.dot(a, b, preferred_element_type=F32)


def _nt(a, b):
    return lax.dot_general(a, b, (((1,), (1,)), ((), ())), preferred_element_type=F32)


def _tn(a, b):
    return lax.dot_general(a, b, (((0,), (0,)), ((), ())), preferred_element_type=F32)


def _rms(x):
    r = lax.rsqrt(jnp.mean(x * x, axis=-1, keepdims=True) + EPS)
    return x * r, r


def _rms_bwd(xn, r, gy):
    return r * (gy - xn * jnp.mean(xn * gy, axis=-1, keepdims=True))


def _params(*sem):
    return pltpu.CompilerParams(dimension_semantics=sem, vmem_limit_bytes=VMEM_LIMIT)


def _full(shape):
    n = len(shape)
    return pl.BlockSpec(shape, lambda *_: (0,) * n)


def _mesh_pos():
    return lax.axis_index("x"), lax.axis_index("y"), lax.axis_index("c")


def _row_chunk(r):
    for rc in (128, 64, 32, 16):
        if r % rc == 0:
            return rc
    raise ValueError(f"rows {r} not a multiple of 16")


class _Task:
    def __init__(self, inputs, out_shapes, sems, start, finish, mid=None, mid_step=None, alias=None):
        self.inputs, self.out_shapes, self.sems = list(inputs), list(out_shapes), list(sems)
        self.start, self.finish, self.mid, self.mid_step = start, finish, mid, mid_step
        self.alias = alias


def _hosted_call(body, name, grid, in_specs, out_specs, out_shape, operands, tasks, scratch_shapes=()):
    n_in, n_out, n_scr = len(in_specs), len(out_specs), len(scratch_shapes)
    t_in = [len(t.inputs) for t in tasks]
    t_out = [len(t.out_shapes) for t in tasks]
    t_sem = [len(t.sems) for t in tasks]
    n_steps = int(np.prod(grid))

    def wrapped(*refs):
        ins, rest = refs[:n_in], refs[n_in:]
        task_ins, rest = rest[:sum(t_in)], rest[sum(t_in):]
        outs, rest = rest[:n_out], rest[n_out:]
        task_outs, rest = rest[:sum(t_out)], rest[sum(t_out):]
        scr, task_sems = rest[:n_scr], rest[n_scr:]
        step = pl.program_id(0)
        for ax in range(1, len(grid)):
            step = step * grid[ax] + pl.program_id(ax)
        parts = []
        oi = oo = os_ = 0
        for t, a, b, c in zip(tasks, t_in, t_out, t_sem):
            parts.append((t, task_ins[oi:oi + a], task_outs[oo:oo + b], task_sems[os_:os_ + c]))
            oi, oo, os_ = oi + a, oo + b, os_ + c
        for t, ti, to, ts in parts:
            pl.when(step == 0)(functools.partial(t.start, ti, to, ts))
            if t.mid is not None:
                pl.when(step == t.mid_step)(functools.partial(t.mid, ti, to, ts))
        body(*ins, *outs, *scr)
        for t, ti, to, ts in parts:
            pl.when(step == n_steps - 1)(functools.partial(t.finish, ti, to, ts))

    hbm = pl.BlockSpec(memory_space=pl.ANY)
    aliases, oi, oo = {}, n_in, n_out
    for t, a, b in zip(tasks, t_in, t_out):
        if t.alias is not None:
            aliases[oi + t.alias[0]] = oo + t.alias[1]
        oi, oo = oi + a, oo + b
    res = pl.pallas_call(
        wrapped, name=name, grid=grid,
        in_specs=list(in_specs) + [hbm] * sum(t_in),
        out_specs=list(out_specs) + [hbm] * sum(t_out),
        out_shape=list(out_shape) + [s for t in tasks for s in t.out_shapes],
        scratch_shapes=list(scratch_shapes) + [s for t in tasks for s in t.sems],
        input_output_aliases=aliases,
        compiler_params=_params(*(["arbitrary"] * len(grid))),
    )(*operands, *[a for t in tasks for a in t.inputs])
    own, extra = res[:n_out], res[n_out:]
    per_task, o = [], 0
    for b in t_out:
        per_task.append(extra[o:o + b])
        o += b
    return own, per_task


def _gather_task(shard, mid_step, rows=None, into=None):
    lo, n = (0, shard.shape[0]) if rows is None else rows

    def parts(ins, outs, sems):
        x_ref, out_ref, (send_sems, recv_sems, local_sem) = ins[0], outs[0], sems
        x_, y_, c_ = _mesh_pos()
        me, sibling = (x_, y_, c_), (x_, y_, 1 - c_)
        chips = [(1 - x_, y_), (x_, 1 - y_), (1 - x_, 1 - y_)]
        x_ref = x_ref.at[pl.ds(lo, n)]

        def rows(px, py, pc):
            return out_ref.at[4 * px + 2 * py + pc, pl.ds(lo, n)]

        def copy(k, block, to, src=None):
            return pltpu.make_async_remote_copy(
                src_ref=rows(*block) if src is None else src, dst_ref=rows(*block),
                send_sem=send_sems.at[k], recv_sem=recv_sems.at[k], device_id=to, device_id_type=MESH_ID)

        return dict(
            mine=lambda: pltpu.make_async_copy(x_ref, rows(*me), local_sem),
            first=lambda: [copy(0, me, sibling, src=x_ref)] + [copy(1 + j, me, (*chip, c_), src=x_ref) for j, chip in enumerate(chips)],
            passed=lambda: [copy(4 + j, (*chip, c_), sibling) for j, chip in enumerate(chips)],
            landed=lambda: [copy(1 + j, (*chip, c_), me) for j, chip in enumerate(chips)],
            last=lambda: [copy(0, sibling, me)] + [copy(4 + j, (*chip, 1 - c_), me) for j, chip in enumerate(chips)])

    def start(ins, outs, sems):
        p = parts(ins, outs, sems)
        p["mine"]().start()
        for cp in p["first"]():
            cp.start()

    def mid(ins, outs, sems):
        p = parts(ins, outs, sems)
        for cp, fw in zip(p["landed"](), p["passed"]()):
            cp.wait_recv()
            fw.start()

    def finish(ins, outs, sems):
        p = parts(ins, outs, sems)
        for cp in p["last"]():
            cp.wait_recv()
        for cp in p["first"]() + p["passed"]():
            cp.wait_send()
        p["mine"]().wait()

    return _Task([shard] if into is None else [shard, into], [jax.ShapeDtypeStruct((N_DEV,) + shard.shape, shard.dtype)],
                 [pltpu.SemaphoreType.DMA((7,)), pltpu.SemaphoreType.DMA((7,)), pltpu.SemaphoreType.DMA],
                 start, finish, mid, mid_step, alias=None if into is None else (1, 0))


def _swap_task(g8):
    _, R, C = g8.shape

    def copies(ins, outs, sems):
        (g_ref,), (recv_ref,), (ss, rs) = ins, outs, sems
        x_, y_, c_ = _mesh_pos()
        return [pltpu.make_async_remote_copy(src_ref=g_ref.at[2 * k + (1 - c_)], dst_ref=recv_ref.at[k], send_sem=ss.at[k],
                                             recv_sem=rs.at[k], device_id=(x_, y_, 1 - c_), device_id_type=MESH_ID)
                for k in range(4)]

    def start(ins, outs, sems):
        for cp in copies(ins, outs, sems):
            cp.start()

    def finish(ins, outs, sems):
        cps = copies(ins, outs, sems)
        for cp in cps:
            cp.wait_recv()
        for cp in cps:
            cp.wait_send()

    return _Task([g8], [jax.ShapeDtypeStruct((4, R, C), g8.dtype)],
                 [pltpu.SemaphoreType.DMA((4,)), pltpu.SemaphoreType.DMA((4,))], start, finish)


def _chip_sums(g8, recva):
    _, R, C = g8.shape
    rc = _row_chunk(R)

    def body(g_ref, a_ref, send_ref, own_ref):
        x_, y_, c_ = _mesh_pos()
        chips = [(1 - x_, y_), (x_, 1 - y_), (1 - x_, 1 - y_), (x_, y_)]

        def chunk(i, carry):
            rows = pl.ds(pl.multiple_of(i * rc, rc), rc)
            for j, (tx, ty) in enumerate(chips):
                k = 2 * tx + ty
                s = g_ref[2 * k + c_, rows, :].astype(F32) + a_ref[k, rows, :].astype(F32)
                if j < 3:
                    send_ref[j, rows, :] = s.astype(BF16)
                else:
                    own_ref[rows, :] = s
            return carry

        lax.fori_loop(0, R // rc, chunk, 0)

    return pl.pallas_call(body, name="chip_sums", out_shape=[jax.ShapeDtypeStruct((3, R, C), BF16), jax.ShapeDtypeStruct((R, C), F32)],
                          compiler_params=pltpu.CompilerParams(vmem_limit_bytes=VMEM_LIMIT))(g8, recva)


def _exchange_task(sendb):
    def copies(ins, outs, sems):
        (s_ref,), (recv_ref,), (ss, rs) = ins, outs, sems
        x_, y_, c_ = _mesh_pos()
        flips = [(1 - x_, y_), (x_, 1 - y_), (1 - x_, 1 - y_)]
        return [pltpu.make_async_remote_copy(src_ref=s_ref.at[j], dst_ref=recv_ref.at[j], send_sem=ss.at[j], recv_sem=rs.at[j],
                                             device_id=(tx, ty, c_), device_id_type=MESH_ID) for j, (tx, ty) in enumerate(flips)]

    def start(ins, outs, sems):
        for cp in copies(ins, outs, sems):
            cp.start()

    def finish(ins, outs, sems):
        cps = copies(ins, outs, sems)
        for cp in cps:
            cp.wait_recv()
        for cp in cps:
            cp.wait_send()

    return _Task([sendb], [jax.ShapeDtypeStruct(sendb.shape, sendb.dtype)],
                 [pltpu.SemaphoreType.DMA((3,)), pltpu.SemaphoreType.DMA((3,))], start, finish)


def _silu(v):
    return v * (1.0 / (1.0 + jnp.exp(-v)))


def _ada_fwd(c_slab, w_ada_l, b_ada_l, w_in_shard, n_seq):
    W = c_slab.shape[1]
    D, cols = w_ada_l.shape
    n_rows = N_DEV * n_seq
    t_c, t_w = _gather_task(c_slab, 0), _gather_task(w_in_shard, 0)
    t_m = _gather_task(jax.ShapeDtypeStruct((n_rows, cols), F32), 0)

    def body(c_ref, w_ref, b_ref, ws_ref, slabs_ref, mod_ref, win_ref, c_vm, m_vm, copy_sem, *sems):
        sc, sw, sm = sems[0:3], sems[3:6], sems[6:9]
        t_c.start((c_ref,), (slabs_ref,), sc)
        t_w.start((ws_ref,), (win_ref,), sw)
        t_c.mid((c_ref,), (slabs_ref,), sc)
        t_c.finish((c_ref,), (slabs_ref,), sc)
        cp = pltpu.make_async_copy(slabs_ref, c_vm, copy_sem)
        cp.start()
        cp.wait()
        c_all = c_vm[:, :, 0:D].reshape(N_DEV * 8, D)
        m64 = jnp.dot(_silu(c_all), w_ref[...], precision=lax.Precision.HIGHEST, preferred_element_type=F32) + b_ref[...]
        r = lax.broadcasted_iota(jnp.int32, (n_rows, N_DEV * 8), 0)
        c = lax.broadcasted_iota(jnp.int32, (n_rows, N_DEV * 8), 1)
        pick = jnp.where(c == 8 * (r // n_seq) + r % n_seq, 1.0, 0.0)
        m_vm[...] = jnp.dot(pick, m64, precision=lax.Precision.HIGHEST, preferred_element_type=F32)
        t_m.start((m_vm,), (mod_ref,), sm)
        t_w.mid((ws_ref,), (win_ref,), sw)
        t_m.mid((m_vm,), (mod_ref,), sm)
        t_m.finish((m_vm,), (mod_ref,), sm)
        t_w.finish((ws_ref,), (win_ref,), sw)

    hbm, vm = pl.BlockSpec(memory_space=pl.ANY), pl.BlockSpec(memory_space=pltpu.VMEM)
    return pl.pallas_call(
        body, name="ada_fwd", in_specs=[hbm, vm, vm, hbm], out_specs=[hbm, hbm, hbm],
        out_shape=t_c.out_shapes + t_m.out_shapes + t_w.out_shapes,
        scratch_shapes=[pltpu.VMEM((N_DEV, 8, W), F32), pltpu.VMEM((n_rows, cols), F32), pltpu.SemaphoreType.DMA]
        + t_c.sems + t_w.sems + t_m.sems,
        compiler_params=pltpu.CompilerParams(vmem_limit_bytes=VMEM_LIMIT),
    )(c_slab, w_ada_l, b_ada_l, w_in_shard)


def _ada_bwd(c_all, dmod_cols):
    def body(c_ref, d_ref, o_ref):
        o_ref[...] = lax.dot_general(_silu(c_ref[...]), d_ref[...], (((0,), (0,)), ((), ())),
                                     precision=lax.Precision.HIGHEST, preferred_element_type=F32)
    return pl.pallas_call(body, name="ada_bwd", out_shape=jax.ShapeDtypeStruct((c_all.shape[1], dmod_cols.shape[1]), F32),
                          compiler_params=pltpu.CompilerParams(vmem_limit_bytes=VMEM_LIMIT))(c_all, dmod_cols)


NA_PAIRS = NA_HEADS // 2
N_DR_PAD = 16


def _na_bias_table(na_rpb):
    rev = jnp.pad(jnp.flip(na_rpb, axis=2), ((0, 0), (0, N_DR_PAD - N_DR), (0, GRID_W - N_DC)))
    rev = jnp.transpose(rev.reshape(NA_PAIRS, 2, N_DR_PAD, GRID_W), (0, 2, 1, 3)).reshape(NA_PAIRS, N_DR_PAD, 128)

    def body(r_ref, o_ref):
        k = lax.broadcasted_iota(jnp.int32, (GRID_W, 128), 0)
        lane = lax.broadcasted_iota(jnp.int32, (GRID_W, 128), 1)
        q = lane % GRID_W
        cs = jnp.clip(q - NA_COLS // 2, 0, GRID_W - NA_COLS)
        ok = (k >= cs) & (k < cs + NA_COLS)
        left = lane < GRID_W
        for dr in range(N_DR):
            row = jnp.broadcast_to(r_ref[0, dr:dr + 1, :], (GRID_W, 128))
            r0 = jnp.where(left, row, 0.0)
            r1 = jnp.where(left, pltpu.roll(row, GRID_W, axis=1), 0.0)
            y0 = pltpu.roll(r0, 128 - (NA_COLS - 1), axis=1, stride=1, stride_axis=0)
            y1 = pltpu.roll(r1, GRID_W - (NA_COLS - 1), axis=1, stride=1, stride_axis=0)
            o_ref[0, dr * GRID_W:(dr + 1) * GRID_W, :] = jnp.where(ok, jnp.where(left, y0, y1), NEG)

    return pl.pallas_call(
        body, name="rpb_expand", grid=(NA_PAIRS,),
        in_specs=[pl.BlockSpec((1, N_DR_PAD, 128), lambda p: (p, 0, 0))],
        out_specs=pl.BlockSpec((1, N_DR * GRID_W, 128), lambda p: (p, 0, 0)),
        out_shape=jax.ShapeDtypeStruct((NA_PAIRS, N_DR * GRID_W, 128), F32),
        compiler_params=_params("parallel"),
    )(rev)


def _na_bias_grad(db):
    a = np.arange(128)
    flip = jnp.asarray(((a[:, None] // GRID_W == a[None, :] // GRID_W)
                        & (a[:, None] % GRID_W + a[None, :] % GRID_W == GRID_W - 1)).astype(np.float32))

    def body(d_ref, j_ref, o_ref):
        o_ref[...] = jnp.zeros_like(o_ref)
        for dr in range(N_DR):
            t = jnp.dot(d_ref[0, dr * GRID_W:(dr + 1) * GRID_W, :], j_ref[...], precision=lax.Precision.HIGHEST, preferred_element_type=F32)
            t = pltpu.roll(t, GRID_W + NA_COLS, axis=1, stride=1, stride_axis=0)
            o_ref[0, dr:dr + 1, :] = jnp.sum(t, axis=0, keepdims=True)

    rows = pl.pallas_call(
        body, name="rpb_reduce", grid=(NA_PAIRS,),
        in_specs=[pl.BlockSpec((1, N_DR * GRID_W, 128), lambda p: (p, 0, 0)), _full((128, 128))],
        out_specs=pl.BlockSpec((1, N_DR_PAD, 128), lambda p: (p, 0, 0)),
        out_shape=jax.ShapeDtypeStruct((NA_PAIRS, N_DR_PAD, 128), F32),
        compiler_params=_params("parallel"),
    )(db, flip)
    g = rows.reshape(NA_PAIRS, N_DR_PAD, 2, GRID_W)[:, :N_DR, :, :N_DC]
    return jnp.transpose(g, (0, 2, 1, 3)).reshape(-1)


def _rope_tables(S):
    half = HEAD_DIM // 2
    inv = np.float32(ROPE_THETA) ** (-np.arange(half, dtype=np.float32) / np.float32(half))
    ang = np.arange(S).astype(np.float32)[:, None] * inv[None, :]
    cos, sin = np.cos(ang).astype(np.float32), np.sin(ang).astype(np.float32)
    return jnp.asarray(np.tile(np.concatenate([cos, cos], axis=1), (1, 2))), jnp.asarray(np.tile(np.concatenate([-sin, sin], axis=1), (1, 2)))


def _rope_spec(tps, tm=TOKEN_TILE):
    return pl.BlockSpec((tm, 2 * HEAD_DIM), lambda i: (i % tps, 0))


def _rot_half(t):
    w = t.shape[1]
    lane = lax.broadcasted_iota(jnp.int32, t.shape, 1)
    return jnp.where((lane % HEAD_DIM) < HEAD_DIM // 2, pltpu.roll(t, w - HEAD_DIM // 2, axis=1),
                     pltpu.roll(t, HEAD_DIM // 2, axis=1))


def _tok_spec(w, tm=TOKEN_TILE):
    return pl.BlockSpec((tm, w), lambda i: (i, 0))


def _mod_spec(tps, d):
    return pl.BlockSpec((1, 6, d), lambda i: (i // tps, 0, 0))


def _bstat_spec(tps, w):
    return pl.BlockSpec((1, 8, w), lambda i: (i // tps, 0, 0))


def _attn_in(x2d, mod3, g_attn, w_in, cos_t, sin_t, S, tasks=(), tm=WIDE_TILE):
    T, D = x2d.shape
    tps = S // tm

    def body(x_ref, mod_ref, g_ref, w_ref, cos_ref, sin_ref, h_ref, qkv_ref):
        xn, _ = _rms(x_ref[...])
        h = (xn * g_ref[...]) * (1.0 + mod_ref[0, 1:2, :]) + mod_ref[0, 0:1, :]
        hb = h.astype(BF16)
        h_ref[...] = hb
        proj = _nt(hb, w_ref[...])
        rb = proj[:, ROPE_LO:ROPE_LO + ROPE_WIDTH]
        reps = (1, ROPE_WIDTH // (2 * HEAD_DIM))
        rb = rb * jnp.tile(cos_ref[...], reps) + _rot_half(rb) * jnp.tile(sin_ref[...], reps)
        qkv_ref[:, 0:NA_WIDTH] = (proj[:, 0:NA_WIDTH] * Q_SCALE).astype(BF16)
        qkv_ref[:, NA_WIDTH:ROPE_LO] = proj[:, NA_WIDTH:ROPE_LO].astype(BF16)
        qkv_ref[:, ROPE_LO:ROPE_LO + SW_WIDTH] = (rb[:, 0:SW_WIDTH] * Q_SCALE).astype(BF16)
        qkv_ref[:, ROPE_LO + SW_WIDTH:ROPE_LO + ROPE_WIDTH] = rb[:, SW_WIDTH:].astype(BF16)
        qkv_ref[:, ROPE_LO + ROPE_WIDTH:] = proj[:, ROPE_LO + ROPE_WIDTH:].astype(BF16)

    return _hosted_call(
        body, "attn_in", (T // tm,),
        [_tok_spec(D, tm), _mod_spec(tps, D), _full((1, D)), _full(w_in.shape), _rope_spec(tps, tm), _rope_spec(tps, tm)],
        [_tok_spec(D, tm), _tok_spec(IN_WIDTH, tm)],
        [jax.ShapeDtypeStruct((T, D), BF16), jax.ShapeDtypeStruct((T, IN_WIDTH), BF16)],
        (x2d, mod3, g_attn, w_in, cos_t, sin_t), tasks)


def _attn_out(oa, ob, x2d, mod3, g_na, g_sw, w_out, S, tasks=(), tm=WIDE_TILE):
    T, D = x2d.shape
    tps = S // tm

    def body(oa_ref, ob_ref, x_ref, mod_ref, gna_ref, gsw_ref, w_ref, mixin_ref, mix_ref, x1_ref):
        oan, _ = _rms(oa_ref[...])
        obn, _ = _rms(ob_ref[...])
        mixin = jnp.concatenate([oan * gna_ref[...], obn * gsw_ref[...]], axis=1).astype(BF16)
        mixin_ref[...] = mixin
        mix = _nn(mixin, w_ref[...])
        mix_ref[...] = mix
        x1_ref[...] = x_ref[...] + mod_ref[0, 2:3, :] * mix

    return _hosted_call(
        body, "attn_out", (T // tm,),
        [_tok_spec(NA_WIDTH, tm), _tok_spec(SW_WIDTH, tm), _tok_spec(D, tm), _mod_spec(tps, D),
         _full((1, NA_WIDTH)), _full((1, SW_WIDTH)), _full(w_out.shape)],
        [_tok_spec(NA_WIDTH + SW_WIDTH, tm), _tok_spec(D, tm), _tok_spec(D, tm)],
        [jax.ShapeDtypeStruct((T, NA_WIDTH + SW_WIDTH), BF16), jax.ShapeDtypeStruct((T, D), F32), jax.ShapeDtypeStruct((T, D), F32)],
        (oa, ob, x2d, mod3, g_na, g_sw, w_out), tasks)


def _ffn_up(x1, mod3, g_ffn, w_up, S, tasks=(), tm=WIDE_TILE):
    T, D = x1.shape
    F = w_up.shape[0] // 2
    tps = S // tm

    def body(x1_ref, mod_ref, g_ref, w_ref, h2_ref, val_ref, gt_ref):
        xn, _ = _rms(x1_ref[...])
        h2 = ((xn * g_ref[...]) * (1.0 + mod_ref[0, 4:5, :]) + mod_ref[0, 3:4, :]).astype(BF16)
        h2_ref[...] = h2
        u = _nt(h2, w_ref[...])
        val_ref[...] = u[:, :F].astype(BF16)
        gt_ref[...] = u[:, F:].astype(BF16)

    return _hosted_call(
        body, "ffn_up", (T // tm,), [_tok_spec(D, tm), _mod_spec(tps, D), _full((1, D)), _full(w_up.shape)],
        [_tok_spec(D, tm), _tok_spec(F, tm), _tok_spec(F, tm)],
        [jax.ShapeDtypeStruct((T, D), BF16), jax.ShapeDtypeStruct((T, F), BF16), jax.ShapeDtypeStruct((T, F), BF16)],
        (x1, mod3, g_ffn, w_up), tasks)


def _halo_specs(T, tps, w):
    per = TOKEN_TILE // 8
    prev = pl.BlockSpec((8, w), lambda i: (jnp.maximum(i * per - 1, 0), 0))
    nxt = pl.BlockSpec((8, w), lambda i: (jnp.minimum((i + 1) * per, T // 8 - 1), 0))
    return prev, nxt


def _seq_shifts(cur, before, after, ti, tps):
    tm = cur.shape[0]
    row = lax.broadcasted_iota(jnp.int32, cur.shape, 0)
    before = jnp.where(ti > 0, before.astype(F32), 0.0)
    after = jnp.where(ti < tps - 1, after.astype(F32), 0.0)
    return jnp.where(row == 0, before, pltpu.roll(cur, 1, axis=0)), jnp.where(row == tm - 1, after, pltpu.roll(cur, tm - 1, axis=0))


def _ffn_down(gt, val, conv_w, conv_b, w_down, x1, mod3, g_final, target, B, S):
    T, D = x1.shape
    F = gt.shape[1]
    tps = S // TOKEN_TILE
    prev, nxt = _halo_specs(T, tps, F)

    def body(gt_ref, prev_ref, next_ref, val_ref, cw_ref, cb_ref, w_ref, x1_ref, mod_ref, gf_ref, tgt_ref,
             a_ref, act_ref, vd_ref, dx2_ref, df_ref, gstat_ref, bstat_ref):
        i = pl.program_id(0)
        g = gt_ref[...].astype(F32)
        gprev, gnext = _seq_shifts(g, prev_ref[7:8, :], next_ref[0:1, :], i % tps, tps)
        gc = gprev * cw_ref[0:1, :] + g * cw_ref[1:2, :] + gnext * cw_ref[2:3, :] + cb_ref[...]
        sig = 1.0 / (1.0 + jnp.exp(-gc))
        act = gc * sig
        val = val_ref[...].astype(F32)
        act_ref[...] = act.astype(BF16)
        vd_ref[...] = (val * (sig + act - act * sig)).astype(BF16)
        a = (act * val).astype(BF16)
        a_ref[...] = a
        f = _nn(a, w_ref[...])
        gate = mod_ref[0, 5:6, :]
        x2 = x1_ref[...] + gate * f
        xn, r = _rms(x2)
        err = xn * gf_ref[...] - tgt_ref[...]
        dy = err * (1.0 / D)
        dx2 = _rms_bwd(xn, r, dy * gf_ref[...])
        dx2_ref[...] = dx2
        df_ref[...] = (gate * dx2).astype(BF16)

        @pl.when(i == 0)
        def _():
            gstat_ref[...] = jnp.zeros_like(gstat_ref)

        @pl.when(i % tps == 0)
        def _():
            bstat_ref[...] = jnp.zeros_like(bstat_ref)

        gstat_ref[0:1, :] += jnp.sum(dy * xn, axis=0, keepdims=True)
        tile_loss = jnp.sum(jnp.sum(err * err, axis=1, keepdims=True), axis=0, keepdims=True) * (0.5 / D)
        gstat_ref[1:2, :] += jnp.broadcast_to(tile_loss, (1, D))
        bstat_ref[0, 0:1, :] += jnp.sum(dx2 * f, axis=0, keepdims=True)

    return pl.pallas_call(
        body, name="ffn_down", grid=(T // TOKEN_TILE,),
        in_specs=[_tok_spec(F), prev, nxt, _tok_spec(F), _full(conv_w.shape), _full((1, F)), _full(w_down.shape),
                  _tok_spec(D), _mod_spec(tps, D), _full((1, D)), _tok_spec(D)],
        out_specs=[_tok_spec(F), _tok_spec(F), _tok_spec(F), _tok_spec(D), _tok_spec(D), _full((8, D)), _bstat_spec(tps, D)],
        out_shape=[jax.ShapeDtypeStruct((T, F), BF16), jax.ShapeDtypeStruct((T, F), BF16), jax.ShapeDtypeStruct((T, F), BF16),
                   jax.ShapeDtypeStruct((T, D), F32), jax.ShapeDtypeStruct((T, D), BF16),
                   jax.ShapeDtypeStruct((8, D), F32), jax.ShapeDtypeStruct((B, 8, D), F32)],
        compiler_params=_params("arbitrary"),
    )(gt, gt, gt, val, conv_w, conv_b, w_down, x1, mod3, g_final, target)


def _ffn_down_bwd(df, w_down, act, vd, tasks=(), tm=WIDE_TILE):
    T, D = df.shape
    F = act.shape[1]

    def body(df_ref, w_ref, act_ref, vd_ref, dval_ref, dgc_ref, cstat_ref):
        da = _nt(df_ref[...], w_ref[...])
        dval_ref[...] = (da * act_ref[...].astype(F32)).astype(BF16)
        dgc = da * vd_ref[...].astype(F32)
        dgc_ref[...] = dgc.astype(BF16)

        @pl.when(pl.program_id(0) == 0)
        def _():
            cstat_ref[...] = jnp.zeros_like(cstat_ref)

        cstat_ref[0:1, :] += jnp.sum(dgc, axis=0, keepdims=True)

    return _hosted_call(
        body, "ffn_down_bwd", (T // tm,),
        [_tok_spec(D, tm), _full(w_down.shape), _tok_spec(F, tm), _tok_spec(F, tm)],
        [_tok_spec(F, tm), _tok_spec(F, tm), _full((8, F))],
        [jax.ShapeDtypeStruct((T, F), BF16), jax.ShapeDtypeStruct((T, F), BF16), jax.ShapeDtypeStruct((8, F), F32)],
        (df, w_down, act, vd), tasks)


def _ffn_up_bwd(dgc, dval, gt, conv_w, w_up, x1, mod3, g_ffn, dx2, mix, B, S, tasks=()):
    T, D = x1.shape
    F = dgc.shape[1]
    tps = S // TOKEN_TILE
    prev, nxt = _halo_specs(T, tps, F)

    def body(dgc_ref, prev_ref, next_ref, dval_ref, gt_ref, cw_ref, w_ref, x1_ref, mod_ref, g_ref, dx2_ref, mix_ref,
             du_ref, dx1_ref, dmix_ref, gstat_ref, bstat_ref, cstat_ref):
        i = pl.program_id(0)
        d = dgc_ref[...].astype(F32)
        dprev, dnext = _seq_shifts(d, prev_ref[7:8, :], next_ref[0:1, :], i % tps, tps)
        g = gt_ref[...].astype(F32)

        @pl.when(i == 0)
        def _():
            cstat_ref[...] = jnp.zeros_like(cstat_ref)

        cstat_ref[1:2, :] += jnp.sum(dnext * g, axis=0, keepdims=True)
        cstat_ref[2:3, :] += jnp.sum(d * g, axis=0, keepdims=True)
        cstat_ref[3:4, :] += jnp.sum(dprev * g, axis=0, keepdims=True)
        dgt = dnext * cw_ref[0:1, :] + d * cw_ref[1:2, :] + dprev * cw_ref[2:3, :]
        du = jnp.concatenate([dval_ref[...], dgt.astype(BF16)], axis=1)
        du_ref[...] = du
        dh2 = _nn(du, w_ref[...])
        xn, r = _rms(x1_ref[...])
        scale1 = 1.0 + mod_ref[0, 4:5, :]
        xg = xn * g_ref[...]
        dx1 = dx2_ref[...] + _rms_bwd(xn, r, dh2 * g_ref[...] * scale1)
        dx1_ref[...] = dx1
        dmix_ref[...] = (mod_ref[0, 2:3, :] * dx1).astype(BF16)

        @pl.when(i == 0)
        def _():
            gstat_ref[...] = jnp.zeros_like(gstat_ref)

        @pl.when(i % tps == 0)
        def _():
            bstat_ref[...] = jnp.zeros_like(bstat_ref)

        gstat_ref[0:1, :] += jnp.sum(dh2 * scale1 * xn, axis=0, keepdims=True)
        bstat_ref[0, 0:1, :] += jnp.sum(dh2, axis=0, keepdims=True)
        bstat_ref[0, 1:2, :] += jnp.sum(dh2 * xg, axis=0, keepdims=True)
        bstat_ref[0, 2:3, :] += jnp.sum(dx1 * mix_ref[...], axis=0, keepdims=True)

    return _hosted_call(
        body, "ffn_up_bwd", (T // TOKEN_TILE,),
        [_tok_spec(F), prev, nxt, _tok_spec(F), _tok_spec(F), _full(conv_w.shape), _full(w_up.shape), _tok_spec(D),
         _mod_spec(tps, D), _full((1, D)), _tok_spec(D), _tok_spec(D)],
        [_tok_spec(2 * F), _tok_spec(D), _tok_spec(D), _full((8, D)), _bstat_spec(tps, D), _full((8, F))],
        [jax.ShapeDtypeStruct((T, 2 * F), BF16), jax.ShapeDtypeStruct((T, D), F32), jax.ShapeDtypeStruct((T, D), BF16),
         jax.ShapeDtypeStruct((8, D), F32), jax.ShapeDtypeStruct((B, 8, D), F32), jax.ShapeDtypeStruct((8, F), F32)],
        (dgc, dgc, dgc, dval, gt, conv_w, w_up, x1, mod3, g_ffn, dx2, mix), tasks)


def _attn_out_bwd(dmix, w_out, oa, ob, g_na, g_sw, tasks=(), tm=WIDE_TILE):
    T, D = dmix.shape

    def body(dmix_ref, w_ref, oa_ref, ob_ref, gna_ref, gsw_ref, doa_ref, dob_ref, gstat_ref):
        dmixin = _nt(dmix_ref[...], w_ref[...])

        @pl.when(pl.program_id(0) == 0)
        def _():
            gstat_ref[...] = jnp.zeros_like(gstat_ref)

        for k, (o_ref, g_ref, do_ref) in enumerate(((oa_ref, gna_ref, doa_ref), (ob_ref, gsw_ref, dob_ref))):
            dn = dmixin[:, k * NA_WIDTH:(k + 1) * NA_WIDTH]
            on, r = _rms(o_ref[...])
            gstat_ref[k:k + 1, :] += jnp.sum(dn * on, axis=0, keepdims=True)
            do_ref[...] = _rms_bwd(on, r, dn * g_ref[...]).astype(BF16)

    hs = jax.ShapeDtypeStruct((T, NA_WIDTH), BF16)
    return _hosted_call(
        body, "attn_out_bwd", (T // tm,),
        [_tok_spec(D, tm), _full(w_out.shape), _tok_spec(NA_WIDTH, tm), _tok_spec(SW_WIDTH, tm), _full((1, NA_WIDTH)), _full((1, SW_WIDTH))],
        [_tok_spec(NA_WIDTH, tm), _tok_spec(SW_WIDTH, tm), _full((8, NA_WIDTH))],
        [hs, hs, jax.ShapeDtypeStruct((8, NA_WIDTH), F32)],
        (dmix, w_out, oa, ob, g_na, g_sw), tasks)


def _attn_in_bwd(dqa, dka, dva, dqb, dkb, dvb, cos_t, sin_t, w_in, x2d, mod3, g_attn, dx1, B, S, tm=WIDE_TILE):
    T, D = x2d.shape
    tps = S // tm

    def body(dqa_ref, dka_ref, dva_ref, dqb_ref, dkb_ref, dvb_ref, cos_ref, sin_ref, w_ref, x_ref, mod_ref, g_ref, dx1_ref,
             gx_ref, dproj_ref, gstat_ref, bstat_ref):
        i = pl.program_id(0)
        drb = jnp.concatenate([dqb_ref[...] * Q_SCALE, dkb_ref[...]], axis=1)
        reps = (1, ROPE_WIDTH // (2 * HEAD_DIM))
        drb = drb * jnp.tile(cos_ref[...], reps) + _rot_half(drb * jnp.tile(sin_ref[...], reps))
        dproj = jnp.concatenate([(dqa_ref[...] * Q_SCALE).astype(BF16), dka_ref[...].astype(BF16), dva_ref[...].astype(BF16),
                                 drb.astype(BF16), dvb_ref[...].astype(BF16)], axis=1)
        dproj_ref[...] = dproj
        dh = _nn(dproj, w_ref[...])
        xn, r = _rms(x_ref[...])
        scale1 = 1.0 + mod_ref[0, 1:2, :]
        gx_ref[...] = dx1_ref[...] + _rms_bwd(xn, r, dh * g_ref[...] * scale1)

        @pl.when(i == 0)
        def _():
            gstat_ref[...] = jnp.zeros_like(gstat_ref)

        @pl.when(i % tps == 0)
        def _():
            bstat_ref[...] = jnp.zeros_like(bstat_ref)

        gstat_ref[0:1, :] += jnp.sum(dh * scale1 * xn, axis=0, keepdims=True)
        bstat_ref[0, 0:1, :] += jnp.sum(dh, axis=0, keepdims=True)
        bstat_ref[0, 1:2, :] += jnp.sum(dh * (xn * g_ref[...]), axis=0, keepdims=True)

    rope = _rope_spec(tps, tm)
    return pl.pallas_call(
        body, name="attn_in_bwd", grid=(T // tm,),
        in_specs=[_tok_spec(NA_WIDTH, tm), _tok_spec(NA_WIDTH, tm), _tok_spec(NA_WIDTH, tm), _tok_spec(SW_WIDTH, tm),
                  _tok_spec(SW_KV_WIDTH, tm), _tok_spec(SW_KV_WIDTH, tm), rope, rope, _full(w_in.shape), _tok_spec(D, tm),
                  _mod_spec(tps, D), _full((1, D)), _tok_spec(D, tm)],
        out_specs=[_tok_spec(D, tm), _tok_spec(IN_WIDTH, tm), _full((8, D)), _bstat_spec(tps, D)],
        out_shape=[jax.ShapeDtypeStruct((T, D), F32), jax.ShapeDtypeStruct((T, IN_WIDTH), BF16),
                   jax.ShapeDtypeStruct((8, D), F32), jax.ShapeDtypeStruct((B, 8, D), F32)],
        compiler_params=_params("arbitrary"),
    )(dqa, dka, dva, dqb, dkb, dvb, cos_t, sin_t, w_in, x2d, mod3, g_attn, dx1)


def _matmul_tn(a, b, name, tm=None, tk=512):
    T, M = a.shape
    N = b.shape[1]
    tm = M if tm is None else tm
    nk = T // tk

    def body(a_ref, b_ref, o_ref, acc):
        k = pl.program_id(1)

        @pl.when(k == 0)
        def _():
            acc[...] = jnp.zeros_like(acc)

        acc[...] += _tn(a_ref[...], b_ref[...])

        @pl.when(k == nk - 1)
        def _():
            o_ref[...] = acc[...].astype(BF16)

    return pl.pallas_call(
        body, name=name, grid=(M // tm, nk),
        in_specs=[pl.BlockSpec((tk, tm), lambda i, k: (k, i)), pl.BlockSpec((tk, N), lambda i, k: (k, 0))],
        out_specs=pl.BlockSpec((tm, N), lambda i, k: (i, 0)),
        out_shape=jax.ShapeDtypeStruct((M, N), BF16),
        scratch_shapes=[pltpu.VMEM((tm, N), F32)],
        compiler_params=_params("parallel", "arbitrary"),
    )(a, b)


def _na_geometry(S):
    rows = S // GRID_W
    wr = min(NA_ROWS_MAX, rows)
    return rows, wr


def _na_window(r, rows, wr):
    rs = jnp.clip(r - wr // 2, 0, rows - wr)
    return pl.multiple_of(rs * GRID_W, GRID_W), pl.multiple_of((rs - r + NA_ROWS_MAX - 1) * GRID_W, GRID_W)


NA_STEP_PAIRS = 2
NA_GW = NA_STEP_PAIRS * 128
NA_BWD_ROWS = 4
NA_ROWS_PER_STEP = 4


def _na_specs(S, kw_n, order):
    ng = NA_PAIRS // NA_STEP_PAIRS

    def col(k):
        return pl.BlockSpec((1, S, NA_GW), lambda *ids: (order(*ids)[0], 0, k * ng + order(*ids)[1]))
    bias = pl.BlockSpec((NA_STEP_PAIRS, N_DR * GRID_W, 128), lambda *ids: (order(*ids)[1], 0, 0))
    out = pl.BlockSpec((1, S, NA_GW), lambda *ids: (order(*ids)[0], 0, order(*ids)[1]))
    return col(0), col(1), col(2), bias, out


def _block_diag(t):
    left = lax.broadcasted_iota(jnp.int32, t.shape, 1) < HEAD_DIM
    zero = jnp.zeros_like(t)
    return jnp.concatenate([jnp.where(left, t, zero), jnp.where(left, zero, t)], axis=0)


def _diag_blocks(res):
    left = lax.broadcasted_iota(jnp.int32, (HEAD_DIM, 128), 1) < HEAD_DIM
    return jnp.where(left, res[:HEAD_DIM], res[HEAD_DIM:])


def _col_softmax(st):
    e = jnp.exp(st - jnp.max(st, axis=0, keepdims=True))
    return e * (1.0 / jnp.sum(e, axis=0, keepdims=True))


def _na_fwd(qkv, bias, tasks=()):
    B, S, _ = qkv.shape
    rows, wr = _na_geometry(S)
    kw_n = wr * GRID_W

    def body(q_ref, k_ref, v_ref, b_ref, o_ref):
        def step(it, carry):
            win = [_na_window(it * NA_ROWS_PER_STEP + u, rows, wr) for u in range(NA_ROWS_PER_STEP)]
            qrows = [pl.ds(pl.multiple_of((it * NA_ROWS_PER_STEP + u) * GRID_W, GRID_W), GRID_W) for u in range(NA_ROWS_PER_STEP)]
            krows = [pl.ds(w[0], kw_n) for w in win]
            brows = [pl.ds(w[1], kw_n) for w in win]
            lanes = [pl.ds(p * 128, 128) for p in range(NA_STEP_PAIRS)]
            chains = [(u, p) for u in range(NA_ROWS_PER_STEP) for p in range(NA_STEP_PAIRS)]
            st = {(u, p): _nt(k_ref[0, krows[u], lanes[p]], _block_diag(q_ref[0, qrows[u], lanes[p]])) for u, p in chains}
            pn = {(u, p): _col_softmax(st[(u, p)] + b_ref[p, brows[u], :]).astype(BF16) for u, p in chains}
            out = {(u, p): _diag_blocks(_tn(pn[(u, p)], v_ref[0, krows[u], lanes[p]])) for u, p in chains}
            for u in range(NA_ROWS_PER_STEP):
                o_ref[0, qrows[u], :] = jnp.concatenate([out[(u, p)] for p in range(NA_STEP_PAIRS)], axis=1)
            return carry

        lax.fori_loop(0, rows // NA_ROWS_PER_STEP, step, 0)

    q, k, v, bs, out = _na_specs(S, kw_n, lambda b, g: (b, g))
    return _hosted_call(body, "na_fwd", (B, NA_PAIRS // NA_STEP_PAIRS), [q, k, v, bs], [out],
                        [jax.ShapeDtypeStruct((B, S, NA_WIDTH), F32)], (qkv, qkv, qkv, bias), tasks)


def _na_bwd(qkv, bias, doa, tasks=()):
    B, S, _ = qkv.shape
    rows, wr = _na_geometry(S)
    kw_n = wr * GRID_W

    def body(q_ref, k_ref, v_ref, b_ref, do_ref, dq_ref, dk_ref, dv_ref, db_ref, dk_acc, dv_acc):
        @pl.when(pl.program_id(1) == 0)
        def _():
            db_ref[...] = jnp.zeros_like(db_ref)

        dk_acc[...] = jnp.zeros_like(dk_acc)
        dv_acc[...] = jnp.zeros_like(dv_acc)

        def step(it, carry):
            nu, pairs = range(NA_BWD_ROWS), range(NA_STEP_PAIRS)
            win = [_na_window(it * NA_BWD_ROWS + u, rows, wr) for u in nu]
            qrows = [pl.ds(pl.multiple_of((it * NA_BWD_ROWS + u) * GRID_W, GRID_W), GRID_W) for u in nu]
            krows = [pl.ds(w[0], kw_n) for w in win]
            brows = [pl.ds(w[1], kw_n) for w in win]
            lanes = [pl.ds(p * 128, 128) for p in pairs]
            chains = [(u, p) for u in nu for p in pairs]
            kp = {(u, p): k_ref[0, krows[u], lanes[p]] for u, p in chains}
            qbd = {(u, p): _block_diag(q_ref[0, qrows[u], lanes[p]]) for u, p in chains}
            dobd = {(u, p): _block_diag(do_ref[0, qrows[u], lanes[p]]) for u, p in chains}
            st = {c: _nt(kp[c], qbd[c]) for c in chains}
            dpt = {(u, p): _nt(v_ref[0, krows[u], lanes[p]], dobd[(u, p)]) for u, p in chains}
            pn = {(u, p): _col_softmax(st[(u, p)] + b_ref[p, brows[u], :]) for u, p in chains}
            dst = {c: pn[c] * (dpt[c] - jnp.sum(pn[c] * dpt[c], axis=0, keepdims=True)) for c in chains}
            dsb = {c: dst[c].astype(BF16) for c in chains}
            dq = {c: _diag_blocks(_tn(dsb[c], kp[c])) for c in chains}
            dk = {c: _nn(dsb[c], qbd[c]) for c in chains}
            dv = {c: _nn(pn[c].astype(BF16), dobd[c]) for c in chains}
            for u in nu:
                dq_ref[0, qrows[u], :] = jnp.concatenate([dq[(u, p)] for p in pairs], axis=1).astype(BF16)
                dk_acc[krows[u], :] += jnp.concatenate([dk[(u, p)] for p in pairs], axis=1)
                dv_acc[krows[u], :] += jnp.concatenate([dv[(u, p)] for p in pairs], axis=1)
                for p in pairs:
                    db_ref[p, brows[u], :] += dst[(u, p)]
            return carry

        lax.fori_loop(0, rows // NA_BWD_ROWS, step, 0)

        def emit(i, carry):
            r = pl.ds(pl.multiple_of(i * 256, 256), 256)
            dk_ref[0, r, :] = dk_acc[r, :].astype(BF16)
            dv_ref[0, r, :] = dv_acc[r, :].astype(BF16)
            return carry

        lax.fori_loop(0, S // 256, emit, 0)

    q, k, v, bs, out = _na_specs(S, kw_n, lambda g, b: (b, g))
    hs = jax.ShapeDtypeStruct((B, S, NA_WIDTH), BF16)
    return _hosted_call(body, "na_bwd", (NA_PAIRS // NA_STEP_PAIRS, B), [q, k, v, bs, out], [out, out, out, bs],
                        [hs, hs, hs, jax.ShapeDtypeStruct((NA_PAIRS, N_DR * GRID_W, 128), F32)], (qkv, qkv, qkv, bias, doa), tasks,
                        scratch_shapes=[pltpu.VMEM((S, NA_GW), F32), pltpu.VMEM((S, NA_GW), F32)])


SW_PAIRS = SW_HEADS // 2


def _sw_band(n, S):
    kw_n = 3 * SW_BLOCK
    start = pl.multiple_of(jnp.clip(n * SW_BLOCK - SW_BLOCK, 0, S - kw_n), SW_BLOCK)
    kpos = start + lax.broadcasted_iota(jnp.int32, (kw_n, SW_BLOCK), 0)
    qpos = n * SW_BLOCK + lax.broadcasted_iota(jnp.int32, (kw_n, SW_BLOCK), 1)
    return start, jnp.abs(qpos - kpos) <= SW_WINDOW


def _kv_halves(t):
    left = lax.broadcasted_iota(jnp.int32, t.shape, 1) < HEAD_DIM
    swapped = pltpu.roll(t, HEAD_DIM, axis=1)
    zero = jnp.zeros_like(t)
    return {(0, 0): jnp.where(left, t, zero), (0, 1): jnp.where(left, zero, swapped),
            (1, 0): jnp.where(left, swapped, zero), (1, 1): jnp.where(left, zero, t)}


def _sw_probs(st, ok, sk):
    st = jnp.where(ok, st, NEG)
    m = jnp.maximum(jnp.max(st, axis=0, keepdims=True), sk)
    e = jnp.exp(st - m)
    esk = jnp.exp(sk - m)
    inv = 1.0 / (jnp.sum(e, axis=0, keepdims=True) + esk)
    return e * inv, esk * inv


def _sw_specs(S):
    q = pl.BlockSpec((1, S, SW_WIDTH), lambda b, *_: (b, 0, ROPE_LO // SW_WIDTH))
    k = pl.BlockSpec((1, S, SW_KV_WIDTH), lambda b, *_: (b, 0, (ROPE_LO + SW_WIDTH) // SW_KV_WIDTH))
    v = pl.BlockSpec((1, S, SW_KV_WIDTH), lambda b, *_: (b, 0, (ROPE_LO + ROPE_WIDTH) // SW_KV_WIDTH))
    return q, k, v


SW_FWD_SPLIT = 2


def _sw_fwd(sink, qkv, tasks=()):
    B, S, _ = qkv.shape
    kw_n = 3 * SW_BLOCK

    def body(sink_ref, q_ref, k_ref, v_ref, o_ref):
        def step(n, carry):
            start, ok = _sw_band(n, S)
            qrows = pl.ds(pl.multiple_of(n * SW_BLOCK, SW_BLOCK), SW_BLOCK)
            krows = pl.ds(start, kw_n)
            kh, vh = _kv_halves(k_ref[0, krows, :]), _kv_halves(v_ref[0, krows, :])
            heads = [(p, e) for p in range(SW_PAIRS) for e in range(2)]
            qp = [q_ref[0, qrows, pl.ds(p * 128, 128)] for p in range(SW_PAIRS)]
            kv_of = lambda p: p // (SW_PAIRS // SW_KV_HEADS)
            st = {(p, e): _nt(kh[(kv_of(p), e)], qp[p]) for p, e in heads}
            pn = {(p, e): _sw_probs(st[(p, e)], ok, sink_ref[2 * p + e])[0].astype(BF16) for p, e in heads}
            outs = [_tn(pn[(p, 0)], vh[(kv_of(p), 0)]) + _tn(pn[(p, 1)], vh[(kv_of(p), 1)]) for p in range(SW_PAIRS)]
            o_ref[0, qrows, :] = jnp.concatenate(outs, axis=1)
            return carry

        half = (S // SW_BLOCK) // SW_FWD_SPLIT
        lax.fori_loop(pl.program_id(1) * half, (pl.program_id(1) + 1) * half, step, 0)

    q, k, v = _sw_specs(S)
    return _hosted_call(
        body, "sw_fwd", (B, SW_FWD_SPLIT), [pl.BlockSpec(memory_space=pltpu.SMEM), q, k, v],
        [pl.BlockSpec((1, S, SW_WIDTH), lambda b, s: (b, 0, 0))], [jax.ShapeDtypeStruct((B, S, SW_WIDTH), F32)],
        (sink, qkv, qkv, qkv), tasks)


def _sw_bwd(sink, qkv, dob):
    B, S, _ = qkv.shape
    kw_n = 3 * SW_BLOCK

    fold_rows = 256

    def body(sink_ref, q_ref, k_ref, v_ref, do_ref, dq_ref, dk_ref, dv_ref, dsink_ref, dk_acc, dv_acc):
        @pl.when(pl.program_id(0) == 0)
        def _():
            dsink_ref[...] = jnp.zeros_like(dsink_ref)

        dk_acc[...] = jnp.zeros_like(dk_acc)
        dv_acc[...] = jnp.zeros_like(dv_acc)
        ppk = SW_PAIRS // SW_KV_HEADS

        def step(n, carry):
            start, ok = _sw_band(n, S)
            qrows = pl.ds(pl.multiple_of(n * SW_BLOCK, SW_BLOCK), SW_BLOCK)
            krows = pl.ds(start, kw_n)
            kh, vh = _kv_halves(k_ref[0, krows, :]), _kv_halves(v_ref[0, krows, :])
            heads = [(p, e) for p in range(SW_PAIRS) for e in range(2)]
            qp = [q_ref[0, qrows, pl.ds(p * 128, 128)] for p in range(SW_PAIRS)]
            dop = [do_ref[0, qrows, pl.ds(p * 128, 128)] for p in range(SW_PAIRS)]
            st = {(p, e): _nt(kh[(p // ppk, e)], qp[p]) for p, e in heads}
            dpt = {(p, e): _nt(vh[(p // ppk, e)], dop[p]) for p, e in heads}
            pnb, dsb = {}, {}
            for p, e in heads:
                pn, psink = _sw_probs(st[(p, e)], ok, sink_ref[2 * p + e])
                delta = jnp.sum(pn * dpt[(p, e)], axis=0, keepdims=True)
                dsb[(p, e)] = (pn * (dpt[(p, e)] - delta)).astype(BF16)
                pnb[(p, e)] = pn.astype(BF16)
                dsink_ref[2 * p + e:2 * p + e + 1, :] += -(psink * delta)
            dq_ref[0, qrows, :] = jnp.concatenate(
                [_tn(dsb[(p, 0)], kh[(p // ppk, 0)]) + _tn(dsb[(p, 1)], kh[(p // ppk, 1)]) for p in range(SW_PAIRS)], axis=1)
            left = lax.broadcasted_iota(jnp.int32, (kw_n, 128), 1) < HEAD_DIM
            dks, dvs = [], []
            for kv in range(SW_KV_HEADS):
                dk = dv = None
                for p in range(kv * ppk, (kv + 1) * ppk):
                    dk_p = jnp.where(left, _nn(dsb[(p, 0)], qp[p]), _nn(dsb[(p, 1)], qp[p]))
                    dv_p = jnp.where(left, _nn(pnb[(p, 0)], dop[p]), _nn(pnb[(p, 1)], dop[p]))
                    dk = dk_p if dk is None else dk + dk_p
                    dv = dv_p if dv is None else dv + dv_p
                dks.append(dk)
                dvs.append(dv)
            dk_acc[krows, :] += jnp.concatenate(dks, axis=1)
            dv_acc[krows, :] += jnp.concatenate(dvs, axis=1)
            return carry

        lax.fori_loop(0, S // SW_BLOCK, step, 0)

        def fold(i, carry):
            rows = pl.ds(pl.multiple_of(i * fold_rows, fold_rows), fold_rows)
            left = lax.broadcasted_iota(jnp.int32, (fold_rows, 128), 1) < HEAD_DIM
            for acc, out_ref in ((dk_acc, dk_ref), (dv_acc, dv_ref)):
                a, b = acc[rows, 0:128], acc[rows, 128:256]
                out_ref[0, rows, :] = jnp.where(left, a + pltpu.roll(a, HEAD_DIM, axis=1), b + pltpu.roll(b, HEAD_DIM, axis=1))
            return carry

        lax.fori_loop(0, S // fold_rows, fold, 0)

        @pl.when(pl.program_id(0) == B - 1)
        def _():
            dsink_ref[...] = jnp.broadcast_to(jnp.sum(dsink_ref[...], axis=1, keepdims=True), dsink_ref.shape)

    q, k, v = _sw_specs(S)
    qo = pl.BlockSpec((1, S, SW_WIDTH), lambda b: (b, 0, 0))
    ko = pl.BlockSpec((1, S, SW_KV_WIDTH), lambda b: (b, 0, 0))
    return pl.pallas_call(
        body, name="sw_bwd", grid=(B,),
        in_specs=[pl.BlockSpec(memory_space=pltpu.SMEM), q, k, v, qo],
        out_specs=[qo, ko, ko, _full((SW_HEADS, 128))],
        out_shape=[jax.ShapeDtypeStruct((B, S, SW_WIDTH), F32), jax.ShapeDtypeStruct((B, S, SW_KV_WIDTH), F32),
                   jax.ShapeDtypeStruct((B, S, SW_KV_WIDTH), F32), jax.ShapeDtypeStruct((SW_HEADS, 128), F32)],
        scratch_shapes=[pltpu.VMEM((S, 2 * SW_KV_WIDTH), F32), pltpu.VMEM((S, 2 * SW_KV_WIDTH), F32)],
        compiler_params=_params("arbitrary"),
    )(sink, qkv, qkv, qkv, dob)


def _pack_sum_adamw(packs, params):
    W = packs.shape[1]
    n_p = len(params)

    def body(p_ref, *refs):
        ins, tot_ref, outs = refs[:3 * n_p], refs[3 * n_p], refs[3 * n_p + 1:]
        tot = p_ref[0:8, :]
        for d in range(1, N_DEV):
            tot = tot + p_ref[8 * d:8 * d + 8, :]
        tot_ref[...] = tot
        for i, (w, _, _, rows, off) in enumerate(params):
            n = w.shape[1]
            g = tot[rows[0]:rows[0] + 1, off:off + n]
            for r in rows[1:]:
                g = g + tot[r:r + 1, off:off + n]
            w_ref, m_ref, v_ref = ins[3 * i:3 * i + 3]
            g_ref, d_ref, nm_ref, nv_ref = outs[4 * i:4 * i + 4]
            g_ref[...] = g
            d_ref[...], nm_ref[...], nv_ref[...] = _adam_update(w_ref[...], g, m_ref[...], v_ref[...])

    res = pl.pallas_call(
        body, name="small_adamw",
        out_shape=[jax.ShapeDtypeStruct((8, W), F32)] + [jax.ShapeDtypeStruct(p[0].shape, F32) for p in params for _ in range(4)],
        compiler_params=pltpu.CompilerParams(vmem_limit_bytes=VMEM_LIMIT),
    )(packs, *[a for p in params for a in p[:3]])
    return res[0], [res[1 + 4 * i:5 + 4 * i] for i in range(n_p)]


def _adam_update(w, g, m, v):
    c1 = 1.0 - ADAM_B1 ** ADAM_STEP
    c2 = 1.0 - ADAM_B2 ** ADAM_STEP
    nm = ADAM_B1 * m + (1.0 - ADAM_B1) * g
    nv = ADAM_B2 * v + (1.0 - ADAM_B2) * (g * g)
    return -ADAM_LR * ((nm / c1) / (jnp.sqrt(nv / c2) + ADAM_EPS) + ADAM_WD * w), nm, nv


def _adamw(w, g, m, v, name):
    def body(w_ref, g_ref, m_ref, v_ref, d_ref, nm_ref, nv_ref):
        d_ref[...], nm_ref[...], nv_ref[...] = _adam_update(w_ref[...], g_ref[...], m_ref[...], v_ref[...])

    s = jax.ShapeDtypeStruct(w.shape, F32)
    return pl.pallas_call(body, name=name, out_shape=[s, s, s],
                          compiler_params=pltpu.CompilerParams(vmem_limit_bytes=VMEM_LIMIT))(w, g, m, v)


def _sum_adamw_rows(R):
    return max(r for r in range(16, min(R, 256) + 1, 16) if R % r == 0)


def _sum_adamw_steps(R):
    return R // _sum_adamw_rows(R)


def _sum_adamw(own, recvb, w, m, v, name, tasks=()):
    R, C = own.shape
    rc = _sum_adamw_rows(R)

    def body(own_ref, r_ref, w_ref, m_ref, v_ref, g_ref, d_ref, nm_ref, nv_ref):
        g = own_ref[...]
        for j in range(3):
            g = g + r_ref[j].astype(F32)
        g_ref[...] = g
        d_ref[...], nm_ref[...], nv_ref[...] = _adam_update(w_ref[...], g, m_ref[...], v_ref[...])

    blk = pl.BlockSpec((rc, C), lambda i: (i, 0))
    s = jax.ShapeDtypeStruct((R, C), F32)
    return _hosted_call(body, name, (R // rc,), [blk, pl.BlockSpec((3, rc, C), lambda i: (0, i, 0)), blk, blk, blk],
                        [blk, blk, blk, blk], [s, s, s, s], (own, recvb, w, m, v), tasks)


def _by_device(dw):
    return dw.reshape(N_DEV, dw.shape[0] // N_DEV, dw.shape[1])


def _local_step(x, mod, g_attn, w_in, bias, sw_sink, g_na_out, g_sw_out, w_out, g_ffn, w_up, conv_w, conv_b, w_down,
                g_final, target, sharded):
    B, S, D = x.shape
    T = B * S
    x2d = x.reshape(T, D)
    mod3 = mod.reshape(B, 6, D)
    cos_t, sin_t = _rope_tables(S)
    sink = sw_sink.reshape(SW_HEADS)
    n_tiles = T // WIDE_TILE
    full = lambda g: g.reshape(N_DEV * g.shape[1], g.shape[2])

    rider = lambda w, mid, lo, n, into=None: [_gather_task(w, mid, rows=(lo, n), into=into)] if sharded else []
    if sharded:
        qu, hd = w_up.shape[0] // 4, w_down.shape[0] // 2
    (h, qkv), got = _attn_in(x2d, mod3, g_attn, w_in, cos_t, sin_t, S,
                             [_gather_task(w_out, n_tiles // 2)] + rider(w_up, n_tiles - 1, 0, qu) if sharded else [])
    if sharded:
        w_out, w_up_buf = full(got[0][0]), got[1][0]
    qkv3 = qkv.reshape(B, S, IN_WIDTH)
    na_steps, sw_steps = B * (NA_PAIRS // NA_STEP_PAIRS), B * SW_FWD_SPLIT
    (oa,), got = _na_fwd(qkv3, bias, rider(w_up, na_steps - 1, qu, 2 * qu, w_up_buf) if sharded else [])
    if sharded:
        w_up_buf = got[0][0]
    oa = oa.reshape(T, NA_WIDTH)
    (ob,), got = _sw_fwd(sink, qkv3, rider(w_up, sw_steps // 2, 3 * qu, qu, w_up_buf) if sharded else [])
    if sharded:
        w_up = full(got[0][0])
    ob = ob.reshape(T, SW_WIDTH)
    (mixin, mix, x1), _ = _attn_out(oa, ob, x2d, mod3, g_na_out, g_sw_out, w_out, S)
    (h2, val, gt), got = _ffn_up(x1, mod3, g_ffn, w_up, S, rider(w_down, 3 * n_tiles // 4, 0, 2 * hd) if sharded else [])
    if sharded:
        w_down = full(got[0][0])
    a, act, vd, dx2, df, gstat_f, bstat_f = _ffn_down(gt, val, conv_w, conv_b, w_down, x1, mod3, g_final, target.reshape(T, D), B, S)
    F = val.shape[1]

    dw_down = _matmul_tn(a, df, "dw_down")
    (dval, dgc, cstat), got = _ffn_down_bwd(df, w_down, act, vd, [_swap_task(_by_device(dw_down))] if sharded else [])
    if sharded:
        send_down, own_down = _chip_sums(_by_device(dw_down), got[0][0])
    (du, dx1, dmix, gstat_u, bstat_u, cstat_w), got = _ffn_up_bwd(dgc, dval, gt, conv_w, w_up, x1, mod3, g_ffn, dx2, mix, B, S,
                                                                  [_exchange_task(send_down)] if sharded else [])
    if sharded:
        dw_down = (own_down, got[0][0])
    dw_up = _matmul_tn(du, h2, "dw_up", tm=F)
    dw_out = _matmul_tn(mixin, dmix, "dw_out")
    (doa, dob, gstat_o), got = _attn_out_bwd(dmix, w_out, oa, ob, g_na_out, g_sw_out,
                                             [_swap_task(_by_device(dw_up)), _swap_task(_by_device(dw_out))] if sharded else [])
    if sharded:
        send_up, own_up = _chip_sums(_by_device(dw_up), got[0][0])
        send_out, own_out = _chip_sums(_by_device(dw_out), got[1][0])
    (dqa, dka, dva, dbt), got = _na_bwd(qkv3, bias, doa.reshape(B, S, NA_WIDTH),
                                        [_exchange_task(send_up), _exchange_task(send_out)] if sharded else [])
    if sharded:
        dw_up, dw_out = (own_up, got[0][0]), (own_out, got[1][0])
    dqb, dkb, dvb, dsink = _sw_bwd(sink, qkv3, dob.reshape(B, S, SW_WIDTH))
    r2 = lambda t: t.reshape(T, t.shape[-1])
    grad_x, dproj, gstat_i, bstat_i = _attn_in_bwd(r2(dqa), r2(dka), r2(dva), r2(dqb), r2(dkb), r2(dvb), cos_t, sin_t, w_in, x2d, mod3,
                                                   g_attn, dx1, B, S)
    dw_in = _matmul_tn(dproj, h, "dw_in")

    dmod = jnp.stack([bstat_i[:, 0], bstat_i[:, 1], bstat_u[:, 2], bstat_u[:, 0], bstat_u[:, 1], bstat_f[:, 0]], axis=1)
    small = dict(g_attn=gstat_i[0], g_ffn=gstat_u[0], g_final=gstat_f[0], loss=gstat_f[1, 0], g_na_out=gstat_o[0], g_sw_out=gstat_o[1],
                 sw_sink=dsink[:, 0], conv_b=cstat[0], conv_w=cstat_w[1:4], dbt=dbt,
                 raw=(bstat_i, bstat_u, bstat_f, gstat_i, gstat_u, gstat_f, gstat_o, dsink, cstat, cstat_w))
    return grad_x.reshape(B, S, D), dict(w_in=dw_in, w_out=dw_out, w_up=dw_up, w_down=dw_down), dmod, small


def _pack_slab(raw, drpb):
    D, F = raw[3].shape[1], raw[8].shape[1]
    n_seq = raw[0].shape[0]

    def body(bi_ref, bu_ref, bf_ref, gi_ref, gu_ref, gf_ref, go_ref, ds_ref, cs_ref, cw_ref, rp_ref, o_ref):
        o_ref[...] = jnp.zeros_like(o_ref)
        for b in range(n_seq):
            mods = (bi_ref[b, 0:1, :], bi_ref[b, 1:2, :], bu_ref[b, 2:3, :], bu_ref[b, 0:1, :], bu_ref[b, 1:2, :], bf_ref[b, 0:1, :])
            for k, row in enumerate(mods):
                o_ref[b:b + 1, k * D:(k + 1) * D] = row
        o = 0
        for row in (gi_ref[0:1, :], gu_ref[0:1, :], gf_ref[0:1, :], go_ref[0:1, :], go_ref[1:2, :]):
            o_ref[2:3, o:o + row.shape[1]] = row
            o += row.shape[1]
        ds = ds_ref[...]
        eye = lax.broadcasted_iota(jnp.int32, ds.shape, 0) == lax.broadcasted_iota(jnp.int32, ds.shape, 1)
        o_ref[2:3, o:o + 128] = jnp.sum(jnp.where(eye, ds, 0.0), axis=0, keepdims=True)
        o_ref[2:3, o + 128:o + 256] = gf_ref[1:2, 0:128]
        o_ref[3:4, 0:F] = cs_ref[0:1, :]
        o_ref[4:5, 0:rp_ref.shape[1]] = rp_ref[...]
        o_ref[5:8, 0:F] = cw_ref[1:4, :]

    return pl.pallas_call(body, name="pack_slab", out_shape=jax.ShapeDtypeStruct((8, PACK_W), F32),
                          compiler_params=pltpu.CompilerParams(vmem_limit_bytes=VMEM_LIMIT))(*raw, drpb)


def kernel(x, c, w_ada, b_ada, g_attn, w_in, na_rpb, sw_sink, g_na_out, g_sw_out, w_out, g_ffn, w_up, conv_w, conv_b, w_down, g_final, loss_target, m_w_ada, m_b_ada, m_g_attn, m_w_in, m_na_rpb, m_sw_sink, m_g_na_out, m_g_sw_out, m_w_out, m_g_ffn, m_w_up, m_conv_w, m_conv_b, m_w_down, m_g_final, v_w_ada, v_b_ada, v_g_attn, v_w_in, v_na_rpb, v_sw_sink, v_g_na_out, v_g_sw_out, v_w_out, v_g_ffn, v_w_up, v_conv_w, v_conv_b, v_w_down, v_g_final):
    B, S, D = x.shape
    me = 4 * lax.axis_index("x") + 2 * lax.axis_index("y") + lax.axis_index("c")
    ada_c = w_ada.shape[2]
    F_l = conv_w.shape[2]

    cw_l = jnp.pad(conv_w[0], ((0, 8 - conv_w.shape[1]), (0, 0)))
    c_l = jnp.pad(c, ((0, 8 - B), (0, 0)))
    tr = {"w_in", "w_up"}
    w_in_t = jnp.transpose(w_in[0])
    shards = dict(w_out=w_out[0].astype(BF16), w_up=jnp.transpose(w_up[0]).astype(BF16), w_down=w_down[0].astype(BF16))

    b_ada_l = lax.dynamic_slice(b_ada, (0, me * ada_c), (1, ada_c))
    slabs, mod_all, w_in_all = _ada_fwd(jnp.concatenate([c_l, cw_l], axis=1), w_ada[0], b_ada_l, w_in_t.astype(BF16), B)
    c_all = slabs[:, :, :D].reshape(N_DEV * 8, D)
    conv_w_f = jnp.transpose(slabs[:, :3, D:], (1, 0, 2)).reshape(3, N_DEV * F_l)
    mod_mine = lax.dynamic_slice(mod_all, (0, me * B, 0), (N_DEV, B, ada_c))
    mod = jnp.transpose(mod_mine, (1, 0, 2)).reshape(B, N_DEV * ada_c)
    w_in_f = w_in_all.reshape(N_DEV * w_in_t.shape[0], D)

    bias = _na_bias_table(na_rpb[0])

    grad_x, dw, dmod, small = _local_step(x, mod, g_attn, w_in_f, bias, sw_sink, g_na_out, g_sw_out, shards["w_out"], g_ffn,
                                          shards["w_up"], conv_w_f, conv_b, shards["w_down"], g_final.reshape(1, D), loss_target,
                                          sharded=True)
    drpb = _na_bias_grad(small["dbt"])

    slab = _pack_slab(small["raw"], drpb.reshape(1, -1))
    weights = dict(w_ada=w_ada, b_ada=b_ada, g_attn=g_attn, w_in=w_in, na_rpb=na_rpb, sw_sink=sw_sink, g_na_out=g_na_out,
                   g_sw_out=g_sw_out, w_out=w_out, g_ffn=g_ffn, w_up=w_up, conv_w=conv_w, conv_b=conv_b, w_down=w_down, g_final=g_final)
    ms = dict(w_ada=m_w_ada, b_ada=m_b_ada, g_attn=m_g_attn, w_in=m_w_in, na_rpb=m_na_rpb, sw_sink=m_sw_sink, g_na_out=m_g_na_out,
              g_sw_out=m_g_sw_out, w_out=m_w_out, g_ffn=m_g_ffn, w_up=m_w_up, conv_w=m_conv_w, conv_b=m_conv_b, w_down=m_w_down, g_final=m_g_final)
    vs = dict(w_ada=v_w_ada, b_ada=v_b_ada, g_attn=v_g_attn, w_in=v_w_in, na_rpb=v_na_rpb, sw_sink=v_sw_sink, g_na_out=v_g_na_out,
              g_sw_out=v_g_sw_out, w_out=v_w_out, g_ffn=v_g_ffn, w_up=v_w_up, conv_w=v_conv_w, conv_b=v_conv_b, w_down=v_w_down, g_final=v_g_final)
    names = list(weights)
    grads, deltas, new_m, new_v = {}, {}, {}, {}
    flat = lambda t: t.reshape(1, -1)

    def shard2d(nm):
        if nm in tr:
            return (lambda t: jnp.transpose(t[0])), (lambda t: jnp.transpose(t)[None])
        return (lambda t: t[0]), (lambda t: t[None])

    def finish_sum(nm, own, recvb, tasks=()):
        r, back = shard2d(nm)
        (g2, d_, m_, v_), got = _sum_adamw(own, recvb, r(weights[nm]), r(ms[nm]), r(vs[nm]), "adamw_" + nm, tasks)
        grads[nm], deltas[nm], new_m[nm], new_v[nm] = back(g2), back(d_), back(m_), back(v_)
        return got

    g8_in = _by_device(dw["w_in"])
    got = finish_sum("w_down", *dw["w_down"], [_swap_task(g8_in)])
    send_in, own_in = _chip_sums(g8_in, got[0][0])
    n_up = _sum_adamw_steps(dw["w_up"][0].shape[0])
    got = finish_sum("w_up", *dw["w_up"], [_exchange_task(send_in), _gather_task(slab, n_up - 1)])
    packs = got[1][0]
    finish_sum("w_in", own_in, got[0][0])
    finish_sum("w_out", *dw["w_out"])

    where = dict(b_ada=((0, 1), 0), g_attn=((2,), 0), g_ffn=((2,), D), g_final=((2,), 2 * D), g_na_out=((2,), 3 * D),
                 g_sw_out=((2,), 3 * D + NA_WIDTH), sw_sink=((2,), 3 * D + NA_WIDTH + SW_WIDTH), conv_b=((3,), 0), na_rpb=((4,), 0))
    tot, small_out = _pack_sum_adamw(packs.reshape(N_DEV * 8, PACK_W),
                                     [(flat(weights[n]), flat(ms[n]), flat(vs[n])) + where[n] for n in where])
    for n, (g_, d_, m_, v_) in zip(where, small_out):
        shp = weights[n].shape
        grads[n], deltas[n], new_m[n], new_v[n] = g_.reshape(shp), d_.reshape(shp), m_.reshape(shp), v_.reshape(shp)
    loss = tot[2, 3 * D + NA_WIDTH + SW_WIDTH + 128]

    dmod_cols = lax.dynamic_slice(packs.reshape(N_DEV * 8, PACK_W), (0, me * ada_c), (N_DEV * 8, ada_c))
    for nm, g2 in (("w_ada", _ada_bwd(c_all, dmod_cols)), ("conv_w", lax.dynamic_slice(tot[5:8], (0, me * F_l), (3, F_l)))):
        r, back = shard2d(nm)
        d_, m_, v_ = _adamw(r(weights[nm]), g2, r(ms[nm]), r(vs[nm]), "adamw_" + nm)
        grads[nm], deltas[nm], new_m[nm], new_v[nm] = back(g2), back(d_), back(m_), back(v_)
    return (loss, grad_x, *[grads[n] for n in names], *[deltas[n] for n in names], *[new_m[n] for n in names],
            *[new_v[n] for n in names])
```

```python
import functools

import numpy as np
import jax
import jax.numpy as jnp
from jax import lax
from jax.experimental import pallas as pl
from jax.experimental.pallas import tpu as pltpu

F32, BF16 = jnp.float32, jnp.bfloat16
MESH_ID = pl.DeviceIdType.MESH
N_DEV = 8

HEAD_DIM = 64
NA_HEADS = 8
SW_HEADS = 8
SW_KV_HEADS = 2
SW_GROUP = SW_HEADS // SW_KV_HEADS
NA_WIDTH = NA_HEADS * HEAD_DIM
SW_WIDTH = SW_HEADS * HEAD_DIM
SW_KV_WIDTH = SW_KV_HEADS * HEAD_DIM
ROPE_WIDTH = SW_WIDTH + SW_KV_WIDTH
IN_WIDTH = 3 * NA_WIDTH + SW_WIDTH + 2 * SW_KV_WIDTH
ROPE_LO = 3 * NA_WIDTH
GRID_W = 64
NA_ROWS_MAX = 8
NA_COLS = 16
N_DR = 2 * NA_ROWS_MAX - 1
N_DC = 2 * NA_COLS - 1
SW_WINDOW = 128
SW_BLOCK = 128
ROPE_THETA = 10000.0
EPS = 1e-6
NEG = -1e30
Q_SCALE = HEAD_DIM ** -0.5

ADAM_LR = 0.001
ADAM_B1 = 0.9
ADAM_B2 = 0.999
ADAM_EPS = 1e-08
ADAM_WD = 0.01
ADAM_STEP = 10

TOKEN_TILE = 256
WIDE_TILE = 512
VMEM_LIMIT = 56 * 1024 * 1024

PACK_W = 6144


def _nn(a, b):
    return jnp.dot(a, b, preferred_element_type=F32)


def _nt(a, b):
    return lax.dot_general(a, b, (((1,), (1,)), ((), ())), preferred_element_type=F32)


def _tn(a, b):
    return lax.dot_general(a, b, (((0,), (0,)), ((), ())), preferred_element_type=F32)


def _rms(x):
    r = lax.rsqrt(jnp.mean(x * x, axis=-1, keepdims=True) + EPS)
    return x * r, r


def _rms_bwd(xn, r, gy):
    return r * (gy - xn * jnp.mean(xn * gy, axis=-1, keepdims=True))


def _params(*sem):
    return pltpu.CompilerParams(dimension_semantics=sem, vmem_limit_bytes=VMEM_LIMIT)


def _full(shape):
    n = len(shape)
    return pl.BlockSpec(shape, lambda *_: (0,) * n)


def _mesh_pos():
    return lax.axis_index("x"), lax.axis_index("y"), lax.axis_index("c")


def _row_chunk(r):
    for rc in (128, 64, 32, 16):
        if r % rc == 0:
            return rc
    raise ValueError(f"rows {r} not a multiple of 16")


class _Task:
    def __init__(self, inputs, out_shapes, sems, start, finish, mid=None, mid_step=None, alias=None):
        self.inputs, self.out_shapes, self.sems = list(inputs), list(out_shapes), list(sems)
        self.start, self.finish, self.mid, self.mid_step = start, finish, mid, mid_step
        self.alias = alias


def _hosted_call(body, name, grid, in_specs, out_specs, out_shape, operands, tasks, scratch_shapes=()):
    n_in, n_out, n_scr = len(in_specs), len(out_specs), len(scratch_shapes)
    t_in = [len(t.inputs) for t in tasks]
    t_out = [len(t.out_shapes) for t in tasks]
    t_sem = [len(t.sems) for t in tasks]
    n_steps = int(np.prod(grid))

    def wrapped(*refs):
        ins, rest = refs[:n_in], refs[n_in:]
        task_ins, rest = rest[:sum(t_in)], rest[sum(t_in):]
        outs, rest = rest[:n_out], rest[n_out:]
        task_outs, rest = rest[:sum(t_out)], rest[sum(t_out):]
        scr, task_sems = rest[:n_scr], rest[n_scr:]
        step = pl.program_id(0)
        for ax in range(1, len(grid)):
            step = step * grid[ax] + pl.program_id(ax)
        parts = []
        oi = oo = os_ = 0
        for t, a, b, c in zip(tasks, t_in, t_out, t_sem):
            parts.append((t, task_ins[oi:oi + a], task_outs[oo:oo + b], task_sems[os_:os_ + c]))
            oi, oo, os_ = oi + a, oo + b, os_ + c
        for t, ti, to, ts in parts:
            pl.when(step == 0)(functools.partial(t.start, ti, to, ts))
            if t.mid is not None:
                pl.when(step == t.mid_step)(functools.partial(t.mid, ti, to, ts))
        body(*ins, *outs, *scr)
        for t, ti, to, ts in parts:
            pl.when(step == n_steps - 1)(functools.partial(t.finish, ti, to, ts))

    hbm = pl.BlockSpec(memory_space=pl.ANY)
    aliases, oi, oo = {}, n_in, n_out
    for t, a, b in zip(tasks, t_in, t_out):
        if t.alias is not None:
            aliases[oi + t.alias[0]] = oo + t.alias[1]
        oi, oo = oi + a, oo + b
    res = pl.pallas_call(
        wrapped, name=name, grid=grid,
        in_specs=list(in_specs) + [hbm] * sum(t_in),
        out_specs=list(out_specs) + [hbm] * sum(t_out),
        out_shape=list(out_shape) + [s for t in tasks for s in t.out_shapes],
        scratch_shapes=list(scratch_shapes) + [s for t in tasks for s in t.sems],
        input_output_aliases=aliases,
        compiler_params=_params(*(["arbitrary"] * len(grid))),
    )(*operands, *[a for t in tasks for a in t.inputs])
    own, extra = res[:n_out], res[n_out:]
    per_task, o = [], 0
    for b in t_out:
        per_task.append(extra[o:o + b])
        o += b
    return own, per_task


def _gather_task(shard, mid_step, rows=None, into=None):
    lo, n = (0, shard.shape[0]) if rows is None else rows

    def parts(ins, outs, sems):
        x_ref, out_ref, (send_sems, recv_sems, local_sem) = ins[0], outs[0], sems
        x_, y_, c_ = _mesh_pos()
        me, sibling = (x_, y_, c_), (x_, y_, 1 - c_)
        chips = [(1 - x_, y_), (x_, 1 - y_), (1 - x_, 1 - y_)]
        x_ref = x_ref.at[pl.ds(lo, n)]

        def rows(px, py, pc):
            return out_ref.at[4 * px + 2 * py + pc, pl.ds(lo, n)]

        def copy(k, block, to, src=None):
            return pltpu.make_async_remote_copy(
                src_ref=rows(*block) if src is None else src, dst_ref=rows(*block),
                send_sem=send_sems.at[k], recv_sem=recv_sems.at[k], device_id=to, device_id_type=MESH_ID)

        return dict(
            mine=lambda: pltpu.make_async_copy(x_ref, rows(*me), local_sem),
            first=lambda: [copy(0, me, sibling, src=x_ref)] + [copy(1 + j, me, (*chip, c_), src=x_ref) for j, chip in enumerate(chips)],
            passed=lambda: [copy(4 + j, (*chip, c_), sibling) for j, chip in enumerate(chips)],
            landed=lambda: [copy(1 + j, (*chip, c_), me) for j, chip in enumerate(chips)],
            last=lambda: [copy(0, sibling, me)] + [copy(4 + j, (*chip, 1 - c_), me) for j, chip in enumerate(chips)])

    def start(ins, outs, sems):
        p = parts(ins, outs, sems)
        p["mine"]().start()
        for cp in p["first"]():
            cp.start()

    def mid(ins, outs, sems):
        p = parts(ins, outs, sems)
        for cp, fw in zip(p["landed"](), p["passed"]()):
            cp.wait_recv()
            fw.start()

    def finish(ins, outs, sems):
        p = parts(ins, outs, sems)
        for cp in p["last"]():
            cp.wait_recv()
        for cp in p["first"]() + p["passed"]():
            cp.wait_send()
        p["mine"]().wait()

    return _Task([shard] if into is None else [shard, into], [jax.ShapeDtypeStruct((N_DEV,) + shard.shape, shard.dtype)],
                 [pltpu.SemaphoreType.DMA((7,)), pltpu.SemaphoreType.DMA((7,)), pltpu.SemaphoreType.DMA],
                 start, finish, mid, mid_step, alias=None if into is None else (1, 0))


def _swap_task(g8):
    _, R, C = g8.shape

    def copies(ins, outs, sems):
        (g_ref,), (recv_ref,), (ss, rs) = ins, outs, sems
        x_, y_, c_ = _mesh_pos()
        return [pltpu.make_async_remote_copy(src_ref=g_ref.at[2 * k + (1 - c_)], dst_ref=recv_ref.at[k], send_sem=ss.at[k],
                                             recv_sem=rs.at[k], device_id=(x_, y_, 1 - c_), device_id_type=MESH_ID)
                for k in range(4)]

    def start(ins, outs, sems):
        for cp in copies(ins, outs, sems):
            cp.start()

    def finish(ins, outs, sems):
        cps = copies(ins, outs, sems)
        for cp in cps:
            cp.wait_recv()
        for cp in cps:
            cp.wait_send()

    return _Task([g8], [jax.ShapeDtypeStruct((4, R, C), g8.dtype)],
                 [pltpu.SemaphoreType.DMA((4,)), pltpu.SemaphoreType.DMA((4,))], start, finish)


def _chip_sums(g8, recva):
    _, R, C = g8.shape
    rc = _row_chunk(R)

    def body(g_ref, a_ref, send_ref, own_ref):
        x_, y_, c_ = _mesh_pos()
        chips = [(1 - x_, y_), (x_, 1 - y_), (1 - x_, 1 - y_), (x_, y_)]

        def chunk(i, carry):
            rows = pl.ds(pl.multiple_of(i * rc, rc), rc)
            for j, (tx, ty) in enumerate(chips):
                k = 2 * tx + ty
                s = g_ref[2 * k + c_, rows, :].astype(F32) + a_ref[k, rows, :].astype(F32)
                if j < 3:
                    send_ref[j, rows, :] = s.astype(BF16)
                else:
                    own_ref[rows, :] = s
            return carry

        lax.fori_loop(0, R // rc, chunk, 0)

    return pl.pallas_call(body, name="chip_sums", out_shape=[jax.ShapeDtypeStruct((3, R, C), BF16), jax.ShapeDtypeStruct((R, C), F32)],
                          compiler_params=pltpu.CompilerParams(vmem_limit_bytes=VMEM_LIMIT))(g8, recva)


def _exchange_task(sendb):
    def copies(ins, outs, sems):
        (s_ref,), (recv_ref,), (ss, rs) = ins, outs, sems
        x_, y_, c_ = _mesh_pos()
        flips = [(1 - x_, y_), (x_, 1 - y_), (1 - x_, 1 - y_)]
        return [pltpu.make_async_remote_copy(src_ref=s_ref.at[j], dst_ref=recv_ref.at[j], send_sem=ss.at[j], recv_sem=rs.at[j],
                                             device_id=(tx, ty, c_), device_id_type=MESH_ID) for j, (tx, ty) in enumerate(flips)]

    def start(ins, outs, sems):
        for cp in copies(ins, outs, sems):
            cp.start()

    def finish(ins, outs, sems):
        cps = copies(ins, outs, sems)
        for cp in cps:
            cp.wait_recv()
        for cp in cps:
            cp.wait_send()

    return _Task([sendb], [jax.ShapeDtypeStruct(sendb.shape, sendb.dtype)],
                 [pltpu.SemaphoreType.DMA((3,)), pltpu.SemaphoreType.DMA((3,))], start, finish)


def _silu(v):
    return v * (1.0 / (1.0 + jnp.exp(-v)))


def _ada_fwd(c_slab, w_ada_l, b_ada_l, w_in_shard, n_seq):
    W = c_slab.shape[1]
    D, cols = w_ada_l.shape
    n_rows = N_DEV * n_seq
    t_c, t_w = _gather_task(c_slab, 0), _gather_task(w_in_shard, 0)
    t_m = _gather_task(jax.ShapeDtypeStruct((n_rows, cols), F32), 0)

    def body(c_ref, w_ref, b_ref, ws_ref, slabs_ref, mod_ref, win_ref, c_vm, m_vm, copy_sem, *sems):
        sc, sw, sm = sems[0:3], sems[3:6], sems[6:9]
        t_c.start((c_ref,), (slabs_ref,), sc)
        t_w.start((ws_ref,), (win_ref,), sw)
        t_c.mid((c_ref,), (slabs_ref,), sc)
        t_c.finish((c_ref,), (slabs_ref,), sc)
        cp = pltpu.make_async_copy(slabs_ref, c_vm, copy_sem)
        cp.start()
        cp.wait()
        c_all = c_vm[:, :, 0:D].reshape(N_DEV * 8, D)
        m64 = jnp.dot(_silu(c_all), w_ref[...], precision=lax.Precision.HIGHEST, preferred_element_type=F32) + b_ref[...]
        r = lax.broadcasted_iota(jnp.int32, (n_rows, N_DEV * 8), 0)
        c = lax.broadcasted_iota(jnp.int32, (n_rows, N_DEV * 8), 1)
        pick = jnp.where(c == 8 * (r // n_seq) + r % n_seq, 1.0, 0.0)
        m_vm[...] = jnp.dot(pick, m64, precision=lax.Precision.HIGHEST, preferred_element_type=F32)
        t_m.start((m_vm,), (mod_ref,), sm)
        t_w.mid((ws_ref,), (win_ref,), sw)
        t_m.mid((m_vm,), (mod_ref,), sm)
        t_m.finish((m_vm,), (mod_ref,), sm)
        t_w.finish((ws_ref,), (win_ref,), sw)

    hbm, vm = pl.BlockSpec(memory_space=pl.ANY), pl.BlockSpec(memory_space=pltpu.VMEM)
    return pl.pallas_call(
        body, name="ada_fwd", in_specs=[hbm, vm, vm, hbm], out_specs=[hbm, hbm, hbm],
        out_shape=t_c.out_shapes + t_m.out_shapes + t_w.out_shapes,
        scratch_shapes=[pltpu.VMEM((N_DEV, 8, W), F32), pltpu.VMEM((n_rows, cols), F32), pltpu.SemaphoreType.DMA]
        + t_c.sems + t_w.sems + t_m.sems,
        compiler_params=pltpu.CompilerParams(vmem_limit_bytes=VMEM_LIMIT),
    )(c_slab, w_ada_l, b_ada_l, w_in_shard)


def _ada_bwd(c_all, dmod_cols):
    def body(c_ref, d_ref, o_ref):
        o_ref[...] = lax.dot_general(_silu(c_ref[...]), d_ref[...], (((0,), (0,)), ((), ())),
                                     precision=lax.Precision.HIGHEST, preferred_element_type=F32)
    return pl.pallas_call(body, name="ada_bwd", out_shape=jax.ShapeDtypeStruct((c_all.shape[1], dmod_cols.shape[1]), F32),
                          compiler_params=pltpu.CompilerParams(vmem_limit_bytes=VMEM_LIMIT))(c_all, dmod_cols)


NA_PAIRS = NA_HEADS // 2
N_DR_PAD = 16


def _na_bias_table(na_rpb):
    rev = jnp.pad(jnp.flip(na_rpb, axis=2), ((0, 0), (0, N_DR_PAD - N_DR), (0, GRID_W - N_DC)))
    rev = jnp.transpose(rev.reshape(NA_PAIRS, 2, N_DR_PAD, GRID_W), (0, 2, 1, 3)).reshape(NA_PAIRS, N_DR_PAD, 128)

    def body(r_ref, o_ref):
        k = lax.broadcasted_iota(jnp.int32, (GRID_W, 128), 0)
        lane = lax.broadcasted_iota(jnp.int32, (GRID_W, 128), 1)
        q = lane % GRID_W
        cs = jnp.clip(q - NA_COLS // 2, 0, GRID_W - NA_COLS)
        ok = (k >= cs) & (k < cs + NA_COLS)
        left = lane < GRID_W
        for dr in range(N_DR):
            row = jnp.broadcast_to(r_ref[0, dr:dr + 1, :], (GRID_W, 128))
            r0 = jnp.where(left, row, 0.0)
            r1 = jnp.where(left, pltpu.roll(row, GRID_W, axis=1), 0.0)
            y0 = pltpu.roll(r0, 128 - (NA_COLS - 1), axis=1, stride=1, stride_axis=0)
            y1 = pltpu.roll(r1, GRID_W - (NA_COLS - 1), axis=1, stride=1, stride_axis=0)
            o_ref[0, dr * GRID_W:(dr + 1) * GRID_W, :] = jnp.where(ok, jnp.where(left, y0, y1), NEG)

    return pl.pallas_call(
        body, name="rpb_expand", grid=(NA_PAIRS,),
        in_specs=[pl.BlockSpec((1, N_DR_PAD, 128), lambda p: (p, 0, 0))],
        out_specs=pl.BlockSpec((1, N_DR * GRID_W, 128), lambda p: (p, 0, 0)),
        out_shape=jax.ShapeDtypeStruct((NA_PAIRS, N_DR * GRID_W, 128), F32),
        compiler_params=_params("parallel"),
    )(rev)


def _na_bias_grad(db, tasks=()):
    a = np.arange(128)
    flip = jnp.asarray(((a[:, None] // GRID_W == a[None, :] // GRID_W)
                        & (a[:, None] % GRID_W + a[None, :] % GRID_W == GRID_W - 1)).astype(np.float32))

    def body(d_ref, j_ref, o_ref):
        o_ref[...] = jnp.zeros_like(o_ref)
        for dr in range(N_DR):
            t = jnp.dot(d_ref[0, dr * GRID_W:(dr + 1) * GRID_W, :], j_ref[...], precision=lax.Precision.HIGHEST, preferred_element_type=F32)
            t = pltpu.roll(t, GRID_W + NA_COLS, axis=1, stride=1, stride_axis=0)
            o_ref[0, dr:dr + 1, :] = jnp.sum(t, axis=0, keepdims=True)

    (rows,), got = _hosted_call(
        body, "rpb_reduce", (NA_PAIRS,),
        [pl.BlockSpec((1, N_DR * GRID_W, 128), lambda p: (p, 0, 0)), _full((128, 128))],
        [pl.BlockSpec((1, N_DR_PAD, 128), lambda p: (p, 0, 0))],
        [jax.ShapeDtypeStruct((NA_PAIRS, N_DR_PAD, 128), F32)], (db, flip), tasks)
    g = rows.reshape(NA_PAIRS, N_DR_PAD, 2, GRID_W)[:, :N_DR, :, :N_DC]
    return jnp.transpose(g, (0, 2, 1, 3)).reshape(-1), got


def _rope_tables(S):
    half = HEAD_DIM // 2
    inv = np.float32(ROPE_THETA) ** (-np.arange(half, dtype=np.float32) / np.float32(half))
    ang = np.arange(S).astype(np.float32)[:, None] * inv[None, :]
    cos, sin = np.cos(ang).astype(np.float32), np.sin(ang).astype(np.float32)
    return jnp.asarray(np.tile(np.concatenate([cos, cos], axis=1), (1, 2))), jnp.asarray(np.tile(np.concatenate([-sin, sin], axis=1), (1, 2)))


def _rope_spec(tps, tm=TOKEN_TILE):
    return pl.BlockSpec((tm, 2 * HEAD_DIM), lambda i: (i % tps, 0))


def _rot_half(t):
    w = t.shape[1]
    lane = lax.broadcasted_iota(jnp.int32, t.shape, 1)
    return jnp.where((lane % HEAD_DIM) < HEAD_DIM // 2, pltpu.roll(t, w - HEAD_DIM // 2, axis=1),
                     pltpu.roll(t, HEAD_DIM // 2, axis=1))


def _tok_spec(w, tm=TOKEN_TILE):
    return pl.BlockSpec((tm, w), lambda i: (i, 0))


def _mod_spec(tps, d):
    return pl.BlockSpec((1, 6, d), lambda i: (i // tps, 0, 0))


def _bstat_spec(tps, w):
    return pl.BlockSpec((1, 8, w), lambda i: (i // tps, 0, 0))


def _attn_in(x2d, mod3, g_attn, w_in, cos_t, sin_t, S, tasks=(), tm=WIDE_TILE):
    T, D = x2d.shape
    tps = S // tm

    def body(x_ref, mod_ref, g_ref, w_ref, cos_ref, sin_ref, h_ref, qkv_ref):
        xn, _ = _rms(x_ref[...])
        h = (xn * g_ref[...]) * (1.0 + mod_ref[0, 1:2, :]) + mod_ref[0, 0:1, :]
        hb = h.astype(BF16)
        h_ref[...] = hb
        proj = _nt(hb, w_ref[...])
        rb = proj[:, ROPE_LO:ROPE_LO + ROPE_WIDTH]
        reps = (1, ROPE_WIDTH // (2 * HEAD_DIM))
        rb = rb * jnp.tile(cos_ref[...], reps) + _rot_half(rb) * jnp.tile(sin_ref[...], reps)
        qkv_ref[:, 0:NA_WIDTH] = (proj[:, 0:NA_WIDTH] * Q_SCALE).astype(BF16)
        qkv_ref[:, NA_WIDTH:ROPE_LO] = proj[:, NA_WIDTH:ROPE_LO].astype(BF16)
        qkv_ref[:, ROPE_LO:ROPE_LO + SW_WIDTH] = (rb[:, 0:SW_WIDTH] * Q_SCALE).astype(BF16)
        qkv_ref[:, ROPE_LO + SW_WIDTH:ROPE_LO + ROPE_WIDTH] = rb[:, SW_WIDTH:].astype(BF16)
        qkv_ref[:, ROPE_LO + ROPE_WIDTH:] = proj[:, ROPE_LO + ROPE_WIDTH:].astype(BF16)

    return _hosted_call(
        body, "attn_in", (T // tm,),
        [_tok_spec(D, tm), _mod_spec(tps, D), _full((1, D)), _full(w_in.shape), _rope_spec(tps, tm), _rope_spec(tps, tm)],
        [_tok_spec(D, tm), _tok_spec(IN_WIDTH, tm)],
        [jax.ShapeDtypeStruct((T, D), BF16), jax.ShapeDtypeStruct((T, IN_WIDTH), BF16)],
        (x2d, mod3, g_attn, w_in, cos_t, sin_t), tasks)


def _attn_out(oa, ob, x2d, mod3, g_na, g_sw, w_out, S, tasks=(), tm=WIDE_TILE):
    T, D = x2d.shape
    tps = S // tm

    def body(oa_ref, ob_ref, x_ref, mod_ref, gna_ref, gsw_ref, w_ref, mixin_ref, mix_ref, x1_ref):
        oan, _ = _rms(oa_ref[...])
        obn, _ = _rms(ob_ref[...])
        mixin = jnp.concatenate([oan * gna_ref[...], obn * gsw_ref[...]], axis=1).astype(BF16)
        mixin_ref[...] = mixin
        mix = _nn(mixin, w_ref[...])
        mix_ref[...] = mix
        x1_ref[...] = x_ref[...] + mod_ref[0, 2:3, :] * mix

    return _hosted_call(
        body, "attn_out", (T // tm,),
        [_tok_spec(NA_WIDTH, tm), _tok_spec(SW_WIDTH, tm), _tok_spec(D, tm), _mod_spec(tps, D),
         _full((1, NA_WIDTH)), _full((1, SW_WIDTH)), _full(w_out.shape)],
        [_tok_spec(NA_WIDTH + SW_WIDTH, tm), _tok_spec(D, tm), _tok_spec(D, tm)],
        [jax.ShapeDtypeStruct((T, NA_WIDTH + SW_WIDTH), BF16), jax.ShapeDtypeStruct((T, D), F32), jax.ShapeDtypeStruct((T, D), F32)],
        (oa, ob, x2d, mod3, g_na, g_sw, w_out), tasks)


def _ffn_up(x1, mod3, g_ffn, w_up, S, tasks=(), tm=WIDE_TILE):
    T, D = x1.shape
    F = w_up.shape[0] // 2
    tps = S // tm

    def body(x1_ref, mod_ref, g_ref, w_ref, h2_ref, val_ref, gt_ref):
        xn, _ = _rms(x1_ref[...])
        h2 = ((xn * g_ref[...]) * (1.0 + mod_ref[0, 4:5, :]) + mod_ref[0, 3:4, :]).astype(BF16)
        h2_ref[...] = h2
        u = _nt(h2, w_ref[...])
        val_ref[...] = u[:, :F].astype(BF16)
        gt_ref[...] = u[:, F:].astype(BF16)

    return _hosted_call(
        body, "ffn_up", (T // tm,), [_tok_spec(D, tm), _mod_spec(tps, D), _full((1, D)), _full(w_up.shape)],
        [_tok_spec(D, tm), _tok_spec(F, tm), _tok_spec(F, tm)],
        [jax.ShapeDtypeStruct((T, D), BF16), jax.ShapeDtypeStruct((T, F), BF16), jax.ShapeDtypeStruct((T, F), BF16)],
        (x1, mod3, g_ffn, w_up), tasks)


def _halo_specs(T, tps, w):
    per = TOKEN_TILE // 8
    prev = pl.BlockSpec((8, w), lambda i: (jnp.maximum(i * per - 1, 0), 0))
    nxt = pl.BlockSpec((8, w), lambda i: (jnp.minimum((i + 1) * per, T // 8 - 1), 0))
    return prev, nxt


def _seq_shifts(cur, before, after, ti, tps):
    tm = cur.shape[0]
    row = lax.broadcasted_iota(jnp.int32, cur.shape, 0)
    before = jnp.where(ti > 0, before.astype(F32), 0.0)
    after = jnp.where(ti < tps - 1, after.astype(F32), 0.0)
    return jnp.where(row == 0, before, pltpu.roll(cur, 1, axis=0)), jnp.where(row == tm - 1, after, pltpu.roll(cur, tm - 1, axis=0))


def _ffn_down(gt, val, conv_w, conv_b, w_down, x1, mod3, g_final, target, B, S):
    T, D = x1.shape
    F = gt.shape[1]
    tps = S // TOKEN_TILE
    prev, nxt = _halo_specs(T, tps, F)

    def body(gt_ref, prev_ref, next_ref, val_ref, cw_ref, cb_ref, w_ref, x1_ref, mod_ref, gf_ref, tgt_ref,
             a_ref, act_ref, vd_ref, dx2_ref, df_ref, gstat_ref, bstat_ref):
        i = pl.program_id(0)
        g = gt_ref[...].astype(F32)
        gprev, gnext = _seq_shifts(g, prev_ref[7:8, :], next_ref[0:1, :], i % tps, tps)
        gc = gprev * cw_ref[0:1, :] + g * cw_ref[1:2, :] + gnext * cw_ref[2:3, :] + cb_ref[...]
        sig = 1.0 / (1.0 + jnp.exp(-gc))
        act = gc * sig
        val = val_ref[...].astype(F32)
        act_ref[...] = act.astype(BF16)
        vd_ref[...] = (val * (sig + act - act * sig)).astype(BF16)
        a = (act * val).astype(BF16)
        a_ref[...] = a
        f = _nn(a, w_ref[...])
        gate = mod_ref[0, 5:6, :]
        x2 = x1_ref[...] + gate * f
        xn, r = _rms(x2)
        err = xn * gf_ref[...] - tgt_ref[...]
        dy = err * (1.0 / D)
        dx2 = _rms_bwd(xn, r, dy * gf_ref[...])
        dx2_ref[...] = dx2
        df_ref[...] = (gate * dx2).astype(BF16)

        @pl.when(i == 0)
        def _():
            gstat_ref[...] = jnp.zeros_like(gstat_ref)

        @pl.when(i % tps == 0)
        def _():
            bstat_ref[...] = jnp.zeros_like(bstat_ref)

        gstat_ref[0:1, :] += jnp.sum(dy * xn, axis=0, keepdims=True)
        tile_loss = jnp.sum(jnp.sum(err * err, axis=1, keepdims=True), axis=0, keepdims=True) * (0.5 / D)
        gstat_ref[1:2, :] += jnp.broadcast_to(tile_loss, (1, D))
        bstat_ref[0, 0:1, :] += jnp.sum(dx2 * f, axis=0, keepdims=True)

    return pl.pallas_call(
        body, name="ffn_down", grid=(T // TOKEN_TILE,),
        in_specs=[_tok_spec(F), prev, nxt, _tok_spec(F), _full(conv_w.shape), _full((1, F)), _full(w_down.shape),
                  _tok_spec(D), _mod_spec(tps, D), _full((1, D)), _tok_spec(D)],
        out_specs=[_tok_spec(F), _tok_spec(F), _tok_spec(F), _tok_spec(D), _tok_spec(D), _full((8, D)), _bstat_spec(tps, D)],
        out_shape=[jax.ShapeDtypeStruct((T, F), BF16), jax.ShapeDtypeStruct((T, F), BF16), jax.ShapeDtypeStruct((T, F), BF16),
                   jax.ShapeDtypeStruct((T, D), F32), jax.ShapeDtypeStruct((T, D), BF16),
                   jax.ShapeDtypeStruct((8, D), F32), jax.ShapeDtypeStruct((B, 8, D), F32)],
        compiler_params=_params("arbitrary"),
    )(gt, gt, gt, val, conv_w, conv_b, w_down, x1, mod3, g_final, target)


def _ffn_down_bwd(df, w_down, act, vd, tasks=(), tm=WIDE_TILE):
    T, D = df.shape
    F = act.shape[1]

    def body(df_ref, w_ref, act_ref, vd_ref, dval_ref, dgc_ref, cstat_ref):
        da = _nt(df_ref[...], w_ref[...])
        dval_ref[...] = (da * act_ref[...].astype(F32)).astype(BF16)
        dgc = da * vd_ref[...].astype(F32)
        dgc_ref[...] = dgc.astype(BF16)

        @pl.when(pl.program_id(0) == 0)
        def _():
            cstat_ref[...] = jnp.zeros_like(cstat_ref)

        cstat_ref[0:1, :] += jnp.sum(dgc, axis=0, keepdims=True)

    return _hosted_call(
        body, "ffn_down_bwd", (T // tm,),
        [_tok_spec(D, tm), _full(w_down.shape), _tok_spec(F, tm), _tok_spec(F, tm)],
        [_tok_spec(F, tm), _tok_spec(F, tm), _full((8, F))],
        [jax.ShapeDtypeStruct((T, F), BF16), jax.ShapeDtypeStruct((T, F), BF16), jax.ShapeDtypeStruct((8, F), F32)],
        (df, w_down, act, vd), tasks)


def _ffn_up_bwd(dgc, dval, gt, conv_w, w_up, x1, mod3, g_ffn, dx2, mix, B, S, tasks=()):
    T, D = x1.shape
    F = dgc.shape[1]
    tps = S // TOKEN_TILE
    prev, nxt = _halo_specs(T, tps, F)

    def body(dgc_ref, prev_ref, next_ref, dval_ref, gt_ref, cw_ref, w_ref, x1_ref, mod_ref, g_ref, dx2_ref, mix_ref,
             du_ref, dx1_ref, dmix_ref, gstat_ref, bstat_ref, cstat_ref):
        i = pl.program_id(0)
        d = dgc_ref[...].astype(F32)
        dprev, dnext = _seq_shifts(d, prev_ref[7:8, :], next_ref[0:1, :], i % tps, tps)
        g = gt_ref[...].astype(F32)

        @pl.when(i == 0)
        def _():
            cstat_ref[...] = jnp.zeros_like(cstat_ref)

        cstat_ref[1:2, :] += jnp.sum(dnext * g, axis=0, keepdims=True)
        cstat_ref[2:3, :] += jnp.sum(d * g, axis=0, keepdims=True)
        cstat_ref[3:4, :] += jnp.sum(dprev * g, axis=0, keepdims=True)
        dgt = dnext * cw_ref[0:1, :] + d * cw_ref[1:2, :] + dprev * cw_ref[2:3, :]
        du = jnp.concatenate([dval_ref[...], dgt.astype(BF16)], axis=1)
        du_ref[...] = du
        dh2 = _nn(du, w_ref[...])
        xn, r = _rms(x1_ref[...])
        scale1 = 1.0 + mod_ref[0, 4:5, :]
        xg = xn * g_ref[...]
        dx1 = dx2_ref[...] + _rms_bwd(xn, r, dh2 * g_ref[...] * scale1)
        dx1_ref[...] = dx1
        dmix_ref[...] = (mod_ref[0, 2:3, :] * dx1).astype(BF16)

        @pl.when(i == 0)
        def _():
            gstat_ref[...] = jnp.zeros_like(gstat_ref)

        @pl.when(i % tps == 0)
        def _():
            bstat_ref[...] = jnp.zeros_like(bstat_ref)

        gstat_ref[0:1, :] += jnp.sum(dh2 * scale1 * xn, axis=0, keepdims=True)
        bstat_ref[0, 0:1, :] += jnp.sum(dh2, axis=0, keepdims=True)
        bstat_ref[0, 1:2, :] += jnp.sum(dh2 * xg, axis=0, keepdims=True)
        bstat_ref[0, 2:3, :] += jnp.sum(dx1 * mix_ref[...], axis=0, keepdims=True)

    return _hosted_call(
        body, "ffn_up_bwd", (T // TOKEN_TILE,),
        [_tok_spec(F), prev, nxt, _tok_spec(F), _tok_spec(F), _full(conv_w.shape), _full(w_up.shape), _tok_spec(D),
         _mod_spec(tps, D), _full((1, D)), _tok_spec(D), _tok_spec(D)],
        [_tok_spec(2 * F), _tok_spec(D), _tok_spec(D), _full((8, D)), _bstat_spec(tps, D), _full((8, F))],
        [jax.ShapeDtypeStruct((T, 2 * F), BF16), jax.ShapeDtypeStruct((T, D), F32), jax.ShapeDtypeStruct((T, D), BF16),
         jax.ShapeDtypeStruct((8, D), F32), jax.ShapeDtypeStruct((B, 8, D), F32), jax.ShapeDtypeStruct((8, F), F32)],
        (dgc, dgc, dgc, dval, gt, conv_w, w_up, x1, mod3, g_ffn, dx2, mix), tasks)


def _attn_out_bwd(dmix, w_out, oa, ob, g_na, g_sw, tasks=(), tm=WIDE_TILE):
    T, D = dmix.shape

    def body(dmix_ref, w_ref, oa_ref, ob_ref, gna_ref, gsw_ref, doa_ref, dob_ref, gstat_ref):
        dmixin = _nt(dmix_ref[...], w_ref[...])

        @pl.when(pl.program_id(0) == 0)
        def _():
            gstat_ref[...] = jnp.zeros_like(gstat_ref)

        for k, (o_ref, g_ref, do_ref) in enumerate(((oa_ref, gna_ref, doa_ref), (ob_ref, gsw_ref, dob_ref))):
            dn = dmixin[:, k * NA_WIDTH:(k + 1) * NA_WIDTH]
            on, r = _rms(o_ref[...])
            gstat_ref[k:k + 1, :] += jnp.sum(dn * on, axis=0, keepdims=True)
            do_ref[...] = _rms_bwd(on, r, dn * g_ref[...]).astype(BF16)

    hs = jax.ShapeDtypeStruct((T, NA_WIDTH), BF16)
    return _hosted_call(
        body, "attn_out_bwd", (T // tm,),
        [_tok_spec(D, tm), _full(w_out.shape), _tok_spec(NA_WIDTH, tm), _tok_spec(SW_WIDTH, tm), _full((1, NA_WIDTH)), _full((1, SW_WIDTH))],
        [_tok_spec(NA_WIDTH, tm), _tok_spec(SW_WIDTH, tm), _full((8, NA_WIDTH))],
        [hs, hs, jax.ShapeDtypeStruct((8, NA_WIDTH), F32)],
        (dmix, w_out, oa, ob, g_na, g_sw), tasks)


def _attn_in_bwd(dqa, dka, dva, dqb, dkb, dvb, cos_t, sin_t, w_in, x2d, mod3, g_attn, dx1, B, S, tm=WIDE_TILE):
    T, D = x2d.shape
    tps = S // tm

    def body(dqa_ref, dka_ref, dva_ref, dqb_ref, dkb_ref, dvb_ref, cos_ref, sin_ref, w_ref, x_ref, mod_ref, g_ref, dx1_ref,
             gx_ref, dproj_ref, gstat_ref, bstat_ref):
        i = pl.program_id(0)
        drb = jnp.concatenate([dqb_ref[...] * Q_SCALE, dkb_ref[...]], axis=1)
        reps = (1, ROPE_WIDTH // (2 * HEAD_DIM))
        drb = drb * jnp.tile(cos_ref[...], reps) + _rot_half(drb * jnp.tile(sin_ref[...], reps))
        dproj = jnp.concatenate([(dqa_ref[...] * Q_SCALE).astype(BF16), dka_ref[...].astype(BF16), dva_ref[...].astype(BF16),
                                 drb.astype(BF16), dvb_ref[...].astype(BF16)], axis=1)
        dproj_ref[...] = dproj
        dh = _nn(dproj, w_ref[...])
        xn, r = _rms(x_ref[...])
        scale1 = 1.0 + mod_ref[0, 1:2, :]
        gx_ref[...] = dx1_ref[...] + _rms_bwd(xn, r, dh * g_ref[...] * scale1)

        @pl.when(i == 0)
        def _():
            gstat_ref[...] = jnp.zeros_like(gstat_ref)

        @pl.when(i % tps == 0)
        def _():
            bstat_ref[...] = jnp.zeros_like(bstat_ref)

        gstat_ref[0:1, :] += jnp.sum(dh * scale1 * xn, axis=0, keepdims=True)
        bstat_ref[0, 0:1, :] += jnp.sum(dh, axis=0, keepdims=True)
        bstat_ref[0, 1:2, :] += jnp.sum(dh * (xn * g_ref[...]), axis=0, keepdims=True)

    rope = _rope_spec(tps, tm)
    return pl.pallas_call(
        body, name="attn_in_bwd", grid=(T // tm,),
        in_specs=[_tok_spec(NA_WIDTH, tm), _tok_spec(NA_WIDTH, tm), _tok_spec(NA_WIDTH, tm), _tok_spec(SW_WIDTH, tm),
                  _tok_spec(SW_KV_WIDTH, tm), _tok_spec(SW_KV_WIDTH, tm), rope, rope, _full(w_in.shape), _tok_spec(D, tm),
                  _mod_spec(tps, D), _full((1, D)), _tok_spec(D, tm)],
        out_specs=[_tok_spec(D, tm), _tok_spec(IN_WIDTH, tm), _full((8, D)), _bstat_spec(tps, D)],
        out_shape=[jax.ShapeDtypeStruct((T, D), F32), jax.ShapeDtypeStruct((T, IN_WIDTH), BF16),
                   jax.ShapeDtypeStruct((8, D), F32), jax.ShapeDtypeStruct((B, 8, D), F32)],
        compiler_params=_params("arbitrary"),
    )(dqa, dka, dva, dqb, dkb, dvb, cos_t, sin_t, w_in, x2d, mod3, g_attn, dx1)


def _matmul_tn(a, b, name, tm=None, tk=512):
    T, M = a.shape
    N = b.shape[1]
    tm = M if tm is None else tm
    nk = T // tk

    def body(a_ref, b_ref, o_ref, acc):
        k = pl.program_id(1)

        @pl.when(k == 0)
        def _():
            acc[...] = jnp.zeros_like(acc)

        acc[...] += _tn(a_ref[...], b_ref[...])

        @pl.when(k == nk - 1)
        def _():
            o_ref[...] = acc[...].astype(BF16)

    return pl.pallas_call(
        body, name=name, grid=(M // tm, nk),
        in_specs=[pl.BlockSpec((tk, tm), lambda i, k: (k, i)), pl.BlockSpec((tk, N), lambda i, k: (k, 0))],
        out_specs=pl.BlockSpec((tm, N), lambda i, k: (i, 0)),
        out_shape=jax.ShapeDtypeStruct((M, N), BF16),
        scratch_shapes=[pltpu.VMEM((tm, N), F32)],
        compiler_params=_params("parallel", "arbitrary"),
    )(a, b)


def _na_geometry(S):
    rows = S // GRID_W
    wr = min(NA_ROWS_MAX, rows)
    return rows, wr


def _na_window(r, rows, wr):
    rs = jnp.clip(r - wr // 2, 0, rows - wr)
    return pl.multiple_of(rs * GRID_W, GRID_W), pl.multiple_of((rs - r + NA_ROWS_MAX - 1) * GRID_W, GRID_W)


NA_STEP_PAIRS = 2
NA_GW = NA_STEP_PAIRS * 128
NA_BWD_ROWS = 4
NA_ROWS_PER_STEP = 4


def _na_specs(S, kw_n, order):
    ng = NA_PAIRS // NA_STEP_PAIRS

    def col(k):
        return pl.BlockSpec((1, S, NA_GW), lambda *ids: (order(*ids)[0], 0, k * ng + order(*ids)[1]))
    bias = pl.BlockSpec((NA_STEP_PAIRS, N_DR * GRID_W, 128), lambda *ids: (order(*ids)[1], 0, 0))
    out = pl.BlockSpec((1, S, NA_GW), lambda *ids: (order(*ids)[0], 0, order(*ids)[1]))
    return col(0), col(1), col(2), bias, out


def _block_diag(t):
    left = lax.broadcasted_iota(jnp.int32, t.shape, 1) < HEAD_DIM
    zero = jnp.zeros_like(t)
    return jnp.concatenate([jnp.where(left, t, zero), jnp.where(left, zero, t)], axis=0)


def _diag_blocks(res):
    left = lax.broadcasted_iota(jnp.int32, (HEAD_DIM, 128), 1) < HEAD_DIM
    return jnp.where(left, res[:HEAD_DIM], res[HEAD_DIM:])


def _col_softmax(st):
    e = jnp.exp(st - jnp.max(st, axis=0, keepdims=True))
    return e * (1.0 / jnp.sum(e, axis=0, keepdims=True))


def _na_fwd(qkv, bias, tasks=()):
    B, S, _ = qkv.shape
    rows, wr = _na_geometry(S)
    kw_n = wr * GRID_W

    def body(q_ref, k_ref, v_ref, b_ref, o_ref):
        def step(it, carry):
            win = [_na_window(it * NA_ROWS_PER_STEP + u, rows, wr) for u in range(NA_ROWS_PER_STEP)]
            qrows = [pl.ds(pl.multiple_of((it * NA_ROWS_PER_STEP + u) * GRID_W, GRID_W), GRID_W) for u in range(NA_ROWS_PER_STEP)]
            krows = [pl.ds(w[0], kw_n) for w in win]
            brows = [pl.ds(w[1], kw_n) for w in win]
            lanes = [pl.ds(p * 128, 128) for p in range(NA_STEP_PAIRS)]
            chains = [(u, p) for u in range(NA_ROWS_PER_STEP) for p in range(NA_STEP_PAIRS)]
            st = {(u, p): _nt(k_ref[0, krows[u], lanes[p]], _block_diag(q_ref[0, qrows[u], lanes[p]])) for u, p in chains}
            pn = {(u, p): _col_softmax(st[(u, p)] + b_ref[p, brows[u], :]).astype(BF16) for u, p in chains}
            out = {(u, p): _diag_blocks(_tn(pn[(u, p)], v_ref[0, krows[u], lanes[p]])) for u, p in chains}
            for u in range(NA_ROWS_PER_STEP):
                o_ref[0, qrows[u], :] = jnp.concatenate([out[(u, p)] for p in range(NA_STEP_PAIRS)], axis=1)
            return carry

        lax.fori_loop(0, rows // NA_ROWS_PER_STEP, step, 0)

    q, k, v, bs, out = _na_specs(S, kw_n, lambda b, g: (b, g))
    return _hosted_call(body, "na_fwd", (B, NA_PAIRS // NA_STEP_PAIRS), [q, k, v, bs], [out],
                        [jax.ShapeDtypeStruct((B, S, NA_WIDTH), F32)], (qkv, qkv, qkv, bias), tasks)


def _na_bwd(qkv, bias, doa, tasks=()):
    B, S, _ = qkv.shape
    rows, wr = _na_geometry(S)
    kw_n = wr * GRID_W

    def body(q_ref, k_ref, v_ref, b_ref, do_ref, dq_ref, dk_ref, dv_ref, db_ref, dk_acc, dv_acc):
        @pl.when(pl.program_id(1) == 0)
        def _():
            db_ref[...] = jnp.zeros_like(db_ref)

        dk_acc[...] = jnp.zeros_like(dk_acc)
        dv_acc[...] = jnp.zeros_like(dv_acc)

        def step(it, carry):
            nu, pairs = range(NA_BWD_ROWS), range(NA_STEP_PAIRS)
            win = [_na_window(it * NA_BWD_ROWS + u, rows, wr) for u in nu]
            qrows = [pl.ds(pl.multiple_of((it * NA_BWD_ROWS + u) * GRID_W, GRID_W), GRID_W) for u in nu]
            krows = [pl.ds(w[0], kw_n) for w in win]
            brows = [pl.ds(w[1], kw_n) for w in win]
            lanes = [pl.ds(p * 128, 128) for p in pairs]
            chains = [(u, p) for u in nu for p in pairs]
            kp = {(u, p): k_ref[0, krows[u], lanes[p]] for u, p in chains}
            qbd = {(u, p): _block_diag(q_ref[0, qrows[u], lanes[p]]) for u, p in chains}
            dobd = {(u, p): _block_diag(do_ref[0, qrows[u], lanes[p]]) for u, p in chains}
            st = {c: _nt(kp[c], qbd[c]) for c in chains}
            dpt = {(u, p): _nt(v_ref[0, krows[u], lanes[p]], dobd[(u, p)]) for u, p in chains}
            pn = {(u, p): _col_softmax(st[(u, p)] + b_ref[p, brows[u], :]) for u, p in chains}
            dst = {c: pn[c] * (dpt[c] - jnp.sum(pn[c] * dpt[c], axis=0, keepdims=True)) for c in chains}
            dsb = {c: dst[c].astype(BF16) for c in chains}
            dq = {c: _diag_blocks(_tn(dsb[c], kp[c])) for c in chains}
            dk = {c: _nn(dsb[c], qbd[c]) for c in chains}
            dv = {c: _nn(pn[c].astype(BF16), dobd[c]) for c in chains}
            for u in nu:
                dq_ref[0, qrows[u], :] = jnp.concatenate([dq[(u, p)] for p in pairs], axis=1).astype(BF16)
                dk_acc[krows[u], :] += jnp.concatenate([dk[(u, p)] for p in pairs], axis=1)
                dv_acc[krows[u], :] += jnp.concatenate([dv[(u, p)] for p in pairs], axis=1)
                for p in pairs:
                    db_ref[p, brows[u], :] += dst[(u, p)]
            return carry

        lax.fori_loop(0, rows // NA_BWD_ROWS, step, 0)

        def emit(i, carry):
            r = pl.ds(pl.multiple_of(i * 256, 256), 256)
            dk_ref[0, r, :] = dk_acc[r, :].astype(BF16)
            dv_ref[0, r, :] = dv_acc[r, :].astype(BF16)
            return carry

        lax.fori_loop(0, S // 256, emit, 0)

    q, k, v, bs, out = _na_specs(S, kw_n, lambda g, b: (b, g))
    hs = jax.ShapeDtypeStruct((B, S, NA_WIDTH), BF16)
    return _hosted_call(body, "na_bwd", (NA_PAIRS // NA_STEP_PAIRS, B), [q, k, v, bs, out], [out, out, out, bs],
                        [hs, hs, hs, jax.ShapeDtypeStruct((NA_PAIRS, N_DR * GRID_W, 128), F32)], (qkv, qkv, qkv, bias, doa), tasks,
                        scratch_shapes=[pltpu.VMEM((S, NA_GW), F32), pltpu.VMEM((S, NA_GW), F32)])


SW_PAIRS = SW_HEADS // 2


def _sw_band(n, S):
    kw_n = 3 * SW_BLOCK
    start = pl.multiple_of(jnp.clip(n * SW_BLOCK - SW_BLOCK, 0, S - kw_n), SW_BLOCK)
    kpos = start + lax.broadcasted_iota(jnp.int32, (kw_n, SW_BLOCK), 0)
    qpos = n * SW_BLOCK + lax.broadcasted_iota(jnp.int32, (kw_n, SW_BLOCK), 1)
    return start, jnp.abs(qpos - kpos) <= SW_WINDOW


def _kv_halves(t):
    left = lax.broadcasted_iota(jnp.int32, t.shape, 1) < HEAD_DIM
    swapped = pltpu.roll(t, HEAD_DIM, axis=1)
    zero = jnp.zeros_like(t)
    return {(0, 0): jnp.where(left, t, zero), (0, 1): jnp.where(left, zero, swapped),
            (1, 0): jnp.where(left, swapped, zero), (1, 1): jnp.where(left, zero, t)}


def _sw_probs(st, ok, sk):
    st = jnp.where(ok, st, NEG)
    m = jnp.maximum(jnp.max(st, axis=0, keepdims=True), sk)
    e = jnp.exp(st - m)
    esk = jnp.exp(sk - m)
    inv = 1.0 / (jnp.sum(e, axis=0, keepdims=True) + esk)
    return e * inv, esk * inv


def _sw_specs(S):
    q = pl.BlockSpec((1, S, SW_WIDTH), lambda b, *_: (b, 0, ROPE_LO // SW_WIDTH))
    k = pl.BlockSpec((1, S, SW_KV_WIDTH), lambda b, *_: (b, 0, (ROPE_LO + SW_WIDTH) // SW_KV_WIDTH))
    v = pl.BlockSpec((1, S, SW_KV_WIDTH), lambda b, *_: (b, 0, (ROPE_LO + ROPE_WIDTH) // SW_KV_WIDTH))
    return q, k, v


SW_FWD_SPLIT = 2


def _sw_fwd(sink, qkv, tasks=()):
    B, S, _ = qkv.shape
    kw_n = 3 * SW_BLOCK

    def body(sink_ref, q_ref, k_ref, v_ref, o_ref):
        def step(n, carry):
            start, ok = _sw_band(n, S)
            qrows = pl.ds(pl.multiple_of(n * SW_BLOCK, SW_BLOCK), SW_BLOCK)
            krows = pl.ds(start, kw_n)
            kh, vh = _kv_halves(k_ref[0, krows, :]), _kv_halves(v_ref[0, krows, :])
            heads = [(p, e) for p in range(SW_PAIRS) for e in range(2)]
            qp = [q_ref[0, qrows, pl.ds(p * 128, 128)] for p in range(SW_PAIRS)]
            kv_of = lambda p: p // (SW_PAIRS // SW_KV_HEADS)
            st = {(p, e): _nt(kh[(kv_of(p), e)], qp[p]) for p, e in heads}
            pn = {(p, e): _sw_probs(st[(p, e)], ok, sink_ref[2 * p + e])[0].astype(BF16) for p, e in heads}
            outs = [_tn(pn[(p, 0)], vh[(kv_of(p), 0)]) + _tn(pn[(p, 1)], vh[(kv_of(p), 1)]) for p in range(SW_PAIRS)]
            o_ref[0, qrows, :] = jnp.concatenate(outs, axis=1)
            return carry

        half = (S // SW_BLOCK) // SW_FWD_SPLIT
        lax.fori_loop(pl.program_id(1) * half, (pl.program_id(1) + 1) * half, step, 0)

    q, k, v = _sw_specs(S)
    return _hosted_call(
        body, "sw_fwd", (B, SW_FWD_SPLIT), [pl.BlockSpec(memory_space=pltpu.SMEM), q, k, v],
        [pl.BlockSpec((1, S, SW_WIDTH), lambda b, s: (b, 0, 0))], [jax.ShapeDtypeStruct((B, S, SW_WIDTH), F32)],
        (sink, qkv, qkv, qkv), tasks)


def _sw_bwd(sink, qkv, dob):
    B, S, _ = qkv.shape
    kw_n = 3 * SW_BLOCK

    fold_rows = 256

    def body(sink_ref, q_ref, k_ref, v_ref, do_ref, dq_ref, dk_ref, dv_ref, dsink_ref, dk_acc, dv_acc):
        @pl.when(pl.program_id(0) == 0)
        def _():
            dsink_ref[...] = jnp.zeros_like(dsink_ref)

        dk_acc[...] = jnp.zeros_like(dk_acc)
        dv_acc[...] = jnp.zeros_like(dv_acc)
        ppk = SW_PAIRS // SW_KV_HEADS

        def step(n, carry):
            start, ok = _sw_band(n, S)
            qrows = pl.ds(pl.multiple_of(n * SW_BLOCK, SW_BLOCK), SW_BLOCK)
            krows = pl.ds(start, kw_n)
            kh, vh = _kv_halves(k_ref[0, krows, :]), _kv_halves(v_ref[0, krows, :])
            heads = [(p, e) for p in range(SW_PAIRS) for e in range(2)]
            qp = [q_ref[0, qrows, pl.ds(p * 128, 128)] for p in range(SW_PAIRS)]
            dop = [do_ref[0, qrows, pl.ds(p * 128, 128)] for p in range(SW_PAIRS)]
            st = {(p, e): _nt(kh[(p // ppk, e)], qp[p]) for p, e in heads}
            dpt = {(p, e): _nt(vh[(p // ppk, e)], dop[p]) for p, e in heads}
            pnb, dsb = {}, {}
            for p, e in heads:
                pn, psink = _sw_probs(st[(p, e)], ok, sink_ref[2 * p + e])
                delta = jnp.sum(pn * dpt[(p, e)], axis=0, keepdims=True)
                dsb[(p, e)] = (pn * (dpt[(p, e)] - delta)).astype(BF16)
                pnb[(p, e)] = pn.astype(BF16)
                dsink_ref[2 * p + e:2 * p + e + 1, :] += -(psink * delta)
            dq_ref[0, qrows, :] = jnp.concatenate(
                [_tn(dsb[(p, 0)], kh[(p // ppk, 0)]) + _tn(dsb[(p, 1)], kh[(p // ppk, 1)]) for p in range(SW_PAIRS)], axis=1)
            left = lax.broadcasted_iota(jnp.int32, (kw_n, 128), 1) < HEAD_DIM
            dks, dvs = [], []
            for kv in range(SW_KV_HEADS):
                dk = dv = None
                for p in range(kv * ppk, (kv + 1) * ppk):
                    dk_p = jnp.where(left, _nn(dsb[(p, 0)], qp[p]), _nn(dsb[(p, 1)], qp[p]))
                    dv_p = jnp.where(left, _nn(pnb[(p, 0)], dop[p]), _nn(pnb[(p, 1)], dop[p]))
                    dk = dk_p if dk is None else dk + dk_p
                    dv = dv_p if dv is None else dv + dv_p
                dks.append(dk)
                dvs.append(dv)
            dk_acc[krows, :] += jnp.concatenate(dks, axis=1)
            dv_acc[krows, :] += jnp.concatenate(dvs, axis=1)
            return carry

        lax.fori_loop(0, S // SW_BLOCK, step, 0)

        def fold(i, carry):
            rows = pl.ds(pl.multiple_of(i * fold_rows, fold_rows), fold_rows)
            left = lax.broadcasted_iota(jnp.int32, (fold_rows, 128), 1) < HEAD_DIM
            for acc, out_ref in ((dk_acc, dk_ref), (dv_acc, dv_ref)):
                a, b = acc[rows, 0:128], acc[rows, 128:256]
                out_ref[0, rows, :] = jnp.where(left, a + pltpu.roll(a, HEAD_DIM, axis=1), b + pltpu.roll(b, HEAD_DIM, axis=1))
            return carry

        lax.fori_loop(0, S // fold_rows, fold, 0)

        @pl.when(pl.program_id(0) == B - 1)
        def _():
            dsink_ref[...] = jnp.broadcast_to(jnp.sum(dsink_ref[...], axis=1, keepdims=True), dsink_ref.shape)

    q, k, v = _sw_specs(S)
    qo = pl.BlockSpec((1, S, SW_WIDTH), lambda b: (b, 0, 0))
    ko = pl.BlockSpec((1, S, SW_KV_WIDTH), lambda b: (b, 0, 0))
    return pl.pallas_call(
        body, name="sw_bwd", grid=(B,),
        in_specs=[pl.BlockSpec(memory_space=pltpu.SMEM), q, k, v, qo],
        out_specs=[qo, ko, ko, _full((SW_HEADS, 128))],
        out_shape=[jax.ShapeDtypeStruct((B, S, SW_WIDTH), F32), jax.ShapeDtypeStruct((B, S, SW_KV_WIDTH), F32),
                   jax.ShapeDtypeStruct((B, S, SW_KV_WIDTH), F32), jax.ShapeDtypeStruct((SW_HEADS, 128), F32)],
        scratch_shapes=[pltpu.VMEM((S, 2 * SW_KV_WIDTH), F32), pltpu.VMEM((S, 2 * SW_KV_WIDTH), F32)],
        compiler_params=_params("arbitrary"),
    )(sink, qkv, qkv, qkv, dob)


def _pack_sum_adamw(packs, params):
    W = packs.shape[1]
    n_p = len(params)

    def body(p_ref, *refs):
        ins, tot_ref, outs = refs[:3 * n_p], refs[3 * n_p], refs[3 * n_p + 1:]
        tot = p_ref[0:8, :]
        for d in range(1, N_DEV):
            tot = tot + p_ref[8 * d:8 * d + 8, :]
        tot_ref[...] = tot
        for i, (w, _, _, rows, off) in enumerate(params):
            n = w.shape[1]
            g = tot[rows[0]:rows[0] + 1, off:off + n]
            for r in rows[1:]:
                g = g + tot[r:r + 1, off:off + n]
            w_ref, m_ref, v_ref = ins[3 * i:3 * i + 3]
            g_ref, d_ref, nm_ref, nv_ref = outs[4 * i:4 * i + 4]
            g_ref[...] = g
            d_ref[...], nm_ref[...], nv_ref[...] = _adam_update(w_ref[...], g, m_ref[...], v_ref[...])

    res = pl.pallas_call(
        body, name="small_adamw",
        out_shape=[jax.ShapeDtypeStruct((8, W), F32)] + [jax.ShapeDtypeStruct(p[0].shape, F32) for p in params for _ in range(4)],
        compiler_params=pltpu.CompilerParams(vmem_limit_bytes=VMEM_LIMIT),
    )(packs, *[a for p in params for a in p[:3]])
    return res[0], [res[1 + 4 * i:5 + 4 * i] for i in range(n_p)]


def _adam_update(w, g, m, v):
    c1 = 1.0 - ADAM_B1 ** ADAM_STEP
    c2 = 1.0 - ADAM_B2 ** ADAM_STEP
    nm = ADAM_B1 * m + (1.0 - ADAM_B1) * g
    nv = ADAM_B2 * v + (1.0 - ADAM_B2) * (g * g)
    return -ADAM_LR * ((nm / c1) / (jnp.sqrt(nv / c2) + ADAM_EPS) + ADAM_WD * w), nm, nv


def _adamw(w, g, m, v, name):
    def body(w_ref, g_ref, m_ref, v_ref, d_ref, nm_ref, nv_ref):
        d_ref[...], nm_ref[...], nv_ref[...] = _adam_update(w_ref[...], g_ref[...], m_ref[...], v_ref[...])

    s = jax.ShapeDtypeStruct(w.shape, F32)
    return pl.pallas_call(body, name=name, out_shape=[s, s, s],
                          compiler_params=pltpu.CompilerParams(vmem_limit_bytes=VMEM_LIMIT))(w, g, m, v)


def _sum_adamw_rows(R):
    return max(r for r in range(16, min(R, 256) + 1, 16) if R % r == 0)


def _sum_adamw_steps(R):
    return R // _sum_adamw_rows(R)


def _sum_first(own, recvb, name, tasks):
    R, C = own.shape
    rc = _sum_adamw_rows(R)

    def body(own_ref, r_ref, p_ref):
        p_ref[...] = (own_ref[...] + r_ref[0].astype(F32)) + r_ref[1].astype(F32)

    blk = pl.BlockSpec((rc, C), lambda i: (i, 0))
    (part,), got = _hosted_call(body, name, (R // rc,), [blk, pl.BlockSpec((2, rc, C), lambda i: (0, i, 0))], [blk],
                                [jax.ShapeDtypeStruct((R, C), F32)], (own, recvb), tasks)
    return part, got


def _sum_adamw(own, recvb, w, m, v, name, done=0):
    R, C = own.shape
    rc = _sum_adamw_rows(R)
    left = 3 - done
    assert 3 % left == 0

    def body(own_ref, r_ref, w_ref, m_ref, v_ref, g_ref, d_ref, nm_ref, nv_ref):
        g = own_ref[...]
        for j in range(left):
            g = g + r_ref[j].astype(F32)
        g_ref[...] = g
        d_ref[...], nm_ref[...], nv_ref[...] = _adam_update(w_ref[...], g, m_ref[...], v_ref[...])

    blk = pl.BlockSpec((rc, C), lambda i: (i, 0))
    s = jax.ShapeDtypeStruct((R, C), F32)
    return pl.pallas_call(
        body, name=name, grid=(R // rc,),
        in_specs=[blk, pl.BlockSpec((left, rc, C), lambda i: (done // left, i, 0)), blk, blk, blk],
        out_specs=[blk, blk, blk, blk], out_shape=[s, s, s, s], compiler_params=_params("parallel"),
    )(own, recvb, w, m, v)


def _by_device(dw):
    return dw.reshape(N_DEV, dw.shape[0] // N_DEV, dw.shape[1])


def _local_step(x, mod, g_attn, w_in, bias, sw_sink, g_na_out, g_sw_out, w_out, g_ffn, w_up, conv_w, conv_b, w_down,
                g_final, target, sharded):
    B, S, D = x.shape
    T = B * S
    x2d = x.reshape(T, D)
    mod3 = mod.reshape(B, 6, D)
    cos_t, sin_t = _rope_tables(S)
    sink = sw_sink.reshape(SW_HEADS)
    n_tiles = T // WIDE_TILE
    full = lambda g: g.reshape(N_DEV * g.shape[1], g.shape[2])

    rider = lambda w, mid, lo, n, into=None: [_gather_task(w, mid, rows=(lo, n), into=into)] if sharded else []
    if sharded:
        qu, hd = w_up.shape[0] // 4, w_down.shape[0] // 2
    (h, qkv), got = _attn_in(x2d, mod3, g_attn, w_in, cos_t, sin_t, S,
                             [_gather_task(w_out, n_tiles // 2)] + rider(w_up, n_tiles - 1, 0, qu) if sharded else [])
    if sharded:
        w_out, w_up_buf = full(got[0][0]), got[1][0]
    qkv3 = qkv.reshape(B, S, IN_WIDTH)
    na_steps, sw_steps = B * (NA_PAIRS // NA_STEP_PAIRS), B * SW_FWD_SPLIT
    (oa,), got = _na_fwd(qkv3, bias, rider(w_up, na_steps - 1, qu, 2 * qu, w_up_buf) if sharded else [])
    if sharded:
        w_up_buf = got[0][0]
    oa = oa.reshape(T, NA_WIDTH)
    (ob,), got = _sw_fwd(sink, qkv3, rider(w_up, sw_steps // 2, 3 * qu, qu, w_up_buf) if sharded else [])
    if sharded:
        w_up = full(got[0][0])
    ob = ob.reshape(T, SW_WIDTH)
    (mixin, mix, x1), _ = _attn_out(oa, ob, x2d, mod3, g_na_out, g_sw_out, w_out, S)
    (h2, val, gt), got = _ffn_up(x1, mod3, g_ffn, w_up, S, rider(w_down, 3 * n_tiles // 4, 0, 2 * hd) if sharded else [])
    if sharded:
        w_down = full(got[0][0])
    a, act, vd, dx2, df, gstat_f, bstat_f = _ffn_down(gt, val, conv_w, conv_b, w_down, x1, mod3, g_final, target.reshape(T, D), B, S)
    F = val.shape[1]

    dw_down = _matmul_tn(a, df, "dw_down")
    (dval, dgc, cstat), got = _ffn_down_bwd(df, w_down, act, vd, [_swap_task(_by_device(dw_down))] if sharded else [])
    if sharded:
        send_down, own_down = _chip_sums(_by_device(dw_down), got[0][0])
    (du, dx1, dmix, gstat_u, bstat_u, cstat_w), got = _ffn_up_bwd(dgc, dval, gt, conv_w, w_up, x1, mod3, g_ffn, dx2, mix, B, S,
                                                                  [_exchange_task(send_down)] if sharded else [])
    if sharded:
        dw_down = (own_down, got[0][0])
    dw_up = _matmul_tn(du, h2, "dw_up", tm=F)
    dw_out = _matmul_tn(mixin, dmix, "dw_out")
    (doa, dob, gstat_o), got = _attn_out_bwd(dmix, w_out, oa, ob, g_na_out, g_sw_out,
                                             [_swap_task(_by_device(dw_up)), _swap_task(_by_device(dw_out))] if sharded else [])
    if sharded:
        send_up, own_up = _chip_sums(_by_device(dw_up), got[0][0])
        send_out, own_out = _chip_sums(_by_device(dw_out), got[1][0])
    (dqa, dka, dva, dbt), got = _na_bwd(qkv3, bias, doa.reshape(B, S, NA_WIDTH),
                                        [_exchange_task(send_up), _exchange_task(send_out)] if sharded else [])
    if sharded:
        dw_up, dw_out = (own_up, got[0][0]), (own_out, got[1][0])
    dqb, dkb, dvb, dsink = _sw_bwd(sink, qkv3, dob.reshape(B, S, SW_WIDTH))
    r2 = lambda t: t.reshape(T, t.shape[-1])
    grad_x, dproj, gstat_i, bstat_i = _attn_in_bwd(r2(dqa), r2(dka), r2(dva), r2(dqb), r2(dkb), r2(dvb), cos_t, sin_t, w_in, x2d, mod3,
                                                   g_attn, dx1, B, S)
    dw_in = _matmul_tn(dproj, h, "dw_in")

    dmod = jnp.stack([bstat_i[:, 0], bstat_i[:, 1], bstat_u[:, 2], bstat_u[:, 0], bstat_u[:, 1], bstat_f[:, 0]], axis=1)
    small = dict(g_attn=gstat_i[0], g_ffn=gstat_u[0], g_final=gstat_f[0], loss=gstat_f[1, 0], g_na_out=gstat_o[0], g_sw_out=gstat_o[1],
                 sw_sink=dsink[:, 0], conv_b=cstat[0], conv_w=cstat_w[1:4], dbt=dbt,
                 raw=(bstat_i, bstat_u, bstat_f, gstat_i, gstat_u, gstat_f, gstat_o, dsink, cstat, cstat_w))
    return grad_x.reshape(B, S, D), dict(w_in=dw_in, w_out=dw_out, w_up=dw_up, w_down=dw_down), dmod, small


def _pack_slab(raw, drpb):
    D, F = raw[3].shape[1], raw[8].shape[1]
    n_seq = raw[0].shape[0]

    def body(bi_ref, bu_ref, bf_ref, gi_ref, gu_ref, gf_ref, go_ref, ds_ref, cs_ref, cw_ref, rp_ref, o_ref):
        o_ref[...] = jnp.zeros_like(o_ref)
        for b in range(n_seq):
            mods = (bi_ref[b, 0:1, :], bi_ref[b, 1:2, :], bu_ref[b, 2:3, :], bu_ref[b, 0:1, :], bu_ref[b, 1:2, :], bf_ref[b, 0:1, :])
            for k, row in enumerate(mods):
                o_ref[b:b + 1, k * D:(k + 1) * D] = row
        o = 0
        for row in (gi_ref[0:1, :], gu_ref[0:1, :], gf_ref[0:1, :], go_ref[0:1, :], go_ref[1:2, :]):
            o_ref[2:3, o:o + row.shape[1]] = row
            o += row.shape[1]
        ds = ds_ref[...]
        eye = lax.broadcasted_iota(jnp.int32, ds.shape, 0) == lax.broadcasted_iota(jnp.int32, ds.shape, 1)
        o_ref[2:3, o:o + 128] = jnp.sum(jnp.where(eye, ds, 0.0), axis=0, keepdims=True)
        o_ref[2:3, o + 128:o + 256] = gf_ref[1:2, 0:128]
        o_ref[3:4, 0:F] = cs_ref[0:1, :]
        o_ref[4:5, 0:rp_ref.shape[1]] = rp_ref[...]
        o_ref[5:8, 0:F] = cw_ref[1:4, :]

    return pl.pallas_call(body, name="pack_slab", out_shape=jax.ShapeDtypeStruct((8, PACK_W), F32),
                          compiler_params=pltpu.CompilerParams(vmem_limit_bytes=VMEM_LIMIT))(*raw, drpb)


def kernel(x, c, w_ada, b_ada, g_attn, w_in, na_rpb, sw_sink, g_na_out, g_sw_out, w_out, g_ffn, w_up, conv_w, conv_b, w_down, g_final, loss_target, m_w_ada, m_b_ada, m_g_attn, m_w_in, m_na_rpb, m_sw_sink, m_g_na_out, m_g_sw_out, m_w_out, m_g_ffn, m_w_up, m_conv_w, m_conv_b, m_w_down, m_g_final, v_w_ada, v_b_ada, v_g_attn, v_w_in, v_na_rpb, v_sw_sink, v_g_na_out, v_g_sw_out, v_w_out, v_g_ffn, v_w_up, v_conv_w, v_conv_b, v_w_down, v_g_final):
    B, S, D = x.shape
    me = 4 * lax.axis_index("x") + 2 * lax.axis_index("y") + lax.axis_index("c")
    ada_c = w_ada.shape[2]
    F_l = conv_w.shape[2]

    cw_l = jnp.pad(conv_w[0], ((0, 8 - conv_w.shape[1]), (0, 0)))
    c_l = jnp.pad(c, ((0, 8 - B), (0, 0)))
    tr = {"w_in", "w_up"}
    w_in_t = jnp.transpose(w_in[0])
    shards = dict(w_out=w_out[0].astype(BF16), w_up=jnp.transpose(w_up[0]).astype(BF16), w_down=w_down[0].astype(BF16))

    b_ada_l = lax.dynamic_slice(b_ada, (0, me * ada_c), (1, ada_c))
    slabs, mod_all, w_in_all = _ada_fwd(jnp.concatenate([c_l, cw_l], axis=1), w_ada[0], b_ada_l, w_in_t.astype(BF16), B)
    c_all = slabs[:, :, :D].reshape(N_DEV * 8, D)
    conv_w_f = jnp.transpose(slabs[:, :3, D:], (1, 0, 2)).reshape(3, N_DEV * F_l)
    mod_mine = lax.dynamic_slice(mod_all, (0, me * B, 0), (N_DEV, B, ada_c))
    mod = jnp.transpose(mod_mine, (1, 0, 2)).reshape(B, N_DEV * ada_c)
    w_in_f = w_in_all.reshape(N_DEV * w_in_t.shape[0], D)

    bias = _na_bias_table(na_rpb[0])

    grad_x, dw, dmod, small = _local_step(x, mod, g_attn, w_in_f, bias, sw_sink, g_na_out, g_sw_out, shards["w_out"], g_ffn,
                                          shards["w_up"], conv_w_f, conv_b, shards["w_down"], g_final.reshape(1, D), loss_target,
                                          sharded=True)
    g8_in = _by_device(dw["w_in"])
    drpb, got = _na_bias_grad(small["dbt"], [_swap_task(g8_in)])
    send_in, own_in = _chip_sums(g8_in, got[0][0])

    slab = _pack_slab(small["raw"], drpb.reshape(1, -1))
    weights = dict(w_ada=w_ada, b_ada=b_ada, g_attn=g_attn, w_in=w_in, na_rpb=na_rpb, sw_sink=sw_sink, g_na_out=g_na_out,
                   g_sw_out=g_sw_out, w_out=w_out, g_ffn=g_ffn, w_up=w_up, conv_w=conv_w, conv_b=conv_b, w_down=w_down, g_final=g_final)
    ms = dict(w_ada=m_w_ada, b_ada=m_b_ada, g_attn=m_g_attn, w_in=m_w_in, na_rpb=m_na_rpb, sw_sink=m_sw_sink, g_na_out=m_g_na_out,
              g_sw_out=m_g_sw_out, w_out=m_w_out, g_ffn=m_g_ffn, w_up=m_w_up, conv_w=m_conv_w, conv_b=m_conv_b, w_down=m_w_down, g_final=m_g_final)
    vs = dict(w_ada=v_w_ada, b_ada=v_b_ada, g_attn=v_g_attn, w_in=v_w_in, na_rpb=v_na_rpb, sw_sink=v_sw_sink, g_na_out=v_g_na_out,
              g_sw_out=v_g_sw_out, w_out=v_w_out, g_ffn=v_g_ffn, w_up=v_w_up, conv_w=v_conv_w, conv_b=v_conv_b, w_down=v_w_down, g_final=v_g_final)
    names = list(weights)
    grads, deltas, new_m, new_v = {}, {}, {}, {}
    flat = lambda t: t.reshape(1, -1)

    def shard2d(nm):
        if nm in tr:
            return (lambda t: jnp.transpose(t[0])), (lambda t: jnp.transpose(t)[None])
        return (lambda t: t[0]), (lambda t: t[None])

    def finish_sum(nm, own, recvb, done=0):
        r, back = shard2d(nm)
        g2, d_, m_, v_ = _sum_adamw(own, recvb, r(weights[nm]), r(ms[nm]), r(vs[nm]), "adamw_" + nm, done)
        grads[nm], deltas[nm], new_m[nm], new_v[nm] = back(g2), back(d_), back(m_), back(v_)

    own_up, recv_up = dw["w_up"]
    n_up = _sum_adamw_steps(own_up.shape[0])
    part_up, got = _sum_first(own_up, recv_up, "sum_first_w_up", [_exchange_task(send_in), _gather_task(slab, n_up - 1)])
    packs = got[1][0]
    finish_sum("w_up", part_up, recv_up, done=2)
    finish_sum("w_down", *dw["w_down"])
    finish_sum("w_in", own_in, got[0][0])
    finish_sum("w_out", *dw["w_out"])

    where = dict(b_ada=((0, 1), 0), g_attn=((2,), 0), g_ffn=((2,), D), g_final=((2,), 2 * D), g_na_out=((2,), 3 * D),
                 g_sw_out=((2,), 3 * D + NA_WIDTH), sw_sink=((2,), 3 * D + NA_WIDTH + SW_WIDTH), conv_b=((3,), 0), na_rpb=((4,), 0))
    tot, small_out = _pack_sum_adamw(packs.reshape(N_DEV * 8, PACK_W),
                                     [(flat(weights[n]), flat(ms[n]), flat(vs[n])) + where[n] for n in where])
    for n, (g_, d_, m_, v_) in zip(where, small_out):
        shp = weights[n].shape
        grads[n], deltas[n], new_m[n], new_v[n] = g_.reshape(shp), d_.reshape(shp), m_.reshape(shp), v_.reshape(shp)
    loss = tot[2, 3 * D + NA_WIDTH + SW_WIDTH + 128]

    dmod_cols = lax.dynamic_slice(packs.reshape(N_DEV * 8, PACK_W), (0, me * ada_c), (N_DEV * 8, ada_c))
    for nm, g2 in (("w_ada", _ada_bwd(c_all, dmod_cols)), ("conv_w", lax.dynamic_slice(tot[5:8], (0, me * F_l), (3, F_l)))):
        r, back = shard2d(nm)
        d_, m_, v_ = _adamw(r(weights[nm]), g2, r(ms[nm]), r(vs[nm]), "adamw_" + nm)
        grads[nm], deltas[nm], new_m[nm], new_v[nm] = back(g2), back(d_), back(m_), back(v_)
    return (loss, grad_x, *[grads[n] for n in names], *[deltas[n] for n in names], *[new_m[n] for n in names],
            *[new_v[n] for n in names])
```

```python
import functools

import numpy as np
import jax
import jax.numpy as jnp
from jax import lax
from jax.experimental import pallas as pl
from jax.experimental.pallas import tpu as pltpu

F32, BF16 = jnp.float32, jnp.bfloat16
MESH_ID = pl.DeviceIdType.MESH
N_DEV = 8

HEAD_DIM = 64
NA_HEADS = 8
SW_HEADS = 8
SW_KV_HEADS = 2
SW_GROUP = SW_HEADS // SW_KV_HEADS
NA_WIDTH = NA_HEADS * HEAD_DIM
SW_WIDTH = SW_HEADS * HEAD_DIM
SW_KV_WIDTH = SW_KV_HEADS * HEAD_DIM
ROPE_WIDTH = SW_WIDTH + SW_KV_WIDTH
IN_WIDTH = 3 * NA_WIDTH + SW_WIDTH + 2 * SW_KV_WIDTH
ROPE_LO = 3 * NA_WIDTH
GRID_W = 64
NA_ROWS_MAX = 8
NA_COLS = 16
N_DR = 2 * NA_ROWS_MAX - 1
N_DC = 2 * NA_COLS - 1
SW_WINDOW = 128
SW_BLOCK = 128
ROPE_THETA = 10000.0
EPS = 1e-6
NEG = -1e30
Q_SCALE = HEAD_DIM ** -0.5

ADAM_LR = 0.001
ADAM_B1 = 0.9
ADAM_B2 = 0.999
ADAM_EPS = 1e-08
ADAM_WD = 0.01
ADAM_STEP = 10

TOKEN_TILE = 256
WIDE_TILE = 512
VMEM_LIMIT = 56 * 1024 * 1024

PACK_W = 6144


def _nn(a, b):
    return jnp.dot(a, b, preferred_element_type=F32)


def _nt(a, b):
    return lax.dot_general(a, b, (((1,), (1,)), ((), ())), preferred_element_type=F32)


def _tn(a, b):
    return lax.dot_general(a, b, (((0,), (0,)), ((), ())), preferred_element_type=F32)


def _rms(x):
    r = lax.rsqrt(jnp.mean(x * x, axis=-1, keepdims=True) + EPS)
    return x * r, r


def _rms_bwd(xn, r, gy):
    return r * (gy - xn * jnp.mean(xn * gy, axis=-1, keepdims=True))


def _params(*sem):
    return pltpu.CompilerParams(dimension_semantics=sem, vmem_limit_bytes=VMEM_LIMIT)


def _full(shape):
    n = len(shape)
    return pl.BlockSpec(shape, lambda *_: (0,) * n)


def _mesh_pos():
    return lax.axis_index("x"), lax.axis_index("y"), lax.axis_index("c")


def _row_chunk(r):
    for rc in (128, 64, 32, 16):
        if r % rc == 0:
            return rc
    raise ValueError(f"rows {r} not a multiple of 16")


class _Task:
    def __init__(self, inputs, out_shapes, sems, start, finish, mid=None, mid_step=None, alias=None):
        self.inputs, self.out_shapes, self.sems = list(inputs), list(out_shapes), list(sems)
        self.start, self.finish, self.mid, self.mid_step = start, finish, mid, mid_step
        self.alias = alias


def _hosted_call(body, name, grid, in_specs, out_specs, out_shape, operands, tasks, scratch_shapes=()):
    n_in, n_out, n_scr = len(in_specs), len(out_specs), len(scratch_shapes)
    t_in = [len(t.inputs) for t in tasks]
    t_out = [len(t.out_shapes) for t in tasks]
    t_sem = [len(t.sems) for t in tasks]
    n_steps = int(np.prod(grid))

    def wrapped(*refs):
        ins, rest = refs[:n_in], refs[n_in:]
        task_ins, rest = rest[:sum(t_in)], rest[sum(t_in):]
        outs, rest = rest[:n_out], rest[n_out:]
        task_outs, rest = rest[:sum(t_out)], rest[sum(t_out):]
        scr, task_sems = rest[:n_scr], rest[n_scr:]
        step = pl.program_id(0)
        for ax in range(1, len(grid)):
            step = step * grid[ax] + pl.program_id(ax)
        parts = []
        oi = oo = os_ = 0
        for t, a, b, c in zip(tasks, t_in, t_out, t_sem):
            parts.append((t, task_ins[oi:oi + a], task_outs[oo:oo + b], task_sems[os_:os_ + c]))
            oi, oo, os_ = oi + a, oo + b, os_ + c
        for t, ti, to, ts in parts:
            pl.when(step == 0)(functools.partial(t.start, ti, to, ts))
            if t.mid is not None:
                pl.when(step == t.mid_step)(functools.partial(t.mid, ti, to, ts))
        body(*ins, *outs, *scr)
        for t, ti, to, ts in parts:
            pl.when(step == n_steps - 1)(functools.partial(t.finish, ti, to, ts))

    hbm = pl.BlockSpec(memory_space=pl.ANY)
    aliases, oi, oo = {}, n_in, n_out
    for t, a, b in zip(tasks, t_in, t_out):
        if t.alias is not None:
            aliases[oi + t.alias[0]] = oo + t.alias[1]
        oi, oo = oi + a, oo + b
    res = pl.pallas_call(
        wrapped, name=name, grid=grid,
        in_specs=list(in_specs) + [hbm] * sum(t_in),
        out_specs=list(out_specs) + [hbm] * sum(t_out),
        out_shape=list(out_shape) + [s for t in tasks for s in t.out_shapes],
        scratch_shapes=list(scratch_shapes) + [s for t in tasks for s in t.sems],
        input_output_aliases=aliases,
        compiler_params=_params(*(["arbitrary"] * len(grid))),
    )(*operands, *[a for t in tasks for a in t.inputs])
    own, extra = res[:n_out], res[n_out:]
    per_task, o = [], 0
    for b in t_out:
        per_task.append(extra[o:o + b])
        o += b
    return own, per_task


def _gather_task(shard, mid_step, rows=None, into=None):
    lo, n = (0, shard.shape[0]) if rows is None else rows

    def parts(ins, outs, sems):
        x_ref, out_ref, (send_sems, recv_sems, local_sem) = ins[0], outs[0], sems
        x_, y_, c_ = _mesh_pos()
        me, sibling = (x_, y_, c_), (x_, y_, 1 - c_)
        chips = [(1 - x_, y_), (x_, 1 - y_), (1 - x_, 1 - y_)]
        x_ref = x_ref.at[pl.ds(lo, n)]

        def rows(px, py, pc):
            return out_ref.at[4 * px + 2 * py + pc, pl.ds(lo, n)]

        def copy(k, block, to, src=None):
            return pltpu.make_async_remote_copy(
                src_ref=rows(*block) if src is None else src, dst_ref=rows(*block),
                send_sem=send_sems.at[k], recv_sem=recv_sems.at[k], device_id=to, device_id_type=MESH_ID)

        return dict(
            mine=lambda: pltpu.make_async_copy(x_ref, rows(*me), local_sem),
            first=lambda: [copy(0, me, sibling, src=x_ref)] + [copy(1 + j, me, (*chip, c_), src=x_ref) for j, chip in enumerate(chips)],
            passed=lambda: [copy(4 + j, (*chip, c_), sibling) for j, chip in enumerate(chips)],
            landed=lambda: [copy(1 + j, (*chip, c_), me) for j, chip in enumerate(chips)],
            last=lambda: [copy(0, sibling, me)] + [copy(4 + j, (*chip, 1 - c_), me) for j, chip in enumerate(chips)])

    def start(ins, outs, sems):
        p = parts(ins, outs, sems)
        p["mine"]().start()
        for cp in p["first"]():
            cp.start()

    def mid(ins, outs, sems):
        p = parts(ins, outs, sems)
        for cp, fw in zip(p["landed"](), p["passed"]()):
            cp.wait_recv()
            fw.start()

    def finish(ins, outs, sems):
        p = parts(ins, outs, sems)
        for cp in p["last"]():
            cp.wait_recv()
        for cp in p["first"]() + p["passed"]():
            cp.wait_send()
        p["mine"]().wait()

    return _Task([shard] if into is None else [shard, into], [jax.ShapeDtypeStruct((N_DEV,) + shard.shape, shard.dtype)],
                 [pltpu.SemaphoreType.DMA((7,)), pltpu.SemaphoreType.DMA((7,)), pltpu.SemaphoreType.DMA],
                 start, finish, mid, mid_step, alias=None if into is None else (1, 0))


def _swap_task(g8):
    _, R, C = g8.shape

    def copies(ins, outs, sems):
        (g_ref,), (recv_ref,), (ss, rs) = ins, outs, sems
        x_, y_, c_ = _mesh_pos()
        return [pltpu.make_async_remote_copy(src_ref=g_ref.at[2 * k + (1 - c_)], dst_ref=recv_ref.at[k], send_sem=ss.at[k],
                                             recv_sem=rs.at[k], device_id=(x_, y_, 1 - c_), device_id_type=MESH_ID)
                for k in range(4)]

    def start(ins, outs, sems):
        for cp in copies(ins, outs, sems):
            cp.start()

    def finish(ins, outs, sems):
        cps = copies(ins, outs, sems)
        for cp in cps:
            cp.wait_recv()
        for cp in cps:
            cp.wait_send()

    return _Task([g8], [jax.ShapeDtypeStruct((4, R, C), g8.dtype)],
                 [pltpu.SemaphoreType.DMA((4,)), pltpu.SemaphoreType.DMA((4,))], start, finish)


def _chip_sums(g8, recva):
    _, R, C = g8.shape
    rt = _sum_adamw_rows(R)
    rc = _row_chunk(rt)

    def body(core_ref, g_ref, a_ref, send_ref, own_ref):
        x_, y_, _ = _mesh_pos()
        chips = [(1 - x_, y_), (x_, 1 - y_), (1 - x_, 1 - y_), (x_, y_)]

        def chunk(i, carry):
            rows = pl.ds(pl.multiple_of(i * rc, rc), rc)
            for j, (tx, ty) in enumerate(chips):
                k = 2 * tx + ty
                s = g_ref[k, rows, :].astype(F32) + a_ref[k, rows, :].astype(F32)
                if j < 3:
                    send_ref[j, rows, :] = s.astype(BF16)
                else:
                    own_ref[rows, :] = s
            return carry

        lax.fori_loop(0, rt // rc, chunk, 0)

    core = lax.axis_index("c").astype(jnp.int32).reshape(1)
    return pl.pallas_call(
        body, name="chip_sums",
        grid_spec=pltpu.PrefetchScalarGridSpec(
            num_scalar_prefetch=1, grid=(R // rt,),
            in_specs=[pl.BlockSpec((4, None, rt, C), lambda i, c: (0, c[0], i, 0)), pl.BlockSpec((4, rt, C), lambda i, c: (0, i, 0))],
            out_specs=[pl.BlockSpec((3, rt, C), lambda i, c: (0, i, 0)), pl.BlockSpec((rt, C), lambda i, c: (i, 0))]),
        out_shape=[jax.ShapeDtypeStruct((3, R, C), BF16), jax.ShapeDtypeStruct((R, C), F32)],
        compiler_params=_params("parallel"),
    )(core, g8.reshape(4, 2, R, C), recva)


def _exchange_task(sendb):
    def copies(ins, outs, sems):
        (s_ref,), (recv_ref,), (ss, rs) = ins, outs, sems
        x_, y_, c_ = _mesh_pos()
        flips = [(1 - x_, y_), (x_, 1 - y_), (1 - x_, 1 - y_)]
        return [pltpu.make_async_remote_copy(src_ref=s_ref.at[j], dst_ref=recv_ref.at[j], send_sem=ss.at[j], recv_sem=rs.at[j],
                                             device_id=(tx, ty, c_), device_id_type=MESH_ID) for j, (tx, ty) in enumerate(flips)]

    def start(ins, outs, sems):
        for cp in copies(ins, outs, sems):
            cp.start()

    def finish(ins, outs, sems):
        cps = copies(ins, outs, sems)
        for cp in cps:
            cp.wait_recv()
        for cp in cps:
            cp.wait_send()

    return _Task([sendb], [jax.ShapeDtypeStruct(sendb.shape, sendb.dtype)],
                 [pltpu.SemaphoreType.DMA((3,)), pltpu.SemaphoreType.DMA((3,))], start, finish)


def _silu(v):
    return v * (1.0 / (1.0 + jnp.exp(-v)))


def _ada_fwd(c_slab, w_ada_l, b_ada_l, w_in_shard, n_seq):
    W = c_slab.shape[1]
    D, cols = w_ada_l.shape
    n_rows = N_DEV * n_seq
    t_c, t_w = _gather_task(c_slab, 0), _gather_task(w_in_shard, 0)
    t_m = _gather_task(jax.ShapeDtypeStruct((n_rows, cols), F32), 0)

    def body(c_ref, w_ref, b_ref, ws_ref, slabs_ref, mod_ref, win_ref, c_vm, m_vm, copy_sem, *sems):
        sc, sw, sm = sems[0:3], sems[3:6], sems[6:9]
        t_c.start((c_ref,), (slabs_ref,), sc)
        t_w.start((ws_ref,), (win_ref,), sw)
        t_c.mid((c_ref,), (slabs_ref,), sc)
        t_c.finish((c_ref,), (slabs_ref,), sc)
        cp = pltpu.make_async_copy(slabs_ref, c_vm, copy_sem)
        cp.start()
        cp.wait()
        c_all = c_vm[:, :, 0:D].reshape(N_DEV * 8, D)
        m64 = jnp.dot(_silu(c_all), w_ref[...], precision=lax.Precision.HIGHEST, preferred_element_type=F32) + b_ref[...]
        r = lax.broadcasted_iota(jnp.int32, (n_rows, N_DEV * 8), 0)
        c = lax.broadcasted_iota(jnp.int32, (n_rows, N_DEV * 8), 1)
        pick = jnp.where(c == 8 * (r // n_seq) + r % n_seq, 1.0, 0.0)
        m_vm[...] = jnp.dot(pick, m64, precision=lax.Precision.HIGHEST, preferred_element_type=F32)
        t_m.start((m_vm,), (mod_ref,), sm)
        t_w.mid((ws_ref,), (win_ref,), sw)
        t_m.mid((m_vm,), (mod_ref,), sm)
        t_m.finish((m_vm,), (mod_ref,), sm)
        t_w.finish((ws_ref,), (win_ref,), sw)

    hbm, vm = pl.BlockSpec(memory_space=pl.ANY), pl.BlockSpec(memory_space=pltpu.VMEM)
    return pl.pallas_call(
        body, name="ada_fwd", in_specs=[hbm, vm, vm, hbm], out_specs=[hbm, hbm, hbm],
        out_shape=t_c.out_shapes + t_m.out_shapes + t_w.out_shapes,
        scratch_shapes=[pltpu.VMEM((N_DEV, 8, W), F32), pltpu.VMEM((n_rows, cols), F32), pltpu.SemaphoreType.DMA]
        + t_c.sems + t_w.sems + t_m.sems,
        compiler_params=pltpu.CompilerParams(vmem_limit_bytes=VMEM_LIMIT),
    )(c_slab, w_ada_l, b_ada_l, w_in_shard)


def _ada_bwd(c_all, dmod_cols):
    def body(c_ref, d_ref, o_ref):
        o_ref[...] = lax.dot_general(_silu(c_ref[...]), d_ref[...], (((0,), (0,)), ((), ())),
                                     precision=lax.Precision.HIGHEST, preferred_element_type=F32)
    return pl.pallas_call(body, name="ada_bwd", out_shape=jax.ShapeDtypeStruct((c_all.shape[1], dmod_cols.shape[1]), F32),
                          compiler_params=pltpu.CompilerParams(vmem_limit_bytes=VMEM_LIMIT))(c_all, dmod_cols)


NA_PAIRS = NA_HEADS // 2
N_DR_PAD = 16


def _na_bias_table(na_rpb):
    rev = jnp.pad(jnp.flip(na_rpb, axis=2), ((0, 0), (0, N_DR_PAD - N_DR), (0, GRID_W - N_DC)))
    rev = jnp.transpose(rev.reshape(NA_PAIRS, 2, N_DR_PAD, GRID_W), (0, 2, 1, 3)).reshape(NA_PAIRS, N_DR_PAD, 128)

    def body(r_ref, o_ref):
        k = lax.broadcasted_iota(jnp.int32, (GRID_W, 128), 0)
        lane = lax.broadcasted_iota(jnp.int32, (GRID_W, 128), 1)
        q = lane % GRID_W
        cs = jnp.clip(q - NA_COLS // 2, 0, GRID_W - NA_COLS)
        ok = (k >= cs) & (k < cs + NA_COLS)
        left = lane < GRID_W
        for dr in range(N_DR):
            row = jnp.broadcast_to(r_ref[0, dr:dr + 1, :], (GRID_W, 128))
            r0 = jnp.where(left, row, 0.0)
            r1 = jnp.where(left, pltpu.roll(row, GRID_W, axis=1), 0.0)
            y0 = pltpu.roll(r0, 128 - (NA_COLS - 1), axis=1, stride=1, stride_axis=0)
            y1 = pltpu.roll(r1, GRID_W - (NA_COLS - 1), axis=1, stride=1, stride_axis=0)
            o_ref[0, dr * GRID_W:(dr + 1) * GRID_W, :] = jnp.where(ok, jnp.where(left, y0, y1), NEG)

    return pl.pallas_call(
        body, name="rpb_expand", grid=(NA_PAIRS,),
        in_specs=[pl.BlockSpec((1, N_DR_PAD, 128), lambda p: (p, 0, 0))],
        out_specs=pl.BlockSpec((1, N_DR * GRID_W, 128), lambda p: (p, 0, 0)),
        out_shape=jax.ShapeDtypeStruct((NA_PAIRS, N_DR * GRID_W, 128), F32),
        compiler_params=_params("parallel"),
    )(rev)


def _na_bias_grad(db, tasks=()):
    a = np.arange(128)
    flip = jnp.asarray(((a[:, None] // GRID_W == a[None, :] // GRID_W)
                        & (a[:, None] % GRID_W + a[None, :] % GRID_W == GRID_W - 1)).astype(np.float32))

    def body(d_ref, j_ref, o_ref):
        o_ref[...] = jnp.zeros_like(o_ref)
        for dr in range(N_DR):
            t = jnp.dot(d_ref[0, dr * GRID_W:(dr + 1) * GRID_W, :], j_ref[...], precision=lax.Precision.HIGHEST, preferred_element_type=F32)
            t = pltpu.roll(t, GRID_W + NA_COLS, axis=1, stride=1, stride_axis=0)
            o_ref[0, dr:dr + 1, :] = jnp.sum(t, axis=0, keepdims=True)

    (rows,), got = _hosted_call(
        body, "rpb_reduce", (NA_PAIRS,),
        [pl.BlockSpec((1, N_DR * GRID_W, 128), lambda p: (p, 0, 0)), _full((128, 128))],
        [pl.BlockSpec((1, N_DR_PAD, 128), lambda p: (p, 0, 0))],
        [jax.ShapeDtypeStruct((NA_PAIRS, N_DR_PAD, 128), F32)], (db, flip), tasks)
    g = rows.reshape(NA_PAIRS, N_DR_PAD, 2, GRID_W)[:, :N_DR, :, :N_DC]
    return jnp.transpose(g, (0, 2, 1, 3)).reshape(-1), got


def _rope_tables(S):
    half = HEAD_DIM // 2
    inv = np.float32(ROPE_THETA) ** (-np.arange(half, dtype=np.float32) / np.float32(half))
    ang = np.arange(S).astype(np.float32)[:, None] * inv[None, :]
    cos, sin = np.cos(ang).astype(np.float32), np.sin(ang).astype(np.float32)
    return jnp.asarray(np.tile(np.concatenate([cos, cos], axis=1), (1, 2))), jnp.asarray(np.tile(np.concatenate([-sin, sin], axis=1), (1, 2)))


def _rope_spec(tps, tm=TOKEN_TILE):
    return pl.BlockSpec((tm, 2 * HEAD_DIM), lambda i: (i % tps, 0))


def _rot_half(t):
    w = t.shape[1]
    lane = lax.broadcasted_iota(jnp.int32, t.shape, 1)
    return jnp.where((lane % HEAD_DIM) < HEAD_DIM // 2, pltpu.roll(t, w - HEAD_DIM // 2, axis=1),
                     pltpu.roll(t, HEAD_DIM // 2, axis=1))


def _tok_spec(w, tm=TOKEN_TILE):
    return pl.BlockSpec((tm, w), lambda i: (i, 0))


def _mod_spec(tps, d):
    return pl.BlockSpec((1, 6, d), lambda i: (i // tps, 0, 0))


def _bstat_spec(tps, w):
    return pl.BlockSpec((1, 8, w), lambda i: (i // tps, 0, 0))


def _attn_in(x2d, mod3, g_attn, w_in, cos_t, sin_t, S, tasks=(), tm=WIDE_TILE):
    T, D = x2d.shape
    tps = S // tm

    def body(x_ref, mod_ref, g_ref, w_ref, cos_ref, sin_ref, h_ref, qkv_ref):
        xn, _ = _rms(x_ref[...])
        h = (xn * g_ref[...]) * (1.0 + mod_ref[0, 1:2, :]) + mod_ref[0, 0:1, :]
        hb = h.astype(BF16)
        h_ref[...] = hb
        proj = _nt(hb, w_ref[...])
        rb = proj[:, ROPE_LO:ROPE_LO + ROPE_WIDTH]
        reps = (1, ROPE_WIDTH // (2 * HEAD_DIM))
        rb = rb * jnp.tile(cos_ref[...], reps) + _rot_half(rb) * jnp.tile(sin_ref[...], reps)
        qkv_ref[:, 0:NA_WIDTH] = (proj[:, 0:NA_WIDTH] * Q_SCALE).astype(BF16)
        qkv_ref[:, NA_WIDTH:ROPE_LO] = proj[:, NA_WIDTH:ROPE_LO].astype(BF16)
        qkv_ref[:, ROPE_LO:ROPE_LO + SW_WIDTH] = (rb[:, 0:SW_WIDTH] * Q_SCALE).astype(BF16)
        qkv_ref[:, ROPE_LO + SW_WIDTH:ROPE_LO + ROPE_WIDTH] = rb[:, SW_WIDTH:].astype(BF16)
        qkv_ref[:, ROPE_LO + ROPE_WIDTH:] = proj[:, ROPE_LO + ROPE_WIDTH:].astype(BF16)

    return _hosted_call(
        body, "attn_in", (T // tm,),
        [_tok_spec(D, tm), _mod_spec(tps, D), _full((1, D)), _full(w_in.shape), _rope_spec(tps, tm), _rope_spec(tps, tm)],
        [_tok_spec(D, tm), _tok_spec(IN_WIDTH, tm)],
        [jax.ShapeDtypeStruct((T, D), BF16), jax.ShapeDtypeStruct((T, IN_WIDTH), BF16)],
        (x2d, mod3, g_attn, w_in, cos_t, sin_t), tasks)


def _attn_out(oa, ob, x2d, mod3, g_na, g_sw, w_out, S, tasks=(), tm=WIDE_TILE):
    T, D = x2d.shape
    tps = S // tm

    def body(oa_ref, ob_ref, x_ref, mod_ref, gna_ref, gsw_ref, w_ref, mixin_ref, mix_ref, x1_ref):
        oan, _ = _rms(oa_ref[...])
        obn, _ = _rms(ob_ref[...])
        mixin = jnp.concatenate([oan * gna_ref[...], obn * gsw_ref[...]], axis=1).astype(BF16)
        mixin_ref[...] = mixin
        mix = _nn(mixin, w_ref[...])
        mix_ref[...] = mix
        x1_ref[...] = x_ref[...] + mod_ref[0, 2:3, :] * mix

    return _hosted_call(
        body, "attn_out", (T // tm,),
        [_tok_spec(NA_WIDTH, tm), _tok_spec(SW_WIDTH, tm), _tok_spec(D, tm), _mod_spec(tps, D),
         _full((1, NA_WIDTH)), _full((1, SW_WIDTH)), _full(w_out.shape)],
        [_tok_spec(NA_WIDTH + SW_WIDTH, tm), _tok_spec(D, tm), _tok_spec(D, tm)],
        [jax.ShapeDtypeStruct((T, NA_WIDTH + SW_WIDTH), BF16), jax.ShapeDtypeStruct((T, D), F32), jax.ShapeDtypeStruct((T, D), F32)],
        (oa, ob, x2d, mod3, g_na, g_sw, w_out), tasks)


def _ffn_up(x1, mod3, g_ffn, w_up, S, tasks=(), tm=WIDE_TILE):
    T, D = x1.shape
    F = w_up.shape[0] // 2
    tps = S // tm

    def body(x1_ref, mod_ref, g_ref, w_ref, h2_ref, val_ref, gt_ref):
        xn, _ = _rms(x1_ref[...])
        h2 = ((xn * g_ref[...]) * (1.0 + mod_ref[0, 4:5, :]) + mod_ref[0, 3:4, :]).astype(BF16)
        h2_ref[...] = h2
        u = _nt(h2, w_ref[...])
        val_ref[...] = u[:, :F].astype(BF16)
        gt_ref[...] = u[:, F:].astype(BF16)

    return _hosted_call(
        body, "ffn_up", (T // tm,), [_tok_spec(D, tm), _mod_spec(tps, D), _full((1, D)), _full(w_up.shape)],
        [_tok_spec(D, tm), _tok_spec(F, tm), _tok_spec(F, tm)],
        [jax.ShapeDtypeStruct((T, D), BF16), jax.ShapeDtypeStruct((T, F), BF16), jax.ShapeDtypeStruct((T, F), BF16)],
        (x1, mod3, g_ffn, w_up), tasks)


def _halo_specs(T, tps, w):
    per = TOKEN_TILE // 8
    prev = pl.BlockSpec((8, w), lambda i: (jnp.maximum(i * per - 1, 0), 0))
    nxt = pl.BlockSpec((8, w), lambda i: (jnp.minimum((i + 1) * per, T // 8 - 1), 0))
    return prev, nxt


def _seq_shifts(cur, before, after, ti, tps):
    tm = cur.shape[0]
    row = lax.broadcasted_iota(jnp.int32, cur.shape, 0)
    before = jnp.where(ti > 0, before.astype(F32), 0.0)
    after = jnp.where(ti < tps - 1, after.astype(F32), 0.0)
    return jnp.where(row == 0, before, pltpu.roll(cur, 1, axis=0)), jnp.where(row == tm - 1, after, pltpu.roll(cur, tm - 1, axis=0))


def _ffn_down(gt, val, conv_w, conv_b, w_down, x1, mod3, g_final, target, B, S):
    T, D = x1.shape
    F = gt.shape[1]
    tps = S // TOKEN_TILE
    prev, nxt = _halo_specs(T, tps, F)

    def body(gt_ref, prev_ref, next_ref, val_ref, cw_ref, cb_ref, w_ref, x1_ref, mod_ref, gf_ref, tgt_ref,
             a_ref, act_ref, vd_ref, dx2_ref, df_ref, gstat_ref, bstat_ref):
        i = pl.program_id(0)
        g = gt_ref[...].astype(F32)
        gprev, gnext = _seq_shifts(g, prev_ref[7:8, :], next_ref[0:1, :], i % tps, tps)
        gc = gprev * cw_ref[0:1, :] + g * cw_ref[1:2, :] + gnext * cw_ref[2:3, :] + cb_ref[...]
        sig = 1.0 / (1.0 + jnp.exp(-gc))
        act = gc * sig
        val = val_ref[...].astype(F32)
        act_ref[...] = act.astype(BF16)
        vd_ref[...] = (val * (sig + act - act * sig)).astype(BF16)
        a = (act * val).astype(BF16)
        a_ref[...] = a
        f = _nn(a, w_ref[...])
        gate = mod_ref[0, 5:6, :]
        x2 = x1_ref[...] + gate * f
        xn, r = _rms(x2)
        err = xn * gf_ref[...] - tgt_ref[...]
        dy = err * (1.0 / D)
        dx2 = _rms_bwd(xn, r, dy * gf_ref[...])
        dx2_ref[...] = dx2
        df_ref[...] = (gate * dx2).astype(BF16)

        @pl.when(i == 0)
        def _():
            gstat_ref[...] = jnp.zeros_like(gstat_ref)

        @pl.when(i % tps == 0)
        def _():
            bstat_ref[...] = jnp.zeros_like(bstat_ref)

        gstat_ref[0:1, :] += jnp.sum(dy * xn, axis=0, keepdims=True)
        tile_loss = jnp.sum(jnp.sum(err * err, axis=1, keepdims=True), axis=0, keepdims=True) * (0.5 / D)
        gstat_ref[1:2, :] += jnp.broadcast_to(tile_loss, (1, D))
        bstat_ref[0, 0:1, :] += jnp.sum(dx2 * f, axis=0, keepdims=True)

    return pl.pallas_call(
        body, name="ffn_down", grid=(T // TOKEN_TILE,),
        in_specs=[_tok_spec(F), prev, nxt, _tok_spec(F), _full(conv_w.shape), _full((1, F)), _full(w_down.shape),
                  _tok_spec(D), _mod_spec(tps, D), _full((1, D)), _tok_spec(D)],
        out_specs=[_tok_spec(F), _tok_spec(F), _tok_spec(F), _tok_spec(D), _tok_spec(D), _full((8, D)), _bstat_spec(tps, D)],
        out_shape=[jax.ShapeDtypeStruct((T, F), BF16), jax.ShapeDtypeStruct((T, F), BF16), jax.ShapeDtypeStruct((T, F), BF16),
                   jax.ShapeDtypeStruct((T, D), F32), jax.ShapeDtypeStruct((T, D), BF16),
                   jax.ShapeDtypeStruct((8, D), F32), jax.ShapeDtypeStruct((B, 8, D), F32)],
        compiler_params=_params("arbitrary"),
    )(gt, gt, gt, val, conv_w, conv_b, w_down, x1, mod3, g_final, target)


def _ffn_down_bwd(df, w_down, act, vd, tasks=(), tm=WIDE_TILE):
    T, D = df.shape
    F = act.shape[1]

    def body(df_ref, w_ref, act_ref, vd_ref, dval_ref, dgc_ref, cstat_ref):
        da = _nt(df_ref[...], w_ref[...])
        dval_ref[...] = (da * act_ref[...].astype(F32)).astype(BF16)
        dgc = da * vd_ref[...].astype(F32)
        dgc_ref[...] = dgc.astype(BF16)

        @pl.when(pl.program_id(0) == 0)
        def _():
            cstat_ref[...] = jnp.zeros_like(cstat_ref)

        cstat_ref[0:1, :] += jnp.sum(dgc, axis=0, keepdims=True)

    return _hosted_call(
        body, "ffn_down_bwd", (T // tm,),
        [_tok_spec(D, tm), _full(w_down.shape), _tok_spec(F, tm), _tok_spec(F, tm)],
        [_tok_spec(F, tm), _tok_spec(F, tm), _full((8, F))],
        [jax.ShapeDtypeStruct((T, F), BF16), jax.ShapeDtypeStruct((T, F), BF16), jax.ShapeDtypeStruct((8, F), F32)],
        (df, w_down, act, vd), tasks)


def _ffn_up_bwd(dgc, dval, gt, conv_w, w_up, x1, mod3, g_ffn, dx2, mix, B, S, tasks=()):
    T, D = x1.shape
    F = dgc.shape[1]
    tps = S // TOKEN_TILE
    prev, nxt = _halo_specs(T, tps, F)

    def body(dgc_ref, prev_ref, next_ref, dval_ref, gt_ref, cw_ref, w_ref, x1_ref, mod_ref, g_ref, dx2_ref, mix_ref,
             du_ref, dx1_ref, dmix_ref, gstat_ref, bstat_ref, cstat_ref):
        i = pl.program_id(0)
        d = dgc_ref[...].astype(F32)
        dprev, dnext = _seq_shifts(d, prev_ref[7:8, :], next_ref[0:1, :], i % tps, tps)
        g = gt_ref[...].astype(F32)

        @pl.when(i == 0)
        def _():
            cstat_ref[...] = jnp.zeros_like(cstat_ref)

        cstat_ref[1:2, :] += jnp.sum(dnext * g, axis=0, keepdims=True)
        cstat_ref[2:3, :] += jnp.sum(d * g, axis=0, keepdims=True)
        cstat_ref[3:4, :] += jnp.sum(dprev * g, axis=0, keepdims=True)
        dgt = dnext * cw_ref[0:1, :] + d * cw_ref[1:2, :] + dprev * cw_ref[2:3, :]
        du = jnp.concatenate([dval_ref[...], dgt.astype(BF16)], axis=1)
        du_ref[...] = du
        dh2 = _nn(du, w_ref[...])
        xn, r = _rms(x1_ref[...])
        scale1 = 1.0 + mod_ref[0, 4:5, :]
        xg = xn * g_ref[...]
        dx1 = dx2_ref[...] + _rms_bwd(xn, r, dh2 * g_ref[...] * scale1)
        dx1_ref[...] = dx1
        dmix_ref[...] = (mod_ref[0, 2:3, :] * dx1).astype(BF16)

        @pl.when(i == 0)
        def _():
            gstat_ref[...] = jnp.zeros_like(gstat_ref)

        @pl.when(i % tps == 0)
        def _():
            bstat_ref[...] = jnp.zeros_like(bstat_ref)

        gstat_ref[0:1, :] += jnp.sum(dh2 * scale1 * xn, axis=0, keepdims=True)
        bstat_ref[0, 0:1, :] += jnp.sum(dh2, axis=0, keepdims=True)
        bstat_ref[0, 1:2, :] += jnp.sum(dh2 * xg, axis=0, keepdims=True)
        bstat_ref[0, 2:3, :] += jnp.sum(dx1 * mix_ref[...], axis=0, keepdims=True)

    return _hosted_call(
        body, "ffn_up_bwd", (T // TOKEN_TILE,),
        [_tok_spec(F), prev, nxt, _tok_spec(F), _tok_spec(F), _full(conv_w.shape), _full(w_up.shape), _tok_spec(D),
         _mod_spec(tps, D), _full((1, D)), _tok_spec(D), _tok_spec(D)],
        [_tok_spec(2 * F), _tok_spec(D), _tok_spec(D), _full((8, D)), _bstat_spec(tps, D), _full((8, F))],
        [jax.ShapeDtypeStruct((T, 2 * F), BF16), jax.ShapeDtypeStruct((T, D), F32), jax.ShapeDtypeStruct((T, D), BF16),
         jax.ShapeDtypeStruct((8, D), F32), jax.ShapeDtypeStruct((B, 8, D), F32), jax.ShapeDtypeStruct((8, F), F32)],
        (dgc, dgc, dgc, dval, gt, conv_w, w_up, x1, mod3, g_ffn, dx2, mix), tasks)


def _attn_out_bwd(dmix, w_out, oa, ob, g_na, g_sw, tasks=(), tm=WIDE_TILE):
    T, D = dmix.shape

    def body(dmix_ref, w_ref, oa_ref, ob_ref, gna_ref, gsw_ref, doa_ref, dob_ref, gstat_ref):
        dmixin = _nt(dmix_ref[...], w_ref[...])

        @pl.when(pl.program_id(0) == 0)
        def _():
            gstat_ref[...] = jnp.zeros_like(gstat_ref)

        for k, (o_ref, g_ref, do_ref) in enumerate(((oa_ref, gna_ref, doa_ref), (ob_ref, gsw_ref, dob_ref))):
            dn = dmixin[:, k * NA_WIDTH:(k + 1) * NA_WIDTH]
            on, r = _rms(o_ref[...])
            gstat_ref[k:k + 1, :] += jnp.sum(dn * on, axis=0, keepdims=True)
            do_ref[...] = _rms_bwd(on, r, dn * g_ref[...]).astype(BF16)

    hs = jax.ShapeDtypeStruct((T, NA_WIDTH), BF16)
    return _hosted_call(
        body, "attn_out_bwd", (T // tm,),
        [_tok_spec(D, tm), _full(w_out.shape), _tok_spec(NA_WIDTH, tm), _tok_spec(SW_WIDTH, tm), _full((1, NA_WIDTH)), _full((1, SW_WIDTH))],
        [_tok_spec(NA_WIDTH, tm), _tok_spec(SW_WIDTH, tm), _full((8, NA_WIDTH))],
        [hs, hs, jax.ShapeDtypeStruct((8, NA_WIDTH), F32)],
        (dmix, w_out, oa, ob, g_na, g_sw), tasks)


def _attn_in_bwd(dqa, dka, dva, dqb, dkb, dvb, cos_t, sin_t, w_in, x2d, mod3, g_attn, dx1, B, S, tm=WIDE_TILE):
    T, D = x2d.shape
    tps = S // tm

    def body(dqa_ref, dka_ref, dva_ref, dqb_ref, dkb_ref, dvb_ref, cos_ref, sin_ref, w_ref, x_ref, mod_ref, g_ref, dx1_ref,
             gx_ref, dproj_ref, gstat_ref, bstat_ref):
        i = pl.program_id(0)
        drb = jnp.concatenate([dqb_ref[...] * Q_SCALE, dkb_ref[...]], axis=1)
        reps = (1, ROPE_WIDTH // (2 * HEAD_DIM))
        drb = drb * jnp.tile(cos_ref[...], reps) + _rot_half(drb * jnp.tile(sin_ref[...], reps))
        dproj = jnp.concatenate([(dqa_ref[...] * Q_SCALE).astype(BF16), dka_ref[...].astype(BF16), dva_ref[...].astype(BF16),
                                 drb.astype(BF16), dvb_ref[...].astype(BF16)], axis=1)
        dproj_ref[...] = dproj
        dh = _nn(dproj, w_ref[...])
        xn, r = _rms(x_ref[...])
        scale1 = 1.0 + mod_ref[0, 1:2, :]
        gx_ref[...] = dx1_ref[...] + _rms_bwd(xn, r, dh * g_ref[...] * scale1)

        @pl.when(i == 0)
        def _():
            gstat_ref[...] = jnp.zeros_like(gstat_ref)

        @pl.when(i % tps == 0)
        def _():
            bstat_ref[...] = jnp.zeros_like(bstat_ref)

        gstat_ref[0:1, :] += jnp.sum(dh * scale1 * xn, axis=0, keepdims=True)
        bstat_ref[0, 0:1, :] += jnp.sum(dh, axis=0, keepdims=True)
        bstat_ref[0, 1:2, :] += jnp.sum(dh * (xn * g_ref[...]), axis=0, keepdims=True)

    rope = _rope_spec(tps, tm)
    return pl.pallas_call(
        body, name="attn_in_bwd", grid=(T // tm,),
        in_specs=[_tok_spec(NA_WIDTH, tm), _tok_spec(NA_WIDTH, tm), _tok_spec(NA_WIDTH, tm), _tok_spec(SW_WIDTH, tm),
                  _tok_spec(SW_KV_WIDTH, tm), _tok_spec(SW_KV_WIDTH, tm), rope, rope, _full(w_in.shape), _tok_spec(D, tm),
                  _mod_spec(tps, D), _full((1, D)), _tok_spec(D, tm)],
        out_specs=[_tok_spec(D, tm), _tok_spec(IN_WIDTH, tm), _full((8, D)), _bstat_spec(tps, D)],
        out_shape=[jax.ShapeDtypeStruct((T, D), F32), jax.ShapeDtypeStruct((T, IN_WIDTH), BF16),
                   jax.ShapeDtypeStruct((8, D), F32), jax.ShapeDtypeStruct((B, 8, D), F32)],
        compiler_params=_params("arbitrary"),
    )(dqa, dka, dva, dqb, dkb, dvb, cos_t, sin_t, w_in, x2d, mod3, g_attn, dx1)


def _matmul_tn(a, b, name, tm=None, tk=512):
    T, M = a.shape
    N = b.shape[1]
    tm = M if tm is None else tm
    nk = T // tk

    def body(a_ref, b_ref, o_ref, acc):
        k = pl.program_id(1)

        @pl.when(k == 0)
        def _():
            acc[...] = jnp.zeros_like(acc)

        acc[...] += _tn(a_ref[...], b_ref[...])

        @pl.when(k == nk - 1)
        def _():
            o_ref[...] = acc[...].astype(BF16)

    return pl.pallas_call(
        body, name=name, grid=(M // tm, nk),
        in_specs=[pl.BlockSpec((tk, tm), lambda i, k: (k, i)), pl.BlockSpec((tk, N), lambda i, k: (k, 0))],
        out_specs=pl.BlockSpec((tm, N), lambda i, k: (i, 0)),
        out_shape=jax.ShapeDtypeStruct((M, N), BF16),
        scratch_shapes=[pltpu.VMEM((tm, N), F32)],
        compiler_params=_params("parallel", "arbitrary"),
    )(a, b)


def _na_geometry(S):
    rows = S // GRID_W
    wr = min(NA_ROWS_MAX, rows)
    return rows, wr


def _na_window(r, rows, wr):
    rs = jnp.clip(r - wr // 2, 0, rows - wr)
    return pl.multiple_of(rs * GRID_W, GRID_W), pl.multiple_of((rs - r + NA_ROWS_MAX - 1) * GRID_W, GRID_W)


NA_STEP_PAIRS = 2
NA_GW = NA_STEP_PAIRS * 128
NA_BWD_ROWS = 4
NA_ROWS_PER_STEP = 4


def _na_specs(S, kw_n, order):
    ng = NA_PAIRS // NA_STEP_PAIRS

    def col(k):
        return pl.BlockSpec((1, S, NA_GW), lambda *ids: (order(*ids)[0], 0, k * ng + order(*ids)[1]))
    bias = pl.BlockSpec((NA_STEP_PAIRS, N_DR * GRID_W, 128), lambda *ids: (order(*ids)[1], 0, 0))
    out = pl.BlockSpec((1, S, NA_GW), lambda *ids: (order(*ids)[0], 0, order(*ids)[1]))
    return col(0), col(1), col(2), bias, out


def _block_diag(t):
    left = lax.broadcasted_iota(jnp.int32, t.shape, 1) < HEAD_DIM
    zero = jnp.zeros_like(t)
    return jnp.concatenate([jnp.where(left, t, zero), jnp.where(left, zero, t)], axis=0)


def _diag_blocks(res):
    left = lax.broadcasted_iota(jnp.int32, (HEAD_DIM, 128), 1) < HEAD_DIM
    return jnp.where(left, res[:HEAD_DIM], res[HEAD_DIM:])


def _col_softmax(st):
    e = jnp.exp(st - jnp.max(st, axis=0, keepdims=True))
    return e * (1.0 / jnp.sum(e, axis=0, keepdims=True))


def _na_fwd(qkv, bias, tasks=()):
    B, S, _ = qkv.shape
    rows, wr = _na_geometry(S)
    kw_n = wr * GRID_W

    def body(q_ref, k_ref, v_ref, b_ref, o_ref):
        def step(it, carry):
            win = [_na_window(it * NA_ROWS_PER_STEP + u, rows, wr) for u in range(NA_ROWS_PER_STEP)]
            qrows = [pl.ds(pl.multiple_of((it * NA_ROWS_PER_STEP + u) * GRID_W, GRID_W), GRID_W) for u in range(NA_ROWS_PER_STEP)]
            krows = [pl.ds(w[0], kw_n) for w in win]
            brows = [pl.ds(w[1], kw_n) for w in win]
            lanes = [pl.ds(p * 128, 128) for p in range(NA_STEP_PAIRS)]
            chains = [(u, p) for u in range(NA_ROWS_PER_STEP) for p in range(NA_STEP_PAIRS)]
            st = {(u, p): _nt(k_ref[0, krows[u], lanes[p]], _block_diag(q_ref[0, qrows[u], lanes[p]])) for u, p in chains}
            pn = {(u, p): _col_softmax(st[(u, p)] + b_ref[p, brows[u], :]).astype(BF16) for u, p in chains}
            out = {(u, p): _diag_blocks(_tn(pn[(u, p)], v_ref[0, krows[u], lanes[p]])) for u, p in chains}
            for u in range(NA_ROWS_PER_STEP):
                o_ref[0, qrows[u], :] = jnp.concatenate([out[(u, p)] for p in range(NA_STEP_PAIRS)], axis=1)
            return carry

        lax.fori_loop(0, rows // NA_ROWS_PER_STEP, step, 0)

    q, k, v, bs, out = _na_specs(S, kw_n, lambda b, g: (b, g))
    return _hosted_call(body, "na_fwd", (B, NA_PAIRS // NA_STEP_PAIRS), [q, k, v, bs], [out],
                        [jax.ShapeDtypeStruct((B, S, NA_WIDTH), F32)], (qkv, qkv, qkv, bias), tasks)


def _na_bwd(qkv, bias, doa, tasks=()):
    B, S, _ = qkv.shape
    rows, wr = _na_geometry(S)
    kw_n = wr * GRID_W

    def body(q_ref, k_ref, v_ref, b_ref, do_ref, dq_ref, dk_ref, dv_ref, db_ref, dk_acc, dv_acc):
        @pl.when(pl.program_id(1) == 0)
        def _():
            db_ref[...] = jnp.zeros_like(db_ref)

        dk_acc[...] = jnp.zeros_like(dk_acc)
        dv_acc[...] = jnp.zeros_like(dv_acc)

        def step(it, carry):
            nu, pairs = range(NA_BWD_ROWS), range(NA_STEP_PAIRS)
            win = [_na_window(it * NA_BWD_ROWS + u, rows, wr) for u in nu]
            qrows = [pl.ds(pl.multiple_of((it * NA_BWD_ROWS + u) * GRID_W, GRID_W), GRID_W) for u in nu]
            krows = [pl.ds(w[0], kw_n) for w in win]
            brows = [pl.ds(w[1], kw_n) for w in win]
            lanes = [pl.ds(p * 128, 128) for p in pairs]
            chains = [(u, p) for u in nu for p in pairs]
            kp = {(u, p): k_ref[0, krows[u], lanes[p]] for u, p in chains}
            qbd = {(u, p): _block_diag(q_ref[0, qrows[u], lanes[p]]) for u, p in chains}
            dobd = {(u, p): _block_diag(do_ref[0, qrows[u], lanes[p]]) for u, p in chains}
            st = {c: _nt(kp[c], qbd[c]) for c in chains}
            dpt = {(u, p): _nt(v_ref[0, krows[u], lanes[p]], dobd[(u, p)]) for u, p in chains}
            pn = {(u, p): _col_softmax(st[(u, p)] + b_ref[p, brows[u], :]) for u, p in chains}
            dst = {c: pn[c] * (dpt[c] - jnp.sum(pn[c] * dpt[c], axis=0, keepdims=True)) for c in chains}
            dsb = {c: dst[c].astype(BF16) for c in chains}
            dq = {c: _diag_blocks(_tn(dsb[c], kp[c])) for c in chains}
            dk = {c: _nn(dsb[c], qbd[c]) for c in chains}
            dv = {c: _nn(pn[c].astype(BF16), dobd[c]) for c in chains}
            for u in nu:
                dq_ref[0, qrows[u], :] = jnp.concatenate([dq[(u, p)] for p in pairs], axis=1).astype(BF16)
                dk_acc[krows[u], :] += jnp.concatenate([dk[(u, p)] for p in pairs], axis=1)
                dv_acc[krows[u], :] += jnp.concatenate([dv[(u, p)] for p in pairs], axis=1)
                for p in pairs:
                    db_ref[p, brows[u], :] += dst[(u, p)]
            return carry

        lax.fori_loop(0, rows // NA_BWD_ROWS, step, 0)

        def emit(i, carry):
            r = pl.ds(pl.multiple_of(i * 256, 256), 256)
            dk_ref[0, r, :] = dk_acc[r, :].astype(BF16)
            dv_ref[0, r, :] = dv_acc[r, :].astype(BF16)
            return carry

        lax.fori_loop(0, S // 256, emit, 0)

    q, k, v, bs, out = _na_specs(S, kw_n, lambda g, b: (b, g))
    hs = jax.ShapeDtypeStruct((B, S, NA_WIDTH), BF16)
    return _hosted_call(body, "na_bwd", (NA_PAIRS // NA_STEP_PAIRS, B), [q, k, v, bs, out], [out, out, out, bs],
                        [hs, hs, hs, jax.ShapeDtypeStruct((NA_PAIRS, N_DR * GRID_W, 128), F32)], (qkv, qkv, qkv, bias, doa), tasks,
                        scratch_shapes=[pltpu.VMEM((S, NA_GW), F32), pltpu.VMEM((S, NA_GW), F32)])


SW_PAIRS = SW_HEADS // 2


def _sw_band(n, S):
    kw_n = 3 * SW_BLOCK
    start = pl.multiple_of(jnp.clip(n * SW_BLOCK - SW_BLOCK, 0, S - kw_n), SW_BLOCK)
    kpos = start + lax.broadcasted_iota(jnp.int32, (kw_n, SW_BLOCK), 0)
    qpos = n * SW_BLOCK + lax.broadcasted_iota(jnp.int32, (kw_n, SW_BLOCK), 1)
    return start, jnp.abs(qpos - kpos) <= SW_WINDOW


def _kv_halves(t):
    left = lax.broadcasted_iota(jnp.int32, t.shape, 1) < HEAD_DIM
    swapped = pltpu.roll(t, HEAD_DIM, axis=1)
    zero = jnp.zeros_like(t)
    return {(0, 0): jnp.where(left, t, zero), (0, 1): jnp.where(left, zero, swapped),
            (1, 0): jnp.where(left, swapped, zero), (1, 1): jnp.where(left, zero, t)}


def _sw_probs(st, ok, sk):
    st = jnp.where(ok, st, NEG)
    m = jnp.maximum(jnp.max(st, axis=0, keepdims=True), sk)
    e = jnp.exp(st - m)
    esk = jnp.exp(sk - m)
    inv = 1.0 / (jnp.sum(e, axis=0, keepdims=True) + esk)
    return e * inv, esk * inv


def _sw_specs(S):
    q = pl.BlockSpec((1, S, SW_WIDTH), lambda b, *_: (b, 0, ROPE_LO // SW_WIDTH))
    k = pl.BlockSpec((1, S, SW_KV_WIDTH), lambda b, *_: (b, 0, (ROPE_LO + SW_WIDTH) // SW_KV_WIDTH))
    v = pl.BlockSpec((1, S, SW_KV_WIDTH), lambda b, *_: (b, 0, (ROPE_LO + ROPE_WIDTH) // SW_KV_WIDTH))
    return q, k, v


SW_FWD_SPLIT = 2


def _sw_fwd(sink, qkv, tasks=()):
    B, S, _ = qkv.shape
    kw_n = 3 * SW_BLOCK

    def body(sink_ref, q_ref, k_ref, v_ref, o_ref):
        def step(n, carry):
            start, ok = _sw_band(n, S)
            qrows = pl.ds(pl.multiple_of(n * SW_BLOCK, SW_BLOCK), SW_BLOCK)
            krows = pl.ds(start, kw_n)
            kh, vh = _kv_halves(k_ref[0, krows, :]), _kv_halves(v_ref[0, krows, :])
            heads = [(p, e) for p in range(SW_PAIRS) for e in range(2)]
            qp = [q_ref[0, qrows, pl.ds(p * 128, 128)] for p in range(SW_PAIRS)]
            kv_of = lambda p: p // (SW_PAIRS // SW_KV_HEADS)
            st = {(p, e): _nt(kh[(kv_of(p), e)], qp[p]) for p, e in heads}
            pn = {(p, e): _sw_probs(st[(p, e)], ok, sink_ref[2 * p + e])[0].astype(BF16) for p, e in heads}
            outs = [_tn(pn[(p, 0)], vh[(kv_of(p), 0)]) + _tn(pn[(p, 1)], vh[(kv_of(p), 1)]) for p in range(SW_PAIRS)]
            o_ref[0, qrows, :] = jnp.concatenate(outs, axis=1)
            return carry

        half = (S // SW_BLOCK) // SW_FWD_SPLIT
        lax.fori_loop(pl.program_id(1) * half, (pl.program_id(1) + 1) * half, step, 0)

    q, k, v = _sw_specs(S)
    return _hosted_call(
        body, "sw_fwd", (B, SW_FWD_SPLIT), [pl.BlockSpec(memory_space=pltpu.SMEM), q, k, v],
        [pl.BlockSpec((1, S, SW_WIDTH), lambda b, s: (b, 0, 0))], [jax.ShapeDtypeStruct((B, S, SW_WIDTH), F32)],
        (sink, qkv, qkv, qkv), tasks)


def _sw_bwd(sink, qkv, dob):
    B, S, _ = qkv.shape
    kw_n = 3 * SW_BLOCK

    fold_rows = 256

    def body(sink_ref, q_ref, k_ref, v_ref, do_ref, dq_ref, dk_ref, dv_ref, dsink_ref, dk_acc, dv_acc):
        @pl.when(pl.program_id(0) == 0)
        def _():
            dsink_ref[...] = jnp.zeros_like(dsink_ref)

        dk_acc[...] = jnp.zeros_like(dk_acc)
        dv_acc[...] = jnp.zeros_like(dv_acc)
        ppk = SW_PAIRS // SW_KV_HEADS

        def step(n, carry):
            start, ok = _sw_band(n, S)
            qrows = pl.ds(pl.multiple_of(n * SW_BLOCK, SW_BLOCK), SW_BLOCK)
            krows = pl.ds(start, kw_n)
            kh, vh = _kv_halves(k_ref[0, krows, :]), _kv_halves(v_ref[0, krows, :])
            heads = [(p, e) for p in range(SW_PAIRS) for e in range(2)]
            qp = [q_ref[0, qrows, pl.ds(p * 128, 128)] for p in range(SW_PAIRS)]
            dop = [do_ref[0, qrows, pl.ds(p * 128, 128)] for p in range(SW_PAIRS)]
            st = {(p, e): _nt(kh[(p // ppk, e)], qp[p]) for p, e in heads}
            dpt = {(p, e): _nt(vh[(p // ppk, e)], dop[p]) for p, e in heads}
            pnb, dsb = {}, {}
            for p, e in heads:
                pn, psink = _sw_probs(st[(p, e)], ok, sink_ref[2 * p + e])
                delta = jnp.sum(pn * dpt[(p, e)], axis=0, keepdims=True)
                dsb[(p, e)] = (pn * (dpt[(p, e)] - delta)).astype(BF16)
                pnb[(p, e)] = pn.astype(BF16)
                dsink_ref[2 * p + e:2 * p + e + 1, :] += -(psink * delta)
            dq_ref[0, qrows, :] = jnp.concatenate(
                [_tn(dsb[(p, 0)], kh[(p // ppk, 0)]) + _tn(dsb[(p, 1)], kh[(p // ppk, 1)]) for p in range(SW_PAIRS)], axis=1)
            left = lax.broadcasted_iota(jnp.int32, (kw_n, 128), 1) < HEAD_DIM
            dks, dvs = [], []
            for kv in range(SW_KV_HEADS):
                dk = dv = None
                for p in range(kv * ppk, (kv + 1) * ppk):
                    dk_p = jnp.where(left, _nn(dsb[(p, 0)], qp[p]), _nn(dsb[(p, 1)], qp[p]))
                    dv_p = jnp.where(left, _nn(pnb[(p, 0)], dop[p]), _nn(pnb[(p, 1)], dop[p]))
                    dk = dk_p if dk is None else dk + dk_p
                    dv = dv_p if dv is None else dv + dv_p
                dks.append(dk)
                dvs.append(dv)
            dk_acc[krows, :] += jnp.concatenate(dks, axis=1)
            dv_acc[krows, :] += jnp.concatenate(dvs, axis=1)
            return carry

        lax.fori_loop(0, S // SW_BLOCK, step, 0)

        def fold(i, carry):
            rows = pl.ds(pl.multiple_of(i * fold_rows, fold_rows), fold_rows)
            left = lax.broadcasted_iota(jnp.int32, (fold_rows, 128), 1) < HEAD_DIM
            for acc, out_ref in ((dk_acc, dk_ref), (dv_acc, dv_ref)):
                a, b = acc[rows, 0:128], acc[rows, 128:256]
                out_ref[0, rows, :] = jnp.where(left, a + pltpu.roll(a, HEAD_DIM, axis=1), b + pltpu.roll(b, HEAD_DIM, axis=1))
            return carry

        lax.fori_loop(0, S // fold_rows, fold, 0)

        @pl.when(pl.program_id(0) == B - 1)
        def _():
            dsink_ref[...] = jnp.broadcast_to(jnp.sum(dsink_ref[...], axis=1, keepdims=True), dsink_ref.shape)

    q, k, v = _sw_specs(S)
    qo = pl.BlockSpec((1, S, SW_WIDTH), lambda b: (b, 0, 0))
    ko = pl.BlockSpec((1, S, SW_KV_WIDTH), lambda b: (b, 0, 0))
    return pl.pallas_call(
        body, name="sw_bwd", grid=(B,),
        in_specs=[pl.BlockSpec(memory_space=pltpu.SMEM), q, k, v, qo],
        out_specs=[qo, ko, ko, _full((SW_HEADS, 128))],
        out_shape=[jax.ShapeDtypeStruct((B, S, SW_WIDTH), F32), jax.ShapeDtypeStruct((B, S, SW_KV_WIDTH), F32),
                   jax.ShapeDtypeStruct((B, S, SW_KV_WIDTH), F32), jax.ShapeDtypeStruct((SW_HEADS, 128), F32)],
        scratch_shapes=[pltpu.VMEM((S, 2 * SW_KV_WIDTH), F32), pltpu.VMEM((S, 2 * SW_KV_WIDTH), F32)],
        compiler_params=_params("arbitrary"),
    )(sink, qkv, qkv, qkv, dob)


def _pack_sum_adamw(packs, params, pick):
    W = packs.shape[1]
    n_p = len(params)

    def body(p_ref, *refs):
        ins, tot_ref, pick_ref, outs = refs[:3 * n_p], refs[3 * n_p], refs[3 * n_p + 1], refs[3 * n_p + 2:]
        tot = p_ref[0:8, :]
        for d in range(1, N_DEV):
            tot = tot + p_ref[8 * d:8 * d + 8, :]
        tot_ref[...] = tot
        pick_ref[...] = tot[pick[0]:pick[0] + 1, pick[1]:pick[1] + 1]
        for i, (w, _, _, rows, off) in enumerate(params):
            w_ref, m_ref, v_ref = ins[3 * i:3 * i + 3]
            g_ref, d_ref, nm_ref, nv_ref = outs[4 * i:4 * i + 4]
            if w.ndim == 3:
                n_a, n_b, n = w.shape
                for a in range(n_a):
                    for b in range(n_b):
                        o = off + (b * n_a + a) * n
                        g_ref[a, b:b + 1, :] = tot_ref[rows[0]:rows[0] + 1, o:o + n]
                g = g_ref[...]
            else:
                n = w.shape[1]
                g = tot[rows[0]:rows[0] + 1, off:off + n]
                for r in rows[1:]:
                    g = g + tot[r:r + 1, off:off + n]
                g_ref[...] = g
            d_ref[...], nm_ref[...], nv_ref[...] = _adam_update(w_ref[...], g, m_ref[...], v_ref[...])

    res = pl.pallas_call(
        body, name="small_adamw",
        out_shape=[jax.ShapeDtypeStruct((8, W), F32), jax.ShapeDtypeStruct((1, 1), F32)]
        + [jax.ShapeDtypeStruct(p[0].shape, F32) for p in params for _ in range(4)],
        compiler_params=pltpu.CompilerParams(vmem_limit_bytes=VMEM_LIMIT),
    )(packs, *[a for p in params for a in p[:3]])
    return res[0], res[1], [res[2 + 4 * i:6 + 4 * i] for i in range(n_p)]


def _adam_update(w, g, m, v):
    c1 = 1.0 - ADAM_B1 ** ADAM_STEP
    c2 = 1.0 - ADAM_B2 ** ADAM_STEP
    nm = ADAM_B1 * m + (1.0 - ADAM_B1) * g
    nv = ADAM_B2 * v + (1.0 - ADAM_B2) * (g * g)
    return -ADAM_LR * ((nm / c1) / (jnp.sqrt(nv / c2) + ADAM_EPS) + ADAM_WD * w), nm, nv


def _adamw(w, g, m, v, name):
    def body(w_ref, g_ref, m_ref, v_ref, d_ref, nm_ref, nv_ref):
        d_ref[...], nm_ref[...], nv_ref[...] = _adam_update(w_ref[...], g_ref[...], m_ref[...], v_ref[...])

    s = jax.ShapeDtypeStruct(w.shape, F32)
    return pl.pallas_call(body, name=name, out_shape=[s, s, s],
                          compiler_params=pltpu.CompilerParams(vmem_limit_bytes=VMEM_LIMIT))(w, g, m, v)


def _sum_adamw_rows(R):
    return max(r for r in range(16, min(R, 256) + 1, 16) if R % r == 0)


def _sum_adamw_steps(R):
    return R // _sum_adamw_rows(R)


def _sum_first(own, recvb, name, tasks):
    R, C = own.shape
    rc = _sum_adamw_rows(R)

    def body(own_ref, r_ref, p_ref):
        p_ref[...] = (own_ref[...] + r_ref[0].astype(F32)) + r_ref[1].astype(F32)

    blk = pl.BlockSpec((rc, C), lambda i: (i, 0))
    (part,), got = _hosted_call(body, name, (R // rc,), [blk, pl.BlockSpec((2, rc, C), lambda i: (0, i, 0))], [blk],
                                [jax.ShapeDtypeStruct((R, C), F32)], (own, recvb), tasks)
    return part, got


def _sum_adamw(own, recvb, w, m, v, name, done=0):
    R, C = own.shape
    rc = _sum_adamw_rows(R)
    left = 3 - done
    assert 3 % left == 0

    def body(own_ref, r_ref, w_ref, m_ref, v_ref, g_ref, d_ref, nm_ref, nv_ref):
        g = own_ref[...]
        for j in range(left):
            g = g + r_ref[j].astype(F32)
        g_ref[...] = g
        d_ref[...], nm_ref[...], nv_ref[...] = _adam_update(w_ref[...], g, m_ref[...], v_ref[...])

    blk = pl.BlockSpec((rc, C), lambda i: (i, 0))
    s = jax.ShapeDtypeStruct((R, C), F32)
    return pl.pallas_call(
        body, name=name, grid=(R // rc,),
        in_specs=[blk, pl.BlockSpec((left, rc, C), lambda i: (done // left, i, 0)), blk, blk, blk],
        out_specs=[blk, blk, blk, blk], out_shape=[s, s, s, s], compiler_params=_params("parallel"),
    )(own, recvb, w, m, v)


def _by_device(dw):
    return dw.reshape(N_DEV, dw.shape[0] // N_DEV, dw.shape[1])


def _local_step(x, mod, g_attn, w_in, bias, sw_sink, g_na_out, g_sw_out, w_out, g_ffn, w_up, conv_w, conv_b, w_down,
                g_final, target, sharded):
    B, S, D = x.shape
    T = B * S
    x2d = x.reshape(T, D)
    mod3 = mod.reshape(B, 6, D)
    cos_t, sin_t = _rope_tables(S)
    sink = sw_sink.reshape(SW_HEADS)
    n_tiles = T // WIDE_TILE
    full = lambda g: g.reshape(N_DEV * g.shape[1], g.shape[2])

    rider = lambda w, mid, lo, n, into=None: [_gather_task(w, mid, rows=(lo, n), into=into)] if sharded else []
    if sharded:
        qu, hd = w_up.shape[0] // 4, w_down.shape[0] // 2
    (h, qkv), got = _attn_in(x2d, mod3, g_attn, w_in, cos_t, sin_t, S,
                             [_gather_task(w_out, n_tiles // 2)] + rider(w_up, n_tiles - 1, 0, qu) if sharded else [])
    if sharded:
        w_out, w_up_buf = full(got[0][0]), got[1][0]
    qkv3 = qkv.reshape(B, S, IN_WIDTH)
    na_steps, sw_steps = B * (NA_PAIRS // NA_STEP_PAIRS), B * SW_FWD_SPLIT
    (oa,), got = _na_fwd(qkv3, bias, rider(w_up, na_steps - 1, qu, 2 * qu, w_up_buf) if sharded else [])
    if sharded:
        w_up_buf = got[0][0]
    oa = oa.reshape(T, NA_WIDTH)
    (ob,), got = _sw_fwd(sink, qkv3, rider(w_up, sw_steps // 2, 3 * qu, qu, w_up_buf) if sharded else [])
    if sharded:
        w_up = full(got[0][0])
    ob = ob.reshape(T, SW_WIDTH)
    (mixin, mix, x1), _ = _attn_out(oa, ob, x2d, mod3, g_na_out, g_sw_out, w_out, S)
    (h2, val, gt), got = _ffn_up(x1, mod3, g_ffn, w_up, S, rider(w_down, 3 * n_tiles // 4, 0, 2 * hd) if sharded else [])
    if sharded:
        w_down = full(got[0][0])
    a, act, vd, dx2, df, gstat_f, bstat_f = _ffn_down(gt, val, conv_w, conv_b, w_down, x1, mod3, g_final, target.reshape(T, D), B, S)
    F = val.shape[1]

    dw_down = _matmul_tn(a, df, "dw_down")
    (dval, dgc, cstat), got = _ffn_down_bwd(df, w_down, act, vd, [_swap_task(_by_device(dw_down))] if sharded else [])
    if sharded:
        send_down, own_down = _chip_sums(_by_device(dw_down), got[0][0])
    (du, dx1, dmix, gstat_u, bstat_u, cstat_w), got = _ffn_up_bwd(dgc, dval, gt, conv_w, w_up, x1, mod3, g_ffn, dx2, mix, B, S,
                                                                  [_exchange_task(send_down)] if sharded else [])
    if sharded:
        dw_down = (own_down, got[0][0])
    dw_up = _matmul_tn(du, h2, "dw_up", tm=F)
    dw_out = _matmul_tn(mixin, dmix, "dw_out")
    (doa, dob, gstat_o), got = _attn_out_bwd(dmix, w_out, oa, ob, g_na_out, g_sw_out,
                                             [_swap_task(_by_device(dw_up)), _swap_task(_by_device(dw_out))] if sharded else [])
    if sharded:
        send_up, own_up = _chip_sums(_by_device(dw_up), got[0][0])
        send_out, own_out = _chip_sums(_by_device(dw_out), got[1][0])
    (dqa, dka, dva, dbt), got = _na_bwd(qkv3, bias, doa.reshape(B, S, NA_WIDTH),
                                        [_exchange_task(send_up), _exchange_task(send_out)] if sharded else [])
    if sharded:
        dw_up, dw_out = (own_up, got[0][0]), (own_out, got[1][0])
    dqb, dkb, dvb, dsink = _sw_bwd(sink, qkv3, dob.reshape(B, S, SW_WIDTH))
    r2 = lambda t: t.reshape(T, t.shape[-1])
    grad_x, dproj, gstat_i, bstat_i = _attn_in_bwd(r2(dqa), r2(dka), r2(dva), r2(dqb), r2(dkb), r2(dvb), cos_t, sin_t, w_in, x2d, mod3,
                                                   g_attn, dx1, B, S)
    dw_in = _matmul_tn(dproj, h, "dw_in")

    dmod = jnp.stack([bstat_i[:, 0], bstat_i[:, 1], bstat_u[:, 2], bstat_u[:, 0], bstat_u[:, 1], bstat_f[:, 0]], axis=1)
    small = dict(g_attn=gstat_i[0], g_ffn=gstat_u[0], g_final=gstat_f[0], loss=gstat_f[1, 0], g_na_out=gstat_o[0], g_sw_out=gstat_o[1],
                 sw_sink=dsink[:, 0], conv_b=cstat[0], conv_w=cstat_w[1:4], dbt=dbt,
                 raw=(bstat_i, bstat_u, bstat_f, gstat_i, gstat_u, gstat_f, gstat_o, dsink, cstat, cstat_w))
    return grad_x.reshape(B, S, D), dict(w_in=dw_in, w_out=dw_out, w_up=dw_up, w_down=dw_down), dmod, small


def _pack_slab(raw, drpb):
    D, F = raw[3].shape[1], raw[8].shape[1]
    n_seq = raw[0].shape[0]

    def body(bi_ref, bu_ref, bf_ref, gi_ref, gu_ref, gf_ref, go_ref, ds_ref, cs_ref, cw_ref, rp_ref, o_ref):
        o_ref[...] = jnp.zeros_like(o_ref)
        for b in range(n_seq):
            mods = (bi_ref[b, 0:1, :], bi_ref[b, 1:2, :], bu_ref[b, 2:3, :], bu_ref[b, 0:1, :], bu_ref[b, 1:2, :], bf_ref[b, 0:1, :])
            for k, row in enumerate(mods):
                o_ref[b:b + 1, k * D:(k + 1) * D] = row
        o = 0
        for row in (gi_ref[0:1, :], gu_ref[0:1, :], gf_ref[0:1, :], go_ref[0:1, :], go_ref[1:2, :]):
            o_ref[2:3, o:o + row.shape[1]] = row
            o += row.shape[1]
        ds = ds_ref[...]
        eye = lax.broadcasted_iota(jnp.int32, ds.shape, 0) == lax.broadcasted_iota(jnp.int32, ds.shape, 1)
        o_ref[2:3, o:o + 128] = jnp.sum(jnp.where(eye, ds, 0.0), axis=0, keepdims=True)
        o_ref[2:3, o + 128:o + 256] = gf_ref[1:2, 0:128]
        o_ref[3:4, 0:F] = cs_ref[0:1, :]
        o_ref[4:5, 0:rp_ref.shape[1]] = rp_ref[...]
        o_ref[5:8, 0:F] = cw_ref[1:4, :]

    return pl.pallas_call(body, name="pack_slab", out_shape=jax.ShapeDtypeStruct((8, PACK_W), F32),
                          compiler_params=pltpu.CompilerParams(vmem_limit_bytes=VMEM_LIMIT))(*raw, drpb)


def kernel(x, c, w_ada, b_ada, g_attn, w_in, na_rpb, sw_sink, g_na_out, g_sw_out, w_out, g_ffn, w_up, conv_w, conv_b, w_down, g_final, loss_target, m_w_ada, m_b_ada, m_g_attn, m_w_in, m_na_rpb, m_sw_sink, m_g_na_out, m_g_sw_out, m_w_out, m_g_ffn, m_w_up, m_conv_w, m_conv_b, m_w_down, m_g_final, v_w_ada, v_b_ada, v_g_attn, v_w_in, v_na_rpb, v_sw_sink, v_g_na_out, v_g_sw_out, v_w_out, v_g_ffn, v_w_up, v_conv_w, v_conv_b, v_w_down, v_g_final):
    B, S, D = x.shape
    me = 4 * lax.axis_index("x") + 2 * lax.axis_index("y") + lax.axis_index("c")
    ada_c = w_ada.shape[2]
    F_l = conv_w.shape[2]

    cw_l = jnp.pad(conv_w[0], ((0, 8 - conv_w.shape[1]), (0, 0)))
    c_l = jnp.pad(c, ((0, 8 - B), (0, 0)))
    tr = {"w_in", "w_up"}
    w_in_t = jnp.transpose(w_in[0])
    shards = dict(w_out=w_out[0].astype(BF16), w_up=jnp.transpose(w_up[0]).astype(BF16), w_down=w_down[0].astype(BF16))

    b_ada_l = lax.dynamic_slice(b_ada, (0, me * ada_c), (1, ada_c))
    slabs, mod_all, w_in_all = _ada_fwd(jnp.concatenate([c_l, cw_l], axis=1), w_ada[0], b_ada_l, w_in_t.astype(BF16), B)
    c_all = slabs[:, :, :D].reshape(N_DEV * 8, D)
    conv_w_f = jnp.transpose(slabs[:, :3, D:], (1, 0, 2)).reshape(3, N_DEV * F_l)
    mod_mine = lax.dynamic_slice(mod_all, (0, me * B, 0), (N_DEV, B, ada_c))
    mod = jnp.transpose(mod_mine, (1, 0, 2)).reshape(B, N_DEV * ada_c)
    w_in_f = w_in_all.reshape(N_DEV * w_in_t.shape[0], D)

    bias = _na_bias_table(na_rpb[0])

    grad_x, dw, dmod, small = _local_step(x, mod, g_attn, w_in_f, bias, sw_sink, g_na_out, g_sw_out, shards["w_out"], g_ffn,
                                          shards["w_up"], conv_w_f, conv_b, shards["w_down"], g_final.reshape(1, D), loss_target,
                                          sharded=True)
    g8_in = _by_device(dw["w_in"])
    drpb, got = _na_bias_grad(small["dbt"], [_swap_task(g8_in)])
    send_in, own_in = _chip_sums(g8_in, got[0][0])

    slab = _pack_slab(small["raw"], drpb.reshape(1, -1))
    weights = dict(w_ada=w_ada, b_ada=b_ada, g_attn=g_attn, w_in=w_in, na_rpb=na_rpb, sw_sink=sw_sink, g_na_out=g_na_out,
                   g_sw_out=g_sw_out, w_out=w_out, g_ffn=g_ffn, w_up=w_up, conv_w=conv_w, conv_b=conv_b, w_down=w_down, g_final=g_final)
    ms = dict(w_ada=m_w_ada, b_ada=m_b_ada, g_attn=m_g_attn, w_in=m_w_in, na_rpb=m_na_rpb, sw_sink=m_sw_sink, g_na_out=m_g_na_out,
              g_sw_out=m_g_sw_out, w_out=m_w_out, g_ffn=m_g_ffn, w_up=m_w_up, conv_w=m_conv_w, conv_b=m_conv_b, w_down=m_w_down, g_final=m_g_final)
    vs = dict(w_ada=v_w_ada, b_ada=v_b_ada, g_attn=v_g_attn, w_in=v_w_in, na_rpb=v_na_rpb, sw_sink=v_sw_sink, g_na_out=v_g_na_out,
              g_sw_out=v_g_sw_out, w_out=v_w_out, g_ffn=v_g_ffn, w_up=v_w_up, conv_w=v_conv_w, conv_b=v_conv_b, w_down=v_w_down, g_final=v_g_final)
    names = list(weights)
    grads, deltas, new_m, new_v = {}, {}, {}, {}
    flat = lambda t: t.reshape(1, -1)

    def shard2d(nm):
        if nm in tr:
            return (lambda t: jnp.transpose(t[0])), (lambda t: jnp.transpose(t)[None])
        return (lambda t: t[0]), (lambda t: t[None])

    def finish_sum(nm, own, recvb, done=0):
        r, back = shard2d(nm)
        g2, d_, m_, v_ = _sum_adamw(own, recvb, r(weights[nm]), r(ms[nm]), r(vs[nm]), "adamw_" + nm, done)
        grads[nm], deltas[nm], new_m[nm], new_v[nm] = back(g2), back(d_), back(m_), back(v_)

    own_up, recv_up = dw["w_up"]
    n_up = _sum_adamw_steps(own_up.shape[0])
    part_up, got = _sum_first(own_up, recv_up, "sum_first_w_up", [_exchange_task(send_in), _gather_task(slab, n_up - 1)])
    packs = got[1][0]
    finish_sum("w_up", part_up, recv_up, done=2)
    finish_sum("w_down", *dw["w_down"])
    finish_sum("w_in", own_in, got[0][0])
    finish_sum("w_out", *dw["w_out"])

    where = dict(b_ada=((0, 1), 0), g_attn=((2,), 0), g_ffn=((2,), D), g_final=((2,), 2 * D), g_na_out=((2,), 3 * D),
                 g_sw_out=((2,), 3 * D + NA_WIDTH), sw_sink=((2,), 3 * D + NA_WIDTH + SW_WIDTH), conv_b=((3,), 0), na_rpb=((4,), 0))

    def small_view(n):
        if n == "na_rpb":
            return (lambda t: jnp.transpose(t[0], (1, 0, 2))), (lambda t: jnp.transpose(t, (1, 0, 2))[None])
        return flat, (lambda t: t.reshape(weights[n].shape))

    tot, loss, small_out = _pack_sum_adamw(
        packs.reshape(N_DEV * 8, PACK_W),
        [tuple(small_view(n)[0](t[n]) for t in (weights, ms, vs)) + where[n] for n in where],
        (2, 3 * D + NA_WIDTH + SW_WIDTH + 128))
    for n, res in zip(where, small_out):
        grads[n], deltas[n], new_m[n], new_v[n] = [small_view(n)[1](t) for t in res]
    loss = loss.reshape(())

    dmod_cols = lax.dynamic_slice(packs.reshape(N_DEV * 8, PACK_W), (0, me * ada_c), (N_DEV * 8, ada_c))
    for nm, g2 in (("w_ada", _ada_bwd(c_all, dmod_cols)), ("conv_w", lax.dynamic_slice(tot, (5, me * F_l), (3, F_l)))):
        r, back = shard2d(nm)
        d_, m_, v_ = _adamw(r(weights[nm]), g2, r(ms[nm]), r(vs[nm]), "adamw_" + nm)
        grads[nm], deltas[nm], new_m[nm], new_v[nm] = back(g2), back(d_), back(m_), back(v_)
    return (loss, grad_x, *[grads[n] for n in names], *[deltas[n] for n in names], *[new_m[n] for n in names],
            *[new_v[n] for n in names])
```

```python
import functools

import numpy as np
import jax
import jax.numpy as jnp
from jax import lax
from jax.experimental import pallas as pl
from jax.experimental.pallas import tpu as pltpu

F32, BF16 = jnp.float32, jnp.bfloat16
MESH_ID = pl.DeviceIdType.MESH
N_DEV = 8

HEAD_DIM = 64
NA_HEADS = 8
SW_HEADS = 8
SW_KV_HEADS = 2
SW_GROUP = SW_HEADS // SW_KV_HEADS
NA_WIDTH = NA_HEADS * HEAD_DIM
SW_WIDTH = SW_HEADS * HEAD_DIM
SW_KV_WIDTH = SW_KV_HEADS * HEAD_DIM
ROPE_WIDTH = SW_WIDTH + SW_KV_WIDTH
IN_WIDTH = 3 * NA_WIDTH + SW_WIDTH + 2 * SW_KV_WIDTH
ROPE_LO = 3 * NA_WIDTH
GRID_W = 64
NA_ROWS_MAX = 8
NA_COLS = 16
N_DR = 2 * NA_ROWS_MAX - 1
N_DC = 2 * NA_COLS - 1
SW_WINDOW = 128
SW_BLOCK = 128
ROPE_THETA = 10000.0
EPS = 1e-6
NEG = -1e30
Q_SCALE = HEAD_DIM ** -0.5

ADAM_LR = 0.001
ADAM_B1 = 0.9
ADAM_B2 = 0.999
ADAM_EPS = 1e-08
ADAM_WD = 0.01
ADAM_STEP = 10

TOKEN_TILE = 256
WIDE_TILE = 512
VMEM_LIMIT = 56 * 1024 * 1024

PACK_W = 6144


def _nn(a, b):
    return jnp.dot(a, b, preferred_element_type=F32)


def _nt(a, b):
    return lax.dot_general(a, b, (((1,), (1,)), ((), ())), preferred_element_type=F32)


def _tn(a, b):
    return lax.dot_general(a, b, (((0,), (0,)), ((), ())), preferred_element_type=F32)


def _rms(x):
    r = lax.rsqrt(jnp.mean(x * x, axis=-1, keepdims=True) + EPS)
    return x * r, r


def _rms_bwd(xn, r, gy):
    return r * (gy - xn * jnp.mean(xn * gy, axis=-1, keepdims=True))


def _params(*sem):
    return pltpu.CompilerParams(dimension_semantics=sem, vmem_limit_bytes=VMEM_LIMIT)


def _full(shape):
    n = len(shape)
    return pl.BlockSpec(shape, lambda *_: (0,) * n)


def _mesh_pos():
    return lax.axis_index("x"), lax.axis_index("y"), lax.axis_index("c")


def _row_chunk(r):
    for rc in (128, 64, 32, 16):
        if r % rc == 0:
            return rc
    raise ValueError(f"rows {r} not a multiple of 16")


class _Task:
    def __init__(self, inputs, out_shapes, sems, start, finish, mid=None, mid_step=None, alias=None):
        self.inputs, self.out_shapes, self.sems = list(inputs), list(out_shapes), list(sems)
        self.start, self.finish, self.mid, self.mid_step = start, finish, mid, mid_step
        self.alias = alias


def _hosted_call(body, name, grid, in_specs, out_specs, out_shape, operands, tasks, scratch_shapes=()):
    n_in, n_out, n_scr = len(in_specs), len(out_specs), len(scratch_shapes)
    t_in = [len(t.inputs) for t in tasks]
    t_out = [len(t.out_shapes) for t in tasks]
    t_sem = [len(t.sems) for t in tasks]
    n_steps = int(np.prod(grid))

    def wrapped(*refs):
        ins, rest = refs[:n_in], refs[n_in:]
        task_ins, rest = rest[:sum(t_in)], rest[sum(t_in):]
        outs, rest = rest[:n_out], rest[n_out:]
        task_outs, rest = rest[:sum(t_out)], rest[sum(t_out):]
        scr, task_sems = rest[:n_scr], rest[n_scr:]
        step = pl.program_id(0)
        for ax in range(1, len(grid)):
            step = step * grid[ax] + pl.program_id(ax)
        parts = []
        oi = oo = os_ = 0
        for t, a, b, c in zip(tasks, t_in, t_out, t_sem):
            parts.append((t, task_ins[oi:oi + a], task_outs[oo:oo + b], task_sems[os_:os_ + c]))
            oi, oo, os_ = oi + a, oo + b, os_ + c
        for t, ti, to, ts in parts:
            pl.when(step == 0)(functools.partial(t.start, ti, to, ts))
            if t.mid is not None:
                pl.when(step == t.mid_step)(functools.partial(t.mid, ti, to, ts))
        body(*ins, *outs, *scr)
        for t, ti, to, ts in parts:
            pl.when(step == n_steps - 1)(functools.partial(t.finish, ti, to, ts))

    hbm = pl.BlockSpec(memory_space=pl.ANY)
    aliases, oi, oo = {}, n_in, n_out
    for t, a, b in zip(tasks, t_in, t_out):
        if t.alias is not None:
            aliases[oi + t.alias[0]] = oo + t.alias[1]
        oi, oo = oi + a, oo + b
    res = pl.pallas_call(
        wrapped, name=name, grid=grid,
        in_specs=list(in_specs) + [hbm] * sum(t_in),
        out_specs=list(out_specs) + [hbm] * sum(t_out),
        out_shape=list(out_shape) + [s for t in tasks for s in t.out_shapes],
        scratch_shapes=list(scratch_shapes) + [s for t in tasks for s in t.sems],
        input_output_aliases=aliases,
        compiler_params=_params(*(["arbitrary"] * len(grid))),
    )(*operands, *[a for t in tasks for a in t.inputs])
    own, extra = res[:n_out], res[n_out:]
    per_task, o = [], 0
    for b in t_out:
        per_task.append(extra[o:o + b])
        o += b
    return own, per_task


def _gather_task(shard, mid_step, rows=None, into=None):
    lo, n = (0, shard.shape[0]) if rows is None else rows

    def parts(ins, outs, sems):
        x_ref, out_ref, (send_sems, recv_sems, local_sem) = ins[0], outs[0], sems
        x_, y_, c_ = _mesh_pos()
        me, sibling = (x_, y_, c_), (x_, y_, 1 - c_)
        chips = [(1 - x_, y_), (x_, 1 - y_), (1 - x_, 1 - y_)]
        x_ref = x_ref.at[pl.ds(lo, n)]

        def rows(px, py, pc):
            return out_ref.at[4 * px + 2 * py + pc, pl.ds(lo, n)]

        def copy(k, block, to, src=None):
            return pltpu.make_async_remote_copy(
                src_ref=rows(*block) if src is None else src, dst_ref=rows(*block),
                send_sem=send_sems.at[k], recv_sem=recv_sems.at[k], device_id=to, device_id_type=MESH_ID)

        return dict(
            mine=lambda: pltpu.make_async_copy(x_ref, rows(*me), local_sem),
            first=lambda: [copy(0, me, sibling, src=x_ref)] + [copy(1 + j, me, (*chip, c_), src=x_ref) for j, chip in enumerate(chips)],
            passed=lambda: [copy(4 + j, (*chip, c_), sibling) for j, chip in enumerate(chips)],
            landed=lambda: [copy(1 + j, (*chip, c_), me) for j, chip in enumerate(chips)],
            last=lambda: [copy(0, sibling, me)] + [copy(4 + j, (*chip, 1 - c_), me) for j, chip in enumerate(chips)])

    def start(ins, outs, sems):
        p = parts(ins, outs, sems)
        p["mine"]().start()
        for cp in p["first"]():
            cp.start()

    def mid(ins, outs, sems):
        p = parts(ins, outs, sems)
        for cp, fw in zip(p["landed"](), p["passed"]()):
            cp.wait_recv()
            fw.start()

    def finish(ins, outs, sems):
        p = parts(ins, outs, sems)
        for cp in p["last"]():
            cp.wait_recv()
        for cp in p["first"]() + p["passed"]():
            cp.wait_send()
        p["mine"]().wait()

    return _Task([shard] if into is None else [shard, into], [jax.ShapeDtypeStruct((N_DEV,) + shard.shape, shard.dtype)],
                 [pltpu.SemaphoreType.DMA((7,)), pltpu.SemaphoreType.DMA((7,)), pltpu.SemaphoreType.DMA],
                 start, finish, mid, mid_step, alias=None if into is None else (1, 0))


def _swap_task(g8):
    _, R, C = g8.shape

    def copies(ins, outs, sems):
        (g_ref,), (recv_ref,), (ss, rs) = ins, outs, sems
        x_, y_, c_ = _mesh_pos()
        return [pltpu.make_async_remote_copy(src_ref=g_ref.at[2 * k + (1 - c_)], dst_ref=recv_ref.at[k], send_sem=ss.at[k],
                                             recv_sem=rs.at[k], device_id=(x_, y_, 1 - c_), device_id_type=MESH_ID)
                for k in range(4)]

    def start(ins, outs, sems):
        for cp in copies(ins, outs, sems):
            cp.start()

    def finish(ins, outs, sems):
        cps = copies(ins, outs, sems)
        for cp in cps:
            cp.wait_recv()
        for cp in cps:
            cp.wait_send()

    return _Task([g8], [jax.ShapeDtypeStruct((4, R, C), g8.dtype)],
                 [pltpu.SemaphoreType.DMA((4,)), pltpu.SemaphoreType.DMA((4,))], start, finish)


def _chip_sums(g8, recva):
    _, R, C = g8.shape
    rt = _sum_adamw_rows(R)
    rc = _row_chunk(rt)

    def body(core_ref, g_ref, a_ref, send_ref, own_ref):
        x_, y_, _ = _mesh_pos()
        chips = [(1 - x_, y_), (x_, 1 - y_), (1 - x_, 1 - y_), (x_, y_)]

        def chunk(i, carry):
            rows = pl.ds(pl.multiple_of(i * rc, rc), rc)
            for j, (tx, ty) in enumerate(chips):
                k = 2 * tx + ty
                s = g_ref[k, rows, :].astype(F32) + a_ref[k, rows, :].astype(F32)
                if j < 3:
                    send_ref[j, rows, :] = s.astype(BF16)
                else:
                    own_ref[rows, :] = s
            return carry

        lax.fori_loop(0, rt // rc, chunk, 0)

    core = lax.axis_index("c").astype(jnp.int32).reshape(1)
    return pl.pallas_call(
        body, name="chip_sums",
        grid_spec=pltpu.PrefetchScalarGridSpec(
            num_scalar_prefetch=1, grid=(R // rt,),
            in_specs=[pl.BlockSpec((4, None, rt, C), lambda i, c: (0, c[0], i, 0)), pl.BlockSpec((4, rt, C), lambda i, c: (0, i, 0))],
            out_specs=[pl.BlockSpec((3, rt, C), lambda i, c: (0, i, 0)), pl.BlockSpec((rt, C), lambda i, c: (i, 0))]),
        out_shape=[jax.ShapeDtypeStruct((3, R, C), BF16), jax.ShapeDtypeStruct((R, C), F32)],
        compiler_params=_params("parallel"),
    )(core, g8.reshape(4, 2, R, C), recva)


def _exchange_task(sendb):
    def copies(ins, outs, sems):
        (s_ref,), (recv_ref,), (ss, rs) = ins, outs, sems
        x_, y_, c_ = _mesh_pos()
        flips = [(1 - x_, y_), (x_, 1 - y_), (1 - x_, 1 - y_)]
        return [pltpu.make_async_remote_copy(src_ref=s_ref.at[j], dst_ref=recv_ref.at[j], send_sem=ss.at[j], recv_sem=rs.at[j],
                                             device_id=(tx, ty, c_), device_id_type=MESH_ID) for j, (tx, ty) in enumerate(flips)]

    def start(ins, outs, sems):
        for cp in copies(ins, outs, sems):
            cp.start()

    def finish(ins, outs, sems):
        cps = copies(ins, outs, sems)
        for cp in cps:
            cp.wait_recv()
        for cp in cps:
            cp.wait_send()

    return _Task([sendb], [jax.ShapeDtypeStruct(sendb.shape, sendb.dtype)],
                 [pltpu.SemaphoreType.DMA((3,)), pltpu.SemaphoreType.DMA((3,))], start, finish)


def _silu(v):
    return v * (1.0 / (1.0 + jnp.exp(-v)))


def _ada_fwd(c_slab, w_ada_l, b_ada_l, w_in_shard, n_seq):
    W = c_slab.shape[1]
    D, cols = w_ada_l.shape
    n_rows = N_DEV * n_seq
    t_c, t_w = _gather_task(c_slab, 0), _gather_task(w_in_shard, 0)
    t_m = _gather_task(jax.ShapeDtypeStruct((n_rows, cols), F32), 0)

    def body(c_ref, w_ref, b_ref, ws_ref, slabs_ref, mod_ref, win_ref, c_vm, m_vm, copy_sem, *sems):
        sc, sw, sm = sems[0:3], sems[3:6], sems[6:9]
        t_c.start((c_ref,), (slabs_ref,), sc)
        t_w.start((ws_ref,), (win_ref,), sw)
        t_c.mid((c_ref,), (slabs_ref,), sc)
        t_c.finish((c_ref,), (slabs_ref,), sc)
        cp = pltpu.make_async_copy(slabs_ref, c_vm, copy_sem)
        cp.start()
        cp.wait()
        c_all = c_vm[:, :, 0:D].reshape(N_DEV * 8, D)
        m64 = jnp.dot(_silu(c_all), w_ref[...], precision=lax.Precision.HIGHEST, preferred_element_type=F32) + b_ref[...]
        r = lax.broadcasted_iota(jnp.int32, (n_rows, N_DEV * 8), 0)
        c = lax.broadcasted_iota(jnp.int32, (n_rows, N_DEV * 8), 1)
        pick = jnp.where(c == 8 * (r // n_seq) + r % n_seq, 1.0, 0.0)
        m_vm[...] = jnp.dot(pick, m64, precision=lax.Precision.HIGHEST, preferred_element_type=F32)
        t_m.start((m_vm,), (mod_ref,), sm)
        t_w.mid((ws_ref,), (win_ref,), sw)
        t_m.mid((m_vm,), (mod_ref,), sm)
        t_m.finish((m_vm,), (mod_ref,), sm)
        t_w.finish((ws_ref,), (win_ref,), sw)

    hbm, vm = pl.BlockSpec(memory_space=pl.ANY), pl.BlockSpec(memory_space=pltpu.VMEM)
    return pl.pallas_call(
        body, name="ada_fwd", in_specs=[hbm, vm, vm, hbm], out_specs=[hbm, hbm, hbm],
        out_shape=t_c.out_shapes + t_m.out_shapes + t_w.out_shapes,
        scratch_shapes=[pltpu.VMEM((N_DEV, 8, W), F32), pltpu.VMEM((n_rows, cols), F32), pltpu.SemaphoreType.DMA]
        + t_c.sems + t_w.sems + t_m.sems,
        compiler_params=pltpu.CompilerParams(vmem_limit_bytes=VMEM_LIMIT),
    )(c_slab, w_ada_l, b_ada_l, w_in_shard)


def _ada_bwd(c_all, dmod_cols):
    def body(c_ref, d_ref, o_ref):
        o_ref[...] = lax.dot_general(_silu(c_ref[...]), d_ref[...], (((0,), (0,)), ((), ())),
                                     precision=lax.Precision.HIGHEST, preferred_element_type=F32)
    return pl.pallas_call(body, name="ada_bwd", out_shape=jax.ShapeDtypeStruct((c_all.shape[1], dmod_cols.shape[1]), F32),
                          compiler_params=pltpu.CompilerParams(vmem_limit_bytes=VMEM_LIMIT))(c_all, dmod_cols)


NA_PAIRS = NA_HEADS // 2
N_DR_PAD = 16


def _na_bias_table(na_rpb):
    rev = jnp.pad(jnp.flip(na_rpb, axis=2), ((0, 0), (0, N_DR_PAD - N_DR), (0, GRID_W - N_DC)))
    rev = jnp.transpose(rev.reshape(NA_PAIRS, 2, N_DR_PAD, GRID_W), (0, 2, 1, 3)).reshape(NA_PAIRS, N_DR_PAD, 128)

    def body(r_ref, o_ref):
        k = lax.broadcasted_iota(jnp.int32, (GRID_W, 128), 0)
        lane = lax.broadcasted_iota(jnp.int32, (GRID_W, 128), 1)
        q = lane % GRID_W
        cs = jnp.clip(q - NA_COLS // 2, 0, GRID_W - NA_COLS)
        ok = (k >= cs) & (k < cs + NA_COLS)
        left = lane < GRID_W
        for dr in range(N_DR):
            row = jnp.broadcast_to(r_ref[0, dr:dr + 1, :], (GRID_W, 128))
            r0 = jnp.where(left, row, 0.0)
            r1 = jnp.where(left, pltpu.roll(row, GRID_W, axis=1), 0.0)
            y0 = pltpu.roll(r0, 128 - (NA_COLS - 1), axis=1, stride=1, stride_axis=0)
            y1 = pltpu.roll(r1, GRID_W - (NA_COLS - 1), axis=1, stride=1, stride_axis=0)
            o_ref[0, dr * GRID_W:(dr + 1) * GRID_W, :] = jnp.where(ok, jnp.where(left, y0, y1), NEG)

    return pl.pallas_call(
        body, name="rpb_expand", grid=(NA_PAIRS,),
        in_specs=[pl.BlockSpec((1, N_DR_PAD, 128), lambda p: (p, 0, 0))],
        out_specs=pl.BlockSpec((1, N_DR * GRID_W, 128), lambda p: (p, 0, 0)),
        out_shape=jax.ShapeDtypeStruct((NA_PAIRS, N_DR * GRID_W, 128), F32),
        compiler_params=_params("parallel"),
    )(rev)


def _na_bias_grad(db, tasks=()):
    a = np.arange(128)
    flip = jnp.asarray(((a[:, None] // GRID_W == a[None, :] // GRID_W)
                        & (a[:, None] % GRID_W + a[None, :] % GRID_W == GRID_W - 1)).astype(np.float32))

    def body(d_ref, j_ref, o_ref):
        o_ref[...] = jnp.zeros_like(o_ref)
        for dr in range(N_DR):
            t = jnp.dot(d_ref[0, dr * GRID_W:(dr + 1) * GRID_W, :], j_ref[...], precision=lax.Precision.HIGHEST, preferred_element_type=F32)
            t = pltpu.roll(t, GRID_W + NA_COLS, axis=1, stride=1, stride_axis=0)
            o_ref[0, dr:dr + 1, :] = jnp.sum(t, axis=0, keepdims=True)

    (rows,), got = _hosted_call(
        body, "rpb_reduce", (NA_PAIRS,),
        [pl.BlockSpec((1, N_DR * GRID_W, 128), lambda p: (p, 0, 0)), _full((128, 128))],
        [pl.BlockSpec((1, N_DR_PAD, 128), lambda p: (p, 0, 0))],
        [jax.ShapeDtypeStruct((NA_PAIRS, N_DR_PAD, 128), F32)], (db, flip), tasks)
    g = rows.reshape(NA_PAIRS, N_DR_PAD, 2, GRID_W)[:, :N_DR, :, :N_DC]
    return jnp.transpose(g, (0, 2, 1, 3)).reshape(-1), got


def _rope_tables(S):
    half = HEAD_DIM // 2
    inv = np.float32(ROPE_THETA) ** (-np.arange(half, dtype=np.float32) / np.float32(half))
    ang = np.arange(S).astype(np.float32)[:, None] * inv[None, :]
    cos, sin = np.cos(ang).astype(np.float32), np.sin(ang).astype(np.float32)
    return jnp.asarray(np.tile(np.concatenate([cos, cos], axis=1), (1, 2))), jnp.asarray(np.tile(np.concatenate([-sin, sin], axis=1), (1, 2)))


def _rope_spec(tps, tm=TOKEN_TILE):
    return pl.BlockSpec((tm, 2 * HEAD_DIM), lambda i: (i % tps, 0))


def _rot_half(t):
    w = t.shape[1]
    lane = lax.broadcasted_iota(jnp.int32, t.shape, 1)
    return jnp.where((lane % HEAD_DIM) < HEAD_DIM // 2, pltpu.roll(t, w - HEAD_DIM // 2, axis=1),
                     pltpu.roll(t, HEAD_DIM // 2, axis=1))


def _tok_spec(w, tm=TOKEN_TILE):
    return pl.BlockSpec((tm, w), lambda i: (i, 0))


def _mod_spec(tps, d):
    return pl.BlockSpec((1, 6, d), lambda i: (i // tps, 0, 0))


def _bstat_spec(tps, w):
    return pl.BlockSpec((1, 8, w), lambda i: (i // tps, 0, 0))


def _attn_in(x2d, mod3, g_attn, w_in, cos_t, sin_t, S, tasks=(), tm=WIDE_TILE):
    T, D = x2d.shape
    tps = S // tm

    def body(x_ref, mod_ref, g_ref, w_ref, cos_ref, sin_ref, h_ref, qkv_ref):
        xn, _ = _rms(x_ref[...])
        h = (xn * g_ref[...]) * (1.0 + mod_ref[0, 1:2, :]) + mod_ref[0, 0:1, :]
        hb = h.astype(BF16)
        h_ref[...] = hb
        proj = _nt(hb, w_ref[...])
        rb = proj[:, ROPE_LO:ROPE_LO + ROPE_WIDTH]
        reps = (1, ROPE_WIDTH // (2 * HEAD_DIM))
        rb = rb * jnp.tile(cos_ref[...], reps) + _rot_half(rb) * jnp.tile(sin_ref[...], reps)
        qkv_ref[:, 0:NA_WIDTH] = (proj[:, 0:NA_WIDTH] * Q_SCALE).astype(BF16)
        qkv_ref[:, NA_WIDTH:ROPE_LO] = proj[:, NA_WIDTH:ROPE_LO].astype(BF16)
        qkv_ref[:, ROPE_LO:ROPE_LO + SW_WIDTH] = (rb[:, 0:SW_WIDTH] * Q_SCALE).astype(BF16)
        qkv_ref[:, ROPE_LO + SW_WIDTH:ROPE_LO + ROPE_WIDTH] = rb[:, SW_WIDTH:].astype(BF16)
        qkv_ref[:, ROPE_LO + ROPE_WIDTH:] = proj[:, ROPE_LO + ROPE_WIDTH:].astype(BF16)

    return _hosted_call(
        body, "attn_in", (T // tm,),
        [_tok_spec(D, tm), _mod_spec(tps, D), _full((1, D)), _full(w_in.shape), _rope_spec(tps, tm), _rope_spec(tps, tm)],
        [_tok_spec(D, tm), _tok_spec(IN_WIDTH, tm)],
        [jax.ShapeDtypeStruct((T, D), BF16), jax.ShapeDtypeStruct((T, IN_WIDTH), BF16)],
        (x2d, mod3, g_attn, w_in, cos_t, sin_t), tasks)


def _attn_out(oa, ob, x2d, mod3, g_na, g_sw, w_out, S, tasks=(), tm=WIDE_TILE):
    T, D = x2d.shape
    tps = S // tm

    def body(oa_ref, ob_ref, x_ref, mod_ref, gna_ref, gsw_ref, w_ref, mixin_ref, mix_ref, x1_ref):
        oan, _ = _rms(oa_ref[...])
        obn, _ = _rms(ob_ref[...])
        mixin = jnp.concatenate([oan * gna_ref[...], obn * gsw_ref[...]], axis=1).astype(BF16)
        mixin_ref[...] = mixin
        mix = _nn(mixin, w_ref[...])
        mix_ref[...] = mix
        x1_ref[...] = x_ref[...] + mod_ref[0, 2:3, :] * mix

    return _hosted_call(
        body, "attn_out", (T // tm,),
        [_tok_spec(NA_WIDTH, tm), _tok_spec(SW_WIDTH, tm), _tok_spec(D, tm), _mod_spec(tps, D),
         _full((1, NA_WIDTH)), _full((1, SW_WIDTH)), _full(w_out.shape)],
        [_tok_spec(NA_WIDTH + SW_WIDTH, tm), _tok_spec(D, tm), _tok_spec(D, tm)],
        [jax.ShapeDtypeStruct((T, NA_WIDTH + SW_WIDTH), BF16), jax.ShapeDtypeStruct((T, D), F32), jax.ShapeDtypeStruct((T, D), F32)],
        (oa, ob, x2d, mod3, g_na, g_sw, w_out), tasks)


def _ffn_up(x1, mod3, g_ffn, w_up, S, tasks=(), tm=WIDE_TILE):
    T, D = x1.shape
    F = w_up.shape[0] // 2
    tps = S // tm

    def body(x1_ref, mod_ref, g_ref, w_ref, h2_ref, val_ref, gt_ref):
        xn, _ = _rms(x1_ref[...])
        h2 = ((xn * g_ref[...]) * (1.0 + mod_ref[0, 4:5, :]) + mod_ref[0, 3:4, :]).astype(BF16)
        h2_ref[...] = h2
        u = _nt(h2, w_ref[...])
        val_ref[...] = u[:, :F].astype(BF16)
        gt_ref[...] = u[:, F:].astype(BF16)

    return _hosted_call(
        body, "ffn_up", (T // tm,), [_tok_spec(D, tm), _mod_spec(tps, D), _full((1, D)), _full(w_up.shape)],
        [_tok_spec(D, tm), _tok_spec(F, tm), _tok_spec(F, tm)],
        [jax.ShapeDtypeStruct((T, D), BF16), jax.ShapeDtypeStruct((T, F), BF16), jax.ShapeDtypeStruct((T, F), BF16)],
        (x1, mod3, g_ffn, w_up), tasks)


def _halo_specs(T, tps, w):
    per = TOKEN_TILE // 8
    prev = pl.BlockSpec((8, w), lambda i: (jnp.maximum(i * per - 1, 0), 0))
    nxt = pl.BlockSpec((8, w), lambda i: (jnp.minimum((i + 1) * per, T // 8 - 1), 0))
    return prev, nxt


def _seq_shifts(cur, before, after, ti, tps):
    tm = cur.shape[0]
    row = lax.broadcasted_iota(jnp.int32, cur.shape, 0)
    before = jnp.where(ti > 0, before.astype(F32), 0.0)
    after = jnp.where(ti < tps - 1, after.astype(F32), 0.0)
    return jnp.where(row == 0, before, pltpu.roll(cur, 1, axis=0)), jnp.where(row == tm - 1, after, pltpu.roll(cur, tm - 1, axis=0))


def _ffn_down(gt, val, conv_w, conv_b, w_down, x1, mod3, g_final, target, B, S):
    T, D = x1.shape
    F = gt.shape[1]
    tps = S // TOKEN_TILE
    prev, nxt = _halo_specs(T, tps, F)

    def body(gt_ref, prev_ref, next_ref, val_ref, cw_ref, cb_ref, w_ref, x1_ref, mod_ref, gf_ref, tgt_ref,
             a_ref, act_ref, vd_ref, dx2_ref, df_ref, gstat_ref, bstat_ref):
        i = pl.program_id(0)
        g = gt_ref[...].astype(F32)
        gprev, gnext = _seq_shifts(g, prev_ref[7:8, :], next_ref[0:1, :], i % tps, tps)
        gc = gprev * cw_ref[0:1, :] + g * cw_ref[1:2, :] + gnext * cw_ref[2:3, :] + cb_ref[...]
        sig = 1.0 / (1.0 + jnp.exp(-gc))
        act = gc * sig
        val = val_ref[...].astype(F32)
        act_ref[...] = act.astype(BF16)
        vd_ref[...] = (val * (sig + act - act * sig)).astype(BF16)
        a = (act * val).astype(BF16)
        a_ref[...] = a
        f = _nn(a, w_ref[...])
        gate = mod_ref[0, 5:6, :]
        x2 = x1_ref[...] + gate * f
        xn, r = _rms(x2)
        err = xn * gf_ref[...] - tgt_ref[...]
        dy = err * (1.0 / D)
        dx2 = _rms_bwd(xn, r, dy * gf_ref[...])
        dx2_ref[...] = dx2
        df_ref[...] = (gate * dx2).astype(BF16)

        @pl.when(i == 0)
        def _():
            gstat_ref[...] = jnp.zeros_like(gstat_ref)

        @pl.when(i % tps == 0)
        def _():
            bstat_ref[...] = jnp.zeros_like(bstat_ref)

        gstat_ref[0:1, :] += jnp.sum(dy * xn, axis=0, keepdims=True)
        tile_loss = jnp.sum(jnp.sum(err * err, axis=1, keepdims=True), axis=0, keepdims=True) * (0.5 / D)
        gstat_ref[1:2, :] += jnp.broadcast_to(tile_loss, (1, D))
        bstat_ref[0, 0:1, :] += jnp.sum(dx2 * f, axis=0, keepdims=True)

    return pl.pallas_call(
        body, name="ffn_down", grid=(T // TOKEN_TILE,),
        in_specs=[_tok_spec(F), prev, nxt, _tok_spec(F), _full(conv_w.shape), _full((1, F)), _full(w_down.shape),
                  _tok_spec(D), _mod_spec(tps, D), _full((1, D)), _tok_spec(D)],
        out_specs=[_tok_spec(F), _tok_spec(F), _tok_spec(F), _tok_spec(D), _tok_spec(D), _full((8, D)), _bstat_spec(tps, D)],
        out_shape=[jax.ShapeDtypeStruct((T, F), BF16), jax.ShapeDtypeStruct((T, F), BF16), jax.ShapeDtypeStruct((T, F), BF16),
                   jax.ShapeDtypeStruct((T, D), F32), jax.ShapeDtypeStruct((T, D), BF16),
                   jax.ShapeDtypeStruct((8, D), F32), jax.ShapeDtypeStruct((B, 8, D), F32)],
        compiler_params=_params("arbitrary"),
    )(gt, gt, gt, val, conv_w, conv_b, w_down, x1, mod3, g_final, target)


def _ffn_down_bwd(df, w_down, act, vd, tasks=(), tm=WIDE_TILE):
    T, D = df.shape
    F = act.shape[1]

    def body(df_ref, w_ref, act_ref, vd_ref, dval_ref, dgc_ref, cstat_ref):
        da = _nt(df_ref[...], w_ref[...])
        dval_ref[...] = (da * act_ref[...].astype(F32)).astype(BF16)
        dgc = da * vd_ref[...].astype(F32)
        dgc_ref[...] = dgc.astype(BF16)

        @pl.when(pl.program_id(0) == 0)
        def _():
            cstat_ref[...] = jnp.zeros_like(cstat_ref)

        cstat_ref[0:1, :] += jnp.sum(dgc, axis=0, keepdims=True)

    return _hosted_call(
        body, "ffn_down_bwd", (T // tm,),
        [_tok_spec(D, tm), _full(w_down.shape), _tok_spec(F, tm), _tok_spec(F, tm)],
        [_tok_spec(F, tm), _tok_spec(F, tm), _full((8, F))],
        [jax.ShapeDtypeStruct((T, F), BF16), jax.ShapeDtypeStruct((T, F), BF16), jax.ShapeDtypeStruct((8, F), F32)],
        (df, w_down, act, vd), tasks)


def _ffn_up_bwd(dgc, dval, gt, conv_w, w_up, x1, mod3, g_ffn, dx2, mix, B, S, tasks=()):
    T, D = x1.shape
    F = dgc.shape[1]
    tps = S // TOKEN_TILE
    prev, nxt = _halo_specs(T, tps, F)

    def body(dgc_ref, prev_ref, next_ref, dval_ref, gt_ref, cw_ref, w_ref, x1_ref, mod_ref, g_ref, dx2_ref, mix_ref,
             du_ref, dx1_ref, dmix_ref, gstat_ref, bstat_ref, cstat_ref):
        i = pl.program_id(0)
        d = dgc_ref[...].astype(F32)
        dprev, dnext = _seq_shifts(d, prev_ref[7:8, :], next_ref[0:1, :], i % tps, tps)
        g = gt_ref[...].astype(F32)

        @pl.when(i == 0)
        def _():
            cstat_ref[...] = jnp.zeros_like(cstat_ref)

        cstat_ref[1:2, :] += jnp.sum(dnext * g, axis=0, keepdims=True)
        cstat_ref[2:3, :] += jnp.sum(d * g, axis=0, keepdims=True)
        cstat_ref[3:4, :] += jnp.sum(dprev * g, axis=0, keepdims=True)
        dgt = dnext * cw_ref[0:1, :] + d * cw_ref[1:2, :] + dprev * cw_ref[2:3, :]
        du = jnp.concatenate([dval_ref[...], dgt.astype(BF16)], axis=1)
        du_ref[...] = du
        dh2 = _nn(du, w_ref[...])
        xn, r = _rms(x1_ref[...])
        scale1 = 1.0 + mod_ref[0, 4:5, :]
        xg = xn * g_ref[...]
        dx1 = dx2_ref[...] + _rms_bwd(xn, r, dh2 * g_ref[...] * scale1)
        dx1_ref[...] = dx1
        dmix_ref[...] = (mod_ref[0, 2:3, :] * dx1).astype(BF16)

        @pl.when(i == 0)
        def _():
            gstat_ref[...] = jnp.zeros_like(gstat_ref)

        @pl.when(i % tps == 0)
        def _():
            bstat_ref[...] = jnp.zeros_like(bstat_ref)

        gstat_ref[0:1, :] += jnp.sum(dh2 * scale1 * xn, axis=0, keepdims=True)
        bstat_ref[0, 0:1, :] += jnp.sum(dh2, axis=0, keepdims=True)
        bstat_ref[0, 1:2, :] += jnp.sum(dh2 * xg, axis=0, keepdims=True)
        bstat_ref[0, 2:3, :] += jnp.sum(dx1 * mix_ref[...], axis=0, keepdims=True)

    return _hosted_call(
        body, "ffn_up_bwd", (T // TOKEN_TILE,),
        [_tok_spec(F), prev, nxt, _tok_spec(F), _tok_spec(F), _full(conv_w.shape), _full(w_up.shape), _tok_spec(D),
         _mod_spec(tps, D), _full((1, D)), _tok_spec(D), _tok_spec(D)],
        [_tok_spec(2 * F), _tok_spec(D), _tok_spec(D), _full((8, D)), _bstat_spec(tps, D), _full((8, F))],
        [jax.ShapeDtypeStruct((T, 2 * F), BF16), jax.ShapeDtypeStruct((T, D), F32), jax.ShapeDtypeStruct((T, D), BF16),
         jax.ShapeDtypeStruct((8, D), F32), jax.ShapeDtypeStruct((B, 8, D), F32), jax.ShapeDtypeStruct((8, F), F32)],
        (dgc, dgc, dgc, dval, gt, conv_w, w_up, x1, mod3, g_ffn, dx2, mix), tasks)


def _attn_out_bwd(dmix, w_out, oa, ob, g_na, g_sw, tasks=(), tm=WIDE_TILE):
    T, D = dmix.shape

    def body(dmix_ref, w_ref, oa_ref, ob_ref, gna_ref, gsw_ref, doa_ref, dob_ref, gstat_ref):
        dmixin = _nt(dmix_ref[...], w_ref[...])

        @pl.when(pl.program_id(0) == 0)
        def _():
            gstat_ref[...] = jnp.zeros_like(gstat_ref)

        for k, (o_ref, g_ref, do_ref) in enumerate(((oa_ref, gna_ref, doa_ref), (ob_ref, gsw_ref, dob_ref))):
            dn = dmixin[:, k * NA_WIDTH:(k + 1) * NA_WIDTH]
            on, r = _rms(o_ref[...])
            gstat_ref[k:k + 1, :] += jnp.sum(dn * on, axis=0, keepdims=True)
            do_ref[...] = _rms_bwd(on, r, dn * g_ref[...]).astype(BF16)

    hs = jax.ShapeDtypeStruct((T, NA_WIDTH), BF16)
    return _hosted_call(
        body, "attn_out_bwd", (T // tm,),
        [_tok_spec(D, tm), _full(w_out.shape), _tok_spec(NA_WIDTH, tm), _tok_spec(SW_WIDTH, tm), _full((1, NA_WIDTH)), _full((1, SW_WIDTH))],
        [_tok_spec(NA_WIDTH, tm), _tok_spec(SW_WIDTH, tm), _full((8, NA_WIDTH))],
        [hs, hs, jax.ShapeDtypeStruct((8, NA_WIDTH), F32)],
        (dmix, w_out, oa, ob, g_na, g_sw), tasks)


def _attn_in_bwd(dqa, dka, dva, dqb, dkb, dvb, cos_t, sin_t, w_in, x2d, mod3, g_attn, dx1, B, S, tm=WIDE_TILE):
    T, D = x2d.shape
    tps = S // tm

    def body(dqa_ref, dka_ref, dva_ref, dqb_ref, dkb_ref, dvb_ref, cos_ref, sin_ref, w_ref, x_ref, mod_ref, g_ref, dx1_ref,
             gx_ref, dproj_ref, gstat_ref, bstat_ref):
        i = pl.program_id(0)
        drb = jnp.concatenate([dqb_ref[...] * Q_SCALE, dkb_ref[...]], axis=1).astype(F32)
        reps = (1, ROPE_WIDTH // (2 * HEAD_DIM))
        drb = drb * jnp.tile(cos_ref[...], reps) + _rot_half(drb * jnp.tile(sin_ref[...], reps))
        dproj = jnp.concatenate([(dqa_ref[...] * Q_SCALE).astype(BF16), dka_ref[...].astype(BF16), dva_ref[...].astype(BF16),
                                 drb.astype(BF16), dvb_ref[...].astype(BF16)], axis=1)
        dproj_ref[...] = dproj
        dh = _nn(dproj, w_ref[...])
        xn, r = _rms(x_ref[...])
        scale1 = 1.0 + mod_ref[0, 1:2, :]
        gx_ref[...] = dx1_ref[...] + _rms_bwd(xn, r, dh * g_ref[...] * scale1)

        @pl.when(i == 0)
        def _():
            gstat_ref[...] = jnp.zeros_like(gstat_ref)

        @pl.when(i % tps == 0)
        def _():
            bstat_ref[...] = jnp.zeros_like(bstat_ref)

        gstat_ref[0:1, :] += jnp.sum(dh * scale1 * xn, axis=0, keepdims=True)
        bstat_ref[0, 0:1, :] += jnp.sum(dh, axis=0, keepdims=True)
        bstat_ref[0, 1:2, :] += jnp.sum(dh * (xn * g_ref[...]), axis=0, keepdims=True)

    rope = _rope_spec(tps, tm)
    return pl.pallas_call(
        body, name="attn_in_bwd", grid=(T // tm,),
        in_specs=[_tok_spec(NA_WIDTH, tm), _tok_spec(NA_WIDTH, tm), _tok_spec(NA_WIDTH, tm), _tok_spec(SW_WIDTH, tm),
                  _tok_spec(SW_KV_WIDTH, tm), _tok_spec(SW_KV_WIDTH, tm), rope, rope, _full(w_in.shape), _tok_spec(D, tm),
                  _mod_spec(tps, D), _full((1, D)), _tok_spec(D, tm)],
        out_specs=[_tok_spec(D, tm), _tok_spec(IN_WIDTH, tm), _full((8, D)), _bstat_spec(tps, D)],
        out_shape=[jax.ShapeDtypeStruct((T, D), F32), jax.ShapeDtypeStruct((T, IN_WIDTH), BF16),
                   jax.ShapeDtypeStruct((8, D), F32), jax.ShapeDtypeStruct((B, 8, D), F32)],
        compiler_params=_params("arbitrary"),
    )(dqa, dka, dva, dqb, dkb, dvb, cos_t, sin_t, w_in, x2d, mod3, g_attn, dx1)


def _matmul_tn(a, b, name, tm=None, tk=512):
    T, M = a.shape
    N = b.shape[1]
    tm = M if tm is None else tm
    nk = T // tk

    def body(a_ref, b_ref, o_ref, acc):
        k = pl.program_id(1)

        @pl.when(k == 0)
        def _():
            acc[...] = jnp.zeros_like(acc)

        acc[...] += _tn(a_ref[...], b_ref[...])

        @pl.when(k == nk - 1)
        def _():
            o_ref[...] = acc[...].astype(BF16)

    return pl.pallas_call(
        body, name=name, grid=(M // tm, nk),
        in_specs=[pl.BlockSpec((tk, tm), lambda i, k: (k, i)), pl.BlockSpec((tk, N), lambda i, k: (k, 0))],
        out_specs=pl.BlockSpec((tm, N), lambda i, k: (i, 0)),
        out_shape=jax.ShapeDtypeStruct((M, N), BF16),
        scratch_shapes=[pltpu.VMEM((tm, N), F32)],
        compiler_params=_params("parallel", "arbitrary"),
    )(a, b)


def _na_geometry(S):
    rows = S // GRID_W
    wr = min(NA_ROWS_MAX, rows)
    return rows, wr


def _na_window(r, rows, wr):
    rs = jnp.clip(r - wr // 2, 0, rows - wr)
    return pl.multiple_of(rs * GRID_W, GRID_W), pl.multiple_of((rs - r + NA_ROWS_MAX - 1) * GRID_W, GRID_W)


NA_STEP_PAIRS = 2
NA_GW = NA_STEP_PAIRS * 128
NA_BWD_ROWS = 4
NA_ROWS_PER_STEP = 4


def _na_specs(S, kw_n, order):
    ng = NA_PAIRS // NA_STEP_PAIRS

    def col(k):
        return pl.BlockSpec((1, S, NA_GW), lambda *ids: (order(*ids)[0], 0, k * ng + order(*ids)[1]))
    bias = pl.BlockSpec((NA_STEP_PAIRS, N_DR * GRID_W, 128), lambda *ids: (order(*ids)[1], 0, 0))
    out = pl.BlockSpec((1, S, NA_GW), lambda *ids: (order(*ids)[0], 0, order(*ids)[1]))
    return col(0), col(1), col(2), bias, out


def _block_diag(t):
    left = lax.broadcasted_iota(jnp.int32, t.shape, 1) < HEAD_DIM
    zero = jnp.zeros_like(t)
    return jnp.concatenate([jnp.where(left, t, zero), jnp.where(left, zero, t)], axis=0)


def _diag_blocks(res):
    left = lax.broadcasted_iota(jnp.int32, (HEAD_DIM, 128), 1) < HEAD_DIM
    return jnp.where(left, res[:HEAD_DIM], res[HEAD_DIM:])


def _col_softmax(st):
    e = jnp.exp(st - jnp.max(st, axis=0, keepdims=True))
    return e * (1.0 / jnp.sum(e, axis=0, keepdims=True))


def _na_fwd(qkv, bias, tasks=()):
    B, S, _ = qkv.shape
    rows, wr = _na_geometry(S)
    kw_n = wr * GRID_W

    def body(q_ref, k_ref, v_ref, b_ref, o_ref):
        def step(it, carry):
            win = [_na_window(it * NA_ROWS_PER_STEP + u, rows, wr) for u in range(NA_ROWS_PER_STEP)]
            qrows = [pl.ds(pl.multiple_of((it * NA_ROWS_PER_STEP + u) * GRID_W, GRID_W), GRID_W) for u in range(NA_ROWS_PER_STEP)]
            krows = [pl.ds(w[0], kw_n) for w in win]
            brows = [pl.ds(w[1], kw_n) for w in win]
            lanes = [pl.ds(p * 128, 128) for p in range(NA_STEP_PAIRS)]
            chains = [(u, p) for u in range(NA_ROWS_PER_STEP) for p in range(NA_STEP_PAIRS)]
            st = {(u, p): _nt(k_ref[0, krows[u], lanes[p]], _block_diag(q_ref[0, qrows[u], lanes[p]])) for u, p in chains}
            pn = {(u, p): _col_softmax(st[(u, p)] + b_ref[p, brows[u], :]).astype(BF16) for u, p in chains}
            out = {(u, p): _diag_blocks(_tn(pn[(u, p)], v_ref[0, krows[u], lanes[p]])) for u, p in chains}
            for u in range(NA_ROWS_PER_STEP):
                o_ref[0, qrows[u], :] = jnp.concatenate([out[(u, p)] for p in range(NA_STEP_PAIRS)], axis=1)
            return carry

        lax.fori_loop(0, rows // NA_ROWS_PER_STEP, step, 0)

    q, k, v, bs, out = _na_specs(S, kw_n, lambda b, g: (b, g))
    return _hosted_call(body, "na_fwd", (B, NA_PAIRS // NA_STEP_PAIRS), [q, k, v, bs], [out],
                        [jax.ShapeDtypeStruct((B, S, NA_WIDTH), F32)], (qkv, qkv, qkv, bias), tasks)


def _na_bwd(qkv, bias, doa, tasks=()):
    B, S, _ = qkv.shape
    rows, wr = _na_geometry(S)
    kw_n = wr * GRID_W

    def body(q_ref, k_ref, v_ref, b_ref, do_ref, dq_ref, dk_ref, dv_ref, db_ref, dk_acc, dv_acc):
        @pl.when(pl.program_id(1) == 0)
        def _():
            db_ref[...] = jnp.zeros_like(db_ref)

        dk_acc[...] = jnp.zeros_like(dk_acc)
        dv_acc[...] = jnp.zeros_like(dv_acc)

        def step(it, carry):
            nu, pairs = range(NA_BWD_ROWS), range(NA_STEP_PAIRS)
            win = [_na_window(it * NA_BWD_ROWS + u, rows, wr) for u in nu]
            qrows = [pl.ds(pl.multiple_of((it * NA_BWD_ROWS + u) * GRID_W, GRID_W), GRID_W) for u in nu]
            krows = [pl.ds(w[0], kw_n) for w in win]
            brows = [pl.ds(w[1], kw_n) for w in win]
            lanes = [pl.ds(p * 128, 128) for p in pairs]
            chains = [(u, p) for u in nu for p in pairs]
            kp = {(u, p): k_ref[0, krows[u], lanes[p]] for u, p in chains}
            qbd = {(u, p): _block_diag(q_ref[0, qrows[u], lanes[p]]) for u, p in chains}
            dobd = {(u, p): _block_diag(do_ref[0, qrows[u], lanes[p]]) for u, p in chains}
            st = {c: _nt(kp[c], qbd[c]) for c in chains}
            dpt = {(u, p): _nt(v_ref[0, krows[u], lanes[p]], dobd[(u, p)]) for u, p in chains}
            pn = {(u, p): _col_softmax(st[(u, p)] + b_ref[p, brows[u], :]) for u, p in chains}
            dst = {c: pn[c] * (dpt[c] - jnp.sum(pn[c] * dpt[c], axis=0, keepdims=True)) for c in chains}
            dsb = {c: dst[c].astype(BF16) for c in chains}
            dq = {c: _diag_blocks(_tn(dsb[c], kp[c])) for c in chains}
            dk = {c: _nn(dsb[c], qbd[c]) for c in chains}
            dv = {c: _nn(pn[c].astype(BF16), dobd[c]) for c in chains}
            for u in nu:
                dq_ref[0, qrows[u], :] = jnp.concatenate([dq[(u, p)] for p in pairs], axis=1).astype(BF16)
                dk_acc[krows[u], :] += jnp.concatenate([dk[(u, p)] for p in pairs], axis=1)
                dv_acc[krows[u], :] += jnp.concatenate([dv[(u, p)] for p in pairs], axis=1)
                for p in pairs:
                    db_ref[p, brows[u], :] += dst[(u, p)]
            return carry

        lax.fori_loop(0, rows // NA_BWD_ROWS, step, 0)

        def emit(i, carry):
            r = pl.ds(pl.multiple_of(i * 256, 256), 256)
            dk_ref[0, r, :] = dk_acc[r, :].astype(BF16)
            dv_ref[0, r, :] = dv_acc[r, :].astype(BF16)
            return carry

        lax.fori_loop(0, S // 256, emit, 0)

    q, k, v, bs, out = _na_specs(S, kw_n, lambda g, b: (b, g))
    hs = jax.ShapeDtypeStruct((B, S, NA_WIDTH), BF16)
    return _hosted_call(body, "na_bwd", (NA_PAIRS // NA_STEP_PAIRS, B), [q, k, v, bs, out], [out, out, out, bs],
                        [hs, hs, hs, jax.ShapeDtypeStruct((NA_PAIRS, N_DR * GRID_W, 128), F32)], (qkv, qkv, qkv, bias, doa), tasks,
                        scratch_shapes=[pltpu.VMEM((S, NA_GW), F32), pltpu.VMEM((S, NA_GW), F32)])


SW_PAIRS = SW_HEADS // 2


def _sw_band(n, S):
    kw_n = 3 * SW_BLOCK
    start = pl.multiple_of(jnp.clip(n * SW_BLOCK - SW_BLOCK, 0, S - kw_n), SW_BLOCK)
    kpos = start + lax.broadcasted_iota(jnp.int32, (kw_n, SW_BLOCK), 0)
    qpos = n * SW_BLOCK + lax.broadcasted_iota(jnp.int32, (kw_n, SW_BLOCK), 1)
    return start, jnp.abs(qpos - kpos) <= SW_WINDOW


def _kv_halves(t):
    left = lax.broadcasted_iota(jnp.int32, t.shape, 1) < HEAD_DIM
    swapped = pltpu.roll(t, HEAD_DIM, axis=1)
    zero = jnp.zeros_like(t)
    return {(0, 0): jnp.where(left, t, zero), (0, 1): jnp.where(left, zero, swapped),
            (1, 0): jnp.where(left, swapped, zero), (1, 1): jnp.where(left, zero, t)}


def _sw_probs(st, ok, sk):
    st = jnp.where(ok, st, NEG)
    m = jnp.maximum(jnp.max(st, axis=0, keepdims=True), sk)
    e = jnp.exp(st - m)
    esk = jnp.exp(sk - m)
    inv = 1.0 / (jnp.sum(e, axis=0, keepdims=True) + esk)
    return e * inv, esk * inv


def _sw_specs(S):
    q = pl.BlockSpec((1, S, SW_WIDTH), lambda b, *_: (b, 0, ROPE_LO // SW_WIDTH))
    k = pl.BlockSpec((1, S, SW_KV_WIDTH), lambda b, *_: (b, 0, (ROPE_LO + SW_WIDTH) // SW_KV_WIDTH))
    v = pl.BlockSpec((1, S, SW_KV_WIDTH), lambda b, *_: (b, 0, (ROPE_LO + ROPE_WIDTH) // SW_KV_WIDTH))
    return q, k, v


SW_FWD_SPLIT = 2


def _sw_fwd(sink, qkv, tasks=()):
    B, S, _ = qkv.shape
    kw_n = 3 * SW_BLOCK

    def body(sink_ref, q_ref, k_ref, v_ref, o_ref):
        def step(n, carry):
            start, ok = _sw_band(n, S)
            qrows = pl.ds(pl.multiple_of(n * SW_BLOCK, SW_BLOCK), SW_BLOCK)
            krows = pl.ds(start, kw_n)
            kh, vh = _kv_halves(k_ref[0, krows, :]), _kv_halves(v_ref[0, krows, :])
            heads = [(p, e) for p in range(SW_PAIRS) for e in range(2)]
            qp = [q_ref[0, qrows, pl.ds(p * 128, 128)] for p in range(SW_PAIRS)]
            kv_of = lambda p: p // (SW_PAIRS // SW_KV_HEADS)
            st = {(p, e): _nt(kh[(kv_of(p), e)], qp[p]) for p, e in heads}
            pn = {(p, e): _sw_probs(st[(p, e)], ok, sink_ref[2 * p + e])[0].astype(BF16) for p, e in heads}
            outs = [_tn(pn[(p, 0)], vh[(kv_of(p), 0)]) + _tn(pn[(p, 1)], vh[(kv_of(p), 1)]) for p in range(SW_PAIRS)]
            o_ref[0, qrows, :] = jnp.concatenate(outs, axis=1)
            return carry

        half = (S // SW_BLOCK) // SW_FWD_SPLIT
        lax.fori_loop(pl.program_id(1) * half, (pl.program_id(1) + 1) * half, step, 0)

    q, k, v = _sw_specs(S)
    return _hosted_call(
        body, "sw_fwd", (B, SW_FWD_SPLIT), [pl.BlockSpec(memory_space=pltpu.SMEM), q, k, v],
        [pl.BlockSpec((1, S, SW_WIDTH), lambda b, s: (b, 0, 0))], [jax.ShapeDtypeStruct((B, S, SW_WIDTH), F32)],
        (sink, qkv, qkv, qkv), tasks)


def _sw_bwd(sink, qkv, dob):
    B, S, _ = qkv.shape
    kw_n = 3 * SW_BLOCK

    fold_rows = 256

    def body(sink_ref, q_ref, k_ref, v_ref, do_ref, dq_ref, dk_ref, dv_ref, dsink_ref, dk_acc, dv_acc):
        @pl.when(pl.program_id(0) == 0)
        def _():
            dsink_ref[...] = jnp.zeros_like(dsink_ref)

        dk_acc[...] = jnp.zeros_like(dk_acc)
        dv_acc[...] = jnp.zeros_like(dv_acc)
        ppk = SW_PAIRS // SW_KV_HEADS

        def step(n, carry):
            start, ok = _sw_band(n, S)
            qrows = pl.ds(pl.multiple_of(n * SW_BLOCK, SW_BLOCK), SW_BLOCK)
            krows = pl.ds(start, kw_n)
            kh, vh = _kv_halves(k_ref[0, krows, :]), _kv_halves(v_ref[0, krows, :])
            heads = [(p, e) for p in range(SW_PAIRS) for e in range(2)]
            qp = [q_ref[0, qrows, pl.ds(p * 128, 128)] for p in range(SW_PAIRS)]
            dop = [do_ref[0, qrows, pl.ds(p * 128, 128)] for p in range(SW_PAIRS)]
            st = {(p, e): _nt(kh[(p // ppk, e)], qp[p]) for p, e in heads}
            dpt = {(p, e): _nt(vh[(p // ppk, e)], dop[p]) for p, e in heads}
            pnb, dsb = {}, {}
            for p, e in heads:
                pn, psink = _sw_probs(st[(p, e)], ok, sink_ref[2 * p + e])
                delta = jnp.sum(pn * dpt[(p, e)], axis=0, keepdims=True)
                dsb[(p, e)] = (pn * (dpt[(p, e)] - delta)).astype(BF16)
                pnb[(p, e)] = pn.astype(BF16)
                dsink_ref[2 * p + e:2 * p + e + 1, :] += -(psink * delta)
            dq_ref[0, qrows, :] = jnp.concatenate(
                [_tn(dsb[(p, 0)], kh[(p // ppk, 0)]) + _tn(dsb[(p, 1)], kh[(p // ppk, 1)]) for p in range(SW_PAIRS)],
                axis=1).astype(BF16)
            left = lax.broadcasted_iota(jnp.int32, (kw_n, 128), 1) < HEAD_DIM
            dks, dvs = [], []
            for kv in range(SW_KV_HEADS):
                dk = dv = None
                for p in range(kv * ppk, (kv + 1) * ppk):
                    dk_p = jnp.where(left, _nn(dsb[(p, 0)], qp[p]), _nn(dsb[(p, 1)], qp[p]))
                    dv_p = jnp.where(left, _nn(pnb[(p, 0)], dop[p]), _nn(pnb[(p, 1)], dop[p]))
                    dk = dk_p if dk is None else dk + dk_p
                    dv = dv_p if dv is None else dv + dv_p
                dks.append(dk)
                dvs.append(dv)
            dk_acc[krows, :] += jnp.concatenate(dks, axis=1)
            dv_acc[krows, :] += jnp.concatenate(dvs, axis=1)
            return carry

        lax.fori_loop(0, S // SW_BLOCK, step, 0)

        def fold(i, carry):
            rows = pl.ds(pl.multiple_of(i * fold_rows, fold_rows), fold_rows)
            left = lax.broadcasted_iota(jnp.int32, (fold_rows, 128), 1) < HEAD_DIM
            for acc, out_ref in ((dk_acc, dk_ref), (dv_acc, dv_ref)):
                a, b = acc[rows, 0:128], acc[rows, 128:256]
                out_ref[0, rows, :] = jnp.where(left, a + pltpu.roll(a, HEAD_DIM, axis=1),
                                                b + pltpu.roll(b, HEAD_DIM, axis=1)).astype(BF16)
            return carry

        lax.fori_loop(0, S // fold_rows, fold, 0)

        @pl.when(pl.program_id(0) == B - 1)
        def _():
            dsink_ref[...] = jnp.broadcast_to(jnp.sum(dsink_ref[...], axis=1, keepdims=True), dsink_ref.shape)

    q, k, v = _sw_specs(S)
    qo = pl.BlockSpec((1, S, SW_WIDTH), lambda b: (b, 0, 0))
    ko = pl.BlockSpec((1, S, SW_KV_WIDTH), lambda b: (b, 0, 0))
    return pl.pallas_call(
        body, name="sw_bwd", grid=(B,),
        in_specs=[pl.BlockSpec(memory_space=pltpu.SMEM), q, k, v, qo],
        out_specs=[qo, ko, ko, _full((SW_HEADS, 128))],
        out_shape=[jax.ShapeDtypeStruct((B, S, SW_WIDTH), BF16), jax.ShapeDtypeStruct((B, S, SW_KV_WIDTH), BF16),
                   jax.ShapeDtypeStruct((B, S, SW_KV_WIDTH), BF16), jax.ShapeDtypeStruct((SW_HEADS, 128), F32)],
        scratch_shapes=[pltpu.VMEM((S, 2 * SW_KV_WIDTH), F32), pltpu.VMEM((S, 2 * SW_KV_WIDTH), F32)],
        compiler_params=_params("arbitrary"),
    )(sink, qkv, qkv, qkv, dob)


def _pack_sum_adamw(packs, params, pick):
    W = packs.shape[1]
    n_p = len(params)

    def body(p_ref, *refs):
        ins, tot_ref, pick_ref, outs = refs[:3 * n_p], refs[3 * n_p], refs[3 * n_p + 1], refs[3 * n_p + 2:]
        tot = p_ref[0:8, :]
        for d in range(1, N_DEV):
            tot = tot + p_ref[8 * d:8 * d + 8, :]
        tot_ref[...] = tot
        pick_ref[...] = tot[pick[0]:pick[0] + 1, pick[1]:pick[1] + 1]
        for i, (w, _, _, rows, off) in enumerate(params):
            w_ref, m_ref, v_ref = ins[3 * i:3 * i + 3]
            g_ref, d_ref, nm_ref, nv_ref = outs[4 * i:4 * i + 4]
            if w.ndim == 3:
                n_a, n_b, n = w.shape
                for a in range(n_a):
                    for b in range(n_b):
                        o = off + (b * n_a + a) * n
                        g_ref[a, b:b + 1, :] = tot_ref[rows[0]:rows[0] + 1, o:o + n]
                g = g_ref[...]
            else:
                n = w.shape[1]
                g = tot[rows[0]:rows[0] + 1, off:off + n]
                for r in rows[1:]:
                    g = g + tot[r:r + 1, off:off + n]
                g_ref[...] = g
            d_ref[...], nm_ref[...], nv_ref[...] = _adam_update(w_ref[...], g, m_ref[...], v_ref[...])

    res = pl.pallas_call(
        body, name="small_adamw",
        out_shape=[jax.ShapeDtypeStruct((8, W), F32), jax.ShapeDtypeStruct((1, 1), F32)]
        + [jax.ShapeDtypeStruct(p[0].shape, F32) for p in params for _ in range(4)],
        compiler_params=pltpu.CompilerParams(vmem_limit_bytes=VMEM_LIMIT),
    )(packs, *[a for p in params for a in p[:3]])
    return res[0], res[1], [res[2 + 4 * i:6 + 4 * i] for i in range(n_p)]


def _adam_update(w, g, m, v):
    c1 = 1.0 - ADAM_B1 ** ADAM_STEP
    c2 = 1.0 - ADAM_B2 ** ADAM_STEP
    nm = ADAM_B1 * m + (1.0 - ADAM_B1) * g
    nv = ADAM_B2 * v + (1.0 - ADAM_B2) * (g * g)
    return -ADAM_LR * ((nm / c1) / (jnp.sqrt(nv / c2) + ADAM_EPS) + ADAM_WD * w), nm, nv


def _adamw(w, g, m, v, name):
    def body(w_ref, g_ref, m_ref, v_ref, d_ref, nm_ref, nv_ref):
        d_ref[...], nm_ref[...], nv_ref[...] = _adam_update(w_ref[...], g_ref[...], m_ref[...], v_ref[...])

    s = jax.ShapeDtypeStruct(w.shape, F32)
    return pl.pallas_call(body, name=name, out_shape=[s, s, s],
                          compiler_params=pltpu.CompilerParams(vmem_limit_bytes=VMEM_LIMIT))(w, g, m, v)


def _sum_adamw_rows(R):
    return max(r for r in range(16, min(R, 256) + 1, 16) if R % r == 0)


def _sum_adamw_steps(R):
    return R // _sum_adamw_rows(R)


def _sum_first(own, recvb, name, tasks):
    R, C = own.shape
    rc = _sum_adamw_rows(R)

    def body(own_ref, r_ref, p_ref):
        p_ref[...] = (own_ref[...] + r_ref[0].astype(F32)) + r_ref[1].astype(F32)

    blk = pl.BlockSpec((rc, C), lambda i: (i, 0))
    (part,), got = _hosted_call(body, name, (R // rc,), [blk, pl.BlockSpec((2, rc, C), lambda i: (0, i, 0))], [blk],
                                [jax.ShapeDtypeStruct((R, C), F32)], (own, recvb), tasks)
    return part, got


def _sum_adamw(own, recvb, w, m, v, name, done=0):
    R, C = own.shape
    rc = _sum_adamw_rows(R)
    left = 3 - done
    assert 3 % left == 0

    def body(own_ref, r_ref, w_ref, m_ref, v_ref, g_ref, d_ref, nm_ref, nv_ref):
        g = own_ref[...]
        for j in range(left):
            g = g + r_ref[j].astype(F32)
        g_ref[...] = g
        d_ref[...], nm_ref[...], nv_ref[...] = _adam_update(w_ref[...], g, m_ref[...], v_ref[...])

    blk = pl.BlockSpec((rc, C), lambda i: (i, 0))
    s = jax.ShapeDtypeStruct((R, C), F32)
    return pl.pallas_call(
        body, name=name, grid=(R // rc,),
        in_specs=[blk, pl.BlockSpec((left, rc, C), lambda i: (done // left, i, 0)), blk, blk, blk],
        out_specs=[blk, blk, blk, blk], out_shape=[s, s, s, s], compiler_params=_params("parallel"),
    )(own, recvb, w, m, v)


def _by_device(dw):
    return dw.reshape(N_DEV, dw.shape[0] // N_DEV, dw.shape[1])


def _local_step(x, mod, g_attn, w_in, bias, sw_sink, g_na_out, g_sw_out, w_out, g_ffn, w_up, conv_w, conv_b, w_down,
                g_final, target, sharded):
    B, S, D = x.shape
    T = B * S
    x2d = x.reshape(T, D)
    mod3 = mod.reshape(B, 6, D)
    cos_t, sin_t = _rope_tables(S)
    sink = sw_sink.reshape(SW_HEADS)
    n_tiles = T // WIDE_TILE
    full = lambda g: g.reshape(N_DEV * g.shape[1], g.shape[2])

    rider = lambda w, mid, lo, n, into=None: [_gather_task(w, mid, rows=(lo, n), into=into)] if sharded else []
    if sharded:
        qu, hd = w_up.shape[0] // 4, w_down.shape[0] // 2
    (h, qkv), got = _attn_in(x2d, mod3, g_attn, w_in, cos_t, sin_t, S,
                             [_gather_task(w_out, n_tiles // 2)] + rider(w_up, n_tiles - 1, 0, qu) if sharded else [])
    if sharded:
        w_out, w_up_buf = full(got[0][0]), got[1][0]
    qkv3 = qkv.reshape(B, S, IN_WIDTH)
    na_steps, sw_steps = B * (NA_PAIRS // NA_STEP_PAIRS), B * SW_FWD_SPLIT
    (oa,), got = _na_fwd(qkv3, bias, rider(w_up, na_steps - 1, qu, 2 * qu, w_up_buf) if sharded else [])
    if sharded:
        w_up_buf = got[0][0]
    oa = oa.reshape(T, NA_WIDTH)
    (ob,), got = _sw_fwd(sink, qkv3, rider(w_up, sw_steps // 2, 3 * qu, qu, w_up_buf) if sharded else [])
    if sharded:
        w_up = full(got[0][0])
    ob = ob.reshape(T, SW_WIDTH)
    (mixin, mix, x1), _ = _attn_out(oa, ob, x2d, mod3, g_na_out, g_sw_out, w_out, S)
    (h2, val, gt), got = _ffn_up(x1, mod3, g_ffn, w_up, S, rider(w_down, 3 * n_tiles // 4, 0, 2 * hd) if sharded else [])
    if sharded:
        w_down = full(got[0][0])
    a, act, vd, dx2, df, gstat_f, bstat_f = _ffn_down(gt, val, conv_w, conv_b, w_down, x1, mod3, g_final, target.reshape(T, D), B, S)
    F = val.shape[1]

    dw_down = _matmul_tn(a, df, "dw_down")
    (dval, dgc, cstat), got = _ffn_down_bwd(df, w_down, act, vd, [_swap_task(_by_device(dw_down))] if sharded else [])
    if sharded:
        send_down, own_down = _chip_sums(_by_device(dw_down), got[0][0])
    (du, dx1, dmix, gstat_u, bstat_u, cstat_w), got = _ffn_up_bwd(dgc, dval, gt, conv_w, w_up, x1, mod3, g_ffn, dx2, mix, B, S,
                                                                  [_exchange_task(send_down)] if sharded else [])
    if sharded:
        dw_down = (own_down, got[0][0])
    dw_up = _matmul_tn(du, h2, "dw_up", tm=F)
    dw_out = _matmul_tn(mixin, dmix, "dw_out")
    (doa, dob, gstat_o), got = _attn_out_bwd(dmix, w_out, oa, ob, g_na_out, g_sw_out,
                                             [_swap_task(_by_device(dw_up)), _swap_task(_by_device(dw_out))] if sharded else [])
    if sharded:
        send_up, own_up = _chip_sums(_by_device(dw_up), got[0][0])
        send_out, own_out = _chip_sums(_by_device(dw_out), got[1][0])
    (dqa, dka, dva, dbt), got = _na_bwd(qkv3, bias, doa.reshape(B, S, NA_WIDTH),
                                        [_exchange_task(send_up), _exchange_task(send_out)] if sharded else [])
    if sharded:
        dw_up, dw_out = (own_up, got[0][0]), (own_out, got[1][0])
    dqb, dkb, dvb, dsink = _sw_bwd(sink, qkv3, dob.reshape(B, S, SW_WIDTH))
    r2 = lambda t: t.reshape(T, t.shape[-1])
    grad_x, dproj, gstat_i, bstat_i = _attn_in_bwd(r2(dqa), r2(dka), r2(dva), r2(dqb), r2(dkb), r2(dvb), cos_t, sin_t, w_in, x2d, mod3,
                                                   g_attn, dx1, B, S)
    dw_in = _matmul_tn(dproj, h, "dw_in")

    dmod = jnp.stack([bstat_i[:, 0], bstat_i[:, 1], bstat_u[:, 2], bstat_u[:, 0], bstat_u[:, 1], bstat_f[:, 0]], axis=1)
    small = dict(g_attn=gstat_i[0], g_ffn=gstat_u[0], g_final=gstat_f[0], loss=gstat_f[1, 0], g_na_out=gstat_o[0], g_sw_out=gstat_o[1],
                 sw_sink=dsink[:, 0], conv_b=cstat[0], conv_w=cstat_w[1:4], dbt=dbt,
                 raw=(bstat_i, bstat_u, bstat_f, gstat_i, gstat_u, gstat_f, gstat_o, dsink, cstat, cstat_w))
    return grad_x.reshape(B, S, D), dict(w_in=dw_in, w_out=dw_out, w_up=dw_up, w_down=dw_down), dmod, small


def _pack_slab(raw, drpb):
    D, F = raw[3].shape[1], raw[8].shape[1]
    n_seq = raw[0].shape[0]

    def body(bi_ref, bu_ref, bf_ref, gi_ref, gu_ref, gf_ref, go_ref, ds_ref, cs_ref, cw_ref, rp_ref, o_ref):
        o_ref[...] = jnp.zeros_like(o_ref)
        for b in range(n_seq):
            mods = (bi_ref[b, 0:1, :], bi_ref[b, 1:2, :], bu_ref[b, 2:3, :], bu_ref[b, 0:1, :], bu_ref[b, 1:2, :], bf_ref[b, 0:1, :])
            for k, row in enumerate(mods):
                o_ref[b:b + 1, k * D:(k + 1) * D] = row
        o = 0
        for row in (gi_ref[0:1, :], gu_ref[0:1, :], gf_ref[0:1, :], go_ref[0:1, :], go_ref[1:2, :]):
            o_ref[2:3, o:o + row.shape[1]] = row
            o += row.shape[1]
        ds = ds_ref[...]
        eye = lax.broadcasted_iota(jnp.int32, ds.shape, 0) == lax.broadcasted_iota(jnp.int32, ds.shape, 1)
        o_ref[2:3, o:o + 128] = jnp.sum(jnp.where(eye, ds, 0.0), axis=0, keepdims=True)
        o_ref[2:3, o + 128:o + 256] = gf_ref[1:2, 0:128]
        o_ref[3:4, 0:F] = cs_ref[0:1, :]
        o_ref[4:5, 0:rp_ref.shape[1]] = rp_ref[...]
        o_ref[5:8, 0:F] = cw_ref[1:4, :]

    return pl.pallas_call(body, name="pack_slab", out_shape=jax.ShapeDtypeStruct((8, PACK_W), F32),
                          compiler_params=pltpu.CompilerParams(vmem_limit_bytes=VMEM_LIMIT))(*raw, drpb)


def kernel(x, c, w_ada, b_ada, g_attn, w_in, na_rpb, sw_sink, g_na_out, g_sw_out, w_out, g_ffn, w_up, conv_w, conv_b, w_down, g_final, loss_target, m_w_ada, m_b_ada, m_g_attn, m_w_in, m_na_rpb, m_sw_sink, m_g_na_out, m_g_sw_out, m_w_out, m_g_ffn, m_w_up, m_conv_w, m_conv_b, m_w_down, m_g_final, v_w_ada, v_b_ada, v_g_attn, v_w_in, v_na_rpb, v_sw_sink, v_g_na_out, v_g_sw_out, v_w_out, v_g_ffn, v_w_up, v_conv_w, v_conv_b, v_w_down, v_g_final):
    B, S, D = x.shape
    me = 4 * lax.axis_index("x") + 2 * lax.axis_index("y") + lax.axis_index("c")
    ada_c = w_ada.shape[2]
    F_l = conv_w.shape[2]

    cw_l = jnp.pad(conv_w[0], ((0, 8 - conv_w.shape[1]), (0, 0)))
    c_l = jnp.pad(c, ((0, 8 - B), (0, 0)))
    tr = {"w_in", "w_up"}
    w_in_t = jnp.transpose(w_in[0])
    shards = dict(w_out=w_out[0].astype(BF16), w_up=jnp.transpose(w_up[0]).astype(BF16), w_down=w_down[0].astype(BF16))

    b_ada_l = lax.dynamic_slice(b_ada, (0, me * ada_c), (1, ada_c))
    slabs, mod_all, w_in_all = _ada_fwd(jnp.concatenate([c_l, cw_l], axis=1), w_ada[0], b_ada_l, w_in_t.astype(BF16), B)
    c_all = slabs[:, :, :D].reshape(N_DEV * 8, D)
    conv_w_f = jnp.transpose(slabs[:, :3, D:], (1, 0, 2)).reshape(3, N_DEV * F_l)
    mod_mine = lax.dynamic_slice(mod_all, (0, me * B, 0), (N_DEV, B, ada_c))
    mod = jnp.transpose(mod_mine, (1, 0, 2)).reshape(B, N_DEV * ada_c)
    w_in_f = w_in_all.reshape(N_DEV * w_in_t.shape[0], D)

    bias = _na_bias_table(na_rpb[0])

    grad_x, dw, dmod, small = _local_step(x, mod, g_attn, w_in_f, bias, sw_sink, g_na_out, g_sw_out, shards["w_out"], g_ffn,
                                          shards["w_up"], conv_w_f, conv_b, shards["w_down"], g_final.reshape(1, D), loss_target,
                                          sharded=True)
    g8_in = _by_device(dw["w_in"])
    drpb, got = _na_bias_grad(small["dbt"], [_swap_task(g8_in)])
    send_in, own_in = _chip_sums(g8_in, got[0][0])

    slab = _pack_slab(small["raw"], drpb.reshape(1, -1))
    weights = dict(w_ada=w_ada, b_ada=b_ada, g_attn=g_attn, w_in=w_in, na_rpb=na_rpb, sw_sink=sw_sink, g_na_out=g_na_out,
                   g_sw_out=g_sw_out, w_out=w_out, g_ffn=g_ffn, w_up=w_up, conv_w=conv_w, conv_b=conv_b, w_down=w_down, g_final=g_final)
    ms = dict(w_ada=m_w_ada, b_ada=m_b_ada, g_attn=m_g_attn, w_in=m_w_in, na_rpb=m_na_rpb, sw_sink=m_sw_sink, g_na_out=m_g_na_out,
              g_sw_out=m_g_sw_out, w_out=m_w_out, g_ffn=m_g_ffn, w_up=m_w_up, conv_w=m_conv_w, conv_b=m_conv_b, w_down=m_w_down, g_final=m_g_final)
    vs = dict(w_ada=v_w_ada, b_ada=v_b_ada, g_attn=v_g_attn, w_in=v_w_in, na_rpb=v_na_rpb, sw_sink=v_sw_sink, g_na_out=v_g_na_out,
              g_sw_out=v_g_sw_out, w_out=v_w_out, g_ffn=v_g_ffn, w_up=v_w_up, conv_w=v_conv_w, conv_b=v_conv_b, w_down=v_w_down, g_final=v_g_final)
    names = list(weights)
    grads, deltas, new_m, new_v = {}, {}, {}, {}
    flat = lambda t: t.reshape(1, -1)

    def shard2d(nm):
        if nm in tr:
            return (lambda t: jnp.transpose(t[0])), (lambda t: jnp.transpose(t)[None])
        if nm == "conv_w":
            return (lambda t: jnp.transpose(t, (1, 0, 2))), (lambda t: jnp.transpose(t, (1, 0, 2)))
        return (lambda t: t[0]), (lambda t: t[None])

    def finish_sum(nm, own, recvb, done=0):
        r, back = shard2d(nm)
        g2, d_, m_, v_ = _sum_adamw(own, recvb, r(weights[nm]), r(ms[nm]), r(vs[nm]), "adamw_" + nm, done)
        grads[nm], deltas[nm], new_m[nm], new_v[nm] = back(g2), back(d_), back(m_), back(v_)

    own_up, recv_up = dw["w_up"]
    n_up = _sum_adamw_steps(own_up.shape[0])
    part_up, got = _sum_first(own_up, recv_up, "sum_first_w_up", [_exchange_task(send_in), _gather_task(slab, n_up - 1)])
    packs = got[1][0]
    finish_sum("w_up", part_up, recv_up, done=2)
    finish_sum("w_down", *dw["w_down"])
    finish_sum("w_in", own_in, got[0][0])
    finish_sum("w_out", *dw["w_out"])

    where = dict(b_ada=((0, 1), 0), g_attn=((2,), 0), g_ffn=((2,), D), g_final=((2,), 2 * D), g_na_out=((2,), 3 * D),
                 g_sw_out=((2,), 3 * D + NA_WIDTH), sw_sink=((2,), 3 * D + NA_WIDTH + SW_WIDTH), conv_b=((3,), 0), na_rpb=((4,), 0))

    def small_view(n):
        if n == "na_rpb":
            return (lambda t: jnp.transpose(t[0], (1, 0, 2))), (lambda t: jnp.transpose(t, (1, 0, 2))[None])
        return flat, (lambda t: t.reshape(weights[n].shape))

    tot, loss, small_out = _pack_sum_adamw(
        packs.reshape(N_DEV * 8, PACK_W),
        [tuple(small_view(n)[0](t[n]) for t in (weights, ms, vs)) + where[n] for n in where],
        (2, 3 * D + NA_WIDTH + SW_WIDTH + 128))
    for n, res in zip(where, small_out):
        grads[n], deltas[n], new_m[n], new_v[n] = [small_view(n)[1](t) for t in res]
    loss = loss.reshape(())

    dmod_cols = lax.dynamic_slice(packs.reshape(N_DEV * 8, PACK_W), (0, me * ada_c), (N_DEV * 8, ada_c))
    for nm, g2 in (("w_ada", _ada_bwd(c_all, dmod_cols)), ("conv_w", lax.dynamic_slice(tot, (5, me * F_l), (3, F_l))[:, None])):
        r, back = shard2d(nm)
        d_, m_, v_ = _adamw(r(weights[nm]), g2, r(ms[nm]), r(vs[nm]), "adamw_" + nm)
        grads[nm], deltas[nm], new_m[nm], new_v[nm] = back(g2), back(d_), back(m_), back(v_)
    return (loss, grad_x, *[grads[n] for n in names], *[deltas[n] for n in names], *[new_m[n] for n in names],
            *[new_v[n] for n in names])
```

```python
import functools

import numpy as np
import jax
import jax.numpy as jnp
from jax import lax
from jax.experimental import pallas as pl
from jax.experimental.pallas import tpu as pltpu

F32, BF16 = jnp.float32, jnp.bfloat16
MESH_ID = pl.DeviceIdType.MESH
N_DEV = 8

HEAD_DIM = 64
NA_HEADS = 8
SW_HEADS = 8
SW_KV_HEADS = 2
SW_GROUP = SW_HEADS // SW_KV_HEADS
NA_WIDTH = NA_HEADS * HEAD_DIM
SW_WIDTH = SW_HEADS * HEAD_DIM
SW_KV_WIDTH = SW_KV_HEADS * HEAD_DIM
ROPE_WIDTH = SW_WIDTH + SW_KV_WIDTH
IN_WIDTH = 3 * NA_WIDTH + SW_WIDTH + 2 * SW_KV_WIDTH
ROPE_LO = 3 * NA_WIDTH
GRID_W = 64
NA_ROWS_MAX = 8
NA_COLS = 16
N_DR = 2 * NA_ROWS_MAX - 1
N_DC = 2 * NA_COLS - 1
SW_WINDOW = 128
SW_BLOCK = 128
ROPE_THETA = 10000.0
EPS = 1e-6
NEG = -1e30
Q_SCALE = HEAD_DIM ** -0.5

ADAM_LR = 0.001
ADAM_B1 = 0.9
ADAM_B2 = 0.999
ADAM_EPS = 1e-08
ADAM_WD = 0.01
ADAM_STEP = 10

TOKEN_TILE = 256
WIDE_TILE = 512
VMEM_LIMIT = 56 * 1024 * 1024

PACK_W = 6144


def _nn(a, b):
    return jnp.dot(a, b, preferred_element_type=F32)


def _nt(a, b):
    return lax.dot_general(a, b, (((1,), (1,)), ((), ())), preferred_element_type=F32)


def _tn(a, b):
    return lax.dot_general(a, b, (((0,), (0,)), ((), ())), preferred_element_type=F32)


def _rms(x):
    r = lax.rsqrt(jnp.mean(x * x, axis=-1, keepdims=True) + EPS)
    return x * r, r


def _rms_bwd(xn, r, gy):
    return r * (gy - xn * jnp.mean(xn * gy, axis=-1, keepdims=True))


def _params(*sem):
    return pltpu.CompilerParams(dimension_semantics=sem, vmem_limit_bytes=VMEM_LIMIT)


def _full(shape):
    n = len(shape)
    return pl.BlockSpec(shape, lambda *_: (0,) * n)


def _mesh_pos():
    return lax.axis_index("x"), lax.axis_index("y"), lax.axis_index("c")


def _row_chunk(r):
    for rc in (128, 64, 32, 16):
        if r % rc == 0:
            return rc
    raise ValueError(f"rows {r} not a multiple of 16")


class _Task:
    def __init__(self, inputs, out_shapes, sems, start, finish, mid=None, mid_step=None, alias=None):
        self.inputs, self.out_shapes, self.sems = list(inputs), list(out_shapes), list(sems)
        self.start, self.finish, self.mid, self.mid_step = start, finish, mid, mid_step
        self.alias = alias


def _hosted_call(body, name, grid, in_specs, out_specs, out_shape, operands, tasks, scratch_shapes=()):
    n_in, n_out, n_scr = len(in_specs), len(out_specs), len(scratch_shapes)
    t_in = [len(t.inputs) for t in tasks]
    t_out = [len(t.out_shapes) for t in tasks]
    t_sem = [len(t.sems) for t in tasks]
    n_steps = int(np.prod(grid))

    def wrapped(*refs):
        ins, rest = refs[:n_in], refs[n_in:]
        task_ins, rest = rest[:sum(t_in)], rest[sum(t_in):]
        outs, rest = rest[:n_out], rest[n_out:]
        task_outs, rest = rest[:sum(t_out)], rest[sum(t_out):]
        scr, task_sems = rest[:n_scr], rest[n_scr:]
        step = pl.program_id(0)
        for ax in range(1, len(grid)):
            step = step * grid[ax] + pl.program_id(ax)
        parts = []
        oi = oo = os_ = 0
        for t, a, b, c in zip(tasks, t_in, t_out, t_sem):
            parts.append((t, task_ins[oi:oi + a], task_outs[oo:oo + b], task_sems[os_:os_ + c]))
            oi, oo, os_ = oi + a, oo + b, os_ + c
        for t, ti, to, ts in parts:
            pl.when(step == 0)(functools.partial(t.start, ti, to, ts))
            if t.mid is not None:
                pl.when(step == t.mid_step)(functools.partial(t.mid, ti, to, ts))
        body(*ins, *outs, *scr)
        for t, ti, to, ts in parts:
            pl.when(step == n_steps - 1)(functools.partial(t.finish, ti, to, ts))

    hbm = pl.BlockSpec(memory_space=pl.ANY)
    aliases, oi, oo = {}, n_in, n_out
    for t, a, b in zip(tasks, t_in, t_out):
        if t.alias is not None:
            aliases[oi + t.alias[0]] = oo + t.alias[1]
        oi, oo = oi + a, oo + b
    res = pl.pallas_call(
        wrapped, name=name, grid=grid,
        in_specs=list(in_specs) + [hbm] * sum(t_in),
        out_specs=list(out_specs) + [hbm] * sum(t_out),
        out_shape=list(out_shape) + [s for t in tasks for s in t.out_shapes],
        scratch_shapes=list(scratch_shapes) + [s for t in tasks for s in t.sems],
        input_output_aliases=aliases,
        compiler_params=_params(*(["arbitrary"] * len(grid))),
    )(*operands, *[a for t in tasks for a in t.inputs])
    own, extra = res[:n_out], res[n_out:]
    per_task, o = [], 0
    for b in t_out:
        per_task.append(extra[o:o + b])
        o += b
    return own, per_task


def _gather_task(shard, mid_step, rows=None, into=None):
    lo, n = (0, shard.shape[0]) if rows is None else rows

    def parts(ins, outs, sems):
        x_ref, out_ref, (send_sems, recv_sems, local_sem) = ins[0], outs[0], sems
        x_, y_, c_ = _mesh_pos()
        me, sibling = (x_, y_, c_), (x_, y_, 1 - c_)
        chips = [(1 - x_, y_), (x_, 1 - y_), (1 - x_, 1 - y_)]
        x_ref = x_ref.at[pl.ds(lo, n)]

        def rows(px, py, pc):
            return out_ref.at[4 * px + 2 * py + pc, pl.ds(lo, n)]

        def copy(k, block, to, src=None):
            return pltpu.make_async_remote_copy(
                src_ref=rows(*block) if src is None else src, dst_ref=rows(*block),
                send_sem=send_sems.at[k], recv_sem=recv_sems.at[k], device_id=to, device_id_type=MESH_ID)

        return dict(
            mine=lambda: pltpu.make_async_copy(x_ref, rows(*me), local_sem),
            first=lambda: [copy(0, me, sibling, src=x_ref)] + [copy(1 + j, me, (*chip, c_), src=x_ref) for j, chip in enumerate(chips)],
            passed=lambda: [copy(4 + j, (*chip, c_), sibling) for j, chip in enumerate(chips)],
            landed=lambda: [copy(1 + j, (*chip, c_), me) for j, chip in enumerate(chips)],
            last=lambda: [copy(0, sibling, me)] + [copy(4 + j, (*chip, 1 - c_), me) for j, chip in enumerate(chips)])

    def start(ins, outs, sems):
        p = parts(ins, outs, sems)
        p["mine"]().start()
        for cp in p["first"]():
            cp.start()

    def mid(ins, outs, sems):
        p = parts(ins, outs, sems)
        for cp, fw in zip(p["landed"](), p["passed"]()):
            cp.wait_recv()
            fw.start()

    def finish(ins, outs, sems):
        p = parts(ins, outs, sems)
        for cp in p["last"]():
            cp.wait_recv()
        for cp in p["first"]() + p["passed"]():
            cp.wait_send()
        p["mine"]().wait()

    return _Task([shard] if into is None else [shard, into], [jax.ShapeDtypeStruct((N_DEV,) + shard.shape, shard.dtype)],
                 [pltpu.SemaphoreType.DMA((7,)), pltpu.SemaphoreType.DMA((7,)), pltpu.SemaphoreType.DMA],
                 start, finish, mid, mid_step, alias=None if into is None else (1, 0))


def _swap_task(g8):
    _, R, C = g8.shape

    def copies(ins, outs, sems):
        (g_ref,), (recv_ref,), (ss, rs) = ins, outs, sems
        x_, y_, c_ = _mesh_pos()
        return [pltpu.make_async_remote_copy(src_ref=g_ref.at[2 * k + (1 - c_)], dst_ref=recv_ref.at[k], send_sem=ss.at[k],
                                             recv_sem=rs.at[k], device_id=(x_, y_, 1 - c_), device_id_type=MESH_ID)
                for k in range(4)]

    def start(ins, outs, sems):
        for cp in copies(ins, outs, sems):
            cp.start()

    def finish(ins, outs, sems):
        cps = copies(ins, outs, sems)
        for cp in cps:
            cp.wait_recv()
        for cp in cps:
            cp.wait_send()

    return _Task([g8], [jax.ShapeDtypeStruct((4, R, C), g8.dtype)],
                 [pltpu.SemaphoreType.DMA((4,)), pltpu.SemaphoreType.DMA((4,))], start, finish)


def _chip_sums(g8, recva):
    _, R, C = g8.shape
    rt = _sum_adamw_rows(R)
    rc = _row_chunk(rt)

    def body(core_ref, g_ref, a_ref, send_ref, own_ref):
        x_, y_, _ = _mesh_pos()
        chips = [(1 - x_, y_), (x_, 1 - y_), (1 - x_, 1 - y_), (x_, y_)]

        def chunk(i, carry):
            rows = pl.ds(pl.multiple_of(i * rc, rc), rc)
            for j, (tx, ty) in enumerate(chips):
                k = 2 * tx + ty
                s = g_ref[k, rows, :].astype(F32) + a_ref[k, rows, :].astype(F32)
                if j < 3:
                    send_ref[j, rows, :] = s.astype(BF16)
                else:
                    own_ref[rows, :] = s
            return carry

        lax.fori_loop(0, rt // rc, chunk, 0)

    core = lax.axis_index("c").astype(jnp.int32).reshape(1)
    return pl.pallas_call(
        body, name="chip_sums",
        grid_spec=pltpu.PrefetchScalarGridSpec(
            num_scalar_prefetch=1, grid=(R // rt,),
            in_specs=[pl.BlockSpec((4, None, rt, C), lambda i, c: (0, c[0], i, 0)), pl.BlockSpec((4, rt, C), lambda i, c: (0, i, 0))],
            out_specs=[pl.BlockSpec((3, rt, C), lambda i, c: (0, i, 0)), pl.BlockSpec((rt, C), lambda i, c: (i, 0))]),
        out_shape=[jax.ShapeDtypeStruct((3, R, C), BF16), jax.ShapeDtypeStruct((R, C), F32)],
        compiler_params=_params("parallel"),
    )(core, g8.reshape(4, 2, R, C), recva)


def _exchange_task(sendb):
    def copies(ins, outs, sems):
        (s_ref,), (recv_ref,), (ss, rs) = ins, outs, sems
        x_, y_, c_ = _mesh_pos()
        flips = [(1 - x_, y_), (x_, 1 - y_), (1 - x_, 1 - y_)]
        return [pltpu.make_async_remote_copy(src_ref=s_ref.at[j], dst_ref=recv_ref.at[j], send_sem=ss.at[j], recv_sem=rs.at[j],
                                             device_id=(tx, ty, c_), device_id_type=MESH_ID) for j, (tx, ty) in enumerate(flips)]

    def start(ins, outs, sems):
        for cp in copies(ins, outs, sems):
            cp.start()

    def finish(ins, outs, sems):
        cps = copies(ins, outs, sems)
        for cp in cps:
            cp.wait_recv()
        for cp in cps:
            cp.wait_send()

    return _Task([sendb], [jax.ShapeDtypeStruct(sendb.shape, sendb.dtype)],
                 [pltpu.SemaphoreType.DMA((3,)), pltpu.SemaphoreType.DMA((3,))], start, finish)


def _silu(v):
    return v * (1.0 / (1.0 + jnp.exp(-v)))


def _ada_fwd(c_slab, w_ada_l, b_ada_l, w_in_shard, n_seq, casts, rpb_rev):
    W = c_slab.shape[1]
    D, cols = w_ada_l.shape
    n_rows = N_DEV * n_seq
    n_c = len(casts)
    t_c, t_w = _gather_task(c_slab, 0), _gather_task(jax.ShapeDtypeStruct(w_in_shard.shape, BF16), 0)
    t_m = _gather_task(jax.ShapeDtypeStruct((n_rows, cols), F32), 0)

    def body(c_ref, w_ref, b_ref, ws_ref, rev_ref, *refs):
        cast_in, refs = refs[:n_c], refs[n_c:]
        (slabs_ref, mod_ref, win_ref), refs = refs[:3], refs[3:]
        cast_out, bias_ref, refs = refs[:n_c], refs[n_c], refs[n_c + 1:]
        (c_vm, m_vm, ws_f, ws_b), refs = refs[:4], refs[4:]
        stage_f, stage_b, refs = refs[:n_c], refs[n_c:2 * n_c], refs[2 * n_c:]
        copy_sem, in_sems, out_sems, sems = refs[0], refs[1], refs[2], refs[3:]
        sc, sw, sm = sems[0:3], sems[3:6], sems[6:9]
        t_c.start((c_ref,), (slabs_ref,), sc)
        cp = pltpu.make_async_copy(ws_ref, ws_f, copy_sem)
        cp.start()
        cp.wait()
        ws_b[...] = ws_f[...].astype(BF16)
        t_w.start((ws_b,), (win_ref,), sw)
        loads = [pltpu.make_async_copy(cast_in[j], stage_f[j], in_sems.at[j]) for j in range(n_c)]
        for ld in loads:
            ld.start()
        t_c.mid((c_ref,), (slabs_ref,), sc)
        t_c.finish((c_ref,), (slabs_ref,), sc)
        cp = pltpu.make_async_copy(slabs_ref, c_vm, copy_sem)
        cp.start()
        cp.wait()
        c_all = c_vm[:, :, 0:D].reshape(N_DEV * 8, D)
        m64 = jnp.dot(_silu(c_all), w_ref[...], precision=lax.Precision.HIGHEST, preferred_element_type=F32) + b_ref[...]
        r = lax.broadcasted_iota(jnp.int32, (n_rows, N_DEV * 8), 0)
        c = lax.broadcasted_iota(jnp.int32, (n_rows, N_DEV * 8), 1)
        pick = jnp.where(c == 8 * (r // n_seq) + r % n_seq, 1.0, 0.0)
        m_vm[...] = jnp.dot(pick, m64, precision=lax.Precision.HIGHEST, preferred_element_type=F32)
        t_m.start((m_vm,), (mod_ref,), sm)
        stores = [pltpu.make_async_copy(stage_b[j], cast_out[j], out_sems.at[j]) for j in range(n_c)]
        for j in range(n_c):
            loads[j].wait()
            stage_b[j][...] = stage_f[j][...].astype(BF16)
            stores[j].start()
        for p in range(NA_PAIRS):
            _na_bias_rows(rev_ref, bias_ref, p)
        t_w.mid((ws_b,), (win_ref,), sw)
        t_m.mid((m_vm,), (mod_ref,), sm)
        t_m.finish((m_vm,), (mod_ref,), sm)
        t_w.finish((ws_b,), (win_ref,), sw)
        for st in stores:
            st.wait()

    hbm, vm = pl.BlockSpec(memory_space=pl.ANY), pl.BlockSpec(memory_space=pltpu.VMEM)
    res = pl.pallas_call(
        body, name="ada_fwd", in_specs=[hbm, vm, vm, hbm, vm] + [hbm] * n_c, out_specs=[hbm, hbm, hbm] + [hbm] * n_c + [vm],
        out_shape=t_c.out_shapes + t_m.out_shapes + t_w.out_shapes + [jax.ShapeDtypeStruct(a.shape, BF16) for a in casts]
        + [jax.ShapeDtypeStruct((NA_PAIRS, N_DR * GRID_W, 128), F32)],
        scratch_shapes=[pltpu.VMEM((N_DEV, 8, W), F32), pltpu.VMEM((n_rows, cols), F32),
                        pltpu.VMEM(w_in_shard.shape, F32), pltpu.VMEM(w_in_shard.shape, BF16)]
        + [pltpu.VMEM(a.shape, F32) for a in casts] + [pltpu.VMEM(a.shape, BF16) for a in casts]
        + [pltpu.SemaphoreType.DMA, pltpu.SemaphoreType.DMA((n_c,)), pltpu.SemaphoreType.DMA((n_c,))]
        + t_c.sems + t_w.sems + t_m.sems,
        compiler_params=pltpu.CompilerParams(vmem_limit_bytes=VMEM_LIMIT),
    )(c_slab, w_ada_l, b_ada_l, w_in_shard, rpb_rev, *casts)
    return res[0], res[1], res[2], res[3:3 + n_c], res[3 + n_c]


def _ada_bwd(c_all, dmod_cols):
    def body(c_ref, d_ref, o_ref):
        o_ref[...] = lax.dot_general(_silu(c_ref[...]), d_ref[...], (((0,), (0,)), ((), ())),
                                     precision=lax.Precision.HIGHEST, preferred_element_type=F32)
    return pl.pallas_call(body, name="ada_bwd", out_shape=jax.ShapeDtypeStruct((c_all.shape[1], dmod_cols.shape[1]), F32),
                          compiler_params=pltpu.CompilerParams(vmem_limit_bytes=VMEM_LIMIT))(c_all, dmod_cols)


NA_PAIRS = NA_HEADS // 2
N_DR_PAD = 16


def _na_bias_rev(na_rpb):
    rev = jnp.pad(jnp.flip(na_rpb, axis=2), ((0, 0), (0, N_DR_PAD - N_DR), (0, GRID_W - N_DC)))
    return jnp.transpose(rev.reshape(NA_PAIRS, 2, N_DR_PAD, GRID_W), (0, 2, 1, 3)).reshape(NA_PAIRS, N_DR_PAD, 128)


def _na_bias_rows(r_ref, o_ref, p):
    k = lax.broadcasted_iota(jnp.int32, (GRID_W, 128), 0)
    lane = lax.broadcasted_iota(jnp.int32, (GRID_W, 128), 1)
    q = lane % GRID_W
    cs = jnp.clip(q - NA_COLS // 2, 0, GRID_W - NA_COLS)
    ok = (k >= cs) & (k < cs + NA_COLS)
    left = lane < GRID_W
    for dr in range(N_DR):
        row = jnp.broadcast_to(r_ref[p, dr:dr + 1, :], (GRID_W, 128))
        r0 = jnp.where(left, row, 0.0)
        r1 = jnp.where(left, pltpu.roll(row, GRID_W, axis=1), 0.0)
        y0 = pltpu.roll(r0, 128 - (NA_COLS - 1), axis=1, stride=1, stride_axis=0)
        y1 = pltpu.roll(r1, GRID_W - (NA_COLS - 1), axis=1, stride=1, stride_axis=0)
        o_ref[p, dr * GRID_W:(dr + 1) * GRID_W, :] = jnp.where(ok, jnp.where(left, y0, y1), NEG)


def _na_bias_grad(db, tasks=()):
    a = np.arange(128)
    flip = jnp.asarray(((a[:, None] // GRID_W == a[None, :] // GRID_W)
                        & (a[:, None] % GRID_W + a[None, :] % GRID_W == GRID_W - 1)).astype(np.float32))

    def body(d_ref, j_ref, o_ref):
        o_ref[...] = jnp.zeros_like(o_ref)
        for dr in range(N_DR):
            t = jnp.dot(d_ref[0, dr * GRID_W:(dr + 1) * GRID_W, :], j_ref[...], precision=lax.Precision.HIGHEST, preferred_element_type=F32)
            t = pltpu.roll(t, GRID_W + NA_COLS, axis=1, stride=1, stride_axis=0)
            o_ref[0, dr:dr + 1, :] = jnp.sum(t, axis=0, keepdims=True)

    (rows,), got = _hosted_call(
        body, "rpb_reduce", (NA_PAIRS,),
        [pl.BlockSpec((1, N_DR * GRID_W, 128), lambda p: (p, 0, 0)), _full((128, 128))],
        [pl.BlockSpec((1, N_DR_PAD, 128), lambda p: (p, 0, 0))],
        [jax.ShapeDtypeStruct((NA_PAIRS, N_DR_PAD, 128), F32)], (db, flip), tasks)
    g = rows.reshape(NA_PAIRS, N_DR_PAD, 2, GRID_W)[:, :N_DR, :, :N_DC]
    return jnp.transpose(g, (0, 2, 1, 3)).reshape(-1), got


def _rope_tables(S):
    half = HEAD_DIM // 2
    inv = np.float32(ROPE_THETA) ** (-np.arange(half, dtype=np.float32) / np.float32(half))
    ang = np.arange(S).astype(np.float32)[:, None] * inv[None, :]
    cos, sin = np.cos(ang).astype(np.float32), np.sin(ang).astype(np.float32)
    return jnp.asarray(np.tile(np.concatenate([cos, cos], axis=1), (1, 2))), jnp.asarray(np.tile(np.concatenate([-sin, sin], axis=1), (1, 2)))


def _rope_spec(tps, tm=TOKEN_TILE):
    return pl.BlockSpec((tm, 2 * HEAD_DIM), lambda i: (i % tps, 0))


def _rot_half(t):
    w = t.shape[1]
    lane = lax.broadcasted_iota(jnp.int32, t.shape, 1)
    return jnp.where((lane % HEAD_DIM) < HEAD_DIM // 2, pltpu.roll(t, w - HEAD_DIM // 2, axis=1),
                     pltpu.roll(t, HEAD_DIM // 2, axis=1))


def _tok_spec(w, tm=TOKEN_TILE):
    return pl.BlockSpec((tm, w), lambda i: (i, 0))


def _mod_spec(tps, d):
    return pl.BlockSpec((1, 6, d), lambda i: (i // tps, 0, 0))


def _bstat_spec(tps, w):
    return pl.BlockSpec((1, 8, w), lambda i: (i // tps, 0, 0))


def _attn_in(x2d, mod3, g_attn, w_in, cos_t, sin_t, S, tasks=(), tm=WIDE_TILE):
    T, D = x2d.shape
    tps = S // tm

    def body(x_ref, mod_ref, g_ref, w_ref, cos_ref, sin_ref, h_ref, qkv_ref):
        xn, _ = _rms(x_ref[...])
        h = (xn * g_ref[...]) * (1.0 + mod_ref[0, 1:2, :]) + mod_ref[0, 0:1, :]
        hb = h.astype(BF16)
        h_ref[...] = hb
        proj = _nt(hb, w_ref[...])
        rb = proj[:, ROPE_LO:ROPE_LO + ROPE_WIDTH]
        reps = (1, ROPE_WIDTH // (2 * HEAD_DIM))
        rb = rb * jnp.tile(cos_ref[...], reps) + _rot_half(rb) * jnp.tile(sin_ref[...], reps)
        qkv_ref[:, 0:NA_WIDTH] = (proj[:, 0:NA_WIDTH] * Q_SCALE).astype(BF16)
        qkv_ref[:, NA_WIDTH:ROPE_LO] = proj[:, NA_WIDTH:ROPE_LO].astype(BF16)
        qkv_ref[:, ROPE_LO:ROPE_LO + SW_WIDTH] = (rb[:, 0:SW_WIDTH] * Q_SCALE).astype(BF16)
        qkv_ref[:, ROPE_LO + SW_WIDTH:ROPE_LO + ROPE_WIDTH] = rb[:, SW_WIDTH:].astype(BF16)
        qkv_ref[:, ROPE_LO + ROPE_WIDTH:] = proj[:, ROPE_LO + ROPE_WIDTH:].astype(BF16)

    return _hosted_call(
        body, "attn_in", (T // tm,),
        [_tok_spec(D, tm), _mod_spec(tps, D), _full((1, D)), _full(w_in.shape), _rope_spec(tps, tm), _rope_spec(tps, tm)],
        [_tok_spec(D, tm), _tok_spec(IN_WIDTH, tm)],
        [jax.ShapeDtypeStruct((T, D), BF16), jax.ShapeDtypeStruct((T, IN_WIDTH), BF16)],
        (x2d, mod3, g_attn, w_in, cos_t, sin_t), tasks)


def _attn_out(oa, ob, x2d, mod3, g_na, g_sw, w_out, S, tasks=(), tm=WIDE_TILE):
    T, D = x2d.shape
    tps = S // tm

    def body(oa_ref, ob_ref, x_ref, mod_ref, gna_ref, gsw_ref, w_ref, mixin_ref, mix_ref, x1_ref):
        oan, _ = _rms(oa_ref[...])
        obn, _ = _rms(ob_ref[...])
        mixin = jnp.concatenate([oan * gna_ref[...], obn * gsw_ref[...]], axis=1).astype(BF16)
        mixin_ref[...] = mixin
        mix = _nn(mixin, w_ref[...])
        mix_ref[...] = mix
        x1_ref[...] = x_ref[...] + mod_ref[0, 2:3, :] * mix

    return _hosted_call(
        body, "attn_out", (T // tm,),
        [_tok_spec(NA_WIDTH, tm), _tok_spec(SW_WIDTH, tm), _tok_spec(D, tm), _mod_spec(tps, D),
         _full((1, NA_WIDTH)), _full((1, SW_WIDTH)), _full(w_out.shape)],
        [_tok_spec(NA_WIDTH + SW_WIDTH, tm), _tok_spec(D, tm), _tok_spec(D, tm)],
        [jax.ShapeDtypeStruct((T, NA_WIDTH + SW_WIDTH), BF16), jax.ShapeDtypeStruct((T, D), F32), jax.ShapeDtypeStruct((T, D), F32)],
        (oa, ob, x2d, mod3, g_na, g_sw, w_out), tasks)


def _ffn_up(x1, mod3, g_ffn, w_up, S, tasks=(), tm=WIDE_TILE):
    T, D = x1.shape
    F = w_up.shape[0] // 2
    tps = S // tm

    def body(x1_ref, mod_ref, g_ref, w_ref, h2_ref, val_ref, gt_ref):
        xn, _ = _rms(x1_ref[...])
        h2 = ((xn * g_ref[...]) * (1.0 + mod_ref[0, 4:5, :]) + mod_ref[0, 3:4, :]).astype(BF16)
        h2_ref[...] = h2
        u = _nt(h2, w_ref[...])
        val_ref[...] = u[:, :F].astype(BF16)
        gt_ref[...] = u[:, F:].astype(BF16)

    return _hosted_call(
        body, "ffn_up", (T // tm,), [_tok_spec(D, tm), _mod_spec(tps, D), _full((1, D)), _full(w_up.shape)],
        [_tok_spec(D, tm), _tok_spec(F, tm), _tok_spec(F, tm)],
        [jax.ShapeDtypeStruct((T, D), BF16), jax.ShapeDtypeStruct((T, F), BF16), jax.ShapeDtypeStruct((T, F), BF16)],
        (x1, mod3, g_ffn, w_up), tasks)


def _halo_specs(T, tps, w):
    per = TOKEN_TILE // 8
    prev = pl.BlockSpec((8, w), lambda i: (jnp.maximum(i * per - 1, 0), 0))
    nxt = pl.BlockSpec((8, w), lambda i: (jnp.minimum((i + 1) * per, T // 8 - 1), 0))
    return prev, nxt


def _seq_shifts(cur, before, after, ti, tps):
    tm = cur.shape[0]
    row = lax.broadcasted_iota(jnp.int32, cur.shape, 0)
    before = jnp.where(ti > 0, before.astype(F32), 0.0)
    after = jnp.where(ti < tps - 1, after.astype(F32), 0.0)
    return jnp.where(row == 0, before, pltpu.roll(cur, 1, axis=0)), jnp.where(row == tm - 1, after, pltpu.roll(cur, tm - 1, axis=0))


def _ffn_down(gt, val, conv_w, conv_b, w_down, x1, mod3, g_final, target, B, S):
    T, D = x1.shape
    F = gt.shape[1]
    tps = S // TOKEN_TILE
    prev, nxt = _halo_specs(T, tps, F)

    def body(gt_ref, prev_ref, next_ref, val_ref, cw_ref, cb_ref, w_ref, x1_ref, mod_ref, gf_ref, tgt_ref,
             a_ref, act_ref, vd_ref, dx2_ref, df_ref, gstat_ref, bstat_ref):
        i = pl.program_id(0)
        g = gt_ref[...].astype(F32)
        gprev, gnext = _seq_shifts(g, prev_ref[7:8, :], next_ref[0:1, :], i % tps, tps)
        gc = gprev * cw_ref[0:1, :] + g * cw_ref[1:2, :] + gnext * cw_ref[2:3, :] + cb_ref[...]
        sig = 1.0 / (1.0 + jnp.exp(-gc))
        act = gc * sig
        val = val_ref[...].astype(F32)
        act_ref[...] = act.astype(BF16)
        vd_ref[...] = (val * (sig + act - act * sig)).astype(BF16)
        a = (act * val).astype(BF16)
        a_ref[...] = a
        f = _nn(a, w_ref[...])
        gate = mod_ref[0, 5:6, :]
        x2 = x1_ref[...] + gate * f
        xn, r = _rms(x2)
        err = xn * gf_ref[...] - tgt_ref[...]
        dy = err * (1.0 / D)
        dx2 = _rms_bwd(xn, r, dy * gf_ref[...])
        dx2_ref[...] = dx2
        df_ref[...] = (gate * dx2).astype(BF16)

        @pl.when(i == 0)
        def _():
            gstat_ref[...] = jnp.zeros_like(gstat_ref)

        @pl.when(i % tps == 0)
        def _():
            bstat_ref[...] = jnp.zeros_like(bstat_ref)

        gstat_ref[0:1, :] += jnp.sum(dy * xn, axis=0, keepdims=True)
        tile_loss = jnp.sum(jnp.sum(err * err, axis=1, keepdims=True), axis=0, keepdims=True) * (0.5 / D)
        gstat_ref[1:2, :] += jnp.broadcast_to(tile_loss, (1, D))
        bstat_ref[0, 0:1, :] += jnp.sum(dx2 * f, axis=0, keepdims=True)

    return pl.pallas_call(
        body, name="ffn_down", grid=(T // TOKEN_TILE,),
        in_specs=[_tok_spec(F), prev, nxt, _tok_spec(F), _full(conv_w.shape), _full((1, F)), _full(w_down.shape),
                  _tok_spec(D), _mod_spec(tps, D), _full((1, D)), _tok_spec(D)],
        out_specs=[_tok_spec(F), _tok_spec(F), _tok_spec(F), _tok_spec(D), _tok_spec(D), _full((8, D)), _bstat_spec(tps, D)],
        out_shape=[jax.ShapeDtypeStruct((T, F), BF16), jax.ShapeDtypeStruct((T, F), BF16), jax.ShapeDtypeStruct((T, F), BF16),
                   jax.ShapeDtypeStruct((T, D), F32), jax.ShapeDtypeStruct((T, D), BF16),
                   jax.ShapeDtypeStruct((8, D), F32), jax.ShapeDtypeStruct((B, 8, D), F32)],
        compiler_params=_params("arbitrary"),
    )(gt, gt, gt, val, conv_w, conv_b, w_down, x1, mod3, g_final, target)


def _ffn_down_bwd(df, w_down, act, vd, tasks=(), tm=WIDE_TILE):
    T, D = df.shape
    F = act.shape[1]

    def body(df_ref, w_ref, act_ref, vd_ref, dval_ref, dgc_ref, cstat_ref):
        da = _nt(df_ref[...], w_ref[...])
        dval_ref[...] = (da * act_ref[...].astype(F32)).astype(BF16)
        dgc = da * vd_ref[...].astype(F32)
        dgc_ref[...] = dgc.astype(BF16)

        @pl.when(pl.program_id(0) == 0)
        def _():
            cstat_ref[...] = jnp.zeros_like(cstat_ref)

        cstat_ref[0:1, :] += jnp.sum(dgc, axis=0, keepdims=True)

    return _hosted_call(
        body, "ffn_down_bwd", (T // tm,),
        [_tok_spec(D, tm), _full(w_down.shape), _tok_spec(F, tm), _tok_spec(F, tm)],
        [_tok_spec(F, tm), _tok_spec(F, tm), _full((8, F))],
        [jax.ShapeDtypeStruct((T, F), BF16), jax.ShapeDtypeStruct((T, F), BF16), jax.ShapeDtypeStruct((8, F), F32)],
        (df, w_down, act, vd), tasks)


def _ffn_up_bwd(dgc, dval, gt, conv_w, w_up, x1, mod3, g_ffn, dx2, mix, B, S, tasks=()):
    T, D = x1.shape
    F = dgc.shape[1]
    tps = S // TOKEN_TILE
    prev, nxt = _halo_specs(T, tps, F)

    def body(dgc_ref, prev_ref, next_ref, dval_ref, gt_ref, cw_ref, w_ref, x1_ref, mod_ref, g_ref, dx2_ref, mix_ref,
             du_ref, dx1_ref, dmix_ref, gstat_ref, bstat_ref, cstat_ref):
        i = pl.program_id(0)
        d = dgc_ref[...].astype(F32)
        dprev, dnext = _seq_shifts(d, prev_ref[7:8, :], next_ref[0:1, :], i % tps, tps)
        g = gt_ref[...].astype(F32)

        @pl.when(i == 0)
        def _():
            cstat_ref[...] = jnp.zeros_like(cstat_ref)

        cstat_ref[1:2, :] += jnp.sum(dnext * g, axis=0, keepdims=True)
        cstat_ref[2:3, :] += jnp.sum(d * g, axis=0, keepdims=True)
        cstat_ref[3:4, :] += jnp.sum(dprev * g, axis=0, keepdims=True)
        dgt = dnext * cw_ref[0:1, :] + d * cw_ref[1:2, :] + dprev * cw_ref[2:3, :]
        du = jnp.concatenate([dval_ref[...], dgt.astype(BF16)], axis=1)
        du_ref[...] = du
        dh2 = _nn(du, w_ref[...])
        xn, r = _rms(x1_ref[...])
        scale1 = 1.0 + mod_ref[0, 4:5, :]
        xg = xn * g_ref[...]
        dx1 = dx2_ref[...] + _rms_bwd(xn, r, dh2 * g_ref[...] * scale1)
        dx1_ref[...] = dx1
        dmix_ref[...] = (mod_ref[0, 2:3, :] * dx1).astype(BF16)

        @pl.when(i == 0)
        def _():
            gstat_ref[...] = jnp.zeros_like(gstat_ref)

        @pl.when(i % tps == 0)
        def _():
            bstat_ref[...] = jnp.zeros_like(bstat_ref)

        gstat_ref[0:1, :] += jnp.sum(dh2 * scale1 * xn, axis=0, keepdims=True)
        bstat_ref[0, 0:1, :] += jnp.sum(dh2, axis=0, keepdims=True)
        bstat_ref[0, 1:2, :] += jnp.sum(dh2 * xg, axis=0, keepdims=True)
        bstat_ref[0, 2:3, :] += jnp.sum(dx1 * mix_ref[...], axis=0, keepdims=True)

    return _hosted_call(
        body, "ffn_up_bwd", (T // TOKEN_TILE,),
        [_tok_spec(F), prev, nxt, _tok_spec(F), _tok_spec(F), _full(conv_w.shape), _full(w_up.shape), _tok_spec(D),
         _mod_spec(tps, D), _full((1, D)), _tok_spec(D), _tok_spec(D)],
        [_tok_spec(2 * F), _tok_spec(D), _tok_spec(D), _full((8, D)), _bstat_spec(tps, D), _full((8, F))],
        [jax.ShapeDtypeStruct((T, 2 * F), BF16), jax.ShapeDtypeStruct((T, D), F32), jax.ShapeDtypeStruct((T, D), BF16),
         jax.ShapeDtypeStruct((8, D), F32), jax.ShapeDtypeStruct((B, 8, D), F32), jax.ShapeDtypeStruct((8, F), F32)],
        (dgc, dgc, dgc, dval, gt, conv_w, w_up, x1, mod3, g_ffn, dx2, mix), tasks)


def _attn_out_bwd(dmix, w_out, oa, ob, g_na, g_sw, tasks=(), tm=WIDE_TILE):
    T, D = dmix.shape

    def body(dmix_ref, w_ref, oa_ref, ob_ref, gna_ref, gsw_ref, doa_ref, dob_ref, gstat_ref):
        dmixin = _nt(dmix_ref[...], w_ref[...])

        @pl.when(pl.program_id(0) == 0)
        def _():
            gstat_ref[...] = jnp.zeros_like(gstat_ref)

        for k, (o_ref, g_ref, do_ref) in enumerate(((oa_ref, gna_ref, doa_ref), (ob_ref, gsw_ref, dob_ref))):
            dn = dmixin[:, k * NA_WIDTH:(k + 1) * NA_WIDTH]
            on, r = _rms(o_ref[...])
            gstat_ref[k:k + 1, :] += jnp.sum(dn * on, axis=0, keepdims=True)
            do_ref[...] = _rms_bwd(on, r, dn * g_ref[...]).astype(BF16)

    hs = jax.ShapeDtypeStruct((T, NA_WIDTH), BF16)
    return _hosted_call(
        body, "attn_out_bwd", (T // tm,),
        [_tok_spec(D, tm), _full(w_out.shape), _tok_spec(NA_WIDTH, tm), _tok_spec(SW_WIDTH, tm), _full((1, NA_WIDTH)), _full((1, SW_WIDTH))],
        [_tok_spec(NA_WIDTH, tm), _tok_spec(SW_WIDTH, tm), _full((8, NA_WIDTH))],
        [hs, hs, jax.ShapeDtypeStruct((8, NA_WIDTH), F32)],
        (dmix, w_out, oa, ob, g_na, g_sw), tasks)


def _attn_in_bwd(dqa, dka, dva, dqb, dkb, dvb, cos_t, sin_t, w_in, x2d, mod3, g_attn, dx1, B, S, tm=WIDE_TILE):
    T, D = x2d.shape
    tps = S // tm

    def body(dqa_ref, dka_ref, dva_ref, dqb_ref, dkb_ref, dvb_ref, cos_ref, sin_ref, w_ref, x_ref, mod_ref, g_ref, dx1_ref,
             gx_ref, dproj_ref, gstat_ref, bstat_ref):
        i = pl.program_id(0)
        drb = jnp.concatenate([dqb_ref[...] * Q_SCALE, dkb_ref[...]], axis=1).astype(F32)
        reps = (1, ROPE_WIDTH // (2 * HEAD_DIM))
        drb = drb * jnp.tile(cos_ref[...], reps) + _rot_half(drb * jnp.tile(sin_ref[...], reps))
        dproj = jnp.concatenate([(dqa_ref[...] * Q_SCALE).astype(BF16), dka_ref[...].astype(BF16), dva_ref[...].astype(BF16),
                                 drb.astype(BF16), dvb_ref[...].astype(BF16)], axis=1)
        dproj_ref[...] = dproj
        dh = _nn(dproj, w_ref[...])
        xn, r = _rms(x_ref[...])
        scale1 = 1.0 + mod_ref[0, 1:2, :]
        gx_ref[...] = dx1_ref[...] + _rms_bwd(xn, r, dh * g_ref[...] * scale1)

        @pl.when(i == 0)
        def _():
            gstat_ref[...] = jnp.zeros_like(gstat_ref)

        @pl.when(i % tps == 0)
        def _():
            bstat_ref[...] = jnp.zeros_like(bstat_ref)

        gstat_ref[0:1, :] += jnp.sum(dh * scale1 * xn, axis=0, keepdims=True)
        bstat_ref[0, 0:1, :] += jnp.sum(dh, axis=0, keepdims=True)
        bstat_ref[0, 1:2, :] += jnp.sum(dh * (xn * g_ref[...]), axis=0, keepdims=True)

    rope = _rope_spec(tps, tm)
    return pl.pallas_call(
        body, name="attn_in_bwd", grid=(T // tm,),
        in_specs=[_tok_spec(NA_WIDTH, tm), _tok_spec(NA_WIDTH, tm), _tok_spec(NA_WIDTH, tm), _tok_spec(SW_WIDTH, tm),
                  _tok_spec(SW_KV_WIDTH, tm), _tok_spec(SW_KV_WIDTH, tm), rope, rope, _full(w_in.shape), _tok_spec(D, tm),
                  _mod_spec(tps, D), _full((1, D)), _tok_spec(D, tm)],
        out_specs=[_tok_spec(D, tm), _tok_spec(IN_WIDTH, tm), _full((8, D)), _bstat_spec(tps, D)],
        out_shape=[jax.ShapeDtypeStruct((T, D), F32), jax.ShapeDtypeStruct((T, IN_WIDTH), BF16),
                   jax.ShapeDtypeStruct((8, D), F32), jax.ShapeDtypeStruct((B, 8, D), F32)],
        compiler_params=_params("arbitrary"),
    )(dqa, dka, dva, dqb, dkb, dvb, cos_t, sin_t, w_in, x2d, mod3, g_attn, dx1)


def _matmul_tn(a, b, name, tm=None, tk=512):
    T, M = a.shape
    N = b.shape[1]
    tm = M if tm is None else tm
    nk = T // tk

    def body(a_ref, b_ref, o_ref, acc):
        k = pl.program_id(1)

        @pl.when(k == 0)
        def _():
            acc[...] = jnp.zeros_like(acc)

        acc[...] += _tn(a_ref[...], b_ref[...])

        @pl.when(k == nk - 1)
        def _():
            o_ref[...] = acc[...].astype(BF16)

    return pl.pallas_call(
        body, name=name, grid=(M // tm, nk),
        in_specs=[pl.BlockSpec((tk, tm), lambda i, k: (k, i)), pl.BlockSpec((tk, N), lambda i, k: (k, 0))],
        out_specs=pl.BlockSpec((tm, N), lambda i, k: (i, 0)),
        out_shape=jax.ShapeDtypeStruct((M, N), BF16),
        scratch_shapes=[pltpu.VMEM((tm, N), F32)],
        compiler_params=_params("parallel", "arbitrary"),
    )(a, b)


def _na_geometry(S):
    rows = S // GRID_W
    wr = min(NA_ROWS_MAX, rows)
    return rows, wr


def _na_window(r, rows, wr):
    rs = jnp.clip(r - wr // 2, 0, rows - wr)
    return pl.multiple_of(rs * GRID_W, GRID_W), pl.multiple_of((rs - r + NA_ROWS_MAX - 1) * GRID_W, GRID_W)


NA_STEP_PAIRS = 2
NA_GW = NA_STEP_PAIRS * 128
NA_BWD_ROWS = 4
NA_ROWS_PER_STEP = 4


def _na_specs(S, kw_n, order):
    ng = NA_PAIRS // NA_STEP_PAIRS

    def col(k):
        return pl.BlockSpec((1, S, NA_GW), lambda *ids: (order(*ids)[0], 0, k * ng + order(*ids)[1]))
    bias = pl.BlockSpec((NA_STEP_PAIRS, N_DR * GRID_W, 128), lambda *ids: (order(*ids)[1], 0, 0))
    out = pl.BlockSpec((1, S, NA_GW), lambda *ids: (order(*ids)[0], 0, order(*ids)[1]))
    return col(0), col(1), col(2), bias, out


def _block_diag(t):
    left = lax.broadcasted_iota(jnp.int32, t.shape, 1) < HEAD_DIM
    zero = jnp.zeros_like(t)
    return jnp.concatenate([jnp.where(left, t, zero), jnp.where(left, zero, t)], axis=0)


def _diag_blocks(res):
    left = lax.broadcasted_iota(jnp.int32, (HEAD_DIM, 128), 1) < HEAD_DIM
    return jnp.where(left, res[:HEAD_DIM], res[HEAD_DIM:])


def _col_softmax(st):
    e = jnp.exp(st - jnp.max(st, axis=0, keepdims=True))
    return e * (1.0 / jnp.sum(e, axis=0, keepdims=True))


def _na_fwd(qkv, bias, tasks=()):
    B, S, _ = qkv.shape
    rows, wr = _na_geometry(S)
    kw_n = wr * GRID_W

    def body(q_ref, k_ref, v_ref, b_ref, o_ref):
        def step(it, carry):
            win = [_na_window(it * NA_ROWS_PER_STEP + u, rows, wr) for u in range(NA_ROWS_PER_STEP)]
            qrows = [pl.ds(pl.multiple_of((it * NA_ROWS_PER_STEP + u) * GRID_W, GRID_W), GRID_W) for u in range(NA_ROWS_PER_STEP)]
            krows = [pl.ds(w[0], kw_n) for w in win]
            brows = [pl.ds(w[1], kw_n) for w in win]
            lanes = [pl.ds(p * 128, 128) for p in range(NA_STEP_PAIRS)]
            chains = [(u, p) for u in range(NA_ROWS_PER_STEP) for p in range(NA_STEP_PAIRS)]
            st = {(u, p): _nt(k_ref[0, krows[u], lanes[p]], _block_diag(q_ref[0, qrows[u], lanes[p]])) for u, p in chains}
            pn = {(u, p): _col_softmax(st[(u, p)] + b_ref[p, brows[u], :]).astype(BF16) for u, p in chains}
            out = {(u, p): _diag_blocks(_tn(pn[(u, p)], v_ref[0, krows[u], lanes[p]])) for u, p in chains}
            for u in range(NA_ROWS_PER_STEP):
                o_ref[0, qrows[u], :] = jnp.concatenate([out[(u, p)] for p in range(NA_STEP_PAIRS)], axis=1)
            return carry

        lax.fori_loop(0, rows // NA_ROWS_PER_STEP, step, 0)

    q, k, v, bs, out = _na_specs(S, kw_n, lambda b, g: (b, g))
    return _hosted_call(body, "na_fwd", (B, NA_PAIRS // NA_STEP_PAIRS), [q, k, v, bs], [out],
                        [jax.ShapeDtypeStruct((B, S, NA_WIDTH), F32)], (qkv, qkv, qkv, bias), tasks)


def _na_bwd(qkv, bias, doa, tasks=()):
    B, S, _ = qkv.shape
    rows, wr = _na_geometry(S)
    kw_n = wr * GRID_W

    def body(q_ref, k_ref, v_ref, b_ref, do_ref, dq_ref, dk_ref, dv_ref, db_ref, dk_acc, dv_acc):
        @pl.when(pl.program_id(1) == 0)
        def _():
            db_ref[...] = jnp.zeros_like(db_ref)

        dk_acc[...] = jnp.zeros_like(dk_acc)
        dv_acc[...] = jnp.zeros_like(dv_acc)

        def step(it, carry):
            nu, pairs = range(NA_BWD_ROWS), range(NA_STEP_PAIRS)
            win = [_na_window(it * NA_BWD_ROWS + u, rows, wr) for u in nu]
            qrows = [pl.ds(pl.multiple_of((it * NA_BWD_ROWS + u) * GRID_W, GRID_W), GRID_W) for u in nu]
            krows = [pl.ds(w[0], kw_n) for w in win]
            brows = [pl.ds(w[1], kw_n) for w in win]
            lanes = [pl.ds(p * 128, 128) for p in pairs]
            chains = [(u, p) for u in nu for p in pairs]
            kp = {(u, p): k_ref[0, krows[u], lanes[p]] for u, p in chains}
            qbd = {(u, p): _block_diag(q_ref[0, qrows[u], lanes[p]]) for u, p in chains}
            dobd = {(u, p): _block_diag(do_ref[0, qrows[u], lanes[p]]) for u, p in chains}
            st = {c: _nt(kp[c], qbd[c]) for c in chains}
            dpt = {(u, p): _nt(v_ref[0, krows[u], lanes[p]], dobd[(u, p)]) for u, p in chains}
            pn = {(u, p): _col_softmax(st[(u, p)] + b_ref[p, brows[u], :]) for u, p in chains}
            dst = {c: pn[c] * (dpt[c] - jnp.sum(pn[c] * dpt[c], axis=0, keepdims=True)) for c in chains}
            dsb = {c: dst[c].astype(BF16) for c in chains}
            dq = {c: _diag_blocks(_tn(dsb[c], kp[c])) for c in chains}
            dk = {c: _nn(dsb[c], qbd[c]) for c in chains}
            dv = {c: _nn(pn[c].astype(BF16), dobd[c]) for c in chains}
            for u in nu:
                dq_ref[0, qrows[u], :] = jnp.concatenate([dq[(u, p)] for p in pairs], axis=1).astype(BF16)
                dk_acc[krows[u], :] += jnp.concatenate([dk[(u, p)] for p in pairs], axis=1)
                dv_acc[krows[u], :] += jnp.concatenate([dv[(u, p)] for p in pairs], axis=1)
                for p in pairs:
                    db_ref[p, brows[u], :] += dst[(u, p)]
            return carry

        lax.fori_loop(0, rows // NA_BWD_ROWS, step, 0)

        def emit(i, carry):
            r = pl.ds(pl.multiple_of(i * 256, 256), 256)
            dk_ref[0, r, :] = dk_acc[r, :].astype(BF16)
            dv_ref[0, r, :] = dv_acc[r, :].astype(BF16)
            return carry

        lax.fori_loop(0, S // 256, emit, 0)

    q, k, v, bs, out = _na_specs(S, kw_n, lambda g, b: (b, g))
    hs = jax.ShapeDtypeStruct((B, S, NA_WIDTH), BF16)
    return _hosted_call(body, "na_bwd", (NA_PAIRS // NA_STEP_PAIRS, B), [q, k, v, bs, out], [out, out, out, bs],
                        [hs, hs, hs, jax.ShapeDtypeStruct((NA_PAIRS, N_DR * GRID_W, 128), F32)], (qkv, qkv, qkv, bias, doa), tasks,
                        scratch_shapes=[pltpu.VMEM((S, NA_GW), F32), pltpu.VMEM((S, NA_GW), F32)])


SW_PAIRS = SW_HEADS // 2


def _sw_band(n, S):
    kw_n = 3 * SW_BLOCK
    start = pl.multiple_of(jnp.clip(n * SW_BLOCK - SW_BLOCK, 0, S - kw_n), SW_BLOCK)
    kpos = start + lax.broadcasted_iota(jnp.int32, (kw_n, SW_BLOCK), 0)
    qpos = n * SW_BLOCK + lax.broadcasted_iota(jnp.int32, (kw_n, SW_BLOCK), 1)
    return start, jnp.abs(qpos - kpos) <= SW_WINDOW


def _kv_halves(t):
    left = lax.broadcasted_iota(jnp.int32, t.shape, 1) < HEAD_DIM
    swapped = pltpu.roll(t, HEAD_DIM, axis=1)
    zero = jnp.zeros_like(t)
    return {(0, 0): jnp.where(left, t, zero), (0, 1): jnp.where(left, zero, swapped),
            (1, 0): jnp.where(left, swapped, zero), (1, 1): jnp.where(left, zero, t)}


def _sw_probs(st, ok, sk):
    st = jnp.where(ok, st, NEG)
    m = jnp.maximum(jnp.max(st, axis=0, keepdims=True), sk)
    e = jnp.exp(st - m)
    esk = jnp.exp(sk - m)
    inv = 1.0 / (jnp.sum(e, axis=0, keepdims=True) + esk)
    return e * inv, esk * inv


def _sw_specs(S):
    q = pl.BlockSpec((1, S, SW_WIDTH), lambda b, *_: (b, 0, ROPE_LO // SW_WIDTH))
    k = pl.BlockSpec((1, S, SW_KV_WIDTH), lambda b, *_: (b, 0, (ROPE_LO + SW_WIDTH) // SW_KV_WIDTH))
    v = pl.BlockSpec((1, S, SW_KV_WIDTH), lambda b, *_: (b, 0, (ROPE_LO + ROPE_WIDTH) // SW_KV_WIDTH))
    return q, k, v


SW_FWD_SPLIT = 2


def _sw_fwd(sink, qkv, tasks=()):
    B, S, _ = qkv.shape
    kw_n = 3 * SW_BLOCK

    def body(sink_ref, q_ref, k_ref, v_ref, o_ref):
        def step(n, carry):
            start, ok = _sw_band(n, S)
            qrows = pl.ds(pl.multiple_of(n * SW_BLOCK, SW_BLOCK), SW_BLOCK)
            krows = pl.ds(start, kw_n)
            kh, vh = _kv_halves(k_ref[0, krows, :]), _kv_halves(v_ref[0, krows, :])
            heads = [(p, e) for p in range(SW_PAIRS) for e in range(2)]
            qp = [q_ref[0, qrows, pl.ds(p * 128, 128)] for p in range(SW_PAIRS)]
            kv_of = lambda p: p // (SW_PAIRS // SW_KV_HEADS)
            st = {(p, e): _nt(kh[(kv_of(p), e)], qp[p]) for p, e in heads}
            pn = {(p, e): _sw_probs(st[(p, e)], ok, sink_ref[2 * p + e])[0].astype(BF16) for p, e in heads}
            outs = [_tn(pn[(p, 0)], vh[(kv_of(p), 0)]) + _tn(pn[(p, 1)], vh[(kv_of(p), 1)]) for p in range(SW_PAIRS)]
            o_ref[0, qrows, :] = jnp.concatenate(outs, axis=1)
            return carry

        half = (S // SW_BLOCK) // SW_FWD_SPLIT
        lax.fori_loop(pl.program_id(1) * half, (pl.program_id(1) + 1) * half, step, 0)

    q, k, v = _sw_specs(S)
    return _hosted_call(
        body, "sw_fwd", (B, SW_FWD_SPLIT), [pl.BlockSpec(memory_space=pltpu.SMEM), q, k, v],
        [pl.BlockSpec((1, S, SW_WIDTH), lambda b, s: (b, 0, 0))], [jax.ShapeDtypeStruct((B, S, SW_WIDTH), F32)],
        (sink, qkv, qkv, qkv), tasks)


def _sw_bwd(sink, qkv, dob):
    B, S, _ = qkv.shape
    kw_n = 3 * SW_BLOCK

    fold_rows = 256

    def body(sink_ref, q_ref, k_ref, v_ref, do_ref, dq_ref, dk_ref, dv_ref, dsink_ref, dk_acc, dv_acc):
        @pl.when(pl.program_id(0) == 0)
        def _():
            dsink_ref[...] = jnp.zeros_like(dsink_ref)

        dk_acc[...] = jnp.zeros_like(dk_acc)
        dv_acc[...] = jnp.zeros_like(dv_acc)
        ppk = SW_PAIRS // SW_KV_HEADS

        def step(n, carry):
            start, ok = _sw_band(n, S)
            qrows = pl.ds(pl.multiple_of(n * SW_BLOCK, SW_BLOCK), SW_BLOCK)
            krows = pl.ds(start, kw_n)
            kh, vh = _kv_halves(k_ref[0, krows, :]), _kv_halves(v_ref[0, krows, :])
            heads = [(p, e) for p in range(SW_PAIRS) for e in range(2)]
            qp = [q_ref[0, qrows, pl.ds(p * 128, 128)] for p in range(SW_PAIRS)]
            dop = [do_ref[0, qrows, pl.ds(p * 128, 128)] for p in range(SW_PAIRS)]
            st = {(p, e): _nt(kh[(p // ppk, e)], qp[p]) for p, e in heads}
            dpt = {(p, e): _nt(vh[(p // ppk, e)], dop[p]) for p, e in heads}
            pnb, dsb = {}, {}
            for p, e in heads:
                pn, psink = _sw_probs(st[(p, e)], ok, sink_ref[2 * p + e])
                delta = jnp.sum(pn * dpt[(p, e)], axis=0, keepdims=True)
                dsb[(p, e)] = (pn * (dpt[(p, e)] - delta)).astype(BF16)
                pnb[(p, e)] = pn.astype(BF16)
                dsink_ref[2 * p + e:2 * p + e + 1, :] += -(psink * delta)
            dq_ref[0, qrows, :] = jnp.concatenate(
                [_tn(dsb[(p, 0)], kh[(p // ppk, 0)]) + _tn(dsb[(p, 1)], kh[(p // ppk, 1)]) for p in range(SW_PAIRS)],
                axis=1).astype(BF16)
            left = lax.broadcasted_iota(jnp.int32, (kw_n, 128), 1) < HEAD_DIM
            dks, dvs = [], []
            for kv in range(SW_KV_HEADS):
                dk = dv = None
                for p in range(kv * ppk, (kv + 1) * ppk):
                    dk_p = jnp.where(left, _nn(dsb[(p, 0)], qp[p]), _nn(dsb[(p, 1)], qp[p]))
                    dv_p = jnp.where(left, _nn(pnb[(p, 0)], dop[p]), _nn(pnb[(p, 1)], dop[p]))
                    dk = dk_p if dk is None else dk + dk_p
                    dv = dv_p if dv is None else dv + dv_p
                dks.append(dk)
                dvs.append(dv)
            dk_acc[krows, :] += jnp.concatenate(dks, axis=1)
            dv_acc[krows, :] += jnp.concatenate(dvs, axis=1)
            return carry

        lax.fori_loop(0, S // SW_BLOCK, step, 0)

        def fold(i, carry):
            rows = pl.ds(pl.multiple_of(i * fold_rows, fold_rows), fold_rows)
            left = lax.broadcasted_iota(jnp.int32, (fold_rows, 128), 1) < HEAD_DIM
            for acc, out_ref in ((dk_acc, dk_ref), (dv_acc, dv_ref)):
                a, b = acc[rows, 0:128], acc[rows, 128:256]
                out_ref[0, rows, :] = jnp.where(left, a + pltpu.roll(a, HEAD_DIM, axis=1),
                                                b + pltpu.roll(b, HEAD_DIM, axis=1)).astype(BF16)
            return carry

        lax.fori_loop(0, S // fold_rows, fold, 0)

        @pl.when(pl.program_id(0) == B - 1)
        def _():
            dsink_ref[...] = jnp.broadcast_to(jnp.sum(dsink_ref[...], axis=1, keepdims=True), dsink_ref.shape)

    q, k, v = _sw_specs(S)
    qo = pl.BlockSpec((1, S, SW_WIDTH), lambda b: (b, 0, 0))
    ko = pl.BlockSpec((1, S, SW_KV_WIDTH), lambda b: (b, 0, 0))
    return pl.pallas_call(
        body, name="sw_bwd", grid=(B,),
        in_specs=[pl.BlockSpec(memory_space=pltpu.SMEM), q, k, v, qo],
        out_specs=[qo, ko, ko, _full((SW_HEADS, 128))],
        out_shape=[jax.ShapeDtypeStruct((B, S, SW_WIDTH), BF16), jax.ShapeDtypeStruct((B, S, SW_KV_WIDTH), BF16),
                   jax.ShapeDtypeStruct((B, S, SW_KV_WIDTH), BF16), jax.ShapeDtypeStruct((SW_HEADS, 128), F32)],
        scratch_shapes=[pltpu.VMEM((S, 2 * SW_KV_WIDTH), F32), pltpu.VMEM((S, 2 * SW_KV_WIDTH), F32)],
        compiler_params=_params("arbitrary"),
    )(sink, qkv, qkv, qkv, dob)


def _pack_sum_adamw(packs, params, pick):
    W = packs.shape[1]
    n_p = len(params)

    def body(p_ref, *refs):
        ins, tot_ref, pick_ref, outs = refs[:3 * n_p], refs[3 * n_p], refs[3 * n_p + 1], refs[3 * n_p + 2:]
        tot = p_ref[0:8, :]
        for d in range(1, N_DEV):
            tot = tot + p_ref[8 * d:8 * d + 8, :]
        tot_ref[...] = tot
        pick_ref[...] = tot[pick[0]:pick[0] + 1, pick[1]:pick[1] + 1]
        for i, (w, _, _, rows, off) in enumerate(params):
            w_ref, m_ref, v_ref = ins[3 * i:3 * i + 3]
            g_ref, d_ref, nm_ref, nv_ref = outs[4 * i:4 * i + 4]
            if w.ndim == 3:
                n_a, n_b, n = w.shape
                for a in range(n_a):
                    for b in range(n_b):
                        o = off + (b * n_a + a) * n
                        g_ref[a, b:b + 1, :] = tot_ref[rows[0]:rows[0] + 1, o:o + n]
                g = g_ref[...]
            else:
                n = w.shape[1]
                g = tot[rows[0]:rows[0] + 1, off:off + n]
                for r in rows[1:]:
                    g = g + tot[r:r + 1, off:off + n]
                g_ref[...] = g
            d_ref[...], nm_ref[...], nv_ref[...] = _adam_update(w_ref[...], g, m_ref[...], v_ref[...])

    res = pl.pallas_call(
        body, name="small_adamw",
        out_shape=[jax.ShapeDtypeStruct((8, W), F32), jax.ShapeDtypeStruct((1, 1), F32)]
        + [jax.ShapeDtypeStruct(p[0].shape, F32) for p in params for _ in range(4)],
        compiler_params=pltpu.CompilerParams(vmem_limit_bytes=VMEM_LIMIT),
    )(packs, *[a for p in params for a in p[:3]])
    return res[0], res[1], [res[2 + 4 * i:6 + 4 * i] for i in range(n_p)]


def _adam_update(w, g, m, v):
    c1 = 1.0 - ADAM_B1 ** ADAM_STEP
    c2 = 1.0 - ADAM_B2 ** ADAM_STEP
    nm = ADAM_B1 * m + (1.0 - ADAM_B1) * g
    nv = ADAM_B2 * v + (1.0 - ADAM_B2) * (g * g)
    return -ADAM_LR * ((nm / c1) / (jnp.sqrt(nv / c2) + ADAM_EPS) + ADAM_WD * w), nm, nv


def _adamw(w, g, m, v, name):
    def body(w_ref, g_ref, m_ref, v_ref, d_ref, nm_ref, nv_ref):
        d_ref[...], nm_ref[...], nv_ref[...] = _adam_update(w_ref[...], g_ref[...], m_ref[...], v_ref[...])

    s = jax.ShapeDtypeStruct(w.shape, F32)
    return pl.pallas_call(body, name=name, out_shape=[s, s, s],
                          compiler_params=pltpu.CompilerParams(vmem_limit_bytes=VMEM_LIMIT))(w, g, m, v)


def _sum_adamw_rows(R):
    return max(r for r in range(16, min(R, 256) + 1, 16) if R % r == 0)


def _sum_adamw_steps(R):
    return R // _sum_adamw_rows(R)


def _sum_first(own, recvb, name, tasks):
    R, C = own.shape
    rc = _sum_adamw_rows(R)

    def body(own_ref, r_ref, p_ref):
        p_ref[...] = (own_ref[...] + r_ref[0].astype(F32)) + r_ref[1].astype(F32)

    blk = pl.BlockSpec((rc, C), lambda i: (i, 0))
    (part,), got = _hosted_call(body, name, (R // rc,), [blk, pl.BlockSpec((2, rc, C), lambda i: (0, i, 0))], [blk],
                                [jax.ShapeDtypeStruct((R, C), F32)], (own, recvb), tasks)
    return part, got


def _sum_adamw(own, recvb, w, m, v, name, done=0):
    R, C = own.shape
    rc = _sum_adamw_rows(R)
    left = 3 - done
    assert 3 % left == 0

    def body(own_ref, r_ref, w_ref, m_ref, v_ref, g_ref, d_ref, nm_ref, nv_ref):
        g = own_ref[...]
        for j in range(left):
            g = g + r_ref[j].astype(F32)
        g_ref[...] = g
        d_ref[...], nm_ref[...], nv_ref[...] = _adam_update(w_ref[...], g, m_ref[...], v_ref[...])

    blk = pl.BlockSpec((rc, C), lambda i: (i, 0))
    s = jax.ShapeDtypeStruct((R, C), F32)
    return pl.pallas_call(
        body, name=name, grid=(R // rc,),
        in_specs=[blk, pl.BlockSpec((left, rc, C), lambda i: (done // left, i, 0)), blk, blk, blk],
        out_specs=[blk, blk, blk, blk], out_shape=[s, s, s, s], compiler_params=_params("parallel"),
    )(own, recvb, w, m, v)


def _by_device(dw):
    return dw.reshape(N_DEV, dw.shape[0] // N_DEV, dw.shape[1])


def _local_step(x, mod, g_attn, w_in, bias, sw_sink, g_na_out, g_sw_out, w_out, g_ffn, w_up, conv_w, conv_b, w_down,
                g_final, target, sharded):
    B, S, D = x.shape
    T = B * S
    x2d = x.reshape(T, D)
    mod3 = mod.reshape(B, 6, D)
    cos_t, sin_t = _rope_tables(S)
    sink = sw_sink.reshape(SW_HEADS)
    n_tiles = T // WIDE_TILE
    full = lambda g: g.reshape(N_DEV * g.shape[1], g.shape[2])

    rider = lambda w, mid, lo, n, into=None: [_gather_task(w, mid, rows=(lo, n), into=into)] if sharded else []
    if sharded:
        qu, hd = w_up.shape[0] // 4, w_down.shape[0] // 2
    (h, qkv), got = _attn_in(x2d, mod3, g_attn, w_in, cos_t, sin_t, S,
                             [_gather_task(w_out, n_tiles // 2)] + rider(w_up, n_tiles - 1, 0, qu) if sharded else [])
    if sharded:
        w_out, w_up_buf = full(got[0][0]), got[1][0]
    qkv3 = qkv.reshape(B, S, IN_WIDTH)
    na_steps, sw_steps = B * (NA_PAIRS // NA_STEP_PAIRS), B * SW_FWD_SPLIT
    (oa,), got = _na_fwd(qkv3, bias, rider(w_up, na_steps - 1, qu, 2 * qu, w_up_buf) if sharded else [])
    if sharded:
        w_up_buf = got[0][0]
    oa = oa.reshape(T, NA_WIDTH)
    (ob,), got = _sw_fwd(sink, qkv3, rider(w_up, sw_steps // 2, 3 * qu, qu, w_up_buf) if sharded else [])
    if sharded:
        w_up = full(got[0][0])
    ob = ob.reshape(T, SW_WIDTH)
    (mixin, mix, x1), _ = _attn_out(oa, ob, x2d, mod3, g_na_out, g_sw_out, w_out, S)
    (h2, val, gt), got = _ffn_up(x1, mod3, g_ffn, w_up, S, rider(w_down, 3 * n_tiles // 4, 0, 2 * hd) if sharded else [])
    if sharded:
        w_down = full(got[0][0])
    a, act, vd, dx2, df, gstat_f, bstat_f = _ffn_down(gt, val, conv_w, conv_b, w_down, x1, mod3, g_final, target.reshape(T, D), B, S)
    F = val.shape[1]

    dw_down = _matmul_tn(a, df, "dw_down")
    (dval, dgc, cstat), got = _ffn_down_bwd(df, w_down, act, vd, [_swap_task(_by_device(dw_down))] if sharded else [])
    if sharded:
        send_down, own_down = _chip_sums(_by_device(dw_down), got[0][0])
    (du, dx1, dmix, gstat_u, bstat_u, cstat_w), got = _ffn_up_bwd(dgc, dval, gt, conv_w, w_up, x1, mod3, g_ffn, dx2, mix, B, S,
                                                                  [_exchange_task(send_down)] if sharded else [])
    if sharded:
        dw_down = (own_down, got[0][0])
    dw_up = _matmul_tn(du, h2, "dw_up", tm=F)
    dw_out = _matmul_tn(mixin, dmix, "dw_out")
    (doa, dob, gstat_o), got = _attn_out_bwd(dmix, w_out, oa, ob, g_na_out, g_sw_out,
                                             [_swap_task(_by_device(dw_up)), _swap_task(_by_device(dw_out))] if sharded else [])
    if sharded:
        send_up, own_up = _chip_sums(_by_device(dw_up), got[0][0])
        send_out, own_out = _chip_sums(_by_device(dw_out), got[1][0])
    (dqa, dka, dva, dbt), got = _na_bwd(qkv3, bias, doa.reshape(B, S, NA_WIDTH),
                                        [_exchange_task(send_up), _exchange_task(send_out)] if sharded else [])
    if sharded:
        dw_up, dw_out = (own_up, got[0][0]), (own_out, got[1][0])
    dqb, dkb, dvb, dsink = _sw_bwd(sink, qkv3, dob.reshape(B, S, SW_WIDTH))
    r2 = lambda t: t.reshape(T, t.shape[-1])
    grad_x, dproj, gstat_i, bstat_i = _attn_in_bwd(r2(dqa), r2(dka), r2(dva), r2(dqb), r2(dkb), r2(dvb), cos_t, sin_t, w_in, x2d, mod3,
                                                   g_attn, dx1, B, S)
    dw_in = _matmul_tn(dproj, h, "dw_in")

    dmod = jnp.stack([bstat_i[:, 0], bstat_i[:, 1], bstat_u[:, 2], bstat_u[:, 0], bstat_u[:, 1], bstat_f[:, 0]], axis=1)
    small = dict(g_attn=gstat_i[0], g_ffn=gstat_u[0], g_final=gstat_f[0], loss=gstat_f[1, 0], g_na_out=gstat_o[0], g_sw_out=gstat_o[1],
                 sw_sink=dsink[:, 0], conv_b=cstat[0], conv_w=cstat_w[1:4], dbt=dbt,
                 raw=(bstat_i, bstat_u, bstat_f, gstat_i, gstat_u, gstat_f, gstat_o, dsink, cstat, cstat_w))
    return grad_x.reshape(B, S, D), dict(w_in=dw_in, w_out=dw_out, w_up=dw_up, w_down=dw_down), dmod, small


def _pack_slab(raw, drpb):
    D, F = raw[3].shape[1], raw[8].shape[1]
    n_seq = raw[0].shape[0]

    def body(bi_ref, bu_ref, bf_ref, gi_ref, gu_ref, gf_ref, go_ref, ds_ref, cs_ref, cw_ref, rp_ref, o_ref):
        o_ref[...] = jnp.zeros_like(o_ref)
        for b in range(n_seq):
            mods = (bi_ref[b, 0:1, :], bi_ref[b, 1:2, :], bu_ref[b, 2:3, :], bu_ref[b, 0:1, :], bu_ref[b, 1:2, :], bf_ref[b, 0:1, :])
            for k, row in enumerate(mods):
                o_ref[b:b + 1, k * D:(k + 1) * D] = row
        o = 0
        for row in (gi_ref[0:1, :], gu_ref[0:1, :], gf_ref[0:1, :], go_ref[0:1, :], go_ref[1:2, :]):
            o_ref[2:3, o:o + row.shape[1]] = row
            o += row.shape[1]
        ds = ds_ref[...]
        eye = lax.broadcasted_iota(jnp.int32, ds.shape, 0) == lax.broadcasted_iota(jnp.int32, ds.shape, 1)
        o_ref[2:3, o:o + 128] = jnp.sum(jnp.where(eye, ds, 0.0), axis=0, keepdims=True)
        o_ref[2:3, o + 128:o + 256] = gf_ref[1:2, 0:128]
        o_ref[3:4, 0:F] = cs_ref[0:1, :]
        o_ref[4:5, 0:rp_ref.shape[1]] = rp_ref[...]
        o_ref[5:8, 0:F] = cw_ref[1:4, :]

    return pl.pallas_call(body, name="pack_slab", out_shape=jax.ShapeDtypeStruct((8, PACK_W), F32),
                          compiler_params=pltpu.CompilerParams(vmem_limit_bytes=VMEM_LIMIT))(*raw, drpb)


def kernel(x, c, w_ada, b_ada, g_attn, w_in, na_rpb, sw_sink, g_na_out, g_sw_out, w_out, g_ffn, w_up, conv_w, conv_b, w_down, g_final, loss_target, m_w_ada, m_b_ada, m_g_attn, m_w_in, m_na_rpb, m_sw_sink, m_g_na_out, m_g_sw_out, m_w_out, m_g_ffn, m_w_up, m_conv_w, m_conv_b, m_w_down, m_g_final, v_w_ada, v_b_ada, v_g_attn, v_w_in, v_na_rpb, v_sw_sink, v_g_na_out, v_g_sw_out, v_w_out, v_g_ffn, v_w_up, v_conv_w, v_conv_b, v_w_down, v_g_final):
    B, S, D = x.shape
    me = 4 * lax.axis_index("x") + 2 * lax.axis_index("y") + lax.axis_index("c")
    ada_c = w_ada.shape[2]
    F_l = conv_w.shape[2]

    cw_l = jnp.pad(conv_w[0], ((0, 8 - conv_w.shape[1]), (0, 0)))
    c_l = jnp.pad(c, ((0, 8 - B), (0, 0)))
    tr = {"w_in", "w_up"}
    w_in_t = jnp.transpose(w_in[0])
    shards_f = dict(w_out=w_out[0], w_up=jnp.transpose(w_up[0]), w_down=w_down[0])

    b_ada_l = lax.dynamic_slice(b_ada, (0, me * ada_c), (1, ada_c))
    slabs, mod_all, w_in_all, casts, bias = _ada_fwd(jnp.concatenate([c_l, cw_l], axis=1), w_ada[0], b_ada_l, w_in_t, B,
                                                     list(shards_f.values()), _na_bias_rev(na_rpb[0]))
    shards = dict(zip(shards_f, casts))
    c_all = slabs[:, :, :D].reshape(N_DEV * 8, D)
    conv_w_f = jnp.transpose(slabs[:, :3, D:], (1, 0, 2)).reshape(3, N_DEV * F_l)
    mod_mine = lax.dynamic_slice(mod_all, (0, me * B, 0), (N_DEV, B, ada_c))
    mod = jnp.transpose(mod_mine, (1, 0, 2)).reshape(B, N_DEV * ada_c)
    w_in_f = w_in_all.reshape(N_DEV * w_in_t.shape[0], D)

    grad_x, dw, dmod, small = _local_step(x, mod, g_attn, w_in_f, bias, sw_sink, g_na_out, g_sw_out, shards["w_out"], g_ffn,
                                          shards["w_up"], conv_w_f, conv_b, shards["w_down"], g_final.reshape(1, D), loss_target,
                                          sharded=True)
    g8_in = _by_device(dw["w_in"])
    drpb, got = _na_bias_grad(small["dbt"], [_swap_task(g8_in)])
    send_in, own_in = _chip_sums(g8_in, got[0][0])

    slab = _pack_slab(small["raw"], drpb.reshape(1, -1))
    weights = dict(w_ada=w_ada, b_ada=b_ada, g_attn=g_attn, w_in=w_in, na_rpb=na_rpb, sw_sink=sw_sink, g_na_out=g_na_out,
                   g_sw_out=g_sw_out, w_out=w_out, g_ffn=g_ffn, w_up=w_up, conv_w=conv_w, conv_b=conv_b, w_down=w_down, g_final=g_final)
    ms = dict(w_ada=m_w_ada, b_ada=m_b_ada, g_attn=m_g_attn, w_in=m_w_in, na_rpb=m_na_rpb, sw_sink=m_sw_sink, g_na_out=m_g_na_out,
              g_sw_out=m_g_sw_out, w_out=m_w_out, g_ffn=m_g_ffn, w_up=m_w_up, conv_w=m_conv_w, conv_b=m_conv_b, w_down=m_w_down, g_final=m_g_final)
    vs = dict(w_ada=v_w_ada, b_ada=v_b_ada, g_attn=v_g_attn, w_in=v_w_in, na_rpb=v_na_rpb, sw_sink=v_sw_sink, g_na_out=v_g_na_out,
              g_sw_out=v_g_sw_out, w_out=v_w_out, g_ffn=v_g_ffn, w_up=v_w_up, conv_w=v_conv_w, conv_b=v_conv_b, w_down=v_w_down, g_final=v_g_final)
    names = list(weights)
    grads, deltas, new_m, new_v = {}, {}, {}, {}
    flat = lambda t: t.reshape(1, -1)

    def shard2d(nm):
        if nm in tr:
            return (lambda t: jnp.transpose(t[0])), (lambda t: jnp.transpose(t)[None])
        if nm == "conv_w":
            return (lambda t: jnp.transpose(t, (1, 0, 2))), (lambda t: jnp.transpose(t, (1, 0, 2)))
        return (lambda t: t[0]), (lambda t: t[None])

    def finish_sum(nm, own, recvb, done=0):
        r, back = shard2d(nm)
        g2, d_, m_, v_ = _sum_adamw(own, recvb, r(weights[nm]), r(ms[nm]), r(vs[nm]), "adamw_" + nm, done)
        grads[nm], deltas[nm], new_m[nm], new_v[nm] = back(g2), back(d_), back(m_), back(v_)

    own_up, recv_up = dw["w_up"]
    n_up = _sum_adamw_steps(own_up.shape[0])
    part_up, got = _sum_first(own_up, recv_up, "sum_first_w_up", [_exchange_task(send_in), _gather_task(slab, n_up - 1)])
    packs = got[1][0]
    finish_sum("w_up", part_up, recv_up, done=2)
    finish_sum("w_down", *dw["w_down"])
    finish_sum("w_in", own_in, got[0][0])
    finish_sum("w_out", *dw["w_out"])

    where = dict(b_ada=((0, 1), 0), g_attn=((2,), 0), g_ffn=((2,), D), g_final=((2,), 2 * D), g_na_out=((2,), 3 * D),
                 g_sw_out=((2,), 3 * D + NA_WIDTH), sw_sink=((2,), 3 * D + NA_WIDTH + SW_WIDTH), conv_b=((3,), 0), na_rpb=((4,), 0))

    def small_view(n):
        if n == "na_rpb":
            return (lambda t: jnp.transpose(t[0], (1, 0, 2))), (lambda t: jnp.transpose(t, (1, 0, 2))[None])
        return flat, (lambda t: t.reshape(weights[n].shape))

    tot, loss, small_out = _pack_sum_adamw(
        packs.reshape(N_DEV * 8, PACK_W),
        [tuple(small_view(n)[0](t[n]) for t in (weights, ms, vs)) + where[n] for n in where],
        (2, 3 * D + NA_WIDTH + SW_WIDTH + 128))
    for n, res in zip(where, small_out):
        grads[n], deltas[n], new_m[n], new_v[n] = [small_view(n)[1](t) for t in res]
    loss = loss.reshape(())

    dmod_cols = lax.dynamic_slice(packs.reshape(N_DEV * 8, PACK_W), (0, me * ada_c), (N_DEV * 8, ada_c))
    for nm, g2 in (("w_ada", _ada_bwd(c_all, dmod_cols)), ("conv_w", lax.dynamic_slice(tot, (5, me * F_l), (3, F_l))[:, None])):
        r, back = shard2d(nm)
        d_, m_, v_ = _adamw(r(weights[nm]), g2, r(ms[nm]), r(vs[nm]), "adamw_" + nm)
        grads[nm], deltas[nm], new_m[nm], new_v[nm] = back(g2), back(d_), back(m_), back(v_)
    return (loss, grad_x, *[grads[n] for n in names], *[deltas[n] for n in names], *[new_m[n] for n in names],
            *[new_v[n] for n in names])
```

```python
import functools

import numpy as np
import jax
import jax.numpy as jnp
from jax import lax
from jax.experimental import pallas as pl
from jax.experimental.pallas import tpu as pltpu

F32, BF16 = jnp.float32, jnp.bfloat16
MESH_ID = pl.DeviceIdType.MESH
N_DEV = 8

HEAD_DIM = 64
NA_HEADS = 8
SW_HEADS = 8
SW_KV_HEADS = 2
SW_GROUP = SW_HEADS // SW_KV_HEADS
NA_WIDTH = NA_HEADS * HEAD_DIM
SW_WIDTH = SW_HEADS * HEAD_DIM
SW_KV_WIDTH = SW_KV_HEADS * HEAD_DIM
ROPE_WIDTH = SW_WIDTH + SW_KV_WIDTH
IN_WIDTH = 3 * NA_WIDTH + SW_WIDTH + 2 * SW_KV_WIDTH
ROPE_LO = 3 * NA_WIDTH
GRID_W = 64
NA_ROWS_MAX = 8
NA_COLS = 16
N_DR = 2 * NA_ROWS_MAX - 1
N_DC = 2 * NA_COLS - 1
SW_WINDOW = 128
SW_BLOCK = 128
ROPE_THETA = 10000.0
EPS = 1e-6
NEG = -1e30
Q_SCALE = HEAD_DIM ** -0.5

ADAM_LR = 0.001
ADAM_B1 = 0.9
ADAM_B2 = 0.999
ADAM_EPS = 1e-08
ADAM_WD = 0.01
ADAM_STEP = 10

TOKEN_TILE = 256
WIDE_TILE = 512
VMEM_LIMIT = 56 * 1024 * 1024

PACK_W = 6144


def _nn(a, b):
    return jnp.dot(a, b, preferred_element_type=F32)


def _nt(a, b):
    return lax.dot_general(a, b, (((1,), (1,)), ((), ())), preferred_element_type=F32)


def _tn(a, b):
    return lax.dot_general(a, b, (((0,), (0,)), ((), ())), preferred_element_type=F32)


def _rms(x):
    r = lax.rsqrt(jnp.mean(x * x, axis=-1, keepdims=True) + EPS)
    return x * r, r


def _rms_bwd(xn, r, gy):
    return r * (gy - xn * jnp.mean(xn * gy, axis=-1, keepdims=True))


def _params(*sem):
    return pltpu.CompilerParams(dimension_semantics=sem, vmem_limit_bytes=VMEM_LIMIT)


def _full(shape):
    n = len(shape)
    return pl.BlockSpec(shape, lambda *_: (0,) * n)


def _mesh_pos():
    return lax.axis_index("x"), lax.axis_index("y"), lax.axis_index("c")


def _row_chunk(r):
    for rc in (128, 64, 32, 16):
        if r % rc == 0:
            return rc
    raise ValueError(f"rows {r} not a multiple of 16")


class _Task:
    def __init__(self, inputs, out_shapes, sems, start, finish, mid=None, mid_step=None, alias=None):
        self.inputs, self.out_shapes, self.sems = list(inputs), list(out_shapes), list(sems)
        self.start, self.finish, self.mid, self.mid_step = start, finish, mid, mid_step
        self.alias = alias


def _hosted_call(body, name, grid, in_specs, out_specs, out_shape, operands, tasks, scratch_shapes=()):
    n_in, n_out, n_scr = len(in_specs), len(out_specs), len(scratch_shapes)
    t_in = [len(t.inputs) for t in tasks]
    t_out = [len(t.out_shapes) for t in tasks]
    t_sem = [len(t.sems) for t in tasks]
    n_steps = int(np.prod(grid))

    def wrapped(*refs):
        ins, rest = refs[:n_in], refs[n_in:]
        task_ins, rest = rest[:sum(t_in)], rest[sum(t_in):]
        outs, rest = rest[:n_out], rest[n_out:]
        task_outs, rest = rest[:sum(t_out)], rest[sum(t_out):]
        scr, task_sems = rest[:n_scr], rest[n_scr:]
        step = pl.program_id(0)
        for ax in range(1, len(grid)):
            step = step * grid[ax] + pl.program_id(ax)
        parts = []
        oi = oo = os_ = 0
        for t, a, b, c in zip(tasks, t_in, t_out, t_sem):
            parts.append((t, task_ins[oi:oi + a], task_outs[oo:oo + b], task_sems[os_:os_ + c]))
            oi, oo, os_ = oi + a, oo + b, os_ + c
        for t, ti, to, ts in parts:
            pl.when(step == 0)(functools.partial(t.start, ti, to, ts))
            if t.mid is not None:
                pl.when(step == t.mid_step)(functools.partial(t.mid, ti, to, ts))
        body(*ins, *outs, *scr)
        for t, ti, to, ts in parts:
            pl.when(step == n_steps - 1)(functools.partial(t.finish, ti, to, ts))

    hbm = pl.BlockSpec(memory_space=pl.ANY)
    aliases, oi, oo = {}, n_in, n_out
    for t, a, b in zip(tasks, t_in, t_out):
        if t.alias is not None:
            aliases[oi + t.alias[0]] = oo + t.alias[1]
        oi, oo = oi + a, oo + b
    res = pl.pallas_call(
        wrapped, name=name, grid=grid,
        in_specs=list(in_specs) + [hbm] * sum(t_in),
        out_specs=list(out_specs) + [hbm] * sum(t_out),
        out_shape=list(out_shape) + [s for t in tasks for s in t.out_shapes],
        scratch_shapes=list(scratch_shapes) + [s for t in tasks for s in t.sems],
        input_output_aliases=aliases,
        compiler_params=_params(*(["arbitrary"] * len(grid))),
    )(*operands, *[a for t in tasks for a in t.inputs])
    own, extra = res[:n_out], res[n_out:]
    per_task, o = [], 0
    for b in t_out:
        per_task.append(extra[o:o + b])
        o += b
    return own, per_task


def _gather_task(shard, mid_step, rows=None, into=None):
    lo, n = (0, shard.shape[0]) if rows is None else rows

    def parts(ins, outs, sems):
        x_ref, out_ref, (send_sems, recv_sems, local_sem) = ins[0], outs[0], sems
        x_, y_, c_ = _mesh_pos()
        me, sibling = (x_, y_, c_), (x_, y_, 1 - c_)
        chips = [(1 - x_, y_), (x_, 1 - y_), (1 - x_, 1 - y_)]
        x_ref = x_ref.at[pl.ds(lo, n)]

        def rows(px, py, pc):
            return out_ref.at[4 * px + 2 * py + pc, pl.ds(lo, n)]

        def copy(k, block, to, src=None):
            return pltpu.make_async_remote_copy(
                src_ref=rows(*block) if src is None else src, dst_ref=rows(*block),
                send_sem=send_sems.at[k], recv_sem=recv_sems.at[k], device_id=to, device_id_type=MESH_ID)

        return dict(
            mine=lambda: pltpu.make_async_copy(x_ref, rows(*me), local_sem),
            first=lambda: [copy(0, me, sibling, src=x_ref)] + [copy(1 + j, me, (*chip, c_), src=x_ref) for j, chip in enumerate(chips)],
            passed=lambda: [copy(4 + j, (*chip, c_), sibling) for j, chip in enumerate(chips)],
            landed=lambda: [copy(1 + j, (*chip, c_), me) for j, chip in enumerate(chips)],
            last=lambda: [copy(0, sibling, me)] + [copy(4 + j, (*chip, 1 - c_), me) for j, chip in enumerate(chips)])

    def start(ins, outs, sems):
        p = parts(ins, outs, sems)
        p["mine"]().start()
        for cp in p["first"]():
            cp.start()

    def mid(ins, outs, sems):
        p = parts(ins, outs, sems)
        for cp, fw in zip(p["landed"](), p["passed"]()):
            cp.wait_recv()
            fw.start()

    def finish(ins, outs, sems):
        p = parts(ins, outs, sems)
        for cp in p["last"]():
            cp.wait_recv()
        for cp in p["first"]() + p["passed"]():
            cp.wait_send()
        p["mine"]().wait()

    return _Task([shard] if into is None else [shard, into], [jax.ShapeDtypeStruct((N_DEV,) + shard.shape, shard.dtype)],
                 [pltpu.SemaphoreType.DMA((7,)), pltpu.SemaphoreType.DMA((7,)), pltpu.SemaphoreType.DMA],
                 start, finish, mid, mid_step, alias=None if into is None else (1, 0))


def _swap_task(g8):
    _, R, C = g8.shape

    def copies(ins, outs, sems):
        (g_ref,), (recv_ref,), (ss, rs) = ins, outs, sems
        x_, y_, c_ = _mesh_pos()
        return [pltpu.make_async_remote_copy(src_ref=g_ref.at[2 * k + (1 - c_)], dst_ref=recv_ref.at[k], send_sem=ss.at[k],
                                             recv_sem=rs.at[k], device_id=(x_, y_, 1 - c_), device_id_type=MESH_ID)
                for k in range(4)]

    def start(ins, outs, sems):
        for cp in copies(ins, outs, sems):
            cp.start()

    def finish(ins, outs, sems):
        cps = copies(ins, outs, sems)
        for cp in cps:
            cp.wait_recv()
        for cp in cps:
            cp.wait_send()

    return _Task([g8], [jax.ShapeDtypeStruct((4, R, C), g8.dtype)],
                 [pltpu.SemaphoreType.DMA((4,)), pltpu.SemaphoreType.DMA((4,))], start, finish)


def _chip_sums(g8, recva):
    _, R, C = g8.shape
    rt = _sum_adamw_rows(R)
    rc = _row_chunk(rt)

    def body(core_ref, g_ref, a_ref, send_ref, own_ref):
        x_, y_, _ = _mesh_pos()
        chips = [(1 - x_, y_), (x_, 1 - y_), (1 - x_, 1 - y_), (x_, y_)]

        def chunk(i, carry):
            rows = pl.ds(pl.multiple_of(i * rc, rc), rc)
            for j, (tx, ty) in enumerate(chips):
                k = 2 * tx + ty
                s = g_ref[k, rows, :].astype(F32) + a_ref[k, rows, :].astype(F32)
                if j < 3:
                    send_ref[j, rows, :] = s.astype(BF16)
                else:
                    own_ref[rows, :] = s
            return carry

        lax.fori_loop(0, rt // rc, chunk, 0)

    core = lax.axis_index("c").astype(jnp.int32).reshape(1)
    return pl.pallas_call(
        body, name="chip_sums",
        grid_spec=pltpu.PrefetchScalarGridSpec(
            num_scalar_prefetch=1, grid=(R // rt,),
            in_specs=[pl.BlockSpec((4, None, rt, C), lambda i, c: (0, c[0], i, 0)), pl.BlockSpec((4, rt, C), lambda i, c: (0, i, 0))],
            out_specs=[pl.BlockSpec((3, rt, C), lambda i, c: (0, i, 0)), pl.BlockSpec((rt, C), lambda i, c: (i, 0))]),
        out_shape=[jax.ShapeDtypeStruct((3, R, C), BF16), jax.ShapeDtypeStruct((R, C), F32)],
        compiler_params=_params("parallel"),
    )(core, g8.reshape(4, 2, R, C), recva)


def _exchange_task(sendb):
    def copies(ins, outs, sems):
        (s_ref,), (recv_ref,), (ss, rs) = ins, outs, sems
        x_, y_, c_ = _mesh_pos()
        flips = [(1 - x_, y_), (x_, 1 - y_), (1 - x_, 1 - y_)]
        return [pltpu.make_async_remote_copy(src_ref=s_ref.at[j], dst_ref=recv_ref.at[j], send_sem=ss.at[j], recv_sem=rs.at[j],
                                             device_id=(tx, ty, c_), device_id_type=MESH_ID) for j, (tx, ty) in enumerate(flips)]

    def start(ins, outs, sems):
        for cp in copies(ins, outs, sems):
            cp.start()

    def finish(ins, outs, sems):
        cps = copies(ins, outs, sems)
        for cp in cps:
            cp.wait_recv()
        for cp in cps:
            cp.wait_send()

    return _Task([sendb], [jax.ShapeDtypeStruct(sendb.shape, sendb.dtype)],
                 [pltpu.SemaphoreType.DMA((3,)), pltpu.SemaphoreType.DMA((3,))], start, finish)


def _silu(v):
    return v * (1.0 / (1.0 + jnp.exp(-v)))


def _ada_fwd(c, taps, w_ada_l, b_ada, w_in_shard, casts, rpb_rev):
    n_seq = c.shape[0]
    D, cols = w_ada_l.shape
    W = D + taps.shape[1]
    n_rows = N_DEV * n_seq
    n_c = len(casts)
    t_c = _gather_task(jax.ShapeDtypeStruct((8, W), F32), 0)
    t_w = _gather_task(jax.ShapeDtypeStruct(w_in_shard.shape, BF16), 0)
    t_m = _gather_task(jax.ShapeDtypeStruct((n_rows, cols), F32), 0)

    def body(c_ref, taps_ref, w_ref, b_ref, ws_ref, rev_ref, *refs):
        cast_in, refs = refs[:n_c], refs[n_c:]
        (slabs_ref, call_ref, mod_ref, win_ref), refs = refs[:4], refs[4:]
        cast_out, bias_ref, refs = refs[:n_c], refs[n_c], refs[n_c + 1:]
        (slab_vm, c_vm, m_vm, ws_f, ws_b), refs = refs[:5], refs[5:]
        stage_f, stage_b, refs = refs[:n_c], refs[n_c:2 * n_c], refs[2 * n_c:]
        copy_sem, in_sems, out_sems, sems = refs[0], refs[1], refs[2], refs[3:]
        sc, sw, sm = sems[0:3], sems[3:6], sems[6:9]
        x_, y_, c_ = _mesh_pos()
        slab_vm[...] = jnp.zeros_like(slab_vm)
        slab_vm[0:n_seq, 0:D] = c_ref[...]
        slab_vm[0:taps.shape[0], D:W] = taps_ref[...]
        t_c.start((slab_vm,), (slabs_ref,), sc)
        cp = pltpu.make_async_copy(ws_ref, ws_f, copy_sem)
        cp.start()
        cp.wait()
        ws_b[...] = ws_f[...].astype(BF16)
        t_w.start((ws_b,), (win_ref,), sw)
        loads = [pltpu.make_async_copy(cast_in[j], stage_f[j], in_sems.at[j]) for j in range(n_c)]
        for ld in loads:
            ld.start()
        t_c.mid((slab_vm,), (slabs_ref,), sc)
        t_c.finish((slab_vm,), (slabs_ref,), sc)
        cp = pltpu.make_async_copy(slabs_ref, c_vm, copy_sem)
        cp.start()
        cp.wait()
        c_all = c_vm[:, :, 0:D].reshape(N_DEV * 8, D)
        call_ref[...] = c_all
        b_mine = b_ref[:, pl.ds(pl.multiple_of((4 * x_ + 2 * y_ + c_) * cols, 128), cols)]
        m64 = jnp.dot(_silu(c_all), w_ref[...], precision=lax.Precision.HIGHEST, preferred_element_type=F32) + b_mine
        r = lax.broadcasted_iota(jnp.int32, (n_rows, N_DEV * 8), 0)
        c = lax.broadcasted_iota(jnp.int32, (n_rows, N_DEV * 8), 1)
        pick = jnp.where(c == 8 * (r // n_seq) + r % n_seq, 1.0, 0.0)
        m_vm[...] = jnp.dot(pick, m64, precision=lax.Precision.HIGHEST, preferred_element_type=F32)
        t_m.start((m_vm,), (mod_ref,), sm)
        stores = [pltpu.make_async_copy(stage_b[j], cast_out[j], out_sems.at[j]) for j in range(n_c)]
        for j in range(n_c):
            loads[j].wait()
            stage_b[j][...] = stage_f[j][...].astype(BF16)
            stores[j].start()
        for p in range(NA_PAIRS):
            _na_bias_rows(rev_ref, bias_ref, p)
        t_w.mid((ws_b,), (win_ref,), sw)
        t_m.mid((m_vm,), (mod_ref,), sm)
        t_m.finish((m_vm,), (mod_ref,), sm)
        t_w.finish((ws_b,), (win_ref,), sw)
        for st in stores:
            st.wait()

    hbm, vm = pl.BlockSpec(memory_space=pl.ANY), pl.BlockSpec(memory_space=pltpu.VMEM)
    res = pl.pallas_call(
        body, name="ada_fwd", in_specs=[vm, vm, vm, vm, hbm, vm] + [hbm] * n_c,
        out_specs=[hbm, vm, hbm, hbm] + [hbm] * n_c + [vm],
        out_shape=t_c.out_shapes + [jax.ShapeDtypeStruct((N_DEV * 8, D), F32)] + t_m.out_shapes + t_w.out_shapes
        + [jax.ShapeDtypeStruct(a.shape, BF16) for a in casts] + [jax.ShapeDtypeStruct((NA_PAIRS, N_DR * GRID_W, 128), F32)],
        scratch_shapes=[pltpu.VMEM((8, W), F32), pltpu.VMEM((N_DEV, 8, W), F32), pltpu.VMEM((n_rows, cols), F32),
                        pltpu.VMEM(w_in_shard.shape, F32), pltpu.VMEM(w_in_shard.shape, BF16)]
        + [pltpu.VMEM(a.shape, F32) for a in casts] + [pltpu.VMEM(a.shape, BF16) for a in casts]
        + [pltpu.SemaphoreType.DMA, pltpu.SemaphoreType.DMA((n_c,)), pltpu.SemaphoreType.DMA((n_c,))]
        + t_c.sems + t_w.sems + t_m.sems,
        compiler_params=pltpu.CompilerParams(vmem_limit_bytes=VMEM_LIMIT),
    )(c, taps, w_ada_l, b_ada, w_in_shard, rpb_rev, *casts)
    return res[0], res[1], res[2], res[3], res[4:4 + n_c], res[4 + n_c]


def _ada_bwd(c_all, dmod_cols):
    def body(c_ref, d_ref, o_ref):
        o_ref[...] = lax.dot_general(_silu(c_ref[...]), d_ref[...], (((0,), (0,)), ((), ())),
                                     precision=lax.Precision.HIGHEST, preferred_element_type=F32)
    return pl.pallas_call(body, name="ada_bwd", out_shape=jax.ShapeDtypeStruct((c_all.shape[1], dmod_cols.shape[1]), F32),
                          compiler_params=pltpu.CompilerParams(vmem_limit_bytes=VMEM_LIMIT))(c_all, dmod_cols)


NA_PAIRS = NA_HEADS // 2
N_DR_PAD = 16


def _na_bias_rev(na_rpb):
    rev = jnp.pad(jnp.flip(na_rpb, axis=2), ((0, 0), (0, N_DR_PAD - N_DR), (0, GRID_W - N_DC)))
    return jnp.transpose(rev.reshape(NA_PAIRS, 2, N_DR_PAD, GRID_W), (0, 2, 1, 3)).reshape(NA_PAIRS, N_DR_PAD, 128)


def _na_bias_rows(r_ref, o_ref, p):
    k = lax.broadcasted_iota(jnp.int32, (GRID_W, 128), 0)
    lane = lax.broadcasted_iota(jnp.int32, (GRID_W, 128), 1)
    q = lane % GRID_W
    cs = jnp.clip(q - NA_COLS // 2, 0, GRID_W - NA_COLS)
    ok = (k >= cs) & (k < cs + NA_COLS)
    left = lane < GRID_W
    for dr in range(N_DR):
        row = jnp.broadcast_to(r_ref[p, dr:dr + 1, :], (GRID_W, 128))
        r0 = jnp.where(left, row, 0.0)
        r1 = jnp.where(left, pltpu.roll(row, GRID_W, axis=1), 0.0)
        y0 = pltpu.roll(r0, 128 - (NA_COLS - 1), axis=1, stride=1, stride_axis=0)
        y1 = pltpu.roll(r1, GRID_W - (NA_COLS - 1), axis=1, stride=1, stride_axis=0)
        o_ref[p, dr * GRID_W:(dr + 1) * GRID_W, :] = jnp.where(ok, jnp.where(left, y0, y1), NEG)


def _na_bias_grad(db, tasks=()):
    a = np.arange(128)
    flip = jnp.asarray(((a[:, None] // GRID_W == a[None, :] // GRID_W)
                        & (a[:, None] % GRID_W + a[None, :] % GRID_W == GRID_W - 1)).astype(np.float32))

    def body(d_ref, j_ref, o_ref):
        o_ref[...] = jnp.zeros_like(o_ref)
        for dr in range(N_DR):
            t = jnp.dot(d_ref[0, dr * GRID_W:(dr + 1) * GRID_W, :], j_ref[...], precision=lax.Precision.HIGHEST, preferred_element_type=F32)
            t = pltpu.roll(t, GRID_W + NA_COLS, axis=1, stride=1, stride_axis=0)
            o_ref[0, dr:dr + 1, :] = jnp.sum(t, axis=0, keepdims=True)

    (rows,), got = _hosted_call(
        body, "rpb_reduce", (NA_PAIRS,),
        [pl.BlockSpec((1, N_DR * GRID_W, 128), lambda p: (p, 0, 0)), _full((128, 128))],
        [pl.BlockSpec((1, N_DR_PAD, 128), lambda p: (p, 0, 0))],
        [jax.ShapeDtypeStruct((NA_PAIRS, N_DR_PAD, 128), F32)], (db, flip), tasks)
    g = rows.reshape(NA_PAIRS, N_DR_PAD, 2, GRID_W)[:, :N_DR, :, :N_DC]
    return jnp.transpose(g, (0, 2, 1, 3)).reshape(-1), got


def _rope_tables(S):
    half = HEAD_DIM // 2
    inv = np.float32(ROPE_THETA) ** (-np.arange(half, dtype=np.float32) / np.float32(half))
    ang = np.arange(S).astype(np.float32)[:, None] * inv[None, :]
    cos, sin = np.cos(ang).astype(np.float32), np.sin(ang).astype(np.float32)
    return jnp.asarray(np.tile(np.concatenate([cos, cos], axis=1), (1, 2))), jnp.asarray(np.tile(np.concatenate([-sin, sin], axis=1), (1, 2)))


def _rope_spec(tps, tm=TOKEN_TILE):
    return pl.BlockSpec((tm, 2 * HEAD_DIM), lambda i: (i % tps, 0))


def _rot_half(t):
    w = t.shape[1]
    lane = lax.broadcasted_iota(jnp.int32, t.shape, 1)
    return jnp.where((lane % HEAD_DIM) < HEAD_DIM // 2, pltpu.roll(t, w - HEAD_DIM // 2, axis=1),
                     pltpu.roll(t, HEAD_DIM // 2, axis=1))


def _tok_spec(w, tm=TOKEN_TILE):
    return pl.BlockSpec((tm, w), lambda i: (i, 0))


def _mod_spec(tps, d):
    return pl.BlockSpec((1, 6, d), lambda i: (i // tps, 0, 0))


def _bstat_spec(tps, w):
    return pl.BlockSpec((1, 8, w), lambda i: (i // tps, 0, 0))


def _attn_in(x2d, mod3, g_attn, w_in, cos_t, sin_t, S, tasks=(), tm=WIDE_TILE):
    T, D = x2d.shape
    tps = S // tm

    def body(x_ref, mod_ref, g_ref, w_ref, cos_ref, sin_ref, h_ref, qkv_ref):
        xn, _ = _rms(x_ref[...])
        h = (xn * g_ref[...]) * (1.0 + mod_ref[0, 1:2, :]) + mod_ref[0, 0:1, :]
        hb = h.astype(BF16)
        h_ref[...] = hb
        proj = _nt(hb, w_ref[...])
        rb = proj[:, ROPE_LO:ROPE_LO + ROPE_WIDTH]
        reps = (1, ROPE_WIDTH // (2 * HEAD_DIM))
        rb = rb * jnp.tile(cos_ref[...], reps) + _rot_half(rb) * jnp.tile(sin_ref[...], reps)
        qkv_ref[:, 0:NA_WIDTH] = (proj[:, 0:NA_WIDTH] * Q_SCALE).astype(BF16)
        qkv_ref[:, NA_WIDTH:ROPE_LO] = proj[:, NA_WIDTH:ROPE_LO].astype(BF16)
        qkv_ref[:, ROPE_LO:ROPE_LO + SW_WIDTH] = (rb[:, 0:SW_WIDTH] * Q_SCALE).astype(BF16)
        qkv_ref[:, ROPE_LO + SW_WIDTH:ROPE_LO + ROPE_WIDTH] = rb[:, SW_WIDTH:].astype(BF16)
        qkv_ref[:, ROPE_LO + ROPE_WIDTH:] = proj[:, ROPE_LO + ROPE_WIDTH:].astype(BF16)

    return _hosted_call(
        body, "attn_in", (T // tm,),
        [_tok_spec(D, tm), _mod_spec(tps, D), _full((1, D)), _full(w_in.shape), _rope_spec(tps, tm), _rope_spec(tps, tm)],
        [_tok_spec(D, tm), _tok_spec(IN_WIDTH, tm)],
        [jax.ShapeDtypeStruct((T, D), BF16), jax.ShapeDtypeStruct((T, IN_WIDTH), BF16)],
        (x2d, mod3, g_attn, w_in, cos_t, sin_t), tasks)


def _attn_out(oa, ob, x2d, mod3, g_na, g_sw, w_out, S, tasks=(), tm=WIDE_TILE):
    T, D = x2d.shape
    tps = S // tm

    def body(oa_ref, ob_ref, x_ref, mod_ref, gna_ref, gsw_ref, w_ref, mixin_ref, mix_ref, x1_ref):
        oan, _ = _rms(oa_ref[...])
        obn, _ = _rms(ob_ref[...])
        mixin = jnp.concatenate([oan * gna_ref[...], obn * gsw_ref[...]], axis=1).astype(BF16)
        mixin_ref[...] = mixin
        mix = _nn(mixin, w_ref[...])
        mix_ref[...] = mix
        x1_ref[...] = x_ref[...] + mod_ref[0, 2:3, :] * mix

    return _hosted_call(
        body, "attn_out", (T // tm,),
        [_tok_spec(NA_WIDTH, tm), _tok_spec(SW_WIDTH, tm), _tok_spec(D, tm), _mod_spec(tps, D),
         _full((1, NA_WIDTH)), _full((1, SW_WIDTH)), _full(w_out.shape)],
        [_tok_spec(NA_WIDTH + SW_WIDTH, tm), _tok_spec(D, tm), _tok_spec(D, tm)],
        [jax.ShapeDtypeStruct((T, NA_WIDTH + SW_WIDTH), BF16), jax.ShapeDtypeStruct((T, D), F32), jax.ShapeDtypeStruct((T, D), F32)],
        (oa, ob, x2d, mod3, g_na, g_sw, w_out), tasks)


def _ffn_up(x1, mod3, g_ffn, w_up, S, tasks=(), tm=WIDE_TILE):
    T, D = x1.shape
    F = w_up.shape[0] // 2
    tps = S // tm

    def body(x1_ref, mod_ref, g_ref, w_ref, h2_ref, val_ref, gt_ref):
        xn, _ = _rms(x1_ref[...])
        h2 = ((xn * g_ref[...]) * (1.0 + mod_ref[0, 4:5, :]) + mod_ref[0, 3:4, :]).astype(BF16)
        h2_ref[...] = h2
        u = _nt(h2, w_ref[...])
        val_ref[...] = u[:, :F].astype(BF16)
        gt_ref[...] = u[:, F:].astype(BF16)

    return _hosted_call(
        body, "ffn_up", (T // tm,), [_tok_spec(D, tm), _mod_spec(tps, D), _full((1, D)), _full(w_up.shape)],
        [_tok_spec(D, tm), _tok_spec(F, tm), _tok_spec(F, tm)],
        [jax.ShapeDtypeStruct((T, D), BF16), jax.ShapeDtypeStruct((T, F), BF16), jax.ShapeDtypeStruct((T, F), BF16)],
        (x1, mod3, g_ffn, w_up), tasks)


def _halo_specs(T, tps, w):
    per = TOKEN_TILE // 8
    prev = pl.BlockSpec((8, w), lambda i: (jnp.maximum(i * per - 1, 0), 0))
    nxt = pl.BlockSpec((8, w), lambda i: (jnp.minimum((i + 1) * per, T // 8 - 1), 0))
    return prev, nxt


def _seq_shifts(cur, before, after, ti, tps):
    tm = cur.shape[0]
    row = lax.broadcasted_iota(jnp.int32, cur.shape, 0)
    before = jnp.where(ti > 0, before.astype(F32), 0.0)
    after = jnp.where(ti < tps - 1, after.astype(F32), 0.0)
    return jnp.where(row == 0, before, pltpu.roll(cur, 1, axis=0)), jnp.where(row == tm - 1, after, pltpu.roll(cur, tm - 1, axis=0))


def _ffn_down(gt, val, conv_w, conv_b, w_down, x1, mod3, g_final, target, B, S):
    T, D = x1.shape
    F = gt.shape[1]
    tps = S // TOKEN_TILE
    prev, nxt = _halo_specs(T, tps, F)

    def body(gt_ref, prev_ref, next_ref, val_ref, cw_ref, cb_ref, w_ref, x1_ref, mod_ref, gf_ref, tgt_ref,
             a_ref, act_ref, vd_ref, dx2_ref, df_ref, gstat_ref, bstat_ref):
        i = pl.program_id(0)
        g = gt_ref[...].astype(F32)
        gprev, gnext = _seq_shifts(g, prev_ref[7:8, :], next_ref[0:1, :], i % tps, tps)
        gc = gprev * cw_ref[0:1, :] + g * cw_ref[1:2, :] + gnext * cw_ref[2:3, :] + cb_ref[...]
        sig = 1.0 / (1.0 + jnp.exp(-gc))
        act = gc * sig
        val = val_ref[...].astype(F32)
        act_ref[...] = act.astype(BF16)
        vd_ref[...] = (val * (sig + act - act * sig)).astype(BF16)
        a = (act * val).astype(BF16)
        a_ref[...] = a
        f = _nn(a, w_ref[...])
        gate = mod_ref[0, 5:6, :]
        x2 = x1_ref[...] + gate * f
        xn, r = _rms(x2)
        err = xn * gf_ref[...] - tgt_ref[...]
        dy = err * (1.0 / D)
        dx2 = _rms_bwd(xn, r, dy * gf_ref[...])
        dx2_ref[...] = dx2
        df_ref[...] = (gate * dx2).astype(BF16)

        @pl.when(i == 0)
        def _():
            gstat_ref[...] = jnp.zeros_like(gstat_ref)

        @pl.when(i % tps == 0)
        def _():
            bstat_ref[...] = jnp.zeros_like(bstat_ref)

        gstat_ref[0:1, :] += jnp.sum(dy * xn, axis=0, keepdims=True)
        tile_loss = jnp.sum(jnp.sum(err * err, axis=1, keepdims=True), axis=0, keepdims=True) * (0.5 / D)
        gstat_ref[1:2, :] += jnp.broadcast_to(tile_loss, (1, D))
        bstat_ref[0, 0:1, :] += jnp.sum(dx2 * f, axis=0, keepdims=True)

    return pl.pallas_call(
        body, name="ffn_down", grid=(T // TOKEN_TILE,),
        in_specs=[_tok_spec(F), prev, nxt, _tok_spec(F), _full(conv_w.shape), _full((1, F)), _full(w_down.shape),
                  _tok_spec(D), _mod_spec(tps, D), _full((1, D)), _tok_spec(D)],
        out_specs=[_tok_spec(F), _tok_spec(F), _tok_spec(F), _tok_spec(D), _tok_spec(D), _full((8, D)), _bstat_spec(tps, D)],
        out_shape=[jax.ShapeDtypeStruct((T, F), BF16), jax.ShapeDtypeStruct((T, F), BF16), jax.ShapeDtypeStruct((T, F), BF16),
                   jax.ShapeDtypeStruct((T, D), F32), jax.ShapeDtypeStruct((T, D), BF16),
                   jax.ShapeDtypeStruct((8, D), F32), jax.ShapeDtypeStruct((B, 8, D), F32)],
        compiler_params=_params("arbitrary"),
    )(gt, gt, gt, val, conv_w, conv_b, w_down, x1, mod3, g_final, target)


def _ffn_down_bwd(df, w_down, act, vd, tasks=(), tm=WIDE_TILE):
    T, D = df.shape
    F = act.shape[1]

    def body(df_ref, w_ref, act_ref, vd_ref, dval_ref, dgc_ref, cstat_ref):
        da = _nt(df_ref[...], w_ref[...])
        dval_ref[...] = (da * act_ref[...].astype(F32)).astype(BF16)
        dgc = da * vd_ref[...].astype(F32)
        dgc_ref[...] = dgc.astype(BF16)

        @pl.when(pl.program_id(0) == 0)
        def _():
            cstat_ref[...] = jnp.zeros_like(cstat_ref)

        cstat_ref[0:1, :] += jnp.sum(dgc, axis=0, keepdims=True)

    return _hosted_call(
        body, "ffn_down_bwd", (T // tm,),
        [_tok_spec(D, tm), _full(w_down.shape), _tok_spec(F, tm), _tok_spec(F, tm)],
        [_tok_spec(F, tm), _tok_spec(F, tm), _full((8, F))],
        [jax.ShapeDtypeStruct((T, F), BF16), jax.ShapeDtypeStruct((T, F), BF16), jax.ShapeDtypeStruct((8, F), F32)],
        (df, w_down, act, vd), tasks)


def _ffn_up_bwd(dgc, dval, gt, conv_w, w_up, x1, mod3, g_ffn, dx2, mix, B, S, tasks=()):
    T, D = x1.shape
    F = dgc.shape[1]
    tps = S // TOKEN_TILE
    prev, nxt = _halo_specs(T, tps, F)

    def body(dgc_ref, prev_ref, next_ref, dval_ref, gt_ref, cw_ref, w_ref, x1_ref, mod_ref, g_ref, dx2_ref, mix_ref,
             du_ref, dx1_ref, dmix_ref, gstat_ref, bstat_ref, cstat_ref):
        i = pl.program_id(0)
        d = dgc_ref[...].astype(F32)
        dprev, dnext = _seq_shifts(d, prev_ref[7:8, :], next_ref[0:1, :], i % tps, tps)
        g = gt_ref[...].astype(F32)

        @pl.when(i == 0)
        def _():
            cstat_ref[...] = jnp.zeros_like(cstat_ref)

        cstat_ref[1:2, :] += jnp.sum(dnext * g, axis=0, keepdims=True)
        cstat_ref[2:3, :] += jnp.sum(d * g, axis=0, keepdims=True)
        cstat_ref[3:4, :] += jnp.sum(dprev * g, axis=0, keepdims=True)
        dgt = dnext * cw_ref[0:1, :] + d * cw_ref[1:2, :] + dprev * cw_ref[2:3, :]
        du = jnp.concatenate([dval_ref[...], dgt.astype(BF16)], axis=1)
        du_ref[...] = du
        dh2 = _nn(du, w_ref[...])
        xn, r = _rms(x1_ref[...])
        scale1 = 1.0 + mod_ref[0, 4:5, :]
        xg = xn * g_ref[...]
        dx1 = dx2_ref[...] + _rms_bwd(xn, r, dh2 * g_ref[...] * scale1)
        dx1_ref[...] = dx1
        dmix_ref[...] = (mod_ref[0, 2:3, :] * dx1).astype(BF16)

        @pl.when(i == 0)
        def _():
            gstat_ref[...] = jnp.zeros_like(gstat_ref)

        @pl.when(i % tps == 0)
        def _():
            bstat_ref[...] = jnp.zeros_like(bstat_ref)

        gstat_ref[0:1, :] += jnp.sum(dh2 * scale1 * xn, axis=0, keepdims=True)
        bstat_ref[0, 0:1, :] += jnp.sum(dh2, axis=0, keepdims=True)
        bstat_ref[0, 1:2, :] += jnp.sum(dh2 * xg, axis=0, keepdims=True)
        bstat_ref[0, 2:3, :] += jnp.sum(dx1 * mix_ref[...], axis=0, keepdims=True)

    return _hosted_call(
        body, "ffn_up_bwd", (T // TOKEN_TILE,),
        [_tok_spec(F), prev, nxt, _tok_spec(F), _tok_spec(F), _full(conv_w.shape), _full(w_up.shape), _tok_spec(D),
         _mod_spec(tps, D), _full((1, D)), _tok_spec(D), _tok_spec(D)],
        [_tok_spec(2 * F), _tok_spec(D), _tok_spec(D), _full((8, D)), _bstat_spec(tps, D), _full((8, F))],
        [jax.ShapeDtypeStruct((T, 2 * F), BF16), jax.ShapeDtypeStruct((T, D), F32), jax.ShapeDtypeStruct((T, D), BF16),
         jax.ShapeDtypeStruct((8, D), F32), jax.ShapeDtypeStruct((B, 8, D), F32), jax.ShapeDtypeStruct((8, F), F32)],
        (dgc, dgc, dgc, dval, gt, conv_w, w_up, x1, mod3, g_ffn, dx2, mix), tasks)


def _attn_out_bwd(dmix, w_out, oa, ob, g_na, g_sw, tasks=(), tm=WIDE_TILE):
    T, D = dmix.shape

    def body(dmix_ref, w_ref, oa_ref, ob_ref, gna_ref, gsw_ref, doa_ref, dob_ref, gstat_ref):
        dmixin = _nt(dmix_ref[...], w_ref[...])

        @pl.when(pl.program_id(0) == 0)
        def _():
            gstat_ref[...] = jnp.zeros_like(gstat_ref)

        for k, (o_ref, g_ref, do_ref) in enumerate(((oa_ref, gna_ref, doa_ref), (ob_ref, gsw_ref, dob_ref))):
            dn = dmixin[:, k * NA_WIDTH:(k + 1) * NA_WIDTH]
            on, r = _rms(o_ref[...])
            gstat_ref[k:k + 1, :] += jnp.sum(dn * on, axis=0, keepdims=True)
            do_ref[...] = _rms_bwd(on, r, dn * g_ref[...]).astype(BF16)

    hs = jax.ShapeDtypeStruct((T, NA_WIDTH), BF16)
    return _hosted_call(
        body, "attn_out_bwd", (T // tm,),
        [_tok_spec(D, tm), _full(w_out.shape), _tok_spec(NA_WIDTH, tm), _tok_spec(SW_WIDTH, tm), _full((1, NA_WIDTH)), _full((1, SW_WIDTH))],
        [_tok_spec(NA_WIDTH, tm), _tok_spec(SW_WIDTH, tm), _full((8, NA_WIDTH))],
        [hs, hs, jax.ShapeDtypeStruct((8, NA_WIDTH), F32)],
        (dmix, w_out, oa, ob, g_na, g_sw), tasks)


def _attn_in_bwd(dqa, dka, dva, dqb, dkb, dvb, cos_t, sin_t, w_in, x2d, mod3, g_attn, dx1, B, S, tm=WIDE_TILE):
    T, D = x2d.shape
    tps = S // tm

    def body(dqa_ref, dka_ref, dva_ref, dqb_ref, dkb_ref, dvb_ref, cos_ref, sin_ref, w_ref, x_ref, mod_ref, g_ref, dx1_ref,
             gx_ref, dproj_ref, gstat_ref, bstat_ref):
        i = pl.program_id(0)
        drb = jnp.concatenate([dqb_ref[...] * Q_SCALE, dkb_ref[...]], axis=1).astype(F32)
        reps = (1, ROPE_WIDTH // (2 * HEAD_DIM))
        drb = drb * jnp.tile(cos_ref[...], reps) + _rot_half(drb * jnp.tile(sin_ref[...], reps))
        dproj = jnp.concatenate([(dqa_ref[...] * Q_SCALE).astype(BF16), dka_ref[...].astype(BF16), dva_ref[...].astype(BF16),
                                 drb.astype(BF16), dvb_ref[...].astype(BF16)], axis=1)
        dproj_ref[...] = dproj
        dh = _nn(dproj, w_ref[...])
        xn, r = _rms(x_ref[...])
        scale1 = 1.0 + mod_ref[0, 1:2, :]
        gx_ref[...] = dx1_ref[...] + _rms_bwd(xn, r, dh * g_ref[...] * scale1)

        @pl.when(i == 0)
        def _():
            gstat_ref[...] = jnp.zeros_like(gstat_ref)

        @pl.when(i % tps == 0)
        def _():
            bstat_ref[...] = jnp.zeros_like(bstat_ref)

        gstat_ref[0:1, :] += jnp.sum(dh * scale1 * xn, axis=0, keepdims=True)
        bstat_ref[0, 0:1, :] += jnp.sum(dh, axis=0, keepdims=True)
        bstat_ref[0, 1:2, :] += jnp.sum(dh * (xn * g_ref[...]), axis=0, keepdims=True)

    rope = _rope_spec(tps, tm)
    return pl.pallas_call(
        body, name="attn_in_bwd", grid=(T // tm,),
        in_specs=[_tok_spec(NA_WIDTH, tm), _tok_spec(NA_WIDTH, tm), _tok_spec(NA_WIDTH, tm), _tok_spec(SW_WIDTH, tm),
                  _tok_spec(SW_KV_WIDTH, tm), _tok_spec(SW_KV_WIDTH, tm), rope, rope, _full(w_in.shape), _tok_spec(D, tm),
                  _mod_spec(tps, D), _full((1, D)), _tok_spec(D, tm)],
        out_specs=[_tok_spec(D, tm), _tok_spec(IN_WIDTH, tm), _full((8, D)), _bstat_spec(tps, D)],
        out_shape=[jax.ShapeDtypeStruct((T, D), F32), jax.ShapeDtypeStruct((T, IN_WIDTH), BF16),
                   jax.ShapeDtypeStruct((8, D), F32), jax.ShapeDtypeStruct((B, 8, D), F32)],
        compiler_params=_params("arbitrary"),
    )(dqa, dka, dva, dqb, dkb, dvb, cos_t, sin_t, w_in, x2d, mod3, g_attn, dx1)


def _matmul_tn(a, b, name, tm=None, tk=512):
    T, M = a.shape
    N = b.shape[1]
    tm = M if tm is None else tm
    nk = T // tk

    def body(a_ref, b_ref, o_ref, acc):
        k = pl.program_id(1)

        @pl.when(k == 0)
        def _():
            acc[...] = jnp.zeros_like(acc)

        acc[...] += _tn(a_ref[...], b_ref[...])

        @pl.when(k == nk - 1)
        def _():
            o_ref[...] = acc[...].astype(BF16)

    return pl.pallas_call(
        body, name=name, grid=(M // tm, nk),
        in_specs=[pl.BlockSpec((tk, tm), lambda i, k: (k, i)), pl.BlockSpec((tk, N), lambda i, k: (k, 0))],
        out_specs=pl.BlockSpec((tm, N), lambda i, k: (i, 0)),
        out_shape=jax.ShapeDtypeStruct((M, N), BF16),
        scratch_shapes=[pltpu.VMEM((tm, N), F32)],
        compiler_params=_params("parallel", "arbitrary"),
    )(a, b)


def _na_geometry(S):
    rows = S // GRID_W
    wr = min(NA_ROWS_MAX, rows)
    return rows, wr


def _na_window(r, rows, wr):
    rs = jnp.clip(r - wr // 2, 0, rows - wr)
    return pl.multiple_of(rs * GRID_W, GRID_W), pl.multiple_of((rs - r + NA_ROWS_MAX - 1) * GRID_W, GRID_W)


NA_STEP_PAIRS = 2
NA_GW = NA_STEP_PAIRS * 128
NA_BWD_ROWS = 4
NA_ROWS_PER_STEP = 4


def _na_specs(S, kw_n, order):
    ng = NA_PAIRS // NA_STEP_PAIRS

    def col(k):
        return pl.BlockSpec((1, S, NA_GW), lambda *ids: (order(*ids)[0], 0, k * ng + order(*ids)[1]))
    bias = pl.BlockSpec((NA_STEP_PAIRS, N_DR * GRID_W, 128), lambda *ids: (order(*ids)[1], 0, 0))
    out = pl.BlockSpec((1, S, NA_GW), lambda *ids: (order(*ids)[0], 0, order(*ids)[1]))
    return col(0), col(1), col(2), bias, out


def _block_diag(t):
    left = lax.broadcasted_iota(jnp.int32, t.shape, 1) < HEAD_DIM
    zero = jnp.zeros_like(t)
    return jnp.concatenate([jnp.where(left, t, zero), jnp.where(left, zero, t)], axis=0)


def _diag_blocks(res):
    left = lax.broadcasted_iota(jnp.int32, (HEAD_DIM, 128), 1) < HEAD_DIM
    return jnp.where(left, res[:HEAD_DIM], res[HEAD_DIM:])


def _col_softmax(st):
    e = jnp.exp(st - jnp.max(st, axis=0, keepdims=True))
    return e * (1.0 / jnp.sum(e, axis=0, keepdims=True))


def _na_fwd(qkv, bias, tasks=()):
    B, S, _ = qkv.shape
    rows, wr = _na_geometry(S)
    kw_n = wr * GRID_W

    def body(q_ref, k_ref, v_ref, b_ref, o_ref):
        def step(it, carry):
            win = [_na_window(it * NA_ROWS_PER_STEP + u, rows, wr) for u in range(NA_ROWS_PER_STEP)]
            qrows = [pl.ds(pl.multiple_of((it * NA_ROWS_PER_STEP + u) * GRID_W, GRID_W), GRID_W) for u in range(NA_ROWS_PER_STEP)]
            krows = [pl.ds(w[0], kw_n) for w in win]
            brows = [pl.ds(w[1], kw_n) for w in win]
            lanes = [pl.ds(p * 128, 128) for p in range(NA_STEP_PAIRS)]
            chains = [(u, p) for u in range(NA_ROWS_PER_STEP) for p in range(NA_STEP_PAIRS)]
            st = {(u, p): _nt(k_ref[0, krows[u], lanes[p]], _block_diag(q_ref[0, qrows[u], lanes[p]])) for u, p in chains}
            pn = {(u, p): _col_softmax(st[(u, p)] + b_ref[p, brows[u], :]).astype(BF16) for u, p in chains}
            out = {(u, p): _diag_blocks(_tn(pn[(u, p)], v_ref[0, krows[u], lanes[p]])) for u, p in chains}
            for u in range(NA_ROWS_PER_STEP):
                o_ref[0, qrows[u], :] = jnp.concatenate([out[(u, p)] for p in range(NA_STEP_PAIRS)], axis=1)
            return carry

        lax.fori_loop(0, rows // NA_ROWS_PER_STEP, step, 0)

    q, k, v, bs, out = _na_specs(S, kw_n, lambda b, g: (b, g))
    return _hosted_call(body, "na_fwd", (B, NA_PAIRS // NA_STEP_PAIRS), [q, k, v, bs], [out],
                        [jax.ShapeDtypeStruct((B, S, NA_WIDTH), F32)], (qkv, qkv, qkv, bias), tasks)


def _na_bwd(qkv, bias, doa, tasks=()):
    B, S, _ = qkv.shape
    rows, wr = _na_geometry(S)
    kw_n = wr * GRID_W

    def body(q_ref, k_ref, v_ref, b_ref, do_ref, dq_ref, dk_ref, dv_ref, db_ref, dk_acc, dv_acc):
        @pl.when(pl.program_id(1) == 0)
        def _():
            db_ref[...] = jnp.zeros_like(db_ref)

        dk_acc[...] = jnp.zeros_like(dk_acc)
        dv_acc[...] = jnp.zeros_like(dv_acc)

        def step(it, carry):
            nu, pairs = range(NA_BWD_ROWS), range(NA_STEP_PAIRS)
            win = [_na_window(it * NA_BWD_ROWS + u, rows, wr) for u in nu]
            qrows = [pl.ds(pl.multiple_of((it * NA_BWD_ROWS + u) * GRID_W, GRID_W), GRID_W) for u in nu]
            krows = [pl.ds(w[0], kw_n) for w in win]
            brows = [pl.ds(w[1], kw_n) for w in win]
            lanes = [pl.ds(p * 128, 128) for p in pairs]
            chains = [(u, p) for u in nu for p in pairs]
            kp = {(u, p): k_ref[0, krows[u], lanes[p]] for u, p in chains}
            qbd = {(u, p): _block_diag(q_ref[0, qrows[u], lanes[p]]) for u, p in chains}
            dobd = {(u, p): _block_diag(do_ref[0, qrows[u], lanes[p]]) for u, p in chains}
            st = {c: _nt(kp[c], qbd[c]) for c in chains}
            dpt = {(u, p): _nt(v_ref[0, krows[u], lanes[p]], dobd[(u, p)]) for u, p in chains}
            pn = {(u, p): _col_softmax(st[(u, p)] + b_ref[p, brows[u], :]) for u, p in chains}
            dst = {c: pn[c] * (dpt[c] - jnp.sum(pn[c] * dpt[c], axis=0, keepdims=True)) for c in chains}
            dsb = {c: dst[c].astype(BF16) for c in chains}
            dq = {c: _diag_blocks(_tn(dsb[c], kp[c])) for c in chains}
            dk = {c: _nn(dsb[c], qbd[c]) for c in chains}
            dv = {c: _nn(pn[c].astype(BF16), dobd[c]) for c in chains}
            for u in nu:
                dq_ref[0, qrows[u], :] = jnp.concatenate([dq[(u, p)] for p in pairs], axis=1).astype(BF16)
                dk_acc[krows[u], :] += jnp.concatenate([dk[(u, p)] for p in pairs], axis=1)
                dv_acc[krows[u], :] += jnp.concatenate([dv[(u, p)] for p in pairs], axis=1)
                for p in pairs:
                    db_ref[p, brows[u], :] += dst[(u, p)]
            return carry

        lax.fori_loop(0, rows // NA_BWD_ROWS, step, 0)

        def emit(i, carry):
            r = pl.ds(pl.multiple_of(i * 256, 256), 256)
            dk_ref[0, r, :] = dk_acc[r, :].astype(BF16)
            dv_ref[0, r, :] = dv_acc[r, :].astype(BF16)
            return carry

        lax.fori_loop(0, S // 256, emit, 0)

    q, k, v, bs, out = _na_specs(S, kw_n, lambda g, b: (b, g))
    hs = jax.ShapeDtypeStruct((B, S, NA_WIDTH), BF16)
    return _hosted_call(body, "na_bwd", (NA_PAIRS // NA_STEP_PAIRS, B), [q, k, v, bs, out], [out, out, out, bs],
                        [hs, hs, hs, jax.ShapeDtypeStruct((NA_PAIRS, N_DR * GRID_W, 128), F32)], (qkv, qkv, qkv, bias, doa), tasks,
                        scratch_shapes=[pltpu.VMEM((S, NA_GW), F32), pltpu.VMEM((S, NA_GW), F32)])


SW_PAIRS = SW_HEADS // 2


def _sw_band(n, S):
    kw_n = 3 * SW_BLOCK
    start = pl.multiple_of(jnp.clip(n * SW_BLOCK - SW_BLOCK, 0, S - kw_n), SW_BLOCK)
    kpos = start + lax.broadcasted_iota(jnp.int32, (kw_n, SW_BLOCK), 0)
    qpos = n * SW_BLOCK + lax.broadcasted_iota(jnp.int32, (kw_n, SW_BLOCK), 1)
    return start, jnp.abs(qpos - kpos) <= SW_WINDOW


def _kv_halves(t):
    left = lax.broadcasted_iota(jnp.int32, t.shape, 1) < HEAD_DIM
    swapped = pltpu.roll(t, HEAD_DIM, axis=1)
    zero = jnp.zeros_like(t)
    return {(0, 0): jnp.where(left, t, zero), (0, 1): jnp.where(left, zero, swapped),
            (1, 0): jnp.where(left, swapped, zero), (1, 1): jnp.where(left, zero, t)}


def _sw_probs(st, ok, sk):
    st = jnp.where(ok, st, NEG)
    m = jnp.maximum(jnp.max(st, axis=0, keepdims=True), sk)
    e = jnp.exp(st - m)
    esk = jnp.exp(sk - m)
    inv = 1.0 / (jnp.sum(e, axis=0, keepdims=True) + esk)
    return e * inv, esk * inv


def _sw_specs(S):
    q = pl.BlockSpec((1, S, SW_WIDTH), lambda b, *_: (b, 0, ROPE_LO // SW_WIDTH))
    k = pl.BlockSpec((1, S, SW_KV_WIDTH), lambda b, *_: (b, 0, (ROPE_LO + SW_WIDTH) // SW_KV_WIDTH))
    v = pl.BlockSpec((1, S, SW_KV_WIDTH), lambda b, *_: (b, 0, (ROPE_LO + ROPE_WIDTH) // SW_KV_WIDTH))
    return q, k, v


SW_FWD_SPLIT = 2


def _sw_fwd(sink, qkv, tasks=()):
    B, S, _ = qkv.shape
    kw_n = 3 * SW_BLOCK

    def body(sink_ref, q_ref, k_ref, v_ref, o_ref):
        def step(n, carry):
            start, ok = _sw_band(n, S)
            qrows = pl.ds(pl.multiple_of(n * SW_BLOCK, SW_BLOCK), SW_BLOCK)
            krows = pl.ds(start, kw_n)
            kh, vh = _kv_halves(k_ref[0, krows, :]), _kv_halves(v_ref[0, krows, :])
            heads = [(p, e) for p in range(SW_PAIRS) for e in range(2)]
            qp = [q_ref[0, qrows, pl.ds(p * 128, 128)] for p in range(SW_PAIRS)]
            kv_of = lambda p: p // (SW_PAIRS // SW_KV_HEADS)
            st = {(p, e): _nt(kh[(kv_of(p), e)], qp[p]) for p, e in heads}
            pn = {(p, e): _sw_probs(st[(p, e)], ok, sink_ref[2 * p + e])[0].astype(BF16) for p, e in heads}
            outs = [_tn(pn[(p, 0)], vh[(kv_of(p), 0)]) + _tn(pn[(p, 1)], vh[(kv_of(p), 1)]) for p in range(SW_PAIRS)]
            o_ref[0, qrows, :] = jnp.concatenate(outs, axis=1)
            return carry

        half = (S // SW_BLOCK) // SW_FWD_SPLIT
        lax.fori_loop(pl.program_id(1) * half, (pl.program_id(1) + 1) * half, step, 0)

    q, k, v = _sw_specs(S)
    return _hosted_call(
        body, "sw_fwd", (B, SW_FWD_SPLIT), [pl.BlockSpec(memory_space=pltpu.SMEM), q, k, v],
        [pl.BlockSpec((1, S, SW_WIDTH), lambda b, s: (b, 0, 0))], [jax.ShapeDtypeStruct((B, S, SW_WIDTH), F32)],
        (sink, qkv, qkv, qkv), tasks)


def _sw_bwd(sink, qkv, dob):
    B, S, _ = qkv.shape
    kw_n = 3 * SW_BLOCK

    fold_rows = 256

    def body(sink_ref, q_ref, k_ref, v_ref, do_ref, dq_ref, dk_ref, dv_ref, dsink_ref, dk_acc, dv_acc):
        @pl.when(pl.program_id(0) == 0)
        def _():
            dsink_ref[...] = jnp.zeros_like(dsink_ref)

        dk_acc[...] = jnp.zeros_like(dk_acc)
        dv_acc[...] = jnp.zeros_like(dv_acc)
        ppk = SW_PAIRS // SW_KV_HEADS

        def step(n, carry):
            start, ok = _sw_band(n, S)
            qrows = pl.ds(pl.multiple_of(n * SW_BLOCK, SW_BLOCK), SW_BLOCK)
            krows = pl.ds(start, kw_n)
            kh, vh = _kv_halves(k_ref[0, krows, :]), _kv_halves(v_ref[0, krows, :])
            heads = [(p, e) for p in range(SW_PAIRS) for e in range(2)]
            qp = [q_ref[0, qrows, pl.ds(p * 128, 128)] for p in range(SW_PAIRS)]
            dop = [do_ref[0, qrows, pl.ds(p * 128, 128)] for p in range(SW_PAIRS)]
            st = {(p, e): _nt(kh[(p // ppk, e)], qp[p]) for p, e in heads}
            dpt = {(p, e): _nt(vh[(p // ppk, e)], dop[p]) for p, e in heads}
            pnb, dsb = {}, {}
            for p, e in heads:
                pn, psink = _sw_probs(st[(p, e)], ok, sink_ref[2 * p + e])
                delta = jnp.sum(pn * dpt[(p, e)], axis=0, keepdims=True)
                dsb[(p, e)] = (pn * (dpt[(p, e)] - delta)).astype(BF16)
                pnb[(p, e)] = pn.astype(BF16)
                dsink_ref[2 * p + e:2 * p + e + 1, :] += -(psink * delta)
            dq_ref[0, qrows, :] = jnp.concatenate(
                [_tn(dsb[(p, 0)], kh[(p // ppk, 0)]) + _tn(dsb[(p, 1)], kh[(p // ppk, 1)]) for p in range(SW_PAIRS)],
                axis=1).astype(BF16)
            left = lax.broadcasted_iota(jnp.int32, (kw_n, 128), 1) < HEAD_DIM
            dks, dvs = [], []
            for kv in range(SW_KV_HEADS):
                dk = dv = None
                for p in range(kv * ppk, (kv + 1) * ppk):
                    dk_p = jnp.where(left, _nn(dsb[(p, 0)], qp[p]), _nn(dsb[(p, 1)], qp[p]))
                    dv_p = jnp.where(left, _nn(pnb[(p, 0)], dop[p]), _nn(pnb[(p, 1)], dop[p]))
                    dk = dk_p if dk is None else dk + dk_p
                    dv = dv_p if dv is None else dv + dv_p
                dks.append(dk)
                dvs.append(dv)
            dk_acc[krows, :] += jnp.concatenate(dks, axis=1)
            dv_acc[krows, :] += jnp.concatenate(dvs, axis=1)
            return carry

        lax.fori_loop(0, S // SW_BLOCK, step, 0)

        def fold(i, carry):
            rows = pl.ds(pl.multiple_of(i * fold_rows, fold_rows), fold_rows)
            left = lax.broadcasted_iota(jnp.int32, (fold_rows, 128), 1) < HEAD_DIM
            for acc, out_ref in ((dk_acc, dk_ref), (dv_acc, dv_ref)):
                a, b = acc[rows, 0:128], acc[rows, 128:256]
                out_ref[0, rows, :] = jnp.where(left, a + pltpu.roll(a, HEAD_DIM, axis=1),
                                                b + pltpu.roll(b, HEAD_DIM, axis=1)).astype(BF16)
            return carry

        lax.fori_loop(0, S // fold_rows, fold, 0)

        @pl.when(pl.program_id(0) == B - 1)
        def _():
            dsink_ref[...] = jnp.broadcast_to(jnp.sum(dsink_ref[...], axis=1, keepdims=True), dsink_ref.shape)

    q, k, v = _sw_specs(S)
    qo = pl.BlockSpec((1, S, SW_WIDTH), lambda b: (b, 0, 0))
    ko = pl.BlockSpec((1, S, SW_KV_WIDTH), lambda b: (b, 0, 0))
    return pl.pallas_call(
        body, name="sw_bwd", grid=(B,),
        in_specs=[pl.BlockSpec(memory_space=pltpu.SMEM), q, k, v, qo],
        out_specs=[qo, ko, ko, _full((SW_HEADS, 128))],
        out_shape=[jax.ShapeDtypeStruct((B, S, SW_WIDTH), BF16), jax.ShapeDtypeStruct((B, S, SW_KV_WIDTH), BF16),
                   jax.ShapeDtypeStruct((B, S, SW_KV_WIDTH), BF16), jax.ShapeDtypeStruct((SW_HEADS, 128), F32)],
        scratch_shapes=[pltpu.VMEM((S, 2 * SW_KV_WIDTH), F32), pltpu.VMEM((S, 2 * SW_KV_WIDTH), F32)],
        compiler_params=_params("arbitrary"),
    )(sink, qkv, qkv, qkv, dob)


def _pack_sum_adamw(packs, params, pick):
    W = packs.shape[1]
    n_p = len(params)

    def body(p_ref, *refs):
        ins, tot_ref, pick_ref, outs = refs[:3 * n_p], refs[3 * n_p], refs[3 * n_p + 1], refs[3 * n_p + 2:]
        tot = p_ref[0:8, :]
        for d in range(1, N_DEV):
            tot = tot + p_ref[8 * d:8 * d + 8, :]
        tot_ref[...] = tot
        pick_ref[...] = tot[pick[0]:pick[0] + 1, pick[1]:pick[1] + 1]
        for i, (w, _, _, rows, off) in enumerate(params):
            w_ref, m_ref, v_ref = ins[3 * i:3 * i + 3]
            g_ref, d_ref, nm_ref, nv_ref = outs[4 * i:4 * i + 4]
            if w.ndim == 3:
                n_a, n_b, n = w.shape
                for a in range(n_a):
                    for b in range(n_b):
                        o = off + (b * n_a + a) * n
                        g_ref[a, b:b + 1, :] = tot_ref[rows[0]:rows[0] + 1, o:o + n]
                g = g_ref[...]
            else:
                n = w.shape[1]
                g = tot[rows[0]:rows[0] + 1, off:off + n]
                for r in rows[1:]:
                    g = g + tot[r:r + 1, off:off + n]
                g_ref[...] = g
            d_ref[...], nm_ref[...], nv_ref[...] = _adam_update(w_ref[...], g, m_ref[...], v_ref[...])

    res = pl.pallas_call(
        body, name="small_adamw",
        out_shape=[jax.ShapeDtypeStruct((8, W), F32), jax.ShapeDtypeStruct((1, 1), F32)]
        + [jax.ShapeDtypeStruct(p[0].shape, F32) for p in params for _ in range(4)],
        compiler_params=pltpu.CompilerParams(vmem_limit_bytes=VMEM_LIMIT),
    )(packs, *[a for p in params for a in p[:3]])
    return res[0], res[1], [res[2 + 4 * i:6 + 4 * i] for i in range(n_p)]


def _adam_update(w, g, m, v):
    c1 = 1.0 - ADAM_B1 ** ADAM_STEP
    c2 = 1.0 - ADAM_B2 ** ADAM_STEP
    nm = ADAM_B1 * m + (1.0 - ADAM_B1) * g
    nv = ADAM_B2 * v + (1.0 - ADAM_B2) * (g * g)
    return -ADAM_LR * ((nm / c1) / (jnp.sqrt(nv / c2) + ADAM_EPS) + ADAM_WD * w), nm, nv


def _adamw(w, g, m, v, name):
    def body(w_ref, g_ref, m_ref, v_ref, d_ref, nm_ref, nv_ref):
        d_ref[...], nm_ref[...], nv_ref[...] = _adam_update(w_ref[...], g_ref[...], m_ref[...], v_ref[...])

    s = jax.ShapeDtypeStruct(w.shape, F32)
    return pl.pallas_call(body, name=name, out_shape=[s, s, s],
                          compiler_params=pltpu.CompilerParams(vmem_limit_bytes=VMEM_LIMIT))(w, g, m, v)


def _sum_adamw_rows(R):
    return max(r for r in range(16, min(R, 256) + 1, 16) if R % r == 0)


def _sum_adamw_steps(R):
    return R // _sum_adamw_rows(R)


def _sum_first(own, recvb, name, tasks):
    R, C = own.shape
    rc = _sum_adamw_rows(R)

    def body(own_ref, r_ref, p_ref):
        p_ref[...] = (own_ref[...] + r_ref[0].astype(F32)) + r_ref[1].astype(F32)

    blk = pl.BlockSpec((rc, C), lambda i: (i, 0))
    (part,), got = _hosted_call(body, name, (R // rc,), [blk, pl.BlockSpec((2, rc, C), lambda i: (0, i, 0))], [blk],
                                [jax.ShapeDtypeStruct((R, C), F32)], (own, recvb), tasks)
    return part, got


def _sum_adamw(own, recvb, w, m, v, name, done=0):
    R, C = own.shape
    rc = _sum_adamw_rows(R)
    left = 3 - done
    assert 3 % left == 0

    def body(own_ref, r_ref, w_ref, m_ref, v_ref, g_ref, d_ref, nm_ref, nv_ref):
        g = own_ref[...]
        for j in range(left):
            g = g + r_ref[j].astype(F32)
        g_ref[...] = g
        d_ref[...], nm_ref[...], nv_ref[...] = _adam_update(w_ref[...], g, m_ref[...], v_ref[...])

    blk = pl.BlockSpec((rc, C), lambda i: (i, 0))
    s = jax.ShapeDtypeStruct((R, C), F32)
    return pl.pallas_call(
        body, name=name, grid=(R // rc,),
        in_specs=[blk, pl.BlockSpec((left, rc, C), lambda i: (done // left, i, 0)), blk, blk, blk],
        out_specs=[blk, blk, blk, blk], out_shape=[s, s, s, s], compiler_params=_params("parallel"),
    )(own, recvb, w, m, v)


def _by_device(dw):
    return dw.reshape(N_DEV, dw.shape[0] // N_DEV, dw.shape[1])


def _local_step(x, mod, g_attn, w_in, bias, sw_sink, g_na_out, g_sw_out, w_out, g_ffn, w_up, conv_w, conv_b, w_down,
                g_final, target, sharded):
    B, S, D = x.shape
    T = B * S
    x2d = x.reshape(T, D)
    mod3 = mod.reshape(B, 6, D)
    cos_t, sin_t = _rope_tables(S)
    sink = sw_sink.reshape(SW_HEADS)
    n_tiles = T // WIDE_TILE
    full = lambda g: g.reshape(N_DEV * g.shape[1], g.shape[2])

    rider = lambda w, mid, lo, n, into=None: [_gather_task(w, mid, rows=(lo, n), into=into)] if sharded else []
    if sharded:
        qu, hd = w_up.shape[0] // 4, w_down.shape[0] // 2
    (h, qkv), got = _attn_in(x2d, mod3, g_attn, w_in, cos_t, sin_t, S,
                             [_gather_task(w_out, n_tiles // 2)] + rider(w_up, n_tiles - 1, 0, qu) if sharded else [])
    if sharded:
        w_out, w_up_buf = full(got[0][0]), got[1][0]
    qkv3 = qkv.reshape(B, S, IN_WIDTH)
    na_steps, sw_steps = B * (NA_PAIRS // NA_STEP_PAIRS), B * SW_FWD_SPLIT
    (oa,), got = _na_fwd(qkv3, bias, rider(w_up, na_steps - 1, qu, 2 * qu, w_up_buf) if sharded else [])
    if sharded:
        w_up_buf = got[0][0]
    oa = oa.reshape(T, NA_WIDTH)
    (ob,), got = _sw_fwd(sink, qkv3, rider(w_up, sw_steps // 2, 3 * qu, qu, w_up_buf) if sharded else [])
    if sharded:
        w_up = full(got[0][0])
    ob = ob.reshape(T, SW_WIDTH)
    (mixin, mix, x1), _ = _attn_out(oa, ob, x2d, mod3, g_na_out, g_sw_out, w_out, S)
    (h2, val, gt), got = _ffn_up(x1, mod3, g_ffn, w_up, S, rider(w_down, 3 * n_tiles // 4, 0, 2 * hd) if sharded else [])
    if sharded:
        w_down = full(got[0][0])
    a, act, vd, dx2, df, gstat_f, bstat_f = _ffn_down(gt, val, conv_w, conv_b, w_down, x1, mod3, g_final, target.reshape(T, D), B, S)
    F = val.shape[1]

    dw_down = _matmul_tn(a, df, "dw_down")
    (dval, dgc, cstat), got = _ffn_down_bwd(df, w_down, act, vd, [_swap_task(_by_device(dw_down))] if sharded else [])
    if sharded:
        send_down, own_down = _chip_sums(_by_device(dw_down), got[0][0])
    (du, dx1, dmix, gstat_u, bstat_u, cstat_w), got = _ffn_up_bwd(dgc, dval, gt, conv_w, w_up, x1, mod3, g_ffn, dx2, mix, B, S,
                                                                  [_exchange_task(send_down)] if sharded else [])
    if sharded:
        dw_down = (own_down, got[0][0])
    dw_up = _matmul_tn(du, h2, "dw_up", tm=F)
    dw_out = _matmul_tn(mixin, dmix, "dw_out")
    (doa, dob, gstat_o), got = _attn_out_bwd(dmix, w_out, oa, ob, g_na_out, g_sw_out,
                                             [_swap_task(_by_device(dw_up)), _swap_task(_by_device(dw_out))] if sharded else [])
    if sharded:
        send_up, own_up = _chip_sums(_by_device(dw_up), got[0][0])
        send_out, own_out = _chip_sums(_by_device(dw_out), got[1][0])
    (dqa, dka, dva, dbt), got = _na_bwd(qkv3, bias, doa.reshape(B, S, NA_WIDTH),
                                        [_exchange_task(send_up), _exchange_task(send_out)] if sharded else [])
    if sharded:
        dw_up, dw_out = (own_up, got[0][0]), (own_out, got[1][0])
    dqb, dkb, dvb, dsink = _sw_bwd(sink, qkv3, dob.reshape(B, S, SW_WIDTH))
    r2 = lambda t: t.reshape(T, t.shape[-1])
    grad_x, dproj, gstat_i, bstat_i = _attn_in_bwd(r2(dqa), r2(dka), r2(dva), r2(dqb), r2(dkb), r2(dvb), cos_t, sin_t, w_in, x2d, mod3,
                                                   g_attn, dx1, B, S)
    dw_in = _matmul_tn(dproj, h, "dw_in")

    dmod = jnp.stack([bstat_i[:, 0], bstat_i[:, 1], bstat_u[:, 2], bstat_u[:, 0], bstat_u[:, 1], bstat_f[:, 0]], axis=1)
    small = dict(g_attn=gstat_i[0], g_ffn=gstat_u[0], g_final=gstat_f[0], loss=gstat_f[1, 0], g_na_out=gstat_o[0], g_sw_out=gstat_o[1],
                 sw_sink=dsink[:, 0], conv_b=cstat[0], conv_w=cstat_w[1:4], dbt=dbt,
                 raw=(bstat_i, bstat_u, bstat_f, gstat_i, gstat_u, gstat_f, gstat_o, dsink, cstat, cstat_w))
    return grad_x.reshape(B, S, D), dict(w_in=dw_in, w_out=dw_out, w_up=dw_up, w_down=dw_down), dmod, small


def _pack_slab(raw, drpb):
    D, F = raw[3].shape[1], raw[8].shape[1]
    n_seq = raw[0].shape[0]

    def body(bi_ref, bu_ref, bf_ref, gi_ref, gu_ref, gf_ref, go_ref, ds_ref, cs_ref, cw_ref, rp_ref, o_ref):
        o_ref[...] = jnp.zeros_like(o_ref)
        for b in range(n_seq):
            mods = (bi_ref[b, 0:1, :], bi_ref[b, 1:2, :], bu_ref[b, 2:3, :], bu_ref[b, 0:1, :], bu_ref[b, 1:2, :], bf_ref[b, 0:1, :])
            for k, row in enumerate(mods):
                o_ref[b:b + 1, k * D:(k + 1) * D] = row
        o = 0
        for row in (gi_ref[0:1, :], gu_ref[0:1, :], gf_ref[0:1, :], go_ref[0:1, :], go_ref[1:2, :]):
            o_ref[2:3, o:o + row.shape[1]] = row
            o += row.shape[1]
        ds = ds_ref[...]
        eye = lax.broadcasted_iota(jnp.int32, ds.shape, 0) == lax.broadcasted_iota(jnp.int32, ds.shape, 1)
        o_ref[2:3, o:o + 128] = jnp.sum(jnp.where(eye, ds, 0.0), axis=0, keepdims=True)
        o_ref[2:3, o + 128:o + 256] = gf_ref[1:2, 0:128]
        o_ref[3:4, 0:F] = cs_ref[0:1, :]
        o_ref[4:5, 0:rp_ref.shape[1]] = rp_ref[...]
        o_ref[5:8, 0:F] = cw_ref[1:4, :]

    return pl.pallas_call(body, name="pack_slab", out_shape=jax.ShapeDtypeStruct((8, PACK_W), F32),
                          compiler_params=pltpu.CompilerParams(vmem_limit_bytes=VMEM_LIMIT))(*raw, drpb)


def kernel(x, c, w_ada, b_ada, g_attn, w_in, na_rpb, sw_sink, g_na_out, g_sw_out, w_out, g_ffn, w_up, conv_w, conv_b, w_down, g_final, loss_target, m_w_ada, m_b_ada, m_g_attn, m_w_in, m_na_rpb, m_sw_sink, m_g_na_out, m_g_sw_out, m_w_out, m_g_ffn, m_w_up, m_conv_w, m_conv_b, m_w_down, m_g_final, v_w_ada, v_b_ada, v_g_attn, v_w_in, v_na_rpb, v_sw_sink, v_g_na_out, v_g_sw_out, v_w_out, v_g_ffn, v_w_up, v_conv_w, v_conv_b, v_w_down, v_g_final):
    B, S, D = x.shape
    me = 4 * lax.axis_index("x") + 2 * lax.axis_index("y") + lax.axis_index("c")
    ada_c = w_ada.shape[2]
    F_l = conv_w.shape[2]

    tr = {"w_in", "w_up"}
    w_in_t = jnp.transpose(w_in[0])
    shards_f = dict(w_out=w_out[0], w_up=jnp.transpose(w_up[0]), w_down=w_down[0])

    slabs, c_all, mod_all, w_in_all, casts, bias = _ada_fwd(c, conv_w[0], w_ada[0], b_ada, w_in_t, list(shards_f.values()),
                                                            _na_bias_rev(na_rpb[0]))
    shards = dict(zip(shards_f, casts))
    conv_w_f = jnp.transpose(slabs[:, :3, D:], (1, 0, 2)).reshape(3, N_DEV * F_l)
    mod_mine = lax.dynamic_slice(mod_all, (0, me * B, 0), (N_DEV, B, ada_c))
    mod = jnp.transpose(mod_mine, (1, 0, 2)).reshape(B, N_DEV * ada_c)
    w_in_f = w_in_all.reshape(N_DEV * w_in_t.shape[0], D)

    grad_x, dw, dmod, small = _local_step(x, mod, g_attn, w_in_f, bias, sw_sink, g_na_out, g_sw_out, shards["w_out"], g_ffn,
                                          shards["w_up"], conv_w_f, conv_b, shards["w_down"], g_final.reshape(1, D), loss_target,
                                          sharded=True)
    g8_in = _by_device(dw["w_in"])
    drpb, got = _na_bias_grad(small["dbt"], [_swap_task(g8_in)])
    send_in, own_in = _chip_sums(g8_in, got[0][0])

    slab = _pack_slab(small["raw"], drpb.reshape(1, -1))
    weights = dict(w_ada=w_ada, b_ada=b_ada, g_attn=g_attn, w_in=w_in, na_rpb=na_rpb, sw_sink=sw_sink, g_na_out=g_na_out,
                   g_sw_out=g_sw_out, w_out=w_out, g_ffn=g_ffn, w_up=w_up, conv_w=conv_w, conv_b=conv_b, w_down=w_down, g_final=g_final)
    ms = dict(w_ada=m_w_ada, b_ada=m_b_ada, g_attn=m_g_attn, w_in=m_w_in, na_rpb=m_na_rpb, sw_sink=m_sw_sink, g_na_out=m_g_na_out,
              g_sw_out=m_g_sw_out, w_out=m_w_out, g_ffn=m_g_ffn, w_up=m_w_up, conv_w=m_conv_w, conv_b=m_conv_b, w_down=m_w_down, g_final=m_g_final)
    vs = dict(w_ada=v_w_ada, b_ada=v_b_ada, g_attn=v_g_attn, w_in=v_w_in, na_rpb=v_na_rpb, sw_sink=v_sw_sink, g_na_out=v_g_na_out,
              g_sw_out=v_g_sw_out, w_out=v_w_out, g_ffn=v_g_ffn, w_up=v_w_up, conv_w=v_conv_w, conv_b=v_conv_b, w_down=v_w_down, g_final=v_g_final)
    names = list(weights)
    grads, deltas, new_m, new_v = {}, {}, {}, {}
    flat = lambda t: t.reshape(1, -1)

    def shard2d(nm):
        if nm in tr:
            return (lambda t: jnp.transpose(t[0])), (lambda t: jnp.transpose(t)[None])
        if nm == "conv_w":
            return (lambda t: jnp.transpose(t, (1, 0, 2))), (lambda t: jnp.transpose(t, (1, 0, 2)))
        return (lambda t: t[0]), (lambda t: t[None])

    def finish_sum(nm, own, recvb, done=0):
        r, back = shard2d(nm)
        g2, d_, m_, v_ = _sum_adamw(own, recvb, r(weights[nm]), r(ms[nm]), r(vs[nm]), "adamw_" + nm, done)
        grads[nm], deltas[nm], new_m[nm], new_v[nm] = back(g2), back(d_), back(m_), back(v_)

    own_up, recv_up = dw["w_up"]
    n_up = _sum_adamw_steps(own_up.shape[0])
    part_up, got = _sum_first(own_up, recv_up, "sum_first_w_up", [_gather_task(slab, n_up - 1), _exchange_task(send_in)])
    packs = got[0][0]
    finish_sum("w_up", part_up, recv_up, done=2)
    finish_sum("w_down", *dw["w_down"])
    finish_sum("w_in", own_in, got[1][0])
    finish_sum("w_out", *dw["w_out"])

    where = dict(b_ada=((0, 1), 0), g_attn=((2,), 0), g_ffn=((2,), D), g_final=((2,), 2 * D), g_na_out=((2,), 3 * D),
                 g_sw_out=((2,), 3 * D + NA_WIDTH), sw_sink=((2,), 3 * D + NA_WIDTH + SW_WIDTH), conv_b=((3,), 0), na_rpb=((4,), 0))

    def small_view(n):
        if n == "na_rpb":
            return (lambda t: jnp.transpose(t[0], (1, 0, 2))), (lambda t: jnp.transpose(t, (1, 0, 2))[None])
        return flat, (lambda t: t.reshape(weights[n].shape))

    tot, loss, small_out = _pack_sum_adamw(
        packs.reshape(N_DEV * 8, PACK_W),
        [tuple(small_view(n)[0](t[n]) for t in (weights, ms, vs)) + where[n] for n in where],
        (2, 3 * D + NA_WIDTH + SW_WIDTH + 128))
    for n, res in zip(where, small_out):
        grads[n], deltas[n], new_m[n], new_v[n] = [small_view(n)[1](t) for t in res]
    loss = loss.reshape(())

    dmod_cols = lax.dynamic_slice(packs.reshape(N_DEV * 8, PACK_W), (0, me * ada_c), (N_DEV * 8, ada_c))
    for nm, g2 in (("w_ada", _ada_bwd(c_all, dmod_cols)), ("conv_w", lax.dynamic_slice(tot, (5, me * F_l), (3, F_l))[:, None])):
        r, back = shard2d(nm)
        d_, m_, v_ = _adamw(r(weights[nm]), g2, r(ms[nm]), r(vs[nm]), "adamw_" + nm)
        grads[nm], deltas[nm], new_m[nm], new_v[nm] = back(g2), back(d_), back(m_), back(v_)
    return (loss, grad_x, *[grads[n] for n in names], *[deltas[n] for n in names], *[new_m[n] for n in names],
            *[new_v[n] for n in names])
```

```python
import functools

import numpy as np
import jax
import jax.numpy as jnp
from jax import lax
from jax.experimental import pallas as pl
from jax.experimental.pallas import tpu as pltpu

F32, BF16 = jnp.float32, jnp.bfloat16
MESH_ID = pl.DeviceIdType.MESH
N_DEV = 8

HEAD_DIM = 64
NA_HEADS = 8
SW_HEADS = 8
SW_KV_HEADS = 2
SW_GROUP = SW_HEADS // SW_KV_HEADS
NA_WIDTH = NA_HEADS * HEAD_DIM
SW_WIDTH = SW_HEADS * HEAD_DIM
SW_KV_WIDTH = SW_KV_HEADS * HEAD_DIM
ROPE_WIDTH = SW_WIDTH + SW_KV_WIDTH
IN_WIDTH = 3 * NA_WIDTH + SW_WIDTH + 2 * SW_KV_WIDTH
ROPE_LO = 3 * NA_WIDTH
GRID_W = 64
NA_ROWS_MAX = 8
NA_COLS = 16
N_DR = 2 * NA_ROWS_MAX - 1
N_DC = 2 * NA_COLS - 1
SW_WINDOW = 128
SW_BLOCK = 128
ROPE_THETA = 10000.0
EPS = 1e-6
NEG = -1e30
Q_SCALE = HEAD_DIM ** -0.5

ADAM_LR = 0.001
ADAM_B1 = 0.9
ADAM_B2 = 0.999
ADAM_EPS = 1e-08
ADAM_WD = 0.01
ADAM_STEP = 10

TOKEN_TILE = 256
WIDE_TILE = 512
VMEM_LIMIT = 56 * 1024 * 1024

PACK_W = 6144


def _nn(a, b):
    return jnp.dot(a, b, preferred_element_type=F32)


def _nt(a, b):
    return lax.dot_general(a, b, (((1,), (1,)), ((), ())), preferred_element_type=F32)


def _tn(a, b):
    return lax.dot_general(a, b, (((0,), (0,)), ((), ())), preferred_element_type=F32)


def _rms(x):
    r = lax.rsqrt(jnp.mean(x * x, axis=-1, keepdims=True) + EPS)
    return x * r, r


def _rms_bwd(xn, r, gy):
    return r * (gy - xn * jnp.mean(xn * gy, axis=-1, keepdims=True))


def _params(*sem):
    return pltpu.CompilerParams(dimension_semantics=sem, vmem_limit_bytes=VMEM_LIMIT)


def _full(shape):
    n = len(shape)
    return pl.BlockSpec(shape, lambda *_: (0,) * n)


def _mesh_pos():
    return lax.axis_index("x"), lax.axis_index("y"), lax.axis_index("c")


def _row_chunk(r):
    for rc in (128, 64, 32, 16):
        if r % rc == 0:
            return rc
    raise ValueError(f"rows {r} not a multiple of 16")


class _Task:
    def __init__(self, inputs, out_shapes, sems, start, finish, mid=None, mid_step=None, alias=None):
        self.inputs, self.out_shapes, self.sems = list(inputs), list(out_shapes), list(sems)
        self.start, self.finish, self.mid, self.mid_step = start, finish, mid, mid_step
        self.alias = alias


def _hosted_call(body, name, grid, in_specs, out_specs, out_shape, operands, tasks, scratch_shapes=()):
    n_in, n_out, n_scr = len(in_specs), len(out_specs), len(scratch_shapes)
    t_in = [len(t.inputs) for t in tasks]
    t_out = [len(t.out_shapes) for t in tasks]
    t_sem = [len(t.sems) for t in tasks]
    n_steps = int(np.prod(grid))

    def wrapped(*refs):
        ins, rest = refs[:n_in], refs[n_in:]
        task_ins, rest = rest[:sum(t_in)], rest[sum(t_in):]
        outs, rest = rest[:n_out], rest[n_out:]
        task_outs, rest = rest[:sum(t_out)], rest[sum(t_out):]
        scr, task_sems = rest[:n_scr], rest[n_scr:]
        step = pl.program_id(0)
        for ax in range(1, len(grid)):
            step = step * grid[ax] + pl.program_id(ax)
        parts = []
        oi = oo = os_ = 0
        for t, a, b, c in zip(tasks, t_in, t_out, t_sem):
            parts.append((t, task_ins[oi:oi + a], task_outs[oo:oo + b], task_sems[os_:os_ + c]))
            oi, oo, os_ = oi + a, oo + b, os_ + c
        for t, ti, to, ts in parts:
            pl.when(step == 0)(functools.partial(t.start, ti, to, ts))
            if t.mid is not None:
                pl.when(step == t.mid_step)(functools.partial(t.mid, ti, to, ts))
        body(*ins, *outs, *scr)
        for t, ti, to, ts in parts:
            pl.when(step == n_steps - 1)(functools.partial(t.finish, ti, to, ts))

    hbm = pl.BlockSpec(memory_space=pl.ANY)
    aliases, oi, oo = {}, n_in, n_out
    for t, a, b in zip(tasks, t_in, t_out):
        if t.alias is not None:
            aliases[oi + t.alias[0]] = oo + t.alias[1]
        oi, oo = oi + a, oo + b
    res = pl.pallas_call(
        wrapped, name=name, grid=grid,
        in_specs=list(in_specs) + [hbm] * sum(t_in),
        out_specs=list(out_specs) + [hbm] * sum(t_out),
        out_shape=list(out_shape) + [s for t in tasks for s in t.out_shapes],
        scratch_shapes=list(scratch_shapes) + [s for t in tasks for s in t.sems],
        input_output_aliases=aliases,
        compiler_params=_params(*(["arbitrary"] * len(grid))),
    )(*operands, *[a for t in tasks for a in t.inputs])
    own, extra = res[:n_out], res[n_out:]
    per_task, o = [], 0
    for b in t_out:
        per_task.append(extra[o:o + b])
        o += b
    return own, per_task


def _gather_task(shard, mid_step, rows=None, into=None):
    lo, n = (0, shard.shape[0]) if rows is None else rows

    def parts(ins, outs, sems):
        x_ref, out_ref, (send_sems, recv_sems, local_sem) = ins[0], outs[0], sems
        x_, y_, c_ = _mesh_pos()
        me, sibling = (x_, y_, c_), (x_, y_, 1 - c_)
        chips = [(1 - x_, y_), (x_, 1 - y_), (1 - x_, 1 - y_)]
        x_ref = x_ref.at[pl.ds(lo, n)]

        def rows(px, py, pc):
            return out_ref.at[4 * px + 2 * py + pc, pl.ds(lo, n)]

        def copy(k, block, to, src=None):
            return pltpu.make_async_remote_copy(
                src_ref=rows(*block) if src is None else src, dst_ref=rows(*block),
                send_sem=send_sems.at[k], recv_sem=recv_sems.at[k], device_id=to, device_id_type=MESH_ID)

        return dict(
            mine=lambda: pltpu.make_async_copy(x_ref, rows(*me), local_sem),
            first=lambda: [copy(0, me, sibling, src=x_ref)] + [copy(1 + j, me, (*chip, c_), src=x_ref) for j, chip in enumerate(chips)],
            passed=lambda: [copy(4 + j, (*chip, c_), sibling) for j, chip in enumerate(chips)],
            landed=lambda: [copy(1 + j, (*chip, c_), me) for j, chip in enumerate(chips)],
            last=lambda: [copy(0, sibling, me)] + [copy(4 + j, (*chip, 1 - c_), me) for j, chip in enumerate(chips)])

    def start(ins, outs, sems):
        p = parts(ins, outs, sems)
        p["mine"]().start()
        for cp in p["first"]():
            cp.start()

    def mid(ins, outs, sems):
        p = parts(ins, outs, sems)
        for cp, fw in zip(p["landed"](), p["passed"]()):
            cp.wait_recv()
            fw.start()

    def finish(ins, outs, sems):
        p = parts(ins, outs, sems)
        for cp in p["last"]():
            cp.wait_recv()
        for cp in p["first"]() + p["passed"]():
            cp.wait_send()
        p["mine"]().wait()

    return _Task([shard] if into is None else [shard, into], [jax.ShapeDtypeStruct((N_DEV,) + shard.shape, shard.dtype)],
                 [pltpu.SemaphoreType.DMA((7,)), pltpu.SemaphoreType.DMA((7,)), pltpu.SemaphoreType.DMA],
                 start, finish, mid, mid_step, alias=None if into is None else (1, 0))


def _swap_task(g8):
    _, R, C = g8.shape

    def copies(ins, outs, sems):
        (g_ref,), (recv_ref,), (ss, rs) = ins, outs, sems
        x_, y_, c_ = _mesh_pos()
        return [pltpu.make_async_remote_copy(src_ref=g_ref.at[2 * k + (1 - c_)], dst_ref=recv_ref.at[k], send_sem=ss.at[k],
                                             recv_sem=rs.at[k], device_id=(x_, y_, 1 - c_), device_id_type=MESH_ID)
                for k in range(4)]

    def start(ins, outs, sems):
        for cp in copies(ins, outs, sems):
            cp.start()

    def finish(ins, outs, sems):
        cps = copies(ins, outs, sems)
        for cp in cps:
            cp.wait_recv()
        for cp in cps:
            cp.wait_send()

    return _Task([g8], [jax.ShapeDtypeStruct((4, R, C), g8.dtype)],
                 [pltpu.SemaphoreType.DMA((4,)), pltpu.SemaphoreType.DMA((4,))], start, finish)


def _chip_sums(g8, recva):
    _, R, C = g8.shape
    rt = _sum_adamw_rows(R)
    rc = _row_chunk(rt)

    def body(core_ref, g_ref, a_ref, send_ref, own_ref):
        x_, y_, _ = _mesh_pos()
        chips = [(1 - x_, y_), (x_, 1 - y_), (1 - x_, 1 - y_), (x_, y_)]

        def chunk(i, carry):
            rows = pl.ds(pl.multiple_of(i * rc, rc), rc)
            for j, (tx, ty) in enumerate(chips):
                k = 2 * tx + ty
                s = g_ref[k, rows, :].astype(F32) + a_ref[k, rows, :].astype(F32)
                if j < 3:
                    send_ref[j, rows, :] = s.astype(BF16)
                else:
                    own_ref[rows, :] = s
            return carry

        lax.fori_loop(0, rt // rc, chunk, 0)

    core = lax.axis_index("c").astype(jnp.int32).reshape(1)
    return pl.pallas_call(
        body, name="chip_sums",
        grid_spec=pltpu.PrefetchScalarGridSpec(
            num_scalar_prefetch=1, grid=(R // rt,),
            in_specs=[pl.BlockSpec((4, None, rt, C), lambda i, c: (0, c[0], i, 0)), pl.BlockSpec((4, rt, C), lambda i, c: (0, i, 0))],
            out_specs=[pl.BlockSpec((3, rt, C), lambda i, c: (0, i, 0)), pl.BlockSpec((rt, C), lambda i, c: (i, 0))]),
        out_shape=[jax.ShapeDtypeStruct((3, R, C), BF16), jax.ShapeDtypeStruct((R, C), F32)],
        compiler_params=_params("parallel"),
    )(core, g8.reshape(4, 2, R, C), recva)


def _exchange_task(sendb):
    def copies(ins, outs, sems):
        (s_ref,), (recv_ref,), (ss, rs) = ins, outs, sems
        x_, y_, c_ = _mesh_pos()
        flips = [(1 - x_, y_), (x_, 1 - y_), (1 - x_, 1 - y_)]
        return [pltpu.make_async_remote_copy(src_ref=s_ref.at[j], dst_ref=recv_ref.at[j], send_sem=ss.at[j], recv_sem=rs.at[j],
                                             device_id=(tx, ty, c_), device_id_type=MESH_ID) for j, (tx, ty) in enumerate(flips)]

    def start(ins, outs, sems):
        for cp in copies(ins, outs, sems):
            cp.start()

    def finish(ins, outs, sems):
        cps = copies(ins, outs, sems)
        for cp in cps:
            cp.wait_recv()
        for cp in cps:
            cp.wait_send()

    return _Task([sendb], [jax.ShapeDtypeStruct(sendb.shape, sendb.dtype)],
                 [pltpu.SemaphoreType.DMA((3,)), pltpu.SemaphoreType.DMA((3,))], start, finish)


def _silu(v):
    return v * (1.0 / (1.0 + jnp.exp(-v)))


def _ada_fwd(c, taps, w_ada_l, b_ada, w_in_shard, casts, rpb_rev):
    n_seq = c.shape[0]
    D, cols = w_ada_l.shape
    F_l = taps.shape[1]
    W = D + F_l
    n_rows = N_DEV * n_seq
    n_c = len(casts)
    t_c = _gather_task(jax.ShapeDtypeStruct((8, W), F32), 0)
    t_w = _gather_task(jax.ShapeDtypeStruct(w_in_shard.shape, BF16), 0)
    t_m = _gather_task(jax.ShapeDtypeStruct((n_rows, cols), F32), 0)

    def body(c_ref, taps_ref, w_ref, b_ref, ws_ref, rev_ref, *refs):
        cast_in, refs = refs[:n_c], refs[n_c:]
        (slabs_ref, call_ref, taps_all_ref, mod_ref, win_ref), refs = refs[:5], refs[5:]
        cast_out, bias_ref, refs = refs[:n_c], refs[n_c], refs[n_c + 1:]
        (slab_vm, c_vm, m_vm, ws_f, ws_b), refs = refs[:5], refs[5:]
        stage_f, stage_b, refs = refs[:n_c], refs[n_c:2 * n_c], refs[2 * n_c:]
        copy_sem, in_sems, out_sems, sems = refs[0], refs[1], refs[2], refs[3:]
        sc, sw, sm = sems[0:3], sems[3:6], sems[6:9]
        x_, y_, c_ = _mesh_pos()
        slab_vm[...] = jnp.zeros_like(slab_vm)
        slab_vm[0:n_seq, 0:D] = c_ref[...]
        slab_vm[0:taps.shape[0], D:W] = taps_ref[...]
        t_c.start((slab_vm,), (slabs_ref,), sc)
        cp = pltpu.make_async_copy(ws_ref, ws_f, copy_sem)
        cp.start()
        cp.wait()
        ws_b[...] = ws_f[...].astype(BF16)
        t_w.start((ws_b,), (win_ref,), sw)
        loads = [pltpu.make_async_copy(cast_in[j], stage_f[j], in_sems.at[j]) for j in range(n_c)]
        for ld in loads:
            ld.start()
        t_c.mid((slab_vm,), (slabs_ref,), sc)
        t_c.finish((slab_vm,), (slabs_ref,), sc)
        cp = pltpu.make_async_copy(slabs_ref, c_vm, copy_sem)
        cp.start()
        cp.wait()
        c_all = c_vm[:, :, 0:D].reshape(N_DEV * 8, D)
        call_ref[...] = c_all
        for j in range(N_DEV):
            taps_all_ref[:, j * F_l:(j + 1) * F_l] = c_vm[j, 0:taps.shape[0], D:W]
        b_mine = b_ref[:, pl.ds(pl.multiple_of((4 * x_ + 2 * y_ + c_) * cols, 128), cols)]
        m64 = jnp.dot(_silu(c_all), w_ref[...], precision=lax.Precision.HIGHEST, preferred_element_type=F32) + b_mine
        r = lax.broadcasted_iota(jnp.int32, (n_rows, N_DEV * 8), 0)
        c = lax.broadcasted_iota(jnp.int32, (n_rows, N_DEV * 8), 1)
        pick = jnp.where(c == 8 * (r // n_seq) + r % n_seq, 1.0, 0.0)
        m_vm[...] = jnp.dot(pick, m64, precision=lax.Precision.HIGHEST, preferred_element_type=F32)
        t_m.start((m_vm,), (mod_ref,), sm)
        stores = [pltpu.make_async_copy(stage_b[j], cast_out[j], out_sems.at[j]) for j in range(n_c)]
        for j in range(n_c):
            loads[j].wait()
            stage_b[j][...] = stage_f[j][...].astype(BF16)
            stores[j].start()
        for p in range(NA_PAIRS):
            _na_bias_rows(rev_ref, bias_ref, p)
        t_w.mid((ws_b,), (win_ref,), sw)
        t_m.mid((m_vm,), (mod_ref,), sm)
        t_m.finish((m_vm,), (mod_ref,), sm)
        t_w.finish((ws_b,), (win_ref,), sw)
        for st in stores:
            st.wait()

    hbm, vm = pl.BlockSpec(memory_space=pl.ANY), pl.BlockSpec(memory_space=pltpu.VMEM)
    res = pl.pallas_call(
        body, name="ada_fwd", in_specs=[vm, vm, vm, vm, hbm, vm] + [hbm] * n_c,
        out_specs=[hbm, vm, vm, hbm, hbm] + [hbm] * n_c + [vm],
        out_shape=t_c.out_shapes + [jax.ShapeDtypeStruct((N_DEV * 8, D), F32), jax.ShapeDtypeStruct((taps.shape[0], N_DEV * F_l), F32)]
        + t_m.out_shapes + t_w.out_shapes
        + [jax.ShapeDtypeStruct(a.shape, BF16) for a in casts] + [jax.ShapeDtypeStruct((NA_PAIRS, N_DR * GRID_W, 128), F32)],
        scratch_shapes=[pltpu.VMEM((8, W), F32), pltpu.VMEM((N_DEV, 8, W), F32), pltpu.VMEM((n_rows, cols), F32),
                        pltpu.VMEM(w_in_shard.shape, F32), pltpu.VMEM(w_in_shard.shape, BF16)]
        + [pltpu.VMEM(a.shape, F32) for a in casts] + [pltpu.VMEM(a.shape, BF16) for a in casts]
        + [pltpu.SemaphoreType.DMA, pltpu.SemaphoreType.DMA((n_c,)), pltpu.SemaphoreType.DMA((n_c,))]
        + t_c.sems + t_w.sems + t_m.sems,
        compiler_params=pltpu.CompilerParams(vmem_limit_bytes=VMEM_LIMIT),
    )(c, taps, w_ada_l, b_ada, w_in_shard, rpb_rev, *casts)
    return res[1], res[2], res[3], res[4], res[5:5 + n_c], res[5 + n_c]


def _ada_bwd(c_all, packs, cols):
    def body(c_ref, d_ref, o_ref):
        x_, y_, c_ = _mesh_pos()
        mine = d_ref[:, pl.ds(pl.multiple_of((4 * x_ + 2 * y_ + c_) * cols, 128), cols)]
        o_ref[...] = lax.dot_general(_silu(c_ref[...]), mine, (((0,), (0,)), ((), ())),
                                     precision=lax.Precision.HIGHEST, preferred_element_type=F32)
    return pl.pallas_call(body, name="ada_bwd", out_shape=jax.ShapeDtypeStruct((c_all.shape[1], cols), F32),
                          compiler_params=pltpu.CompilerParams(vmem_limit_bytes=VMEM_LIMIT))(c_all, packs)


NA_PAIRS = NA_HEADS // 2
N_DR_PAD = 16


def _na_bias_rev(na_rpb):
    rev = jnp.pad(jnp.flip(na_rpb, axis=2), ((0, 0), (0, N_DR_PAD - N_DR), (0, GRID_W - N_DC)))
    return jnp.transpose(rev.reshape(NA_PAIRS, 2, N_DR_PAD, GRID_W), (0, 2, 1, 3)).reshape(NA_PAIRS, N_DR_PAD, 128)


def _na_bias_rows(r_ref, o_ref, p):
    k = lax.broadcasted_iota(jnp.int32, (GRID_W, 128), 0)
    lane = lax.broadcasted_iota(jnp.int32, (GRID_W, 128), 1)
    q = lane % GRID_W
    cs = jnp.clip(q - NA_COLS // 2, 0, GRID_W - NA_COLS)
    ok = (k >= cs) & (k < cs + NA_COLS)
    left = lane < GRID_W
    for dr in range(N_DR):
        row = jnp.broadcast_to(r_ref[p, dr:dr + 1, :], (GRID_W, 128))
        r0 = jnp.where(left, row, 0.0)
        r1 = jnp.where(left, pltpu.roll(row, GRID_W, axis=1), 0.0)
        y0 = pltpu.roll(r0, 128 - (NA_COLS - 1), axis=1, stride=1, stride_axis=0)
        y1 = pltpu.roll(r1, GRID_W - (NA_COLS - 1), axis=1, stride=1, stride_axis=0)
        o_ref[p, dr * GRID_W:(dr + 1) * GRID_W, :] = jnp.where(ok, jnp.where(left, y0, y1), NEG)


def _na_bias_grad(db, tasks=()):
    a = np.arange(128)
    flip = jnp.asarray(((a[:, None] // GRID_W == a[None, :] // GRID_W)
                        & (a[:, None] % GRID_W + a[None, :] % GRID_W == GRID_W - 1)).astype(np.float32))

    def body(d_ref, j_ref, o_ref):
        o_ref[...] = jnp.zeros_like(o_ref)
        for dr in range(N_DR):
            t = jnp.dot(d_ref[0, dr * GRID_W:(dr + 1) * GRID_W, :], j_ref[...], precision=lax.Precision.HIGHEST, preferred_element_type=F32)
            t = pltpu.roll(t, GRID_W + NA_COLS, axis=1, stride=1, stride_axis=0)
            o_ref[0, dr:dr + 1, :] = jnp.sum(t, axis=0, keepdims=True)

    (rows,), got = _hosted_call(
        body, "rpb_reduce", (NA_PAIRS,),
        [pl.BlockSpec((1, N_DR * GRID_W, 128), lambda p: (p, 0, 0)), _full((128, 128))],
        [pl.BlockSpec((1, N_DR_PAD, 128), lambda p: (p, 0, 0))],
        [jax.ShapeDtypeStruct((NA_PAIRS, N_DR_PAD, 128), F32)], (db, flip), tasks)
    g = rows.reshape(NA_PAIRS, N_DR_PAD, 2, GRID_W)[:, :N_DR, :, :N_DC]
    return jnp.transpose(g, (0, 2, 1, 3)).reshape(-1), got


def _rope_tables(S):
    half = HEAD_DIM // 2
    inv = np.float32(ROPE_THETA) ** (-np.arange(half, dtype=np.float32) / np.float32(half))
    ang = np.arange(S).astype(np.float32)[:, None] * inv[None, :]
    cos, sin = np.cos(ang).astype(np.float32), np.sin(ang).astype(np.float32)
    return jnp.asarray(np.tile(np.concatenate([cos, cos], axis=1), (1, 2))), jnp.asarray(np.tile(np.concatenate([-sin, sin], axis=1), (1, 2)))


def _rope_spec(tps, tm=TOKEN_TILE):
    return pl.BlockSpec((tm, 2 * HEAD_DIM), lambda i: (i % tps, 0))


def _rot_half(t):
    w = t.shape[1]
    lane = lax.broadcasted_iota(jnp.int32, t.shape, 1)
    return jnp.where((lane % HEAD_DIM) < HEAD_DIM // 2, pltpu.roll(t, w - HEAD_DIM // 2, axis=1),
                     pltpu.roll(t, HEAD_DIM // 2, axis=1))


def _tok_spec(w, tm=TOKEN_TILE):
    return pl.BlockSpec((tm, w), lambda i: (i, 0))


def _mod_spec(tps, d):
    return pl.BlockSpec((1, 6, d), lambda i: (i // tps, 0, 0))


def _bstat_spec(tps, w):
    return pl.BlockSpec((1, 8, w), lambda i: (i // tps, 0, 0))


def _attn_in(x2d, mod3, g_attn, w_in, cos_t, sin_t, S, tasks=(), tm=WIDE_TILE):
    T, D = x2d.shape
    tps = S // tm

    def body(x_ref, mod_ref, g_ref, w_ref, cos_ref, sin_ref, h_ref, qkv_ref):
        xn, _ = _rms(x_ref[...])
        h = (xn * g_ref[...]) * (1.0 + mod_ref[0, 1:2, :]) + mod_ref[0, 0:1, :]
        hb = h.astype(BF16)
        h_ref[...] = hb
        proj = _nt(hb, w_ref[...])
        rb = proj[:, ROPE_LO:ROPE_LO + ROPE_WIDTH]
        reps = (1, ROPE_WIDTH // (2 * HEAD_DIM))
        rb = rb * jnp.tile(cos_ref[...], reps) + _rot_half(rb) * jnp.tile(sin_ref[...], reps)
        qkv_ref[:, 0:NA_WIDTH] = (proj[:, 0:NA_WIDTH] * Q_SCALE).astype(BF16)
        qkv_ref[:, NA_WIDTH:ROPE_LO] = proj[:, NA_WIDTH:ROPE_LO].astype(BF16)
        qkv_ref[:, ROPE_LO:ROPE_LO + SW_WIDTH] = (rb[:, 0:SW_WIDTH] * Q_SCALE).astype(BF16)
        qkv_ref[:, ROPE_LO + SW_WIDTH:ROPE_LO + ROPE_WIDTH] = rb[:, SW_WIDTH:].astype(BF16)
        qkv_ref[:, ROPE_LO + ROPE_WIDTH:] = proj[:, ROPE_LO + ROPE_WIDTH:].astype(BF16)

    return _hosted_call(
        body, "attn_in", (T // tm,),
        [_tok_spec(D, tm), _mod_spec(tps, D), _full((1, D)), _full(w_in.shape), _rope_spec(tps, tm), _rope_spec(tps, tm)],
        [_tok_spec(D, tm), _tok_spec(IN_WIDTH, tm)],
        [jax.ShapeDtypeStruct((T, D), BF16), jax.ShapeDtypeStruct((T, IN_WIDTH), BF16)],
        (x2d, mod3, g_attn, w_in, cos_t, sin_t), tasks)


def _attn_out(oa, ob, x2d, mod3, g_na, g_sw, w_out, S, tasks=(), tm=WIDE_TILE):
    T, D = x2d.shape
    tps = S // tm

    def body(oa_ref, ob_ref, x_ref, mod_ref, gna_ref, gsw_ref, w_ref, mixin_ref, mix_ref, x1_ref):
        oan, _ = _rms(oa_ref[...])
        obn, _ = _rms(ob_ref[...])
        mixin = jnp.concatenate([oan * gna_ref[...], obn * gsw_ref[...]], axis=1).astype(BF16)
        mixin_ref[...] = mixin
        mix = _nn(mixin, w_ref[...])
        mix_ref[...] = mix
        x1_ref[...] = x_ref[...] + mod_ref[0, 2:3, :] * mix

    return _hosted_call(
        body, "attn_out", (T // tm,),
        [_tok_spec(NA_WIDTH, tm), _tok_spec(SW_WIDTH, tm), _tok_spec(D, tm), _mod_spec(tps, D),
         _full((1, NA_WIDTH)), _full((1, SW_WIDTH)), _full(w_out.shape)],
        [_tok_spec(NA_WIDTH + SW_WIDTH, tm), _tok_spec(D, tm), _tok_spec(D, tm)],
        [jax.ShapeDtypeStruct((T, NA_WIDTH + SW_WIDTH), BF16), jax.ShapeDtypeStruct((T, D), F32), jax.ShapeDtypeStruct((T, D), F32)],
        (oa, ob, x2d, mod3, g_na, g_sw, w_out), tasks)


def _ffn_up(x1, mod3, g_ffn, w_up, S, tasks=(), tm=WIDE_TILE):
    T, D = x1.shape
    F = w_up.shape[0] // 2
    tps = S // tm

    def body(x1_ref, mod_ref, g_ref, w_ref, h2_ref, val_ref, gt_ref):
        xn, _ = _rms(x1_ref[...])
        h2 = ((xn * g_ref[...]) * (1.0 + mod_ref[0, 4:5, :]) + mod_ref[0, 3:4, :]).astype(BF16)
        h2_ref[...] = h2
        u = _nt(h2, w_ref[...])
        val_ref[...] = u[:, :F].astype(BF16)
        gt_ref[...] = u[:, F:].astype(BF16)

    return _hosted_call(
        body, "ffn_up", (T // tm,), [_tok_spec(D, tm), _mod_spec(tps, D), _full((1, D)), _full(w_up.shape)],
        [_tok_spec(D, tm), _tok_spec(F, tm), _tok_spec(F, tm)],
        [jax.ShapeDtypeStruct((T, D), BF16), jax.ShapeDtypeStruct((T, F), BF16), jax.ShapeDtypeStruct((T, F), BF16)],
        (x1, mod3, g_ffn, w_up), tasks)


def _halo_specs(T, tps, w):
    per = TOKEN_TILE // 8
    prev = pl.BlockSpec((8, w), lambda i: (jnp.maximum(i * per - 1, 0), 0))
    nxt = pl.BlockSpec((8, w), lambda i: (jnp.minimum((i + 1) * per, T // 8 - 1), 0))
    return prev, nxt


def _seq_shifts(cur, before, after, ti, tps):
    tm = cur.shape[0]
    row = lax.broadcasted_iota(jnp.int32, cur.shape, 0)
    before = jnp.where(ti > 0, before.astype(F32), 0.0)
    after = jnp.where(ti < tps - 1, after.astype(F32), 0.0)
    return jnp.where(row == 0, before, pltpu.roll(cur, 1, axis=0)), jnp.where(row == tm - 1, after, pltpu.roll(cur, tm - 1, axis=0))


def _ffn_down(gt, val, conv_w, conv_b, w_down, x1, mod3, g_final, target, B, S):
    T, D = x1.shape
    F = gt.shape[1]
    tps = S // TOKEN_TILE
    prev, nxt = _halo_specs(T, tps, F)

    def body(gt_ref, prev_ref, next_ref, val_ref, cw_ref, cb_ref, w_ref, x1_ref, mod_ref, gf_ref, tgt_ref,
             a_ref, act_ref, vd_ref, dx2_ref, df_ref, gstat_ref, bstat_ref):
        i = pl.program_id(0)
        g = gt_ref[...].astype(F32)
        gprev, gnext = _seq_shifts(g, prev_ref[7:8, :], next_ref[0:1, :], i % tps, tps)
        gc = gprev * cw_ref[0:1, :] + g * cw_ref[1:2, :] + gnext * cw_ref[2:3, :] + cb_ref[...]
        sig = 1.0 / (1.0 + jnp.exp(-gc))
        act = gc * sig
        val = val_ref[...].astype(F32)
        act_ref[...] = act.astype(BF16)
        vd_ref[...] = (val * (sig + act - act * sig)).astype(BF16)
        a = (act * val).astype(BF16)
        a_ref[...] = a
        f = _nn(a, w_ref[...])
        gate = mod_ref[0, 5:6, :]
        x2 = x1_ref[...] + gate * f
        xn, r = _rms(x2)
        err = xn * gf_ref[...] - tgt_ref[...]
        dy = err * (1.0 / D)
        dx2 = _rms_bwd(xn, r, dy * gf_ref[...])
        dx2_ref[...] = dx2
        df_ref[...] = (gate * dx2).astype(BF16)

        @pl.when(i == 0)
        def _():
            gstat_ref[...] = jnp.zeros_like(gstat_ref)

        @pl.when(i % tps == 0)
        def _():
            bstat_ref[...] = jnp.zeros_like(bstat_ref)

        gstat_ref[0:1, :] += jnp.sum(dy * xn, axis=0, keepdims=True)
        tile_loss = jnp.sum(jnp.sum(err * err, axis=1, keepdims=True), axis=0, keepdims=True) * (0.5 / D)
        gstat_ref[1:2, :] += jnp.broadcast_to(tile_loss, (1, D))
        bstat_ref[0, 0:1, :] += jnp.sum(dx2 * f, axis=0, keepdims=True)

    return pl.pallas_call(
        body, name="ffn_down", grid=(T // TOKEN_TILE,),
        in_specs=[_tok_spec(F), prev, nxt, _tok_spec(F), _full(conv_w.shape), _full((1, F)), _full(w_down.shape),
                  _tok_spec(D), _mod_spec(tps, D), _full((1, D)), _tok_spec(D)],
        out_specs=[_tok_spec(F), _tok_spec(F), _tok_spec(F), _tok_spec(D), _tok_spec(D), _full((8, D)), _bstat_spec(tps, D)],
        out_shape=[jax.ShapeDtypeStruct((T, F), BF16), jax.ShapeDtypeStruct((T, F), BF16), jax.ShapeDtypeStruct((T, F), BF16),
                   jax.ShapeDtypeStruct((T, D), F32), jax.ShapeDtypeStruct((T, D), BF16),
                   jax.ShapeDtypeStruct((8, D), F32), jax.ShapeDtypeStruct((B, 8, D), F32)],
        compiler_params=_params("arbitrary"),
    )(gt, gt, gt, val, conv_w, conv_b, w_down, x1, mod3, g_final, target)


def _ffn_down_bwd(df, w_down, act, vd, tasks=(), tm=WIDE_TILE):
    T, D = df.shape
    F = act.shape[1]

    def body(df_ref, w_ref, act_ref, vd_ref, dval_ref, dgc_ref, cstat_ref):
        da = _nt(df_ref[...], w_ref[...])
        dval_ref[...] = (da * act_ref[...].astype(F32)).astype(BF16)
        dgc = da * vd_ref[...].astype(F32)
        dgc_ref[...] = dgc.astype(BF16)

        @pl.when(pl.program_id(0) == 0)
        def _():
            cstat_ref[...] = jnp.zeros_like(cstat_ref)

        cstat_ref[0:1, :] += jnp.sum(dgc, axis=0, keepdims=True)

    return _hosted_call(
        body, "ffn_down_bwd", (T // tm,),
        [_tok_spec(D, tm), _full(w_down.shape), _tok_spec(F, tm), _tok_spec(F, tm)],
        [_tok_spec(F, tm), _tok_spec(F, tm), _full((8, F))],
        [jax.ShapeDtypeStruct((T, F), BF16), jax.ShapeDtypeStruct((T, F), BF16), jax.ShapeDtypeStruct((8, F), F32)],
        (df, w_down, act, vd), tasks)


def _ffn_up_bwd(dgc, dval, gt, conv_w, w_up, x1, mod3, g_ffn, dx2, mix, B, S, tasks=()):
    T, D = x1.shape
    F = dgc.shape[1]
    tps = S // TOKEN_TILE
    prev, nxt = _halo_specs(T, tps, F)

    def body(dgc_ref, prev_ref, next_ref, dval_ref, gt_ref, cw_ref, w_ref, x1_ref, mod_ref, g_ref, dx2_ref, mix_ref,
             du_ref, dx1_ref, dmix_ref, gstat_ref, bstat_ref, cstat_ref):
        i = pl.program_id(0)
        d = dgc_ref[...].astype(F32)
        dprev, dnext = _seq_shifts(d, prev_ref[7:8, :], next_ref[0:1, :], i % tps, tps)
        g = gt_ref[...].astype(F32)

        @pl.when(i == 0)
        def _():
            cstat_ref[...] = jnp.zeros_like(cstat_ref)

        cstat_ref[1:2, :] += jnp.sum(dnext * g, axis=0, keepdims=True)
        cstat_ref[2:3, :] += jnp.sum(d * g, axis=0, keepdims=True)
        cstat_ref[3:4, :] += jnp.sum(dprev * g, axis=0, keepdims=True)
        dgt = dnext * cw_ref[0:1, :] + d * cw_ref[1:2, :] + dprev * cw_ref[2:3, :]
        du = jnp.concatenate([dval_ref[...], dgt.astype(BF16)], axis=1)
        du_ref[...] = du
        dh2 = _nn(du, w_ref[...])
        xn, r = _rms(x1_ref[...])
        scale1 = 1.0 + mod_ref[0, 4:5, :]
        xg = xn * g_ref[...]
        dx1 = dx2_ref[...] + _rms_bwd(xn, r, dh2 * g_ref[...] * scale1)
        dx1_ref[...] = dx1
        dmix_ref[...] = (mod_ref[0, 2:3, :] * dx1).astype(BF16)

        @pl.when(i == 0)
        def _():
            gstat_ref[...] = jnp.zeros_like(gstat_ref)

        @pl.when(i % tps == 0)
        def _():
            bstat_ref[...] = jnp.zeros_like(bstat_ref)

        gstat_ref[0:1, :] += jnp.sum(dh2 * scale1 * xn, axis=0, keepdims=True)
        bstat_ref[0, 0:1, :] += jnp.sum(dh2, axis=0, keepdims=True)
        bstat_ref[0, 1:2, :] += jnp.sum(dh2 * xg, axis=0, keepdims=True)
        bstat_ref[0, 2:3, :] += jnp.sum(dx1 * mix_ref[...], axis=0, keepdims=True)

    return _hosted_call(
        body, "ffn_up_bwd", (T // TOKEN_TILE,),
        [_tok_spec(F), prev, nxt, _tok_spec(F), _tok_spec(F), _full(conv_w.shape), _full(w_up.shape), _tok_spec(D),
         _mod_spec(tps, D), _full((1, D)), _tok_spec(D), _tok_spec(D)],
        [_tok_spec(2 * F), _tok_spec(D), _tok_spec(D), _full((8, D)), _bstat_spec(tps, D), _full((8, F))],
        [jax.ShapeDtypeStruct((T, 2 * F), BF16), jax.ShapeDtypeStruct((T, D), F32), jax.ShapeDtypeStruct((T, D), BF16),
         jax.ShapeDtypeStruct((8, D), F32), jax.ShapeDtypeStruct((B, 8, D), F32), jax.ShapeDtypeStruct((8, F), F32)],
        (dgc, dgc, dgc, dval, gt, conv_w, w_up, x1, mod3, g_ffn, dx2, mix), tasks)


def _attn_out_bwd(dmix, w_out, oa, ob, g_na, g_sw, tasks=(), tm=WIDE_TILE):
    T, D = dmix.shape

    def body(dmix_ref, w_ref, oa_ref, ob_ref, gna_ref, gsw_ref, doa_ref, dob_ref, gstat_ref):
        dmixin = _nt(dmix_ref[...], w_ref[...])

        @pl.when(pl.program_id(0) == 0)
        def _():
            gstat_ref[...] = jnp.zeros_like(gstat_ref)

        for k, (o_ref, g_ref, do_ref) in enumerate(((oa_ref, gna_ref, doa_ref), (ob_ref, gsw_ref, dob_ref))):
            dn = dmixin[:, k * NA_WIDTH:(k + 1) * NA_WIDTH]
            on, r = _rms(o_ref[...])
            gstat_ref[k:k + 1, :] += jnp.sum(dn * on, axis=0, keepdims=True)
            do_ref[...] = _rms_bwd(on, r, dn * g_ref[...]).astype(BF16)

    hs = jax.ShapeDtypeStruct((T, NA_WIDTH), BF16)
    return _hosted_call(
        body, "attn_out_bwd", (T // tm,),
        [_tok_spec(D, tm), _full(w_out.shape), _tok_spec(NA_WIDTH, tm), _tok_spec(SW_WIDTH, tm), _full((1, NA_WIDTH)), _full((1, SW_WIDTH))],
        [_tok_spec(NA_WIDTH, tm), _tok_spec(SW_WIDTH, tm), _full((8, NA_WIDTH))],
        [hs, hs, jax.ShapeDtypeStruct((8, NA_WIDTH), F32)],
        (dmix, w_out, oa, ob, g_na, g_sw), tasks)


def _attn_in_bwd(dqa, dka, dva, dqb, dkb, dvb, cos_t, sin_t, w_in, x2d, mod3, g_attn, dx1, B, S, tm=WIDE_TILE):
    T, D = x2d.shape
    tps = S // tm

    def body(dqa_ref, dka_ref, dva_ref, dqb_ref, dkb_ref, dvb_ref, cos_ref, sin_ref, w_ref, x_ref, mod_ref, g_ref, dx1_ref,
             gx_ref, dproj_ref, gstat_ref, bstat_ref):
        i = pl.program_id(0)
        drb = jnp.concatenate([dqb_ref[...] * Q_SCALE, dkb_ref[...]], axis=1).astype(F32)
        reps = (1, ROPE_WIDTH // (2 * HEAD_DIM))
        drb = drb * jnp.tile(cos_ref[...], reps) + _rot_half(drb * jnp.tile(sin_ref[...], reps))
        dproj = jnp.concatenate([(dqa_ref[...] * Q_SCALE).astype(BF16), dka_ref[...].astype(BF16), dva_ref[...].astype(BF16),
                                 drb.astype(BF16), dvb_ref[...].astype(BF16)], axis=1)
        dproj_ref[...] = dproj
        dh = _nn(dproj, w_ref[...])
        xn, r = _rms(x_ref[...])
        scale1 = 1.0 + mod_ref[0, 1:2, :]
        gx_ref[...] = dx1_ref[...] + _rms_bwd(xn, r, dh * g_ref[...] * scale1)

        @pl.when(i == 0)
        def _():
            gstat_ref[...] = jnp.zeros_like(gstat_ref)

        @pl.when(i % tps == 0)
        def _():
            bstat_ref[...] = jnp.zeros_like(bstat_ref)

        gstat_ref[0:1, :] += jnp.sum(dh * scale1 * xn, axis=0, keepdims=True)
        bstat_ref[0, 0:1, :] += jnp.sum(dh, axis=0, keepdims=True)
        bstat_ref[0, 1:2, :] += jnp.sum(dh * (xn * g_ref[...]), axis=0, keepdims=True)

    rope = _rope_spec(tps, tm)
    return pl.pallas_call(
        body, name="attn_in_bwd", grid=(T // tm,),
        in_specs=[_tok_spec(NA_WIDTH, tm), _tok_spec(NA_WIDTH, tm), _tok_spec(NA_WIDTH, tm), _tok_spec(SW_WIDTH, tm),
                  _tok_spec(SW_KV_WIDTH, tm), _tok_spec(SW_KV_WIDTH, tm), rope, rope, _full(w_in.shape), _tok_spec(D, tm),
                  _mod_spec(tps, D), _full((1, D)), _tok_spec(D, tm)],
        out_specs=[_tok_spec(D, tm), _tok_spec(IN_WIDTH, tm), _full((8, D)), _bstat_spec(tps, D)],
        out_shape=[jax.ShapeDtypeStruct((T, D), F32), jax.ShapeDtypeStruct((T, IN_WIDTH), BF16),
                   jax.ShapeDtypeStruct((8, D), F32), jax.ShapeDtypeStruct((B, 8, D), F32)],
        compiler_params=_params("arbitrary"),
    )(dqa, dka, dva, dqb, dkb, dvb, cos_t, sin_t, w_in, x2d, mod3, g_attn, dx1)


def _matmul_tn(a, b, name, tm=None, tk=512):
    T, M = a.shape
    N = b.shape[1]
    tm = M if tm is None else tm
    nk = T // tk

    def body(a_ref, b_ref, o_ref, acc):
        k = pl.program_id(1)

        @pl.when(k == 0)
        def _():
            acc[...] = jnp.zeros_like(acc)

        acc[...] += _tn(a_ref[...], b_ref[...])

        @pl.when(k == nk - 1)
        def _():
            o_ref[...] = acc[...].astype(BF16)

    return pl.pallas_call(
        body, name=name, grid=(M // tm, nk),
        in_specs=[pl.BlockSpec((tk, tm), lambda i, k: (k, i)), pl.BlockSpec((tk, N), lambda i, k: (k, 0))],
        out_specs=pl.BlockSpec((tm, N), lambda i, k: (i, 0)),
        out_shape=jax.ShapeDtypeStruct((M, N), BF16),
        scratch_shapes=[pltpu.VMEM((tm, N), F32)],
        compiler_params=_params("parallel", "arbitrary"),
    )(a, b)


def _na_geometry(S):
    rows = S // GRID_W
    wr = min(NA_ROWS_MAX, rows)
    return rows, wr


def _na_window(r, rows, wr):
    rs = jnp.clip(r - wr // 2, 0, rows - wr)
    return pl.multiple_of(rs * GRID_W, GRID_W), pl.multiple_of((rs - r + NA_ROWS_MAX - 1) * GRID_W, GRID_W)


NA_STEP_PAIRS = 2
NA_GW = NA_STEP_PAIRS * 128
NA_BWD_ROWS = 4
NA_ROWS_PER_STEP = 4


def _na_specs(S, kw_n, order):
    ng = NA_PAIRS // NA_STEP_PAIRS

    def col(k):
        return pl.BlockSpec((1, S, NA_GW), lambda *ids: (order(*ids)[0], 0, k * ng + order(*ids)[1]))
    bias = pl.BlockSpec((NA_STEP_PAIRS, N_DR * GRID_W, 128), lambda *ids: (order(*ids)[1], 0, 0))
    out = pl.BlockSpec((1, S, NA_GW), lambda *ids: (order(*ids)[0], 0, order(*ids)[1]))
    return col(0), col(1), col(2), bias, out


def _block_diag(t):
    left = lax.broadcasted_iota(jnp.int32, t.shape, 1) < HEAD_DIM
    zero = jnp.zeros_like(t)
    return jnp.concatenate([jnp.where(left, t, zero), jnp.where(left, zero, t)], axis=0)


def _diag_blocks(res):
    left = lax.broadcasted_iota(jnp.int32, (HEAD_DIM, 128), 1) < HEAD_DIM
    return jnp.where(left, res[:HEAD_DIM], res[HEAD_DIM:])


def _col_softmax(st):
    e = jnp.exp(st - jnp.max(st, axis=0, keepdims=True))
    return e * (1.0 / jnp.sum(e, axis=0, keepdims=True))


def _na_fwd(qkv, bias, tasks=()):
    B, S, _ = qkv.shape
    rows, wr = _na_geometry(S)
    kw_n = wr * GRID_W

    def body(q_ref, k_ref, v_ref, b_ref, o_ref):
        def step(it, carry):
            win = [_na_window(it * NA_ROWS_PER_STEP + u, rows, wr) for u in range(NA_ROWS_PER_STEP)]
            qrows = [pl.ds(pl.multiple_of((it * NA_ROWS_PER_STEP + u) * GRID_W, GRID_W), GRID_W) for u in range(NA_ROWS_PER_STEP)]
            krows = [pl.ds(w[0], kw_n) for w in win]
            brows = [pl.ds(w[1], kw_n) for w in win]
            lanes = [pl.ds(p * 128, 128) for p in range(NA_STEP_PAIRS)]
            chains = [(u, p) for u in range(NA_ROWS_PER_STEP) for p in range(NA_STEP_PAIRS)]
            st = {(u, p): _nt(k_ref[0, krows[u], lanes[p]], _block_diag(q_ref[0, qrows[u], lanes[p]])) for u, p in chains}
            pn = {(u, p): _col_softmax(st[(u, p)] + b_ref[p, brows[u], :]).astype(BF16) for u, p in chains}
            out = {(u, p): _diag_blocks(_tn(pn[(u, p)], v_ref[0, krows[u], lanes[p]])) for u, p in chains}
            for u in range(NA_ROWS_PER_STEP):
                o_ref[0, qrows[u], :] = jnp.concatenate([out[(u, p)] for p in range(NA_STEP_PAIRS)], axis=1)
            return carry

        lax.fori_loop(0, rows // NA_ROWS_PER_STEP, step, 0)

    q, k, v, bs, out = _na_specs(S, kw_n, lambda b, g: (b, g))
    return _hosted_call(body, "na_fwd", (B, NA_PAIRS // NA_STEP_PAIRS), [q, k, v, bs], [out],
                        [jax.ShapeDtypeStruct((B, S, NA_WIDTH), F32)], (qkv, qkv, qkv, bias), tasks)


def _na_bwd(qkv, bias, doa, tasks=()):
    B, S, _ = qkv.shape
    rows, wr = _na_geometry(S)
    kw_n = wr * GRID_W

    def body(q_ref, k_ref, v_ref, b_ref, do_ref, dq_ref, dk_ref, dv_ref, db_ref, dk_acc, dv_acc):
        @pl.when(pl.program_id(1) == 0)
        def _():
            db_ref[...] = jnp.zeros_like(db_ref)

        dk_acc[...] = jnp.zeros_like(dk_acc)
        dv_acc[...] = jnp.zeros_like(dv_acc)

        def step(it, carry):
            nu, pairs = range(NA_BWD_ROWS), range(NA_STEP_PAIRS)
            win = [_na_window(it * NA_BWD_ROWS + u, rows, wr) for u in nu]
            qrows = [pl.ds(pl.multiple_of((it * NA_BWD_ROWS + u) * GRID_W, GRID_W), GRID_W) for u in nu]
            krows = [pl.ds(w[0], kw_n) for w in win]
            brows = [pl.ds(w[1], kw_n) for w in win]
            lanes = [pl.ds(p * 128, 128) for p in pairs]
            chains = [(u, p) for u in nu for p in pairs]
            kp = {(u, p): k_ref[0, krows[u], lanes[p]] for u, p in chains}
            qbd = {(u, p): _block_diag(q_ref[0, qrows[u], lanes[p]]) for u, p in chains}
            dobd = {(u, p): _block_diag(do_ref[0, qrows[u], lanes[p]]) for u, p in chains}
            st = {c: _nt(kp[c], qbd[c]) for c in chains}
            dpt = {(u, p): _nt(v_ref[0, krows[u], lanes[p]], dobd[(u, p)]) for u, p in chains}
            pn = {(u, p): _col_softmax(st[(u, p)] + b_ref[p, brows[u], :]) for u, p in chains}
            dst = {c: pn[c] * (dpt[c] - jnp.sum(pn[c] * dpt[c], axis=0, keepdims=True)) for c in chains}
            dsb = {c: dst[c].astype(BF16) for c in chains}
            dq = {c: _diag_blocks(_tn(dsb[c], kp[c])) for c in chains}
            dk = {c: _nn(dsb[c], qbd[c]) for c in chains}
            dv = {c: _nn(pn[c].astype(BF16), dobd[c]) for c in chains}
            for u in nu:
                dq_ref[0, qrows[u], :] = jnp.concatenate([dq[(u, p)] for p in pairs], axis=1).astype(BF16)
                dk_acc[krows[u], :] += jnp.concatenate([dk[(u, p)] for p in pairs], axis=1)
                dv_acc[krows[u], :] += jnp.concatenate([dv[(u, p)] for p in pairs], axis=1)
                for p in pairs:
                    db_ref[p, brows[u], :] += dst[(u, p)]
            return carry

        lax.fori_loop(0, rows // NA_BWD_ROWS, step, 0)

        def emit(i, carry):
            r = pl.ds(pl.multiple_of(i * 256, 256), 256)
            dk_ref[0, r, :] = dk_acc[r, :].astype(BF16)
            dv_ref[0, r, :] = dv_acc[r, :].astype(BF16)
            return carry

        lax.fori_loop(0, S // 256, emit, 0)

    q, k, v, bs, out = _na_specs(S, kw_n, lambda g, b: (b, g))
    hs = jax.ShapeDtypeStruct((B, S, NA_WIDTH), BF16)
    return _hosted_call(body, "na_bwd", (NA_PAIRS // NA_STEP_PAIRS, B), [q, k, v, bs, out], [out, out, out, bs],
                        [hs, hs, hs, jax.ShapeDtypeStruct((NA_PAIRS, N_DR * GRID_W, 128), F32)], (qkv, qkv, qkv, bias, doa), tasks,
                        scratch_shapes=[pltpu.VMEM((S, NA_GW), F32), pltpu.VMEM((S, NA_GW), F32)])


SW_PAIRS = SW_HEADS // 2


def _sw_band(n, S):
    kw_n = 3 * SW_BLOCK
    start = pl.multiple_of(jnp.clip(n * SW_BLOCK - SW_BLOCK, 0, S - kw_n), SW_BLOCK)
    kpos = start + lax.broadcasted_iota(jnp.int32, (kw_n, SW_BLOCK), 0)
    qpos = n * SW_BLOCK + lax.broadcasted_iota(jnp.int32, (kw_n, SW_BLOCK), 1)
    return start, jnp.abs(qpos - kpos) <= SW_WINDOW


def _kv_halves(t):
    left = lax.broadcasted_iota(jnp.int32, t.shape, 1) < HEAD_DIM
    swapped = pltpu.roll(t, HEAD_DIM, axis=1)
    zero = jnp.zeros_like(t)
    return {(0, 0): jnp.where(left, t, zero), (0, 1): jnp.where(left, zero, swapped),
            (1, 0): jnp.where(left, swapped, zero), (1, 1): jnp.where(left, zero, t)}


def _sw_probs(st, ok, sk):
    st = jnp.where(ok, st, NEG)
    m = jnp.maximum(jnp.max(st, axis=0, keepdims=True), sk)
    e = jnp.exp(st - m)
    esk = jnp.exp(sk - m)
    inv = 1.0 / (jnp.sum(e, axis=0, keepdims=True) + esk)
    return e * inv, esk * inv


def _sw_specs(S):
    q = pl.BlockSpec((1, S, SW_WIDTH), lambda b, *_: (b, 0, ROPE_LO // SW_WIDTH))
    k = pl.BlockSpec((1, S, SW_KV_WIDTH), lambda b, *_: (b, 0, (ROPE_LO + SW_WIDTH) // SW_KV_WIDTH))
    v = pl.BlockSpec((1, S, SW_KV_WIDTH), lambda b, *_: (b, 0, (ROPE_LO + ROPE_WIDTH) // SW_KV_WIDTH))
    return q, k, v


SW_FWD_SPLIT = 2


def _sw_fwd(sink, qkv, tasks=()):
    B, S, _ = qkv.shape
    kw_n = 3 * SW_BLOCK

    def body(sink_ref, q_ref, k_ref, v_ref, o_ref):
        def step(n, carry):
            start, ok = _sw_band(n, S)
            qrows = pl.ds(pl.multiple_of(n * SW_BLOCK, SW_BLOCK), SW_BLOCK)
            krows = pl.ds(start, kw_n)
            kh, vh = _kv_halves(k_ref[0, krows, :]), _kv_halves(v_ref[0, krows, :])
            heads = [(p, e) for p in range(SW_PAIRS) for e in range(2)]
            qp = [q_ref[0, qrows, pl.ds(p * 128, 128)] for p in range(SW_PAIRS)]
            kv_of = lambda p: p // (SW_PAIRS // SW_KV_HEADS)
            st = {(p, e): _nt(kh[(kv_of(p), e)], qp[p]) for p, e in heads}
            pn = {(p, e): _sw_probs(st[(p, e)], ok, sink_ref[2 * p + e])[0].astype(BF16) for p, e in heads}
            outs = [_tn(pn[(p, 0)], vh[(kv_of(p), 0)]) + _tn(pn[(p, 1)], vh[(kv_of(p), 1)]) for p in range(SW_PAIRS)]
            o_ref[0, qrows, :] = jnp.concatenate(outs, axis=1)
            return carry

        half = (S // SW_BLOCK) // SW_FWD_SPLIT
        lax.fori_loop(pl.program_id(1) * half, (pl.program_id(1) + 1) * half, step, 0)

    q, k, v = _sw_specs(S)
    return _hosted_call(
        body, "sw_fwd", (B, SW_FWD_SPLIT), [pl.BlockSpec(memory_space=pltpu.SMEM), q, k, v],
        [pl.BlockSpec((1, S, SW_WIDTH), lambda b, s: (b, 0, 0))], [jax.ShapeDtypeStruct((B, S, SW_WIDTH), F32)],
        (sink, qkv, qkv, qkv), tasks)


def _sw_bwd(sink, qkv, dob):
    B, S, _ = qkv.shape
    kw_n = 3 * SW_BLOCK

    fold_rows = 256

    def body(sink_ref, q_ref, k_ref, v_ref, do_ref, dq_ref, dk_ref, dv_ref, dsink_ref, dk_acc, dv_acc):
        @pl.when(pl.program_id(0) == 0)
        def _():
            dsink_ref[...] = jnp.zeros_like(dsink_ref)

        dk_acc[...] = jnp.zeros_like(dk_acc)
        dv_acc[...] = jnp.zeros_like(dv_acc)
        ppk = SW_PAIRS // SW_KV_HEADS

        def step(n, carry):
            start, ok = _sw_band(n, S)
            qrows = pl.ds(pl.multiple_of(n * SW_BLOCK, SW_BLOCK), SW_BLOCK)
            krows = pl.ds(start, kw_n)
            kh, vh = _kv_halves(k_ref[0, krows, :]), _kv_halves(v_ref[0, krows, :])
            heads = [(p, e) for p in range(SW_PAIRS) for e in range(2)]
            qp = [q_ref[0, qrows, pl.ds(p * 128, 128)] for p in range(SW_PAIRS)]
            dop = [do_ref[0, qrows, pl.ds(p * 128, 128)] for p in range(SW_PAIRS)]
            st = {(p, e): _nt(kh[(p // ppk, e)], qp[p]) for p, e in heads}
            dpt = {(p, e): _nt(vh[(p // ppk, e)], dop[p]) for p, e in heads}
            pnb, dsb = {}, {}
            for p, e in heads:
                pn, psink = _sw_probs(st[(p, e)], ok, sink_ref[2 * p + e])
                delta = jnp.sum(pn * dpt[(p, e)], axis=0, keepdims=True)
                dsb[(p, e)] = (pn * (dpt[(p, e)] - delta)).astype(BF16)
                pnb[(p, e)] = pn.astype(BF16)
                dsink_ref[2 * p + e:2 * p + e + 1, :] += -(psink * delta)
            dq_ref[0, qrows, :] = jnp.concatenate(
                [_tn(dsb[(p, 0)], kh[(p // ppk, 0)]) + _tn(dsb[(p, 1)], kh[(p // ppk, 1)]) for p in range(SW_PAIRS)],
                axis=1).astype(BF16)
            left = lax.broadcasted_iota(jnp.int32, (kw_n, 128), 1) < HEAD_DIM
            dks, dvs = [], []
            for kv in range(SW_KV_HEADS):
                dk = dv = None
                for p in range(kv * ppk, (kv + 1) * ppk):
                    dk_p = jnp.where(left, _nn(dsb[(p, 0)], qp[p]), _nn(dsb[(p, 1)], qp[p]))
                    dv_p = jnp.where(left, _nn(pnb[(p, 0)], dop[p]), _nn(pnb[(p, 1)], dop[p]))
                    dk = dk_p if dk is None else dk + dk_p
                    dv = dv_p if dv is None else dv + dv_p
                dks.append(dk)
                dvs.append(dv)
            dk_acc[krows, :] += jnp.concatenate(dks, axis=1)
            dv_acc[krows, :] += jnp.concatenate(dvs, axis=1)
            return carry

        lax.fori_loop(0, S // SW_BLOCK, step, 0)

        def fold(i, carry):
            rows = pl.ds(pl.multiple_of(i * fold_rows, fold_rows), fold_rows)
            left = lax.broadcasted_iota(jnp.int32, (fold_rows, 128), 1) < HEAD_DIM
            for acc, out_ref in ((dk_acc, dk_ref), (dv_acc, dv_ref)):
                a, b = acc[rows, 0:128], acc[rows, 128:256]
                out_ref[0, rows, :] = jnp.where(left, a + pltpu.roll(a, HEAD_DIM, axis=1),
                                                b + pltpu.roll(b, HEAD_DIM, axis=1)).astype(BF16)
            return carry

        lax.fori_loop(0, S // fold_rows, fold, 0)

        @pl.when(pl.program_id(0) == B - 1)
        def _():
            dsink_ref[...] = jnp.broadcast_to(jnp.sum(dsink_ref[...], axis=1, keepdims=True), dsink_ref.shape)

    q, k, v = _sw_specs(S)
    qo = pl.BlockSpec((1, S, SW_WIDTH), lambda b: (b, 0, 0))
    ko = pl.BlockSpec((1, S, SW_KV_WIDTH), lambda b: (b, 0, 0))
    return pl.pallas_call(
        body, name="sw_bwd", grid=(B,),
        in_specs=[pl.BlockSpec(memory_space=pltpu.SMEM), q, k, v, qo],
        out_specs=[qo, ko, ko, _full((SW_HEADS, 128))],
        out_shape=[jax.ShapeDtypeStruct((B, S, SW_WIDTH), BF16), jax.ShapeDtypeStruct((B, S, SW_KV_WIDTH), BF16),
                   jax.ShapeDtypeStruct((B, S, SW_KV_WIDTH), BF16), jax.ShapeDtypeStruct((SW_HEADS, 128), F32)],
        scratch_shapes=[pltpu.VMEM((S, 2 * SW_KV_WIDTH), F32), pltpu.VMEM((S, 2 * SW_KV_WIDTH), F32)],
        compiler_params=_params("arbitrary"),
    )(sink, qkv, qkv, qkv, dob)


def _pack_sum_adamw(packs, params, pick):
    W = packs.shape[1]
    n_p = len(params)

    def body(p_ref, *refs):
        ins, tot_ref, pick_ref, outs = refs[:3 * n_p], refs[3 * n_p], refs[3 * n_p + 1], refs[3 * n_p + 2:]
        tot = p_ref[0:8, :]
        for d in range(1, N_DEV):
            tot = tot + p_ref[8 * d:8 * d + 8, :]
        tot_ref[...] = tot
        pick_ref[...] = tot[pick[0]:pick[0] + 1, pick[1]:pick[1] + 1]
        for i, (w, _, _, rows, off) in enumerate(params):
            w_ref, m_ref, v_ref = ins[3 * i:3 * i + 3]
            g_ref, d_ref, nm_ref, nv_ref = outs[4 * i:4 * i + 4]
            if w.ndim == 3:
                n_a, n_b, n = w.shape
                for a in range(n_a):
                    for b in range(n_b):
                        o = off + (b * n_a + a) * n
                        g_ref[a, b:b + 1, :] = tot_ref[rows[0]:rows[0] + 1, o:o + n]
                g = g_ref[...]
            else:
                n = w.shape[1]
                g = tot[rows[0]:rows[0] + 1, off:off + n]
                for r in rows[1:]:
                    g = g + tot[r:r + 1, off:off + n]
                g_ref[...] = g
            d_ref[...], nm_ref[...], nv_ref[...] = _adam_update(w_ref[...], g, m_ref[...], v_ref[...])

    res = pl.pallas_call(
        body, name="small_adamw",
        out_shape=[jax.ShapeDtypeStruct((8, W), F32), jax.ShapeDtypeStruct((1, 1), F32)]
        + [jax.ShapeDtypeStruct(p[0].shape, F32) for p in params for _ in range(4)],
        compiler_params=pltpu.CompilerParams(vmem_limit_bytes=VMEM_LIMIT),
    )(packs, *[a for p in params for a in p[:3]])
    return res[0], res[1], [res[2 + 4 * i:6 + 4 * i] for i in range(n_p)]


def _adam_update(w, g, m, v):
    c1 = 1.0 - ADAM_B1 ** ADAM_STEP
    c2 = 1.0 - ADAM_B2 ** ADAM_STEP
    nm = ADAM_B1 * m + (1.0 - ADAM_B1) * g
    nv = ADAM_B2 * v + (1.0 - ADAM_B2) * (g * g)
    return -ADAM_LR * ((nm / c1) / (jnp.sqrt(nv / c2) + ADAM_EPS) + ADAM_WD * w), nm, nv


def _adamw(w, g, m, v, name):
    def body(w_ref, g_ref, m_ref, v_ref, d_ref, nm_ref, nv_ref):
        d_ref[...], nm_ref[...], nv_ref[...] = _adam_update(w_ref[...], g_ref[...], m_ref[...], v_ref[...])

    s = jax.ShapeDtypeStruct(w.shape, F32)
    return pl.pallas_call(body, name=name, out_shape=[s, s, s],
                          compiler_params=pltpu.CompilerParams(vmem_limit_bytes=VMEM_LIMIT))(w, g, m, v)


def _sum_adamw_rows(R):
    return max(r for r in range(16, min(R, 256) + 1, 16) if R % r == 0)


def _sum_adamw_steps(R):
    return R // _sum_adamw_rows(R)


def _sum_first(own, recvb, name, tasks):
    R, C = own.shape
    rc = _sum_adamw_rows(R)

    def body(own_ref, r_ref, p_ref):
        p_ref[...] = (own_ref[...] + r_ref[0].astype(F32)) + r_ref[1].astype(F32)

    blk = pl.BlockSpec((rc, C), lambda i: (i, 0))
    (part,), got = _hosted_call(body, name, (R // rc,), [blk, pl.BlockSpec((2, rc, C), lambda i: (0, i, 0))], [blk],
                                [jax.ShapeDtypeStruct((R, C), F32)], (own, recvb), tasks)
    return part, got


def _sum_adamw(own, recvb, w, m, v, name, done=0):
    R, C = own.shape
    rc = _sum_adamw_rows(R)
    left = 3 - done
    assert 3 % left == 0

    def body(own_ref, r_ref, w_ref, m_ref, v_ref, g_ref, d_ref, nm_ref, nv_ref):
        g = own_ref[...]
        for j in range(left):
            g = g + r_ref[j].astype(F32)
        g_ref[...] = g
        d_ref[...], nm_ref[...], nv_ref[...] = _adam_update(w_ref[...], g, m_ref[...], v_ref[...])

    blk = pl.BlockSpec((rc, C), lambda i: (i, 0))
    s = jax.ShapeDtypeStruct((R, C), F32)
    return pl.pallas_call(
        body, name=name, grid=(R // rc,),
        in_specs=[blk, pl.BlockSpec((left, rc, C), lambda i: (done // left, i, 0)), blk, blk, blk],
        out_specs=[blk, blk, blk, blk], out_shape=[s, s, s, s], compiler_params=_params("parallel"),
    )(own, recvb, w, m, v)


def _by_device(dw):
    return dw.reshape(N_DEV, dw.shape[0] // N_DEV, dw.shape[1])


def _local_step(x, mod, g_attn, w_in, bias, sw_sink, g_na_out, g_sw_out, w_out, g_ffn, w_up, conv_w, conv_b, w_down,
                g_final, target, sharded):
    B, S, D = x.shape
    T = B * S
    x2d = x.reshape(T, D)
    mod3 = mod.reshape(B, 6, D)
    cos_t, sin_t = _rope_tables(S)
    sink = sw_sink.reshape(SW_HEADS)
    n_tiles = T // WIDE_TILE
    full = lambda g: g.reshape(N_DEV * g.shape[1], g.shape[2])

    rider = lambda w, mid, lo, n, into=None: [_gather_task(w, mid, rows=(lo, n), into=into)] if sharded else []
    if sharded:
        qu, hd = w_up.shape[0] // 4, w_down.shape[0] // 2
    (h, qkv), got = _attn_in(x2d, mod3, g_attn, w_in, cos_t, sin_t, S,
                             [_gather_task(w_out, n_tiles // 2)] + rider(w_up, n_tiles - 1, 0, qu) if sharded else [])
    if sharded:
        w_out, w_up_buf = full(got[0][0]), got[1][0]
    qkv3 = qkv.reshape(B, S, IN_WIDTH)
    na_steps, sw_steps = B * (NA_PAIRS // NA_STEP_PAIRS), B * SW_FWD_SPLIT
    (oa,), got = _na_fwd(qkv3, bias, rider(w_up, na_steps - 1, qu, 2 * qu, w_up_buf) if sharded else [])
    if sharded:
        w_up_buf = got[0][0]
    oa = oa.reshape(T, NA_WIDTH)
    (ob,), got = _sw_fwd(sink, qkv3, rider(w_up, sw_steps // 2, 3 * qu, qu, w_up_buf) if sharded else [])
    if sharded:
        w_up = full(got[0][0])
    ob = ob.reshape(T, SW_WIDTH)
    (mixin, mix, x1), _ = _attn_out(oa, ob, x2d, mod3, g_na_out, g_sw_out, w_out, S)
    (h2, val, gt), got = _ffn_up(x1, mod3, g_ffn, w_up, S, rider(w_down, 3 * n_tiles // 4, 0, 2 * hd) if sharded else [])
    if sharded:
        w_down = full(got[0][0])
    a, act, vd, dx2, df, gstat_f, bstat_f = _ffn_down(gt, val, conv_w, conv_b, w_down, x1, mod3, g_final, target.reshape(T, D), B, S)
    F = val.shape[1]

    dw_down = _matmul_tn(a, df, "dw_down")
    (dval, dgc, cstat), got = _ffn_down_bwd(df, w_down, act, vd, [_swap_task(_by_device(dw_down))] if sharded else [])
    if sharded:
        send_down, own_down = _chip_sums(_by_device(dw_down), got[0][0])
    (du, dx1, dmix, gstat_u, bstat_u, cstat_w), got = _ffn_up_bwd(dgc, dval, gt, conv_w, w_up, x1, mod3, g_ffn, dx2, mix, B, S,
                                                                  [_exchange_task(send_down)] if sharded else [])
    if sharded:
        dw_down = (own_down, got[0][0])
    dw_up = _matmul_tn(du, h2, "dw_up", tm=F)
    dw_out = _matmul_tn(mixin, dmix, "dw_out")
    (doa, dob, gstat_o), got = _attn_out_bwd(dmix, w_out, oa, ob, g_na_out, g_sw_out,
                                             [_swap_task(_by_device(dw_up)), _swap_task(_by_device(dw_out))] if sharded else [])
    if sharded:
        send_up, own_up = _chip_sums(_by_device(dw_up), got[0][0])
        send_out, own_out = _chip_sums(_by_device(dw_out), got[1][0])
    (dqa, dka, dva, dbt), got = _na_bwd(qkv3, bias, doa.reshape(B, S, NA_WIDTH),
                                        [_exchange_task(send_up), _exchange_task(send_out)] if sharded else [])
    if sharded:
        dw_up, dw_out = (own_up, got[0][0]), (own_out, got[1][0])
    dqb, dkb, dvb, dsink = _sw_bwd(sink, qkv3, dob.reshape(B, S, SW_WIDTH))
    r2 = lambda t: t.reshape(T, t.shape[-1])
    grad_x, dproj, gstat_i, bstat_i = _attn_in_bwd(r2(dqa), r2(dka), r2(dva), r2(dqb), r2(dkb), r2(dvb), cos_t, sin_t, w_in, x2d, mod3,
                                                   g_attn, dx1, B, S)
    dw_in = _matmul_tn(dproj, h, "dw_in")

    dmod = jnp.stack([bstat_i[:, 0], bstat_i[:, 1], bstat_u[:, 2], bstat_u[:, 0], bstat_u[:, 1], bstat_f[:, 0]], axis=1)
    small = dict(g_attn=gstat_i[0], g_ffn=gstat_u[0], g_final=gstat_f[0], loss=gstat_f[1, 0], g_na_out=gstat_o[0], g_sw_out=gstat_o[1],
                 sw_sink=dsink[:, 0], conv_b=cstat[0], conv_w=cstat_w[1:4], dbt=dbt,
                 raw=(bstat_i, bstat_u, bstat_f, gstat_i, gstat_u, gstat_f, gstat_o, dsink, cstat, cstat_w))
    return grad_x.reshape(B, S, D), dict(w_in=dw_in, w_out=dw_out, w_up=dw_up, w_down=dw_down), dmod, small


def _pack_slab(raw, drpb):
    D, F = raw[3].shape[1], raw[8].shape[1]
    n_seq = raw[0].shape[0]

    def body(bi_ref, bu_ref, bf_ref, gi_ref, gu_ref, gf_ref, go_ref, ds_ref, cs_ref, cw_ref, rp_ref, o_ref):
        o_ref[...] = jnp.zeros_like(o_ref)
        for b in range(n_seq):
            mods = (bi_ref[b, 0:1, :], bi_ref[b, 1:2, :], bu_ref[b, 2:3, :], bu_ref[b, 0:1, :], bu_ref[b, 1:2, :], bf_ref[b, 0:1, :])
            for k, row in enumerate(mods):
                o_ref[b:b + 1, k * D:(k + 1) * D] = row
        o = 0
        for row in (gi_ref[0:1, :], gu_ref[0:1, :], gf_ref[0:1, :], go_ref[0:1, :], go_ref[1:2, :]):
            o_ref[2:3, o:o + row.shape[1]] = row
            o += row.shape[1]
        ds = ds_ref[...]
        eye = lax.broadcasted_iota(jnp.int32, ds.shape, 0) == lax.broadcasted_iota(jnp.int32, ds.shape, 1)
        o_ref[2:3, o:o + 128] = jnp.sum(jnp.where(eye, ds, 0.0), axis=0, keepdims=True)
        o_ref[2:3, o + 128:o + 256] = gf_ref[1:2, 0:128]
        o_ref[3:4, 0:F] = cs_ref[0:1, :]
        o_ref[4:5, 0:rp_ref.shape[1]] = rp_ref[...]
        o_ref[5:8, 0:F] = cw_ref[1:4, :]

    return pl.pallas_call(body, name="pack_slab", out_shape=jax.ShapeDtypeStruct((8, PACK_W), F32),
                          compiler_params=pltpu.CompilerParams(vmem_limit_bytes=VMEM_LIMIT))(*raw, drpb)


def kernel(x, c, w_ada, b_ada, g_attn, w_in, na_rpb, sw_sink, g_na_out, g_sw_out, w_out, g_ffn, w_up, conv_w, conv_b, w_down, g_final, loss_target, m_w_ada, m_b_ada, m_g_attn, m_w_in, m_na_rpb, m_sw_sink, m_g_na_out, m_g_sw_out, m_w_out, m_g_ffn, m_w_up, m_conv_w, m_conv_b, m_w_down, m_g_final, v_w_ada, v_b_ada, v_g_attn, v_w_in, v_na_rpb, v_sw_sink, v_g_na_out, v_g_sw_out, v_w_out, v_g_ffn, v_w_up, v_conv_w, v_conv_b, v_w_down, v_g_final):
    B, S, D = x.shape
    me = 4 * lax.axis_index("x") + 2 * lax.axis_index("y") + lax.axis_index("c")
    ada_c = w_ada.shape[2]
    F_l = conv_w.shape[2]

    tr = {"w_in", "w_up"}
    w_in_t = jnp.transpose(w_in[0])
    shards_f = dict(w_out=w_out[0], w_up=jnp.transpose(w_up[0]), w_down=w_down[0])

    c_all, conv_w_f, mod_all, w_in_all, casts, bias = _ada_fwd(c, conv_w[0], w_ada[0], b_ada, w_in_t, list(shards_f.values()),
                                                               _na_bias_rev(na_rpb[0]))
    shards = dict(zip(shards_f, casts))
    mod_mine = lax.dynamic_slice(mod_all, (0, me * B, 0), (N_DEV, B, ada_c))
    mod = jnp.transpose(mod_mine, (1, 0, 2)).reshape(B, N_DEV * ada_c)
    w_in_f = w_in_all.reshape(N_DEV * w_in_t.shape[0], D)

    grad_x, dw, dmod, small = _local_step(x, mod, g_attn, w_in_f, bias, sw_sink, g_na_out, g_sw_out, shards["w_out"], g_ffn,
                                          shards["w_up"], conv_w_f, conv_b, shards["w_down"], g_final.reshape(1, D), loss_target,
                                          sharded=True)
    g8_in = _by_device(dw["w_in"])
    drpb, got = _na_bias_grad(small["dbt"], [_swap_task(g8_in)])
    send_in, own_in = _chip_sums(g8_in, got[0][0])

    slab = _pack_slab(small["raw"], drpb.reshape(1, -1))
    weights = dict(w_ada=w_ada, b_ada=b_ada, g_attn=g_attn, w_in=w_in, na_rpb=na_rpb, sw_sink=sw_sink, g_na_out=g_na_out,
                   g_sw_out=g_sw_out, w_out=w_out, g_ffn=g_ffn, w_up=w_up, conv_w=conv_w, conv_b=conv_b, w_down=w_down, g_final=g_final)
    ms = dict(w_ada=m_w_ada, b_ada=m_b_ada, g_attn=m_g_attn, w_in=m_w_in, na_rpb=m_na_rpb, sw_sink=m_sw_sink, g_na_out=m_g_na_out,
              g_sw_out=m_g_sw_out, w_out=m_w_out, g_ffn=m_g_ffn, w_up=m_w_up, conv_w=m_conv_w, conv_b=m_conv_b, w_down=m_w_down, g_final=m_g_final)
    vs = dict(w_ada=v_w_ada, b_ada=v_b_ada, g_attn=v_g_attn, w_in=v_w_in, na_rpb=v_na_rpb, sw_sink=v_sw_sink, g_na_out=v_g_na_out,
              g_sw_out=v_g_sw_out, w_out=v_w_out, g_ffn=v_g_ffn, w_up=v_w_up, conv_w=v_conv_w, conv_b=v_conv_b, w_down=v_w_down, g_final=v_g_final)
    names = list(weights)
    grads, deltas, new_m, new_v = {}, {}, {}, {}
    flat = lambda t: t.reshape(1, -1)

    def shard2d(nm):
        if nm in tr:
            return (lambda t: jnp.transpose(t[0])), (lambda t: jnp.transpose(t)[None])
        if nm == "conv_w":
            return (lambda t: jnp.transpose(t, (1, 0, 2))), (lambda t: jnp.transpose(t, (1, 0, 2)))
        return (lambda t: t[0]), (lambda t: t[None])

    def finish_sum(nm, own, recvb, done=0):
        r, back = shard2d(nm)
        g2, d_, m_, v_ = _sum_adamw(own, recvb, r(weights[nm]), r(ms[nm]), r(vs[nm]), "adamw_" + nm, done)
        grads[nm], deltas[nm], new_m[nm], new_v[nm] = back(g2), back(d_), back(m_), back(v_)

    own_up, recv_up = dw["w_up"]
    n_up = _sum_adamw_steps(own_up.shape[0])
    part_up, got = _sum_first(own_up, recv_up, "sum_first_w_up", [_gather_task(slab, n_up - 1), _exchange_task(send_in)])
    packs = got[0][0]
    finish_sum("w_up", part_up, recv_up, done=2)
    finish_sum("w_down", *dw["w_down"])
    finish_sum("w_in", own_in, got[1][0])
    finish_sum("w_out", *dw["w_out"])

    where = dict(b_ada=((0, 1), 0), g_attn=((2,), 0), g_ffn=((2,), D), g_final=((2,), 2 * D), g_na_out=((2,), 3 * D),
                 g_sw_out=((2,), 3 * D + NA_WIDTH), sw_sink=((2,), 3 * D + NA_WIDTH + SW_WIDTH), conv_b=((3,), 0), na_rpb=((4,), 0))

    def small_view(n):
        if n == "na_rpb":
            return (lambda t: jnp.transpose(t[0], (1, 0, 2))), (lambda t: jnp.transpose(t, (1, 0, 2))[None])
        return flat, (lambda t: t.reshape(weights[n].shape))

    tot, loss, small_out = _pack_sum_adamw(
        packs.reshape(N_DEV * 8, PACK_W),
        [tuple(small_view(n)[0](t[n]) for t in (weights, ms, vs)) + where[n] for n in where],
        (2, 3 * D + NA_WIDTH + SW_WIDTH + 128))
    for n, res in zip(where, small_out):
        grads[n], deltas[n], new_m[n], new_v[n] = [small_view(n)[1](t) for t in res]
    loss = loss.reshape(())

    for nm, g2 in (("w_ada", _ada_bwd(c_all, packs.reshape(N_DEV * 8, PACK_W), ada_c)), ("conv_w", lax.dynamic_slice(tot, (5, me * F_l), (3, F_l))[:, None])):
        r, back = shard2d(nm)
        d_, m_, v_ = _adamw(r(weights[nm]), g2, r(ms[nm]), r(vs[nm]), "adamw_" + nm)
        grads[nm], deltas[nm], new_m[nm], new_v[nm] = back(g2), back(d_), back(m_), back(v_)
    return (loss, grad_x, *[grads[n] for n in names], *[deltas[n] for n in names], *[new_m[n] for n in names],
            *[new_v[n] for n in names])
```

```python
import functools

import numpy as np
import jax
import jax.numpy as jnp
from jax import lax
from jax.experimental import pallas as pl
from jax.experimental.pallas import tpu as pltpu

F32, BF16 = jnp.float32, jnp.bfloat16
MESH_ID = pl.DeviceIdType.MESH
N_DEV = 8

HEAD_DIM = 64
NA_HEADS = 8
SW_HEADS = 8
SW_KV_HEADS = 2
SW_GROUP = SW_HEADS // SW_KV_HEADS
NA_WIDTH = NA_HEADS * HEAD_DIM
SW_WIDTH = SW_HEADS * HEAD_DIM
SW_KV_WIDTH = SW_KV_HEADS * HEAD_DIM
ROPE_WIDTH = SW_WIDTH + SW_KV_WIDTH
IN_WIDTH = 3 * NA_WIDTH + SW_WIDTH + 2 * SW_KV_WIDTH
ROPE_LO = 3 * NA_WIDTH
GRID_W = 64
NA_ROWS_MAX = 8
NA_COLS = 16
N_DR = 2 * NA_ROWS_MAX - 1
N_DC = 2 * NA_COLS - 1
SW_WINDOW = 128
SW_BLOCK = 128
ROPE_THETA = 10000.0
EPS = 1e-6
NEG = -1e30
Q_SCALE = HEAD_DIM ** -0.5

ADAM_LR = 0.001
ADAM_B1 = 0.9
ADAM_B2 = 0.999
ADAM_EPS = 1e-08
ADAM_WD = 0.01
ADAM_STEP = 10

TOKEN_TILE = 256
WIDE_TILE = 512
VMEM_LIMIT = 56 * 1024 * 1024

PACK_W = 6144


def _nn(a, b):
    return jnp.dot(a, b, preferred_element_type=F32)


def _nt(a, b):
    return lax.dot_general(a, b, (((1,), (1,)), ((), ())), preferred_element_type=F32)


def _tn(a, b):
    return lax.dot_general(a, b, (((0,), (0,)), ((), ())), preferred_element_type=F32)


def _rms(x):
    r = lax.rsqrt(jnp.mean(x * x, axis=-1, keepdims=True) + EPS)
    return x * r, r


def _rms_bwd(xn, r, gy):
    return r * (gy - xn * jnp.mean(xn * gy, axis=-1, keepdims=True))


def _params(*sem):
    return pltpu.CompilerParams(dimension_semantics=sem, vmem_limit_bytes=VMEM_LIMIT)


def _full(shape):
    n = len(shape)
    return pl.BlockSpec(shape, lambda *_: (0,) * n)


def _mesh_pos():
    return lax.axis_index("x"), lax.axis_index("y"), lax.axis_index("c")


def _row_chunk(r):
    for rc in (128, 64, 32, 16):
        if r % rc == 0:
            return rc
    raise ValueError(f"rows {r} not a multiple of 16")


class _Task:
    def __init__(self, inputs, out_shapes, sems, start, finish, mid=None, mid_step=None, alias=None):
        self.inputs, self.out_shapes, self.sems = list(inputs), list(out_shapes), list(sems)
        self.start, self.finish, self.mid, self.mid_step = start, finish, mid, mid_step
        self.alias = alias


def _hosted_call(body, name, grid, in_specs, out_specs, out_shape, operands, tasks, scratch_shapes=()):
    n_in, n_out, n_scr = len(in_specs), len(out_specs), len(scratch_shapes)
    t_in = [len(t.inputs) for t in tasks]
    t_out = [len(t.out_shapes) for t in tasks]
    t_sem = [len(t.sems) for t in tasks]
    n_steps = int(np.prod(grid))

    def wrapped(*refs):
        ins, rest = refs[:n_in], refs[n_in:]
        task_ins, rest = rest[:sum(t_in)], rest[sum(t_in):]
        outs, rest = rest[:n_out], rest[n_out:]
        task_outs, rest = rest[:sum(t_out)], rest[sum(t_out):]
        scr, task_sems = rest[:n_scr], rest[n_scr:]
        step = pl.program_id(0)
        for ax in range(1, len(grid)):
            step = step * grid[ax] + pl.program_id(ax)
        parts = []
        oi = oo = os_ = 0
        for t, a, b, c in zip(tasks, t_in, t_out, t_sem):
            parts.append((t, task_ins[oi:oi + a], task_outs[oo:oo + b], task_sems[os_:os_ + c]))
            oi, oo, os_ = oi + a, oo + b, os_ + c
        for t, ti, to, ts in parts:
            pl.when(step == 0)(functools.partial(t.start, ti, to, ts))
            if t.mid is not None:
                pl.when(step == t.mid_step)(functools.partial(t.mid, ti, to, ts))
        body(*ins, *outs, *scr)
        for t, ti, to, ts in parts:
            pl.when(step == n_steps - 1)(functools.partial(t.finish, ti, to, ts))

    hbm = pl.BlockSpec(memory_space=pl.ANY)
    aliases, oi, oo = {}, n_in, n_out
    for t, a, b in zip(tasks, t_in, t_out):
        if t.alias is not None:
            aliases[oi + t.alias[0]] = oo + t.alias[1]
        oi, oo = oi + a, oo + b
    res = pl.pallas_call(
        wrapped, name=name, grid=grid,
        in_specs=list(in_specs) + [hbm] * sum(t_in),
        out_specs=list(out_specs) + [hbm] * sum(t_out),
        out_shape=list(out_shape) + [s for t in tasks for s in t.out_shapes],
        scratch_shapes=list(scratch_shapes) + [s for t in tasks for s in t.sems],
        input_output_aliases=aliases,
        compiler_params=_params(*(["arbitrary"] * len(grid))),
    )(*operands, *[a for t in tasks for a in t.inputs])
    own, extra = res[:n_out], res[n_out:]
    per_task, o = [], 0
    for b in t_out:
        per_task.append(extra[o:o + b])
        o += b
    return own, per_task


def _gather_task(shard, mid_step, rows=None, into=None):
    lo, n = (0, shard.shape[0]) if rows is None else rows

    def parts(ins, outs, sems):
        x_ref, out_ref, (send_sems, recv_sems, local_sem) = ins[0], outs[0], sems
        x_, y_, c_ = _mesh_pos()
        me, sibling = (x_, y_, c_), (x_, y_, 1 - c_)
        chips = [(1 - x_, y_), (x_, 1 - y_), (1 - x_, 1 - y_)]
        x_ref = x_ref.at[pl.ds(lo, n)]

        def rows(px, py, pc):
            return out_ref.at[4 * px + 2 * py + pc, pl.ds(lo, n)]

        def copy(k, block, to, src=None):
            return pltpu.make_async_remote_copy(
                src_ref=rows(*block) if src is None else src, dst_ref=rows(*block),
                send_sem=send_sems.at[k], recv_sem=recv_sems.at[k], device_id=to, device_id_type=MESH_ID)

        return dict(
            mine=lambda: pltpu.make_async_copy(x_ref, rows(*me), local_sem),
            first=lambda: [copy(0, me, sibling, src=x_ref)] + [copy(1 + j, me, (*chip, c_), src=x_ref) for j, chip in enumerate(chips)],
            passed=lambda: [copy(4 + j, (*chip, c_), sibling) for j, chip in enumerate(chips)],
            landed=lambda: [copy(1 + j, (*chip, c_), me) for j, chip in enumerate(chips)],
            last=lambda: [copy(0, sibling, me)] + [copy(4 + j, (*chip, 1 - c_), me) for j, chip in enumerate(chips)])

    def start(ins, outs, sems):
        p = parts(ins, outs, sems)
        p["mine"]().start()
        for cp in p["first"]():
            cp.start()

    def mid(ins, outs, sems):
        p = parts(ins, outs, sems)
        for cp, fw in zip(p["landed"](), p["passed"]()):
            cp.wait_recv()
            fw.start()

    def finish(ins, outs, sems):
        p = parts(ins, outs, sems)
        for cp in p["last"]():
            cp.wait_recv()
        for cp in p["first"]() + p["passed"]():
            cp.wait_send()
        p["mine"]().wait()

    return _Task([shard] if into is None else [shard, into], [jax.ShapeDtypeStruct((N_DEV,) + shard.shape, shard.dtype)],
                 [pltpu.SemaphoreType.DMA((7,)), pltpu.SemaphoreType.DMA((7,)), pltpu.SemaphoreType.DMA],
                 start, finish, mid, mid_step, alias=None if into is None else (1, 0))


def _swap_task(g8):
    _, R, C = g8.shape

    def copies(ins, outs, sems):
        (g_ref,), (recv_ref,), (ss, rs) = ins, outs, sems
        x_, y_, c_ = _mesh_pos()
        return [pltpu.make_async_remote_copy(src_ref=g_ref.at[2 * k + (1 - c_)], dst_ref=recv_ref.at[k], send_sem=ss.at[k],
                                             recv_sem=rs.at[k], device_id=(x_, y_, 1 - c_), device_id_type=MESH_ID)
                for k in range(4)]

    def start(ins, outs, sems):
        for cp in copies(ins, outs, sems):
            cp.start()

    def finish(ins, outs, sems):
        cps = copies(ins, outs, sems)
        for cp in cps:
            cp.wait_recv()
        for cp in cps:
            cp.wait_send()

    return _Task([g8], [jax.ShapeDtypeStruct((4, R, C), g8.dtype)],
                 [pltpu.SemaphoreType.DMA((4,)), pltpu.SemaphoreType.DMA((4,))], start, finish)


def _chip_sums(g8, recva):
    _, R, C = g8.shape
    rt = _sum_adamw_rows(R)
    rc = _row_chunk(rt)

    def body(core_ref, g_ref, a_ref, send_ref, own_ref):
        x_, y_, _ = _mesh_pos()
        chips = [(1 - x_, y_), (x_, 1 - y_), (1 - x_, 1 - y_), (x_, y_)]

        def chunk(i, carry):
            rows = pl.ds(pl.multiple_of(i * rc, rc), rc)
            for j, (tx, ty) in enumerate(chips):
                k = 2 * tx + ty
                s = g_ref[k, rows, :].astype(F32) + a_ref[k, rows, :].astype(F32)
                if j < 3:
                    send_ref[j, rows, :] = s.astype(BF16)
                else:
                    own_ref[rows, :] = s
            return carry

        lax.fori_loop(0, rt // rc, chunk, 0)

    core = lax.axis_index("c").astype(jnp.int32).reshape(1)
    return pl.pallas_call(
        body, name="chip_sums",
        grid_spec=pltpu.PrefetchScalarGridSpec(
            num_scalar_prefetch=1, grid=(R // rt,),
            in_specs=[pl.BlockSpec((4, None, rt, C), lambda i, c: (0, c[0], i, 0)), pl.BlockSpec((4, rt, C), lambda i, c: (0, i, 0))],
            out_specs=[pl.BlockSpec((3, rt, C), lambda i, c: (0, i, 0)), pl.BlockSpec((rt, C), lambda i, c: (i, 0))]),
        out_shape=[jax.ShapeDtypeStruct((3, R, C), BF16), jax.ShapeDtypeStruct((R, C), F32)],
        compiler_params=_params("parallel"),
    )(core, g8.reshape(4, 2, R, C), recva)


def _exchange_task(sendb):
    def copies(ins, outs, sems):
        (s_ref,), (recv_ref,), (ss, rs) = ins, outs, sems
        x_, y_, c_ = _mesh_pos()
        flips = [(1 - x_, y_), (x_, 1 - y_), (1 - x_, 1 - y_)]
        return [pltpu.make_async_remote_copy(src_ref=s_ref.at[j], dst_ref=recv_ref.at[j], send_sem=ss.at[j], recv_sem=rs.at[j],
                                             device_id=(tx, ty, c_), device_id_type=MESH_ID) for j, (tx, ty) in enumerate(flips)]

    def start(ins, outs, sems):
        for cp in copies(ins, outs, sems):
            cp.start()

    def finish(ins, outs, sems):
        cps = copies(ins, outs, sems)
        for cp in cps:
            cp.wait_recv()
        for cp in cps:
            cp.wait_send()

    return _Task([sendb], [jax.ShapeDtypeStruct(sendb.shape, sendb.dtype)],
                 [pltpu.SemaphoreType.DMA((3,)), pltpu.SemaphoreType.DMA((3,))], start, finish)


def _silu(v):
    return v * (1.0 / (1.0 + jnp.exp(-v)))


def _ada_fwd(c, taps, w_ada_l, b_ada, w_in_shard, casts, rpb_rev):
    n_seq = c.shape[0]
    D, cols = w_ada_l.shape
    F_l = taps.shape[1]
    W = D + F_l
    n_rows = N_DEV * n_seq
    n_c = len(casts)
    t_c = _gather_task(jax.ShapeDtypeStruct((8, W), F32), 0)
    t_w = _gather_task(jax.ShapeDtypeStruct(w_in_shard.shape, BF16), 0)
    t_m = _gather_task(jax.ShapeDtypeStruct((n_rows, cols), F32), 0)

    def body(c_ref, taps_ref, w_ref, b_ref, ws_ref, rev_ref, *refs):
        cast_in, refs = refs[:n_c], refs[n_c:]
        (slabs_ref, call_ref, taps_all_ref, mod_ref, win_ref), refs = refs[:5], refs[5:]
        cast_out, bias_ref, refs = refs[:n_c], refs[n_c], refs[n_c + 1:]
        (slab_vm, c_vm, m_vm, ws_f, ws_b), refs = refs[:5], refs[5:]
        stage_f, stage_b, refs = refs[:n_c], refs[n_c:2 * n_c], refs[2 * n_c:]
        copy_sem, in_sems, out_sems, sems = refs[0], refs[1], refs[2], refs[3:]
        sc, sw, sm = sems[0:3], sems[3:6], sems[6:9]
        x_, y_, c_ = _mesh_pos()
        slab_vm[...] = jnp.zeros_like(slab_vm)
        slab_vm[0:n_seq, 0:D] = c_ref[...]
        slab_vm[0:taps.shape[0], D:W] = taps_ref[...]
        t_c.start((slab_vm,), (slabs_ref,), sc)
        cp = pltpu.make_async_copy(ws_ref, ws_f, copy_sem)
        cp.start()
        cp.wait()
        ws_b[...] = ws_f[...].astype(BF16)
        t_w.start((ws_b,), (win_ref,), sw)
        loads = [pltpu.make_async_copy(cast_in[j], stage_f[j], in_sems.at[j]) for j in range(n_c)]
        for ld in loads:
            ld.start()
        t_c.mid((slab_vm,), (slabs_ref,), sc)
        t_c.finish((slab_vm,), (slabs_ref,), sc)
        cp = pltpu.make_async_copy(slabs_ref, c_vm, copy_sem)
        cp.start()
        cp.wait()
        c_all = c_vm[:, :, 0:D].reshape(N_DEV * 8, D)
        call_ref[...] = c_all
        for j in range(N_DEV):
            taps_all_ref[:, j * F_l:(j + 1) * F_l] = c_vm[j, 0:taps.shape[0], D:W]
        b_mine = b_ref[:, pl.ds(pl.multiple_of((4 * x_ + 2 * y_ + c_) * cols, 128), cols)]
        m64 = jnp.dot(_silu(c_all), w_ref[...], precision=lax.Precision.HIGHEST, preferred_element_type=F32) + b_mine
        r = lax.broadcasted_iota(jnp.int32, (n_rows, N_DEV * 8), 0)
        c = lax.broadcasted_iota(jnp.int32, (n_rows, N_DEV * 8), 1)
        pick = jnp.where(c == 8 * (r // n_seq) + r % n_seq, 1.0, 0.0)
        m_vm[...] = jnp.dot(pick, m64, precision=lax.Precision.HIGHEST, preferred_element_type=F32)
        t_m.start((m_vm,), (mod_ref,), sm)
        stores = [pltpu.make_async_copy(stage_b[j], cast_out[j], out_sems.at[j]) for j in range(n_c)]
        for j in range(n_c):
            loads[j].wait()
            stage_b[j][...] = stage_f[j][...].astype(BF16)
            stores[j].start()
        for p in range(NA_PAIRS):
            _na_bias_rows(rev_ref, bias_ref, p)
        t_w.mid((ws_b,), (win_ref,), sw)
        t_m.mid((m_vm,), (mod_ref,), sm)
        t_m.finish((m_vm,), (mod_ref,), sm)
        t_w.finish((ws_b,), (win_ref,), sw)
        for st in stores:
            st.wait()

    hbm, vm = pl.BlockSpec(memory_space=pl.ANY), pl.BlockSpec(memory_space=pltpu.VMEM)
    res = pl.pallas_call(
        body, name="ada_fwd", in_specs=[vm, vm, vm, vm, hbm, vm] + [hbm] * n_c,
        out_specs=[hbm, vm, vm, hbm, hbm] + [hbm] * n_c + [vm],
        out_shape=t_c.out_shapes + [jax.ShapeDtypeStruct((N_DEV * 8, D), F32), jax.ShapeDtypeStruct((taps.shape[0], N_DEV * F_l), F32)]
        + t_m.out_shapes + t_w.out_shapes
        + [jax.ShapeDtypeStruct(a.shape, BF16) for a in casts] + [jax.ShapeDtypeStruct((NA_PAIRS, N_DR * GRID_W, 128), F32)],
        scratch_shapes=[pltpu.VMEM((8, W), F32), pltpu.VMEM((N_DEV, 8, W), F32), pltpu.VMEM((n_rows, cols), F32),
                        pltpu.VMEM(w_in_shard.shape, F32), pltpu.VMEM(w_in_shard.shape, BF16)]
        + [pltpu.VMEM(a.shape, F32) for a in casts] + [pltpu.VMEM(a.shape, BF16) for a in casts]
        + [pltpu.SemaphoreType.DMA, pltpu.SemaphoreType.DMA((n_c,)), pltpu.SemaphoreType.DMA((n_c,))]
        + t_c.sems + t_w.sems + t_m.sems,
        compiler_params=pltpu.CompilerParams(vmem_limit_bytes=VMEM_LIMIT),
    )(c, taps, w_ada_l, b_ada, w_in_shard, rpb_rev, *casts)
    return res[1], res[2], res[3], res[4], res[5:5 + n_c], res[5 + n_c]


def _ada_bwd(c_all, packs, cols):
    def body(c_ref, d_ref, o_ref):
        x_, y_, c_ = _mesh_pos()
        mine = d_ref[:, pl.ds(pl.multiple_of((4 * x_ + 2 * y_ + c_) * cols, 128), cols)]
        o_ref[...] = lax.dot_general(_silu(c_ref[...]), mine, (((0,), (0,)), ((), ())),
                                     precision=lax.Precision.HIGHEST, preferred_element_type=F32)
    return pl.pallas_call(body, name="ada_bwd", out_shape=jax.ShapeDtypeStruct((c_all.shape[1], cols), F32),
                          compiler_params=pltpu.CompilerParams(vmem_limit_bytes=VMEM_LIMIT))(c_all, packs)


NA_PAIRS = NA_HEADS // 2
N_DR_PAD = 16


def _na_bias_rev(na_rpb):
    rev = jnp.pad(jnp.flip(na_rpb, axis=2), ((0, 0), (0, N_DR_PAD - N_DR), (0, GRID_W - N_DC)))
    return jnp.transpose(rev.reshape(NA_PAIRS, 2, N_DR_PAD, GRID_W), (0, 2, 1, 3)).reshape(NA_PAIRS, N_DR_PAD, 128)


def _na_bias_rows(r_ref, o_ref, p):
    k = lax.broadcasted_iota(jnp.int32, (GRID_W, 128), 0)
    lane = lax.broadcasted_iota(jnp.int32, (GRID_W, 128), 1)
    q = lane % GRID_W
    cs = jnp.clip(q - NA_COLS // 2, 0, GRID_W - NA_COLS)
    ok = (k >= cs) & (k < cs + NA_COLS)
    left = lane < GRID_W
    for dr in range(N_DR):
        row = jnp.broadcast_to(r_ref[p, dr:dr + 1, :], (GRID_W, 128))
        r0 = jnp.where(left, row, 0.0)
        r1 = jnp.where(left, pltpu.roll(row, GRID_W, axis=1), 0.0)
        y0 = pltpu.roll(r0, 128 - (NA_COLS - 1), axis=1, stride=1, stride_axis=0)
        y1 = pltpu.roll(r1, GRID_W - (NA_COLS - 1), axis=1, stride=1, stride_axis=0)
        o_ref[p, dr * GRID_W:(dr + 1) * GRID_W, :] = jnp.where(ok, jnp.where(left, y0, y1), NEG)


def _na_bias_grad(db, tasks=()):
    a = np.arange(128)
    flip = jnp.asarray(((a[:, None] // GRID_W == a[None, :] // GRID_W)
                        & (a[:, None] % GRID_W + a[None, :] % GRID_W == GRID_W - 1)).astype(np.float32))

    def body(d_ref, j_ref, o_ref):
        o_ref[...] = jnp.zeros_like(o_ref)
        for dr in range(N_DR):
            t = jnp.dot(d_ref[0, dr * GRID_W:(dr + 1) * GRID_W, :], j_ref[...], precision=lax.Precision.HIGHEST, preferred_element_type=F32)
            t = pltpu.roll(t, GRID_W + NA_COLS, axis=1, stride=1, stride_axis=0)
            o_ref[0, dr:dr + 1, :] = jnp.sum(t, axis=0, keepdims=True)

    (rows,), got = _hosted_call(
        body, "rpb_reduce", (NA_PAIRS,),
        [pl.BlockSpec((1, N_DR * GRID_W, 128), lambda p: (p, 0, 0)), _full((128, 128))],
        [pl.BlockSpec((1, N_DR_PAD, 128), lambda p: (p, 0, 0))],
        [jax.ShapeDtypeStruct((NA_PAIRS, N_DR_PAD, 128), F32)], (db, flip), tasks)
    g = rows.reshape(NA_PAIRS, N_DR_PAD, 2, GRID_W)[:, :N_DR, :, :N_DC]
    return jnp.transpose(g, (0, 2, 1, 3)).reshape(-1), got


def _rope_tables(S):
    half = HEAD_DIM // 2
    inv = np.float32(ROPE_THETA) ** (-np.arange(half, dtype=np.float32) / np.float32(half))
    ang = np.arange(S).astype(np.float32)[:, None] * inv[None, :]
    cos, sin = np.cos(ang).astype(np.float32), np.sin(ang).astype(np.float32)
    return jnp.asarray(np.tile(np.concatenate([cos, cos], axis=1), (1, 2))), jnp.asarray(np.tile(np.concatenate([-sin, sin], axis=1), (1, 2)))


def _rope_spec(tps, tm=TOKEN_TILE):
    return pl.BlockSpec((tm, 2 * HEAD_DIM), lambda i: (i % tps, 0))


def _rot_half(t):
    w = t.shape[1]
    lane = lax.broadcasted_iota(jnp.int32, t.shape, 1)
    return jnp.where((lane % HEAD_DIM) < HEAD_DIM // 2, pltpu.roll(t, w - HEAD_DIM // 2, axis=1),
                     pltpu.roll(t, HEAD_DIM // 2, axis=1))


def _tok_spec(w, tm=TOKEN_TILE):
    return pl.BlockSpec((tm, w), lambda i: (i, 0))


def _mod_spec(tps, d):
    return pl.BlockSpec((1, 6, d), lambda i: (i // tps, 0, 0))


def _bstat_spec(tps, w):
    return pl.BlockSpec((1, 8, w), lambda i: (i // tps, 0, 0))


def _attn_in(x2d, mod3, g_attn, w_in, cos_t, sin_t, S, tasks=(), tm=WIDE_TILE):
    T, D = x2d.shape
    tps = S // tm

    def body(x_ref, mod_ref, g_ref, w_ref, cos_ref, sin_ref, h_ref, qkv_ref):
        xn, _ = _rms(x_ref[...])
        h = (xn * g_ref[...]) * (1.0 + mod_ref[0, 1:2, :]) + mod_ref[0, 0:1, :]
        hb = h.astype(BF16)
        h_ref[...] = hb
        proj = _nt(hb, w_ref[...])
        rb = proj[:, ROPE_LO:ROPE_LO + ROPE_WIDTH]
        reps = (1, ROPE_WIDTH // (2 * HEAD_DIM))
        rb = rb * jnp.tile(cos_ref[...], reps) + _rot_half(rb) * jnp.tile(sin_ref[...], reps)
        qkv_ref[:, 0:NA_WIDTH] = (proj[:, 0:NA_WIDTH] * Q_SCALE).astype(BF16)
        qkv_ref[:, NA_WIDTH:ROPE_LO] = proj[:, NA_WIDTH:ROPE_LO].astype(BF16)
        qkv_ref[:, ROPE_LO:ROPE_LO + SW_WIDTH] = (rb[:, 0:SW_WIDTH] * Q_SCALE).astype(BF16)
        qkv_ref[:, ROPE_LO + SW_WIDTH:ROPE_LO + ROPE_WIDTH] = rb[:, SW_WIDTH:].astype(BF16)
        qkv_ref[:, ROPE_LO + ROPE_WIDTH:] = proj[:, ROPE_LO + ROPE_WIDTH:].astype(BF16)

    return _hosted_call(
        body, "attn_in", (T // tm,),
        [_tok_spec(D, tm), _mod_spec(tps, D), _full((1, D)), _full(w_in.shape), _rope_spec(tps, tm), _rope_spec(tps, tm)],
        [_tok_spec(D, tm), _tok_spec(IN_WIDTH, tm)],
        [jax.ShapeDtypeStruct((T, D), BF16), jax.ShapeDtypeStruct((T, IN_WIDTH), BF16)],
        (x2d, mod3, g_attn, w_in, cos_t, sin_t), tasks)


def _attn_out(oa, ob, x2d, mod3, g_na, g_sw, w_out, S, tasks=(), tm=WIDE_TILE):
    T, D = x2d.shape
    tps = S // tm

    def body(oa_ref, ob_ref, x_ref, mod_ref, gna_ref, gsw_ref, w_ref, mixin_ref, mix_ref, x1_ref):
        oan, _ = _rms(oa_ref[...])
        obn, _ = _rms(ob_ref[...])
        mixin = jnp.concatenate([oan * gna_ref[...], obn * gsw_ref[...]], axis=1).astype(BF16)
        mixin_ref[...] = mixin
        mix = _nn(mixin, w_ref[...])
        mix_ref[...] = mix
        x1_ref[...] = x_ref[...] + mod_ref[0, 2:3, :] * mix

    return _hosted_call(
        body, "attn_out", (T // tm,),
        [_tok_spec(NA_WIDTH, tm), _tok_spec(SW_WIDTH, tm), _tok_spec(D, tm), _mod_spec(tps, D),
         _full((1, NA_WIDTH)), _full((1, SW_WIDTH)), _full(w_out.shape)],
        [_tok_spec(NA_WIDTH + SW_WIDTH, tm), _tok_spec(D, tm), _tok_spec(D, tm)],
        [jax.ShapeDtypeStruct((T, NA_WIDTH + SW_WIDTH), BF16), jax.ShapeDtypeStruct((T, D), F32), jax.ShapeDtypeStruct((T, D), F32)],
        (oa, ob, x2d, mod3, g_na, g_sw, w_out), tasks)


def _ffn_up(x1, mod3, g_ffn, w_up, S, tasks=(), tm=WIDE_TILE):
    T, D = x1.shape
    F = w_up.shape[0] // 2
    tps = S // tm

    def body(x1_ref, mod_ref, g_ref, w_ref, h2_ref, val_ref, gt_ref):
        xn, _ = _rms(x1_ref[...])
        h2 = ((xn * g_ref[...]) * (1.0 + mod_ref[0, 4:5, :]) + mod_ref[0, 3:4, :]).astype(BF16)
        h2_ref[...] = h2
        u = _nt(h2, w_ref[...])
        val_ref[...] = u[:, :F].astype(BF16)
        gt_ref[...] = u[:, F:].astype(BF16)

    return _hosted_call(
        body, "ffn_up", (T // tm,), [_tok_spec(D, tm), _mod_spec(tps, D), _full((1, D)), _full(w_up.shape)],
        [_tok_spec(D, tm), _tok_spec(F, tm), _tok_spec(F, tm)],
        [jax.ShapeDtypeStruct((T, D), BF16), jax.ShapeDtypeStruct((T, F), BF16), jax.ShapeDtypeStruct((T, F), BF16)],
        (x1, mod3, g_ffn, w_up), tasks)


def _halo_specs(T, tps, w):
    per = TOKEN_TILE // 8
    prev = pl.BlockSpec((8, w), lambda i: (jnp.maximum(i * per - 1, 0), 0))
    nxt = pl.BlockSpec((8, w), lambda i: (jnp.minimum((i + 1) * per, T // 8 - 1), 0))
    return prev, nxt


def _seq_shifts(cur, before, after, ti, tps):
    tm = cur.shape[0]
    row = lax.broadcasted_iota(jnp.int32, cur.shape, 0)
    before = jnp.where(ti > 0, before.astype(F32), 0.0)
    after = jnp.where(ti < tps - 1, after.astype(F32), 0.0)
    return jnp.where(row == 0, before, pltpu.roll(cur, 1, axis=0)), jnp.where(row == tm - 1, after, pltpu.roll(cur, tm - 1, axis=0))


def _ffn_down(gt, val, conv_w, conv_b, w_down, x1, mod3, g_final, target, B, S):
    T, D = x1.shape
    F = gt.shape[1]
    tps = S // TOKEN_TILE
    prev, nxt = _halo_specs(T, tps, F)

    def body(gt_ref, prev_ref, next_ref, val_ref, cw_ref, cb_ref, w_ref, x1_ref, mod_ref, gf_ref, tgt_ref,
             a_ref, act_ref, vd_ref, dx2_ref, df_ref, gstat_ref, bstat_ref):
        i = pl.program_id(0)
        g = gt_ref[...].astype(F32)
        gprev, gnext = _seq_shifts(g, prev_ref[7:8, :], next_ref[0:1, :], i % tps, tps)
        gc = gprev * cw_ref[0:1, :] + g * cw_ref[1:2, :] + gnext * cw_ref[2:3, :] + cb_ref[...]
        sig = 1.0 / (1.0 + jnp.exp(-gc))
        act = gc * sig
        val = val_ref[...].astype(F32)
        act_ref[...] = act.astype(BF16)
        vd_ref[...] = (val * (sig + act - act * sig)).astype(BF16)
        a = (act * val).astype(BF16)
        a_ref[...] = a
        f = _nn(a, w_ref[...])
        gate = mod_ref[0, 5:6, :]
        x2 = x1_ref[...] + gate * f
        xn, r = _rms(x2)
        err = xn * gf_ref[...] - tgt_ref[...]
        dy = err * (1.0 / D)
        dx2 = _rms_bwd(xn, r, dy * gf_ref[...])
        dx2_ref[...] = dx2
        df_ref[...] = (gate * dx2).astype(BF16)

        @pl.when(i == 0)
        def _():
            gstat_ref[...] = jnp.zeros_like(gstat_ref)

        @pl.when(i % tps == 0)
        def _():
            bstat_ref[...] = jnp.zeros_like(bstat_ref)

        gstat_ref[0:1, :] += jnp.sum(dy * xn, axis=0, keepdims=True)
        tile_loss = jnp.sum(jnp.sum(err * err, axis=1, keepdims=True), axis=0, keepdims=True) * (0.5 / D)
        gstat_ref[1:2, :] += jnp.broadcast_to(tile_loss, (1, D))
        bstat_ref[0, 0:1, :] += jnp.sum(dx2 * f, axis=0, keepdims=True)

    return pl.pallas_call(
        body, name="ffn_down", grid=(T // TOKEN_TILE,),
        in_specs=[_tok_spec(F), prev, nxt, _tok_spec(F), _full(conv_w.shape), _full((1, F)), _full(w_down.shape),
                  _tok_spec(D), _mod_spec(tps, D), _full((1, D)), _tok_spec(D)],
        out_specs=[_tok_spec(F), _tok_spec(F), _tok_spec(F), _tok_spec(D), _tok_spec(D), _full((8, D)), _bstat_spec(tps, D)],
        out_shape=[jax.ShapeDtypeStruct((T, F), BF16), jax.ShapeDtypeStruct((T, F), BF16), jax.ShapeDtypeStruct((T, F), BF16),
                   jax.ShapeDtypeStruct((T, D), F32), jax.ShapeDtypeStruct((T, D), BF16),
                   jax.ShapeDtypeStruct((8, D), F32), jax.ShapeDtypeStruct((B, 8, D), F32)],
        compiler_params=_params("arbitrary"),
    )(gt, gt, gt, val, conv_w, conv_b, w_down, x1, mod3, g_final, target)


def _ffn_down_bwd(df, w_down, act, vd, tasks=(), tm=WIDE_TILE):
    T, D = df.shape
    F = act.shape[1]

    def body(df_ref, w_ref, act_ref, vd_ref, dval_ref, dgc_ref, cstat_ref):
        da = _nt(df_ref[...], w_ref[...])
        dval_ref[...] = (da * act_ref[...].astype(F32)).astype(BF16)
        dgc = da * vd_ref[...].astype(F32)
        dgc_ref[...] = dgc.astype(BF16)

        @pl.when(pl.program_id(0) == 0)
        def _():
            cstat_ref[...] = jnp.zeros_like(cstat_ref)

        cstat_ref[0:1, :] += jnp.sum(dgc, axis=0, keepdims=True)

    return _hosted_call(
        body, "ffn_down_bwd", (T // tm,),
        [_tok_spec(D, tm), _full(w_down.shape), _tok_spec(F, tm), _tok_spec(F, tm)],
        [_tok_spec(F, tm), _tok_spec(F, tm), _full((8, F))],
        [jax.ShapeDtypeStruct((T, F), BF16), jax.ShapeDtypeStruct((T, F), BF16), jax.ShapeDtypeStruct((8, F), F32)],
        (df, w_down, act, vd), tasks)


def _ffn_up_bwd(dgc, dval, gt, conv_w, w_up, x1, mod3, g_ffn, dx2, mix, B, S, tasks=()):
    T, D = x1.shape
    F = dgc.shape[1]
    tps = S // TOKEN_TILE
    prev, nxt = _halo_specs(T, tps, F)

    def body(dgc_ref, prev_ref, next_ref, dval_ref, gt_ref, cw_ref, w_ref, x1_ref, mod_ref, g_ref, dx2_ref, mix_ref,
             du_ref, dx1_ref, dmix_ref, gstat_ref, bstat_ref, cstat_ref):
        i = pl.program_id(0)
        d = dgc_ref[...].astype(F32)
        dprev, dnext = _seq_shifts(d, prev_ref[7:8, :], next_ref[0:1, :], i % tps, tps)
        g = gt_ref[...].astype(F32)

        @pl.when(i == 0)
        def _():
            cstat_ref[...] = jnp.zeros_like(cstat_ref)

        cstat_ref[1:2, :] += jnp.sum(dnext * g, axis=0, keepdims=True)
        cstat_ref[2:3, :] += jnp.sum(d * g, axis=0, keepdims=True)
        cstat_ref[3:4, :] += jnp.sum(dprev * g, axis=0, keepdims=True)
        dgt = dnext * cw_ref[0:1, :] + d * cw_ref[1:2, :] + dprev * cw_ref[2:3, :]
        du = jnp.concatenate([dval_ref[...], dgt.astype(BF16)], axis=1)
        du_ref[...] = du
        dh2 = _nn(du, w_ref[...])
        xn, r = _rms(x1_ref[...])
        scale1 = 1.0 + mod_ref[0, 4:5, :]
        xg = xn * g_ref[...]
        dx1 = dx2_ref[...] + _rms_bwd(xn, r, dh2 * g_ref[...] * scale1)
        dx1_ref[...] = dx1
        dmix_ref[...] = (mod_ref[0, 2:3, :] * dx1).astype(BF16)

        @pl.when(i == 0)
        def _():
            gstat_ref[...] = jnp.zeros_like(gstat_ref)

        @pl.when(i % tps == 0)
        def _():
            bstat_ref[...] = jnp.zeros_like(bstat_ref)

        gstat_ref[0:1, :] += jnp.sum(dh2 * scale1 * xn, axis=0, keepdims=True)
        bstat_ref[0, 0:1, :] += jnp.sum(dh2, axis=0, keepdims=True)
        bstat_ref[0, 1:2, :] += jnp.sum(dh2 * xg, axis=0, keepdims=True)
        bstat_ref[0, 2:3, :] += jnp.sum(dx1 * mix_ref[...], axis=0, keepdims=True)

    return _hosted_call(
        body, "ffn_up_bwd", (T // TOKEN_TILE,),
        [_tok_spec(F), prev, nxt, _tok_spec(F), _tok_spec(F), _full(conv_w.shape), _full(w_up.shape), _tok_spec(D),
         _mod_spec(tps, D), _full((1, D)), _tok_spec(D), _tok_spec(D)],
        [_tok_spec(2 * F), _tok_spec(D), _tok_spec(D), _full((8, D)), _bstat_spec(tps, D), _full((8, F))],
        [jax.ShapeDtypeStruct((T, 2 * F), BF16), jax.ShapeDtypeStruct((T, D), F32), jax.ShapeDtypeStruct((T, D), BF16),
         jax.ShapeDtypeStruct((8, D), F32), jax.ShapeDtypeStruct((B, 8, D), F32), jax.ShapeDtypeStruct((8, F), F32)],
        (dgc, dgc, dgc, dval, gt, conv_w, w_up, x1, mod3, g_ffn, dx2, mix), tasks)


def _attn_out_bwd(dmix, w_out, oa, ob, g_na, g_sw, tasks=(), tm=WIDE_TILE):
    T, D = dmix.shape

    def body(dmix_ref, w_ref, oa_ref, ob_ref, gna_ref, gsw_ref, doa_ref, dob_ref, gstat_ref):
        dmixin = _nt(dmix_ref[...], w_ref[...])

        @pl.when(pl.program_id(0) == 0)
        def _():
            gstat_ref[...] = jnp.zeros_like(gstat_ref)

        for k, (o_ref, g_ref, do_ref) in enumerate(((oa_ref, gna_ref, doa_ref), (ob_ref, gsw_ref, dob_ref))):
            dn = dmixin[:, k * NA_WIDTH:(k + 1) * NA_WIDTH]
            on, r = _rms(o_ref[...])
            gstat_ref[k:k + 1, :] += jnp.sum(dn * on, axis=0, keepdims=True)
            do_ref[...] = _rms_bwd(on, r, dn * g_ref[...]).astype(BF16)

    hs = jax.ShapeDtypeStruct((T, NA_WIDTH), BF16)
    return _hosted_call(
        body, "attn_out_bwd", (T // tm,),
        [_tok_spec(D, tm), _full(w_out.shape), _tok_spec(NA_WIDTH, tm), _tok_spec(SW_WIDTH, tm), _full((1, NA_WIDTH)), _full((1, SW_WIDTH))],
        [_tok_spec(NA_WIDTH, tm), _tok_spec(SW_WIDTH, tm), _full((8, NA_WIDTH))],
        [hs, hs, jax.ShapeDtypeStruct((8, NA_WIDTH), F32)],
        (dmix, w_out, oa, ob, g_na, g_sw), tasks)


def _attn_in_bwd(dqa, dka, dva, dqb, dkb, dvb, cos_t, sin_t, w_in, x2d, mod3, g_attn, dx1, B, S, tm=WIDE_TILE):
    T, D = x2d.shape
    tps = S // tm

    def body(dqa_ref, dka_ref, dva_ref, dqb_ref, dkb_ref, dvb_ref, cos_ref, sin_ref, w_ref, x_ref, mod_ref, g_ref, dx1_ref,
             gx_ref, dproj_ref, gstat_ref, bstat_ref):
        i = pl.program_id(0)
        drb = jnp.concatenate([dqb_ref[...] * Q_SCALE, dkb_ref[...]], axis=1).astype(F32)
        reps = (1, ROPE_WIDTH // (2 * HEAD_DIM))
        drb = drb * jnp.tile(cos_ref[...], reps) + _rot_half(drb * jnp.tile(sin_ref[...], reps))
        dproj = jnp.concatenate([(dqa_ref[...] * Q_SCALE).astype(BF16), dka_ref[...].astype(BF16), dva_ref[...].astype(BF16),
                                 drb.astype(BF16), dvb_ref[...].astype(BF16)], axis=1)
        dproj_ref[...] = dproj
        dh = _nn(dproj, w_ref[...])
        xn, r = _rms(x_ref[...])
        scale1 = 1.0 + mod_ref[0, 1:2, :]
        gx_ref[...] = dx1_ref[...] + _rms_bwd(xn, r, dh * g_ref[...] * scale1)

        @pl.when(i == 0)
        def _():
            gstat_ref[...] = jnp.zeros_like(gstat_ref)

        @pl.when(i % tps == 0)
        def _():
            bstat_ref[...] = jnp.zeros_like(bstat_ref)

        gstat_ref[0:1, :] += jnp.sum(dh * scale1 * xn, axis=0, keepdims=True)
        bstat_ref[0, 0:1, :] += jnp.sum(dh, axis=0, keepdims=True)
        bstat_ref[0, 1:2, :] += jnp.sum(dh * (xn * g_ref[...]), axis=0, keepdims=True)

    rope = _rope_spec(tps, tm)
    return pl.pallas_call(
        body, name="attn_in_bwd", grid=(T // tm,),
        in_specs=[_tok_spec(NA_WIDTH, tm), _tok_spec(NA_WIDTH, tm), _tok_spec(NA_WIDTH, tm), _tok_spec(SW_WIDTH, tm),
                  _tok_spec(SW_KV_WIDTH, tm), _tok_spec(SW_KV_WIDTH, tm), rope, rope, _full(w_in.shape), _tok_spec(D, tm),
                  _mod_spec(tps, D), _full((1, D)), _tok_spec(D, tm)],
        out_specs=[_tok_spec(D, tm), _tok_spec(IN_WIDTH, tm), _full((8, D)), _bstat_spec(tps, D)],
        out_shape=[jax.ShapeDtypeStruct((T, D), F32), jax.ShapeDtypeStruct((T, IN_WIDTH), BF16),
                   jax.ShapeDtypeStruct((8, D), F32), jax.ShapeDtypeStruct((B, 8, D), F32)],
        compiler_params=_params("arbitrary"),
    )(dqa, dka, dva, dqb, dkb, dvb, cos_t, sin_t, w_in, x2d, mod3, g_attn, dx1)


def _matmul_tn(a, b, name, tm=None, tk=512):
    T, M = a.shape
    N = b.shape[1]
    tm = M if tm is None else tm
    nk = T // tk

    def body(a_ref, b_ref, o_ref, acc):
        k = pl.program_id(1)

        @pl.when(k == 0)
        def _():
            acc[...] = jnp.zeros_like(acc)

        acc[...] += _tn(a_ref[...], b_ref[...])

        @pl.when(k == nk - 1)
        def _():
            o_ref[...] = acc[...].astype(BF16)

    return pl.pallas_call(
        body, name=name, grid=(M // tm, nk),
        in_specs=[pl.BlockSpec((tk, tm), lambda i, k: (k, i)), pl.BlockSpec((tk, N), lambda i, k: (k, 0))],
        out_specs=pl.BlockSpec((tm, N), lambda i, k: (i, 0)),
        out_shape=jax.ShapeDtypeStruct((M, N), BF16),
        scratch_shapes=[pltpu.VMEM((tm, N), F32)],
        compiler_params=_params("parallel", "arbitrary"),
    )(a, b)


def _na_geometry(S):
    rows = S // GRID_W
    wr = min(NA_ROWS_MAX, rows)
    return rows, wr


def _na_window(r, rows, wr):
    rs = jnp.clip(r - wr // 2, 0, rows - wr)
    return pl.multiple_of(rs * GRID_W, GRID_W), pl.multiple_of((rs - r + NA_ROWS_MAX - 1) * GRID_W, GRID_W)


NA_STEP_PAIRS = 2
NA_GW = NA_STEP_PAIRS * 128
NA_BWD_ROWS = 4
NA_ROWS_PER_STEP = 4


def _na_specs(S, kw_n, order):
    ng = NA_PAIRS // NA_STEP_PAIRS

    def col(k):
        return pl.BlockSpec((1, S, NA_GW), lambda *ids: (order(*ids)[0], 0, k * ng + order(*ids)[1]))
    bias = pl.BlockSpec((NA_STEP_PAIRS, N_DR * GRID_W, 128), lambda *ids: (order(*ids)[1], 0, 0))
    out = pl.BlockSpec((1, S, NA_GW), lambda *ids: (order(*ids)[0], 0, order(*ids)[1]))
    return col(0), col(1), col(2), bias, out


def _block_diag(t):
    left = lax.broadcasted_iota(jnp.int32, t.shape, 1) < HEAD_DIM
    zero = jnp.zeros_like(t)
    return jnp.concatenate([jnp.where(left, t, zero), jnp.where(left, zero, t)], axis=0)


def _diag_blocks(res):
    left = lax.broadcasted_iota(jnp.int32, (HEAD_DIM, 128), 1) < HEAD_DIM
    return jnp.where(left, res[:HEAD_DIM], res[HEAD_DIM:])


def _col_softmax(st):
    e = jnp.exp(st - jnp.max(st, axis=0, keepdims=True))
    return e * (1.0 / jnp.sum(e, axis=0, keepdims=True))


def _na_fwd(qkv, bias, tasks=()):
    B, S, _ = qkv.shape
    rows, wr = _na_geometry(S)
    kw_n = wr * GRID_W

    def body(q_ref, k_ref, v_ref, b_ref, o_ref):
        def step(it, carry):
            win = [_na_window(it * NA_ROWS_PER_STEP + u, rows, wr) for u in range(NA_ROWS_PER_STEP)]
            qrows = [pl.ds(pl.multiple_of((it * NA_ROWS_PER_STEP + u) * GRID_W, GRID_W), GRID_W) for u in range(NA_ROWS_PER_STEP)]
            krows = [pl.ds(w[0], kw_n) for w in win]
            brows = [pl.ds(w[1], kw_n) for w in win]
            lanes = [pl.ds(p * 128, 128) for p in range(NA_STEP_PAIRS)]
            chains = [(u, p) for u in range(NA_ROWS_PER_STEP) for p in range(NA_STEP_PAIRS)]
            st = {(u, p): _nt(k_ref[0, krows[u], lanes[p]], _block_diag(q_ref[0, qrows[u], lanes[p]])) for u, p in chains}
            pn = {(u, p): _col_softmax(st[(u, p)] + b_ref[p, brows[u], :]).astype(BF16) for u, p in chains}
            out = {(u, p): _diag_blocks(_tn(pn[(u, p)], v_ref[0, krows[u], lanes[p]])) for u, p in chains}
            for u in range(NA_ROWS_PER_STEP):
                o_ref[0, qrows[u], :] = jnp.concatenate([out[(u, p)] for p in range(NA_STEP_PAIRS)], axis=1)
            return carry

        lax.fori_loop(0, rows // NA_ROWS_PER_STEP, step, 0)

    q, k, v, bs, out = _na_specs(S, kw_n, lambda b, g: (b, g))
    return _hosted_call(body, "na_fwd", (B, NA_PAIRS // NA_STEP_PAIRS), [q, k, v, bs], [out],
                        [jax.ShapeDtypeStruct((B, S, NA_WIDTH), F32)], (qkv, qkv, qkv, bias), tasks)


def _na_bwd(qkv, bias, doa, tasks=()):
    B, S, _ = qkv.shape
    rows, wr = _na_geometry(S)
    kw_n = wr * GRID_W

    def body(q_ref, k_ref, v_ref, b_ref, do_ref, dq_ref, dk_ref, dv_ref, db_ref, dk_acc, dv_acc):
        @pl.when(pl.program_id(1) == 0)
        def _():
            db_ref[...] = jnp.zeros_like(db_ref)

        dk_acc[...] = jnp.zeros_like(dk_acc)
        dv_acc[...] = jnp.zeros_like(dv_acc)

        def step(it, carry):
            nu, pairs = range(NA_BWD_ROWS), range(NA_STEP_PAIRS)
            win = [_na_window(it * NA_BWD_ROWS + u, rows, wr) for u in nu]
            qrows = [pl.ds(pl.multiple_of((it * NA_BWD_ROWS + u) * GRID_W, GRID_W), GRID_W) for u in nu]
            krows = [pl.ds(w[0], kw_n) for w in win]
            brows = [pl.ds(w[1], kw_n) for w in win]
            lanes = [pl.ds(p * 128, 128) for p in pairs]
            chains = [(u, p) for u in nu for p in pairs]
            kp = {(u, p): k_ref[0, krows[u], lanes[p]] for u, p in chains}
            qbd = {(u, p): _block_diag(q_ref[0, qrows[u], lanes[p]]) for u, p in chains}
            dobd = {(u, p): _block_diag(do_ref[0, qrows[u], lanes[p]]) for u, p in chains}
            st = {c: _nt(kp[c], qbd[c]) for c in chains}
            dpt = {(u, p): _nt(v_ref[0, krows[u], lanes[p]], dobd[(u, p)]) for u, p in chains}
            pn = {(u, p): _col_softmax(st[(u, p)] + b_ref[p, brows[u], :]) for u, p in chains}
            dst = {c: pn[c] * (dpt[c] - jnp.sum(pn[c] * dpt[c], axis=0, keepdims=True)) for c in chains}
            dsb = {c: dst[c].astype(BF16) for c in chains}
            dq = {c: _diag_blocks(_tn(dsb[c], kp[c])) for c in chains}
            dk = {c: _nn(dsb[c], qbd[c]) for c in chains}
            dv = {c: _nn(pn[c].astype(BF16), dobd[c]) for c in chains}
            for u in nu:
                dq_ref[0, qrows[u], :] = jnp.concatenate([dq[(u, p)] for p in pairs], axis=1).astype(BF16)
                dk_acc[krows[u], :] += jnp.concatenate([dk[(u, p)] for p in pairs], axis=1)
                dv_acc[krows[u], :] += jnp.concatenate([dv[(u, p)] for p in pairs], axis=1)
                for p in pairs:
                    db_ref[p, brows[u], :] += dst[(u, p)]
            return carry

        lax.fori_loop(0, rows // NA_BWD_ROWS, step, 0)

        def emit(i, carry):
            r = pl.ds(pl.multiple_of(i * 256, 256), 256)
            dk_ref[0, r, :] = dk_acc[r, :].astype(BF16)
            dv_ref[0, r, :] = dv_acc[r, :].astype(BF16)
            return carry

        lax.fori_loop(0, S // 256, emit, 0)

    q, k, v, bs, out = _na_specs(S, kw_n, lambda g, b: (b, g))
    hs = jax.ShapeDtypeStruct((B, S, NA_WIDTH), BF16)
    return _hosted_call(body, "na_bwd", (NA_PAIRS // NA_STEP_PAIRS, B), [q, k, v, bs, out], [out, out, out, bs],
                        [hs, hs, hs, jax.ShapeDtypeStruct((NA_PAIRS, N_DR * GRID_W, 128), F32)], (qkv, qkv, qkv, bias, doa), tasks,
                        scratch_shapes=[pltpu.VMEM((S, NA_GW), F32), pltpu.VMEM((S, NA_GW), F32)])


SW_PAIRS = SW_HEADS // 2


def _sw_band(n, S):
    kw_n = 3 * SW_BLOCK
    start = pl.multiple_of(jnp.clip(n * SW_BLOCK - SW_BLOCK, 0, S - kw_n), SW_BLOCK)
    kpos = start + lax.broadcasted_iota(jnp.int32, (kw_n, SW_BLOCK), 0)
    qpos = n * SW_BLOCK + lax.broadcasted_iota(jnp.int32, (kw_n, SW_BLOCK), 1)
    return start, jnp.abs(qpos - kpos) <= SW_WINDOW


def _kv_halves(t):
    left = lax.broadcasted_iota(jnp.int32, t.shape, 1) < HEAD_DIM
    swapped = pltpu.roll(t, HEAD_DIM, axis=1)
    zero = jnp.zeros_like(t)
    return {(0, 0): jnp.where(left, t, zero), (0, 1): jnp.where(left, zero, swapped),
            (1, 0): jnp.where(left, swapped, zero), (1, 1): jnp.where(left, zero, t)}


def _sw_probs(st, ok, sk):
    st = jnp.where(ok, st, NEG)
    m = jnp.maximum(jnp.max(st, axis=0, keepdims=True), sk)
    e = jnp.exp(st - m)
    esk = jnp.exp(sk - m)
    inv = 1.0 / (jnp.sum(e, axis=0, keepdims=True) + esk)
    return e * inv, esk * inv


def _sw_specs(S):
    q = pl.BlockSpec((1, S, SW_WIDTH), lambda b, *_: (b, 0, ROPE_LO // SW_WIDTH))
    k = pl.BlockSpec((1, S, SW_KV_WIDTH), lambda b, *_: (b, 0, (ROPE_LO + SW_WIDTH) // SW_KV_WIDTH))
    v = pl.BlockSpec((1, S, SW_KV_WIDTH), lambda b, *_: (b, 0, (ROPE_LO + ROPE_WIDTH) // SW_KV_WIDTH))
    return q, k, v


SW_FWD_SPLIT = 2


def _sw_fwd(sink, qkv, tasks=()):
    B, S, _ = qkv.shape
    kw_n = 3 * SW_BLOCK

    def body(sink_ref, q_ref, k_ref, v_ref, o_ref):
        def step(n, carry):
            start, ok = _sw_band(n, S)
            qrows = pl.ds(pl.multiple_of(n * SW_BLOCK, SW_BLOCK), SW_BLOCK)
            krows = pl.ds(start, kw_n)
            kh, vh = _kv_halves(k_ref[0, krows, :]), _kv_halves(v_ref[0, krows, :])
            heads = [(p, e) for p in range(SW_PAIRS) for e in range(2)]
            qp = [q_ref[0, qrows, pl.ds(p * 128, 128)] for p in range(SW_PAIRS)]
            kv_of = lambda p: p // (SW_PAIRS // SW_KV_HEADS)
            st = {(p, e): _nt(kh[(kv_of(p), e)], qp[p]) for p, e in heads}
            pn = {(p, e): _sw_probs(st[(p, e)], ok, sink_ref[2 * p + e])[0].astype(BF16) for p, e in heads}
            outs = [_tn(pn[(p, 0)], vh[(kv_of(p), 0)]) + _tn(pn[(p, 1)], vh[(kv_of(p), 1)]) for p in range(SW_PAIRS)]
            o_ref[0, qrows, :] = jnp.concatenate(outs, axis=1)
            return carry

        half = (S // SW_BLOCK) // SW_FWD_SPLIT
        lax.fori_loop(pl.program_id(1) * half, (pl.program_id(1) + 1) * half, step, 0)

    q, k, v = _sw_specs(S)
    return _hosted_call(
        body, "sw_fwd", (B, SW_FWD_SPLIT), [pl.BlockSpec(memory_space=pltpu.SMEM), q, k, v],
        [pl.BlockSpec((1, S, SW_WIDTH), lambda b, s: (b, 0, 0))], [jax.ShapeDtypeStruct((B, S, SW_WIDTH), F32)],
        (sink, qkv, qkv, qkv), tasks)


def _sw_bwd(sink, qkv, dob):
    B, S, _ = qkv.shape
    kw_n = 3 * SW_BLOCK

    fold_rows = 256

    def body(sink_ref, q_ref, k_ref, v_ref, do_ref, dq_ref, dk_ref, dv_ref, dsink_ref, dk_acc, dv_acc):
        @pl.when(pl.program_id(0) == 0)
        def _():
            dsink_ref[...] = jnp.zeros_like(dsink_ref)

        dk_acc[...] = jnp.zeros_like(dk_acc)
        dv_acc[...] = jnp.zeros_like(dv_acc)
        ppk = SW_PAIRS // SW_KV_HEADS

        def step(n, carry):
            start, ok = _sw_band(n, S)
            qrows = pl.ds(pl.multiple_of(n * SW_BLOCK, SW_BLOCK), SW_BLOCK)
            krows = pl.ds(start, kw_n)
            kh, vh = _kv_halves(k_ref[0, krows, :]), _kv_halves(v_ref[0, krows, :])
            heads = [(p, e) for p in range(SW_PAIRS) for e in range(2)]
            qp = [q_ref[0, qrows, pl.ds(p * 128, 128)] for p in range(SW_PAIRS)]
            dop = [do_ref[0, qrows, pl.ds(p * 128, 128)] for p in range(SW_PAIRS)]
            st = {(p, e): _nt(kh[(p // ppk, e)], qp[p]) for p, e in heads}
            dpt = {(p, e): _nt(vh[(p // ppk, e)], dop[p]) for p, e in heads}
            pnb, dsb = {}, {}
            for p, e in heads:
                pn, psink = _sw_probs(st[(p, e)], ok, sink_ref[2 * p + e])
                delta = jnp.sum(pn * dpt[(p, e)], axis=0, keepdims=True)
                dsb[(p, e)] = (pn * (dpt[(p, e)] - delta)).astype(BF16)
                pnb[(p, e)] = pn.astype(BF16)
                dsink_ref[2 * p + e:2 * p + e + 1, :] += -(psink * delta)
            dq_ref[0, qrows, :] = jnp.concatenate(
                [_tn(dsb[(p, 0)], kh[(p // ppk, 0)]) + _tn(dsb[(p, 1)], kh[(p // ppk, 1)]) for p in range(SW_PAIRS)],
                axis=1).astype(BF16)
            left = lax.broadcasted_iota(jnp.int32, (kw_n, 128), 1) < HEAD_DIM
            dks, dvs = [], []
            for kv in range(SW_KV_HEADS):
                dk = dv = None
                for p in range(kv * ppk, (kv + 1) * ppk):
                    dk_p = jnp.where(left, _nn(dsb[(p, 0)], qp[p]), _nn(dsb[(p, 1)], qp[p]))
                    dv_p = jnp.where(left, _nn(pnb[(p, 0)], dop[p]), _nn(pnb[(p, 1)], dop[p]))
                    dk = dk_p if dk is None else dk + dk_p
                    dv = dv_p if dv is None else dv + dv_p
                dks.append(dk)
                dvs.append(dv)
            dk_acc[krows, :] += jnp.concatenate(dks, axis=1)
            dv_acc[krows, :] += jnp.concatenate(dvs, axis=1)
            return carry

        lax.fori_loop(0, S // SW_BLOCK, step, 0)

        def fold(i, carry):
            rows = pl.ds(pl.multiple_of(i * fold_rows, fold_rows), fold_rows)
            left = lax.broadcasted_iota(jnp.int32, (fold_rows, 128), 1) < HEAD_DIM
            for acc, out_ref in ((dk_acc, dk_ref), (dv_acc, dv_ref)):
                a, b = acc[rows, 0:128], acc[rows, 128:256]
                out_ref[0, rows, :] = jnp.where(left, a + pltpu.roll(a, HEAD_DIM, axis=1),
                                                b + pltpu.roll(b, HEAD_DIM, axis=1)).astype(BF16)
            return carry

        lax.fori_loop(0, S // fold_rows, fold, 0)

        @pl.when(pl.program_id(0) == B - 1)
        def _():
            dsink_ref[...] = jnp.broadcast_to(jnp.sum(dsink_ref[...], axis=1, keepdims=True), dsink_ref.shape)

    q, k, v = _sw_specs(S)
    qo = pl.BlockSpec((1, S, SW_WIDTH), lambda b: (b, 0, 0))
    ko = pl.BlockSpec((1, S, SW_KV_WIDTH), lambda b: (b, 0, 0))
    return pl.pallas_call(
        body, name="sw_bwd", grid=(B,),
        in_specs=[pl.BlockSpec(memory_space=pltpu.SMEM), q, k, v, qo],
        out_specs=[qo, ko, ko, _full((SW_HEADS, 128))],
        out_shape=[jax.ShapeDtypeStruct((B, S, SW_WIDTH), BF16), jax.ShapeDtypeStruct((B, S, SW_KV_WIDTH), BF16),
                   jax.ShapeDtypeStruct((B, S, SW_KV_WIDTH), BF16), jax.ShapeDtypeStruct((SW_HEADS, 128), F32)],
        scratch_shapes=[pltpu.VMEM((S, 2 * SW_KV_WIDTH), F32), pltpu.VMEM((S, 2 * SW_KV_WIDTH), F32)],
        compiler_params=_params("arbitrary"),
    )(sink, qkv, qkv, qkv, dob)


def _pack_sum_adamw(packs, params, pick):
    W = packs.shape[1]
    n_p = len(params)

    def body(p_ref, *refs):
        ins, tot_ref, pick_ref, outs = refs[:3 * n_p], refs[3 * n_p], refs[3 * n_p + 1], refs[3 * n_p + 2:]
        tot = p_ref[0:8, :]
        for d in range(1, N_DEV):
            tot = tot + p_ref[8 * d:8 * d + 8, :]
        tot_ref[...] = tot
        pick_ref[...] = tot[pick[0]:pick[0] + 1, pick[1]:pick[1] + 1]
        for i, (w, _, _, rows, off) in enumerate(params):
            w_ref, m_ref, v_ref = ins[3 * i:3 * i + 3]
            g_ref, d_ref, nm_ref, nv_ref = outs[4 * i:4 * i + 4]
            if w.ndim == 3:
                n_a, n_b, n = w.shape
                for a in range(n_a):
                    for b in range(n_b):
                        o = off + (b * n_a + a) * n
                        g_ref[a, b:b + 1, :] = tot_ref[rows[0]:rows[0] + 1, o:o + n]
                g = g_ref[...]
            else:
                n = w.shape[1]
                g = tot[rows[0]:rows[0] + 1, off:off + n]
                for r in rows[1:]:
                    g = g + tot[r:r + 1, off:off + n]
                g_ref[...] = g
            d_ref[...], nm_ref[...], nv_ref[...] = _adam_update(w_ref[...], g, m_ref[...], v_ref[...])

    res = pl.pallas_call(
        body, name="small_adamw",
        out_shape=[jax.ShapeDtypeStruct((8, W), F32), jax.ShapeDtypeStruct((1, 1), F32)]
        + [jax.ShapeDtypeStruct(p[0].shape, F32) for p in params for _ in range(4)],
        compiler_params=pltpu.CompilerParams(vmem_limit_bytes=VMEM_LIMIT),
    )(packs, *[a for p in params for a in p[:3]])
    return res[0], res[1], [res[2 + 4 * i:6 + 4 * i] for i in range(n_p)]


def _adam_update(w, g, m, v):
    c1 = 1.0 - ADAM_B1 ** ADAM_STEP
    c2 = 1.0 - ADAM_B2 ** ADAM_STEP
    nm = ADAM_B1 * m + (1.0 - ADAM_B1) * g
    nv = ADAM_B2 * v + (1.0 - ADAM_B2) * (g * g)
    return -ADAM_LR * ((nm / c1) / (jnp.sqrt(nv / c2) + ADAM_EPS) + ADAM_WD * w), nm, nv


def _adamw(w, g, m, v, name):
    def body(w_ref, g_ref, m_ref, v_ref, d_ref, nm_ref, nv_ref):
        d_ref[...], nm_ref[...], nv_ref[...] = _adam_update(w_ref[...], g_ref[...], m_ref[...], v_ref[...])

    s = jax.ShapeDtypeStruct(w.shape, F32)
    return pl.pallas_call(body, name=name, out_shape=[s, s, s],
                          compiler_params=pltpu.CompilerParams(vmem_limit_bytes=VMEM_LIMIT))(w, g, m, v)


def _sum_adamw_rows(R):
    return max(r for r in range(16, min(R, 256) + 1, 16) if R % r == 0)


def _sum_adamw_steps(R):
    return R // _sum_adamw_rows(R)


def _sum_first(own, recvb, name, tasks):
    R, C = own.shape
    rc = _sum_adamw_rows(R)

    def body(own_ref, r_ref, p_ref):
        p_ref[...] = (own_ref[...] + r_ref[0].astype(F32)) + r_ref[1].astype(F32)

    blk = pl.BlockSpec((rc, C), lambda i: (i, 0))
    (part,), got = _hosted_call(body, name, (R // rc,), [blk, pl.BlockSpec((2, rc, C), lambda i: (0, i, 0))], [blk],
                                [jax.ShapeDtypeStruct((R, C), F32)], (own, recvb), tasks)
    return part, got


def _sum_adamw(own, recvb, w, m, v, name, done=0):
    R, C = own.shape
    rc = _sum_adamw_rows(R)
    left = 3 - done
    assert 3 % left == 0

    def body(own_ref, r_ref, w_ref, m_ref, v_ref, g_ref, d_ref, nm_ref, nv_ref):
        g = own_ref[...]
        for j in range(left):
            g = g + r_ref[j].astype(F32)
        g_ref[...] = g
        d_ref[...], nm_ref[...], nv_ref[...] = _adam_update(w_ref[...], g, m_ref[...], v_ref[...])

    blk = pl.BlockSpec((rc, C), lambda i: (i, 0))
    s = jax.ShapeDtypeStruct((R, C), F32)
    return pl.pallas_call(
        body, name=name, grid=(R // rc,),
        in_specs=[blk, pl.BlockSpec((left, rc, C), lambda i: (done // left, i, 0)), blk, blk, blk],
        out_specs=[blk, blk, blk, blk], out_shape=[s, s, s, s], compiler_params=_params("parallel"),
    )(own, recvb, w, m, v)


def _by_device(dw):
    return dw.reshape(N_DEV, dw.shape[0] // N_DEV, dw.shape[1])


def _local_step(x, mod, g_attn, w_in, bias, sw_sink, g_na_out, g_sw_out, w_out, g_ffn, w_up, conv_w, conv_b, w_down,
                g_final, target, sharded):
    B, S, D = x.shape
    T = B * S
    x2d = x.reshape(T, D)
    mod3 = mod.reshape(B, 6, D)
    cos_t, sin_t = _rope_tables(S)
    sink = sw_sink.reshape(SW_HEADS)
    n_tiles = T // WIDE_TILE
    full = lambda g: g.reshape(N_DEV * g.shape[1], g.shape[2])

    rider = lambda w, mid, lo, n, into=None: [_gather_task(w, mid, rows=(lo, n), into=into)] if sharded else []
    if sharded:
        qu, hd = w_up.shape[0] // 4, w_down.shape[0] // 2
    (h, qkv), got = _attn_in(x2d, mod3, g_attn, w_in, cos_t, sin_t, S, rider(w_up, 3 * n_tiles // 4, 0, qu) if sharded else [])
    if sharded:
        w_up_buf = got[0][0]
    qkv3 = qkv.reshape(B, S, IN_WIDTH)
    na_steps, sw_steps = B * (NA_PAIRS // NA_STEP_PAIRS), B * SW_FWD_SPLIT
    (oa,), got = _na_fwd(qkv3, bias, rider(w_up, na_steps - 1, qu, 2 * qu, w_up_buf) if sharded else [])
    if sharded:
        w_up_buf = got[0][0]
    oa = oa.reshape(T, NA_WIDTH)
    (ob,), got = _sw_fwd(sink, qkv3,
                         rider(w_up, sw_steps // 2, 3 * qu, qu, w_up_buf) + [_gather_task(w_out, sw_steps - 1)] if sharded else [])
    if sharded:
        w_up, w_out = full(got[0][0]), full(got[1][0])
    ob = ob.reshape(T, SW_WIDTH)
    (mixin, mix, x1), _ = _attn_out(oa, ob, x2d, mod3, g_na_out, g_sw_out, w_out, S)
    (h2, val, gt), got = _ffn_up(x1, mod3, g_ffn, w_up, S, rider(w_down, 3 * n_tiles // 4, 0, 2 * hd) if sharded else [])
    if sharded:
        w_down = full(got[0][0])
    a, act, vd, dx2, df, gstat_f, bstat_f = _ffn_down(gt, val, conv_w, conv_b, w_down, x1, mod3, g_final, target.reshape(T, D), B, S)
    F = val.shape[1]

    dw_down = _matmul_tn(a, df, "dw_down")
    (dval, dgc, cstat), got = _ffn_down_bwd(df, w_down, act, vd, [_swap_task(_by_device(dw_down))] if sharded else [])
    if sharded:
        send_down, own_down = _chip_sums(_by_device(dw_down), got[0][0])
    (du, dx1, dmix, gstat_u, bstat_u, cstat_w), got = _ffn_up_bwd(dgc, dval, gt, conv_w, w_up, x1, mod3, g_ffn, dx2, mix, B, S,
                                                                  [_exchange_task(send_down)] if sharded else [])
    if sharded:
        dw_down = (own_down, got[0][0])
    dw_up = _matmul_tn(du, h2, "dw_up", tm=F)
    dw_out = _matmul_tn(mixin, dmix, "dw_out")
    (doa, dob, gstat_o), got = _attn_out_bwd(dmix, w_out, oa, ob, g_na_out, g_sw_out,
                                             [_swap_task(_by_device(dw_up)), _swap_task(_by_device(dw_out))] if sharded else [])
    if sharded:
        send_up, own_up = _chip_sums(_by_device(dw_up), got[0][0])
        send_out, own_out = _chip_sums(_by_device(dw_out), got[1][0])
    (dqa, dka, dva, dbt), got = _na_bwd(qkv3, bias, doa.reshape(B, S, NA_WIDTH),
                                        [_exchange_task(send_up), _exchange_task(send_out)] if sharded else [])
    if sharded:
        dw_up, dw_out = (own_up, got[0][0]), (own_out, got[1][0])
    dqb, dkb, dvb, dsink = _sw_bwd(sink, qkv3, dob.reshape(B, S, SW_WIDTH))
    r2 = lambda t: t.reshape(T, t.shape[-1])
    grad_x, dproj, gstat_i, bstat_i = _attn_in_bwd(r2(dqa), r2(dka), r2(dva), r2(dqb), r2(dkb), r2(dvb), cos_t, sin_t, w_in, x2d, mod3,
                                                   g_attn, dx1, B, S)
    dw_in = _matmul_tn(dproj, h, "dw_in")

    dmod = jnp.stack([bstat_i[:, 0], bstat_i[:, 1], bstat_u[:, 2], bstat_u[:, 0], bstat_u[:, 1], bstat_f[:, 0]], axis=1)
    small = dict(g_attn=gstat_i[0], g_ffn=gstat_u[0], g_final=gstat_f[0], loss=gstat_f[1, 0], g_na_out=gstat_o[0], g_sw_out=gstat_o[1],
                 sw_sink=dsink[:, 0], conv_b=cstat[0], conv_w=cstat_w[1:4], dbt=dbt,
                 raw=(bstat_i, bstat_u, bstat_f, gstat_i, gstat_u, gstat_f, gstat_o, dsink, cstat, cstat_w))
    return grad_x.reshape(B, S, D), dict(w_in=dw_in, w_out=dw_out, w_up=dw_up, w_down=dw_down), dmod, small


def _pack_slab(raw, drpb):
    D, F = raw[3].shape[1], raw[8].shape[1]
    n_seq = raw[0].shape[0]

    def body(bi_ref, bu_ref, bf_ref, gi_ref, gu_ref, gf_ref, go_ref, ds_ref, cs_ref, cw_ref, rp_ref, o_ref):
        o_ref[...] = jnp.zeros_like(o_ref)
        for b in range(n_seq):
            mods = (bi_ref[b, 0:1, :], bi_ref[b, 1:2, :], bu_ref[b, 2:3, :], bu_ref[b, 0:1, :], bu_ref[b, 1:2, :], bf_ref[b, 0:1, :])
            for k, row in enumerate(mods):
                o_ref[b:b + 1, k * D:(k + 1) * D] = row
        o = 0
        for row in (gi_ref[0:1, :], gu_ref[0:1, :], gf_ref[0:1, :], go_ref[0:1, :], go_ref[1:2, :]):
            o_ref[2:3, o:o + row.shape[1]] = row
            o += row.shape[1]
        ds = ds_ref[...]
        eye = lax.broadcasted_iota(jnp.int32, ds.shape, 0) == lax.broadcasted_iota(jnp.int32, ds.shape, 1)
        o_ref[2:3, o:o + 128] = jnp.sum(jnp.where(eye, ds, 0.0), axis=0, keepdims=True)
        o_ref[2:3, o + 128:o + 256] = gf_ref[1:2, 0:128]
        o_ref[3:4, 0:F] = cs_ref[0:1, :]
        o_ref[4:5, 0:rp_ref.shape[1]] = rp_ref[...]
        o_ref[5:8, 0:F] = cw_ref[1:4, :]

    return pl.pallas_call(body, name="pack_slab", out_shape=jax.ShapeDtypeStruct((8, PACK_W), F32),
                          compiler_params=pltpu.CompilerParams(vmem_limit_bytes=VMEM_LIMIT))(*raw, drpb)


def kernel(x, c, w_ada, b_ada, g_attn, w_in, na_rpb, sw_sink, g_na_out, g_sw_out, w_out, g_ffn, w_up, conv_w, conv_b, w_down, g_final, loss_target, m_w_ada, m_b_ada, m_g_attn, m_w_in, m_na_rpb, m_sw_sink, m_g_na_out, m_g_sw_out, m_w_out, m_g_ffn, m_w_up, m_conv_w, m_conv_b, m_w_down, m_g_final, v_w_ada, v_b_ada, v_g_attn, v_w_in, v_na_rpb, v_sw_sink, v_g_na_out, v_g_sw_out, v_w_out, v_g_ffn, v_w_up, v_conv_w, v_conv_b, v_w_down, v_g_final):
    B, S, D = x.shape
    me = 4 * lax.axis_index("x") + 2 * lax.axis_index("y") + lax.axis_index("c")
    ada_c = w_ada.shape[2]
    F_l = conv_w.shape[2]

    tr = {"w_in", "w_up"}
    w_in_t = jnp.transpose(w_in[0])
    shards_f = dict(w_out=w_out[0], w_up=jnp.transpose(w_up[0]), w_down=w_down[0])

    c_all, conv_w_f, mod_all, w_in_all, casts, bias = _ada_fwd(c, conv_w[0], w_ada[0], b_ada, w_in_t, list(shards_f.values()),
                                                               _na_bias_rev(na_rpb[0]))
    shards = dict(zip(shards_f, casts))
    mod_mine = lax.dynamic_slice(mod_all, (0, me * B, 0), (N_DEV, B, ada_c))
    mod = jnp.transpose(mod_mine, (1, 0, 2)).reshape(B, N_DEV * ada_c)
    w_in_f = w_in_all.reshape(N_DEV * w_in_t.shape[0], D)

    grad_x, dw, dmod, small = _local_step(x, mod, g_attn, w_in_f, bias, sw_sink, g_na_out, g_sw_out, shards["w_out"], g_ffn,
                                          shards["w_up"], conv_w_f, conv_b, shards["w_down"], g_final.reshape(1, D), loss_target,
                                          sharded=True)
    g8_in = _by_device(dw["w_in"])
    drpb, got = _na_bias_grad(small["dbt"], [_swap_task(g8_in)])
    send_in, own_in = _chip_sums(g8_in, got[0][0])

    slab = _pack_slab(small["raw"], drpb.reshape(1, -1))
    weights = dict(w_ada=w_ada, b_ada=b_ada, g_attn=g_attn, w_in=w_in, na_rpb=na_rpb, sw_sink=sw_sink, g_na_out=g_na_out,
                   g_sw_out=g_sw_out, w_out=w_out, g_ffn=g_ffn, w_up=w_up, conv_w=conv_w, conv_b=conv_b, w_down=w_down, g_final=g_final)
    ms = dict(w_ada=m_w_ada, b_ada=m_b_ada, g_attn=m_g_attn, w_in=m_w_in, na_rpb=m_na_rpb, sw_sink=m_sw_sink, g_na_out=m_g_na_out,
              g_sw_out=m_g_sw_out, w_out=m_w_out, g_ffn=m_g_ffn, w_up=m_w_up, conv_w=m_conv_w, conv_b=m_conv_b, w_down=m_w_down, g_final=m_g_final)
    vs = dict(w_ada=v_w_ada, b_ada=v_b_ada, g_attn=v_g_attn, w_in=v_w_in, na_rpb=v_na_rpb, sw_sink=v_sw_sink, g_na_out=v_g_na_out,
              g_sw_out=v_g_sw_out, w_out=v_w_out, g_ffn=v_g_ffn, w_up=v_w_up, conv_w=v_conv_w, conv_b=v_conv_b, w_down=v_w_down, g_final=v_g_final)
    names = list(weights)
    grads, deltas, new_m, new_v = {}, {}, {}, {}
    flat = lambda t: t.reshape(1, -1)

    def shard2d(nm):
        if nm in tr:
            return (lambda t: jnp.transpose(t[0])), (lambda t: jnp.transpose(t)[None])
        if nm == "conv_w":
            return (lambda t: jnp.transpose(t, (1, 0, 2))), (lambda t: jnp.transpose(t, (1, 0, 2)))
        return (lambda t: t[0]), (lambda t: t[None])

    def finish_sum(nm, own, recvb, done=0):
        r, back = shard2d(nm)
        g2, d_, m_, v_ = _sum_adamw(own, recvb, r(weights[nm]), r(ms[nm]), r(vs[nm]), "adamw_" + nm, done)
        grads[nm], deltas[nm], new_m[nm], new_v[nm] = back(g2), back(d_), back(m_), back(v_)

    own_up, recv_up = dw["w_up"]
    n_up = _sum_adamw_steps(own_up.shape[0])
    part_up, got = _sum_first(own_up, recv_up, "sum_first_w_up", [_gather_task(slab, n_up - 1), _exchange_task(send_in)])
    packs = got[0][0]
    finish_sum("w_up", part_up, recv_up, done=2)
    finish_sum("w_down", *dw["w_down"])
    finish_sum("w_in", own_in, got[1][0])
    finish_sum("w_out", *dw["w_out"])

    where = dict(b_ada=((0, 1), 0), g_attn=((2,), 0), g_ffn=((2,), D), g_final=((2,), 2 * D), g_na_out=((2,), 3 * D),
                 g_sw_out=((2,), 3 * D + NA_WIDTH), sw_sink=((2,), 3 * D + NA_WIDTH + SW_WIDTH), conv_b=((3,), 0), na_rpb=((4,), 0))

    def small_view(n):
        if n == "na_rpb":
            return (lambda t: jnp.transpose(t[0], (1, 0, 2))), (lambda t: jnp.transpose(t, (1, 0, 2))[None])
        return flat, (lambda t: t.reshape(weights[n].shape))

    tot, loss, small_out = _pack_sum_adamw(
        packs.reshape(N_DEV * 8, PACK_W),
        [tuple(small_view(n)[0](t[n]) for t in (weights, ms, vs)) + where[n] for n in where],
        (2, 3 * D + NA_WIDTH + SW_WIDTH + 128))
    for n, res in zip(where, small_out):
        grads[n], deltas[n], new_m[n], new_v[n] = [small_view(n)[1](t) for t in res]
    loss = loss.reshape(())

    for nm, g2 in (("w_ada", _ada_bwd(c_all, packs.reshape(N_DEV * 8, PACK_W), ada_c)), ("conv_w", lax.dynamic_slice(tot, (5, me * F_l), (3, F_l))[:, None])):
        r, back = shard2d(nm)
        d_, m_, v_ = _adamw(r(weights[nm]), g2, r(ms[nm]), r(vs[nm]), "adamw_" + nm)
        grads[nm], deltas[nm], new_m[nm], new_v[nm] = back(g2), back(d_), back(m_), back(v_)
    return (loss, grad_x, *[grads[n] for n in names], *[deltas[n] for n in names], *[new_m[n] for n in names],
            *[new_v[n] for n in names])
```

```python
import functools

import numpy as np
import jax
import jax.numpy as jnp
from jax import lax
from jax.experimental import pallas as pl
from jax.experimental.pallas import tpu as pltpu

F32, BF16 = jnp.float32, jnp.bfloat16
MESH_ID = pl.DeviceIdType.MESH
N_DEV = 8

HEAD_DIM = 64
NA_HEADS = 8
SW_HEADS = 8
SW_KV_HEADS = 2
SW_GROUP = SW_HEADS // SW_KV_HEADS
NA_WIDTH = NA_HEADS * HEAD_DIM
SW_WIDTH = SW_HEADS * HEAD_DIM
SW_KV_WIDTH = SW_KV_HEADS * HEAD_DIM
ROPE_WIDTH = SW_WIDTH + SW_KV_WIDTH
IN_WIDTH = 3 * NA_WIDTH + SW_WIDTH + 2 * SW_KV_WIDTH
ROPE_LO = 3 * NA_WIDTH
GRID_W = 64
NA_ROWS_MAX = 8
NA_COLS = 16
N_DR = 2 * NA_ROWS_MAX - 1
N_DC = 2 * NA_COLS - 1
SW_WINDOW = 128
SW_BLOCK = 128
ROPE_THETA = 10000.0
EPS = 1e-6
NEG = -1e30
Q_SCALE = HEAD_DIM ** -0.5

ADAM_LR = 0.001
ADAM_B1 = 0.9
ADAM_B2 = 0.999
ADAM_EPS = 1e-08
ADAM_WD = 0.01
ADAM_STEP = 10

TOKEN_TILE = 256
WIDE_TILE = 512
VMEM_LIMIT = 56 * 1024 * 1024

PACK_W = 6144


def _nn(a, b):
    return jnp.dot(a, b, preferred_element_type=F32)


def _nt(a, b):
    return lax.dot_general(a, b, (((1,), (1,)), ((), ())), preferred_element_type=F32)


def _tn(a, b):
    return lax.dot_general(a, b, (((0,), (0,)), ((), ())), preferred_element_type=F32)


def _rms(x):
    r = lax.rsqrt(jnp.mean(x * x, axis=-1, keepdims=True) + EPS)
    return x * r, r


def _rms_bwd(xn, r, gy):
    return r * (gy - xn * jnp.mean(xn * gy, axis=-1, keepdims=True))


def _params(*sem):
    return pltpu.CompilerParams(dimension_semantics=sem, vmem_limit_bytes=VMEM_LIMIT)


def _full(shape):
    n = len(shape)
    return pl.BlockSpec(shape, lambda *_: (0,) * n)


def _mesh_pos():
    return lax.axis_index("x"), lax.axis_index("y"), lax.axis_index("c")


def _row_chunk(r):
    for rc in (128, 64, 32, 16):
        if r % rc == 0:
            return rc
    raise ValueError(f"rows {r} not a multiple of 16")


class _Task:
    def __init__(self, inputs, out_shapes, sems, start, finish, mid=None, mid_step=None, alias=None):
        self.inputs, self.out_shapes, self.sems = list(inputs), list(out_shapes), list(sems)
        self.start, self.finish, self.mid, self.mid_step = start, finish, mid, mid_step
        self.alias = alias


def _hosted_call(body, name, grid, in_specs, out_specs, out_shape, operands, tasks, scratch_shapes=()):
    n_in, n_out, n_scr = len(in_specs), len(out_specs), len(scratch_shapes)
    t_in = [len(t.inputs) for t in tasks]
    t_out = [len(t.out_shapes) for t in tasks]
    t_sem = [len(t.sems) for t in tasks]
    n_steps = int(np.prod(grid))

    def wrapped(*refs):
        ins, rest = refs[:n_in], refs[n_in:]
        task_ins, rest = rest[:sum(t_in)], rest[sum(t_in):]
        outs, rest = rest[:n_out], rest[n_out:]
        task_outs, rest = rest[:sum(t_out)], rest[sum(t_out):]
        scr, task_sems = rest[:n_scr], rest[n_scr:]
        step = pl.program_id(0)
        for ax in range(1, len(grid)):
            step = step * grid[ax] + pl.program_id(ax)
        parts = []
        oi = oo = os_ = 0
        for t, a, b, c in zip(tasks, t_in, t_out, t_sem):
            parts.append((t, task_ins[oi:oi + a], task_outs[oo:oo + b], task_sems[os_:os_ + c]))
            oi, oo, os_ = oi + a, oo + b, os_ + c
        for t, ti, to, ts in parts:
            pl.when(step == 0)(functools.partial(t.start, ti, to, ts))
            if t.mid is not None:
                pl.when(step == t.mid_step)(functools.partial(t.mid, ti, to, ts))
        body(*ins, *outs, *scr)
        for t, ti, to, ts in parts:
            pl.when(step == n_steps - 1)(functools.partial(t.finish, ti, to, ts))

    hbm = pl.BlockSpec(memory_space=pl.ANY)
    aliases, oi, oo = {}, n_in, n_out
    for t, a, b in zip(tasks, t_in, t_out):
        if t.alias is not None:
            aliases[oi + t.alias[0]] = oo + t.alias[1]
        oi, oo = oi + a, oo + b
    res = pl.pallas_call(
        wrapped, name=name, grid=grid,
        in_specs=list(in_specs) + [hbm] * sum(t_in),
        out_specs=list(out_specs) + [hbm] * sum(t_out),
        out_shape=list(out_shape) + [s for t in tasks for s in t.out_shapes],
        scratch_shapes=list(scratch_shapes) + [s for t in tasks for s in t.sems],
        input_output_aliases=aliases,
        compiler_params=_params(*(["arbitrary"] * len(grid))),
    )(*operands, *[a for t in tasks for a in t.inputs])
    own, extra = res[:n_out], res[n_out:]
    per_task, o = [], 0
    for b in t_out:
        per_task.append(extra[o:o + b])
        o += b
    return own, per_task


def _gather_task(shard, mid_step, rows=None, into=None):
    lo, n = (0, shard.shape[0]) if rows is None else rows

    def parts(ins, outs, sems):
        x_ref, out_ref, (send_sems, recv_sems, local_sem) = ins[0], outs[0], sems
        x_, y_, c_ = _mesh_pos()
        me, sibling = (x_, y_, c_), (x_, y_, 1 - c_)
        chips = [(1 - x_, y_), (x_, 1 - y_), (1 - x_, 1 - y_)]
        x_ref = x_ref.at[pl.ds(lo, n)]

        def rows(px, py, pc):
            return out_ref.at[4 * px + 2 * py + pc, pl.ds(lo, n)]

        def copy(k, block, to, src=None):
            return pltpu.make_async_remote_copy(
                src_ref=rows(*block) if src is None else src, dst_ref=rows(*block),
                send_sem=send_sems.at[k], recv_sem=recv_sems.at[k], device_id=to, device_id_type=MESH_ID)

        return dict(
            mine=lambda: pltpu.make_async_copy(x_ref, rows(*me), local_sem),
            first=lambda: [copy(0, me, sibling, src=x_ref)] + [copy(1 + j, me, (*chip, c_), src=x_ref) for j, chip in enumerate(chips)],
            passed=lambda: [copy(4 + j, (*chip, c_), sibling) for j, chip in enumerate(chips)],
            landed=lambda: [copy(1 + j, (*chip, c_), me) for j, chip in enumerate(chips)],
            last=lambda: [copy(0, sibling, me)] + [copy(4 + j, (*chip, 1 - c_), me) for j, chip in enumerate(chips)])

    def start(ins, outs, sems):
        p = parts(ins, outs, sems)
        p["mine"]().start()
        for cp in p["first"]():
            cp.start()

    def mid(ins, outs, sems):
        p = parts(ins, outs, sems)
        for cp, fw in zip(p["landed"](), p["passed"]()):
            cp.wait_recv()
            fw.start()

    def finish(ins, outs, sems):
        p = parts(ins, outs, sems)
        for cp in p["last"]():
            cp.wait_recv()
        for cp in p["first"]() + p["passed"]():
            cp.wait_send()
        p["mine"]().wait()

    return _Task([shard] if into is None else [shard, into], [jax.ShapeDtypeStruct((N_DEV,) + shard.shape, shard.dtype)],
                 [pltpu.SemaphoreType.DMA((7,)), pltpu.SemaphoreType.DMA((7,)), pltpu.SemaphoreType.DMA],
                 start, finish, mid, mid_step, alias=None if into is None else (1, 0))


def _swap_task(g8):
    _, R, C = g8.shape

    def copies(ins, outs, sems):
        (g_ref,), (recv_ref,), (ss, rs) = ins, outs, sems
        x_, y_, c_ = _mesh_pos()
        return [pltpu.make_async_remote_copy(src_ref=g_ref.at[2 * k + (1 - c_)], dst_ref=recv_ref.at[k], send_sem=ss.at[k],
                                             recv_sem=rs.at[k], device_id=(x_, y_, 1 - c_), device_id_type=MESH_ID)
                for k in range(4)]

    def start(ins, outs, sems):
        for cp in copies(ins, outs, sems):
            cp.start()

    def finish(ins, outs, sems):
        cps = copies(ins, outs, sems)
        for cp in cps:
            cp.wait_recv()
        for cp in cps:
            cp.wait_send()

    return _Task([g8], [jax.ShapeDtypeStruct((4, R, C), g8.dtype)],
                 [pltpu.SemaphoreType.DMA((4,)), pltpu.SemaphoreType.DMA((4,))], start, finish)


def _chip_sums(g8, recva):
    _, R, C = g8.shape
    rt = _sum_adamw_rows(R)
    rc = _row_chunk(rt)

    def body(core_ref, g_ref, a_ref, send_ref, own_ref):
        x_, y_, _ = _mesh_pos()
        chips = [(1 - x_, y_), (x_, 1 - y_), (1 - x_, 1 - y_), (x_, y_)]

        def chunk(i, carry):
            rows = pl.ds(pl.multiple_of(i * rc, rc), rc)
            for j, (tx, ty) in enumerate(chips):
                k = 2 * tx + ty
                s = g_ref[k, rows, :].astype(F32) + a_ref[k, rows, :].astype(F32)
                if j < 3:
                    send_ref[j, rows, :] = s.astype(BF16)
                else:
                    own_ref[rows, :] = s
            return carry

        lax.fori_loop(0, rt // rc, chunk, 0)

    core = lax.axis_index("c").astype(jnp.int32).reshape(1)
    return pl.pallas_call(
        body, name="chip_sums",
        grid_spec=pltpu.PrefetchScalarGridSpec(
            num_scalar_prefetch=1, grid=(R // rt,),
            in_specs=[pl.BlockSpec((4, None, rt, C), lambda i, c: (0, c[0], i, 0)), pl.BlockSpec((4, rt, C), lambda i, c: (0, i, 0))],
            out_specs=[pl.BlockSpec((3, rt, C), lambda i, c: (0, i, 0)), pl.BlockSpec((rt, C), lambda i, c: (i, 0))]),
        out_shape=[jax.ShapeDtypeStruct((3, R, C), BF16), jax.ShapeDtypeStruct((R, C), F32)],
        compiler_params=_params("parallel"),
    )(core, g8.reshape(4, 2, R, C), recva)


def _exchange_task(sendb):
    def copies(ins, outs, sems):
        (s_ref,), (recv_ref,), (ss, rs) = ins, outs, sems
        x_, y_, c_ = _mesh_pos()
        flips = [(1 - x_, y_), (x_, 1 - y_), (1 - x_, 1 - y_)]
        return [pltpu.make_async_remote_copy(src_ref=s_ref.at[j], dst_ref=recv_ref.at[j], send_sem=ss.at[j], recv_sem=rs.at[j],
                                             device_id=(tx, ty, c_), device_id_type=MESH_ID) for j, (tx, ty) in enumerate(flips)]

    def start(ins, outs, sems):
        for cp in copies(ins, outs, sems):
            cp.start()

    def finish(ins, outs, sems):
        cps = copies(ins, outs, sems)
        for cp in cps:
            cp.wait_recv()
        for cp in cps:
            cp.wait_send()

    return _Task([sendb], [jax.ShapeDtypeStruct(sendb.shape, sendb.dtype)],
                 [pltpu.SemaphoreType.DMA((3,)), pltpu.SemaphoreType.DMA((3,))], start, finish)


def _silu(v):
    return v * (1.0 / (1.0 + jnp.exp(-v)))


def _ada_fwd(c, taps, w_ada_l, b_ada, w_in_shard, casts, rpb_rev):
    n_seq = c.shape[0]
    D, cols = w_ada_l.shape
    F_l = taps.shape[1]
    W = D + F_l
    n_rows = N_DEV * n_seq
    n_c = len(casts)
    t_c = _gather_task(jax.ShapeDtypeStruct((8, W), F32), 0)
    half = w_in_shard.shape[0] // 2
    t_w = [_gather_task(jax.ShapeDtypeStruct(w_in_shard.shape, BF16), 0, rows=(k * half, half)) for k in range(2)]
    t_m = _gather_task(jax.ShapeDtypeStruct((n_rows, cols), F32), 0)

    def body(c_ref, taps_ref, w_ref, b_ref, ws_ref, rev_ref, *refs):
        cast_in, refs = refs[:n_c], refs[n_c:]
        (slabs_ref, call_ref, taps_all_ref, mod_ref, win_ref), refs = refs[:5], refs[5:]
        cast_out, bias_ref, refs = refs[:n_c], refs[n_c], refs[n_c + 1:]
        (slab_vm, c_vm, m_vm, ws_f, ws_b), refs = refs[:5], refs[5:]
        stage_f, stage_b, refs = refs[:n_c], refs[n_c:2 * n_c], refs[2 * n_c:]
        copy_sem, in_sems, out_sems, sems = refs[0], refs[1], refs[2], refs[3:]
        sc, sw, sm = sems[0:3], (sems[3:6], sems[6:9]), sems[9:12]
        x_, y_, c_ = _mesh_pos()
        slab_vm[...] = jnp.zeros_like(slab_vm)
        slab_vm[0:n_seq, 0:D] = c_ref[...]
        slab_vm[0:taps.shape[0], D:W] = taps_ref[...]
        t_c.start((slab_vm,), (slabs_ref,), sc)
        cp = pltpu.make_async_copy(ws_ref, ws_f, copy_sem)
        cp.start()
        cp.wait()
        ws_b[...] = ws_f[...].astype(BF16)
        for k in range(2):
            t_w[k].start((ws_b,), (win_ref,), sw[k])
        loads = [pltpu.make_async_copy(cast_in[j], stage_f[j], in_sems.at[j]) for j in range(n_c)]
        for ld in loads:
            ld.start()
        t_c.mid((slab_vm,), (slabs_ref,), sc)
        t_c.finish((slab_vm,), (slabs_ref,), sc)
        cp = pltpu.make_async_copy(slabs_ref, c_vm, copy_sem)
        cp.start()
        cp.wait()
        c_all = c_vm[:, :, 0:D].reshape(N_DEV * 8, D)
        call_ref[...] = c_all
        for j in range(N_DEV):
            taps_all_ref[:, j * F_l:(j + 1) * F_l] = c_vm[j, 0:taps.shape[0], D:W]
        b_mine = b_ref[:, pl.ds(pl.multiple_of((4 * x_ + 2 * y_ + c_) * cols, 128), cols)]
        m64 = jnp.dot(_silu(c_all), w_ref[...], precision=lax.Precision.HIGHEST, preferred_element_type=F32) + b_mine
        r = lax.broadcasted_iota(jnp.int32, (n_rows, N_DEV * 8), 0)
        c = lax.broadcasted_iota(jnp.int32, (n_rows, N_DEV * 8), 1)
        pick = jnp.where(c == 8 * (r // n_seq) + r % n_seq, 1.0, 0.0)
        m_vm[...] = jnp.dot(pick, m64, precision=lax.Precision.HIGHEST, preferred_element_type=F32)
        t_m.start((m_vm,), (mod_ref,), sm)
        stores = [pltpu.make_async_copy(stage_b[j], cast_out[j], out_sems.at[j]) for j in range(n_c)]
        for j in range(n_c):
            loads[j].wait()
            stage_b[j][...] = stage_f[j][...].astype(BF16)
            stores[j].start()
        for p in range(NA_PAIRS):
            _na_bias_rows(rev_ref, bias_ref, p)
        for k in range(2):
            t_w[k].mid((ws_b,), (win_ref,), sw[k])
        t_m.mid((m_vm,), (mod_ref,), sm)
        t_m.finish((m_vm,), (mod_ref,), sm)
        for k in range(2):
            t_w[k].finish((ws_b,), (win_ref,), sw[k])
        for st in stores:
            st.wait()

    hbm, vm = pl.BlockSpec(memory_space=pl.ANY), pl.BlockSpec(memory_space=pltpu.VMEM)
    res = pl.pallas_call(
        body, name="ada_fwd", in_specs=[vm, vm, vm, vm, hbm, vm] + [hbm] * n_c,
        out_specs=[hbm, vm, vm, hbm, hbm] + [hbm] * n_c + [vm],
        out_shape=t_c.out_shapes + [jax.ShapeDtypeStruct((N_DEV * 8, D), F32), jax.ShapeDtypeStruct((taps.shape[0], N_DEV * F_l), F32)]
        + t_m.out_shapes + t_w[0].out_shapes
        + [jax.ShapeDtypeStruct(a.shape, BF16) for a in casts] + [jax.ShapeDtypeStruct((NA_PAIRS, N_DR * GRID_W, 128), F32)],
        scratch_shapes=[pltpu.VMEM((8, W), F32), pltpu.VMEM((N_DEV, 8, W), F32), pltpu.VMEM((n_rows, cols), F32),
                        pltpu.VMEM(w_in_shard.shape, F32), pltpu.VMEM(w_in_shard.shape, BF16)]
        + [pltpu.VMEM(a.shape, F32) for a in casts] + [pltpu.VMEM(a.shape, BF16) for a in casts]
        + [pltpu.SemaphoreType.DMA, pltpu.SemaphoreType.DMA((n_c,)), pltpu.SemaphoreType.DMA((n_c,))]
        + t_c.sems + t_w[0].sems + t_w[1].sems + t_m.sems,
        compiler_params=pltpu.CompilerParams(vmem_limit_bytes=VMEM_LIMIT),
    )(c, taps, w_ada_l, b_ada, w_in_shard, rpb_rev, *casts)
    return res[1], res[2], res[3], res[4], res[5:5 + n_c], res[5 + n_c]


def _ada_bwd(c_all, packs, cols):
    def body(c_ref, d_ref, o_ref):
        x_, y_, c_ = _mesh_pos()
        mine = d_ref[:, pl.ds(pl.multiple_of((4 * x_ + 2 * y_ + c_) * cols, 128), cols)]
        o_ref[...] = lax.dot_general(_silu(c_ref[...]), mine, (((0,), (0,)), ((), ())),
                                     precision=lax.Precision.HIGHEST, preferred_element_type=F32)
    return pl.pallas_call(body, name="ada_bwd", out_shape=jax.ShapeDtypeStruct((c_all.shape[1], cols), F32),
                          compiler_params=pltpu.CompilerParams(vmem_limit_bytes=VMEM_LIMIT))(c_all, packs)


NA_PAIRS = NA_HEADS // 2
N_DR_PAD = 16


def _na_bias_rev(na_rpb):
    rev = jnp.pad(jnp.flip(na_rpb, axis=2), ((0, 0), (0, N_DR_PAD - N_DR), (0, GRID_W - N_DC)))
    return jnp.transpose(rev.reshape(NA_PAIRS, 2, N_DR_PAD, GRID_W), (0, 2, 1, 3)).reshape(NA_PAIRS, N_DR_PAD, 128)


def _na_bias_rows(r_ref, o_ref, p):
    k = lax.broadcasted_iota(jnp.int32, (GRID_W, 128), 0)
    lane = lax.broadcasted_iota(jnp.int32, (GRID_W, 128), 1)
    q = lane % GRID_W
    cs = jnp.clip(q - NA_COLS // 2, 0, GRID_W - NA_COLS)
    ok = (k >= cs) & (k < cs + NA_COLS)
    left = lane < GRID_W
    for dr in range(N_DR):
        row = jnp.broadcast_to(r_ref[p, dr:dr + 1, :], (GRID_W, 128))
        r0 = jnp.where(left, row, 0.0)
        r1 = jnp.where(left, pltpu.roll(row, GRID_W, axis=1), 0.0)
        y0 = pltpu.roll(r0, 128 - (NA_COLS - 1), axis=1, stride=1, stride_axis=0)
        y1 = pltpu.roll(r1, GRID_W - (NA_COLS - 1), axis=1, stride=1, stride_axis=0)
        o_ref[p, dr * GRID_W:(dr + 1) * GRID_W, :] = jnp.where(ok, jnp.where(left, y0, y1), NEG)


def _na_bias_grad(db, tasks=()):
    a = np.arange(128)
    flip = jnp.asarray(((a[:, None] // GRID_W == a[None, :] // GRID_W)
                        & (a[:, None] % GRID_W + a[None, :] % GRID_W == GRID_W - 1)).astype(np.float32))

    def body(d_ref, j_ref, o_ref):
        o_ref[...] = jnp.zeros_like(o_ref)
        for dr in range(N_DR):
            t = jnp.dot(d_ref[0, dr * GRID_W:(dr + 1) * GRID_W, :], j_ref[...], precision=lax.Precision.HIGHEST, preferred_element_type=F32)
            t = pltpu.roll(t, GRID_W + NA_COLS, axis=1, stride=1, stride_axis=0)
            o_ref[0, dr:dr + 1, :] = jnp.sum(t, axis=0, keepdims=True)

    (rows,), got = _hosted_call(
        body, "rpb_reduce", (NA_PAIRS,),
        [pl.BlockSpec((1, N_DR * GRID_W, 128), lambda p: (p, 0, 0)), _full((128, 128))],
        [pl.BlockSpec((1, N_DR_PAD, 128), lambda p: (p, 0, 0))],
        [jax.ShapeDtypeStruct((NA_PAIRS, N_DR_PAD, 128), F32)], (db, flip), tasks)
    g = rows.reshape(NA_PAIRS, N_DR_PAD, 2, GRID_W)[:, :N_DR, :, :N_DC]
    return jnp.transpose(g, (0, 2, 1, 3)).reshape(-1), got


def _rope_tables(S):
    half = HEAD_DIM // 2
    inv = np.float32(ROPE_THETA) ** (-np.arange(half, dtype=np.float32) / np.float32(half))
    ang = np.arange(S).astype(np.float32)[:, None] * inv[None, :]
    cos, sin = np.cos(ang).astype(np.float32), np.sin(ang).astype(np.float32)
    return jnp.asarray(np.tile(np.concatenate([cos, cos], axis=1), (1, 2))), jnp.asarray(np.tile(np.concatenate([-sin, sin], axis=1), (1, 2)))


def _rope_spec(tps, tm=TOKEN_TILE):
    return pl.BlockSpec((tm, 2 * HEAD_DIM), lambda i: (i % tps, 0))


def _rot_half(t):
    w = t.shape[1]
    lane = lax.broadcasted_iota(jnp.int32, t.shape, 1)
    return jnp.where((lane % HEAD_DIM) < HEAD_DIM // 2, pltpu.roll(t, w - HEAD_DIM // 2, axis=1),
                     pltpu.roll(t, HEAD_DIM // 2, axis=1))


def _tok_spec(w, tm=TOKEN_TILE):
    return pl.BlockSpec((tm, w), lambda i: (i, 0))


def _mod_spec(tps, d):
    return pl.BlockSpec((1, 6, d), lambda i: (i // tps, 0, 0))


def _bstat_spec(tps, w):
    return pl.BlockSpec((1, 8, w), lambda i: (i // tps, 0, 0))


def _attn_in(x2d, mod3, g_attn, w_in, cos_t, sin_t, S, tasks=(), tm=WIDE_TILE):
    T, D = x2d.shape
    tps = S // tm

    def body(x_ref, mod_ref, g_ref, w_ref, cos_ref, sin_ref, h_ref, qkv_ref):
        xn, _ = _rms(x_ref[...])
        h = (xn * g_ref[...]) * (1.0 + mod_ref[0, 1:2, :]) + mod_ref[0, 0:1, :]
        hb = h.astype(BF16)
        h_ref[...] = hb
        proj = _nt(hb, w_ref[...])
        rb = proj[:, ROPE_LO:ROPE_LO + ROPE_WIDTH]
        reps = (1, ROPE_WIDTH // (2 * HEAD_DIM))
        rb = rb * jnp.tile(cos_ref[...], reps) + _rot_half(rb) * jnp.tile(sin_ref[...], reps)
        qkv_ref[:, 0:NA_WIDTH] = (proj[:, 0:NA_WIDTH] * Q_SCALE).astype(BF16)
        qkv_ref[:, NA_WIDTH:ROPE_LO] = proj[:, NA_WIDTH:ROPE_LO].astype(BF16)
        qkv_ref[:, ROPE_LO:ROPE_LO + SW_WIDTH] = (rb[:, 0:SW_WIDTH] * Q_SCALE).astype(BF16)
        qkv_ref[:, ROPE_LO + SW_WIDTH:ROPE_LO + ROPE_WIDTH] = rb[:, SW_WIDTH:].astype(BF16)
        qkv_ref[:, ROPE_LO + ROPE_WIDTH:] = proj[:, ROPE_LO + ROPE_WIDTH:].astype(BF16)

    return _hosted_call(
        body, "attn_in", (T // tm,),
        [_tok_spec(D, tm), _mod_spec(tps, D), _full((1, D)), _full(w_in.shape), _rope_spec(tps, tm), _rope_spec(tps, tm)],
        [_tok_spec(D, tm), _tok_spec(IN_WIDTH, tm)],
        [jax.ShapeDtypeStruct((T, D), BF16), jax.ShapeDtypeStruct((T, IN_WIDTH), BF16)],
        (x2d, mod3, g_attn, w_in, cos_t, sin_t), tasks)


def _attn_out(oa, ob, x2d, mod3, g_na, g_sw, w_out, S, tasks=(), tm=WIDE_TILE):
    T, D = x2d.shape
    tps = S // tm

    def body(oa_ref, ob_ref, x_ref, mod_ref, gna_ref, gsw_ref, w_ref, mixin_ref, mix_ref, x1_ref):
        oan, _ = _rms(oa_ref[...])
        obn, _ = _rms(ob_ref[...])
        mixin = jnp.concatenate([oan * gna_ref[...], obn * gsw_ref[...]], axis=1).astype(BF16)
        mixin_ref[...] = mixin
        mix = _nn(mixin, w_ref[...])
        mix_ref[...] = mix
        x1_ref[...] = x_ref[...] + mod_ref[0, 2:3, :] * mix

    return _hosted_call(
        body, "attn_out", (T // tm,),
        [_tok_spec(NA_WIDTH, tm), _tok_spec(SW_WIDTH, tm), _tok_spec(D, tm), _mod_spec(tps, D),
         _full((1, NA_WIDTH)), _full((1, SW_WIDTH)), _full(w_out.shape)],
        [_tok_spec(NA_WIDTH + SW_WIDTH, tm), _tok_spec(D, tm), _tok_spec(D, tm)],
        [jax.ShapeDtypeStruct((T, NA_WIDTH + SW_WIDTH), BF16), jax.ShapeDtypeStruct((T, D), F32), jax.ShapeDtypeStruct((T, D), F32)],
        (oa, ob, x2d, mod3, g_na, g_sw, w_out), tasks)


def _ffn_up(x1, mod3, g_ffn, w_up, S, tasks=(), tm=WIDE_TILE):
    T, D = x1.shape
    F = w_up.shape[0] // 2
    tps = S // tm

    def body(x1_ref, mod_ref, g_ref, w_ref, h2_ref, val_ref, gt_ref):
        xn, _ = _rms(x1_ref[...])
        h2 = ((xn * g_ref[...]) * (1.0 + mod_ref[0, 4:5, :]) + mod_ref[0, 3:4, :]).astype(BF16)
        h2_ref[...] = h2
        u = _nt(h2, w_ref[...])
        val_ref[...] = u[:, :F].astype(BF16)
        gt_ref[...] = u[:, F:].astype(BF16)

    return _hosted_call(
        body, "ffn_up", (T // tm,), [_tok_spec(D, tm), _mod_spec(tps, D), _full((1, D)), _full(w_up.shape)],
        [_tok_spec(D, tm), _tok_spec(F, tm), _tok_spec(F, tm)],
        [jax.ShapeDtypeStruct((T, D), BF16), jax.ShapeDtypeStruct((T, F), BF16), jax.ShapeDtypeStruct((T, F), BF16)],
        (x1, mod3, g_ffn, w_up), tasks)


def _halo_specs(T, tps, w):
    per = TOKEN_TILE // 8
    prev = pl.BlockSpec((8, w), lambda i: (jnp.maximum(i * per - 1, 0), 0))
    nxt = pl.BlockSpec((8, w), lambda i: (jnp.minimum((i + 1) * per, T // 8 - 1), 0))
    return prev, nxt


def _seq_shifts(cur, before, after, ti, tps):
    tm = cur.shape[0]
    row = lax.broadcasted_iota(jnp.int32, cur.shape, 0)
    before = jnp.where(ti > 0, before.astype(F32), 0.0)
    after = jnp.where(ti < tps - 1, after.astype(F32), 0.0)
    return jnp.where(row == 0, before, pltpu.roll(cur, 1, axis=0)), jnp.where(row == tm - 1, after, pltpu.roll(cur, tm - 1, axis=0))


def _ffn_down(gt, val, conv_w, conv_b, w_down, x1, mod3, g_final, target, B, S):
    T, D = x1.shape
    F = gt.shape[1]
    tps = S // TOKEN_TILE
    prev, nxt = _halo_specs(T, tps, F)

    def body(gt_ref, prev_ref, next_ref, val_ref, cw_ref, cb_ref, w_ref, x1_ref, mod_ref, gf_ref, tgt_ref,
             a_ref, act_ref, vd_ref, dx2_ref, df_ref, gstat_ref, bstat_ref):
        i = pl.program_id(0)
        g = gt_ref[...].astype(F32)
        gprev, gnext = _seq_shifts(g, prev_ref[7:8, :], next_ref[0:1, :], i % tps, tps)
        gc = gprev * cw_ref[0:1, :] + g * cw_ref[1:2, :] + gnext * cw_ref[2:3, :] + cb_ref[...]
        sig = 1.0 / (1.0 + jnp.exp(-gc))
        act = gc * sig
        val = val_ref[...].astype(F32)
        act_ref[...] = act.astype(BF16)
        vd_ref[...] = (val * (sig + act - act * sig)).astype(BF16)
        a = (act * val).astype(BF16)
        a_ref[...] = a
        f = _nn(a, w_ref[...])
        gate = mod_ref[0, 5:6, :]
        x2 = x1_ref[...] + gate * f
        xn, r = _rms(x2)
        err = xn * gf_ref[...] - tgt_ref[...]
        dy = err * (1.0 / D)
        dx2 = _rms_bwd(xn, r, dy * gf_ref[...])
        dx2_ref[...] = dx2
        df_ref[...] = (gate * dx2).astype(BF16)

        @pl.when(i == 0)
        def _():
            gstat_ref[...] = jnp.zeros_like(gstat_ref)

        @pl.when(i % tps == 0)
        def _():
            bstat_ref[...] = jnp.zeros_like(bstat_ref)

        gstat_ref[0:1, :] += jnp.sum(dy * xn, axis=0, keepdims=True)
        tile_loss = jnp.sum(jnp.sum(err * err, axis=1, keepdims=True), axis=0, keepdims=True) * (0.5 / D)
        gstat_ref[1:2, :] += jnp.broadcast_to(tile_loss, (1, D))
        bstat_ref[0, 0:1, :] += jnp.sum(dx2 * f, axis=0, keepdims=True)

    return pl.pallas_call(
        body, name="ffn_down", grid=(T // TOKEN_TILE,),
        in_specs=[_tok_spec(F), prev, nxt, _tok_spec(F), _full(conv_w.shape), _full((1, F)), _full(w_down.shape),
                  _tok_spec(D), _mod_spec(tps, D), _full((1, D)), _tok_spec(D)],
        out_specs=[_tok_spec(F), _tok_spec(F), _tok_spec(F), _tok_spec(D), _tok_spec(D), _full((8, D)), _bstat_spec(tps, D)],
        out_shape=[jax.ShapeDtypeStruct((T, F), BF16), jax.ShapeDtypeStruct((T, F), BF16), jax.ShapeDtypeStruct((T, F), BF16),
                   jax.ShapeDtypeStruct((T, D), F32), jax.ShapeDtypeStruct((T, D), BF16),
                   jax.ShapeDtypeStruct((8, D), F32), jax.ShapeDtypeStruct((B, 8, D), F32)],
        compiler_params=_params("arbitrary"),
    )(gt, gt, gt, val, conv_w, conv_b, w_down, x1, mod3, g_final, target)


def _ffn_down_bwd(df, w_down, act, vd, tasks=(), tm=WIDE_TILE):
    T, D = df.shape
    F = act.shape[1]

    def body(df_ref, w_ref, act_ref, vd_ref, dval_ref, dgc_ref, cstat_ref):
        da = _nt(df_ref[...], w_ref[...])
        dval_ref[...] = (da * act_ref[...].astype(F32)).astype(BF16)
        dgc = da * vd_ref[...].astype(F32)
        dgc_ref[...] = dgc.astype(BF16)

        @pl.when(pl.program_id(0) == 0)
        def _():
            cstat_ref[...] = jnp.zeros_like(cstat_ref)

        cstat_ref[0:1, :] += jnp.sum(dgc, axis=0, keepdims=True)

    return _hosted_call(
        body, "ffn_down_bwd", (T // tm,),
        [_tok_spec(D, tm), _full(w_down.shape), _tok_spec(F, tm), _tok_spec(F, tm)],
        [_tok_spec(F, tm), _tok_spec(F, tm), _full((8, F))],
        [jax.ShapeDtypeStruct((T, F), BF16), jax.ShapeDtypeStruct((T, F), BF16), jax.ShapeDtypeStruct((8, F), F32)],
        (df, w_down, act, vd), tasks)


def _ffn_up_bwd(dgc, dval, gt, conv_w, w_up, x1, mod3, g_ffn, dx2, mix, B, S, tasks=()):
    T, D = x1.shape
    F = dgc.shape[1]
    tps = S // TOKEN_TILE
    prev, nxt = _halo_specs(T, tps, F)

    def body(dgc_ref, prev_ref, next_ref, dval_ref, gt_ref, cw_ref, w_ref, x1_ref, mod_ref, g_ref, dx2_ref, mix_ref,
             du_ref, dx1_ref, dmix_ref, gstat_ref, bstat_ref, cstat_ref):
        i = pl.program_id(0)
        d = dgc_ref[...].astype(F32)
        dprev, dnext = _seq_shifts(d, prev_ref[7:8, :], next_ref[0:1, :], i % tps, tps)
        g = gt_ref[...].astype(F32)

        @pl.when(i == 0)
        def _():
            cstat_ref[...] = jnp.zeros_like(cstat_ref)

        cstat_ref[1:2, :] += jnp.sum(dnext * g, axis=0, keepdims=True)
        cstat_ref[2:3, :] += jnp.sum(d * g, axis=0, keepdims=True)
        cstat_ref[3:4, :] += jnp.sum(dprev * g, axis=0, keepdims=True)
        dgt = dnext * cw_ref[0:1, :] + d * cw_ref[1:2, :] + dprev * cw_ref[2:3, :]
        du = jnp.concatenate([dval_ref[...], dgt.astype(BF16)], axis=1)
        du_ref[...] = du
        dh2 = _nn(du, w_ref[...])
        xn, r = _rms(x1_ref[...])
        scale1 = 1.0 + mod_ref[0, 4:5, :]
        xg = xn * g_ref[...]
        dx1 = dx2_ref[...] + _rms_bwd(xn, r, dh2 * g_ref[...] * scale1)
        dx1_ref[...] = dx1
        dmix_ref[...] = (mod_ref[0, 2:3, :] * dx1).astype(BF16)

        @pl.when(i == 0)
        def _():
            gstat_ref[...] = jnp.zeros_like(gstat_ref)

        @pl.when(i % tps == 0)
        def _():
            bstat_ref[...] = jnp.zeros_like(bstat_ref)

        gstat_ref[0:1, :] += jnp.sum(dh2 * scale1 * xn, axis=0, keepdims=True)
        bstat_ref[0, 0:1, :] += jnp.sum(dh2, axis=0, keepdims=True)
        bstat_ref[0, 1:2, :] += jnp.sum(dh2 * xg, axis=0, keepdims=True)
        bstat_ref[0, 2:3, :] += jnp.sum(dx1 * mix_ref[...], axis=0, keepdims=True)

    return _hosted_call(
        body, "ffn_up_bwd", (T // TOKEN_TILE,),
        [_tok_spec(F), prev, nxt, _tok_spec(F), _tok_spec(F), _full(conv_w.shape), _full(w_up.shape), _tok_spec(D),
         _mod_spec(tps, D), _full((1, D)), _tok_spec(D), _tok_spec(D)],
        [_tok_spec(2 * F), _tok_spec(D), _tok_spec(D), _full((8, D)), _bstat_spec(tps, D), _full((8, F))],
        [jax.ShapeDtypeStruct((T, 2 * F), BF16), jax.ShapeDtypeStruct((T, D), F32), jax.ShapeDtypeStruct((T, D), BF16),
         jax.ShapeDtypeStruct((8, D), F32), jax.ShapeDtypeStruct((B, 8, D), F32), jax.ShapeDtypeStruct((8, F), F32)],
        (dgc, dgc, dgc, dval, gt, conv_w, w_up, x1, mod3, g_ffn, dx2, mix), tasks)


def _attn_out_bwd(dmix, w_out, oa, ob, g_na, g_sw, tasks=(), tm=WIDE_TILE):
    T, D = dmix.shape

    def body(dmix_ref, w_ref, oa_ref, ob_ref, gna_ref, gsw_ref, doa_ref, dob_ref, gstat_ref):
        dmixin = _nt(dmix_ref[...], w_ref[...])

        @pl.when(pl.program_id(0) == 0)
        def _():
            gstat_ref[...] = jnp.zeros_like(gstat_ref)

        for k, (o_ref, g_ref, do_ref) in enumerate(((oa_ref, gna_ref, doa_ref), (ob_ref, gsw_ref, dob_ref))):
            dn = dmixin[:, k * NA_WIDTH:(k + 1) * NA_WIDTH]
            on, r = _rms(o_ref[...])
            gstat_ref[k:k + 1, :] += jnp.sum(dn * on, axis=0, keepdims=True)
            do_ref[...] = _rms_bwd(on, r, dn * g_ref[...]).astype(BF16)

    hs = jax.ShapeDtypeStruct((T, NA_WIDTH), BF16)
    return _hosted_call(
        body, "attn_out_bwd", (T // tm,),
        [_tok_spec(D, tm), _full(w_out.shape), _tok_spec(NA_WIDTH, tm), _tok_spec(SW_WIDTH, tm), _full((1, NA_WIDTH)), _full((1, SW_WIDTH))],
        [_tok_spec(NA_WIDTH, tm), _tok_spec(SW_WIDTH, tm), _full((8, NA_WIDTH))],
        [hs, hs, jax.ShapeDtypeStruct((8, NA_WIDTH), F32)],
        (dmix, w_out, oa, ob, g_na, g_sw), tasks)


def _attn_in_bwd(dqa, dka, dva, dqb, dkb, dvb, cos_t, sin_t, w_in, x2d, mod3, g_attn, dx1, B, S, tm=WIDE_TILE):
    T, D = x2d.shape
    tps = S // tm

    def body(dqa_ref, dka_ref, dva_ref, dqb_ref, dkb_ref, dvb_ref, cos_ref, sin_ref, w_ref, x_ref, mod_ref, g_ref, dx1_ref,
             gx_ref, dproj_ref, gstat_ref, bstat_ref):
        i = pl.program_id(0)
        drb = jnp.concatenate([dqb_ref[...] * Q_SCALE, dkb_ref[...]], axis=1).astype(F32)
        reps = (1, ROPE_WIDTH // (2 * HEAD_DIM))
        drb = drb * jnp.tile(cos_ref[...], reps) + _rot_half(drb * jnp.tile(sin_ref[...], reps))
        dproj = jnp.concatenate([(dqa_ref[...] * Q_SCALE).astype(BF16), dka_ref[...].astype(BF16), dva_ref[...].astype(BF16),
                                 drb.astype(BF16), dvb_ref[...].astype(BF16)], axis=1)
        dproj_ref[...] = dproj
        dh = _nn(dproj, w_ref[...])
        xn, r = _rms(x_ref[...])
        scale1 = 1.0 + mod_ref[0, 1:2, :]
        gx_ref[...] = dx1_ref[...] + _rms_bwd(xn, r, dh * g_ref[...] * scale1)

        @pl.when(i == 0)
        def _():
            gstat_ref[...] = jnp.zeros_like(gstat_ref)

        @pl.when(i % tps == 0)
        def _():
            bstat_ref[...] = jnp.zeros_like(bstat_ref)

        gstat_ref[0:1, :] += jnp.sum(dh * scale1 * xn, axis=0, keepdims=True)
        bstat_ref[0, 0:1, :] += jnp.sum(dh, axis=0, keepdims=True)
        bstat_ref[0, 1:2, :] += jnp.sum(dh * (xn * g_ref[...]), axis=0, keepdims=True)

    rope = _rope_spec(tps, tm)
    return pl.pallas_call(
        body, name="attn_in_bwd", grid=(T // tm,),
        in_specs=[_tok_spec(NA_WIDTH, tm), _tok_spec(NA_WIDTH, tm), _tok_spec(NA_WIDTH, tm), _tok_spec(SW_WIDTH, tm),
                  _tok_spec(SW_KV_WIDTH, tm), _tok_spec(SW_KV_WIDTH, tm), rope, rope, _full(w_in.shape), _tok_spec(D, tm),
                  _mod_spec(tps, D), _full((1, D)), _tok_spec(D, tm)],
        out_specs=[_tok_spec(D, tm), _tok_spec(IN_WIDTH, tm), _full((8, D)), _bstat_spec(tps, D)],
        out_shape=[jax.ShapeDtypeStruct((T, D), F32), jax.ShapeDtypeStruct((T, IN_WIDTH), BF16),
                   jax.ShapeDtypeStruct((8, D), F32), jax.ShapeDtypeStruct((B, 8, D), F32)],
        compiler_params=_params("arbitrary"),
    )(dqa, dka, dva, dqb, dkb, dvb, cos_t, sin_t, w_in, x2d, mod3, g_attn, dx1)


def _matmul_tn(a, b, name, tm=None, tk=512):
    T, M = a.shape
    N = b.shape[1]
    tm = M if tm is None else tm
    nk = T // tk

    def body(a_ref, b_ref, o_ref, acc):
        k = pl.program_id(1)

        @pl.when(k == 0)
        def _():
            acc[...] = jnp.zeros_like(acc)

        acc[...] += _tn(a_ref[...], b_ref[...])

        @pl.when(k == nk - 1)
        def _():
            o_ref[...] = acc[...].astype(BF16)

    return pl.pallas_call(
        body, name=name, grid=(M // tm, nk),
        in_specs=[pl.BlockSpec((tk, tm), lambda i, k: (k, i)), pl.BlockSpec((tk, N), lambda i, k: (k, 0))],
        out_specs=pl.BlockSpec((tm, N), lambda i, k: (i, 0)),
        out_shape=jax.ShapeDtypeStruct((M, N), BF16),
        scratch_shapes=[pltpu.VMEM((tm, N), F32)],
        compiler_params=_params("parallel", "arbitrary"),
    )(a, b)


def _na_geometry(S):
    rows = S // GRID_W
    wr = min(NA_ROWS_MAX, rows)
    return rows, wr


def _na_window(r, rows, wr):
    rs = jnp.clip(r - wr // 2, 0, rows - wr)
    return pl.multiple_of(rs * GRID_W, GRID_W), pl.multiple_of((rs - r + NA_ROWS_MAX - 1) * GRID_W, GRID_W)


NA_STEP_PAIRS = 2
NA_GW = NA_STEP_PAIRS * 128
NA_BWD_ROWS = 4
NA_ROWS_PER_STEP = 4


def _na_specs(S, kw_n, order):
    ng = NA_PAIRS // NA_STEP_PAIRS

    def col(k):
        return pl.BlockSpec((1, S, NA_GW), lambda *ids: (order(*ids)[0], 0, k * ng + order(*ids)[1]))
    bias = pl.BlockSpec((NA_STEP_PAIRS, N_DR * GRID_W, 128), lambda *ids: (order(*ids)[1], 0, 0))
    out = pl.BlockSpec((1, S, NA_GW), lambda *ids: (order(*ids)[0], 0, order(*ids)[1]))
    return col(0), col(1), col(2), bias, out


def _block_diag(t):
    left = lax.broadcasted_iota(jnp.int32, t.shape, 1) < HEAD_DIM
    zero = jnp.zeros_like(t)
    return jnp.concatenate([jnp.where(left, t, zero), jnp.where(left, zero, t)], axis=0)


def _diag_blocks(res):
    left = lax.broadcasted_iota(jnp.int32, (HEAD_DIM, 128), 1) < HEAD_DIM
    return jnp.where(left, res[:HEAD_DIM], res[HEAD_DIM:])


def _col_softmax(st):
    e = jnp.exp(st - jnp.max(st, axis=0, keepdims=True))
    return e * (1.0 / jnp.sum(e, axis=0, keepdims=True))


def _na_fwd(qkv, bias, tasks=()):
    B, S, _ = qkv.shape
    rows, wr = _na_geometry(S)
    kw_n = wr * GRID_W

    def body(q_ref, k_ref, v_ref, b_ref, o_ref):
        def step(it, carry):
            win = [_na_window(it * NA_ROWS_PER_STEP + u, rows, wr) for u in range(NA_ROWS_PER_STEP)]
            qrows = [pl.ds(pl.multiple_of((it * NA_ROWS_PER_STEP + u) * GRID_W, GRID_W), GRID_W) for u in range(NA_ROWS_PER_STEP)]
            krows = [pl.ds(w[0], kw_n) for w in win]
            brows = [pl.ds(w[1], kw_n) for w in win]
            lanes = [pl.ds(p * 128, 128) for p in range(NA_STEP_PAIRS)]
            chains = [(u, p) for u in range(NA_ROWS_PER_STEP) for p in range(NA_STEP_PAIRS)]
            st = {(u, p): _nt(k_ref[0, krows[u], lanes[p]], _block_diag(q_ref[0, qrows[u], lanes[p]])) for u, p in chains}
            pn = {(u, p): _col_softmax(st[(u, p)] + b_ref[p, brows[u], :]).astype(BF16) for u, p in chains}
            out = {(u, p): _diag_blocks(_tn(pn[(u, p)], v_ref[0, krows[u], lanes[p]])) for u, p in chains}
            for u in range(NA_ROWS_PER_STEP):
                o_ref[0, qrows[u], :] = jnp.concatenate([out[(u, p)] for p in range(NA_STEP_PAIRS)], axis=1)
            return carry

        lax.fori_loop(0, rows // NA_ROWS_PER_STEP, step, 0)

    q, k, v, bs, out = _na_specs(S, kw_n, lambda b, g: (b, g))
    return _hosted_call(body, "na_fwd", (B, NA_PAIRS // NA_STEP_PAIRS), [q, k, v, bs], [out],
                        [jax.ShapeDtypeStruct((B, S, NA_WIDTH), F32)], (qkv, qkv, qkv, bias), tasks)


def _na_bwd(qkv, bias, doa, tasks=()):
    B, S, _ = qkv.shape
    rows, wr = _na_geometry(S)
    kw_n = wr * GRID_W

    def body(q_ref, k_ref, v_ref, b_ref, do_ref, dq_ref, dk_ref, dv_ref, db_ref, dk_acc, dv_acc):
        @pl.when(pl.program_id(1) == 0)
        def _():
            db_ref[...] = jnp.zeros_like(db_ref)

        dk_acc[...] = jnp.zeros_like(dk_acc)
        dv_acc[...] = jnp.zeros_like(dv_acc)

        def step(it, carry):
            nu, pairs = range(NA_BWD_ROWS), range(NA_STEP_PAIRS)
            win = [_na_window(it * NA_BWD_ROWS + u, rows, wr) for u in nu]
            qrows = [pl.ds(pl.multiple_of((it * NA_BWD_ROWS + u) * GRID_W, GRID_W), GRID_W) for u in nu]
            krows = [pl.ds(w[0], kw_n) for w in win]
            brows = [pl.ds(w[1], kw_n) for w in win]
            lanes = [pl.ds(p * 128, 128) for p in pairs]
            chains = [(u, p) for u in nu for p in pairs]
            kp = {(u, p): k_ref[0, krows[u], lanes[p]] for u, p in chains}
            qbd = {(u, p): _block_diag(q_ref[0, qrows[u], lanes[p]]) for u, p in chains}
            dobd = {(u, p): _block_diag(do_ref[0, qrows[u], lanes[p]]) for u, p in chains}
            st = {c: _nt(kp[c], qbd[c]) for c in chains}
            dpt = {(u, p): _nt(v_ref[0, krows[u], lanes[p]], dobd[(u, p)]) for u, p in chains}
            pn = {(u, p): _col_softmax(st[(u, p)] + b_ref[p, brows[u], :]) for u, p in chains}
            dst = {c: pn[c] * (dpt[c] - jnp.sum(pn[c] * dpt[c], axis=0, keepdims=True)) for c in chains}
            dsb = {c: dst[c].astype(BF16) for c in chains}
            dq = {c: _diag_blocks(_tn(dsb[c], kp[c])) for c in chains}
            dk = {c: _nn(dsb[c], qbd[c]) for c in chains}
            dv = {c: _nn(pn[c].astype(BF16), dobd[c]) for c in chains}
            for u in nu:
                dq_ref[0, qrows[u], :] = jnp.concatenate([dq[(u, p)] for p in pairs], axis=1).astype(BF16)
                dk_acc[krows[u], :] += jnp.concatenate([dk[(u, p)] for p in pairs], axis=1)
                dv_acc[krows[u], :] += jnp.concatenate([dv[(u, p)] for p in pairs], axis=1)
                for p in pairs:
                    db_ref[p, brows[u], :] += dst[(u, p)]
            return carry

        lax.fori_loop(0, rows // NA_BWD_ROWS, step, 0)

        def emit(i, carry):
            r = pl.ds(pl.multiple_of(i * 256, 256), 256)
            dk_ref[0, r, :] = dk_acc[r, :].astype(BF16)
            dv_ref[0, r, :] = dv_acc[r, :].astype(BF16)
            return carry

        lax.fori_loop(0, S // 256, emit, 0)

    q, k, v, bs, out = _na_specs(S, kw_n, lambda g, b: (b, g))
    hs = jax.ShapeDtypeStruct((B, S, NA_WIDTH), BF16)
    return _hosted_call(body, "na_bwd", (NA_PAIRS // NA_STEP_PAIRS, B), [q, k, v, bs, out], [out, out, out, bs],
                        [hs, hs, hs, jax.ShapeDtypeStruct((NA_PAIRS, N_DR * GRID_W, 128), F32)], (qkv, qkv, qkv, bias, doa), tasks,
                        scratch_shapes=[pltpu.VMEM((S, NA_GW), F32), pltpu.VMEM((S, NA_GW), F32)])


SW_PAIRS = SW_HEADS // 2


def _sw_band(n, S):
    kw_n = 3 * SW_BLOCK
    start = pl.multiple_of(jnp.clip(n * SW_BLOCK - SW_BLOCK, 0, S - kw_n), SW_BLOCK)
    kpos = start + lax.broadcasted_iota(jnp.int32, (kw_n, SW_BLOCK), 0)
    qpos = n * SW_BLOCK + lax.broadcasted_iota(jnp.int32, (kw_n, SW_BLOCK), 1)
    return start, jnp.abs(qpos - kpos) <= SW_WINDOW


def _kv_halves(t):
    left = lax.broadcasted_iota(jnp.int32, t.shape, 1) < HEAD_DIM
    swapped = pltpu.roll(t, HEAD_DIM, axis=1)
    zero = jnp.zeros_like(t)
    return {(0, 0): jnp.where(left, t, zero), (0, 1): jnp.where(left, zero, swapped),
            (1, 0): jnp.where(left, swapped, zero), (1, 1): jnp.where(left, zero, t)}


def _sw_probs(st, ok, sk):
    st = jnp.where(ok, st, NEG)
    m = jnp.maximum(jnp.max(st, axis=0, keepdims=True), sk)
    e = jnp.exp(st - m)
    esk = jnp.exp(sk - m)
    inv = 1.0 / (jnp.sum(e, axis=0, keepdims=True) + esk)
    return e * inv, esk * inv


def _sw_specs(S):
    q = pl.BlockSpec((1, S, SW_WIDTH), lambda b, *_: (b, 0, ROPE_LO // SW_WIDTH))
    k = pl.BlockSpec((1, S, SW_KV_WIDTH), lambda b, *_: (b, 0, (ROPE_LO + SW_WIDTH) // SW_KV_WIDTH))
    v = pl.BlockSpec((1, S, SW_KV_WIDTH), lambda b, *_: (b, 0, (ROPE_LO + ROPE_WIDTH) // SW_KV_WIDTH))
    return q, k, v


SW_FWD_SPLIT = 2


def _sw_fwd(sink, qkv, tasks=()):
    B, S, _ = qkv.shape
    kw_n = 3 * SW_BLOCK

    def body(sink_ref, q_ref, k_ref, v_ref, o_ref):
        def step(n, carry):
            start, ok = _sw_band(n, S)
            qrows = pl.ds(pl.multiple_of(n * SW_BLOCK, SW_BLOCK), SW_BLOCK)
            krows = pl.ds(start, kw_n)
            kh, vh = _kv_halves(k_ref[0, krows, :]), _kv_halves(v_ref[0, krows, :])
            heads = [(p, e) for p in range(SW_PAIRS) for e in range(2)]
            qp = [q_ref[0, qrows, pl.ds(p * 128, 128)] for p in range(SW_PAIRS)]
            kv_of = lambda p: p // (SW_PAIRS // SW_KV_HEADS)
            st = {(p, e): _nt(kh[(kv_of(p), e)], qp[p]) for p, e in heads}
            pn = {(p, e): _sw_probs(st[(p, e)], ok, sink_ref[2 * p + e])[0].astype(BF16) for p, e in heads}
            outs = [_tn(pn[(p, 0)], vh[(kv_of(p), 0)]) + _tn(pn[(p, 1)], vh[(kv_of(p), 1)]) for p in range(SW_PAIRS)]
            o_ref[0, qrows, :] = jnp.concatenate(outs, axis=1)
            return carry

        half = (S // SW_BLOCK) // SW_FWD_SPLIT
        lax.fori_loop(pl.program_id(1) * half, (pl.program_id(1) + 1) * half, step, 0)

    q, k, v = _sw_specs(S)
    return _hosted_call(
        body, "sw_fwd", (B, SW_FWD_SPLIT), [pl.BlockSpec(memory_space=pltpu.SMEM), q, k, v],
        [pl.BlockSpec((1, S, SW_WIDTH), lambda b, s: (b, 0, 0))], [jax.ShapeDtypeStruct((B, S, SW_WIDTH), F32)],
        (sink, qkv, qkv, qkv), tasks)


def _sw_bwd(sink, qkv, dob):
    B, S, _ = qkv.shape
    kw_n = 3 * SW_BLOCK

    fold_rows = 256

    def body(sink_ref, q_ref, k_ref, v_ref, do_ref, dq_ref, dk_ref, dv_ref, dsink_ref, dk_acc, dv_acc):
        @pl.when(pl.program_id(0) == 0)
        def _():
            dsink_ref[...] = jnp.zeros_like(dsink_ref)

        dk_acc[...] = jnp.zeros_like(dk_acc)
        dv_acc[...] = jnp.zeros_like(dv_acc)
        ppk = SW_PAIRS // SW_KV_HEADS

        def step(n, carry):
            start, ok = _sw_band(n, S)
            qrows = pl.ds(pl.multiple_of(n * SW_BLOCK, SW_BLOCK), SW_BLOCK)
            krows = pl.ds(start, kw_n)
            kh, vh = _kv_halves(k_ref[0, krows, :]), _kv_halves(v_ref[0, krows, :])
            heads = [(p, e) for p in range(SW_PAIRS) for e in range(2)]
            qp = [q_ref[0, qrows, pl.ds(p * 128, 128)] for p in range(SW_PAIRS)]
            dop = [do_ref[0, qrows, pl.ds(p * 128, 128)] for p in range(SW_PAIRS)]
            st = {(p, e): _nt(kh[(p // ppk, e)], qp[p]) for p, e in heads}
            dpt = {(p, e): _nt(vh[(p // ppk, e)], dop[p]) for p, e in heads}
            pnb, dsb = {}, {}
            for p, e in heads:
                pn, psink = _sw_probs(st[(p, e)], ok, sink_ref[2 * p + e])
                delta = jnp.sum(pn * dpt[(p, e)], axis=0, keepdims=True)
                dsb[(p, e)] = (pn * (dpt[(p, e)] - delta)).astype(BF16)
                pnb[(p, e)] = pn.astype(BF16)
                dsink_ref[2 * p + e:2 * p + e + 1, :] += -(psink * delta)
            dq_ref[0, qrows, :] = jnp.concatenate(
                [_tn(dsb[(p, 0)], kh[(p // ppk, 0)]) + _tn(dsb[(p, 1)], kh[(p // ppk, 1)]) for p in range(SW_PAIRS)],
                axis=1).astype(BF16)
            left = lax.broadcasted_iota(jnp.int32, (kw_n, 128), 1) < HEAD_DIM
            dks, dvs = [], []
            for kv in range(SW_KV_HEADS):
                dk = dv = None
                for p in range(kv * ppk, (kv + 1) * ppk):
                    dk_p = jnp.where(left, _nn(dsb[(p, 0)], qp[p]), _nn(dsb[(p, 1)], qp[p]))
                    dv_p = jnp.where(left, _nn(pnb[(p, 0)], dop[p]), _nn(pnb[(p, 1)], dop[p]))
                    dk = dk_p if dk is None else dk + dk_p
                    dv = dv_p if dv is None else dv + dv_p
                dks.append(dk)
                dvs.append(dv)
            dk_acc[krows, :] += jnp.concatenate(dks, axis=1)
            dv_acc[krows, :] += jnp.concatenate(dvs, axis=1)
            return carry

        lax.fori_loop(0, S // SW_BLOCK, step, 0)

        def fold(i, carry):
            rows = pl.ds(pl.multiple_of(i * fold_rows, fold_rows), fold_rows)
            left = lax.broadcasted_iota(jnp.int32, (fold_rows, 128), 1) < HEAD_DIM
            for acc, out_ref in ((dk_acc, dk_ref), (dv_acc, dv_ref)):
                a, b = acc[rows, 0:128], acc[rows, 128:256]
                out_ref[0, rows, :] = jnp.where(left, a + pltpu.roll(a, HEAD_DIM, axis=1),
                                                b + pltpu.roll(b, HEAD_DIM, axis=1)).astype(BF16)
            return carry

        lax.fori_loop(0, S // fold_rows, fold, 0)

        @pl.when(pl.program_id(0) == B - 1)
        def _():
            dsink_ref[...] = jnp.broadcast_to(jnp.sum(dsink_ref[...], axis=1, keepdims=True), dsink_ref.shape)

    q, k, v = _sw_specs(S)
    qo = pl.BlockSpec((1, S, SW_WIDTH), lambda b: (b, 0, 0))
    ko = pl.BlockSpec((1, S, SW_KV_WIDTH), lambda b: (b, 0, 0))
    return pl.pallas_call(
        body, name="sw_bwd", grid=(B,),
        in_specs=[pl.BlockSpec(memory_space=pltpu.SMEM), q, k, v, qo],
        out_specs=[qo, ko, ko, _full((SW_HEADS, 128))],
        out_shape=[jax.ShapeDtypeStruct((B, S, SW_WIDTH), BF16), jax.ShapeDtypeStruct((B, S, SW_KV_WIDTH), BF16),
                   jax.ShapeDtypeStruct((B, S, SW_KV_WIDTH), BF16), jax.ShapeDtypeStruct((SW_HEADS, 128), F32)],
        scratch_shapes=[pltpu.VMEM((S, 2 * SW_KV_WIDTH), F32), pltpu.VMEM((S, 2 * SW_KV_WIDTH), F32)],
        compiler_params=_params("arbitrary"),
    )(sink, qkv, qkv, qkv, dob)


def _pack_sum_adamw(packs, params, pick):
    W = packs.shape[1]
    n_p = len(params)

    def body(p_ref, *refs):
        ins, tot_ref, pick_ref, outs = refs[:3 * n_p], refs[3 * n_p], refs[3 * n_p + 1], refs[3 * n_p + 2:]
        tot = p_ref[0:8, :]
        for d in range(1, N_DEV):
            tot = tot + p_ref[8 * d:8 * d + 8, :]
        tot_ref[...] = tot
        pick_ref[...] = tot[pick[0]:pick[0] + 1, pick[1]:pick[1] + 1]
        for i, (w, _, _, rows, off) in enumerate(params):
            w_ref, m_ref, v_ref = ins[3 * i:3 * i + 3]
            g_ref, d_ref, nm_ref, nv_ref = outs[4 * i:4 * i + 4]
            if w.ndim == 3:
                n_a, n_b, n = w.shape
                for a in range(n_a):
                    for b in range(n_b):
                        o = off + (b * n_a + a) * n
                        g_ref[a, b:b + 1, :] = tot_ref[rows[0]:rows[0] + 1, o:o + n]
                g = g_ref[...]
            else:
                n = w.shape[1]
                g = tot[rows[0]:rows[0] + 1, off:off + n]
                for r in rows[1:]:
                    g = g + tot[r:r + 1, off:off + n]
                g_ref[...] = g
            d_ref[...], nm_ref[...], nv_ref[...] = _adam_update(w_ref[...], g, m_ref[...], v_ref[...])

    res = pl.pallas_call(
        body, name="small_adamw",
        out_shape=[jax.ShapeDtypeStruct((8, W), F32), jax.ShapeDtypeStruct((1, 1), F32)]
        + [jax.ShapeDtypeStruct(p[0].shape, F32) for p in params for _ in range(4)],
        compiler_params=pltpu.CompilerParams(vmem_limit_bytes=VMEM_LIMIT),
    )(packs, *[a for p in params for a in p[:3]])
    return res[0], res[1], [res[2 + 4 * i:6 + 4 * i] for i in range(n_p)]


def _adam_update(w, g, m, v):
    c1 = 1.0 - ADAM_B1 ** ADAM_STEP
    c2 = 1.0 - ADAM_B2 ** ADAM_STEP
    nm = ADAM_B1 * m + (1.0 - ADAM_B1) * g
    nv = ADAM_B2 * v + (1.0 - ADAM_B2) * (g * g)
    return -ADAM_LR * ((nm / c1) / (jnp.sqrt(nv / c2) + ADAM_EPS) + ADAM_WD * w), nm, nv


def _adamw(w, g, m, v, name):
    def body(w_ref, g_ref, m_ref, v_ref, d_ref, nm_ref, nv_ref):
        d_ref[...], nm_ref[...], nv_ref[...] = _adam_update(w_ref[...], g_ref[...], m_ref[...], v_ref[...])

    s = jax.ShapeDtypeStruct(w.shape, F32)
    return pl.pallas_call(body, name=name, out_shape=[s, s, s],
                          compiler_params=pltpu.CompilerParams(vmem_limit_bytes=VMEM_LIMIT))(w, g, m, v)


def _sum_adamw_rows(R):
    return max(r for r in range(16, min(R, 256) + 1, 16) if R % r == 0)


def _sum_adamw_steps(R):
    return R // _sum_adamw_rows(R)


def _sum_first(own, recvb, name, tasks):
    R, C = own.shape
    rc = _sum_adamw_rows(R)

    def body(own_ref, r_ref, p_ref):
        p_ref[...] = (own_ref[...] + r_ref[0].astype(F32)) + r_ref[1].astype(F32)

    blk = pl.BlockSpec((rc, C), lambda i: (i, 0))
    (part,), got = _hosted_call(body, name, (R // rc,), [blk, pl.BlockSpec((2, rc, C), lambda i: (0, i, 0))], [blk],
                                [jax.ShapeDtypeStruct((R, C), F32)], (own, recvb), tasks)
    return part, got


def _sum_adamw(own, recvb, w, m, v, name, done=0):
    R, C = own.shape
    rc = _sum_adamw_rows(R)
    left = 3 - done
    assert 3 % left == 0

    def body(own_ref, r_ref, w_ref, m_ref, v_ref, g_ref, d_ref, nm_ref, nv_ref):
        g = own_ref[...]
        for j in range(left):
            g = g + r_ref[j].astype(F32)
        g_ref[...] = g
        d_ref[...], nm_ref[...], nv_ref[...] = _adam_update(w_ref[...], g, m_ref[...], v_ref[...])

    blk = pl.BlockSpec((rc, C), lambda i: (i, 0))
    s = jax.ShapeDtypeStruct((R, C), F32)
    return pl.pallas_call(
        body, name=name, grid=(R // rc,),
        in_specs=[blk, pl.BlockSpec((left, rc, C), lambda i: (done // left, i, 0)), blk, blk, blk],
        out_specs=[blk, blk, blk, blk], out_shape=[s, s, s, s], compiler_params=_params("parallel"),
    )(own, recvb, w, m, v)


def _by_device(dw):
    return dw.reshape(N_DEV, dw.shape[0] // N_DEV, dw.shape[1])


def _local_step(x, mod, g_attn, w_in, bias, sw_sink, g_na_out, g_sw_out, w_out, g_ffn, w_up, conv_w, conv_b, w_down,
                g_final, target, sharded):
    B, S, D = x.shape
    T = B * S
    x2d = x.reshape(T, D)
    mod3 = mod.reshape(B, 6, D)
    cos_t, sin_t = _rope_tables(S)
    sink = sw_sink.reshape(SW_HEADS)
    n_tiles = T // WIDE_TILE
    full = lambda g: g.reshape(N_DEV * g.shape[1], g.shape[2])

    rider = lambda w, mid, lo, n, into=None: [_gather_task(w, mid, rows=(lo, n), into=into)] if sharded else []
    if sharded:
        qu, hd = w_up.shape[0] // 4, w_down.shape[0] // 2
    (h, qkv), got = _attn_in(x2d, mod3, g_attn, w_in, cos_t, sin_t, S, rider(w_up, 3 * n_tiles // 4, 0, qu) if sharded else [])
    if sharded:
        w_up_buf = got[0][0]
    qkv3 = qkv.reshape(B, S, IN_WIDTH)
    na_steps, sw_steps = B * (NA_PAIRS // NA_STEP_PAIRS), B * SW_FWD_SPLIT
    (oa,), got = _na_fwd(qkv3, bias, rider(w_up, na_steps - 1, qu, 2 * qu, w_up_buf) if sharded else [])
    if sharded:
        w_up_buf = got[0][0]
    oa = oa.reshape(T, NA_WIDTH)
    (ob,), got = _sw_fwd(sink, qkv3,
                         rider(w_up, sw_steps // 2, 3 * qu, qu, w_up_buf) + [_gather_task(w_out, sw_steps - 1)] if sharded else [])
    if sharded:
        w_up, w_out = full(got[0][0]), full(got[1][0])
    ob = ob.reshape(T, SW_WIDTH)
    (mixin, mix, x1), _ = _attn_out(oa, ob, x2d, mod3, g_na_out, g_sw_out, w_out, S)
    (h2, val, gt), got = _ffn_up(x1, mod3, g_ffn, w_up, S, rider(w_down, 3 * n_tiles // 4, 0, 2 * hd) if sharded else [])
    if sharded:
        w_down = full(got[0][0])
    a, act, vd, dx2, df, gstat_f, bstat_f = _ffn_down(gt, val, conv_w, conv_b, w_down, x1, mod3, g_final, target.reshape(T, D), B, S)
    F = val.shape[1]

    dw_down = _matmul_tn(a, df, "dw_down")
    (dval, dgc, cstat), got = _ffn_down_bwd(df, w_down, act, vd, [_swap_task(_by_device(dw_down))] if sharded else [])
    if sharded:
        send_down, own_down = _chip_sums(_by_device(dw_down), got[0][0])
    (du, dx1, dmix, gstat_u, bstat_u, cstat_w), got = _ffn_up_bwd(dgc, dval, gt, conv_w, w_up, x1, mod3, g_ffn, dx2, mix, B, S,
                                                                  [_exchange_task(send_down)] if sharded else [])
    if sharded:
        dw_down = (own_down, got[0][0])
    dw_up = _matmul_tn(du, h2, "dw_up", tm=F)
    dw_out = _matmul_tn(mixin, dmix, "dw_out")
    (doa, dob, gstat_o), got = _attn_out_bwd(dmix, w_out, oa, ob, g_na_out, g_sw_out,
                                             [_swap_task(_by_device(dw_up)), _swap_task(_by_device(dw_out))] if sharded else [])
    if sharded:
        send_up, own_up = _chip_sums(_by_device(dw_up), got[0][0])
        send_out, own_out = _chip_sums(_by_device(dw_out), got[1][0])
    (dqa, dka, dva, dbt), got = _na_bwd(qkv3, bias, doa.reshape(B, S, NA_WIDTH),
                                        [_exchange_task(send_up), _exchange_task(send_out)] if sharded else [])
    if sharded:
        dw_up, dw_out = (own_up, got[0][0]), (own_out, got[1][0])
    dqb, dkb, dvb, dsink = _sw_bwd(sink, qkv3, dob.reshape(B, S, SW_WIDTH))
    r2 = lambda t: t.reshape(T, t.shape[-1])
    grad_x, dproj, gstat_i, bstat_i = _attn_in_bwd(r2(dqa), r2(dka), r2(dva), r2(dqb), r2(dkb), r2(dvb), cos_t, sin_t, w_in, x2d, mod3,
                                                   g_attn, dx1, B, S)
    dw_in = _matmul_tn(dproj, h, "dw_in")

    dmod = jnp.stack([bstat_i[:, 0], bstat_i[:, 1], bstat_u[:, 2], bstat_u[:, 0], bstat_u[:, 1], bstat_f[:, 0]], axis=1)
    small = dict(g_attn=gstat_i[0], g_ffn=gstat_u[0], g_final=gstat_f[0], loss=gstat_f[1, 0], g_na_out=gstat_o[0], g_sw_out=gstat_o[1],
                 sw_sink=dsink[:, 0], conv_b=cstat[0], conv_w=cstat_w[1:4], dbt=dbt,
                 raw=(bstat_i, bstat_u, bstat_f, gstat_i, gstat_u, gstat_f, gstat_o, dsink, cstat, cstat_w))
    return grad_x.reshape(B, S, D), dict(w_in=dw_in, w_out=dw_out, w_up=dw_up, w_down=dw_down), dmod, small


def _pack_slab(raw, drpb):
    D, F = raw[3].shape[1], raw[8].shape[1]
    n_seq = raw[0].shape[0]

    def body(bi_ref, bu_ref, bf_ref, gi_ref, gu_ref, gf_ref, go_ref, ds_ref, cs_ref, cw_ref, rp_ref, o_ref):
        o_ref[...] = jnp.zeros_like(o_ref)
        for b in range(n_seq):
            mods = (bi_ref[b, 0:1, :], bi_ref[b, 1:2, :], bu_ref[b, 2:3, :], bu_ref[b, 0:1, :], bu_ref[b, 1:2, :], bf_ref[b, 0:1, :])
            for k, row in enumerate(mods):
                o_ref[b:b + 1, k * D:(k + 1) * D] = row
        o = 0
        for row in (gi_ref[0:1, :], gu_ref[0:1, :], gf_ref[0:1, :], go_ref[0:1, :], go_ref[1:2, :]):
            o_ref[2:3, o:o + row.shape[1]] = row
            o += row.shape[1]
        ds = ds_ref[...]
        eye = lax.broadcasted_iota(jnp.int32, ds.shape, 0) == lax.broadcasted_iota(jnp.int32, ds.shape, 1)
        o_ref[2:3, o:o + 128] = jnp.sum(jnp.where(eye, ds, 0.0), axis=0, keepdims=True)
        o_ref[2:3, o + 128:o + 256] = gf_ref[1:2, 0:128]
        o_ref[3:4, 0:F] = cs_ref[0:1, :]
        o_ref[4:5, 0:rp_ref.shape[1]] = rp_ref[...]
        o_ref[5:8, 0:F] = cw_ref[1:4, :]

    return pl.pallas_call(body, name="pack_slab", out_shape=jax.ShapeDtypeStruct((8, PACK_W), F32),
                          compiler_params=pltpu.CompilerParams(vmem_limit_bytes=VMEM_LIMIT))(*raw, drpb)


def kernel(x, c, w_ada, b_ada, g_attn, w_in, na_rpb, sw_sink, g_na_out, g_sw_out, w_out, g_ffn, w_up, conv_w, conv_b, w_down, g_final, loss_target, m_w_ada, m_b_ada, m_g_attn, m_w_in, m_na_rpb, m_sw_sink, m_g_na_out, m_g_sw_out, m_w_out, m_g_ffn, m_w_up, m_conv_w, m_conv_b, m_w_down, m_g_final, v_w_ada, v_b_ada, v_g_attn, v_w_in, v_na_rpb, v_sw_sink, v_g_na_out, v_g_sw_out, v_w_out, v_g_ffn, v_w_up, v_conv_w, v_conv_b, v_w_down, v_g_final):
    B, S, D = x.shape
    me = 4 * lax.axis_index("x") + 2 * lax.axis_index("y") + lax.axis_index("c")
    ada_c = w_ada.shape[2]
    F_l = conv_w.shape[2]

    tr = {"w_in", "w_up"}
    w_in_t = jnp.transpose(w_in[0])
    shards_f = dict(w_out=w_out[0], w_up=jnp.transpose(w_up[0]), w_down=w_down[0])

    c_all, conv_w_f, mod_all, w_in_all, casts, bias = _ada_fwd(c, conv_w[0], w_ada[0], b_ada, w_in_t, list(shards_f.values()),
                                                               _na_bias_rev(na_rpb[0]))
    shards = dict(zip(shards_f, casts))
    mod_mine = lax.dynamic_slice(mod_all, (0, me * B, 0), (N_DEV, B, ada_c))
    mod = jnp.transpose(mod_mine, (1, 0, 2)).reshape(B, N_DEV * ada_c)
    w_in_f = w_in_all.reshape(N_DEV * w_in_t.shape[0], D)

    grad_x, dw, dmod, small = _local_step(x, mod, g_attn, w_in_f, bias, sw_sink, g_na_out, g_sw_out, shards["w_out"], g_ffn,
                                          shards["w_up"], conv_w_f, conv_b, shards["w_down"], g_final.reshape(1, D), loss_target,
                                          sharded=True)
    g8_in = _by_device(dw["w_in"])
    drpb, got = _na_bias_grad(small["dbt"], [_swap_task(g8_in)])
    send_in, own_in = _chip_sums(g8_in, got[0][0])

    slab = _pack_slab(small["raw"], drpb.reshape(1, -1))
    weights = dict(w_ada=w_ada, b_ada=b_ada, g_attn=g_attn, w_in=w_in, na_rpb=na_rpb, sw_sink=sw_sink, g_na_out=g_na_out,
                   g_sw_out=g_sw_out, w_out=w_out, g_ffn=g_ffn, w_up=w_up, conv_w=conv_w, conv_b=conv_b, w_down=w_down, g_final=g_final)
    ms = dict(w_ada=m_w_ada, b_ada=m_b_ada, g_attn=m_g_attn, w_in=m_w_in, na_rpb=m_na_rpb, sw_sink=m_sw_sink, g_na_out=m_g_na_out,
              g_sw_out=m_g_sw_out, w_out=m_w_out, g_ffn=m_g_ffn, w_up=m_w_up, conv_w=m_conv_w, conv_b=m_conv_b, w_down=m_w_down, g_final=m_g_final)
    vs = dict(w_ada=v_w_ada, b_ada=v_b_ada, g_attn=v_g_attn, w_in=v_w_in, na_rpb=v_na_rpb, sw_sink=v_sw_sink, g_na_out=v_g_na_out,
              g_sw_out=v_g_sw_out, w_out=v_w_out, g_ffn=v_g_ffn, w_up=v_w_up, conv_w=v_conv_w, conv_b=v_conv_b, w_down=v_w_down, g_final=v_g_final)
    names = list(weights)
    grads, deltas, new_m, new_v = {}, {}, {}, {}
    flat = lambda t: t.reshape(1, -1)

    def shard2d(nm):
        if nm in tr:
            return (lambda t: jnp.transpose(t[0])), (lambda t: jnp.transpose(t)[None])
        if nm == "conv_w":
            return (lambda t: jnp.transpose(t, (1, 0, 2))), (lambda t: jnp.transpose(t, (1, 0, 2)))
        return (lambda t: t[0]), (lambda t: t[None])

    def finish_sum(nm, own, recvb, done=0):
        r, back = shard2d(nm)
        g2, d_, m_, v_ = _sum_adamw(own, recvb, r(weights[nm]), r(ms[nm]), r(vs[nm]), "adamw_" + nm, done)
        grads[nm], deltas[nm], new_m[nm], new_v[nm] = back(g2), back(d_), back(m_), back(v_)

    own_up, recv_up = dw["w_up"]
    n_up = _sum_adamw_steps(own_up.shape[0])
    part_up, got = _sum_first(own_up, recv_up, "sum_first_w_up", [_gather_task(slab, n_up - 1), _exchange_task(send_in)])
    packs = got[0][0]
    finish_sum("w_up", part_up, recv_up, done=2)
    finish_sum("w_down", *dw["w_down"])
    finish_sum("w_in", own_in, got[1][0])
    finish_sum("w_out", *dw["w_out"])

    where = dict(b_ada=((0, 1), 0), g_attn=((2,), 0), g_ffn=((2,), D), g_final=((2,), 2 * D), g_na_out=((2,), 3 * D),
                 g_sw_out=((2,), 3 * D + NA_WIDTH), sw_sink=((2,), 3 * D + NA_WIDTH + SW_WIDTH), conv_b=((3,), 0), na_rpb=((4,), 0))

    def small_view(n):
        if n == "na_rpb":
            return (lambda t: jnp.transpose(t[0], (1, 0, 2))), (lambda t: jnp.transpose(t, (1, 0, 2))[None])
        return flat, (lambda t: t.reshape(weights[n].shape))

    tot, loss, small_out = _pack_sum_adamw(
        packs.reshape(N_DEV * 8, PACK_W),
        [tuple(small_view(n)[0](t[n]) for t in (weights, ms, vs)) + where[n] for n in where],
        (2, 3 * D + NA_WIDTH + SW_WIDTH + 128))
    for n, res in zip(where, small_out):
        grads[n], deltas[n], new_m[n], new_v[n] = [small_view(n)[1](t) for t in res]
    loss = loss.reshape(())

    for nm, g2 in (("w_ada", _ada_bwd(c_all, packs.reshape(N_DEV * 8, PACK_W), ada_c)), ("conv_w", lax.dynamic_slice(tot, (5, me * F_l), (3, F_l))[:, None])):
        r, back = shard2d(nm)
        d_, m_, v_ = _adamw(r(weights[nm]), g2, r(ms[nm]), r(vs[nm]), "adamw_" + nm)
        grads[nm], deltas[nm], new_m[nm], new_v[nm] = back(g2), back(d_), back(m_), back(v_)
    return (loss, grad_x, *[grads[n] for n in names], *[deltas[n] for n in names], *[new_m[n] for n in names],
            *[new_v[n] for n in names])
```

```python
import functools

import numpy as np
import jax
import jax.numpy as jnp
from jax import lax
from jax.experimental import pallas as pl
from jax.experimental.pallas import tpu as pltpu

F32, BF16 = jnp.float32, jnp.bfloat16
MESH_ID = pl.DeviceIdType.MESH
N_DEV = 8

HEAD_DIM = 64
NA_HEADS = 8
SW_HEADS = 8
SW_KV_HEADS = 2
SW_GROUP = SW_HEADS // SW_KV_HEADS
NA_WIDTH = NA_HEADS * HEAD_DIM
SW_WIDTH = SW_HEADS * HEAD_DIM
SW_KV_WIDTH = SW_KV_HEADS * HEAD_DIM
ROPE_WIDTH = SW_WIDTH + SW_KV_WIDTH
IN_WIDTH = 3 * NA_WIDTH + SW_WIDTH + 2 * SW_KV_WIDTH
ROPE_LO = 3 * NA_WIDTH
GRID_W = 64
NA_ROWS_MAX = 8
NA_COLS = 16
N_DR = 2 * NA_ROWS_MAX - 1
N_DC = 2 * NA_COLS - 1
SW_WINDOW = 128
SW_BLOCK = 128
ROPE_THETA = 10000.0
EPS = 1e-6
NEG = -1e30
Q_SCALE = HEAD_DIM ** -0.5

ADAM_LR = 0.001
ADAM_B1 = 0.9
ADAM_B2 = 0.999
ADAM_EPS = 1e-08
ADAM_WD = 0.01
ADAM_STEP = 10

TOKEN_TILE = 256
WIDE_TILE = 512
VMEM_LIMIT = 56 * 1024 * 1024

PACK_W = 6144


def _nn(a, b):
    return jnp.dot(a, b, preferred_element_type=F32)


def _nt(a, b):
    return lax.dot_general(a, b, (((1,), (1,)), ((), ())), preferred_element_type=F32)


def _tn(a, b):
    return lax.dot_general(a, b, (((0,), (0,)), ((), ())), preferred_element_type=F32)


def _rms(x):
    r = lax.rsqrt(jnp.mean(x * x, axis=-1, keepdims=True) + EPS)
    return x * r, r


def _rms_bwd(xn, r, gy):
    return r * (gy - xn * jnp.mean(xn * gy, axis=-1, keepdims=True))


def _params(*sem):
    return pltpu.CompilerParams(dimension_semantics=sem, vmem_limit_bytes=VMEM_LIMIT)


def _full(shape):
    n = len(shape)
    return pl.BlockSpec(shape, lambda *_: (0,) * n)


def _mesh_pos():
    return lax.axis_index("x"), lax.axis_index("y"), lax.axis_index("c")


def _row_chunk(r):
    for rc in (128, 64, 32, 16):
        if r % rc == 0:
            return rc
    raise ValueError(f"rows {r} not a multiple of 16")


class _Task:
    def __init__(self, inputs, out_shapes, sems, start, finish, mid=None, mid_step=None, alias=None):
        self.inputs, self.out_shapes, self.sems = list(inputs), list(out_shapes), list(sems)
        self.start, self.finish, self.mid, self.mid_step = start, finish, mid, mid_step
        self.alias = alias


def _hosted_call(body, name, grid, in_specs, out_specs, out_shape, operands, tasks, scratch_shapes=()):
    n_in, n_out, n_scr = len(in_specs), len(out_specs), len(scratch_shapes)
    t_in = [len(t.inputs) for t in tasks]
    t_out = [len(t.out_shapes) for t in tasks]
    t_sem = [len(t.sems) for t in tasks]
    n_steps = int(np.prod(grid))

    def wrapped(*refs):
        ins, rest = refs[:n_in], refs[n_in:]
        task_ins, rest = rest[:sum(t_in)], rest[sum(t_in):]
        outs, rest = rest[:n_out], rest[n_out:]
        task_outs, rest = rest[:sum(t_out)], rest[sum(t_out):]
        scr, task_sems = rest[:n_scr], rest[n_scr:]
        step = pl.program_id(0)
        for ax in range(1, len(grid)):
            step = step * grid[ax] + pl.program_id(ax)
        parts = []
        oi = oo = os_ = 0
        for t, a, b, c in zip(tasks, t_in, t_out, t_sem):
            parts.append((t, task_ins[oi:oi + a], task_outs[oo:oo + b], task_sems[os_:os_ + c]))
            oi, oo, os_ = oi + a, oo + b, os_ + c
        for t, ti, to, ts in parts:
            pl.when(step == 0)(functools.partial(t.start, ti, to, ts))
            if t.mid is not None:
                pl.when(step == t.mid_step)(functools.partial(t.mid, ti, to, ts))
        body(*ins, *outs, *scr)
        for t, ti, to, ts in parts:
            pl.when(step == n_steps - 1)(functools.partial(t.finish, ti, to, ts))

    hbm = pl.BlockSpec(memory_space=pl.ANY)
    aliases, oi, oo = {}, n_in, n_out
    for t, a, b in zip(tasks, t_in, t_out):
        if t.alias is not None:
            aliases[oi + t.alias[0]] = oo + t.alias[1]
        oi, oo = oi + a, oo + b
    res = pl.pallas_call(
        wrapped, name=name, grid=grid,
        in_specs=list(in_specs) + [hbm] * sum(t_in),
        out_specs=list(out_specs) + [hbm] * sum(t_out),
        out_shape=list(out_shape) + [s for t in tasks for s in t.out_shapes],
        scratch_shapes=list(scratch_shapes) + [s for t in tasks for s in t.sems],
        input_output_aliases=aliases,
        compiler_params=_params(*(["arbitrary"] * len(grid))),
    )(*operands, *[a for t in tasks for a in t.inputs])
    own, extra = res[:n_out], res[n_out:]
    per_task, o = [], 0
    for b in t_out:
        per_task.append(extra[o:o + b])
        o += b
    return own, per_task


def _gather_task(shard, mid_step, rows=None, into=None):
    lo, n = (0, shard.shape[0]) if rows is None else rows

    def parts(ins, outs, sems):
        x_ref, out_ref, (send_sems, recv_sems, local_sem) = ins[0], outs[0], sems
        x_, y_, c_ = _mesh_pos()
        me, sibling = (x_, y_, c_), (x_, y_, 1 - c_)
        chips = [(1 - x_, y_), (x_, 1 - y_), (1 - x_, 1 - y_)]
        x_ref = x_ref.at[pl.ds(lo, n)]

        def rows(px, py, pc):
            return out_ref.at[4 * px + 2 * py + pc, pl.ds(lo, n)]

        def copy(k, block, to, src=None):
            return pltpu.make_async_remote_copy(
                src_ref=rows(*block) if src is None else src, dst_ref=rows(*block),
                send_sem=send_sems.at[k], recv_sem=recv_sems.at[k], device_id=to, device_id_type=MESH_ID)

        return dict(
            mine=lambda: pltpu.make_async_copy(x_ref, rows(*me), local_sem),
            first=lambda: [copy(0, me, sibling, src=x_ref)] + [copy(1 + j, me, (*chip, c_), src=x_ref) for j, chip in enumerate(chips)],
            passed=lambda: [copy(4 + j, (*chip, c_), sibling) for j, chip in enumerate(chips)],
            landed=lambda: [copy(1 + j, (*chip, c_), me) for j, chip in enumerate(chips)],
            last=lambda: [copy(0, sibling, me)] + [copy(4 + j, (*chip, 1 - c_), me) for j, chip in enumerate(chips)])

    def start(ins, outs, sems):
        p = parts(ins, outs, sems)
        p["mine"]().start()
        for cp in p["first"]():
            cp.start()

    def mid(ins, outs, sems):
        p = parts(ins, outs, sems)
        for cp, fw in zip(p["landed"](), p["passed"]()):
            cp.wait_recv()
            fw.start()

    def finish(ins, outs, sems):
        p = parts(ins, outs, sems)
        for cp in p["last"]():
            cp.wait_recv()
        for cp in p["first"]() + p["passed"]():
            cp.wait_send()
        p["mine"]().wait()

    return _Task([shard] if into is None else [shard, into], [jax.ShapeDtypeStruct((N_DEV,) + shard.shape, shard.dtype)],
                 [pltpu.SemaphoreType.DMA((7,)), pltpu.SemaphoreType.DMA((7,)), pltpu.SemaphoreType.DMA],
                 start, finish, mid, mid_step, alias=None if into is None else (1, 0))


def _swap_task(g8):
    _, R, C = g8.shape

    def copies(ins, outs, sems):
        (g_ref,), (recv_ref,), (ss, rs) = ins, outs, sems
        x_, y_, c_ = _mesh_pos()
        return [pltpu.make_async_remote_copy(src_ref=g_ref.at[2 * k + (1 - c_)], dst_ref=recv_ref.at[k], send_sem=ss.at[k],
                                             recv_sem=rs.at[k], device_id=(x_, y_, 1 - c_), device_id_type=MESH_ID)
                for k in range(4)]

    def start(ins, outs, sems):
        for cp in copies(ins, outs, sems):
            cp.start()

    def finish(ins, outs, sems):
        cps = copies(ins, outs, sems)
        for cp in cps:
            cp.wait_recv()
        for cp in cps:
            cp.wait_send()

    return _Task([g8], [jax.ShapeDtypeStruct((4, R, C), g8.dtype)],
                 [pltpu.SemaphoreType.DMA((4,)), pltpu.SemaphoreType.DMA((4,))], start, finish)


def _chip_sums(g8, recva):
    _, R, C = g8.shape
    rt = _sum_adamw_rows(R)
    rc = _row_chunk(rt)

    def body(core_ref, g_ref, a_ref, send_ref, own_ref):
        x_, y_, _ = _mesh_pos()
        chips = [(1 - x_, y_), (x_, 1 - y_), (1 - x_, 1 - y_), (x_, y_)]

        def chunk(i, carry):
            rows = pl.ds(pl.multiple_of(i * rc, rc), rc)
            for j, (tx, ty) in enumerate(chips):
                k = 2 * tx + ty
                s = g_ref[k, rows, :].astype(F32) + a_ref[k, rows, :].astype(F32)
                if j < 3:
                    send_ref[j, rows, :] = s.astype(BF16)
                else:
                    own_ref[rows, :] = s
            return carry

        lax.fori_loop(0, rt // rc, chunk, 0)

    core = lax.axis_index("c").astype(jnp.int32).reshape(1)
    return pl.pallas_call(
        body, name="chip_sums",
        grid_spec=pltpu.PrefetchScalarGridSpec(
            num_scalar_prefetch=1, grid=(R // rt,),
            in_specs=[pl.BlockSpec((4, None, rt, C), lambda i, c: (0, c[0], i, 0)), pl.BlockSpec((4, rt, C), lambda i, c: (0, i, 0))],
            out_specs=[pl.BlockSpec((3, rt, C), lambda i, c: (0, i, 0)), pl.BlockSpec((rt, C), lambda i, c: (i, 0))]),
        out_shape=[jax.ShapeDtypeStruct((3, R, C), BF16), jax.ShapeDtypeStruct((R, C), F32)],
        compiler_params=_params("parallel"),
    )(core, g8.reshape(4, 2, R, C), recva)


def _exchange_task(sendb):
    def copies(ins, outs, sems):
        (s_ref,), (recv_ref,), (ss, rs) = ins, outs, sems
        x_, y_, c_ = _mesh_pos()
        flips = [(1 - x_, y_), (x_, 1 - y_), (1 - x_, 1 - y_)]
        return [pltpu.make_async_remote_copy(src_ref=s_ref.at[j], dst_ref=recv_ref.at[j], send_sem=ss.at[j], recv_sem=rs.at[j],
                                             device_id=(tx, ty, c_), device_id_type=MESH_ID) for j, (tx, ty) in enumerate(flips)]

    def start(ins, outs, sems):
        for cp in copies(ins, outs, sems):
            cp.start()

    def finish(ins, outs, sems):
        cps = copies(ins, outs, sems)
        for cp in cps:
            cp.wait_recv()
        for cp in cps:
            cp.wait_send()

    return _Task([sendb], [jax.ShapeDtypeStruct(sendb.shape, sendb.dtype)],
                 [pltpu.SemaphoreType.DMA((3,)), pltpu.SemaphoreType.DMA((3,))], start, finish)


def _silu(v):
    return v * (1.0 / (1.0 + jnp.exp(-v)))


def _ada_fwd(c, taps, w_ada_l, b_ada, w_in_shard, casts, rpb_rev):
    n_seq = c.shape[0]
    D, cols = w_ada_l.shape
    F_l = taps.shape[1]
    W = D + F_l
    n_rows = N_DEV * n_seq
    n_c = len(casts)
    t_c = _gather_task(jax.ShapeDtypeStruct((8, W), F32), 0)
    t_w = _gather_task(jax.ShapeDtypeStruct(w_in_shard.shape, BF16), 0)
    t_m = _gather_task(jax.ShapeDtypeStruct((n_rows, cols), F32), 0)

    def body(c_ref, taps_ref, w_ref, b_ref, ws_ref, rev_ref, *refs):
        cast_in, refs = refs[:n_c], refs[n_c:]
        (slabs_ref, call_ref, taps_all_ref, mod_ref, win_ref), refs = refs[:5], refs[5:]
        cast_out, bias_ref, refs = refs[:n_c], refs[n_c], refs[n_c + 1:]
        (slab_vm, c_vm, m_vm, ws_f, ws_b), refs = refs[:5], refs[5:]
        stage_f, stage_b, refs = refs[:n_c], refs[n_c:2 * n_c], refs[2 * n_c:]
        copy_sem, in_sems, out_sems, sems = refs[0], refs[1], refs[2], refs[3:]
        sc, sw, sm = sems[0:3], sems[3:6], sems[6:9]
        x_, y_, c_ = _mesh_pos()
        slab_vm[...] = jnp.zeros_like(slab_vm)
        slab_vm[0:n_seq, 0:D] = c_ref[...]
        slab_vm[0:taps.shape[0], D:W] = taps_ref[...]
        t_c.start((slab_vm,), (slabs_ref,), sc)
        cp = pltpu.make_async_copy(ws_ref, ws_f, copy_sem)
        cp.start()
        cp.wait()
        ws_b[...] = ws_f[...].astype(BF16)
        t_w.start((ws_b,), (win_ref,), sw)
        loads = [pltpu.make_async_copy(cast_in[j], stage_f[j], in_sems.at[j]) for j in range(n_c)]
        for ld in loads:
            ld.start()
        t_c.mid((slab_vm,), (slabs_ref,), sc)
        t_c.finish((slab_vm,), (slabs_ref,), sc)
        cp = pltpu.make_async_copy(slabs_ref, c_vm, copy_sem)
        cp.start()
        cp.wait()
        c_all = c_vm[:, :, 0:D].reshape(N_DEV * 8, D)
        call_ref[...] = c_all
        for j in range(N_DEV):
            taps_all_ref[:, j * F_l:(j + 1) * F_l] = c_vm[j, 0:taps.shape[0], D:W]
        b_mine = b_ref[:, pl.ds(pl.multiple_of((4 * x_ + 2 * y_ + c_) * cols, 128), cols)]
        m64 = jnp.dot(_silu(c_all), w_ref[...], precision=lax.Precision.HIGHEST, preferred_element_type=F32) + b_mine
        r = lax.broadcasted_iota(jnp.int32, (n_rows, N_DEV * 8), 0)
        c = lax.broadcasted_iota(jnp.int32, (n_rows, N_DEV * 8), 1)
        pick = jnp.where(c == 8 * (r // n_seq) + r % n_seq, 1.0, 0.0)
        m_vm[...] = jnp.dot(pick, m64, precision=lax.Precision.HIGHEST, preferred_element_type=F32)
        t_m.start((m_vm,), (mod_ref,), sm)
        stores = [pltpu.make_async_copy(stage_b[j], cast_out[j], out_sems.at[j]) for j in range(n_c)]
        for j in range(n_c):
            loads[j].wait()
            stage_b[j][...] = stage_f[j][...].astype(BF16)
            stores[j].start()
        for p in range(NA_PAIRS):
            _na_bias_rows(rev_ref, bias_ref, p)
        t_w.mid((ws_b,), (win_ref,), sw)
        t_m.mid((m_vm,), (mod_ref,), sm)
        t_m.finish((m_vm,), (mod_ref,), sm)
        t_w.finish((ws_b,), (win_ref,), sw)
        for st in stores:
            st.wait()

    hbm, vm = pl.BlockSpec(memory_space=pl.ANY), pl.BlockSpec(memory_space=pltpu.VMEM)
    res = pl.pallas_call(
        body, name="ada_fwd", in_specs=[vm, vm, vm, vm, hbm, vm] + [hbm] * n_c,
        out_specs=[hbm, vm, vm, hbm, hbm] + [hbm] * n_c + [vm],
        out_shape=t_c.out_shapes + [jax.ShapeDtypeStruct((N_DEV * 8, D), F32), jax.ShapeDtypeStruct((taps.shape[0], N_DEV * F_l), F32)]
        + t_m.out_shapes + t_w.out_shapes
        + [jax.ShapeDtypeStruct(a.shape, BF16) for a in casts] + [jax.ShapeDtypeStruct((NA_PAIRS, N_DR * GRID_W, 128), F32)],
        scratch_shapes=[pltpu.VMEM((8, W), F32), pltpu.VMEM((N_DEV, 8, W), F32), pltpu.VMEM((n_rows, cols), F32),
                        pltpu.VMEM(w_in_shard.shape, F32), pltpu.VMEM(w_in_shard.shape, BF16)]
        + [pltpu.VMEM(a.shape, F32) for a in casts] + [pltpu.VMEM(a.shape, BF16) for a in casts]
        + [pltpu.SemaphoreType.DMA, pltpu.SemaphoreType.DMA((n_c,)), pltpu.SemaphoreType.DMA((n_c,))]
        + t_c.sems + t_w.sems + t_m.sems,
        compiler_params=pltpu.CompilerParams(vmem_limit_bytes=VMEM_LIMIT),
    )(c, taps, w_ada_l, b_ada, w_in_shard, rpb_rev, *casts)
    return res[1], res[2], res[3], res[4], res[5:5 + n_c], res[5 + n_c]


def _ada_bwd(c_all, packs, cols):
    def body(c_ref, d_ref, o_ref):
        x_, y_, c_ = _mesh_pos()
        mine = d_ref[:, pl.ds(pl.multiple_of((4 * x_ + 2 * y_ + c_) * cols, 128), cols)]
        o_ref[...] = lax.dot_general(_silu(c_ref[...]), mine, (((0,), (0,)), ((), ())),
                                     precision=lax.Precision.HIGHEST, preferred_element_type=F32)
    return pl.pallas_call(body, name="ada_bwd", out_shape=jax.ShapeDtypeStruct((c_all.shape[1], cols), F32),
                          compiler_params=pltpu.CompilerParams(vmem_limit_bytes=VMEM_LIMIT))(c_all, packs)


NA_PAIRS = NA_HEADS // 2
N_DR_PAD = 16


def _na_bias_rev(na_rpb):
    rev = jnp.pad(jnp.flip(na_rpb, axis=2), ((0, 0), (0, N_DR_PAD - N_DR), (0, GRID_W - N_DC)))
    return jnp.transpose(rev.reshape(NA_PAIRS, 2, N_DR_PAD, GRID_W), (0, 2, 1, 3)).reshape(NA_PAIRS, N_DR_PAD, 128)


def _na_bias_rows(r_ref, o_ref, p):
    k = lax.broadcasted_iota(jnp.int32, (GRID_W, 128), 0)
    lane = lax.broadcasted_iota(jnp.int32, (GRID_W, 128), 1)
    q = lane % GRID_W
    cs = jnp.clip(q - NA_COLS // 2, 0, GRID_W - NA_COLS)
    ok = (k >= cs) & (k < cs + NA_COLS)
    left = lane < GRID_W
    for dr in range(N_DR):
        row = jnp.broadcast_to(r_ref[p, dr:dr + 1, :], (GRID_W, 128))
        r0 = jnp.where(left, row, 0.0)
        r1 = jnp.where(left, pltpu.roll(row, GRID_W, axis=1), 0.0)
        y0 = pltpu.roll(r0, 128 - (NA_COLS - 1), axis=1, stride=1, stride_axis=0)
        y1 = pltpu.roll(r1, GRID_W - (NA_COLS - 1), axis=1, stride=1, stride_axis=0)
        o_ref[p, dr * GRID_W:(dr + 1) * GRID_W, :] = jnp.where(ok, jnp.where(left, y0, y1), NEG)


def _na_bias_grad(db, tasks=()):
    a = np.arange(128)
    flip = jnp.asarray(((a[:, None] // GRID_W == a[None, :] // GRID_W)
                        & (a[:, None] % GRID_W + a[None, :] % GRID_W == GRID_W - 1)).astype(np.float32))

    def body(d_ref, j_ref, o_ref):
        o_ref[...] = jnp.zeros_like(o_ref)
        for dr in range(N_DR):
            t = jnp.dot(d_ref[0, dr * GRID_W:(dr + 1) * GRID_W, :], j_ref[...], precision=lax.Precision.HIGHEST, preferred_element_type=F32)
            t = pltpu.roll(t, GRID_W + NA_COLS, axis=1, stride=1, stride_axis=0)
            o_ref[0, dr:dr + 1, :] = jnp.sum(t, axis=0, keepdims=True)

    (rows,), got = _hosted_call(
        body, "rpb_reduce", (NA_PAIRS,),
        [pl.BlockSpec((1, N_DR * GRID_W, 128), lambda p: (p, 0, 0)), _full((128, 128))],
        [pl.BlockSpec((1, N_DR_PAD, 128), lambda p: (p, 0, 0))],
        [jax.ShapeDtypeStruct((NA_PAIRS, N_DR_PAD, 128), F32)], (db, flip), tasks)
    g = rows.reshape(NA_PAIRS, N_DR_PAD, 2, GRID_W)[:, :N_DR, :, :N_DC]
    return jnp.transpose(g, (0, 2, 1, 3)).reshape(-1), got


def _rope_tables(S):
    half = HEAD_DIM // 2
    inv = np.float32(ROPE_THETA) ** (-np.arange(half, dtype=np.float32) / np.float32(half))
    ang = np.arange(S).astype(np.float32)[:, None] * inv[None, :]
    cos, sin = np.cos(ang).astype(np.float32), np.sin(ang).astype(np.float32)
    return jnp.asarray(np.tile(np.concatenate([cos, cos], axis=1), (1, 2))), jnp.asarray(np.tile(np.concatenate([-sin, sin], axis=1), (1, 2)))


def _rope_spec(tps, tm=TOKEN_TILE):
    return pl.BlockSpec((tm, 2 * HEAD_DIM), lambda i: (i % tps, 0))


def _rot_half(t):
    w = t.shape[1]
    lane = lax.broadcasted_iota(jnp.int32, t.shape, 1)
    return jnp.where((lane % HEAD_DIM) < HEAD_DIM // 2, pltpu.roll(t, w - HEAD_DIM // 2, axis=1),
                     pltpu.roll(t, HEAD_DIM // 2, axis=1))


def _tok_spec(w, tm=TOKEN_TILE):
    return pl.BlockSpec((tm, w), lambda i: (i, 0))


def _mod_spec(tps, d):
    return pl.BlockSpec((1, 6, d), lambda i: (i // tps, 0, 0))


def _bstat_spec(tps, w):
    return pl.BlockSpec((1, 8, w), lambda i: (i // tps, 0, 0))


def _attn_in(x2d, mod3, g_attn, w_in, cos_t, sin_t, S, tasks=(), tm=WIDE_TILE):
    T, D = x2d.shape
    tps = S // tm

    def body(x_ref, mod_ref, g_ref, w_ref, cos_ref, sin_ref, h_ref, qkv_ref):
        xn, _ = _rms(x_ref[...])
        h = (xn * g_ref[...]) * (1.0 + mod_ref[0, 1:2, :]) + mod_ref[0, 0:1, :]
        hb = h.astype(BF16)
        h_ref[...] = hb
        proj = _nt(hb, w_ref[...])
        rb = proj[:, ROPE_LO:ROPE_LO + ROPE_WIDTH]
        reps = (1, ROPE_WIDTH // (2 * HEAD_DIM))
        rb = rb * jnp.tile(cos_ref[...], reps) + _rot_half(rb) * jnp.tile(sin_ref[...], reps)
        qkv_ref[:, 0:NA_WIDTH] = (proj[:, 0:NA_WIDTH] * Q_SCALE).astype(BF16)
        qkv_ref[:, NA_WIDTH:ROPE_LO] = proj[:, NA_WIDTH:ROPE_LO].astype(BF16)
        qkv_ref[:, ROPE_LO:ROPE_LO + SW_WIDTH] = (rb[:, 0:SW_WIDTH] * Q_SCALE).astype(BF16)
        qkv_ref[:, ROPE_LO + SW_WIDTH:ROPE_LO + ROPE_WIDTH] = rb[:, SW_WIDTH:].astype(BF16)
        qkv_ref[:, ROPE_LO + ROPE_WIDTH:] = proj[:, ROPE_LO + ROPE_WIDTH:].astype(BF16)

    return _hosted_call(
        body, "attn_in", (T // tm,),
        [_tok_spec(D, tm), _mod_spec(tps, D), _full((1, D)), _full(w_in.shape), _rope_spec(tps, tm), _rope_spec(tps, tm)],
        [_tok_spec(D, tm), _tok_spec(IN_WIDTH, tm)],
        [jax.ShapeDtypeStruct((T, D), BF16), jax.ShapeDtypeStruct((T, IN_WIDTH), BF16)],
        (x2d, mod3, g_attn, w_in, cos_t, sin_t), tasks)


def _attn_out(oa, ob, x2d, mod3, g_na, g_sw, w_out, S, tasks=(), tm=WIDE_TILE):
    T, D = x2d.shape
    tps = S // tm

    def body(oa_ref, ob_ref, x_ref, mod_ref, gna_ref, gsw_ref, w_ref, mixin_ref, mix_ref, x1_ref):
        oan, _ = _rms(oa_ref[...])
        obn, _ = _rms(ob_ref[...])
        mixin = jnp.concatenate([oan * gna_ref[...], obn * gsw_ref[...]], axis=1).astype(BF16)
        mixin_ref[...] = mixin
        mix = _nn(mixin, w_ref[...])
        mix_ref[...] = mix
        x1_ref[...] = x_ref[...] + mod_ref[0, 2:3, :] * mix

    return _hosted_call(
        body, "attn_out", (T // tm,),
        [_tok_spec(NA_WIDTH, tm), _tok_spec(SW_WIDTH, tm), _tok_spec(D, tm), _mod_spec(tps, D),
         _full((1, NA_WIDTH)), _full((1, SW_WIDTH)), _full(w_out.shape)],
        [_tok_spec(NA_WIDTH + SW_WIDTH, tm), _tok_spec(D, tm), _tok_spec(D, tm)],
        [jax.ShapeDtypeStruct((T, NA_WIDTH + SW_WIDTH), BF16), jax.ShapeDtypeStruct((T, D), F32), jax.ShapeDtypeStruct((T, D), F32)],
        (oa, ob, x2d, mod3, g_na, g_sw, w_out), tasks)


def _ffn_up(x1, mod3, g_ffn, w_up, S, tasks=(), tm=WIDE_TILE):
    T, D = x1.shape
    F = w_up.shape[0] // 2
    tps = S // tm

    def body(x1_ref, mod_ref, g_ref, w_ref, h2_ref, val_ref, gt_ref):
        xn, _ = _rms(x1_ref[...])
        h2 = ((xn * g_ref[...]) * (1.0 + mod_ref[0, 4:5, :]) + mod_ref[0, 3:4, :]).astype(BF16)
        h2_ref[...] = h2
        u = _nt(h2, w_ref[...])
        val_ref[...] = u[:, :F].astype(BF16)
        gt_ref[...] = u[:, F:].astype(BF16)

    return _hosted_call(
        body, "ffn_up", (T // tm,), [_tok_spec(D, tm), _mod_spec(tps, D), _full((1, D)), _full(w_up.shape)],
        [_tok_spec(D, tm), _tok_spec(F, tm), _tok_spec(F, tm)],
        [jax.ShapeDtypeStruct((T, D), BF16), jax.ShapeDtypeStruct((T, F), BF16), jax.ShapeDtypeStruct((T, F), BF16)],
        (x1, mod3, g_ffn, w_up), tasks)


def _halo_specs(T, tps, w):
    per = TOKEN_TILE // 8
    prev = pl.BlockSpec((8, w), lambda i: (jnp.maximum(i * per - 1, 0), 0))
    nxt = pl.BlockSpec((8, w), lambda i: (jnp.minimum((i + 1) * per, T // 8 - 1), 0))
    return prev, nxt


def _seq_shifts(cur, before, after, ti, tps):
    tm = cur.shape[0]
    row = lax.broadcasted_iota(jnp.int32, cur.shape, 0)
    before = jnp.where(ti > 0, before.astype(F32), 0.0)
    after = jnp.where(ti < tps - 1, after.astype(F32), 0.0)
    return jnp.where(row == 0, before, pltpu.roll(cur, 1, axis=0)), jnp.where(row == tm - 1, after, pltpu.roll(cur, tm - 1, axis=0))


def _ffn_down(gt, val, conv_w, conv_b, w_down, x1, mod3, g_final, target, B, S):
    T, D = x1.shape
    F = gt.shape[1]
    tps = S // TOKEN_TILE
    prev, nxt = _halo_specs(T, tps, F)

    def body(gt_ref, prev_ref, next_ref, val_ref, cw_ref, cb_ref, w_ref, x1_ref, mod_ref, gf_ref, tgt_ref,
             a_ref, act_ref, vd_ref, dx2_ref, df_ref, gstat_ref, bstat_ref):
        i = pl.program_id(0)
        g = gt_ref[...].astype(F32)
        gprev, gnext = _seq_shifts(g, prev_ref[7:8, :], next_ref[0:1, :], i % tps, tps)
        gc = gprev * cw_ref[0:1, :] + g * cw_ref[1:2, :] + gnext * cw_ref[2:3, :] + cb_ref[...]
        sig = 1.0 / (1.0 + jnp.exp(-gc))
        act = gc * sig
        val = val_ref[...].astype(F32)
        act_ref[...] = act.astype(BF16)
        vd_ref[...] = (val * (sig + act - act * sig)).astype(BF16)
        a = (act * val).astype(BF16)
        a_ref[...] = a
        f = _nn(a, w_ref[...])
        gate = mod_ref[0, 5:6, :]
        x2 = x1_ref[...] + gate * f
        xn, r = _rms(x2)
        err = xn * gf_ref[...] - tgt_ref[...]
        dy = err * (1.0 / D)
        dx2 = _rms_bwd(xn, r, dy * gf_ref[...])
        dx2_ref[...] = dx2
        df_ref[...] = (gate * dx2).astype(BF16)

        @pl.when(i == 0)
        def _():
            gstat_ref[...] = jnp.zeros_like(gstat_ref)

        @pl.when(i % tps == 0)
        def _():
            bstat_ref[...] = jnp.zeros_like(bstat_ref)

        gstat_ref[0:1, :] += jnp.sum(dy * xn, axis=0, keepdims=True)
        tile_loss = jnp.sum(jnp.sum(err * err, axis=1, keepdims=True), axis=0, keepdims=True) * (0.5 / D)
        gstat_ref[1:2, :] += jnp.broadcast_to(tile_loss, (1, D))
        bstat_ref[0, 0:1, :] += jnp.sum(dx2 * f, axis=0, keepdims=True)

    return pl.pallas_call(
        body, name="ffn_down", grid=(T // TOKEN_TILE,),
        in_specs=[_tok_spec(F), prev, nxt, _tok_spec(F), _full(conv_w.shape), _full((1, F)), _full(w_down.shape),
                  _tok_spec(D), _mod_spec(tps, D), _full((1, D)), _tok_spec(D)],
        out_specs=[_tok_spec(F), _tok_spec(F), _tok_spec(F), _tok_spec(D), _tok_spec(D), _full((8, D)), _bstat_spec(tps, D)],
        out_shape=[jax.ShapeDtypeStruct((T, F), BF16), jax.ShapeDtypeStruct((T, F), BF16), jax.ShapeDtypeStruct((T, F), BF16),
                   jax.ShapeDtypeStruct((T, D), F32), jax.ShapeDtypeStruct((T, D), BF16),
                   jax.ShapeDtypeStruct((8, D), F32), jax.ShapeDtypeStruct((B, 8, D), F32)],
        compiler_params=_params("arbitrary"),
    )(gt, gt, gt, val, conv_w, conv_b, w_down, x1, mod3, g_final, target)


def _ffn_down_bwd(df, w_down, act, vd, tasks=(), tm=WIDE_TILE):
    T, D = df.shape
    F = act.shape[1]

    def body(df_ref, w_ref, act_ref, vd_ref, dval_ref, dgc_ref, cstat_ref):
        da = _nt(df_ref[...], w_ref[...])
        dval_ref[...] = (da * act_ref[...].astype(F32)).astype(BF16)
        dgc = da * vd_ref[...].astype(F32)
        dgc_ref[...] = dgc.astype(BF16)

        @pl.when(pl.program_id(0) == 0)
        def _():
            cstat_ref[...] = jnp.zeros_like(cstat_ref)

        cstat_ref[0:1, :] += jnp.sum(dgc, axis=0, keepdims=True)

    return _hosted_call(
        body, "ffn_down_bwd", (T // tm,),
        [_tok_spec(D, tm), _full(w_down.shape), _tok_spec(F, tm), _tok_spec(F, tm)],
        [_tok_spec(F, tm), _tok_spec(F, tm), _full((8, F))],
        [jax.ShapeDtypeStruct((T, F), BF16), jax.ShapeDtypeStruct((T, F), BF16), jax.ShapeDtypeStruct((8, F), F32)],
        (df, w_down, act, vd), tasks)


def _ffn_up_bwd(dgc, dval, gt, conv_w, w_up, x1, mod3, g_ffn, dx2, mix, B, S, tasks=()):
    T, D = x1.shape
    F = dgc.shape[1]
    tps = S // TOKEN_TILE
    prev, nxt = _halo_specs(T, tps, F)

    def body(dgc_ref, prev_ref, next_ref, dval_ref, gt_ref, cw_ref, w_ref, x1_ref, mod_ref, g_ref, dx2_ref, mix_ref,
             du_ref, dx1_ref, dmix_ref, gstat_ref, bstat_ref, cstat_ref):
        i = pl.program_id(0)
        d = dgc_ref[...].astype(F32)
        dprev, dnext = _seq_shifts(d, prev_ref[7:8, :], next_ref[0:1, :], i % tps, tps)
        g = gt_ref[...].astype(F32)

        @pl.when(i == 0)
        def _():
            cstat_ref[...] = jnp.zeros_like(cstat_ref)

        cstat_ref[1:2, :] += jnp.sum(dnext * g, axis=0, keepdims=True)
        cstat_ref[2:3, :] += jnp.sum(d * g, axis=0, keepdims=True)
        cstat_ref[3:4, :] += jnp.sum(dprev * g, axis=0, keepdims=True)
        dgt = dnext * cw_ref[0:1, :] + d * cw_ref[1:2, :] + dprev * cw_ref[2:3, :]
        du = jnp.concatenate([dval_ref[...], dgt.astype(BF16)], axis=1)
        du_ref[...] = du
        dh2 = _nn(du, w_ref[...])
        xn, r = _rms(x1_ref[...])
        scale1 = 1.0 + mod_ref[0, 4:5, :]
        xg = xn * g_ref[...]
        dx1 = dx2_ref[...] + _rms_bwd(xn, r, dh2 * g_ref[...] * scale1)
        dx1_ref[...] = dx1
        dmix_ref[...] = (mod_ref[0, 2:3, :] * dx1).astype(BF16)

        @pl.when(i == 0)
        def _():
            gstat_ref[...] = jnp.zeros_like(gstat_ref)

        @pl.when(i % tps == 0)
        def _():
            bstat_ref[...] = jnp.zeros_like(bstat_ref)

        gstat_ref[0:1, :] += jnp.sum(dh2 * scale1 * xn, axis=0, keepdims=True)
        bstat_ref[0, 0:1, :] += jnp.sum(dh2, axis=0, keepdims=True)
        bstat_ref[0, 1:2, :] += jnp.sum(dh2 * xg, axis=0, keepdims=True)
        bstat_ref[0, 2:3, :] += jnp.sum(dx1 * mix_ref[...], axis=0, keepdims=True)

    return _hosted_call(
        body, "ffn_up_bwd", (T // TOKEN_TILE,),
        [_tok_spec(F), prev, nxt, _tok_spec(F), _tok_spec(F), _full(conv_w.shape), _full(w_up.shape), _tok_spec(D),
         _mod_spec(tps, D), _full((1, D)), _tok_spec(D), _tok_spec(D)],
        [_tok_spec(2 * F), _tok_spec(D), _tok_spec(D), _full((8, D)), _bstat_spec(tps, D), _full((8, F))],
        [jax.ShapeDtypeStruct((T, 2 * F), BF16), jax.ShapeDtypeStruct((T, D), F32), jax.ShapeDtypeStruct((T, D), BF16),
         jax.ShapeDtypeStruct((8, D), F32), jax.ShapeDtypeStruct((B, 8, D), F32), jax.ShapeDtypeStruct((8, F), F32)],
        (dgc, dgc, dgc, dval, gt, conv_w, w_up, x1, mod3, g_ffn, dx2, mix), tasks)


def _attn_out_bwd(dmix, w_out, oa, ob, g_na, g_sw, tasks=(), tm=WIDE_TILE):
    T, D = dmix.shape

    def body(dmix_ref, w_ref, oa_ref, ob_ref, gna_ref, gsw_ref, doa_ref, dob_ref, gstat_ref):
        dmixin = _nt(dmix_ref[...], w_ref[...])

        @pl.when(pl.program_id(0) == 0)
        def _():
            gstat_ref[...] = jnp.zeros_like(gstat_ref)

        for k, (o_ref, g_ref, do_ref) in enumerate(((oa_ref, gna_ref, doa_ref), (ob_ref, gsw_ref, dob_ref))):
            dn = dmixin[:, k * NA_WIDTH:(k + 1) * NA_WIDTH]
            on, r = _rms(o_ref[...])
            gstat_ref[k:k + 1, :] += jnp.sum(dn * on, axis=0, keepdims=True)
            do_ref[...] = _rms_bwd(on, r, dn * g_ref[...]).astype(BF16)

    hs = jax.ShapeDtypeStruct((T, NA_WIDTH), BF16)
    return _hosted_call(
        body, "attn_out_bwd", (T // tm,),
        [_tok_spec(D, tm), _full(w_out.shape), _tok_spec(NA_WIDTH, tm), _tok_spec(SW_WIDTH, tm), _full((1, NA_WIDTH)), _full((1, SW_WIDTH))],
        [_tok_spec(NA_WIDTH, tm), _tok_spec(SW_WIDTH, tm), _full((8, NA_WIDTH))],
        [hs, hs, jax.ShapeDtypeStruct((8, NA_WIDTH), F32)],
        (dmix, w_out, oa, ob, g_na, g_sw), tasks)


def _attn_in_bwd(dqa, dka, dva, dqb, dkb, dvb, cos_t, sin_t, w_in, x2d, mod3, g_attn, dx1, B, S, tm=WIDE_TILE):
    T, D = x2d.shape
    tps = S // tm

    def body(dqa_ref, dka_ref, dva_ref, dqb_ref, dkb_ref, dvb_ref, cos_ref, sin_ref, w_ref, x_ref, mod_ref, g_ref, dx1_ref,
             gx_ref, dproj_ref, gstat_ref, bstat_ref):
        i = pl.program_id(0)
        drb = jnp.concatenate([dqb_ref[...] * Q_SCALE, dkb_ref[...]], axis=1).astype(F32)
        reps = (1, ROPE_WIDTH // (2 * HEAD_DIM))
        drb = drb * jnp.tile(cos_ref[...], reps) + _rot_half(drb * jnp.tile(sin_ref[...], reps))
        dproj = jnp.concatenate([(dqa_ref[...] * Q_SCALE).astype(BF16), dka_ref[...].astype(BF16), dva_ref[...].astype(BF16),
                                 drb.astype(BF16), dvb_ref[...].astype(BF16)], axis=1)
        dproj_ref[...] = dproj
        dh = _nn(dproj, w_ref[...])
        xn, r = _rms(x_ref[...])
        scale1 = 1.0 + mod_ref[0, 1:2, :]
        gx_ref[...] = dx1_ref[...] + _rms_bwd(xn, r, dh * g_ref[...] * scale1)

        @pl.when(i == 0)
        def _():
            gstat_ref[...] = jnp.zeros_like(gstat_ref)

        @pl.when(i % tps == 0)
        def _():
            bstat_ref[...] = jnp.zeros_like(bstat_ref)

        gstat_ref[0:1, :] += jnp.sum(dh * scale1 * xn, axis=0, keepdims=True)
        bstat_ref[0, 0:1, :] += jnp.sum(dh, axis=0, keepdims=True)
        bstat_ref[0, 1:2, :] += jnp.sum(dh * (xn * g_ref[...]), axis=0, keepdims=True)

    rope = _rope_spec(tps, tm)
    return pl.pallas_call(
        body, name="attn_in_bwd", grid=(T // tm,),
        in_specs=[_tok_spec(NA_WIDTH, tm), _tok_spec(NA_WIDTH, tm), _tok_spec(NA_WIDTH, tm), _tok_spec(SW_WIDTH, tm),
                  _tok_spec(SW_KV_WIDTH, tm), _tok_spec(SW_KV_WIDTH, tm), rope, rope, _full(w_in.shape), _tok_spec(D, tm),
                  _mod_spec(tps, D), _full((1, D)), _tok_spec(D, tm)],
        out_specs=[_tok_spec(D, tm), _tok_spec(IN_WIDTH, tm), _full((8, D)), _bstat_spec(tps, D)],
        out_shape=[jax.ShapeDtypeStruct((T, D), F32), jax.ShapeDtypeStruct((T, IN_WIDTH), BF16),
                   jax.ShapeDtypeStruct((8, D), F32), jax.ShapeDtypeStruct((B, 8, D), F32)],
        compiler_params=_params("arbitrary"),
    )(dqa, dka, dva, dqb, dkb, dvb, cos_t, sin_t, w_in, x2d, mod3, g_attn, dx1)


def _matmul_tn(a, b, name, tm=None, tk=512):
    T, M = a.shape
    N = b.shape[1]
    tm = M if tm is None else tm
    nk = T // tk

    def body(a_ref, b_ref, o_ref, acc):
        k = pl.program_id(1)

        @pl.when(k == 0)
        def _():
            acc[...] = jnp.zeros_like(acc)

        acc[...] += _tn(a_ref[...], b_ref[...])

        @pl.when(k == nk - 1)
        def _():
            o_ref[...] = acc[...].astype(BF16)

    return pl.pallas_call(
        body, name=name, grid=(M // tm, nk),
        in_specs=[pl.BlockSpec((tk, tm), lambda i, k: (k, i)), pl.BlockSpec((tk, N), lambda i, k: (k, 0))],
        out_specs=pl.BlockSpec((tm, N), lambda i, k: (i, 0)),
        out_shape=jax.ShapeDtypeStruct((M, N), BF16),
        scratch_shapes=[pltpu.VMEM((tm, N), F32)],
        compiler_params=_params("parallel", "arbitrary"),
    )(a, b)


def _na_geometry(S):
    rows = S // GRID_W
    wr = min(NA_ROWS_MAX, rows)
    return rows, wr


def _na_window(r, rows, wr):
    rs = jnp.clip(r - wr // 2, 0, rows - wr)
    return pl.multiple_of(rs * GRID_W, GRID_W), pl.multiple_of((rs - r + NA_ROWS_MAX - 1) * GRID_W, GRID_W)


NA_STEP_PAIRS = 2
NA_GW = NA_STEP_PAIRS * 128
NA_BWD_ROWS = 4
NA_ROWS_PER_STEP = 4


def _na_specs(S, kw_n, order):
    ng = NA_PAIRS // NA_STEP_PAIRS

    def col(k):
        return pl.BlockSpec((1, S, NA_GW), lambda *ids: (order(*ids)[0], 0, k * ng + order(*ids)[1]))
    bias = pl.BlockSpec((NA_STEP_PAIRS, N_DR * GRID_W, 128), lambda *ids: (order(*ids)[1], 0, 0))
    out = pl.BlockSpec((1, S, NA_GW), lambda *ids: (order(*ids)[0], 0, order(*ids)[1]))
    return col(0), col(1), col(2), bias, out


def _block_diag(t):
    left = lax.broadcasted_iota(jnp.int32, t.shape, 1) < HEAD_DIM
    zero = jnp.zeros_like(t)
    return jnp.concatenate([jnp.where(left, t, zero), jnp.where(left, zero, t)], axis=0)


def _diag_blocks(res):
    left = lax.broadcasted_iota(jnp.int32, (HEAD_DIM, 128), 1) < HEAD_DIM
    return jnp.where(left, res[:HEAD_DIM], res[HEAD_DIM:])


def _col_softmax(st):
    e = jnp.exp(st - jnp.max(st, axis=0, keepdims=True))
    return e * (1.0 / jnp.sum(e, axis=0, keepdims=True))


def _na_fwd(qkv, bias, tasks=()):
    B, S, _ = qkv.shape
    rows, wr = _na_geometry(S)
    kw_n = wr * GRID_W

    def body(q_ref, k_ref, v_ref, b_ref, o_ref):
        def step(it, carry):
            win = [_na_window(it * NA_ROWS_PER_STEP + u, rows, wr) for u in range(NA_ROWS_PER_STEP)]
            qrows = [pl.ds(pl.multiple_of((it * NA_ROWS_PER_STEP + u) * GRID_W, GRID_W), GRID_W) for u in range(NA_ROWS_PER_STEP)]
            krows = [pl.ds(w[0], kw_n) for w in win]
            brows = [pl.ds(w[1], kw_n) for w in win]
            lanes = [pl.ds(p * 128, 128) for p in range(NA_STEP_PAIRS)]
            chains = [(u, p) for u in range(NA_ROWS_PER_STEP) for p in range(NA_STEP_PAIRS)]
            st = {(u, p): _nt(k_ref[0, krows[u], lanes[p]], _block_diag(q_ref[0, qrows[u], lanes[p]])) for u, p in chains}
            pn = {(u, p): _col_softmax(st[(u, p)] + b_ref[p, brows[u], :]).astype(BF16) for u, p in chains}
            out = {(u, p): _diag_blocks(_tn(pn[(u, p)], v_ref[0, krows[u], lanes[p]])) for u, p in chains}
            for u in range(NA_ROWS_PER_STEP):
                o_ref[0, qrows[u], :] = jnp.concatenate([out[(u, p)] for p in range(NA_STEP_PAIRS)], axis=1)
            return carry

        lax.fori_loop(0, rows // NA_ROWS_PER_STEP, step, 0)

    q, k, v, bs, out = _na_specs(S, kw_n, lambda b, g: (b, g))
    return _hosted_call(body, "na_fwd", (B, NA_PAIRS // NA_STEP_PAIRS), [q, k, v, bs], [out],
                        [jax.ShapeDtypeStruct((B, S, NA_WIDTH), F32)], (qkv, qkv, qkv, bias), tasks)


def _na_bwd(qkv, bias, doa, tasks=()):
    B, S, _ = qkv.shape
    rows, wr = _na_geometry(S)
    kw_n = wr * GRID_W

    def body(q_ref, k_ref, v_ref, b_ref, do_ref, dq_ref, dk_ref, dv_ref, db_ref, dk_acc, dv_acc):
        @pl.when(pl.program_id(1) == 0)
        def _():
            db_ref[...] = jnp.zeros_like(db_ref)

        dk_acc[...] = jnp.zeros_like(dk_acc)
        dv_acc[...] = jnp.zeros_like(dv_acc)

        def step(it, carry):
            nu, pairs = range(NA_BWD_ROWS), range(NA_STEP_PAIRS)
            win = [_na_window(it * NA_BWD_ROWS + u, rows, wr) for u in nu]
            qrows = [pl.ds(pl.multiple_of((it * NA_BWD_ROWS + u) * GRID_W, GRID_W), GRID_W) for u in nu]
            krows = [pl.ds(w[0], kw_n) for w in win]
            brows = [pl.ds(w[1], kw_n) for w in win]
            lanes = [pl.ds(p * 128, 128) for p in pairs]
            chains = [(u, p) for u in nu for p in pairs]
            kp = {(u, p): k_ref[0, krows[u], lanes[p]] for u, p in chains}
            qbd = {(u, p): _block_diag(q_ref[0, qrows[u], lanes[p]]) for u, p in chains}
            dobd = {(u, p): _block_diag(do_ref[0, qrows[u], lanes[p]]) for u, p in chains}
            st = {c: _nt(kp[c], qbd[c]) for c in chains}
            dpt = {(u, p): _nt(v_ref[0, krows[u], lanes[p]], dobd[(u, p)]) for u, p in chains}
            pn = {(u, p): _col_softmax(st[(u, p)] + b_ref[p, brows[u], :]) for u, p in chains}
            dst = {c: pn[c] * (dpt[c] - jnp.sum(pn[c] * dpt[c], axis=0, keepdims=True)) for c in chains}
            dsb = {c: dst[c].astype(BF16) for c in chains}
            dq = {c: _diag_blocks(_tn(dsb[c], kp[c])) for c in chains}
            dk = {c: _nn(dsb[c], qbd[c]) for c in chains}
            dv = {c: _nn(pn[c].astype(BF16), dobd[c]) for c in chains}
            for u in nu:
                dq_ref[0, qrows[u], :] = jnp.concatenate([dq[(u, p)] for p in pairs], axis=1).astype(BF16)
                dk_acc[krows[u], :] += jnp.concatenate([dk[(u, p)] for p in pairs], axis=1)
                dv_acc[krows[u], :] += jnp.concatenate([dv[(u, p)] for p in pairs], axis=1)
                for p in pairs:
                    db_ref[p, brows[u], :] += dst[(u, p)]
            return carry

        lax.fori_loop(0, rows // NA_BWD_ROWS, step, 0)

        def emit(i, carry):
            r = pl.ds(pl.multiple_of(i * 256, 256), 256)
            dk_ref[0, r, :] = dk_acc[r, :].astype(BF16)
            dv_ref[0, r, :] = dv_acc[r, :].astype(BF16)
            return carry

        lax.fori_loop(0, S // 256, emit, 0)

    q, k, v, bs, out = _na_specs(S, kw_n, lambda g, b: (b, g))
    hs = jax.ShapeDtypeStruct((B, S, NA_WIDTH), BF16)
    return _hosted_call(body, "na_bwd", (NA_PAIRS // NA_STEP_PAIRS, B), [q, k, v, bs, out], [out, out, out, bs],
                        [hs, hs, hs, jax.ShapeDtypeStruct((NA_PAIRS, N_DR * GRID_W, 128), F32)], (qkv, qkv, qkv, bias, doa), tasks,
                        scratch_shapes=[pltpu.VMEM((S, NA_GW), F32), pltpu.VMEM((S, NA_GW), F32)])


SW_PAIRS = SW_HEADS // 2


def _sw_band(n, S):
    kw_n = 3 * SW_BLOCK
    start = pl.multiple_of(jnp.clip(n * SW_BLOCK - SW_BLOCK, 0, S - kw_n), SW_BLOCK)
    kpos = start + lax.broadcasted_iota(jnp.int32, (kw_n, SW_BLOCK), 0)
    qpos = n * SW_BLOCK + lax.broadcasted_iota(jnp.int32, (kw_n, SW_BLOCK), 1)
    return start, jnp.abs(qpos - kpos) <= SW_WINDOW


def _kv_halves(t):
    left = lax.broadcasted_iota(jnp.int32, t.shape, 1) < HEAD_DIM
    swapped = pltpu.roll(t, HEAD_DIM, axis=1)
    zero = jnp.zeros_like(t)
    return {(0, 0): jnp.where(left, t, zero), (0, 1): jnp.where(left, zero, swapped),
            (1, 0): jnp.where(left, swapped, zero), (1, 1): jnp.where(left, zero, t)}


def _sw_probs(st, ok, sk):
    st = jnp.where(ok, st, NEG)
    m = jnp.maximum(jnp.max(st, axis=0, keepdims=True), sk)
    e = jnp.exp(st - m)
    esk = jnp.exp(sk - m)
    inv = 1.0 / (jnp.sum(e, axis=0, keepdims=True) + esk)
    return e * inv, esk * inv


def _sw_specs(S):
    q = pl.BlockSpec((1, S, SW_WIDTH), lambda b, *_: (b, 0, ROPE_LO // SW_WIDTH))
    k = pl.BlockSpec((1, S, SW_KV_WIDTH), lambda b, *_: (b, 0, (ROPE_LO + SW_WIDTH) // SW_KV_WIDTH))
    v = pl.BlockSpec((1, S, SW_KV_WIDTH), lambda b, *_: (b, 0, (ROPE_LO + ROPE_WIDTH) // SW_KV_WIDTH))
    return q, k, v


SW_FWD_SPLIT = 2


def _sw_fwd(sink, qkv, tasks=()):
    B, S, _ = qkv.shape
    kw_n = 3 * SW_BLOCK

    def body(sink_ref, q_ref, k_ref, v_ref, o_ref):
        def step(n, carry):
            start, ok = _sw_band(n, S)
            qrows = pl.ds(pl.multiple_of(n * SW_BLOCK, SW_BLOCK), SW_BLOCK)
            krows = pl.ds(start, kw_n)
            kh, vh = _kv_halves(k_ref[0, krows, :]), _kv_halves(v_ref[0, krows, :])
            heads = [(p, e) for p in range(SW_PAIRS) for e in range(2)]
            qp = [q_ref[0, qrows, pl.ds(p * 128, 128)] for p in range(SW_PAIRS)]
            kv_of = lambda p: p // (SW_PAIRS // SW_KV_HEADS)
            st = {(p, e): _nt(kh[(kv_of(p), e)], qp[p]) for p, e in heads}
            pn = {(p, e): _sw_probs(st[(p, e)], ok, sink_ref[2 * p + e])[0].astype(BF16) for p, e in heads}
            outs = [_tn(pn[(p, 0)], vh[(kv_of(p), 0)]) + _tn(pn[(p, 1)], vh[(kv_of(p), 1)]) for p in range(SW_PAIRS)]
            o_ref[0, qrows, :] = jnp.concatenate(outs, axis=1)
            return carry

        half = (S // SW_BLOCK) // SW_FWD_SPLIT
        lax.fori_loop(pl.program_id(1) * half, (pl.program_id(1) + 1) * half, step, 0)

    q, k, v = _sw_specs(S)
    return _hosted_call(
        body, "sw_fwd", (B, SW_FWD_SPLIT), [pl.BlockSpec(memory_space=pltpu.SMEM), q, k, v],
        [pl.BlockSpec((1, S, SW_WIDTH), lambda b, s: (b, 0, 0))], [jax.ShapeDtypeStruct((B, S, SW_WIDTH), F32)],
        (sink, qkv, qkv, qkv), tasks)


def _sw_bwd(sink, qkv, dob):
    B, S, _ = qkv.shape
    kw_n = 3 * SW_BLOCK

    fold_rows = 256

    def body(sink_ref, q_ref, k_ref, v_ref, do_ref, dq_ref, dk_ref, dv_ref, dsink_ref, dk_acc, dv_acc):
        @pl.when(pl.program_id(0) == 0)
        def _():
            dsink_ref[...] = jnp.zeros_like(dsink_ref)

        dk_acc[...] = jnp.zeros_like(dk_acc)
        dv_acc[...] = jnp.zeros_like(dv_acc)
        ppk = SW_PAIRS // SW_KV_HEADS

        def step(n, carry):
            start, ok = _sw_band(n, S)
            qrows = pl.ds(pl.multiple_of(n * SW_BLOCK, SW_BLOCK), SW_BLOCK)
            krows = pl.ds(start, kw_n)
            kh, vh = _kv_halves(k_ref[0, krows, :]), _kv_halves(v_ref[0, krows, :])
            heads = [(p, e) for p in range(SW_PAIRS) for e in range(2)]
            qp = [q_ref[0, qrows, pl.ds(p * 128, 128)] for p in range(SW_PAIRS)]
            dop = [do_ref[0, qrows, pl.ds(p * 128, 128)] for p in range(SW_PAIRS)]
            st = {(p, e): _nt(kh[(p // ppk, e)], qp[p]) for p, e in heads}
            dpt = {(p, e): _nt(vh[(p // ppk, e)], dop[p]) for p, e in heads}
            pnb, dsb = {}, {}
            for p, e in heads:
                pn, psink = _sw_probs(st[(p, e)], ok, sink_ref[2 * p + e])
                delta = jnp.sum(pn * dpt[(p, e)], axis=0, keepdims=True)
                dsb[(p, e)] = (pn * (dpt[(p, e)] - delta)).astype(BF16)
                pnb[(p, e)] = pn.astype(BF16)
                dsink_ref[2 * p + e:2 * p + e + 1, :] += -(psink * delta)
            dq_ref[0, qrows, :] = jnp.concatenate(
                [_tn(dsb[(p, 0)], kh[(p // ppk, 0)]) + _tn(dsb[(p, 1)], kh[(p // ppk, 1)]) for p in range(SW_PAIRS)],
                axis=1).astype(BF16)
            left = lax.broadcasted_iota(jnp.int32, (kw_n, 128), 1) < HEAD_DIM
            dks, dvs = [], []
            for kv in range(SW_KV_HEADS):
                dk = dv = None
                for p in range(kv * ppk, (kv + 1) * ppk):
                    dk_p = jnp.where(left, _nn(dsb[(p, 0)], qp[p]), _nn(dsb[(p, 1)], qp[p]))
                    dv_p = jnp.where(left, _nn(pnb[(p, 0)], dop[p]), _nn(pnb[(p, 1)], dop[p]))
                    dk = dk_p if dk is None else dk + dk_p
                    dv = dv_p if dv is None else dv + dv_p
                dks.append(dk)
                dvs.append(dv)
            dk_acc[krows, :] += jnp.concatenate(dks, axis=1)
            dv_acc[krows, :] += jnp.concatenate(dvs, axis=1)
            return carry

        lax.fori_loop(0, S // SW_BLOCK, step, 0)

        def fold(i, carry):
            rows = pl.ds(pl.multiple_of(i * fold_rows, fold_rows), fold_rows)
            left = lax.broadcasted_iota(jnp.int32, (fold_rows, 128), 1) < HEAD_DIM
            for acc, out_ref in ((dk_acc, dk_ref), (dv_acc, dv_ref)):
                a, b = acc[rows, 0:128], acc[rows, 128:256]
                out_ref[0, rows, :] = jnp.where(left, a + pltpu.roll(a, HEAD_DIM, axis=1),
                                                b + pltpu.roll(b, HEAD_DIM, axis=1)).astype(BF16)
            return carry

        lax.fori_loop(0, S // fold_rows, fold, 0)

        @pl.when(pl.program_id(0) == B - 1)
        def _():
            dsink_ref[...] = jnp.broadcast_to(jnp.sum(dsink_ref[...], axis=1, keepdims=True), dsink_ref.shape)

    q, k, v = _sw_specs(S)
    qo = pl.BlockSpec((1, S, SW_WIDTH), lambda b: (b, 0, 0))
    ko = pl.BlockSpec((1, S, SW_KV_WIDTH), lambda b: (b, 0, 0))
    return pl.pallas_call(
        body, name="sw_bwd", grid=(B,),
        in_specs=[pl.BlockSpec(memory_space=pltpu.SMEM), q, k, v, qo],
        out_specs=[qo, ko, ko, _full((SW_HEADS, 128))],
        out_shape=[jax.ShapeDtypeStruct((B, S, SW_WIDTH), BF16), jax.ShapeDtypeStruct((B, S, SW_KV_WIDTH), BF16),
                   jax.ShapeDtypeStruct((B, S, SW_KV_WIDTH), BF16), jax.ShapeDtypeStruct((SW_HEADS, 128), F32)],
        scratch_shapes=[pltpu.VMEM((S, 2 * SW_KV_WIDTH), F32), pltpu.VMEM((S, 2 * SW_KV_WIDTH), F32)],
        compiler_params=_params("arbitrary"),
    )(sink, qkv, qkv, qkv, dob)


def _pack_sum_adamw(packs, params, pick):
    W = packs.shape[1]
    n_p = len(params)

    def body(p_ref, *refs):
        ins, tot_ref, pick_ref, outs = refs[:3 * n_p], refs[3 * n_p], refs[3 * n_p + 1], refs[3 * n_p + 2:]
        tot = p_ref[0:8, :]
        for d in range(1, N_DEV):
            tot = tot + p_ref[8 * d:8 * d + 8, :]
        tot_ref[...] = tot
        pick_ref[...] = tot[pick[0]:pick[0] + 1, pick[1]:pick[1] + 1]
        for i, (w, _, _, rows, off) in enumerate(params):
            w_ref, m_ref, v_ref = ins[3 * i:3 * i + 3]
            g_ref, d_ref, nm_ref, nv_ref = outs[4 * i:4 * i + 4]
            if w.ndim == 3:
                n_a, n_b, n = w.shape
                for a in range(n_a):
                    for b in range(n_b):
                        o = off + (b * n_a + a) * n
                        g_ref[a, b:b + 1, :] = tot_ref[rows[0]:rows[0] + 1, o:o + n]
                g = g_ref[...]
            else:
                n = w.shape[1]
                g = tot[rows[0]:rows[0] + 1, off:off + n]
                for r in rows[1:]:
                    g = g + tot[r:r + 1, off:off + n]
                g_ref[...] = g
            d_ref[...], nm_ref[...], nv_ref[...] = _adam_update(w_ref[...], g, m_ref[...], v_ref[...])

    res = pl.pallas_call(
        body, name="small_adamw",
        out_shape=[jax.ShapeDtypeStruct((8, W), F32), jax.ShapeDtypeStruct((1, 1), F32)]
        + [jax.ShapeDtypeStruct(p[0].shape, F32) for p in params for _ in range(4)],
        compiler_params=pltpu.CompilerParams(vmem_limit_bytes=VMEM_LIMIT),
    )(packs, *[a for p in params for a in p[:3]])
    return res[0], res[1], [res[2 + 4 * i:6 + 4 * i] for i in range(n_p)]


def _adam_update(w, g, m, v):
    c1 = 1.0 - ADAM_B1 ** ADAM_STEP
    c2 = 1.0 - ADAM_B2 ** ADAM_STEP
    nm = ADAM_B1 * m + (1.0 - ADAM_B1) * g
    nv = ADAM_B2 * v + (1.0 - ADAM_B2) * (g * g)
    return -ADAM_LR * ((nm / c1) / (jnp.sqrt(nv / c2) + ADAM_EPS) + ADAM_WD * w), nm, nv


def _adamw(w, g, m, v, name):
    def body(w_ref, g_ref, m_ref, v_ref, d_ref, nm_ref, nv_ref):
        d_ref[...], nm_ref[...], nv_ref[...] = _adam_update(w_ref[...], g_ref[...], m_ref[...], v_ref[...])

    s = jax.ShapeDtypeStruct(w.shape, F32)
    if w.ndim == 2 and w.shape[0] % 16 == 0:
        blk = pl.BlockSpec((_sum_adamw_rows(w.shape[0]), w.shape[1]), lambda i: (i, 0))
        return pl.pallas_call(body, name=name, grid=(_sum_adamw_steps(w.shape[0]),), in_specs=[blk] * 4, out_specs=[blk] * 3,
                              out_shape=[s, s, s], compiler_params=_params("parallel"))(w, g, m, v)
    return pl.pallas_call(body, name=name, out_shape=[s, s, s],
                          compiler_params=pltpu.CompilerParams(vmem_limit_bytes=VMEM_LIMIT))(w, g, m, v)


def _sum_adamw_rows(R):
    return max(r for r in range(16, min(R, 256) + 1, 16) if R % r == 0)


def _sum_adamw_steps(R):
    return R // _sum_adamw_rows(R)


def _sum_first(own, recvb, name, tasks):
    R, C = own.shape
    rc = _sum_adamw_rows(R)

    def body(own_ref, r_ref, p_ref):
        p_ref[...] = (own_ref[...] + r_ref[0].astype(F32)) + r_ref[1].astype(F32)

    blk = pl.BlockSpec((rc, C), lambda i: (i, 0))
    (part,), got = _hosted_call(body, name, (R // rc,), [blk, pl.BlockSpec((2, rc, C), lambda i: (0, i, 0))], [blk],
                                [jax.ShapeDtypeStruct((R, C), F32)], (own, recvb), tasks)
    return part, got


def _sum_adamw(own, recvb, w, m, v, name, done=0):
    R, C = own.shape
    rc = _sum_adamw_rows(R)
    left = 3 - done
    assert 3 % left == 0

    def body(own_ref, r_ref, w_ref, m_ref, v_ref, g_ref, d_ref, nm_ref, nv_ref):
        g = own_ref[...]
        for j in range(left):
            g = g + r_ref[j].astype(F32)
        g_ref[...] = g
        d_ref[...], nm_ref[...], nv_ref[...] = _adam_update(w_ref[...], g, m_ref[...], v_ref[...])

    blk = pl.BlockSpec((rc, C), lambda i: (i, 0))
    s = jax.ShapeDtypeStruct((R, C), F32)
    return pl.pallas_call(
        body, name=name, grid=(R // rc,),
        in_specs=[blk, pl.BlockSpec((left, rc, C), lambda i: (done // left, i, 0)), blk, blk, blk],
        out_specs=[blk, blk, blk, blk], out_shape=[s, s, s, s], compiler_params=_params("parallel"),
    )(own, recvb, w, m, v)


def _by_device(dw):
    return dw.reshape(N_DEV, dw.shape[0] // N_DEV, dw.shape[1])


def _local_step(x, mod, g_attn, w_in, bias, sw_sink, g_na_out, g_sw_out, w_out, g_ffn, w_up, conv_w, conv_b, w_down,
                g_final, target, sharded):
    B, S, D = x.shape
    T = B * S
    x2d = x.reshape(T, D)
    mod3 = mod.reshape(B, 6, D)
    cos_t, sin_t = _rope_tables(S)
    sink = sw_sink.reshape(SW_HEADS)
    n_tiles = T // WIDE_TILE
    full = lambda g: g.reshape(N_DEV * g.shape[1], g.shape[2])

    rider = lambda w, mid, lo, n, into=None: [_gather_task(w, mid, rows=(lo, n), into=into)] if sharded else []
    if sharded:
        qu, hd = w_up.shape[0] // 4, w_down.shape[0] // 2
    (h, qkv), got = _attn_in(x2d, mod3, g_attn, w_in, cos_t, sin_t, S, rider(w_up, 3 * n_tiles // 4, 0, qu) if sharded else [])
    if sharded:
        w_up_buf = got[0][0]
    qkv3 = qkv.reshape(B, S, IN_WIDTH)
    na_steps, sw_steps = B * (NA_PAIRS // NA_STEP_PAIRS), B * SW_FWD_SPLIT
    (oa,), got = _na_fwd(qkv3, bias, rider(w_up, na_steps - 1, qu, 2 * qu, w_up_buf) if sharded else [])
    if sharded:
        w_up_buf = got[0][0]
    oa = oa.reshape(T, NA_WIDTH)
    (ob,), got = _sw_fwd(sink, qkv3,
                         rider(w_up, sw_steps // 2, 3 * qu, qu, w_up_buf) + [_gather_task(w_out, sw_steps - 1)] if sharded else [])
    if sharded:
        w_up, w_out = full(got[0][0]), full(got[1][0])
    ob = ob.reshape(T, SW_WIDTH)
    (mixin, mix, x1), _ = _attn_out(oa, ob, x2d, mod3, g_na_out, g_sw_out, w_out, S)
    (h2, val, gt), got = _ffn_up(x1, mod3, g_ffn, w_up, S, rider(w_down, 3 * n_tiles // 4, 0, 2 * hd) if sharded else [])
    if sharded:
        w_down = full(got[0][0])
    a, act, vd, dx2, df, gstat_f, bstat_f = _ffn_down(gt, val, conv_w, conv_b, w_down, x1, mod3, g_final, target.reshape(T, D), B, S)
    F = val.shape[1]

    dw_down = _matmul_tn(a, df, "dw_down")
    (dval, dgc, cstat), got = _ffn_down_bwd(df, w_down, act, vd, [_swap_task(_by_device(dw_down))] if sharded else [])
    if sharded:
        send_down, own_down = _chip_sums(_by_device(dw_down), got[0][0])
    (du, dx1, dmix, gstat_u, bstat_u, cstat_w), got = _ffn_up_bwd(dgc, dval, gt, conv_w, w_up, x1, mod3, g_ffn, dx2, mix, B, S,
                                                                  [_exchange_task(send_down)] if sharded else [])
    if sharded:
        dw_down = (own_down, got[0][0])
    dw_up = _matmul_tn(du, h2, "dw_up", tm=F)
    dw_out = _matmul_tn(mixin, dmix, "dw_out")
    (doa, dob, gstat_o), got = _attn_out_bwd(dmix, w_out, oa, ob, g_na_out, g_sw_out,
                                             [_swap_task(_by_device(dw_up)), _swap_task(_by_device(dw_out))] if sharded else [])
    if sharded:
        send_up, own_up = _chip_sums(_by_device(dw_up), got[0][0])
        send_out, own_out = _chip_sums(_by_device(dw_out), got[1][0])
    (dqa, dka, dva, dbt), got = _na_bwd(qkv3, bias, doa.reshape(B, S, NA_WIDTH),
                                        [_exchange_task(send_up), _exchange_task(send_out)] if sharded else [])
    if sharded:
        dw_up, dw_out = (own_up, got[0][0]), (own_out, got[1][0])
    dqb, dkb, dvb, dsink = _sw_bwd(sink, qkv3, dob.reshape(B, S, SW_WIDTH))
    r2 = lambda t: t.reshape(T, t.shape[-1])
    grad_x, dproj, gstat_i, bstat_i = _attn_in_bwd(r2(dqa), r2(dka), r2(dva), r2(dqb), r2(dkb), r2(dvb), cos_t, sin_t, w_in, x2d, mod3,
                                                   g_attn, dx1, B, S)
    dw_in = _matmul_tn(dproj, h, "dw_in")

    dmod = jnp.stack([bstat_i[:, 0], bstat_i[:, 1], bstat_u[:, 2], bstat_u[:, 0], bstat_u[:, 1], bstat_f[:, 0]], axis=1)
    small = dict(g_attn=gstat_i[0], g_ffn=gstat_u[0], g_final=gstat_f[0], loss=gstat_f[1, 0], g_na_out=gstat_o[0], g_sw_out=gstat_o[1],
                 sw_sink=dsink[:, 0], conv_b=cstat[0], conv_w=cstat_w[1:4], dbt=dbt,
                 raw=(bstat_i, bstat_u, bstat_f, gstat_i, gstat_u, gstat_f, gstat_o, dsink, cstat, cstat_w))
    return grad_x.reshape(B, S, D), dict(w_in=dw_in, w_out=dw_out, w_up=dw_up, w_down=dw_down), dmod, small


def _pack_slab(raw, drpb):
    D, F = raw[3].shape[1], raw[8].shape[1]
    n_seq = raw[0].shape[0]

    def body(bi_ref, bu_ref, bf_ref, gi_ref, gu_ref, gf_ref, go_ref, ds_ref, cs_ref, cw_ref, rp_ref, o_ref):
        o_ref[...] = jnp.zeros_like(o_ref)
        for b in range(n_seq):
            mods = (bi_ref[b, 0:1, :], bi_ref[b, 1:2, :], bu_ref[b, 2:3, :], bu_ref[b, 0:1, :], bu_ref[b, 1:2, :], bf_ref[b, 0:1, :])
            for k, row in enumerate(mods):
                o_ref[b:b + 1, k * D:(k + 1) * D] = row
        o = 0
        for row in (gi_ref[0:1, :], gu_ref[0:1, :], gf_ref[0:1, :], go_ref[0:1, :], go_ref[1:2, :]):
            o_ref[2:3, o:o + row.shape[1]] = row
            o += row.shape[1]
        ds = ds_ref[...]
        eye = lax.broadcasted_iota(jnp.int32, ds.shape, 0) == lax.broadcasted_iota(jnp.int32, ds.shape, 1)
        o_ref[2:3, o:o + 128] = jnp.sum(jnp.where(eye, ds, 0.0), axis=0, keepdims=True)
        o_ref[2:3, o + 128:o + 256] = gf_ref[1:2, 0:128]
        o_ref[3:4, 0:F] = cs_ref[0:1, :]
        o_ref[4:5, 0:rp_ref.shape[1]] = rp_ref[...]
        o_ref[5:8, 0:F] = cw_ref[1:4, :]

    return pl.pallas_call(body, name="pack_slab", out_shape=jax.ShapeDtypeStruct((8, PACK_W), F32),
                          compiler_params=pltpu.CompilerParams(vmem_limit_bytes=VMEM_LIMIT))(*raw, drpb)


def kernel(x, c, w_ada, b_ada, g_attn, w_in, na_rpb, sw_sink, g_na_out, g_sw_out, w_out, g_ffn, w_up, conv_w, conv_b, w_down, g_final, loss_target, m_w_ada, m_b_ada, m_g_attn, m_w_in, m_na_rpb, m_sw_sink, m_g_na_out, m_g_sw_out, m_w_out, m_g_ffn, m_w_up, m_conv_w, m_conv_b, m_w_down, m_g_final, v_w_ada, v_b_ada, v_g_attn, v_w_in, v_na_rpb, v_sw_sink, v_g_na_out, v_g_sw_out, v_w_out, v_g_ffn, v_w_up, v_conv_w, v_conv_b, v_w_down, v_g_final):
    B, S, D = x.shape
    me = 4 * lax.axis_index("x") + 2 * lax.axis_index("y") + lax.axis_index("c")
    ada_c = w_ada.shape[2]
    F_l = conv_w.shape[2]

    tr = {"w_in", "w_up"}
    w_in_t = jnp.transpose(w_in[0])
    shards_f = dict(w_out=w_out[0], w_up=jnp.transpose(w_up[0]), w_down=w_down[0])

    c_all, conv_w_f, mod_all, w_in_all, casts, bias = _ada_fwd(c, conv_w[0], w_ada[0], b_ada, w_in_t, list(shards_f.values()),
                                                               _na_bias_rev(na_rpb[0]))
    shards = dict(zip(shards_f, casts))
    mod_mine = lax.dynamic_slice(mod_all, (0, me * B, 0), (N_DEV, B, ada_c))
    mod = jnp.transpose(mod_mine, (1, 0, 2)).reshape(B, N_DEV * ada_c)
    w_in_f = w_in_all.reshape(N_DEV * w_in_t.shape[0], D)

    grad_x, dw, dmod, small = _local_step(x, mod, g_attn, w_in_f, bias, sw_sink, g_na_out, g_sw_out, shards["w_out"], g_ffn,
                                          shards["w_up"], conv_w_f, conv_b, shards["w_down"], g_final.reshape(1, D), loss_target,
                                          sharded=True)
    g8_in = _by_device(dw["w_in"])
    drpb, got = _na_bias_grad(small["dbt"], [_swap_task(g8_in)])
    send_in, own_in = _chip_sums(g8_in, got[0][0])

    slab = _pack_slab(small["raw"], drpb.reshape(1, -1))
    weights = dict(w_ada=w_ada, b_ada=b_ada, g_attn=g_attn, w_in=w_in, na_rpb=na_rpb, sw_sink=sw_sink, g_na_out=g_na_out,
                   g_sw_out=g_sw_out, w_out=w_out, g_ffn=g_ffn, w_up=w_up, conv_w=conv_w, conv_b=conv_b, w_down=w_down, g_final=g_final)
    ms = dict(w_ada=m_w_ada, b_ada=m_b_ada, g_attn=m_g_attn, w_in=m_w_in, na_rpb=m_na_rpb, sw_sink=m_sw_sink, g_na_out=m_g_na_out,
              g_sw_out=m_g_sw_out, w_out=m_w_out, g_ffn=m_g_ffn, w_up=m_w_up, conv_w=m_conv_w, conv_b=m_conv_b, w_down=m_w_down, g_final=m_g_final)
    vs = dict(w_ada=v_w_ada, b_ada=v_b_ada, g_attn=v_g_attn, w_in=v_w_in, na_rpb=v_na_rpb, sw_sink=v_sw_sink, g_na_out=v_g_na_out,
              g_sw_out=v_g_sw_out, w_out=v_w_out, g_ffn=v_g_ffn, w_up=v_w_up, conv_w=v_conv_w, conv_b=v_conv_b, w_down=v_w_down, g_final=v_g_final)
    names = list(weights)
    grads, deltas, new_m, new_v = {}, {}, {}, {}
    flat = lambda t: t.reshape(1, -1)

    def shard2d(nm):
        if nm in tr:
            return (lambda t: jnp.transpose(t[0])), (lambda t: jnp.transpose(t)[None])
        if nm == "conv_w":
            return (lambda t: jnp.transpose(t, (1, 0, 2))), (lambda t: jnp.transpose(t, (1, 0, 2)))
        return (lambda t: t[0]), (lambda t: t[None])

    def finish_sum(nm, own, recvb, done=0):
        r, back = shard2d(nm)
        g2, d_, m_, v_ = _sum_adamw(own, recvb, r(weights[nm]), r(ms[nm]), r(vs[nm]), "adamw_" + nm, done)
        grads[nm], deltas[nm], new_m[nm], new_v[nm] = back(g2), back(d_), back(m_), back(v_)

    own_up, recv_up = dw["w_up"]
    n_up = _sum_adamw_steps(own_up.shape[0])
    part_up, got = _sum_first(own_up, recv_up, "sum_first_w_up", [_gather_task(slab, n_up - 1), _exchange_task(send_in)])
    packs = got[0][0]
    finish_sum("w_up", part_up, recv_up, done=2)
    finish_sum("w_down", *dw["w_down"])
    finish_sum("w_in", own_in, got[1][0])
    finish_sum("w_out", *dw["w_out"])

    where = dict(b_ada=((0, 1), 0), g_attn=((2,), 0), g_ffn=((2,), D), g_final=((2,), 2 * D), g_na_out=((2,), 3 * D),
                 g_sw_out=((2,), 3 * D + NA_WIDTH), sw_sink=((2,), 3 * D + NA_WIDTH + SW_WIDTH), conv_b=((3,), 0), na_rpb=((4,), 0))

    def small_view(n):
        if n == "na_rpb":
            return (lambda t: jnp.transpose(t[0], (1, 0, 2))), (lambda t: jnp.transpose(t, (1, 0, 2))[None])
        return flat, (lambda t: t.reshape(weights[n].shape))

    tot, loss, small_out = _pack_sum_adamw(
        packs.reshape(N_DEV * 8, PACK_W),
        [tuple(small_view(n)[0](t[n]) for t in (weights, ms, vs)) + where[n] for n in where],
        (2, 3 * D + NA_WIDTH + SW_WIDTH + 128))
    for n, res in zip(where, small_out):
        grads[n], deltas[n], new_m[n], new_v[n] = [small_view(n)[1](t) for t in res]
    loss = loss.reshape(())

    for nm, g2 in (("w_ada", _ada_bwd(c_all, packs.reshape(N_DEV * 8, PACK_W), ada_c)), ("conv_w", lax.dynamic_slice(tot, (5, me * F_l), (3, F_l))[:, None])):
        r, back = shard2d(nm)
        d_, m_, v_ = _adamw(r(weights[nm]), g2, r(ms[nm]), r(vs[nm]), "adamw_" + nm)
        grads[nm], deltas[nm], new_m[nm], new_v[nm] = back(g2), back(d_), back(m_), back(v_)
    return (loss, grad_x, *[grads[n] for n in names], *[deltas[n] for n in names], *[new_m[n] for n in names],
            *[new_v[n] for n in names])
```

```python
import functools

import numpy as np
import jax
import jax.numpy as jnp
from jax import lax
from jax.experimental import pallas as pl
from jax.experimental.pallas import tpu as pltpu

F32, BF16 = jnp.float32, jnp.bfloat16
MESH_ID = pl.DeviceIdType.MESH
N_DEV = 8

HEAD_DIM = 64
NA_HEADS = 8
SW_HEADS = 8
SW_KV_HEADS = 2
SW_GROUP = SW_HEADS // SW_KV_HEADS
NA_WIDTH = NA_HEADS * HEAD_DIM
SW_WIDTH = SW_HEADS * HEAD_DIM
SW_KV_WIDTH = SW_KV_HEADS * HEAD_DIM
ROPE_WIDTH = SW_WIDTH + SW_KV_WIDTH
IN_WIDTH = 3 * NA_WIDTH + SW_WIDTH + 2 * SW_KV_WIDTH
ROPE_LO = 3 * NA_WIDTH
GRID_W = 64
NA_ROWS_MAX = 8
NA_COLS = 16
N_DR = 2 * NA_ROWS_MAX - 1
N_DC = 2 * NA_COLS - 1
SW_WINDOW = 128
SW_BLOCK = 128
ROPE_THETA = 10000.0
EPS = 1e-6
NEG = -1e30
Q_SCALE = HEAD_DIM ** -0.5

ADAM_LR = 0.001
ADAM_B1 = 0.9
ADAM_B2 = 0.999
ADAM_EPS = 1e-08
ADAM_WD = 0.01
ADAM_STEP = 10

TOKEN_TILE = 256
WIDE_TILE = 512
VMEM_LIMIT = 56 * 1024 * 1024

PACK_W = 6144


def _nn(a, b):
    return jnp.dot(a, b, preferred_element_type=F32)


def _nt(a, b):
    return lax.dot_general(a, b, (((1,), (1,)), ((), ())), preferred_element_type=F32)


def _tn(a, b):
    return lax.dot_general(a, b, (((0,), (0,)), ((), ())), preferred_element_type=F32)


def _rms(x):
    r = lax.rsqrt(jnp.mean(x * x, axis=-1, keepdims=True) + EPS)
    return x * r, r


def _rms_bwd(xn, r, gy):
    return r * (gy - xn * jnp.mean(xn * gy, axis=-1, keepdims=True))


def _params(*sem):
    return pltpu.CompilerParams(dimension_semantics=sem, vmem_limit_bytes=VMEM_LIMIT)


def _full(shape):
    n = len(shape)
    return pl.BlockSpec(shape, lambda *_: (0,) * n)


def _mesh_pos():
    return lax.axis_index("x"), lax.axis_index("y"), lax.axis_index("c")


def _row_chunk(r):
    for rc in (128, 64, 32, 16):
        if r % rc == 0:
            return rc
    raise ValueError(f"rows {r} not a multiple of 16")


class _Task:
    def __init__(self, inputs, out_shapes, sems, start, finish, mid=None, mid_step=None, alias=None):
        self.inputs, self.out_shapes, self.sems = list(inputs), list(out_shapes), list(sems)
        self.start, self.finish, self.mid, self.mid_step = start, finish, mid, mid_step
        self.alias = alias


def _hosted_call(body, name, grid, in_specs, out_specs, out_shape, operands, tasks, scratch_shapes=()):
    n_in, n_out, n_scr = len(in_specs), len(out_specs), len(scratch_shapes)
    t_in = [len(t.inputs) for t in tasks]
    t_out = [len(t.out_shapes) for t in tasks]
    t_sem = [len(t.sems) for t in tasks]
    n_steps = int(np.prod(grid))

    def wrapped(*refs):
        ins, rest = refs[:n_in], refs[n_in:]
        task_ins, rest = rest[:sum(t_in)], rest[sum(t_in):]
        outs, rest = rest[:n_out], rest[n_out:]
        task_outs, rest = rest[:sum(t_out)], rest[sum(t_out):]
        scr, task_sems = rest[:n_scr], rest[n_scr:]
        step = pl.program_id(0)
        for ax in range(1, len(grid)):
            step = step * grid[ax] + pl.program_id(ax)
        parts = []
        oi = oo = os_ = 0
        for t, a, b, c in zip(tasks, t_in, t_out, t_sem):
            parts.append((t, task_ins[oi:oi + a], task_outs[oo:oo + b], task_sems[os_:os_ + c]))
            oi, oo, os_ = oi + a, oo + b, os_ + c
        for t, ti, to, ts in parts:
            pl.when(step == 0)(functools.partial(t.start, ti, to, ts))
            if t.mid is not None:
                pl.when(step == t.mid_step)(functools.partial(t.mid, ti, to, ts))
        body(*ins, *outs, *scr)
        for t, ti, to, ts in parts:
            pl.when(step == n_steps - 1)(functools.partial(t.finish, ti, to, ts))

    hbm = pl.BlockSpec(memory_space=pl.ANY)
    aliases, oi, oo = {}, n_in, n_out
    for t, a, b in zip(tasks, t_in, t_out):
        if t.alias is not None:
            aliases[oi + t.alias[0]] = oo + t.alias[1]
        oi, oo = oi + a, oo + b
    res = pl.pallas_call(
        wrapped, name=name, grid=grid,
        in_specs=list(in_specs) + [hbm] * sum(t_in),
        out_specs=list(out_specs) + [hbm] * sum(t_out),
        out_shape=list(out_shape) + [s for t in tasks for s in t.out_shapes],
        scratch_shapes=list(scratch_shapes) + [s for t in tasks for s in t.sems],
        input_output_aliases=aliases,
        compiler_params=_params(*(["arbitrary"] * len(grid))),
    )(*operands, *[a for t in tasks for a in t.inputs])
    own, extra = res[:n_out], res[n_out:]
    per_task, o = [], 0
    for b in t_out:
        per_task.append(extra[o:o + b])
        o += b
    return own, per_task


def _gather_task(shard, mid_step, rows=None, into=None):
    lo, n = (0, shard.shape[0]) if rows is None else rows

    def parts(ins, outs, sems):
        x_ref, out_ref, (send_sems, recv_sems, local_sem) = ins[0], outs[0], sems
        x_, y_, c_ = _mesh_pos()
        me, sibling = (x_, y_, c_), (x_, y_, 1 - c_)
        chips = [(1 - x_, y_), (x_, 1 - y_), (1 - x_, 1 - y_)]
        x_ref = x_ref.at[pl.ds(lo, n)]

        def rows(px, py, pc):
            return out_ref.at[4 * px + 2 * py + pc, pl.ds(lo, n)]

        def copy(k, block, to, src=None):
            return pltpu.make_async_remote_copy(
                src_ref=rows(*block) if src is None else src, dst_ref=rows(*block),
                send_sem=send_sems.at[k], recv_sem=recv_sems.at[k], device_id=to, device_id_type=MESH_ID)

        return dict(
            mine=lambda: pltpu.make_async_copy(x_ref, rows(*me), local_sem),
            first=lambda: [copy(0, me, sibling, src=x_ref)] + [copy(1 + j, me, (*chip, c_), src=x_ref) for j, chip in enumerate(chips)],
            passed=lambda: [copy(4 + j, (*chip, c_), sibling) for j, chip in enumerate(chips)],
            landed=lambda: [copy(1 + j, (*chip, c_), me) for j, chip in enumerate(chips)],
            last=lambda: [copy(0, sibling, me)] + [copy(4 + j, (*chip, 1 - c_), me) for j, chip in enumerate(chips)])

    def start(ins, outs, sems):
        p = parts(ins, outs, sems)
        p["mine"]().start()
        for cp in p["first"]():
            cp.start()

    def mid(ins, outs, sems):
        p = parts(ins, outs, sems)
        for cp, fw in zip(p["landed"](), p["passed"]()):
            cp.wait_recv()
            fw.start()

    def finish(ins, outs, sems):
        p = parts(ins, outs, sems)
        for cp in p["last"]():
            cp.wait_recv()
        for cp in p["first"]() + p["passed"]():
            cp.wait_send()
        p["mine"]().wait()

    return _Task([shard] if into is None else [shard, into], [jax.ShapeDtypeStruct((N_DEV,) + shard.shape, shard.dtype)],
                 [pltpu.SemaphoreType.DMA((7,)), pltpu.SemaphoreType.DMA((7,)), pltpu.SemaphoreType.DMA],
                 start, finish, mid, mid_step, alias=None if into is None else (1, 0))


def _swap_task(g8, chips=(0, 1, 2, 3), into=None):
    _, R, C = g8.shape

    def copies(ins, outs, sems):
        g_ref, (recv_ref,), (ss, rs) = ins[0], outs, sems
        x_, y_, c_ = _mesh_pos()
        return [pltpu.make_async_remote_copy(src_ref=g_ref.at[2 * k + (1 - c_)], dst_ref=recv_ref.at[k], send_sem=ss.at[k],
                                             recv_sem=rs.at[k], device_id=(x_, y_, 1 - c_), device_id_type=MESH_ID)
                for k in chips]

    def start(ins, outs, sems):
        for cp in copies(ins, outs, sems):
            cp.start()

    def finish(ins, outs, sems):
        cps = copies(ins, outs, sems)
        for cp in cps:
            cp.wait_recv()
        for cp in cps:
            cp.wait_send()

    return _Task([g8] if into is None else [g8, into], [jax.ShapeDtypeStruct((4, R, C), g8.dtype)],
                 [pltpu.SemaphoreType.DMA((4,)), pltpu.SemaphoreType.DMA((4,))], start, finish,
                 alias=None if into is None else (1, 0))


def _chip_sums(g8, recva):
    _, R, C = g8.shape
    rt = _sum_adamw_rows(R)
    rc = _row_chunk(rt)

    def body(core_ref, g_ref, a_ref, send_ref, own_ref):
        x_, y_, _ = _mesh_pos()
        chips = [(1 - x_, y_), (x_, 1 - y_), (1 - x_, 1 - y_), (x_, y_)]

        def chunk(i, carry):
            rows = pl.ds(pl.multiple_of(i * rc, rc), rc)
            for j, (tx, ty) in enumerate(chips):
                k = 2 * tx + ty
                s = g_ref[k, rows, :].astype(F32) + a_ref[k, rows, :].astype(F32)
                if j < 3:
                    send_ref[j, rows, :] = s.astype(BF16)
                else:
                    own_ref[rows, :] = s
            return carry

        lax.fori_loop(0, rt // rc, chunk, 0)

    core = lax.axis_index("c").astype(jnp.int32).reshape(1)
    return pl.pallas_call(
        body, name="chip_sums",
        grid_spec=pltpu.PrefetchScalarGridSpec(
            num_scalar_prefetch=1, grid=(R // rt,),
            in_specs=[pl.BlockSpec((4, None, rt, C), lambda i, c: (0, c[0], i, 0)), pl.BlockSpec((4, rt, C), lambda i, c: (0, i, 0))],
            out_specs=[pl.BlockSpec((3, rt, C), lambda i, c: (0, i, 0)), pl.BlockSpec((rt, C), lambda i, c: (i, 0))]),
        out_shape=[jax.ShapeDtypeStruct((3, R, C), BF16), jax.ShapeDtypeStruct((R, C), F32)],
        compiler_params=_params("parallel"),
    )(core, g8.reshape(4, 2, R, C), recva)


def _exchange_task(sendb):
    def copies(ins, outs, sems):
        (s_ref,), (recv_ref,), (ss, rs) = ins, outs, sems
        x_, y_, c_ = _mesh_pos()
        flips = [(1 - x_, y_), (x_, 1 - y_), (1 - x_, 1 - y_)]
        return [pltpu.make_async_remote_copy(src_ref=s_ref.at[j], dst_ref=recv_ref.at[j], send_sem=ss.at[j], recv_sem=rs.at[j],
                                             device_id=(tx, ty, c_), device_id_type=MESH_ID) for j, (tx, ty) in enumerate(flips)]

    def start(ins, outs, sems):
        for cp in copies(ins, outs, sems):
            cp.start()

    def finish(ins, outs, sems):
        cps = copies(ins, outs, sems)
        for cp in cps:
            cp.wait_recv()
        for cp in cps:
            cp.wait_send()

    return _Task([sendb], [jax.ShapeDtypeStruct(sendb.shape, sendb.dtype)],
                 [pltpu.SemaphoreType.DMA((3,)), pltpu.SemaphoreType.DMA((3,))], start, finish)


def _silu(v):
    return v * (1.0 / (1.0 + jnp.exp(-v)))


def _ada_fwd(c, taps, w_ada_l, b_ada, w_in_shard, casts, rpb_rev):
    n_seq = c.shape[0]
    D, cols = w_ada_l.shape
    F_l = taps.shape[1]
    W = D + F_l
    n_rows = N_DEV * n_seq
    n_c = len(casts)
    t_c = _gather_task(jax.ShapeDtypeStruct((8, W), F32), 0)
    t_w = _gather_task(jax.ShapeDtypeStruct(w_in_shard.shape, BF16), 0)
    t_m = _gather_task(jax.ShapeDtypeStruct((n_rows, cols), F32), 0)

    def body(c_ref, taps_ref, w_ref, b_ref, ws_ref, rev_ref, *refs):
        cast_in, refs = refs[:n_c], refs[n_c:]
        (slabs_ref, call_ref, taps_all_ref, mod_ref, win_ref), refs = refs[:5], refs[5:]
        cast_out, bias_ref, refs = refs[:n_c], refs[n_c], refs[n_c + 1:]
        (slab_vm, c_vm, m_vm, ws_f, ws_b), refs = refs[:5], refs[5:]
        stage_f, stage_b, refs = refs[:n_c], refs[n_c:2 * n_c], refs[2 * n_c:]
        copy_sem, in_sems, out_sems, sems = refs[0], refs[1], refs[2], refs[3:]
        sc, sw, sm = sems[0:3], sems[3:6], sems[6:9]
        x_, y_, c_ = _mesh_pos()
        slab_vm[...] = jnp.zeros_like(slab_vm)
        slab_vm[0:n_seq, 0:D] = c_ref[...]
        slab_vm[0:taps.shape[0], D:W] = taps_ref[...]
        t_c.start((slab_vm,), (slabs_ref,), sc)
        cp = pltpu.make_async_copy(ws_ref, ws_f, copy_sem)
        cp.start()
        cp.wait()
        ws_b[...] = ws_f[...].astype(BF16)
        t_w.start((ws_b,), (win_ref,), sw)
        loads = [pltpu.make_async_copy(cast_in[j], stage_f[j], in_sems.at[j]) for j in range(n_c)]
        for ld in loads:
            ld.start()
        t_c.mid((slab_vm,), (slabs_ref,), sc)
        t_c.finish((slab_vm,), (slabs_ref,), sc)
        cp = pltpu.make_async_copy(slabs_ref, c_vm, copy_sem)
        cp.start()
        cp.wait()
        c_all = c_vm[:, :, 0:D].reshape(N_DEV * 8, D)
        call_ref[...] = c_all
        for j in range(N_DEV):
            taps_all_ref[:, j * F_l:(j + 1) * F_l] = c_vm[j, 0:taps.shape[0], D:W]
        b_mine = b_ref[:, pl.ds(pl.multiple_of((4 * x_ + 2 * y_ + c_) * cols, 128), cols)]
        m64 = jnp.dot(_silu(c_all), w_ref[...], precision=lax.Precision.HIGHEST, preferred_element_type=F32) + b_mine
        r = lax.broadcasted_iota(jnp.int32, (n_rows, N_DEV * 8), 0)
        c = lax.broadcasted_iota(jnp.int32, (n_rows, N_DEV * 8), 1)
        pick = jnp.where(c == 8 * (r // n_seq) + r % n_seq, 1.0, 0.0)
        m_vm[...] = jnp.dot(pick, m64, precision=lax.Precision.HIGHEST, preferred_element_type=F32)
        t_m.start((m_vm,), (mod_ref,), sm)
        stores = [pltpu.make_async_copy(stage_b[j], cast_out[j], out_sems.at[j]) for j in range(n_c)]
        for j in range(n_c):
            loads[j].wait()
            stage_b[j][...] = stage_f[j][...].astype(BF16)
            stores[j].start()
        for p in range(NA_PAIRS):
            _na_bias_rows(rev_ref, bias_ref, p)
        t_w.mid((ws_b,), (win_ref,), sw)
        t_m.mid((m_vm,), (mod_ref,), sm)
        t_m.finish((m_vm,), (mod_ref,), sm)
        t_w.finish((ws_b,), (win_ref,), sw)
        for st in stores:
            st.wait()

    hbm, vm = pl.BlockSpec(memory_space=pl.ANY), pl.BlockSpec(memory_space=pltpu.VMEM)
    res = pl.pallas_call(
        body, name="ada_fwd", in_specs=[vm, vm, vm, vm, hbm, vm] + [hbm] * n_c,
        out_specs=[hbm, vm, vm, hbm, hbm] + [hbm] * n_c + [vm],
        out_shape=t_c.out_shapes + [jax.ShapeDtypeStruct((N_DEV * 8, D), F32), jax.ShapeDtypeStruct((taps.shape[0], N_DEV * F_l), F32)]
        + t_m.out_shapes + t_w.out_shapes
        + [jax.ShapeDtypeStruct(a.shape, BF16) for a in casts] + [jax.ShapeDtypeStruct((NA_PAIRS, N_DR * GRID_W, 128), F32)],
        scratch_shapes=[pltpu.VMEM((8, W), F32), pltpu.VMEM((N_DEV, 8, W), F32), pltpu.VMEM((n_rows, cols), F32),
                        pltpu.VMEM(w_in_shard.shape, F32), pltpu.VMEM(w_in_shard.shape, BF16)]
        + [pltpu.VMEM(a.shape, F32) for a in casts] + [pltpu.VMEM(a.shape, BF16) for a in casts]
        + [pltpu.SemaphoreType.DMA, pltpu.SemaphoreType.DMA((n_c,)), pltpu.SemaphoreType.DMA((n_c,))]
        + t_c.sems + t_w.sems + t_m.sems,
        compiler_params=pltpu.CompilerParams(vmem_limit_bytes=VMEM_LIMIT),
    )(c, taps, w_ada_l, b_ada, w_in_shard, rpb_rev, *casts)
    return res[1], res[2], res[3], res[4], res[5:5 + n_c], res[5 + n_c]


def _ada_bwd(c_all, packs, cols):
    def body(c_ref, d_ref, o_ref):
        x_, y_, c_ = _mesh_pos()
        mine = d_ref[:, pl.ds(pl.multiple_of((4 * x_ + 2 * y_ + c_) * cols, 128), cols)]
        o_ref[...] = lax.dot_general(_silu(c_ref[...]), mine, (((0,), (0,)), ((), ())),
                                     precision=lax.Precision.HIGHEST, preferred_element_type=F32)
    return pl.pallas_call(body, name="ada_bwd", out_shape=jax.ShapeDtypeStruct((c_all.shape[1], cols), F32),
                          compiler_params=pltpu.CompilerParams(vmem_limit_bytes=VMEM_LIMIT))(c_all, packs)


NA_PAIRS = NA_HEADS // 2
N_DR_PAD = 16


def _na_bias_rev(na_rpb):
    rev = jnp.pad(jnp.flip(na_rpb, axis=2), ((0, 0), (0, N_DR_PAD - N_DR), (0, GRID_W - N_DC)))
    return jnp.transpose(rev.reshape(NA_PAIRS, 2, N_DR_PAD, GRID_W), (0, 2, 1, 3)).reshape(NA_PAIRS, N_DR_PAD, 128)


def _na_bias_rows(r_ref, o_ref, p):
    k = lax.broadcasted_iota(jnp.int32, (GRID_W, 128), 0)
    lane = lax.broadcasted_iota(jnp.int32, (GRID_W, 128), 1)
    q = lane % GRID_W
    cs = jnp.clip(q - NA_COLS // 2, 0, GRID_W - NA_COLS)
    ok = (k >= cs) & (k < cs + NA_COLS)
    left = lane < GRID_W
    for dr in range(N_DR):
        row = jnp.broadcast_to(r_ref[p, dr:dr + 1, :], (GRID_W, 128))
        r0 = jnp.where(left, row, 0.0)
        r1 = jnp.where(left, pltpu.roll(row, GRID_W, axis=1), 0.0)
        y0 = pltpu.roll(r0, 128 - (NA_COLS - 1), axis=1, stride=1, stride_axis=0)
        y1 = pltpu.roll(r1, GRID_W - (NA_COLS - 1), axis=1, stride=1, stride_axis=0)
        o_ref[p, dr * GRID_W:(dr + 1) * GRID_W, :] = jnp.where(ok, jnp.where(left, y0, y1), NEG)


def _na_bias_grad(db, tasks=()):
    a = np.arange(128)
    flip = jnp.asarray(((a[:, None] // GRID_W == a[None, :] // GRID_W)
                        & (a[:, None] % GRID_W + a[None, :] % GRID_W == GRID_W - 1)).astype(np.float32))

    def body(d_ref, j_ref, o_ref):
        o_ref[...] = jnp.zeros_like(o_ref)
        for dr in range(N_DR):
            t = jnp.dot(d_ref[0, dr * GRID_W:(dr + 1) * GRID_W, :], j_ref[...], precision=lax.Precision.HIGHEST, preferred_element_type=F32)
            t = pltpu.roll(t, GRID_W + NA_COLS, axis=1, stride=1, stride_axis=0)
            o_ref[0, dr:dr + 1, :] = jnp.sum(t, axis=0, keepdims=True)

    (rows,), got = _hosted_call(
        body, "rpb_reduce", (NA_PAIRS,),
        [pl.BlockSpec((1, N_DR * GRID_W, 128), lambda p: (p, 0, 0)), _full((128, 128))],
        [pl.BlockSpec((1, N_DR_PAD, 128), lambda p: (p, 0, 0))],
        [jax.ShapeDtypeStruct((NA_PAIRS, N_DR_PAD, 128), F32)], (db, flip), tasks)
    g = rows.reshape(NA_PAIRS, N_DR_PAD, 2, GRID_W)[:, :N_DR, :, :N_DC]
    return jnp.transpose(g, (0, 2, 1, 3)).reshape(-1), got


def _rope_tables(S):
    half = HEAD_DIM // 2
    inv = np.float32(ROPE_THETA) ** (-np.arange(half, dtype=np.float32) / np.float32(half))
    ang = np.arange(S).astype(np.float32)[:, None] * inv[None, :]
    cos, sin = np.cos(ang).astype(np.float32), np.sin(ang).astype(np.float32)
    return jnp.asarray(np.tile(np.concatenate([cos, cos], axis=1), (1, 2))), jnp.asarray(np.tile(np.concatenate([-sin, sin], axis=1), (1, 2)))


def _rope_spec(tps, tm=TOKEN_TILE):
    return pl.BlockSpec((tm, 2 * HEAD_DIM), lambda i: (i % tps, 0))


def _rot_half(t):
    w = t.shape[1]
    lane = lax.broadcasted_iota(jnp.int32, t.shape, 1)
    return jnp.where((lane % HEAD_DIM) < HEAD_DIM // 2, pltpu.roll(t, w - HEAD_DIM // 2, axis=1),
                     pltpu.roll(t, HEAD_DIM // 2, axis=1))


def _tok_spec(w, tm=TOKEN_TILE):
    return pl.BlockSpec((tm, w), lambda i: (i, 0))


def _mod_spec(tps, d):
    return pl.BlockSpec((1, 6, d), lambda i: (i // tps, 0, 0))


def _bstat_spec(tps, w):
    return pl.BlockSpec((1, 8, w), lambda i: (i // tps, 0, 0))


def _attn_in(x2d, mod3, g_attn, w_in, cos_t, sin_t, S, tasks=(), tm=WIDE_TILE):
    T, D = x2d.shape
    tps = S // tm

    def body(x_ref, mod_ref, g_ref, w_ref, cos_ref, sin_ref, h_ref, qkv_ref):
        xn, _ = _rms(x_ref[...])
        h = (xn * g_ref[...]) * (1.0 + mod_ref[0, 1:2, :]) + mod_ref[0, 0:1, :]
        hb = h.astype(BF16)
        h_ref[...] = hb
        proj = _nt(hb, w_ref[...])
        rb = proj[:, ROPE_LO:ROPE_LO + ROPE_WIDTH]
        reps = (1, ROPE_WIDTH // (2 * HEAD_DIM))
        rb = rb * jnp.tile(cos_ref[...], reps) + _rot_half(rb) * jnp.tile(sin_ref[...], reps)
        qkv_ref[:, 0:NA_WIDTH] = (proj[:, 0:NA_WIDTH] * Q_SCALE).astype(BF16)
        qkv_ref[:, NA_WIDTH:ROPE_LO] = proj[:, NA_WIDTH:ROPE_LO].astype(BF16)
        qkv_ref[:, ROPE_LO:ROPE_LO + SW_WIDTH] = (rb[:, 0:SW_WIDTH] * Q_SCALE).astype(BF16)
        qkv_ref[:, ROPE_LO + SW_WIDTH:ROPE_LO + ROPE_WIDTH] = rb[:, SW_WIDTH:].astype(BF16)
        qkv_ref[:, ROPE_LO + ROPE_WIDTH:] = proj[:, ROPE_LO + ROPE_WIDTH:].astype(BF16)

    return _hosted_call(
        body, "attn_in", (T // tm,),
        [_tok_spec(D, tm), _mod_spec(tps, D), _full((1, D)), _full(w_in.shape), _rope_spec(tps, tm), _rope_spec(tps, tm)],
        [_tok_spec(D, tm), _tok_spec(IN_WIDTH, tm)],
        [jax.ShapeDtypeStruct((T, D), BF16), jax.ShapeDtypeStruct((T, IN_WIDTH), BF16)],
        (x2d, mod3, g_attn, w_in, cos_t, sin_t), tasks)


def _attn_out(oa, ob, x2d, mod3, g_na, g_sw, w_out, S, tasks=(), tm=WIDE_TILE):
    T, D = x2d.shape
    tps = S // tm

    def body(oa_ref, ob_ref, x_ref, mod_ref, gna_ref, gsw_ref, w_ref, mixin_ref, mix_ref, x1_ref):
        oan, _ = _rms(oa_ref[...])
        obn, _ = _rms(ob_ref[...])
        mixin = jnp.concatenate([oan * gna_ref[...], obn * gsw_ref[...]], axis=1).astype(BF16)
        mixin_ref[...] = mixin
        mix = _nn(mixin, w_ref[...])
        mix_ref[...] = mix
        x1_ref[...] = x_ref[...] + mod_ref[0, 2:3, :] * mix

    return _hosted_call(
        body, "attn_out", (T // tm,),
        [_tok_spec(NA_WIDTH, tm), _tok_spec(SW_WIDTH, tm), _tok_spec(D, tm), _mod_spec(tps, D),
         _full((1, NA_WIDTH)), _full((1, SW_WIDTH)), _full(w_out.shape)],
        [_tok_spec(NA_WIDTH + SW_WIDTH, tm), _tok_spec(D, tm), _tok_spec(D, tm)],
        [jax.ShapeDtypeStruct((T, NA_WIDTH + SW_WIDTH), BF16), jax.ShapeDtypeStruct((T, D), F32), jax.ShapeDtypeStruct((T, D), F32)],
        (oa, ob, x2d, mod3, g_na, g_sw, w_out), tasks)


def _ffn_up(x1, mod3, g_ffn, w_up, S, tasks=(), tm=WIDE_TILE):
    T, D = x1.shape
    F = w_up.shape[0] // 2
    tps = S // tm

    def body(x1_ref, mod_ref, g_ref, w_ref, h2_ref, val_ref, gt_ref):
        xn, _ = _rms(x1_ref[...])
        h2 = ((xn * g_ref[...]) * (1.0 + mod_ref[0, 4:5, :]) + mod_ref[0, 3:4, :]).astype(BF16)
        h2_ref[...] = h2
        u = _nt(h2, w_ref[...])
        val_ref[...] = u[:, :F].astype(BF16)
        gt_ref[...] = u[:, F:].astype(BF16)

    return _hosted_call(
        body, "ffn_up", (T // tm,), [_tok_spec(D, tm), _mod_spec(tps, D), _full((1, D)), _full(w_up.shape)],
        [_tok_spec(D, tm), _tok_spec(F, tm), _tok_spec(F, tm)],
        [jax.ShapeDtypeStruct((T, D), BF16), jax.ShapeDtypeStruct((T, F), BF16), jax.ShapeDtypeStruct((T, F), BF16)],
        (x1, mod3, g_ffn, w_up), tasks)


def _halo_specs(T, tps, w):
    per = TOKEN_TILE // 8
    prev = pl.BlockSpec((8, w), lambda i: (jnp.maximum(i * per - 1, 0), 0))
    nxt = pl.BlockSpec((8, w), lambda i: (jnp.minimum((i + 1) * per, T // 8 - 1), 0))
    return prev, nxt


def _seq_shifts(cur, before, after, ti, tps):
    tm = cur.shape[0]
    row = lax.broadcasted_iota(jnp.int32, cur.shape, 0)
    before = jnp.where(ti > 0, before.astype(F32), 0.0)
    after = jnp.where(ti < tps - 1, after.astype(F32), 0.0)
    return jnp.where(row == 0, before, pltpu.roll(cur, 1, axis=0)), jnp.where(row == tm - 1, after, pltpu.roll(cur, tm - 1, axis=0))


def _ffn_down(gt, val, conv_w, conv_b, w_down, x1, mod3, g_final, target, B, S):
    T, D = x1.shape
    F = gt.shape[1]
    tps = S // TOKEN_TILE
    prev, nxt = _halo_specs(T, tps, F)

    def body(gt_ref, prev_ref, next_ref, val_ref, cw_ref, cb_ref, w_ref, x1_ref, mod_ref, gf_ref, tgt_ref,
             a_ref, act_ref, vd_ref, dx2_ref, df_ref, gstat_ref, bstat_ref):
        i = pl.program_id(0)
        g = gt_ref[...].astype(F32)
        gprev, gnext = _seq_shifts(g, prev_ref[7:8, :], next_ref[0:1, :], i % tps, tps)
        gc = gprev * cw_ref[0:1, :] + g * cw_ref[1:2, :] + gnext * cw_ref[2:3, :] + cb_ref[...]
        sig = 1.0 / (1.0 + jnp.exp(-gc))
        act = gc * sig
        val = val_ref[...].astype(F32)
        act_ref[...] = act.astype(BF16)
        vd_ref[...] = (val * (sig + act - act * sig)).astype(BF16)
        a = (act * val).astype(BF16)
        a_ref[...] = a
        f = _nn(a, w_ref[...])
        gate = mod_ref[0, 5:6, :]
        x2 = x1_ref[...] + gate * f
        xn, r = _rms(x2)
        err = xn * gf_ref[...] - tgt_ref[...]
        dy = err * (1.0 / D)
        dx2 = _rms_bwd(xn, r, dy * gf_ref[...])
        dx2_ref[...] = dx2
        df_ref[...] = (gate * dx2).astype(BF16)

        @pl.when(i == 0)
        def _():
            gstat_ref[...] = jnp.zeros_like(gstat_ref)

        @pl.when(i % tps == 0)
        def _():
            bstat_ref[...] = jnp.zeros_like(bstat_ref)

        gstat_ref[0:1, :] += jnp.sum(dy * xn, axis=0, keepdims=True)
        tile_loss = jnp.sum(jnp.sum(err * err, axis=1, keepdims=True), axis=0, keepdims=True) * (0.5 / D)
        gstat_ref[1:2, :] += jnp.broadcast_to(tile_loss, (1, D))
        bstat_ref[0, 0:1, :] += jnp.sum(dx2 * f, axis=0, keepdims=True)

    return pl.pallas_call(
        body, name="ffn_down", grid=(T // TOKEN_TILE,),
        in_specs=[_tok_spec(F), prev, nxt, _tok_spec(F), _full(conv_w.shape), _full((1, F)), _full(w_down.shape),
                  _tok_spec(D), _mod_spec(tps, D), _full((1, D)), _tok_spec(D)],
        out_specs=[_tok_spec(F), _tok_spec(F), _tok_spec(F), _tok_spec(D), _tok_spec(D), _full((8, D)), _bstat_spec(tps, D)],
        out_shape=[jax.ShapeDtypeStruct((T, F), BF16), jax.ShapeDtypeStruct((T, F), BF16), jax.ShapeDtypeStruct((T, F), BF16),
                   jax.ShapeDtypeStruct((T, D), F32), jax.ShapeDtypeStruct((T, D), BF16),
                   jax.ShapeDtypeStruct((8, D), F32), jax.ShapeDtypeStruct((B, 8, D), F32)],
        compiler_params=_params("arbitrary"),
    )(gt, gt, gt, val, conv_w, conv_b, w_down, x1, mod3, g_final, target)


def _ffn_down_bwd(df, w_down, act, vd, tasks=(), tm=WIDE_TILE):
    T, D = df.shape
    F = act.shape[1]

    def body(df_ref, w_ref, act_ref, vd_ref, dval_ref, dgc_ref, cstat_ref):
        da = _nt(df_ref[...], w_ref[...])
        dval_ref[...] = (da * act_ref[...].astype(F32)).astype(BF16)
        dgc = da * vd_ref[...].astype(F32)
        dgc_ref[...] = dgc.astype(BF16)

        @pl.when(pl.program_id(0) == 0)
        def _():
            cstat_ref[...] = jnp.zeros_like(cstat_ref)

        cstat_ref[0:1, :] += jnp.sum(dgc, axis=0, keepdims=True)

    return _hosted_call(
        body, "ffn_down_bwd", (T // tm,),
        [_tok_spec(D, tm), _full(w_down.shape), _tok_spec(F, tm), _tok_spec(F, tm)],
        [_tok_spec(F, tm), _tok_spec(F, tm), _full((8, F))],
        [jax.ShapeDtypeStruct((T, F), BF16), jax.ShapeDtypeStruct((T, F), BF16), jax.ShapeDtypeStruct((8, F), F32)],
        (df, w_down, act, vd), tasks)


def _ffn_up_bwd(dgc, dval, gt, conv_w, w_up, x1, mod3, g_ffn, dx2, mix, B, S, tasks=()):
    T, D = x1.shape
    F = dgc.shape[1]
    tps = S // TOKEN_TILE
    prev, nxt = _halo_specs(T, tps, F)

    def body(dgc_ref, prev_ref, next_ref, dval_ref, gt_ref, cw_ref, w_ref, x1_ref, mod_ref, g_ref, dx2_ref, mix_ref,
             du_ref, dx1_ref, dmix_ref, gstat_ref, bstat_ref, cstat_ref):
        i = pl.program_id(0)
        d = dgc_ref[...].astype(F32)
        dprev, dnext = _seq_shifts(d, prev_ref[7:8, :], next_ref[0:1, :], i % tps, tps)
        g = gt_ref[...].astype(F32)

        @pl.when(i == 0)
        def _():
            cstat_ref[...] = jnp.zeros_like(cstat_ref)

        cstat_ref[1:2, :] += jnp.sum(dnext * g, axis=0, keepdims=True)
        cstat_ref[2:3, :] += jnp.sum(d * g, axis=0, keepdims=True)
        cstat_ref[3:4, :] += jnp.sum(dprev * g, axis=0, keepdims=True)
        dgt = dnext * cw_ref[0:1, :] + d * cw_ref[1:2, :] + dprev * cw_ref[2:3, :]
        du = jnp.concatenate([dval_ref[...], dgt.astype(BF16)], axis=1)
        du_ref[...] = du
        dh2 = _nn(du, w_ref[...])
        xn, r = _rms(x1_ref[...])
        scale1 = 1.0 + mod_ref[0, 4:5, :]
        xg = xn * g_ref[...]
        dx1 = dx2_ref[...] + _rms_bwd(xn, r, dh2 * g_ref[...] * scale1)
        dx1_ref[...] = dx1
        dmix_ref[...] = (mod_ref[0, 2:3, :] * dx1).astype(BF16)

        @pl.when(i == 0)
        def _():
            gstat_ref[...] = jnp.zeros_like(gstat_ref)

        @pl.when(i % tps == 0)
        def _():
            bstat_ref[...] = jnp.zeros_like(bstat_ref)

        gstat_ref[0:1, :] += jnp.sum(dh2 * scale1 * xn, axis=0, keepdims=True)
        bstat_ref[0, 0:1, :] += jnp.sum(dh2, axis=0, keepdims=True)
        bstat_ref[0, 1:2, :] += jnp.sum(dh2 * xg, axis=0, keepdims=True)
        bstat_ref[0, 2:3, :] += jnp.sum(dx1 * mix_ref[...], axis=0, keepdims=True)

    return _hosted_call(
        body, "ffn_up_bwd", (T // TOKEN_TILE,),
        [_tok_spec(F), prev, nxt, _tok_spec(F), _tok_spec(F), _full(conv_w.shape), _full(w_up.shape), _tok_spec(D),
         _mod_spec(tps, D), _full((1, D)), _tok_spec(D), _tok_spec(D)],
        [_tok_spec(2 * F), _tok_spec(D), _tok_spec(D), _full((8, D)), _bstat_spec(tps, D), _full((8, F))],
        [jax.ShapeDtypeStruct((T, 2 * F), BF16), jax.ShapeDtypeStruct((T, D), F32), jax.ShapeDtypeStruct((T, D), BF16),
         jax.ShapeDtypeStruct((8, D), F32), jax.ShapeDtypeStruct((B, 8, D), F32), jax.ShapeDtypeStruct((8, F), F32)],
        (dgc, dgc, dgc, dval, gt, conv_w, w_up, x1, mod3, g_ffn, dx2, mix), tasks)


def _attn_out_bwd(dmix, w_out, oa, ob, g_na, g_sw, tasks=(), tm=WIDE_TILE):
    T, D = dmix.shape

    def body(dmix_ref, w_ref, oa_ref, ob_ref, gna_ref, gsw_ref, doa_ref, dob_ref, gstat_ref):
        dmixin = _nt(dmix_ref[...], w_ref[...])

        @pl.when(pl.program_id(0) == 0)
        def _():
            gstat_ref[...] = jnp.zeros_like(gstat_ref)

        for k, (o_ref, g_ref, do_ref) in enumerate(((oa_ref, gna_ref, doa_ref), (ob_ref, gsw_ref, dob_ref))):
            dn = dmixin[:, k * NA_WIDTH:(k + 1) * NA_WIDTH]
            on, r = _rms(o_ref[...])
            gstat_ref[k:k + 1, :] += jnp.sum(dn * on, axis=0, keepdims=True)
            do_ref[...] = _rms_bwd(on, r, dn * g_ref[...]).astype(BF16)

    hs = jax.ShapeDtypeStruct((T, NA_WIDTH), BF16)
    return _hosted_call(
        body, "attn_out_bwd", (T // tm,),
        [_tok_spec(D, tm), _full(w_out.shape), _tok_spec(NA_WIDTH, tm), _tok_spec(SW_WIDTH, tm), _full((1, NA_WIDTH)), _full((1, SW_WIDTH))],
        [_tok_spec(NA_WIDTH, tm), _tok_spec(SW_WIDTH, tm), _full((8, NA_WIDTH))],
        [hs, hs, jax.ShapeDtypeStruct((8, NA_WIDTH), F32)],
        (dmix, w_out, oa, ob, g_na, g_sw), tasks)


def _attn_in_bwd(dqa, dka, dva, dqb, dkb, dvb, cos_t, sin_t, w_in, x2d, mod3, g_attn, dx1, B, S, tm=WIDE_TILE):
    T, D = x2d.shape
    tps = S // tm

    def body(dqa_ref, dka_ref, dva_ref, dqb_ref, dkb_ref, dvb_ref, cos_ref, sin_ref, w_ref, x_ref, mod_ref, g_ref, dx1_ref,
             gx_ref, dproj_ref, gstat_ref, bstat_ref):
        i = pl.program_id(0)
        drb = jnp.concatenate([dqb_ref[...] * Q_SCALE, dkb_ref[...]], axis=1).astype(F32)
        reps = (1, ROPE_WIDTH // (2 * HEAD_DIM))
        drb = drb * jnp.tile(cos_ref[...], reps) + _rot_half(drb * jnp.tile(sin_ref[...], reps))
        dproj = jnp.concatenate([(dqa_ref[...] * Q_SCALE).astype(BF16), dka_ref[...].astype(BF16), dva_ref[...].astype(BF16),
                                 drb.astype(BF16), dvb_ref[...].astype(BF16)], axis=1)
        dproj_ref[...] = dproj
        dh = _nn(dproj, w_ref[...])
        xn, r = _rms(x_ref[...])
        scale1 = 1.0 + mod_ref[0, 1:2, :]
        gx_ref[...] = dx1_ref[...] + _rms_bwd(xn, r, dh * g_ref[...] * scale1)

        @pl.when(i == 0)
        def _():
            gstat_ref[...] = jnp.zeros_like(gstat_ref)

        @pl.when(i % tps == 0)
        def _():
            bstat_ref[...] = jnp.zeros_like(bstat_ref)

        gstat_ref[0:1, :] += jnp.sum(dh * scale1 * xn, axis=0, keepdims=True)
        bstat_ref[0, 0:1, :] += jnp.sum(dh, axis=0, keepdims=True)
        bstat_ref[0, 1:2, :] += jnp.sum(dh * (xn * g_ref[...]), axis=0, keepdims=True)

    rope = _rope_spec(tps, tm)
    return pl.pallas_call(
        body, name="attn_in_bwd", grid=(T // tm,),
        in_specs=[_tok_spec(NA_WIDTH, tm), _tok_spec(NA_WIDTH, tm), _tok_spec(NA_WIDTH, tm), _tok_spec(SW_WIDTH, tm),
                  _tok_spec(SW_KV_WIDTH, tm), _tok_spec(SW_KV_WIDTH, tm), rope, rope, _full(w_in.shape), _tok_spec(D, tm),
                  _mod_spec(tps, D), _full((1, D)), _tok_spec(D, tm)],
        out_specs=[_tok_spec(D, tm), _tok_spec(IN_WIDTH, tm), _full((8, D)), _bstat_spec(tps, D)],
        out_shape=[jax.ShapeDtypeStruct((T, D), F32), jax.ShapeDtypeStruct((T, IN_WIDTH), BF16),
                   jax.ShapeDtypeStruct((8, D), F32), jax.ShapeDtypeStruct((B, 8, D), F32)],
        compiler_params=_params("arbitrary"),
    )(dqa, dka, dva, dqb, dkb, dvb, cos_t, sin_t, w_in, x2d, mod3, g_attn, dx1)


def _matmul_tn(a, b, name, tm=None, tk=512, tasks=()):
    T, M = a.shape
    N = b.shape[1]
    tm = M if tm is None else tm
    nk = T // tk

    def body(a_ref, b_ref, o_ref, acc):
        k = pl.program_id(1)

        @pl.when(k == 0)
        def _():
            acc[...] = jnp.zeros_like(acc)

        acc[...] += _tn(a_ref[...], b_ref[...])

        @pl.when(k == nk - 1)
        def _():
            o_ref[...] = acc[...].astype(BF16)

    (out,), got = _hosted_call(
        body, name, (M // tm, nk),
        [pl.BlockSpec((tk, tm), lambda i, k: (k, i)), pl.BlockSpec((tk, N), lambda i, k: (k, 0))],
        [pl.BlockSpec((tm, N), lambda i, k: (i, 0))], [jax.ShapeDtypeStruct((M, N), BF16)], (a, b), tasks,
        scratch_shapes=[pltpu.VMEM((tm, N), F32)])
    return out, got


def _na_geometry(S):
    rows = S // GRID_W
    wr = min(NA_ROWS_MAX, rows)
    return rows, wr


def _na_window(r, rows, wr):
    rs = jnp.clip(r - wr // 2, 0, rows - wr)
    return pl.multiple_of(rs * GRID_W, GRID_W), pl.multiple_of((rs - r + NA_ROWS_MAX - 1) * GRID_W, GRID_W)


NA_STEP_PAIRS = 2
NA_GW = NA_STEP_PAIRS * 128
NA_BWD_ROWS = 4
NA_ROWS_PER_STEP = 4


def _na_specs(S, kw_n, order):
    ng = NA_PAIRS // NA_STEP_PAIRS

    def col(k):
        return pl.BlockSpec((1, S, NA_GW), lambda *ids: (order(*ids)[0], 0, k * ng + order(*ids)[1]))
    bias = pl.BlockSpec((NA_STEP_PAIRS, N_DR * GRID_W, 128), lambda *ids: (order(*ids)[1], 0, 0))
    out = pl.BlockSpec((1, S, NA_GW), lambda *ids: (order(*ids)[0], 0, order(*ids)[1]))
    return col(0), col(1), col(2), bias, out


def _block_diag(t):
    left = lax.broadcasted_iota(jnp.int32, t.shape, 1) < HEAD_DIM
    zero = jnp.zeros_like(t)
    return jnp.concatenate([jnp.where(left, t, zero), jnp.where(left, zero, t)], axis=0)


def _diag_blocks(res):
    left = lax.broadcasted_iota(jnp.int32, (HEAD_DIM, 128), 1) < HEAD_DIM
    return jnp.where(left, res[:HEAD_DIM], res[HEAD_DIM:])


def _col_softmax(st):
    e = jnp.exp(st - jnp.max(st, axis=0, keepdims=True))
    return e * (1.0 / jnp.sum(e, axis=0, keepdims=True))


def _na_fwd(qkv, bias, tasks=()):
    B, S, _ = qkv.shape
    rows, wr = _na_geometry(S)
    kw_n = wr * GRID_W

    def body(q_ref, k_ref, v_ref, b_ref, o_ref):
        def step(it, carry):
            win = [_na_window(it * NA_ROWS_PER_STEP + u, rows, wr) for u in range(NA_ROWS_PER_STEP)]
            qrows = [pl.ds(pl.multiple_of((it * NA_ROWS_PER_STEP + u) * GRID_W, GRID_W), GRID_W) for u in range(NA_ROWS_PER_STEP)]
            krows = [pl.ds(w[0], kw_n) for w in win]
            brows = [pl.ds(w[1], kw_n) for w in win]
            lanes = [pl.ds(p * 128, 128) for p in range(NA_STEP_PAIRS)]
            chains = [(u, p) for u in range(NA_ROWS_PER_STEP) for p in range(NA_STEP_PAIRS)]
            st = {(u, p): _nt(k_ref[0, krows[u], lanes[p]], _block_diag(q_ref[0, qrows[u], lanes[p]])) for u, p in chains}
            pn = {(u, p): _col_softmax(st[(u, p)] + b_ref[p, brows[u], :]).astype(BF16) for u, p in chains}
            out = {(u, p): _diag_blocks(_tn(pn[(u, p)], v_ref[0, krows[u], lanes[p]])) for u, p in chains}
            for u in range(NA_ROWS_PER_STEP):
                o_ref[0, qrows[u], :] = jnp.concatenate([out[(u, p)] for p in range(NA_STEP_PAIRS)], axis=1)
            return carry

        lax.fori_loop(0, rows // NA_ROWS_PER_STEP, step, 0)

    q, k, v, bs, out = _na_specs(S, kw_n, lambda b, g: (b, g))
    return _hosted_call(body, "na_fwd", (B, NA_PAIRS // NA_STEP_PAIRS), [q, k, v, bs], [out],
                        [jax.ShapeDtypeStruct((B, S, NA_WIDTH), F32)], (qkv, qkv, qkv, bias), tasks)


def _na_bwd(qkv, bias, doa, tasks=()):
    B, S, _ = qkv.shape
    rows, wr = _na_geometry(S)
    kw_n = wr * GRID_W

    def body(q_ref, k_ref, v_ref, b_ref, do_ref, dq_ref, dk_ref, dv_ref, db_ref, dk_acc, dv_acc):
        @pl.when(pl.program_id(1) == 0)
        def _():
            db_ref[...] = jnp.zeros_like(db_ref)

        dk_acc[...] = jnp.zeros_like(dk_acc)
        dv_acc[...] = jnp.zeros_like(dv_acc)

        def step(it, carry):
            nu, pairs = range(NA_BWD_ROWS), range(NA_STEP_PAIRS)
            win = [_na_window(it * NA_BWD_ROWS + u, rows, wr) for u in nu]
            qrows = [pl.ds(pl.multiple_of((it * NA_BWD_ROWS + u) * GRID_W, GRID_W), GRID_W) for u in nu]
            krows = [pl.ds(w[0], kw_n) for w in win]
            brows = [pl.ds(w[1], kw_n) for w in win]
            lanes = [pl.ds(p * 128, 128) for p in pairs]
            chains = [(u, p) for u in nu for p in pairs]
            kp = {(u, p): k_ref[0, krows[u], lanes[p]] for u, p in chains}
            qbd = {(u, p): _block_diag(q_ref[0, qrows[u], lanes[p]]) for u, p in chains}
            dobd = {(u, p): _block_diag(do_ref[0, qrows[u], lanes[p]]) for u, p in chains}
            st = {c: _nt(kp[c], qbd[c]) for c in chains}
            dpt = {(u, p): _nt(v_ref[0, krows[u], lanes[p]], dobd[(u, p)]) for u, p in chains}
            pn = {(u, p): _col_softmax(st[(u, p)] + b_ref[p, brows[u], :]) for u, p in chains}
            dst = {c: pn[c] * (dpt[c] - jnp.sum(pn[c] * dpt[c], axis=0, keepdims=True)) for c in chains}
            dsb = {c: dst[c].astype(BF16) for c in chains}
            dq = {c: _diag_blocks(_tn(dsb[c], kp[c])) for c in chains}
            dk = {c: _nn(dsb[c], qbd[c]) for c in chains}
            dv = {c: _nn(pn[c].astype(BF16), dobd[c]) for c in chains}
            for u in nu:
                dq_ref[0, qrows[u], :] = jnp.concatenate([dq[(u, p)] for p in pairs], axis=1).astype(BF16)
                dk_acc[krows[u], :] += jnp.concatenate([dk[(u, p)] for p in pairs], axis=1)
                dv_acc[krows[u], :] += jnp.concatenate([dv[(u, p)] for p in pairs], axis=1)
                for p in pairs:
                    db_ref[p, brows[u], :] += dst[(u, p)]
            return carry

        lax.fori_loop(0, rows // NA_BWD_ROWS, step, 0)

        def emit(i, carry):
            r = pl.ds(pl.multiple_of(i * 256, 256), 256)
            dk_ref[0, r, :] = dk_acc[r, :].astype(BF16)
            dv_ref[0, r, :] = dv_acc[r, :].astype(BF16)
            return carry

        lax.fori_loop(0, S // 256, emit, 0)

    q, k, v, bs, out = _na_specs(S, kw_n, lambda g, b: (b, g))
    hs = jax.ShapeDtypeStruct((B, S, NA_WIDTH), BF16)
    return _hosted_call(body, "na_bwd", (NA_PAIRS // NA_STEP_PAIRS, B), [q, k, v, bs, out], [out, out, out, bs],
                        [hs, hs, hs, jax.ShapeDtypeStruct((NA_PAIRS, N_DR * GRID_W, 128), F32)], (qkv, qkv, qkv, bias, doa), tasks,
                        scratch_shapes=[pltpu.VMEM((S, NA_GW), F32), pltpu.VMEM((S, NA_GW), F32)])


SW_PAIRS = SW_HEADS // 2


def _sw_band(n, S):
    kw_n = 3 * SW_BLOCK
    start = pl.multiple_of(jnp.clip(n * SW_BLOCK - SW_BLOCK, 0, S - kw_n), SW_BLOCK)
    kpos = start + lax.broadcasted_iota(jnp.int32, (kw_n, SW_BLOCK), 0)
    qpos = n * SW_BLOCK + lax.broadcasted_iota(jnp.int32, (kw_n, SW_BLOCK), 1)
    return start, jnp.abs(qpos - kpos) <= SW_WINDOW


def _kv_halves(t):
    left = lax.broadcasted_iota(jnp.int32, t.shape, 1) < HEAD_DIM
    swapped = pltpu.roll(t, HEAD_DIM, axis=1)
    zero = jnp.zeros_like(t)
    return {(0, 0): jnp.where(left, t, zero), (0, 1): jnp.where(left, zero, swapped),
            (1, 0): jnp.where(left, swapped, zero), (1, 1): jnp.where(left, zero, t)}


def _sw_probs(st, ok, sk):
    st = jnp.where(ok, st, NEG)
    m = jnp.maximum(jnp.max(st, axis=0, keepdims=True), sk)
    e = jnp.exp(st - m)
    esk = jnp.exp(sk - m)
    inv = 1.0 / (jnp.sum(e, axis=0, keepdims=True) + esk)
    return e * inv, esk * inv


def _sw_specs(S):
    q = pl.BlockSpec((1, S, SW_WIDTH), lambda b, *_: (b, 0, ROPE_LO // SW_WIDTH))
    k = pl.BlockSpec((1, S, SW_KV_WIDTH), lambda b, *_: (b, 0, (ROPE_LO + SW_WIDTH) // SW_KV_WIDTH))
    v = pl.BlockSpec((1, S, SW_KV_WIDTH), lambda b, *_: (b, 0, (ROPE_LO + ROPE_WIDTH) // SW_KV_WIDTH))
    return q, k, v


SW_FWD_SPLIT = 2


def _sw_fwd(sink, qkv, tasks=()):
    B, S, _ = qkv.shape
    kw_n = 3 * SW_BLOCK

    def body(sink_ref, q_ref, k_ref, v_ref, o_ref):
        def step(n, carry):
            start, ok = _sw_band(n, S)
            qrows = pl.ds(pl.multiple_of(n * SW_BLOCK, SW_BLOCK), SW_BLOCK)
            krows = pl.ds(start, kw_n)
            kh, vh = _kv_halves(k_ref[0, krows, :]), _kv_halves(v_ref[0, krows, :])
            heads = [(p, e) for p in range(SW_PAIRS) for e in range(2)]
            qp = [q_ref[0, qrows, pl.ds(p * 128, 128)] for p in range(SW_PAIRS)]
            kv_of = lambda p: p // (SW_PAIRS // SW_KV_HEADS)
            st = {(p, e): _nt(kh[(kv_of(p), e)], qp[p]) for p, e in heads}
            pn = {(p, e): _sw_probs(st[(p, e)], ok, sink_ref[2 * p + e])[0].astype(BF16) for p, e in heads}
            outs = [_tn(pn[(p, 0)], vh[(kv_of(p), 0)]) + _tn(pn[(p, 1)], vh[(kv_of(p), 1)]) for p in range(SW_PAIRS)]
            o_ref[0, qrows, :] = jnp.concatenate(outs, axis=1)
            return carry

        half = (S // SW_BLOCK) // SW_FWD_SPLIT
        lax.fori_loop(pl.program_id(1) * half, (pl.program_id(1) + 1) * half, step, 0)

    q, k, v = _sw_specs(S)
    return _hosted_call(
        body, "sw_fwd", (B, SW_FWD_SPLIT), [pl.BlockSpec(memory_space=pltpu.SMEM), q, k, v],
        [pl.BlockSpec((1, S, SW_WIDTH), lambda b, s: (b, 0, 0))], [jax.ShapeDtypeStruct((B, S, SW_WIDTH), F32)],
        (sink, qkv, qkv, qkv), tasks)


def _sw_bwd(sink, qkv, dob):
    B, S, _ = qkv.shape
    kw_n = 3 * SW_BLOCK

    fold_rows = 256

    def body(sink_ref, q_ref, k_ref, v_ref, do_ref, dq_ref, dk_ref, dv_ref, dsink_ref, dk_acc, dv_acc):
        @pl.when(pl.program_id(0) == 0)
        def _():
            dsink_ref[...] = jnp.zeros_like(dsink_ref)

        dk_acc[...] = jnp.zeros_like(dk_acc)
        dv_acc[...] = jnp.zeros_like(dv_acc)
        ppk = SW_PAIRS // SW_KV_HEADS

        def step(n, carry):
            start, ok = _sw_band(n, S)
            qrows = pl.ds(pl.multiple_of(n * SW_BLOCK, SW_BLOCK), SW_BLOCK)
            krows = pl.ds(start, kw_n)
            kh, vh = _kv_halves(k_ref[0, krows, :]), _kv_halves(v_ref[0, krows, :])
            heads = [(p, e) for p in range(SW_PAIRS) for e in range(2)]
            qp = [q_ref[0, qrows, pl.ds(p * 128, 128)] for p in range(SW_PAIRS)]
            dop = [do_ref[0, qrows, pl.ds(p * 128, 128)] for p in range(SW_PAIRS)]
            st = {(p, e): _nt(kh[(p // ppk, e)], qp[p]) for p, e in heads}
            dpt = {(p, e): _nt(vh[(p // ppk, e)], dop[p]) for p, e in heads}
            pnb, dsb = {}, {}
            for p, e in heads:
                pn, psink = _sw_probs(st[(p, e)], ok, sink_ref[2 * p + e])
                delta = jnp.sum(pn * dpt[(p, e)], axis=0, keepdims=True)
                dsb[(p, e)] = (pn * (dpt[(p, e)] - delta)).astype(BF16)
                pnb[(p, e)] = pn.astype(BF16)
                dsink_ref[2 * p + e:2 * p + e + 1, :] += -(psink * delta)
            dq_ref[0, qrows, :] = jnp.concatenate(
                [_tn(dsb[(p, 0)], kh[(p // ppk, 0)]) + _tn(dsb[(p, 1)], kh[(p // ppk, 1)]) for p in range(SW_PAIRS)],
                axis=1).astype(BF16)
            left = lax.broadcasted_iota(jnp.int32, (kw_n, 128), 1) < HEAD_DIM
            dks, dvs = [], []
            for kv in range(SW_KV_HEADS):
                dk = dv = None
                for p in range(kv * ppk, (kv + 1) * ppk):
                    dk_p = jnp.where(left, _nn(dsb[(p, 0)], qp[p]), _nn(dsb[(p, 1)], qp[p]))
                    dv_p = jnp.where(left, _nn(pnb[(p, 0)], dop[p]), _nn(pnb[(p, 1)], dop[p]))
                    dk = dk_p if dk is None else dk + dk_p
                    dv = dv_p if dv is None else dv + dv_p
                dks.append(dk)
                dvs.append(dv)
            dk_acc[krows, :] += jnp.concatenate(dks, axis=1)
            dv_acc[krows, :] += jnp.concatenate(dvs, axis=1)
            return carry

        lax.fori_loop(0, S // SW_BLOCK, step, 0)

        def fold(i, carry):
            rows = pl.ds(pl.multiple_of(i * fold_rows, fold_rows), fold_rows)
            left = lax.broadcasted_iota(jnp.int32, (fold_rows, 128), 1) < HEAD_DIM
            for acc, out_ref in ((dk_acc, dk_ref), (dv_acc, dv_ref)):
                a, b = acc[rows, 0:128], acc[rows, 128:256]
                out_ref[0, rows, :] = jnp.where(left, a + pltpu.roll(a, HEAD_DIM, axis=1),
                                                b + pltpu.roll(b, HEAD_DIM, axis=1)).astype(BF16)
            return carry

        lax.fori_loop(0, S // fold_rows, fold, 0)

        @pl.when(pl.program_id(0) == B - 1)
        def _():
            dsink_ref[...] = jnp.broadcast_to(jnp.sum(dsink_ref[...], axis=1, keepdims=True), dsink_ref.shape)

    q, k, v = _sw_specs(S)
    qo = pl.BlockSpec((1, S, SW_WIDTH), lambda b: (b, 0, 0))
    ko = pl.BlockSpec((1, S, SW_KV_WIDTH), lambda b: (b, 0, 0))
    return pl.pallas_call(
        body, name="sw_bwd", grid=(B,),
        in_specs=[pl.BlockSpec(memory_space=pltpu.SMEM), q, k, v, qo],
        out_specs=[qo, ko, ko, _full((SW_HEADS, 128))],
        out_shape=[jax.ShapeDtypeStruct((B, S, SW_WIDTH), BF16), jax.ShapeDtypeStruct((B, S, SW_KV_WIDTH), BF16),
                   jax.ShapeDtypeStruct((B, S, SW_KV_WIDTH), BF16), jax.ShapeDtypeStruct((SW_HEADS, 128), F32)],
        scratch_shapes=[pltpu.VMEM((S, 2 * SW_KV_WIDTH), F32), pltpu.VMEM((S, 2 * SW_KV_WIDTH), F32)],
        compiler_params=_params("arbitrary"),
    )(sink, qkv, qkv, qkv, dob)


def _pack_sum_adamw(packs, params, pick):
    W = packs.shape[1]
    n_p = len(params)

    def body(p_ref, *refs):
        ins, tot_ref, pick_ref, outs = refs[:3 * n_p], refs[3 * n_p], refs[3 * n_p + 1], refs[3 * n_p + 2:]
        tot = p_ref[0:8, :]
        for d in range(1, N_DEV):
            tot = tot + p_ref[8 * d:8 * d + 8, :]
        tot_ref[...] = tot
        pick_ref[...] = tot[pick[0]:pick[0] + 1, pick[1]:pick[1] + 1]
        for i, (w, _, _, rows, off) in enumerate(params):
            w_ref, m_ref, v_ref = ins[3 * i:3 * i + 3]
            g_ref, d_ref, nm_ref, nv_ref = outs[4 * i:4 * i + 4]
            if w.ndim == 3:
                n_a, n_b, n = w.shape
                for a in range(n_a):
                    for b in range(n_b):
                        o = off + (b * n_a + a) * n
                        g_ref[a, b:b + 1, :] = tot_ref[rows[0]:rows[0] + 1, o:o + n]
                g = g_ref[...]
            else:
                n = w.shape[1]
                g = tot[rows[0]:rows[0] + 1, off:off + n]
                for r in rows[1:]:
                    g = g + tot[r:r + 1, off:off + n]
                g_ref[...] = g
            d_ref[...], nm_ref[...], nv_ref[...] = _adam_update(w_ref[...], g, m_ref[...], v_ref[...])

    res = pl.pallas_call(
        body, name="small_adamw",
        out_shape=[jax.ShapeDtypeStruct((8, W), F32), jax.ShapeDtypeStruct((1, 1), F32)]
        + [jax.ShapeDtypeStruct(p[0].shape, F32) for p in params for _ in range(4)],
        compiler_params=pltpu.CompilerParams(vmem_limit_bytes=VMEM_LIMIT),
    )(packs, *[a for p in params for a in p[:3]])
    return res[0], res[1], [res[2 + 4 * i:6 + 4 * i] for i in range(n_p)]


def _adam_update(w, g, m, v):
    c1 = 1.0 - ADAM_B1 ** ADAM_STEP
    c2 = 1.0 - ADAM_B2 ** ADAM_STEP
    nm = ADAM_B1 * m + (1.0 - ADAM_B1) * g
    nv = ADAM_B2 * v + (1.0 - ADAM_B2) * (g * g)
    return -ADAM_LR * ((nm / c1) / (jnp.sqrt(nv / c2) + ADAM_EPS) + ADAM_WD * w), nm, nv


def _adamw(w, g, m, v, name):
    def body(w_ref, g_ref, m_ref, v_ref, d_ref, nm_ref, nv_ref):
        d_ref[...], nm_ref[...], nv_ref[...] = _adam_update(w_ref[...], g_ref[...], m_ref[...], v_ref[...])

    s = jax.ShapeDtypeStruct(w.shape, F32)
    return pl.pallas_call(body, name=name, out_shape=[s, s, s],
                          compiler_params=pltpu.CompilerParams(vmem_limit_bytes=VMEM_LIMIT))(w, g, m, v)


def _sum_adamw_rows(R):
    return max(r for r in range(16, min(R, 256) + 1, 16) if R % r == 0)


def _sum_adamw_steps(R):
    return R // _sum_adamw_rows(R)


def _sum_first(own, recvb, name, tasks):
    R, C = own.shape
    rc = _sum_adamw_rows(R)

    def body(own_ref, r_ref, p_ref):
        p_ref[...] = (own_ref[...] + r_ref[0].astype(F32)) + r_ref[1].astype(F32)

    blk = pl.BlockSpec((rc, C), lambda i: (i, 0))
    (part,), got = _hosted_call(body, name, (R // rc,), [blk, pl.BlockSpec((2, rc, C), lambda i: (0, i, 0))], [blk],
                                [jax.ShapeDtypeStruct((R, C), F32)], (own, recvb), tasks)
    return part, got


def _sum_adamw(own, recvb, w, m, v, name, done=0):
    R, C = own.shape
    rc = _sum_adamw_rows(R)
    left = 3 - done
    assert 3 % left == 0

    def body(own_ref, r_ref, w_ref, m_ref, v_ref, g_ref, d_ref, nm_ref, nv_ref):
        g = own_ref[...]
        for j in range(left):
            g = g + r_ref[j].astype(F32)
        g_ref[...] = g
        d_ref[...], nm_ref[...], nv_ref[...] = _adam_update(w_ref[...], g, m_ref[...], v_ref[...])

    blk = pl.BlockSpec((rc, C), lambda i: (i, 0))
    s = jax.ShapeDtypeStruct((R, C), F32)
    return pl.pallas_call(
        body, name=name, grid=(R // rc,),
        in_specs=[blk, pl.BlockSpec((left, rc, C), lambda i: (done // left, i, 0)), blk, blk, blk],
        out_specs=[blk, blk, blk, blk], out_shape=[s, s, s, s], compiler_params=_params("parallel"),
    )(own, recvb, w, m, v)


def _by_device(dw):
    return dw.reshape(N_DEV, dw.shape[0] // N_DEV, dw.shape[1])


def _local_step(x, mod, g_attn, w_in, bias, sw_sink, g_na_out, g_sw_out, w_out, g_ffn, w_up, conv_w, conv_b, w_down,
                g_final, target, sharded):
    B, S, D = x.shape
    T = B * S
    x2d = x.reshape(T, D)
    mod3 = mod.reshape(B, 6, D)
    cos_t, sin_t = _rope_tables(S)
    sink = sw_sink.reshape(SW_HEADS)
    n_tiles = T // WIDE_TILE
    full = lambda g: g.reshape(N_DEV * g.shape[1], g.shape[2])

    rider = lambda w, mid, lo, n, into=None: [_gather_task(w, mid, rows=(lo, n), into=into)] if sharded else []
    if sharded:
        qu, hd = w_up.shape[0] // 4, w_down.shape[0] // 2
    (h, qkv), got = _attn_in(x2d, mod3, g_attn, w_in, cos_t, sin_t, S, rider(w_up, 3 * n_tiles // 4, 0, qu) if sharded else [])
    if sharded:
        w_up_buf = got[0][0]
    qkv3 = qkv.reshape(B, S, IN_WIDTH)
    na_steps, sw_steps = B * (NA_PAIRS // NA_STEP_PAIRS), B * SW_FWD_SPLIT
    (oa,), got = _na_fwd(qkv3, bias, rider(w_up, na_steps - 1, qu, 2 * qu, w_up_buf) if sharded else [])
    if sharded:
        w_up_buf = got[0][0]
    oa = oa.reshape(T, NA_WIDTH)
    (ob,), got = _sw_fwd(sink, qkv3,
                         rider(w_up, sw_steps // 2, 3 * qu, qu, w_up_buf) + [_gather_task(w_out, sw_steps - 1)] if sharded else [])
    if sharded:
        w_up, w_out = full(got[0][0]), full(got[1][0])
    ob = ob.reshape(T, SW_WIDTH)
    (mixin, mix, x1), _ = _attn_out(oa, ob, x2d, mod3, g_na_out, g_sw_out, w_out, S)
    (h2, val, gt), got = _ffn_up(x1, mod3, g_ffn, w_up, S, rider(w_down, 3 * n_tiles // 4, 0, 2 * hd) if sharded else [])
    if sharded:
        w_down = full(got[0][0])
    a, act, vd, dx2, df, gstat_f, bstat_f = _ffn_down(gt, val, conv_w, conv_b, w_down, x1, mod3, g_final, target.reshape(T, D), B, S)
    F = val.shape[1]

    dw_down, _ = _matmul_tn(a, df, "dw_down")
    (dval, dgc, cstat), got = _ffn_down_bwd(df, w_down, act, vd, [_swap_task(_by_device(dw_down))] if sharded else [])
    if sharded:
        send_down, own_down = _chip_sums(_by_device(dw_down), got[0][0])
    (du, dx1, dmix, gstat_u, bstat_u, cstat_w), got = _ffn_up_bwd(dgc, dval, gt, conv_w, w_up, x1, mod3, g_ffn, dx2, mix, B, S,
                                                                  [_exchange_task(send_down)] if sharded else [])
    if sharded:
        dw_down = (own_down, got[0][0])
    dw_up, _ = _matmul_tn(du, h2, "dw_up", tm=F)
    dw_out, got = _matmul_tn(mixin, dmix, "dw_out", tasks=[_swap_task(_by_device(dw_up), chips=(0, 1))] if sharded else [])
    (doa, dob, gstat_o), got = _attn_out_bwd(dmix, w_out, oa, ob, g_na_out, g_sw_out,
                                             [_swap_task(_by_device(dw_up), chips=(2, 3), into=got[0][0]),
                                              _swap_task(_by_device(dw_out))] if sharded else [])
    if sharded:
        send_up, own_up = _chip_sums(_by_device(dw_up), got[0][0])
        send_out, own_out = _chip_sums(_by_device(dw_out), got[1][0])
    (dqa, dka, dva, dbt), got = _na_bwd(qkv3, bias, doa.reshape(B, S, NA_WIDTH),
                                        [_exchange_task(send_up), _exchange_task(send_out)] if sharded else [])
    if sharded:
        dw_up, dw_out = (own_up, got[0][0]), (own_out, got[1][0])
    dqb, dkb, dvb, dsink = _sw_bwd(sink, qkv3, dob.reshape(B, S, SW_WIDTH))
    r2 = lambda t: t.reshape(T, t.shape[-1])
    grad_x, dproj, gstat_i, bstat_i = _attn_in_bwd(r2(dqa), r2(dka), r2(dva), r2(dqb), r2(dkb), r2(dvb), cos_t, sin_t, w_in, x2d, mod3,
                                                   g_attn, dx1, B, S)
    dw_in, _ = _matmul_tn(dproj, h, "dw_in")

    dmod = jnp.stack([bstat_i[:, 0], bstat_i[:, 1], bstat_u[:, 2], bstat_u[:, 0], bstat_u[:, 1], bstat_f[:, 0]], axis=1)
    small = dict(g_attn=gstat_i[0], g_ffn=gstat_u[0], g_final=gstat_f[0], loss=gstat_f[1, 0], g_na_out=gstat_o[0], g_sw_out=gstat_o[1],
                 sw_sink=dsink[:, 0], conv_b=cstat[0], conv_w=cstat_w[1:4], dbt=dbt,
                 raw=(bstat_i, bstat_u, bstat_f, gstat_i, gstat_u, gstat_f, gstat_o, dsink, cstat, cstat_w))
    return grad_x.reshape(B, S, D), dict(w_in=dw_in, w_out=dw_out, w_up=dw_up, w_down=dw_down), dmod, small


def _pack_slab(raw, drpb):
    D, F = raw[3].shape[1], raw[8].shape[1]
    n_seq = raw[0].shape[0]

    def body(bi_ref, bu_ref, bf_ref, gi_ref, gu_ref, gf_ref, go_ref, ds_ref, cs_ref, cw_ref, rp_ref, o_ref):
        o_ref[...] = jnp.zeros_like(o_ref)
        for b in range(n_seq):
            mods = (bi_ref[b, 0:1, :], bi_ref[b, 1:2, :], bu_ref[b, 2:3, :], bu_ref[b, 0:1, :], bu_ref[b, 1:2, :], bf_ref[b, 0:1, :])
            for k, row in enumerate(mods):
                o_ref[b:b + 1, k * D:(k + 1) * D] = row
        o = 0
        for row in (gi_ref[0:1, :], gu_ref[0:1, :], gf_ref[0:1, :], go_ref[0:1, :], go_ref[1:2, :]):
            o_ref[2:3, o:o + row.shape[1]] = row
            o += row.shape[1]
        ds = ds_ref[...]
        eye = lax.broadcasted_iota(jnp.int32, ds.shape, 0) == lax.broadcasted_iota(jnp.int32, ds.shape, 1)
        o_ref[2:3, o:o + 128] = jnp.sum(jnp.where(eye, ds, 0.0), axis=0, keepdims=True)
        o_ref[2:3, o + 128:o + 256] = gf_ref[1:2, 0:128]
        o_ref[3:4, 0:F] = cs_ref[0:1, :]
        o_ref[4:5, 0:rp_ref.shape[1]] = rp_ref[...]
        o_ref[5:8, 0:F] = cw_ref[1:4, :]

    return pl.pallas_call(body, name="pack_slab", out_shape=jax.ShapeDtypeStruct((8, PACK_W), F32),
                          compiler_params=pltpu.CompilerParams(vmem_limit_bytes=VMEM_LIMIT))(*raw, drpb)


def kernel(x, c, w_ada, b_ada, g_attn, w_in, na_rpb, sw_sink, g_na_out, g_sw_out, w_out, g_ffn, w_up, conv_w, conv_b, w_down, g_final, loss_target, m_w_ada, m_b_ada, m_g_attn, m_w_in, m_na_rpb, m_sw_sink, m_g_na_out, m_g_sw_out, m_w_out, m_g_ffn, m_w_up, m_conv_w, m_conv_b, m_w_down, m_g_final, v_w_ada, v_b_ada, v_g_attn, v_w_in, v_na_rpb, v_sw_sink, v_g_na_out, v_g_sw_out, v_w_out, v_g_ffn, v_w_up, v_conv_w, v_conv_b, v_w_down, v_g_final):
    B, S, D = x.shape
    me = 4 * lax.axis_index("x") + 2 * lax.axis_index("y") + lax.axis_index("c")
    ada_c = w_ada.shape[2]
    F_l = conv_w.shape[2]

    tr = {"w_in", "w_up"}
    w_in_t = jnp.transpose(w_in[0])
    shards_f = dict(w_out=w_out[0], w_up=jnp.transpose(w_up[0]), w_down=w_down[0])

    c_all, conv_w_f, mod_all, w_in_all, casts, bias = _ada_fwd(c, conv_w[0], w_ada[0], b_ada, w_in_t, list(shards_f.values()),
                                                               _na_bias_rev(na_rpb[0]))
    shards = dict(zip(shards_f, casts))
    mod_mine = lax.dynamic_slice(mod_all, (0, me * B, 0), (N_DEV, B, ada_c))
    mod = jnp.transpose(mod_mine, (1, 0, 2)).reshape(B, N_DEV * ada_c)
    w_in_f = w_in_all.reshape(N_DEV * w_in_t.shape[0], D)

    grad_x, dw, dmod, small = _local_step(x, mod, g_attn, w_in_f, bias, sw_sink, g_na_out, g_sw_out, shards["w_out"], g_ffn,
                                          shards["w_up"], conv_w_f, conv_b, shards["w_down"], g_final.reshape(1, D), loss_target,
                                          sharded=True)
    g8_in = _by_device(dw["w_in"])
    drpb, got = _na_bias_grad(small["dbt"], [_swap_task(g8_in)])
    send_in, own_in = _chip_sums(g8_in, got[0][0])

    slab = _pack_slab(small["raw"], drpb.reshape(1, -1))
    weights = dict(w_ada=w_ada, b_ada=b_ada, g_attn=g_attn, w_in=w_in, na_rpb=na_rpb, sw_sink=sw_sink, g_na_out=g_na_out,
                   g_sw_out=g_sw_out, w_out=w_out, g_ffn=g_ffn, w_up=w_up, conv_w=conv_w, conv_b=conv_b, w_down=w_down, g_final=g_final)
    ms = dict(w_ada=m_w_ada, b_ada=m_b_ada, g_attn=m_g_attn, w_in=m_w_in, na_rpb=m_na_rpb, sw_sink=m_sw_sink, g_na_out=m_g_na_out,
              g_sw_out=m_g_sw_out, w_out=m_w_out, g_ffn=m_g_ffn, w_up=m_w_up, conv_w=m_conv_w, conv_b=m_conv_b, w_down=m_w_down, g_final=m_g_final)
    vs = dict(w_ada=v_w_ada, b_ada=v_b_ada, g_attn=v_g_attn, w_in=v_w_in, na_rpb=v_na_rpb, sw_sink=v_sw_sink, g_na_out=v_g_na_out,
              g_sw_out=v_g_sw_out, w_out=v_w_out, g_ffn=v_g_ffn, w_up=v_w_up, conv_w=v_conv_w, conv_b=v_conv_b, w_down=v_w_down, g_final=v_g_final)
    names = list(weights)
    grads, deltas, new_m, new_v = {}, {}, {}, {}
    flat = lambda t: t.reshape(1, -1)

    def shard2d(nm):
        if nm in tr:
            return (lambda t: jnp.transpose(t[0])), (lambda t: jnp.transpose(t)[None])
        if nm == "conv_w":
            return (lambda t: jnp.transpose(t, (1, 0, 2))), (lambda t: jnp.transpose(t, (1, 0, 2)))
        return (lambda t: t[0]), (lambda t: t[None])

    def finish_sum(nm, own, recvb, done=0):
        r, back = shard2d(nm)
        g2, d_, m_, v_ = _sum_adamw(own, recvb, r(weights[nm]), r(ms[nm]), r(vs[nm]), "adamw_" + nm, done)
        grads[nm], deltas[nm], new_m[nm], new_v[nm] = back(g2), back(d_), back(m_), back(v_)

    own_up, recv_up = dw["w_up"]
    n_up = _sum_adamw_steps(own_up.shape[0])
    part_up, got = _sum_first(own_up, recv_up, "sum_first_w_up", [_gather_task(slab, n_up - 1), _exchange_task(send_in)])
    packs = got[0][0]
    finish_sum("w_up", part_up, recv_up, done=2)
    finish_sum("w_down", *dw["w_down"])
    finish_sum("w_in", own_in, got[1][0])
    finish_sum("w_out", *dw["w_out"])

    where = dict(b_ada=((0, 1), 0), g_attn=((2,), 0), g_ffn=((2,), D), g_final=((2,), 2 * D), g_na_out=((2,), 3 * D),
                 g_sw_out=((2,), 3 * D + NA_WIDTH), sw_sink=((2,), 3 * D + NA_WIDTH + SW_WIDTH), conv_b=((3,), 0), na_rpb=((4,), 0))

    def small_view(n):
        if n == "na_rpb":
            return (lambda t: jnp.transpose(t[0], (1, 0, 2))), (lambda t: jnp.transpose(t, (1, 0, 2))[None])
        return flat, (lambda t: t.reshape(weights[n].shape))

    tot, loss, small_out = _pack_sum_adamw(
        packs.reshape(N_DEV * 8, PACK_W),
        [tuple(small_view(n)[0](t[n]) for t in (weights, ms, vs)) + where[n] for n in where],
        (2, 3 * D + NA_WIDTH + SW_WIDTH + 128))
    for n, res in zip(where, small_out):
        grads[n], deltas[n], new_m[n], new_v[n] = [small_view(n)[1](t) for t in res]
    loss = loss.reshape(())

    for nm, g2 in (("w_ada", _ada_bwd(c_all, packs.reshape(N_DEV * 8, PACK_W), ada_c)), ("conv_w", lax.dynamic_slice(tot, (5, me * F_l), (3, F_l))[:, None])):
        r, back = shard2d(nm)
        d_, m_, v_ = _adamw(r(weights[nm]), g2, r(ms[nm]), r(vs[nm]), "adamw_" + nm)
        grads[nm], deltas[nm], new_m[nm], new_v[nm] = back(g2), back(d_), back(m_), back(v_)
    return (loss, grad_x, *[grads[n] for n in names], *[deltas[n] for n in names], *[new_m[n] for n in names],
            *[new_v[n] for n in names])
```

```python
import functools

import numpy as np
import jax
import jax.numpy as jnp
from jax import lax
from jax.experimental import pallas as pl
from jax.experimental.pallas import tpu as pltpu

F32, BF16 = jnp.float32, jnp.bfloat16
MESH_ID = pl.DeviceIdType.MESH
N_DEV = 8

HEAD_DIM = 64
NA_HEADS = 8
SW_HEADS = 8
SW_KV_HEADS = 2
SW_GROUP = SW_HEADS // SW_KV_HEADS
NA_WIDTH = NA_HEADS * HEAD_DIM
SW_WIDTH = SW_HEADS * HEAD_DIM
SW_KV_WIDTH = SW_KV_HEADS * HEAD_DIM
ROPE_WIDTH = SW_WIDTH + SW_KV_WIDTH
IN_WIDTH = 3 * NA_WIDTH + SW_WIDTH + 2 * SW_KV_WIDTH
ROPE_LO = 3 * NA_WIDTH
GRID_W = 64
NA_ROWS_MAX = 8
NA_COLS = 16
N_DR = 2 * NA_ROWS_MAX - 1
N_DC = 2 * NA_COLS - 1
SW_WINDOW = 128
SW_BLOCK = 128
ROPE_THETA = 10000.0
EPS = 1e-6
NEG = -1e30
Q_SCALE = HEAD_DIM ** -0.5

ADAM_LR = 0.001
ADAM_B1 = 0.9
ADAM_B2 = 0.999
ADAM_EPS = 1e-08
ADAM_WD = 0.01
ADAM_STEP = 10

TOKEN_TILE = 256
WIDE_TILE = 512
VMEM_LIMIT = 56 * 1024 * 1024

PACK_W = 6144


def _nn(a, b):
    return jnp.dot(a, b, preferred_element_type=F32)


def _nt(a, b):
    return lax.dot_general(a, b, (((1,), (1,)), ((), ())), preferred_element_type=F32)


def _tn(a, b):
    return lax.dot_general(a, b, (((0,), (0,)), ((), ())), preferred_element_type=F32)


def _rms(x):
    r = lax.rsqrt(jnp.mean(x * x, axis=-1, keepdims=True) + EPS)
    return x * r, r


def _rms_bwd(xn, r, gy):
    return r * (gy - xn * jnp.mean(xn * gy, axis=-1, keepdims=True))


def _params(*sem):
    return pltpu.CompilerParams(dimension_semantics=sem, vmem_limit_bytes=VMEM_LIMIT)


def _full(shape):
    n = len(shape)
    return pl.BlockSpec(shape, lambda *_: (0,) * n)


def _mesh_pos():
    return lax.axis_index("x"), lax.axis_index("y"), lax.axis_index("c")


def _row_chunk(r):
    for rc in (128, 64, 32, 16):
        if r % rc == 0:
            return rc
    raise ValueError(f"rows {r} not a multiple of 16")


class _Task:
    def __init__(self, inputs, out_shapes, sems, start, finish, mid=None, mid_step=None, alias=None):
        self.inputs, self.out_shapes, self.sems = list(inputs), list(out_shapes), list(sems)
        self.start, self.finish, self.mid, self.mid_step = start, finish, mid, mid_step
        self.alias = alias


def _hosted_call(body, name, grid, in_specs, out_specs, out_shape, operands, tasks, scratch_shapes=()):
    n_in, n_out, n_scr = len(in_specs), len(out_specs), len(scratch_shapes)
    t_in = [len(t.inputs) for t in tasks]
    t_out = [len(t.out_shapes) for t in tasks]
    t_sem = [len(t.sems) for t in tasks]
    n_steps = int(np.prod(grid))

    def wrapped(*refs):
        ins, rest = refs[:n_in], refs[n_in:]
        task_ins, rest = rest[:sum(t_in)], rest[sum(t_in):]
        outs, rest = rest[:n_out], rest[n_out:]
        task_outs, rest = rest[:sum(t_out)], rest[sum(t_out):]
        scr, task_sems = rest[:n_scr], rest[n_scr:]
        step = pl.program_id(0)
        for ax in range(1, len(grid)):
            step = step * grid[ax] + pl.program_id(ax)
        parts = []
        oi = oo = os_ = 0
        for t, a, b, c in zip(tasks, t_in, t_out, t_sem):
            parts.append((t, task_ins[oi:oi + a], task_outs[oo:oo + b], task_sems[os_:os_ + c]))
            oi, oo, os_ = oi + a, oo + b, os_ + c
        for t, ti, to, ts in parts:
            pl.when(step == 0)(functools.partial(t.start, ti, to, ts))
            if t.mid is not None:
                pl.when(step == t.mid_step)(functools.partial(t.mid, ti, to, ts))
        body(*ins, *outs, *scr)
        for t, ti, to, ts in parts:
            pl.when(step == n_steps - 1)(functools.partial(t.finish, ti, to, ts))

    hbm = pl.BlockSpec(memory_space=pl.ANY)
    aliases, oi, oo = {}, n_in, n_out
    for t, a, b in zip(tasks, t_in, t_out):
        if t.alias is not None:
            aliases[oi + t.alias[0]] = oo + t.alias[1]
        oi, oo = oi + a, oo + b
    res = pl.pallas_call(
        wrapped, name=name, grid=grid,
        in_specs=list(in_specs) + [hbm] * sum(t_in),
        out_specs=list(out_specs) + [hbm] * sum(t_out),
        out_shape=list(out_shape) + [s for t in tasks for s in t.out_shapes],
        scratch_shapes=list(scratch_shapes) + [s for t in tasks for s in t.sems],
        input_output_aliases=aliases,
        compiler_params=_params(*(["arbitrary"] * len(grid))),
    )(*operands, *[a for t in tasks for a in t.inputs])
    own, extra = res[:n_out], res[n_out:]
    per_task, o = [], 0
    for b in t_out:
        per_task.append(extra[o:o + b])
        o += b
    return own, per_task


def _gather_task(shard, mid_step, rows=None, into=None):
    lo, n = (0, shard.shape[0]) if rows is None else rows

    def parts(ins, outs, sems):
        x_ref, out_ref, (send_sems, recv_sems, local_sem) = ins[0], outs[0], sems
        x_, y_, c_ = _mesh_pos()
        me, sibling = (x_, y_, c_), (x_, y_, 1 - c_)
        chips = [(1 - x_, y_), (x_, 1 - y_), (1 - x_, 1 - y_)]
        x_ref = x_ref.at[pl.ds(lo, n)]

        def rows(px, py, pc):
            return out_ref.at[4 * px + 2 * py + pc, pl.ds(lo, n)]

        def copy(k, block, to, src=None):
            return pltpu.make_async_remote_copy(
                src_ref=rows(*block) if src is None else src, dst_ref=rows(*block),
                send_sem=send_sems.at[k], recv_sem=recv_sems.at[k], device_id=to, device_id_type=MESH_ID)

        return dict(
            mine=lambda: pltpu.make_async_copy(x_ref, rows(*me), local_sem),
            first=lambda: [copy(0, me, sibling, src=x_ref)] + [copy(1 + j, me, (*chip, c_), src=x_ref) for j, chip in enumerate(chips)],
            passed=lambda: [copy(4 + j, (*chip, c_), sibling) for j, chip in enumerate(chips)],
            landed=lambda: [copy(1 + j, (*chip, c_), me) for j, chip in enumerate(chips)],
            last=lambda: [copy(0, sibling, me)] + [copy(4 + j, (*chip, 1 - c_), me) for j, chip in enumerate(chips)])

    def start(ins, outs, sems):
        p = parts(ins, outs, sems)
        p["mine"]().start()
        for cp in p["first"]():
            cp.start()

    def mid(ins, outs, sems):
        p = parts(ins, outs, sems)
        for cp, fw in zip(p["landed"](), p["passed"]()):
            cp.wait_recv()
            fw.start()

    def finish(ins, outs, sems):
        p = parts(ins, outs, sems)
        for cp in p["last"]():
            cp.wait_recv()
        for cp in p["first"]() + p["passed"]():
            cp.wait_send()
        p["mine"]().wait()

    return _Task([shard] if into is None else [shard, into], [jax.ShapeDtypeStruct((N_DEV,) + shard.shape, shard.dtype)],
                 [pltpu.SemaphoreType.DMA((7,)), pltpu.SemaphoreType.DMA((7,)), pltpu.SemaphoreType.DMA],
                 start, finish, mid, mid_step, alias=None if into is None else (1, 0))


def _swap_task(g8):
    _, R, C = g8.shape

    def copies(ins, outs, sems):
        (g_ref,), (recv_ref,), (ss, rs) = ins, outs, sems
        x_, y_, c_ = _mesh_pos()
        return [pltpu.make_async_remote_copy(src_ref=g_ref.at[2 * k + (1 - c_)], dst_ref=recv_ref.at[k], send_sem=ss.at[k],
                                             recv_sem=rs.at[k], device_id=(x_, y_, 1 - c_), device_id_type=MESH_ID)
                for k in range(4)]

    def start(ins, outs, sems):
        for cp in copies(ins, outs, sems):
            cp.start()

    def finish(ins, outs, sems):
        cps = copies(ins, outs, sems)
        for cp in cps:
            cp.wait_recv()
        for cp in cps:
            cp.wait_send()

    return _Task([g8], [jax.ShapeDtypeStruct((4, R, C), g8.dtype)],
                 [pltpu.SemaphoreType.DMA((4,)), pltpu.SemaphoreType.DMA((4,))], start, finish)


def _chip_sums(g8, recva):
    _, R, C = g8.shape
    rt = _sum_adamw_rows(R)
    rc = _row_chunk(rt)

    def body(core_ref, g_ref, a_ref, send_ref, own_ref):
        x_, y_, _ = _mesh_pos()
        chips = [(1 - x_, y_), (x_, 1 - y_), (1 - x_, 1 - y_), (x_, y_)]

        def chunk(i, carry):
            rows = pl.ds(pl.multiple_of(i * rc, rc), rc)
            for j, (tx, ty) in enumerate(chips):
                k = 2 * tx + ty
                s = g_ref[k, rows, :].astype(F32) + a_ref[k, rows, :].astype(F32)
                if j < 3:
                    send_ref[j, rows, :] = s.astype(BF16)
                else:
                    own_ref[rows, :] = s
            return carry

        lax.fori_loop(0, rt // rc, chunk, 0)

    core = lax.axis_index("c").astype(jnp.int32).reshape(1)
    return pl.pallas_call(
        body, name="chip_sums",
        grid_spec=pltpu.PrefetchScalarGridSpec(
            num_scalar_prefetch=1, grid=(R // rt,),
            in_specs=[pl.BlockSpec((4, None, rt, C), lambda i, c: (0, c[0], i, 0)), pl.BlockSpec((4, rt, C), lambda i, c: (0, i, 0))],
            out_specs=[pl.BlockSpec((3, rt, C), lambda i, c: (0, i, 0)), pl.BlockSpec((rt, C), lambda i, c: (i, 0))]),
        out_shape=[jax.ShapeDtypeStruct((3, R, C), BF16), jax.ShapeDtypeStruct((R, C), F32)],
        compiler_params=_params("parallel"),
    )(core, g8.reshape(4, 2, R, C), recva)


def _exchange_task(sendb):
    def copies(ins, outs, sems):
        (s_ref,), (recv_ref,), (ss, rs) = ins, outs, sems
        x_, y_, c_ = _mesh_pos()
        flips = [(1 - x_, y_), (x_, 1 - y_), (1 - x_, 1 - y_)]
        return [pltpu.make_async_remote_copy(src_ref=s_ref.at[j], dst_ref=recv_ref.at[j], send_sem=ss.at[j], recv_sem=rs.at[j],
                                             device_id=(tx, ty, c_), device_id_type=MESH_ID) for j, (tx, ty) in enumerate(flips)]

    def start(ins, outs, sems):
        for cp in copies(ins, outs, sems):
            cp.start()

    def finish(ins, outs, sems):
        cps = copies(ins, outs, sems)
        for cp in cps:
            cp.wait_recv()
        for cp in cps:
            cp.wait_send()

    return _Task([sendb], [jax.ShapeDtypeStruct(sendb.shape, sendb.dtype)],
                 [pltpu.SemaphoreType.DMA((3,)), pltpu.SemaphoreType.DMA((3,))], start, finish)


def _reduce_scatter_call(g8, slab):
    _, R, C = g8.shape
    rc = _row_chunk(R)
    t_s = _gather_task(slab, 0)

    def body(g_ref, slab_ref, recva_ref, recvb_ref, own_ref, packs_ref, mine, theirs, send_vm, ld_sems, in_sems, sw_s, sw_r,
             ex_s, ex_r, *s_sems):
        x_, y_, c_ = _mesh_pos()
        chips = [(1 - x_, y_), (x_, 1 - y_), (1 - x_, 1 - y_), (x_, y_)]
        t_s.start((slab_ref,), (packs_ref,), s_sems)
        swaps, loads = [], []
        for j, (tx, ty) in enumerate(chips):
            k = 2 * tx + ty
            swaps.append(pltpu.make_async_remote_copy(src_ref=g_ref.at[2 * k + (1 - c_)], dst_ref=recva_ref.at[j], send_sem=sw_s.at[j],
                                                      recv_sem=sw_r.at[j], device_id=(x_, y_, 1 - c_), device_id_type=MESH_ID))
            loads.append(pltpu.make_async_copy(g_ref.at[2 * k + c_], mine.at[j], ld_sems.at[j]))
            swaps[j].start()
            loads[j].start()
        sends = []
        for j, (tx, ty) in enumerate(chips):
            swaps[j].wait_recv()
            cp = pltpu.make_async_copy(recva_ref.at[j], theirs.at[j], in_sems.at[j])
            cp.start()
            cp.wait()
            loads[j].wait()

            def chunk(i, carry):
                rows = pl.ds(pl.multiple_of(i * rc, rc), rc)
                s = mine[j, rows, :].astype(F32) + theirs[j, rows, :].astype(F32)
                if j < 3:
                    send_vm[j, rows, :] = s.astype(BF16)
                else:
                    own_ref[rows, :] = s
                return carry

            lax.fori_loop(0, R // rc, chunk, 0)
            if j < 3:
                sends.append(pltpu.make_async_remote_copy(src_ref=send_vm.at[j], dst_ref=recvb_ref.at[j], send_sem=ex_s.at[j],
                                                          recv_sem=ex_r.at[j], device_id=(tx, ty, c_), device_id_type=MESH_ID))
                sends[j].start()
        t_s.mid((slab_ref,), (packs_ref,), s_sems)
        t_s.finish((slab_ref,), (packs_ref,), s_sems)
        for cp in sends:
            cp.wait_recv()
        for cp in sends + swaps:
            cp.wait_send()

    hbm, vm = pl.BlockSpec(memory_space=pl.ANY), pl.BlockSpec(memory_space=pltpu.VMEM)
    blocks = lambda n: jax.ShapeDtypeStruct((n, R, C), BF16)
    _, recvb, own, packs = pl.pallas_call(
        body, name="reduce_scatter_w_in", in_specs=[hbm, hbm], out_specs=[hbm, hbm, vm, hbm],
        out_shape=[blocks(4), blocks(3), jax.ShapeDtypeStruct((R, C), F32)] + t_s.out_shapes,
        scratch_shapes=[pltpu.VMEM((4, R, C), BF16), pltpu.VMEM((4, R, C), BF16), pltpu.VMEM((3, R, C), BF16)]
        + [pltpu.SemaphoreType.DMA((4,))] * 4 + [pltpu.SemaphoreType.DMA((3,))] * 2 + t_s.sems,
        compiler_params=pltpu.CompilerParams(vmem_limit_bytes=VMEM_LIMIT),
    )(g8, slab)
    return own, recvb, packs


def _silu(v):
    return v * (1.0 / (1.0 + jnp.exp(-v)))


def _ada_fwd(c, taps, w_ada_l, b_ada, w_in_shard, casts, rpb_rev):
    n_seq = c.shape[0]
    D, cols = w_ada_l.shape
    F_l = taps.shape[1]
    W = D + F_l
    n_rows = N_DEV * n_seq
    n_c = len(casts)
    t_c = _gather_task(jax.ShapeDtypeStruct((8, W), F32), 0)
    t_w = _gather_task(jax.ShapeDtypeStruct(w_in_shard.shape, BF16), 0)
    t_m = _gather_task(jax.ShapeDtypeStruct((n_rows, cols), F32), 0)

    def body(c_ref, taps_ref, w_ref, b_ref, ws_ref, rev_ref, *refs):
        cast_in, refs = refs[:n_c], refs[n_c:]
        (slabs_ref, call_ref, taps_all_ref, mod_ref, win_ref), refs = refs[:5], refs[5:]
        cast_out, bias_ref, refs = refs[:n_c], refs[n_c], refs[n_c + 1:]
        (slab_vm, c_vm, m_vm, ws_f, ws_b), refs = refs[:5], refs[5:]
        stage_f, stage_b, refs = refs[:n_c], refs[n_c:2 * n_c], refs[2 * n_c:]
        copy_sem, in_sems, out_sems, sems = refs[0], refs[1], refs[2], refs[3:]
        sc, sw, sm = sems[0:3], sems[3:6], sems[6:9]
        x_, y_, c_ = _mesh_pos()
        slab_vm[...] = jnp.zeros_like(slab_vm)
        slab_vm[0:n_seq, 0:D] = c_ref[...]
        slab_vm[0:taps.shape[0], D:W] = taps_ref[...]
        t_c.start((slab_vm,), (slabs_ref,), sc)
        cp = pltpu.make_async_copy(ws_ref, ws_f, copy_sem)
        cp.start()
        cp.wait()
        ws_b[...] = ws_f[...].astype(BF16)
        t_w.start((ws_b,), (win_ref,), sw)
        loads = [pltpu.make_async_copy(cast_in[j], stage_f[j], in_sems.at[j]) for j in range(n_c)]
        for ld in loads:
            ld.start()
        t_c.mid((slab_vm,), (slabs_ref,), sc)
        t_c.finish((slab_vm,), (slabs_ref,), sc)
        cp = pltpu.make_async_copy(slabs_ref, c_vm, copy_sem)
        cp.start()
        cp.wait()
        c_all = c_vm[:, :, 0:D].reshape(N_DEV * 8, D)
        call_ref[...] = c_all
        for j in range(N_DEV):
            taps_all_ref[:, j * F_l:(j + 1) * F_l] = c_vm[j, 0:taps.shape[0], D:W]
        b_mine = b_ref[:, pl.ds(pl.multiple_of((4 * x_ + 2 * y_ + c_) * cols, 128), cols)]
        m64 = jnp.dot(_silu(c_all), w_ref[...], precision=lax.Precision.HIGHEST, preferred_element_type=F32) + b_mine
        r = lax.broadcasted_iota(jnp.int32, (n_rows, N_DEV * 8), 0)
        c = lax.broadcasted_iota(jnp.int32, (n_rows, N_DEV * 8), 1)
        pick = jnp.where(c == 8 * (r // n_seq) + r % n_seq, 1.0, 0.0)
        m_vm[...] = jnp.dot(pick, m64, precision=lax.Precision.HIGHEST, preferred_element_type=F32)
        t_m.start((m_vm,), (mod_ref,), sm)
        stores = [pltpu.make_async_copy(stage_b[j], cast_out[j], out_sems.at[j]) for j in range(n_c)]
        for j in range(n_c):
            loads[j].wait()
            stage_b[j][...] = stage_f[j][...].astype(BF16)
            stores[j].start()
        for p in range(NA_PAIRS):
            _na_bias_rows(rev_ref, bias_ref, p)
        t_w.mid((ws_b,), (win_ref,), sw)
        t_m.mid((m_vm,), (mod_ref,), sm)
        t_m.finish((m_vm,), (mod_ref,), sm)
        t_w.finish((ws_b,), (win_ref,), sw)
        for st in stores:
            st.wait()

    hbm, vm = pl.BlockSpec(memory_space=pl.ANY), pl.BlockSpec(memory_space=pltpu.VMEM)
    res = pl.pallas_call(
        body, name="ada_fwd", in_specs=[vm, vm, vm, vm, hbm, vm] + [hbm] * n_c,
        out_specs=[hbm, vm, vm, hbm, hbm] + [hbm] * n_c + [vm],
        out_shape=t_c.out_shapes + [jax.ShapeDtypeStruct((N_DEV * 8, D), F32), jax.ShapeDtypeStruct((taps.shape[0], N_DEV * F_l), F32)]
        + t_m.out_shapes + t_w.out_shapes
        + [jax.ShapeDtypeStruct(a.shape, BF16) for a in casts] + [jax.ShapeDtypeStruct((NA_PAIRS, N_DR * GRID_W, 128), F32)],
        scratch_shapes=[pltpu.VMEM((8, W), F32), pltpu.VMEM((N_DEV, 8, W), F32), pltpu.VMEM((n_rows, cols), F32),
                        pltpu.VMEM(w_in_shard.shape, F32), pltpu.VMEM(w_in_shard.shape, BF16)]
        + [pltpu.VMEM(a.shape, F32) for a in casts] + [pltpu.VMEM(a.shape, BF16) for a in casts]
        + [pltpu.SemaphoreType.DMA, pltpu.SemaphoreType.DMA((n_c,)), pltpu.SemaphoreType.DMA((n_c,))]
        + t_c.sems + t_w.sems + t_m.sems,
        compiler_params=pltpu.CompilerParams(vmem_limit_bytes=VMEM_LIMIT),
    )(c, taps, w_ada_l, b_ada, w_in_shard, rpb_rev, *casts)
    return res[1], res[2], res[3], res[4], res[5:5 + n_c], res[5 + n_c]


def _ada_bwd(c_all, packs, cols):
    def body(c_ref, d_ref, o_ref):
        x_, y_, c_ = _mesh_pos()
        mine = d_ref[:, pl.ds(pl.multiple_of((4 * x_ + 2 * y_ + c_) * cols, 128), cols)]
        o_ref[...] = lax.dot_general(_silu(c_ref[...]), mine, (((0,), (0,)), ((), ())),
                                     precision=lax.Precision.HIGHEST, preferred_element_type=F32)
    return pl.pallas_call(body, name="ada_bwd", out_shape=jax.ShapeDtypeStruct((c_all.shape[1], cols), F32),
                          compiler_params=pltpu.CompilerParams(vmem_limit_bytes=VMEM_LIMIT))(c_all, packs)


NA_PAIRS = NA_HEADS // 2
N_DR_PAD = 16


def _na_bias_rev(na_rpb):
    rev = jnp.pad(jnp.flip(na_rpb, axis=2), ((0, 0), (0, N_DR_PAD - N_DR), (0, GRID_W - N_DC)))
    return jnp.transpose(rev.reshape(NA_PAIRS, 2, N_DR_PAD, GRID_W), (0, 2, 1, 3)).reshape(NA_PAIRS, N_DR_PAD, 128)


def _na_bias_rows(r_ref, o_ref, p):
    k = lax.broadcasted_iota(jnp.int32, (GRID_W, 128), 0)
    lane = lax.broadcasted_iota(jnp.int32, (GRID_W, 128), 1)
    q = lane % GRID_W
    cs = jnp.clip(q - NA_COLS // 2, 0, GRID_W - NA_COLS)
    ok = (k >= cs) & (k < cs + NA_COLS)
    left = lane < GRID_W
    for dr in range(N_DR):
        row = jnp.broadcast_to(r_ref[p, dr:dr + 1, :], (GRID_W, 128))
        r0 = jnp.where(left, row, 0.0)
        r1 = jnp.where(left, pltpu.roll(row, GRID_W, axis=1), 0.0)
        y0 = pltpu.roll(r0, 128 - (NA_COLS - 1), axis=1, stride=1, stride_axis=0)
        y1 = pltpu.roll(r1, GRID_W - (NA_COLS - 1), axis=1, stride=1, stride_axis=0)
        o_ref[p, dr * GRID_W:(dr + 1) * GRID_W, :] = jnp.where(ok, jnp.where(left, y0, y1), NEG)


def _na_bias_grad(db, tasks=()):
    a = np.arange(128)
    flip = jnp.asarray(((a[:, None] // GRID_W == a[None, :] // GRID_W)
                        & (a[:, None] % GRID_W + a[None, :] % GRID_W == GRID_W - 1)).astype(np.float32))

    def body(d_ref, j_ref, o_ref):
        o_ref[...] = jnp.zeros_like(o_ref)
        for dr in range(N_DR):
            t = jnp.dot(d_ref[0, dr * GRID_W:(dr + 1) * GRID_W, :], j_ref[...], precision=lax.Precision.HIGHEST, preferred_element_type=F32)
            t = pltpu.roll(t, GRID_W + NA_COLS, axis=1, stride=1, stride_axis=0)
            o_ref[0, dr:dr + 1, :] = jnp.sum(t, axis=0, keepdims=True)

    (rows,), got = _hosted_call(
        body, "rpb_reduce", (NA_PAIRS,),
        [pl.BlockSpec((1, N_DR * GRID_W, 128), lambda p: (p, 0, 0)), _full((128, 128))],
        [pl.BlockSpec((1, N_DR_PAD, 128), lambda p: (p, 0, 0))],
        [jax.ShapeDtypeStruct((NA_PAIRS, N_DR_PAD, 128), F32)], (db, flip), tasks)
    g = rows.reshape(NA_PAIRS, N_DR_PAD, 2, GRID_W)[:, :N_DR, :, :N_DC]
    return jnp.transpose(g, (0, 2, 1, 3)).reshape(-1), got


def _rope_tables(S):
    half = HEAD_DIM // 2
    inv = np.float32(ROPE_THETA) ** (-np.arange(half, dtype=np.float32) / np.float32(half))
    ang = np.arange(S).astype(np.float32)[:, None] * inv[None, :]
    cos, sin = np.cos(ang).astype(np.float32), np.sin(ang).astype(np.float32)
    return jnp.asarray(np.tile(np.concatenate([cos, cos], axis=1), (1, 2))), jnp.asarray(np.tile(np.concatenate([-sin, sin], axis=1), (1, 2)))


def _rope_spec(tps, tm=TOKEN_TILE):
    return pl.BlockSpec((tm, 2 * HEAD_DIM), lambda i: (i % tps, 0))


def _rot_half(t):
    w = t.shape[1]
    lane = lax.broadcasted_iota(jnp.int32, t.shape, 1)
    return jnp.where((lane % HEAD_DIM) < HEAD_DIM // 2, pltpu.roll(t, w - HEAD_DIM // 2, axis=1),
                     pltpu.roll(t, HEAD_DIM // 2, axis=1))


def _tok_spec(w, tm=TOKEN_TILE):
    return pl.BlockSpec((tm, w), lambda i: (i, 0))


def _mod_spec(tps, d):
    return pl.BlockSpec((1, 6, d), lambda i: (i // tps, 0, 0))


def _bstat_spec(tps, w):
    return pl.BlockSpec((1, 8, w), lambda i: (i // tps, 0, 0))


def _attn_in(x2d, mod3, g_attn, w_in, cos_t, sin_t, S, tasks=(), tm=WIDE_TILE):
    T, D = x2d.shape
    tps = S // tm

    def body(x_ref, mod_ref, g_ref, w_ref, cos_ref, sin_ref, h_ref, qkv_ref):
        xn, _ = _rms(x_ref[...])
        h = (xn * g_ref[...]) * (1.0 + mod_ref[0, 1:2, :]) + mod_ref[0, 0:1, :]
        hb = h.astype(BF16)
        h_ref[...] = hb
        proj = _nt(hb, w_ref[...])
        rb = proj[:, ROPE_LO:ROPE_LO + ROPE_WIDTH]
        reps = (1, ROPE_WIDTH // (2 * HEAD_DIM))
        rb = rb * jnp.tile(cos_ref[...], reps) + _rot_half(rb) * jnp.tile(sin_ref[...], reps)
        qkv_ref[:, 0:NA_WIDTH] = (proj[:, 0:NA_WIDTH] * Q_SCALE).astype(BF16)
        qkv_ref[:, NA_WIDTH:ROPE_LO] = proj[:, NA_WIDTH:ROPE_LO].astype(BF16)
        qkv_ref[:, ROPE_LO:ROPE_LO + SW_WIDTH] = (rb[:, 0:SW_WIDTH] * Q_SCALE).astype(BF16)
        qkv_ref[:, ROPE_LO + SW_WIDTH:ROPE_LO + ROPE_WIDTH] = rb[:, SW_WIDTH:].astype(BF16)
        qkv_ref[:, ROPE_LO + ROPE_WIDTH:] = proj[:, ROPE_LO + ROPE_WIDTH:].astype(BF16)

    return _hosted_call(
        body, "attn_in", (T // tm,),
        [_tok_spec(D, tm), _mod_spec(tps, D), _full((1, D)), _full(w_in.shape), _rope_spec(tps, tm), _rope_spec(tps, tm)],
        [_tok_spec(D, tm), _tok_spec(IN_WIDTH, tm)],
        [jax.ShapeDtypeStruct((T, D), BF16), jax.ShapeDtypeStruct((T, IN_WIDTH), BF16)],
        (x2d, mod3, g_attn, w_in, cos_t, sin_t), tasks)


def _attn_out(oa, ob, x2d, mod3, g_na, g_sw, w_out, S, tasks=(), tm=WIDE_TILE):
    T, D = x2d.shape
    tps = S // tm

    def body(oa_ref, ob_ref, x_ref, mod_ref, gna_ref, gsw_ref, w_ref, mixin_ref, mix_ref, x1_ref):
        oan, _ = _rms(oa_ref[...])
        obn, _ = _rms(ob_ref[...])
        mixin = jnp.concatenate([oan * gna_ref[...], obn * gsw_ref[...]], axis=1).astype(BF16)
        mixin_ref[...] = mixin
        mix = _nn(mixin, w_ref[...])
        mix_ref[...] = mix
        x1_ref[...] = x_ref[...] + mod_ref[0, 2:3, :] * mix

    return _hosted_call(
        body, "attn_out", (T // tm,),
        [_tok_spec(NA_WIDTH, tm), _tok_spec(SW_WIDTH, tm), _tok_spec(D, tm), _mod_spec(tps, D),
         _full((1, NA_WIDTH)), _full((1, SW_WIDTH)), _full(w_out.shape)],
        [_tok_spec(NA_WIDTH + SW_WIDTH, tm), _tok_spec(D, tm), _tok_spec(D, tm)],
        [jax.ShapeDtypeStruct((T, NA_WIDTH + SW_WIDTH), BF16), jax.ShapeDtypeStruct((T, D), F32), jax.ShapeDtypeStruct((T, D), F32)],
        (oa, ob, x2d, mod3, g_na, g_sw, w_out), tasks)


def _ffn_up(x1, mod3, g_ffn, w_up, S, tasks=(), tm=WIDE_TILE):
    T, D = x1.shape
    F = w_up.shape[0] // 2
    tps = S // tm

    def body(x1_ref, mod_ref, g_ref, w_ref, h2_ref, val_ref, gt_ref):
        xn, _ = _rms(x1_ref[...])
        h2 = ((xn * g_ref[...]) * (1.0 + mod_ref[0, 4:5, :]) + mod_ref[0, 3:4, :]).astype(BF16)
        h2_ref[...] = h2
        u = _nt(h2, w_ref[...])
        val_ref[...] = u[:, :F].astype(BF16)
        gt_ref[...] = u[:, F:].astype(BF16)

    return _hosted_call(
        body, "ffn_up", (T // tm,), [_tok_spec(D, tm), _mod_spec(tps, D), _full((1, D)), _full(w_up.shape)],
        [_tok_spec(D, tm), _tok_spec(F, tm), _tok_spec(F, tm)],
        [jax.ShapeDtypeStruct((T, D), BF16), jax.ShapeDtypeStruct((T, F), BF16), jax.ShapeDtypeStruct((T, F), BF16)],
        (x1, mod3, g_ffn, w_up), tasks)


def _halo_specs(T, tps, w):
    per = TOKEN_TILE // 8
    prev = pl.BlockSpec((8, w), lambda i: (jnp.maximum(i * per - 1, 0), 0))
    nxt = pl.BlockSpec((8, w), lambda i: (jnp.minimum((i + 1) * per, T // 8 - 1), 0))
    return prev, nxt


def _seq_shifts(cur, before, after, ti, tps):
    tm = cur.shape[0]
    row = lax.broadcasted_iota(jnp.int32, cur.shape, 0)
    before = jnp.where(ti > 0, before.astype(F32), 0.0)
    after = jnp.where(ti < tps - 1, after.astype(F32), 0.0)
    return jnp.where(row == 0, before, pltpu.roll(cur, 1, axis=0)), jnp.where(row == tm - 1, after, pltpu.roll(cur, tm - 1, axis=0))


def _ffn_down(gt, val, conv_w, conv_b, w_down, x1, mod3, g_final, target, B, S):
    T, D = x1.shape
    F = gt.shape[1]
    tps = S // TOKEN_TILE
    prev, nxt = _halo_specs(T, tps, F)

    def body(gt_ref, prev_ref, next_ref, val_ref, cw_ref, cb_ref, w_ref, x1_ref, mod_ref, gf_ref, tgt_ref,
             a_ref, act_ref, vd_ref, dx2_ref, df_ref, gstat_ref, bstat_ref):
        i = pl.program_id(0)
        g = gt_ref[...].astype(F32)
        gprev, gnext = _seq_shifts(g, prev_ref[7:8, :], next_ref[0:1, :], i % tps, tps)
        gc = gprev * cw_ref[0:1, :] + g * cw_ref[1:2, :] + gnext * cw_ref[2:3, :] + cb_ref[...]
        sig = 1.0 / (1.0 + jnp.exp(-gc))
        act = gc * sig
        val = val_ref[...].astype(F32)
        act_ref[...] = act.astype(BF16)
        vd_ref[...] = (val * (sig + act - act * sig)).astype(BF16)
        a = (act * val).astype(BF16)
        a_ref[...] = a
        f = _nn(a, w_ref[...])
        gate = mod_ref[0, 5:6, :]
        x2 = x1_ref[...] + gate * f
        xn, r = _rms(x2)
        err = xn * gf_ref[...] - tgt_ref[...]
        dy = err * (1.0 / D)
        dx2 = _rms_bwd(xn, r, dy * gf_ref[...])
        dx2_ref[...] = dx2
        df_ref[...] = (gate * dx2).astype(BF16)

        @pl.when(i == 0)
        def _():
            gstat_ref[...] = jnp.zeros_like(gstat_ref)

        @pl.when(i % tps == 0)
        def _():
            bstat_ref[...] = jnp.zeros_like(bstat_ref)

        gstat_ref[0:1, :] += jnp.sum(dy * xn, axis=0, keepdims=True)
        tile_loss = jnp.sum(jnp.sum(err * err, axis=1, keepdims=True), axis=0, keepdims=True) * (0.5 / D)
        gstat_ref[1:2, :] += jnp.broadcast_to(tile_loss, (1, D))
        bstat_ref[0, 0:1, :] += jnp.sum(dx2 * f, axis=0, keepdims=True)

    return pl.pallas_call(
        body, name="ffn_down", grid=(T // TOKEN_TILE,),
        in_specs=[_tok_spec(F), prev, nxt, _tok_spec(F), _full(conv_w.shape), _full((1, F)), _full(w_down.shape),
                  _tok_spec(D), _mod_spec(tps, D), _full((1, D)), _tok_spec(D)],
        out_specs=[_tok_spec(F), _tok_spec(F), _tok_spec(F), _tok_spec(D), _tok_spec(D), _full((8, D)), _bstat_spec(tps, D)],
        out_shape=[jax.ShapeDtypeStruct((T, F), BF16), jax.ShapeDtypeStruct((T, F), BF16), jax.ShapeDtypeStruct((T, F), BF16),
                   jax.ShapeDtypeStruct((T, D), F32), jax.ShapeDtypeStruct((T, D), BF16),
                   jax.ShapeDtypeStruct((8, D), F32), jax.ShapeDtypeStruct((B, 8, D), F32)],
        compiler_params=_params("arbitrary"),
    )(gt, gt, gt, val, conv_w, conv_b, w_down, x1, mod3, g_final, target)


def _ffn_down_bwd(df, w_down, act, vd, tasks=(), tm=WIDE_TILE):
    T, D = df.shape
    F = act.shape[1]

    def body(df_ref, w_ref, act_ref, vd_ref, dval_ref, dgc_ref, cstat_ref):
        da = _nt(df_ref[...], w_ref[...])
        dval_ref[...] = (da * act_ref[...].astype(F32)).astype(BF16)
        dgc = da * vd_ref[...].astype(F32)
        dgc_ref[...] = dgc.astype(BF16)

        @pl.when(pl.program_id(0) == 0)
        def _():
            cstat_ref[...] = jnp.zeros_like(cstat_ref)

        cstat_ref[0:1, :] += jnp.sum(dgc, axis=0, keepdims=True)

    return _hosted_call(
        body, "ffn_down_bwd", (T // tm,),
        [_tok_spec(D, tm), _full(w_down.shape), _tok_spec(F, tm), _tok_spec(F, tm)],
        [_tok_spec(F, tm), _tok_spec(F, tm), _full((8, F))],
        [jax.ShapeDtypeStruct((T, F), BF16), jax.ShapeDtypeStruct((T, F), BF16), jax.ShapeDtypeStruct((8, F), F32)],
        (df, w_down, act, vd), tasks)


def _ffn_up_bwd(dgc, dval, gt, conv_w, w_up, x1, mod3, g_ffn, dx2, mix, B, S, tasks=()):
    T, D = x1.shape
    F = dgc.shape[1]
    tps = S // TOKEN_TILE
    prev, nxt = _halo_specs(T, tps, F)

    def body(dgc_ref, prev_ref, next_ref, dval_ref, gt_ref, cw_ref, w_ref, x1_ref, mod_ref, g_ref, dx2_ref, mix_ref,
             du_ref, dx1_ref, dmix_ref, gstat_ref, bstat_ref, cstat_ref):
        i = pl.program_id(0)
        d = dgc_ref[...].astype(F32)
        dprev, dnext = _seq_shifts(d, prev_ref[7:8, :], next_ref[0:1, :], i % tps, tps)
        g = gt_ref[...].astype(F32)

        @pl.when(i == 0)
        def _():
            cstat_ref[...] = jnp.zeros_like(cstat_ref)

        cstat_ref[1:2, :] += jnp.sum(dnext * g, axis=0, keepdims=True)
        cstat_ref[2:3, :] += jnp.sum(d * g, axis=0, keepdims=True)
        cstat_ref[3:4, :] += jnp.sum(dprev * g, axis=0, keepdims=True)
        dgt = dnext * cw_ref[0:1, :] + d * cw_ref[1:2, :] + dprev * cw_ref[2:3, :]
        du = jnp.concatenate([dval_ref[...], dgt.astype(BF16)], axis=1)
        du_ref[...] = du
        dh2 = _nn(du, w_ref[...])
        xn, r = _rms(x1_ref[...])
        scale1 = 1.0 + mod_ref[0, 4:5, :]
        xg = xn * g_ref[...]
        dx1 = dx2_ref[...] + _rms_bwd(xn, r, dh2 * g_ref[...] * scale1)
        dx1_ref[...] = dx1
        dmix_ref[...] = (mod_ref[0, 2:3, :] * dx1).astype(BF16)

        @pl.when(i == 0)
        def _():
            gstat_ref[...] = jnp.zeros_like(gstat_ref)

        @pl.when(i % tps == 0)
        def _():
            bstat_ref[...] = jnp.zeros_like(bstat_ref)

        gstat_ref[0:1, :] += jnp.sum(dh2 * scale1 * xn, axis=0, keepdims=True)
        bstat_ref[0, 0:1, :] += jnp.sum(dh2, axis=0, keepdims=True)
        bstat_ref[0, 1:2, :] += jnp.sum(dh2 * xg, axis=0, keepdims=True)
        bstat_ref[0, 2:3, :] += jnp.sum(dx1 * mix_ref[...], axis=0, keepdims=True)

    return _hosted_call(
        body, "ffn_up_bwd", (T // TOKEN_TILE,),
        [_tok_spec(F), prev, nxt, _tok_spec(F), _tok_spec(F), _full(conv_w.shape), _full(w_up.shape), _tok_spec(D),
         _mod_spec(tps, D), _full((1, D)), _tok_spec(D), _tok_spec(D)],
        [_tok_spec(2 * F), _tok_spec(D), _tok_spec(D), _full((8, D)), _bstat_spec(tps, D), _full((8, F))],
        [jax.ShapeDtypeStruct((T, 2 * F), BF16), jax.ShapeDtypeStruct((T, D), F32), jax.ShapeDtypeStruct((T, D), BF16),
         jax.ShapeDtypeStruct((8, D), F32), jax.ShapeDtypeStruct((B, 8, D), F32), jax.ShapeDtypeStruct((8, F), F32)],
        (dgc, dgc, dgc, dval, gt, conv_w, w_up, x1, mod3, g_ffn, dx2, mix), tasks)


def _attn_out_bwd(dmix, w_out, oa, ob, g_na, g_sw, tasks=(), tm=WIDE_TILE):
    T, D = dmix.shape

    def body(dmix_ref, w_ref, oa_ref, ob_ref, gna_ref, gsw_ref, doa_ref, dob_ref, gstat_ref):
        dmixin = _nt(dmix_ref[...], w_ref[...])

        @pl.when(pl.program_id(0) == 0)
        def _():
            gstat_ref[...] = jnp.zeros_like(gstat_ref)

        for k, (o_ref, g_ref, do_ref) in enumerate(((oa_ref, gna_ref, doa_ref), (ob_ref, gsw_ref, dob_ref))):
            dn = dmixin[:, k * NA_WIDTH:(k + 1) * NA_WIDTH]
            on, r = _rms(o_ref[...])
            gstat_ref[k:k + 1, :] += jnp.sum(dn * on, axis=0, keepdims=True)
            do_ref[...] = _rms_bwd(on, r, dn * g_ref[...]).astype(BF16)

    hs = jax.ShapeDtypeStruct((T, NA_WIDTH), BF16)
    return _hosted_call(
        body, "attn_out_bwd", (T // tm,),
        [_tok_spec(D, tm), _full(w_out.shape), _tok_spec(NA_WIDTH, tm), _tok_spec(SW_WIDTH, tm), _full((1, NA_WIDTH)), _full((1, SW_WIDTH))],
        [_tok_spec(NA_WIDTH, tm), _tok_spec(SW_WIDTH, tm), _full((8, NA_WIDTH))],
        [hs, hs, jax.ShapeDtypeStruct((8, NA_WIDTH), F32)],
        (dmix, w_out, oa, ob, g_na, g_sw), tasks)


def _attn_in_bwd(dqa, dka, dva, dqb, dkb, dvb, cos_t, sin_t, w_in, x2d, mod3, g_attn, dx1, B, S, tm=WIDE_TILE):
    T, D = x2d.shape
    tps = S // tm

    def body(dqa_ref, dka_ref, dva_ref, dqb_ref, dkb_ref, dvb_ref, cos_ref, sin_ref, w_ref, x_ref, mod_ref, g_ref, dx1_ref,
             gx_ref, dproj_ref, gstat_ref, bstat_ref):
        i = pl.program_id(0)
        drb = jnp.concatenate([dqb_ref[...] * Q_SCALE, dkb_ref[...]], axis=1).astype(F32)
        reps = (1, ROPE_WIDTH // (2 * HEAD_DIM))
        drb = drb * jnp.tile(cos_ref[...], reps) + _rot_half(drb * jnp.tile(sin_ref[...], reps))
        dproj = jnp.concatenate([(dqa_ref[...] * Q_SCALE).astype(BF16), dka_ref[...].astype(BF16), dva_ref[...].astype(BF16),
                                 drb.astype(BF16), dvb_ref[...].astype(BF16)], axis=1)
        dproj_ref[...] = dproj
        dh = _nn(dproj, w_ref[...])
        xn, r = _rms(x_ref[...])
        scale1 = 1.0 + mod_ref[0, 1:2, :]
        gx_ref[...] = dx1_ref[...] + _rms_bwd(xn, r, dh * g_ref[...] * scale1)

        @pl.when(i == 0)
        def _():
            gstat_ref[...] = jnp.zeros_like(gstat_ref)

        @pl.when(i % tps == 0)
        def _():
            bstat_ref[...] = jnp.zeros_like(bstat_ref)

        gstat_ref[0:1, :] += jnp.sum(dh * scale1 * xn, axis=0, keepdims=True)
        bstat_ref[0, 0:1, :] += jnp.sum(dh, axis=0, keepdims=True)
        bstat_ref[0, 1:2, :] += jnp.sum(dh * (xn * g_ref[...]), axis=0, keepdims=True)

    rope = _rope_spec(tps, tm)
    return pl.pallas_call(
        body, name="attn_in_bwd", grid=(T // tm,),
        in_specs=[_tok_spec(NA_WIDTH, tm), _tok_spec(NA_WIDTH, tm), _tok_spec(NA_WIDTH, tm), _tok_spec(SW_WIDTH, tm),
                  _tok_spec(SW_KV_WIDTH, tm), _tok_spec(SW_KV_WIDTH, tm), rope, rope, _full(w_in.shape), _tok_spec(D, tm),
                  _mod_spec(tps, D), _full((1, D)), _tok_spec(D, tm)],
        out_specs=[_tok_spec(D, tm), _tok_spec(IN_WIDTH, tm), _full((8, D)), _bstat_spec(tps, D)],
        out_shape=[jax.ShapeDtypeStruct((T, D), F32), jax.ShapeDtypeStruct((T, IN_WIDTH), BF16),
                   jax.ShapeDtypeStruct((8, D), F32), jax.ShapeDtypeStruct((B, 8, D), F32)],
        compiler_params=_params("arbitrary"),
    )(dqa, dka, dva, dqb, dkb, dvb, cos_t, sin_t, w_in, x2d, mod3, g_attn, dx1)


def _matmul_tn(a, b, name, tm=None, tk=512):
    T, M = a.shape
    N = b.shape[1]
    tm = M if tm is None else tm
    nk = T // tk

    def body(a_ref, b_ref, o_ref, acc):
        k = pl.program_id(1)

        @pl.when(k == 0)
        def _():
            acc[...] = jnp.zeros_like(acc)

        acc[...] += _tn(a_ref[...], b_ref[...])

        @pl.when(k == nk - 1)
        def _():
            o_ref[...] = acc[...].astype(BF16)

    return pl.pallas_call(
        body, name=name, grid=(M // tm, nk),
        in_specs=[pl.BlockSpec((tk, tm), lambda i, k: (k, i)), pl.BlockSpec((tk, N), lambda i, k: (k, 0))],
        out_specs=pl.BlockSpec((tm, N), lambda i, k: (i, 0)),
        out_shape=jax.ShapeDtypeStruct((M, N), BF16),
        scratch_shapes=[pltpu.VMEM((tm, N), F32)],
        compiler_params=_params("parallel", "arbitrary"),
    )(a, b)


def _na_geometry(S):
    rows = S // GRID_W
    wr = min(NA_ROWS_MAX, rows)
    return rows, wr


def _na_window(r, rows, wr):
    rs = jnp.clip(r - wr // 2, 0, rows - wr)
    return pl.multiple_of(rs * GRID_W, GRID_W), pl.multiple_of((rs - r + NA_ROWS_MAX - 1) * GRID_W, GRID_W)


NA_STEP_PAIRS = 2
NA_GW = NA_STEP_PAIRS * 128
NA_BWD_ROWS = 4
NA_ROWS_PER_STEP = 4


def _na_specs(S, kw_n, order):
    ng = NA_PAIRS // NA_STEP_PAIRS

    def col(k):
        return pl.BlockSpec((1, S, NA_GW), lambda *ids: (order(*ids)[0], 0, k * ng + order(*ids)[1]))
    bias = pl.BlockSpec((NA_STEP_PAIRS, N_DR * GRID_W, 128), lambda *ids: (order(*ids)[1], 0, 0))
    out = pl.BlockSpec((1, S, NA_GW), lambda *ids: (order(*ids)[0], 0, order(*ids)[1]))
    return col(0), col(1), col(2), bias, out


def _block_diag(t):
    left = lax.broadcasted_iota(jnp.int32, t.shape, 1) < HEAD_DIM
    zero = jnp.zeros_like(t)
    return jnp.concatenate([jnp.where(left, t, zero), jnp.where(left, zero, t)], axis=0)


def _diag_blocks(res):
    left = lax.broadcasted_iota(jnp.int32, (HEAD_DIM, 128), 1) < HEAD_DIM
    return jnp.where(left, res[:HEAD_DIM], res[HEAD_DIM:])


def _col_softmax(st):
    e = jnp.exp(st - jnp.max(st, axis=0, keepdims=True))
    return e * (1.0 / jnp.sum(e, axis=0, keepdims=True))


def _na_fwd(qkv, bias, tasks=()):
    B, S, _ = qkv.shape
    rows, wr = _na_geometry(S)
    kw_n = wr * GRID_W

    def body(q_ref, k_ref, v_ref, b_ref, o_ref):
        def step(it, carry):
            win = [_na_window(it * NA_ROWS_PER_STEP + u, rows, wr) for u in range(NA_ROWS_PER_STEP)]
            qrows = [pl.ds(pl.multiple_of((it * NA_ROWS_PER_STEP + u) * GRID_W, GRID_W), GRID_W) for u in range(NA_ROWS_PER_STEP)]
            krows = [pl.ds(w[0], kw_n) for w in win]
            brows = [pl.ds(w[1], kw_n) for w in win]
            lanes = [pl.ds(p * 128, 128) for p in range(NA_STEP_PAIRS)]
            chains = [(u, p) for u in range(NA_ROWS_PER_STEP) for p in range(NA_STEP_PAIRS)]
            st = {(u, p): _nt(k_ref[0, krows[u], lanes[p]], _block_diag(q_ref[0, qrows[u], lanes[p]])) for u, p in chains}
            pn = {(u, p): _col_softmax(st[(u, p)] + b_ref[p, brows[u], :]).astype(BF16) for u, p in chains}
            out = {(u, p): _diag_blocks(_tn(pn[(u, p)], v_ref[0, krows[u], lanes[p]])) for u, p in chains}
            for u in range(NA_ROWS_PER_STEP):
                o_ref[0, qrows[u], :] = jnp.concatenate([out[(u, p)] for p in range(NA_STEP_PAIRS)], axis=1)
            return carry

        lax.fori_loop(0, rows // NA_ROWS_PER_STEP, step, 0)

    q, k, v, bs, out = _na_specs(S, kw_n, lambda b, g: (b, g))
    return _hosted_call(body, "na_fwd", (B, NA_PAIRS // NA_STEP_PAIRS), [q, k, v, bs], [out],
                        [jax.ShapeDtypeStruct((B, S, NA_WIDTH), F32)], (qkv, qkv, qkv, bias), tasks)


def _na_bwd(qkv, bias, doa, tasks=()):
    B, S, _ = qkv.shape
    rows, wr = _na_geometry(S)
    kw_n = wr * GRID_W

    def body(q_ref, k_ref, v_ref, b_ref, do_ref, dq_ref, dk_ref, dv_ref, db_ref, dk_acc, dv_acc):
        @pl.when(pl.program_id(1) == 0)
        def _():
            db_ref[...] = jnp.zeros_like(db_ref)

        dk_acc[...] = jnp.zeros_like(dk_acc)
        dv_acc[...] = jnp.zeros_like(dv_acc)

        def step(it, carry):
            nu, pairs = range(NA_BWD_ROWS), range(NA_STEP_PAIRS)
            win = [_na_window(it * NA_BWD_ROWS + u, rows, wr) for u in nu]
            qrows = [pl.ds(pl.multiple_of((it * NA_BWD_ROWS + u) * GRID_W, GRID_W), GRID_W) for u in nu]
            krows = [pl.ds(w[0], kw_n) for w in win]
            brows = [pl.ds(w[1], kw_n) for w in win]
            lanes = [pl.ds(p * 128, 128) for p in pairs]
            chains = [(u, p) for u in nu for p in pairs]
            kp = {(u, p): k_ref[0, krows[u], lanes[p]] for u, p in chains}
            qbd = {(u, p): _block_diag(q_ref[0, qrows[u], lanes[p]]) for u, p in chains}
            dobd = {(u, p): _block_diag(do_ref[0, qrows[u], lanes[p]]) for u, p in chains}
            st = {c: _nt(kp[c], qbd[c]) for c in chains}
            dpt = {(u, p): _nt(v_ref[0, krows[u], lanes[p]], dobd[(u, p)]) for u, p in chains}
            pn = {(u, p): _col_softmax(st[(u, p)] + b_ref[p, brows[u], :]) for u, p in chains}
            dst = {c: pn[c] * (dpt[c] - jnp.sum(pn[c] * dpt[c], axis=0, keepdims=True)) for c in chains}
            dsb = {c: dst[c].astype(BF16) for c in chains}
            dq = {c: _diag_blocks(_tn(dsb[c], kp[c])) for c in chains}
            dk = {c: _nn(dsb[c], qbd[c]) for c in chains}
            dv = {c: _nn(pn[c].astype(BF16), dobd[c]) for c in chains}
            for u in nu:
                dq_ref[0, qrows[u], :] = jnp.concatenate([dq[(u, p)] for p in pairs], axis=1).astype(BF16)
                dk_acc[krows[u], :] += jnp.concatenate([dk[(u, p)] for p in pairs], axis=1)
                dv_acc[krows[u], :] += jnp.concatenate([dv[(u, p)] for p in pairs], axis=1)
                for p in pairs:
                    db_ref[p, brows[u], :] += dst[(u, p)]
            return carry

        lax.fori_loop(0, rows // NA_BWD_ROWS, step, 0)

        def emit(i, carry):
            r = pl.ds(pl.multiple_of(i * 256, 256), 256)
            dk_ref[0, r, :] = dk_acc[r, :].astype(BF16)
            dv_ref[0, r, :] = dv_acc[r, :].astype(BF16)
            return carry

        lax.fori_loop(0, S // 256, emit, 0)

    q, k, v, bs, out = _na_specs(S, kw_n, lambda g, b: (b, g))
    hs = jax.ShapeDtypeStruct((B, S, NA_WIDTH), BF16)
    return _hosted_call(body, "na_bwd", (NA_PAIRS // NA_STEP_PAIRS, B), [q, k, v, bs, out], [out, out, out, bs],
                        [hs, hs, hs, jax.ShapeDtypeStruct((NA_PAIRS, N_DR * GRID_W, 128), F32)], (qkv, qkv, qkv, bias, doa), tasks,
                        scratch_shapes=[pltpu.VMEM((S, NA_GW), F32), pltpu.VMEM((S, NA_GW), F32)])


SW_PAIRS = SW_HEADS // 2


def _sw_band(n, S):
    kw_n = 3 * SW_BLOCK
    start = pl.multiple_of(jnp.clip(n * SW_BLOCK - SW_BLOCK, 0, S - kw_n), SW_BLOCK)
    kpos = start + lax.broadcasted_iota(jnp.int32, (kw_n, SW_BLOCK), 0)
    qpos = n * SW_BLOCK + lax.broadcasted_iota(jnp.int32, (kw_n, SW_BLOCK), 1)
    return start, jnp.abs(qpos - kpos) <= SW_WINDOW


def _kv_halves(t):
    left = lax.broadcasted_iota(jnp.int32, t.shape, 1) < HEAD_DIM
    swapped = pltpu.roll(t, HEAD_DIM, axis=1)
    zero = jnp.zeros_like(t)
    return {(0, 0): jnp.where(left, t, zero), (0, 1): jnp.where(left, zero, swapped),
            (1, 0): jnp.where(left, swapped, zero), (1, 1): jnp.where(left, zero, t)}


def _sw_probs(st, ok, sk):
    st = jnp.where(ok, st, NEG)
    m = jnp.maximum(jnp.max(st, axis=0, keepdims=True), sk)
    e = jnp.exp(st - m)
    esk = jnp.exp(sk - m)
    inv = 1.0 / (jnp.sum(e, axis=0, keepdims=True) + esk)
    return e * inv, esk * inv


def _sw_specs(S):
    q = pl.BlockSpec((1, S, SW_WIDTH), lambda b, *_: (b, 0, ROPE_LO // SW_WIDTH))
    k = pl.BlockSpec((1, S, SW_KV_WIDTH), lambda b, *_: (b, 0, (ROPE_LO + SW_WIDTH) // SW_KV_WIDTH))
    v = pl.BlockSpec((1, S, SW_KV_WIDTH), lambda b, *_: (b, 0, (ROPE_LO + ROPE_WIDTH) // SW_KV_WIDTH))
    return q, k, v


SW_FWD_SPLIT = 2


def _sw_fwd(sink, qkv, tasks=()):
    B, S, _ = qkv.shape
    kw_n = 3 * SW_BLOCK

    def body(sink_ref, q_ref, k_ref, v_ref, o_ref):
        def step(n, carry):
            start, ok = _sw_band(n, S)
            qrows = pl.ds(pl.multiple_of(n * SW_BLOCK, SW_BLOCK), SW_BLOCK)
            krows = pl.ds(start, kw_n)
            kh, vh = _kv_halves(k_ref[0, krows, :]), _kv_halves(v_ref[0, krows, :])
            heads = [(p, e) for p in range(SW_PAIRS) for e in range(2)]
            qp = [q_ref[0, qrows, pl.ds(p * 128, 128)] for p in range(SW_PAIRS)]
            kv_of = lambda p: p // (SW_PAIRS // SW_KV_HEADS)
            st = {(p, e): _nt(kh[(kv_of(p), e)], qp[p]) for p, e in heads}
            pn = {(p, e): _sw_probs(st[(p, e)], ok, sink_ref[2 * p + e])[0].astype(BF16) for p, e in heads}
            outs = [_tn(pn[(p, 0)], vh[(kv_of(p), 0)]) + _tn(pn[(p, 1)], vh[(kv_of(p), 1)]) for p in range(SW_PAIRS)]
            o_ref[0, qrows, :] = jnp.concatenate(outs, axis=1)
            return carry

        half = (S // SW_BLOCK) // SW_FWD_SPLIT
        lax.fori_loop(pl.program_id(1) * half, (pl.program_id(1) + 1) * half, step, 0)

    q, k, v = _sw_specs(S)
    return _hosted_call(
        body, "sw_fwd", (B, SW_FWD_SPLIT), [pl.BlockSpec(memory_space=pltpu.SMEM), q, k, v],
        [pl.BlockSpec((1, S, SW_WIDTH), lambda b, s: (b, 0, 0))], [jax.ShapeDtypeStruct((B, S, SW_WIDTH), F32)],
        (sink, qkv, qkv, qkv), tasks)


def _sw_bwd(sink, qkv, dob):
    B, S, _ = qkv.shape
    kw_n = 3 * SW_BLOCK

    fold_rows = 256

    def body(sink_ref, q_ref, k_ref, v_ref, do_ref, dq_ref, dk_ref, dv_ref, dsink_ref, dk_acc, dv_acc):
        @pl.when(pl.program_id(0) == 0)
        def _():
            dsink_ref[...] = jnp.zeros_like(dsink_ref)

        dk_acc[...] = jnp.zeros_like(dk_acc)
        dv_acc[...] = jnp.zeros_like(dv_acc)
        ppk = SW_PAIRS // SW_KV_HEADS

        def step(n, carry):
            start, ok = _sw_band(n, S)
            qrows = pl.ds(pl.multiple_of(n * SW_BLOCK, SW_BLOCK), SW_BLOCK)
            krows = pl.ds(start, kw_n)
            kh, vh = _kv_halves(k_ref[0, krows, :]), _kv_halves(v_ref[0, krows, :])
            heads = [(p, e) for p in range(SW_PAIRS) for e in range(2)]
            qp = [q_ref[0, qrows, pl.ds(p * 128, 128)] for p in range(SW_PAIRS)]
            dop = [do_ref[0, qrows, pl.ds(p * 128, 128)] for p in range(SW_PAIRS)]
            st = {(p, e): _nt(kh[(p // ppk, e)], qp[p]) for p, e in heads}
            dpt = {(p, e): _nt(vh[(p // ppk, e)], dop[p]) for p, e in heads}
            pnb, dsb = {}, {}
            for p, e in heads:
                pn, psink = _sw_probs(st[(p, e)], ok, sink_ref[2 * p + e])
                delta = jnp.sum(pn * dpt[(p, e)], axis=0, keepdims=True)
                dsb[(p, e)] = (pn * (dpt[(p, e)] - delta)).astype(BF16)
                pnb[(p, e)] = pn.astype(BF16)
                dsink_ref[2 * p + e:2 * p + e + 1, :] += -(psink * delta)
            dq_ref[0, qrows, :] = jnp.concatenate(
                [_tn(dsb[(p, 0)], kh[(p // ppk, 0)]) + _tn(dsb[(p, 1)], kh[(p // ppk, 1)]) for p in range(SW_PAIRS)],
                axis=1).astype(BF16)
            left = lax.broadcasted_iota(jnp.int32, (kw_n, 128), 1) < HEAD_DIM
            dks, dvs = [], []
            for kv in range(SW_KV_HEADS):
                dk = dv = None
                for p in range(kv * ppk, (kv + 1) * ppk):
                    dk_p = jnp.where(left, _nn(dsb[(p, 0)], qp[p]), _nn(dsb[(p, 1)], qp[p]))
                    dv_p = jnp.where(left, _nn(pnb[(p, 0)], dop[p]), _nn(pnb[(p, 1)], dop[p]))
                    dk = dk_p if dk is None else dk + dk_p
                    dv = dv_p if dv is None else dv + dv_p
                dks.append(dk)
                dvs.append(dv)
            dk_acc[krows, :] += jnp.concatenate(dks, axis=1)
            dv_acc[krows, :] += jnp.concatenate(dvs, axis=1)
            return carry

        lax.fori_loop(0, S // SW_BLOCK, step, 0)

        def fold(i, carry):
            rows = pl.ds(pl.multiple_of(i * fold_rows, fold_rows), fold_rows)
            left = lax.broadcasted_iota(jnp.int32, (fold_rows, 128), 1) < HEAD_DIM
            for acc, out_ref in ((dk_acc, dk_ref), (dv_acc, dv_ref)):
                a, b = acc[rows, 0:128], acc[rows, 128:256]
                out_ref[0, rows, :] = jnp.where(left, a + pltpu.roll(a, HEAD_DIM, axis=1),
                                                b + pltpu.roll(b, HEAD_DIM, axis=1)).astype(BF16)
            return carry

        lax.fori_loop(0, S // fold_rows, fold, 0)

        @pl.when(pl.program_id(0) == B - 1)
        def _():
            dsink_ref[...] = jnp.broadcast_to(jnp.sum(dsink_ref[...], axis=1, keepdims=True), dsink_ref.shape)

    q, k, v = _sw_specs(S)
    qo = pl.BlockSpec((1, S, SW_WIDTH), lambda b: (b, 0, 0))
    ko = pl.BlockSpec((1, S, SW_KV_WIDTH), lambda b: (b, 0, 0))
    return pl.pallas_call(
        body, name="sw_bwd", grid=(B,),
        in_specs=[pl.BlockSpec(memory_space=pltpu.SMEM), q, k, v, qo],
        out_specs=[qo, ko, ko, _full((SW_HEADS, 128))],
        out_shape=[jax.ShapeDtypeStruct((B, S, SW_WIDTH), BF16), jax.ShapeDtypeStruct((B, S, SW_KV_WIDTH), BF16),
                   jax.ShapeDtypeStruct((B, S, SW_KV_WIDTH), BF16), jax.ShapeDtypeStruct((SW_HEADS, 128), F32)],
        scratch_shapes=[pltpu.VMEM((S, 2 * SW_KV_WIDTH), F32), pltpu.VMEM((S, 2 * SW_KV_WIDTH), F32)],
        compiler_params=_params("arbitrary"),
    )(sink, qkv, qkv, qkv, dob)


def _pack_sum_adamw(packs, params, pick):
    W = packs.shape[1]
    n_p = len(params)

    def body(p_ref, *refs):
        ins, tot_ref, pick_ref, outs = refs[:3 * n_p], refs[3 * n_p], refs[3 * n_p + 1], refs[3 * n_p + 2:]
        tot = p_ref[0:8, :]
        for d in range(1, N_DEV):
            tot = tot + p_ref[8 * d:8 * d + 8, :]
        tot_ref[...] = tot
        pick_ref[...] = tot[pick[0]:pick[0] + 1, pick[1]:pick[1] + 1]
        for i, (w, _, _, rows, off) in enumerate(params):
            w_ref, m_ref, v_ref = ins[3 * i:3 * i + 3]
            g_ref, d_ref, nm_ref, nv_ref = outs[4 * i:4 * i + 4]
            if w.ndim == 3:
                n_a, n_b, n = w.shape
                for a in range(n_a):
                    for b in range(n_b):
                        o = off + (b * n_a + a) * n
                        g_ref[a, b:b + 1, :] = tot_ref[rows[0]:rows[0] + 1, o:o + n]
                g = g_ref[...]
            else:
                n = w.shape[1]
                g = tot[rows[0]:rows[0] + 1, off:off + n]
                for r in rows[1:]:
                    g = g + tot[r:r + 1, off:off + n]
                g_ref[...] = g
            d_ref[...], nm_ref[...], nv_ref[...] = _adam_update(w_ref[...], g, m_ref[...], v_ref[...])

    res = pl.pallas_call(
        body, name="small_adamw",
        out_shape=[jax.ShapeDtypeStruct((8, W), F32), jax.ShapeDtypeStruct((1, 1), F32)]
        + [jax.ShapeDtypeStruct(p[0].shape, F32) for p in params for _ in range(4)],
        compiler_params=pltpu.CompilerParams(vmem_limit_bytes=VMEM_LIMIT),
    )(packs, *[a for p in params for a in p[:3]])
    return res[0], res[1], [res[2 + 4 * i:6 + 4 * i] for i in range(n_p)]


def _adam_update(w, g, m, v):
    c1 = 1.0 - ADAM_B1 ** ADAM_STEP
    c2 = 1.0 - ADAM_B2 ** ADAM_STEP
    nm = ADAM_B1 * m + (1.0 - ADAM_B1) * g
    nv = ADAM_B2 * v + (1.0 - ADAM_B2) * (g * g)
    return -ADAM_LR * ((nm / c1) / (jnp.sqrt(nv / c2) + ADAM_EPS) + ADAM_WD * w), nm, nv


def _adamw(w, g, m, v, name):
    def body(w_ref, g_ref, m_ref, v_ref, d_ref, nm_ref, nv_ref):
        d_ref[...], nm_ref[...], nv_ref[...] = _adam_update(w_ref[...], g_ref[...], m_ref[...], v_ref[...])

    s = jax.ShapeDtypeStruct(w.shape, F32)
    return pl.pallas_call(body, name=name, out_shape=[s, s, s],
                          compiler_params=pltpu.CompilerParams(vmem_limit_bytes=VMEM_LIMIT))(w, g, m, v)


def _sum_adamw_rows(R):
    return max(r for r in range(16, min(R, 256) + 1, 16) if R % r == 0)


def _sum_adamw_steps(R):
    return R // _sum_adamw_rows(R)


def _sum_first(own, recvb, name, tasks):
    R, C = own.shape
    rc = _sum_adamw_rows(R)

    def body(own_ref, r_ref, p_ref):
        p_ref[...] = (own_ref[...] + r_ref[0].astype(F32)) + r_ref[1].astype(F32)

    blk = pl.BlockSpec((rc, C), lambda i: (i, 0))
    (part,), got = _hosted_call(body, name, (R // rc,), [blk, pl.BlockSpec((2, rc, C), lambda i: (0, i, 0))], [blk],
                                [jax.ShapeDtypeStruct((R, C), F32)], (own, recvb), tasks)
    return part, got


def _sum_adamw(own, recvb, w, m, v, name, done=0):
    R, C = own.shape
    rc = _sum_adamw_rows(R)
    left = 3 - done
    assert 3 % left == 0

    def body(own_ref, r_ref, w_ref, m_ref, v_ref, g_ref, d_ref, nm_ref, nv_ref):
        g = own_ref[...]
        for j in range(left):
            g = g + r_ref[j].astype(F32)
        g_ref[...] = g
        d_ref[...], nm_ref[...], nv_ref[...] = _adam_update(w_ref[...], g, m_ref[...], v_ref[...])

    blk = pl.BlockSpec((rc, C), lambda i: (i, 0))
    s = jax.ShapeDtypeStruct((R, C), F32)
    return pl.pallas_call(
        body, name=name, grid=(R // rc,),
        in_specs=[blk, pl.BlockSpec((left, rc, C), lambda i: (done // left, i, 0)), blk, blk, blk],
        out_specs=[blk, blk, blk, blk], out_shape=[s, s, s, s], compiler_params=_params("parallel"),
    )(own, recvb, w, m, v)


def _by_device(dw):
    return dw.reshape(N_DEV, dw.shape[0] // N_DEV, dw.shape[1])


def _local_step(x, mod, g_attn, w_in, bias, sw_sink, g_na_out, g_sw_out, w_out, g_ffn, w_up, conv_w, conv_b, w_down,
                g_final, target, sharded):
    B, S, D = x.shape
    T = B * S
    x2d = x.reshape(T, D)
    mod3 = mod.reshape(B, 6, D)
    cos_t, sin_t = _rope_tables(S)
    sink = sw_sink.reshape(SW_HEADS)
    n_tiles = T // WIDE_TILE
    full = lambda g: g.reshape(N_DEV * g.shape[1], g.shape[2])

    rider = lambda w, mid, lo, n, into=None: [_gather_task(w, mid, rows=(lo, n), into=into)] if sharded else []
    if sharded:
        qu, hd = w_up.shape[0] // 4, w_down.shape[0] // 2
    (h, qkv), got = _attn_in(x2d, mod3, g_attn, w_in, cos_t, sin_t, S, rider(w_up, 3 * n_tiles // 4, 0, qu) if sharded else [])
    if sharded:
        w_up_buf = got[0][0]
    qkv3 = qkv.reshape(B, S, IN_WIDTH)
    na_steps, sw_steps = B * (NA_PAIRS // NA_STEP_PAIRS), B * SW_FWD_SPLIT
    (oa,), got = _na_fwd(qkv3, bias, rider(w_up, na_steps - 1, qu, 2 * qu, w_up_buf) if sharded else [])
    if sharded:
        w_up_buf = got[0][0]
    oa = oa.reshape(T, NA_WIDTH)
    (ob,), got = _sw_fwd(sink, qkv3,
                         rider(w_up, sw_steps // 2, 3 * qu, qu, w_up_buf) + [_gather_task(w_out, sw_steps - 1)] if sharded else [])
    if sharded:
        w_up, w_out = full(got[0][0]), full(got[1][0])
    ob = ob.reshape(T, SW_WIDTH)
    (mixin, mix, x1), _ = _attn_out(oa, ob, x2d, mod3, g_na_out, g_sw_out, w_out, S)
    (h2, val, gt), got = _ffn_up(x1, mod3, g_ffn, w_up, S, rider(w_down, 3 * n_tiles // 4, 0, 2 * hd) if sharded else [])
    if sharded:
        w_down = full(got[0][0])
    a, act, vd, dx2, df, gstat_f, bstat_f = _ffn_down(gt, val, conv_w, conv_b, w_down, x1, mod3, g_final, target.reshape(T, D), B, S)
    F = val.shape[1]

    dw_down = _matmul_tn(a, df, "dw_down")
    (dval, dgc, cstat), got = _ffn_down_bwd(df, w_down, act, vd, [_swap_task(_by_device(dw_down))] if sharded else [])
    if sharded:
        send_down, own_down = _chip_sums(_by_device(dw_down), got[0][0])
    (du, dx1, dmix, gstat_u, bstat_u, cstat_w), got = _ffn_up_bwd(dgc, dval, gt, conv_w, w_up, x1, mod3, g_ffn, dx2, mix, B, S,
                                                                  [_exchange_task(send_down)] if sharded else [])
    if sharded:
        dw_down = (own_down, got[0][0])
    dw_up = _matmul_tn(du, h2, "dw_up", tm=F)
    dw_out = _matmul_tn(mixin, dmix, "dw_out")
    (doa, dob, gstat_o), got = _attn_out_bwd(dmix, w_out, oa, ob, g_na_out, g_sw_out,
                                             [_swap_task(_by_device(dw_up)), _swap_task(_by_device(dw_out))] if sharded else [])
    if sharded:
        send_up, own_up = _chip_sums(_by_device(dw_up), got[0][0])
        send_out, own_out = _chip_sums(_by_device(dw_out), got[1][0])
    (dqa, dka, dva, dbt), got = _na_bwd(qkv3, bias, doa.reshape(B, S, NA_WIDTH),
                                        [_exchange_task(send_up), _exchange_task(send_out)] if sharded else [])
    if sharded:
        dw_up, dw_out = (own_up, got[0][0]), (own_out, got[1][0])
    dqb, dkb, dvb, dsink = _sw_bwd(sink, qkv3, dob.reshape(B, S, SW_WIDTH))
    r2 = lambda t: t.reshape(T, t.shape[-1])
    grad_x, dproj, gstat_i, bstat_i = _attn_in_bwd(r2(dqa), r2(dka), r2(dva), r2(dqb), r2(dkb), r2(dvb), cos_t, sin_t, w_in, x2d, mod3,
                                                   g_attn, dx1, B, S)
    dw_in = _matmul_tn(dproj, h, "dw_in")

    dmod = jnp.stack([bstat_i[:, 0], bstat_i[:, 1], bstat_u[:, 2], bstat_u[:, 0], bstat_u[:, 1], bstat_f[:, 0]], axis=1)
    small = dict(g_attn=gstat_i[0], g_ffn=gstat_u[0], g_final=gstat_f[0], loss=gstat_f[1, 0], g_na_out=gstat_o[0], g_sw_out=gstat_o[1],
                 sw_sink=dsink[:, 0], conv_b=cstat[0], conv_w=cstat_w[1:4], dbt=dbt,
                 raw=(bstat_i, bstat_u, bstat_f, gstat_i, gstat_u, gstat_f, gstat_o, dsink, cstat, cstat_w))
    return grad_x.reshape(B, S, D), dict(w_in=dw_in, w_out=dw_out, w_up=dw_up, w_down=dw_down), dmod, small


def _pack_slab(raw, drpb):
    D, F = raw[3].shape[1], raw[8].shape[1]
    n_seq = raw[0].shape[0]

    def body(bi_ref, bu_ref, bf_ref, gi_ref, gu_ref, gf_ref, go_ref, ds_ref, cs_ref, cw_ref, rp_ref, o_ref):
        o_ref[...] = jnp.zeros_like(o_ref)
        for b in range(n_seq):
            mods = (bi_ref[b, 0:1, :], bi_ref[b, 1:2, :], bu_ref[b, 2:3, :], bu_ref[b, 0:1, :], bu_ref[b, 1:2, :], bf_ref[b, 0:1, :])
            for k, row in enumerate(mods):
                o_ref[b:b + 1, k * D:(k + 1) * D] = row
        o = 0
        for row in (gi_ref[0:1, :], gu_ref[0:1, :], gf_ref[0:1, :], go_ref[0:1, :], go_ref[1:2, :]):
            o_ref[2:3, o:o + row.shape[1]] = row
            o += row.shape[1]
        ds = ds_ref[...]
        eye = lax.broadcasted_iota(jnp.int32, ds.shape, 0) == lax.broadcasted_iota(jnp.int32, ds.shape, 1)
        o_ref[2:3, o:o + 128] = jnp.sum(jnp.where(eye, ds, 0.0), axis=0, keepdims=True)
        o_ref[2:3, o + 128:o + 256] = gf_ref[1:2, 0:128]
        o_ref[3:4, 0:F] = cs_ref[0:1, :]
        o_ref[4:5, 0:rp_ref.shape[1]] = rp_ref[...]
        o_ref[5:8, 0:F] = cw_ref[1:4, :]

    return pl.pallas_call(body, name="pack_slab", out_shape=jax.ShapeDtypeStruct((8, PACK_W), F32),
                          compiler_params=pltpu.CompilerParams(vmem_limit_bytes=VMEM_LIMIT))(*raw, drpb)


def kernel(x, c, w_ada, b_ada, g_attn, w_in, na_rpb, sw_sink, g_na_out, g_sw_out, w_out, g_ffn, w_up, conv_w, conv_b, w_down, g_final, loss_target, m_w_ada, m_b_ada, m_g_attn, m_w_in, m_na_rpb, m_sw_sink, m_g_na_out, m_g_sw_out, m_w_out, m_g_ffn, m_w_up, m_conv_w, m_conv_b, m_w_down, m_g_final, v_w_ada, v_b_ada, v_g_attn, v_w_in, v_na_rpb, v_sw_sink, v_g_na_out, v_g_sw_out, v_w_out, v_g_ffn, v_w_up, v_conv_w, v_conv_b, v_w_down, v_g_final):
    B, S, D = x.shape
    me = 4 * lax.axis_index("x") + 2 * lax.axis_index("y") + lax.axis_index("c")
    ada_c = w_ada.shape[2]
    F_l = conv_w.shape[2]

    tr = {"w_in", "w_up"}
    w_in_t = jnp.transpose(w_in[0])
    shards_f = dict(w_out=w_out[0], w_up=jnp.transpose(w_up[0]), w_down=w_down[0])

    c_all, conv_w_f, mod_all, w_in_all, casts, bias = _ada_fwd(c, conv_w[0], w_ada[0], b_ada, w_in_t, list(shards_f.values()),
                                                               _na_bias_rev(na_rpb[0]))
    shards = dict(zip(shards_f, casts))
    mod_mine = lax.dynamic_slice(mod_all, (0, me * B, 0), (N_DEV, B, ada_c))
    mod = jnp.transpose(mod_mine, (1, 0, 2)).reshape(B, N_DEV * ada_c)
    w_in_f = w_in_all.reshape(N_DEV * w_in_t.shape[0], D)

    grad_x, dw, dmod, small = _local_step(x, mod, g_attn, w_in_f, bias, sw_sink, g_na_out, g_sw_out, shards["w_out"], g_ffn,
                                          shards["w_up"], conv_w_f, conv_b, shards["w_down"], g_final.reshape(1, D), loss_target,
                                          sharded=True)
    g8_in = _by_device(dw["w_in"])
    drpb, _ = _na_bias_grad(small["dbt"])

    slab = _pack_slab(small["raw"], drpb.reshape(1, -1))
    weights = dict(w_ada=w_ada, b_ada=b_ada, g_attn=g_attn, w_in=w_in, na_rpb=na_rpb, sw_sink=sw_sink, g_na_out=g_na_out,
                   g_sw_out=g_sw_out, w_out=w_out, g_ffn=g_ffn, w_up=w_up, conv_w=conv_w, conv_b=conv_b, w_down=w_down, g_final=g_final)
    ms = dict(w_ada=m_w_ada, b_ada=m_b_ada, g_attn=m_g_attn, w_in=m_w_in, na_rpb=m_na_rpb, sw_sink=m_sw_sink, g_na_out=m_g_na_out,
              g_sw_out=m_g_sw_out, w_out=m_w_out, g_ffn=m_g_ffn, w_up=m_w_up, conv_w=m_conv_w, conv_b=m_conv_b, w_down=m_w_down, g_final=m_g_final)
    vs = dict(w_ada=v_w_ada, b_ada=v_b_ada, g_attn=v_g_attn, w_in=v_w_in, na_rpb=v_na_rpb, sw_sink=v_sw_sink, g_na_out=v_g_na_out,
              g_sw_out=v_g_sw_out, w_out=v_w_out, g_ffn=v_g_ffn, w_up=v_w_up, conv_w=v_conv_w, conv_b=v_conv_b, w_down=v_w_down, g_final=v_g_final)
    names = list(weights)
    grads, deltas, new_m, new_v = {}, {}, {}, {}
    flat = lambda t: t.reshape(1, -1)

    def shard2d(nm):
        if nm in tr:
            return (lambda t: jnp.transpose(t[0])), (lambda t: jnp.transpose(t)[None])
        if nm == "conv_w":
            return (lambda t: jnp.transpose(t, (1, 0, 2))), (lambda t: jnp.transpose(t, (1, 0, 2)))
        return (lambda t: t[0]), (lambda t: t[None])

    def finish_sum(nm, own, recvb, done=0):
        r, back = shard2d(nm)
        g2, d_, m_, v_ = _sum_adamw(own, recvb, r(weights[nm]), r(ms[nm]), r(vs[nm]), "adamw_" + nm, done)
        grads[nm], deltas[nm], new_m[nm], new_v[nm] = back(g2), back(d_), back(m_), back(v_)

    own_in, recv_in, packs = _reduce_scatter_call(g8_in, slab)
    finish_sum("w_up", *dw["w_up"])
    finish_sum("w_down", *dw["w_down"])
    finish_sum("w_in", own_in, recv_in)
    finish_sum("w_out", *dw["w_out"])

    where = dict(b_ada=((0, 1), 0), g_attn=((2,), 0), g_ffn=((2,), D), g_final=((2,), 2 * D), g_na_out=((2,), 3 * D),
                 g_sw_out=((2,), 3 * D + NA_WIDTH), sw_sink=((2,), 3 * D + NA_WIDTH + SW_WIDTH), conv_b=((3,), 0), na_rpb=((4,), 0))

    def small_view(n):
        if n == "na_rpb":
            return (lambda t: jnp.transpose(t[0], (1, 0, 2))), (lambda t: jnp.transpose(t, (1, 0, 2))[None])
        return flat, (lambda t: t.reshape(weights[n].shape))

    tot, loss, small_out = _pack_sum_adamw(
        packs.reshape(N_DEV * 8, PACK_W),
        [tuple(small_view(n)[0](t[n]) for t in (weights, ms, vs)) + where[n] for n in where],
        (2, 3 * D + NA_WIDTH + SW_WIDTH + 128))
    for n, res in zip(where, small_out):
        grads[n], deltas[n], new_m[n], new_v[n] = [small_view(n)[1](t) for t in res]
    loss = loss.reshape(())

    for nm, g2 in (("w_ada", _ada_bwd(c_all, packs.reshape(N_DEV * 8, PACK_W), ada_c)), ("conv_w", lax.dynamic_slice(tot, (5, me * F_l), (3, F_l))[:, None])):
        r, back = shard2d(nm)
        d_, m_, v_ = _adamw(r(weights[nm]), g2, r(ms[nm]), r(vs[nm]), "adamw_" + nm)
        grads[nm], deltas[nm], new_m[nm], new_v[nm] = back(g2), back(d_), back(m_), back(v_)
    return (loss, grad_x, *[grads[n] for n in names], *[deltas[n] for n in names], *[new_m[n] for n in names],
            *[new_v[n] for n in names])
```

```python
import functools

import numpy as np
import jax
import jax.numpy as jnp
from jax import lax
from jax.experimental import pallas as pl
from jax.experimental.pallas import tpu as pltpu

F32, BF16 = jnp.float32, jnp.bfloat16
MESH_ID = pl.DeviceIdType.MESH
N_DEV = 8

HEAD_DIM = 64
NA_HEADS = 8
SW_HEADS = 8
SW_KV_HEADS = 2
SW_GROUP = SW_HEADS // SW_KV_HEADS
NA_WIDTH = NA_HEADS * HEAD_DIM
SW_WIDTH = SW_HEADS * HEAD_DIM
SW_KV_WIDTH = SW_KV_HEADS * HEAD_DIM
ROPE_WIDTH = SW_WIDTH + SW_KV_WIDTH
IN_WIDTH = 3 * NA_WIDTH + SW_WIDTH + 2 * SW_KV_WIDTH
ROPE_LO = 3 * NA_WIDTH
GRID_W = 64
NA_ROWS_MAX = 8
NA_COLS = 16
N_DR = 2 * NA_ROWS_MAX - 1
N_DC = 2 * NA_COLS - 1
SW_WINDOW = 128
SW_BLOCK = 128
ROPE_THETA = 10000.0
EPS = 1e-6
NEG = -1e30
Q_SCALE = HEAD_DIM ** -0.5

ADAM_LR = 0.001
ADAM_B1 = 0.9
ADAM_B2 = 0.999
ADAM_EPS = 1e-08
ADAM_WD = 0.01
ADAM_STEP = 10

TOKEN_TILE = 256
WIDE_TILE = 512
VMEM_LIMIT = 56 * 1024 * 1024

PACK_W = 6144


def _nn(a, b):
    return jnp.dot(a, b, preferred_element_type=F32)


def _nt(a, b):
    return lax.dot_general(a, b, (((1,), (1,)), ((), ())), preferred_element_type=F32)


def _tn(a, b):
    return lax.dot_general(a, b, (((0,), (0,)), ((), ())), preferred_element_type=F32)


def _rms(x):
    r = lax.rsqrt(jnp.mean(x * x, axis=-1, keepdims=True) + EPS)
    return x * r, r


def _rms_bwd(xn, r, gy):
    return r * (gy - xn * jnp.mean(xn * gy, axis=-1, keepdims=True))


def _params(*sem):
    return pltpu.CompilerParams(dimension_semantics=sem, vmem_limit_bytes=VMEM_LIMIT)


def _full(shape):
    n = len(shape)
    return pl.BlockSpec(shape, lambda *_: (0,) * n)


def _mesh_pos():
    return lax.axis_index("x"), lax.axis_index("y"), lax.axis_index("c")


def _row_chunk(r):
    for rc in (128, 64, 32, 16):
        if r % rc == 0:
            return rc
    raise ValueError(f"rows {r} not a multiple of 16")


class _Task:
    def __init__(self, inputs, out_shapes, sems, start, finish, mid=None, mid_step=None, alias=None):
        self.inputs, self.out_shapes, self.sems = list(inputs), list(out_shapes), list(sems)
        self.start, self.finish, self.mid, self.mid_step = start, finish, mid, mid_step
        self.alias = alias


def _hosted_call(body, name, grid, in_specs, out_specs, out_shape, operands, tasks, scratch_shapes=()):
    n_in, n_out, n_scr = len(in_specs), len(out_specs), len(scratch_shapes)
    t_in = [len(t.inputs) for t in tasks]
    t_out = [len(t.out_shapes) for t in tasks]
    t_sem = [len(t.sems) for t in tasks]
    n_steps = int(np.prod(grid))

    def wrapped(*refs):
        ins, rest = refs[:n_in], refs[n_in:]
        task_ins, rest = rest[:sum(t_in)], rest[sum(t_in):]
        outs, rest = rest[:n_out], rest[n_out:]
        task_outs, rest = rest[:sum(t_out)], rest[sum(t_out):]
        scr, task_sems = rest[:n_scr], rest[n_scr:]
        step = pl.program_id(0)
        for ax in range(1, len(grid)):
            step = step * grid[ax] + pl.program_id(ax)
        parts = []
        oi = oo = os_ = 0
        for t, a, b, c in zip(tasks, t_in, t_out, t_sem):
            parts.append((t, task_ins[oi:oi + a], task_outs[oo:oo + b], task_sems[os_:os_ + c]))
            oi, oo, os_ = oi + a, oo + b, os_ + c
        for t, ti, to, ts in parts:
            pl.when(step == 0)(functools.partial(t.start, ti, to, ts))
            if t.mid is not None:
                pl.when(step == t.mid_step)(functools.partial(t.mid, ti, to, ts))
        body(*ins, *outs, *scr)
        for t, ti, to, ts in parts:
            pl.when(step == n_steps - 1)(functools.partial(t.finish, ti, to, ts))

    hbm = pl.BlockSpec(memory_space=pl.ANY)
    aliases, oi, oo = {}, n_in, n_out
    for t, a, b in zip(tasks, t_in, t_out):
        if t.alias is not None:
            aliases[oi + t.alias[0]] = oo + t.alias[1]
        oi, oo = oi + a, oo + b
    res = pl.pallas_call(
        wrapped, name=name, grid=grid,
        in_specs=list(in_specs) + [hbm] * sum(t_in),
        out_specs=list(out_specs) + [hbm] * sum(t_out),
        out_shape=list(out_shape) + [s for t in tasks for s in t.out_shapes],
        scratch_shapes=list(scratch_shapes) + [s for t in tasks for s in t.sems],
        input_output_aliases=aliases,
        compiler_params=_params(*(["arbitrary"] * len(grid))),
    )(*operands, *[a for t in tasks for a in t.inputs])
    own, extra = res[:n_out], res[n_out:]
    per_task, o = [], 0
    for b in t_out:
        per_task.append(extra[o:o + b])
        o += b
    return own, per_task


def _gather_task(shard, mid_step, rows=None, into=None):
    lo, n = (0, shard.shape[0]) if rows is None else rows

    def parts(ins, outs, sems):
        x_ref, out_ref, (send_sems, recv_sems, local_sem) = ins[0], outs[0], sems
        x_, y_, c_ = _mesh_pos()
        me, sibling = (x_, y_, c_), (x_, y_, 1 - c_)
        chips = [(1 - x_, y_), (x_, 1 - y_), (1 - x_, 1 - y_)]
        x_ref = x_ref.at[pl.ds(lo, n)]

        def rows(px, py, pc):
            return out_ref.at[4 * px + 2 * py + pc, pl.ds(lo, n)]

        def copy(k, block, to, src=None):
            return pltpu.make_async_remote_copy(
                src_ref=rows(*block) if src is None else src, dst_ref=rows(*block),
                send_sem=send_sems.at[k], recv_sem=recv_sems.at[k], device_id=to, device_id_type=MESH_ID)

        return dict(
            mine=lambda: pltpu.make_async_copy(x_ref, rows(*me), local_sem),
            first=lambda: [copy(0, me, sibling, src=x_ref)] + [copy(1 + j, me, (*chip, c_), src=x_ref) for j, chip in enumerate(chips)],
            passed=lambda: [copy(4 + j, (*chip, c_), sibling) for j, chip in enumerate(chips)],
            landed=lambda: [copy(1 + j, (*chip, c_), me) for j, chip in enumerate(chips)],
            last=lambda: [copy(0, sibling, me)] + [copy(4 + j, (*chip, 1 - c_), me) for j, chip in enumerate(chips)])

    def start(ins, outs, sems):
        p = parts(ins, outs, sems)
        p["mine"]().start()
        for cp in p["first"]():
            cp.start()

    def mid(ins, outs, sems):
        p = parts(ins, outs, sems)
        for cp, fw in zip(p["landed"](), p["passed"]()):
            cp.wait_recv()
            fw.start()

    def finish(ins, outs, sems):
        p = parts(ins, outs, sems)
        for cp in p["last"]():
            cp.wait_recv()
        for cp in p["first"]() + p["passed"]():
            cp.wait_send()
        p["mine"]().wait()

    return _Task([shard] if into is None else [shard, into], [jax.ShapeDtypeStruct((N_DEV,) + shard.shape, shard.dtype)],
                 [pltpu.SemaphoreType.DMA((7,)), pltpu.SemaphoreType.DMA((7,)), pltpu.SemaphoreType.DMA],
                 start, finish, mid, mid_step, alias=None if into is None else (1, 0))


def _swap_task(g8):
    _, R, C = g8.shape

    def copies(ins, outs, sems):
        (g_ref,), (recv_ref,), (ss, rs) = ins, outs, sems
        x_, y_, c_ = _mesh_pos()
        return [pltpu.make_async_remote_copy(src_ref=g_ref.at[2 * k + (1 - c_)], dst_ref=recv_ref.at[k], send_sem=ss.at[k],
                                             recv_sem=rs.at[k], device_id=(x_, y_, 1 - c_), device_id_type=MESH_ID)
                for k in range(4)]

    def start(ins, outs, sems):
        for cp in copies(ins, outs, sems):
            cp.start()

    def finish(ins, outs, sems):
        cps = copies(ins, outs, sems)
        for cp in cps:
            cp.wait_recv()
        for cp in cps:
            cp.wait_send()

    return _Task([g8], [jax.ShapeDtypeStruct((4, R, C), g8.dtype)],
                 [pltpu.SemaphoreType.DMA((4,)), pltpu.SemaphoreType.DMA((4,))], start, finish)


def _chip_sums(g8, recva):
    _, R, C = g8.shape
    rt = _sum_adamw_rows(R)
    rc = _row_chunk(rt)

    def body(core_ref, g_ref, a_ref, send_ref, own_ref):
        x_, y_, _ = _mesh_pos()
        chips = [(1 - x_, y_), (x_, 1 - y_), (1 - x_, 1 - y_), (x_, y_)]

        def chunk(i, carry):
            rows = pl.ds(pl.multiple_of(i * rc, rc), rc)
            for j, (tx, ty) in enumerate(chips):
                k = 2 * tx + ty
                s = g_ref[k, rows, :].astype(F32) + a_ref[k, rows, :].astype(F32)
                if j < 3:
                    send_ref[j, rows, :] = s.astype(BF16)
                else:
                    own_ref[rows, :] = s
            return carry

        lax.fori_loop(0, rt // rc, chunk, 0)

    core = lax.axis_index("c").astype(jnp.int32).reshape(1)
    return pl.pallas_call(
        body, name="chip_sums",
        grid_spec=pltpu.PrefetchScalarGridSpec(
            num_scalar_prefetch=1, grid=(R // rt,),
            in_specs=[pl.BlockSpec((4, None, rt, C), lambda i, c: (0, c[0], i, 0)), pl.BlockSpec((4, rt, C), lambda i, c: (0, i, 0))],
            out_specs=[pl.BlockSpec((3, rt, C), lambda i, c: (0, i, 0)), pl.BlockSpec((rt, C), lambda i, c: (i, 0))]),
        out_shape=[jax.ShapeDtypeStruct((3, R, C), BF16), jax.ShapeDtypeStruct((R, C), F32)],
        compiler_params=_params("parallel"),
    )(core, g8.reshape(4, 2, R, C), recva)


def _exchange_task(sendb):
    def copies(ins, outs, sems):
        (s_ref,), (recv_ref,), (ss, rs) = ins, outs, sems
        x_, y_, c_ = _mesh_pos()
        flips = [(1 - x_, y_), (x_, 1 - y_), (1 - x_, 1 - y_)]
        return [pltpu.make_async_remote_copy(src_ref=s_ref.at[j], dst_ref=recv_ref.at[j], send_sem=ss.at[j], recv_sem=rs.at[j],
                                             device_id=(tx, ty, c_), device_id_type=MESH_ID) for j, (tx, ty) in enumerate(flips)]

    def start(ins, outs, sems):
        for cp in copies(ins, outs, sems):
            cp.start()

    def finish(ins, outs, sems):
        cps = copies(ins, outs, sems)
        for cp in cps:
            cp.wait_recv()
        for cp in cps:
            cp.wait_send()

    return _Task([sendb], [jax.ShapeDtypeStruct(sendb.shape, sendb.dtype)],
                 [pltpu.SemaphoreType.DMA((3,)), pltpu.SemaphoreType.DMA((3,))], start, finish)


def _reduce_scatter_call(g8, slab):
    _, R, C = g8.shape
    rc = _row_chunk(R)
    t_s = _gather_task(slab, 0)

    def body(g_ref, slab_ref, recva_ref, recvb_ref, own_ref, packs_ref, mine, theirs, send_vm, ld_sems, in_sems, sw_s, sw_r,
             ex_s, ex_r, *s_sems):
        x_, y_, c_ = _mesh_pos()
        chips = [(1 - x_, y_), (x_, 1 - y_), (1 - x_, 1 - y_), (x_, y_)]
        swaps, loads = [], []
        for j, (tx, ty) in enumerate(chips):
            k = 2 * tx + ty
            swaps.append(pltpu.make_async_remote_copy(src_ref=g_ref.at[2 * k + (1 - c_)], dst_ref=recva_ref.at[j], send_sem=sw_s.at[j],
                                                      recv_sem=sw_r.at[j], device_id=(x_, y_, 1 - c_), device_id_type=MESH_ID))
            loads.append(pltpu.make_async_copy(g_ref.at[2 * k + c_], mine.at[j], ld_sems.at[j]))
            swaps[j].start()
            loads[j].start()
        t_s.start((slab_ref,), (packs_ref,), s_sems)
        sends = []
        for j, (tx, ty) in enumerate(chips):
            swaps[j].wait_recv()
            cp = pltpu.make_async_copy(recva_ref.at[j], theirs.at[j], in_sems.at[j])
            cp.start()
            cp.wait()
            loads[j].wait()

            def chunk(i, carry):
                rows = pl.ds(pl.multiple_of(i * rc, rc), rc)
                s = mine[j, rows, :].astype(F32) + theirs[j, rows, :].astype(F32)
                if j < 3:
                    send_vm[j, rows, :] = s.astype(BF16)
                else:
                    own_ref[rows, :] = s
                return carry

            lax.fori_loop(0, R // rc, chunk, 0)
            if j < 3:
                sends.append(pltpu.make_async_remote_copy(src_ref=send_vm.at[j], dst_ref=recvb_ref.at[j], send_sem=ex_s.at[j],
                                                          recv_sem=ex_r.at[j], device_id=(tx, ty, c_), device_id_type=MESH_ID))
                sends[j].start()
        t_s.mid((slab_ref,), (packs_ref,), s_sems)
        t_s.finish((slab_ref,), (packs_ref,), s_sems)
        for cp in sends:
            cp.wait_recv()
        for cp in sends + swaps:
            cp.wait_send()

    hbm, vm = pl.BlockSpec(memory_space=pl.ANY), pl.BlockSpec(memory_space=pltpu.VMEM)
    blocks = lambda n: jax.ShapeDtypeStruct((n, R, C), BF16)
    _, recvb, own, packs = pl.pallas_call(
        body, name="reduce_scatter_w_in", in_specs=[hbm, hbm], out_specs=[hbm, hbm, vm, hbm],
        out_shape=[blocks(4), blocks(3), jax.ShapeDtypeStruct((R, C), F32)] + t_s.out_shapes,
        scratch_shapes=[pltpu.VMEM((4, R, C), BF16), pltpu.VMEM((4, R, C), BF16), pltpu.VMEM((3, R, C), BF16)]
        + [pltpu.SemaphoreType.DMA((4,))] * 4 + [pltpu.SemaphoreType.DMA((3,))] * 2 + t_s.sems,
        compiler_params=pltpu.CompilerParams(vmem_limit_bytes=VMEM_LIMIT),
    )(g8, slab)
    return own, recvb, packs


def _silu(v):
    return v * (1.0 / (1.0 + jnp.exp(-v)))


def _ada_fwd(c, taps, w_ada_l, b_ada, w_in_shard, casts, rpb_rev):
    n_seq = c.shape[0]
    D, cols = w_ada_l.shape
    F_l = taps.shape[1]
    W = D + F_l
    n_rows = N_DEV * n_seq
    n_c = len(casts)
    t_c = _gather_task(jax.ShapeDtypeStruct((8, W), F32), 0)
    t_w = _gather_task(jax.ShapeDtypeStruct(w_in_shard.shape, BF16), 0)
    t_m = _gather_task(jax.ShapeDtypeStruct((n_rows, cols), F32), 0)

    def body(c_ref, taps_ref, w_ref, b_ref, ws_ref, rev_ref, *refs):
        cast_in, refs = refs[:n_c], refs[n_c:]
        (slabs_ref, call_ref, taps_all_ref, mod_ref, win_ref), refs = refs[:5], refs[5:]
        cast_out, bias_ref, refs = refs[:n_c], refs[n_c], refs[n_c + 1:]
        (slab_vm, c_vm, m_vm, ws_f, ws_b), refs = refs[:5], refs[5:]
        stage_f, stage_b, refs = refs[:n_c], refs[n_c:2 * n_c], refs[2 * n_c:]
        copy_sem, in_sems, out_sems, sems = refs[0], refs[1], refs[2], refs[3:]
        sc, sw, sm = sems[0:3], sems[3:6], sems[6:9]
        x_, y_, c_ = _mesh_pos()
        slab_vm[...] = jnp.zeros_like(slab_vm)
        slab_vm[0:n_seq, 0:D] = c_ref[...]
        slab_vm[0:taps.shape[0], D:W] = taps_ref[...]
        t_c.start((slab_vm,), (slabs_ref,), sc)
        cp = pltpu.make_async_copy(ws_ref, ws_f, copy_sem)
        cp.start()
        cp.wait()
        ws_b[...] = ws_f[...].astype(BF16)
        t_w.start((ws_b,), (win_ref,), sw)
        loads = [pltpu.make_async_copy(cast_in[j], stage_f[j], in_sems.at[j]) for j in range(n_c)]
        for ld in loads:
            ld.start()
        t_c.mid((slab_vm,), (slabs_ref,), sc)
        t_c.finish((slab_vm,), (slabs_ref,), sc)
        cp = pltpu.make_async_copy(slabs_ref, c_vm, copy_sem)
        cp.start()
        cp.wait()
        c_all = c_vm[:, :, 0:D].reshape(N_DEV * 8, D)
        call_ref[...] = c_all
        for j in range(N_DEV):
            taps_all_ref[:, j * F_l:(j + 1) * F_l] = c_vm[j, 0:taps.shape[0], D:W]
        b_mine = b_ref[:, pl.ds(pl.multiple_of((4 * x_ + 2 * y_ + c_) * cols, 128), cols)]
        m64 = jnp.dot(_silu(c_all), w_ref[...], precision=lax.Precision.HIGHEST, preferred_element_type=F32) + b_mine
        r = lax.broadcasted_iota(jnp.int32, (n_rows, N_DEV * 8), 0)
        c = lax.broadcasted_iota(jnp.int32, (n_rows, N_DEV * 8), 1)
        pick = jnp.where(c == 8 * (r // n_seq) + r % n_seq, 1.0, 0.0)
        m_vm[...] = jnp.dot(pick, m64, precision=lax.Precision.HIGHEST, preferred_element_type=F32)
        t_m.start((m_vm,), (mod_ref,), sm)
        stores = [pltpu.make_async_copy(stage_b[j], cast_out[j], out_sems.at[j]) for j in range(n_c)]
        for j in range(n_c):
            loads[j].wait()
            stage_b[j][...] = stage_f[j][...].astype(BF16)
            stores[j].start()
        for p in range(NA_PAIRS):
            _na_bias_rows(rev_ref, bias_ref, p)
        t_w.mid((ws_b,), (win_ref,), sw)
        t_m.mid((m_vm,), (mod_ref,), sm)
        t_m.finish((m_vm,), (mod_ref,), sm)
        t_w.finish((ws_b,), (win_ref,), sw)
        for st in stores:
            st.wait()

    hbm, vm = pl.BlockSpec(memory_space=pl.ANY), pl.BlockSpec(memory_space=pltpu.VMEM)
    res = pl.pallas_call(
        body, name="ada_fwd", in_specs=[vm, vm, vm, vm, hbm, vm] + [hbm] * n_c,
        out_specs=[hbm, vm, vm, hbm, hbm] + [hbm] * n_c + [vm],
        out_shape=t_c.out_shapes + [jax.ShapeDtypeStruct((N_DEV * 8, D), F32), jax.ShapeDtypeStruct((taps.shape[0], N_DEV * F_l), F32)]
        + t_m.out_shapes + t_w.out_shapes
        + [jax.ShapeDtypeStruct(a.shape, BF16) for a in casts] + [jax.ShapeDtypeStruct((NA_PAIRS, N_DR * GRID_W, 128), F32)],
        scratch_shapes=[pltpu.VMEM((8, W), F32), pltpu.VMEM((N_DEV, 8, W), F32), pltpu.VMEM((n_rows, cols), F32),
                        pltpu.VMEM(w_in_shard.shape, F32), pltpu.VMEM(w_in_shard.shape, BF16)]
        + [pltpu.VMEM(a.shape, F32) for a in casts] + [pltpu.VMEM(a.shape, BF16) for a in casts]
        + [pltpu.SemaphoreType.DMA, pltpu.SemaphoreType.DMA((n_c,)), pltpu.SemaphoreType.DMA((n_c,))]
        + t_c.sems + t_w.sems + t_m.sems,
        compiler_params=pltpu.CompilerParams(vmem_limit_bytes=VMEM_LIMIT),
    )(c, taps, w_ada_l, b_ada, w_in_shard, rpb_rev, *casts)
    return res[1], res[2], res[3], res[4], res[5:5 + n_c], res[5 + n_c]


def _ada_bwd(c_all, packs, cols):
    def body(c_ref, d_ref, o_ref):
        x_, y_, c_ = _mesh_pos()
        mine = d_ref[:, pl.ds(pl.multiple_of((4 * x_ + 2 * y_ + c_) * cols, 128), cols)]
        o_ref[...] = lax.dot_general(_silu(c_ref[...]), mine, (((0,), (0,)), ((), ())),
                                     precision=lax.Precision.HIGHEST, preferred_element_type=F32)
    return pl.pallas_call(body, name="ada_bwd", out_shape=jax.ShapeDtypeStruct((c_all.shape[1], cols), F32),
                          compiler_params=pltpu.CompilerParams(vmem_limit_bytes=VMEM_LIMIT))(c_all, packs)


NA_PAIRS = NA_HEADS // 2
N_DR_PAD = 16


def _na_bias_rev(na_rpb):
    rev = jnp.pad(jnp.flip(na_rpb, axis=2), ((0, 0), (0, N_DR_PAD - N_DR), (0, GRID_W - N_DC)))
    return jnp.transpose(rev.reshape(NA_PAIRS, 2, N_DR_PAD, GRID_W), (0, 2, 1, 3)).reshape(NA_PAIRS, N_DR_PAD, 128)


def _na_bias_rows(r_ref, o_ref, p):
    k = lax.broadcasted_iota(jnp.int32, (GRID_W, 128), 0)
    lane = lax.broadcasted_iota(jnp.int32, (GRID_W, 128), 1)
    q = lane % GRID_W
    cs = jnp.clip(q - NA_COLS // 2, 0, GRID_W - NA_COLS)
    ok = (k >= cs) & (k < cs + NA_COLS)
    left = lane < GRID_W
    for dr in range(N_DR):
        row = jnp.broadcast_to(r_ref[p, dr:dr + 1, :], (GRID_W, 128))
        r0 = jnp.where(left, row, 0.0)
        r1 = jnp.where(left, pltpu.roll(row, GRID_W, axis=1), 0.0)
        y0 = pltpu.roll(r0, 128 - (NA_COLS - 1), axis=1, stride=1, stride_axis=0)
        y1 = pltpu.roll(r1, GRID_W - (NA_COLS - 1), axis=1, stride=1, stride_axis=0)
        o_ref[p, dr * GRID_W:(dr + 1) * GRID_W, :] = jnp.where(ok, jnp.where(left, y0, y1), NEG)


def _na_bias_grad(db, tasks=()):
    a = np.arange(128)
    flip = jnp.asarray(((a[:, None] // GRID_W == a[None, :] // GRID_W)
                        & (a[:, None] % GRID_W + a[None, :] % GRID_W == GRID_W - 1)).astype(np.float32))

    def body(d_ref, j_ref, o_ref):
        o_ref[...] = jnp.zeros_like(o_ref)
        for dr in range(N_DR):
            t = jnp.dot(d_ref[0, dr * GRID_W:(dr + 1) * GRID_W, :], j_ref[...], precision=lax.Precision.HIGHEST, preferred_element_type=F32)
            t = pltpu.roll(t, GRID_W + NA_COLS, axis=1, stride=1, stride_axis=0)
            o_ref[0, dr:dr + 1, :] = jnp.sum(t, axis=0, keepdims=True)

    (rows,), got = _hosted_call(
        body, "rpb_reduce", (NA_PAIRS,),
        [pl.BlockSpec((1, N_DR * GRID_W, 128), lambda p: (p, 0, 0)), _full((128, 128))],
        [pl.BlockSpec((1, N_DR_PAD, 128), lambda p: (p, 0, 0))],
        [jax.ShapeDtypeStruct((NA_PAIRS, N_DR_PAD, 128), F32)], (db, flip), tasks)
    g = rows.reshape(NA_PAIRS, N_DR_PAD, 2, GRID_W)[:, :N_DR, :, :N_DC]
    return jnp.transpose(g, (0, 2, 1, 3)).reshape(-1), got


def _rope_tables(S):
    half = HEAD_DIM // 2
    inv = np.float32(ROPE_THETA) ** (-np.arange(half, dtype=np.float32) / np.float32(half))
    ang = np.arange(S).astype(np.float32)[:, None] * inv[None, :]
    cos, sin = np.cos(ang).astype(np.float32), np.sin(ang).astype(np.float32)
    return jnp.asarray(np.tile(np.concatenate([cos, cos], axis=1), (1, 2))), jnp.asarray(np.tile(np.concatenate([-sin, sin], axis=1), (1, 2)))


def _rope_spec(tps, tm=TOKEN_TILE):
    return pl.BlockSpec((tm, 2 * HEAD_DIM), lambda i: (i % tps, 0))


def _rot_half(t):
    w = t.shape[1]
    lane = lax.broadcasted_iota(jnp.int32, t.shape, 1)
    return jnp.where((lane % HEAD_DIM) < HEAD_DIM // 2, pltpu.roll(t, w - HEAD_DIM // 2, axis=1),
                     pltpu.roll(t, HEAD_DIM // 2, axis=1))


def _tok_spec(w, tm=TOKEN_TILE):
    return pl.BlockSpec((tm, w), lambda i: (i, 0))


def _mod_spec(tps, d):
    return pl.BlockSpec((1, 6, d), lambda i: (i // tps, 0, 0))


def _bstat_spec(tps, w):
    return pl.BlockSpec((1, 8, w), lambda i: (i // tps, 0, 0))


def _attn_in(x2d, mod3, g_attn, w_in, cos_t, sin_t, S, tasks=(), tm=WIDE_TILE):
    T, D = x2d.shape
    tps = S // tm

    def body(x_ref, mod_ref, g_ref, w_ref, cos_ref, sin_ref, h_ref, qkv_ref):
        xn, _ = _rms(x_ref[...])
        h = (xn * g_ref[...]) * (1.0 + mod_ref[0, 1:2, :]) + mod_ref[0, 0:1, :]
        hb = h.astype(BF16)
        h_ref[...] = hb
        proj = _nt(hb, w_ref[...])
        rb = proj[:, ROPE_LO:ROPE_LO + ROPE_WIDTH]
        reps = (1, ROPE_WIDTH // (2 * HEAD_DIM))
        rb = rb * jnp.tile(cos_ref[...], reps) + _rot_half(rb) * jnp.tile(sin_ref[...], reps)
        qkv_ref[:, 0:NA_WIDTH] = (proj[:, 0:NA_WIDTH] * Q_SCALE).astype(BF16)
        qkv_ref[:, NA_WIDTH:ROPE_LO] = proj[:, NA_WIDTH:ROPE_LO].astype(BF16)
        qkv_ref[:, ROPE_LO:ROPE_LO + SW_WIDTH] = (rb[:, 0:SW_WIDTH] * Q_SCALE).astype(BF16)
        qkv_ref[:, ROPE_LO + SW_WIDTH:ROPE_LO + ROPE_WIDTH] = rb[:, SW_WIDTH:].astype(BF16)
        qkv_ref[:, ROPE_LO + ROPE_WIDTH:] = proj[:, ROPE_LO + ROPE_WIDTH:].astype(BF16)

    return _hosted_call(
        body, "attn_in", (T // tm,),
        [_tok_spec(D, tm), _mod_spec(tps, D), _full((1, D)), _full(w_in.shape), _rope_spec(tps, tm), _rope_spec(tps, tm)],
        [_tok_spec(D, tm), _tok_spec(IN_WIDTH, tm)],
        [jax.ShapeDtypeStruct((T, D), BF16), jax.ShapeDtypeStruct((T, IN_WIDTH), BF16)],
        (x2d, mod3, g_attn, w_in, cos_t, sin_t), tasks)


def _attn_out(oa, ob, x2d, mod3, g_na, g_sw, w_out, S, tasks=(), tm=WIDE_TILE):
    T, D = x2d.shape
    tps = S // tm

    def body(oa_ref, ob_ref, x_ref, mod_ref, gna_ref, gsw_ref, w_ref, mixin_ref, mix_ref, x1_ref):
        oan, _ = _rms(oa_ref[...])
        obn, _ = _rms(ob_ref[...])
        mixin = jnp.concatenate([oan * gna_ref[...], obn * gsw_ref[...]], axis=1).astype(BF16)
        mixin_ref[...] = mixin
        mix = _nn(mixin, w_ref[...])
        mix_ref[...] = mix
        x1_ref[...] = x_ref[...] + mod_ref[0, 2:3, :] * mix

    return _hosted_call(
        body, "attn_out", (T // tm,),
        [_tok_spec(NA_WIDTH, tm), _tok_spec(SW_WIDTH, tm), _tok_spec(D, tm), _mod_spec(tps, D),
         _full((1, NA_WIDTH)), _full((1, SW_WIDTH)), _full(w_out.shape)],
        [_tok_spec(NA_WIDTH + SW_WIDTH, tm), _tok_spec(D, tm), _tok_spec(D, tm)],
        [jax.ShapeDtypeStruct((T, NA_WIDTH + SW_WIDTH), BF16), jax.ShapeDtypeStruct((T, D), F32), jax.ShapeDtypeStruct((T, D), F32)],
        (oa, ob, x2d, mod3, g_na, g_sw, w_out), tasks)


def _ffn_up(x1, mod3, g_ffn, w_up, S, tasks=(), tm=WIDE_TILE):
    T, D = x1.shape
    F = w_up.shape[0] // 2
    tps = S // tm

    def body(x1_ref, mod_ref, g_ref, w_ref, h2_ref, val_ref, gt_ref):
        xn, _ = _rms(x1_ref[...])
        h2 = ((xn * g_ref[...]) * (1.0 + mod_ref[0, 4:5, :]) + mod_ref[0, 3:4, :]).astype(BF16)
        h2_ref[...] = h2
        u = _nt(h2, w_ref[...])
        val_ref[...] = u[:, :F].astype(BF16)
        gt_ref[...] = u[:, F:].astype(BF16)

    return _hosted_call(
        body, "ffn_up", (T // tm,), [_tok_spec(D, tm), _mod_spec(tps, D), _full((1, D)), _full(w_up.shape)],
        [_tok_spec(D, tm), _tok_spec(F, tm), _tok_spec(F, tm)],
        [jax.ShapeDtypeStruct((T, D), BF16), jax.ShapeDtypeStruct((T, F), BF16), jax.ShapeDtypeStruct((T, F), BF16)],
        (x1, mod3, g_ffn, w_up), tasks)


def _halo_specs(T, tps, w):
    per = TOKEN_TILE // 8
    prev = pl.BlockSpec((8, w), lambda i: (jnp.maximum(i * per - 1, 0), 0))
    nxt = pl.BlockSpec((8, w), lambda i: (jnp.minimum((i + 1) * per, T // 8 - 1), 0))
    return prev, nxt


def _seq_shifts(cur, before, after, ti, tps):
    tm = cur.shape[0]
    row = lax.broadcasted_iota(jnp.int32, cur.shape, 0)
    before = jnp.where(ti > 0, before.astype(F32), 0.0)
    after = jnp.where(ti < tps - 1, after.astype(F32), 0.0)
    return jnp.where(row == 0, before, pltpu.roll(cur, 1, axis=0)), jnp.where(row == tm - 1, after, pltpu.roll(cur, tm - 1, axis=0))


def _ffn_down(gt, val, conv_w, conv_b, w_down, x1, mod3, g_final, target, B, S):
    T, D = x1.shape
    F = gt.shape[1]
    tps = S // TOKEN_TILE
    prev, nxt = _halo_specs(T, tps, F)

    def body(gt_ref, prev_ref, next_ref, val_ref, cw_ref, cb_ref, w_ref, x1_ref, mod_ref, gf_ref, tgt_ref,
             a_ref, act_ref, vd_ref, dx2_ref, df_ref, gstat_ref, bstat_ref):
        i = pl.program_id(0)
        g = gt_ref[...].astype(F32)
        gprev, gnext = _seq_shifts(g, prev_ref[7:8, :], next_ref[0:1, :], i % tps, tps)
        gc = gprev * cw_ref[0:1, :] + g * cw_ref[1:2, :] + gnext * cw_ref[2:3, :] + cb_ref[...]
        sig = 1.0 / (1.0 + jnp.exp(-gc))
        act = gc * sig
        val = val_ref[...].astype(F32)
        act_ref[...] = act.astype(BF16)
        vd_ref[...] = (val * (sig + act - act * sig)).astype(BF16)
        a = (act * val).astype(BF16)
        a_ref[...] = a
        f = _nn(a, w_ref[...])
        gate = mod_ref[0, 5:6, :]
        x2 = x1_ref[...] + gate * f
        xn, r = _rms(x2)
        err = xn * gf_ref[...] - tgt_ref[...]
        dy = err * (1.0 / D)
        dx2 = _rms_bwd(xn, r, dy * gf_ref[...])
        dx2_ref[...] = dx2
        df_ref[...] = (gate * dx2).astype(BF16)

        @pl.when(i == 0)
        def _():
            gstat_ref[...] = jnp.zeros_like(gstat_ref)

        @pl.when(i % tps == 0)
        def _():
            bstat_ref[...] = jnp.zeros_like(bstat_ref)

        gstat_ref[0:1, :] += jnp.sum(dy * xn, axis=0, keepdims=True)
        tile_loss = jnp.sum(jnp.sum(err * err, axis=1, keepdims=True), axis=0, keepdims=True) * (0.5 / D)
        gstat_ref[1:2, :] += jnp.broadcast_to(tile_loss, (1, D))
        bstat_ref[0, 0:1, :] += jnp.sum(dx2 * f, axis=0, keepdims=True)

    return pl.pallas_call(
        body, name="ffn_down", grid=(T // TOKEN_TILE,),
        in_specs=[_tok_spec(F), prev, nxt, _tok_spec(F), _full(conv_w.shape), _full((1, F)), _full(w_down.shape),
                  _tok_spec(D), _mod_spec(tps, D), _full((1, D)), _tok_spec(D)],
        out_specs=[_tok_spec(F), _tok_spec(F), _tok_spec(F), _tok_spec(D), _tok_spec(D), _full((8, D)), _bstat_spec(tps, D)],
        out_shape=[jax.ShapeDtypeStruct((T, F), BF16), jax.ShapeDtypeStruct((T, F), BF16), jax.ShapeDtypeStruct((T, F), BF16),
                   jax.ShapeDtypeStruct((T, D), F32), jax.ShapeDtypeStruct((T, D), BF16),
                   jax.ShapeDtypeStruct((8, D), F32), jax.ShapeDtypeStruct((B, 8, D), F32)],
        compiler_params=_params("arbitrary"),
    )(gt, gt, gt, val, conv_w, conv_b, w_down, x1, mod3, g_final, target)


def _ffn_down_bwd(df, w_down, act, vd, tasks=(), tm=WIDE_TILE):
    T, D = df.shape
    F = act.shape[1]

    def body(df_ref, w_ref, act_ref, vd_ref, dval_ref, dgc_ref, cstat_ref):
        da = _nt(df_ref[...], w_ref[...])
        dval_ref[...] = (da * act_ref[...].astype(F32)).astype(BF16)
        dgc = da * vd_ref[...].astype(F32)
        dgc_ref[...] = dgc.astype(BF16)

        @pl.when(pl.program_id(0) == 0)
        def _():
            cstat_ref[...] = jnp.zeros_like(cstat_ref)

        cstat_ref[0:1, :] += jnp.sum(dgc, axis=0, keepdims=True)

    return _hosted_call(
        body, "ffn_down_bwd", (T // tm,),
        [_tok_spec(D, tm), _full(w_down.shape), _tok_spec(F, tm), _tok_spec(F, tm)],
        [_tok_spec(F, tm), _tok_spec(F, tm), _full((8, F))],
        [jax.ShapeDtypeStruct((T, F), BF16), jax.ShapeDtypeStruct((T, F), BF16), jax.ShapeDtypeStruct((8, F), F32)],
        (df, w_down, act, vd), tasks)


def _ffn_up_bwd(dgc, dval, gt, conv_w, w_up, x1, mod3, g_ffn, dx2, mix, B, S, tasks=()):
    T, D = x1.shape
    F = dgc.shape[1]
    tps = S // TOKEN_TILE
    prev, nxt = _halo_specs(T, tps, F)

    def body(dgc_ref, prev_ref, next_ref, dval_ref, gt_ref, cw_ref, w_ref, x1_ref, mod_ref, g_ref, dx2_ref, mix_ref,
             du_ref, dx1_ref, dmix_ref, gstat_ref, bstat_ref, cstat_ref):
        i = pl.program_id(0)
        d = dgc_ref[...].astype(F32)
        dprev, dnext = _seq_shifts(d, prev_ref[7:8, :], next_ref[0:1, :], i % tps, tps)
        g = gt_ref[...].astype(F32)

        @pl.when(i == 0)
        def _():
            cstat_ref[...] = jnp.zeros_like(cstat_ref)

        cstat_ref[1:2, :] += jnp.sum(dnext * g, axis=0, keepdims=True)
        cstat_ref[2:3, :] += jnp.sum(d * g, axis=0, keepdims=True)
        cstat_ref[3:4, :] += jnp.sum(dprev * g, axis=0, keepdims=True)
        dgt = dnext * cw_ref[0:1, :] + d * cw_ref[1:2, :] + dprev * cw_ref[2:3, :]
        du = jnp.concatenate([dval_ref[...], dgt.astype(BF16)], axis=1)
        du_ref[...] = du
        dh2 = _nn(du, w_ref[...])
        xn, r = _rms(x1_ref[...])
        scale1 = 1.0 + mod_ref[0, 4:5, :]
        xg = xn * g_ref[...]
        dx1 = dx2_ref[...] + _rms_bwd(xn, r, dh2 * g_ref[...] * scale1)
        dx1_ref[...] = dx1
        dmix_ref[...] = (mod_ref[0, 2:3, :] * dx1).astype(BF16)

        @pl.when(i == 0)
        def _():
            gstat_ref[...] = jnp.zeros_like(gstat_ref)

        @pl.when(i % tps == 0)
        def _():
            bstat_ref[...] = jnp.zeros_like(bstat_ref)

        gstat_ref[0:1, :] += jnp.sum(dh2 * scale1 * xn, axis=0, keepdims=True)
        bstat_ref[0, 0:1, :] += jnp.sum(dh2, axis=0, keepdims=True)
        bstat_ref[0, 1:2, :] += jnp.sum(dh2 * xg, axis=0, keepdims=True)
        bstat_ref[0, 2:3, :] += jnp.sum(dx1 * mix_ref[...], axis=0, keepdims=True)

    return _hosted_call(
        body, "ffn_up_bwd", (T // TOKEN_TILE,),
        [_tok_spec(F), prev, nxt, _tok_spec(F), _tok_spec(F), _full(conv_w.shape), _full(w_up.shape), _tok_spec(D),
         _mod_spec(tps, D), _full((1, D)), _tok_spec(D), _tok_spec(D)],
        [_tok_spec(2 * F), _tok_spec(D), _tok_spec(D), _full((8, D)), _bstat_spec(tps, D), _full((8, F))],
        [jax.ShapeDtypeStruct((T, 2 * F), BF16), jax.ShapeDtypeStruct((T, D), F32), jax.ShapeDtypeStruct((T, D), BF16),
         jax.ShapeDtypeStruct((8, D), F32), jax.ShapeDtypeStruct((B, 8, D), F32), jax.ShapeDtypeStruct((8, F), F32)],
        (dgc, dgc, dgc, dval, gt, conv_w, w_up, x1, mod3, g_ffn, dx2, mix), tasks)


def _attn_out_bwd(dmix, w_out, oa, ob, g_na, g_sw, tasks=(), tm=WIDE_TILE):
    T, D = dmix.shape

    def body(dmix_ref, w_ref, oa_ref, ob_ref, gna_ref, gsw_ref, doa_ref, dob_ref, gstat_ref):
        dmixin = _nt(dmix_ref[...], w_ref[...])

        @pl.when(pl.program_id(0) == 0)
        def _():
            gstat_ref[...] = jnp.zeros_like(gstat_ref)

        for k, (o_ref, g_ref, do_ref) in enumerate(((oa_ref, gna_ref, doa_ref), (ob_ref, gsw_ref, dob_ref))):
            dn = dmixin[:, k * NA_WIDTH:(k + 1) * NA_WIDTH]
            on, r = _rms(o_ref[...])
            gstat_ref[k:k + 1, :] += jnp.sum(dn * on, axis=0, keepdims=True)
            do_ref[...] = _rms_bwd(on, r, dn * g_ref[...]).astype(BF16)

    hs = jax.ShapeDtypeStruct((T, NA_WIDTH), BF16)
    return _hosted_call(
        body, "attn_out_bwd", (T // tm,),
        [_tok_spec(D, tm), _full(w_out.shape), _tok_spec(NA_WIDTH, tm), _tok_spec(SW_WIDTH, tm), _full((1, NA_WIDTH)), _full((1, SW_WIDTH))],
        [_tok_spec(NA_WIDTH, tm), _tok_spec(SW_WIDTH, tm), _full((8, NA_WIDTH))],
        [hs, hs, jax.ShapeDtypeStruct((8, NA_WIDTH), F32)],
        (dmix, w_out, oa, ob, g_na, g_sw), tasks)


def _attn_in_bwd(dqa, dka, dva, dqb, dkb, dvb, cos_t, sin_t, w_in, x2d, mod3, g_attn, dx1, B, S, tm=WIDE_TILE):
    T, D = x2d.shape
    tps = S // tm

    def body(dqa_ref, dka_ref, dva_ref, dqb_ref, dkb_ref, dvb_ref, cos_ref, sin_ref, w_ref, x_ref, mod_ref, g_ref, dx1_ref,
             gx_ref, dproj_ref, gstat_ref, bstat_ref):
        i = pl.program_id(0)
        drb = jnp.concatenate([dqb_ref[...] * Q_SCALE, dkb_ref[...]], axis=1).astype(F32)
        reps = (1, ROPE_WIDTH // (2 * HEAD_DIM))
        drb = drb * jnp.tile(cos_ref[...], reps) + _rot_half(drb * jnp.tile(sin_ref[...], reps))
        dproj = jnp.concatenate([(dqa_ref[...] * Q_SCALE).astype(BF16), dka_ref[...].astype(BF16), dva_ref[...].astype(BF16),
                                 drb.astype(BF16), dvb_ref[...].astype(BF16)], axis=1)
        dproj_ref[...] = dproj
        dh = _nn(dproj, w_ref[...])
        xn, r = _rms(x_ref[...])
        scale1 = 1.0 + mod_ref[0, 1:2, :]
        gx_ref[...] = dx1_ref[...] + _rms_bwd(xn, r, dh * g_ref[...] * scale1)

        @pl.when(i == 0)
        def _():
            gstat_ref[...] = jnp.zeros_like(gstat_ref)

        @pl.when(i % tps == 0)
        def _():
            bstat_ref[...] = jnp.zeros_like(bstat_ref)

        gstat_ref[0:1, :] += jnp.sum(dh * scale1 * xn, axis=0, keepdims=True)
        bstat_ref[0, 0:1, :] += jnp.sum(dh, axis=0, keepdims=True)
        bstat_ref[0, 1:2, :] += jnp.sum(dh * (xn * g_ref[...]), axis=0, keepdims=True)

    rope = _rope_spec(tps, tm)
    return pl.pallas_call(
        body, name="attn_in_bwd", grid=(T // tm,),
        in_specs=[_tok_spec(NA_WIDTH, tm), _tok_spec(NA_WIDTH, tm), _tok_spec(NA_WIDTH, tm), _tok_spec(SW_WIDTH, tm),
                  _tok_spec(SW_KV_WIDTH, tm), _tok_spec(SW_KV_WIDTH, tm), rope, rope, _full(w_in.shape), _tok_spec(D, tm),
                  _mod_spec(tps, D), _full((1, D)), _tok_spec(D, tm)],
        out_specs=[_tok_spec(D, tm), _tok_spec(IN_WIDTH, tm), _full((8, D)), _bstat_spec(tps, D)],
        out_shape=[jax.ShapeDtypeStruct((T, D), F32), jax.ShapeDtypeStruct((T, IN_WIDTH), BF16),
                   jax.ShapeDtypeStruct((8, D), F32), jax.ShapeDtypeStruct((B, 8, D), F32)],
        compiler_params=_params("arbitrary"),
    )(dqa, dka, dva, dqb, dkb, dvb, cos_t, sin_t, w_in, x2d, mod3, g_attn, dx1)


def _matmul_tn(a, b, name, tm=None, tk=512):
    T, M = a.shape
    N = b.shape[1]
    tm = M if tm is None else tm
    nk = T // tk

    def body(a_ref, b_ref, o_ref, acc):
        k = pl.program_id(1)

        @pl.when(k == 0)
        def _():
            acc[...] = jnp.zeros_like(acc)

        acc[...] += _tn(a_ref[...], b_ref[...])

        @pl.when(k == nk - 1)
        def _():
            o_ref[...] = acc[...].astype(BF16)

    return pl.pallas_call(
        body, name=name, grid=(M // tm, nk),
        in_specs=[pl.BlockSpec((tk, tm), lambda i, k: (k, i)), pl.BlockSpec((tk, N), lambda i, k: (k, 0))],
        out_specs=pl.BlockSpec((tm, N), lambda i, k: (i, 0)),
        out_shape=jax.ShapeDtypeStruct((M, N), BF16),
        scratch_shapes=[pltpu.VMEM((tm, N), F32)],
        compiler_params=_params("parallel", "arbitrary"),
    )(a, b)


def _na_geometry(S):
    rows = S // GRID_W
    wr = min(NA_ROWS_MAX, rows)
    return rows, wr


def _na_window(r, rows, wr):
    rs = jnp.clip(r - wr // 2, 0, rows - wr)
    return pl.multiple_of(rs * GRID_W, GRID_W), pl.multiple_of((rs - r + NA_ROWS_MAX - 1) * GRID_W, GRID_W)


NA_STEP_PAIRS = 2
NA_GW = NA_STEP_PAIRS * 128
NA_BWD_ROWS = 4
NA_ROWS_PER_STEP = 4


def _na_specs(S, kw_n, order):
    ng = NA_PAIRS // NA_STEP_PAIRS

    def col(k):
        return pl.BlockSpec((1, S, NA_GW), lambda *ids: (order(*ids)[0], 0, k * ng + order(*ids)[1]))
    bias = pl.BlockSpec((NA_STEP_PAIRS, N_DR * GRID_W, 128), lambda *ids: (order(*ids)[1], 0, 0))
    out = pl.BlockSpec((1, S, NA_GW), lambda *ids: (order(*ids)[0], 0, order(*ids)[1]))
    return col(0), col(1), col(2), bias, out


def _block_diag(t):
    left = lax.broadcasted_iota(jnp.int32, t.shape, 1) < HEAD_DIM
    zero = jnp.zeros_like(t)
    return jnp.concatenate([jnp.where(left, t, zero), jnp.where(left, zero, t)], axis=0)


def _diag_blocks(res):
    left = lax.broadcasted_iota(jnp.int32, (HEAD_DIM, 128), 1) < HEAD_DIM
    return jnp.where(left, res[:HEAD_DIM], res[HEAD_DIM:])


def _col_softmax(st):
    e = jnp.exp(st - jnp.max(st, axis=0, keepdims=True))
    return e * (1.0 / jnp.sum(e, axis=0, keepdims=True))


def _na_fwd(qkv, bias, tasks=()):
    B, S, _ = qkv.shape
    rows, wr = _na_geometry(S)
    kw_n = wr * GRID_W

    def body(q_ref, k_ref, v_ref, b_ref, o_ref):
        def step(it, carry):
            win = [_na_window(it * NA_ROWS_PER_STEP + u, rows, wr) for u in range(NA_ROWS_PER_STEP)]
            qrows = [pl.ds(pl.multiple_of((it * NA_ROWS_PER_STEP + u) * GRID_W, GRID_W), GRID_W) for u in range(NA_ROWS_PER_STEP)]
            krows = [pl.ds(w[0], kw_n) for w in win]
            brows = [pl.ds(w[1], kw_n) for w in win]
            lanes = [pl.ds(p * 128, 128) for p in range(NA_STEP_PAIRS)]
            chains = [(u, p) for u in range(NA_ROWS_PER_STEP) for p in range(NA_STEP_PAIRS)]
            st = {(u, p): _nt(k_ref[0, krows[u], lanes[p]], _block_diag(q_ref[0, qrows[u], lanes[p]])) for u, p in chains}
            pn = {(u, p): _col_softmax(st[(u, p)] + b_ref[p, brows[u], :]).astype(BF16) for u, p in chains}
            out = {(u, p): _diag_blocks(_tn(pn[(u, p)], v_ref[0, krows[u], lanes[p]])) for u, p in chains}
            for u in range(NA_ROWS_PER_STEP):
                o_ref[0, qrows[u], :] = jnp.concatenate([out[(u, p)] for p in range(NA_STEP_PAIRS)], axis=1)
            return carry

        lax.fori_loop(0, rows // NA_ROWS_PER_STEP, step, 0)

    q, k, v, bs, out = _na_specs(S, kw_n, lambda b, g: (b, g))
    return _hosted_call(body, "na_fwd", (B, NA_PAIRS // NA_STEP_PAIRS), [q, k, v, bs], [out],
                        [jax.ShapeDtypeStruct((B, S, NA_WIDTH), F32)], (qkv, qkv, qkv, bias), tasks)


def _na_bwd(qkv, bias, doa, tasks=()):
    B, S, _ = qkv.shape
    rows, wr = _na_geometry(S)
    kw_n = wr * GRID_W

    def body(q_ref, k_ref, v_ref, b_ref, do_ref, dq_ref, dk_ref, dv_ref, db_ref, dk_acc, dv_acc):
        @pl.when(pl.program_id(1) == 0)
        def _():
            db_ref[...] = jnp.zeros_like(db_ref)

        dk_acc[...] = jnp.zeros_like(dk_acc)
        dv_acc[...] = jnp.zeros_like(dv_acc)

        def step(it, carry):
            nu, pairs = range(NA_BWD_ROWS), range(NA_STEP_PAIRS)
            win = [_na_window(it * NA_BWD_ROWS + u, rows, wr) for u in nu]
            qrows = [pl.ds(pl.multiple_of((it * NA_BWD_ROWS + u) * GRID_W, GRID_W), GRID_W) for u in nu]
            krows = [pl.ds(w[0], kw_n) for w in win]
            brows = [pl.ds(w[1], kw_n) for w in win]
            lanes = [pl.ds(p * 128, 128) for p in pairs]
            chains = [(u, p) for u in nu for p in pairs]
            kp = {(u, p): k_ref[0, krows[u], lanes[p]] for u, p in chains}
            qbd = {(u, p): _block_diag(q_ref[0, qrows[u], lanes[p]]) for u, p in chains}
            dobd = {(u, p): _block_diag(do_ref[0, qrows[u], lanes[p]]) for u, p in chains}
            st = {c: _nt(kp[c], qbd[c]) for c in chains}
            dpt = {(u, p): _nt(v_ref[0, krows[u], lanes[p]], dobd[(u, p)]) for u, p in chains}
            pn = {(u, p): _col_softmax(st[(u, p)] + b_ref[p, brows[u], :]) for u, p in chains}
            dst = {c: pn[c] * (dpt[c] - jnp.sum(pn[c] * dpt[c], axis=0, keepdims=True)) for c in chains}
            dsb = {c: dst[c].astype(BF16) for c in chains}
            dq = {c: _diag_blocks(_tn(dsb[c], kp[c])) for c in chains}
            dk = {c: _nn(dsb[c], qbd[c]) for c in chains}
            dv = {c: _nn(pn[c].astype(BF16), dobd[c]) for c in chains}
            for u in nu:
                dq_ref[0, qrows[u], :] = jnp.concatenate([dq[(u, p)] for p in pairs], axis=1).astype(BF16)
                dk_acc[krows[u], :] += jnp.concatenate([dk[(u, p)] for p in pairs], axis=1)
                dv_acc[krows[u], :] += jnp.concatenate([dv[(u, p)] for p in pairs], axis=1)
                for p in pairs:
                    db_ref[p, brows[u], :] += dst[(u, p)]
            return carry

        lax.fori_loop(0, rows // NA_BWD_ROWS, step, 0)

        def emit(i, carry):
            r = pl.ds(pl.multiple_of(i * 256, 256), 256)
            dk_ref[0, r, :] = dk_acc[r, :].astype(BF16)
            dv_ref[0, r, :] = dv_acc[r, :].astype(BF16)
            return carry

        lax.fori_loop(0, S // 256, emit, 0)

    q, k, v, bs, out = _na_specs(S, kw_n, lambda g, b: (b, g))
    hs = jax.ShapeDtypeStruct((B, S, NA_WIDTH), BF16)
    return _hosted_call(body, "na_bwd", (NA_PAIRS // NA_STEP_PAIRS, B), [q, k, v, bs, out], [out, out, out, bs],
                        [hs, hs, hs, jax.ShapeDtypeStruct((NA_PAIRS, N_DR * GRID_W, 128), F32)], (qkv, qkv, qkv, bias, doa), tasks,
                        scratch_shapes=[pltpu.VMEM((S, NA_GW), F32), pltpu.VMEM((S, NA_GW), F32)])


SW_PAIRS = SW_HEADS // 2


def _sw_band(n, S):
    kw_n = 3 * SW_BLOCK
    start = pl.multiple_of(jnp.clip(n * SW_BLOCK - SW_BLOCK, 0, S - kw_n), SW_BLOCK)
    kpos = start + lax.broadcasted_iota(jnp.int32, (kw_n, SW_BLOCK), 0)
    qpos = n * SW_BLOCK + lax.broadcasted_iota(jnp.int32, (kw_n, SW_BLOCK), 1)
    return start, jnp.abs(qpos - kpos) <= SW_WINDOW


def _kv_halves(t):
    left = lax.broadcasted_iota(jnp.int32, t.shape, 1) < HEAD_DIM
    swapped = pltpu.roll(t, HEAD_DIM, axis=1)
    zero = jnp.zeros_like(t)
    return {(0, 0): jnp.where(left, t, zero), (0, 1): jnp.where(left, zero, swapped),
            (1, 0): jnp.where(left, swapped, zero), (1, 1): jnp.where(left, zero, t)}


def _sw_probs(st, ok, sk):
    st = jnp.where(ok, st, NEG)
    m = jnp.maximum(jnp.max(st, axis=0, keepdims=True), sk)
    e = jnp.exp(st - m)
    esk = jnp.exp(sk - m)
    inv = 1.0 / (jnp.sum(e, axis=0, keepdims=True) + esk)
    return e * inv, esk * inv


def _sw_specs(S):
    q = pl.BlockSpec((1, S, SW_WIDTH), lambda b, *_: (b, 0, ROPE_LO // SW_WIDTH))
    k = pl.BlockSpec((1, S, SW_KV_WIDTH), lambda b, *_: (b, 0, (ROPE_LO + SW_WIDTH) // SW_KV_WIDTH))
    v = pl.BlockSpec((1, S, SW_KV_WIDTH), lambda b, *_: (b, 0, (ROPE_LO + ROPE_WIDTH) // SW_KV_WIDTH))
    return q, k, v


SW_FWD_SPLIT = 2


def _sw_fwd(sink, qkv, tasks=()):
    B, S, _ = qkv.shape
    kw_n = 3 * SW_BLOCK

    def body(sink_ref, q_ref, k_ref, v_ref, o_ref):
        def step(n, carry):
            start, ok = _sw_band(n, S)
            qrows = pl.ds(pl.multiple_of(n * SW_BLOCK, SW_BLOCK), SW_BLOCK)
            krows = pl.ds(start, kw_n)
            kh, vh = _kv_halves(k_ref[0, krows, :]), _kv_halves(v_ref[0, krows, :])
            heads = [(p, e) for p in range(SW_PAIRS) for e in range(2)]
            qp = [q_ref[0, qrows, pl.ds(p * 128, 128)] for p in range(SW_PAIRS)]
            kv_of = lambda p: p // (SW_PAIRS // SW_KV_HEADS)
            st = {(p, e): _nt(kh[(kv_of(p), e)], qp[p]) for p, e in heads}
            pn = {(p, e): _sw_probs(st[(p, e)], ok, sink_ref[2 * p + e])[0].astype(BF16) for p, e in heads}
            outs = [_tn(pn[(p, 0)], vh[(kv_of(p), 0)]) + _tn(pn[(p, 1)], vh[(kv_of(p), 1)]) for p in range(SW_PAIRS)]
            o_ref[0, qrows, :] = jnp.concatenate(outs, axis=1)
            return carry

        half = (S // SW_BLOCK) // SW_FWD_SPLIT
        lax.fori_loop(pl.program_id(1) * half, (pl.program_id(1) + 1) * half, step, 0)

    q, k, v = _sw_specs(S)
    return _hosted_call(
        body, "sw_fwd", (B, SW_FWD_SPLIT), [pl.BlockSpec(memory_space=pltpu.SMEM), q, k, v],
        [pl.BlockSpec((1, S, SW_WIDTH), lambda b, s: (b, 0, 0))], [jax.ShapeDtypeStruct((B, S, SW_WIDTH), F32)],
        (sink, qkv, qkv, qkv), tasks)


def _sw_bwd(sink, qkv, dob):
    B, S, _ = qkv.shape
    kw_n = 3 * SW_BLOCK

    fold_rows = 256

    def body(sink_ref, q_ref, k_ref, v_ref, do_ref, dq_ref, dk_ref, dv_ref, dsink_ref, dk_acc, dv_acc):
        @pl.when(pl.program_id(0) == 0)
        def _():
            dsink_ref[...] = jnp.zeros_like(dsink_ref)

        dk_acc[...] = jnp.zeros_like(dk_acc)
        dv_acc[...] = jnp.zeros_like(dv_acc)
        ppk = SW_PAIRS // SW_KV_HEADS

        def step(n, carry):
            start, ok = _sw_band(n, S)
            qrows = pl.ds(pl.multiple_of(n * SW_BLOCK, SW_BLOCK), SW_BLOCK)
            krows = pl.ds(start, kw_n)
            kh, vh = _kv_halves(k_ref[0, krows, :]), _kv_halves(v_ref[0, krows, :])
            heads = [(p, e) for p in range(SW_PAIRS) for e in range(2)]
            qp = [q_ref[0, qrows, pl.ds(p * 128, 128)] for p in range(SW_PAIRS)]
            dop = [do_ref[0, qrows, pl.ds(p * 128, 128)] for p in range(SW_PAIRS)]
            st = {(p, e): _nt(kh[(p // ppk, e)], qp[p]) for p, e in heads}
            dpt = {(p, e): _nt(vh[(p // ppk, e)], dop[p]) for p, e in heads}
            pnb, dsb = {}, {}
            for p, e in heads:
                pn, psink = _sw_probs(st[(p, e)], ok, sink_ref[2 * p + e])
                delta = jnp.sum(pn * dpt[(p, e)], axis=0, keepdims=True)
                dsb[(p, e)] = (pn * (dpt[(p, e)] - delta)).astype(BF16)
                pnb[(p, e)] = pn.astype(BF16)
                dsink_ref[2 * p + e:2 * p + e + 1, :] += -(psink * delta)
            dq_ref[0, qrows, :] = jnp.concatenate(
                [_tn(dsb[(p, 0)], kh[(p // ppk, 0)]) + _tn(dsb[(p, 1)], kh[(p // ppk, 1)]) for p in range(SW_PAIRS)],
                axis=1).astype(BF16)
            left = lax.broadcasted_iota(jnp.int32, (kw_n, 128), 1) < HEAD_DIM
            dks, dvs = [], []
            for kv in range(SW_KV_HEADS):
                dk = dv = None
                for p in range(kv * ppk, (kv + 1) * ppk):
                    dk_p = jnp.where(left, _nn(dsb[(p, 0)], qp[p]), _nn(dsb[(p, 1)], qp[p]))
                    dv_p = jnp.where(left, _nn(pnb[(p, 0)], dop[p]), _nn(pnb[(p, 1)], dop[p]))
                    dk = dk_p if dk is None else dk + dk_p
                    dv = dv_p if dv is None else dv + dv_p
                dks.append(dk)
                dvs.append(dv)
            dk_acc[krows, :] += jnp.concatenate(dks, axis=1)
            dv_acc[krows, :] += jnp.concatenate(dvs, axis=1)
            return carry

        lax.fori_loop(0, S // SW_BLOCK, step, 0)

        def fold(i, carry):
            rows = pl.ds(pl.multiple_of(i * fold_rows, fold_rows), fold_rows)
            left = lax.broadcasted_iota(jnp.int32, (fold_rows, 128), 1) < HEAD_DIM
            for acc, out_ref in ((dk_acc, dk_ref), (dv_acc, dv_ref)):
                a, b = acc[rows, 0:128], acc[rows, 128:256]
                out_ref[0, rows, :] = jnp.where(left, a + pltpu.roll(a, HEAD_DIM, axis=1),
                                                b + pltpu.roll(b, HEAD_DIM, axis=1)).astype(BF16)
            return carry

        lax.fori_loop(0, S // fold_rows, fold, 0)

        @pl.when(pl.program_id(0) == B - 1)
        def _():
            dsink_ref[...] = jnp.broadcast_to(jnp.sum(dsink_ref[...], axis=1, keepdims=True), dsink_ref.shape)

    q, k, v = _sw_specs(S)
    qo = pl.BlockSpec((1, S, SW_WIDTH), lambda b: (b, 0, 0))
    ko = pl.BlockSpec((1, S, SW_KV_WIDTH), lambda b: (b, 0, 0))
    return pl.pallas_call(
        body, name="sw_bwd", grid=(B,),
        in_specs=[pl.BlockSpec(memory_space=pltpu.SMEM), q, k, v, qo],
        out_specs=[qo, ko, ko, _full((SW_HEADS, 128))],
        out_shape=[jax.ShapeDtypeStruct((B, S, SW_WIDTH), BF16), jax.ShapeDtypeStruct((B, S, SW_KV_WIDTH), BF16),
                   jax.ShapeDtypeStruct((B, S, SW_KV_WIDTH), BF16), jax.ShapeDtypeStruct((SW_HEADS, 128), F32)],
        scratch_shapes=[pltpu.VMEM((S, 2 * SW_KV_WIDTH), F32), pltpu.VMEM((S, 2 * SW_KV_WIDTH), F32)],
        compiler_params=_params("arbitrary"),
    )(sink, qkv, qkv, qkv, dob)


def _pack_sum_adamw(packs, params, pick):
    W = packs.shape[1]
    n_p = len(params)

    def body(p_ref, *refs):
        ins, tot_ref, pick_ref, outs = refs[:3 * n_p], refs[3 * n_p], refs[3 * n_p + 1], refs[3 * n_p + 2:]
        tot = p_ref[0:8, :]
        for d in range(1, N_DEV):
            tot = tot + p_ref[8 * d:8 * d + 8, :]
        tot_ref[...] = tot
        pick_ref[...] = tot[pick[0]:pick[0] + 1, pick[1]:pick[1] + 1]
        for i, (w, _, _, rows, off) in enumerate(params):
            w_ref, m_ref, v_ref = ins[3 * i:3 * i + 3]
            g_ref, d_ref, nm_ref, nv_ref = outs[4 * i:4 * i + 4]
            if w.ndim == 3:
                n_a, n_b, n = w.shape
                for a in range(n_a):
                    for b in range(n_b):
                        o = off + (b * n_a + a) * n
                        g_ref[a, b:b + 1, :] = tot_ref[rows[0]:rows[0] + 1, o:o + n]
                g = g_ref[...]
            else:
                n = w.shape[1]
                g = tot[rows[0]:rows[0] + 1, off:off + n]
                for r in rows[1:]:
                    g = g + tot[r:r + 1, off:off + n]
                g_ref[...] = g
            d_ref[...], nm_ref[...], nv_ref[...] = _adam_update(w_ref[...], g, m_ref[...], v_ref[...])

    res = pl.pallas_call(
        body, name="small_adamw",
        out_shape=[jax.ShapeDtypeStruct((8, W), F32), jax.ShapeDtypeStruct((1, 1), F32)]
        + [jax.ShapeDtypeStruct(p[0].shape, F32) for p in params for _ in range(4)],
        compiler_params=pltpu.CompilerParams(vmem_limit_bytes=VMEM_LIMIT),
    )(packs, *[a for p in params for a in p[:3]])
    return res[0], res[1], [res[2 + 4 * i:6 + 4 * i] for i in range(n_p)]


def _adam_update(w, g, m, v):
    c1 = 1.0 - ADAM_B1 ** ADAM_STEP
    c2 = 1.0 - ADAM_B2 ** ADAM_STEP
    nm = ADAM_B1 * m + (1.0 - ADAM_B1) * g
    nv = ADAM_B2 * v + (1.0 - ADAM_B2) * (g * g)
    return -ADAM_LR * ((nm / c1) / (jnp.sqrt(nv / c2) + ADAM_EPS) + ADAM_WD * w), nm, nv


def _adamw(w, g, m, v, name):
    def body(w_ref, g_ref, m_ref, v_ref, d_ref, nm_ref, nv_ref):
        d_ref[...], nm_ref[...], nv_ref[...] = _adam_update(w_ref[...], g_ref[...], m_ref[...], v_ref[...])

    s = jax.ShapeDtypeStruct(w.shape, F32)
    return pl.pallas_call(body, name=name, out_shape=[s, s, s],
                          compiler_params=pltpu.CompilerParams(vmem_limit_bytes=VMEM_LIMIT))(w, g, m, v)


def _sum_adamw_rows(R):
    return max(r for r in range(16, min(R, 256) + 1, 16) if R % r == 0)


def _sum_adamw_steps(R):
    return R // _sum_adamw_rows(R)


def _sum_adamw(own, recvb, w, m, v, name):
    R, C = own.shape
    rc = _sum_adamw_rows(R)

    def body(own_ref, r_ref, w_ref, m_ref, v_ref, g_ref, d_ref, nm_ref, nv_ref):
        g = own_ref[...]
        for j in range(3):
            g = g + r_ref[j].astype(F32)
        g_ref[...] = g
        d_ref[...], nm_ref[...], nv_ref[...] = _adam_update(w_ref[...], g, m_ref[...], v_ref[...])

    blk = pl.BlockSpec((rc, C), lambda i: (i, 0))
    s = jax.ShapeDtypeStruct((R, C), F32)
    return pl.pallas_call(
        body, name=name, grid=(R // rc,),
        in_specs=[blk, pl.BlockSpec((3, rc, C), lambda i: (0, i, 0)), blk, blk, blk],
        out_specs=[blk, blk, blk, blk], out_shape=[s, s, s, s], compiler_params=_params("parallel"),
    )(own, recvb, w, m, v)


def _by_device(dw):
    return dw.reshape(N_DEV, dw.shape[0] // N_DEV, dw.shape[1])


def _local_step(x, mod, g_attn, w_in, bias, sw_sink, g_na_out, g_sw_out, w_out, g_ffn, w_up, conv_w, conv_b, w_down,
                g_final, target, sharded):
    B, S, D = x.shape
    T = B * S
    x2d = x.reshape(T, D)
    mod3 = mod.reshape(B, 6, D)
    cos_t, sin_t = _rope_tables(S)
    sink = sw_sink.reshape(SW_HEADS)
    n_tiles = T // WIDE_TILE
    full = lambda g: g.reshape(N_DEV * g.shape[1], g.shape[2])

    rider = lambda w, mid, lo, n, into=None: [_gather_task(w, mid, rows=(lo, n), into=into)] if sharded else []
    if sharded:
        qu, hd = w_up.shape[0] // 4, w_down.shape[0] // 2
    (h, qkv), got = _attn_in(x2d, mod3, g_attn, w_in, cos_t, sin_t, S, rider(w_up, 3 * n_tiles // 4, 0, qu) if sharded else [])
    if sharded:
        w_up_buf = got[0][0]
    qkv3 = qkv.reshape(B, S, IN_WIDTH)
    na_steps, sw_steps = B * (NA_PAIRS // NA_STEP_PAIRS), B * SW_FWD_SPLIT
    (oa,), got = _na_fwd(qkv3, bias, rider(w_up, na_steps - 1, qu, 2 * qu, w_up_buf) if sharded else [])
    if sharded:
        w_up_buf = got[0][0]
    oa = oa.reshape(T, NA_WIDTH)
    (ob,), got = _sw_fwd(sink, qkv3,
                         rider(w_up, sw_steps // 2, 3 * qu, qu, w_up_buf) + [_gather_task(w_out, sw_steps - 1)] if sharded else [])
    if sharded:
        w_up, w_out = full(got[0][0]), full(got[1][0])
    ob = ob.reshape(T, SW_WIDTH)
    (mixin, mix, x1), _ = _attn_out(oa, ob, x2d, mod3, g_na_out, g_sw_out, w_out, S)
    (h2, val, gt), got = _ffn_up(x1, mod3, g_ffn, w_up, S, rider(w_down, 3 * n_tiles // 4, 0, 2 * hd) if sharded else [])
    if sharded:
        w_down = full(got[0][0])
    a, act, vd, dx2, df, gstat_f, bstat_f = _ffn_down(gt, val, conv_w, conv_b, w_down, x1, mod3, g_final, target.reshape(T, D), B, S)
    F = val.shape[1]

    dw_down = _matmul_tn(a, df, "dw_down")
    (dval, dgc, cstat), got = _ffn_down_bwd(df, w_down, act, vd, [_swap_task(_by_device(dw_down))] if sharded else [])
    if sharded:
        send_down, own_down = _chip_sums(_by_device(dw_down), got[0][0])
    (du, dx1, dmix, gstat_u, bstat_u, cstat_w), got = _ffn_up_bwd(dgc, dval, gt, conv_w, w_up, x1, mod3, g_ffn, dx2, mix, B, S,
                                                                  [_exchange_task(send_down)] if sharded else [])
    if sharded:
        dw_down = (own_down, got[0][0])
    dw_up = _matmul_tn(du, h2, "dw_up", tm=F)
    dw_out = _matmul_tn(mixin, dmix, "dw_out")
    (doa, dob, gstat_o), got = _attn_out_bwd(dmix, w_out, oa, ob, g_na_out, g_sw_out,
                                             [_swap_task(_by_device(dw_up)), _swap_task(_by_device(dw_out))] if sharded else [])
    if sharded:
        send_up, own_up = _chip_sums(_by_device(dw_up), got[0][0])
        send_out, own_out = _chip_sums(_by_device(dw_out), got[1][0])
    (dqa, dka, dva, dbt), got = _na_bwd(qkv3, bias, doa.reshape(B, S, NA_WIDTH),
                                        [_exchange_task(send_up), _exchange_task(send_out)] if sharded else [])
    if sharded:
        dw_up, dw_out = (own_up, got[0][0]), (own_out, got[1][0])
    dqb, dkb, dvb, dsink = _sw_bwd(sink, qkv3, dob.reshape(B, S, SW_WIDTH))
    r2 = lambda t: t.reshape(T, t.shape[-1])
    grad_x, dproj, gstat_i, bstat_i = _attn_in_bwd(r2(dqa), r2(dka), r2(dva), r2(dqb), r2(dkb), r2(dvb), cos_t, sin_t, w_in, x2d, mod3,
                                                   g_attn, dx1, B, S)
    dw_in = _matmul_tn(dproj, h, "dw_in")

    dmod = jnp.stack([bstat_i[:, 0], bstat_i[:, 1], bstat_u[:, 2], bstat_u[:, 0], bstat_u[:, 1], bstat_f[:, 0]], axis=1)
    small = dict(g_attn=gstat_i[0], g_ffn=gstat_u[0], g_final=gstat_f[0], loss=gstat_f[1, 0], g_na_out=gstat_o[0], g_sw_out=gstat_o[1],
                 sw_sink=dsink[:, 0], conv_b=cstat[0], conv_w=cstat_w[1:4], dbt=dbt,
                 raw=(bstat_i, bstat_u, bstat_f, gstat_i, gstat_u, gstat_f, gstat_o, dsink, cstat, cstat_w))
    return grad_x.reshape(B, S, D), dict(w_in=dw_in, w_out=dw_out, w_up=dw_up, w_down=dw_down), dmod, small


def _pack_slab(raw, drpb):
    D, F = raw[3].shape[1], raw[8].shape[1]
    n_seq = raw[0].shape[0]

    def body(bi_ref, bu_ref, bf_ref, gi_ref, gu_ref, gf_ref, go_ref, ds_ref, cs_ref, cw_ref, rp_ref, o_ref):
        o_ref[...] = jnp.zeros_like(o_ref)
        for b in range(n_seq):
            mods = (bi_ref[b, 0:1, :], bi_ref[b, 1:2, :], bu_ref[b, 2:3, :], bu_ref[b, 0:1, :], bu_ref[b, 1:2, :], bf_ref[b, 0:1, :])
            for k, row in enumerate(mods):
                o_ref[b:b + 1, k * D:(k + 1) * D] = row
        o = 0
        for row in (gi_ref[0:1, :], gu_ref[0:1, :], gf_ref[0:1, :], go_ref[0:1, :], go_ref[1:2, :]):
            o_ref[2:3, o:o + row.shape[1]] = row
            o += row.shape[1]
        ds = ds_ref[...]
        eye = lax.broadcasted_iota(jnp.int32, ds.shape, 0) == lax.broadcasted_iota(jnp.int32, ds.shape, 1)
        o_ref[2:3, o:o + 128] = jnp.sum(jnp.where(eye, ds, 0.0), axis=0, keepdims=True)
        o_ref[2:3, o + 128:o + 256] = gf_ref[1:2, 0:128]
        o_ref[3:4, 0:F] = cs_ref[0:1, :]
        o_ref[4:5, 0:rp_ref.shape[1]] = rp_ref[...]
        o_ref[5:8, 0:F] = cw_ref[1:4, :]

    return pl.pallas_call(body, name="pack_slab", out_shape=jax.ShapeDtypeStruct((8, PACK_W), F32),
                          compiler_params=pltpu.CompilerParams(vmem_limit_bytes=VMEM_LIMIT))(*raw, drpb)


def kernel(x, c, w_ada, b_ada, g_attn, w_in, na_rpb, sw_sink, g_na_out, g_sw_out, w_out, g_ffn, w_up, conv_w, conv_b, w_down, g_final, loss_target, m_w_ada, m_b_ada, m_g_attn, m_w_in, m_na_rpb, m_sw_sink, m_g_na_out, m_g_sw_out, m_w_out, m_g_ffn, m_w_up, m_conv_w, m_conv_b, m_w_down, m_g_final, v_w_ada, v_b_ada, v_g_attn, v_w_in, v_na_rpb, v_sw_sink, v_g_na_out, v_g_sw_out, v_w_out, v_g_ffn, v_w_up, v_conv_w, v_conv_b, v_w_down, v_g_final):
    B, S, D = x.shape
    me = 4 * lax.axis_index("x") + 2 * lax.axis_index("y") + lax.axis_index("c")
    ada_c = w_ada.shape[2]
    F_l = conv_w.shape[2]

    tr = {"w_in", "w_up"}
    w_in_t = jnp.transpose(w_in[0])
    shards_f = dict(w_out=w_out[0], w_up=jnp.transpose(w_up[0]), w_down=w_down[0])

    c_all, conv_w_f, mod_all, w_in_all, casts, bias = _ada_fwd(c, conv_w[0], w_ada[0], b_ada, w_in_t, list(shards_f.values()),
                                                               _na_bias_rev(na_rpb[0]))
    shards = dict(zip(shards_f, casts))
    mod_mine = lax.dynamic_slice(mod_all, (0, me * B, 0), (N_DEV, B, ada_c))
    mod = jnp.transpose(mod_mine, (1, 0, 2)).reshape(B, N_DEV * ada_c)
    w_in_f = w_in_all.reshape(N_DEV * w_in_t.shape[0], D)

    grad_x, dw, dmod, small = _local_step(x, mod, g_attn, w_in_f, bias, sw_sink, g_na_out, g_sw_out, shards["w_out"], g_ffn,
                                          shards["w_up"], conv_w_f, conv_b, shards["w_down"], g_final.reshape(1, D), loss_target,
                                          sharded=True)
    g8_in = _by_device(dw["w_in"])
    drpb, _ = _na_bias_grad(small["dbt"])

    slab = _pack_slab(small["raw"], drpb.reshape(1, -1))
    weights = dict(w_ada=w_ada, b_ada=b_ada, g_attn=g_attn, w_in=w_in, na_rpb=na_rpb, sw_sink=sw_sink, g_na_out=g_na_out,
                   g_sw_out=g_sw_out, w_out=w_out, g_ffn=g_ffn, w_up=w_up, conv_w=conv_w, conv_b=conv_b, w_down=w_down, g_final=g_final)
    ms = dict(w_ada=m_w_ada, b_ada=m_b_ada, g_attn=m_g_attn, w_in=m_w_in, na_rpb=m_na_rpb, sw_sink=m_sw_sink, g_na_out=m_g_na_out,
              g_sw_out=m_g_sw_out, w_out=m_w_out, g_ffn=m_g_ffn, w_up=m_w_up, conv_w=m_conv_w, conv_b=m_conv_b, w_down=m_w_down, g_final=m_g_final)
    vs = dict(w_ada=v_w_ada, b_ada=v_b_ada, g_attn=v_g_attn, w_in=v_w_in, na_rpb=v_na_rpb, sw_sink=v_sw_sink, g_na_out=v_g_na_out,
              g_sw_out=v_g_sw_out, w_out=v_w_out, g_ffn=v_g_ffn, w_up=v_w_up, conv_w=v_conv_w, conv_b=v_conv_b, w_down=v_w_down, g_final=v_g_final)
    names = list(weights)
    grads, deltas, new_m, new_v = {}, {}, {}, {}
    flat = lambda t: t.reshape(1, -1)

    def shard2d(nm):
        if nm in tr:
            return (lambda t: jnp.transpose(t[0])), (lambda t: jnp.transpose(t)[None])
        if nm == "conv_w":
            return (lambda t: jnp.transpose(t, (1, 0, 2))), (lambda t: jnp.transpose(t, (1, 0, 2)))
        return (lambda t: t[0]), (lambda t: t[None])

    def finish_sum(nm, own, recvb):
        r, back = shard2d(nm)
        g2, d_, m_, v_ = _sum_adamw(own, recvb, r(weights[nm]), r(ms[nm]), r(vs[nm]), "adamw_" + nm)
        grads[nm], deltas[nm], new_m[nm], new_v[nm] = back(g2), back(d_), back(m_), back(v_)

    own_in, recv_in, packs = _reduce_scatter_call(g8_in, slab)
    finish_sum("w_up", *dw["w_up"])
    finish_sum("w_down", *dw["w_down"])
    finish_sum("w_in", own_in, recv_in)
    finish_sum("w_out", *dw["w_out"])

    where = dict(b_ada=((0, 1), 0), g_attn=((2,), 0), g_ffn=((2,), D), g_final=((2,), 2 * D), g_na_out=((2,), 3 * D),
                 g_sw_out=((2,), 3 * D + NA_WIDTH), sw_sink=((2,), 3 * D + NA_WIDTH + SW_WIDTH), conv_b=((3,), 0), na_rpb=((4,), 0))

    def small_view(n):
        if n == "na_rpb":
            return (lambda t: jnp.transpose(t[0], (1, 0, 2))), (lambda t: jnp.transpose(t, (1, 0, 2))[None])
        return flat, (lambda t: t.reshape(weights[n].shape))

    tot, loss, small_out = _pack_sum_adamw(
        packs.reshape(N_DEV * 8, PACK_W),
        [tuple(small_view(n)[0](t[n]) for t in (weights, ms, vs)) + where[n] for n in where],
        (2, 3 * D + NA_WIDTH + SW_WIDTH + 128))
    for n, res in zip(where, small_out):
        grads[n], deltas[n], new_m[n], new_v[n] = [small_view(n)[1](t) for t in res]
    loss = loss.reshape(())

    for nm, g2 in (("w_ada", _ada_bwd(c_all, packs.reshape(N_DEV * 8, PACK_W), ada_c)), ("conv_w", lax.dynamic_slice(tot, (5, me * F_l), (3, F_l))[:, None])):
        r, back = shard2d(nm)
        d_, m_, v_ = _adamw(r(weights[nm]), g2, r(ms[nm]), r(vs[nm]), "adamw_" + nm)
        grads[nm], deltas[nm], new_m[nm], new_v[nm] = back(g2), back(d_), back(m_), back(v_)
    return (loss, grad_x, *[grads[n] for n in names], *[deltas[n] for n in names], *[new_m[n] for n in names],
            *[new_v[n] for n in names])
```

```python
import functools

import numpy as np
import jax
import jax.numpy as jnp
from jax import lax
from jax.experimental import pallas as pl
from jax.experimental.pallas import tpu as pltpu

F32, BF16 = jnp.float32, jnp.bfloat16
MESH_ID = pl.DeviceIdType.MESH
N_DEV = 8

HEAD_DIM = 64
NA_HEADS = 8
SW_HEADS = 8
SW_KV_HEADS = 2
SW_GROUP = SW_HEADS // SW_KV_HEADS
NA_WIDTH = NA_HEADS * HEAD_DIM
SW_WIDTH = SW_HEADS * HEAD_DIM
SW_KV_WIDTH = SW_KV_HEADS * HEAD_DIM
ROPE_WIDTH = SW_WIDTH + SW_KV_WIDTH
IN_WIDTH = 3 * NA_WIDTH + SW_WIDTH + 2 * SW_KV_WIDTH
ROPE_LO = 3 * NA_WIDTH
GRID_W = 64
NA_ROWS_MAX = 8
NA_COLS = 16
N_DR = 2 * NA_ROWS_MAX - 1
N_DC = 2 * NA_COLS - 1
SW_WINDOW = 128
SW_BLOCK = 128
ROPE_THETA = 10000.0
EPS = 1e-6
NEG = -1e30
Q_SCALE = HEAD_DIM ** -0.5

ADAM_LR = 0.001
ADAM_B1 = 0.9
ADAM_B2 = 0.999
ADAM_EPS = 1e-08
ADAM_WD = 0.01
ADAM_STEP = 10

TOKEN_TILE = 256
WIDE_TILE = 512
VMEM_LIMIT = 56 * 1024 * 1024

PACK_W = 6144


def _nn(a, b):
    return jnp.dot(a, b, preferred_element_type=F32)


def _nt(a, b):
    return lax.dot_general(a, b, (((1,), (1,)), ((), ())), preferred_element_type=F32)


def _tn(a, b):
    return lax.dot_general(a, b, (((0,), (0,)), ((), ())), preferred_element_type=F32)


def _rms(x):
    r = lax.rsqrt(jnp.mean(x * x, axis=-1, keepdims=True) + EPS)
    return x * r, r


def _rms_bwd(xn, r, gy):
    return r * (gy - xn * jnp.mean(xn * gy, axis=-1, keepdims=True))


def _params(*sem):
    return pltpu.CompilerParams(dimension_semantics=sem, vmem_limit_bytes=VMEM_LIMIT)


def _full(shape):
    n = len(shape)
    return pl.BlockSpec(shape, lambda *_: (0,) * n)


def _mesh_pos():
    return lax.axis_index("x"), lax.axis_index("y"), lax.axis_index("c")


def _row_chunk(r):
    for rc in (128, 64, 32, 16):
        if r % rc == 0:
            return rc
    raise ValueError(f"rows {r} not a multiple of 16")


class _Task:
    def __init__(self, inputs, out_shapes, sems, start, finish, mid=None, mid_step=None, alias=None):
        self.inputs, self.out_shapes, self.sems = list(inputs), list(out_shapes), list(sems)
        self.start, self.finish, self.mid, self.mid_step = start, finish, mid, mid_step
        self.alias = alias


def _hosted_call(body, name, grid, in_specs, out_specs, out_shape, operands, tasks, scratch_shapes=()):
    n_in, n_out, n_scr = len(in_specs), len(out_specs), len(scratch_shapes)
    t_in = [len(t.inputs) for t in tasks]
    t_out = [len(t.out_shapes) for t in tasks]
    t_sem = [len(t.sems) for t in tasks]
    n_steps = int(np.prod(grid))

    def wrapped(*refs):
        ins, rest = refs[:n_in], refs[n_in:]
        task_ins, rest = rest[:sum(t_in)], rest[sum(t_in):]
        outs, rest = rest[:n_out], rest[n_out:]
        task_outs, rest = rest[:sum(t_out)], rest[sum(t_out):]
        scr, task_sems = rest[:n_scr], rest[n_scr:]
        step = pl.program_id(0)
        for ax in range(1, len(grid)):
            step = step * grid[ax] + pl.program_id(ax)
        parts = []
        oi = oo = os_ = 0
        for t, a, b, c in zip(tasks, t_in, t_out, t_sem):
            parts.append((t, task_ins[oi:oi + a], task_outs[oo:oo + b], task_sems[os_:os_ + c]))
            oi, oo, os_ = oi + a, oo + b, os_ + c
        for t, ti, to, ts in parts:
            pl.when(step == 0)(functools.partial(t.start, ti, to, ts))
            if t.mid is not None:
                pl.when(step == t.mid_step)(functools.partial(t.mid, ti, to, ts))
        body(*ins, *outs, *scr)
        for t, ti, to, ts in parts:
            pl.when(step == n_steps - 1)(functools.partial(t.finish, ti, to, ts))

    hbm = pl.BlockSpec(memory_space=pl.ANY)
    aliases, oi, oo = {}, n_in, n_out
    for t, a, b in zip(tasks, t_in, t_out):
        if t.alias is not None:
            aliases[oi + t.alias[0]] = oo + t.alias[1]
        oi, oo = oi + a, oo + b
    res = pl.pallas_call(
        wrapped, name=name, grid=grid,
        in_specs=list(in_specs) + [hbm] * sum(t_in),
        out_specs=list(out_specs) + [hbm] * sum(t_out),
        out_shape=list(out_shape) + [s for t in tasks for s in t.out_shapes],
        scratch_shapes=list(scratch_shapes) + [s for t in tasks for s in t.sems],
        input_output_aliases=aliases,
        compiler_params=_params(*(["arbitrary"] * len(grid))),
    )(*operands, *[a for t in tasks for a in t.inputs])
    own, extra = res[:n_out], res[n_out:]
    per_task, o = [], 0
    for b in t_out:
        per_task.append(extra[o:o + b])
        o += b
    return own, per_task


def _gather_task(shard, mid_step, rows=None, into=None):
    lo, n = (0, shard.shape[0]) if rows is None else rows

    def parts(ins, outs, sems):
        x_ref, out_ref, (send_sems, recv_sems, local_sem) = ins[0], outs[0], sems
        x_, y_, c_ = _mesh_pos()
        me, sibling = (x_, y_, c_), (x_, y_, 1 - c_)
        chips = [(1 - x_, y_), (x_, 1 - y_), (1 - x_, 1 - y_)]
        x_ref = x_ref.at[pl.ds(lo, n)]

        def rows(px, py, pc):
            return out_ref.at[4 * px + 2 * py + pc, pl.ds(lo, n)]

        def copy(k, block, to, src=None):
            return pltpu.make_async_remote_copy(
                src_ref=rows(*block) if src is None else src, dst_ref=rows(*block),
                send_sem=send_sems.at[k], recv_sem=recv_sems.at[k], device_id=to, device_id_type=MESH_ID)

        return dict(
            mine=lambda: pltpu.make_async_copy(x_ref, rows(*me), local_sem),
            first=lambda: [copy(0, me, sibling, src=x_ref)] + [copy(1 + j, me, (*chip, c_), src=x_ref) for j, chip in enumerate(chips)],
            passed=lambda: [copy(4 + j, (*chip, c_), sibling) for j, chip in enumerate(chips)],
            landed=lambda: [copy(1 + j, (*chip, c_), me) for j, chip in enumerate(chips)],
            last=lambda: [copy(0, sibling, me)] + [copy(4 + j, (*chip, 1 - c_), me) for j, chip in enumerate(chips)])

    def start(ins, outs, sems):
        p = parts(ins, outs, sems)
        p["mine"]().start()
        for cp in p["first"]():
            cp.start()

    def mid(ins, outs, sems):
        p = parts(ins, outs, sems)
        for cp, fw in zip(p["landed"](), p["passed"]()):
            cp.wait_recv()
            fw.start()

    def finish(ins, outs, sems):
        p = parts(ins, outs, sems)
        for cp in p["last"]():
            cp.wait_recv()
        for cp in p["first"]() + p["passed"]():
            cp.wait_send()
        p["mine"]().wait()

    return _Task([shard] if into is None else [shard, into], [jax.ShapeDtypeStruct((N_DEV,) + shard.shape, shard.dtype)],
                 [pltpu.SemaphoreType.DMA((7,)), pltpu.SemaphoreType.DMA((7,)), pltpu.SemaphoreType.DMA],
                 start, finish, mid, mid_step, alias=None if into is None else (1, 0))


def _swap_task(g8):
    _, R, C = g8.shape

    def copies(ins, outs, sems):
        (g_ref,), (recv_ref,), (ss, rs) = ins, outs, sems
        x_, y_, c_ = _mesh_pos()
        return [pltpu.make_async_remote_copy(src_ref=g_ref.at[2 * k + (1 - c_)], dst_ref=recv_ref.at[k], send_sem=ss.at[k],
                                             recv_sem=rs.at[k], device_id=(x_, y_, 1 - c_), device_id_type=MESH_ID)
                for k in range(4)]

    def start(ins, outs, sems):
        for cp in copies(ins, outs, sems):
            cp.start()

    def finish(ins, outs, sems):
        cps = copies(ins, outs, sems)
        for cp in cps:
            cp.wait_recv()
        for cp in cps:
            cp.wait_send()

    return _Task([g8], [jax.ShapeDtypeStruct((4, R, C), g8.dtype)],
                 [pltpu.SemaphoreType.DMA((4,)), pltpu.SemaphoreType.DMA((4,))], start, finish)


def _chip_sums(g8, recva):
    _, R, C = g8.shape
    rt = _sum_adamw_rows(R)
    rc = _row_chunk(rt)

    def body(core_ref, g_ref, a_ref, send_ref, own_ref):
        x_, y_, _ = _mesh_pos()
        chips = [(1 - x_, y_), (x_, 1 - y_), (1 - x_, 1 - y_), (x_, y_)]

        def chunk(i, carry):
            rows = pl.ds(pl.multiple_of(i * rc, rc), rc)
            for j, (tx, ty) in enumerate(chips):
                k = 2 * tx + ty
                s = g_ref[k, rows, :].astype(F32) + a_ref[k, rows, :].astype(F32)
                if j < 3:
                    send_ref[j, rows, :] = s.astype(BF16)
                else:
                    own_ref[rows, :] = s
            return carry

        lax.fori_loop(0, rt // rc, chunk, 0)

    core = lax.axis_index("c").astype(jnp.int32).reshape(1)
    return pl.pallas_call(
        body, name="chip_sums",
        grid_spec=pltpu.PrefetchScalarGridSpec(
            num_scalar_prefetch=1, grid=(R // rt,),
            in_specs=[pl.BlockSpec((4, None, rt, C), lambda i, c: (0, c[0], i, 0)), pl.BlockSpec((4, rt, C), lambda i, c: (0, i, 0))],
            out_specs=[pl.BlockSpec((3, rt, C), lambda i, c: (0, i, 0)), pl.BlockSpec((rt, C), lambda i, c: (i, 0))]),
        out_shape=[jax.ShapeDtypeStruct((3, R, C), BF16), jax.ShapeDtypeStruct((R, C), F32)],
        compiler_params=_params("parallel"),
    )(core, g8.reshape(4, 2, R, C), recva)


def _exchange_task(sendb):
    def copies(ins, outs, sems):
        (s_ref,), (recv_ref,), (ss, rs) = ins, outs, sems
        x_, y_, c_ = _mesh_pos()
        flips = [(1 - x_, y_), (x_, 1 - y_), (1 - x_, 1 - y_)]
        return [pltpu.make_async_remote_copy(src_ref=s_ref.at[j], dst_ref=recv_ref.at[j], send_sem=ss.at[j], recv_sem=rs.at[j],
                                             device_id=(tx, ty, c_), device_id_type=MESH_ID) for j, (tx, ty) in enumerate(flips)]

    def start(ins, outs, sems):
        for cp in copies(ins, outs, sems):
            cp.start()

    def finish(ins, outs, sems):
        cps = copies(ins, outs, sems)
        for cp in cps:
            cp.wait_recv()
        for cp in cps:
            cp.wait_send()

    return _Task([sendb], [jax.ShapeDtypeStruct(sendb.shape, sendb.dtype)],
                 [pltpu.SemaphoreType.DMA((3,)), pltpu.SemaphoreType.DMA((3,))], start, finish)


def _reduce_scatter_call(g8, slab):
    _, R, C = g8.shape
    rc = _row_chunk(R)
    t_s = _gather_task(slab, 0)

    def body(g_ref, slab_ref, recva_ref, recvb_ref, own_ref, packs_ref, mine, theirs, send_vm, ld_sems, in_sems, sw_s, sw_r,
             ex_s, ex_r, *s_sems):
        x_, y_, c_ = _mesh_pos()
        chips = [(1 - x_, y_), (x_, 1 - y_), (1 - x_, 1 - y_), (x_, y_)]
        swaps, loads = [], []
        for j, (tx, ty) in enumerate(chips):
            k = 2 * tx + ty
            swaps.append(pltpu.make_async_remote_copy(src_ref=g_ref.at[2 * k + (1 - c_)], dst_ref=recva_ref.at[j], send_sem=sw_s.at[j],
                                                      recv_sem=sw_r.at[j], device_id=(x_, y_, 1 - c_), device_id_type=MESH_ID))
            loads.append(pltpu.make_async_copy(g_ref.at[2 * k + c_], mine.at[j], ld_sems.at[j]))
        order = (2, 0, 1, 3)
        for j in order:
            swaps[j].start()
            loads[j].start()
        t_s.start((slab_ref,), (packs_ref,), s_sems)
        sends = []
        for j in order:
            tx, ty = chips[j]
            swaps[j].wait_recv()
            cp = pltpu.make_async_copy(recva_ref.at[j], theirs.at[j], in_sems.at[j])
            cp.start()
            cp.wait()
            loads[j].wait()

            def chunk(i, carry):
                rows = pl.ds(pl.multiple_of(i * rc, rc), rc)
                s = mine[j, rows, :].astype(F32) + theirs[j, rows, :].astype(F32)
                if j < 3:
                    send_vm[j, rows, :] = s.astype(BF16)
                else:
                    own_ref[rows, :] = s
                return carry

            lax.fori_loop(0, R // rc, chunk, 0)
            if j < 3:
                sends.append(pltpu.make_async_remote_copy(src_ref=send_vm.at[j], dst_ref=recvb_ref.at[j], send_sem=ex_s.at[j],
                                                          recv_sem=ex_r.at[j], device_id=(tx, ty, c_), device_id_type=MESH_ID))
                sends[-1].start()
        t_s.mid((slab_ref,), (packs_ref,), s_sems)
        t_s.finish((slab_ref,), (packs_ref,), s_sems)
        for cp in sends:
            cp.wait_recv()
        for cp in sends + swaps:
            cp.wait_send()

    hbm, vm = pl.BlockSpec(memory_space=pl.ANY), pl.BlockSpec(memory_space=pltpu.VMEM)
    blocks = lambda n: jax.ShapeDtypeStruct((n, R, C), BF16)
    _, recvb, own, packs = pl.pallas_call(
        body, name="reduce_scatter_w_in", in_specs=[hbm, hbm], out_specs=[hbm, hbm, vm, hbm],
        out_shape=[blocks(4), blocks(3), jax.ShapeDtypeStruct((R, C), F32)] + t_s.out_shapes,
        scratch_shapes=[pltpu.VMEM((4, R, C), BF16), pltpu.VMEM((4, R, C), BF16), pltpu.VMEM((3, R, C), BF16)]
        + [pltpu.SemaphoreType.DMA((4,))] * 4 + [pltpu.SemaphoreType.DMA((3,))] * 2 + t_s.sems,
        compiler_params=pltpu.CompilerParams(vmem_limit_bytes=VMEM_LIMIT),
    )(g8, slab)
    return own, recvb, packs


def _silu(v):
    return v * (1.0 / (1.0 + jnp.exp(-v)))


def _ada_fwd(c, taps, w_ada_l, b_ada, w_in_shard, casts, rpb_rev):
    n_seq = c.shape[0]
    D, cols = w_ada_l.shape
    F_l = taps.shape[1]
    W = D + F_l
    n_rows = N_DEV * n_seq
    n_c = len(casts)
    t_c = _gather_task(jax.ShapeDtypeStruct((8, W), F32), 0)
    t_w = _gather_task(jax.ShapeDtypeStruct(w_in_shard.shape, BF16), 0)
    t_m = _gather_task(jax.ShapeDtypeStruct((n_rows, cols), F32), 0)

    def body(c_ref, taps_ref, w_ref, b_ref, ws_ref, rev_ref, *refs):
        cast_in, refs = refs[:n_c], refs[n_c:]
        (slabs_ref, call_ref, taps_all_ref, mod_ref, win_ref), refs = refs[:5], refs[5:]
        cast_out, bias_ref, refs = refs[:n_c], refs[n_c], refs[n_c + 1:]
        (slab_vm, c_vm, m_vm, ws_f, ws_b), refs = refs[:5], refs[5:]
        stage_f, stage_b, refs = refs[:n_c], refs[n_c:2 * n_c], refs[2 * n_c:]
        copy_sem, in_sems, out_sems, sems = refs[0], refs[1], refs[2], refs[3:]
        sc, sw, sm = sems[0:3], sems[3:6], sems[6:9]
        x_, y_, c_ = _mesh_pos()
        slab_vm[...] = jnp.zeros_like(slab_vm)
        slab_vm[0:n_seq, 0:D] = c_ref[...]
        slab_vm[0:taps.shape[0], D:W] = taps_ref[...]
        t_c.start((slab_vm,), (slabs_ref,), sc)
        cp = pltpu.make_async_copy(ws_ref, ws_f, copy_sem)
        cp.start()
        cp.wait()
        ws_b[...] = ws_f[...].astype(BF16)
        t_w.start((ws_b,), (win_ref,), sw)
        loads = [pltpu.make_async_copy(cast_in[j], stage_f[j], in_sems.at[j]) for j in range(n_c)]
        for ld in loads:
            ld.start()
        t_c.mid((slab_vm,), (slabs_ref,), sc)
        t_c.finish((slab_vm,), (slabs_ref,), sc)
        cp = pltpu.make_async_copy(slabs_ref, c_vm, copy_sem)
        cp.start()
        cp.wait()
        c_all = c_vm[:, :, 0:D].reshape(N_DEV * 8, D)
        call_ref[...] = c_all
        for j in range(N_DEV):
            taps_all_ref[:, j * F_l:(j + 1) * F_l] = c_vm[j, 0:taps.shape[0], D:W]
        b_mine = b_ref[:, pl.ds(pl.multiple_of((4 * x_ + 2 * y_ + c_) * cols, 128), cols)]
        m64 = jnp.dot(_silu(c_all), w_ref[...], precision=lax.Precision.HIGHEST, preferred_element_type=F32) + b_mine
        r = lax.broadcasted_iota(jnp.int32, (n_rows, N_DEV * 8), 0)
        c = lax.broadcasted_iota(jnp.int32, (n_rows, N_DEV * 8), 1)
        pick = jnp.where(c == 8 * (r // n_seq) + r % n_seq, 1.0, 0.0)
        m_vm[...] = jnp.dot(pick, m64, precision=lax.Precision.HIGHEST, preferred_element_type=F32)
        t_m.start((m_vm,), (mod_ref,), sm)
        stores = [pltpu.make_async_copy(stage_b[j], cast_out[j], out_sems.at[j]) for j in range(n_c)]
        for j in range(n_c):
            loads[j].wait()
            stage_b[j][...] = stage_f[j][...].astype(BF16)
            stores[j].start()
        for p in range(NA_PAIRS):
            _na_bias_rows(rev_ref, bias_ref, p)
        t_w.mid((ws_b,), (win_ref,), sw)
        t_m.mid((m_vm,), (mod_ref,), sm)
        t_m.finish((m_vm,), (mod_ref,), sm)
        t_w.finish((ws_b,), (win_ref,), sw)
        for st in stores:
            st.wait()

    hbm, vm = pl.BlockSpec(memory_space=pl.ANY), pl.BlockSpec(memory_space=pltpu.VMEM)
    res = pl.pallas_call(
        body, name="ada_fwd", in_specs=[vm, vm, vm, vm, hbm, vm] + [hbm] * n_c,
        out_specs=[hbm, vm, vm, hbm, hbm] + [hbm] * n_c + [vm],
        out_shape=t_c.out_shapes + [jax.ShapeDtypeStruct((N_DEV * 8, D), F32), jax.ShapeDtypeStruct((taps.shape[0], N_DEV * F_l), F32)]
        + t_m.out_shapes + t_w.out_shapes
        + [jax.ShapeDtypeStruct(a.shape, BF16) for a in casts] + [jax.ShapeDtypeStruct((NA_PAIRS, N_DR * GRID_W, 128), F32)],
        scratch_shapes=[pltpu.VMEM((8, W), F32), pltpu.VMEM((N_DEV, 8, W), F32), pltpu.VMEM((n_rows, cols), F32),
                        pltpu.VMEM(w_in_shard.shape, F32), pltpu.VMEM(w_in_shard.shape, BF16)]
        + [pltpu.VMEM(a.shape, F32) for a in casts] + [pltpu.VMEM(a.shape, BF16) for a in casts]
        + [pltpu.SemaphoreType.DMA, pltpu.SemaphoreType.DMA((n_c,)), pltpu.SemaphoreType.DMA((n_c,))]
        + t_c.sems + t_w.sems + t_m.sems,
        compiler_params=pltpu.CompilerParams(vmem_limit_bytes=VMEM_LIMIT),
    )(c, taps, w_ada_l, b_ada, w_in_shard, rpb_rev, *casts)
    return res[1], res[2], res[3], res[4], res[5:5 + n_c], res[5 + n_c]


def _ada_bwd(c_all, packs, cols):
    def body(c_ref, d_ref, o_ref):
        x_, y_, c_ = _mesh_pos()
        mine = d_ref[:, pl.ds(pl.multiple_of((4 * x_ + 2 * y_ + c_) * cols, 128), cols)]
        o_ref[...] = lax.dot_general(_silu(c_ref[...]), mine, (((0,), (0,)), ((), ())),
                                     precision=lax.Precision.HIGHEST, preferred_element_type=F32)
    return pl.pallas_call(body, name="ada_bwd", out_shape=jax.ShapeDtypeStruct((c_all.shape[1], cols), F32),
                          compiler_params=pltpu.CompilerParams(vmem_limit_bytes=VMEM_LIMIT))(c_all, packs)


NA_PAIRS = NA_HEADS // 2
N_DR_PAD = 16


def _na_bias_rev(na_rpb):
    rev = jnp.pad(jnp.flip(na_rpb, axis=2), ((0, 0), (0, N_DR_PAD - N_DR), (0, GRID_W - N_DC)))
    return jnp.transpose(rev.reshape(NA_PAIRS, 2, N_DR_PAD, GRID_W), (0, 2, 1, 3)).reshape(NA_PAIRS, N_DR_PAD, 128)


def _na_bias_rows(r_ref, o_ref, p):
    k = lax.broadcasted_iota(jnp.int32, (GRID_W, 128), 0)
    lane = lax.broadcasted_iota(jnp.int32, (GRID_W, 128), 1)
    q = lane % GRID_W
    cs = jnp.clip(q - NA_COLS // 2, 0, GRID_W - NA_COLS)
    ok = (k >= cs) & (k < cs + NA_COLS)
    left = lane < GRID_W
    for dr in range(N_DR):
        row = jnp.broadcast_to(r_ref[p, dr:dr + 1, :], (GRID_W, 128))
        r0 = jnp.where(left, row, 0.0)
        r1 = jnp.where(left, pltpu.roll(row, GRID_W, axis=1), 0.0)
        y0 = pltpu.roll(r0, 128 - (NA_COLS - 1), axis=1, stride=1, stride_axis=0)
        y1 = pltpu.roll(r1, GRID_W - (NA_COLS - 1), axis=1, stride=1, stride_axis=0)
        o_ref[p, dr * GRID_W:(dr + 1) * GRID_W, :] = jnp.where(ok, jnp.where(left, y0, y1), NEG)


def _na_bias_grad(db, tasks=()):
    a = np.arange(128)
    flip = jnp.asarray(((a[:, None] // GRID_W == a[None, :] // GRID_W)
                        & (a[:, None] % GRID_W + a[None, :] % GRID_W == GRID_W - 1)).astype(np.float32))

    def body(d_ref, j_ref, o_ref):
        o_ref[...] = jnp.zeros_like(o_ref)
        for dr in range(N_DR):
            t = jnp.dot(d_ref[0, dr * GRID_W:(dr + 1) * GRID_W, :], j_ref[...], precision=lax.Precision.HIGHEST, preferred_element_type=F32)
            t = pltpu.roll(t, GRID_W + NA_COLS, axis=1, stride=1, stride_axis=0)
            o_ref[0, dr:dr + 1, :] = jnp.sum(t, axis=0, keepdims=True)

    (rows,), got = _hosted_call(
        body, "rpb_reduce", (NA_PAIRS,),
        [pl.BlockSpec((1, N_DR * GRID_W, 128), lambda p: (p, 0, 0)), _full((128, 128))],
        [pl.BlockSpec((1, N_DR_PAD, 128), lambda p: (p, 0, 0))],
        [jax.ShapeDtypeStruct((NA_PAIRS, N_DR_PAD, 128), F32)], (db, flip), tasks)
    g = rows.reshape(NA_PAIRS, N_DR_PAD, 2, GRID_W)[:, :N_DR, :, :N_DC]
    return jnp.transpose(g, (0, 2, 1, 3)).reshape(-1), got


def _rope_tables(S):
    half = HEAD_DIM // 2
    inv = np.float32(ROPE_THETA) ** (-np.arange(half, dtype=np.float32) / np.float32(half))
    ang = np.arange(S).astype(np.float32)[:, None] * inv[None, :]
    cos, sin = np.cos(ang).astype(np.float32), np.sin(ang).astype(np.float32)
    return jnp.asarray(np.tile(np.concatenate([cos, cos], axis=1), (1, 2))), jnp.asarray(np.tile(np.concatenate([-sin, sin], axis=1), (1, 2)))


def _rope_spec(tps, tm=TOKEN_TILE):
    return pl.BlockSpec((tm, 2 * HEAD_DIM), lambda i: (i % tps, 0))


def _rot_half(t):
    w = t.shape[1]
    lane = lax.broadcasted_iota(jnp.int32, t.shape, 1)
    return jnp.where((lane % HEAD_DIM) < HEAD_DIM // 2, pltpu.roll(t, w - HEAD_DIM // 2, axis=1),
                     pltpu.roll(t, HEAD_DIM // 2, axis=1))


def _tok_spec(w, tm=TOKEN_TILE):
    return pl.BlockSpec((tm, w), lambda i: (i, 0))


def _mod_spec(tps, d):
    return pl.BlockSpec((1, 6, d), lambda i: (i // tps, 0, 0))


def _bstat_spec(tps, w):
    return pl.BlockSpec((1, 8, w), lambda i: (i // tps, 0, 0))


def _attn_in(x2d, mod3, g_attn, w_in, cos_t, sin_t, S, tasks=(), tm=WIDE_TILE):
    T, D = x2d.shape
    tps = S // tm

    def body(x_ref, mod_ref, g_ref, w_ref, cos_ref, sin_ref, h_ref, qkv_ref):
        xn, _ = _rms(x_ref[...])
        h = (xn * g_ref[...]) * (1.0 + mod_ref[0, 1:2, :]) + mod_ref[0, 0:1, :]
        hb = h.astype(BF16)
        h_ref[...] = hb
        proj = _nt(hb, w_ref[...])
        rb = proj[:, ROPE_LO:ROPE_LO + ROPE_WIDTH]
        reps = (1, ROPE_WIDTH // (2 * HEAD_DIM))
        rb = rb * jnp.tile(cos_ref[...], reps) + _rot_half(rb) * jnp.tile(sin_ref[...], reps)
        qkv_ref[:, 0:NA_WIDTH] = (proj[:, 0:NA_WIDTH] * Q_SCALE).astype(BF16)
        qkv_ref[:, NA_WIDTH:ROPE_LO] = proj[:, NA_WIDTH:ROPE_LO].astype(BF16)
        qkv_ref[:, ROPE_LO:ROPE_LO + SW_WIDTH] = (rb[:, 0:SW_WIDTH] * Q_SCALE).astype(BF16)
        qkv_ref[:, ROPE_LO + SW_WIDTH:ROPE_LO + ROPE_WIDTH] = rb[:, SW_WIDTH:].astype(BF16)
        qkv_ref[:, ROPE_LO + ROPE_WIDTH:] = proj[:, ROPE_LO + ROPE_WIDTH:].astype(BF16)

    return _hosted_call(
        body, "attn_in", (T // tm,),
        [_tok_spec(D, tm), _mod_spec(tps, D), _full((1, D)), _full(w_in.shape), _rope_spec(tps, tm), _rope_spec(tps, tm)],
        [_tok_spec(D, tm), _tok_spec(IN_WIDTH, tm)],
        [jax.ShapeDtypeStruct((T, D), BF16), jax.ShapeDtypeStruct((T, IN_WIDTH), BF16)],
        (x2d, mod3, g_attn, w_in, cos_t, sin_t), tasks)


def _attn_out(oa, ob, x2d, mod3, g_na, g_sw, w_out, S, tasks=(), tm=WIDE_TILE):
    T, D = x2d.shape
    tps = S // tm

    def body(oa_ref, ob_ref, x_ref, mod_ref, gna_ref, gsw_ref, w_ref, mixin_ref, mix_ref, x1_ref):
        oan, _ = _rms(oa_ref[...])
        obn, _ = _rms(ob_ref[...])
        mixin = jnp.concatenate([oan * gna_ref[...], obn * gsw_ref[...]], axis=1).astype(BF16)
        mixin_ref[...] = mixin
        mix = _nn(mixin, w_ref[...])
        mix_ref[...] = mix
        x1_ref[...] = x_ref[...] + mod_ref[0, 2:3, :] * mix

    return _hosted_call(
        body, "attn_out", (T // tm,),
        [_tok_spec(NA_WIDTH, tm), _tok_spec(SW_WIDTH, tm), _tok_spec(D, tm), _mod_spec(tps, D),
         _full((1, NA_WIDTH)), _full((1, SW_WIDTH)), _full(w_out.shape)],
        [_tok_spec(NA_WIDTH + SW_WIDTH, tm), _tok_spec(D, tm), _tok_spec(D, tm)],
        [jax.ShapeDtypeStruct((T, NA_WIDTH + SW_WIDTH), BF16), jax.ShapeDtypeStruct((T, D), F32), jax.ShapeDtypeStruct((T, D), F32)],
        (oa, ob, x2d, mod3, g_na, g_sw, w_out), tasks)


def _ffn_up(x1, mod3, g_ffn, w_up, S, tasks=(), tm=WIDE_TILE):
    T, D = x1.shape
    F = w_up.shape[0] // 2
    tps = S // tm

    def body(x1_ref, mod_ref, g_ref, w_ref, h2_ref, val_ref, gt_ref):
        xn, _ = _rms(x1_ref[...])
        h2 = ((xn * g_ref[...]) * (1.0 + mod_ref[0, 4:5, :]) + mod_ref[0, 3:4, :]).astype(BF16)
        h2_ref[...] = h2
        u = _nt(h2, w_ref[...])
        val_ref[...] = u[:, :F].astype(BF16)
        gt_ref[...] = u[:, F:].astype(BF16)

    return _hosted_call(
        body, "ffn_up", (T // tm,), [_tok_spec(D, tm), _mod_spec(tps, D), _full((1, D)), _full(w_up.shape)],
        [_tok_spec(D, tm), _tok_spec(F, tm), _tok_spec(F, tm)],
        [jax.ShapeDtypeStruct((T, D), BF16), jax.ShapeDtypeStruct((T, F), BF16), jax.ShapeDtypeStruct((T, F), BF16)],
        (x1, mod3, g_ffn, w_up), tasks)


def _halo_specs(T, tps, w):
    per = TOKEN_TILE // 8
    prev = pl.BlockSpec((8, w), lambda i: (jnp.maximum(i * per - 1, 0), 0))
    nxt = pl.BlockSpec((8, w), lambda i: (jnp.minimum((i + 1) * per, T // 8 - 1), 0))
    return prev, nxt


def _seq_shifts(cur, before, after, ti, tps):
    tm = cur.shape[0]
    row = lax.broadcasted_iota(jnp.int32, cur.shape, 0)
    before = jnp.where(ti > 0, before.astype(F32), 0.0)
    after = jnp.where(ti < tps - 1, after.astype(F32), 0.0)
    return jnp.where(row == 0, before, pltpu.roll(cur, 1, axis=0)), jnp.where(row == tm - 1, after, pltpu.roll(cur, tm - 1, axis=0))


def _ffn_down(gt, val, conv_w, conv_b, w_down, x1, mod3, g_final, target, B, S):
    T, D = x1.shape
    F = gt.shape[1]
    tps = S // TOKEN_TILE
    prev, nxt = _halo_specs(T, tps, F)

    def body(gt_ref, prev_ref, next_ref, val_ref, cw_ref, cb_ref, w_ref, x1_ref, mod_ref, gf_ref, tgt_ref,
             a_ref, act_ref, vd_ref, dx2_ref, df_ref, gstat_ref, bstat_ref):
        i = pl.program_id(0)
        g = gt_ref[...].astype(F32)
        gprev, gnext = _seq_shifts(g, prev_ref[7:8, :], next_ref[0:1, :], i % tps, tps)
        gc = gprev * cw_ref[0:1, :] + g * cw_ref[1:2, :] + gnext * cw_ref[2:3, :] + cb_ref[...]
        sig = 1.0 / (1.0 + jnp.exp(-gc))
        act = gc * sig
        val = val_ref[...].astype(F32)
        act_ref[...] = act.astype(BF16)
        vd_ref[...] = (val * (sig + act - act * sig)).astype(BF16)
        a = (act * val).astype(BF16)
        a_ref[...] = a
        f = _nn(a, w_ref[...])
        gate = mod_ref[0, 5:6, :]
        x2 = x1_ref[...] + gate * f
        xn, r = _rms(x2)
        err = xn * gf_ref[...] - tgt_ref[...]
        dy = err * (1.0 / D)
        dx2 = _rms_bwd(xn, r, dy * gf_ref[...])
        dx2_ref[...] = dx2
        df_ref[...] = (gate * dx2).astype(BF16)

        @pl.when(i == 0)
        def _():
            gstat_ref[...] = jnp.zeros_like(gstat_ref)

        @pl.when(i % tps == 0)
        def _():
            bstat_ref[...] = jnp.zeros_like(bstat_ref)

        gstat_ref[0:1, :] += jnp.sum(dy * xn, axis=0, keepdims=True)
        tile_loss = jnp.sum(jnp.sum(err * err, axis=1, keepdims=True), axis=0, keepdims=True) * (0.5 / D)
        gstat_ref[1:2, :] += jnp.broadcast_to(tile_loss, (1, D))
        bstat_ref[0, 0:1, :] += jnp.sum(dx2 * f, axis=0, keepdims=True)

    return pl.pallas_call(
        body, name="ffn_down", grid=(T // TOKEN_TILE,),
        in_specs=[_tok_spec(F), prev, nxt, _tok_spec(F), _full(conv_w.shape), _full((1, F)), _full(w_down.shape),
                  _tok_spec(D), _mod_spec(tps, D), _full((1, D)), _tok_spec(D)],
        out_specs=[_tok_spec(F), _tok_spec(F), _tok_spec(F), _tok_spec(D), _tok_spec(D), _full((8, D)), _bstat_spec(tps, D)],
        out_shape=[jax.ShapeDtypeStruct((T, F), BF16), jax.ShapeDtypeStruct((T, F), BF16), jax.ShapeDtypeStruct((T, F), BF16),
                   jax.ShapeDtypeStruct((T, D), F32), jax.ShapeDtypeStruct((T, D), BF16),
                   jax.ShapeDtypeStruct((8, D), F32), jax.ShapeDtypeStruct((B, 8, D), F32)],
        compiler_params=_params("arbitrary"),
    )(gt, gt, gt, val, conv_w, conv_b, w_down, x1, mod3, g_final, target)


def _ffn_down_bwd(df, w_down, act, vd, tasks=(), tm=WIDE_TILE):
    T, D = df.shape
    F = act.shape[1]

    def body(df_ref, w_ref, act_ref, vd_ref, dval_ref, dgc_ref, cstat_ref):
        da = _nt(df_ref[...], w_ref[...])
        dval_ref[...] = (da * act_ref[...].astype(F32)).astype(BF16)
        dgc = da * vd_ref[...].astype(F32)
        dgc_ref[...] = dgc.astype(BF16)

        @pl.when(pl.program_id(0) == 0)
        def _():
            cstat_ref[...] = jnp.zeros_like(cstat_ref)

        cstat_ref[0:1, :] += jnp.sum(dgc, axis=0, keepdims=True)

    return _hosted_call(
        body, "ffn_down_bwd", (T // tm,),
        [_tok_spec(D, tm), _full(w_down.shape), _tok_spec(F, tm), _tok_spec(F, tm)],
        [_tok_spec(F, tm), _tok_spec(F, tm), _full((8, F))],
        [jax.ShapeDtypeStruct((T, F), BF16), jax.ShapeDtypeStruct((T, F), BF16), jax.ShapeDtypeStruct((8, F), F32)],
        (df, w_down, act, vd), tasks)


def _ffn_up_bwd(dgc, dval, gt, conv_w, w_up, x1, mod3, g_ffn, dx2, mix, B, S, tasks=()):
    T, D = x1.shape
    F = dgc.shape[1]
    tps = S // TOKEN_TILE
    prev, nxt = _halo_specs(T, tps, F)

    def body(dgc_ref, prev_ref, next_ref, dval_ref, gt_ref, cw_ref, w_ref, x1_ref, mod_ref, g_ref, dx2_ref, mix_ref,
             du_ref, dx1_ref, dmix_ref, gstat_ref, bstat_ref, cstat_ref):
        i = pl.program_id(0)
        d = dgc_ref[...].astype(F32)
        dprev, dnext = _seq_shifts(d, prev_ref[7:8, :], next_ref[0:1, :], i % tps, tps)
        g = gt_ref[...].astype(F32)

        @pl.when(i == 0)
        def _():
            cstat_ref[...] = jnp.zeros_like(cstat_ref)

        cstat_ref[1:2, :] += jnp.sum(dnext * g, axis=0, keepdims=True)
        cstat_ref[2:3, :] += jnp.sum(d * g, axis=0, keepdims=True)
        cstat_ref[3:4, :] += jnp.sum(dprev * g, axis=0, keepdims=True)
        dgt = dnext * cw_ref[0:1, :] + d * cw_ref[1:2, :] + dprev * cw_ref[2:3, :]
        du = jnp.concatenate([dval_ref[...], dgt.astype(BF16)], axis=1)
        du_ref[...] = du
        dh2 = _nn(du, w_ref[...])
        xn, r = _rms(x1_ref[...])
        scale1 = 1.0 + mod_ref[0, 4:5, :]
        xg = xn * g_ref[...]
        dx1 = dx2_ref[...] + _rms_bwd(xn, r, dh2 * g_ref[...] * scale1)
        dx1_ref[...] = dx1
        dmix_ref[...] = (mod_ref[0, 2:3, :] * dx1).astype(BF16)

        @pl.when(i == 0)
        def _():
            gstat_ref[...] = jnp.zeros_like(gstat_ref)

        @pl.when(i % tps == 0)
        def _():
            bstat_ref[...] = jnp.zeros_like(bstat_ref)

        gstat_ref[0:1, :] += jnp.sum(dh2 * scale1 * xn, axis=0, keepdims=True)
        bstat_ref[0, 0:1, :] += jnp.sum(dh2, axis=0, keepdims=True)
        bstat_ref[0, 1:2, :] += jnp.sum(dh2 * xg, axis=0, keepdims=True)
        bstat_ref[0, 2:3, :] += jnp.sum(dx1 * mix_ref[...], axis=0, keepdims=True)

    return _hosted_call(
        body, "ffn_up_bwd", (T // TOKEN_TILE,),
        [_tok_spec(F), prev, nxt, _tok_spec(F), _tok_spec(F), _full(conv_w.shape), _full(w_up.shape), _tok_spec(D),
         _mod_spec(tps, D), _full((1, D)), _tok_spec(D), _tok_spec(D)],
        [_tok_spec(2 * F), _tok_spec(D), _tok_spec(D), _full((8, D)), _bstat_spec(tps, D), _full((8, F))],
        [jax.ShapeDtypeStruct((T, 2 * F), BF16), jax.ShapeDtypeStruct((T, D), F32), jax.ShapeDtypeStruct((T, D), BF16),
         jax.ShapeDtypeStruct((8, D), F32), jax.ShapeDtypeStruct((B, 8, D), F32), jax.ShapeDtypeStruct((8, F), F32)],
        (dgc, dgc, dgc, dval, gt, conv_w, w_up, x1, mod3, g_ffn, dx2, mix), tasks)


def _attn_out_bwd(dmix, w_out, oa, ob, g_na, g_sw, tasks=(), tm=WIDE_TILE):
    T, D = dmix.shape

    def body(dmix_ref, w_ref, oa_ref, ob_ref, gna_ref, gsw_ref, doa_ref, dob_ref, gstat_ref):
        dmixin = _nt(dmix_ref[...], w_ref[...])

        @pl.when(pl.program_id(0) == 0)
        def _():
            gstat_ref[...] = jnp.zeros_like(gstat_ref)

        for k, (o_ref, g_ref, do_ref) in enumerate(((oa_ref, gna_ref, doa_ref), (ob_ref, gsw_ref, dob_ref))):
            dn = dmixin[:, k * NA_WIDTH:(k + 1) * NA_WIDTH]
            on, r = _rms(o_ref[...])
            gstat_ref[k:k + 1, :] += jnp.sum(dn * on, axis=0, keepdims=True)
            do_ref[...] = _rms_bwd(on, r, dn * g_ref[...]).astype(BF16)

    hs = jax.ShapeDtypeStruct((T, NA_WIDTH), BF16)
    return _hosted_call(
        body, "attn_out_bwd", (T // tm,),
        [_tok_spec(D, tm), _full(w_out.shape), _tok_spec(NA_WIDTH, tm), _tok_spec(SW_WIDTH, tm), _full((1, NA_WIDTH)), _full((1, SW_WIDTH))],
        [_tok_spec(NA_WIDTH, tm), _tok_spec(SW_WIDTH, tm), _full((8, NA_WIDTH))],
        [hs, hs, jax.ShapeDtypeStruct((8, NA_WIDTH), F32)],
        (dmix, w_out, oa, ob, g_na, g_sw), tasks)


def _attn_in_bwd(dqa, dka, dva, dqb, dkb, dvb, cos_t, sin_t, w_in, x2d, mod3, g_attn, dx1, B, S, tm=WIDE_TILE):
    T, D = x2d.shape
    tps = S // tm

    def body(dqa_ref, dka_ref, dva_ref, dqb_ref, dkb_ref, dvb_ref, cos_ref, sin_ref, w_ref, x_ref, mod_ref, g_ref, dx1_ref,
             gx_ref, dproj_ref, gstat_ref, bstat_ref):
        i = pl.program_id(0)
        drb = jnp.concatenate([dqb_ref[...] * Q_SCALE, dkb_ref[...]], axis=1).astype(F32)
        reps = (1, ROPE_WIDTH // (2 * HEAD_DIM))
        drb = drb * jnp.tile(cos_ref[...], reps) + _rot_half(drb * jnp.tile(sin_ref[...], reps))
        dproj = jnp.concatenate([(dqa_ref[...] * Q_SCALE).astype(BF16), dka_ref[...].astype(BF16), dva_ref[...].astype(BF16),
                                 drb.astype(BF16), dvb_ref[...].astype(BF16)], axis=1)
        dproj_ref[...] = dproj
        dh = _nn(dproj, w_ref[...])
        xn, r = _rms(x_ref[...])
        scale1 = 1.0 + mod_ref[0, 1:2, :]
        gx_ref[...] = dx1_ref[...] + _rms_bwd(xn, r, dh * g_ref[...] * scale1)

        @pl.when(i == 0)
        def _():
            gstat_ref[...] = jnp.zeros_like(gstat_ref)

        @pl.when(i % tps == 0)
        def _():
            bstat_ref[...] = jnp.zeros_like(bstat_ref)

        gstat_ref[0:1, :] += jnp.sum(dh * scale1 * xn, axis=0, keepdims=True)
        bstat_ref[0, 0:1, :] += jnp.sum(dh, axis=0, keepdims=True)
        bstat_ref[0, 1:2, :] += jnp.sum(dh * (xn * g_ref[...]), axis=0, keepdims=True)

    rope = _rope_spec(tps, tm)
    return pl.pallas_call(
        body, name="attn_in_bwd", grid=(T // tm,),
        in_specs=[_tok_spec(NA_WIDTH, tm), _tok_spec(NA_WIDTH, tm), _tok_spec(NA_WIDTH, tm), _tok_spec(SW_WIDTH, tm),
                  _tok_spec(SW_KV_WIDTH, tm), _tok_spec(SW_KV_WIDTH, tm), rope, rope, _full(w_in.shape), _tok_spec(D, tm),
                  _mod_spec(tps, D), _full((1, D)), _tok_spec(D, tm)],
        out_specs=[_tok_spec(D, tm), _tok_spec(IN_WIDTH, tm), _full((8, D)), _bstat_spec(tps, D)],
        out_shape=[jax.ShapeDtypeStruct((T, D), F32), jax.ShapeDtypeStruct((T, IN_WIDTH), BF16),
                   jax.ShapeDtypeStruct((8, D), F32), jax.ShapeDtypeStruct((B, 8, D), F32)],
        compiler_params=_params("arbitrary"),
    )(dqa, dka, dva, dqb, dkb, dvb, cos_t, sin_t, w_in, x2d, mod3, g_attn, dx1)


def _matmul_tn(a, b, name, tm=None, tk=512):
    T, M = a.shape
    N = b.shape[1]
    tm = M if tm is None else tm
    nk = T // tk

    def body(a_ref, b_ref, o_ref, acc):
        k = pl.program_id(1)

        @pl.when(k == 0)
        def _():
            acc[...] = jnp.zeros_like(acc)

        acc[...] += _tn(a_ref[...], b_ref[...])

        @pl.when(k == nk - 1)
        def _():
            o_ref[...] = acc[...].astype(BF16)

    return pl.pallas_call(
        body, name=name, grid=(M // tm, nk),
        in_specs=[pl.BlockSpec((tk, tm), lambda i, k: (k, i)), pl.BlockSpec((tk, N), lambda i, k: (k, 0))],
        out_specs=pl.BlockSpec((tm, N), lambda i, k: (i, 0)),
        out_shape=jax.ShapeDtypeStruct((M, N), BF16),
        scratch_shapes=[pltpu.VMEM((tm, N), F32)],
        compiler_params=_params("parallel", "arbitrary"),
    )(a, b)


def _na_geometry(S):
    rows = S // GRID_W
    wr = min(NA_ROWS_MAX, rows)
    return rows, wr


def _na_window(r, rows, wr):
    rs = jnp.clip(r - wr // 2, 0, rows - wr)
    return pl.multiple_of(rs * GRID_W, GRID_W), pl.multiple_of((rs - r + NA_ROWS_MAX - 1) * GRID_W, GRID_W)


NA_STEP_PAIRS = 2
NA_GW = NA_STEP_PAIRS * 128
NA_BWD_ROWS = 4
NA_ROWS_PER_STEP = 4


def _na_specs(S, kw_n, order):
    ng = NA_PAIRS // NA_STEP_PAIRS

    def col(k):
        return pl.BlockSpec((1, S, NA_GW), lambda *ids: (order(*ids)[0], 0, k * ng + order(*ids)[1]))
    bias = pl.BlockSpec((NA_STEP_PAIRS, N_DR * GRID_W, 128), lambda *ids: (order(*ids)[1], 0, 0))
    out = pl.BlockSpec((1, S, NA_GW), lambda *ids: (order(*ids)[0], 0, order(*ids)[1]))
    return col(0), col(1), col(2), bias, out


def _block_diag(t):
    left = lax.broadcasted_iota(jnp.int32, t.shape, 1) < HEAD_DIM
    zero = jnp.zeros_like(t)
    return jnp.concatenate([jnp.where(left, t, zero), jnp.where(left, zero, t)], axis=0)


def _diag_blocks(res):
    left = lax.broadcasted_iota(jnp.int32, (HEAD_DIM, 128), 1) < HEAD_DIM
    return jnp.where(left, res[:HEAD_DIM], res[HEAD_DIM:])


def _col_softmax(st):
    e = jnp.exp(st - jnp.max(st, axis=0, keepdims=True))
    return e * (1.0 / jnp.sum(e, axis=0, keepdims=True))


def _na_fwd(qkv, bias, tasks=()):
    B, S, _ = qkv.shape
    rows, wr = _na_geometry(S)
    kw_n = wr * GRID_W

    def body(q_ref, k_ref, v_ref, b_ref, o_ref):
        def step(it, carry):
            win = [_na_window(it * NA_ROWS_PER_STEP + u, rows, wr) for u in range(NA_ROWS_PER_STEP)]
            qrows = [pl.ds(pl.multiple_of((it * NA_ROWS_PER_STEP + u) * GRID_W, GRID_W), GRID_W) for u in range(NA_ROWS_PER_STEP)]
            krows = [pl.ds(w[0], kw_n) for w in win]
            brows = [pl.ds(w[1], kw_n) for w in win]
            lanes = [pl.ds(p * 128, 128) for p in range(NA_STEP_PAIRS)]
            chains = [(u, p) for u in range(NA_ROWS_PER_STEP) for p in range(NA_STEP_PAIRS)]
            st = {(u, p): _nt(k_ref[0, krows[u], lanes[p]], _block_diag(q_ref[0, qrows[u], lanes[p]])) for u, p in chains}
            pn = {(u, p): _col_softmax(st[(u, p)] + b_ref[p, brows[u], :]).astype(BF16) for u, p in chains}
            out = {(u, p): _diag_blocks(_tn(pn[(u, p)], v_ref[0, krows[u], lanes[p]])) for u, p in chains}
            for u in range(NA_ROWS_PER_STEP):
                o_ref[0, qrows[u], :] = jnp.concatenate([out[(u, p)] for p in range(NA_STEP_PAIRS)], axis=1)
            return carry

        lax.fori_loop(0, rows // NA_ROWS_PER_STEP, step, 0)

    q, k, v, bs, out = _na_specs(S, kw_n, lambda b, g: (b, g))
    return _hosted_call(body, "na_fwd", (B, NA_PAIRS // NA_STEP_PAIRS), [q, k, v, bs], [out],
                        [jax.ShapeDtypeStruct((B, S, NA_WIDTH), F32)], (qkv, qkv, qkv, bias), tasks)


def _na_bwd(qkv, bias, doa, tasks=()):
    B, S, _ = qkv.shape
    rows, wr = _na_geometry(S)
    kw_n = wr * GRID_W

    def body(q_ref, k_ref, v_ref, b_ref, do_ref, dq_ref, dk_ref, dv_ref, db_ref, dk_acc, dv_acc):
        @pl.when(pl.program_id(1) == 0)
        def _():
            db_ref[...] = jnp.zeros_like(db_ref)

        dk_acc[...] = jnp.zeros_like(dk_acc)
        dv_acc[...] = jnp.zeros_like(dv_acc)

        def step(it, carry):
            nu, pairs = range(NA_BWD_ROWS), range(NA_STEP_PAIRS)
            win = [_na_window(it * NA_BWD_ROWS + u, rows, wr) for u in nu]
            qrows = [pl.ds(pl.multiple_of((it * NA_BWD_ROWS + u) * GRID_W, GRID_W), GRID_W) for u in nu]
            krows = [pl.ds(w[0], kw_n) for w in win]
            brows = [pl.ds(w[1], kw_n) for w in win]
            lanes = [pl.ds(p * 128, 128) for p in pairs]
            chains = [(u, p) for u in nu for p in pairs]
            kp = {(u, p): k_ref[0, krows[u], lanes[p]] for u, p in chains}
            qbd = {(u, p): _block_diag(q_ref[0, qrows[u], lanes[p]]) for u, p in chains}
            dobd = {(u, p): _block_diag(do_ref[0, qrows[u], lanes[p]]) for u, p in chains}
            st = {c: _nt(kp[c], qbd[c]) for c in chains}
            dpt = {(u, p): _nt(v_ref[0, krows[u], lanes[p]], dobd[(u, p)]) for u, p in chains}
            pn = {(u, p): _col_softmax(st[(u, p)] + b_ref[p, brows[u], :]) for u, p in chains}
            dst = {c: pn[c] * (dpt[c] - jnp.sum(pn[c] * dpt[c], axis=0, keepdims=True)) for c in chains}
            dsb = {c: dst[c].astype(BF16) for c in chains}
            dq = {c: _diag_blocks(_tn(dsb[c], kp[c])) for c in chains}
            dk = {c: _nn(dsb[c], qbd[c]) for c in chains}
            dv = {c: _nn(pn[c].astype(BF16), dobd[c]) for c in chains}
            for u in nu:
                dq_ref[0, qrows[u], :] = jnp.concatenate([dq[(u, p)] for p in pairs], axis=1).astype(BF16)
                dk_acc[krows[u], :] += jnp.concatenate([dk[(u, p)] for p in pairs], axis=1)
                dv_acc[krows[u], :] += jnp.concatenate([dv[(u, p)] for p in pairs], axis=1)
                for p in pairs:
                    db_ref[p, brows[u], :] += dst[(u, p)]
            return carry

        lax.fori_loop(0, rows // NA_BWD_ROWS, step, 0)

        def emit(i, carry):
            r = pl.ds(pl.multiple_of(i * 256, 256), 256)
            dk_ref[0, r, :] = dk_acc[r, :].astype(BF16)
            dv_ref[0, r, :] = dv_acc[r, :].astype(BF16)
            return carry

        lax.fori_loop(0, S // 256, emit, 0)

    q, k, v, bs, out = _na_specs(S, kw_n, lambda g, b: (b, g))
    hs = jax.ShapeDtypeStruct((B, S, NA_WIDTH), BF16)
    return _hosted_call(body, "na_bwd", (NA_PAIRS // NA_STEP_PAIRS, B), [q, k, v, bs, out], [out, out, out, bs],
                        [hs, hs, hs, jax.ShapeDtypeStruct((NA_PAIRS, N_DR * GRID_W, 128), F32)], (qkv, qkv, qkv, bias, doa), tasks,
                        scratch_shapes=[pltpu.VMEM((S, NA_GW), F32), pltpu.VMEM((S, NA_GW), F32)])


SW_PAIRS = SW_HEADS // 2


def _sw_band(n, S):
    kw_n = 3 * SW_BLOCK
    start = pl.multiple_of(jnp.clip(n * SW_BLOCK - SW_BLOCK, 0, S - kw_n), SW_BLOCK)
    kpos = start + lax.broadcasted_iota(jnp.int32, (kw_n, SW_BLOCK), 0)
    qpos = n * SW_BLOCK + lax.broadcasted_iota(jnp.int32, (kw_n, SW_BLOCK), 1)
    return start, jnp.abs(qpos - kpos) <= SW_WINDOW


def _kv_halves(t):
    left = lax.broadcasted_iota(jnp.int32, t.shape, 1) < HEAD_DIM
    swapped = pltpu.roll(t, HEAD_DIM, axis=1)
    zero = jnp.zeros_like(t)
    return {(0, 0): jnp.where(left, t, zero), (0, 1): jnp.where(left, zero, swapped),
            (1, 0): jnp.where(left, swapped, zero), (1, 1): jnp.where(left, zero, t)}


def _sw_probs(st, ok, sk):
    st = jnp.where(ok, st, NEG)
    m = jnp.maximum(jnp.max(st, axis=0, keepdims=True), sk)
    e = jnp.exp(st - m)
    esk = jnp.exp(sk - m)
    inv = 1.0 / (jnp.sum(e, axis=0, keepdims=True) + esk)
    return e * inv, esk * inv


def _sw_specs(S):
    q = pl.BlockSpec((1, S, SW_WIDTH), lambda b, *_: (b, 0, ROPE_LO // SW_WIDTH))
    k = pl.BlockSpec((1, S, SW_KV_WIDTH), lambda b, *_: (b, 0, (ROPE_LO + SW_WIDTH) // SW_KV_WIDTH))
    v = pl.BlockSpec((1, S, SW_KV_WIDTH), lambda b, *_: (b, 0, (ROPE_LO + ROPE_WIDTH) // SW_KV_WIDTH))
    return q, k, v


SW_FWD_SPLIT = 2


def _sw_fwd(sink, qkv, tasks=()):
    B, S, _ = qkv.shape
    kw_n = 3 * SW_BLOCK

    def body(sink_ref, q_ref, k_ref, v_ref, o_ref):
        def step(n, carry):
            start, ok = _sw_band(n, S)
            qrows = pl.ds(pl.multiple_of(n * SW_BLOCK, SW_BLOCK), SW_BLOCK)
            krows = pl.ds(start, kw_n)
            kh, vh = _kv_halves(k_ref[0, krows, :]), _kv_halves(v_ref[0, krows, :])
            heads = [(p, e) for p in range(SW_PAIRS) for e in range(2)]
            qp = [q_ref[0, qrows, pl.ds(p * 128, 128)] for p in range(SW_PAIRS)]
            kv_of = lambda p: p // (SW_PAIRS // SW_KV_HEADS)
            st = {(p, e): _nt(kh[(kv_of(p), e)], qp[p]) for p, e in heads}
            pn = {(p, e): _sw_probs(st[(p, e)], ok, sink_ref[2 * p + e])[0].astype(BF16) for p, e in heads}
            outs = [_tn(pn[(p, 0)], vh[(kv_of(p), 0)]) + _tn(pn[(p, 1)], vh[(kv_of(p), 1)]) for p in range(SW_PAIRS)]
            o_ref[0, qrows, :] = jnp.concatenate(outs, axis=1)
            return carry

        half = (S // SW_BLOCK) // SW_FWD_SPLIT
        lax.fori_loop(pl.program_id(1) * half, (pl.program_id(1) + 1) * half, step, 0)

    q, k, v = _sw_specs(S)
    return _hosted_call(
        body, "sw_fwd", (B, SW_FWD_SPLIT), [pl.BlockSpec(memory_space=pltpu.SMEM), q, k, v],
        [pl.BlockSpec((1, S, SW_WIDTH), lambda b, s: (b, 0, 0))], [jax.ShapeDtypeStruct((B, S, SW_WIDTH), F32)],
        (sink, qkv, qkv, qkv), tasks)


def _sw_bwd(sink, qkv, dob):
    B, S, _ = qkv.shape
    kw_n = 3 * SW_BLOCK

    fold_rows = 256

    def body(sink_ref, q_ref, k_ref, v_ref, do_ref, dq_ref, dk_ref, dv_ref, dsink_ref, dk_acc, dv_acc):
        @pl.when(pl.program_id(0) == 0)
        def _():
            dsink_ref[...] = jnp.zeros_like(dsink_ref)

        dk_acc[...] = jnp.zeros_like(dk_acc)
        dv_acc[...] = jnp.zeros_like(dv_acc)
        ppk = SW_PAIRS // SW_KV_HEADS

        def step(n, carry):
            start, ok = _sw_band(n, S)
            qrows = pl.ds(pl.multiple_of(n * SW_BLOCK, SW_BLOCK), SW_BLOCK)
            krows = pl.ds(start, kw_n)
            kh, vh = _kv_halves(k_ref[0, krows, :]), _kv_halves(v_ref[0, krows, :])
            heads = [(p, e) for p in range(SW_PAIRS) for e in range(2)]
            qp = [q_ref[0, qrows, pl.ds(p * 128, 128)] for p in range(SW_PAIRS)]
            dop = [do_ref[0, qrows, pl.ds(p * 128, 128)] for p in range(SW_PAIRS)]
            st = {(p, e): _nt(kh[(p // ppk, e)], qp[p]) for p, e in heads}
            dpt = {(p, e): _nt(vh[(p // ppk, e)], dop[p]) for p, e in heads}
            pnb, dsb = {}, {}
            for p, e in heads:
                pn, psink = _sw_probs(st[(p, e)], ok, sink_ref[2 * p + e])
                delta = jnp.sum(pn * dpt[(p, e)], axis=0, keepdims=True)
                dsb[(p, e)] = (pn * (dpt[(p, e)] - delta)).astype(BF16)
                pnb[(p, e)] = pn.astype(BF16)
                dsink_ref[2 * p + e:2 * p + e + 1, :] += -(psink * delta)
            dq_ref[0, qrows, :] = jnp.concatenate(
                [_tn(dsb[(p, 0)], kh[(p // ppk, 0)]) + _tn(dsb[(p, 1)], kh[(p // ppk, 1)]) for p in range(SW_PAIRS)],
                axis=1).astype(BF16)
            left = lax.broadcasted_iota(jnp.int32, (kw_n, 128), 1) < HEAD_DIM
            dks, dvs = [], []
            for kv in range(SW_KV_HEADS):
                dk = dv = None
                for p in range(kv * ppk, (kv + 1) * ppk):
                    dk_p = jnp.where(left, _nn(dsb[(p, 0)], qp[p]), _nn(dsb[(p, 1)], qp[p]))
                    dv_p = jnp.where(left, _nn(pnb[(p, 0)], dop[p]), _nn(pnb[(p, 1)], dop[p]))
                    dk = dk_p if dk is None else dk + dk_p
                    dv = dv_p if dv is None else dv + dv_p
                dks.append(dk)
                dvs.append(dv)
            dk_acc[krows, :] += jnp.concatenate(dks, axis=1)
            dv_acc[krows, :] += jnp.concatenate(dvs, axis=1)
            return carry

        lax.fori_loop(0, S // SW_BLOCK, step, 0)

        def fold(i, carry):
            rows = pl.ds(pl.multiple_of(i * fold_rows, fold_rows), fold_rows)
            left = lax.broadcasted_iota(jnp.int32, (fold_rows, 128), 1) < HEAD_DIM
            for acc, out_ref in ((dk_acc, dk_ref), (dv_acc, dv_ref)):
                a, b = acc[rows, 0:128], acc[rows, 128:256]
                out_ref[0, rows, :] = jnp.where(left, a + pltpu.roll(a, HEAD_DIM, axis=1),
                                                b + pltpu.roll(b, HEAD_DIM, axis=1)).astype(BF16)
            return carry

        lax.fori_loop(0, S // fold_rows, fold, 0)

        @pl.when(pl.program_id(0) == B - 1)
        def _():
            dsink_ref[...] = jnp.broadcast_to(jnp.sum(dsink_ref[...], axis=1, keepdims=True), dsink_ref.shape)

    q, k, v = _sw_specs(S)
    qo = pl.BlockSpec((1, S, SW_WIDTH), lambda b: (b, 0, 0))
    ko = pl.BlockSpec((1, S, SW_KV_WIDTH), lambda b: (b, 0, 0))
    return pl.pallas_call(
        body, name="sw_bwd", grid=(B,),
        in_specs=[pl.BlockSpec(memory_space=pltpu.SMEM), q, k, v, qo],
        out_specs=[qo, ko, ko, _full((SW_HEADS, 128))],
        out_shape=[jax.ShapeDtypeStruct((B, S, SW_WIDTH), BF16), jax.ShapeDtypeStruct((B, S, SW_KV_WIDTH), BF16),
                   jax.ShapeDtypeStruct((B, S, SW_KV_WIDTH), BF16), jax.ShapeDtypeStruct((SW_HEADS, 128), F32)],
        scratch_shapes=[pltpu.VMEM((S, 2 * SW_KV_WIDTH), F32), pltpu.VMEM((S, 2 * SW_KV_WIDTH), F32)],
        compiler_params=_params("arbitrary"),
    )(sink, qkv, qkv, qkv, dob)


def _pack_sum_adamw(packs, params, pick):
    W = packs.shape[1]
    n_p = len(params)

    def body(p_ref, *refs):
        ins, tot_ref, pick_ref, outs = refs[:3 * n_p], refs[3 * n_p], refs[3 * n_p + 1], refs[3 * n_p + 2:]
        tot = p_ref[0:8, :]
        for d in range(1, N_DEV):
            tot = tot + p_ref[8 * d:8 * d + 8, :]
        tot_ref[...] = tot
        pick_ref[...] = tot[pick[0]:pick[0] + 1, pick[1]:pick[1] + 1]
        for i, (w, _, _, rows, off) in enumerate(params):
            w_ref, m_ref, v_ref = ins[3 * i:3 * i + 3]
            g_ref, d_ref, nm_ref, nv_ref = outs[4 * i:4 * i + 4]
            if w.ndim == 3:
                n_a, n_b, n = w.shape
                for a in range(n_a):
                    for b in range(n_b):
                        o = off + (b * n_a + a) * n
                        g_ref[a, b:b + 1, :] = tot_ref[rows[0]:rows[0] + 1, o:o + n]
                g = g_ref[...]
            else:
                n = w.shape[1]
                g = tot[rows[0]:rows[0] + 1, off:off + n]
                for r in rows[1:]:
                    g = g + tot[r:r + 1, off:off + n]
                g_ref[...] = g
            d_ref[...], nm_ref[...], nv_ref[...] = _adam_update(w_ref[...], g, m_ref[...], v_ref[...])

    res = pl.pallas_call(
        body, name="small_adamw",
        out_shape=[jax.ShapeDtypeStruct((8, W), F32), jax.ShapeDtypeStruct((1, 1), F32)]
        + [jax.ShapeDtypeStruct(p[0].shape, F32) for p in params for _ in range(4)],
        compiler_params=pltpu.CompilerParams(vmem_limit_bytes=VMEM_LIMIT),
    )(packs, *[a for p in params for a in p[:3]])
    return res[0], res[1], [res[2 + 4 * i:6 + 4 * i] for i in range(n_p)]


def _adam_update(w, g, m, v):
    c1 = 1.0 - ADAM_B1 ** ADAM_STEP
    c2 = 1.0 - ADAM_B2 ** ADAM_STEP
    nm = ADAM_B1 * m + (1.0 - ADAM_B1) * g
    nv = ADAM_B2 * v + (1.0 - ADAM_B2) * (g * g)
    return -ADAM_LR * ((nm / c1) / (jnp.sqrt(nv / c2) + ADAM_EPS) + ADAM_WD * w), nm, nv


def _adamw(w, g, m, v, name):
    def body(w_ref, g_ref, m_ref, v_ref, d_ref, nm_ref, nv_ref):
        d_ref[...], nm_ref[...], nv_ref[...] = _adam_update(w_ref[...], g_ref[...], m_ref[...], v_ref[...])

    s = jax.ShapeDtypeStruct(w.shape, F32)
    return pl.pallas_call(body, name=name, out_shape=[s, s, s],
                          compiler_params=pltpu.CompilerParams(vmem_limit_bytes=VMEM_LIMIT))(w, g, m, v)


def _sum_adamw_rows(R):
    return max(r for r in range(16, min(R, 256) + 1, 16) if R % r == 0)


def _sum_adamw_steps(R):
    return R // _sum_adamw_rows(R)


def _sum_adamw(own, recvb, w, m, v, name):
    R, C = own.shape
    rc = _sum_adamw_rows(R)

    def body(own_ref, r_ref, w_ref, m_ref, v_ref, g_ref, d_ref, nm_ref, nv_ref):
        g = own_ref[...]
        for j in range(3):
            g = g + r_ref[j].astype(F32)
        g_ref[...] = g
        d_ref[...], nm_ref[...], nv_ref[...] = _adam_update(w_ref[...], g, m_ref[...], v_ref[...])

    blk = pl.BlockSpec((rc, C), lambda i: (i, 0))
    s = jax.ShapeDtypeStruct((R, C), F32)
    return pl.pallas_call(
        body, name=name, grid=(R // rc,),
        in_specs=[blk, pl.BlockSpec((3, rc, C), lambda i: (0, i, 0)), blk, blk, blk],
        out_specs=[blk, blk, blk, blk], out_shape=[s, s, s, s], compiler_params=_params("parallel"),
    )(own, recvb, w, m, v)


def _by_device(dw):
    return dw.reshape(N_DEV, dw.shape[0] // N_DEV, dw.shape[1])


def _local_step(x, mod, g_attn, w_in, bias, sw_sink, g_na_out, g_sw_out, w_out, g_ffn, w_up, conv_w, conv_b, w_down,
                g_final, target, sharded):
    B, S, D = x.shape
    T = B * S
    x2d = x.reshape(T, D)
    mod3 = mod.reshape(B, 6, D)
    cos_t, sin_t = _rope_tables(S)
    sink = sw_sink.reshape(SW_HEADS)
    n_tiles = T // WIDE_TILE
    full = lambda g: g.reshape(N_DEV * g.shape[1], g.shape[2])

    rider = lambda w, mid, lo, n, into=None: [_gather_task(w, mid, rows=(lo, n), into=into)] if sharded else []
    if sharded:
        qu, hd = w_up.shape[0] // 4, w_down.shape[0] // 2
    (h, qkv), got = _attn_in(x2d, mod3, g_attn, w_in, cos_t, sin_t, S, rider(w_up, 3 * n_tiles // 4, 0, qu) if sharded else [])
    if sharded:
        w_up_buf = got[0][0]
    qkv3 = qkv.reshape(B, S, IN_WIDTH)
    na_steps, sw_steps = B * (NA_PAIRS // NA_STEP_PAIRS), B * SW_FWD_SPLIT
    (oa,), got = _na_fwd(qkv3, bias, rider(w_up, na_steps - 1, qu, 2 * qu, w_up_buf) if sharded else [])
    if sharded:
        w_up_buf = got[0][0]
    oa = oa.reshape(T, NA_WIDTH)
    (ob,), got = _sw_fwd(sink, qkv3,
                         rider(w_up, sw_steps // 2, 3 * qu, qu, w_up_buf) + [_gather_task(w_out, sw_steps - 1)] if sharded else [])
    if sharded:
        w_up, w_out = full(got[0][0]), full(got[1][0])
    ob = ob.reshape(T, SW_WIDTH)
    (mixin, mix, x1), _ = _attn_out(oa, ob, x2d, mod3, g_na_out, g_sw_out, w_out, S)
    (h2, val, gt), got = _ffn_up(x1, mod3, g_ffn, w_up, S, rider(w_down, 3 * n_tiles // 4, 0, 2 * hd) if sharded else [])
    if sharded:
        w_down = full(got[0][0])
    a, act, vd, dx2, df, gstat_f, bstat_f = _ffn_down(gt, val, conv_w, conv_b, w_down, x1, mod3, g_final, target.reshape(T, D), B, S)
    F = val.shape[1]

    dw_down = _matmul_tn(a, df, "dw_down")
    (dval, dgc, cstat), got = _ffn_down_bwd(df, w_down, act, vd, [_swap_task(_by_device(dw_down))] if sharded else [])
    if sharded:
        send_down, own_down = _chip_sums(_by_device(dw_down), got[0][0])
    (du, dx1, dmix, gstat_u, bstat_u, cstat_w), got = _ffn_up_bwd(dgc, dval, gt, conv_w, w_up, x1, mod3, g_ffn, dx2, mix, B, S,
                                                                  [_exchange_task(send_down)] if sharded else [])
    if sharded:
        dw_down = (own_down, got[0][0])
    dw_up = _matmul_tn(du, h2, "dw_up", tm=F)
    dw_out = _matmul_tn(mixin, dmix, "dw_out")
    (doa, dob, gstat_o), got = _attn_out_bwd(dmix, w_out, oa, ob, g_na_out, g_sw_out,
                                             [_swap_task(_by_device(dw_up)), _swap_task(_by_device(dw_out))] if sharded else [])
    if sharded:
        send_up, own_up = _chip_sums(_by_device(dw_up), got[0][0])
        send_out, own_out = _chip_sums(_by_device(dw_out), got[1][0])
    (dqa, dka, dva, dbt), got = _na_bwd(qkv3, bias, doa.reshape(B, S, NA_WIDTH),
                                        [_exchange_task(send_up), _exchange_task(send_out)] if sharded else [])
    if sharded:
        dw_up, dw_out = (own_up, got[0][0]), (own_out, got[1][0])
    dqb, dkb, dvb, dsink = _sw_bwd(sink, qkv3, dob.reshape(B, S, SW_WIDTH))
    r2 = lambda t: t.reshape(T, t.shape[-1])
    grad_x, dproj, gstat_i, bstat_i = _attn_in_bwd(r2(dqa), r2(dka), r2(dva), r2(dqb), r2(dkb), r2(dvb), cos_t, sin_t, w_in, x2d, mod3,
                                                   g_attn, dx1, B, S)
    dw_in = _matmul_tn(dproj, h, "dw_in")

    dmod = jnp.stack([bstat_i[:, 0], bstat_i[:, 1], bstat_u[:, 2], bstat_u[:, 0], bstat_u[:, 1], bstat_f[:, 0]], axis=1)
    small = dict(g_attn=gstat_i[0], g_ffn=gstat_u[0], g_final=gstat_f[0], loss=gstat_f[1, 0], g_na_out=gstat_o[0], g_sw_out=gstat_o[1],
                 sw_sink=dsink[:, 0], conv_b=cstat[0], conv_w=cstat_w[1:4], dbt=dbt,
                 raw=(bstat_i, bstat_u, bstat_f, gstat_i, gstat_u, gstat_f, gstat_o, dsink, cstat, cstat_w))
    return grad_x.reshape(B, S, D), dict(w_in=dw_in, w_out=dw_out, w_up=dw_up, w_down=dw_down), dmod, small


def _pack_slab(raw, drpb):
    D, F = raw[3].shape[1], raw[8].shape[1]
    n_seq = raw[0].shape[0]

    def body(bi_ref, bu_ref, bf_ref, gi_ref, gu_ref, gf_ref, go_ref, ds_ref, cs_ref, cw_ref, rp_ref, o_ref):
        o_ref[...] = jnp.zeros_like(o_ref)
        for b in range(n_seq):
            mods = (bi_ref[b, 0:1, :], bi_ref[b, 1:2, :], bu_ref[b, 2:3, :], bu_ref[b, 0:1, :], bu_ref[b, 1:2, :], bf_ref[b, 0:1, :])
            for k, row in enumerate(mods):
                o_ref[b:b + 1, k * D:(k + 1) * D] = row
        o = 0
        for row in (gi_ref[0:1, :], gu_ref[0:1, :], gf_ref[0:1, :], go_ref[0:1, :], go_ref[1:2, :]):
            o_ref[2:3, o:o + row.shape[1]] = row
            o += row.shape[1]
        ds = ds_ref[...]
        eye = lax.broadcasted_iota(jnp.int32, ds.shape, 0) == lax.broadcasted_iota(jnp.int32, ds.shape, 1)
        o_ref[2:3, o:o + 128] = jnp.sum(jnp.where(eye, ds, 0.0), axis=0, keepdims=True)
        o_ref[2:3, o + 128:o + 256] = gf_ref[1:2, 0:128]
        o_ref[3:4, 0:F] = cs_ref[0:1, :]
        o_ref[4:5, 0:rp_ref.shape[1]] = rp_ref[...]
        o_ref[5:8, 0:F] = cw_ref[1:4, :]

    return pl.pallas_call(body, name="pack_slab", out_shape=jax.ShapeDtypeStruct((8, PACK_W), F32),
                          compiler_params=pltpu.CompilerParams(vmem_limit_bytes=VMEM_LIMIT))(*raw, drpb)


def kernel(x, c, w_ada, b_ada, g_attn, w_in, na_rpb, sw_sink, g_na_out, g_sw_out, w_out, g_ffn, w_up, conv_w, conv_b, w_down, g_final, loss_target, m_w_ada, m_b_ada, m_g_attn, m_w_in, m_na_rpb, m_sw_sink, m_g_na_out, m_g_sw_out, m_w_out, m_g_ffn, m_w_up, m_conv_w, m_conv_b, m_w_down, m_g_final, v_w_ada, v_b_ada, v_g_attn, v_w_in, v_na_rpb, v_sw_sink, v_g_na_out, v_g_sw_out, v_w_out, v_g_ffn, v_w_up, v_conv_w, v_conv_b, v_w_down, v_g_final):
    B, S, D = x.shape
    me = 4 * lax.axis_index("x") + 2 * lax.axis_index("y") + lax.axis_index("c")
    ada_c = w_ada.shape[2]
    F_l = conv_w.shape[2]

    tr = {"w_in", "w_up"}
    w_in_t = jnp.transpose(w_in[0])
    shards_f = dict(w_out=w_out[0], w_up=jnp.transpose(w_up[0]), w_down=w_down[0])

    c_all, conv_w_f, mod_all, w_in_all, casts, bias = _ada_fwd(c, conv_w[0], w_ada[0], b_ada, w_in_t, list(shards_f.values()),
                                                               _na_bias_rev(na_rpb[0]))
    shards = dict(zip(shards_f, casts))
    mod_mine = lax.dynamic_slice(mod_all, (0, me * B, 0), (N_DEV, B, ada_c))
    mod = jnp.transpose(mod_mine, (1, 0, 2)).reshape(B, N_DEV * ada_c)
    w_in_f = w_in_all.reshape(N_DEV * w_in_t.shape[0], D)

    grad_x, dw, dmod, small = _local_step(x, mod, g_attn, w_in_f, bias, sw_sink, g_na_out, g_sw_out, shards["w_out"], g_ffn,
                                          shards["w_up"], conv_w_f, conv_b, shards["w_down"], g_final.reshape(1, D), loss_target,
                                          sharded=True)
    g8_in = _by_device(dw["w_in"])
    drpb, _ = _na_bias_grad(small["dbt"])

    slab = _pack_slab(small["raw"], drpb.reshape(1, -1))
    weights = dict(w_ada=w_ada, b_ada=b_ada, g_attn=g_attn, w_in=w_in, na_rpb=na_rpb, sw_sink=sw_sink, g_na_out=g_na_out,
                   g_sw_out=g_sw_out, w_out=w_out, g_ffn=g_ffn, w_up=w_up, conv_w=conv_w, conv_b=conv_b, w_down=w_down, g_final=g_final)
    ms = dict(w_ada=m_w_ada, b_ada=m_b_ada, g_attn=m_g_attn, w_in=m_w_in, na_rpb=m_na_rpb, sw_sink=m_sw_sink, g_na_out=m_g_na_out,
              g_sw_out=m_g_sw_out, w_out=m_w_out, g_ffn=m_g_ffn, w_up=m_w_up, conv_w=m_conv_w, conv_b=m_conv_b, w_down=m_w_down, g_final=m_g_final)
    vs = dict(w_ada=v_w_ada, b_ada=v_b_ada, g_attn=v_g_attn, w_in=v_w_in, na_rpb=v_na_rpb, sw_sink=v_sw_sink, g_na_out=v_g_na_out,
              g_sw_out=v_g_sw_out, w_out=v_w_out, g_ffn=v_g_ffn, w_up=v_w_up, conv_w=v_conv_w, conv_b=v_conv_b, w_down=v_w_down, g_final=v_g_final)
    names = list(weights)
    grads, deltas, new_m, new_v = {}, {}, {}, {}
    flat = lambda t: t.reshape(1, -1)

    def shard2d(nm):
        if nm in tr:
            return (lambda t: jnp.transpose(t[0])), (lambda t: jnp.transpose(t)[None])
        if nm == "conv_w":
            return (lambda t: jnp.transpose(t, (1, 0, 2))), (lambda t: jnp.transpose(t, (1, 0, 2)))
        return (lambda t: t[0]), (lambda t: t[None])

    def finish_sum(nm, own, recvb):
        r, back = shard2d(nm)
        g2, d_, m_, v_ = _sum_adamw(own, recvb, r(weights[nm]), r(ms[nm]), r(vs[nm]), "adamw_" + nm)
        grads[nm], deltas[nm], new_m[nm], new_v[nm] = back(g2), back(d_), back(m_), back(v_)

    own_in, recv_in, packs = _reduce_scatter_call(g8_in, slab)
    finish_sum("w_up", *dw["w_up"])
    finish_sum("w_down", *dw["w_down"])
    finish_sum("w_in", own_in, recv_in)
    finish_sum("w_out", *dw["w_out"])

    where = dict(b_ada=((0, 1), 0), g_attn=((2,), 0), g_ffn=((2,), D), g_final=((2,), 2 * D), g_na_out=((2,), 3 * D),
                 g_sw_out=((2,), 3 * D + NA_WIDTH), sw_sink=((2,), 3 * D + NA_WIDTH + SW_WIDTH), conv_b=((3,), 0), na_rpb=((4,), 0))

    def small_view(n):
        if n == "na_rpb":
            return (lambda t: jnp.transpose(t[0], (1, 0, 2))), (lambda t: jnp.transpose(t, (1, 0, 2))[None])
        return flat, (lambda t: t.reshape(weights[n].shape))

    tot, loss, small_out = _pack_sum_adamw(
        packs.reshape(N_DEV * 8, PACK_W),
        [tuple(small_view(n)[0](t[n]) for t in (weights, ms, vs)) + where[n] for n in where],
        (2, 3 * D + NA_WIDTH + SW_WIDTH + 128))
    for n, res in zip(where, small_out):
        grads[n], deltas[n], new_m[n], new_v[n] = [small_view(n)[1](t) for t in res]
    loss = loss.reshape(())

    for nm, g2 in (("w_ada", _ada_bwd(c_all, packs.reshape(N_DEV * 8, PACK_W), ada_c)), ("conv_w", lax.dynamic_slice(tot, (5, me * F_l), (3, F_l))[:, None])):
        r, back = shard2d(nm)
        d_, m_, v_ = _adamw(r(weights[nm]), g2, r(ms[nm]), r(vs[nm]), "adamw_" + nm)
        grads[nm], deltas[nm], new_m[nm], new_v[nm] = back(g2), back(d_), back(m_), back(v_)
    return (loss, grad_x, *[grads[n] for n in names], *[deltas[n] for n in names], *[new_m[n] for n in names],
            *[new_v[n] for n in names])
```
